```python
import jax, jax.numpy as jnp
from jax import lax
import numpy as np

D_MODEL = 2048
BATCH = 8
SEQ = 2048
DEPTH = 1

FOX_HEAD_DIM = 128
FOX_WIDTH = D_MODEL // 2
FOX_HEADS = FOX_WIDTH // FOX_HEAD_DIM
RWKV_HEAD_DIM = 64
RWKV_WIDTH = D_MODEL // 2
RWKV_HEADS = RWKV_WIDTH // RWKV_HEAD_DIM
DECAY_LORA = max(32, int(round(D_MODEL ** 0.5 * 1.8 / 32)) * 32)
AAA_LORA = max(32, int(round(D_MODEL ** 0.5 * 1.8 / 32)) * 32)
Q_BLOCK = 128
RMS_EPS = 1e-6
GN_EPS = 64e-5
L2_EPS = 1e-12

FOX_SIZES = (FOX_WIDTH, FOX_WIDTH, FOX_WIDTH, FOX_WIDTH, FOX_HEADS)
RWKV_SIZES = (RWKV_WIDTH, RWKV_WIDTH, RWKV_WIDTH, RWKV_WIDTH, DECAY_LORA, AAA_LORA)
FOX_COLS = sum(FOX_SIZES)
RWKV_COLS = sum(RWKV_SIZES)
IN_COLS = FOX_COLS + RWKV_COLS + 2 * D_MODEL

kernel_name = "fox_rwkv7_gated_parallel_hybrid"


def _split(u, sizes):
    idx = [int(i) for i in np.cumsum(sizes)[:-1]]
    return jnp.split(u, idx, axis=-1)


def _rmsnorm(x, g):
    xf = x.astype(jnp.float32)
    y = xf * lax.rsqrt(jnp.mean(xf * xf, axis=-1, keepdims=True) + RMS_EPS)
    return (y * g.astype(jnp.float32)).astype(x.dtype)


def _token_shift(u, mu):
    prev = jnp.pad(u, ((0, 0), (1, 0), (0, 0)))[:, :-1]
    return u + (prev - u) * mu


def _fox_attention(q, k, v, log_f):
    c = jnp.cumsum(log_f, axis=-1)
    T = q.shape[2]
    scale = FOX_HEAD_DIM ** -0.5
    outs = []
    for i in range(T // Q_BLOCK):
        s0, e = i * Q_BLOCK, (i + 1) * Q_BLOCK
        qb, kb, vb = q[:, :, s0:e], k[:, :, :e], v[:, :, :e]
        logits = (jnp.einsum('bhqd,bhkd->bhqk', qb, kb) * scale
                  + c[:, :, s0:e, None] - c[:, :, None, :e])
        causal = jnp.arange(e)[None, :] <= jnp.arange(s0, e)[:, None]
        logits = jnp.where(causal, logits, -jnp.inf)
        p = jax.nn.softmax(logits, axis=-1)
        outs.append(jnp.einsum('bhqk,bhkd->bhqd', p, vb))
    return jnp.concatenate(outs, axis=2)


def _rwkv7_scan(r, decay, k, v, kk, b):
    B, T, H, N = r.shape
    S0 = jnp.zeros((B, H, N, N), jnp.float32)
    xs = tuple(jnp.moveaxis(t, 1, 0) for t in (r, decay, k, v, kk, b))

    def step(S, inp):
        r_t, w_t, k_t, v_t, kk_t, b_t = inp
        sa = jnp.einsum('bhvk,bhk->bhv', S, -kk_t)
        S = (S * w_t[:, :, None, :] + sa[..., None] * b_t[:, :, None, :]
             + v_t[..., None] * k_t[:, :, None, :])
        y = jnp.einsum('bhvk,bhk->bhv', S, r_t)
        return S, y

    _, ys = lax.scan(step, S0, xs)
    return jnp.moveaxis(ys, 0, 1)


def _hybrid_layer(x, norm_gain, w_in, fox_forget_bias, rwkv_shift_mix, rwkv_w0, rwkv_w2,
                  rwkv_a0, rwkv_a2, rwkv_k_k, rwkv_k_a, rwkv_r_k, rwkv_ln_w, rwkv_ln_b,
                  w_proj_fox, w_proj_rwkv, w_out):
    B, T, _ = x.shape
    f32 = jnp.float32
    h = _rmsnorm(x, norm_gain)
    u = h @ w_in
    u_fox, u_rwkv, g_a, g_b = _split(u, (FOX_COLS, RWKV_COLS, D_MODEL, D_MODEL))

    q, k, v, z_a, f_logit = _split(u_fox.astype(f32), FOX_SIZES)
    to_heads = lambda t: t.reshape(B, T, FOX_HEADS, FOX_HEAD_DIM).transpose(0, 2, 1, 3)
    log_f = jax.nn.log_sigmoid(f_logit + fox_forget_bias.astype(f32)).transpose(0, 2, 1)
    o_a = _fox_attention(to_heads(q), to_heads(k), to_heads(v), log_f)
    o_a = o_a.transpose(0, 2, 1, 3).reshape(B, T, FOX_WIDTH) * jax.nn.silu(z_a)

    u_rwkv = _token_shift(u_rwkv.astype(f32), rwkv_shift_mix.astype(f32))
    r, kr, vr, z_b, w_down, a_down = _split(u_rwkv, RWKV_SIZES)
    w = -jax.nn.softplus(-(rwkv_w0.astype(f32) + jnp.tanh(w_down) @ rwkv_w2.astype(f32))) - 0.5
    decay = jnp.exp(-jnp.exp(w))
    a = jax.nn.sigmoid(rwkv_a0.astype(f32) + a_down @ rwkv_a2.astype(f32))
    heads = lambda t: t.reshape(B, T, RWKV_HEADS, RWKV_HEAD_DIM)
    kk = heads(kr * rwkv_k_k.astype(f32))
    kk = kk / jnp.maximum(jnp.sqrt(jnp.sum(kk * kk, axis=-1, keepdims=True)), L2_EPS)
    kr = kr * (1.0 + (a - 1.0) * rwkv_k_a.astype(f32))
    rh, kh, vh, ah = heads(r), heads(kr), heads(vr), heads(a)
    y = _rwkv7_scan(rh, heads(decay), kh, vh, kk, kk * ah)
    mu = jnp.mean(y, axis=-1, keepdims=True)
    var = jnp.mean(jnp.square(y - mu), axis=-1, keepdims=True)
    y = ((y - mu) * lax.rsqrt(var + GN_EPS)).reshape(B, T, RWKV_WIDTH)
    y = y * rwkv_ln_w.astype(f32) + rwkv_ln_b.astype(f32)
    bonus = jnp.sum(rh * kh * rwkv_r_k.astype(f32), axis=-1, keepdims=True) * vh
    o_b = (y + bonus.reshape(B, T, RWKV_WIDTH)) * jax.nn.silu(z_b)

    o_a = o_a.astype(x.dtype) @ w_proj_fox
    o_b = o_b.astype(x.dtype) @ w_proj_rwkv
    m = jax.nn.sigmoid(g_a) * o_a + jax.nn.sigmoid(g_b) * o_b
    return x + m @ w_out


def _fwd_setup_inputs(seed: int = 0) -> dict:
    key = jax.random.key(seed)
    ks = jax.random.split(key, 20)
    n = jax.random.normal
    L, D = DEPTH, D_MODEL
    return {
        "x": n(ks[0], (BATCH, SEQ, D), jnp.float32),
        "norm_gain": 1.0 + 0.05 * n(ks[1], (L, D), jnp.float32),
        "w_in": n(ks[2], (L, D, IN_COLS), jnp.float32) * D ** -0.5,
        "fox_forget_bias": 3.0 + 0.5 * n(ks[3], (L, FOX_HEADS), jnp.float32),
        "rwkv_shift_mix": jax.random.uniform(ks[4], (L, RWKV_COLS), jnp.float32),
        "rwkv_w0": jax.random.uniform(ks[5], (L, RWKV_WIDTH), jnp.float32, -6.5, -1.5),
        "rwkv_w2": n(ks[6], (L, DECAY_LORA, RWKV_WIDTH), jnp.float32) * 0.5 * DECAY_LORA ** -0.5,
        "rwkv_a0": 0.1 * n(ks[7], (L, RWKV_WIDTH), jnp.float32),
        "rwkv_a2": n(ks[8], (L, AAA_LORA, RWKV_WIDTH), jnp.float32) * AAA_LORA ** -0.5,
        "rwkv_k_k": 0.85 + 0.05 * n(ks[9], (L, RWKV_WIDTH), jnp.float32),
        "rwkv_k_a": 1.0 + 0.05 * n(ks[10], (L, RWKV_WIDTH), jnp.float32),
        "rwkv_r_k": -0.04 + 0.02 * n(ks[11], (L, RWKV_HEADS, RWKV_HEAD_DIM), jnp.float32),
        "rwkv_ln_w": 1.0 + 0.05 * n(ks[12], (L, RWKV_WIDTH), jnp.float32),
        "rwkv_ln_b": 0.02 * n(ks[13], (L, RWKV_WIDTH), jnp.float32),
        "w_proj_fox": n(ks[14], (L, FOX_WIDTH, D), jnp.float32) * FOX_WIDTH ** -0.5,
        "w_proj_rwkv": n(ks[15], (L, RWKV_WIDTH, D), jnp.float32) * RWKV_WIDTH ** -0.5,
        "w_out": n(ks[16], (L, D, D), jnp.float32) * D ** -0.5,
        "final_norm_gain": 1.0 + 0.05 * n(ks[17], (D,), jnp.float32),
    }


def _fwd_reference(x, norm_gain, w_in, fox_forget_bias, rwkv_shift_mix, rwkv_w0, rwkv_w2,
              rwkv_a0, rwkv_a2, rwkv_k_k, rwkv_k_a, rwkv_r_k, rwkv_ln_w, rwkv_ln_b,
              w_proj_fox, w_proj_rwkv, w_out, final_norm_gain):
    h = x
    for l in range(DEPTH):
        h = _hybrid_layer(h, norm_gain[l], w_in[l], fox_forget_bias[l], rwkv_shift_mix[l],
                          rwkv_w0[l], rwkv_w2[l], rwkv_a0[l], rwkv_a2[l], rwkv_k_k[l],
                          rwkv_k_a[l], rwkv_r_k[l], rwkv_ln_w[l], rwkv_ln_b[l],
                          w_proj_fox[l], w_proj_rwkv[l], w_out[l])
    return _rmsnorm(h, final_norm_gain)


import jax as _jax
import jax.numpy as _jnp

TWIN_FORMAT = 'train_step'
FWD_PARAMS = ['x', 'norm_gain', 'w_in', 'fox_forget_bias', 'rwkv_shift_mix', 'rwkv_w0', 'rwkv_w2', 'rwkv_a0', 'rwkv_a2', 'rwkv_k_k', 'rwkv_k_a', 'rwkv_r_k', 'rwkv_ln_w', 'rwkv_ln_b', 'w_proj_fox', 'w_proj_rwkv', 'w_out', 'final_norm_gain']
TWIN_WEIGHTS = ['norm_gain', 'w_in', 'fox_forget_bias', 'rwkv_shift_mix', 'rwkv_w0', 'rwkv_w2', 'rwkv_a0', 'rwkv_a2', 'rwkv_k_k', 'rwkv_k_a', 'rwkv_r_k', 'rwkv_ln_w', 'rwkv_ln_b', 'w_proj_fox', 'w_proj_rwkv', 'w_out', 'final_norm_gain']
TWIN_DIFF_INPUT = 'x'
TWIN_INPUTS = ['x', 'norm_gain', 'w_in', 'fox_forget_bias', 'rwkv_shift_mix', 'rwkv_w0', 'rwkv_w2', 'rwkv_a0', 'rwkv_a2', 'rwkv_k_k', 'rwkv_k_a', 'rwkv_r_k', 'rwkv_ln_w', 'rwkv_ln_b', 'w_proj_fox', 'w_proj_rwkv', 'w_out', 'final_norm_gain', 'loss_target', 'm_norm_gain', 'm_w_in', 'm_fox_forget_bias', 'm_rwkv_shift_mix', 'm_rwkv_w0', 'm_rwkv_w2', 'm_rwkv_a0', 'm_rwkv_a2', 'm_rwkv_k_k', 'm_rwkv_k_a', 'm_rwkv_r_k', 'm_rwkv_ln_w', 'm_rwkv_ln_b', 'm_w_proj_fox', 'm_w_proj_rwkv', 'm_w_out', 'm_final_norm_gain', 'v_norm_gain', 'v_w_in', 'v_fox_forget_bias', 'v_rwkv_shift_mix', 'v_rwkv_w0', 'v_rwkv_w2', 'v_rwkv_a0', 'v_rwkv_a2', 'v_rwkv_k_k', 'v_rwkv_k_a', 'v_rwkv_r_k', 'v_rwkv_ln_w', 'v_rwkv_ln_b', 'v_w_proj_fox', 'v_w_proj_rwkv', 'v_w_out', 'v_final_norm_gain']
TWIN_OUTPUTS = ['loss', 'grad_x', 'grad_norm_gain', 'grad_w_in', 'grad_fox_forget_bias', 'grad_rwkv_shift_mix', 'grad_rwkv_w0', 'grad_rwkv_w2', 'grad_rwkv_a0', 'grad_rwkv_a2', 'grad_rwkv_k_k', 'grad_rwkv_k_a', 'grad_rwkv_r_k', 'grad_rwkv_ln_w', 'grad_rwkv_ln_b', 'grad_w_proj_fox', 'grad_w_proj_rwkv', 'grad_w_out', 'grad_final_norm_gain', 'delta_norm_gain', 'delta_w_in', 'delta_fox_forget_bias', 'delta_rwkv_shift_mix', 'delta_rwkv_w0', 'delta_rwkv_w2', 'delta_rwkv_a0', 'delta_rwkv_a2', 'delta_rwkv_k_k', 'delta_rwkv_k_a', 'delta_rwkv_r_k', 'delta_rwkv_ln_w', 'delta_rwkv_ln_b', 'delta_w_proj_fox', 'delta_w_proj_rwkv', 'delta_w_out', 'delta_final_norm_gain', 'new_m_norm_gain', 'new_m_w_in', 'new_m_fox_forget_bias', 'new_m_rwkv_shift_mix', 'new_m_rwkv_w0', 'new_m_rwkv_w2', 'new_m_rwkv_a0', 'new_m_rwkv_a2', 'new_m_rwkv_k_k', 'new_m_rwkv_k_a', 'new_m_rwkv_r_k', 'new_m_rwkv_ln_w', 'new_m_rwkv_ln_b', 'new_m_w_proj_fox', 'new_m_w_proj_rwkv', 'new_m_w_out', 'new_m_final_norm_gain', 'new_v_norm_gain', 'new_v_w_in', 'new_v_fox_forget_bias', 'new_v_rwkv_shift_mix', 'new_v_rwkv_w0', 'new_v_rwkv_w2', 'new_v_rwkv_a0', 'new_v_rwkv_a2', 'new_v_rwkv_k_k', 'new_v_rwkv_k_a', 'new_v_rwkv_r_k', 'new_v_rwkv_ln_w', 'new_v_rwkv_ln_b', 'new_v_w_proj_fox', 'new_v_w_proj_rwkv', 'new_v_w_out', 'new_v_final_norm_gain']
TWIN_LEAF_KINDS = {'loss': 'loss', 'grad_x': 'grad_x', 'grad_norm_gain': 'grad_w', 'grad_w_in': 'grad_w', 'grad_fox_forget_bias': 'grad_w', 'grad_rwkv_shift_mix': 'grad_w', 'grad_rwkv_w0': 'grad_w', 'grad_rwkv_w2': 'grad_w', 'grad_rwkv_a0': 'grad_w', 'grad_rwkv_a2': 'grad_w', 'grad_rwkv_k_k': 'grad_w', 'grad_rwkv_k_a': 'grad_w', 'grad_rwkv_r_k': 'grad_w', 'grad_rwkv_ln_w': 'grad_w', 'grad_rwkv_ln_b': 'grad_w', 'grad_w_proj_fox': 'grad_w', 'grad_w_proj_rwkv': 'grad_w', 'grad_w_out': 'grad_w', 'grad_final_norm_gain': 'grad_w', 'delta_norm_gain': 'delta_w', 'delta_w_in': 'delta_w', 'delta_fox_forget_bias': 'delta_w', 'delta_rwkv_shift_mix': 'delta_w', 'delta_rwkv_w0': 'delta_w', 'delta_rwkv_w2': 'delta_w', 'delta_rwkv_a0': 'delta_w', 'delta_rwkv_a2': 'delta_w', 'delta_rwkv_k_k': 'delta_w', 'delta_rwkv_k_a': 'delta_w', 'delta_rwkv_r_k': 'delta_w', 'delta_rwkv_ln_w': 'delta_w', 'delta_rwkv_ln_b': 'delta_w', 'delta_w_proj_fox': 'delta_w', 'delta_w_proj_rwkv': 'delta_w', 'delta_w_out': 'delta_w', 'delta_final_norm_gain': 'delta_w', 'new_m_norm_gain': 'new_m', 'new_m_w_in': 'new_m', 'new_m_fox_forget_bias': 'new_m', 'new_m_rwkv_shift_mix': 'new_m', 'new_m_rwkv_w0': 'new_m', 'new_m_rwkv_w2': 'new_m', 'new_m_rwkv_a0': 'new_m', 'new_m_rwkv_a2': 'new_m', 'new_m_rwkv_k_k': 'new_m', 'new_m_rwkv_k_a': 'new_m', 'new_m_rwkv_r_k': 'new_m', 'new_m_rwkv_ln_w': 'new_m', 'new_m_rwkv_ln_b': 'new_m', 'new_m_w_proj_fox': 'new_m', 'new_m_w_proj_rwkv': 'new_m', 'new_m_w_out': 'new_m', 'new_m_final_norm_gain': 'new_m', 'new_v_norm_gain': 'new_v', 'new_v_w_in': 'new_v', 'new_v_fox_forget_bias': 'new_v', 'new_v_rwkv_shift_mix': 'new_v', 'new_v_rwkv_w0': 'new_v', 'new_v_rwkv_w2': 'new_v', 'new_v_rwkv_a0': 'new_v', 'new_v_rwkv_a2': 'new_v', 'new_v_rwkv_k_k': 'new_v', 'new_v_rwkv_k_a': 'new_v', 'new_v_rwkv_r_k': 'new_v', 'new_v_rwkv_ln_w': 'new_v', 'new_v_rwkv_ln_b': 'new_v', 'new_v_w_proj_fox': 'new_v', 'new_v_w_proj_rwkv': 'new_v', 'new_v_w_out': 'new_v', 'new_v_final_norm_gain': 'new_v'}


def _forward(args):
    return _fwd_reference(*[args[k] for k in FWD_PARAMS])


def _output_shape():
    out = _jax.eval_shape(lambda: _forward(_fwd_setup_inputs(0)))
    return out.shape, out.dtype

N_MICROBATCH = 1
ADAM_LR = 0.001
ADAM_B1 = 0.9
ADAM_B2 = 0.999
ADAM_EPS = 1e-08
ADAM_WD = 0.01
ADAM_STEP = 10
PER_EXAMPLE_BATCH_AXIS = {'x': 0, 'loss_target': 0}
SHARED_INPUTS = []
_WEIGHT_DTYPES = {'norm_gain': _jnp.float32, 'w_in': _jnp.float32, 'fox_forget_bias': _jnp.float32, 'rwkv_shift_mix': _jnp.float32, 'rwkv_w0': _jnp.float32, 'rwkv_w2': _jnp.float32, 'rwkv_a0': _jnp.float32, 'rwkv_a2': _jnp.float32, 'rwkv_k_k': _jnp.float32, 'rwkv_k_a': _jnp.float32, 'rwkv_r_k': _jnp.float32, 'rwkv_ln_w': _jnp.float32, 'rwkv_ln_b': _jnp.float32, 'w_proj_fox': _jnp.float32, 'w_proj_rwkv': _jnp.float32, 'w_out': _jnp.float32, 'final_norm_gain': _jnp.float32}
MOMENT_SCALE = {'norm_gain': 3.657253e-02, 'w_in': 1.456442e-02, 'fox_forget_bias': 6.849553e-02, 'rwkv_shift_mix': 3.924915e-02, 'rwkv_w0': 8.597842e-03, 'rwkv_w2': 9.111380e-04, 'rwkv_a0': 8.276543e-03, 'rwkv_a2': 7.855691e-03, 'rwkv_k_k': 3.328331e-02, 'rwkv_k_a': 2.472990e-02, 'rwkv_r_k': 5.066459e-02, 'rwkv_ln_w': 2.174106e-02, 'rwkv_ln_b': 2.262094e-02, 'w_proj_fox': 6.398890e-03, 'w_proj_rwkv': 1.546982e-02, 'w_out': 1.673914e-02, 'final_norm_gain': 8.012635e+00}


def _to_microbatches(a, axis):
    t = _jnp.moveaxis(a, axis, 0)
    t = t.reshape((N_MICROBATCH, t.shape[0] // N_MICROBATCH) + t.shape[1:])
    return _jnp.moveaxis(t, 1, axis + 1)


def setup_inputs(seed: int = 0) -> dict:
    inp = _fwd_setup_inputs(seed)
    key = _jax.random.fold_in(_jax.random.key(seed), 7919)
    shape, _ = _output_shape()
    out = dict(inp)
    out["loss_target"] = _jax.random.normal(_jax.random.fold_in(key, 0), shape, _jnp.float32)
    for i, name in enumerate(TWIN_WEIGHTS):
        w = inp[name].astype(_jnp.float32)
        if MOMENT_SCALE is None:
            s = _jnp.sqrt(_jnp.mean(_jnp.square(w)) + 1e-30)
        else:
            s = MOMENT_SCALE[name]
        km, kv = _jax.random.split(_jax.random.fold_in(key, i + 1))
        out[name] = w
        out["m_" + name] = s * _jax.random.normal(km, w.shape, _jnp.float32)
        out["v_" + name] = (s * s) * _jax.random.uniform(kv, w.shape, _jnp.float32, 0.5, 1.5)
    if N_MICROBATCH > 1:
        for name, axis in PER_EXAMPLE_BATCH_AXIS.items():
            out[name] = _to_microbatches(out[name], axis)
    return {'x': out['x'], 'norm_gain': out['norm_gain'], 'w_in': out['w_in'], 'fox_forget_bias': out['fox_forget_bias'], 'rwkv_shift_mix': out['rwkv_shift_mix'], 'rwkv_w0': out['rwkv_w0'], 'rwkv_w2': out['rwkv_w2'], 'rwkv_a0': out['rwkv_a0'], 'rwkv_a2': out['rwkv_a2'], 'rwkv_k_k': out['rwkv_k_k'], 'rwkv_k_a': out['rwkv_k_a'], 'rwkv_r_k': out['rwkv_r_k'], 'rwkv_ln_w': out['rwkv_ln_w'], 'rwkv_ln_b': out['rwkv_ln_b'], 'w_proj_fox': out['w_proj_fox'], 'w_proj_rwkv': out['w_proj_rwkv'], 'w_out': out['w_out'], 'final_norm_gain': out['final_norm_gain'], 'loss_target': out['loss_target'], 'm_norm_gain': out['m_norm_gain'], 'm_w_in': out['m_w_in'], 'm_fox_forget_bias': out['m_fox_forget_bias'], 'm_rwkv_shift_mix': out['m_rwkv_shift_mix'], 'm_rwkv_w0': out['m_rwkv_w0'], 'm_rwkv_w2': out['m_rwkv_w2'], 'm_rwkv_a0': out['m_rwkv_a0'], 'm_rwkv_a2': out['m_rwkv_a2'], 'm_rwkv_k_k': out['m_rwkv_k_k'], 'm_rwkv_k_a': out['m_rwkv_k_a'], 'm_rwkv_r_k': out['m_rwkv_r_k'], 'm_rwkv_ln_w': out['m_rwkv_ln_w'], 'm_rwkv_ln_b': out['m_rwkv_ln_b'], 'm_w_proj_fox': out['m_w_proj_fox'], 'm_w_proj_rwkv': out['m_w_proj_rwkv'], 'm_w_out': out['m_w_out'], 'm_final_norm_gain': out['m_final_norm_gain'], 'v_norm_gain': out['v_norm_gain'], 'v_w_in': out['v_w_in'], 'v_fox_forget_bias': out['v_fox_forget_bias'], 'v_rwkv_shift_mix': out['v_rwkv_shift_mix'], 'v_rwkv_w0': out['v_rwkv_w0'], 'v_rwkv_w2': out['v_rwkv_w2'], 'v_rwkv_a0': out['v_rwkv_a0'], 'v_rwkv_a2': out['v_rwkv_a2'], 'v_rwkv_k_k': out['v_rwkv_k_k'], 'v_rwkv_k_a': out['v_rwkv_k_a'], 'v_rwkv_r_k': out['v_rwkv_r_k'], 'v_rwkv_ln_w': out['v_rwkv_ln_w'], 'v_rwkv_ln_b': out['v_rwkv_ln_b'], 'v_w_proj_fox': out['v_w_proj_fox'], 'v_w_proj_rwkv': out['v_w_proj_rwkv'], 'v_w_out': out['v_w_out'], 'v_final_norm_gain': out['v_final_norm_gain']}


def _loss(weights, diff, rest, loss_target):
    with _jax.named_scope("forward"):
        args = {**rest, TWIN_DIFF_INPUT: diff, **{k: w.astype(_WEIGHT_DTYPES[k]) for k, w in weights.items()}}
        y = _forward(args)
    with _jax.named_scope("loss_head"):
        err = _jnp.square(y.astype(_jnp.float32) - loss_target)
        return 0.5 * _jnp.sum(_jnp.mean(err, axis=-1)) if err.ndim else 0.5 * err


def _adamw(w, g, m, v):
    m = ADAM_B1 * m + (1.0 - ADAM_B1) * g
    v = ADAM_B2 * v + (1.0 - ADAM_B2) * _jnp.square(g)
    m_hat = m / (1.0 - ADAM_B1 ** ADAM_STEP)
    v_hat = v / (1.0 - ADAM_B2 ** ADAM_STEP)
    delta = -ADAM_LR * (m_hat / (_jnp.sqrt(v_hat) + ADAM_EPS) + ADAM_WD * w)
    return delta, m, v


def reference(x, norm_gain, w_in, fox_forget_bias, rwkv_shift_mix, rwkv_w0, rwkv_w2, rwkv_a0, rwkv_a2, rwkv_k_k, rwkv_k_a, rwkv_r_k, rwkv_ln_w, rwkv_ln_b, w_proj_fox, w_proj_rwkv, w_out, final_norm_gain, loss_target, m_norm_gain, m_w_in, m_fox_forget_bias, m_rwkv_shift_mix, m_rwkv_w0, m_rwkv_w2, m_rwkv_a0, m_rwkv_a2, m_rwkv_k_k, m_rwkv_k_a, m_rwkv_r_k, m_rwkv_ln_w, m_rwkv_ln_b, m_w_proj_fox, m_w_proj_rwkv, m_w_out, m_final_norm_gain, v_norm_gain, v_w_in, v_fox_forget_bias, v_rwkv_shift_mix, v_rwkv_w0, v_rwkv_w2, v_rwkv_a0, v_rwkv_a2, v_rwkv_k_k, v_rwkv_k_a, v_rwkv_r_k, v_rwkv_ln_w, v_rwkv_ln_b, v_w_proj_fox, v_w_proj_rwkv, v_w_out, v_final_norm_gain):
    given = dict(x=x, norm_gain=norm_gain, w_in=w_in, fox_forget_bias=fox_forget_bias, rwkv_shift_mix=rwkv_shift_mix, rwkv_w0=rwkv_w0, rwkv_w2=rwkv_w2, rwkv_a0=rwkv_a0, rwkv_a2=rwkv_a2, rwkv_k_k=rwkv_k_k, rwkv_k_a=rwkv_k_a, rwkv_r_k=rwkv_r_k, rwkv_ln_w=rwkv_ln_w, rwkv_ln_b=rwkv_ln_b, w_proj_fox=w_proj_fox, w_proj_rwkv=w_proj_rwkv, w_out=w_out, final_norm_gain=final_norm_gain, loss_target=loss_target, m_norm_gain=m_norm_gain, m_w_in=m_w_in, m_fox_forget_bias=m_fox_forget_bias, m_rwkv_shift_mix=m_rwkv_shift_mix, m_rwkv_w0=m_rwkv_w0, m_rwkv_w2=m_rwkv_w2, m_rwkv_a0=m_rwkv_a0, m_rwkv_a2=m_rwkv_a2, m_rwkv_k_k=m_rwkv_k_k, m_rwkv_k_a=m_rwkv_k_a, m_rwkv_r_k=m_rwkv_r_k, m_rwkv_ln_w=m_rwkv_ln_w, m_rwkv_ln_b=m_rwkv_ln_b, m_w_proj_fox=m_w_proj_fox, m_w_proj_rwkv=m_w_proj_rwkv, m_w_out=m_w_out, m_final_norm_gain=m_final_norm_gain, v_norm_gain=v_norm_gain, v_w_in=v_w_in, v_fox_forget_bias=v_fox_forget_bias, v_rwkv_shift_mix=v_rwkv_shift_mix, v_rwkv_w0=v_rwkv_w0, v_rwkv_w2=v_rwkv_w2, v_rwkv_a0=v_rwkv_a0, v_rwkv_a2=v_rwkv_a2, v_rwkv_k_k=v_rwkv_k_k, v_rwkv_k_a=v_rwkv_k_a, v_rwkv_r_k=v_rwkv_r_k, v_rwkv_ln_w=v_rwkv_ln_w, v_rwkv_ln_b=v_rwkv_ln_b, v_w_proj_fox=v_w_proj_fox, v_w_proj_rwkv=v_w_proj_rwkv, v_w_out=v_w_out, v_final_norm_gain=v_final_norm_gain)
    weights = {n: given[n] for n in TWIN_WEIGHTS}
    shared = {n: given[n] for n in SHARED_INPUTS}
    per_example = {n: given[n] for n in ['x']}
    grad_fn = _jax.value_and_grad(_loss, argnums=(0, 1))

    def one_microbatch(ex, loss_target):
        ex = dict(ex)
        diff = ex.pop(TWIN_DIFF_INPUT)
        return grad_fn(weights, diff, {**shared, **ex}, loss_target)

    if N_MICROBATCH == 1:
        loss, (grad_w, grad_x) = one_microbatch(per_example, given["loss_target"])
    else:
        def body(carry, xs):
            loss_sum, grad_sum = carry
            l_k, (gw_k, gx_k) = one_microbatch(xs[0], xs[1])
            with _jax.named_scope("update"):
                return (loss_sum + l_k, _jax.tree.map(_jnp.add, grad_sum, gw_k)), gx_k

        init = (_jnp.zeros((), _jnp.float32), _jax.tree.map(_jnp.zeros_like, weights))
        (loss, grad_w), grad_x = _jax.lax.scan(body, init, (per_example, given["loss_target"]))
    with _jax.named_scope("update"):
        delta_w, new_m, new_v = {}, {}, {}
        for n in TWIN_WEIGHTS:
            delta_w[n], new_m[n], new_v[n] = _adamw(weights[n], grad_w[n], given["m_" + n], given["v_" + n])
    return (loss, grad_x, *[grad_w[n] for n in TWIN_WEIGHTS], *[delta_w[n] for n in TWIN_WEIGHTS],
            *[new_m[n] for n in TWIN_WEIGHTS], *[new_v[n] for n in TWIN_WEIGHTS])
```

```python
import functools

import numpy as np
import jax
import jax.numpy as jnp
from jax import lax
from jax.experimental import pallas as pl
from jax.experimental.pallas import tpu as pltpu

F32 = jnp.float32
BF16 = jnp.bfloat16
HI = lax.Precision.HIGHEST
MESH = pl.DeviceIdType.MESH

FOX_HEAD_DIM = 128
RWKV_HEAD_DIM = 64
RMS_EPS = 1e-6
GN_EPS = 64e-5
L2_EPS = 1e-12
ADAM_LR = 0.001
ADAM_B1 = 0.9
ADAM_B2 = 0.999
ADAM_EPS = 1e-08
ADAM_WD = 0.01
ADAM_STEP = 10

LANES = 128
VMEM_LIMIT = 56 * 1024 * 1024
SCAN_CHUNK = 64
N_CHIPS = 4
N_DEV = 8

_pcall = pl.pallas_call


def _cparams(sem=None):
    return pltpu.CompilerParams(dimension_semantics=sem, vmem_limit_bytes=VMEM_LIMIT)


def _softplus(x):
    return jnp.maximum(x, 0.0) + jnp.log(1.0 + jnp.exp(-jnp.abs(x)))


def _silu(z):
    return z * jax.nn.sigmoid(z)


def _rmsn(x, g):
    return x * lax.rsqrt(jnp.mean(x * x, axis=-1, keepdims=True) + RMS_EPS) * g


def _dot(a, b, dims="nn", precision=None):
    dn = {"nn": (((1,), (0,)), ((), ())), "nt": (((1,), (1,)), ((), ())), "tn": (((0,), (0,)), ((), ()))}[dims]
    return lax.dot_general(a, b, dn, precision=precision, preferred_element_type=F32)


def _bdot(a, b, ca, cb):
    return lax.dot_general(a, b, (((ca,), (cb,)), ((0,), (0,))), precision=HI, preferred_element_type=F32)


class _Cfg:
    def __init__(self, T, D, lora):
        self.T, self.D, self.lora = T, D, lora
        self.FW = D // 2
        self.FH = self.FW // FOX_HEAD_DIM
        self.RW = D // 2
        self.RH = self.RW // RWKV_HEAD_DIM
        self.LP = -(-lora // LANES) * LANES
        self.o_fox = 0
        self.o_rwkv = 4 * self.FW
        self.o_gate = self.o_rwkv + 4 * self.RW
        self.o_f = self.o_gate + 2 * D
        self.o_wd = self.o_f + LANES
        self.o_ad = self.o_wd + self.LP
        end = self.o_ad + self.LP
        self.tn = 1280 if D >= 2048 else LANES
        self.ncol = -(-end // self.tn) * self.tn
        self.in_cols = 4 * self.FW + self.FH + 4 * self.RW + 2 * lora + 2 * D
        self.rseg = 4 * self.RW + 2 * self.LP
        self.C = min(SCAN_CHUNK, T)
        self.tr = min(256, T)

    def segments(self):
        FW, FH, RW, lo, D = self.FW, self.FH, self.RW, self.lora, self.D
        g_f = 4 * FW
        g_r = g_f + FH
        g_wd = g_r + 4 * RW
        g_ad = g_wd + lo
        g_g = g_ad + lo
        return [(0, 4 * FW, 0), (g_f, FH, self.o_f), (g_r, 4 * RW, self.o_rwkv), (g_wd, lo, self.o_wd),
                (g_ad, lo, self.o_ad), (g_g, 2 * D, self.o_gate)]


def _to_my_layout(cfg, wg):
    R = wg.shape[0]
    segs = sorted(cfg.segments(), key=lambda s: s[2])
    parts, pos = [], 0
    for g0, w, m0 in segs:
        if m0 > pos:
            parts.append(jnp.zeros((R, m0 - pos), wg.dtype))
        parts.append(wg[:, g0:g0 + w])
        pos = m0 + w
    if cfg.ncol > pos:
        parts.append(jnp.zeros((R, cfg.ncol - pos), wg.dtype))
    return jnp.concatenate(parts, axis=1)


def _from_my_layout(cfg, wm):
    segs = sorted(cfg.segments(), key=lambda s: s[0])
    return jnp.concatenate([wm[:, m0:m0 + w] for g0, w, m0 in segs], axis=1)


def _rwkv_vec_to_my(cfg, v):
    RW4, lo, LP = 4 * cfg.RW, cfg.lora, cfg.LP
    z = jnp.zeros((1, LP - lo), v.dtype)
    return jnp.concatenate([v[:, :RW4], v[:, RW4:RW4 + lo], z, v[:, RW4 + lo:], z], axis=1)


def _rwkv_vec_from_my(cfg, v):
    RW4, lo, LP = 4 * cfg.RW, cfg.lora, cfg.LP
    return jnp.concatenate([v[:, :RW4], v[:, RW4:RW4 + lo], v[:, RW4 + LP:RW4 + LP + lo]], axis=1)


def _mm(name, a, b, dims, out_dtype, tm, tn, tk):
    (M, K) = a.shape if dims != "tn" else a.shape[::-1]
    N = b.shape[0] if dims == "nt" else b.shape[1]
    tm, tn, tk = min(tm, M), min(tn, N), min(tk, K)
    assert M % tm == 0 and N % tn == 0 and K % tk == 0, (name, M, N, K, tm, tn, tk)
    nk = K // tk
    if dims == "nn":
        a_spec = pl.BlockSpec((tm, tk), lambda i, j, k: (i, k))
        b_spec = pl.BlockSpec((tk, tn), lambda i, j, k: (k, j))
    elif dims == "nt":
        a_spec = pl.BlockSpec((tm, tk), lambda i, j, k: (i, k))
        b_spec = pl.BlockSpec((tn, tk), lambda i, j, k: (j, k))
    else:
        a_spec = pl.BlockSpec((tk, tm), lambda i, j, k: (k, i))
        b_spec = pl.BlockSpec((tk, tn), lambda i, j, k: (k, j))

    def body(a_ref, b_ref, o_ref, acc_ref):
        k = pl.program_id(2)

        @pl.when(k == 0)
        def _():
            acc_ref[...] = jnp.zeros_like(acc_ref)

        acc_ref[...] += _dot(a_ref[...], b_ref[...], dims)

        @pl.when(k == nk - 1)
        def _():
            o_ref[...] = acc_ref[...].astype(o_ref.dtype)

    return _pcall(
        body, name=name, grid=(M // tm, N // tn, nk),
        in_specs=[a_spec, b_spec], out_specs=pl.BlockSpec((tm, tn), lambda i, j, k: (i, j)),
        out_shape=jax.ShapeDtypeStruct((M, N), out_dtype), scratch_shapes=[pltpu.VMEM((tm, tn), F32)],
        compiler_params=_cparams(("parallel", "parallel", "arbitrary")),
    )(a, b)


def _tile(tr, w, cb=0):
    return pl.BlockSpec((tr, w), lambda i: (i, cb))


def _const(shape):
    nd = len(shape)
    return pl.BlockSpec(shape, lambda i: (0,) * nd)


def _acc_store(i, ref, val):
    @pl.when(i == 0)
    def _():
        ref[...] = val

    @pl.when(i > 0)
    def _():
        ref[...] += val


def _rms_fwd(cfg, x2, g):
    T, D, tr = cfg.T, cfg.D, cfg.tr

    def body(x_ref, g_ref, h_ref):
        h_ref[...] = _rmsn(x_ref[...], g_ref[...]).astype(BF16)

    return _pcall(body, name="rms_fwd", grid=(T // tr,), in_specs=[_tile(tr, D), _const((1, D))],
                  out_specs=_tile(tr, D), out_shape=jax.ShapeDtypeStruct((T, D), BF16),
                  compiler_params=_cparams(("parallel",)))(x2, g)


def _rms_bwd(cfg, x2, g, dh, dres):
    T, D, tr = cfg.T, cfg.D, cfg.tr

    def body(x_ref, g_ref, dh_ref, dres_ref, gx_ref, dg_ref):
        _, vjp = jax.vjp(_rmsn, x_ref[...], g_ref[...])
        dx, dg = vjp(dh_ref[...])
        gx_ref[...] = dx + dres_ref[...]
        _acc_store(pl.program_id(0), dg_ref, dg)

    return _pcall(body, name="rms_bwd", grid=(T // tr,),
                  in_specs=[_tile(tr, D), _const((1, D)), _tile(tr, D), _tile(tr, D)],
                  out_specs=[_tile(tr, D), _const((1, D))],
                  out_shape=[jax.ShapeDtypeStruct((T, D), F32), jax.ShapeDtypeStruct((1, D), F32)],
                  compiler_params=_cparams(("arbitrary",)))(x2, g, dh, dres)


def _final(cfg, x2, mo, fg, target):
    T, D, tr = cfg.T, cfg.D, cfg.tr

    def loss_fn(hres, g, tgt):
        err = _rmsn(hres, g) - tgt
        return 0.5 * jnp.sum(jnp.mean(err * err, axis=-1, keepdims=True), axis=0, keepdims=True)

    def body(x_ref, mo_ref, g_ref, t_ref, loss_ref, dres_ref, dres16_ref, dg_ref):
        hres = x_ref[...] + mo_ref[...]
        loss, vjp = jax.vjp(functools.partial(loss_fn, tgt=t_ref[...]), hres, g_ref[...])
        dres, dg = vjp(jnp.ones((1, 1), F32))
        dres_ref[...] = dres
        dres16_ref[...] = dres.astype(BF16)
        i = pl.program_id(0)
        _acc_store(i, dg_ref, dg)
        _acc_store(i, loss_ref, jnp.broadcast_to(loss, (8, LANES)))

    return _pcall(body, name="final_loss", grid=(T // tr,),
                  in_specs=[_tile(tr, D), _tile(tr, D), _const((1, D)), _tile(tr, D)],
                  out_specs=[_const((8, LANES)), _tile(tr, D), _tile(tr, D), _const((1, D))],
                  out_shape=[jax.ShapeDtypeStruct((8, LANES), F32), jax.ShapeDtypeStruct((T, D), F32),
                             jax.ShapeDtypeStruct((T, D), BF16), jax.ShapeDtypeStruct((1, D), F32)],
                  compiler_params=_cparams(("arbitrary",)))(x2, mo, fg, target)


def _merge_fn(pa, pb, ga, gb):
    return jax.nn.sigmoid(ga) * pa + jax.nn.sigmoid(gb) * pb


def _merge_fwd(cfg, pa, pb, u):
    T, D, tr = cfg.T, cfg.D, cfg.tr
    cga, cgb = cfg.o_gate // D, cfg.o_gate // D + 1

    def body(pa_ref, pb_ref, ga_ref, gb_ref, m_ref):
        m_ref[...] = _merge_fn(pa_ref[...], pb_ref[...], ga_ref[...], gb_ref[...]).astype(BF16)

    return _pcall(body, name="merge_fwd", grid=(T // tr,),
                  in_specs=[_tile(tr, D), _tile(tr, D), _tile(tr, D, cga), _tile(tr, D, cgb)],
                  out_specs=_tile(tr, D), out_shape=jax.ShapeDtypeStruct((T, D), BF16),
                  compiler_params=_cparams(("parallel",)))(pa, pb, u, u)


def _merge_bwd(cfg, pa, pb, u, dm):
    T, D, tr = cfg.T, cfg.D, cfg.tr
    cga, cgb = cfg.o_gate // D, cfg.o_gate // D + 1

    def body(pa_ref, pb_ref, ga_ref, gb_ref, dm_ref, dpa_ref, dpb_ref, dg_ref):
        _, vjp = jax.vjp(_merge_fn, pa_ref[...], pb_ref[...], ga_ref[...], gb_ref[...])
        dpa, dpb, dga, dgb = vjp(dm_ref[...])
        dpa_ref[...] = dpa.astype(BF16)
        dpb_ref[...] = dpb.astype(BF16)
        dg_ref[:, :D] = dga.astype(BF16)
        dg_ref[:, D:] = dgb.astype(BF16)

    return _pcall(body, name="merge_bwd", grid=(T // tr,),
                  in_specs=[_tile(tr, D), _tile(tr, D), _tile(tr, D, cga), _tile(tr, D, cgb), _tile(tr, D)],
                  out_specs=[_tile(tr, D), _tile(tr, D), _tile(tr, 2 * D)],
                  out_shape=[jax.ShapeDtypeStruct((T, D), BF16), jax.ShapeDtypeStruct((T, D), BF16),
                             jax.ShapeDtypeStruct((T, 2 * D), BF16)],
                  compiler_params=_cparams(("parallel",)))(pa, pb, u, u, dm)


def _gate_fn(o, z):
    return o * _silu(z)


def _gate_a_fwd(cfg, o, u):
    T, FW, tr = cfg.T, cfg.FW, cfg.tr

    def body(o_ref, z_ref, oa_ref):
        oa_ref[...] = _gate_fn(o_ref[...], z_ref[...]).astype(BF16)

    return _pcall(body, name="gate_a_fwd", grid=(T // tr,), in_specs=[_tile(tr, FW), _tile(tr, FW, 3)],
                  out_specs=_tile(tr, FW), out_shape=jax.ShapeDtypeStruct((T, FW), BF16),
                  compiler_params=_cparams(("parallel",)))(o, u)


def _gate_a_bwd(cfg, o, u, doa):
    T, FW, tr = cfg.T, cfg.FW, cfg.tr

    def body(o_ref, z_ref, doa_ref, do_ref, dz_ref):
        _, vjp = jax.vjp(_gate_fn, o_ref[...], z_ref[...])
        do, dz = vjp(doa_ref[...])
        do_ref[...] = do
        dz_ref[...] = dz.astype(BF16)

    return _pcall(body, name="gate_a_bwd", grid=(T // tr,),
                  in_specs=[_tile(tr, FW), _tile(tr, FW, 3), _tile(tr, FW)],
                  out_specs=[_tile(tr, FW), _tile(tr, FW)],
                  out_shape=[jax.ShapeDtypeStruct((T, FW), F32), jax.ShapeDtypeStruct((T, FW), BF16)],
                  compiler_params=_cparams(("parallel",)))(o, u, doa)


def _fox_prep(cfg, u, fb):
    T, tr = cfg.T, cfg.tr
    cf = cfg.o_f // LANES

    def body(f_ref, fb_ref, c_ref, carry_ref):
        i = pl.program_id(0)

        @pl.when(i == 0)
        def _():
            carry_ref[...] = jnp.zeros_like(carry_ref)

        lf = -_softplus(-(f_ref[...] + fb_ref[...]))
        r = lax.broadcasted_iota(jnp.int32, (tr, tr), 0)
        c = lax.broadcasted_iota(jnp.int32, (tr, tr), 1)
        tri = (r >= c).astype(F32)
        c_ref[...] = _dot(tri, lf, precision=HI) + carry_ref[...]
        carry_ref[...] += jnp.sum(lf, axis=0, keepdims=True)

    return _pcall(body, name="fox_prep", grid=(T // tr,), in_specs=[_tile(tr, LANES, cf), _const((1, LANES))],
                  out_specs=_tile(tr, LANES), out_shape=jax.ShapeDtypeStruct((T, LANES), F32),
                  scratch_shapes=[pltpu.VMEM((1, LANES), F32)], compiler_params=_cparams(("arbitrary",)))(u, fb)


def _fox_prep_bwd(cfg, u, fb, dc):
    T, tr = cfg.T, cfg.tr
    cf = cfg.o_f // LANES
    nb = T // tr

    def body(f_ref, fb_ref, dc_ref, df_ref, dfb_ref, carry_ref):
        i = pl.program_id(0)

        @pl.when(i == 0)
        def _():
            carry_ref[...] = jnp.zeros_like(carry_ref)

        dc = dc_ref[...]
        r = lax.broadcasted_iota(jnp.int32, (tr, tr), 0)
        c = lax.broadcasted_iota(jnp.int32, (tr, tr), 1)
        triu = (r <= c).astype(F32)
        dlf = _dot(triu, dc, precision=HI) + carry_ref[...]
        carry_ref[...] += jnp.sum(dc, axis=0, keepdims=True)
        dz = dlf * jax.nn.sigmoid(-(f_ref[...] + fb_ref[...]))
        df_ref[...] = dz.astype(BF16)
        _acc_store(i, dfb_ref, jnp.sum(dz, axis=0, keepdims=True))

    rev = lambda i: (nb - 1 - i, 0)
    return _pcall(body, name="fox_prep_bwd", grid=(nb,),
                  in_specs=[pl.BlockSpec((tr, LANES), lambda i: (nb - 1 - i, cf)), _const((1, LANES)),
                            pl.BlockSpec((tr, LANES), rev)],
                  out_specs=[pl.BlockSpec((tr, LANES), rev), _const((1, LANES))],
                  out_shape=[jax.ShapeDtypeStruct((T, LANES), BF16), jax.ShapeDtypeStruct((1, LANES), F32)],
                  scratch_shapes=[pltpu.VMEM((1, LANES), F32)], compiler_params=_cparams(("arbitrary",)))(u, fb, dc)


def _attn_logits(q_ref, k_ref, c_ref, i, tq, T):
    s = _dot(q_ref[...].astype(BF16), k_ref[...].astype(BF16), "nt") * (FOX_HEAD_DIM ** -0.5) - c_ref[0]
    row = i * tq + lax.broadcasted_iota(jnp.int32, (tq, T), 0)
    col = lax.broadcasted_iota(jnp.int32, (tq, T), 1)
    return jnp.where(col <= row, s, -1e30)


def _attn_fwd(cfg, u, c_rows):
    T, FW, FH = cfg.T, cfg.FW, cfg.FH
    tq = min(256, T)
    dh = FOX_HEAD_DIM

    def body(q_ref, k_ref, v_ref, c_ref, o_ref, lse_ref):
        s = _attn_logits(q_ref, k_ref, c_ref, pl.program_id(1), tq, T)
        m = jnp.max(s, axis=1, keepdims=True)
        p = jnp.exp(s - m)
        l = jnp.sum(p, axis=1, keepdims=True)
        o_ref[...] = _dot(p.astype(BF16), v_ref[...].astype(BF16)) / l
        lse_ref[0] = m + jnp.log(l)

    return _pcall(
        body, name="fox_attn_fwd", grid=(FH, T // tq),
        in_specs=[pl.BlockSpec((tq, dh), lambda h, i: (i, h)), pl.BlockSpec((T, dh), lambda h, i: (0, FH + h)),
                  pl.BlockSpec((T, dh), lambda h, i: (0, 2 * FH + h)), pl.BlockSpec((1, 1, T), lambda h, i: (h, 0, 0))],
        out_specs=[pl.BlockSpec((tq, dh), lambda h, i: (i, h)), pl.BlockSpec((1, tq, 1), lambda h, i: (h, i, 0))],
        out_shape=[jax.ShapeDtypeStruct((T, FW), F32), jax.ShapeDtypeStruct((FH, T, 1), F32)],
        compiler_params=_cparams(("parallel", "arbitrary")),
    )(u, u, u, c_rows)


def _attn_bwd(cfg, u, c_rows, lse, do):
    T, FW, FH = cfg.T, cfg.FW, cfg.FH
    tq = min(256, T)
    dh = FOX_HEAD_DIM
    scale = dh ** -0.5

    def body(q_ref, k_ref, v_ref, c_ref, lse_ref, do_ref, dq_ref, dk_ref, dv_ref, dcol_ref):
        i = pl.program_id(1)
        s = _attn_logits(q_ref, k_ref, c_ref, i, tq, T)
        p = jnp.exp(s - lse_ref[0])
        do_v = do_ref[...]
        dp = _dot(do_v.astype(BF16), v_ref[...].astype(BF16), "nt")
        delta = jnp.sum(p * dp, axis=1, keepdims=True)
        ds = p * (dp - delta)
        ds16 = ds.astype(BF16)
        dq_ref[...] = (_dot(ds16, k_ref[...].astype(BF16)) * scale).astype(BF16)
        _acc_store(i, dk_ref, _dot(ds16, q_ref[...].astype(BF16), "tn") * scale)
        _acc_store(i, dv_ref, _dot(p.astype(BF16), do_v.astype(BF16), "tn"))
        _acc_store(i, dcol_ref, jnp.sum(ds, axis=0, keepdims=True)[None])

    qspec = pl.BlockSpec((tq, dh), lambda h, i: (i, h))
    return _pcall(
        body, name="fox_attn_bwd", grid=(FH, T // tq),
        in_specs=[qspec, pl.BlockSpec((T, dh), lambda h, i: (0, FH + h)),
                  pl.BlockSpec((T, dh), lambda h, i: (0, 2 * FH + h)), pl.BlockSpec((1, 1, T), lambda h, i: (h, 0, 0)),
                  pl.BlockSpec((1, tq, 1), lambda h, i: (h, i, 0)), qspec],
        out_specs=[qspec, pl.BlockSpec((T, dh), lambda h, i: (0, h)), pl.BlockSpec((T, dh), lambda h, i: (0, h)),
                   pl.BlockSpec((1, 1, T), lambda h, i: (h, 0, 0))],
        out_shape=[jax.ShapeDtypeStruct((T, FW), BF16), jax.ShapeDtypeStruct((T, FW), F32),
                   jax.ShapeDtypeStruct((T, FW), F32), jax.ShapeDtypeStruct((FH, 1, T), F32)],
        compiler_params=_cparams(("parallel", "arbitrary")),
    )(u, u, u, c_rows, lse, do)


def _head_indicators(cfg):
    ind = np.zeros((cfg.RW, LANES), np.float32)
    ind[np.arange(cfg.RW), np.arange(cfg.RW) // RWKV_HEAD_DIM] = 1.0
    pad = np.zeros((1, LANES), np.float32)
    pad[0, cfg.RH:] = 1.0
    return jnp.asarray(ind), jnp.asarray(ind.T.copy()), jnp.asarray(pad)


def _prep_fn(us_r, us_k, us_v, us_wd, us_ad, w0, w2p, a0, a2p, k_k, k_a, ind, ind_t, pad):
    wpre = w0 + _dot(jnp.tanh(us_wd), w2p, precision=HI)
    w = -_softplus(-wpre) - 0.5
    lw = -jnp.exp(w)
    a = jax.nn.sigmoid(a0 + _dot(us_ad, a2p, precision=HI))
    kk = us_k * k_k
    ss = _dot(kk * kk, ind, precision=HI) + pad
    inv = 1.0 / jnp.maximum(jnp.sqrt(ss), L2_EPS)
    kkn = kk * _dot(inv, ind_t, precision=HI)
    kp = us_k * (1.0 + (a - 1.0) * k_a)
    return us_r, lw, kp, us_v, -kkn, kkn * a


def _shifted(u, prev_row, mu, first):
    n = u.shape[0]
    rolled = pltpu.roll(u, 1, 0)
    row = lax.broadcasted_iota(jnp.int32, u.shape, 0)
    p0 = jnp.where(first, jnp.zeros_like(prev_row), prev_row)
    prev = jnp.where(row == 0, jnp.broadcast_to(p0, u.shape), rolled)
    return u + (prev - u) * mu, prev


def _rwkv_specs(cfg, tr):
    RW, LP = cfg.RW, cfg.LP
    base = cfg.o_rwkv // RW
    cols = [(RW, base), (RW, base + 1), (RW, base + 2), (RW, base + 3), (LP, cfg.o_wd // LP), (LP, cfg.o_ad // LP)]
    cur = [pl.BlockSpec((tr, w), (lambda i, cb=cb: (i, cb))) for w, cb in cols]
    prv = [pl.BlockSpec((8, w), (lambda i, cb=cb: (jnp.maximum(i * (tr // 8) - 1, 0), cb))) for w, cb in cols]
    return cols, cur, prv


def _mu_pieces(cfg, mu_ref):
    RW, LP = cfg.RW, cfg.LP
    offs = [0, RW, 2 * RW, 3 * RW, 4 * RW, 4 * RW + LP, 4 * RW + 2 * LP]
    return [mu_ref[:, offs[j]:offs[j + 1]] for j in range(6)]


def _rwkv_prep_fwd(cfg, u, mu, w0, w2p, a0, a2p, k_k, k_a):
    T, RW, LP, tr = cfg.T, cfg.RW, cfg.LP, cfg.tr
    ind, ind_t, pad = _head_indicators(cfg)
    cols, cur, prv = _rwkv_specs(cfg, tr)

    def body(*refs):
        u_refs, p_refs = refs[0:6], refs[6:12]
        mu_ref, w0_ref, w2_ref, a0_ref, a2_ref, kk_ref, ka_ref, ind_ref, indt_ref, pad_ref = refs[12:22]
        outs = refs[22:]
        first = pl.program_id(0) == 0
        mus = _mu_pieces(cfg, mu_ref)
        us = [_shifted(u_refs[j][...], p_refs[j][7:8, :], mus[j], first)[0] for j in range(6)]
        res = _prep_fn(us[0], us[1], us[2], us[4], us[5], w0_ref[...], w2_ref[...], a0_ref[...], a2_ref[...],
                       kk_ref[...], ka_ref[...], ind_ref[...], indt_ref[...], pad_ref[...])
        for j in range(6):
            outs[j][...] = res[j]
        outs[6][...] = us[3]

    consts = [mu, w0, w2p, a0, a2p, k_k, k_a, ind, ind_t, pad]
    return _pcall(body, name="rwkv_prep_fwd", grid=(T // tr,),
                  in_specs=cur + prv + [_const(c.shape) for c in consts],
                  out_specs=[_tile(tr, RW)] * 7, out_shape=[jax.ShapeDtypeStruct((T, RW), F32)] * 7,
                  compiler_params=_cparams(("parallel",)))(*([u] * 12), *consts)


def _rwkv_prep_bwd(cfg, u, mu, w0, w2p, a0, a2p, k_k, k_a, cots, dzb):
    T, RW, LP = cfg.T, cfg.RW, cfg.LP
    tr = min(128, T)
    ind, ind_t, pad = _head_indicators(cfg)
    cols, cur, prv = _rwkv_specs(cfg, tr)
    rseg = cfg.rseg

    def body(*refs):
        u_refs, p_refs = refs[0:6], refs[6:12]
        mu_ref, w0_ref, w2_ref, a0_ref, a2_ref, kk_ref, ka_ref, ind_ref, indt_ref, pad_ref = refs[12:22]
        cot_refs, dzb_ref = refs[22:28], refs[28]
        dus_ref, dmu_ref, dw0_ref, dw2_ref, da0_ref, da2_ref, dkk_ref, dka_ref = refs[29:]
        i = pl.program_id(0)
        first = i == 0
        mus = _mu_pieces(cfg, mu_ref)
        sh = [_shifted(u_refs[j][...], p_refs[j][7:8, :], mus[j], first) for j in range(6)]
        us = [s[0] for s in sh]
        fn = functools.partial(_prep_fn, ind=ind_ref[...], ind_t=indt_ref[...], pad=pad_ref[...])
        _, vjp = jax.vjp(fn, us[0], us[1], us[2], us[4], us[5], w0_ref[...], w2_ref[...], a0_ref[...], a2_ref[...],
                         kk_ref[...], ka_ref[...])
        d = vjp(tuple(c[...] for c in cot_refs))
        dus = [d[0], d[1], d[2], dzb_ref[...], d[3], d[4]]
        offs = [0, RW, 2 * RW, 3 * RW, 4 * RW, 4 * RW + LP, 4 * RW + 2 * LP]
        for j in range(6):
            dus_ref[:, offs[j]:offs[j + 1]] = dus[j]
            dmu_j = jnp.sum(dus[j] * (sh[j][1] - u_refs[j][...]), axis=0, keepdims=True)

            @pl.when(first)
            def _(j=j, dmu_j=dmu_j):
                dmu_ref[:, offs[j]:offs[j + 1]] = dmu_j

            @pl.when(i > 0)
            def _(j=j, dmu_j=dmu_j):
                dmu_ref[:, offs[j]:offs[j + 1]] += dmu_j
        for ref, val in zip((dw0_ref, dw2_ref, da0_ref, da2_ref, dkk_ref, dka_ref), d[5:11]):
            _acc_store(i, ref, val)

    consts = [mu, w0, w2p, a0, a2p, k_k, k_a, ind, ind_t, pad]
    vec = jax.ShapeDtypeStruct((1, RW), F32)
    mat = jax.ShapeDtypeStruct((LP, RW), F32)
    return _pcall(body, name="rwkv_prep_bwd", grid=(T // tr,),
                  in_specs=cur + prv + [_const(c.shape) for c in consts] + [_tile(tr, RW)] * 7,
                  out_specs=[_tile(tr, rseg), _const((1, rseg)), _const((1, RW)), _const((LP, RW)), _const((1, RW)),
                             _const((LP, RW)), _const((1, RW)), _const((1, RW))],
                  out_shape=[jax.ShapeDtypeStruct((T, rseg), F32), jax.ShapeDtypeStruct((1, rseg), F32),
                             vec, mat, vec, mat, vec, vec],
                  compiler_params=_cparams(("arbitrary",)))(*([u] * 12), *consts, *cots, dzb)


def _shift_bwd(cfg, dus, mu):
    T, tr, rseg = cfg.T, cfg.tr, cfg.rseg
    nb = T // tr

    def body(d_ref, n_ref, mu_ref, du_ref):
        d = d_ref[...]
        rolled = pltpu.roll(d, tr - 1, 0)
        row = lax.broadcasted_iota(jnp.int32, d.shape, 0)
        n0 = jnp.where(pl.program_id(0) == nb - 1, jnp.zeros_like(n_ref[0:1, :]), n_ref[0:1, :])
        nxt = jnp.where(row == tr - 1, jnp.broadcast_to(n0, d.shape), rolled)
        mu_v = mu_ref[...]
        du_ref[...] = (d * (1.0 - mu_v) + nxt * mu_v).astype(BF16)

    return _pcall(body, name="shift_bwd", grid=(nb,),
                  in_specs=[_tile(tr, rseg),
                            pl.BlockSpec((8, rseg), lambda i: (jnp.minimum((i + 1) * (tr // 8), T // 8 - 1), 0)),
                            _const((1, rseg))],
                  out_specs=_tile(tr, rseg), out_shape=jax.ShapeDtypeStruct((T, rseg), BF16),
                  compiler_params=_cparams(("parallel",)))(dus, dus, mu)


def _chunk_fn(S0, r, lw, k, v, a, b):
    H, C, K = r.shape
    row = lax.broadcasted_iota(jnp.int32, (C, C), 0)
    col = lax.broadcasted_iota(jnp.int32, (C, C), 1)
    incl = jnp.broadcast_to((row >= col).astype(F32)[None], (H, C, C))
    strict = (row > col)[None]
    lower = (row >= col)[None]
    L = _bdot(incl, lw, 2, 1)
    LC = jnp.sum(lw, axis=1, keepdims=True)
    eL = jnp.exp(L)
    eLn = jnp.exp(-L)
    at = a * jnp.exp(L - lw)
    rt = r * eL
    bt = b * eLn
    kt = k * eLn
    eR = jnp.exp(LC - L)
    zero = jnp.zeros((), F32)
    n_ab = jnp.where(strict, _bdot(at, bt, 2, 2), zero)
    n_ak = jnp.where(strict, _bdot(at, kt, 2, 2), zero)
    m_rb = jnp.where(lower, _bdot(rt, bt, 2, 2), zero)
    m_rk = jnp.where(lower, _bdot(rt, kt, 2, 2), zero)
    U = _bdot(at, S0, 2, 2) + _bdot(n_ak, v, 2, 1)
    M = n_ab
    steps = max(1, int(np.ceil(np.log2(C))))
    for s in range(steps):
        U = U + _bdot(M, U, 2, 1)
        if s + 1 < steps:
            M = _bdot(M, M, 2, 1)
    Y = _bdot(rt, S0, 2, 2) + _bdot(m_rb, U, 2, 1) + _bdot(m_rk, v, 2, 1)
    S1 = S0 * jnp.exp(LC) + _bdot(U, b * eR, 1, 1) + _bdot(v, k * eR, 1, 1)
    return Y, S1


def _scan_fwd(cfg, seqs):
    T, RH, C = cfg.T, cfg.RH, cfg.C
    N = RWKV_HEAD_DIM
    HB = 2
    nc = T // C

    def body(r_ref, lw_ref, k_ref, v_ref, a_ref, b_ref, y_ref, ck_ref, s_ref):
        @pl.when(pl.program_id(1) == 0)
        def _():
            s_ref[...] = jnp.zeros_like(s_ref)

        S0 = s_ref[...]
        ck_ref[:, 0] = S0
        Y, S1 = _chunk_fn(S0, r_ref[...], lw_ref[...], k_ref[...], v_ref[...], a_ref[...], b_ref[...])
        y_ref[...] = Y
        s_ref[...] = S1

    seq = pl.BlockSpec((HB, C, N), lambda h, j: (h, j, 0))
    return _pcall(body, name="rwkv_scan_fwd", grid=(RH // HB, nc), in_specs=[seq] * 6,
                  out_specs=[seq, pl.BlockSpec((HB, 1, N, N), lambda h, j: (h, j, 0, 0))],
                  out_shape=[jax.ShapeDtypeStruct((RH, T, N), F32), jax.ShapeDtypeStruct((RH, nc, N, N), F32)],
                  scratch_shapes=[pltpu.VMEM((HB, N, N), F32)],
                  compiler_params=_cparams(("parallel", "arbitrary")))(*seqs)


def _scan_bwd(cfg, seqs, ckpt, dy):
    T, RH, C = cfg.T, cfg.RH, cfg.C
    N = RWKV_HEAD_DIM
    HB = 2
    nc = T // C

    def body(r_ref, lw_ref, k_ref, v_ref, a_ref, b_ref, ck_ref, dy_ref, *rest):
        outs, ds_ref = rest[:6], rest[6]

        @pl.when(pl.program_id(1) == 0)
        def _():
            ds_ref[...] = jnp.zeros_like(ds_ref)

        _, vjp = jax.vjp(_chunk_fn, ck_ref[:, 0], r_ref[...], lw_ref[...], k_ref[...], v_ref[...], a_ref[...],
                         b_ref[...])
        d = vjp((dy_ref[...], ds_ref[...]))
        ds_ref[...] = d[0]
        for j in range(6):
            outs[j][...] = d[1 + j]

    seq = pl.BlockSpec((HB, C, N), lambda h, j: (h, nc - 1 - j, 0))
    return _pcall(body, name="rwkv_scan_bwd", grid=(RH // HB, nc),
                  in_specs=[seq] * 6 + [pl.BlockSpec((HB, 1, N, N), lambda h, j: (h, nc - 1 - j, 0, 0)), seq],
                  out_specs=[seq] * 6, out_shape=[jax.ShapeDtypeStruct((RH, T, N), F32)] * 6,
                  scratch_shapes=[pltpu.VMEM((HB, N, N), F32)],
                  compiler_params=_cparams(("parallel", "arbitrary")))(*seqs, ckpt, dy)


def _post_fn(y, r, kp, v, zb, ln_w, ln_b, rk, ind, ind_t):
    n = float(RWKV_HEAD_DIM)
    mu = _dot(_dot(y, ind, precision=HI) / n, ind_t, precision=HI)
    yc = y - mu
    var = _dot(yc * yc, ind, precision=HI) / n
    rstd = _dot(lax.rsqrt(var + GN_EPS), ind_t, precision=HI)
    yn = yc * rstd * ln_w + ln_b
    bonus = _dot(_dot(r * kp * rk, ind, precision=HI), ind_t, precision=HI) * v
    return (yn + bonus) * _silu(zb)


def _rwkv_post_fwd(cfg, y, r, kp, v, zb, ln_w, ln_b, rk):
    T, RW, tr = cfg.T, cfg.RW, cfg.tr
    ind, ind_t, _ = _head_indicators(cfg)

    def body(y_ref, r_ref, k_ref, v_ref, z_ref, lw_ref, lb_ref, rk_ref, ind_ref, indt_ref, ob_ref):
        ob_ref[...] = _post_fn(y_ref[...], r_ref[...], k_ref[...], v_ref[...], z_ref[...], lw_ref[...], lb_ref[...],
                               rk_ref[...], ind_ref[...], indt_ref[...]).astype(BF16)

    consts = [ln_w, ln_b, rk, ind, ind_t]
    return _pcall(body, name="rwkv_post_fwd", grid=(T // tr,),
                  in_specs=[_tile(tr, RW)] * 5 + [_const(c.shape) for c in consts],
                  out_specs=_tile(tr, RW), out_shape=jax.ShapeDtypeStruct((T, RW), BF16),
                  compiler_params=_cparams(("parallel",)))(y, r, kp, v, zb, *consts)


def _rwkv_post_bwd(cfg, y, r, kp, v, zb, ln_w, ln_b, rk, dob):
    T, RW = cfg.T, cfg.RW
    tr = min(128, T)
    ind, ind_t, _ = _head_indicators(cfg)

    def body(y_ref, r_ref, k_ref, v_ref, z_ref, lw_ref, lb_ref, rk_ref, ind_ref, indt_ref, dob_ref,
             dy_ref, dr_ref, dk_ref, dv_ref, dz_ref, dlw_ref, dlb_ref, drk_ref):
        fn = functools.partial(_post_fn, ind=ind_ref[...], ind_t=indt_ref[...])
        _, vjp = jax.vjp(fn, y_ref[...], r_ref[...], k_ref[...], v_ref[...], z_ref[...], lw_ref[...], lb_ref[...],
                         rk_ref[...])
        d = vjp(dob_ref[...])
        for ref, val in zip((dy_ref, dr_ref, dk_ref, dv_ref, dz_ref), d[:5]):
            ref[...] = val
        i = pl.program_id(0)
        for ref, val in zip((dlw_ref, dlb_ref, drk_ref), d[5:8]):
            _acc_store(i, ref, val)

    consts = [ln_w, ln_b, rk, ind, ind_t]
    vec = jax.ShapeDtypeStruct((1, RW), F32)
    return _pcall(body, name="rwkv_post_bwd", grid=(T // tr,),
                  in_specs=[_tile(tr, RW)] * 5 + [_const(c.shape) for c in consts] + [_tile(tr, RW)],
                  out_specs=[_tile(tr, RW)] * 5 + [_const((1, RW))] * 3,
                  out_shape=[jax.ShapeDtypeStruct((T, RW), F32)] * 5 + [vec] * 3,
                  compiler_params=_cparams(("arbitrary",)))(y, r, kp, v, zb, *consts, dob)


def _adamw_math(w, g, m, v):
    m = ADAM_B1 * m + (1.0 - ADAM_B1) * g
    v = ADAM_B2 * v + (1.0 - ADAM_B2) * (g * g)
    m_hat = m / (1.0 - ADAM_B1 ** ADAM_STEP)
    v_hat = v / (1.0 - ADAM_B2 ** ADAM_STEP)
    delta = -ADAM_LR * (m_hat / (jnp.sqrt(v_hat) + ADAM_EPS) + ADAM_WD * w)
    return delta, m, v


def _adamw(name, w, g, m, v):
    R, Cc = w.shape
    tr = R
    for cand in (512, 256, 128, 64, 32, 16, 8):
        if R % cand == 0 and cand * Cc * 4 <= 2 * 1024 * 1024:
            tr = cand
            break

    def body(w_ref, g_ref, m_ref, v_ref, d_ref, nm_ref, nv_ref):
        d, nm, nv = _adamw_math(w_ref[...], g_ref[...], m_ref[...], v_ref[...])
        d_ref[...] = d
        nm_ref[...] = nm
        nv_ref[...] = nv

    spec = _tile(tr, Cc)
    return _pcall(body, name=name, grid=(R // tr,), in_specs=[spec] * 4, out_specs=[spec] * 3,
                  out_shape=[jax.ShapeDtypeStruct((R, Cc), F32)] * 3,
                  compiler_params=_cparams(("parallel",)))(w, g, m, v)


def _row_tile(R, Cc, itemsize, budget=2 * 1024 * 1024):
    for cand in (1024, 512, 256, 128, 64, 32, 16):
        if R % cand == 0 and cand * Cc * itemsize <= budget:
            return cand
    return R


def _add_halves(name, gs, r1, c_idx):
    _, R, Cc = gs.shape
    half = R // 2
    tr = _row_tile(half, Cc, 4)
    nb = half // tr

    def body(c_ref, g_ref, r_ref, o_ref):
        o_ref[...] = (g_ref[...].astype(F32) + r_ref[...].astype(F32)).astype(BF16)

    grid_spec = pltpu.PrefetchScalarGridSpec(
        num_scalar_prefetch=1, grid=(N_CHIPS, nb),
        in_specs=[pl.BlockSpec((1, tr, Cc), lambda s, i, c: (s, c[0] * nb + i, 0)),
                  pl.BlockSpec((1, tr, Cc), lambda s, i, c: (s, i, 0))],
        out_specs=pl.BlockSpec((1, tr, Cc), lambda s, i, c: (s, i, 0)))
    return _pcall(body, name=name, grid_spec=grid_spec, out_shape=jax.ShapeDtypeStruct((N_CHIPS, half, Cc), BF16),
                  compiler_params=_cparams(("parallel", "parallel")))(c_idx, gs, r1)


def _sum_slots(name, r2):
    S, R, Cc = r2.shape
    tr = _row_tile(R, Cc, 4 * S // 2 if r2.dtype == BF16 else 4 * S)

    def body(r_ref, o_ref):
        acc = r_ref[0].astype(F32)
        for s in range(1, S):
            acc = acc + r_ref[s].astype(F32)
        o_ref[...] = acc

    return _pcall(body, name=name, grid=(R // tr,), in_specs=[pl.BlockSpec((S, tr, Cc), lambda i: (0, i, 0))],
                  out_specs=_tile(tr, Cc), out_shape=jax.ShapeDtypeStruct((R, Cc), F32),
                  compiler_params=_cparams(("parallel",)))(r2)


def _cast_bf16(name, w):
    R, Cc = w.shape
    tr = _row_tile(R, Cc, 4)

    def body(w_ref, o_ref):
        o_ref[...] = w_ref[...].astype(BF16)

    return _pcall(body, name=name, grid=(R // tr,), in_specs=[_tile(tr, Cc)], out_specs=_tile(tr, Cc),
                  out_shape=jax.ShapeDtypeStruct((R, Cc), BF16), compiler_params=_cparams(("parallel",)))(w)


_ANY = pl.BlockSpec(memory_space=pl.ANY)


def _place():
    x, y, c = lax.axis_index("x"), lax.axis_index("y"), lax.axis_index("c")
    others = [(1 - x, y), (x, 1 - y), (1 - x, 1 - y)]
    return x, y, c, others


def _gather_weights(shards):
    n = len(shards)
    halves = [s.shape[0] // 2 for s in shards]

    def body(*refs):
        ins, outs = refs[:n], refs[n:2 * n]
        send_sems, recv_sems, local_sems = refs[2 * n:]
        x, y, c, others = _place()
        me = 2 * x + y

        def rows(k, ref, chip, hc):
            return ref.at[chip, pl.ds(hc * halves[k], halves[k]), :]

        def remote(k, j, src, dst, to):
            return pltpu.make_async_remote_copy(src_ref=src, dst_ref=dst, send_sem=send_sems.at[6 * k + j],
                                                recv_sem=recv_sems.at[6 * k + j], device_id=to, device_id_type=MESH)

        local = [pltpu.make_async_copy(ins[k], outs[k].at[me], local_sems.at[k]) for k in range(n)]
        for cp in local:
            cp.start()
        first, passed = [], []
        for k in range(n):
            mine = ins[k].at[pl.ds(c * halves[k], halves[k]), :]
            for j, (px, py) in enumerate(others):
                cp = remote(k, j, mine, rows(k, outs[k], me, c), (px, py, c))
                cp.start()
                first.append(cp)
        for k in range(n):
            for j, (px, py) in enumerate(others):
                land = rows(k, outs[k], 2 * px + py, c)
                remote(k, j, land, land, (x, y, c)).wait_recv()
                cp = remote(k, 3 + j, land, land, (x, y, 1 - c))
                cp.start()
                passed.append(cp)
        for k in range(n):
            for j, (px, py) in enumerate(others):
                land = rows(k, outs[k], 2 * px + py, 1 - c)
                remote(k, 3 + j, land, land, (x, y, c)).wait_recv()
        for cp in first + passed:
            cp.wait_send()
        for cp in local:
            cp.wait()

    return _pcall(
        body, name="gather_weights", in_specs=[_ANY] * n, out_specs=[_ANY] * n,
        out_shape=[jax.ShapeDtypeStruct((N_CHIPS,) + s.shape, s.dtype) for s in shards],
        scratch_shapes=[pltpu.SemaphoreType.DMA((6 * n,)), pltpu.SemaphoreType.DMA((6 * n,)),
                        pltpu.SemaphoreType.DMA((n,))],
    )(*shards)


def _exchange_halves(grads):
    n = len(grads)
    halves = [g.shape[1] // 2 for g in grads]

    def body(*refs):
        ins, outs = refs[:n], refs[n:2 * n]
        send_sems, recv_sems = refs[2 * n:]
        x, y, c, _ = _place()
        cps = []
        for k in range(n):
            src = ins[k].at[:, pl.ds((1 - c) * halves[k], halves[k]), :]
            cp = pltpu.make_async_remote_copy(src_ref=src, dst_ref=outs[k], send_sem=send_sems.at[k],
                                              recv_sem=recv_sems.at[k], device_id=(x, y, 1 - c), device_id_type=MESH)
            cp.start()
            cps.append(cp)
        for cp in cps:
            cp.wait()

    return _pcall(
        body, name="exchange_halves", in_specs=[_ANY] * n, out_specs=[_ANY] * n,
        out_shape=[jax.ShapeDtypeStruct((N_CHIPS, h) + g.shape[2:], g.dtype) for g, h in zip(grads, halves)],
        scratch_shapes=[pltpu.SemaphoreType.DMA((n,)), pltpu.SemaphoreType.DMA((n,))],
    )(*grads)


def _scatter_to_owners(chip_sums, small):
    n = len(chip_sums)

    def body(*refs):
        ins, small_in = refs[:n], refs[n]
        outs, small_out = refs[n + 1:2 * n + 1], refs[2 * n + 1]
        send_sems, recv_sems, local_sems, ssend, srecv = refs[2 * n + 2:]
        x, y, c, others = _place()
        me = 2 * x + y
        dev = 2 * me + c
        local = [pltpu.make_async_copy(ins[k].at[me], outs[k].at[me], local_sems.at[k]) for k in range(n)]
        local.append(pltpu.make_async_copy(small_in, small_out.at[dev], local_sems.at[n]))
        for cp in local:
            cp.start()
        sends = []
        for k in range(n):
            for j, (px, py) in enumerate(others):
                cp = pltpu.make_async_remote_copy(
                    src_ref=ins[k].at[2 * px + py], dst_ref=outs[k].at[me], send_sem=send_sems.at[3 * k + j],
                    recv_sem=recv_sems.at[3 * k + j], device_id=(px, py, c), device_id_type=MESH)
                cp.start()
                sends.append(cp)
        rel = [(dx, dy, dc) for dx in (0, 1) for dy in (0, 1) for dc in (0, 1)][1:]
        for r, (dx, dy, dc) in enumerate(rel):
            to = (x ^ dx, y ^ dy, c ^ dc)
            cp = pltpu.make_async_remote_copy(src_ref=small_in, dst_ref=small_out.at[dev], send_sem=ssend.at[r],
                                              recv_sem=srecv.at[r], device_id=to, device_id_type=MESH)
            cp.start()
            sends.append(cp)
        for k in range(n):
            for j, (px, py) in enumerate(others):
                land = outs[k].at[2 * px + py]
                pltpu.make_async_remote_copy(src_ref=land, dst_ref=land, send_sem=send_sems.at[3 * k + j],
                                             recv_sem=recv_sems.at[3 * k + j], device_id=(x, y, c),
                                             device_id_type=MESH).wait_recv()
        for r, (dx, dy, dc) in enumerate(rel):
            land = small_out.at[4 * (x ^ dx) + 2 * (y ^ dy) + (c ^ dc)]
            pltpu.make_async_remote_copy(src_ref=land, dst_ref=land, send_sem=ssend.at[r], recv_sem=srecv.at[r],
                                         device_id=(x, y, c), device_id_type=MESH).wait_recv()
        for cp in sends:
            cp.wait_send()
        for cp in local:
            cp.wait()

    return _pcall(
        body, name="scatter_to_owners", in_specs=[_ANY] * (n + 1), out_specs=[_ANY] * (n + 1),
        out_shape=[jax.ShapeDtypeStruct(g.shape, g.dtype) for g in chip_sums]
        + [jax.ShapeDtypeStruct((N_DEV,) + small.shape, small.dtype)],
        scratch_shapes=[pltpu.SemaphoreType.DMA((3 * n,)), pltpu.SemaphoreType.DMA((3 * n,)),
                        pltpu.SemaphoreType.DMA((n + 1,)), pltpu.SemaphoreType.DMA((7,)),
                        pltpu.SemaphoreType.DMA((7,))],
    )(*chip_sums, small)


def _join_halves(halves):
    n = len(halves)
    hs = [h.shape[0] for h in halves]

    def body(*refs):
        ins, outs = refs[:n], refs[n:2 * n]
        send_sems, recv_sems, local_sems = refs[2 * n:]
        x, y, c, _ = _place()
        cps, loc = [], []
        for k in range(n):
            dst = outs[k].at[pl.ds(c * hs[k], hs[k]), :]
            lc = pltpu.make_async_copy(ins[k], dst, local_sems.at[k])
            lc.start()
            loc.append(lc)
            cp = pltpu.make_async_remote_copy(src_ref=ins[k], dst_ref=dst, send_sem=send_sems.at[k],
                                              recv_sem=recv_sems.at[k], device_id=(x, y, 1 - c), device_id_type=MESH)
            cp.start()
            cps.append(cp)
        for k in range(n):
            land = outs[k].at[pl.ds((1 - c) * hs[k], hs[k]), :]
            pltpu.make_async_remote_copy(src_ref=land, dst_ref=land, send_sem=send_sems.at[k],
                                         recv_sem=recv_sems.at[k], device_id=(x, y, c), device_id_type=MESH).wait_recv()
        for cp in cps:
            cp.wait_send()
        for lc in loc:
            lc.wait()

    return _pcall(
        body, name="join_halves", in_specs=[_ANY] * n, out_specs=[_ANY] * n,
        out_shape=[jax.ShapeDtypeStruct((2 * h.shape[0],) + h.shape[1:], h.dtype) for h in halves],
        scratch_shapes=[pltpu.SemaphoreType.DMA((n,)), pltpu.SemaphoreType.DMA((n,)), pltpu.SemaphoreType.DMA((n,))],
    )(*halves)


def _heads(cfg, a):
    return a.reshape(cfg.T, cfg.RH, RWKV_HEAD_DIM).transpose(1, 0, 2)


def _unheads(cfg, a):
    return a.transpose(1, 0, 2).reshape(cfg.T, cfg.RW)


def _local_step(cfg, x2, target, norm_gain, w_my, fb, mu_g, w0, w2, a0, a2, k_k, k_a, r_k, ln_w, ln_b, wpf, wpr, wout,
                fng):
    T, D, FW, FH, RW, RH, LP, lora = cfg.T, cfg.D, cfg.FW, cfg.FH, cfg.RW, cfg.RH, cfg.LP, cfg.lora
    fb_p = jnp.pad(fb, ((0, 0), (0, LANES - FH)))
    mu = _rwkv_vec_to_my(cfg, mu_g)
    w2p = jnp.pad(w2, ((0, LP - lora), (0, 0)))
    a2p = jnp.pad(a2, ((0, LP - lora), (0, 0)))
    rk = r_k.reshape(1, RW)
    tm = min(1024, T)

    h = _rms_fwd(cfg, x2, norm_gain)
    u = _mm("in_proj", h, w_my, "nn", F32, tm, cfg.tn, 512)
    c_cols = _fox_prep(cfg, u, fb_p)
    c_rows = c_cols[:, :FH].T.reshape(FH, 1, T)
    o, lse = _attn_fwd(cfg, u, c_rows)
    oa = _gate_a_fwd(cfg, o, u)
    prep = _rwkv_prep_fwd(cfg, u, mu, w0, w2p, a0, a2p, k_k, k_a)
    r, lw, kp, v, an, b, zb = prep
    seqs = [_heads(cfg, t) for t in (r, lw, kp, v, an, b)]
    y_h, ckpt = _scan_fwd(cfg, seqs)
    y = _unheads(cfg, y_h)
    ob = _rwkv_post_fwd(cfg, y, r, kp, v, zb, ln_w, ln_b, rk)
    pa = _mm("proj_fox", oa, wpf, "nn", F32, tm, 1024, 512)
    pb = _mm("proj_rwkv", ob, wpr, "nn", F32, tm, 1024, 512)
    m = _merge_fwd(cfg, pa, pb, u)
    mo = _mm("out_proj", m, wout, "nn", F32, tm, 1024, 512)
    loss8, dres, dres16, d_fng = _final(cfg, x2, mo, fng.reshape(1, D), target)

    dm = _mm("out_proj_dx", dres16, wout, "nt", F32, tm, 1024, 512)
    d_wout = _mm("out_proj_dw", m, dres16, "tn", BF16, 1024, 1024, 512)
    dpa, dpb, dgate = _merge_bwd(cfg, pa, pb, u, dm)
    doa = _mm("proj_fox_dx", dpa, wpf, "nt", F32, tm, 1024, 512)
    d_wpf = _mm("proj_fox_dw", oa, dpa, "tn", BF16, 1024, 1024, 512)
    dob = _mm("proj_rwkv_dx", dpb, wpr, "nt", F32, tm, 1024, 512)
    d_wpr = _mm("proj_rwkv_dw", ob, dpb, "tn", BF16, 1024, 1024, 512)

    do, dza = _gate_a_bwd(cfg, o, u, doa)
    dq, dk, dv, dcol = _attn_bwd(cfg, u, c_rows, lse, do)
    dc = jnp.pad(-dcol.reshape(FH, T).T, ((0, 0), (0, LANES - FH)))
    df, d_fb = _fox_prep_bwd(cfg, u, fb_p, dc)

    dy, dr_p, dk_p, dv_p, dzb, d_lnw, d_lnb, d_rk = _rwkv_post_bwd(cfg, y, r, kp, v, zb, ln_w, ln_b, rk, dob)
    dseq = _scan_bwd(cfg, seqs, ckpt, _heads(cfg, dy))
    dr_s, dlw_s, dk_s, dv_s, da_s, db_s = [_unheads(cfg, t) for t in dseq]
    cots = [dr_s + dr_p, dlw_s, dk_s + dk_p, dv_s + dv_p, da_s, db_s]
    dus, d_mu, d_w0, d_w2p, d_a0, d_a2p, d_kk, d_ka = _rwkv_prep_bwd(cfg, u, mu, w0, w2p, a0, a2p, k_k, k_a, cots, dzb)
    du_rwkv = _shift_bwd(cfg, dus, mu)

    pad_f = jnp.zeros((T, cfg.ncol - cfg.o_ad - LP), BF16)
    du = jnp.concatenate([dq, dk.astype(BF16), dv.astype(BF16), dza, du_rwkv[:, :4 * RW], dgate, df,
                          du_rwkv[:, 4 * RW:], pad_f], axis=1)
    dh = _mm("in_proj_dx", du, w_my, "nt", F32, tm, 1024, cfg.tn)
    d_wmy = _mm("in_proj_dw", h, du, "tn", BF16, 1024, cfg.tn, 512)
    gx, d_ng = _rms_bwd(cfg, x2, norm_gain, dh, dres)

    small = dict(norm_gain=d_ng, fox_forget_bias=d_fb[:, :FH], rwkv_shift_mix=_rwkv_vec_from_my(cfg, d_mu),
                 rwkv_w0=d_w0, rwkv_a0=d_a0, rwkv_k_k=d_kk, rwkv_k_a=d_ka, rwkv_r_k=d_rk, rwkv_ln_w=d_lnw,
                 rwkv_ln_b=d_lnb, final_norm_gain=d_fng)
    big = dict(w_in=d_wmy, rwkv_w2=d_w2p[:lora], rwkv_a2=d_a2p[:lora], w_proj_fox=d_wpf, w_proj_rwkv=d_wpr,
               w_out=d_wout)
    return loss8[0, 0], gx, small, big


_SMALL = ["norm_gain", "fox_forget_bias", "rwkv_shift_mix", "rwkv_w0", "rwkv_a0", "rwkv_k_k", "rwkv_k_a", "rwkv_r_k",
          "rwkv_ln_w", "rwkv_ln_b", "final_norm_gain"]
_WEIGHTS = ["norm_gain", "w_in", "fox_forget_bias", "rwkv_shift_mix", "rwkv_w0", "rwkv_w2", "rwkv_a0", "rwkv_a2",
            "rwkv_k_k", "rwkv_k_a", "rwkv_r_k", "rwkv_ln_w", "rwkv_ln_b", "w_proj_fox", "w_proj_rwkv", "w_out",
            "final_norm_gain"]


def _pack_small(arrs):
    parts = []
    for a in arrs:
        f = a.reshape(-1)
        parts.append(jnp.pad(f, (0, (-f.shape[0]) % LANES)))
    flat = jnp.concatenate(parts)
    rows = flat.shape[0] // LANES
    flat = jnp.pad(flat, (0, ((-rows) % 8) * LANES))
    return flat.reshape(-1, LANES)


def _unpack_small(packed, shapes):
    flat = packed.reshape(-1)
    out, pos = [], 0
    for s in shapes:
        n = int(np.prod(s))
        out.append(flat[pos:pos + n].reshape(s))
        pos += n + ((-n) % LANES)
    return out


def _shard_major(a, axis):
    parts = jnp.split(a, N_CHIPS, axis=axis)
    return jnp.stack(parts, axis=0)


def kernel(x, norm_gain, w_in, fox_forget_bias, rwkv_shift_mix, rwkv_w0, rwkv_w2, rwkv_a0, rwkv_a2, rwkv_k_k, rwkv_k_a, rwkv_r_k, rwkv_ln_w, rwkv_ln_b, w_proj_fox, w_proj_rwkv, w_out, final_norm_gain, loss_target, m_norm_gain, m_w_in, m_fox_forget_bias, m_rwkv_shift_mix, m_rwkv_w0, m_rwkv_w2, m_rwkv_a0, m_rwkv_a2, m_rwkv_k_k, m_rwkv_k_a, m_rwkv_r_k, m_rwkv_ln_w, m_rwkv_ln_b, m_w_proj_fox, m_w_proj_rwkv, m_w_out, m_final_norm_gain, v_norm_gain, v_w_in, v_fox_forget_bias, v_rwkv_shift_mix, v_rwkv_w0, v_rwkv_w2, v_rwkv_a0, v_rwkv_a2, v_rwkv_k_k, v_rwkv_k_a, v_rwkv_r_k, v_rwkv_ln_w, v_rwkv_ln_b, v_w_proj_fox, v_w_proj_rwkv, v_w_out, v_final_norm_gain):
    args = dict(locals())
    T, D = x.shape[1], x.shape[2]
    lora = rwkv_w2.shape[1]
    cfg = _Cfg(T, D, lora)
    RW = cfg.RW
    c_idx = lax.axis_index("c").astype(jnp.int32).reshape(1)

    w_in_s = w_in[0]
    wp_s = jnp.concatenate([w_proj_fox[0], w_proj_rwkv[0]], axis=0)
    lora_s = jnp.concatenate([rwkv_w2[0], rwkv_a2[0]], axis=0)
    g_in, g_wp, g_out, g_lora = _gather_weights([
        _cast_bf16("cast_w_in", w_in_s), _cast_bf16("cast_w_proj", wp_s), _cast_bf16("cast_w_out", w_out[0]), lora_s])
    w_glob = g_in.transpose(1, 0, 2).reshape(D, cfg.in_cols)
    w_my = _to_my_layout(cfg, w_glob)
    wp = g_wp.transpose(1, 0, 2).reshape(2 * RW, D)
    wout = g_out.reshape(D, D)
    lo = g_lora.transpose(1, 0, 2).reshape(2 * lora, RW)

    loss_dev, gx, small, big = _local_step(
        cfg, x[0], loss_target[0], norm_gain, w_my, fox_forget_bias, rwkv_shift_mix, rwkv_w0, lo[:lora], rwkv_a0,
        lo[lora:], rwkv_k_k, rwkv_k_a, rwkv_r_k, rwkv_ln_w, rwkv_ln_b, wp[:RW], wp[RW:], wout, final_norm_gain)
    loss = lax.psum(loss_dev, ("x", "y", "c"))

    gs_in = _shard_major(_from_my_layout(cfg, big["w_in"]), 1)
    gs_wp = _shard_major(jnp.concatenate([big["w_proj_fox"], big["w_proj_rwkv"]], axis=0), 1)
    gs_out = _shard_major(big["w_out"], 0)
    gs_lora = _shard_major(jnp.concatenate([big["rwkv_w2"], big["rwkv_a2"]], axis=0).astype(BF16), 1)
    gs = [gs_in, gs_wp, gs_out, gs_lora]
    names = ["w_in", "w_proj", "w_out", "lora"]
    recv1 = _exchange_halves(gs)
    chip_sums = [_add_halves("add_halves_" + nm, g, r, c_idx) for nm, g, r in zip(names, gs, recv1)]
    small_shapes = [args[nm].shape for nm in _SMALL]
    packed = _pack_small([small[nm] for nm in _SMALL])
    *recv2, small_all = _scatter_to_owners(chip_sums, packed)
    red_halves = [_sum_slots("sum_chips_" + nm, r) for nm, r in zip(names, recv2)]
    g_small = _sum_slots("sum_small", small_all)
    g_in_f, g_wp_f, g_out_f, g_lora_f = _join_halves(red_halves)

    grads = dict(zip(_SMALL, _unpack_small(g_small, small_shapes)))
    grads["w_in"] = g_in_f[None]
    grads["w_proj_fox"] = g_wp_f[None, :RW]
    grads["w_proj_rwkv"] = g_wp_f[None, RW:]
    grads["w_out"] = g_out_f[None]
    grads["rwkv_w2"] = g_lora_f[None, :lora]
    grads["rwkv_a2"] = g_lora_f[None, lora:]

    delta, new_m, new_v = {}, {}, {}
    w_small = _pack_small([args[nm] for nm in _SMALL])
    m_small = _pack_small([args["m_" + nm] for nm in _SMALL])
    v_small = _pack_small([args["v_" + nm] for nm in _SMALL])
    d_s, m_s, v_s = _adamw("adamw_small", w_small, g_small, m_small, v_small)
    for tgt, pk in ((delta, d_s), (new_m, m_s), (new_v, v_s)):
        tgt.update(zip(_SMALL, _unpack_small(pk, small_shapes)))
    for nm in ("w_in", "w_proj_fox", "w_proj_rwkv", "w_out", "rwkv_w2", "rwkv_a2"):
        shp = args[nm].shape
        two_d = (shp[1], shp[2])
        d_b, m_b, v_b = _adamw("adamw_" + nm, args[nm].reshape(two_d), grads[nm].reshape(two_d),
                               args["m_" + nm].reshape(two_d), args["v_" + nm].reshape(two_d))
        delta[nm], new_m[nm], new_v[nm] = d_b.reshape(shp), m_b.reshape(shp), v_b.reshape(shp)

    return (loss, gx[None], *[grads[n] for n in _WEIGHTS], *[delta[n] for n in _WEIGHTS],
            *[new_m[n] for n in _WEIGHTS], *[new_v[n] for n in _WEIGHTS])
```

```python
import functools

import numpy as np
import jax
import jax.numpy as jnp
from jax import lax
from jax.experimental import pallas as pl
from jax.experimental.pallas import tpu as pltpu

F32 = jnp.float32
BF16 = jnp.bfloat16
HI = lax.Precision.HIGHEST
MESH = pl.DeviceIdType.MESH

FOX_HEAD_DIM = 128
RWKV_HEAD_DIM = 64
RMS_EPS = 1e-6
GN_EPS = 64e-5
L2_EPS = 1e-12
ADAM_LR = 0.001
ADAM_B1 = 0.9
ADAM_B2 = 0.999
ADAM_EPS = 1e-08
ADAM_WD = 0.01
ADAM_STEP = 10

LANES = 128
VMEM_LIMIT = 56 * 1024 * 1024
SCAN_CHUNK = 64
SCAN_HEADS_PER_STEP = 4
N_CHIPS = 4
N_DEV = 8

_pcall = pl.pallas_call


def _cparams(sem=None):
    return pltpu.CompilerParams(dimension_semantics=sem, vmem_limit_bytes=VMEM_LIMIT)


def _softplus(x):
    return jnp.maximum(x, 0.0) + jnp.log(1.0 + jnp.exp(-jnp.abs(x)))


def _silu(z):
    return z * jax.nn.sigmoid(z)


def _rmsn(x, g):
    return x * lax.rsqrt(jnp.mean(x * x, axis=-1, keepdims=True) + RMS_EPS) * g


def _dot(a, b, dims="nn", precision=None):
    dn = {"nn": (((1,), (0,)), ((), ())), "nt": (((1,), (1,)), ((), ())), "tn": (((0,), (0,)), ((), ()))}[dims]
    return lax.dot_general(a, b, dn, precision=precision, preferred_element_type=F32)


def _split_bf16(x):
    hi = x.astype(BF16)
    return hi, (x - hi.astype(F32)).astype(BF16)


def _bdot_raw(a, b, ca, cb):
    dn = (((ca,), (cb,)), ((0,), (0,)))
    ah, al = _split_bf16(a)
    bh, bl = _split_bf16(b)
    mm = lambda p, q: lax.dot_general(p, q, dn, preferred_element_type=F32)
    return mm(ah, bh) + (mm(ah, bl) + mm(al, bh))


@functools.partial(jax.custom_vjp, nondiff_argnums=(2, 3))
def _bdot(a, b, ca, cb):
    return _bdot_raw(a, b, ca, cb)


def _bdot_fwd(a, b, ca, cb):
    return _bdot_raw(a, b, ca, cb), (a, b)


def _bdot_bwd(ca, cb, res, g):
    a, b = res
    if (ca, cb) == (2, 1):
        return _bdot(g, b, 2, 2), _bdot(a, g, 1, 1)
    if (ca, cb) == (2, 2):
        return _bdot(g, b, 2, 1), _bdot(g, a, 1, 1)
    assert (ca, cb) == (1, 1)
    return _bdot(b, g, 2, 2), _bdot(a, g, 2, 1)


_bdot.defvjp(_bdot_fwd, _bdot_bwd)


class _Cfg:
    def __init__(self, T, D, lora):
        self.T, self.D, self.lora = T, D, lora
        self.FW = D // 2
        self.FH = self.FW // FOX_HEAD_DIM
        self.RW = D // 2
        self.RH = self.RW // RWKV_HEAD_DIM
        self.LP = -(-lora // LANES) * LANES
        self.o_fox = 0
        self.o_rwkv = 4 * self.FW
        self.o_gate = self.o_rwkv + 4 * self.RW
        self.o_f = self.o_gate + 2 * D
        self.o_wd = self.o_f + LANES
        self.o_ad = self.o_wd + self.LP
        end = self.o_ad + self.LP
        self.tn = 1280 if D >= 2048 else LANES
        self.ncol = -(-end // self.tn) * self.tn
        self.in_cols = 4 * self.FW + self.FH + 4 * self.RW + 2 * lora + 2 * D
        self.rseg = 4 * self.RW + 2 * self.LP
        self.C = min(SCAN_CHUNK, T)
        self.tr = min(256, T)
        self.hb = min(SCAN_HEADS_PER_STEP, self.RH)

    def segments(self):
        FW, FH, RW, lo, D = self.FW, self.FH, self.RW, self.lora, self.D
        g_f = 4 * FW
        g_r = g_f + FH
        g_wd = g_r + 4 * RW
        g_ad = g_wd + lo
        g_g = g_ad + lo
        return [(0, 4 * FW, 0), (g_f, FH, self.o_f), (g_r, 4 * RW, self.o_rwkv), (g_wd, lo, self.o_wd),
                (g_ad, lo, self.o_ad), (g_g, 2 * D, self.o_gate)]


def _to_my_layout(cfg, wg):
    R = wg.shape[0]
    segs = sorted(cfg.segments(), key=lambda s: s[2])
    parts, pos = [], 0
    for g0, w, m0 in segs:
        if m0 > pos:
            parts.append(jnp.zeros((R, m0 - pos), wg.dtype))
        parts.append(wg[:, g0:g0 + w])
        pos = m0 + w
    if cfg.ncol > pos:
        parts.append(jnp.zeros((R, cfg.ncol - pos), wg.dtype))
    return jnp.concatenate(parts, axis=1)


def _from_my_layout(cfg, wm):
    segs = sorted(cfg.segments(), key=lambda s: s[0])
    return jnp.concatenate([wm[:, m0:m0 + w] for g0, w, m0 in segs], axis=1)


def _rwkv_vec_to_my(cfg, v):
    RW4, lo, LP = 4 * cfg.RW, cfg.lora, cfg.LP
    z = jnp.zeros((1, LP - lo), v.dtype)
    return jnp.concatenate([v[:, :RW4], v[:, RW4:RW4 + lo], z, v[:, RW4 + lo:], z], axis=1)


def _rwkv_vec_from_my(cfg, v):
    RW4, lo, LP = 4 * cfg.RW, cfg.lora, cfg.LP
    return jnp.concatenate([v[:, :RW4], v[:, RW4:RW4 + lo], v[:, RW4 + LP:RW4 + LP + lo]], axis=1)


def _mm(name, a, b, dims, out_dtype, tm, tn, tk):
    (M, K) = a.shape if dims != "tn" else a.shape[::-1]
    N = b.shape[0] if dims == "nt" else b.shape[1]
    tm, tn, tk = min(tm, M), min(tn, N), min(tk, K)
    assert M % tm == 0 and N % tn == 0 and K % tk == 0, (name, M, N, K, tm, tn, tk)
    nk = K // tk
    if dims == "nn":
        a_spec = pl.BlockSpec((tm, tk), lambda i, j, k: (i, k))
        b_spec = pl.BlockSpec((tk, tn), lambda i, j, k: (k, j))
    elif dims == "nt":
        a_spec = pl.BlockSpec((tm, tk), lambda i, j, k: (i, k))
        b_spec = pl.BlockSpec((tn, tk), lambda i, j, k: (j, k))
    else:
        a_spec = pl.BlockSpec((tk, tm), lambda i, j, k: (k, i))
        b_spec = pl.BlockSpec((tk, tn), lambda i, j, k: (k, j))

    def body(a_ref, b_ref, o_ref, acc_ref):
        k = pl.program_id(2)

        @pl.when(k == 0)
        def _():
            acc_ref[...] = jnp.zeros_like(acc_ref)

        acc_ref[...] += _dot(a_ref[...], b_ref[...], dims)

        @pl.when(k == nk - 1)
        def _():
            o_ref[...] = acc_ref[...].astype(o_ref.dtype)

    return _pcall(
        body, name=name, grid=(M // tm, N // tn, nk),
        in_specs=[a_spec, b_spec], out_specs=pl.BlockSpec((tm, tn), lambda i, j, k: (i, j)),
        out_shape=jax.ShapeDtypeStruct((M, N), out_dtype), scratch_shapes=[pltpu.VMEM((tm, tn), F32)],
        compiler_params=_cparams(("parallel", "parallel", "arbitrary")),
    )(a, b)


def _tile(tr, w, cb=0):
    return pl.BlockSpec((tr, w), lambda i: (i, cb))


def _const(shape):
    nd = len(shape)
    return pl.BlockSpec(shape, lambda i: (0,) * nd)


def _acc_store(i, ref, val):
    @pl.when(i == 0)
    def _():
        ref[...] = val

    @pl.when(i > 0)
    def _():
        ref[...] += val


def _rms_fwd(cfg, x2, g):
    T, D, tr = cfg.T, cfg.D, cfg.tr

    def body(x_ref, g_ref, h_ref):
        h_ref[...] = _rmsn(x_ref[...], g_ref[...]).astype(BF16)

    return _pcall(body, name="rms_fwd", grid=(T // tr,), in_specs=[_tile(tr, D), _const((1, D))],
                  out_specs=_tile(tr, D), out_shape=jax.ShapeDtypeStruct((T, D), BF16),
                  compiler_params=_cparams(("parallel",)))(x2, g)


def _rms_bwd(cfg, x2, g, dh, dres):
    T, D, tr = cfg.T, cfg.D, cfg.tr

    def body(x_ref, g_ref, dh_ref, dres_ref, gx_ref, dg_ref):
        _, vjp = jax.vjp(_rmsn, x_ref[...], g_ref[...])
        dx, dg = vjp(dh_ref[...])
        gx_ref[...] = dx + dres_ref[...]
        _acc_store(pl.program_id(0), dg_ref, dg)

    return _pcall(body, name="rms_bwd", grid=(T // tr,),
                  in_specs=[_tile(tr, D), _const((1, D)), _tile(tr, D), _tile(tr, D)],
                  out_specs=[_tile(tr, D), _const((1, D))],
                  out_shape=[jax.ShapeDtypeStruct((T, D), F32), jax.ShapeDtypeStruct((1, D), F32)],
                  compiler_params=_cparams(("arbitrary",)))(x2, g, dh, dres)


def _final(cfg, x2, mo, fg, target):
    T, D, tr = cfg.T, cfg.D, cfg.tr

    def loss_fn(hres, g, tgt):
        err = _rmsn(hres, g) - tgt
        return 0.5 * jnp.sum(jnp.mean(err * err, axis=-1, keepdims=True), axis=0, keepdims=True)

    def body(x_ref, mo_ref, g_ref, t_ref, loss_ref, dres_ref, dres16_ref, dg_ref):
        hres = x_ref[...] + mo_ref[...]
        loss, vjp = jax.vjp(functools.partial(loss_fn, tgt=t_ref[...]), hres, g_ref[...])
        dres, dg = vjp(jnp.ones((1, 1), F32))
        dres_ref[...] = dres
        dres16_ref[...] = dres.astype(BF16)
        i = pl.program_id(0)
        _acc_store(i, dg_ref, dg)
        _acc_store(i, loss_ref, jnp.broadcast_to(loss, (8, LANES)))

    return _pcall(body, name="final_loss", grid=(T // tr,),
                  in_specs=[_tile(tr, D), _tile(tr, D), _const((1, D)), _tile(tr, D)],
                  out_specs=[_const((8, LANES)), _tile(tr, D), _tile(tr, D), _const((1, D))],
                  out_shape=[jax.ShapeDtypeStruct((8, LANES), F32), jax.ShapeDtypeStruct((T, D), F32),
                             jax.ShapeDtypeStruct((T, D), BF16), jax.ShapeDtypeStruct((1, D), F32)],
                  compiler_params=_cparams(("arbitrary",)))(x2, mo, fg, target)


def _merge_fn(pa, pb, ga, gb):
    return jax.nn.sigmoid(ga) * pa + jax.nn.sigmoid(gb) * pb


def _merge_fwd(cfg, pa, pb, u):
    T, D, tr = cfg.T, cfg.D, cfg.tr
    cga, cgb = cfg.o_gate // D, cfg.o_gate // D + 1

    def body(pa_ref, pb_ref, ga_ref, gb_ref, m_ref):
        m_ref[...] = _merge_fn(pa_ref[...], pb_ref[...], ga_ref[...], gb_ref[...]).astype(BF16)

    return _pcall(body, name="merge_fwd", grid=(T // tr,),
                  in_specs=[_tile(tr, D), _tile(tr, D), _tile(tr, D, cga), _tile(tr, D, cgb)],
                  out_specs=_tile(tr, D), out_shape=jax.ShapeDtypeStruct((T, D), BF16),
                  compiler_params=_cparams(("parallel",)))(pa, pb, u, u)


def _merge_bwd(cfg, pa, pb, u, dm):
    T, D, tr = cfg.T, cfg.D, cfg.tr
    cga, cgb = cfg.o_gate // D, cfg.o_gate // D + 1

    def body(pa_ref, pb_ref, ga_ref, gb_ref, dm_ref, dpa_ref, dpb_ref, dg_ref):
        _, vjp = jax.vjp(_merge_fn, pa_ref[...], pb_ref[...], ga_ref[...], gb_ref[...])
        dpa, dpb, dga, dgb = vjp(dm_ref[...])
        dpa_ref[...] = dpa.astype(BF16)
        dpb_ref[...] = dpb.astype(BF16)
        dg_ref[:, :D] = dga.astype(BF16)
        dg_ref[:, D:] = dgb.astype(BF16)

    return _pcall(body, name="merge_bwd", grid=(T // tr,),
                  in_specs=[_tile(tr, D), _tile(tr, D), _tile(tr, D, cga), _tile(tr, D, cgb), _tile(tr, D)],
                  out_specs=[_tile(tr, D), _tile(tr, D), _tile(tr, 2 * D)],
                  out_shape=[jax.ShapeDtypeStruct((T, D), BF16), jax.ShapeDtypeStruct((T, D), BF16),
                             jax.ShapeDtypeStruct((T, 2 * D), BF16)],
                  compiler_params=_cparams(("parallel",)))(pa, pb, u, u, dm)


def _gate_fn(o, z):
    return o * _silu(z)


def _gate_a_fwd(cfg, o, u):
    T, FW, tr = cfg.T, cfg.FW, cfg.tr

    def body(o_ref, z_ref, oa_ref):
        oa_ref[...] = _gate_fn(o_ref[...], z_ref[...]).astype(BF16)

    return _pcall(body, name="gate_a_fwd", grid=(T // tr,), in_specs=[_tile(tr, FW), _tile(tr, FW, 3)],
                  out_specs=_tile(tr, FW), out_shape=jax.ShapeDtypeStruct((T, FW), BF16),
                  compiler_params=_cparams(("parallel",)))(o, u)


def _gate_a_bwd(cfg, o, u, doa):
    T, FW, tr = cfg.T, cfg.FW, cfg.tr

    def body(o_ref, z_ref, doa_ref, do_ref, dz_ref):
        _, vjp = jax.vjp(_gate_fn, o_ref[...], z_ref[...])
        do, dz = vjp(doa_ref[...])
        do_ref[...] = do
        dz_ref[...] = dz.astype(BF16)

    return _pcall(body, name="gate_a_bwd", grid=(T // tr,),
                  in_specs=[_tile(tr, FW), _tile(tr, FW, 3), _tile(tr, FW)],
                  out_specs=[_tile(tr, FW), _tile(tr, FW)],
                  out_shape=[jax.ShapeDtypeStruct((T, FW), F32), jax.ShapeDtypeStruct((T, FW), BF16)],
                  compiler_params=_cparams(("parallel",)))(o, u, doa)


def _fox_prep(cfg, u, fb):
    T, tr = cfg.T, cfg.tr
    cf = cfg.o_f // LANES

    def body(f_ref, fb_ref, c_ref, carry_ref):
        i = pl.program_id(0)

        @pl.when(i == 0)
        def _():
            carry_ref[...] = jnp.zeros_like(carry_ref)

        lf = -_softplus(-(f_ref[...] + fb_ref[...]))
        r = lax.broadcasted_iota(jnp.int32, (tr, tr), 0)
        c = lax.broadcasted_iota(jnp.int32, (tr, tr), 1)
        tri = (r >= c).astype(F32)
        c_ref[...] = _dot(tri, lf, precision=HI) + carry_ref[...]
        carry_ref[...] += jnp.sum(lf, axis=0, keepdims=True)

    return _pcall(body, name="fox_prep", grid=(T // tr,), in_specs=[_tile(tr, LANES, cf), _const((1, LANES))],
                  out_specs=_tile(tr, LANES), out_shape=jax.ShapeDtypeStruct((T, LANES), F32),
                  scratch_shapes=[pltpu.VMEM((1, LANES), F32)], compiler_params=_cparams(("arbitrary",)))(u, fb)


def _fox_prep_bwd(cfg, u, fb, dc):
    T, tr = cfg.T, cfg.tr
    cf = cfg.o_f // LANES
    nb = T // tr

    def body(f_ref, fb_ref, dc_ref, df_ref, dfb_ref, carry_ref):
        i = pl.program_id(0)

        @pl.when(i == 0)
        def _():
            carry_ref[...] = jnp.zeros_like(carry_ref)

        dc = dc_ref[...]
        r = lax.broadcasted_iota(jnp.int32, (tr, tr), 0)
        c = lax.broadcasted_iota(jnp.int32, (tr, tr), 1)
        triu = (r <= c).astype(F32)
        dlf = _dot(triu, dc, precision=HI) + carry_ref[...]
        carry_ref[...] += jnp.sum(dc, axis=0, keepdims=True)
        dz = dlf * jax.nn.sigmoid(-(f_ref[...] + fb_ref[...]))
        df_ref[...] = dz.astype(BF16)
        _acc_store(i, dfb_ref, jnp.sum(dz, axis=0, keepdims=True))

    rev = lambda i: (nb - 1 - i, 0)
    return _pcall(body, name="fox_prep_bwd", grid=(nb,),
                  in_specs=[pl.BlockSpec((tr, LANES), lambda i: (nb - 1 - i, cf)), _const((1, LANES)),
                            pl.BlockSpec((tr, LANES), rev)],
                  out_specs=[pl.BlockSpec((tr, LANES), rev), _const((1, LANES))],
                  out_shape=[jax.ShapeDtypeStruct((T, LANES), BF16), jax.ShapeDtypeStruct((1, LANES), F32)],
                  scratch_shapes=[pltpu.VMEM((1, LANES), F32)], compiler_params=_cparams(("arbitrary",)))(u, fb, dc)


def _attn_logits(q_ref, k_ref, c_ref, i, tq, T):
    s = _dot(q_ref[...].astype(BF16), k_ref[...].astype(BF16), "nt") * (FOX_HEAD_DIM ** -0.5) - c_ref[0]
    row = i * tq + lax.broadcasted_iota(jnp.int32, (tq, T), 0)
    col = lax.broadcasted_iota(jnp.int32, (tq, T), 1)
    return jnp.where(col <= row, s, -1e30)


def _attn_fwd(cfg, u, c_rows):
    T, FW, FH = cfg.T, cfg.FW, cfg.FH
    tq = min(256, T)
    dh = FOX_HEAD_DIM

    def body(q_ref, k_ref, v_ref, c_ref, o_ref, lse_ref):
        s = _attn_logits(q_ref, k_ref, c_ref, pl.program_id(1), tq, T)
        m = jnp.max(s, axis=1, keepdims=True)
        p = jnp.exp(s - m)
        l = jnp.sum(p, axis=1, keepdims=True)
        o_ref[...] = _dot(p.astype(BF16), v_ref[...].astype(BF16)) / l
        lse_ref[0] = m + jnp.log(l)

    return _pcall(
        body, name="fox_attn_fwd", grid=(FH, T // tq),
        in_specs=[pl.BlockSpec((tq, dh), lambda h, i: (i, h)), pl.BlockSpec((T, dh), lambda h, i: (0, FH + h)),
                  pl.BlockSpec((T, dh), lambda h, i: (0, 2 * FH + h)), pl.BlockSpec((1, 1, T), lambda h, i: (h, 0, 0))],
        out_specs=[pl.BlockSpec((tq, dh), lambda h, i: (i, h)), pl.BlockSpec((1, tq, 1), lambda h, i: (h, i, 0))],
        out_shape=[jax.ShapeDtypeStruct((T, FW), F32), jax.ShapeDtypeStruct((FH, T, 1), F32)],
        compiler_params=_cparams(("parallel", "arbitrary")),
    )(u, u, u, c_rows)


def _attn_bwd(cfg, u, c_rows, lse, do):
    T, FW, FH = cfg.T, cfg.FW, cfg.FH
    tq = min(256, T)
    dh = FOX_HEAD_DIM
    scale = dh ** -0.5

    def body(q_ref, k_ref, v_ref, c_ref, lse_ref, do_ref, dq_ref, dk_ref, dv_ref, dcol_ref):
        i = pl.program_id(1)
        s = _attn_logits(q_ref, k_ref, c_ref, i, tq, T)
        p = jnp.exp(s - lse_ref[0])
        do_v = do_ref[...]
        dp = _dot(do_v.astype(BF16), v_ref[...].astype(BF16), "nt")
        delta = jnp.sum(p * dp, axis=1, keepdims=True)
        ds = p * (dp - delta)
        ds16 = ds.astype(BF16)
        dq_ref[...] = (_dot(ds16, k_ref[...].astype(BF16)) * scale).astype(BF16)
        _acc_store(i, dk_ref, _dot(ds16, q_ref[...].astype(BF16), "tn") * scale)
        _acc_store(i, dv_ref, _dot(p.astype(BF16), do_v.astype(BF16), "tn"))
        _acc_store(i, dcol_ref, jnp.sum(ds, axis=0, keepdims=True)[None])

    qspec = pl.BlockSpec((tq, dh), lambda h, i: (i, h))
    return _pcall(
        body, name="fox_attn_bwd", grid=(FH, T // tq),
        in_specs=[qspec, pl.BlockSpec((T, dh), lambda h, i: (0, FH + h)),
                  pl.BlockSpec((T, dh), lambda h, i: (0, 2 * FH + h)), pl.BlockSpec((1, 1, T), lambda h, i: (h, 0, 0)),
                  pl.BlockSpec((1, tq, 1), lambda h, i: (h, i, 0)), qspec],
        out_specs=[qspec, pl.BlockSpec((T, dh), lambda h, i: (0, h)), pl.BlockSpec((T, dh), lambda h, i: (0, h)),
                   pl.BlockSpec((1, 1, T), lambda h, i: (h, 0, 0))],
        out_shape=[jax.ShapeDtypeStruct((T, FW), BF16), jax.ShapeDtypeStruct((T, FW), F32),
                   jax.ShapeDtypeStruct((T, FW), F32), jax.ShapeDtypeStruct((FH, 1, T), F32)],
        compiler_params=_cparams(("parallel", "arbitrary")),
    )(u, u, u, c_rows, lse, do)


def _head_indicators(cfg):
    ind = np.zeros((cfg.RW, LANES), np.float32)
    ind[np.arange(cfg.RW), np.arange(cfg.RW) // RWKV_HEAD_DIM] = 1.0
    pad = np.zeros((1, LANES), np.float32)
    pad[0, cfg.RH:] = 1.0
    return jnp.asarray(ind), jnp.asarray(ind.T.copy()), jnp.asarray(pad)


def _prep_fn(us_r, us_k, us_v, us_wd, us_ad, w0, w2p, a0, a2p, k_k, k_a, ind, ind_t, pad):
    wpre = w0 + _dot(jnp.tanh(us_wd), w2p, precision=HI)
    w = -_softplus(-wpre) - 0.5
    lw = -jnp.exp(w)
    a = jax.nn.sigmoid(a0 + _dot(us_ad, a2p, precision=HI))
    kk = us_k * k_k
    ss = _dot(kk * kk, ind, precision=HI) + pad
    inv = 1.0 / jnp.maximum(jnp.sqrt(ss), L2_EPS)
    kkn = kk * _dot(inv, ind_t, precision=HI)
    kp = us_k * (1.0 + (a - 1.0) * k_a)
    return us_r, lw, kp, us_v, -kkn, kkn * a


def _shifted(u, prev_row, mu, first):
    n = u.shape[0]
    rolled = pltpu.roll(u, 1, 0)
    row = lax.broadcasted_iota(jnp.int32, u.shape, 0)
    p0 = jnp.where(first, jnp.zeros_like(prev_row), prev_row)
    prev = jnp.where(row == 0, jnp.broadcast_to(p0, u.shape), rolled)
    return u + (prev - u) * mu, prev


def _rwkv_specs(cfg, tr):
    RW, LP = cfg.RW, cfg.LP
    base = cfg.o_rwkv // RW
    cols = [(RW, base), (RW, base + 1), (RW, base + 2), (RW, base + 3), (LP, cfg.o_wd // LP), (LP, cfg.o_ad // LP)]
    cur = [pl.BlockSpec((tr, w), (lambda i, cb=cb: (i, cb))) for w, cb in cols]
    prv = [pl.BlockSpec((8, w), (lambda i, cb=cb: (jnp.maximum(i * (tr // 8) - 1, 0), cb))) for w, cb in cols]
    return cols, cur, prv


def _mu_pieces(cfg, mu_ref):
    RW, LP = cfg.RW, cfg.LP
    offs = [0, RW, 2 * RW, 3 * RW, 4 * RW, 4 * RW + LP, 4 * RW + 2 * LP]
    return [mu_ref[:, offs[j]:offs[j + 1]] for j in range(6)]


def _rwkv_prep_fwd(cfg, u, mu, w0, w2p, a0, a2p, k_k, k_a):
    T, RW, LP, tr = cfg.T, cfg.RW, cfg.LP, cfg.tr
    ind, ind_t, pad = _head_indicators(cfg)
    cols, cur, prv = _rwkv_specs(cfg, tr)

    def body(*refs):
        u_refs, p_refs = refs[0:6], refs[6:12]
        mu_ref, w0_ref, w2_ref, a0_ref, a2_ref, kk_ref, ka_ref, ind_ref, indt_ref, pad_ref = refs[12:22]
        outs = refs[22:]
        first = pl.program_id(0) == 0
        mus = _mu_pieces(cfg, mu_ref)
        us = [_shifted(u_refs[j][...], p_refs[j][7:8, :], mus[j], first)[0] for j in range(6)]
        res = _prep_fn(us[0], us[1], us[2], us[4], us[5], w0_ref[...], w2_ref[...], a0_ref[...], a2_ref[...],
                       kk_ref[...], ka_ref[...], ind_ref[...], indt_ref[...], pad_ref[...])
        for j in range(6):
            outs[j][...] = res[j]
        outs[6][...] = us[3]

    consts = [mu, w0, w2p, a0, a2p, k_k, k_a, ind, ind_t, pad]
    return _pcall(body, name="rwkv_prep_fwd", grid=(T // tr,),
                  in_specs=cur + prv + [_const(c.shape) for c in consts],
                  out_specs=[_tile(tr, RW)] * 7, out_shape=[jax.ShapeDtypeStruct((T, RW), F32)] * 7,
                  compiler_params=_cparams(("parallel",)))(*([u] * 12), *consts)


def _rwkv_prep_bwd(cfg, u, mu, w0, w2p, a0, a2p, k_k, k_a, cots, dzb):
    T, RW, LP = cfg.T, cfg.RW, cfg.LP
    tr = min(128, T)
    ind, ind_t, pad = _head_indicators(cfg)
    cols, cur, prv = _rwkv_specs(cfg, tr)
    rseg = cfg.rseg

    def body(*refs):
        u_refs, p_refs = refs[0:6], refs[6:12]
        mu_ref, w0_ref, w2_ref, a0_ref, a2_ref, kk_ref, ka_ref, ind_ref, indt_ref, pad_ref = refs[12:22]
        cot_refs, dzb_ref = refs[22:28], refs[28]
        dus_ref, dmu_ref, dw0_ref, dw2_ref, da0_ref, da2_ref, dkk_ref, dka_ref = refs[29:]
        i = pl.program_id(0)
        first = i == 0
        mus = _mu_pieces(cfg, mu_ref)
        sh = [_shifted(u_refs[j][...], p_refs[j][7:8, :], mus[j], first) for j in range(6)]
        us = [s[0] for s in sh]
        fn = functools.partial(_prep_fn, ind=ind_ref[...], ind_t=indt_ref[...], pad=pad_ref[...])
        _, vjp = jax.vjp(fn, us[0], us[1], us[2], us[4], us[5], w0_ref[...], w2_ref[...], a0_ref[...], a2_ref[...],
                         kk_ref[...], ka_ref[...])
        d = vjp(tuple(c[...] for c in cot_refs))
        dus = [d[0], d[1], d[2], dzb_ref[...], d[3], d[4]]
        offs = [0, RW, 2 * RW, 3 * RW, 4 * RW, 4 * RW + LP, 4 * RW + 2 * LP]
        for j in range(6):
            dus_ref[:, offs[j]:offs[j + 1]] = dus[j]
            dmu_j = jnp.sum(dus[j] * (sh[j][1] - u_refs[j][...]), axis=0, keepdims=True)

            @pl.when(first)
            def _(j=j, dmu_j=dmu_j):
                dmu_ref[:, offs[j]:offs[j + 1]] = dmu_j

            @pl.when(i > 0)
            def _(j=j, dmu_j=dmu_j):
                dmu_ref[:, offs[j]:offs[j + 1]] += dmu_j
        for ref, val in zip((dw0_ref, dw2_ref, da0_ref, da2_ref, dkk_ref, dka_ref), d[5:11]):
            _acc_store(i, ref, val)

    consts = [mu, w0, w2p, a0, a2p, k_k, k_a, ind, ind_t, pad]
    vec = jax.ShapeDtypeStruct((1, RW), F32)
    mat = jax.ShapeDtypeStruct((LP, RW), F32)
    return _pcall(body, name="rwkv_prep_bwd", grid=(T // tr,),
                  in_specs=cur + prv + [_const(c.shape) for c in consts] + [_tile(tr, RW)] * 7,
                  out_specs=[_tile(tr, rseg), _const((1, rseg)), _const((1, RW)), _const((LP, RW)), _const((1, RW)),
                             _const((LP, RW)), _const((1, RW)), _const((1, RW))],
                  out_shape=[jax.ShapeDtypeStruct((T, rseg), F32), jax.ShapeDtypeStruct((1, rseg), F32),
                             vec, mat, vec, mat, vec, vec],
                  compiler_params=_cparams(("arbitrary",)))(*([u] * 12), *consts, *cots, dzb)


def _shift_bwd(cfg, dus, mu):
    T, tr, rseg = cfg.T, cfg.tr, cfg.rseg
    nb = T // tr

    def body(d_ref, n_ref, mu_ref, du_ref):
        d = d_ref[...]
        rolled = pltpu.roll(d, tr - 1, 0)
        row = lax.broadcasted_iota(jnp.int32, d.shape, 0)
        n0 = jnp.where(pl.program_id(0) == nb - 1, jnp.zeros_like(n_ref[0:1, :]), n_ref[0:1, :])
        nxt = jnp.where(row == tr - 1, jnp.broadcast_to(n0, d.shape), rolled)
        mu_v = mu_ref[...]
        du_ref[...] = (d * (1.0 - mu_v) + nxt * mu_v).astype(BF16)

    return _pcall(body, name="shift_bwd", grid=(nb,),
                  in_specs=[_tile(tr, rseg),
                            pl.BlockSpec((8, rseg), lambda i: (jnp.minimum((i + 1) * (tr // 8), T // 8 - 1), 0)),
                            _const((1, rseg))],
                  out_specs=_tile(tr, rseg), out_shape=jax.ShapeDtypeStruct((T, rseg), BF16),
                  compiler_params=_cparams(("parallel",)))(dus, dus, mu)


def _chunk_fn(S0, r, lw, k, v, a, b):
    H, C, K = r.shape
    row = lax.broadcasted_iota(jnp.int32, (C, C), 0)
    col = lax.broadcasted_iota(jnp.int32, (C, C), 1)
    incl = jnp.broadcast_to((row >= col).astype(F32)[None], (H, C, C))
    strict = (row > col)[None]
    lower = (row >= col)[None]
    L = _bdot(incl, lw, 2, 1)
    LC = jnp.sum(lw, axis=1, keepdims=True)
    eL = jnp.exp(L)
    eLn = jnp.exp(-L)
    at = a * jnp.exp(L - lw)
    rt = r * eL
    bt = b * eLn
    kt = k * eLn
    eR = jnp.exp(LC - L)
    zero = jnp.zeros((), F32)
    n_ab = jnp.where(strict, _bdot(at, bt, 2, 2), zero)
    n_ak = jnp.where(strict, _bdot(at, kt, 2, 2), zero)
    m_rb = jnp.where(lower, _bdot(rt, bt, 2, 2), zero)
    m_rk = jnp.where(lower, _bdot(rt, kt, 2, 2), zero)
    U = _bdot(at, S0, 2, 2) + _bdot(n_ak, v, 2, 1)
    M = n_ab
    steps = max(1, int(np.ceil(np.log2(C))))
    for s in range(steps):
        U = U + _bdot(M, U, 2, 1)
        if s + 1 < steps:
            M = _bdot(M, M, 2, 1)
    Y = _bdot(rt, S0, 2, 2) + _bdot(m_rb, U, 2, 1) + _bdot(m_rk, v, 2, 1)
    S1 = S0 * jnp.exp(LC) + _bdot(U, b * eR, 1, 1) + _bdot(v, k * eR, 1, 1)
    return Y, S1


def _scan_fwd(cfg, seqs):
    T, RH, C = cfg.T, cfg.RH, cfg.C
    N = RWKV_HEAD_DIM
    HB = cfg.hb
    nc = T // C

    def body(r_ref, lw_ref, k_ref, v_ref, a_ref, b_ref, y_ref, ck_ref, s_ref):
        @pl.when(pl.program_id(1) == 0)
        def _():
            s_ref[...] = jnp.zeros_like(s_ref)

        S0 = s_ref[...]
        ck_ref[:, 0] = S0
        Y, S1 = _chunk_fn(S0, r_ref[...], lw_ref[...], k_ref[...], v_ref[...], a_ref[...], b_ref[...])
        y_ref[...] = Y
        s_ref[...] = S1

    seq = pl.BlockSpec((HB, C, N), lambda h, j: (h, j, 0))
    return _pcall(body, name="rwkv_scan_fwd", grid=(RH // HB, nc), in_specs=[seq] * 6,
                  out_specs=[seq, pl.BlockSpec((HB, 1, N, N), lambda h, j: (h, j, 0, 0))],
                  out_shape=[jax.ShapeDtypeStruct((RH, T, N), F32), jax.ShapeDtypeStruct((RH, nc, N, N), F32)],
                  scratch_shapes=[pltpu.VMEM((HB, N, N), F32)],
                  compiler_params=_cparams(("parallel", "arbitrary")))(*seqs)


def _scan_bwd(cfg, seqs, ckpt, dy):
    T, RH, C = cfg.T, cfg.RH, cfg.C
    N = RWKV_HEAD_DIM
    HB = cfg.hb
    nc = T // C

    def body(r_ref, lw_ref, k_ref, v_ref, a_ref, b_ref, ck_ref, dy_ref, *rest):
        outs, ds_ref = rest[:6], rest[6]

        @pl.when(pl.program_id(1) == 0)
        def _():
            ds_ref[...] = jnp.zeros_like(ds_ref)

        _, vjp = jax.vjp(_chunk_fn, ck_ref[:, 0], r_ref[...], lw_ref[...], k_ref[...], v_ref[...], a_ref[...],
                         b_ref[...])
        d = vjp((dy_ref[...], ds_ref[...]))
        ds_ref[...] = d[0]
        for j in range(6):
            outs[j][...] = d[1 + j]

    seq = pl.BlockSpec((HB, C, N), lambda h, j: (h, nc - 1 - j, 0))
    return _pcall(body, name="rwkv_scan_bwd", grid=(RH // HB, nc),
                  in_specs=[seq] * 6 + [pl.BlockSpec((HB, 1, N, N), lambda h, j: (h, nc - 1 - j, 0, 0)), seq],
                  out_specs=[seq] * 6, out_shape=[jax.ShapeDtypeStruct((RH, T, N), F32)] * 6,
                  scratch_shapes=[pltpu.VMEM((HB, N, N), F32)],
                  compiler_params=_cparams(("parallel", "arbitrary")))(*seqs, ckpt, dy)


def _post_fn(y, r, kp, v, zb, ln_w, ln_b, rk, ind, ind_t):
    n = float(RWKV_HEAD_DIM)
    mu = _dot(_dot(y, ind, precision=HI) / n, ind_t, precision=HI)
    yc = y - mu
    var = _dot(yc * yc, ind, precision=HI) / n
    rstd = _dot(lax.rsqrt(var + GN_EPS), ind_t, precision=HI)
    yn = yc * rstd * ln_w + ln_b
    bonus = _dot(_dot(r * kp * rk, ind, precision=HI), ind_t, precision=HI) * v
    return (yn + bonus) * _silu(zb)


def _rwkv_post_fwd(cfg, y, r, kp, v, zb, ln_w, ln_b, rk):
    T, RW, tr = cfg.T, cfg.RW, cfg.tr
    ind, ind_t, _ = _head_indicators(cfg)

    def body(y_ref, r_ref, k_ref, v_ref, z_ref, lw_ref, lb_ref, rk_ref, ind_ref, indt_ref, ob_ref):
        ob_ref[...] = _post_fn(y_ref[...], r_ref[...], k_ref[...], v_ref[...], z_ref[...], lw_ref[...], lb_ref[...],
                               rk_ref[...], ind_ref[...], indt_ref[...]).astype(BF16)

    consts = [ln_w, ln_b, rk, ind, ind_t]
    return _pcall(body, name="rwkv_post_fwd", grid=(T // tr,),
                  in_specs=[_tile(tr, RW)] * 5 + [_const(c.shape) for c in consts],
                  out_specs=_tile(tr, RW), out_shape=jax.ShapeDtypeStruct((T, RW), BF16),
                  compiler_params=_cparams(("parallel",)))(y, r, kp, v, zb, *consts)


def _rwkv_post_bwd(cfg, y, r, kp, v, zb, ln_w, ln_b, rk, dob):
    T, RW = cfg.T, cfg.RW
    tr = min(128, T)
    ind, ind_t, _ = _head_indicators(cfg)

    def body(y_ref, r_ref, k_ref, v_ref, z_ref, lw_ref, lb_ref, rk_ref, ind_ref, indt_ref, dob_ref,
             dy_ref, dr_ref, dk_ref, dv_ref, dz_ref, dlw_ref, dlb_ref, drk_ref):
        fn = functools.partial(_post_fn, ind=ind_ref[...], ind_t=indt_ref[...])
        _, vjp = jax.vjp(fn, y_ref[...], r_ref[...], k_ref[...], v_ref[...], z_ref[...], lw_ref[...], lb_ref[...],
                         rk_ref[...])
        d = vjp(dob_ref[...])
        for ref, val in zip((dy_ref, dr_ref, dk_ref, dv_ref, dz_ref), d[:5]):
            ref[...] = val
        i = pl.program_id(0)
        for ref, val in zip((dlw_ref, dlb_ref, drk_ref), d[5:8]):
            _acc_store(i, ref, val)

    consts = [ln_w, ln_b, rk, ind, ind_t]
    vec = jax.ShapeDtypeStruct((1, RW), F32)
    return _pcall(body, name="rwkv_post_bwd", grid=(T // tr,),
                  in_specs=[_tile(tr, RW)] * 5 + [_const(c.shape) for c in consts] + [_tile(tr, RW)],
                  out_specs=[_tile(tr, RW)] * 5 + [_const((1, RW))] * 3,
                  out_shape=[jax.ShapeDtypeStruct((T, RW), F32)] * 5 + [vec] * 3,
                  compiler_params=_cparams(("arbitrary",)))(y, r, kp, v, zb, *consts, dob)


def _adamw_math(w, g, m, v):
    m = ADAM_B1 * m + (1.0 - ADAM_B1) * g
    v = ADAM_B2 * v + (1.0 - ADAM_B2) * (g * g)
    m_hat = m / (1.0 - ADAM_B1 ** ADAM_STEP)
    v_hat = v / (1.0 - ADAM_B2 ** ADAM_STEP)
    delta = -ADAM_LR * (m_hat / (jnp.sqrt(v_hat) + ADAM_EPS) + ADAM_WD * w)
    return delta, m, v


def _adamw(name, w, g, m, v):
    R, Cc = w.shape
    tr = R
    for cand in (512, 256, 128, 64, 32, 16, 8):
        if R % cand == 0 and cand * Cc * 4 <= 2 * 1024 * 1024:
            tr = cand
            break

    def body(w_ref, g_ref, m_ref, v_ref, d_ref, nm_ref, nv_ref):
        d, nm, nv = _adamw_math(w_ref[...], g_ref[...], m_ref[...], v_ref[...])
        d_ref[...] = d
        nm_ref[...] = nm
        nv_ref[...] = nv

    spec = _tile(tr, Cc)
    return _pcall(body, name=name, grid=(R // tr,), in_specs=[spec] * 4, out_specs=[spec] * 3,
                  out_shape=[jax.ShapeDtypeStruct((R, Cc), F32)] * 3,
                  compiler_params=_cparams(("parallel",)))(w, g, m, v)


def _row_tile(R, Cc, itemsize, budget=2 * 1024 * 1024):
    for cand in (1024, 512, 256, 128, 64, 32, 16):
        if R % cand == 0 and cand * Cc * itemsize <= budget:
            return cand
    return R


def _add_halves(name, gs, r1, c_idx):
    _, R, Cc = gs.shape
    half = R // 2
    tr = _row_tile(half, Cc, 4)
    nb = half // tr

    def body(c_ref, g_ref, r_ref, o_ref):
        o_ref[...] = (g_ref[...].astype(F32) + r_ref[...].astype(F32)).astype(BF16)

    grid_spec = pltpu.PrefetchScalarGridSpec(
        num_scalar_prefetch=1, grid=(N_CHIPS, nb),
        in_specs=[pl.BlockSpec((1, tr, Cc), lambda s, i, c: (s, c[0] * nb + i, 0)),
                  pl.BlockSpec((1, tr, Cc), lambda s, i, c: (s, i, 0))],
        out_specs=pl.BlockSpec((1, tr, Cc), lambda s, i, c: (s, i, 0)))
    return _pcall(body, name=name, grid_spec=grid_spec, out_shape=jax.ShapeDtypeStruct((N_CHIPS, half, Cc), BF16),
                  compiler_params=_cparams(("parallel", "parallel")))(c_idx, gs, r1)


def _sum_slots(name, r2):
    S, R, Cc = r2.shape
    tr = _row_tile(R, Cc, 4 * S // 2 if r2.dtype == BF16 else 4 * S)

    def body(r_ref, o_ref):
        acc = r_ref[0].astype(F32)
        for s in range(1, S):
            acc = acc + r_ref[s].astype(F32)
        o_ref[...] = acc

    return _pcall(body, name=name, grid=(R // tr,), in_specs=[pl.BlockSpec((S, tr, Cc), lambda i: (0, i, 0))],
                  out_specs=_tile(tr, Cc), out_shape=jax.ShapeDtypeStruct((R, Cc), F32),
                  compiler_params=_cparams(("parallel",)))(r2)


def _cast_bf16(name, w):
    R, Cc = w.shape
    tr = _row_tile(R, Cc, 4)

    def body(w_ref, o_ref):
        o_ref[...] = w_ref[...].astype(BF16)

    return _pcall(body, name=name, grid=(R // tr,), in_specs=[_tile(tr, Cc)], out_specs=_tile(tr, Cc),
                  out_shape=jax.ShapeDtypeStruct((R, Cc), BF16), compiler_params=_cparams(("parallel",)))(w)


_ANY = pl.BlockSpec(memory_space=pl.ANY)


def _place():
    x, y, c = lax.axis_index("x"), lax.axis_index("y"), lax.axis_index("c")
    others = [(1 - x, y), (x, 1 - y), (1 - x, 1 - y)]
    return x, y, c, others


def _gather_weights(shards):
    n = len(shards)
    halves = [s.shape[0] // 2 for s in shards]

    def body(*refs):
        ins, outs = refs[:n], refs[n:2 * n]
        send_sems, recv_sems, local_sems = refs[2 * n:]
        x, y, c, others = _place()
        me = 2 * x + y

        def rows(k, ref, chip, hc):
            return ref.at[chip, pl.ds(hc * halves[k], halves[k]), :]

        def remote(k, j, src, dst, to):
            return pltpu.make_async_remote_copy(src_ref=src, dst_ref=dst, send_sem=send_sems.at[6 * k + j],
                                                recv_sem=recv_sems.at[6 * k + j], device_id=to, device_id_type=MESH)

        local = [pltpu.make_async_copy(ins[k], outs[k].at[me], local_sems.at[k]) for k in range(n)]
        for cp in local:
            cp.start()
        first, passed = [], []
        for k in range(n):
            mine = ins[k].at[pl.ds(c * halves[k], halves[k]), :]
            for j, (px, py) in enumerate(others):
                cp = remote(k, j, mine, rows(k, outs[k], me, c), (px, py, c))
                cp.start()
                first.append(cp)
        for k in range(n):
            for j, (px, py) in enumerate(others):
                land = rows(k, outs[k], 2 * px + py, c)
                remote(k, j, land, land, (x, y, c)).wait_recv()
                cp = remote(k, 3 + j, land, land, (x, y, 1 - c))
                cp.start()
                passed.append(cp)
        for k in range(n):
            for j, (px, py) in enumerate(others):
                land = rows(k, outs[k], 2 * px + py, 1 - c)
                remote(k, 3 + j, land, land, (x, y, c)).wait_recv()
        for cp in first + passed:
            cp.wait_send()
        for cp in local:
            cp.wait()

    return _pcall(
        body, name="gather_weights", in_specs=[_ANY] * n, out_specs=[_ANY] * n,
        out_shape=[jax.ShapeDtypeStruct((N_CHIPS,) + s.shape, s.dtype) for s in shards],
        scratch_shapes=[pltpu.SemaphoreType.DMA((6 * n,)), pltpu.SemaphoreType.DMA((6 * n,)),
                        pltpu.SemaphoreType.DMA((n,))],
    )(*shards)


def _exchange_halves(grads):
    n = len(grads)
    halves = [g.shape[1] // 2 for g in grads]

    def body(*refs):
        ins, outs = refs[:n], refs[n:2 * n]
        send_sems, recv_sems = refs[2 * n:]
        x, y, c, _ = _place()
        cps = []
        for k in range(n):
            src = ins[k].at[:, pl.ds((1 - c) * halves[k], halves[k]), :]
            cp = pltpu.make_async_remote_copy(src_ref=src, dst_ref=outs[k], send_sem=send_sems.at[k],
                                              recv_sem=recv_sems.at[k], device_id=(x, y, 1 - c), device_id_type=MESH)
            cp.start()
            cps.append(cp)
        for cp in cps:
            cp.wait()

    return _pcall(
        body, name="exchange_halves", in_specs=[_ANY] * n, out_specs=[_ANY] * n,
        out_shape=[jax.ShapeDtypeStruct((N_CHIPS, h) + g.shape[2:], g.dtype) for g, h in zip(grads, halves)],
        scratch_shapes=[pltpu.SemaphoreType.DMA((n,)), pltpu.SemaphoreType.DMA((n,))],
    )(*grads)


def _scatter_to_owners(chip_sums, small):
    n = len(chip_sums)

    def body(*refs):
        ins, small_in = refs[:n], refs[n]
        outs, small_out = refs[n + 1:2 * n + 1], refs[2 * n + 1]
        send_sems, recv_sems, local_sems, ssend, srecv = refs[2 * n + 2:]
        x, y, c, others = _place()
        me = 2 * x + y
        dev = 2 * me + c
        local = [pltpu.make_async_copy(ins[k].at[me], outs[k].at[me], local_sems.at[k]) for k in range(n)]
        local.append(pltpu.make_async_copy(small_in, small_out.at[dev], local_sems.at[n]))
        for cp in local:
            cp.start()
        sends = []
        for k in range(n):
            for j, (px, py) in enumerate(others):
                cp = pltpu.make_async_remote_copy(
                    src_ref=ins[k].at[2 * px + py], dst_ref=outs[k].at[me], send_sem=send_sems.at[3 * k + j],
                    recv_sem=recv_sems.at[3 * k + j], device_id=(px, py, c), device_id_type=MESH)
                cp.start()
                sends.append(cp)
        rel = [(dx, dy, dc) for dx in (0, 1) for dy in (0, 1) for dc in (0, 1)][1:]
        for r, (dx, dy, dc) in enumerate(rel):
            to = (x ^ dx, y ^ dy, c ^ dc)
            cp = pltpu.make_async_remote_copy(src_ref=small_in, dst_ref=small_out.at[dev], send_sem=ssend.at[r],
                                              recv_sem=srecv.at[r], device_id=to, device_id_type=MESH)
            cp.start()
            sends.append(cp)
        for k in range(n):
            for j, (px, py) in enumerate(others):
                land = outs[k].at[2 * px + py]
                pltpu.make_async_remote_copy(src_ref=land, dst_ref=land, send_sem=send_sems.at[3 * k + j],
                                             recv_sem=recv_sems.at[3 * k + j], device_id=(x, y, c),
                                             device_id_type=MESH).wait_recv()
        for r, (dx, dy, dc) in enumerate(rel):
            land = small_out.at[4 * (x ^ dx) + 2 * (y ^ dy) + (c ^ dc)]
            pltpu.make_async_remote_copy(src_ref=land, dst_ref=land, send_sem=ssend.at[r], recv_sem=srecv.at[r],
                                         device_id=(x, y, c), device_id_type=MESH).wait_recv()
        for cp in sends:
            cp.wait_send()
        for cp in local:
            cp.wait()

    return _pcall(
        body, name="scatter_to_owners", in_specs=[_ANY] * (n + 1), out_specs=[_ANY] * (n + 1),
        out_shape=[jax.ShapeDtypeStruct(g.shape, g.dtype) for g in chip_sums]
        + [jax.ShapeDtypeStruct((N_DEV,) + small.shape, small.dtype)],
        scratch_shapes=[pltpu.SemaphoreType.DMA((3 * n,)), pltpu.SemaphoreType.DMA((3 * n,)),
                        pltpu.SemaphoreType.DMA((n + 1,)), pltpu.SemaphoreType.DMA((7,)),
                        pltpu.SemaphoreType.DMA((7,))],
    )(*chip_sums, small)


def _join_halves(halves):
    n = len(halves)
    hs = [h.shape[0] for h in halves]

    def body(*refs):
        ins, outs = refs[:n], refs[n:2 * n]
        send_sems, recv_sems, local_sems = refs[2 * n:]
        x, y, c, _ = _place()
        cps, loc = [], []
        for k in range(n):
            dst = outs[k].at[pl.ds(c * hs[k], hs[k]), :]
            lc = pltpu.make_async_copy(ins[k], dst, local_sems.at[k])
            lc.start()
            loc.append(lc)
            cp = pltpu.make_async_remote_copy(src_ref=ins[k], dst_ref=dst, send_sem=send_sems.at[k],
                                              recv_sem=recv_sems.at[k], device_id=(x, y, 1 - c), device_id_type=MESH)
            cp.start()
            cps.append(cp)
        for k in range(n):
            land = outs[k].at[pl.ds((1 - c) * hs[k], hs[k]), :]
            pltpu.make_async_remote_copy(src_ref=land, dst_ref=land, send_sem=send_sems.at[k],
                                         recv_sem=recv_sems.at[k], device_id=(x, y, c), device_id_type=MESH).wait_recv()
        for cp in cps:
            cp.wait_send()
        for lc in loc:
            lc.wait()

    return _pcall(
        body, name="join_halves", in_specs=[_ANY] * n, out_specs=[_ANY] * n,
        out_shape=[jax.ShapeDtypeStruct((2 * h.shape[0],) + h.shape[1:], h.dtype) for h in halves],
        scratch_shapes=[pltpu.SemaphoreType.DMA((n,)), pltpu.SemaphoreType.DMA((n,)), pltpu.SemaphoreType.DMA((n,))],
    )(*halves)


def _heads(cfg, a):
    return a.reshape(cfg.T, cfg.RH, RWKV_HEAD_DIM).transpose(1, 0, 2)


def _unheads(cfg, a):
    return a.transpose(1, 0, 2).reshape(cfg.T, cfg.RW)


def _local_step(cfg, x2, target, norm_gain, w_my, fb, mu_g, w0, w2, a0, a2, k_k, k_a, r_k, ln_w, ln_b, wpf, wpr, wout,
                fng):
    T, D, FW, FH, RW, RH, LP, lora = cfg.T, cfg.D, cfg.FW, cfg.FH, cfg.RW, cfg.RH, cfg.LP, cfg.lora
    fb_p = jnp.pad(fb, ((0, 0), (0, LANES - FH)))
    mu = _rwkv_vec_to_my(cfg, mu_g)
    w2p = jnp.pad(w2, ((0, LP - lora), (0, 0)))
    a2p = jnp.pad(a2, ((0, LP - lora), (0, 0)))
    rk = r_k.reshape(1, RW)
    tm = min(1024, T)

    h = _rms_fwd(cfg, x2, norm_gain)
    u = _mm("in_proj", h, w_my, "nn", F32, tm, cfg.tn, 512)
    c_cols = _fox_prep(cfg, u, fb_p)
    c_rows = c_cols[:, :FH].T.reshape(FH, 1, T)
    o, lse = _attn_fwd(cfg, u, c_rows)
    oa = _gate_a_fwd(cfg, o, u)
    prep = _rwkv_prep_fwd(cfg, u, mu, w0, w2p, a0, a2p, k_k, k_a)
    r, lw, kp, v, an, b, zb = prep
    seqs = [_heads(cfg, t) for t in (r, lw, kp, v, an, b)]
    y_h, ckpt = _scan_fwd(cfg, seqs)
    y = _unheads(cfg, y_h)
    ob = _rwkv_post_fwd(cfg, y, r, kp, v, zb, ln_w, ln_b, rk)
    pa = _mm("proj_fox", oa, wpf, "nn", F32, tm, 1024, 512)
    pb = _mm("proj_rwkv", ob, wpr, "nn", F32, tm, 1024, 512)
    m = _merge_fwd(cfg, pa, pb, u)
    mo = _mm("out_proj", m, wout, "nn", F32, tm, 1024, 512)
    loss8, dres, dres16, d_fng = _final(cfg, x2, mo, fng.reshape(1, D), target)

    dm = _mm("out_proj_dx", dres16, wout, "nt", F32, tm, 1024, 512)
    d_wout = _mm("out_proj_dw", m, dres16, "tn", BF16, 1024, 1024, 512)
    dpa, dpb, dgate = _merge_bwd(cfg, pa, pb, u, dm)
    doa = _mm("proj_fox_dx", dpa, wpf, "nt", F32, tm, 1024, 512)
    d_wpf = _mm("proj_fox_dw", oa, dpa, "tn", BF16, 1024, 1024, 512)
    dob = _mm("proj_rwkv_dx", dpb, wpr, "nt", F32, tm, 1024, 512)
    d_wpr = _mm("proj_rwkv_dw", ob, dpb, "tn", BF16, 1024, 1024, 512)

    do, dza = _gate_a_bwd(cfg, o, u, doa)
    dq, dk, dv, dcol = _attn_bwd(cfg, u, c_rows, lse, do)
    dc = jnp.pad(-dcol.reshape(FH, T).T, ((0, 0), (0, LANES - FH)))
    df, d_fb = _fox_prep_bwd(cfg, u, fb_p, dc)

    dy, dr_p, dk_p, dv_p, dzb, d_lnw, d_lnb, d_rk = _rwkv_post_bwd(cfg, y, r, kp, v, zb, ln_w, ln_b, rk, dob)
    dseq = _scan_bwd(cfg, seqs, ckpt, _heads(cfg, dy))
    dr_s, dlw_s, dk_s, dv_s, da_s, db_s = [_unheads(cfg, t) for t in dseq]
    cots = [dr_s + dr_p, dlw_s, dk_s + dk_p, dv_s + dv_p, da_s, db_s]
    dus, d_mu, d_w0, d_w2p, d_a0, d_a2p, d_kk, d_ka = _rwkv_prep_bwd(cfg, u, mu, w0, w2p, a0, a2p, k_k, k_a, cots, dzb)
    du_rwkv = _shift_bwd(cfg, dus, mu)

    pad_f = jnp.zeros((T, cfg.ncol - cfg.o_ad - LP), BF16)
    du = jnp.concatenate([dq, dk.astype(BF16), dv.astype(BF16), dza, du_rwkv[:, :4 * RW], dgate, df,
                          du_rwkv[:, 4 * RW:], pad_f], axis=1)
    dh = _mm("in_proj_dx", du, w_my, "nt", F32, tm, 1024, cfg.tn)
    d_wmy = _mm("in_proj_dw", h, du, "tn", BF16, 1024, cfg.tn, 512)
    gx, d_ng = _rms_bwd(cfg, x2, norm_gain, dh, dres)

    small = dict(norm_gain=d_ng, fox_forget_bias=d_fb[:, :FH], rwkv_shift_mix=_rwkv_vec_from_my(cfg, d_mu),
                 rwkv_w0=d_w0, rwkv_a0=d_a0, rwkv_k_k=d_kk, rwkv_k_a=d_ka, rwkv_r_k=d_rk, rwkv_ln_w=d_lnw,
                 rwkv_ln_b=d_lnb, final_norm_gain=d_fng)
    big = dict(w_in=d_wmy, rwkv_w2=d_w2p[:lora], rwkv_a2=d_a2p[:lora], w_proj_fox=d_wpf, w_proj_rwkv=d_wpr,
               w_out=d_wout)
    return loss8[0, 0], gx, small, big


_SMALL = ["norm_gain", "fox_forget_bias", "rwkv_shift_mix", "rwkv_w0", "rwkv_a0", "rwkv_k_k", "rwkv_k_a", "rwkv_r_k",
          "rwkv_ln_w", "rwkv_ln_b", "final_norm_gain"]
_WEIGHTS = ["norm_gain", "w_in", "fox_forget_bias", "rwkv_shift_mix", "rwkv_w0", "rwkv_w2", "rwkv_a0", "rwkv_a2",
            "rwkv_k_k", "rwkv_k_a", "rwkv_r_k", "rwkv_ln_w", "rwkv_ln_b", "w_proj_fox", "w_proj_rwkv", "w_out",
            "final_norm_gain"]


def _pack_small(arrs):
    parts = []
    for a in arrs:
        f = a.reshape(-1)
        parts.append(jnp.pad(f, (0, (-f.shape[0]) % LANES)))
    flat = jnp.concatenate(parts)
    rows = flat.shape[0] // LANES
    flat = jnp.pad(flat, (0, ((-rows) % 8) * LANES))
    return flat.reshape(-1, LANES)


def _unpack_small(packed, shapes):
    flat = packed.reshape(-1)
    out, pos = [], 0
    for s in shapes:
        n = int(np.prod(s))
        out.append(flat[pos:pos + n].reshape(s))
        pos += n + ((-n) % LANES)
    return out


def _shard_major(a, axis):
    parts = jnp.split(a, N_CHIPS, axis=axis)
    return jnp.stack(parts, axis=0)


def kernel(x, norm_gain, w_in, fox_forget_bias, rwkv_shift_mix, rwkv_w0, rwkv_w2, rwkv_a0, rwkv_a2, rwkv_k_k, rwkv_k_a, rwkv_r_k, rwkv_ln_w, rwkv_ln_b, w_proj_fox, w_proj_rwkv, w_out, final_norm_gain, loss_target, m_norm_gain, m_w_in, m_fox_forget_bias, m_rwkv_shift_mix, m_rwkv_w0, m_rwkv_w2, m_rwkv_a0, m_rwkv_a2, m_rwkv_k_k, m_rwkv_k_a, m_rwkv_r_k, m_rwkv_ln_w, m_rwkv_ln_b, m_w_proj_fox, m_w_proj_rwkv, m_w_out, m_final_norm_gain, v_norm_gain, v_w_in, v_fox_forget_bias, v_rwkv_shift_mix, v_rwkv_w0, v_rwkv_w2, v_rwkv_a0, v_rwkv_a2, v_rwkv_k_k, v_rwkv_k_a, v_rwkv_r_k, v_rwkv_ln_w, v_rwkv_ln_b, v_w_proj_fox, v_w_proj_rwkv, v_w_out, v_final_norm_gain):
    args = dict(locals())
    T, D = x.shape[1], x.shape[2]
    lora = rwkv_w2.shape[1]
    cfg = _Cfg(T, D, lora)
    RW = cfg.RW
    c_idx = lax.axis_index("c").astype(jnp.int32).reshape(1)

    w_in_s = w_in[0]
    wp_s = jnp.concatenate([w_proj_fox[0], w_proj_rwkv[0]], axis=0)
    lora_s = jnp.concatenate([rwkv_w2[0], rwkv_a2[0]], axis=0)
    g_in, g_wp, g_out, g_lora = _gather_weights([
        _cast_bf16("cast_w_in", w_in_s), _cast_bf16("cast_w_proj", wp_s), _cast_bf16("cast_w_out", w_out[0]), lora_s])
    w_glob = g_in.transpose(1, 0, 2).reshape(D, cfg.in_cols)
    w_my = _to_my_layout(cfg, w_glob)
    wp = g_wp.transpose(1, 0, 2).reshape(2 * RW, D)
    wout = g_out.reshape(D, D)
    lo = g_lora.transpose(1, 0, 2).reshape(2 * lora, RW)

    loss_dev, gx, small, big = _local_step(
        cfg, x[0], loss_target[0], norm_gain, w_my, fox_forget_bias, rwkv_shift_mix, rwkv_w0, lo[:lora], rwkv_a0,
        lo[lora:], rwkv_k_k, rwkv_k_a, rwkv_r_k, rwkv_ln_w, rwkv_ln_b, wp[:RW], wp[RW:], wout, final_norm_gain)
    loss = lax.psum(loss_dev, ("x", "y", "c"))

    gs_in = _shard_major(_from_my_layout(cfg, big["w_in"]), 1)
    gs_wp = _shard_major(jnp.concatenate([big["w_proj_fox"], big["w_proj_rwkv"]], axis=0), 1)
    gs_out = _shard_major(big["w_out"], 0)
    gs_lora = _shard_major(jnp.concatenate([big["rwkv_w2"], big["rwkv_a2"]], axis=0).astype(BF16), 1)
    gs = [gs_in, gs_wp, gs_out, gs_lora]
    names = ["w_in", "w_proj", "w_out", "lora"]
    recv1 = _exchange_halves(gs)
    chip_sums = [_add_halves("add_halves_" + nm, g, r, c_idx) for nm, g, r in zip(names, gs, recv1)]
    small_shapes = [args[nm].shape for nm in _SMALL]
    packed = _pack_small([small[nm] for nm in _SMALL])
    *recv2, small_all = _scatter_to_owners(chip_sums, packed)
    red_halves = [_sum_slots("sum_chips_" + nm, r) for nm, r in zip(names, recv2)]
    g_small = _sum_slots("sum_small", small_all)
    g_in_f, g_wp_f, g_out_f, g_lora_f = _join_halves(red_halves)

    grads = dict(zip(_SMALL, _unpack_small(g_small, small_shapes)))
    grads["w_in"] = g_in_f[None]
    grads["w_proj_fox"] = g_wp_f[None, :RW]
    grads["w_proj_rwkv"] = g_wp_f[None, RW:]
    grads["w_out"] = g_out_f[None]
    grads["rwkv_w2"] = g_lora_f[None, :lora]
    grads["rwkv_a2"] = g_lora_f[None, lora:]

    delta, new_m, new_v = {}, {}, {}
    w_small = _pack_small([args[nm] for nm in _SMALL])
    m_small = _pack_small([args["m_" + nm] for nm in _SMALL])
    v_small = _pack_small([args["v_" + nm] for nm in _SMALL])
    d_s, m_s, v_s = _adamw("adamw_small", w_small, g_small, m_small, v_small)
    for tgt, pk in ((delta, d_s), (new_m, m_s), (new_v, v_s)):
        tgt.update(zip(_SMALL, _unpack_small(pk, small_shapes)))
    for nm in ("w_in", "w_proj_fox", "w_proj_rwkv", "w_out", "rwkv_w2", "rwkv_a2"):
        shp = args[nm].shape
        two_d = (shp[1], shp[2])
        d_b, m_b, v_b = _adamw("adamw_" + nm, args[nm].reshape(two_d), grads[nm].reshape(two_d),
                               args["m_" + nm].reshape(two_d), args["v_" + nm].reshape(two_d))
        delta[nm], new_m[nm], new_v[nm] = d_b.reshape(shp), m_b.reshape(shp), v_b.reshape(shp)

    return (loss, gx[None], *[grads[n] for n in _WEIGHTS], *[delta[n] for n in _WEIGHTS],
            *[new_m[n] for n in _WEIGHTS], *[new_v[n] for n in _WEIGHTS])
```

```python
import functools

import numpy as np
import jax
import jax.numpy as jnp
from jax import lax
from jax.experimental import pallas as pl
from jax.experimental.pallas import tpu as pltpu

F32 = jnp.float32
BF16 = jnp.bfloat16
HI = lax.Precision.HIGHEST
MESH = pl.DeviceIdType.MESH

FOX_HEAD_DIM = 128
RWKV_HEAD_DIM = 64
RMS_EPS = 1e-6
GN_EPS = 64e-5
L2_EPS = 1e-12
ADAM_LR = 0.001
ADAM_B1 = 0.9
ADAM_B2 = 0.999
ADAM_EPS = 1e-08
ADAM_WD = 0.01
ADAM_STEP = 10

LANES = 128
VMEM_LIMIT = 56 * 1024 * 1024
SCAN_CHUNK = 64
SCAN_HEADS_PER_STEP = 4
N_CHIPS = 4
N_DEV = 8

_pcall = pl.pallas_call


def _cparams(sem=None):
    return pltpu.CompilerParams(dimension_semantics=sem, vmem_limit_bytes=VMEM_LIMIT)


def _softplus(x):
    return jnp.maximum(x, 0.0) + jnp.log(1.0 + jnp.exp(-jnp.abs(x)))


def _silu(z):
    return z * jax.nn.sigmoid(z)


def _rmsn(x, g):
    return x * lax.rsqrt(jnp.mean(x * x, axis=-1, keepdims=True) + RMS_EPS) * g


def _dot(a, b, dims="nn", precision=None):
    dn = {"nn": (((1,), (0,)), ((), ())), "nt": (((1,), (1,)), ((), ())), "tn": (((0,), (0,)), ((), ()))}[dims]
    return lax.dot_general(a, b, dn, precision=precision, preferred_element_type=F32)


def _split_bf16(x):
    hi = x.astype(BF16)
    return hi, (x - hi.astype(F32)).astype(BF16)


def _bdot_raw(a, b, ca, cb):
    dn = (((ca,), (cb,)), ((0,), (0,)))
    ah, al = _split_bf16(a)
    bh, bl = _split_bf16(b)
    mm = lambda p, q: lax.dot_general(p, q, dn, preferred_element_type=F32)
    return mm(ah, bh) + (mm(ah, bl) + mm(al, bh))


@functools.partial(jax.custom_vjp, nondiff_argnums=(2, 3))
def _bdot(a, b, ca, cb):
    return _bdot_raw(a, b, ca, cb)


def _bdot_fwd(a, b, ca, cb):
    return _bdot_raw(a, b, ca, cb), (a, b)


def _bdot_bwd(ca, cb, res, g):
    a, b = res
    if (ca, cb) == (2, 1):
        return _bdot(g, b, 2, 2), _bdot(a, g, 1, 1)
    if (ca, cb) == (2, 2):
        return _bdot(g, b, 2, 1), _bdot(g, a, 1, 1)
    assert (ca, cb) == (1, 1)
    return _bdot(b, g, 2, 2), _bdot(a, g, 2, 1)


_bdot.defvjp(_bdot_fwd, _bdot_bwd)


class _Cfg:
    def __init__(self, T, D, lora):
        self.T, self.D, self.lora = T, D, lora
        self.FW = D // 2
        self.FH = self.FW // FOX_HEAD_DIM
        self.RW = D // 2
        self.RH = self.RW // RWKV_HEAD_DIM
        self.LP = -(-lora // LANES) * LANES
        self.o_fox = 0
        self.o_rwkv = 4 * self.FW
        self.o_gate = self.o_rwkv + 4 * self.RW
        self.o_f = self.o_gate + 2 * D
        self.o_wd = self.o_f + LANES
        self.o_ad = self.o_wd + self.LP
        end = self.o_ad + self.LP
        self.tn = 1280 if D >= 2048 else LANES
        self.ncol = -(-end // self.tn) * self.tn
        self.in_cols = 4 * self.FW + self.FH + 4 * self.RW + 2 * lora + 2 * D
        self.rseg = 4 * self.RW + 2 * self.LP
        self.C = min(SCAN_CHUNK, T)
        self.tr = min(256, T)
        self.hb = min(SCAN_HEADS_PER_STEP, self.RH)

    def segments(self):
        FW, FH, RW, lo, D = self.FW, self.FH, self.RW, self.lora, self.D
        g_f = 4 * FW
        g_r = g_f + FH
        g_wd = g_r + 4 * RW
        g_ad = g_wd + lo
        g_g = g_ad + lo
        return [(0, 4 * FW, 0), (g_f, FH, self.o_f), (g_r, 4 * RW, self.o_rwkv), (g_wd, lo, self.o_wd),
                (g_ad, lo, self.o_ad), (g_g, 2 * D, self.o_gate)]


def _to_my_layout(cfg, wg):
    R = wg.shape[0]
    segs = sorted(cfg.segments(), key=lambda s: s[2])
    parts, pos = [], 0
    for g0, w, m0 in segs:
        if m0 > pos:
            parts.append(jnp.zeros((R, m0 - pos), wg.dtype))
        parts.append(wg[:, g0:g0 + w])
        pos = m0 + w
    if cfg.ncol > pos:
        parts.append(jnp.zeros((R, cfg.ncol - pos), wg.dtype))
    return jnp.concatenate(parts, axis=1)


def _from_my_layout(cfg, wm):
    segs = sorted(cfg.segments(), key=lambda s: s[0])
    return jnp.concatenate([wm[:, m0:m0 + w] for g0, w, m0 in segs], axis=1)


def _shards_to_my_layout(cfg, g):
    R, sc = g.shape[1], g.shape[2]
    segs = sorted(cfg.segments(), key=lambda s: s[2])
    parts, pos = [], 0
    for g0, w, m0 in segs:
        if m0 > pos:
            parts.append(jnp.zeros((R, m0 - pos), g.dtype))
        for s in range(N_CHIPS):
            lo, hi = max(g0, s * sc), min(g0 + w, (s + 1) * sc)
            if lo < hi:
                parts.append(g[s, :, lo - s * sc:hi - s * sc])
        pos = m0 + w
    if cfg.ncol > pos:
        parts.append(jnp.zeros((R, cfg.ncol - pos), g.dtype))
    return jnp.concatenate(parts, axis=1)


def _my_layout_to_shards(cfg, wm):
    sc = cfg.in_cols // N_CHIPS
    segs = sorted(cfg.segments(), key=lambda s: s[0])
    shards = []
    for s in range(N_CHIPS):
        parts = []
        for g0, w, m0 in segs:
            lo, hi = max(g0, s * sc), min(g0 + w, (s + 1) * sc)
            if lo < hi:
                parts.append(wm[:, m0 + lo - g0:m0 + hi - g0])
        shards.append(jnp.concatenate(parts, axis=1))
    return jnp.stack(shards, axis=0)


def _rwkv_vec_to_my(cfg, v):
    RW4, lo, LP = 4 * cfg.RW, cfg.lora, cfg.LP
    z = jnp.zeros((1, LP - lo), v.dtype)
    return jnp.concatenate([v[:, :RW4], v[:, RW4:RW4 + lo], z, v[:, RW4 + lo:], z], axis=1)


def _rwkv_vec_from_my(cfg, v):
    RW4, lo, LP = 4 * cfg.RW, cfg.lora, cfg.LP
    return jnp.concatenate([v[:, :RW4], v[:, RW4:RW4 + lo], v[:, RW4 + LP:RW4 + LP + lo]], axis=1)


def _mm(name, a, b, dims, out_dtype, tm, tn, tk):
    (M, K) = a.shape if dims != "tn" else a.shape[::-1]
    N = b.shape[0] if dims == "nt" else b.shape[1]
    tm, tn, tk = min(tm, M), min(tn, N), min(tk, K)
    assert M % tm == 0 and N % tn == 0 and K % tk == 0, (name, M, N, K, tm, tn, tk)
    nk = K // tk
    if dims == "nn":
        a_spec = pl.BlockSpec((tm, tk), lambda i, j, k: (i, k))
        b_spec = pl.BlockSpec((tk, tn), lambda i, j, k: (k, j))
    elif dims == "nt":
        a_spec = pl.BlockSpec((tm, tk), lambda i, j, k: (i, k))
        b_spec = pl.BlockSpec((tn, tk), lambda i, j, k: (j, k))
    else:
        a_spec = pl.BlockSpec((tk, tm), lambda i, j, k: (k, i))
        b_spec = pl.BlockSpec((tk, tn), lambda i, j, k: (k, j))

    def body(a_ref, b_ref, o_ref, acc_ref):
        k = pl.program_id(2)

        @pl.when(k == 0)
        def _():
            acc_ref[...] = jnp.zeros_like(acc_ref)

        acc_ref[...] += _dot(a_ref[...], b_ref[...], dims)

        @pl.when(k == nk - 1)
        def _():
            o_ref[...] = acc_ref[...].astype(o_ref.dtype)

    return _pcall(
        body, name=name, grid=(M // tm, N // tn, nk),
        in_specs=[a_spec, b_spec], out_specs=pl.BlockSpec((tm, tn), lambda i, j, k: (i, j)),
        out_shape=jax.ShapeDtypeStruct((M, N), out_dtype), scratch_shapes=[pltpu.VMEM((tm, tn), F32)],
        compiler_params=_cparams(("parallel", "parallel", "arbitrary")),
    )(a, b)


def _tile(tr, w, cb=0):
    return pl.BlockSpec((tr, w), lambda i: (i, cb))


def _const(shape):
    nd = len(shape)
    return pl.BlockSpec(shape, lambda i: (0,) * nd)


def _acc_store(i, ref, val):
    @pl.when(i == 0)
    def _():
        ref[...] = val

    @pl.when(i > 0)
    def _():
        ref[...] += val


def _rms_fwd(cfg, x2, g):
    T, D, tr = cfg.T, cfg.D, cfg.tr

    def body(x_ref, g_ref, h_ref):
        h_ref[...] = _rmsn(x_ref[...], g_ref[...]).astype(BF16)

    return _pcall(body, name="rms_fwd", grid=(T // tr,), in_specs=[_tile(tr, D), _const((1, D))],
                  out_specs=_tile(tr, D), out_shape=jax.ShapeDtypeStruct((T, D), BF16),
                  compiler_params=_cparams(("parallel",)))(x2, g)


def _rms_bwd(cfg, x2, g, dh, dres):
    T, D, tr = cfg.T, cfg.D, cfg.tr

    def body(x_ref, g_ref, dh_ref, dres_ref, gx_ref, dg_ref):
        _, vjp = jax.vjp(_rmsn, x_ref[...], g_ref[...])
        dx, dg = vjp(dh_ref[...])
        gx_ref[...] = dx + dres_ref[...]
        _acc_store(pl.program_id(0), dg_ref, dg)

    return _pcall(body, name="rms_bwd", grid=(T // tr,),
                  in_specs=[_tile(tr, D), _const((1, D)), _tile(tr, D), _tile(tr, D)],
                  out_specs=[_tile(tr, D), _const((1, D))],
                  out_shape=[jax.ShapeDtypeStruct((T, D), F32), jax.ShapeDtypeStruct((1, D), F32)],
                  compiler_params=_cparams(("arbitrary",)))(x2, g, dh, dres)


def _final(cfg, x2, mo, fg, target):
    T, D, tr = cfg.T, cfg.D, cfg.tr

    def loss_fn(hres, g, tgt):
        err = _rmsn(hres, g) - tgt
        return 0.5 * jnp.sum(jnp.mean(err * err, axis=-1, keepdims=True), axis=0, keepdims=True)

    def body(x_ref, mo_ref, g_ref, t_ref, loss_ref, dres_ref, dres16_ref, dg_ref):
        hres = x_ref[...] + mo_ref[...]
        loss, vjp = jax.vjp(functools.partial(loss_fn, tgt=t_ref[...]), hres, g_ref[...])
        dres, dg = vjp(jnp.ones((1, 1), F32))
        dres_ref[...] = dres
        dres16_ref[...] = dres.astype(BF16)
        i = pl.program_id(0)
        _acc_store(i, dg_ref, dg)
        _acc_store(i, loss_ref, jnp.broadcast_to(loss, (8, LANES)))

    return _pcall(body, name="final_loss", grid=(T // tr,),
                  in_specs=[_tile(tr, D), _tile(tr, D), _const((1, D)), _tile(tr, D)],
                  out_specs=[_const((8, LANES)), _tile(tr, D), _tile(tr, D), _const((1, D))],
                  out_shape=[jax.ShapeDtypeStruct((8, LANES), F32), jax.ShapeDtypeStruct((T, D), F32),
                             jax.ShapeDtypeStruct((T, D), BF16), jax.ShapeDtypeStruct((1, D), F32)],
                  compiler_params=_cparams(("arbitrary",)))(x2, mo, fg, target)


def _merge_fn(pa, pb, ga, gb):
    return jax.nn.sigmoid(ga) * pa + jax.nn.sigmoid(gb) * pb


def _merge_fwd(cfg, pa, pb, u):
    T, D, tr = cfg.T, cfg.D, cfg.tr
    cga, cgb = cfg.o_gate // D, cfg.o_gate // D + 1

    def body(pa_ref, pb_ref, ga_ref, gb_ref, m_ref):
        m_ref[...] = _merge_fn(pa_ref[...], pb_ref[...], ga_ref[...], gb_ref[...]).astype(BF16)

    return _pcall(body, name="merge_fwd", grid=(T // tr,),
                  in_specs=[_tile(tr, D), _tile(tr, D), _tile(tr, D, cga), _tile(tr, D, cgb)],
                  out_specs=_tile(tr, D), out_shape=jax.ShapeDtypeStruct((T, D), BF16),
                  compiler_params=_cparams(("parallel",)))(pa, pb, u, u)


def _merge_bwd(cfg, pa, pb, u, dm):
    T, D, tr = cfg.T, cfg.D, cfg.tr
    cga, cgb = cfg.o_gate // D, cfg.o_gate // D + 1

    def body(pa_ref, pb_ref, ga_ref, gb_ref, dm_ref, dpa_ref, dpb_ref, dg_ref):
        _, vjp = jax.vjp(_merge_fn, pa_ref[...], pb_ref[...], ga_ref[...], gb_ref[...])
        dpa, dpb, dga, dgb = vjp(dm_ref[...])
        dpa_ref[...] = dpa.astype(BF16)
        dpb_ref[...] = dpb.astype(BF16)
        dg_ref[:, :D] = dga.astype(BF16)
        dg_ref[:, D:] = dgb.astype(BF16)

    return _pcall(body, name="merge_bwd", grid=(T // tr,),
                  in_specs=[_tile(tr, D), _tile(tr, D), _tile(tr, D, cga), _tile(tr, D, cgb), _tile(tr, D)],
                  out_specs=[_tile(tr, D), _tile(tr, D), _tile(tr, 2 * D)],
                  out_shape=[jax.ShapeDtypeStruct((T, D), BF16), jax.ShapeDtypeStruct((T, D), BF16),
                             jax.ShapeDtypeStruct((T, 2 * D), BF16)],
                  compiler_params=_cparams(("parallel",)))(pa, pb, u, u, dm)


def _gate_fn(o, z):
    return o * _silu(z)


def _gate_a_fwd(cfg, o, u):
    T, FW, tr = cfg.T, cfg.FW, cfg.tr

    def body(o_ref, z_ref, oa_ref):
        oa_ref[...] = _gate_fn(o_ref[...], z_ref[...]).astype(BF16)

    return _pcall(body, name="gate_a_fwd", grid=(T // tr,), in_specs=[_tile(tr, FW), _tile(tr, FW, 3)],
                  out_specs=_tile(tr, FW), out_shape=jax.ShapeDtypeStruct((T, FW), BF16),
                  compiler_params=_cparams(("parallel",)))(o, u)


def _gate_a_bwd(cfg, o, u, doa):
    T, FW, tr = cfg.T, cfg.FW, cfg.tr

    def body(o_ref, z_ref, doa_ref, do_ref, dz_ref):
        _, vjp = jax.vjp(_gate_fn, o_ref[...], z_ref[...])
        do, dz = vjp(doa_ref[...])
        do_ref[...] = do
        dz_ref[...] = dz.astype(BF16)

    return _pcall(body, name="gate_a_bwd", grid=(T // tr,),
                  in_specs=[_tile(tr, FW), _tile(tr, FW, 3), _tile(tr, FW)],
                  out_specs=[_tile(tr, FW), _tile(tr, FW)],
                  out_shape=[jax.ShapeDtypeStruct((T, FW), F32), jax.ShapeDtypeStruct((T, FW), BF16)],
                  compiler_params=_cparams(("parallel",)))(o, u, doa)


def _fox_prep(cfg, u, fb):
    T, tr = cfg.T, cfg.tr
    cf = cfg.o_f // LANES

    def body(f_ref, fb_ref, c_ref, carry_ref):
        i = pl.program_id(0)

        @pl.when(i == 0)
        def _():
            carry_ref[...] = jnp.zeros_like(carry_ref)

        lf = -_softplus(-(f_ref[...] + fb_ref[...]))
        r = lax.broadcasted_iota(jnp.int32, (tr, tr), 0)
        c = lax.broadcasted_iota(jnp.int32, (tr, tr), 1)
        tri = (r >= c).astype(F32)
        c_ref[...] = _dot(tri, lf, precision=HI) + carry_ref[...]
        carry_ref[...] += jnp.sum(lf, axis=0, keepdims=True)

    return _pcall(body, name="fox_prep", grid=(T // tr,), in_specs=[_tile(tr, LANES, cf), _const((1, LANES))],
                  out_specs=_tile(tr, LANES), out_shape=jax.ShapeDtypeStruct((T, LANES), F32),
                  scratch_shapes=[pltpu.VMEM((1, LANES), F32)], compiler_params=_cparams(("arbitrary",)))(u, fb)


def _fox_prep_bwd(cfg, u, fb, dc):
    T, tr = cfg.T, cfg.tr
    cf = cfg.o_f // LANES
    nb = T // tr

    def body(f_ref, fb_ref, dc_ref, df_ref, dfb_ref, carry_ref):
        i = pl.program_id(0)

        @pl.when(i == 0)
        def _():
            carry_ref[...] = jnp.zeros_like(carry_ref)

        dc = dc_ref[...]
        r = lax.broadcasted_iota(jnp.int32, (tr, tr), 0)
        c = lax.broadcasted_iota(jnp.int32, (tr, tr), 1)
        triu = (r <= c).astype(F32)
        dlf = _dot(triu, dc, precision=HI) + carry_ref[...]
        carry_ref[...] += jnp.sum(dc, axis=0, keepdims=True)
        dz = dlf * jax.nn.sigmoid(-(f_ref[...] + fb_ref[...]))
        df_ref[...] = dz.astype(BF16)
        _acc_store(i, dfb_ref, jnp.sum(dz, axis=0, keepdims=True))

    rev = lambda i: (nb - 1 - i, 0)
    return _pcall(body, name="fox_prep_bwd", grid=(nb,),
                  in_specs=[pl.BlockSpec((tr, LANES), lambda i: (nb - 1 - i, cf)), _const((1, LANES)),
                            pl.BlockSpec((tr, LANES), rev)],
                  out_specs=[pl.BlockSpec((tr, LANES), rev), _const((1, LANES))],
                  out_shape=[jax.ShapeDtypeStruct((T, LANES), BF16), jax.ShapeDtypeStruct((1, LANES), F32)],
                  scratch_shapes=[pltpu.VMEM((1, LANES), F32)], compiler_params=_cparams(("arbitrary",)))(u, fb, dc)


def _attn_logits(q_ref, k_ref, c_ref, i, tq, T):
    s = _dot(q_ref[...].astype(BF16), k_ref[...].astype(BF16), "nt") * (FOX_HEAD_DIM ** -0.5) - c_ref[0]
    row = i * tq + lax.broadcasted_iota(jnp.int32, (tq, T), 0)
    col = lax.broadcasted_iota(jnp.int32, (tq, T), 1)
    return jnp.where(col <= row, s, -1e30)


def _attn_fwd(cfg, u, c_rows):
    T, FW, FH = cfg.T, cfg.FW, cfg.FH
    tq = min(256, T)
    dh = FOX_HEAD_DIM

    def body(q_ref, k_ref, v_ref, c_ref, o_ref, lse_ref):
        s = _attn_logits(q_ref, k_ref, c_ref, pl.program_id(1), tq, T)
        m = jnp.max(s, axis=1, keepdims=True)
        p = jnp.exp(s - m)
        l = jnp.sum(p, axis=1, keepdims=True)
        o_ref[...] = _dot(p.astype(BF16), v_ref[...].astype(BF16)) / l
        lse_ref[0] = m + jnp.log(l)

    return _pcall(
        body, name="fox_attn_fwd", grid=(FH, T // tq),
        in_specs=[pl.BlockSpec((tq, dh), lambda h, i: (i, h)), pl.BlockSpec((T, dh), lambda h, i: (0, FH + h)),
                  pl.BlockSpec((T, dh), lambda h, i: (0, 2 * FH + h)), pl.BlockSpec((1, 1, T), lambda h, i: (h, 0, 0))],
        out_specs=[pl.BlockSpec((tq, dh), lambda h, i: (i, h)), pl.BlockSpec((1, tq, 1), lambda h, i: (h, i, 0))],
        out_shape=[jax.ShapeDtypeStruct((T, FW), F32), jax.ShapeDtypeStruct((FH, T, 1), F32)],
        compiler_params=_cparams(("parallel", "arbitrary")),
    )(u, u, u, c_rows)


def _attn_bwd(cfg, u, c_rows, lse, do):
    T, FW, FH = cfg.T, cfg.FW, cfg.FH
    tq = min(256, T)
    dh = FOX_HEAD_DIM
    scale = dh ** -0.5

    def body(q_ref, k_ref, v_ref, c_ref, lse_ref, do_ref, dq_ref, dk_ref, dv_ref, dcol_ref):
        i = pl.program_id(1)
        s = _attn_logits(q_ref, k_ref, c_ref, i, tq, T)
        p = jnp.exp(s - lse_ref[0])
        do_v = do_ref[...]
        dp = _dot(do_v.astype(BF16), v_ref[...].astype(BF16), "nt")
        delta = jnp.sum(p * dp, axis=1, keepdims=True)
        ds = p * (dp - delta)
        ds16 = ds.astype(BF16)
        dq_ref[...] = (_dot(ds16, k_ref[...].astype(BF16)) * scale).astype(BF16)
        _acc_store(i, dk_ref, _dot(ds16, q_ref[...].astype(BF16), "tn") * scale)
        _acc_store(i, dv_ref, _dot(p.astype(BF16), do_v.astype(BF16), "tn"))
        _acc_store(i, dcol_ref, jnp.sum(ds, axis=0, keepdims=True)[None])

    qspec = pl.BlockSpec((tq, dh), lambda h, i: (i, h))
    return _pcall(
        body, name="fox_attn_bwd", grid=(FH, T // tq),
        in_specs=[qspec, pl.BlockSpec((T, dh), lambda h, i: (0, FH + h)),
                  pl.BlockSpec((T, dh), lambda h, i: (0, 2 * FH + h)), pl.BlockSpec((1, 1, T), lambda h, i: (h, 0, 0)),
                  pl.BlockSpec((1, tq, 1), lambda h, i: (h, i, 0)), qspec],
        out_specs=[qspec, pl.BlockSpec((T, dh), lambda h, i: (0, h)), pl.BlockSpec((T, dh), lambda h, i: (0, h)),
                   pl.BlockSpec((1, 1, T), lambda h, i: (h, 0, 0))],
        out_shape=[jax.ShapeDtypeStruct((T, FW), BF16), jax.ShapeDtypeStruct((T, FW), F32),
                   jax.ShapeDtypeStruct((T, FW), F32), jax.ShapeDtypeStruct((FH, 1, T), F32)],
        compiler_params=_cparams(("parallel", "arbitrary")),
    )(u, u, u, c_rows, lse, do)


def _head_indicators(cfg):
    ind = np.zeros((cfg.RW, LANES), np.float32)
    ind[np.arange(cfg.RW), np.arange(cfg.RW) // RWKV_HEAD_DIM] = 1.0
    pad = np.zeros((1, LANES), np.float32)
    pad[0, cfg.RH:] = 1.0
    return jnp.asarray(ind), jnp.asarray(ind.T.copy()), jnp.asarray(pad)


def _prep_fn(us_r, us_k, us_v, us_wd, us_ad, w0, w2p, a0, a2p, k_k, k_a, ind, ind_t, pad):
    wpre = w0 + _dot(jnp.tanh(us_wd), w2p, precision=HI)
    w = -_softplus(-wpre) - 0.5
    lw = -jnp.exp(w)
    a = jax.nn.sigmoid(a0 + _dot(us_ad, a2p, precision=HI))
    kk = us_k * k_k
    ss = _dot(kk * kk, ind, precision=HI) + pad
    inv = 1.0 / jnp.maximum(jnp.sqrt(ss), L2_EPS)
    kkn = kk * _dot(inv, ind_t, precision=HI)
    kp = us_k * (1.0 + (a - 1.0) * k_a)
    return us_r, lw, kp, us_v, -kkn, kkn * a


def _shifted(u, prev_row, mu, first):
    n = u.shape[0]
    rolled = pltpu.roll(u, 1, 0)
    row = lax.broadcasted_iota(jnp.int32, u.shape, 0)
    p0 = jnp.where(first, jnp.zeros_like(prev_row), prev_row)
    prev = jnp.where(row == 0, jnp.broadcast_to(p0, u.shape), rolled)
    return u + (prev - u) * mu, prev


def _rwkv_specs(cfg, tr):
    RW, LP = cfg.RW, cfg.LP
    base = cfg.o_rwkv // RW
    cols = [(RW, base), (RW, base + 1), (RW, base + 2), (RW, base + 3), (LP, cfg.o_wd // LP), (LP, cfg.o_ad // LP)]
    cur = [pl.BlockSpec((tr, w), (lambda i, cb=cb: (i, cb))) for w, cb in cols]
    prv = [pl.BlockSpec((8, w), (lambda i, cb=cb: (jnp.maximum(i * (tr // 8) - 1, 0), cb))) for w, cb in cols]
    return cols, cur, prv


def _mu_pieces(cfg, mu_ref):
    RW, LP = cfg.RW, cfg.LP
    offs = [0, RW, 2 * RW, 3 * RW, 4 * RW, 4 * RW + LP, 4 * RW + 2 * LP]
    return [mu_ref[:, offs[j]:offs[j + 1]] for j in range(6)]


def _rwkv_prep_fwd(cfg, u, mu, w0, w2p, a0, a2p, k_k, k_a):
    T, RW, LP, tr = cfg.T, cfg.RW, cfg.LP, cfg.tr
    ind, ind_t, pad = _head_indicators(cfg)
    cols, cur, prv = _rwkv_specs(cfg, tr)

    def body(*refs):
        u_refs, p_refs = refs[0:6], refs[6:12]
        mu_ref, w0_ref, w2_ref, a0_ref, a2_ref, kk_ref, ka_ref, ind_ref, indt_ref, pad_ref = refs[12:22]
        outs = refs[22:]
        first = pl.program_id(0) == 0
        mus = _mu_pieces(cfg, mu_ref)
        us = [_shifted(u_refs[j][...], p_refs[j][7:8, :], mus[j], first)[0] for j in range(6)]
        res = _prep_fn(us[0], us[1], us[2], us[4], us[5], w0_ref[...], w2_ref[...], a0_ref[...], a2_ref[...],
                       kk_ref[...], ka_ref[...], ind_ref[...], indt_ref[...], pad_ref[...])
        for j in range(6):
            outs[j][...] = res[j]
        outs[6][...] = us[3]

    consts = [mu, w0, w2p, a0, a2p, k_k, k_a, ind, ind_t, pad]
    return _pcall(body, name="rwkv_prep_fwd", grid=(T // tr,),
                  in_specs=cur + prv + [_const(c.shape) for c in consts],
                  out_specs=[_tile(tr, RW)] * 7, out_shape=[jax.ShapeDtypeStruct((T, RW), F32)] * 7,
                  compiler_params=_cparams(("parallel",)))(*([u] * 12), *consts)


def _rwkv_prep_bwd(cfg, u, mu, w0, w2p, a0, a2p, k_k, k_a, cots, dzb):
    T, RW, LP = cfg.T, cfg.RW, cfg.LP
    tr = min(128, T)
    ind, ind_t, pad = _head_indicators(cfg)
    cols, cur, prv = _rwkv_specs(cfg, tr)
    rseg = cfg.rseg

    def body(*refs):
        u_refs, p_refs = refs[0:6], refs[6:12]
        mu_ref, w0_ref, w2_ref, a0_ref, a2_ref, kk_ref, ka_ref, ind_ref, indt_ref, pad_ref = refs[12:22]
        cot_refs, dzb_ref = refs[22:28], refs[28]
        dus_ref, dmu_ref, dw0_ref, dw2_ref, da0_ref, da2_ref, dkk_ref, dka_ref = refs[29:]
        i = pl.program_id(0)
        first = i == 0
        mus = _mu_pieces(cfg, mu_ref)
        sh = [_shifted(u_refs[j][...], p_refs[j][7:8, :], mus[j], first) for j in range(6)]
        us = [s[0] for s in sh]
        fn = functools.partial(_prep_fn, ind=ind_ref[...], ind_t=indt_ref[...], pad=pad_ref[...])
        _, vjp = jax.vjp(fn, us[0], us[1], us[2], us[4], us[5], w0_ref[...], w2_ref[...], a0_ref[...], a2_ref[...],
                         kk_ref[...], ka_ref[...])
        d = vjp(tuple(c[...] for c in cot_refs))
        dus = [d[0], d[1], d[2], dzb_ref[...], d[3], d[4]]
        offs = [0, RW, 2 * RW, 3 * RW, 4 * RW, 4 * RW + LP, 4 * RW + 2 * LP]
        for j in range(6):
            dus_ref[:, offs[j]:offs[j + 1]] = dus[j]
            dmu_j = jnp.sum(dus[j] * (sh[j][1] - u_refs[j][...]), axis=0, keepdims=True)

            @pl.when(first)
            def _(j=j, dmu_j=dmu_j):
                dmu_ref[:, offs[j]:offs[j + 1]] = dmu_j

            @pl.when(i > 0)
            def _(j=j, dmu_j=dmu_j):
                dmu_ref[:, offs[j]:offs[j + 1]] += dmu_j
        for ref, val in zip((dw0_ref, dw2_ref, da0_ref, da2_ref, dkk_ref, dka_ref), d[5:11]):
            _acc_store(i, ref, val)

    consts = [mu, w0, w2p, a0, a2p, k_k, k_a, ind, ind_t, pad]
    vec = jax.ShapeDtypeStruct((1, RW), F32)
    mat = jax.ShapeDtypeStruct((LP, RW), F32)
    return _pcall(body, name="rwkv_prep_bwd", grid=(T // tr,),
                  in_specs=cur + prv + [_const(c.shape) for c in consts] + [_tile(tr, RW)] * 7,
                  out_specs=[_tile(tr, rseg), _const((1, rseg)), _const((1, RW)), _const((LP, RW)), _const((1, RW)),
                             _const((LP, RW)), _const((1, RW)), _const((1, RW))],
                  out_shape=[jax.ShapeDtypeStruct((T, rseg), F32), jax.ShapeDtypeStruct((1, rseg), F32),
                             vec, mat, vec, mat, vec, vec],
                  compiler_params=_cparams(("arbitrary",)))(*([u] * 12), *consts, *cots, dzb)


def _shift_bwd(cfg, dus, mu):
    T, tr, rseg = cfg.T, cfg.tr, cfg.rseg
    nb = T // tr

    def body(d_ref, n_ref, mu_ref, du_ref):
        d = d_ref[...]
        rolled = pltpu.roll(d, tr - 1, 0)
        row = lax.broadcasted_iota(jnp.int32, d.shape, 0)
        n0 = jnp.where(pl.program_id(0) == nb - 1, jnp.zeros_like(n_ref[0:1, :]), n_ref[0:1, :])
        nxt = jnp.where(row == tr - 1, jnp.broadcast_to(n0, d.shape), rolled)
        mu_v = mu_ref[...]
        du_ref[...] = (d * (1.0 - mu_v) + nxt * mu_v).astype(BF16)

    return _pcall(body, name="shift_bwd", grid=(nb,),
                  in_specs=[_tile(tr, rseg),
                            pl.BlockSpec((8, rseg), lambda i: (jnp.minimum((i + 1) * (tr // 8), T // 8 - 1), 0)),
                            _const((1, rseg))],
                  out_specs=_tile(tr, rseg), out_shape=jax.ShapeDtypeStruct((T, rseg), BF16),
                  compiler_params=_cparams(("parallel",)))(dus, dus, mu)


def _chunk_fn(S0, r, lw, k, v, a, b):
    H, C, K = r.shape
    row = lax.broadcasted_iota(jnp.int32, (C, C), 0)
    col = lax.broadcasted_iota(jnp.int32, (C, C), 1)
    incl = jnp.broadcast_to((row >= col).astype(F32)[None], (H, C, C))
    strict = (row > col)[None]
    lower = (row >= col)[None]
    L = _bdot(incl, lw, 2, 1)
    LC = jnp.sum(lw, axis=1, keepdims=True)
    eL = jnp.exp(L)
    eLn = jnp.exp(-L)
    at = a * jnp.exp(L - lw)
    rt = r * eL
    bt = b * eLn
    kt = k * eLn
    eR = jnp.exp(LC - L)
    zero = jnp.zeros((), F32)
    n_ab = jnp.where(strict, _bdot(at, bt, 2, 2), zero)
    n_ak = jnp.where(strict, _bdot(at, kt, 2, 2), zero)
    m_rb = jnp.where(lower, _bdot(rt, bt, 2, 2), zero)
    m_rk = jnp.where(lower, _bdot(rt, kt, 2, 2), zero)
    U = _bdot(at, S0, 2, 2) + _bdot(n_ak, v, 2, 1)
    M = n_ab
    steps = max(1, int(np.ceil(np.log2(C))))
    for s in range(steps):
        U = U + _bdot(M, U, 2, 1)
        if s + 1 < steps:
            M = _bdot(M, M, 2, 1)
    Y = _bdot(rt, S0, 2, 2) + _bdot(m_rb, U, 2, 1) + _bdot(m_rk, v, 2, 1)
    S1 = S0 * jnp.exp(LC) + _bdot(U, b * eR, 1, 1) + _bdot(v, k * eR, 1, 1)
    return Y, S1


def _scan_fwd(cfg, seqs):
    T, RH, C = cfg.T, cfg.RH, cfg.C
    N = RWKV_HEAD_DIM
    HB = cfg.hb
    nc = T // C

    def body(r_ref, lw_ref, k_ref, v_ref, a_ref, b_ref, y_ref, ck_ref, s_ref):
        @pl.when(pl.program_id(1) == 0)
        def _():
            s_ref[...] = jnp.zeros_like(s_ref)

        S0 = s_ref[...]
        ck_ref[:, 0] = S0
        Y, S1 = _chunk_fn(S0, r_ref[...], lw_ref[...], k_ref[...], v_ref[...], a_ref[...], b_ref[...])
        y_ref[...] = Y
        s_ref[...] = S1

    seq = pl.BlockSpec((HB, C, N), lambda h, j: (h, j, 0))
    return _pcall(body, name="rwkv_scan_fwd", grid=(RH // HB, nc), in_specs=[seq] * 6,
                  out_specs=[seq, pl.BlockSpec((HB, 1, N, N), lambda h, j: (h, j, 0, 0))],
                  out_shape=[jax.ShapeDtypeStruct((RH, T, N), F32), jax.ShapeDtypeStruct((RH, nc, N, N), F32)],
                  scratch_shapes=[pltpu.VMEM((HB, N, N), F32)],
                  compiler_params=_cparams(("parallel", "arbitrary")))(*seqs)


def _scan_bwd(cfg, seqs, ckpt, dy):
    T, RH, C = cfg.T, cfg.RH, cfg.C
    N = RWKV_HEAD_DIM
    HB = cfg.hb
    nc = T // C

    def body(r_ref, lw_ref, k_ref, v_ref, a_ref, b_ref, ck_ref, dy_ref, *rest):
        outs, ds_ref = rest[:6], rest[6]

        @pl.when(pl.program_id(1) == 0)
        def _():
            ds_ref[...] = jnp.zeros_like(ds_ref)

        _, vjp = jax.vjp(_chunk_fn, ck_ref[:, 0], r_ref[...], lw_ref[...], k_ref[...], v_ref[...], a_ref[...],
                         b_ref[...])
        d = vjp((dy_ref[...], ds_ref[...]))
        ds_ref[...] = d[0]
        for j in range(6):
            outs[j][...] = d[1 + j]

    seq = pl.BlockSpec((HB, C, N), lambda h, j: (h, nc - 1 - j, 0))
    return _pcall(body, name="rwkv_scan_bwd", grid=(RH // HB, nc),
                  in_specs=[seq] * 6 + [pl.BlockSpec((HB, 1, N, N), lambda h, j: (h, nc - 1 - j, 0, 0)), seq],
                  out_specs=[seq] * 6, out_shape=[jax.ShapeDtypeStruct((RH, T, N), F32)] * 6,
                  scratch_shapes=[pltpu.VMEM((HB, N, N), F32)],
                  compiler_params=_cparams(("parallel", "arbitrary")))(*seqs, ckpt, dy)


def _post_fn(y, r, kp, v, zb, ln_w, ln_b, rk, ind, ind_t):
    n = float(RWKV_HEAD_DIM)
    mu = _dot(_dot(y, ind, precision=HI) / n, ind_t, precision=HI)
    yc = y - mu
    var = _dot(yc * yc, ind, precision=HI) / n
    rstd = _dot(lax.rsqrt(var + GN_EPS), ind_t, precision=HI)
    yn = yc * rstd * ln_w + ln_b
    bonus = _dot(_dot(r * kp * rk, ind, precision=HI), ind_t, precision=HI) * v
    return (yn + bonus) * _silu(zb)


def _rwkv_post_fwd(cfg, y, r, kp, v, zb, ln_w, ln_b, rk):
    T, RW, tr = cfg.T, cfg.RW, cfg.tr
    ind, ind_t, _ = _head_indicators(cfg)

    def body(y_ref, r_ref, k_ref, v_ref, z_ref, lw_ref, lb_ref, rk_ref, ind_ref, indt_ref, ob_ref):
        ob_ref[...] = _post_fn(y_ref[...], r_ref[...], k_ref[...], v_ref[...], z_ref[...], lw_ref[...], lb_ref[...],
                               rk_ref[...], ind_ref[...], indt_ref[...]).astype(BF16)

    consts = [ln_w, ln_b, rk, ind, ind_t]
    return _pcall(body, name="rwkv_post_fwd", grid=(T // tr,),
                  in_specs=[_tile(tr, RW)] * 5 + [_const(c.shape) for c in consts],
                  out_specs=_tile(tr, RW), out_shape=jax.ShapeDtypeStruct((T, RW), BF16),
                  compiler_params=_cparams(("parallel",)))(y, r, kp, v, zb, *consts)


def _rwkv_post_bwd(cfg, y, r, kp, v, zb, ln_w, ln_b, rk, dob):
    T, RW = cfg.T, cfg.RW
    tr = min(128, T)
    ind, ind_t, _ = _head_indicators(cfg)

    def body(y_ref, r_ref, k_ref, v_ref, z_ref, lw_ref, lb_ref, rk_ref, ind_ref, indt_ref, dob_ref,
             dy_ref, dr_ref, dk_ref, dv_ref, dz_ref, dlw_ref, dlb_ref, drk_ref):
        fn = functools.partial(_post_fn, ind=ind_ref[...], ind_t=indt_ref[...])
        _, vjp = jax.vjp(fn, y_ref[...], r_ref[...], k_ref[...], v_ref[...], z_ref[...], lw_ref[...], lb_ref[...],
                         rk_ref[...])
        d = vjp(dob_ref[...])
        for ref, val in zip((dy_ref, dr_ref, dk_ref, dv_ref, dz_ref), d[:5]):
            ref[...] = val
        i = pl.program_id(0)
        for ref, val in zip((dlw_ref, dlb_ref, drk_ref), d[5:8]):
            _acc_store(i, ref, val)

    consts = [ln_w, ln_b, rk, ind, ind_t]
    vec = jax.ShapeDtypeStruct((1, RW), F32)
    return _pcall(body, name="rwkv_post_bwd", grid=(T // tr,),
                  in_specs=[_tile(tr, RW)] * 5 + [_const(c.shape) for c in consts] + [_tile(tr, RW)],
                  out_specs=[_tile(tr, RW)] * 5 + [_const((1, RW))] * 3,
                  out_shape=[jax.ShapeDtypeStruct((T, RW), F32)] * 5 + [vec] * 3,
                  compiler_params=_cparams(("arbitrary",)))(y, r, kp, v, zb, *consts, dob)


def _adamw_math(w, g, m, v):
    m = ADAM_B1 * m + (1.0 - ADAM_B1) * g
    v = ADAM_B2 * v + (1.0 - ADAM_B2) * (g * g)
    m_hat = m / (1.0 - ADAM_B1 ** ADAM_STEP)
    v_hat = v / (1.0 - ADAM_B2 ** ADAM_STEP)
    delta = -ADAM_LR * (m_hat / (jnp.sqrt(v_hat) + ADAM_EPS) + ADAM_WD * w)
    return delta, m, v


def _adamw(name, w, g, m, v):
    R, Cc = w.shape
    tr = R
    for cand in (512, 256, 128, 64, 32, 16, 8):
        if R % cand == 0 and cand * Cc * 4 <= 2 * 1024 * 1024:
            tr = cand
            break

    def body(w_ref, g_ref, m_ref, v_ref, d_ref, nm_ref, nv_ref):
        d, nm, nv = _adamw_math(w_ref[...], g_ref[...], m_ref[...], v_ref[...])
        d_ref[...] = d
        nm_ref[...] = nm
        nv_ref[...] = nv

    spec = _tile(tr, Cc)
    return _pcall(body, name=name, grid=(R // tr,), in_specs=[spec] * 4, out_specs=[spec] * 3,
                  out_shape=[jax.ShapeDtypeStruct((R, Cc), F32)] * 3,
                  compiler_params=_cparams(("parallel",)))(w, g, m, v)


def _row_tile(R, Cc, itemsize, budget=2 * 1024 * 1024):
    for cand in (1024, 512, 256, 128, 64, 32, 16):
        if R % cand == 0 and cand * Cc * itemsize <= budget:
            return cand
    return R


def _add_halves(name, gs, r1, c_idx):
    _, R, Cc = gs.shape
    half = R // 2
    tr = _row_tile(half, Cc, 4)
    nb = half // tr

    def body(c_ref, g_ref, r_ref, o_ref):
        o_ref[...] = (g_ref[...].astype(F32) + r_ref[...].astype(F32)).astype(BF16)

    grid_spec = pltpu.PrefetchScalarGridSpec(
        num_scalar_prefetch=1, grid=(N_CHIPS, nb),
        in_specs=[pl.BlockSpec((1, tr, Cc), lambda s, i, c: (s, c[0] * nb + i, 0)),
                  pl.BlockSpec((1, tr, Cc), lambda s, i, c: (s, i, 0))],
        out_specs=pl.BlockSpec((1, tr, Cc), lambda s, i, c: (s, i, 0)))
    return _pcall(body, name=name, grid_spec=grid_spec, out_shape=jax.ShapeDtypeStruct((N_CHIPS, half, Cc), BF16),
                  compiler_params=_cparams(("parallel", "parallel")))(c_idx, gs, r1)


def _sum_slots(name, r2):
    S, R, Cc = r2.shape
    tr = _row_tile(R, Cc, 4 * S // 2 if r2.dtype == BF16 else 4 * S)

    def body(r_ref, o_ref):
        acc = r_ref[0].astype(F32)
        for s in range(1, S):
            acc = acc + r_ref[s].astype(F32)
        o_ref[...] = acc

    return _pcall(body, name=name, grid=(R // tr,), in_specs=[pl.BlockSpec((S, tr, Cc), lambda i: (0, i, 0))],
                  out_specs=_tile(tr, Cc), out_shape=jax.ShapeDtypeStruct((R, Cc), F32),
                  compiler_params=_cparams(("parallel",)))(r2)


def _sum_chips(name, recv, own, place):
    S, H, Cc = recv.shape
    tr = _row_tile(H, Cc, 4, 1024 * 1024)
    nb = H // tr

    def body(p_ref, r_ref, own_ref, o_ref):
        s = pl.program_id(1)
        me = p_ref[0]

        @pl.when(s == 0)
        def _():
            o_ref[...] = jnp.zeros_like(o_ref)

        @pl.when(s == me)
        def _():
            o_ref[...] += own_ref[0].astype(F32)

        @pl.when(s != me)
        def _():
            o_ref[...] += r_ref[0].astype(F32)

    grid_spec = pltpu.PrefetchScalarGridSpec(
        num_scalar_prefetch=1, grid=(nb, S),
        in_specs=[pl.BlockSpec((1, tr, Cc), lambda i, s, p: (jnp.where(s == p[0], (s + 1) % S, s), i, 0)),
                  pl.BlockSpec((1, tr, Cc), lambda i, s, p: (p[0], i, 0))],
        out_specs=pl.BlockSpec((tr, Cc), lambda i, s, p: (p[1] * nb + i, 0)))
    return _pcall(body, name=name, grid_spec=grid_spec, out_shape=jax.ShapeDtypeStruct((2 * H, Cc), F32),
                  compiler_params=_cparams(("parallel", "arbitrary")))(place, recv, own)


def _cast_bf16(name, w):
    R, Cc = w.shape
    tr = _row_tile(R, Cc, 4)

    def body(w_ref, o_ref):
        o_ref[...] = w_ref[...].astype(BF16)

    return _pcall(body, name=name, grid=(R // tr,), in_specs=[_tile(tr, Cc)], out_specs=_tile(tr, Cc),
                  out_shape=jax.ShapeDtypeStruct((R, Cc), BF16), compiler_params=_cparams(("parallel",)))(w)


_ANY = pl.BlockSpec(memory_space=pl.ANY)


def _place():
    x, y, c = lax.axis_index("x"), lax.axis_index("y"), lax.axis_index("c")
    others = [(1 - x, y), (x, 1 - y), (1 - x, 1 - y)]
    return x, y, c, others


def _gather_weights(shards):
    n = len(shards)
    halves = [s.shape[0] // 2 for s in shards]

    def body(*refs):
        ins, outs = refs[:n], refs[n:2 * n]
        send_sems, recv_sems = refs[2 * n:]
        x, y, c, others = _place()
        me = 2 * x + y

        def rows(k, ref, chip, hc):
            return ref.at[chip, pl.ds(hc * halves[k], halves[k]), :]

        def remote(k, j, src, dst, to):
            return pltpu.make_async_remote_copy(src_ref=src, dst_ref=dst, send_sem=send_sems.at[6 * k + j],
                                                recv_sem=recv_sems.at[6 * k + j], device_id=to, device_id_type=MESH)

        first, passed = [], []
        for k in range(n):
            mine = ins[k].at[pl.ds(c * halves[k], halves[k]), :]
            for j, (px, py) in enumerate(others):
                cp = remote(k, j, mine, rows(k, outs[k], me, c), (px, py, c))
                cp.start()
                first.append(cp)
        for k in range(n):
            for j, (px, py) in enumerate(others):
                land = rows(k, outs[k], 2 * px + py, c)
                remote(k, j, land, land, (x, y, c)).wait_recv()
                cp = remote(k, 3 + j, land, land, (x, y, 1 - c))
                cp.start()
                passed.append(cp)
        for k in range(n):
            for j, (px, py) in enumerate(others):
                land = rows(k, outs[k], 2 * px + py, 1 - c)
                remote(k, 3 + j, land, land, (x, y, c)).wait_recv()
        for cp in first + passed:
            cp.wait_send()

    return _pcall(
        body, name="gather_weights", in_specs=[_ANY] * n, out_specs=[_ANY] * n,
        out_shape=[jax.ShapeDtypeStruct((N_CHIPS,) + s.shape, s.dtype) for s in shards],
        scratch_shapes=[pltpu.SemaphoreType.DMA((6 * n,)), pltpu.SemaphoreType.DMA((6 * n,))],
    )(*shards)


def _exchange_halves(grads):
    n = len(grads)
    halves = [g.shape[1] // 2 for g in grads]

    def body(*refs):
        ins, outs = refs[:n], refs[n:2 * n]
        send_sems, recv_sems = refs[2 * n:]
        x, y, c, _ = _place()
        cps = []
        for k in range(n):
            src = ins[k].at[:, pl.ds((1 - c) * halves[k], halves[k]), :]
            cp = pltpu.make_async_remote_copy(src_ref=src, dst_ref=outs[k], send_sem=send_sems.at[k],
                                              recv_sem=recv_sems.at[k], device_id=(x, y, 1 - c), device_id_type=MESH)
            cp.start()
            cps.append(cp)
        for cp in cps:
            cp.wait()

    return _pcall(
        body, name="exchange_halves", in_specs=[_ANY] * n, out_specs=[_ANY] * n,
        out_shape=[jax.ShapeDtypeStruct((N_CHIPS, h) + g.shape[2:], g.dtype) for g, h in zip(grads, halves)],
        scratch_shapes=[pltpu.SemaphoreType.DMA((n,)), pltpu.SemaphoreType.DMA((n,))],
    )(*grads)


def _scatter_to_owners(chip_sums, small):
    n = len(chip_sums)

    def body(*refs):
        ins, small_in = refs[:n], refs[n]
        outs, small_out = refs[n + 1:2 * n + 1], refs[2 * n + 1]
        send_sems, recv_sems, local_sem, ssend, srecv = refs[2 * n + 2:]
        x, y, c, others = _place()
        me = 2 * x + y
        dev = 2 * me + c
        local = pltpu.make_async_copy(small_in, small_out.at[dev], local_sem)
        local.start()
        sends = []
        for k in range(n):
            for j, (px, py) in enumerate(others):
                cp = pltpu.make_async_remote_copy(
                    src_ref=ins[k].at[2 * px + py], dst_ref=outs[k].at[me], send_sem=send_sems.at[3 * k + j],
                    recv_sem=recv_sems.at[3 * k + j], device_id=(px, py, c), device_id_type=MESH)
                cp.start()
                sends.append(cp)
        rel = [(dx, dy, dc) for dx in (0, 1) for dy in (0, 1) for dc in (0, 1)][1:]
        for r, (dx, dy, dc) in enumerate(rel):
            to = (x ^ dx, y ^ dy, c ^ dc)
            cp = pltpu.make_async_remote_copy(src_ref=small_in, dst_ref=small_out.at[dev], send_sem=ssend.at[r],
                                              recv_sem=srecv.at[r], device_id=to, device_id_type=MESH)
            cp.start()
            sends.append(cp)
        for k in range(n):
            for j, (px, py) in enumerate(others):
                land = outs[k].at[2 * px + py]
                pltpu.make_async_remote_copy(src_ref=land, dst_ref=land, send_sem=send_sems.at[3 * k + j],
                                             recv_sem=recv_sems.at[3 * k + j], device_id=(x, y, c),
                                             device_id_type=MESH).wait_recv()
        for r, (dx, dy, dc) in enumerate(rel):
            land = small_out.at[4 * (x ^ dx) + 2 * (y ^ dy) + (c ^ dc)]
            pltpu.make_async_remote_copy(src_ref=land, dst_ref=land, send_sem=ssend.at[r], recv_sem=srecv.at[r],
                                         device_id=(x, y, c), device_id_type=MESH).wait_recv()
        for cp in sends:
            cp.wait_send()
        local.wait()

    return _pcall(
        body, name="scatter_to_owners", in_specs=[_ANY] * (n + 1), out_specs=[_ANY] * (n + 1),
        out_shape=[jax.ShapeDtypeStruct(g.shape, g.dtype) for g in chip_sums]
        + [jax.ShapeDtypeStruct((N_DEV,) + small.shape, small.dtype)],
        scratch_shapes=[pltpu.SemaphoreType.DMA((3 * n,)), pltpu.SemaphoreType.DMA((3 * n,)),
                        pltpu.SemaphoreType.DMA, pltpu.SemaphoreType.DMA((7,)), pltpu.SemaphoreType.DMA((7,))],
    )(*chip_sums, small)


def _join_halves(fulls):
    n = len(fulls)
    hs = [f.shape[0] // 2 for f in fulls]

    def body(*refs):
        ins, outs = refs[:n], refs[n:2 * n]
        send_sems, recv_sems = refs[2 * n:]
        x, y, c, _ = _place()
        cps = []
        for k in range(n):
            mine = pl.ds(c * hs[k], hs[k])
            cp = pltpu.make_async_remote_copy(src_ref=ins[k].at[mine, :], dst_ref=outs[k].at[mine, :],
                                              send_sem=send_sems.at[k], recv_sem=recv_sems.at[k],
                                              device_id=(x, y, 1 - c), device_id_type=MESH)
            cp.start()
            cps.append(cp)
        for k in range(n):
            land = outs[k].at[pl.ds((1 - c) * hs[k], hs[k]), :]
            pltpu.make_async_remote_copy(src_ref=land, dst_ref=land, send_sem=send_sems.at[k],
                                         recv_sem=recv_sems.at[k], device_id=(x, y, c), device_id_type=MESH).wait_recv()
        for cp in cps:
            cp.wait_send()

    return _pcall(
        body, name="join_halves", in_specs=[_ANY] * n, out_specs=[_ANY] * n,
        out_shape=[jax.ShapeDtypeStruct(f.shape, f.dtype) for f in fulls],
        input_output_aliases={k: k for k in range(n)},
        scratch_shapes=[pltpu.SemaphoreType.DMA((n,)), pltpu.SemaphoreType.DMA((n,))],
    )(*fulls)


def _heads(cfg, a):
    return a.reshape(cfg.T, cfg.RH, RWKV_HEAD_DIM).transpose(1, 0, 2)


def _unheads(cfg, a):
    return a.transpose(1, 0, 2).reshape(cfg.T, cfg.RW)


def _local_step(cfg, x2, target, norm_gain, w_my, fb, mu_g, w0, w2, a0, a2, k_k, k_a, r_k, ln_w, ln_b, wpf, wpr, wout,
                fng):
    T, D, FW, FH, RW, RH, LP, lora = cfg.T, cfg.D, cfg.FW, cfg.FH, cfg.RW, cfg.RH, cfg.LP, cfg.lora
    fb_p = jnp.pad(fb, ((0, 0), (0, LANES - FH)))
    mu = _rwkv_vec_to_my(cfg, mu_g)
    w2p = jnp.pad(w2, ((0, LP - lora), (0, 0)))
    a2p = jnp.pad(a2, ((0, LP - lora), (0, 0)))
    rk = r_k.reshape(1, RW)
    tm = min(1024, T)

    h = _rms_fwd(cfg, x2, norm_gain)
    u = _mm("in_proj", h, w_my, "nn", F32, tm, cfg.tn, 512)
    c_cols = _fox_prep(cfg, u, fb_p)
    c_rows = c_cols[:, :FH].T.reshape(FH, 1, T)
    o, lse = _attn_fwd(cfg, u, c_rows)
    oa = _gate_a_fwd(cfg, o, u)
    prep = _rwkv_prep_fwd(cfg, u, mu, w0, w2p, a0, a2p, k_k, k_a)
    r, lw, kp, v, an, b, zb = prep
    seqs = [_heads(cfg, t) for t in (r, lw, kp, v, an, b)]
    y_h, ckpt = _scan_fwd(cfg, seqs)
    y = _unheads(cfg, y_h)
    ob = _rwkv_post_fwd(cfg, y, r, kp, v, zb, ln_w, ln_b, rk)
    pa = _mm("proj_fox", oa, wpf, "nn", F32, tm, 1024, 512)
    pb = _mm("proj_rwkv", ob, wpr, "nn", F32, tm, 1024, 512)
    m = _merge_fwd(cfg, pa, pb, u)
    mo = _mm("out_proj", m, wout, "nn", F32, tm, 1024, 512)
    loss8, dres, dres16, d_fng = _final(cfg, x2, mo, fng.reshape(1, D), target)

    dm = _mm("out_proj_dx", dres16, wout, "nt", F32, tm, 1024, 512)
    d_wout = _mm("out_proj_dw", m, dres16, "tn", BF16, 1024, 1024, 512)
    dpa, dpb, dgate = _merge_bwd(cfg, pa, pb, u, dm)
    doa = _mm("proj_fox_dx", dpa, wpf, "nt", F32, tm, 1024, 512)
    d_wpf = _mm("proj_fox_dw", oa, dpa, "tn", BF16, 1024, 1024, 512)
    dob = _mm("proj_rwkv_dx", dpb, wpr, "nt", F32, tm, 1024, 512)
    d_wpr = _mm("proj_rwkv_dw", ob, dpb, "tn", BF16, 1024, 1024, 512)

    do, dza = _gate_a_bwd(cfg, o, u, doa)
    dq, dk, dv, dcol = _attn_bwd(cfg, u, c_rows, lse, do)
    dc = jnp.pad(-dcol.reshape(FH, T).T, ((0, 0), (0, LANES - FH)))
    df, d_fb = _fox_prep_bwd(cfg, u, fb_p, dc)

    dy, dr_p, dk_p, dv_p, dzb, d_lnw, d_lnb, d_rk = _rwkv_post_bwd(cfg, y, r, kp, v, zb, ln_w, ln_b, rk, dob)
    dseq = _scan_bwd(cfg, seqs, ckpt, _heads(cfg, dy))
    dr_s, dlw_s, dk_s, dv_s, da_s, db_s = [_unheads(cfg, t) for t in dseq]
    cots = [dr_s + dr_p, dlw_s, dk_s + dk_p, dv_s + dv_p, da_s, db_s]
    dus, d_mu, d_w0, d_w2p, d_a0, d_a2p, d_kk, d_ka = _rwkv_prep_bwd(cfg, u, mu, w0, w2p, a0, a2p, k_k, k_a, cots, dzb)
    du_rwkv = _shift_bwd(cfg, dus, mu)

    pad_f = jnp.zeros((T, cfg.ncol - cfg.o_ad - LP), BF16)
    du = jnp.concatenate([dq, dk.astype(BF16), dv.astype(BF16), dza, du_rwkv[:, :4 * RW], dgate, df,
                          du_rwkv[:, 4 * RW:], pad_f], axis=1)
    dh = _mm("in_proj_dx", du, w_my, "nt", F32, tm, 1024, cfg.tn)
    d_wmy = _mm("in_proj_dw", h, du, "tn", BF16, 1024, cfg.tn, 512)
    gx, d_ng = _rms_bwd(cfg, x2, norm_gain, dh, dres)

    small = dict(norm_gain=d_ng, fox_forget_bias=d_fb[:, :FH], rwkv_shift_mix=_rwkv_vec_from_my(cfg, d_mu),
                 rwkv_w0=d_w0, rwkv_a0=d_a0, rwkv_k_k=d_kk, rwkv_k_a=d_ka, rwkv_r_k=d_rk, rwkv_ln_w=d_lnw,
                 rwkv_ln_b=d_lnb, final_norm_gain=d_fng)
    big = dict(w_in=d_wmy, rwkv_w2=d_w2p[:lora], rwkv_a2=d_a2p[:lora], w_proj_fox=d_wpf, w_proj_rwkv=d_wpr,
               w_out=d_wout)
    return loss8[0, 0], gx, small, big


_SMALL = ["norm_gain", "fox_forget_bias", "rwkv_shift_mix", "rwkv_w0", "rwkv_a0", "rwkv_k_k", "rwkv_k_a", "rwkv_r_k",
          "rwkv_ln_w", "rwkv_ln_b", "final_norm_gain"]
_WEIGHTS = ["norm_gain", "w_in", "fox_forget_bias", "rwkv_shift_mix", "rwkv_w0", "rwkv_w2", "rwkv_a0", "rwkv_a2",
            "rwkv_k_k", "rwkv_k_a", "rwkv_r_k", "rwkv_ln_w", "rwkv_ln_b", "w_proj_fox", "w_proj_rwkv", "w_out",
            "final_norm_gain"]


def _pack_small(arrs):
    parts = []
    for a in arrs:
        f = a.reshape(-1)
        parts.append(jnp.pad(f, (0, (-f.shape[0]) % LANES)))
    flat = jnp.concatenate(parts)
    rows = flat.shape[0] // LANES
    flat = jnp.pad(flat, (0, ((-rows) % 8) * LANES))
    return flat.reshape(-1, LANES)


def _unpack_small(packed, shapes):
    flat = packed.reshape(-1)
    out, pos = [], 0
    for s in shapes:
        n = int(np.prod(s))
        out.append(flat[pos:pos + n].reshape(s))
        pos += n + ((-n) % LANES)
    return out


def _shard_major(a, axis):
    parts = jnp.split(a, N_CHIPS, axis=axis)
    return jnp.stack(parts, axis=0)


def kernel(x, norm_gain, w_in, fox_forget_bias, rwkv_shift_mix, rwkv_w0, rwkv_w2, rwkv_a0, rwkv_a2, rwkv_k_k, rwkv_k_a, rwkv_r_k, rwkv_ln_w, rwkv_ln_b, w_proj_fox, w_proj_rwkv, w_out, final_norm_gain, loss_target, m_norm_gain, m_w_in, m_fox_forget_bias, m_rwkv_shift_mix, m_rwkv_w0, m_rwkv_w2, m_rwkv_a0, m_rwkv_a2, m_rwkv_k_k, m_rwkv_k_a, m_rwkv_r_k, m_rwkv_ln_w, m_rwkv_ln_b, m_w_proj_fox, m_w_proj_rwkv, m_w_out, m_final_norm_gain, v_norm_gain, v_w_in, v_fox_forget_bias, v_rwkv_shift_mix, v_rwkv_w0, v_rwkv_w2, v_rwkv_a0, v_rwkv_a2, v_rwkv_k_k, v_rwkv_k_a, v_rwkv_r_k, v_rwkv_ln_w, v_rwkv_ln_b, v_w_proj_fox, v_w_proj_rwkv, v_w_out, v_final_norm_gain):
    args = dict(locals())
    T, D = x.shape[1], x.shape[2]
    lora = rwkv_w2.shape[1]
    cfg = _Cfg(T, D, lora)
    RW = cfg.RW
    c_idx = lax.axis_index("c").astype(jnp.int32).reshape(1)
    me_chip = (2 * lax.axis_index("x") + lax.axis_index("y")).astype(jnp.int32)
    place = jnp.concatenate([me_chip.reshape(1), c_idx])

    w_in_s = w_in[0]
    wp_s = jnp.concatenate([w_proj_fox[0], w_proj_rwkv[0]], axis=0)
    lora_s = jnp.concatenate([rwkv_w2[0], rwkv_a2[0]], axis=0)
    mine = [_cast_bf16("cast_w_in", w_in_s), _cast_bf16("cast_w_proj", wp_s), _cast_bf16("cast_w_out", w_out[0]),
            lora_s]
    gathered = _gather_weights(mine)
    g_in, g_wp, g_out, g_lora = [lax.dynamic_update_slice(g, own[None], (me_chip, 0, 0))
                                 for g, own in zip(gathered, mine)]
    w_my = _shards_to_my_layout(cfg, g_in)
    wp = g_wp.transpose(1, 0, 2).reshape(2 * RW, D)
    wout = g_out.reshape(D, D)
    lo = g_lora.transpose(1, 0, 2).reshape(2 * lora, RW)

    loss_dev, gx, small, big = _local_step(
        cfg, x[0], loss_target[0], norm_gain, w_my, fox_forget_bias, rwkv_shift_mix, rwkv_w0, lo[:lora], rwkv_a0,
        lo[lora:], rwkv_k_k, rwkv_k_a, rwkv_r_k, rwkv_ln_w, rwkv_ln_b, wp[:RW], wp[RW:], wout, final_norm_gain)
    loss = lax.psum(loss_dev, ("x", "y", "c"))

    gs_in = _my_layout_to_shards(cfg, big["w_in"])
    gs_wp = _shard_major(jnp.concatenate([big["w_proj_fox"], big["w_proj_rwkv"]], axis=0), 1)
    gs_out = _shard_major(big["w_out"], 0)
    gs_lora = _shard_major(jnp.concatenate([big["rwkv_w2"], big["rwkv_a2"]], axis=0).astype(BF16), 1)
    gs = [gs_in, gs_wp, gs_out, gs_lora]
    names = ["w_in", "w_proj", "w_out", "lora"]
    recv1 = _exchange_halves(gs)
    chip_sums = [_add_halves("add_halves_" + nm, g, r, c_idx) for nm, g, r in zip(names, gs, recv1)]
    small_shapes = [args[nm].shape for nm in _SMALL]
    packed = _pack_small([small[nm] for nm in _SMALL])
    *recv2, small_all = _scatter_to_owners(chip_sums, packed)
    reduced = [_sum_chips("sum_chips_" + nm, r, own, place) for nm, r, own in zip(names, recv2, chip_sums)]
    g_small = _sum_slots("sum_small", small_all)
    g_in_f, g_wp_f, g_out_f, g_lora_f = _join_halves(reduced)

    grads = dict(zip(_SMALL, _unpack_small(g_small, small_shapes)))
    grads["w_in"] = g_in_f[None]
    grads["w_proj_fox"] = g_wp_f[None, :RW]
    grads["w_proj_rwkv"] = g_wp_f[None, RW:]
    grads["w_out"] = g_out_f[None]
    grads["rwkv_w2"] = g_lora_f[None, :lora]
    grads["rwkv_a2"] = g_lora_f[None, lora:]

    delta, new_m, new_v = {}, {}, {}
    w_small = _pack_small([args[nm] for nm in _SMALL])
    m_small = _pack_small([args["m_" + nm] for nm in _SMALL])
    v_small = _pack_small([args["v_" + nm] for nm in _SMALL])
    d_s, m_s, v_s = _adamw("adamw_small", w_small, g_small, m_small, v_small)
    for tgt, pk in ((delta, d_s), (new_m, m_s), (new_v, v_s)):
        tgt.update(zip(_SMALL, _unpack_small(pk, small_shapes)))
    for nm in ("w_in", "w_proj_fox", "w_proj_rwkv", "w_out", "rwkv_w2", "rwkv_a2"):
        shp = args[nm].shape
        two_d = (shp[1], shp[2])
        d_b, m_b, v_b = _adamw("adamw_" + nm, args[nm].reshape(two_d), grads[nm].reshape(two_d),
                               args["m_" + nm].reshape(two_d), args["v_" + nm].reshape(two_d))
        delta[nm], new_m[nm], new_v[nm] = d_b.reshape(shp), m_b.reshape(shp), v_b.reshape(shp)

    return (loss, gx[None], *[grads[n] for n in _WEIGHTS], *[delta[n] for n in _WEIGHTS],
            *[new_m[n] for n in _WEIGHTS], *[new_v[n] for n in _WEIGHTS])
```

```python
import functools

import numpy as np
import jax
import jax.numpy as jnp
from jax import lax
from jax.experimental import pallas as pl
from jax.experimental.pallas import tpu as pltpu

F32 = jnp.float32
BF16 = jnp.bfloat16
HI = lax.Precision.HIGHEST
MESH = pl.DeviceIdType.MESH

FOX_HEAD_DIM = 128
RWKV_HEAD_DIM = 64
RMS_EPS = 1e-6
GN_EPS = 64e-5
L2_EPS = 1e-12
ADAM_LR = 0.001
ADAM_B1 = 0.9
ADAM_B2 = 0.999
ADAM_EPS = 1e-08
ADAM_WD = 0.01
ADAM_STEP = 10

LANES = 128
VMEM_LIMIT = 56 * 1024 * 1024
SCAN_CHUNK = 64
SCAN_HEADS_PER_STEP = 4
N_CHIPS = 4
N_DEV = 8

_pcall = pl.pallas_call


def _cparams(sem=None):
    return pltpu.CompilerParams(dimension_semantics=sem, vmem_limit_bytes=VMEM_LIMIT)


def _softplus(x):
    return jnp.maximum(x, 0.0) + jnp.log(1.0 + jnp.exp(-jnp.abs(x)))


def _silu(z):
    return z * jax.nn.sigmoid(z)


def _rmsn(x, g):
    return x * lax.rsqrt(jnp.mean(x * x, axis=-1, keepdims=True) + RMS_EPS) * g


def _dot(a, b, dims="nn", precision=None):
    dn = {"nn": (((1,), (0,)), ((), ())), "nt": (((1,), (1,)), ((), ())), "tn": (((0,), (0,)), ((), ()))}[dims]
    return lax.dot_general(a, b, dn, precision=precision, preferred_element_type=F32)


def _split_bf16(x):
    hi = x.astype(BF16)
    return hi, (x - hi.astype(F32)).astype(BF16)


def _bdot_raw(a, b, ca, cb):
    dn = (((ca,), (cb,)), ((0,), (0,)))
    ah, al = _split_bf16(a)
    bh, bl = _split_bf16(b)
    mm = lambda p, q: lax.dot_general(p, q, dn, preferred_element_type=F32)
    return mm(ah, bh) + (mm(ah, bl) + mm(al, bh))


@functools.partial(jax.custom_vjp, nondiff_argnums=(2, 3))
def _bdot(a, b, ca, cb):
    return _bdot_raw(a, b, ca, cb)


def _bdot_fwd(a, b, ca, cb):
    return _bdot_raw(a, b, ca, cb), (a, b)


def _bdot_bwd(ca, cb, res, g):
    a, b = res
    if (ca, cb) == (2, 1):
        return _bdot(g, b, 2, 2), _bdot(a, g, 1, 1)
    if (ca, cb) == (2, 2):
        return _bdot(g, b, 2, 1), _bdot(g, a, 1, 1)
    assert (ca, cb) == (1, 1)
    return _bdot(b, g, 2, 2), _bdot(a, g, 2, 1)


_bdot.defvjp(_bdot_fwd, _bdot_bwd)


class _Cfg:
    def __init__(self, T, D, lora):
        self.T, self.D, self.lora = T, D, lora
        self.FW = D // 2
        self.FH = self.FW // FOX_HEAD_DIM
        self.RW = D // 2
        self.RH = self.RW // RWKV_HEAD_DIM
        self.LP = -(-lora // LANES) * LANES
        self.o_fox = 0
        self.o_rwkv = 4 * self.FW
        self.o_gate = self.o_rwkv + 4 * self.RW
        self.o_f = self.o_gate + 2 * D
        self.o_wd = self.o_f + LANES
        self.o_ad = self.o_wd + self.LP
        end = self.o_ad + self.LP
        self.tn = 1280 if D >= 2048 else LANES
        self.ncol = -(-end // self.tn) * self.tn
        self.in_cols = 4 * self.FW + self.FH + 4 * self.RW + 2 * lora + 2 * D
        self.scp = -(-(self.in_cols // N_CHIPS) // LANES) * LANES
        self.rseg = 4 * self.RW + 2 * self.LP
        self.C = min(SCAN_CHUNK, T)
        self.tr = min(256, T)
        self.hb = min(SCAN_HEADS_PER_STEP, self.RH)

    def segments(self):
        FW, FH, RW, lo, D = self.FW, self.FH, self.RW, self.lora, self.D
        g_f = 4 * FW
        g_r = g_f + FH
        g_wd = g_r + 4 * RW
        g_ad = g_wd + lo
        g_g = g_ad + lo
        return [(0, 4 * FW, 0), (g_f, FH, self.o_f), (g_r, 4 * RW, self.o_rwkv), (g_wd, lo, self.o_wd),
                (g_ad, lo, self.o_ad), (g_g, 2 * D, self.o_gate)]


def _to_my_layout(cfg, wg):
    R = wg.shape[0]
    segs = sorted(cfg.segments(), key=lambda s: s[2])
    parts, pos = [], 0
    for g0, w, m0 in segs:
        if m0 > pos:
            parts.append(jnp.zeros((R, m0 - pos), wg.dtype))
        parts.append(wg[:, g0:g0 + w])
        pos = m0 + w
    if cfg.ncol > pos:
        parts.append(jnp.zeros((R, cfg.ncol - pos), wg.dtype))
    return jnp.concatenate(parts, axis=1)


def _from_my_layout(cfg, wm):
    segs = sorted(cfg.segments(), key=lambda s: s[0])
    return jnp.concatenate([wm[:, m0:m0 + w] for g0, w, m0 in segs], axis=1)


def _shards_to_my_layout(cfg, g):
    R, sc = g.shape[2], cfg.in_cols // N_CHIPS
    segs = sorted(cfg.segments(), key=lambda s: s[2])
    parts, pos = [], 0
    for g0, w, m0 in segs:
        if m0 > pos:
            parts.append(jnp.zeros((m0 - pos, R), g.dtype))
        for s in range(N_CHIPS):
            lo, hi = max(g0, s * sc), min(g0 + w, (s + 1) * sc)
            if lo < hi:
                parts.append(g[s, lo - s * sc:hi - s * sc, :])
        pos = m0 + w
    if cfg.ncol > pos:
        parts.append(jnp.zeros((cfg.ncol - pos, R), g.dtype))
    return jnp.concatenate(parts, axis=0)


def _my_layout_to_shards(cfg, wm):
    sc, R = cfg.in_cols // N_CHIPS, wm.shape[1]
    segs = sorted(cfg.segments(), key=lambda s: s[0])
    shards = []
    for s in range(N_CHIPS):
        parts = []
        for g0, w, m0 in segs:
            lo, hi = max(g0, s * sc), min(g0 + w, (s + 1) * sc)
            if lo < hi:
                parts.append(wm[m0 + lo - g0:m0 + hi - g0, :])
        parts.append(jnp.zeros((cfg.scp - sc, R), wm.dtype))
        shards.append(jnp.concatenate(parts, axis=0))
    return jnp.stack(shards, axis=0)


def _rwkv_vec_to_my(cfg, v):
    RW4, lo, LP = 4 * cfg.RW, cfg.lora, cfg.LP
    z = jnp.zeros((1, LP - lo), v.dtype)
    return jnp.concatenate([v[:, :RW4], v[:, RW4:RW4 + lo], z, v[:, RW4 + lo:], z], axis=1)


def _rwkv_vec_from_my(cfg, v):
    RW4, lo, LP = 4 * cfg.RW, cfg.lora, cfg.LP
    return jnp.concatenate([v[:, :RW4], v[:, RW4:RW4 + lo], v[:, RW4 + LP:RW4 + LP + lo]], axis=1)


def _mm(name, a, b, dims, out_dtype, tm, tn, tk):
    (M, K) = a.shape if dims != "tn" else a.shape[::-1]
    N = b.shape[0] if dims == "nt" else b.shape[1]
    tm, tn, tk = min(tm, M), min(tn, N), min(tk, K)
    assert M % tm == 0 and N % tn == 0 and K % tk == 0, (name, M, N, K, tm, tn, tk)
    nk = K // tk
    if dims == "nn":
        a_spec = pl.BlockSpec((tm, tk), lambda i, j, k: (i, k))
        b_spec = pl.BlockSpec((tk, tn), lambda i, j, k: (k, j))
    elif dims == "nt":
        a_spec = pl.BlockSpec((tm, tk), lambda i, j, k: (i, k))
        b_spec = pl.BlockSpec((tn, tk), lambda i, j, k: (j, k))
    else:
        a_spec = pl.BlockSpec((tk, tm), lambda i, j, k: (k, i))
        b_spec = pl.BlockSpec((tk, tn), lambda i, j, k: (k, j))

    def body(a_ref, b_ref, o_ref, acc_ref):
        k = pl.program_id(2)

        @pl.when(k == 0)
        def _():
            acc_ref[...] = jnp.zeros_like(acc_ref)

        acc_ref[...] += _dot(a_ref[...], b_ref[...], dims)

        @pl.when(k == nk - 1)
        def _():
            o_ref[...] = acc_ref[...].astype(o_ref.dtype)

    return _pcall(
        body, name=name, grid=(M // tm, N // tn, nk),
        in_specs=[a_spec, b_spec], out_specs=pl.BlockSpec((tm, tn), lambda i, j, k: (i, j)),
        out_shape=jax.ShapeDtypeStruct((M, N), out_dtype), scratch_shapes=[pltpu.VMEM((tm, tn), F32)],
        compiler_params=_cparams(("parallel", "parallel", "arbitrary")),
    )(a, b)


def _tile(tr, w, cb=0):
    return pl.BlockSpec((tr, w), lambda i: (i, cb))


def _const(shape):
    nd = len(shape)
    return pl.BlockSpec(shape, lambda i: (0,) * nd)


def _acc_store(i, ref, val):
    @pl.when(i == 0)
    def _():
        ref[...] = val

    @pl.when(i > 0)
    def _():
        ref[...] += val


def _rms_fwd(cfg, x2, g):
    T, D, tr = cfg.T, cfg.D, cfg.tr

    def body(x_ref, g_ref, h_ref):
        h_ref[...] = _rmsn(x_ref[...], g_ref[...]).astype(BF16)

    return _pcall(body, name="rms_fwd", grid=(T // tr,), in_specs=[_tile(tr, D), _const((1, D))],
                  out_specs=_tile(tr, D), out_shape=jax.ShapeDtypeStruct((T, D), BF16),
                  compiler_params=_cparams(("parallel",)))(x2, g)


def _rms_bwd(cfg, x2, g, dh, dres):
    T, D, tr = cfg.T, cfg.D, cfg.tr

    def body(x_ref, g_ref, dh_ref, dres_ref, gx_ref, dg_ref):
        _, vjp = jax.vjp(_rmsn, x_ref[...], g_ref[...])
        dx, dg = vjp(dh_ref[...])
        gx_ref[...] = dx + dres_ref[...]
        _acc_store(pl.program_id(0), dg_ref, dg)

    return _pcall(body, name="rms_bwd", grid=(T // tr,),
                  in_specs=[_tile(tr, D), _const((1, D)), _tile(tr, D), _tile(tr, D)],
                  out_specs=[_tile(tr, D), _const((1, D))],
                  out_shape=[jax.ShapeDtypeStruct((T, D), F32), jax.ShapeDtypeStruct((1, D), F32)],
                  compiler_params=_cparams(("arbitrary",)))(x2, g, dh, dres)


def _final(cfg, x2, mo, fg, target):
    T, D, tr = cfg.T, cfg.D, cfg.tr

    def loss_fn(hres, g, tgt):
        err = _rmsn(hres, g) - tgt
        return 0.5 * jnp.sum(jnp.mean(err * err, axis=-1, keepdims=True), axis=0, keepdims=True)

    def body(x_ref, mo_ref, g_ref, t_ref, loss_ref, dres_ref, dres16_ref, dg_ref):
        hres = x_ref[...] + mo_ref[...]
        loss, vjp = jax.vjp(functools.partial(loss_fn, tgt=t_ref[...]), hres, g_ref[...])
        dres, dg = vjp(jnp.ones((1, 1), F32))
        dres_ref[...] = dres
        dres16_ref[...] = dres.astype(BF16)
        i = pl.program_id(0)
        _acc_store(i, dg_ref, dg)
        _acc_store(i, loss_ref, jnp.broadcast_to(loss, (8, LANES)))

    return _pcall(body, name="final_loss", grid=(T // tr,),
                  in_specs=[_tile(tr, D), _tile(tr, D), _const((1, D)), _tile(tr, D)],
                  out_specs=[_const((8, LANES)), _tile(tr, D), _tile(tr, D), _const((1, D))],
                  out_shape=[jax.ShapeDtypeStruct((8, LANES), F32), jax.ShapeDtypeStruct((T, D), F32),
                             jax.ShapeDtypeStruct((T, D), BF16), jax.ShapeDtypeStruct((1, D), F32)],
                  compiler_params=_cparams(("arbitrary",)))(x2, mo, fg, target)


def _merge_fn(pa, pb, ga, gb):
    return jax.nn.sigmoid(ga) * pa + jax.nn.sigmoid(gb) * pb


def _merge_fwd(cfg, pa, pb, u):
    T, D, tr = cfg.T, cfg.D, cfg.tr
    cga, cgb = cfg.o_gate // D, cfg.o_gate // D + 1

    def body(pa_ref, pb_ref, ga_ref, gb_ref, m_ref):
        m_ref[...] = _merge_fn(pa_ref[...], pb_ref[...], ga_ref[...], gb_ref[...]).astype(BF16)

    return _pcall(body, name="merge_fwd", grid=(T // tr,),
                  in_specs=[_tile(tr, D), _tile(tr, D), _tile(tr, D, cga), _tile(tr, D, cgb)],
                  out_specs=_tile(tr, D), out_shape=jax.ShapeDtypeStruct((T, D), BF16),
                  compiler_params=_cparams(("parallel",)))(pa, pb, u, u)


def _merge_bwd(cfg, pa, pb, u, dm):
    T, D, tr = cfg.T, cfg.D, cfg.tr
    cga, cgb = cfg.o_gate // D, cfg.o_gate // D + 1

    def body(pa_ref, pb_ref, ga_ref, gb_ref, dm_ref, dpa_ref, dpb_ref, dg_ref):
        _, vjp = jax.vjp(_merge_fn, pa_ref[...], pb_ref[...], ga_ref[...], gb_ref[...])
        dpa, dpb, dga, dgb = vjp(dm_ref[...])
        dpa_ref[...] = dpa.astype(BF16)
        dpb_ref[...] = dpb.astype(BF16)
        dg_ref[:, :D] = dga.astype(BF16)
        dg_ref[:, D:] = dgb.astype(BF16)

    return _pcall(body, name="merge_bwd", grid=(T // tr,),
                  in_specs=[_tile(tr, D), _tile(tr, D), _tile(tr, D, cga), _tile(tr, D, cgb), _tile(tr, D)],
                  out_specs=[_tile(tr, D), _tile(tr, D), _tile(tr, 2 * D)],
                  out_shape=[jax.ShapeDtypeStruct((T, D), BF16), jax.ShapeDtypeStruct((T, D), BF16),
                             jax.ShapeDtypeStruct((T, 2 * D), BF16)],
                  compiler_params=_cparams(("parallel",)))(pa, pb, u, u, dm)


def _gate_fn(o, z):
    return o * _silu(z)


def _gate_a_fwd(cfg, o, u):
    T, FW, tr = cfg.T, cfg.FW, cfg.tr

    def body(o_ref, z_ref, oa_ref):
        oa_ref[...] = _gate_fn(o_ref[...], z_ref[...]).astype(BF16)

    return _pcall(body, name="gate_a_fwd", grid=(T // tr,), in_specs=[_tile(tr, FW), _tile(tr, FW, 3)],
                  out_specs=_tile(tr, FW), out_shape=jax.ShapeDtypeStruct((T, FW), BF16),
                  compiler_params=_cparams(("parallel",)))(o, u)


def _gate_a_bwd(cfg, o, u, doa):
    T, FW, tr = cfg.T, cfg.FW, cfg.tr

    def body(o_ref, z_ref, doa_ref, do_ref, dz_ref):
        _, vjp = jax.vjp(_gate_fn, o_ref[...], z_ref[...])
        do, dz = vjp(doa_ref[...])
        do_ref[...] = do
        dz_ref[...] = dz.astype(BF16)

    return _pcall(body, name="gate_a_bwd", grid=(T // tr,),
                  in_specs=[_tile(tr, FW), _tile(tr, FW, 3), _tile(tr, FW)],
                  out_specs=[_tile(tr, FW), _tile(tr, FW)],
                  out_shape=[jax.ShapeDtypeStruct((T, FW), F32), jax.ShapeDtypeStruct((T, FW), BF16)],
                  compiler_params=_cparams(("parallel",)))(o, u, doa)


def _fox_prep(cfg, u, fb):
    T, tr = cfg.T, cfg.tr
    cf = cfg.o_f // LANES

    def body(f_ref, fb_ref, c_ref, carry_ref):
        i = pl.program_id(0)

        @pl.when(i == 0)
        def _():
            carry_ref[...] = jnp.zeros_like(carry_ref)

        lf = -_softplus(-(f_ref[...] + fb_ref[...]))
        r = lax.broadcasted_iota(jnp.int32, (tr, tr), 0)
        c = lax.broadcasted_iota(jnp.int32, (tr, tr), 1)
        tri = (r >= c).astype(F32)
        c_ref[...] = _dot(tri, lf, precision=HI) + carry_ref[...]
        carry_ref[...] += jnp.sum(lf, axis=0, keepdims=True)

    return _pcall(body, name="fox_prep", grid=(T // tr,), in_specs=[_tile(tr, LANES, cf), _const((1, LANES))],
                  out_specs=_tile(tr, LANES), out_shape=jax.ShapeDtypeStruct((T, LANES), F32),
                  scratch_shapes=[pltpu.VMEM((1, LANES), F32)], compiler_params=_cparams(("arbitrary",)))(u, fb)


def _fox_prep_bwd(cfg, u, fb, dc):
    T, tr = cfg.T, cfg.tr
    cf = cfg.o_f // LANES
    nb = T // tr

    def body(f_ref, fb_ref, dc_ref, df_ref, dfb_ref, carry_ref):
        i = pl.program_id(0)

        @pl.when(i == 0)
        def _():
            carry_ref[...] = jnp.zeros_like(carry_ref)

        dc = dc_ref[...]
        r = lax.broadcasted_iota(jnp.int32, (tr, tr), 0)
        c = lax.broadcasted_iota(jnp.int32, (tr, tr), 1)
        triu = (r <= c).astype(F32)
        dlf = _dot(triu, dc, precision=HI) + carry_ref[...]
        carry_ref[...] += jnp.sum(dc, axis=0, keepdims=True)
        dz = dlf * jax.nn.sigmoid(-(f_ref[...] + fb_ref[...]))
        df_ref[...] = dz.astype(BF16)
        _acc_store(i, dfb_ref, jnp.sum(dz, axis=0, keepdims=True))

    rev = lambda i: (nb - 1 - i, 0)
    return _pcall(body, name="fox_prep_bwd", grid=(nb,),
                  in_specs=[pl.BlockSpec((tr, LANES), lambda i: (nb - 1 - i, cf)), _const((1, LANES)),
                            pl.BlockSpec((tr, LANES), rev)],
                  out_specs=[pl.BlockSpec((tr, LANES), rev), _const((1, LANES))],
                  out_shape=[jax.ShapeDtypeStruct((T, LANES), BF16), jax.ShapeDtypeStruct((1, LANES), F32)],
                  scratch_shapes=[pltpu.VMEM((1, LANES), F32)], compiler_params=_cparams(("arbitrary",)))(u, fb, dc)


def _attn_logits(q_ref, k_ref, c_ref, i, tq, T):
    s = _dot(q_ref[...].astype(BF16), k_ref[...].astype(BF16), "nt") * (FOX_HEAD_DIM ** -0.5) - c_ref[0]
    row = i * tq + lax.broadcasted_iota(jnp.int32, (tq, T), 0)
    col = lax.broadcasted_iota(jnp.int32, (tq, T), 1)
    return jnp.where(col <= row, s, -1e30)


def _attn_fwd(cfg, u, c_rows):
    T, FW, FH = cfg.T, cfg.FW, cfg.FH
    tq = min(256, T)
    dh = FOX_HEAD_DIM

    def body(q_ref, k_ref, v_ref, c_ref, o_ref, lse_ref):
        s = _attn_logits(q_ref, k_ref, c_ref, pl.program_id(1), tq, T)
        m = jnp.max(s, axis=1, keepdims=True)
        p = jnp.exp(s - m)
        l = jnp.sum(p, axis=1, keepdims=True)
        o_ref[...] = _dot(p.astype(BF16), v_ref[...].astype(BF16)) / l
        lse_ref[0] = m + jnp.log(l)

    return _pcall(
        body, name="fox_attn_fwd", grid=(FH, T // tq),
        in_specs=[pl.BlockSpec((tq, dh), lambda h, i: (i, h)), pl.BlockSpec((T, dh), lambda h, i: (0, FH + h)),
                  pl.BlockSpec((T, dh), lambda h, i: (0, 2 * FH + h)), pl.BlockSpec((1, 1, T), lambda h, i: (h, 0, 0))],
        out_specs=[pl.BlockSpec((tq, dh), lambda h, i: (i, h)), pl.BlockSpec((1, tq, 1), lambda h, i: (h, i, 0))],
        out_shape=[jax.ShapeDtypeStruct((T, FW), F32), jax.ShapeDtypeStruct((FH, T, 1), F32)],
        compiler_params=_cparams(("parallel", "arbitrary")),
    )(u, u, u, c_rows)


def _attn_bwd(cfg, u, c_rows, lse, do):
    T, FW, FH = cfg.T, cfg.FW, cfg.FH
    tq = min(256, T)
    dh = FOX_HEAD_DIM
    scale = dh ** -0.5

    def body(q_ref, k_ref, v_ref, c_ref, lse_ref, do_ref, dq_ref, dk_ref, dv_ref, dcol_ref):
        i = pl.program_id(1)
        s = _attn_logits(q_ref, k_ref, c_ref, i, tq, T)
        p = jnp.exp(s - lse_ref[0])
        do_v = do_ref[...]
        dp = _dot(do_v.astype(BF16), v_ref[...].astype(BF16), "nt")
        delta = jnp.sum(p * dp, axis=1, keepdims=True)
        ds = p * (dp - delta)
        ds16 = ds.astype(BF16)
        dq_ref[...] = (_dot(ds16, k_ref[...].astype(BF16)) * scale).astype(BF16)
        _acc_store(i, dk_ref, _dot(ds16, q_ref[...].astype(BF16), "tn") * scale)
        _acc_store(i, dv_ref, _dot(p.astype(BF16), do_v.astype(BF16), "tn"))
        _acc_store(i, dcol_ref, jnp.sum(ds, axis=0, keepdims=True)[None])

    qspec = pl.BlockSpec((tq, dh), lambda h, i: (i, h))
    return _pcall(
        body, name="fox_attn_bwd", grid=(FH, T // tq),
        in_specs=[qspec, pl.BlockSpec((T, dh), lambda h, i: (0, FH + h)),
                  pl.BlockSpec((T, dh), lambda h, i: (0, 2 * FH + h)), pl.BlockSpec((1, 1, T), lambda h, i: (h, 0, 0)),
                  pl.BlockSpec((1, tq, 1), lambda h, i: (h, i, 0)), qspec],
        out_specs=[qspec, pl.BlockSpec((T, dh), lambda h, i: (0, h)), pl.BlockSpec((T, dh), lambda h, i: (0, h)),
                   pl.BlockSpec((1, 1, T), lambda h, i: (h, 0, 0))],
        out_shape=[jax.ShapeDtypeStruct((T, FW), BF16), jax.ShapeDtypeStruct((T, FW), F32),
                   jax.ShapeDtypeStruct((T, FW), F32), jax.ShapeDtypeStruct((FH, 1, T), F32)],
        compiler_params=_cparams(("parallel", "arbitrary")),
    )(u, u, u, c_rows, lse, do)


def _head_indicators(cfg):
    ind = np.zeros((cfg.RW, LANES), np.float32)
    ind[np.arange(cfg.RW), np.arange(cfg.RW) // RWKV_HEAD_DIM] = 1.0
    pad = np.zeros((1, LANES), np.float32)
    pad[0, cfg.RH:] = 1.0
    return jnp.asarray(ind), jnp.asarray(ind.T.copy()), jnp.asarray(pad)


def _prep_fn(us_r, us_k, us_v, us_wd, us_ad, w0, w2p, a0, a2p, k_k, k_a, ind, ind_t, pad):
    wpre = w0 + _dot(jnp.tanh(us_wd), w2p, precision=HI)
    w = -_softplus(-wpre) - 0.5
    lw = -jnp.exp(w)
    a = jax.nn.sigmoid(a0 + _dot(us_ad, a2p, precision=HI))
    kk = us_k * k_k
    ss = _dot(kk * kk, ind, precision=HI) + pad
    inv = 1.0 / jnp.maximum(jnp.sqrt(ss), L2_EPS)
    kkn = kk * _dot(inv, ind_t, precision=HI)
    kp = us_k * (1.0 + (a - 1.0) * k_a)
    return us_r, lw, kp, us_v, -kkn, kkn * a


def _shifted(u, prev_row, mu, first):
    n = u.shape[0]
    rolled = pltpu.roll(u, 1, 0)
    row = lax.broadcasted_iota(jnp.int32, u.shape, 0)
    p0 = jnp.where(first, jnp.zeros_like(prev_row), prev_row)
    prev = jnp.where(row == 0, jnp.broadcast_to(p0, u.shape), rolled)
    return u + (prev - u) * mu, prev


def _rwkv_specs(cfg, tr):
    RW, LP = cfg.RW, cfg.LP
    base = cfg.o_rwkv // RW
    cols = [(RW, base), (RW, base + 1), (RW, base + 2), (RW, base + 3), (LP, cfg.o_wd // LP), (LP, cfg.o_ad // LP)]
    cur = [pl.BlockSpec((tr, w), (lambda i, cb=cb: (i, cb))) for w, cb in cols]
    prv = [pl.BlockSpec((8, w), (lambda i, cb=cb: (jnp.maximum(i * (tr // 8) - 1, 0), cb))) for w, cb in cols]
    return cols, cur, prv


def _mu_pieces(cfg, mu_ref):
    RW, LP = cfg.RW, cfg.LP
    offs = [0, RW, 2 * RW, 3 * RW, 4 * RW, 4 * RW + LP, 4 * RW + 2 * LP]
    return [mu_ref[:, offs[j]:offs[j + 1]] for j in range(6)]


def _rwkv_prep_fwd(cfg, u, mu, w0, w2p, a0, a2p, k_k, k_a):
    T, RW, LP, tr = cfg.T, cfg.RW, cfg.LP, cfg.tr
    ind, ind_t, pad = _head_indicators(cfg)
    cols, cur, prv = _rwkv_specs(cfg, tr)

    def body(*refs):
        u_refs, p_refs = refs[0:6], refs[6:12]
        mu_ref, w0_ref, w2_ref, a0_ref, a2_ref, kk_ref, ka_ref, ind_ref, indt_ref, pad_ref = refs[12:22]
        outs = refs[22:]
        first = pl.program_id(0) == 0
        mus = _mu_pieces(cfg, mu_ref)
        us = [_shifted(u_refs[j][...], p_refs[j][7:8, :], mus[j], first)[0] for j in range(6)]
        res = _prep_fn(us[0], us[1], us[2], us[4], us[5], w0_ref[...], w2_ref[...], a0_ref[...], a2_ref[...],
                       kk_ref[...], ka_ref[...], ind_ref[...], indt_ref[...], pad_ref[...])
        for j in range(6):
            outs[j][...] = res[j]
        outs[6][...] = us[3]

    consts = [mu, w0, w2p, a0, a2p, k_k, k_a, ind, ind_t, pad]
    return _pcall(body, name="rwkv_prep_fwd", grid=(T // tr,),
                  in_specs=cur + prv + [_const(c.shape) for c in consts],
                  out_specs=[_tile(tr, RW)] * 7, out_shape=[jax.ShapeDtypeStruct((T, RW), F32)] * 7,
                  compiler_params=_cparams(("parallel",)))(*([u] * 12), *consts)


def _rwkv_prep_bwd(cfg, u, mu, w0, w2p, a0, a2p, k_k, k_a, cots, dzb):
    T, RW, LP = cfg.T, cfg.RW, cfg.LP
    tr = min(128, T)
    ind, ind_t, pad = _head_indicators(cfg)
    cols, cur, prv = _rwkv_specs(cfg, tr)
    rseg = cfg.rseg

    def body(*refs):
        u_refs, p_refs = refs[0:6], refs[6:12]
        mu_ref, w0_ref, w2_ref, a0_ref, a2_ref, kk_ref, ka_ref, ind_ref, indt_ref, pad_ref = refs[12:22]
        cot_refs, dzb_ref = refs[22:28], refs[28]
        dus_ref, dmu_ref, dw0_ref, dw2_ref, da0_ref, da2_ref, dkk_ref, dka_ref = refs[29:]
        i = pl.program_id(0)
        first = i == 0
        mus = _mu_pieces(cfg, mu_ref)
        sh = [_shifted(u_refs[j][...], p_refs[j][7:8, :], mus[j], first) for j in range(6)]
        us = [s[0] for s in sh]
        fn = functools.partial(_prep_fn, ind=ind_ref[...], ind_t=indt_ref[...], pad=pad_ref[...])
        _, vjp = jax.vjp(fn, us[0], us[1], us[2], us[4], us[5], w0_ref[...], w2_ref[...], a0_ref[...], a2_ref[...],
                         kk_ref[...], ka_ref[...])
        d = vjp(tuple(c[...] for c in cot_refs))
        dus = [d[0], d[1], d[2], dzb_ref[...], d[3], d[4]]
        offs = [0, RW, 2 * RW, 3 * RW, 4 * RW, 4 * RW + LP, 4 * RW + 2 * LP]
        for j in range(6):
            dus_ref[:, offs[j]:offs[j + 1]] = dus[j]
            dmu_j = jnp.sum(dus[j] * (sh[j][1] - u_refs[j][...]), axis=0, keepdims=True)

            @pl.when(first)
            def _(j=j, dmu_j=dmu_j):
                dmu_ref[:, offs[j]:offs[j + 1]] = dmu_j

            @pl.when(i > 0)
            def _(j=j, dmu_j=dmu_j):
                dmu_ref[:, offs[j]:offs[j + 1]] += dmu_j
        for ref, val in zip((dw0_ref, dw2_ref, da0_ref, da2_ref, dkk_ref, dka_ref), d[5:11]):
            _acc_store(i, ref, val)

    consts = [mu, w0, w2p, a0, a2p, k_k, k_a, ind, ind_t, pad]
    vec = jax.ShapeDtypeStruct((1, RW), F32)
    mat = jax.ShapeDtypeStruct((LP, RW), F32)
    return _pcall(body, name="rwkv_prep_bwd", grid=(T // tr,),
                  in_specs=cur + prv + [_const(c.shape) for c in consts] + [_tile(tr, RW)] * 7,
                  out_specs=[_tile(tr, rseg), _const((1, rseg)), _const((1, RW)), _const((LP, RW)), _const((1, RW)),
                             _const((LP, RW)), _const((1, RW)), _const((1, RW))],
                  out_shape=[jax.ShapeDtypeStruct((T, rseg), F32), jax.ShapeDtypeStruct((1, rseg), F32),
                             vec, mat, vec, mat, vec, vec],
                  compiler_params=_cparams(("arbitrary",)))(*([u] * 12), *consts, *cots, dzb)


def _shift_bwd(cfg, dus, mu):
    T, tr, rseg = cfg.T, cfg.tr, cfg.rseg
    nb = T // tr

    def body(d_ref, n_ref, mu_ref, du_ref):
        d = d_ref[...]
        rolled = pltpu.roll(d, tr - 1, 0)
        row = lax.broadcasted_iota(jnp.int32, d.shape, 0)
        n0 = jnp.where(pl.program_id(0) == nb - 1, jnp.zeros_like(n_ref[0:1, :]), n_ref[0:1, :])
        nxt = jnp.where(row == tr - 1, jnp.broadcast_to(n0, d.shape), rolled)
        mu_v = mu_ref[...]
        du_ref[...] = (d * (1.0 - mu_v) + nxt * mu_v).astype(BF16)

    return _pcall(body, name="shift_bwd", grid=(nb,),
                  in_specs=[_tile(tr, rseg),
                            pl.BlockSpec((8, rseg), lambda i: (jnp.minimum((i + 1) * (tr // 8), T // 8 - 1), 0)),
                            _const((1, rseg))],
                  out_specs=_tile(tr, rseg), out_shape=jax.ShapeDtypeStruct((T, rseg), BF16),
                  compiler_params=_cparams(("parallel",)))(dus, dus, mu)


def _chunk_fn(S0, r, lw, k, v, a, b):
    H, C, K = r.shape
    row = lax.broadcasted_iota(jnp.int32, (C, C), 0)
    col = lax.broadcasted_iota(jnp.int32, (C, C), 1)
    incl = jnp.broadcast_to((row >= col).astype(F32)[None], (H, C, C))
    strict = (row > col)[None]
    lower = (row >= col)[None]
    L = _bdot(incl, lw, 2, 1)
    LC = jnp.sum(lw, axis=1, keepdims=True)
    eL = jnp.exp(L)
    eLn = jnp.exp(-L)
    at = a * jnp.exp(L - lw)
    rt = r * eL
    bt = b * eLn
    kt = k * eLn
    eR = jnp.exp(LC - L)
    zero = jnp.zeros((), F32)
    n_ab = jnp.where(strict, _bdot(at, bt, 2, 2), zero)
    n_ak = jnp.where(strict, _bdot(at, kt, 2, 2), zero)
    m_rb = jnp.where(lower, _bdot(rt, bt, 2, 2), zero)
    m_rk = jnp.where(lower, _bdot(rt, kt, 2, 2), zero)
    U = _bdot(at, S0, 2, 2) + _bdot(n_ak, v, 2, 1)
    M = n_ab
    steps = max(1, int(np.ceil(np.log2(C))))
    for s in range(steps):
        U = U + _bdot(M, U, 2, 1)
        if s + 1 < steps:
            M = _bdot(M, M, 2, 1)
    Y = _bdot(rt, S0, 2, 2) + _bdot(m_rb, U, 2, 1) + _bdot(m_rk, v, 2, 1)
    S1 = S0 * jnp.exp(LC) + _bdot(U, b * eR, 1, 1) + _bdot(v, k * eR, 1, 1)
    return Y, S1


def _scan_fwd(cfg, seqs):
    T, RH, C = cfg.T, cfg.RH, cfg.C
    N = RWKV_HEAD_DIM
    HB = cfg.hb
    nc = T // C

    def body(r_ref, lw_ref, k_ref, v_ref, a_ref, b_ref, y_ref, ck_ref, s_ref):
        @pl.when(pl.program_id(1) == 0)
        def _():
            s_ref[...] = jnp.zeros_like(s_ref)

        S0 = s_ref[...]
        ck_ref[:, 0] = S0
        Y, S1 = _chunk_fn(S0, r_ref[...], lw_ref[...], k_ref[...], v_ref[...], a_ref[...], b_ref[...])
        y_ref[...] = Y
        s_ref[...] = S1

    seq = pl.BlockSpec((HB, C, N), lambda h, j: (h, j, 0))
    return _pcall(body, name="rwkv_scan_fwd", grid=(RH // HB, nc), in_specs=[seq] * 6,
                  out_specs=[seq, pl.BlockSpec((HB, 1, N, N), lambda h, j: (h, j, 0, 0))],
                  out_shape=[jax.ShapeDtypeStruct((RH, T, N), F32), jax.ShapeDtypeStruct((RH, nc, N, N), F32)],
                  scratch_shapes=[pltpu.VMEM((HB, N, N), F32)],
                  compiler_params=_cparams(("parallel", "arbitrary")))(*seqs)


def _scan_bwd(cfg, seqs, ckpt, dy):
    T, RH, C = cfg.T, cfg.RH, cfg.C
    N = RWKV_HEAD_DIM
    HB = cfg.hb
    nc = T // C

    def body(r_ref, lw_ref, k_ref, v_ref, a_ref, b_ref, ck_ref, dy_ref, *rest):
        outs, ds_ref = rest[:6], rest[6]

        @pl.when(pl.program_id(1) == 0)
        def _():
            ds_ref[...] = jnp.zeros_like(ds_ref)

        _, vjp = jax.vjp(_chunk_fn, ck_ref[:, 0], r_ref[...], lw_ref[...], k_ref[...], v_ref[...], a_ref[...],
                         b_ref[...])
        d = vjp((dy_ref[...], ds_ref[...]))
        ds_ref[...] = d[0]
        for j in range(6):
            outs[j][...] = d[1 + j]

    seq = pl.BlockSpec((HB, C, N), lambda h, j: (h, nc - 1 - j, 0))
    return _pcall(body, name="rwkv_scan_bwd", grid=(RH // HB, nc),
                  in_specs=[seq] * 6 + [pl.BlockSpec((HB, 1, N, N), lambda h, j: (h, nc - 1 - j, 0, 0)), seq],
                  out_specs=[seq] * 6, out_shape=[jax.ShapeDtypeStruct((RH, T, N), F32)] * 6,
                  scratch_shapes=[pltpu.VMEM((HB, N, N), F32)],
                  compiler_params=_cparams(("parallel", "arbitrary")))(*seqs, ckpt, dy)


def _post_fn(y, r, kp, v, zb, ln_w, ln_b, rk, ind, ind_t):
    n = float(RWKV_HEAD_DIM)
    mu = _dot(_dot(y, ind, precision=HI) / n, ind_t, precision=HI)
    yc = y - mu
    var = _dot(yc * yc, ind, precision=HI) / n
    rstd = _dot(lax.rsqrt(var + GN_EPS), ind_t, precision=HI)
    yn = yc * rstd * ln_w + ln_b
    bonus = _dot(_dot(r * kp * rk, ind, precision=HI), ind_t, precision=HI) * v
    return (yn + bonus) * _silu(zb)


def _rwkv_post_fwd(cfg, y, r, kp, v, zb, ln_w, ln_b, rk):
    T, RW, tr = cfg.T, cfg.RW, cfg.tr
    ind, ind_t, _ = _head_indicators(cfg)

    def body(y_ref, r_ref, k_ref, v_ref, z_ref, lw_ref, lb_ref, rk_ref, ind_ref, indt_ref, ob_ref):
        ob_ref[...] = _post_fn(y_ref[...], r_ref[...], k_ref[...], v_ref[...], z_ref[...], lw_ref[...], lb_ref[...],
                               rk_ref[...], ind_ref[...], indt_ref[...]).astype(BF16)

    consts = [ln_w, ln_b, rk, ind, ind_t]
    return _pcall(body, name="rwkv_post_fwd", grid=(T // tr,),
                  in_specs=[_tile(tr, RW)] * 5 + [_const(c.shape) for c in consts],
                  out_specs=_tile(tr, RW), out_shape=jax.ShapeDtypeStruct((T, RW), BF16),
                  compiler_params=_cparams(("parallel",)))(y, r, kp, v, zb, *consts)


def _rwkv_post_bwd(cfg, y, r, kp, v, zb, ln_w, ln_b, rk, dob):
    T, RW = cfg.T, cfg.RW
    tr = min(128, T)
    ind, ind_t, _ = _head_indicators(cfg)

    def body(y_ref, r_ref, k_ref, v_ref, z_ref, lw_ref, lb_ref, rk_ref, ind_ref, indt_ref, dob_ref,
             dy_ref, dr_ref, dk_ref, dv_ref, dz_ref, dlw_ref, dlb_ref, drk_ref):
        fn = functools.partial(_post_fn, ind=ind_ref[...], ind_t=indt_ref[...])
        _, vjp = jax.vjp(fn, y_ref[...], r_ref[...], k_ref[...], v_ref[...], z_ref[...], lw_ref[...], lb_ref[...],
                         rk_ref[...])
        d = vjp(dob_ref[...])
        for ref, val in zip((dy_ref, dr_ref, dk_ref, dv_ref, dz_ref), d[:5]):
            ref[...] = val
        i = pl.program_id(0)
        for ref, val in zip((dlw_ref, dlb_ref, drk_ref), d[5:8]):
            _acc_store(i, ref, val)

    consts = [ln_w, ln_b, rk, ind, ind_t]
    vec = jax.ShapeDtypeStruct((1, RW), F32)
    return _pcall(body, name="rwkv_post_bwd", grid=(T // tr,),
                  in_specs=[_tile(tr, RW)] * 5 + [_const(c.shape) for c in consts] + [_tile(tr, RW)],
                  out_specs=[_tile(tr, RW)] * 5 + [_const((1, RW))] * 3,
                  out_shape=[jax.ShapeDtypeStruct((T, RW), F32)] * 5 + [vec] * 3,
                  compiler_params=_cparams(("arbitrary",)))(y, r, kp, v, zb, *consts, dob)


def _adamw_math(w, g, m, v):
    m = ADAM_B1 * m + (1.0 - ADAM_B1) * g
    v = ADAM_B2 * v + (1.0 - ADAM_B2) * (g * g)
    m_hat = m / (1.0 - ADAM_B1 ** ADAM_STEP)
    v_hat = v / (1.0 - ADAM_B2 ** ADAM_STEP)
    delta = -ADAM_LR * (m_hat / (jnp.sqrt(v_hat) + ADAM_EPS) + ADAM_WD * w)
    return delta, m, v


def _adamw(name, w, g, m, v, copy_grad=False):
    R, Cc = w.shape
    Rp = -(-R // 8) * 8
    tr = Rp
    for nb in range(1, Rp // 8 + 1):
        if (Rp // 8) % nb == 0 and (Rp // nb) * Cc * 4 <= 2 * 1024 * 1024:
            tr = Rp // nb
            break

    def body(w_ref, g_ref, m_ref, v_ref, d_ref, nm_ref, nv_ref, *g_out):
        g_v = g_ref[...]
        d, nm, nv = _adamw_math(w_ref[...], g_v, m_ref[...], v_ref[...])
        d_ref[...] = d
        nm_ref[...] = nm
        nv_ref[...] = nv
        if copy_grad:
            g_out[0][...] = g_v

    spec = _tile(tr, Cc)
    n_out = 4 if copy_grad else 3
    return _pcall(body, name=name, grid=(Rp // tr,), in_specs=[spec] * 4, out_specs=[spec] * n_out,
                  out_shape=[jax.ShapeDtypeStruct((R, Cc), F32)] * n_out,
                  compiler_params=_cparams(("parallel",)))(w, g, m, v)


def _row_tile(R, Cc, itemsize, budget=2 * 1024 * 1024):
    for cand in (1024, 512, 256, 128, 64, 32, 16):
        if R % cand == 0 and cand * Cc * itemsize <= budget:
            return cand
    return R


def _add_halves(name, gs, r1, c_idx):
    _, R, Cc = gs.shape
    half = R // 2
    tr = _row_tile(half, Cc, 4)
    nb = half // tr

    def body(c_ref, g_ref, r_ref, o_ref):
        o_ref[...] = (g_ref[...].astype(F32) + r_ref[...].astype(F32)).astype(BF16)

    grid_spec = pltpu.PrefetchScalarGridSpec(
        num_scalar_prefetch=1, grid=(N_CHIPS, nb),
        in_specs=[pl.BlockSpec((1, tr, Cc), lambda s, i, c: (s, c[0] * nb + i, 0)),
                  pl.BlockSpec((1, tr, Cc), lambda s, i, c: (s, i, 0))],
        out_specs=pl.BlockSpec((1, tr, Cc), lambda s, i, c: (s, i, 0)))
    return _pcall(body, name=name, grid_spec=grid_spec, out_shape=jax.ShapeDtypeStruct((N_CHIPS, half, Cc), BF16),
                  compiler_params=_cparams(("parallel", "parallel")))(c_idx, gs, r1)


def _sum_slots(name, r2):
    S, R, Cc = r2.shape
    tr = _row_tile(R, Cc, 4 * S // 2 if r2.dtype == BF16 else 4 * S)

    def body(r_ref, o_ref):
        acc = r_ref[0].astype(F32)
        for s in range(1, S):
            acc = acc + r_ref[s].astype(F32)
        o_ref[...] = acc

    return _pcall(body, name=name, grid=(R // tr,), in_specs=[pl.BlockSpec((S, tr, Cc), lambda i: (0, i, 0))],
                  out_specs=_tile(tr, Cc), out_shape=jax.ShapeDtypeStruct((R, Cc), F32),
                  compiler_params=_cparams(("parallel",)))(r2)


def _sum_chips(name, recv, own, place):
    S, H, Cc = recv.shape
    tr = _row_tile(H, Cc, 4, 1024 * 1024)
    nb = H // tr

    def body(p_ref, r_ref, own_ref, o_ref):
        s = pl.program_id(1)
        me = p_ref[0]

        @pl.when(s == 0)
        def _():
            o_ref[...] = jnp.zeros_like(o_ref)

        @pl.when(s == me)
        def _():
            o_ref[...] += own_ref[0].astype(F32)

        @pl.when(s != me)
        def _():
            o_ref[...] += r_ref[0].astype(F32)

    grid_spec = pltpu.PrefetchScalarGridSpec(
        num_scalar_prefetch=1, grid=(nb, S),
        in_specs=[pl.BlockSpec((1, tr, Cc), lambda i, s, p: (jnp.where(s == p[0], (s + 1) % S, s), i, 0)),
                  pl.BlockSpec((1, tr, Cc), lambda i, s, p: (p[0], i, 0))],
        out_specs=pl.BlockSpec((tr, Cc), lambda i, s, p: (p[1] * nb + i, 0)))
    return _pcall(body, name=name, grid_spec=grid_spec, out_shape=jax.ShapeDtypeStruct((2 * H, Cc), F32),
                  compiler_params=_cparams(("parallel", "arbitrary")))(place, recv, own)


def _cast_bf16(name, w):
    R, Cc = w.shape
    tr = _row_tile(R, Cc, 4)

    def body(w_ref, o_ref):
        o_ref[...] = w_ref[...].astype(BF16)

    return _pcall(body, name=name, grid=(R // tr,), in_specs=[_tile(tr, Cc)], out_specs=_tile(tr, Cc),
                  out_shape=jax.ShapeDtypeStruct((R, Cc), BF16), compiler_params=_cparams(("parallel",)))(w)


_ANY = pl.BlockSpec(memory_space=pl.ANY)


def _place():
    x, y, c = lax.axis_index("x"), lax.axis_index("y"), lax.axis_index("c")
    others = [(1 - x, y), (x, 1 - y), (1 - x, 1 - y)]
    return x, y, c, others


def _gather_weights(shards):
    n = len(shards)
    halves = [s.shape[0] // 2 for s in shards]

    def body(*refs):
        ins, outs = refs[:n], refs[n:2 * n]
        send_sems, recv_sems = refs[2 * n:]
        x, y, c, others = _place()
        me = 2 * x + y

        def rows(k, ref, chip, hc):
            return ref.at[chip, pl.ds(hc * halves[k], halves[k]), :]

        def remote(k, j, src, dst, to):
            return pltpu.make_async_remote_copy(src_ref=src, dst_ref=dst, send_sem=send_sems.at[6 * k + j],
                                                recv_sem=recv_sems.at[6 * k + j], device_id=to, device_id_type=MESH)

        first, passed = [], []
        for k in range(n):
            mine = ins[k].at[pl.ds(c * halves[k], halves[k]), :]
            for j, (px, py) in enumerate(others):
                cp = remote(k, j, mine, rows(k, outs[k], me, c), (px, py, c))
                cp.start()
                first.append(cp)
        for k in range(n):
            for j, (px, py) in enumerate(others):
                land = rows(k, outs[k], 2 * px + py, c)
                remote(k, j, land, land, (x, y, c)).wait_recv()
                cp = remote(k, 3 + j, land, land, (x, y, 1 - c))
                cp.start()
                passed.append(cp)
        for k in range(n):
            for j, (px, py) in enumerate(others):
                land = rows(k, outs[k], 2 * px + py, 1 - c)
                remote(k, 3 + j, land, land, (x, y, c)).wait_recv()
        for cp in first + passed:
            cp.wait_send()

    return _pcall(
        body, name="gather_weights", in_specs=[_ANY] * n, out_specs=[_ANY] * n,
        out_shape=[jax.ShapeDtypeStruct((N_CHIPS,) + s.shape, s.dtype) for s in shards],
        scratch_shapes=[pltpu.SemaphoreType.DMA((6 * n,)), pltpu.SemaphoreType.DMA((6 * n,))],
    )(*shards)


def _exchange_halves(grads):
    n = len(grads)
    halves = [g.shape[1] // 2 for g in grads]

    def body(*refs):
        ins, outs = refs[:n], refs[n:2 * n]
        send_sems, recv_sems = refs[2 * n:]
        x, y, c, _ = _place()
        cps = []
        for k in range(n):
            src = ins[k].at[:, pl.ds((1 - c) * halves[k], halves[k]), :]
            cp = pltpu.make_async_remote_copy(src_ref=src, dst_ref=outs[k], send_sem=send_sems.at[k],
                                              recv_sem=recv_sems.at[k], device_id=(x, y, 1 - c), device_id_type=MESH)
            cp.start()
            cps.append(cp)
        for cp in cps:
            cp.wait()

    return _pcall(
        body, name="exchange_halves", in_specs=[_ANY] * n, out_specs=[_ANY] * n,
        out_shape=[jax.ShapeDtypeStruct((N_CHIPS, h) + g.shape[2:], g.dtype) for g, h in zip(grads, halves)],
        scratch_shapes=[pltpu.SemaphoreType.DMA((n,)), pltpu.SemaphoreType.DMA((n,))],
    )(*grads)


def _scatter_to_owners(chip_sums, small):
    n = len(chip_sums)

    def body(*refs):
        ins, small_in = refs[:n], refs[n]
        outs, small_out = refs[n + 1:2 * n + 1], refs[2 * n + 1]
        send_sems, recv_sems, local_sem, ssend, srecv = refs[2 * n + 2:]
        x, y, c, others = _place()
        me = 2 * x + y
        dev = 2 * me + c
        local = pltpu.make_async_copy(small_in, small_out.at[dev], local_sem)
        local.start()
        sends = []
        for k in range(n):
            for j, (px, py) in enumerate(others):
                cp = pltpu.make_async_remote_copy(
                    src_ref=ins[k].at[2 * px + py], dst_ref=outs[k].at[me], send_sem=send_sems.at[3 * k + j],
                    recv_sem=recv_sems.at[3 * k + j], device_id=(px, py, c), device_id_type=MESH)
                cp.start()
                sends.append(cp)
        rel = [(dx, dy, dc) for dx in (0, 1) for dy in (0, 1) for dc in (0, 1)][1:]
        for r, (dx, dy, dc) in enumerate(rel):
            to = (x ^ dx, y ^ dy, c ^ dc)
            cp = pltpu.make_async_remote_copy(src_ref=small_in, dst_ref=small_out.at[dev], send_sem=ssend.at[r],
                                              recv_sem=srecv.at[r], device_id=to, device_id_type=MESH)
            cp.start()
            sends.append(cp)
        for k in range(n):
            for j, (px, py) in enumerate(others):
                land = outs[k].at[2 * px + py]
                pltpu.make_async_remote_copy(src_ref=land, dst_ref=land, send_sem=send_sems.at[3 * k + j],
                                             recv_sem=recv_sems.at[3 * k + j], device_id=(x, y, c),
                                             device_id_type=MESH).wait_recv()
        for r, (dx, dy, dc) in enumerate(rel):
            land = small_out.at[4 * (x ^ dx) + 2 * (y ^ dy) + (c ^ dc)]
            pltpu.make_async_remote_copy(src_ref=land, dst_ref=land, send_sem=ssend.at[r], recv_sem=srecv.at[r],
                                         device_id=(x, y, c), device_id_type=MESH).wait_recv()
        for cp in sends:
            cp.wait_send()
        local.wait()

    return _pcall(
        body, name="scatter_to_owners", in_specs=[_ANY] * (n + 1), out_specs=[_ANY] * (n + 1),
        out_shape=[jax.ShapeDtypeStruct(g.shape, g.dtype) for g in chip_sums]
        + [jax.ShapeDtypeStruct((N_DEV,) + small.shape, small.dtype)],
        scratch_shapes=[pltpu.SemaphoreType.DMA((3 * n,)), pltpu.SemaphoreType.DMA((3 * n,)),
                        pltpu.SemaphoreType.DMA, pltpu.SemaphoreType.DMA((7,)), pltpu.SemaphoreType.DMA((7,))],
    )(*chip_sums, small)


def _join_halves(fulls):
    n = len(fulls)
    hs = [f.shape[0] // 2 for f in fulls]

    def body(*refs):
        ins, outs = refs[:n], refs[n:2 * n]
        send_sems, recv_sems = refs[2 * n:]
        x, y, c, _ = _place()
        cps = []
        for k in range(n):
            mine = pl.ds(c * hs[k], hs[k])
            cp = pltpu.make_async_remote_copy(src_ref=ins[k].at[mine, :], dst_ref=outs[k].at[mine, :],
                                              send_sem=send_sems.at[k], recv_sem=recv_sems.at[k],
                                              device_id=(x, y, 1 - c), device_id_type=MESH)
            cp.start()
            cps.append(cp)
        for k in range(n):
            land = outs[k].at[pl.ds((1 - c) * hs[k], hs[k]), :]
            pltpu.make_async_remote_copy(src_ref=land, dst_ref=land, send_sem=send_sems.at[k],
                                         recv_sem=recv_sems.at[k], device_id=(x, y, c), device_id_type=MESH).wait_recv()
        for cp in cps:
            cp.wait_send()

    return _pcall(
        body, name="join_halves", in_specs=[_ANY] * n, out_specs=[_ANY] * n,
        out_shape=[jax.ShapeDtypeStruct(f.shape, f.dtype) for f in fulls],
        input_output_aliases={k: k for k in range(n)},
        scratch_shapes=[pltpu.SemaphoreType.DMA((n,)), pltpu.SemaphoreType.DMA((n,))],
    )(*fulls)


def _heads(cfg, a):
    return a.reshape(cfg.T, cfg.RH, RWKV_HEAD_DIM).transpose(1, 0, 2)


def _unheads(cfg, a):
    return a.transpose(1, 0, 2).reshape(cfg.T, cfg.RW)


def _local_step(cfg, x2, target, norm_gain, w_my, fb, mu_g, w0, w2, a0, a2, k_k, k_a, r_k, ln_w, ln_b, wpf, wpr, wout,
                fng):
    T, D, FW, FH, RW, RH, LP, lora = cfg.T, cfg.D, cfg.FW, cfg.FH, cfg.RW, cfg.RH, cfg.LP, cfg.lora
    fb_p = jnp.pad(fb, ((0, 0), (0, LANES - FH)))
    mu = _rwkv_vec_to_my(cfg, mu_g)
    w2p = jnp.pad(w2, ((0, LP - lora), (0, 0)))
    a2p = jnp.pad(a2, ((0, LP - lora), (0, 0)))
    rk = r_k.reshape(1, RW)
    tm = min(1024, T)

    h = _rms_fwd(cfg, x2, norm_gain)
    u = _mm("in_proj", h, w_my, "nt", F32, tm, cfg.tn, 512)
    c_cols = _fox_prep(cfg, u, fb_p)
    c_rows = c_cols[:, :FH].T.reshape(FH, 1, T)
    o, lse = _attn_fwd(cfg, u, c_rows)
    oa = _gate_a_fwd(cfg, o, u)
    prep = _rwkv_prep_fwd(cfg, u, mu, w0, w2p, a0, a2p, k_k, k_a)
    r, lw, kp, v, an, b, zb = prep
    seqs = [_heads(cfg, t) for t in (r, lw, kp, v, an, b)]
    y_h, ckpt = _scan_fwd(cfg, seqs)
    y = _unheads(cfg, y_h)
    ob = _rwkv_post_fwd(cfg, y, r, kp, v, zb, ln_w, ln_b, rk)
    pa = _mm("proj_fox", oa, wpf, "nn", F32, tm, 1024, 512)
    pb = _mm("proj_rwkv", ob, wpr, "nn", F32, tm, 1024, 512)
    m = _merge_fwd(cfg, pa, pb, u)
    mo = _mm("out_proj", m, wout, "nn", F32, tm, 1024, 512)
    loss8, dres, dres16, d_fng = _final(cfg, x2, mo, fng.reshape(1, D), target)

    dm = _mm("out_proj_dx", dres16, wout, "nt", F32, tm, 1024, 512)
    d_wout = _mm("out_proj_dw", m, dres16, "tn", BF16, 1024, 1024, 512)
    dpa, dpb, dgate = _merge_bwd(cfg, pa, pb, u, dm)
    doa = _mm("proj_fox_dx", dpa, wpf, "nt", F32, tm, 1024, 512)
    d_wpf = _mm("proj_fox_dw", oa, dpa, "tn", BF16, 1024, 1024, 512)
    dob = _mm("proj_rwkv_dx", dpb, wpr, "nt", F32, tm, 1024, 512)
    d_wpr = _mm("proj_rwkv_dw", ob, dpb, "tn", BF16, 1024, 1024, 512)

    do, dza = _gate_a_bwd(cfg, o, u, doa)
    dq, dk, dv, dcol = _attn_bwd(cfg, u, c_rows, lse, do)
    dc = jnp.pad(-dcol.reshape(FH, T).T, ((0, 0), (0, LANES - FH)))
    df, d_fb = _fox_prep_bwd(cfg, u, fb_p, dc)

    dy, dr_p, dk_p, dv_p, dzb, d_lnw, d_lnb, d_rk = _rwkv_post_bwd(cfg, y, r, kp, v, zb, ln_w, ln_b, rk, dob)
    dseq = _scan_bwd(cfg, seqs, ckpt, _heads(cfg, dy))
    dr_s, dlw_s, dk_s, dv_s, da_s, db_s = [_unheads(cfg, t) for t in dseq]
    cots = [dr_s + dr_p, dlw_s, dk_s + dk_p, dv_s + dv_p, da_s, db_s]
    dus, d_mu, d_w0, d_w2p, d_a0, d_a2p, d_kk, d_ka = _rwkv_prep_bwd(cfg, u, mu, w0, w2p, a0, a2p, k_k, k_a, cots, dzb)
    du_rwkv = _shift_bwd(cfg, dus, mu)

    pad_f = jnp.zeros((T, cfg.ncol - cfg.o_ad - LP), BF16)
    du = jnp.concatenate([dq, dk.astype(BF16), dv.astype(BF16), dza, du_rwkv[:, :4 * RW], dgate, df,
                          du_rwkv[:, 4 * RW:], pad_f], axis=1)
    dh = _mm("in_proj_dx", du, w_my, "nn", F32, tm, 1024, cfg.tn)
    d_wmy = _mm("in_proj_dw", du, h, "tn", BF16, cfg.tn, 1024, 512)
    gx, d_ng = _rms_bwd(cfg, x2, norm_gain, dh, dres)

    small = dict(norm_gain=d_ng, fox_forget_bias=d_fb[:, :FH], rwkv_shift_mix=_rwkv_vec_from_my(cfg, d_mu),
                 rwkv_w0=d_w0, rwkv_a0=d_a0, rwkv_k_k=d_kk, rwkv_k_a=d_ka, rwkv_r_k=d_rk, rwkv_ln_w=d_lnw,
                 rwkv_ln_b=d_lnb, final_norm_gain=d_fng)
    big = dict(w_in=d_wmy, rwkv_w2=d_w2p[:lora], rwkv_a2=d_a2p[:lora], w_proj_fox=d_wpf, w_proj_rwkv=d_wpr,
               w_out=d_wout)
    return loss8[0, 0], gx, small, big


_SMALL = ["norm_gain", "fox_forget_bias", "rwkv_shift_mix", "rwkv_w0", "rwkv_a0", "rwkv_k_k", "rwkv_k_a", "rwkv_r_k",
          "rwkv_ln_w", "rwkv_ln_b", "final_norm_gain"]
_WEIGHTS = ["norm_gain", "w_in", "fox_forget_bias", "rwkv_shift_mix", "rwkv_w0", "rwkv_w2", "rwkv_a0", "rwkv_a2",
            "rwkv_k_k", "rwkv_k_a", "rwkv_r_k", "rwkv_ln_w", "rwkv_ln_b", "w_proj_fox", "w_proj_rwkv", "w_out",
            "final_norm_gain"]


def _pack_small(arrs):
    parts = []
    for a in arrs:
        f = a.reshape(-1)
        parts.append(jnp.pad(f, (0, (-f.shape[0]) % LANES)))
    flat = jnp.concatenate(parts)
    rows = flat.shape[0] // LANES
    flat = jnp.pad(flat, (0, ((-rows) % 8) * LANES))
    return flat.reshape(-1, LANES)


def _unpack_small(packed, shapes):
    flat = packed.reshape(-1)
    out, pos = [], 0
    for s in shapes:
        n = int(np.prod(s))
        out.append(flat[pos:pos + n].reshape(s))
        pos += n + ((-n) % LANES)
    return out


def _shard_major(a, axis):
    parts = jnp.split(a, N_CHIPS, axis=axis)
    return jnp.stack(parts, axis=0)


def kernel(x, norm_gain, w_in, fox_forget_bias, rwkv_shift_mix, rwkv_w0, rwkv_w2, rwkv_a0, rwkv_a2, rwkv_k_k, rwkv_k_a, rwkv_r_k, rwkv_ln_w, rwkv_ln_b, w_proj_fox, w_proj_rwkv, w_out, final_norm_gain, loss_target, m_norm_gain, m_w_in, m_fox_forget_bias, m_rwkv_shift_mix, m_rwkv_w0, m_rwkv_w2, m_rwkv_a0, m_rwkv_a2, m_rwkv_k_k, m_rwkv_k_a, m_rwkv_r_k, m_rwkv_ln_w, m_rwkv_ln_b, m_w_proj_fox, m_w_proj_rwkv, m_w_out, m_final_norm_gain, v_norm_gain, v_w_in, v_fox_forget_bias, v_rwkv_shift_mix, v_rwkv_w0, v_rwkv_w2, v_rwkv_a0, v_rwkv_a2, v_rwkv_k_k, v_rwkv_k_a, v_rwkv_r_k, v_rwkv_ln_w, v_rwkv_ln_b, v_w_proj_fox, v_w_proj_rwkv, v_w_out, v_final_norm_gain):
    args = dict(locals())
    T, D = x.shape[1], x.shape[2]
    lora = rwkv_w2.shape[1]
    cfg = _Cfg(T, D, lora)
    RW = cfg.RW
    c_idx = lax.axis_index("c").astype(jnp.int32).reshape(1)
    me_chip = (2 * lax.axis_index("x") + lax.axis_index("y")).astype(jnp.int32)
    place = jnp.concatenate([me_chip.reshape(1), c_idx])

    sc = cfg.in_cols // N_CHIPS
    w_in_t, m_in_t, v_in_t = w_in[0].T, m_w_in[0].T, v_w_in[0].T
    w_in_s = jnp.pad(w_in_t.astype(BF16), ((0, cfg.scp - sc), (0, 0)))
    wp_s = jnp.concatenate([w_proj_fox[0], w_proj_rwkv[0]], axis=0)
    lora_s = jnp.concatenate([rwkv_w2[0], rwkv_a2[0]], axis=0)
    mine = [w_in_s, _cast_bf16("cast_w_proj", wp_s), _cast_bf16("cast_w_out", w_out[0]), lora_s]
    gathered = _gather_weights(mine)
    g_in, g_wp, g_out, g_lora = [lax.dynamic_update_slice(g, own[None], (me_chip, 0, 0))
                                 for g, own in zip(gathered, mine)]
    w_my = _shards_to_my_layout(cfg, g_in)
    wp = g_wp.transpose(1, 0, 2).reshape(2 * RW, D)
    wout = g_out.reshape(D, D)
    lo = g_lora.transpose(1, 0, 2).reshape(2 * lora, RW)

    loss_dev, gx, small, big = _local_step(
        cfg, x[0], loss_target[0], norm_gain, w_my, fox_forget_bias, rwkv_shift_mix, rwkv_w0, lo[:lora], rwkv_a0,
        lo[lora:], rwkv_k_k, rwkv_k_a, rwkv_r_k, rwkv_ln_w, rwkv_ln_b, wp[:RW], wp[RW:], wout, final_norm_gain)
    loss = lax.psum(loss_dev, ("x", "y", "c"))

    gs_in = _my_layout_to_shards(cfg, big["w_in"])
    gs_wp = _shard_major(jnp.concatenate([big["w_proj_fox"], big["w_proj_rwkv"]], axis=0), 1)
    gs_out = _shard_major(big["w_out"], 0)
    gs_lora = _shard_major(jnp.concatenate([big["rwkv_w2"], big["rwkv_a2"]], axis=0).astype(BF16), 1)
    gs = [gs_in, gs_wp, gs_out, gs_lora]
    names = ["w_in", "w_proj", "w_out", "lora"]
    recv1 = _exchange_halves(gs)
    chip_sums = [_add_halves("add_halves_" + nm, g, r, c_idx) for nm, g, r in zip(names, gs, recv1)]
    small_shapes = [args[nm].shape for nm in _SMALL]
    packed = _pack_small([small[nm] for nm in _SMALL])
    *recv2, small_all = _scatter_to_owners(chip_sums, packed)
    reduced = [_sum_chips("sum_chips_" + nm, r, own, place) for nm, r, own in zip(names, recv2, chip_sums)]
    g_small = _sum_slots("sum_small", small_all)
    g_in_f, g_wp_f, g_out_f, g_lora_f = _join_halves(reduced)

    grads = dict(zip(_SMALL, _unpack_small(g_small, small_shapes)))
    grads["w_proj_fox"] = g_wp_f[None, :RW]
    grads["w_proj_rwkv"] = g_wp_f[None, RW:]
    grads["w_out"] = g_out_f[None]
    grads["rwkv_w2"] = g_lora_f[None, :lora]
    grads["rwkv_a2"] = g_lora_f[None, lora:]

    delta, new_m, new_v = {}, {}, {}
    w_small = _pack_small([args[nm] for nm in _SMALL])
    m_small = _pack_small([args["m_" + nm] for nm in _SMALL])
    v_small = _pack_small([args["v_" + nm] for nm in _SMALL])
    d_s, m_s, v_s = _adamw("adamw_small", w_small, g_small, m_small, v_small)
    for tgt, pk in ((delta, d_s), (new_m, m_s), (new_v, v_s)):
        tgt.update(zip(_SMALL, _unpack_small(pk, small_shapes)))
    d_t, m_t, v_t, g_t = _adamw("adamw_w_in", w_in_t, g_in_f, m_in_t, v_in_t, copy_grad=True)
    grads["w_in"], delta["w_in"], new_m["w_in"], new_v["w_in"] = [t.T[None] for t in (g_t, d_t, m_t, v_t)]
    for nm in ("w_proj_fox", "w_proj_rwkv", "w_out", "rwkv_w2", "rwkv_a2"):
        shp = args[nm].shape
        two_d = (shp[1], shp[2])
        d_b, m_b, v_b = _adamw("adamw_" + nm, args[nm].reshape(two_d), grads[nm].reshape(two_d),
                               args["m_" + nm].reshape(two_d), args["v_" + nm].reshape(two_d))
        delta[nm], new_m[nm], new_v[nm] = d_b.reshape(shp), m_b.reshape(shp), v_b.reshape(shp)

    return (loss, gx[None], *[grads[n] for n in _WEIGHTS], *[delta[n] for n in _WEIGHTS],
            *[new_m[n] for n in _WEIGHTS], *[new_v[n] for n in _WEIGHTS])
```

```python
import functools

import numpy as np
import jax
import jax.numpy as jnp
from jax import lax
from jax.experimental import pallas as pl
from jax.experimental.pallas import tpu as pltpu

F32 = jnp.float32
BF16 = jnp.bfloat16
HI = lax.Precision.HIGHEST
MESH = pl.DeviceIdType.MESH

FOX_HEAD_DIM = 128
RWKV_HEAD_DIM = 64
RMS_EPS = 1e-6
GN_EPS = 64e-5
L2_EPS = 1e-12
ADAM_LR = 0.001
ADAM_B1 = 0.9
ADAM_B2 = 0.999
ADAM_EPS = 1e-08
ADAM_WD = 0.01
ADAM_STEP = 10

LANES = 128
VMEM_LIMIT = 56 * 1024 * 1024
SCAN_CHUNK = 64
SCAN_HEADS_PER_STEP = 4
N_CHIPS = 4
N_DEV = 8

_pcall = pl.pallas_call


def _cparams(sem=None):
    return pltpu.CompilerParams(dimension_semantics=sem, vmem_limit_bytes=VMEM_LIMIT)


def _softplus(x):
    return jnp.maximum(x, 0.0) + jnp.log(1.0 + jnp.exp(-jnp.abs(x)))


def _silu(z):
    return z * jax.nn.sigmoid(z)


def _rmsn(x, g):
    return x * lax.rsqrt(jnp.mean(x * x, axis=-1, keepdims=True) + RMS_EPS) * g


def _dot(a, b, dims="nn", precision=None):
    dn = {"nn": (((1,), (0,)), ((), ())), "nt": (((1,), (1,)), ((), ())), "tn": (((0,), (0,)), ((), ()))}[dims]
    return lax.dot_general(a, b, dn, precision=precision, preferred_element_type=F32)


def _split_bf16(x):
    hi = x.astype(BF16)
    return hi, (x - hi.astype(F32)).astype(BF16)


def _bdot_raw(a, b, ca, cb):
    dn = (((ca,), (cb,)), ((0,), (0,)))
    ah, al = _split_bf16(a)
    bh, bl = _split_bf16(b)
    mm = lambda p, q: lax.dot_general(p, q, dn, preferred_element_type=F32)
    return mm(ah, bh) + (mm(ah, bl) + mm(al, bh))


@functools.partial(jax.custom_vjp, nondiff_argnums=(2, 3))
def _bdot(a, b, ca, cb):
    return _bdot_raw(a, b, ca, cb)


def _bdot_fwd(a, b, ca, cb):
    return _bdot_raw(a, b, ca, cb), (a, b)


def _bdot_bwd(ca, cb, res, g):
    a, b = res
    if (ca, cb) == (2, 1):
        return _bdot(g, b, 2, 2), _bdot(a, g, 1, 1)
    if (ca, cb) == (2, 2):
        return _bdot(g, b, 2, 1), _bdot(g, a, 1, 1)
    assert (ca, cb) == (1, 1)
    return _bdot(b, g, 2, 2), _bdot(a, g, 2, 1)


_bdot.defvjp(_bdot_fwd, _bdot_bwd)


def _dot3(a, b):
    return _bdot(a[None], b[None], 2, 1)[0]


@jax.custom_vjp
def _xdot(x, m, mt):
    hi, lo = _split_bf16(x)
    m16 = m.astype(BF16)
    return _dot(hi, m16) + _dot(lo, m16)


def _xdot_fwd(x, m, mt):
    return _xdot(x, m, mt), (m, mt)


def _xdot_bwd(res, g):
    m, mt = res
    return _xdot(g, mt, m), jnp.zeros_like(m), jnp.zeros_like(mt)


_xdot.defvjp(_xdot_fwd, _xdot_bwd)


class _Cfg:
    def __init__(self, T, D, lora):
        self.T, self.D, self.lora = T, D, lora
        self.FW = D // 2
        self.FH = self.FW // FOX_HEAD_DIM
        self.RW = D // 2
        self.RH = self.RW // RWKV_HEAD_DIM
        self.LP = -(-lora // LANES) * LANES
        self.o_fox = 0
        self.o_rwkv = 4 * self.FW
        self.o_gate = self.o_rwkv + 4 * self.RW
        self.o_f = self.o_gate + 2 * D
        self.o_wd = self.o_f + LANES
        self.o_ad = self.o_wd + self.LP
        end = self.o_ad + self.LP
        self.tn = 1280 if D >= 2048 else LANES
        self.ncol = -(-end // self.tn) * self.tn
        self.in_cols = 4 * self.FW + self.FH + 4 * self.RW + 2 * lora + 2 * D
        self.scp = -(-(self.in_cols // N_CHIPS) // LANES) * LANES
        self.rseg = 4 * self.RW + 2 * self.LP
        self.C = min(SCAN_CHUNK, T)
        self.tr = min(256, T)
        self.hb = min(SCAN_HEADS_PER_STEP, self.RH)

    def segments(self):
        FW, FH, RW, lo, D = self.FW, self.FH, self.RW, self.lora, self.D
        g_f = 4 * FW
        g_r = g_f + FH
        g_wd = g_r + 4 * RW
        g_ad = g_wd + lo
        g_g = g_ad + lo
        return [(0, 4 * FW, 0), (g_f, FH, self.o_f), (g_r, 4 * RW, self.o_rwkv), (g_wd, lo, self.o_wd),
                (g_ad, lo, self.o_ad), (g_g, 2 * D, self.o_gate)]


def _to_my_layout(cfg, wg):
    R = wg.shape[0]
    segs = sorted(cfg.segments(), key=lambda s: s[2])
    parts, pos = [], 0
    for g0, w, m0 in segs:
        if m0 > pos:
            parts.append(jnp.zeros((R, m0 - pos), wg.dtype))
        parts.append(wg[:, g0:g0 + w])
        pos = m0 + w
    if cfg.ncol > pos:
        parts.append(jnp.zeros((R, cfg.ncol - pos), wg.dtype))
    return jnp.concatenate(parts, axis=1)


def _from_my_layout(cfg, wm):
    segs = sorted(cfg.segments(), key=lambda s: s[0])
    return jnp.concatenate([wm[:, m0:m0 + w] for g0, w, m0 in segs], axis=1)


def _shards_to_my_layout(cfg, g):
    R, sc = g.shape[1], g.shape[2]
    segs = sorted(cfg.segments(), key=lambda s: s[2])
    parts, pos = [], 0
    for g0, w, m0 in segs:
        if m0 > pos:
            parts.append(jnp.zeros((R, m0 - pos), g.dtype))
        for s in range(N_CHIPS):
            lo, hi = max(g0, s * sc), min(g0 + w, (s + 1) * sc)
            if lo < hi:
                parts.append(g[s, :, lo - s * sc:hi - s * sc])
        pos = m0 + w
    if cfg.ncol > pos:
        parts.append(jnp.zeros((R, cfg.ncol - pos), g.dtype))
    return jnp.concatenate(parts, axis=1)


def _my_layout_to_shards(cfg, wm):
    sc, R = cfg.in_cols // N_CHIPS, wm.shape[1]
    segs = sorted(cfg.segments(), key=lambda s: s[0])
    shards = []
    for s in range(N_CHIPS):
        parts = []
        for g0, w, m0 in segs:
            lo, hi = max(g0, s * sc), min(g0 + w, (s + 1) * sc)
            if lo < hi:
                parts.append(wm[m0 + lo - g0:m0 + hi - g0, :])
        parts.append(jnp.zeros((cfg.scp - sc, R), wm.dtype))
        shards.append(jnp.concatenate(parts, axis=0))
    return jnp.stack(shards, axis=0)


def _rwkv_vec_to_my(cfg, v):
    RW4, lo, LP = 4 * cfg.RW, cfg.lora, cfg.LP
    z = jnp.zeros((1, LP - lo), v.dtype)
    return jnp.concatenate([v[:, :RW4], v[:, RW4:RW4 + lo], z, v[:, RW4 + lo:], z], axis=1)


def _rwkv_vec_from_my(cfg, v):
    RW4, lo, LP = 4 * cfg.RW, cfg.lora, cfg.LP
    return jnp.concatenate([v[:, :RW4], v[:, RW4:RW4 + lo], v[:, RW4 + LP:RW4 + LP + lo]], axis=1)


def _mm(name, a, b, dims, out_dtype, tm, tn, tk):
    (M, K) = a.shape if dims != "tn" else a.shape[::-1]
    N = b.shape[0] if dims == "nt" else b.shape[1]
    tm, tn, tk = min(tm, M), min(tn, N), min(tk, K)
    assert M % tm == 0 and N % tn == 0 and K % tk == 0, (name, M, N, K, tm, tn, tk)
    nk = K // tk
    if dims == "nn":
        a_spec = pl.BlockSpec((tm, tk), lambda i, j, k: (i, k))
        b_spec = pl.BlockSpec((tk, tn), lambda i, j, k: (k, j))
    elif dims == "nt":
        a_spec = pl.BlockSpec((tm, tk), lambda i, j, k: (i, k))
        b_spec = pl.BlockSpec((tn, tk), lambda i, j, k: (j, k))
    else:
        a_spec = pl.BlockSpec((tk, tm), lambda i, j, k: (k, i))
        b_spec = pl.BlockSpec((tk, tn), lambda i, j, k: (k, j))

    def body(a_ref, b_ref, o_ref, acc_ref):
        k = pl.program_id(2)

        @pl.when(k == 0)
        def _():
            acc_ref[...] = jnp.zeros_like(acc_ref)

        acc_ref[...] += _dot(a_ref[...], b_ref[...], dims)

        @pl.when(k == nk - 1)
        def _():
            o_ref[...] = acc_ref[...].astype(o_ref.dtype)

    return _pcall(
        body, name=name, grid=(M // tm, N // tn, nk),
        in_specs=[a_spec, b_spec], out_specs=pl.BlockSpec((tm, tn), lambda i, j, k: (i, j)),
        out_shape=jax.ShapeDtypeStruct((M, N), out_dtype), scratch_shapes=[pltpu.VMEM((tm, tn), F32)],
        compiler_params=_cparams(("parallel", "parallel", "arbitrary")),
    )(a, b)


def _tile(tr, w, cb=0):
    return pl.BlockSpec((tr, w), lambda i: (i, cb))


def _const(shape):
    nd = len(shape)
    return pl.BlockSpec(shape, lambda i: (0,) * nd)


def _acc_store(i, ref, val):
    @pl.when(i == 0)
    def _():
        ref[...] = val

    @pl.when(i > 0)
    def _():
        ref[...] += val


def _rms_fwd(cfg, x2, g):
    T, D, tr = cfg.T, cfg.D, cfg.tr

    def body(x_ref, g_ref, h_ref):
        h_ref[...] = _rmsn(x_ref[...], g_ref[...]).astype(BF16)

    return _pcall(body, name="rms_fwd", grid=(T // tr,), in_specs=[_tile(tr, D), _const((1, D))],
                  out_specs=_tile(tr, D), out_shape=jax.ShapeDtypeStruct((T, D), BF16),
                  compiler_params=_cparams(("parallel",)))(x2, g)


def _rms_bwd(cfg, x2, g, dh, dres):
    T, D, tr = cfg.T, cfg.D, cfg.tr

    def body(x_ref, g_ref, dh_ref, dres_ref, gx_ref, dg_ref):
        _, vjp = jax.vjp(_rmsn, x_ref[...], g_ref[...])
        dx, dg = vjp(dh_ref[...])
        gx_ref[...] = dx + dres_ref[...]
        _acc_store(pl.program_id(0), dg_ref, dg)

    return _pcall(body, name="rms_bwd", grid=(T // tr,),
                  in_specs=[_tile(tr, D), _const((1, D)), _tile(tr, D), _tile(tr, D)],
                  out_specs=[_tile(tr, D), _const((1, D))],
                  out_shape=[jax.ShapeDtypeStruct((T, D), F32), jax.ShapeDtypeStruct((1, D), F32)],
                  compiler_params=_cparams(("arbitrary",)))(x2, g, dh, dres)


def _final(cfg, x2, mo, fg, target):
    T, D, tr = cfg.T, cfg.D, cfg.tr

    def loss_fn(hres, g, tgt):
        err = _rmsn(hres, g) - tgt
        return 0.5 * jnp.sum(jnp.mean(err * err, axis=-1, keepdims=True), axis=0, keepdims=True)

    def body(x_ref, mo_ref, g_ref, t_ref, loss_ref, dres_ref, dres16_ref, dg_ref):
        hres = x_ref[...] + mo_ref[...]
        loss, vjp = jax.vjp(functools.partial(loss_fn, tgt=t_ref[...]), hres, g_ref[...])
        dres, dg = vjp(jnp.ones((1, 1), F32))
        dres_ref[...] = dres
        dres16_ref[...] = dres.astype(BF16)
        i = pl.program_id(0)
        _acc_store(i, dg_ref, dg)
        _acc_store(i, loss_ref, jnp.broadcast_to(loss, (8, LANES)))

    return _pcall(body, name="final_loss", grid=(T // tr,),
                  in_specs=[_tile(tr, D), _tile(tr, D), _const((1, D)), _tile(tr, D)],
                  out_specs=[_const((8, LANES)), _tile(tr, D), _tile(tr, D), _const((1, D))],
                  out_shape=[jax.ShapeDtypeStruct((8, LANES), F32), jax.ShapeDtypeStruct((T, D), F32),
                             jax.ShapeDtypeStruct((T, D), BF16), jax.ShapeDtypeStruct((1, D), F32)],
                  compiler_params=_cparams(("arbitrary",)))(x2, mo, fg, target)


def _merge_fn(pa, pb, ga, gb):
    return jax.nn.sigmoid(ga) * pa + jax.nn.sigmoid(gb) * pb


def _merge_fwd(cfg, pa, pb, u):
    T, D, tr = cfg.T, cfg.D, cfg.tr
    cga, cgb = cfg.o_gate // D, cfg.o_gate // D + 1

    def body(pa_ref, pb_ref, ga_ref, gb_ref, m_ref):
        m_ref[...] = _merge_fn(pa_ref[...], pb_ref[...], ga_ref[...], gb_ref[...]).astype(BF16)

    return _pcall(body, name="merge_fwd", grid=(T // tr,),
                  in_specs=[_tile(tr, D), _tile(tr, D), _tile(tr, D, cga), _tile(tr, D, cgb)],
                  out_specs=_tile(tr, D), out_shape=jax.ShapeDtypeStruct((T, D), BF16),
                  compiler_params=_cparams(("parallel",)))(pa, pb, u, u)


def _merge_bwd(cfg, pa, pb, u, dm):
    T, D, tr = cfg.T, cfg.D, cfg.tr
    cga, cgb = cfg.o_gate // D, cfg.o_gate // D + 1

    def body(pa_ref, pb_ref, ga_ref, gb_ref, dm_ref, dpa_ref, dpb_ref, dg_ref):
        _, vjp = jax.vjp(_merge_fn, pa_ref[...], pb_ref[...], ga_ref[...], gb_ref[...])
        dpa, dpb, dga, dgb = vjp(dm_ref[...])
        dpa_ref[...] = dpa.astype(BF16)
        dpb_ref[...] = dpb.astype(BF16)
        dg_ref[:, :D] = dga.astype(BF16)
        dg_ref[:, D:] = dgb.astype(BF16)

    return _pcall(body, name="merge_bwd", grid=(T // tr,),
                  in_specs=[_tile(tr, D), _tile(tr, D), _tile(tr, D, cga), _tile(tr, D, cgb), _tile(tr, D)],
                  out_specs=[_tile(tr, D), _tile(tr, D), _tile(tr, 2 * D)],
                  out_shape=[jax.ShapeDtypeStruct((T, D), BF16), jax.ShapeDtypeStruct((T, D), BF16),
                             jax.ShapeDtypeStruct((T, 2 * D), BF16)],
                  compiler_params=_cparams(("parallel",)))(pa, pb, u, u, dm)


def _gate_fn(o, z):
    return o * _silu(z)


def _gate_a_fwd(cfg, o, u):
    T, FW, tr = cfg.T, cfg.FW, cfg.tr

    def body(o_ref, z_ref, oa_ref):
        oa_ref[...] = _gate_fn(o_ref[...], z_ref[...]).astype(BF16)

    return _pcall(body, name="gate_a_fwd", grid=(T // tr,), in_specs=[_tile(tr, FW), _tile(tr, FW, 3)],
                  out_specs=_tile(tr, FW), out_shape=jax.ShapeDtypeStruct((T, FW), BF16),
                  compiler_params=_cparams(("parallel",)))(o, u)


def _gate_a_bwd(cfg, o, u, doa):
    T, FW, tr = cfg.T, cfg.FW, cfg.tr

    def body(o_ref, z_ref, doa_ref, do_ref, dz_ref):
        _, vjp = jax.vjp(_gate_fn, o_ref[...], z_ref[...])
        do, dz = vjp(doa_ref[...])
        do_ref[...] = do
        dz_ref[...] = dz.astype(BF16)

    return _pcall(body, name="gate_a_bwd", grid=(T // tr,),
                  in_specs=[_tile(tr, FW), _tile(tr, FW, 3), _tile(tr, FW)],
                  out_specs=[_tile(tr, FW), _tile(tr, FW)],
                  out_shape=[jax.ShapeDtypeStruct((T, FW), F32), jax.ShapeDtypeStruct((T, FW), BF16)],
                  compiler_params=_cparams(("parallel",)))(o, u, doa)


def _fox_prep(cfg, u, fb):
    T, tr = cfg.T, cfg.tr
    cf = cfg.o_f // LANES

    def body(f_ref, fb_ref, c_ref, carry_ref):
        i = pl.program_id(0)

        @pl.when(i == 0)
        def _():
            carry_ref[...] = jnp.zeros_like(carry_ref)

        lf = -_softplus(-(f_ref[...] + fb_ref[...]))
        r = lax.broadcasted_iota(jnp.int32, (tr, tr), 0)
        c = lax.broadcasted_iota(jnp.int32, (tr, tr), 1)
        tri = (r >= c).astype(F32)
        c_ref[...] = _dot(tri, lf, precision=HI) + carry_ref[...]
        carry_ref[...] += jnp.sum(lf, axis=0, keepdims=True)

    return _pcall(body, name="fox_prep", grid=(T // tr,), in_specs=[_tile(tr, LANES, cf), _const((1, LANES))],
                  out_specs=_tile(tr, LANES), out_shape=jax.ShapeDtypeStruct((T, LANES), F32),
                  scratch_shapes=[pltpu.VMEM((1, LANES), F32)], compiler_params=_cparams(("arbitrary",)))(u, fb)


def _fox_prep_bwd(cfg, u, fb, dc):
    T, tr = cfg.T, cfg.tr
    cf = cfg.o_f // LANES
    nb = T // tr

    def body(f_ref, fb_ref, dc_ref, df_ref, dfb_ref, carry_ref):
        i = pl.program_id(0)

        @pl.when(i == 0)
        def _():
            carry_ref[...] = jnp.zeros_like(carry_ref)

        dc = dc_ref[...]
        r = lax.broadcasted_iota(jnp.int32, (tr, tr), 0)
        c = lax.broadcasted_iota(jnp.int32, (tr, tr), 1)
        triu = (r <= c).astype(F32)
        dlf = _dot(triu, dc, precision=HI) + carry_ref[...]
        carry_ref[...] += jnp.sum(dc, axis=0, keepdims=True)
        dz = dlf * jax.nn.sigmoid(-(f_ref[...] + fb_ref[...]))
        df_ref[...] = dz.astype(BF16)
        _acc_store(i, dfb_ref, jnp.sum(dz, axis=0, keepdims=True))

    rev = lambda i: (nb - 1 - i, 0)
    return _pcall(body, name="fox_prep_bwd", grid=(nb,),
                  in_specs=[pl.BlockSpec((tr, LANES), lambda i: (nb - 1 - i, cf)), _const((1, LANES)),
                            pl.BlockSpec((tr, LANES), rev)],
                  out_specs=[pl.BlockSpec((tr, LANES), rev), _const((1, LANES))],
                  out_shape=[jax.ShapeDtypeStruct((T, LANES), BF16), jax.ShapeDtypeStruct((1, LANES), F32)],
                  scratch_shapes=[pltpu.VMEM((1, LANES), F32)], compiler_params=_cparams(("arbitrary",)))(u, fb, dc)


def _attn_logits(q_ref, k_ref, c_ref, i, tq, te):
    s = _dot(q_ref[...].astype(BF16), k_ref[0:te, :].astype(BF16), "nt") * (FOX_HEAD_DIM ** -0.5) - c_ref[0, :, 0:te]
    row = i * tq + lax.broadcasted_iota(jnp.int32, (tq, te), 0)
    col = lax.broadcasted_iota(jnp.int32, (tq, te), 1)
    return jnp.where(col <= row, s, -1e30)


def _per_query_tile(i, nq, tq, fn):
    for ii in range(nq):
        pl.when(i == ii)(functools.partial(fn, (ii + 1) * tq))


def _attn_fwd(cfg, u, c_rows):
    T, FW, FH = cfg.T, cfg.FW, cfg.FH
    tq = min(256, T)
    dh = FOX_HEAD_DIM

    def body(q_ref, k_ref, v_ref, c_ref, o_ref, lse_ref):
        i = pl.program_id(1)

        def tile(te):
            s = _attn_logits(q_ref, k_ref, c_ref, i, tq, te)
            m = jnp.max(s, axis=1, keepdims=True)
            p = jnp.exp(s - m)
            l = jnp.sum(p, axis=1, keepdims=True)
            o_ref[...] = _dot(p.astype(BF16), v_ref[0:te, :].astype(BF16)) / l
            lse_ref[0] = m + jnp.log(l)

        _per_query_tile(i, T // tq, tq, tile)

    return _pcall(
        body, name="fox_attn_fwd", grid=(FH, T // tq),
        in_specs=[pl.BlockSpec((tq, dh), lambda h, i: (i, h)), pl.BlockSpec((T, dh), lambda h, i: (0, FH + h)),
                  pl.BlockSpec((T, dh), lambda h, i: (0, 2 * FH + h)), pl.BlockSpec((1, 1, T), lambda h, i: (h, 0, 0))],
        out_specs=[pl.BlockSpec((tq, dh), lambda h, i: (i, h)), pl.BlockSpec((1, tq, 1), lambda h, i: (h, i, 0))],
        out_shape=[jax.ShapeDtypeStruct((T, FW), F32), jax.ShapeDtypeStruct((FH, T, 1), F32)],
        compiler_params=_cparams(("parallel", "arbitrary")),
    )(u, u, u, c_rows)


def _attn_bwd(cfg, u, c_rows, lse, do):
    T, FW, FH = cfg.T, cfg.FW, cfg.FH
    tq = min(256, T)
    dh = FOX_HEAD_DIM
    scale = dh ** -0.5

    def body(q_ref, k_ref, v_ref, c_ref, lse_ref, do_ref, dq_ref, dk_ref, dv_ref, dcol_ref):
        i = pl.program_id(1)

        @pl.when(i == 0)
        def _():
            dk_ref[...] = jnp.zeros_like(dk_ref)
            dv_ref[...] = jnp.zeros_like(dv_ref)
            dcol_ref[...] = jnp.zeros_like(dcol_ref)

        def tile(te):
            s = _attn_logits(q_ref, k_ref, c_ref, i, tq, te)
            p = jnp.exp(s - lse_ref[0])
            do_v = do_ref[...]
            dp = _dot(do_v.astype(BF16), v_ref[0:te, :].astype(BF16), "nt")
            delta = jnp.sum(p * dp, axis=1, keepdims=True)
            ds = p * (dp - delta)
            ds16 = ds.astype(BF16)
            dq_ref[...] = (_dot(ds16, k_ref[0:te, :].astype(BF16)) * scale).astype(BF16)
            dk_ref[0:te, :] += _dot(ds16, q_ref[...].astype(BF16), "tn") * scale
            dv_ref[0:te, :] += _dot(p.astype(BF16), do_v.astype(BF16), "tn")
            dcol_ref[0, :, 0:te] += jnp.sum(ds, axis=0, keepdims=True)

        _per_query_tile(i, T // tq, tq, tile)

    qspec = pl.BlockSpec((tq, dh), lambda h, i: (i, h))
    return _pcall(
        body, name="fox_attn_bwd", grid=(FH, T // tq),
        in_specs=[qspec, pl.BlockSpec((T, dh), lambda h, i: (0, FH + h)),
                  pl.BlockSpec((T, dh), lambda h, i: (0, 2 * FH + h)), pl.BlockSpec((1, 1, T), lambda h, i: (h, 0, 0)),
                  pl.BlockSpec((1, tq, 1), lambda h, i: (h, i, 0)), qspec],
        out_specs=[qspec, pl.BlockSpec((T, dh), lambda h, i: (0, h)), pl.BlockSpec((T, dh), lambda h, i: (0, h)),
                   pl.BlockSpec((1, 1, T), lambda h, i: (h, 0, 0))],
        out_shape=[jax.ShapeDtypeStruct((T, FW), BF16), jax.ShapeDtypeStruct((T, FW), F32),
                   jax.ShapeDtypeStruct((T, FW), F32), jax.ShapeDtypeStruct((FH, 1, T), F32)],
        compiler_params=_cparams(("parallel", "arbitrary")),
    )(u, u, u, c_rows, lse, do)


def _head_indicators(cfg):
    ind = np.zeros((cfg.RW, LANES), np.float32)
    ind[np.arange(cfg.RW), np.arange(cfg.RW) // RWKV_HEAD_DIM] = 1.0
    pad = np.zeros((1, LANES), np.float32)
    pad[0, cfg.RH:] = 1.0
    return jnp.asarray(ind), jnp.asarray(ind.T.copy()), jnp.asarray(pad)


def _prep_fn(us_r, us_k, us_v, us_wd, us_ad, w0, w2p, a0, a2p, k_k, k_a, ind, ind_t, pad):
    wpre = w0 + _dot3(jnp.tanh(us_wd), w2p)
    w = -_softplus(-wpre) - 0.5
    lw = -jnp.exp(w)
    a = jax.nn.sigmoid(a0 + _dot3(us_ad, a2p))
    kk = us_k * k_k
    ss = _xdot(kk * kk, ind, ind_t) + pad
    inv = 1.0 / jnp.maximum(jnp.sqrt(ss), L2_EPS)
    kkn = kk * _xdot(inv, ind_t, ind)
    kp = us_k * (1.0 + (a - 1.0) * k_a)
    return us_r, lw, kp, us_v, -kkn, kkn * a


def _shifted(u, prev_row, mu, first):
    n = u.shape[0]
    rolled = pltpu.roll(u, 1, 0)
    row = lax.broadcasted_iota(jnp.int32, u.shape, 0)
    p0 = jnp.where(first, jnp.zeros_like(prev_row), prev_row)
    prev = jnp.where(row == 0, jnp.broadcast_to(p0, u.shape), rolled)
    return u + (prev - u) * mu, prev


def _rwkv_specs(cfg, tr):
    RW, LP = cfg.RW, cfg.LP
    base = cfg.o_rwkv // RW
    cols = [(RW, base), (RW, base + 1), (RW, base + 2), (RW, base + 3), (LP, cfg.o_wd // LP), (LP, cfg.o_ad // LP)]
    cur = [pl.BlockSpec((tr, w), (lambda i, cb=cb: (i, cb))) for w, cb in cols]
    prv = [pl.BlockSpec((8, w), (lambda i, cb=cb: (jnp.maximum(i * (tr // 8) - 1, 0), cb))) for w, cb in cols]
    return cols, cur, prv


def _mu_pieces(cfg, mu_ref):
    RW, LP = cfg.RW, cfg.LP
    offs = [0, RW, 2 * RW, 3 * RW, 4 * RW, 4 * RW + LP, 4 * RW + 2 * LP]
    return [mu_ref[:, offs[j]:offs[j + 1]] for j in range(6)]


def _rwkv_prep_fwd(cfg, u, mu, w0, w2p, a0, a2p, k_k, k_a):
    T, RW, LP, tr = cfg.T, cfg.RW, cfg.LP, cfg.tr
    ind, ind_t, pad = _head_indicators(cfg)
    cols, cur, prv = _rwkv_specs(cfg, tr)

    def body(*refs):
        u_refs, p_refs = refs[0:6], refs[6:12]
        mu_ref, w0_ref, w2_ref, a0_ref, a2_ref, kk_ref, ka_ref, ind_ref, indt_ref, pad_ref = refs[12:22]
        outs = refs[22:]
        first = pl.program_id(0) == 0
        mus = _mu_pieces(cfg, mu_ref)
        us = [_shifted(u_refs[j][...], p_refs[j][7:8, :], mus[j], first)[0] for j in range(6)]
        res = _prep_fn(us[0], us[1], us[2], us[4], us[5], w0_ref[...], w2_ref[...], a0_ref[...], a2_ref[...],
                       kk_ref[...], ka_ref[...], ind_ref[...], indt_ref[...], pad_ref[...])
        for j in range(6):
            outs[j][...] = res[j]
        outs[6][...] = us[3]

    consts = [mu, w0, w2p, a0, a2p, k_k, k_a, ind, ind_t, pad]
    return _pcall(body, name="rwkv_prep_fwd", grid=(T // tr,),
                  in_specs=cur + prv + [_const(c.shape) for c in consts],
                  out_specs=[_tile(tr, RW)] * 7, out_shape=[jax.ShapeDtypeStruct((T, RW), F32)] * 7,
                  compiler_params=_cparams(("parallel",)))(*([u] * 12), *consts)


def _rwkv_prep_bwd(cfg, u, mu, w0, w2p, a0, a2p, k_k, k_a, cots, dzb):
    T, RW, LP = cfg.T, cfg.RW, cfg.LP
    tr = min(128, T)
    ind, ind_t, pad = _head_indicators(cfg)
    cols, cur, prv = _rwkv_specs(cfg, tr)
    rseg = cfg.rseg

    def body(*refs):
        u_refs, p_refs = refs[0:6], refs[6:12]
        mu_ref, w0_ref, w2_ref, a0_ref, a2_ref, kk_ref, ka_ref, ind_ref, indt_ref, pad_ref = refs[12:22]
        cot_refs, dzb_ref = refs[22:28], refs[28]
        dus_ref, dmu_ref, dw0_ref, dw2_ref, da0_ref, da2_ref, dkk_ref, dka_ref = refs[29:]
        i = pl.program_id(0)
        first = i == 0
        mus = _mu_pieces(cfg, mu_ref)
        sh = [_shifted(u_refs[j][...], p_refs[j][7:8, :], mus[j], first) for j in range(6)]
        us = [s[0] for s in sh]
        fn = functools.partial(_prep_fn, ind=ind_ref[...], ind_t=indt_ref[...], pad=pad_ref[...])
        _, vjp = jax.vjp(fn, us[0], us[1], us[2], us[4], us[5], w0_ref[...], w2_ref[...], a0_ref[...], a2_ref[...],
                         kk_ref[...], ka_ref[...])
        d = vjp(tuple(c[...] for c in cot_refs))
        dus = [d[0], d[1], d[2], dzb_ref[...], d[3], d[4]]
        offs = [0, RW, 2 * RW, 3 * RW, 4 * RW, 4 * RW + LP, 4 * RW + 2 * LP]
        for j in range(6):
            dus_ref[:, offs[j]:offs[j + 1]] = dus[j]
            dmu_j = jnp.sum(dus[j] * (sh[j][1] - u_refs[j][...]), axis=0, keepdims=True)

            @pl.when(first)
            def _(j=j, dmu_j=dmu_j):
                dmu_ref[:, offs[j]:offs[j + 1]] = dmu_j

            @pl.when(i > 0)
            def _(j=j, dmu_j=dmu_j):
                dmu_ref[:, offs[j]:offs[j + 1]] += dmu_j
        for ref, val in zip((dw0_ref, dw2_ref, da0_ref, da2_ref, dkk_ref, dka_ref), d[5:11]):
            _acc_store(i, ref, val)

    consts = [mu, w0, w2p, a0, a2p, k_k, k_a, ind, ind_t, pad]
    vec = jax.ShapeDtypeStruct((1, RW), F32)
    mat = jax.ShapeDtypeStruct((LP, RW), F32)
    return _pcall(body, name="rwkv_prep_bwd", grid=(T // tr,),
                  in_specs=cur + prv + [_const(c.shape) for c in consts] + [_tile(tr, RW)] * 7,
                  out_specs=[_tile(tr, rseg), _const((1, rseg)), _const((1, RW)), _const((LP, RW)), _const((1, RW)),
                             _const((LP, RW)), _const((1, RW)), _const((1, RW))],
                  out_shape=[jax.ShapeDtypeStruct((T, rseg), F32), jax.ShapeDtypeStruct((1, rseg), F32),
                             vec, mat, vec, mat, vec, vec],
                  compiler_params=_cparams(("arbitrary",)))(*([u] * 12), *consts, *cots, dzb)


def _shift_bwd(cfg, dus, mu):
    T, tr, rseg = cfg.T, cfg.tr, cfg.rseg
    nb = T // tr

    def body(d_ref, n_ref, mu_ref, du_ref):
        d = d_ref[...]
        rolled = pltpu.roll(d, tr - 1, 0)
        row = lax.broadcasted_iota(jnp.int32, d.shape, 0)
        n0 = jnp.where(pl.program_id(0) == nb - 1, jnp.zeros_like(n_ref[0:1, :]), n_ref[0:1, :])
        nxt = jnp.where(row == tr - 1, jnp.broadcast_to(n0, d.shape), rolled)
        mu_v = mu_ref[...]
        du_ref[...] = (d * (1.0 - mu_v) + nxt * mu_v).astype(BF16)

    return _pcall(body, name="shift_bwd", grid=(nb,),
                  in_specs=[_tile(tr, rseg),
                            pl.BlockSpec((8, rseg), lambda i: (jnp.minimum((i + 1) * (tr // 8), T // 8 - 1), 0)),
                            _const((1, rseg))],
                  out_specs=_tile(tr, rseg), out_shape=jax.ShapeDtypeStruct((T, rseg), BF16),
                  compiler_params=_cparams(("parallel",)))(dus, dus, mu)


def _chunk_fn(S0, r, lw, k, v, a, b):
    H, C, K = r.shape
    row = lax.broadcasted_iota(jnp.int32, (C, C), 0)
    col = lax.broadcasted_iota(jnp.int32, (C, C), 1)
    incl = jnp.broadcast_to((row >= col).astype(F32)[None], (H, C, C))
    strict = (row > col)[None]
    lower = (row >= col)[None]
    L = _bdot(incl, lw, 2, 1)
    LC = jnp.sum(lw, axis=1, keepdims=True)
    eL = jnp.exp(L)
    eLn = jnp.exp(-L)
    at = a * jnp.exp(L - lw)
    rt = r * eL
    bt = b * eLn
    kt = k * eLn
    eR = jnp.exp(LC - L)
    zero = jnp.zeros((), F32)
    n_ab = jnp.where(strict, _bdot(at, bt, 2, 2), zero)
    n_ak = jnp.where(strict, _bdot(at, kt, 2, 2), zero)
    m_rb = jnp.where(lower, _bdot(rt, bt, 2, 2), zero)
    m_rk = jnp.where(lower, _bdot(rt, kt, 2, 2), zero)
    U = _bdot(at, S0, 2, 2) + _bdot(n_ak, v, 2, 1)
    M = n_ab
    steps = max(1, int(np.ceil(np.log2(C))))
    for s in range(steps):
        U = U + _bdot(M, U, 2, 1)
        if s + 1 < steps:
            M = _bdot(M, M, 2, 1)
    Y = _bdot(rt, S0, 2, 2) + _bdot(m_rb, U, 2, 1) + _bdot(m_rk, v, 2, 1)
    S1 = S0 * jnp.exp(LC) + _bdot(U, b * eR, 1, 1) + _bdot(v, k * eR, 1, 1)
    return Y, S1


def _scan_fwd(cfg, seqs):
    T, RH, C = cfg.T, cfg.RH, cfg.C
    N = RWKV_HEAD_DIM
    HB = cfg.hb
    nc = T // C

    def body(r_ref, lw_ref, k_ref, v_ref, a_ref, b_ref, y_ref, ck_ref, s_ref):
        @pl.when(pl.program_id(1) == 0)
        def _():
            s_ref[...] = jnp.zeros_like(s_ref)

        S0 = s_ref[...]
        ck_ref[:, 0] = S0
        Y, S1 = _chunk_fn(S0, r_ref[...], lw_ref[...], k_ref[...], v_ref[...], a_ref[...], b_ref[...])
        y_ref[...] = Y
        s_ref[...] = S1

    seq = pl.BlockSpec((HB, C, N), lambda h, j: (h, j, 0))
    return _pcall(body, name="rwkv_scan_fwd", grid=(RH // HB, nc), in_specs=[seq] * 6,
                  out_specs=[seq, pl.BlockSpec((HB, 1, N, N), lambda h, j: (h, j, 0, 0))],
                  out_shape=[jax.ShapeDtypeStruct((RH, T, N), F32), jax.ShapeDtypeStruct((RH, nc, N, N), F32)],
                  scratch_shapes=[pltpu.VMEM((HB, N, N), F32)],
                  compiler_params=_cparams(("parallel", "arbitrary")))(*seqs)


def _scan_bwd(cfg, seqs, ckpt, dy):
    T, RH, C = cfg.T, cfg.RH, cfg.C
    N = RWKV_HEAD_DIM
    HB = cfg.hb
    nc = T // C

    def body(r_ref, lw_ref, k_ref, v_ref, a_ref, b_ref, ck_ref, dy_ref, *rest):
        outs, ds_ref = rest[:6], rest[6]

        @pl.when(pl.program_id(1) == 0)
        def _():
            ds_ref[...] = jnp.zeros_like(ds_ref)

        _, vjp = jax.vjp(_chunk_fn, ck_ref[:, 0], r_ref[...], lw_ref[...], k_ref[...], v_ref[...], a_ref[...],
                         b_ref[...])
        d = vjp((dy_ref[...], ds_ref[...]))
        ds_ref[...] = d[0]
        for j in range(6):
            outs[j][...] = d[1 + j]

    seq = pl.BlockSpec((HB, C, N), lambda h, j: (h, nc - 1 - j, 0))
    return _pcall(body, name="rwkv_scan_bwd", grid=(RH // HB, nc),
                  in_specs=[seq] * 6 + [pl.BlockSpec((HB, 1, N, N), lambda h, j: (h, nc - 1 - j, 0, 0)), seq],
                  out_specs=[seq] * 6, out_shape=[jax.ShapeDtypeStruct((RH, T, N), F32)] * 6,
                  scratch_shapes=[pltpu.VMEM((HB, N, N), F32)],
                  compiler_params=_cparams(("parallel", "arbitrary")))(*seqs, ckpt, dy)


def _post_fn(y, r, kp, v, zb, ln_w, ln_b, rk, ind, ind_t):
    n = float(RWKV_HEAD_DIM)
    mu = _xdot(_xdot(y, ind, ind_t) / n, ind_t, ind)
    yc = y - mu
    var = _xdot(yc * yc, ind, ind_t) / n
    rstd = _xdot(lax.rsqrt(var + GN_EPS), ind_t, ind)
    yn = yc * rstd * ln_w + ln_b
    bonus = _xdot(_xdot(r * kp * rk, ind, ind_t), ind_t, ind) * v
    return (yn + bonus) * _silu(zb)


def _rwkv_post_fwd(cfg, y, r, kp, v, zb, ln_w, ln_b, rk):
    T, RW, tr = cfg.T, cfg.RW, cfg.tr
    ind, ind_t, _ = _head_indicators(cfg)

    def body(y_ref, r_ref, k_ref, v_ref, z_ref, lw_ref, lb_ref, rk_ref, ind_ref, indt_ref, ob_ref):
        ob_ref[...] = _post_fn(y_ref[...], r_ref[...], k_ref[...], v_ref[...], z_ref[...], lw_ref[...], lb_ref[...],
                               rk_ref[...], ind_ref[...], indt_ref[...]).astype(BF16)

    consts = [ln_w, ln_b, rk, ind, ind_t]
    return _pcall(body, name="rwkv_post_fwd", grid=(T // tr,),
                  in_specs=[_tile(tr, RW)] * 5 + [_const(c.shape) for c in consts],
                  out_specs=_tile(tr, RW), out_shape=jax.ShapeDtypeStruct((T, RW), BF16),
                  compiler_params=_cparams(("parallel",)))(y, r, kp, v, zb, *consts)


def _rwkv_post_bwd(cfg, y, r, kp, v, zb, ln_w, ln_b, rk, dob):
    T, RW = cfg.T, cfg.RW
    tr = min(128, T)
    ind, ind_t, _ = _head_indicators(cfg)

    def body(y_ref, r_ref, k_ref, v_ref, z_ref, lw_ref, lb_ref, rk_ref, ind_ref, indt_ref, dob_ref,
             dy_ref, dr_ref, dk_ref, dv_ref, dz_ref, dlw_ref, dlb_ref, drk_ref):
        fn = functools.partial(_post_fn, ind=ind_ref[...], ind_t=indt_ref[...])
        _, vjp = jax.vjp(fn, y_ref[...], r_ref[...], k_ref[...], v_ref[...], z_ref[...], lw_ref[...], lb_ref[...],
                         rk_ref[...])
        d = vjp(dob_ref[...])
        for ref, val in zip((dy_ref, dr_ref, dk_ref, dv_ref, dz_ref), d[:5]):
            ref[...] = val
        i = pl.program_id(0)
        for ref, val in zip((dlw_ref, dlb_ref, drk_ref), d[5:8]):
            _acc_store(i, ref, val)

    consts = [ln_w, ln_b, rk, ind, ind_t]
    vec = jax.ShapeDtypeStruct((1, RW), F32)
    return _pcall(body, name="rwkv_post_bwd", grid=(T // tr,),
                  in_specs=[_tile(tr, RW)] * 5 + [_const(c.shape) for c in consts] + [_tile(tr, RW)],
                  out_specs=[_tile(tr, RW)] * 5 + [_const((1, RW))] * 3,
                  out_shape=[jax.ShapeDtypeStruct((T, RW), F32)] * 5 + [vec] * 3,
                  compiler_params=_cparams(("arbitrary",)))(y, r, kp, v, zb, *consts, dob)


def _adamw_math(w, g, m, v):
    m = ADAM_B1 * m + (1.0 - ADAM_B1) * g
    v = ADAM_B2 * v + (1.0 - ADAM_B2) * (g * g)
    m_hat = m / (1.0 - ADAM_B1 ** ADAM_STEP)
    v_hat = v / (1.0 - ADAM_B2 ** ADAM_STEP)
    delta = -ADAM_LR * (m_hat / (jnp.sqrt(v_hat) + ADAM_EPS) + ADAM_WD * w)
    return delta, m, v


def _adamw(name, w, g, m, v, copy_grad=False):
    R, Cc = w.shape
    Rp = -(-R // 8) * 8
    tr = Rp
    for nb in range(1, Rp // 8 + 1):
        if (Rp // 8) % nb == 0 and (Rp // nb) * Cc * 4 <= 2 * 1024 * 1024:
            tr = Rp // nb
            break

    def body(w_ref, g_ref, m_ref, v_ref, d_ref, nm_ref, nv_ref, *g_out):
        g_v = g_ref[...]
        d, nm, nv = _adamw_math(w_ref[...], g_v, m_ref[...], v_ref[...])
        d_ref[...] = d
        nm_ref[...] = nm
        nv_ref[...] = nv
        if copy_grad:
            g_out[0][...] = g_v

    spec = _tile(tr, Cc)
    n_out = 4 if copy_grad else 3
    return _pcall(body, name=name, grid=(Rp // tr,), in_specs=[spec] * 4, out_specs=[spec] * n_out,
                  out_shape=[jax.ShapeDtypeStruct((R, Cc), F32)] * n_out,
                  compiler_params=_cparams(("parallel",)))(w, g, m, v)


def _row_tile(R, Cc, itemsize, budget=2 * 1024 * 1024):
    for nb in range(1, R // 16 + 1):
        if R % nb == 0 and (R // nb) % 16 == 0 and (R // nb) * Cc * itemsize <= budget:
            return R // nb
    return R


def _add_halves(name, gs, r1, c_idx):
    _, R, Cc = gs.shape
    half = R // 2
    tr = _row_tile(half, Cc, 4)
    nb = half // tr

    def body(c_ref, g_ref, r_ref, o_ref):
        o_ref[...] = (g_ref[...].astype(F32) + r_ref[...].astype(F32)).astype(BF16)

    grid_spec = pltpu.PrefetchScalarGridSpec(
        num_scalar_prefetch=1, grid=(N_CHIPS, nb),
        in_specs=[pl.BlockSpec((1, tr, Cc), lambda s, i, c: (s, c[0] * nb + i, 0)),
                  pl.BlockSpec((1, tr, Cc), lambda s, i, c: (s, i, 0))],
        out_specs=pl.BlockSpec((1, tr, Cc), lambda s, i, c: (s, i, 0)))
    return _pcall(body, name=name, grid_spec=grid_spec, out_shape=jax.ShapeDtypeStruct((N_CHIPS, half, Cc), BF16),
                  compiler_params=_cparams(("parallel", "parallel")))(c_idx, gs, r1)


def _sum_slots(name, r2):
    S, R, Cc = r2.shape
    tr = _row_tile(R, Cc, 4 * S // 2 if r2.dtype == BF16 else 4 * S)

    def body(r_ref, o_ref):
        acc = r_ref[0].astype(F32)
        for s in range(1, S):
            acc = acc + r_ref[s].astype(F32)
        o_ref[...] = acc

    return _pcall(body, name=name, grid=(R // tr,), in_specs=[pl.BlockSpec((S, tr, Cc), lambda i: (0, i, 0))],
                  out_specs=_tile(tr, Cc), out_shape=jax.ShapeDtypeStruct((R, Cc), F32),
                  compiler_params=_cparams(("parallel",)))(r2)


def _sum_chips(name, recv, own, place):
    S, H, Cc = recv.shape
    tr = _row_tile(H, Cc, 4, 1024 * 1024)
    nb = H // tr

    def body(p_ref, r_ref, own_ref, o_ref):
        s = pl.program_id(1)
        me = p_ref[0]

        @pl.when(s == 0)
        def _():
            o_ref[...] = jnp.zeros_like(o_ref)

        @pl.when(s == me)
        def _():
            o_ref[...] += own_ref[0].astype(F32)

        @pl.when(s != me)
        def _():
            o_ref[...] += r_ref[0].astype(F32)

    grid_spec = pltpu.PrefetchScalarGridSpec(
        num_scalar_prefetch=1, grid=(nb, S),
        in_specs=[pl.BlockSpec((1, tr, Cc), lambda i, s, p: (jnp.where(s == p[0], (s + 1) % S, s), i, 0)),
                  pl.BlockSpec((1, tr, Cc), lambda i, s, p: (p[0], i, 0))],
        out_specs=pl.BlockSpec((tr, Cc), lambda i, s, p: (p[1] * nb + i, 0)))
    return _pcall(body, name=name, grid_spec=grid_spec, out_shape=jax.ShapeDtypeStruct((2 * H, Cc), F32),
                  compiler_params=_cparams(("parallel", "arbitrary")))(place, recv, own)


def _cast_bf16(name, w):
    R, Cc = w.shape
    tr = _row_tile(R, Cc, 4)

    def body(w_ref, o_ref):
        o_ref[...] = w_ref[...].astype(BF16)

    return _pcall(body, name=name, grid=(R // tr,), in_specs=[_tile(tr, Cc)], out_specs=_tile(tr, Cc),
                  out_shape=jax.ShapeDtypeStruct((R, Cc), BF16), compiler_params=_cparams(("parallel",)))(w)


_ANY = pl.BlockSpec(memory_space=pl.ANY)


def _place():
    x, y, c = lax.axis_index("x"), lax.axis_index("y"), lax.axis_index("c")
    others = [(1 - x, y), (x, 1 - y), (1 - x, 1 - y)]
    return x, y, c, others


def _gather_weights(shards):
    n = len(shards)
    halves = [s.shape[0] // 2 for s in shards]

    def body(*refs):
        ins, outs = refs[:n], refs[n:2 * n]
        send_sems, recv_sems = refs[2 * n:]
        x, y, c, others = _place()
        me = 2 * x + y

        def rows(k, ref, chip, hc):
            return ref.at[chip, pl.ds(hc * halves[k], halves[k]), :]

        def remote(k, j, src, dst, to):
            return pltpu.make_async_remote_copy(src_ref=src, dst_ref=dst, send_sem=send_sems.at[6 * k + j],
                                                recv_sem=recv_sems.at[6 * k + j], device_id=to, device_id_type=MESH)

        first, passed = [], []
        for k in range(n):
            mine = ins[k].at[pl.ds(c * halves[k], halves[k]), :]
            for j, (px, py) in enumerate(others):
                cp = remote(k, j, mine, rows(k, outs[k], me, c), (px, py, c))
                cp.start()
                first.append(cp)
        for k in range(n):
            for j, (px, py) in enumerate(others):
                land = rows(k, outs[k], 2 * px + py, c)
                remote(k, j, land, land, (x, y, c)).wait_recv()
                cp = remote(k, 3 + j, land, land, (x, y, 1 - c))
                cp.start()
                passed.append(cp)
        for k in range(n):
            for j, (px, py) in enumerate(others):
                land = rows(k, outs[k], 2 * px + py, 1 - c)
                remote(k, 3 + j, land, land, (x, y, c)).wait_recv()
        for cp in first + passed:
            cp.wait_send()

    return _pcall(
        body, name="gather_weights", in_specs=[_ANY] * n, out_specs=[_ANY] * n,
        out_shape=[jax.ShapeDtypeStruct((N_CHIPS,) + s.shape, s.dtype) for s in shards],
        scratch_shapes=[pltpu.SemaphoreType.DMA((6 * n,)), pltpu.SemaphoreType.DMA((6 * n,))],
    )(*shards)


def _exchange_halves(grads):
    n = len(grads)
    halves = [g.shape[1] // 2 for g in grads]

    def body(*refs):
        ins, outs = refs[:n], refs[n:2 * n]
        send_sems, recv_sems = refs[2 * n:]
        x, y, c, _ = _place()
        cps = []
        for k in range(n):
            src = ins[k].at[:, pl.ds((1 - c) * halves[k], halves[k]), :]
            cp = pltpu.make_async_remote_copy(src_ref=src, dst_ref=outs[k], send_sem=send_sems.at[k],
                                              recv_sem=recv_sems.at[k], device_id=(x, y, 1 - c), device_id_type=MESH)
            cp.start()
            cps.append(cp)
        for cp in cps:
            cp.wait()

    return _pcall(
        body, name="exchange_halves", in_specs=[_ANY] * n, out_specs=[_ANY] * n,
        out_shape=[jax.ShapeDtypeStruct((N_CHIPS, h) + g.shape[2:], g.dtype) for g, h in zip(grads, halves)],
        scratch_shapes=[pltpu.SemaphoreType.DMA((n,)), pltpu.SemaphoreType.DMA((n,))],
    )(*grads)


def _scatter_to_owners(chip_sums, small):
    n = len(chip_sums)

    def body(*refs):
        ins, small_in = refs[:n], refs[n]
        outs, small_out = refs[n + 1:2 * n + 1], refs[2 * n + 1]
        send_sems, recv_sems, local_sem, ssend, srecv = refs[2 * n + 2:]
        x, y, c, others = _place()
        me = 2 * x + y
        dev = 2 * me + c
        local = pltpu.make_async_copy(small_in, small_out.at[dev], local_sem)
        local.start()
        sends = []
        for k in range(n):
            for j, (px, py) in enumerate(others):
                cp = pltpu.make_async_remote_copy(
                    src_ref=ins[k].at[2 * px + py], dst_ref=outs[k].at[me], send_sem=send_sems.at[3 * k + j],
                    recv_sem=recv_sems.at[3 * k + j], device_id=(px, py, c), device_id_type=MESH)
                cp.start()
                sends.append(cp)
        rel = [(dx, dy, dc) for dx in (0, 1) for dy in (0, 1) for dc in (0, 1)][1:]
        for r, (dx, dy, dc) in enumerate(rel):
            to = (x ^ dx, y ^ dy, c ^ dc)
            cp = pltpu.make_async_remote_copy(src_ref=small_in, dst_ref=small_out.at[dev], send_sem=ssend.at[r],
                                              recv_sem=srecv.at[r], device_id=to, device_id_type=MESH)
            cp.start()
            sends.append(cp)
        for k in range(n):
            for j, (px, py) in enumerate(others):
                land = outs[k].at[2 * px + py]
                pltpu.make_async_remote_copy(src_ref=land, dst_ref=land, send_sem=send_sems.at[3 * k + j],
                                             recv_sem=recv_sems.at[3 * k + j], device_id=(x, y, c),
                                             device_id_type=MESH).wait_recv()
        for r, (dx, dy, dc) in enumerate(rel):
            land = small_out.at[4 * (x ^ dx) + 2 * (y ^ dy) + (c ^ dc)]
            pltpu.make_async_remote_copy(src_ref=land, dst_ref=land, send_sem=ssend.at[r], recv_sem=srecv.at[r],
                                         device_id=(x, y, c), device_id_type=MESH).wait_recv()
        for cp in sends:
            cp.wait_send()
        local.wait()

    return _pcall(
        body, name="scatter_to_owners", in_specs=[_ANY] * (n + 1), out_specs=[_ANY] * (n + 1),
        out_shape=[jax.ShapeDtypeStruct(g.shape, g.dtype) for g in chip_sums]
        + [jax.ShapeDtypeStruct((N_DEV,) + small.shape, small.dtype)],
        scratch_shapes=[pltpu.SemaphoreType.DMA((3 * n,)), pltpu.SemaphoreType.DMA((3 * n,)),
                        pltpu.SemaphoreType.DMA, pltpu.SemaphoreType.DMA((7,)), pltpu.SemaphoreType.DMA((7,))],
    )(*chip_sums, small)


def _join_halves(fulls):
    n = len(fulls)
    hs = [f.shape[0] // 2 for f in fulls]

    def body(*refs):
        ins, outs = refs[:n], refs[n:2 * n]
        send_sems, recv_sems = refs[2 * n:]
        x, y, c, _ = _place()
        cps = []
        for k in range(n):
            mine = pl.ds(c * hs[k], hs[k])
            cp = pltpu.make_async_remote_copy(src_ref=ins[k].at[mine, :], dst_ref=outs[k].at[mine, :],
                                              send_sem=send_sems.at[k], recv_sem=recv_sems.at[k],
                                              device_id=(x, y, 1 - c), device_id_type=MESH)
            cp.start()
            cps.append(cp)
        for k in range(n):
            land = outs[k].at[pl.ds((1 - c) * hs[k], hs[k]), :]
            pltpu.make_async_remote_copy(src_ref=land, dst_ref=land, send_sem=send_sems.at[k],
                                         recv_sem=recv_sems.at[k], device_id=(x, y, c), device_id_type=MESH).wait_recv()
        for cp in cps:
            cp.wait_send()

    return _pcall(
        body, name="join_halves", in_specs=[_ANY] * n, out_specs=[_ANY] * n,
        out_shape=[jax.ShapeDtypeStruct(f.shape, f.dtype) for f in fulls],
        input_output_aliases={k: k for k in range(n)},
        scratch_shapes=[pltpu.SemaphoreType.DMA((n,)), pltpu.SemaphoreType.DMA((n,))],
    )(*fulls)


def _heads(cfg, a):
    return a.reshape(cfg.T, cfg.RH, RWKV_HEAD_DIM).transpose(1, 0, 2)


def _unheads(cfg, a):
    return a.transpose(1, 0, 2).reshape(cfg.T, cfg.RW)


def _local_step(cfg, x2, target, norm_gain, w_my, fb, mu_g, w0, w2, a0, a2, k_k, k_a, r_k, ln_w, ln_b, wpf, wpr, wout,
                fng):
    T, D, FW, FH, RW, RH, LP, lora = cfg.T, cfg.D, cfg.FW, cfg.FH, cfg.RW, cfg.RH, cfg.LP, cfg.lora
    fb_p = jnp.pad(fb, ((0, 0), (0, LANES - FH)))
    mu = _rwkv_vec_to_my(cfg, mu_g)
    w2p = jnp.pad(w2, ((0, LP - lora), (0, 0)))
    a2p = jnp.pad(a2, ((0, LP - lora), (0, 0)))
    rk = r_k.reshape(1, RW)
    tm = min(1024, T)

    h = _rms_fwd(cfg, x2, norm_gain)
    u = _mm("in_proj", h, w_my, "nn", F32, tm, cfg.tn, 512)
    c_cols = _fox_prep(cfg, u, fb_p)
    c_rows = c_cols[:, :FH].T.reshape(FH, 1, T)
    o, lse = _attn_fwd(cfg, u, c_rows)
    oa = _gate_a_fwd(cfg, o, u)
    prep = _rwkv_prep_fwd(cfg, u, mu, w0, w2p, a0, a2p, k_k, k_a)
    r, lw, kp, v, an, b, zb = prep
    seqs = [_heads(cfg, t) for t in (r, lw, kp, v, an, b)]
    y_h, ckpt = _scan_fwd(cfg, seqs)
    y = _unheads(cfg, y_h)
    ob = _rwkv_post_fwd(cfg, y, r, kp, v, zb, ln_w, ln_b, rk)
    pa = _mm("proj_fox", oa, wpf, "nn", F32, tm, 1024, 512)
    pb = _mm("proj_rwkv", ob, wpr, "nn", F32, tm, 1024, 512)
    m = _merge_fwd(cfg, pa, pb, u)
    mo = _mm("out_proj", m, wout, "nn", F32, tm, 1024, 512)
    loss8, dres, dres16, d_fng = _final(cfg, x2, mo, fng.reshape(1, D), target)

    dm = _mm("out_proj_dx", dres16, wout, "nt", F32, tm, 1024, 512)
    d_wout = _mm("out_proj_dw", m, dres16, "tn", BF16, 1024, 1024, 512)
    dpa, dpb, dgate = _merge_bwd(cfg, pa, pb, u, dm)
    doa = _mm("proj_fox_dx", dpa, wpf, "nt", F32, tm, 1024, 512)
    d_wpf = _mm("proj_fox_dw", oa, dpa, "tn", BF16, 1024, 1024, 512)
    dob = _mm("proj_rwkv_dx", dpb, wpr, "nt", F32, tm, 1024, 512)
    d_wpr = _mm("proj_rwkv_dw", ob, dpb, "tn", BF16, 1024, 1024, 512)

    do, dza = _gate_a_bwd(cfg, o, u, doa)
    dq, dk, dv, dcol = _attn_bwd(cfg, u, c_rows, lse, do)
    dc = jnp.pad(-dcol.reshape(FH, T).T, ((0, 0), (0, LANES - FH)))
    df, d_fb = _fox_prep_bwd(cfg, u, fb_p, dc)

    dy, dr_p, dk_p, dv_p, dzb, d_lnw, d_lnb, d_rk = _rwkv_post_bwd(cfg, y, r, kp, v, zb, ln_w, ln_b, rk, dob)
    dseq = _scan_bwd(cfg, seqs, ckpt, _heads(cfg, dy))
    dr_s, dlw_s, dk_s, dv_s, da_s, db_s = [_unheads(cfg, t) for t in dseq]
    cots = [dr_s + dr_p, dlw_s, dk_s + dk_p, dv_s + dv_p, da_s, db_s]
    dus, d_mu, d_w0, d_w2p, d_a0, d_a2p, d_kk, d_ka = _rwkv_prep_bwd(cfg, u, mu, w0, w2p, a0, a2p, k_k, k_a, cots, dzb)
    du_rwkv = _shift_bwd(cfg, dus, mu)

    pad_f = jnp.zeros((T, cfg.ncol - cfg.o_ad - LP), BF16)
    du = jnp.concatenate([dq, dk.astype(BF16), dv.astype(BF16), dza, du_rwkv[:, :4 * RW], dgate, df,
                          du_rwkv[:, 4 * RW:], pad_f], axis=1)
    dh = _mm("in_proj_dx", du, w_my, "nt", F32, tm, 1024, cfg.tn)
    d_wmy = _mm("in_proj_dw", du, h, "tn", BF16, cfg.tn, 1024, 512)
    gx, d_ng = _rms_bwd(cfg, x2, norm_gain, dh, dres)

    small = dict(norm_gain=d_ng, fox_forget_bias=d_fb[:, :FH], rwkv_shift_mix=_rwkv_vec_from_my(cfg, d_mu),
                 rwkv_w0=d_w0, rwkv_a0=d_a0, rwkv_k_k=d_kk, rwkv_k_a=d_ka, rwkv_r_k=d_rk, rwkv_ln_w=d_lnw,
                 rwkv_ln_b=d_lnb, final_norm_gain=d_fng)
    big = dict(w_in=d_wmy, rwkv_w2=d_w2p[:lora], rwkv_a2=d_a2p[:lora], w_proj_fox=d_wpf, w_proj_rwkv=d_wpr,
               w_out=d_wout)
    return loss8[0, 0], gx, small, big


_SMALL = ["norm_gain", "fox_forget_bias", "rwkv_shift_mix", "rwkv_w0", "rwkv_a0", "rwkv_k_k", "rwkv_k_a", "rwkv_r_k",
          "rwkv_ln_w", "rwkv_ln_b", "final_norm_gain"]
_WEIGHTS = ["norm_gain", "w_in", "fox_forget_bias", "rwkv_shift_mix", "rwkv_w0", "rwkv_w2", "rwkv_a0", "rwkv_a2",
            "rwkv_k_k", "rwkv_k_a", "rwkv_r_k", "rwkv_ln_w", "rwkv_ln_b", "w_proj_fox", "w_proj_rwkv", "w_out",
            "final_norm_gain"]


def _pack_small(arrs):
    parts = []
    for a in arrs:
        f = a.reshape(-1)
        parts.append(jnp.pad(f, (0, (-f.shape[0]) % LANES)))
    flat = jnp.concatenate(parts)
    rows = flat.shape[0] // LANES
    flat = jnp.pad(flat, (0, ((-rows) % 8) * LANES))
    return flat.reshape(-1, LANES)


def _unpack_small(packed, shapes):
    flat = packed.reshape(-1)
    out, pos = [], 0
    for s in shapes:
        n = int(np.prod(s))
        out.append(flat[pos:pos + n].reshape(s))
        pos += n + ((-n) % LANES)
    return out


def _shard_major(a, axis):
    parts = jnp.split(a, N_CHIPS, axis=axis)
    return jnp.stack(parts, axis=0)


def kernel(x, norm_gain, w_in, fox_forget_bias, rwkv_shift_mix, rwkv_w0, rwkv_w2, rwkv_a0, rwkv_a2, rwkv_k_k, rwkv_k_a, rwkv_r_k, rwkv_ln_w, rwkv_ln_b, w_proj_fox, w_proj_rwkv, w_out, final_norm_gain, loss_target, m_norm_gain, m_w_in, m_fox_forget_bias, m_rwkv_shift_mix, m_rwkv_w0, m_rwkv_w2, m_rwkv_a0, m_rwkv_a2, m_rwkv_k_k, m_rwkv_k_a, m_rwkv_r_k, m_rwkv_ln_w, m_rwkv_ln_b, m_w_proj_fox, m_w_proj_rwkv, m_w_out, m_final_norm_gain, v_norm_gain, v_w_in, v_fox_forget_bias, v_rwkv_shift_mix, v_rwkv_w0, v_rwkv_w2, v_rwkv_a0, v_rwkv_a2, v_rwkv_k_k, v_rwkv_k_a, v_rwkv_r_k, v_rwkv_ln_w, v_rwkv_ln_b, v_w_proj_fox, v_w_proj_rwkv, v_w_out, v_final_norm_gain):
    args = dict(locals())
    T, D = x.shape[1], x.shape[2]
    lora = rwkv_w2.shape[1]
    cfg = _Cfg(T, D, lora)
    RW = cfg.RW
    c_idx = lax.axis_index("c").astype(jnp.int32).reshape(1)
    me_chip = (2 * lax.axis_index("x") + lax.axis_index("y")).astype(jnp.int32)
    place = jnp.concatenate([me_chip.reshape(1), c_idx])

    w_in_t, m_in_t, v_in_t = w_in[0].T, m_w_in[0].T, v_w_in[0].T
    w_in_s = w_in[0].astype(BF16)
    wp_s = jnp.concatenate([w_proj_fox[0], w_proj_rwkv[0]], axis=0)
    lora_s = jnp.concatenate([rwkv_w2[0], rwkv_a2[0]], axis=0)
    mine = [w_in_s, _cast_bf16("cast_w_proj", wp_s), _cast_bf16("cast_w_out", w_out[0]), lora_s]
    gathered = _gather_weights(mine)
    g_in, g_wp, g_out, g_lora = [lax.dynamic_update_slice(g, own[None], (me_chip, 0, 0))
                                 for g, own in zip(gathered, mine)]
    w_my = _shards_to_my_layout(cfg, g_in)
    wp = g_wp.transpose(1, 0, 2).reshape(2 * RW, D)
    wout = g_out.reshape(D, D)
    lo = g_lora.transpose(1, 0, 2).reshape(2 * lora, RW)

    loss_dev, gx, small, big = _local_step(
        cfg, x[0], loss_target[0], norm_gain, w_my, fox_forget_bias, rwkv_shift_mix, rwkv_w0, lo[:lora], rwkv_a0,
        lo[lora:], rwkv_k_k, rwkv_k_a, rwkv_r_k, rwkv_ln_w, rwkv_ln_b, wp[:RW], wp[RW:], wout, final_norm_gain)
    loss = lax.psum(loss_dev, ("x", "y", "c"))

    gs_in = _my_layout_to_shards(cfg, big["w_in"])
    gs_wp = _shard_major(jnp.concatenate([big["w_proj_fox"], big["w_proj_rwkv"]], axis=0), 1)
    gs_out = _shard_major(big["w_out"], 0)
    gs_lora = _shard_major(jnp.concatenate([big["rwkv_w2"], big["rwkv_a2"]], axis=0).astype(BF16), 1)
    gs = [gs_in, gs_wp, gs_out, gs_lora]
    names = ["w_in", "w_proj", "w_out", "lora"]
    recv1 = _exchange_halves(gs)
    chip_sums = [_add_halves("add_halves_" + nm, g, r, c_idx) for nm, g, r in zip(names, gs, recv1)]
    small_shapes = [args[nm].shape for nm in _SMALL]
    packed = _pack_small([small[nm] for nm in _SMALL])
    *recv2, small_all = _scatter_to_owners(chip_sums, packed)
    reduced = [_sum_chips("sum_chips_" + nm, r, own, place) for nm, r, own in zip(names, recv2, chip_sums)]
    g_small = _sum_slots("sum_small", small_all)
    g_in_f, g_wp_f, g_out_f, g_lora_f = _join_halves(reduced)

    grads = dict(zip(_SMALL, _unpack_small(g_small, small_shapes)))
    grads["w_proj_fox"] = g_wp_f[None, :RW]
    grads["w_proj_rwkv"] = g_wp_f[None, RW:]
    grads["w_out"] = g_out_f[None]
    grads["rwkv_w2"] = g_lora_f[None, :lora]
    grads["rwkv_a2"] = g_lora_f[None, lora:]

    delta, new_m, new_v = {}, {}, {}
    w_small = _pack_small([args[nm] for nm in _SMALL])
    m_small = _pack_small([args["m_" + nm] for nm in _SMALL])
    v_small = _pack_small([args["v_" + nm] for nm in _SMALL])
    d_s, m_s, v_s = _adamw("adamw_small", w_small, g_small, m_small, v_small)
    for tgt, pk in ((delta, d_s), (new_m, m_s), (new_v, v_s)):
        tgt.update(zip(_SMALL, _unpack_small(pk, small_shapes)))
    d_t, m_t, v_t, g_t = _adamw("adamw_w_in", w_in_t, g_in_f, m_in_t, v_in_t, copy_grad=True)
    grads["w_in"], delta["w_in"], new_m["w_in"], new_v["w_in"] = [t.T[None] for t in (g_t, d_t, m_t, v_t)]
    for nm in ("w_proj_fox", "w_proj_rwkv", "w_out", "rwkv_w2", "rwkv_a2"):
        shp = args[nm].shape
        two_d = (shp[1], shp[2])
        d_b, m_b, v_b = _adamw("adamw_" + nm, args[nm].reshape(two_d), grads[nm].reshape(two_d),
                               args["m_" + nm].reshape(two_d), args["v_" + nm].reshape(two_d))
        delta[nm], new_m[nm], new_v[nm] = d_b.reshape(shp), m_b.reshape(shp), v_b.reshape(shp)

    return (loss, gx[None], *[grads[n] for n in _WEIGHTS], *[delta[n] for n in _WEIGHTS],
            *[new_m[n] for n in _WEIGHTS], *[new_v[n] for n in _WEIGHTS])
```

```python
import functools

import numpy as np
import jax
import jax.numpy as jnp
from jax import lax
from jax.experimental import pallas as pl
from jax.experimental.pallas import tpu as pltpu

F32 = jnp.float32
BF16 = jnp.bfloat16
HI = lax.Precision.HIGHEST
MESH = pl.DeviceIdType.MESH

FOX_HEAD_DIM = 128
RWKV_HEAD_DIM = 64
RMS_EPS = 1e-6
GN_EPS = 64e-5
L2_EPS = 1e-12
ADAM_LR = 0.001
ADAM_B1 = 0.9
ADAM_B2 = 0.999
ADAM_EPS = 1e-08
ADAM_WD = 0.01
ADAM_STEP = 10

LANES = 128
VMEM_LIMIT = 56 * 1024 * 1024
SCAN_CHUNK = 64
SCAN_HEADS_PER_STEP = 8
SCAN_PASSES = (3, 1, 1)
N_CHIPS = 4
N_DEV = 8

_pcall = pl.pallas_call


def _cparams(sem=None):
    return pltpu.CompilerParams(dimension_semantics=sem, vmem_limit_bytes=VMEM_LIMIT)


def _softplus(x):
    return jnp.maximum(x, 0.0) + jnp.log(1.0 + jnp.exp(-jnp.abs(x)))


def _silu(z):
    return z * jax.nn.sigmoid(z)


def _rmsn(x, g):
    return x * lax.rsqrt(jnp.mean(x * x, axis=-1, keepdims=True) + RMS_EPS) * g


def _dot(a, b, dims="nn", precision=None):
    dn = {"nn": (((1,), (0,)), ((), ())), "nt": (((1,), (1,)), ((), ())), "tn": (((0,), (0,)), ((), ()))}[dims]
    return lax.dot_general(a, b, dn, precision=precision, preferred_element_type=F32)


def _split_bf16(x):
    hi = x.astype(BF16)
    return hi, (x - hi.astype(F32)).astype(BF16)


def _bdot_raw(a, b, ca, cb, passes):
    dn = (((ca,), (cb,)), ((0,), (0,)))
    mm = lambda p, q: lax.dot_general(p, q, dn, preferred_element_type=F32)
    if passes == 1:
        return mm(a.astype(BF16), b.astype(BF16))
    ah, al = _split_bf16(a)
    bh, bl = _split_bf16(b)
    return mm(ah, bh) + (mm(ah, bl) + mm(al, bh))


@functools.partial(jax.custom_vjp, nondiff_argnums=(2, 3, 4))
def _bdot_p(a, b, ca, cb, passes):
    return _bdot_raw(a, b, ca, cb, passes)


def _bdot_fwd(a, b, ca, cb, passes):
    return _bdot_raw(a, b, ca, cb, passes), (a, b)


def _bdot_bwd(ca, cb, passes, res, g):
    a, b = res
    if (ca, cb) == (2, 1):
        return _bdot_p(g, b, 2, 2, passes), _bdot_p(a, g, 1, 1, passes)
    if (ca, cb) == (2, 2):
        return _bdot_p(g, b, 2, 1, passes), _bdot_p(g, a, 1, 1, passes)
    assert (ca, cb) == (1, 1)
    return _bdot_p(b, g, 2, 2, passes), _bdot_p(a, g, 2, 1, passes)


_bdot_p.defvjp(_bdot_fwd, _bdot_bwd)


def _bdot(a, b, ca, cb, passes=3):
    return _bdot_p(a, b, ca, cb, passes)


def _dot3(a, b):
    return _bdot(a[None], b[None], 2, 1)[0]


@jax.custom_vjp
def _xdot(x, m, mt):
    hi, lo = _split_bf16(x)
    m16 = m.astype(BF16)
    return _dot(hi, m16) + _dot(lo, m16)


def _xdot_fwd(x, m, mt):
    return _xdot(x, m, mt), (m, mt)


def _xdot_bwd(res, g):
    m, mt = res
    return _xdot(g, mt, m), jnp.zeros_like(m), jnp.zeros_like(mt)


_xdot.defvjp(_xdot_fwd, _xdot_bwd)


class _Cfg:
    def __init__(self, T, D, lora):
        self.T, self.D, self.lora = T, D, lora
        self.FW = D // 2
        self.FH = self.FW // FOX_HEAD_DIM
        self.RW = D // 2
        self.RH = self.RW // RWKV_HEAD_DIM
        self.LP = -(-lora // LANES) * LANES
        self.o_fox = 0
        self.o_rwkv = 4 * self.FW
        self.o_gate = self.o_rwkv + 4 * self.RW
        self.o_f = self.o_gate + 2 * D
        self.o_wd = self.o_f + LANES
        self.o_ad = self.o_wd + self.LP
        end = self.o_ad + self.LP
        self.tn = 1280 if D >= 2048 else LANES
        self.ncol = -(-end // self.tn) * self.tn
        self.in_cols = 4 * self.FW + self.FH + 4 * self.RW + 2 * lora + 2 * D
        self.scp = -(-(self.in_cols // N_CHIPS) // LANES) * LANES
        self.rseg = 4 * self.RW + 2 * self.LP
        self.C = min(SCAN_CHUNK, T)
        self.tr = min(256, T)
        self.hb = min(SCAN_HEADS_PER_STEP, self.RH)

    def segments(self):
        FW, FH, RW, lo, D = self.FW, self.FH, self.RW, self.lora, self.D
        g_f = 4 * FW
        g_r = g_f + FH
        g_wd = g_r + 4 * RW
        g_ad = g_wd + lo
        g_g = g_ad + lo
        return [(0, 4 * FW, 0), (g_f, FH, self.o_f), (g_r, 4 * RW, self.o_rwkv), (g_wd, lo, self.o_wd),
                (g_ad, lo, self.o_ad), (g_g, 2 * D, self.o_gate)]


def _to_my_layout(cfg, wg):
    R = wg.shape[0]
    segs = sorted(cfg.segments(), key=lambda s: s[2])
    parts, pos = [], 0
    for g0, w, m0 in segs:
        if m0 > pos:
            parts.append(jnp.zeros((R, m0 - pos), wg.dtype))
        parts.append(wg[:, g0:g0 + w])
        pos = m0 + w
    if cfg.ncol > pos:
        parts.append(jnp.zeros((R, cfg.ncol - pos), wg.dtype))
    return jnp.concatenate(parts, axis=1)


def _from_my_layout(cfg, wm):
    segs = sorted(cfg.segments(), key=lambda s: s[0])
    return jnp.concatenate([wm[:, m0:m0 + w] for g0, w, m0 in segs], axis=1)


def _shards_to_my_layout(cfg, g):
    R, sc = g.shape[1], g.shape[2]
    segs = sorted(cfg.segments(), key=lambda s: s[2])
    parts, pos = [], 0
    for g0, w, m0 in segs:
        if m0 > pos:
            parts.append(jnp.zeros((R, m0 - pos), g.dtype))
        for s in range(N_CHIPS):
            lo, hi = max(g0, s * sc), min(g0 + w, (s + 1) * sc)
            if lo < hi:
                parts.append(g[s, :, lo - s * sc:hi - s * sc])
        pos = m0 + w
    if cfg.ncol > pos:
        parts.append(jnp.zeros((R, cfg.ncol - pos), g.dtype))
    return jnp.concatenate(parts, axis=1)


def _my_layout_to_shards(cfg, wm):
    sc, R = cfg.in_cols // N_CHIPS, wm.shape[1]
    segs = sorted(cfg.segments(), key=lambda s: s[0])
    shards = []
    for s in range(N_CHIPS):
        parts = []
        for g0, w, m0 in segs:
            lo, hi = max(g0, s * sc), min(g0 + w, (s + 1) * sc)
            if lo < hi:
                parts.append(wm[m0 + lo - g0:m0 + hi - g0, :])
        parts.append(jnp.zeros((cfg.scp - sc, R), wm.dtype))
        shards.append(jnp.concatenate(parts, axis=0))
    return jnp.stack(shards, axis=0)


def _rwkv_vec_to_my(cfg, v):
    RW4, lo, LP = 4 * cfg.RW, cfg.lora, cfg.LP
    z = jnp.zeros((1, LP - lo), v.dtype)
    return jnp.concatenate([v[:, :RW4], v[:, RW4:RW4 + lo], z, v[:, RW4 + lo:], z], axis=1)


def _rwkv_vec_from_my(cfg, v):
    RW4, lo, LP = 4 * cfg.RW, cfg.lora, cfg.LP
    return jnp.concatenate([v[:, :RW4], v[:, RW4:RW4 + lo], v[:, RW4 + LP:RW4 + LP + lo]], axis=1)


def _mm(name, a, b, dims, out_dtype, tm, tn, tk):
    (M, K) = a.shape if dims != "tn" else a.shape[::-1]
    N = b.shape[0] if dims == "nt" else b.shape[1]
    tm, tn, tk = min(tm, M), min(tn, N), min(tk, K)
    assert M % tm == 0 and N % tn == 0 and K % tk == 0, (name, M, N, K, tm, tn, tk)
    nk = K // tk
    if dims == "nn":
        a_spec = pl.BlockSpec((tm, tk), lambda i, j, k: (i, k))
        b_spec = pl.BlockSpec((tk, tn), lambda i, j, k: (k, j))
    elif dims == "nt":
        a_spec = pl.BlockSpec((tm, tk), lambda i, j, k: (i, k))
        b_spec = pl.BlockSpec((tn, tk), lambda i, j, k: (j, k))
    else:
        a_spec = pl.BlockSpec((tk, tm), lambda i, j, k: (k, i))
        b_spec = pl.BlockSpec((tk, tn), lambda i, j, k: (k, j))

    def body(a_ref, b_ref, o_ref, acc_ref):
        k = pl.program_id(2)

        @pl.when(k == 0)
        def _():
            acc_ref[...] = jnp.zeros_like(acc_ref)

        acc_ref[...] += _dot(a_ref[...], b_ref[...], dims)

        @pl.when(k == nk - 1)
        def _():
            o_ref[...] = acc_ref[...].astype(o_ref.dtype)

    return _pcall(
        body, name=name, grid=(M // tm, N // tn, nk),
        in_specs=[a_spec, b_spec], out_specs=pl.BlockSpec((tm, tn), lambda i, j, k: (i, j)),
        out_shape=jax.ShapeDtypeStruct((M, N), out_dtype), scratch_shapes=[pltpu.VMEM((tm, tn), F32)],
        compiler_params=_cparams(("parallel", "parallel", "arbitrary")),
    )(a, b)


def _tile(tr, w, cb=0):
    return pl.BlockSpec((tr, w), lambda i: (i, cb))


def _const(shape):
    nd = len(shape)
    return pl.BlockSpec(shape, lambda i: (0,) * nd)


def _acc_store(i, ref, val):
    @pl.when(i == 0)
    def _():
        ref[...] = val

    @pl.when(i > 0)
    def _():
        ref[...] += val


def _rms_fwd(cfg, x2, g):
    T, D, tr = cfg.T, cfg.D, cfg.tr

    def body(x_ref, g_ref, h_ref):
        h_ref[...] = _rmsn(x_ref[...], g_ref[...]).astype(BF16)

    return _pcall(body, name="rms_fwd", grid=(T // tr,), in_specs=[_tile(tr, D), _const((1, D))],
                  out_specs=_tile(tr, D), out_shape=jax.ShapeDtypeStruct((T, D), BF16),
                  compiler_params=_cparams(("parallel",)))(x2, g)


def _rms_bwd(cfg, x2, g, dh, dres):
    T, D, tr = cfg.T, cfg.D, cfg.tr

    def body(x_ref, g_ref, dh_ref, dres_ref, gx_ref, dg_ref):
        _, vjp = jax.vjp(_rmsn, x_ref[...], g_ref[...])
        dx, dg = vjp(dh_ref[...])
        gx_ref[...] = dx + dres_ref[...]
        _acc_store(pl.program_id(0), dg_ref, dg)

    return _pcall(body, name="rms_bwd", grid=(T // tr,),
                  in_specs=[_tile(tr, D), _const((1, D)), _tile(tr, D), _tile(tr, D)],
                  out_specs=[_tile(tr, D), _const((1, D))],
                  out_shape=[jax.ShapeDtypeStruct((T, D), F32), jax.ShapeDtypeStruct((1, D), F32)],
                  compiler_params=_cparams(("arbitrary",)))(x2, g, dh, dres)


def _final(cfg, x2, mo, fg, target):
    T, D, tr = cfg.T, cfg.D, cfg.tr

    def loss_fn(hres, g, tgt):
        err = _rmsn(hres, g) - tgt
        return 0.5 * jnp.sum(jnp.mean(err * err, axis=-1, keepdims=True), axis=0, keepdims=True)

    def body(x_ref, mo_ref, g_ref, t_ref, loss_ref, dres_ref, dres16_ref, dg_ref):
        hres = x_ref[...] + mo_ref[...]
        loss, vjp = jax.vjp(functools.partial(loss_fn, tgt=t_ref[...]), hres, g_ref[...])
        dres, dg = vjp(jnp.ones((1, 1), F32))
        dres_ref[...] = dres
        dres16_ref[...] = dres.astype(BF16)
        i = pl.program_id(0)
        _acc_store(i, dg_ref, dg)
        _acc_store(i, loss_ref, jnp.broadcast_to(loss, (8, LANES)))

    return _pcall(body, name="final_loss", grid=(T // tr,),
                  in_specs=[_tile(tr, D), _tile(tr, D), _const((1, D)), _tile(tr, D)],
                  out_specs=[_const((8, LANES)), _tile(tr, D), _tile(tr, D), _const((1, D))],
                  out_shape=[jax.ShapeDtypeStruct((8, LANES), F32), jax.ShapeDtypeStruct((T, D), F32),
                             jax.ShapeDtypeStruct((T, D), BF16), jax.ShapeDtypeStruct((1, D), F32)],
                  compiler_params=_cparams(("arbitrary",)))(x2, mo, fg, target)


def _merge_fn(pa, pb, ga, gb):
    return jax.nn.sigmoid(ga) * pa + jax.nn.sigmoid(gb) * pb


def _merge_fwd(cfg, pa, pb, u):
    T, D, tr = cfg.T, cfg.D, cfg.tr
    cga, cgb = cfg.o_gate // D, cfg.o_gate // D + 1

    def body(pa_ref, pb_ref, ga_ref, gb_ref, m_ref):
        m_ref[...] = _merge_fn(pa_ref[...], pb_ref[...], ga_ref[...], gb_ref[...]).astype(BF16)

    return _pcall(body, name="merge_fwd", grid=(T // tr,),
                  in_specs=[_tile(tr, D), _tile(tr, D), _tile(tr, D, cga), _tile(tr, D, cgb)],
                  out_specs=_tile(tr, D), out_shape=jax.ShapeDtypeStruct((T, D), BF16),
                  compiler_params=_cparams(("parallel",)))(pa, pb, u, u)


def _merge_bwd(cfg, pa, pb, u, dm):
    T, D, tr = cfg.T, cfg.D, cfg.tr
    cga, cgb = cfg.o_gate // D, cfg.o_gate // D + 1

    def body(pa_ref, pb_ref, ga_ref, gb_ref, dm_ref, dpa_ref, dpb_ref, dg_ref):
        _, vjp = jax.vjp(_merge_fn, pa_ref[...], pb_ref[...], ga_ref[...], gb_ref[...])
        dpa, dpb, dga, dgb = vjp(dm_ref[...])
        dpa_ref[...] = dpa.astype(BF16)
        dpb_ref[...] = dpb.astype(BF16)
        dg_ref[:, :D] = dga.astype(BF16)
        dg_ref[:, D:] = dgb.astype(BF16)

    return _pcall(body, name="merge_bwd", grid=(T // tr,),
                  in_specs=[_tile(tr, D), _tile(tr, D), _tile(tr, D, cga), _tile(tr, D, cgb), _tile(tr, D)],
                  out_specs=[_tile(tr, D), _tile(tr, D), _tile(tr, 2 * D)],
                  out_shape=[jax.ShapeDtypeStruct((T, D), BF16), jax.ShapeDtypeStruct((T, D), BF16),
                             jax.ShapeDtypeStruct((T, 2 * D), BF16)],
                  compiler_params=_cparams(("parallel",)))(pa, pb, u, u, dm)


def _gate_fn(o, z):
    return o * _silu(z)


def _gate_a_fwd(cfg, o, u):
    T, FW, tr = cfg.T, cfg.FW, cfg.tr

    def body(o_ref, z_ref, oa_ref):
        oa_ref[...] = _gate_fn(o_ref[...], z_ref[...]).astype(BF16)

    return _pcall(body, name="gate_a_fwd", grid=(T // tr,), in_specs=[_tile(tr, FW), _tile(tr, FW, 3)],
                  out_specs=_tile(tr, FW), out_shape=jax.ShapeDtypeStruct((T, FW), BF16),
                  compiler_params=_cparams(("parallel",)))(o, u)


def _gate_a_bwd(cfg, o, u, doa):
    T, FW, tr = cfg.T, cfg.FW, cfg.tr

    def body(o_ref, z_ref, doa_ref, do_ref, dz_ref):
        _, vjp = jax.vjp(_gate_fn, o_ref[...], z_ref[...])
        do, dz = vjp(doa_ref[...])
        do_ref[...] = do
        dz_ref[...] = dz.astype(BF16)

    return _pcall(body, name="gate_a_bwd", grid=(T // tr,),
                  in_specs=[_tile(tr, FW), _tile(tr, FW, 3), _tile(tr, FW)],
                  out_specs=[_tile(tr, FW), _tile(tr, FW)],
                  out_shape=[jax.ShapeDtypeStruct((T, FW), F32), jax.ShapeDtypeStruct((T, FW), BF16)],
                  compiler_params=_cparams(("parallel",)))(o, u, doa)


def _fox_prep(cfg, u, fb):
    T, tr = cfg.T, cfg.tr
    cf = cfg.o_f // LANES

    def body(f_ref, fb_ref, c_ref, carry_ref):
        i = pl.program_id(0)

        @pl.when(i == 0)
        def _():
            carry_ref[...] = jnp.zeros_like(carry_ref)

        lf = -_softplus(-(f_ref[...] + fb_ref[...]))
        r = lax.broadcasted_iota(jnp.int32, (tr, tr), 0)
        c = lax.broadcasted_iota(jnp.int32, (tr, tr), 1)
        tri = (r >= c).astype(F32)
        c_ref[...] = _dot(tri, lf, precision=HI) + carry_ref[...]
        carry_ref[...] += jnp.sum(lf, axis=0, keepdims=True)

    return _pcall(body, name="fox_prep", grid=(T // tr,), in_specs=[_tile(tr, LANES, cf), _const((1, LANES))],
                  out_specs=_tile(tr, LANES), out_shape=jax.ShapeDtypeStruct((T, LANES), F32),
                  scratch_shapes=[pltpu.VMEM((1, LANES), F32)], compiler_params=_cparams(("arbitrary",)))(u, fb)


def _fox_prep_bwd(cfg, u, fb, dc):
    T, tr = cfg.T, cfg.tr
    cf = cfg.o_f // LANES
    nb = T // tr

    def body(f_ref, fb_ref, dc_ref, df_ref, dfb_ref, carry_ref):
        i = pl.program_id(0)

        @pl.when(i == 0)
        def _():
            carry_ref[...] = jnp.zeros_like(carry_ref)

        dc = dc_ref[...]
        r = lax.broadcasted_iota(jnp.int32, (tr, tr), 0)
        c = lax.broadcasted_iota(jnp.int32, (tr, tr), 1)
        triu = (r <= c).astype(F32)
        dlf = _dot(triu, dc, precision=HI) + carry_ref[...]
        carry_ref[...] += jnp.sum(dc, axis=0, keepdims=True)
        dz = dlf * jax.nn.sigmoid(-(f_ref[...] + fb_ref[...]))
        df_ref[...] = dz.astype(BF16)
        _acc_store(i, dfb_ref, jnp.sum(dz, axis=0, keepdims=True))

    rev = lambda i: (nb - 1 - i, 0)
    return _pcall(body, name="fox_prep_bwd", grid=(nb,),
                  in_specs=[pl.BlockSpec((tr, LANES), lambda i: (nb - 1 - i, cf)), _const((1, LANES)),
                            pl.BlockSpec((tr, LANES), rev)],
                  out_specs=[pl.BlockSpec((tr, LANES), rev), _const((1, LANES))],
                  out_shape=[jax.ShapeDtypeStruct((T, LANES), BF16), jax.ShapeDtypeStruct((1, LANES), F32)],
                  scratch_shapes=[pltpu.VMEM((1, LANES), F32)], compiler_params=_cparams(("arbitrary",)))(u, fb, dc)


def _attn_logits(q_ref, k_ref, c_ref, i, tq, te):
    s = _dot(q_ref[...].astype(BF16), k_ref[0:te, :].astype(BF16), "nt") * (FOX_HEAD_DIM ** -0.5) - c_ref[0, :, 0:te]
    row = i * tq + lax.broadcasted_iota(jnp.int32, (tq, te), 0)
    col = lax.broadcasted_iota(jnp.int32, (tq, te), 1)
    return jnp.where(col <= row, s, -1e30)


def _per_query_tile(i, nq, tq, fn):
    for ii in range(nq):
        pl.when(i == ii)(functools.partial(fn, (ii + 1) * tq))


def _attn_fwd(cfg, u, c_rows):
    T, FW, FH = cfg.T, cfg.FW, cfg.FH
    tq = min(256, T)
    dh = FOX_HEAD_DIM

    def body(q_ref, k_ref, v_ref, c_ref, o_ref, lse_ref):
        i = pl.program_id(1)

        def tile(te):
            s = _attn_logits(q_ref, k_ref, c_ref, i, tq, te)
            m = jnp.max(s, axis=1, keepdims=True)
            p = jnp.exp(s - m)
            l = jnp.sum(p, axis=1, keepdims=True)
            o_ref[...] = _dot(p.astype(BF16), v_ref[0:te, :].astype(BF16)) / l
            lse_ref[0] = m + jnp.log(l)

        _per_query_tile(i, T // tq, tq, tile)

    return _pcall(
        body, name="fox_attn_fwd", grid=(FH, T // tq),
        in_specs=[pl.BlockSpec((tq, dh), lambda h, i: (i, h)), pl.BlockSpec((T, dh), lambda h, i: (0, FH + h)),
                  pl.BlockSpec((T, dh), lambda h, i: (0, 2 * FH + h)), pl.BlockSpec((1, 1, T), lambda h, i: (h, 0, 0))],
        out_specs=[pl.BlockSpec((tq, dh), lambda h, i: (i, h)), pl.BlockSpec((1, tq, 1), lambda h, i: (h, i, 0))],
        out_shape=[jax.ShapeDtypeStruct((T, FW), F32), jax.ShapeDtypeStruct((FH, T, 1), F32)],
        compiler_params=_cparams(("parallel", "arbitrary")),
    )(u, u, u, c_rows)


def _attn_bwd(cfg, u, c_rows, lse, do):
    T, FW, FH = cfg.T, cfg.FW, cfg.FH
    tq = min(256, T)
    dh = FOX_HEAD_DIM
    scale = dh ** -0.5

    def body(q_ref, k_ref, v_ref, c_ref, lse_ref, do_ref, dq_ref, dk_ref, dv_ref, dcol_ref):
        i = pl.program_id(1)

        @pl.when(i == 0)
        def _():
            dk_ref[...] = jnp.zeros_like(dk_ref)
            dv_ref[...] = jnp.zeros_like(dv_ref)
            dcol_ref[...] = jnp.zeros_like(dcol_ref)

        def tile(te):
            s = _attn_logits(q_ref, k_ref, c_ref, i, tq, te)
            p = jnp.exp(s - lse_ref[0])
            do_v = do_ref[...]
            dp = _dot(do_v.astype(BF16), v_ref[0:te, :].astype(BF16), "nt")
            delta = jnp.sum(p * dp, axis=1, keepdims=True)
            ds = p * (dp - delta)
            ds16 = ds.astype(BF16)
            dq_ref[...] = (_dot(ds16, k_ref[0:te, :].astype(BF16)) * scale).astype(BF16)
            dk_ref[0:te, :] += _dot(ds16, q_ref[...].astype(BF16), "tn") * scale
            dv_ref[0:te, :] += _dot(p.astype(BF16), do_v.astype(BF16), "tn")
            dcol_ref[0, :, 0:te] += jnp.sum(ds, axis=0, keepdims=True)

        _per_query_tile(i, T // tq, tq, tile)

    qspec = pl.BlockSpec((tq, dh), lambda h, i: (i, h))
    return _pcall(
        body, name="fox_attn_bwd", grid=(FH, T // tq),
        in_specs=[qspec, pl.BlockSpec((T, dh), lambda h, i: (0, FH + h)),
                  pl.BlockSpec((T, dh), lambda h, i: (0, 2 * FH + h)), pl.BlockSpec((1, 1, T), lambda h, i: (h, 0, 0)),
                  pl.BlockSpec((1, tq, 1), lambda h, i: (h, i, 0)), qspec],
        out_specs=[qspec, pl.BlockSpec((T, dh), lambda h, i: (0, h)), pl.BlockSpec((T, dh), lambda h, i: (0, h)),
                   pl.BlockSpec((1, 1, T), lambda h, i: (h, 0, 0))],
        out_shape=[jax.ShapeDtypeStruct((T, FW), BF16), jax.ShapeDtypeStruct((T, FW), F32),
                   jax.ShapeDtypeStruct((T, FW), F32), jax.ShapeDtypeStruct((FH, 1, T), F32)],
        compiler_params=_cparams(("parallel", "arbitrary")),
    )(u, u, u, c_rows, lse, do)


def _head_indicators(cfg):
    ind = np.zeros((cfg.RW, LANES), np.float32)
    ind[np.arange(cfg.RW), np.arange(cfg.RW) // RWKV_HEAD_DIM] = 1.0
    pad = np.zeros((1, LANES), np.float32)
    pad[0, cfg.RH:] = 1.0
    return jnp.asarray(ind), jnp.asarray(ind.T.copy()), jnp.asarray(pad)


def _prep_fn(us_r, us_k, us_v, us_wd, us_ad, w0, w2p, a0, a2p, k_k, k_a, ind, ind_t, pad):
    wpre = w0 + _dot3(jnp.tanh(us_wd), w2p)
    w = -_softplus(-wpre) - 0.5
    lw = -jnp.exp(w)
    a = jax.nn.sigmoid(a0 + _dot3(us_ad, a2p))
    kk = us_k * k_k
    ss = _xdot(kk * kk, ind, ind_t) + pad
    inv = 1.0 / jnp.maximum(jnp.sqrt(ss), L2_EPS)
    kkn = kk * _xdot(inv, ind_t, ind)
    kp = us_k * (1.0 + (a - 1.0) * k_a)
    return us_r, lw, kp, us_v, -kkn, kkn * a


def _shifted(u, prev_row, mu, first):
    n = u.shape[0]
    rolled = pltpu.roll(u, 1, 0)
    row = lax.broadcasted_iota(jnp.int32, u.shape, 0)
    p0 = jnp.where(first, jnp.zeros_like(prev_row), prev_row)
    prev = jnp.where(row == 0, jnp.broadcast_to(p0, u.shape), rolled)
    return u + (prev - u) * mu, prev


def _rwkv_specs(cfg, tr):
    RW, LP = cfg.RW, cfg.LP
    base = cfg.o_rwkv // RW
    cols = [(RW, base), (RW, base + 1), (RW, base + 2), (RW, base + 3), (LP, cfg.o_wd // LP), (LP, cfg.o_ad // LP)]
    cur = [pl.BlockSpec((tr, w), (lambda i, cb=cb: (i, cb))) for w, cb in cols]
    prv = [pl.BlockSpec((8, w), (lambda i, cb=cb: (jnp.maximum(i * (tr // 8) - 1, 0), cb))) for w, cb in cols]
    return cols, cur, prv


def _mu_pieces(cfg, mu_ref):
    RW, LP = cfg.RW, cfg.LP
    offs = [0, RW, 2 * RW, 3 * RW, 4 * RW, 4 * RW + LP, 4 * RW + 2 * LP]
    return [mu_ref[:, offs[j]:offs[j + 1]] for j in range(6)]


def _rwkv_prep_fwd(cfg, u, mu, w0, w2p, a0, a2p, k_k, k_a):
    T, RW, LP, tr = cfg.T, cfg.RW, cfg.LP, cfg.tr
    ind, ind_t, pad = _head_indicators(cfg)
    cols, cur, prv = _rwkv_specs(cfg, tr)

    def body(*refs):
        u_refs, p_refs = refs[0:6], refs[6:12]
        mu_ref, w0_ref, w2_ref, a0_ref, a2_ref, kk_ref, ka_ref, ind_ref, indt_ref, pad_ref = refs[12:22]
        outs = refs[22:]
        first = pl.program_id(0) == 0
        mus = _mu_pieces(cfg, mu_ref)
        us = [_shifted(u_refs[j][...], p_refs[j][7:8, :], mus[j], first)[0] for j in range(6)]
        res = _prep_fn(us[0], us[1], us[2], us[4], us[5], w0_ref[...], w2_ref[...], a0_ref[...], a2_ref[...],
                       kk_ref[...], ka_ref[...], ind_ref[...], indt_ref[...], pad_ref[...])
        for j in range(6):
            outs[j][...] = res[j]
        outs[6][...] = us[3]

    consts = [mu, w0, w2p, a0, a2p, k_k, k_a, ind, ind_t, pad]
    return _pcall(body, name="rwkv_prep_fwd", grid=(T // tr,),
                  in_specs=cur + prv + [_const(c.shape) for c in consts],
                  out_specs=[_tile(tr, RW)] * 7, out_shape=[jax.ShapeDtypeStruct((T, RW), F32)] * 7,
                  compiler_params=_cparams(("parallel",)))(*([u] * 12), *consts)


def _rwkv_prep_bwd(cfg, u, mu, w0, w2p, a0, a2p, k_k, k_a, cots, dzb):
    T, RW, LP = cfg.T, cfg.RW, cfg.LP
    tr = min(128, T)
    ind, ind_t, pad = _head_indicators(cfg)
    cols, cur, prv = _rwkv_specs(cfg, tr)
    rseg = cfg.rseg

    def body(*refs):
        u_refs, p_refs = refs[0:6], refs[6:12]
        mu_ref, w0_ref, w2_ref, a0_ref, a2_ref, kk_ref, ka_ref, ind_ref, indt_ref, pad_ref = refs[12:22]
        cot_refs, dzb_ref = refs[22:28], refs[28]
        dus_ref, dmu_ref, dw0_ref, dw2_ref, da0_ref, da2_ref, dkk_ref, dka_ref = refs[29:]
        i = pl.program_id(0)
        first = i == 0
        mus = _mu_pieces(cfg, mu_ref)
        sh = [_shifted(u_refs[j][...], p_refs[j][7:8, :], mus[j], first) for j in range(6)]
        us = [s[0] for s in sh]
        fn = functools.partial(_prep_fn, ind=ind_ref[...], ind_t=indt_ref[...], pad=pad_ref[...])
        _, vjp = jax.vjp(fn, us[0], us[1], us[2], us[4], us[5], w0_ref[...], w2_ref[...], a0_ref[...], a2_ref[...],
                         kk_ref[...], ka_ref[...])
        d = vjp(tuple(c[...] for c in cot_refs))
        dus = [d[0], d[1], d[2], dzb_ref[...], d[3], d[4]]
        offs = [0, RW, 2 * RW, 3 * RW, 4 * RW, 4 * RW + LP, 4 * RW + 2 * LP]
        for j in range(6):
            dus_ref[:, offs[j]:offs[j + 1]] = dus[j]
            dmu_j = jnp.sum(dus[j] * (sh[j][1] - u_refs[j][...]), axis=0, keepdims=True)

            @pl.when(first)
            def _(j=j, dmu_j=dmu_j):
                dmu_ref[:, offs[j]:offs[j + 1]] = dmu_j

            @pl.when(i > 0)
            def _(j=j, dmu_j=dmu_j):
                dmu_ref[:, offs[j]:offs[j + 1]] += dmu_j
        for ref, val in zip((dw0_ref, dw2_ref, da0_ref, da2_ref, dkk_ref, dka_ref), d[5:11]):
            _acc_store(i, ref, val)

    consts = [mu, w0, w2p, a0, a2p, k_k, k_a, ind, ind_t, pad]
    vec = jax.ShapeDtypeStruct((1, RW), F32)
    mat = jax.ShapeDtypeStruct((LP, RW), F32)
    return _pcall(body, name="rwkv_prep_bwd", grid=(T // tr,),
                  in_specs=cur + prv + [_const(c.shape) for c in consts] + [_tile(tr, RW)] * 7,
                  out_specs=[_tile(tr, rseg), _const((1, rseg)), _const((1, RW)), _const((LP, RW)), _const((1, RW)),
                             _const((LP, RW)), _const((1, RW)), _const((1, RW))],
                  out_shape=[jax.ShapeDtypeStruct((T, rseg), F32), jax.ShapeDtypeStruct((1, rseg), F32),
                             vec, mat, vec, mat, vec, vec],
                  compiler_params=_cparams(("arbitrary",)))(*([u] * 12), *consts, *cots, dzb)


def _shift_bwd(cfg, dus, mu):
    T, tr, rseg = cfg.T, cfg.tr, cfg.rseg
    nb = T // tr

    def body(d_ref, n_ref, mu_ref, du_ref):
        d = d_ref[...]
        rolled = pltpu.roll(d, tr - 1, 0)
        row = lax.broadcasted_iota(jnp.int32, d.shape, 0)
        n0 = jnp.where(pl.program_id(0) == nb - 1, jnp.zeros_like(n_ref[0:1, :]), n_ref[0:1, :])
        nxt = jnp.where(row == tr - 1, jnp.broadcast_to(n0, d.shape), rolled)
        mu_v = mu_ref[...]
        du_ref[...] = (d * (1.0 - mu_v) + nxt * mu_v).astype(BF16)

    return _pcall(body, name="shift_bwd", grid=(nb,),
                  in_specs=[_tile(tr, rseg),
                            pl.BlockSpec((8, rseg), lambda i: (jnp.minimum((i + 1) * (tr // 8), T // 8 - 1), 0)),
                            _const((1, rseg))],
                  out_specs=_tile(tr, rseg), out_shape=jax.ShapeDtypeStruct((T, rseg), BF16),
                  compiler_params=_cparams(("parallel",)))(dus, dus, mu)


def _chunk_local(r, lw, k, v, a, b):
    H, C, K = r.shape
    row = lax.broadcasted_iota(jnp.int32, (C, C), 0)
    col = lax.broadcasted_iota(jnp.int32, (C, C), 1)
    incl = jnp.broadcast_to((row >= col).astype(F32)[None], (H, C, C))
    strict = (row > col)[None]
    lower = (row >= col)[None]
    eye = (row == col)[None]
    zero = jnp.zeros((), F32)
    L = _bdot(incl, lw, 2, 1)
    LC = jnp.sum(lw, axis=1, keepdims=True)
    eL = jnp.exp(L)
    eLn = jnp.exp(-L)
    at = a * jnp.exp(L - lw)
    rt = r * eL
    bt = b * eLn
    kt = k * eLn
    eR = jnp.exp(LC - L)
    bh = b * eR
    kh = k * eR
    gram = functools.partial(_bdot, passes=SCAN_PASSES[0])
    inv = functools.partial(_bdot, passes=SCAN_PASSES[1])
    app = functools.partial(_bdot, passes=SCAN_PASSES[2])
    n_ab = jnp.where(strict, gram(at, bt, 2, 2), zero)
    n_ak = jnp.where(strict, gram(at, kt, 2, 2), zero)
    m_rb = jnp.where(lower, gram(rt, bt, 2, 2), zero)
    m_rk = jnp.where(lower, gram(rt, kt, 2, 2), zero)
    M = n_ab
    P = jnp.where(eye, 1.0, zero) + n_ab
    for _ in range(1, max(1, int(np.ceil(np.log2(C))))):
        M = inv(M, M, 2, 1)
        P = P + inv(M, P, 2, 1)
    W = app(P, at, 2, 1)
    Uloc = app(P, app(n_ak, v, 2, 1), 2, 1)
    Q = rt + app(m_rb, W, 2, 1)
    Yloc = app(m_rb, Uloc, 2, 1) + app(m_rk, v, 2, 1)
    A = jnp.where(eye, jnp.exp(LC), zero) + app(W, bh, 1, 1)
    Sloc = app(Uloc, bh, 1, 1) + app(v, kh, 1, 1)
    return Q, Yloc, A, Sloc


def _scan_local_specs(cfg):
    N, HB = RWKV_HEAD_DIM, cfg.hb
    grid = (cfg.RH // HB, cfg.T // cfg.C)
    seq = pl.BlockSpec((HB, cfg.C, N), lambda h, j: (h, j, 0))
    mat = pl.BlockSpec((HB, 1, N, N), lambda h, j: (h, j, 0, 0))
    return grid, seq, mat


def _scan_local_fwd(cfg, seqs):
    T, RH, N = cfg.T, cfg.RH, RWKV_HEAD_DIM
    grid, seq, mat = _scan_local_specs(cfg)

    def body(r_ref, lw_ref, k_ref, v_ref, a_ref, b_ref, q_ref, yl_ref, a_out, sl_ref):
        Q, Yloc, A, Sloc = _chunk_local(r_ref[...], lw_ref[...], k_ref[...], v_ref[...], a_ref[...], b_ref[...])
        q_ref[...] = Q
        yl_ref[...] = Yloc
        a_out[:, 0] = A
        sl_ref[:, 0] = Sloc

    sq = jax.ShapeDtypeStruct((RH, T, N), F32)
    mt = jax.ShapeDtypeStruct((RH, T // cfg.C, N, N), F32)
    return _pcall(body, name="rwkv_scan_local_fwd", grid=grid, in_specs=[seq] * 6, out_specs=[seq, seq, mat, mat],
                  out_shape=[sq, sq, mt, mt], compiler_params=_cparams(("parallel", "parallel")))(*seqs)


def _scan_local_bwd(cfg, seqs, dq, dyl, da, dsl):
    T, RH, N = cfg.T, cfg.RH, RWKV_HEAD_DIM
    grid, seq, mat = _scan_local_specs(cfg)

    def body(r_ref, lw_ref, k_ref, v_ref, a_ref, b_ref, dq_ref, dyl_ref, da_ref, dsl_ref, *outs):
        _, vjp = jax.vjp(_chunk_local, r_ref[...], lw_ref[...], k_ref[...], v_ref[...], a_ref[...], b_ref[...])
        d = vjp((dq_ref[...], dyl_ref[...], da_ref[:, 0], dsl_ref[:, 0]))
        for j in range(6):
            outs[j][...] = d[j]

    return _pcall(body, name="rwkv_scan_local_bwd", grid=grid, in_specs=[seq] * 6 + [seq, seq, mat, mat],
                  out_specs=[seq] * 6, out_shape=[jax.ShapeDtypeStruct((RH, T, N), F32)] * 6,
                  compiler_params=_cparams(("parallel", "parallel")))(*seqs, dq, dyl, da, dsl)


def _scan_carry_specs(cfg, rev):
    N, RH, C, nc = RWKV_HEAD_DIM, cfg.RH, cfg.C, cfg.T // cfg.C
    at = (lambda j: nc - 1 - j) if rev else (lambda j: j)
    seq = pl.BlockSpec((RH, C, N), lambda j: (0, at(j), 0))
    mat = pl.BlockSpec((RH, 1, N, N), lambda j: (0, at(j), 0, 0))
    return nc, seq, mat


def _scan_carry_fwd(cfg, q, yloc, a, sloc):
    T, RH, N = cfg.T, cfg.RH, RWKV_HEAD_DIM
    nc, seq, mat = _scan_carry_specs(cfg, False)

    def body(q_ref, yl_ref, a_ref, sl_ref, y_ref, ck_ref, s_ref):
        @pl.when(pl.program_id(0) == 0)
        def _():
            s_ref[...] = jnp.zeros_like(s_ref)

        S = s_ref[...]
        ck_ref[:, 0] = S
        y_ref[...] = _bdot(q_ref[...], S, 2, 2) + yl_ref[...]
        s_ref[...] = _bdot(S, a_ref[:, 0], 2, 1) + sl_ref[:, 0]

    return _pcall(body, name="rwkv_scan_carry_fwd", grid=(nc,), in_specs=[seq, seq, mat, mat], out_specs=[seq, mat],
                  out_shape=[jax.ShapeDtypeStruct((RH, T, N), F32), jax.ShapeDtypeStruct((RH, nc, N, N), F32)],
                  scratch_shapes=[pltpu.VMEM((RH, N, N), F32)],
                  compiler_params=_cparams(("arbitrary",)))(q, yloc, a, sloc)


def _scan_carry_bwd(cfg, q, a, ckpt, dy):
    T, RH, N = cfg.T, cfg.RH, RWKV_HEAD_DIM
    nc, seq, mat = _scan_carry_specs(cfg, True)

    def body(q_ref, a_ref, ck_ref, dy_ref, dq_ref, da_ref, dsl_ref, ds_ref):
        @pl.when(pl.program_id(0) == 0)
        def _():
            ds_ref[...] = jnp.zeros_like(ds_ref)

        S, dS, dY = ck_ref[:, 0], ds_ref[...], dy_ref[...]
        dq_ref[...] = _bdot(dY, S, 2, 1)
        da_ref[:, 0] = _bdot(S, dS, 1, 1)
        dsl_ref[:, 0] = dS
        ds_ref[...] = _bdot(dS, a_ref[:, 0], 2, 2) + _bdot(dY, q_ref[...], 1, 1)

    mt = jax.ShapeDtypeStruct((RH, nc, N, N), F32)
    return _pcall(body, name="rwkv_scan_carry_bwd", grid=(nc,), in_specs=[seq, mat, mat, seq],
                  out_specs=[seq, mat, mat], out_shape=[jax.ShapeDtypeStruct((RH, T, N), F32), mt, mt],
                  scratch_shapes=[pltpu.VMEM((RH, N, N), F32)],
                  compiler_params=_cparams(("arbitrary",)))(q, a, ckpt, dy)


def _post_fn(y, r, kp, v, zb, ln_w, ln_b, rk, ind, ind_t):
    n = float(RWKV_HEAD_DIM)
    mu = _xdot(_xdot(y, ind, ind_t) / n, ind_t, ind)
    yc = y - mu
    var = _xdot(yc * yc, ind, ind_t) / n
    rstd = _xdot(lax.rsqrt(var + GN_EPS), ind_t, ind)
    yn = yc * rstd * ln_w + ln_b
    bonus = _xdot(_xdot(r * kp * rk, ind, ind_t), ind_t, ind) * v
    return (yn + bonus) * _silu(zb)


def _rwkv_post_fwd(cfg, y, r, kp, v, zb, ln_w, ln_b, rk):
    T, RW, tr = cfg.T, cfg.RW, cfg.tr
    ind, ind_t, _ = _head_indicators(cfg)

    def body(y_ref, r_ref, k_ref, v_ref, z_ref, lw_ref, lb_ref, rk_ref, ind_ref, indt_ref, ob_ref):
        ob_ref[...] = _post_fn(y_ref[...], r_ref[...], k_ref[...], v_ref[...], z_ref[...], lw_ref[...], lb_ref[...],
                               rk_ref[...], ind_ref[...], indt_ref[...]).astype(BF16)

    consts = [ln_w, ln_b, rk, ind, ind_t]
    return _pcall(body, name="rwkv_post_fwd", grid=(T // tr,),
                  in_specs=[_tile(tr, RW)] * 5 + [_const(c.shape) for c in consts],
                  out_specs=_tile(tr, RW), out_shape=jax.ShapeDtypeStruct((T, RW), BF16),
                  compiler_params=_cparams(("parallel",)))(y, r, kp, v, zb, *consts)


def _rwkv_post_bwd(cfg, y, r, kp, v, zb, ln_w, ln_b, rk, dob):
    T, RW = cfg.T, cfg.RW
    tr = min(128, T)
    ind, ind_t, _ = _head_indicators(cfg)

    def body(y_ref, r_ref, k_ref, v_ref, z_ref, lw_ref, lb_ref, rk_ref, ind_ref, indt_ref, dob_ref,
             dy_ref, dr_ref, dk_ref, dv_ref, dz_ref, dlw_ref, dlb_ref, drk_ref):
        fn = functools.partial(_post_fn, ind=ind_ref[...], ind_t=indt_ref[...])
        _, vjp = jax.vjp(fn, y_ref[...], r_ref[...], k_ref[...], v_ref[...], z_ref[...], lw_ref[...], lb_ref[...],
                         rk_ref[...])
        d = vjp(dob_ref[...])
        for ref, val in zip((dy_ref, dr_ref, dk_ref, dv_ref, dz_ref), d[:5]):
            ref[...] = val
        i = pl.program_id(0)
        for ref, val in zip((dlw_ref, dlb_ref, drk_ref), d[5:8]):
            _acc_store(i, ref, val)

    consts = [ln_w, ln_b, rk, ind, ind_t]
    vec = jax.ShapeDtypeStruct((1, RW), F32)
    return _pcall(body, name="rwkv_post_bwd", grid=(T // tr,),
                  in_specs=[_tile(tr, RW)] * 5 + [_const(c.shape) for c in consts] + [_tile(tr, RW)],
                  out_specs=[_tile(tr, RW)] * 5 + [_const((1, RW))] * 3,
                  out_shape=[jax.ShapeDtypeStruct((T, RW), F32)] * 5 + [vec] * 3,
                  compiler_params=_cparams(("arbitrary",)))(y, r, kp, v, zb, *consts, dob)


def _adamw_math(w, g, m, v):
    m = ADAM_B1 * m + (1.0 - ADAM_B1) * g
    v = ADAM_B2 * v + (1.0 - ADAM_B2) * (g * g)
    m_hat = m / (1.0 - ADAM_B1 ** ADAM_STEP)
    v_hat = v / (1.0 - ADAM_B2 ** ADAM_STEP)
    delta = -ADAM_LR * (m_hat / (jnp.sqrt(v_hat) + ADAM_EPS) + ADAM_WD * w)
    return delta, m, v


def _adamw(name, w, g, m, v, copy_grad=False):
    R, Cc = w.shape
    Rp = -(-R // 8) * 8
    tr = Rp
    for nb in range(1, Rp // 8 + 1):
        if (Rp // 8) % nb == 0 and (Rp // nb) * Cc * 4 <= 2 * 1024 * 1024:
            tr = Rp // nb
            break

    def body(w_ref, g_ref, m_ref, v_ref, d_ref, nm_ref, nv_ref, *g_out):
        g_v = g_ref[...]
        d, nm, nv = _adamw_math(w_ref[...], g_v, m_ref[...], v_ref[...])
        d_ref[...] = d
        nm_ref[...] = nm
        nv_ref[...] = nv
        if copy_grad:
            g_out[0][...] = g_v

    spec = _tile(tr, Cc)
    n_out = 4 if copy_grad else 3
    return _pcall(body, name=name, grid=(Rp // tr,), in_specs=[spec] * 4, out_specs=[spec] * n_out,
                  out_shape=[jax.ShapeDtypeStruct((R, Cc), F32)] * n_out,
                  compiler_params=_cparams(("parallel",)))(w, g, m, v)


def _row_tile(R, Cc, itemsize, budget=2 * 1024 * 1024):
    for nb in range(1, R // 16 + 1):
        if R % nb == 0 and (R // nb) % 16 == 0 and (R // nb) * Cc * itemsize <= budget:
            return R // nb
    return R


def _add_halves(name, gs, r1, c_idx):
    _, R, Cc = gs.shape
    half = R // 2
    tr = _row_tile(half, Cc, 4)
    nb = half // tr

    def body(c_ref, g_ref, r_ref, o_ref):
        o_ref[...] = (g_ref[...].astype(F32) + r_ref[...].astype(F32)).astype(BF16)

    grid_spec = pltpu.PrefetchScalarGridSpec(
        num_scalar_prefetch=1, grid=(N_CHIPS, nb),
        in_specs=[pl.BlockSpec((1, tr, Cc), lambda s, i, c: (s, c[0] * nb + i, 0)),
                  pl.BlockSpec((1, tr, Cc), lambda s, i, c: (s, i, 0))],
        out_specs=pl.BlockSpec((1, tr, Cc), lambda s, i, c: (s, i, 0)))
    return _pcall(body, name=name, grid_spec=grid_spec, out_shape=jax.ShapeDtypeStruct((N_CHIPS, half, Cc), BF16),
                  compiler_params=_cparams(("parallel", "parallel")))(c_idx, gs, r1)


def _sum_slots(name, r2):
    S, R, Cc = r2.shape
    tr = _row_tile(R, Cc, 4 * S // 2 if r2.dtype == BF16 else 4 * S)

    def body(r_ref, o_ref):
        acc = r_ref[0].astype(F32)
        for s in range(1, S):
            acc = acc + r_ref[s].astype(F32)
        o_ref[...] = acc

    return _pcall(body, name=name, grid=(R // tr,), in_specs=[pl.BlockSpec((S, tr, Cc), lambda i: (0, i, 0))],
                  out_specs=_tile(tr, Cc), out_shape=jax.ShapeDtypeStruct((R, Cc), F32),
                  compiler_params=_cparams(("parallel",)))(r2)


def _sum_chips(name, recv, own, place):
    S, H, Cc = recv.shape
    tr = _row_tile(H, Cc, 4, 1024 * 1024)
    nb = H // tr

    def body(p_ref, r_ref, own_ref, o_ref):
        s = pl.program_id(1)
        me = p_ref[0]

        @pl.when(s == 0)
        def _():
            o_ref[...] = jnp.zeros_like(o_ref)

        @pl.when(s == me)
        def _():
            o_ref[...] += own_ref[0].astype(F32)

        @pl.when(s != me)
        def _():
            o_ref[...] += r_ref[0].astype(F32)

    grid_spec = pltpu.PrefetchScalarGridSpec(
        num_scalar_prefetch=1, grid=(nb, S),
        in_specs=[pl.BlockSpec((1, tr, Cc), lambda i, s, p: (jnp.where(s == p[0], (s + 1) % S, s), i, 0)),
                  pl.BlockSpec((1, tr, Cc), lambda i, s, p: (p[0], i, 0))],
        out_specs=pl.BlockSpec((tr, Cc), lambda i, s, p: (p[1] * nb + i, 0)))
    return _pcall(body, name=name, grid_spec=grid_spec, out_shape=jax.ShapeDtypeStruct((2 * H, Cc), F32),
                  compiler_params=_cparams(("parallel", "arbitrary")))(place, recv, own)


def _cast_bf16(name, w):
    R, Cc = w.shape
    tr = _row_tile(R, Cc, 4)

    def body(w_ref, o_ref):
        o_ref[...] = w_ref[...].astype(BF16)

    return _pcall(body, name=name, grid=(R // tr,), in_specs=[_tile(tr, Cc)], out_specs=_tile(tr, Cc),
                  out_shape=jax.ShapeDtypeStruct((R, Cc), BF16), compiler_params=_cparams(("parallel",)))(w)


_ANY = pl.BlockSpec(memory_space=pl.ANY)


def _place():
    x, y, c = lax.axis_index("x"), lax.axis_index("y"), lax.axis_index("c")
    others = [(1 - x, y), (x, 1 - y), (1 - x, 1 - y)]
    return x, y, c, others


def _gather_weights(shards):
    n = len(shards)
    halves = [s.shape[0] // 2 for s in shards]

    def body(*refs):
        ins, outs = refs[:n], refs[n:2 * n]
        send_sems, recv_sems = refs[2 * n:]
        x, y, c, others = _place()
        me = 2 * x + y

        def rows(k, ref, chip, hc):
            return ref.at[chip, pl.ds(hc * halves[k], halves[k]), :]

        def remote(k, j, src, dst, to):
            return pltpu.make_async_remote_copy(src_ref=src, dst_ref=dst, send_sem=send_sems.at[6 * k + j],
                                                recv_sem=recv_sems.at[6 * k + j], device_id=to, device_id_type=MESH)

        first, passed = [], []
        for k in range(n):
            mine = ins[k].at[pl.ds(c * halves[k], halves[k]), :]
            for j, (px, py) in enumerate(others):
                cp = remote(k, j, mine, rows(k, outs[k], me, c), (px, py, c))
                cp.start()
                first.append(cp)
        for k in range(n):
            for j, (px, py) in enumerate(others):
                land = rows(k, outs[k], 2 * px + py, c)
                remote(k, j, land, land, (x, y, c)).wait_recv()
                cp = remote(k, 3 + j, land, land, (x, y, 1 - c))
                cp.start()
                passed.append(cp)
        for k in range(n):
            for j, (px, py) in enumerate(others):
                land = rows(k, outs[k], 2 * px + py, 1 - c)
                remote(k, 3 + j, land, land, (x, y, c)).wait_recv()
        for cp in first + passed:
            cp.wait_send()

    return _pcall(
        body, name="gather_weights", in_specs=[_ANY] * n, out_specs=[_ANY] * n,
        out_shape=[jax.ShapeDtypeStruct((N_CHIPS,) + s.shape, s.dtype) for s in shards],
        scratch_shapes=[pltpu.SemaphoreType.DMA((6 * n,)), pltpu.SemaphoreType.DMA((6 * n,))],
    )(*shards)


def _exchange_halves(grads):
    n = len(grads)
    halves = [g.shape[1] // 2 for g in grads]

    def body(*refs):
        ins, outs = refs[:n], refs[n:2 * n]
        send_sems, recv_sems = refs[2 * n:]
        x, y, c, _ = _place()
        cps = []
        for k in range(n):
            src = ins[k].at[:, pl.ds((1 - c) * halves[k], halves[k]), :]
            cp = pltpu.make_async_remote_copy(src_ref=src, dst_ref=outs[k], send_sem=send_sems.at[k],
                                              recv_sem=recv_sems.at[k], device_id=(x, y, 1 - c), device_id_type=MESH)
            cp.start()
            cps.append(cp)
        for cp in cps:
            cp.wait()

    return _pcall(
        body, name="exchange_halves", in_specs=[_ANY] * n, out_specs=[_ANY] * n,
        out_shape=[jax.ShapeDtypeStruct((N_CHIPS, h) + g.shape[2:], g.dtype) for g, h in zip(grads, halves)],
        scratch_shapes=[pltpu.SemaphoreType.DMA((n,)), pltpu.SemaphoreType.DMA((n,))],
    )(*grads)


def _scatter_to_owners(chip_sums, small):
    n = len(chip_sums)

    def body(*refs):
        ins, small_in = refs[:n], refs[n]
        outs, small_out = refs[n + 1:2 * n + 1], refs[2 * n + 1]
        send_sems, recv_sems, local_sem, ssend, srecv = refs[2 * n + 2:]
        x, y, c, others = _place()
        me = 2 * x + y
        dev = 2 * me + c
        local = pltpu.make_async_copy(small_in, small_out.at[dev], local_sem)
        local.start()
        sends = []
        for k in range(n):
            for j, (px, py) in enumerate(others):
                cp = pltpu.make_async_remote_copy(
                    src_ref=ins[k].at[2 * px + py], dst_ref=outs[k].at[me], send_sem=send_sems.at[3 * k + j],
                    recv_sem=recv_sems.at[3 * k + j], device_id=(px, py, c), device_id_type=MESH)
                cp.start()
                sends.append(cp)
        rel = [(dx, dy, dc) for dx in (0, 1) for dy in (0, 1) for dc in (0, 1)][1:]
        for r, (dx, dy, dc) in enumerate(rel):
            to = (x ^ dx, y ^ dy, c ^ dc)
            cp = pltpu.make_async_remote_copy(src_ref=small_in, dst_ref=small_out.at[dev], send_sem=ssend.at[r],
                                              recv_sem=srecv.at[r], device_id=to, device_id_type=MESH)
            cp.start()
            sends.append(cp)
        for k in range(n):
            for j, (px, py) in enumerate(others):
                land = outs[k].at[2 * px + py]
                pltpu.make_async_remote_copy(src_ref=land, dst_ref=land, send_sem=send_sems.at[3 * k + j],
                                             recv_sem=recv_sems.at[3 * k + j], device_id=(x, y, c),
                                             device_id_type=MESH).wait_recv()
        for r, (dx, dy, dc) in enumerate(rel):
            land = small_out.at[4 * (x ^ dx) + 2 * (y ^ dy) + (c ^ dc)]
            pltpu.make_async_remote_copy(src_ref=land, dst_ref=land, send_sem=ssend.at[r], recv_sem=srecv.at[r],
                                         device_id=(x, y, c), device_id_type=MESH).wait_recv()
        for cp in sends:
            cp.wait_send()
        local.wait()

    return _pcall(
        body, name="scatter_to_owners", in_specs=[_ANY] * (n + 1), out_specs=[_ANY] * (n + 1),
        out_shape=[jax.ShapeDtypeStruct(g.shape, g.dtype) for g in chip_sums]
        + [jax.ShapeDtypeStruct((N_DEV,) + small.shape, small.dtype)],
        scratch_shapes=[pltpu.SemaphoreType.DMA((3 * n,)), pltpu.SemaphoreType.DMA((3 * n,)),
                        pltpu.SemaphoreType.DMA, pltpu.SemaphoreType.DMA((7,)), pltpu.SemaphoreType.DMA((7,))],
    )(*chip_sums, small)


def _join_halves(fulls):
    n = len(fulls)
    hs = [f.shape[0] // 2 for f in fulls]

    def body(*refs):
        ins, outs = refs[:n], refs[n:2 * n]
        send_sems, recv_sems = refs[2 * n:]
        x, y, c, _ = _place()
        cps = []
        for k in range(n):
            mine = pl.ds(c * hs[k], hs[k])
            cp = pltpu.make_async_remote_copy(src_ref=ins[k].at[mine, :], dst_ref=outs[k].at[mine, :],
                                              send_sem=send_sems.at[k], recv_sem=recv_sems.at[k],
                                              device_id=(x, y, 1 - c), device_id_type=MESH)
            cp.start()
            cps.append(cp)
        for k in range(n):
            land = outs[k].at[pl.ds((1 - c) * hs[k], hs[k]), :]
            pltpu.make_async_remote_copy(src_ref=land, dst_ref=land, send_sem=send_sems.at[k],
                                         recv_sem=recv_sems.at[k], device_id=(x, y, c), device_id_type=MESH).wait_recv()
        for cp in cps:
            cp.wait_send()

    return _pcall(
        body, name="join_halves", in_specs=[_ANY] * n, out_specs=[_ANY] * n,
        out_shape=[jax.ShapeDtypeStruct(f.shape, f.dtype) for f in fulls],
        input_output_aliases={k: k for k in range(n)},
        scratch_shapes=[pltpu.SemaphoreType.DMA((n,)), pltpu.SemaphoreType.DMA((n,))],
    )(*fulls)


def _heads(cfg, a):
    return a.reshape(cfg.T, cfg.RH, RWKV_HEAD_DIM).transpose(1, 0, 2)


def _unheads(cfg, a):
    return a.transpose(1, 0, 2).reshape(cfg.T, cfg.RW)


def _local_step(cfg, x2, target, norm_gain, w_my, fb, mu_g, w0, w2, a0, a2, k_k, k_a, r_k, ln_w, ln_b, wpf, wpr, wout,
                fng):
    T, D, FW, FH, RW, RH, LP, lora = cfg.T, cfg.D, cfg.FW, cfg.FH, cfg.RW, cfg.RH, cfg.LP, cfg.lora
    fb_p = jnp.pad(fb, ((0, 0), (0, LANES - FH)))
    mu = _rwkv_vec_to_my(cfg, mu_g)
    w2p = jnp.pad(w2, ((0, LP - lora), (0, 0)))
    a2p = jnp.pad(a2, ((0, LP - lora), (0, 0)))
    rk = r_k.reshape(1, RW)
    tm = min(1024, T)

    h = _rms_fwd(cfg, x2, norm_gain)
    u = _mm("in_proj", h, w_my, "nn", F32, tm, cfg.tn, 512)
    c_cols = _fox_prep(cfg, u, fb_p)
    c_rows = c_cols[:, :FH].T.reshape(FH, 1, T)
    o, lse = _attn_fwd(cfg, u, c_rows)
    oa = _gate_a_fwd(cfg, o, u)
    prep = _rwkv_prep_fwd(cfg, u, mu, w0, w2p, a0, a2p, k_k, k_a)
    r, lw, kp, v, an, b, zb = prep
    seqs = [_heads(cfg, t) for t in (r, lw, kp, v, an, b)]
    q_s, yloc, a_m, sloc = _scan_local_fwd(cfg, seqs)
    y_h, ckpt = _scan_carry_fwd(cfg, q_s, yloc, a_m, sloc)
    y = _unheads(cfg, y_h)
    ob = _rwkv_post_fwd(cfg, y, r, kp, v, zb, ln_w, ln_b, rk)
    pa = _mm("proj_fox", oa, wpf, "nn", F32, tm, 1024, 512)
    pb = _mm("proj_rwkv", ob, wpr, "nn", F32, tm, 1024, 512)
    m = _merge_fwd(cfg, pa, pb, u)
    mo = _mm("out_proj", m, wout, "nn", F32, tm, 1024, 512)
    loss8, dres, dres16, d_fng = _final(cfg, x2, mo, fng.reshape(1, D), target)

    dm = _mm("out_proj_dx", dres16, wout, "nt", F32, tm, 1024, 512)
    d_wout = _mm("out_proj_dw", m, dres16, "tn", BF16, 1024, 1024, 512)
    dpa, dpb, dgate = _merge_bwd(cfg, pa, pb, u, dm)
    doa = _mm("proj_fox_dx", dpa, wpf, "nt", F32, tm, 1024, 512)
    d_wpf = _mm("proj_fox_dw", oa, dpa, "tn", BF16, 1024, 1024, 512)
    dob = _mm("proj_rwkv_dx", dpb, wpr, "nt", F32, tm, 1024, 512)
    d_wpr = _mm("proj_rwkv_dw", ob, dpb, "tn", BF16, 1024, 1024, 512)

    do, dza = _gate_a_bwd(cfg, o, u, doa)
    dq, dk, dv, dcol = _attn_bwd(cfg, u, c_rows, lse, do)
    dc = jnp.pad(-dcol.reshape(FH, T).T, ((0, 0), (0, LANES - FH)))
    df, d_fb = _fox_prep_bwd(cfg, u, fb_p, dc)

    dy, dr_p, dk_p, dv_p, dzb, d_lnw, d_lnb, d_rk = _rwkv_post_bwd(cfg, y, r, kp, v, zb, ln_w, ln_b, rk, dob)
    dy_h = _heads(cfg, dy)
    dq_s, da_m, dsl = _scan_carry_bwd(cfg, q_s, a_m, ckpt, dy_h)
    dseq = _scan_local_bwd(cfg, seqs, dq_s, dy_h, da_m, dsl)
    dr_s, dlw_s, dk_s, dv_s, da_s, db_s = [_unheads(cfg, t) for t in dseq]
    cots = [dr_s + dr_p, dlw_s, dk_s + dk_p, dv_s + dv_p, da_s, db_s]
    dus, d_mu, d_w0, d_w2p, d_a0, d_a2p, d_kk, d_ka = _rwkv_prep_bwd(cfg, u, mu, w0, w2p, a0, a2p, k_k, k_a, cots, dzb)
    du_rwkv = _shift_bwd(cfg, dus, mu)

    pad_f = jnp.zeros((T, cfg.ncol - cfg.o_ad - LP), BF16)
    du = jnp.concatenate([dq, dk.astype(BF16), dv.astype(BF16), dza, du_rwkv[:, :4 * RW], dgate, df,
                          du_rwkv[:, 4 * RW:], pad_f], axis=1)
    dh = _mm("in_proj_dx", du, w_my, "nt", F32, tm, 1024, cfg.tn)
    d_wmy = _mm("in_proj_dw", du, h, "tn", BF16, cfg.tn, 1024, 512)
    gx, d_ng = _rms_bwd(cfg, x2, norm_gain, dh, dres)

    small = dict(norm_gain=d_ng, fox_forget_bias=d_fb[:, :FH], rwkv_shift_mix=_rwkv_vec_from_my(cfg, d_mu),
                 rwkv_w0=d_w0, rwkv_a0=d_a0, rwkv_k_k=d_kk, rwkv_k_a=d_ka, rwkv_r_k=d_rk, rwkv_ln_w=d_lnw,
                 rwkv_ln_b=d_lnb, final_norm_gain=d_fng)
    big = dict(w_in=d_wmy, rwkv_w2=d_w2p[:lora], rwkv_a2=d_a2p[:lora], w_proj_fox=d_wpf, w_proj_rwkv=d_wpr,
               w_out=d_wout)
    return loss8[0, 0], gx, small, big


_SMALL = ["norm_gain", "fox_forget_bias", "rwkv_shift_mix", "rwkv_w0", "rwkv_a0", "rwkv_k_k", "rwkv_k_a", "rwkv_r_k",
          "rwkv_ln_w", "rwkv_ln_b", "final_norm_gain"]
_WEIGHTS = ["norm_gain", "w_in", "fox_forget_bias", "rwkv_shift_mix", "rwkv_w0", "rwkv_w2", "rwkv_a0", "rwkv_a2",
            "rwkv_k_k", "rwkv_k_a", "rwkv_r_k", "rwkv_ln_w", "rwkv_ln_b", "w_proj_fox", "w_proj_rwkv", "w_out",
            "final_norm_gain"]


def _pack_small(arrs):
    parts = []
    for a in arrs:
        f = a.reshape(-1)
        parts.append(jnp.pad(f, (0, (-f.shape[0]) % LANES)))
    flat = jnp.concatenate(parts)
    rows = flat.shape[0] // LANES
    flat = jnp.pad(flat, (0, ((-rows) % 8) * LANES))
    return flat.reshape(-1, LANES)


def _unpack_small(packed, shapes):
    flat = packed.reshape(-1)
    out, pos = [], 0
    for s in shapes:
        n = int(np.prod(s))
        out.append(flat[pos:pos + n].reshape(s))
        pos += n + ((-n) % LANES)
    return out


def _shard_major(a, axis):
    parts = jnp.split(a, N_CHIPS, axis=axis)
    return jnp.stack(parts, axis=0)


def kernel(x, norm_gain, w_in, fox_forget_bias, rwkv_shift_mix, rwkv_w0, rwkv_w2, rwkv_a0, rwkv_a2, rwkv_k_k, rwkv_k_a, rwkv_r_k, rwkv_ln_w, rwkv_ln_b, w_proj_fox, w_proj_rwkv, w_out, final_norm_gain, loss_target, m_norm_gain, m_w_in, m_fox_forget_bias, m_rwkv_shift_mix, m_rwkv_w0, m_rwkv_w2, m_rwkv_a0, m_rwkv_a2, m_rwkv_k_k, m_rwkv_k_a, m_rwkv_r_k, m_rwkv_ln_w, m_rwkv_ln_b, m_w_proj_fox, m_w_proj_rwkv, m_w_out, m_final_norm_gain, v_norm_gain, v_w_in, v_fox_forget_bias, v_rwkv_shift_mix, v_rwkv_w0, v_rwkv_w2, v_rwkv_a0, v_rwkv_a2, v_rwkv_k_k, v_rwkv_k_a, v_rwkv_r_k, v_rwkv_ln_w, v_rwkv_ln_b, v_w_proj_fox, v_w_proj_rwkv, v_w_out, v_final_norm_gain):
    args = dict(locals())
    T, D = x.shape[1], x.shape[2]
    lora = rwkv_w2.shape[1]
    cfg = _Cfg(T, D, lora)
    RW = cfg.RW
    c_idx = lax.axis_index("c").astype(jnp.int32).reshape(1)
    me_chip = (2 * lax.axis_index("x") + lax.axis_index("y")).astype(jnp.int32)
    place = jnp.concatenate([me_chip.reshape(1), c_idx])

    w_in_t, m_in_t, v_in_t = w_in[0].T, m_w_in[0].T, v_w_in[0].T
    w_in_s = w_in[0].astype(BF16)
    wp_s = jnp.concatenate([w_proj_fox[0], w_proj_rwkv[0]], axis=0)
    lora_s = jnp.concatenate([rwkv_w2[0], rwkv_a2[0]], axis=0)
    mine = [w_in_s, _cast_bf16("cast_w_proj", wp_s), _cast_bf16("cast_w_out", w_out[0]), lora_s]
    gathered = _gather_weights(mine)
    g_in, g_wp, g_out, g_lora = [lax.dynamic_update_slice(g, own[None], (me_chip, 0, 0))
                                 for g, own in zip(gathered, mine)]
    w_my = _shards_to_my_layout(cfg, g_in)
    wp = g_wp.transpose(1, 0, 2).reshape(2 * RW, D)
    wout = g_out.reshape(D, D)
    lo = g_lora.transpose(1, 0, 2).reshape(2 * lora, RW)

    loss_dev, gx, small, big = _local_step(
        cfg, x[0], loss_target[0], norm_gain, w_my, fox_forget_bias, rwkv_shift_mix, rwkv_w0, lo[:lora], rwkv_a0,
        lo[lora:], rwkv_k_k, rwkv_k_a, rwkv_r_k, rwkv_ln_w, rwkv_ln_b, wp[:RW], wp[RW:], wout, final_norm_gain)
    loss = lax.psum(loss_dev, ("x", "y", "c"))

    gs_in = _my_layout_to_shards(cfg, big["w_in"])
    gs_wp = _shard_major(jnp.concatenate([big["w_proj_fox"], big["w_proj_rwkv"]], axis=0), 1)
    gs_out = _shard_major(big["w_out"], 0)
    gs_lora = _shard_major(jnp.concatenate([big["rwkv_w2"], big["rwkv_a2"]], axis=0).astype(BF16), 1)
    gs = [gs_in, gs_wp, gs_out, gs_lora]
    names = ["w_in", "w_proj", "w_out", "lora"]
    recv1 = _exchange_halves(gs)
    chip_sums = [_add_halves("add_halves_" + nm, g, r, c_idx) for nm, g, r in zip(names, gs, recv1)]
    small_shapes = [args[nm].shape for nm in _SMALL]
    packed = _pack_small([small[nm] for nm in _SMALL])
    *recv2, small_all = _scatter_to_owners(chip_sums, packed)
    reduced = [_sum_chips("sum_chips_" + nm, r, own, place) for nm, r, own in zip(names, recv2, chip_sums)]
    g_small = _sum_slots("sum_small", small_all)
    g_in_f, g_wp_f, g_out_f, g_lora_f = _join_halves(reduced)

    grads = dict(zip(_SMALL, _unpack_small(g_small, small_shapes)))
    grads["w_proj_fox"] = g_wp_f[None, :RW]
    grads["w_proj_rwkv"] = g_wp_f[None, RW:]
    grads["w_out"] = g_out_f[None]
    grads["rwkv_w2"] = g_lora_f[None, :lora]
    grads["rwkv_a2"] = g_lora_f[None, lora:]

    delta, new_m, new_v = {}, {}, {}
    w_small = _pack_small([args[nm] for nm in _SMALL])
    m_small = _pack_small([args["m_" + nm] for nm in _SMALL])
    v_small = _pack_small([args["v_" + nm] for nm in _SMALL])
    d_s, m_s, v_s = _adamw("adamw_small", w_small, g_small, m_small, v_small)
    for tgt, pk in ((delta, d_s), (new_m, m_s), (new_v, v_s)):
        tgt.update(zip(_SMALL, _unpack_small(pk, small_shapes)))
    d_t, m_t, v_t, g_t = _adamw("adamw_w_in", w_in_t, g_in_f, m_in_t, v_in_t, copy_grad=True)
    grads["w_in"], delta["w_in"], new_m["w_in"], new_v["w_in"] = [t.T[None] for t in (g_t, d_t, m_t, v_t)]
    for nm in ("w_proj_fox", "w_proj_rwkv", "w_out", "rwkv_w2", "rwkv_a2"):
        shp = args[nm].shape
        two_d = (shp[1], shp[2])
        d_b, m_b, v_b = _adamw("adamw_" + nm, args[nm].reshape(two_d), grads[nm].reshape(two_d),
                               args["m_" + nm].reshape(two_d), args["v_" + nm].reshape(two_d))
        delta[nm], new_m[nm], new_v[nm] = d_b.reshape(shp), m_b.reshape(shp), v_b.reshape(shp)

    return (loss, gx[None], *[grads[n] for n in _WEIGHTS], *[delta[n] for n in _WEIGHTS],
            *[new_m[n] for n in _WEIGHTS], *[new_v[n] for n in _WEIGHTS])
```

```python
import functools

import numpy as np
import jax
import jax.numpy as jnp
from jax import lax
from jax.experimental import pallas as pl
from jax.experimental.pallas import tpu as pltpu

F32 = jnp.float32
BF16 = jnp.bfloat16
HI = lax.Precision.HIGHEST
MESH = pl.DeviceIdType.MESH

FOX_HEAD_DIM = 128
RWKV_HEAD_DIM = 64
RMS_EPS = 1e-6
GN_EPS = 64e-5
L2_EPS = 1e-12
ADAM_LR = 0.001
ADAM_B1 = 0.9
ADAM_B2 = 0.999
ADAM_EPS = 1e-08
ADAM_WD = 0.01
ADAM_STEP = 10

LANES = 128
VMEM_LIMIT = 56 * 1024 * 1024
SCAN_CHUNK = 64
SCAN_HEADS_PER_STEP = 8
SCAN_PASSES = (3, 1, 1)
N_CHIPS = 4
N_DEV = 8

_pcall = pl.pallas_call


def _cparams(sem=None):
    return pltpu.CompilerParams(dimension_semantics=sem, vmem_limit_bytes=VMEM_LIMIT)


def _softplus(x):
    return jnp.maximum(x, 0.0) + jnp.log(1.0 + jnp.exp(-jnp.abs(x)))


def _silu(z):
    return z * jax.nn.sigmoid(z)


def _rmsn(x, g):
    return x * lax.rsqrt(jnp.mean(x * x, axis=-1, keepdims=True) + RMS_EPS) * g


def _dot(a, b, dims="nn", precision=None):
    dn = {"nn": (((1,), (0,)), ((), ())), "nt": (((1,), (1,)), ((), ())), "tn": (((0,), (0,)), ((), ()))}[dims]
    return lax.dot_general(a, b, dn, precision=precision, preferred_element_type=F32)


def _split_bf16(x):
    hi = x.astype(BF16)
    return hi, (x - hi.astype(F32)).astype(BF16)


def _bdot_raw(a, b, ca, cb, passes):
    dn = (((ca,), (cb,)), ((0,), (0,)))
    mm = lambda p, q: lax.dot_general(p, q, dn, preferred_element_type=F32)
    if passes == 1:
        return mm(a.astype(BF16), b.astype(BF16))
    ah, al = _split_bf16(a)
    bh, bl = _split_bf16(b)
    return mm(ah, bh) + (mm(ah, bl) + mm(al, bh))


@functools.partial(jax.custom_vjp, nondiff_argnums=(2, 3, 4))
def _bdot_p(a, b, ca, cb, passes):
    return _bdot_raw(a, b, ca, cb, passes)


def _bdot_fwd(a, b, ca, cb, passes):
    return _bdot_raw(a, b, ca, cb, passes), (a, b)


def _bdot_bwd(ca, cb, passes, res, g):
    a, b = res
    if (ca, cb) == (2, 1):
        return _bdot_p(g, b, 2, 2, passes), _bdot_p(a, g, 1, 1, passes)
    if (ca, cb) == (2, 2):
        return _bdot_p(g, b, 2, 1, passes), _bdot_p(g, a, 1, 1, passes)
    assert (ca, cb) == (1, 1)
    return _bdot_p(b, g, 2, 2, passes), _bdot_p(a, g, 2, 1, passes)


_bdot_p.defvjp(_bdot_fwd, _bdot_bwd)


def _bdot(a, b, ca, cb, passes=3):
    return _bdot_p(a, b, ca, cb, passes)


def _dot3(a, b):
    return _bdot(a[None], b[None], 2, 1)[0]


@jax.custom_vjp
def _xdot(x, m, mt):
    hi, lo = _split_bf16(x)
    m16 = m.astype(BF16)
    return _dot(hi, m16) + _dot(lo, m16)


def _xdot_fwd(x, m, mt):
    return _xdot(x, m, mt), (m, mt)


def _xdot_bwd(res, g):
    m, mt = res
    return _xdot(g, mt, m), jnp.zeros_like(m), jnp.zeros_like(mt)


_xdot.defvjp(_xdot_fwd, _xdot_bwd)


class _Cfg:
    def __init__(self, T, D, lora):
        self.T, self.D, self.lora = T, D, lora
        self.FW = D // 2
        self.FH = self.FW // FOX_HEAD_DIM
        self.RW = D // 2
        self.RH = self.RW // RWKV_HEAD_DIM
        self.LP = -(-lora // LANES) * LANES
        self.o_fox = 0
        self.o_rwkv = 4 * self.FW
        self.o_gate = self.o_rwkv + 4 * self.RW
        self.o_f = self.o_gate + 2 * D
        self.o_wd = self.o_f + LANES
        self.o_ad = self.o_wd + self.LP
        end = self.o_ad + self.LP
        self.tn = 1280 if D >= 2048 else LANES
        self.ncol = -(-end // self.tn) * self.tn
        self.in_cols = 4 * self.FW + self.FH + 4 * self.RW + 2 * lora + 2 * D
        self.scp = -(-(self.in_cols // N_CHIPS) // LANES) * LANES
        self.rseg = 4 * self.RW + 2 * self.LP
        self.C = min(SCAN_CHUNK, T)
        self.tr = min(256, T)
        self.hb = min(SCAN_HEADS_PER_STEP, self.RH)

    def segments(self):
        FW, FH, RW, lo, D = self.FW, self.FH, self.RW, self.lora, self.D
        g_f = 4 * FW
        g_r = g_f + FH
        g_wd = g_r + 4 * RW
        g_ad = g_wd + lo
        g_g = g_ad + lo
        dh = FOX_HEAD_DIM
        qkv = [(j * FW + h * dh, dh, (3 * h + j) * dh) for h in range(FH) for j in range(3)]
        return qkv + [(3 * FW, FW, 3 * FW), (g_f, FH, self.o_f), (g_r, 4 * RW, self.o_rwkv), (g_wd, lo, self.o_wd),
                      (g_ad, lo, self.o_ad), (g_g, 2 * D, self.o_gate)]


def _to_my_layout(cfg, wg):
    R = wg.shape[0]
    segs = sorted(cfg.segments(), key=lambda s: s[2])
    parts, pos = [], 0
    for g0, w, m0 in segs:
        if m0 > pos:
            parts.append(jnp.zeros((R, m0 - pos), wg.dtype))
        parts.append(wg[:, g0:g0 + w])
        pos = m0 + w
    if cfg.ncol > pos:
        parts.append(jnp.zeros((R, cfg.ncol - pos), wg.dtype))
    return jnp.concatenate(parts, axis=1)


def _from_my_layout(cfg, wm):
    segs = sorted(cfg.segments(), key=lambda s: s[0])
    return jnp.concatenate([wm[:, m0:m0 + w] for g0, w, m0 in segs], axis=1)


def _shards_to_my_layout(cfg, g):
    R, sc = g.shape[1], g.shape[2]
    segs = sorted(cfg.segments(), key=lambda s: s[2])
    parts, pos = [], 0
    for g0, w, m0 in segs:
        if m0 > pos:
            parts.append(jnp.zeros((R, m0 - pos), g.dtype))
        for s in range(N_CHIPS):
            lo, hi = max(g0, s * sc), min(g0 + w, (s + 1) * sc)
            if lo < hi:
                parts.append(g[s, :, lo - s * sc:hi - s * sc])
        pos = m0 + w
    if cfg.ncol > pos:
        parts.append(jnp.zeros((R, cfg.ncol - pos), g.dtype))
    return jnp.concatenate(parts, axis=1)


def _my_layout_to_shards(cfg, wm):
    sc, R = cfg.in_cols // N_CHIPS, wm.shape[1]
    segs = sorted(cfg.segments(), key=lambda s: s[0])
    shards = []
    for s in range(N_CHIPS):
        parts = []
        for g0, w, m0 in segs:
            lo, hi = max(g0, s * sc), min(g0 + w, (s + 1) * sc)
            if lo < hi:
                parts.append(wm[m0 + lo - g0:m0 + hi - g0, :])
        parts.append(jnp.zeros((cfg.scp - sc, R), wm.dtype))
        shards.append(jnp.concatenate(parts, axis=0))
    return jnp.stack(shards, axis=0)


def _rwkv_vec_to_my(cfg, v):
    RW4, lo, LP = 4 * cfg.RW, cfg.lora, cfg.LP
    z = jnp.zeros((1, LP - lo), v.dtype)
    return jnp.concatenate([v[:, :RW4], v[:, RW4:RW4 + lo], z, v[:, RW4 + lo:], z], axis=1)


def _rwkv_vec_from_my(cfg, v):
    RW4, lo, LP = 4 * cfg.RW, cfg.lora, cfg.LP
    return jnp.concatenate([v[:, :RW4], v[:, RW4:RW4 + lo], v[:, RW4 + LP:RW4 + LP + lo]], axis=1)


def _mm(name, a, b, dims, out_dtype, tm, tn, tk):
    (M, K) = a.shape if dims != "tn" else a.shape[::-1]
    N = b.shape[0] if dims == "nt" else b.shape[1]
    tm, tn, tk = min(tm, M), min(tn, N), min(tk, K)
    assert M % tm == 0 and N % tn == 0 and K % tk == 0, (name, M, N, K, tm, tn, tk)
    nk = K // tk
    if dims == "nn":
        a_spec = pl.BlockSpec((tm, tk), lambda i, j, k: (i, k))
        b_spec = pl.BlockSpec((tk, tn), lambda i, j, k: (k, j))
    elif dims == "nt":
        a_spec = pl.BlockSpec((tm, tk), lambda i, j, k: (i, k))
        b_spec = pl.BlockSpec((tn, tk), lambda i, j, k: (j, k))
    else:
        a_spec = pl.BlockSpec((tk, tm), lambda i, j, k: (k, i))
        b_spec = pl.BlockSpec((tk, tn), lambda i, j, k: (k, j))

    def body(a_ref, b_ref, o_ref, acc_ref):
        k = pl.program_id(2)

        @pl.when(k == 0)
        def _():
            acc_ref[...] = jnp.zeros_like(acc_ref)

        acc_ref[...] += _dot(a_ref[...], b_ref[...], dims)

        @pl.when(k == nk - 1)
        def _():
            o_ref[...] = acc_ref[...].astype(o_ref.dtype)

    return _pcall(
        body, name=name, grid=(M // tm, N // tn, nk),
        in_specs=[a_spec, b_spec], out_specs=pl.BlockSpec((tm, tn), lambda i, j, k: (i, j)),
        out_shape=jax.ShapeDtypeStruct((M, N), out_dtype), scratch_shapes=[pltpu.VMEM((tm, tn), F32)],
        compiler_params=_cparams(("parallel", "parallel", "arbitrary")),
    )(a, b)


def _tile(tr, w, cb=0):
    return pl.BlockSpec((tr, w), lambda i: (i, cb))


def _const(shape):
    nd = len(shape)
    return pl.BlockSpec(shape, lambda i: (0,) * nd)


def _acc_store(i, ref, val):
    @pl.when(i == 0)
    def _():
        ref[...] = val

    @pl.when(i > 0)
    def _():
        ref[...] += val


def _rms_fwd(cfg, x2, g):
    T, D, tr = cfg.T, cfg.D, cfg.tr

    def body(x_ref, g_ref, h_ref):
        h_ref[...] = _rmsn(x_ref[...], g_ref[...]).astype(BF16)

    return _pcall(body, name="rms_fwd", grid=(T // tr,), in_specs=[_tile(tr, D), _const((1, D))],
                  out_specs=_tile(tr, D), out_shape=jax.ShapeDtypeStruct((T, D), BF16),
                  compiler_params=_cparams(("parallel",)))(x2, g)


def _rms_bwd(cfg, x2, g, dh, dres):
    T, D, tr = cfg.T, cfg.D, cfg.tr

    def body(x_ref, g_ref, dh_ref, dres_ref, gx_ref, dg_ref):
        _, vjp = jax.vjp(_rmsn, x_ref[...], g_ref[...])
        dx, dg = vjp(dh_ref[...])
        gx_ref[...] = dx + dres_ref[...]
        _acc_store(pl.program_id(0), dg_ref, dg)

    return _pcall(body, name="rms_bwd", grid=(T // tr,),
                  in_specs=[_tile(tr, D), _const((1, D)), _tile(tr, D), _tile(tr, D)],
                  out_specs=[_tile(tr, D), _const((1, D))],
                  out_shape=[jax.ShapeDtypeStruct((T, D), F32), jax.ShapeDtypeStruct((1, D), F32)],
                  compiler_params=_cparams(("arbitrary",)))(x2, g, dh, dres)


def _final(cfg, x2, mo, fg, target):
    T, D, tr = cfg.T, cfg.D, cfg.tr

    def loss_fn(hres, g, tgt):
        err = _rmsn(hres, g) - tgt
        return 0.5 * jnp.sum(jnp.mean(err * err, axis=-1, keepdims=True), axis=0, keepdims=True)

    def body(x_ref, mo_ref, g_ref, t_ref, loss_ref, dres_ref, dres16_ref, dg_ref):
        hres = x_ref[...] + mo_ref[...]
        loss, vjp = jax.vjp(functools.partial(loss_fn, tgt=t_ref[...]), hres, g_ref[...])
        dres, dg = vjp(jnp.ones((1, 1), F32))
        dres_ref[...] = dres
        dres16_ref[...] = dres.astype(BF16)
        i = pl.program_id(0)
        _acc_store(i, dg_ref, dg)
        _acc_store(i, loss_ref, jnp.broadcast_to(loss, (8, LANES)))

    return _pcall(body, name="final_loss", grid=(T // tr,),
                  in_specs=[_tile(tr, D), _tile(tr, D), _const((1, D)), _tile(tr, D)],
                  out_specs=[_const((8, LANES)), _tile(tr, D), _tile(tr, D), _const((1, D))],
                  out_shape=[jax.ShapeDtypeStruct((8, LANES), F32), jax.ShapeDtypeStruct((T, D), F32),
                             jax.ShapeDtypeStruct((T, D), BF16), jax.ShapeDtypeStruct((1, D), F32)],
                  compiler_params=_cparams(("arbitrary",)))(x2, mo, fg, target)


def _merge_fn(pa, pb, ga, gb):
    return jax.nn.sigmoid(ga) * pa + jax.nn.sigmoid(gb) * pb


def _merge_fwd(cfg, pa, pb, u):
    T, D, tr = cfg.T, cfg.D, cfg.tr
    cga, cgb = cfg.o_gate // D, cfg.o_gate // D + 1

    def body(pa_ref, pb_ref, ga_ref, gb_ref, m_ref):
        m_ref[...] = _merge_fn(pa_ref[...], pb_ref[...], ga_ref[...], gb_ref[...]).astype(BF16)

    return _pcall(body, name="merge_fwd", grid=(T // tr,),
                  in_specs=[_tile(tr, D), _tile(tr, D), _tile(tr, D, cga), _tile(tr, D, cgb)],
                  out_specs=_tile(tr, D), out_shape=jax.ShapeDtypeStruct((T, D), BF16),
                  compiler_params=_cparams(("parallel",)))(pa, pb, u, u)


def _merge_bwd(cfg, pa, pb, u, dm):
    T, D, tr = cfg.T, cfg.D, cfg.tr
    cga, cgb = cfg.o_gate // D, cfg.o_gate // D + 1

    def body(pa_ref, pb_ref, ga_ref, gb_ref, dm_ref, dpa_ref, dpb_ref, dg_ref):
        _, vjp = jax.vjp(_merge_fn, pa_ref[...], pb_ref[...], ga_ref[...], gb_ref[...])
        dpa, dpb, dga, dgb = vjp(dm_ref[...])
        dpa_ref[...] = dpa.astype(BF16)
        dpb_ref[...] = dpb.astype(BF16)
        dg_ref[:, :D] = dga.astype(BF16)
        dg_ref[:, D:] = dgb.astype(BF16)

    return _pcall(body, name="merge_bwd", grid=(T // tr,),
                  in_specs=[_tile(tr, D), _tile(tr, D), _tile(tr, D, cga), _tile(tr, D, cgb), _tile(tr, D)],
                  out_specs=[_tile(tr, D), _tile(tr, D), _tile(tr, 2 * D, cfg.o_gate // (2 * D))],
                  out_shape=[jax.ShapeDtypeStruct((T, D), BF16), jax.ShapeDtypeStruct((T, D), BF16),
                             jax.ShapeDtypeStruct((T, cfg.ncol), BF16)],
                  compiler_params=_cparams(("parallel",)))(pa, pb, u, u, dm)


def _gate_fn(o, z):
    return o * _silu(z)


def _gate_a_fwd(cfg, o, u):
    T, FW, tr = cfg.T, cfg.FW, cfg.tr

    def body(o_ref, z_ref, oa_ref):
        oa_ref[...] = _gate_fn(o_ref[...], z_ref[...]).astype(BF16)

    return _pcall(body, name="gate_a_fwd", grid=(T // tr,), in_specs=[_tile(tr, FW), _tile(tr, FW, 3)],
                  out_specs=_tile(tr, FW), out_shape=jax.ShapeDtypeStruct((T, FW), BF16),
                  compiler_params=_cparams(("parallel",)))(o, u)


def _gate_a_bwd(cfg, o, u, doa, du):
    T, FW, tr = cfg.T, cfg.FW, cfg.tr

    def body(o_ref, z_ref, doa_ref, du_in, do_ref, dz_ref):
        _, vjp = jax.vjp(_gate_fn, o_ref[...], z_ref[...])
        do, dz = vjp(doa_ref[...])
        do_ref[...] = do
        dz_ref[...] = dz.astype(BF16)

    return _pcall(body, name="gate_a_bwd", grid=(T // tr,),
                  in_specs=[_tile(tr, FW), _tile(tr, FW, 3), _tile(tr, FW), _ANY],
                  out_specs=[_tile(tr, FW), _tile(tr, FW, 3)],
                  out_shape=[jax.ShapeDtypeStruct((T, FW), F32), jax.ShapeDtypeStruct(du.shape, BF16)],
                  input_output_aliases={3: 1},
                  compiler_params=_cparams(("parallel",)))(o, u, doa, du)


def _fox_prep(cfg, u, fb):
    T, tr = cfg.T, cfg.tr
    cf = cfg.o_f // LANES

    def body(f_ref, fb_ref, c_ref, carry_ref):
        i = pl.program_id(0)

        @pl.when(i == 0)
        def _():
            carry_ref[...] = jnp.zeros_like(carry_ref)

        lf = -_softplus(-(f_ref[...] + fb_ref[...]))
        r = lax.broadcasted_iota(jnp.int32, (tr, tr), 0)
        c = lax.broadcasted_iota(jnp.int32, (tr, tr), 1)
        tri = (r >= c).astype(F32)
        c_ref[...] = _dot(tri, lf, precision=HI) + carry_ref[...]
        carry_ref[...] += jnp.sum(lf, axis=0, keepdims=True)

    return _pcall(body, name="fox_prep", grid=(T // tr,), in_specs=[_tile(tr, LANES, cf), _const((1, LANES))],
                  out_specs=_tile(tr, LANES), out_shape=jax.ShapeDtypeStruct((T, LANES), F32),
                  scratch_shapes=[pltpu.VMEM((1, LANES), F32)], compiler_params=_cparams(("arbitrary",)))(u, fb)


def _fox_prep_bwd(cfg, u, fb, dc):
    T, tr = cfg.T, cfg.tr
    cf = cfg.o_f // LANES
    nb = T // tr

    def body(f_ref, fb_ref, dc_ref, df_ref, dfb_ref, carry_ref):
        i = pl.program_id(0)

        @pl.when(i == 0)
        def _():
            carry_ref[...] = jnp.zeros_like(carry_ref)

        dc = dc_ref[...]
        r = lax.broadcasted_iota(jnp.int32, (tr, tr), 0)
        c = lax.broadcasted_iota(jnp.int32, (tr, tr), 1)
        triu = (r <= c).astype(F32)
        dlf = _dot(triu, dc, precision=HI) + carry_ref[...]
        carry_ref[...] += jnp.sum(dc, axis=0, keepdims=True)
        dz = dlf * jax.nn.sigmoid(-(f_ref[...] + fb_ref[...]))
        df_ref[...] = dz.astype(BF16)
        _acc_store(i, dfb_ref, jnp.sum(dz, axis=0, keepdims=True))

    rev = lambda i: (nb - 1 - i, 0)
    return _pcall(body, name="fox_prep_bwd", grid=(nb,),
                  in_specs=[pl.BlockSpec((tr, LANES), lambda i: (nb - 1 - i, cf)), _const((1, LANES)),
                            pl.BlockSpec((tr, LANES), rev)],
                  out_specs=[pl.BlockSpec((tr, LANES), rev), _const((1, LANES))],
                  out_shape=[jax.ShapeDtypeStruct((T, LANES), BF16), jax.ShapeDtypeStruct((1, LANES), F32)],
                  scratch_shapes=[pltpu.VMEM((1, LANES), F32)], compiler_params=_cparams(("arbitrary",)))(u, fb, dc)


def _attn_logits(q_ref, k_ref, c_ref, i, tq, te):
    s = _dot(q_ref[...].astype(BF16), k_ref[0:te, :].astype(BF16), "nt") * (FOX_HEAD_DIM ** -0.5) - c_ref[0, :, 0:te]
    row = i * tq + lax.broadcasted_iota(jnp.int32, (tq, te), 0)
    col = lax.broadcasted_iota(jnp.int32, (tq, te), 1)
    return jnp.where(col <= row, s, -1e30)


def _per_query_tile(i, nq, tq, fn):
    for ii in range(nq):
        pl.when(i == ii)(functools.partial(fn, (ii + 1) * tq))


def _attn_fwd(cfg, u, c_rows):
    T, FW, FH = cfg.T, cfg.FW, cfg.FH
    tq = min(256, T)
    dh = FOX_HEAD_DIM

    def body(q_ref, k_ref, v_ref, c_ref, o_ref, lse_ref):
        i = pl.program_id(1)

        def tile(te):
            s = _attn_logits(q_ref, k_ref, c_ref, i, tq, te)
            m = jnp.max(s, axis=1, keepdims=True)
            p = jnp.exp(s - m)
            l = jnp.sum(p, axis=1, keepdims=True)
            o_ref[...] = _dot(p.astype(BF16), v_ref[0:te, :].astype(BF16)) / l
            lse_ref[0] = m + jnp.log(l)

        _per_query_tile(i, T // tq, tq, tile)

    return _pcall(
        body, name="fox_attn_fwd", grid=(FH, T // tq),
        in_specs=[pl.BlockSpec((tq, dh), lambda h, i: (i, 3 * h)), pl.BlockSpec((T, dh), lambda h, i: (0, 3 * h + 1)),
                  pl.BlockSpec((T, dh), lambda h, i: (0, 3 * h + 2)), pl.BlockSpec((1, 1, T), lambda h, i: (h, 0, 0))],
        out_specs=[pl.BlockSpec((tq, dh), lambda h, i: (i, h)), pl.BlockSpec((1, tq, 1), lambda h, i: (h, i, 0))],
        out_shape=[jax.ShapeDtypeStruct((T, FW), F32), jax.ShapeDtypeStruct((FH, T, 1), F32)],
        compiler_params=_cparams(("parallel", "arbitrary")),
    )(u, u, u, c_rows)


def _attn_bwd(cfg, u, c_rows, lse, do, du):
    T, FW, FH = cfg.T, cfg.FW, cfg.FH
    tq = min(256, T)
    nq = T // tq
    dh = FOX_HEAD_DIM
    scale = dh ** -0.5

    def body(q_ref, k_ref, v_ref, c_ref, lse_ref, do_ref, du_in, du_ref, dcol_ref, dk_acc, dv_acc):
        i = pl.program_id(1)

        @pl.when(i == 0)
        def _():
            dk_acc[...] = jnp.zeros_like(dk_acc)
            dv_acc[...] = jnp.zeros_like(dv_acc)
            dcol_ref[...] = jnp.zeros_like(dcol_ref)

        def tile(te):
            s = _attn_logits(q_ref, k_ref, c_ref, i, tq, te)
            p = jnp.exp(s - lse_ref[0])
            do_v = do_ref[...]
            dp = _dot(do_v.astype(BF16), v_ref[0:te, :].astype(BF16), "nt")
            delta = jnp.sum(p * dp, axis=1, keepdims=True)
            ds = p * (dp - delta)
            ds16 = ds.astype(BF16)
            du_ref[te - tq:te, 0:dh] = (_dot(ds16, k_ref[0:te, :].astype(BF16)) * scale).astype(BF16)
            dk_acc[0:te, :] += _dot(ds16, q_ref[...].astype(BF16), "tn") * scale
            dv_acc[0:te, :] += _dot(p.astype(BF16), do_v.astype(BF16), "tn")
            dcol_ref[0, :, 0:te] += jnp.sum(ds, axis=0, keepdims=True)

        _per_query_tile(i, nq, tq, tile)

        @pl.when(i == nq - 1)
        def _():
            du_ref[:, dh:2 * dh] = dk_acc[...].astype(BF16)
            du_ref[:, 2 * dh:3 * dh] = dv_acc[...].astype(BF16)

    return _pcall(
        body, name="fox_attn_bwd", grid=(FH, nq),
        in_specs=[pl.BlockSpec((tq, dh), lambda h, i: (i, 3 * h)), pl.BlockSpec((T, dh), lambda h, i: (0, 3 * h + 1)),
                  pl.BlockSpec((T, dh), lambda h, i: (0, 3 * h + 2)), pl.BlockSpec((1, 1, T), lambda h, i: (h, 0, 0)),
                  pl.BlockSpec((1, tq, 1), lambda h, i: (h, i, 0)), pl.BlockSpec((tq, dh), lambda h, i: (i, h)), _ANY],
        out_specs=[pl.BlockSpec((T, 3 * dh), lambda h, i: (0, h)), pl.BlockSpec((1, 1, T), lambda h, i: (h, 0, 0))],
        out_shape=[jax.ShapeDtypeStruct(du.shape, BF16), jax.ShapeDtypeStruct((FH, 1, T), F32)],
        scratch_shapes=[pltpu.VMEM((T, dh), F32), pltpu.VMEM((T, dh), F32)],
        input_output_aliases={6: 0},
        compiler_params=_cparams(("parallel", "arbitrary")),
    )(u, u, u, c_rows, lse, do, du)


def _head_indicators(cfg):
    ind = np.zeros((cfg.RW, LANES), np.float32)
    ind[np.arange(cfg.RW), np.arange(cfg.RW) // RWKV_HEAD_DIM] = 1.0
    pad = np.zeros((1, LANES), np.float32)
    pad[0, cfg.RH:] = 1.0
    return jnp.asarray(ind), jnp.asarray(ind.T.copy()), jnp.asarray(pad)


def _prep_fn(us_r, us_k, us_v, us_wd, us_ad, w0, w2p, a0, a2p, k_k, k_a, ind, ind_t, pad):
    wpre = w0 + _dot3(jnp.tanh(us_wd), w2p)
    w = -_softplus(-wpre) - 0.5
    lw = -jnp.exp(w)
    a = jax.nn.sigmoid(a0 + _dot3(us_ad, a2p))
    kk = us_k * k_k
    ss = _xdot(kk * kk, ind, ind_t) + pad
    inv = 1.0 / jnp.maximum(jnp.sqrt(ss), L2_EPS)
    kkn = kk * _xdot(inv, ind_t, ind)
    kp = us_k * (1.0 + (a - 1.0) * k_a)
    return us_r, lw, kp, us_v, -kkn, kkn * a


def _shifted(u, prev_row, mu, first):
    n = u.shape[0]
    rolled = pltpu.roll(u, 1, 0)
    row = lax.broadcasted_iota(jnp.int32, u.shape, 0)
    p0 = jnp.where(first, jnp.zeros_like(prev_row), prev_row)
    prev = jnp.where(row == 0, jnp.broadcast_to(p0, u.shape), rolled)
    return u + (prev - u) * mu, prev


def _rwkv_specs(cfg, tr):
    RW, LP = cfg.RW, cfg.LP
    base = cfg.o_rwkv // RW
    cols = [(RW, base), (RW, base + 1), (RW, base + 2), (RW, base + 3), (LP, cfg.o_wd // LP), (LP, cfg.o_ad // LP)]
    cur = [pl.BlockSpec((tr, w), (lambda i, cb=cb: (i, cb))) for w, cb in cols]
    prv = [pl.BlockSpec((8, w), (lambda i, cb=cb: (jnp.maximum(i * (tr // 8) - 1, 0), cb))) for w, cb in cols]
    return cols, cur, prv


def _mu_pieces(cfg, mu_ref):
    RW, LP = cfg.RW, cfg.LP
    offs = [0, RW, 2 * RW, 3 * RW, 4 * RW, 4 * RW + LP, 4 * RW + 2 * LP]
    return [mu_ref[:, offs[j]:offs[j + 1]] for j in range(6)]


def _rwkv_prep_fwd(cfg, u, mu, w0, w2p, a0, a2p, k_k, k_a):
    T, RW, LP, tr = cfg.T, cfg.RW, cfg.LP, cfg.tr
    ind, ind_t, pad = _head_indicators(cfg)
    cols, cur, prv = _rwkv_specs(cfg, tr)

    def body(*refs):
        u_refs, p_refs = refs[0:6], refs[6:12]
        mu_ref, w0_ref, w2_ref, a0_ref, a2_ref, kk_ref, ka_ref, ind_ref, indt_ref, pad_ref = refs[12:22]
        outs = refs[22:]
        first = pl.program_id(0) == 0
        mus = _mu_pieces(cfg, mu_ref)
        us = [_shifted(u_refs[j][...], p_refs[j][7:8, :], mus[j], first)[0] for j in range(6)]
        res = _prep_fn(us[0], us[1], us[2], us[4], us[5], w0_ref[...], w2_ref[...], a0_ref[...], a2_ref[...],
                       kk_ref[...], ka_ref[...], ind_ref[...], indt_ref[...], pad_ref[...])
        for j in range(6):
            outs[j][...] = res[j]
        outs[6][...] = us[3]

    consts = [mu, w0, w2p, a0, a2p, k_k, k_a, ind, ind_t, pad]
    return _pcall(body, name="rwkv_prep_fwd", grid=(T // tr,),
                  in_specs=cur + prv + [_const(c.shape) for c in consts],
                  out_specs=[_tile(tr, RW)] * 7, out_shape=[jax.ShapeDtypeStruct((T, RW), F32)] * 7,
                  compiler_params=_cparams(("parallel",)))(*([u] * 12), *consts)


def _rwkv_prep_bwd(cfg, u, mu, w0, w2p, a0, a2p, k_k, k_a, cots, dzb):
    T, RW, LP = cfg.T, cfg.RW, cfg.LP
    tr = min(128, T)
    ind, ind_t, pad = _head_indicators(cfg)
    cols, cur, prv = _rwkv_specs(cfg, tr)
    rseg = cfg.rseg

    def body(*refs):
        u_refs, p_refs = refs[0:6], refs[6:12]
        mu_ref, w0_ref, w2_ref, a0_ref, a2_ref, kk_ref, ka_ref, ind_ref, indt_ref, pad_ref = refs[12:22]
        cot_refs, dzb_ref = refs[22:28], refs[28]
        dus_ref, dmu_ref, dw0_ref, dw2_ref, da0_ref, da2_ref, dkk_ref, dka_ref = refs[29:]
        i = pl.program_id(0)
        first = i == 0
        mus = _mu_pieces(cfg, mu_ref)
        sh = [_shifted(u_refs[j][...], p_refs[j][7:8, :], mus[j], first) for j in range(6)]
        us = [s[0] for s in sh]
        fn = functools.partial(_prep_fn, ind=ind_ref[...], ind_t=indt_ref[...], pad=pad_ref[...])
        _, vjp = jax.vjp(fn, us[0], us[1], us[2], us[4], us[5], w0_ref[...], w2_ref[...], a0_ref[...], a2_ref[...],
                         kk_ref[...], ka_ref[...])
        d = vjp(tuple(c[...] for c in cot_refs))
        dus = [d[0], d[1], d[2], dzb_ref[...], d[3], d[4]]
        offs = [0, RW, 2 * RW, 3 * RW, 4 * RW, 4 * RW + LP, 4 * RW + 2 * LP]
        for j in range(6):
            dus_ref[:, offs[j]:offs[j + 1]] = dus[j]
            dmu_j = jnp.sum(dus[j] * (sh[j][1] - u_refs[j][...]), axis=0, keepdims=True)

            @pl.when(first)
            def _(j=j, dmu_j=dmu_j):
                dmu_ref[:, offs[j]:offs[j + 1]] = dmu_j

            @pl.when(i > 0)
            def _(j=j, dmu_j=dmu_j):
                dmu_ref[:, offs[j]:offs[j + 1]] += dmu_j
        for ref, val in zip((dw0_ref, dw2_ref, da0_ref, da2_ref, dkk_ref, dka_ref), d[5:11]):
            _acc_store(i, ref, val)

    consts = [mu, w0, w2p, a0, a2p, k_k, k_a, ind, ind_t, pad]
    vec = jax.ShapeDtypeStruct((1, RW), F32)
    mat = jax.ShapeDtypeStruct((LP, RW), F32)
    return _pcall(body, name="rwkv_prep_bwd", grid=(T // tr,),
                  in_specs=cur + prv + [_const(c.shape) for c in consts] + [_tile(tr, RW)] * 7,
                  out_specs=[_tile(tr, rseg), _const((1, rseg)), _const((1, RW)), _const((LP, RW)), _const((1, RW)),
                             _const((LP, RW)), _const((1, RW)), _const((1, RW))],
                  out_shape=[jax.ShapeDtypeStruct((T, rseg), F32), jax.ShapeDtypeStruct((1, rseg), F32),
                             vec, mat, vec, mat, vec, vec],
                  compiler_params=_cparams(("arbitrary",)))(*([u] * 12), *consts, *cots, dzb)


def _shift_bwd(cfg, dus, mu, df, du):
    T, tr, RW, LP = cfg.T, cfg.tr, cfg.RW, cfg.LP
    nb = T // tr
    tail = cfg.ncol - cfg.o_f
    assert cfg.o_rwkv % (4 * RW) == 0 and (4 * RW) % (2 * LP) == 0 and cfg.o_f % tail == 0

    def shifted(d_ref, n_ref, mu_ref):
        d = d_ref[...]
        rolled = pltpu.roll(d, tr - 1, 0)
        row = lax.broadcasted_iota(jnp.int32, d.shape, 0)
        n0 = jnp.where(pl.program_id(0) == nb - 1, jnp.zeros_like(n_ref[0:1, :]), n_ref[0:1, :])
        nxt = jnp.where(row == tr - 1, jnp.broadcast_to(n0, d.shape), rolled)
        mu_v = mu_ref[...]
        return (d * (1.0 - mu_v) + nxt * mu_v).astype(BF16)

    def main_body(d_ref, n_ref, mu_ref, du_in, du_ref):
        du_ref[...] = shifted(d_ref, n_ref, mu_ref)

    def tail_body(d_ref, n_ref, mu_ref, df_ref, du_in, du_ref):
        du_ref[:, 0:LANES] = df_ref[...]
        du_ref[:, LANES:LANES + 2 * LP] = shifted(d_ref, n_ref, mu_ref)
        if tail > LANES + 2 * LP:
            du_ref[:, LANES + 2 * LP:] = jnp.zeros((tr, tail - LANES - 2 * LP), BF16)

    def specs(w, cb):
        return [_tile(tr, w, cb),
                pl.BlockSpec((8, w), lambda i: (jnp.minimum((i + 1) * (tr // 8), T // 8 - 1), cb)),
                pl.BlockSpec((1, w), lambda i: (0, cb))]

    out = jax.ShapeDtypeStruct(du.shape, BF16)
    du = _pcall(main_body, name="shift_bwd_main", grid=(nb,), in_specs=specs(4 * RW, 0) + [_ANY],
                out_specs=_tile(tr, 4 * RW, cfg.o_rwkv // (4 * RW)), out_shape=out, input_output_aliases={3: 0},
                compiler_params=_cparams(("parallel",)))(dus, dus, mu, du)
    return _pcall(tail_body, name="shift_bwd_tail", grid=(nb,),
                  in_specs=specs(2 * LP, 4 * RW // (2 * LP)) + [_tile(tr, LANES), _ANY],
                  out_specs=_tile(tr, tail, cfg.o_f // tail), out_shape=out, input_output_aliases={4: 0},
                  compiler_params=_cparams(("parallel",)))(dus, dus, mu, df, du)


def _chunk_local(r, lw, k, v, a, b):
    H, C, K = r.shape
    row = lax.broadcasted_iota(jnp.int32, (C, C), 0)
    col = lax.broadcasted_iota(jnp.int32, (C, C), 1)
    incl = jnp.broadcast_to((row >= col).astype(F32)[None], (H, C, C))
    strict = (row > col)[None]
    lower = (row >= col)[None]
    eye = (row == col)[None]
    zero = jnp.zeros((), F32)
    L = _bdot(incl, lw, 2, 1)
    LC = jnp.sum(lw, axis=1, keepdims=True)
    eL = jnp.exp(L)
    eLn = jnp.exp(-L)
    at = a * jnp.exp(L - lw)
    rt = r * eL
    bt = b * eLn
    kt = k * eLn
    eR = jnp.exp(LC - L)
    bh = b * eR
    kh = k * eR
    gram = functools.partial(_bdot, passes=SCAN_PASSES[0])
    inv = functools.partial(_bdot, passes=SCAN_PASSES[1])
    app = functools.partial(_bdot, passes=SCAN_PASSES[2])
    n_ab = jnp.where(strict, gram(at, bt, 2, 2), zero)
    n_ak = jnp.where(strict, gram(at, kt, 2, 2), zero)
    m_rb = jnp.where(lower, gram(rt, bt, 2, 2), zero)
    m_rk = jnp.where(lower, gram(rt, kt, 2, 2), zero)
    M = n_ab
    P = jnp.where(eye, 1.0, zero) + n_ab
    for _ in range(1, max(1, int(np.ceil(np.log2(C))))):
        M = inv(M, M, 2, 1)
        P = P + inv(M, P, 2, 1)
    W = app(P, at, 2, 1)
    Uloc = app(P, app(n_ak, v, 2, 1), 2, 1)
    Q = rt + app(m_rb, W, 2, 1)
    Yloc = app(m_rb, Uloc, 2, 1) + app(m_rk, v, 2, 1)
    A = jnp.where(eye, jnp.exp(LC), zero) + app(W, bh, 1, 1)
    Sloc = app(Uloc, bh, 1, 1) + app(v, kh, 1, 1)
    return Q, Yloc, A, Sloc


def _split_heads(ref, n):
    N = RWKV_HEAD_DIM
    return jnp.stack([ref[:, h * N:(h + 1) * N] for h in range(n)], axis=0)


def _merge_heads(x):
    return jnp.concatenate([x[h] for h in range(x.shape[0])], axis=1)


def _scan_local_specs(cfg):
    N, HB = RWKV_HEAD_DIM, cfg.hb
    grid = (cfg.RH // HB, cfg.T // cfg.C)
    seq = pl.BlockSpec((HB, cfg.C, N), lambda h, j: (h, j, 0))
    mat = pl.BlockSpec((HB, 1, N, N), lambda h, j: (h, j, 0, 0))
    return grid, seq, mat


def _scan_local_fwd(cfg, seqs):
    T, RH, N = cfg.T, cfg.RH, RWKV_HEAD_DIM
    grid, seq, mat = _scan_local_specs(cfg)

    def body(r_ref, lw_ref, k_ref, v_ref, a_ref, b_ref, q_ref, yl_ref, a_out, sl_ref):
        Q, Yloc, A, Sloc = _chunk_local(*[_split_heads(ref, cfg.hb) for ref in (r_ref, lw_ref, k_ref, v_ref, a_ref, b_ref)])
        q_ref[...] = Q
        yl_ref[...] = Yloc
        a_out[:, 0] = A
        sl_ref[:, 0] = Sloc

    tok = pl.BlockSpec((cfg.C, cfg.hb * N), lambda h, j: (j, h))
    sq = jax.ShapeDtypeStruct((RH, T, N), F32)
    mt = jax.ShapeDtypeStruct((RH, T // cfg.C, N, N), F32)
    return _pcall(body, name="rwkv_scan_local_fwd", grid=grid, in_specs=[tok] * 6, out_specs=[seq, seq, mat, mat],
                  out_shape=[sq, sq, mt, mt], compiler_params=_cparams(("parallel", "parallel")))(*seqs)


def _scan_local_bwd(cfg, toks, dq, dy, da, dsl, extra):
    T, RW, N = cfg.T, cfg.RW, RWKV_HEAD_DIM
    grid, seq, mat = _scan_local_specs(cfg)

    def body(r_ref, lw_ref, k_ref, v_ref, a_ref, b_ref, dq_ref, dy_ref, da_ref, dsl_ref, xr_ref, xk_ref, xv_ref,
             *outs):
        ins = [_split_heads(ref, cfg.hb) for ref in (r_ref, lw_ref, k_ref, v_ref, a_ref, b_ref)]
        _, vjp = jax.vjp(_chunk_local, *ins)
        d = vjp((dq_ref[...], _split_heads(dy_ref, cfg.hb), da_ref[:, 0], dsl_ref[:, 0]))
        add = {0: xr_ref, 2: xk_ref, 3: xv_ref}
        for j in range(6):
            dj = _merge_heads(d[j])
            outs[j][...] = dj + add[j][...] if j in add else dj

    tok = pl.BlockSpec((cfg.C, cfg.hb * N), lambda h, j: (j, h))
    return _pcall(body, name="rwkv_scan_local_bwd", grid=grid, in_specs=[tok] * 6 + [seq, tok, mat, mat] + [tok] * 3,
                  out_specs=[tok] * 6, out_shape=[jax.ShapeDtypeStruct((T, RW), F32)] * 6,
                  compiler_params=_cparams(("parallel", "parallel")))(*toks, dq, dy, da, dsl, *extra)


def _scan_carry_specs(cfg, rev):
    N, RH, C, nc = RWKV_HEAD_DIM, cfg.RH, cfg.C, cfg.T // cfg.C
    at = (lambda j: nc - 1 - j) if rev else (lambda j: j)
    seq = pl.BlockSpec((RH, C, N), lambda j: (0, at(j), 0))
    mat = pl.BlockSpec((RH, 1, N, N), lambda j: (0, at(j), 0, 0))
    return nc, seq, mat


def _scan_carry_fwd(cfg, q, yloc, a, sloc):
    T, RH, N = cfg.T, cfg.RH, RWKV_HEAD_DIM
    nc, seq, mat = _scan_carry_specs(cfg, False)

    def body(q_ref, yl_ref, a_ref, sl_ref, y_ref, ck_ref, s_ref):
        @pl.when(pl.program_id(0) == 0)
        def _():
            s_ref[...] = jnp.zeros_like(s_ref)

        S = s_ref[...]
        ck_ref[:, 0] = S
        y_ref[...] = _merge_heads(_bdot(q_ref[...], S, 2, 2) + yl_ref[...])
        s_ref[...] = _bdot(S, a_ref[:, 0], 2, 1) + sl_ref[:, 0]

    tok = pl.BlockSpec((cfg.C, cfg.RW), lambda j: (j, 0))
    return _pcall(body, name="rwkv_scan_carry_fwd", grid=(nc,), in_specs=[seq, seq, mat, mat], out_specs=[tok, mat],
                  out_shape=[jax.ShapeDtypeStruct((T, cfg.RW), F32), jax.ShapeDtypeStruct((RH, nc, N, N), F32)],
                  scratch_shapes=[pltpu.VMEM((RH, N, N), F32)],
                  compiler_params=_cparams(("arbitrary",)))(q, yloc, a, sloc)


def _scan_carry_bwd(cfg, q, a, ckpt, dy):
    T, RH, N = cfg.T, cfg.RH, RWKV_HEAD_DIM
    nc, seq, mat = _scan_carry_specs(cfg, True)

    def body(q_ref, a_ref, ck_ref, dy_ref, dq_ref, da_ref, dsl_ref, ds_ref):
        @pl.when(pl.program_id(0) == 0)
        def _():
            ds_ref[...] = jnp.zeros_like(ds_ref)

        S, dS, dY = ck_ref[:, 0], ds_ref[...], _split_heads(dy_ref, RH)
        dq_ref[...] = _bdot(dY, S, 2, 1)
        da_ref[:, 0] = _bdot(S, dS, 1, 1)
        dsl_ref[:, 0] = dS
        ds_ref[...] = _bdot(dS, a_ref[:, 0], 2, 2) + _bdot(dY, q_ref[...], 1, 1)

    mt = jax.ShapeDtypeStruct((RH, nc, N, N), F32)
    tok = pl.BlockSpec((cfg.C, cfg.RW), lambda j: (nc - 1 - j, 0))
    return _pcall(body, name="rwkv_scan_carry_bwd", grid=(nc,), in_specs=[seq, mat, mat, tok],
                  out_specs=[seq, mat, mat], out_shape=[jax.ShapeDtypeStruct((RH, T, N), F32), mt, mt],
                  scratch_shapes=[pltpu.VMEM((RH, N, N), F32)],
                  compiler_params=_cparams(("arbitrary",)))(q, a, ckpt, dy)


def _post_fn(y, r, kp, v, zb, ln_w, ln_b, rk, ind, ind_t):
    n = float(RWKV_HEAD_DIM)
    mu = _xdot(_xdot(y, ind, ind_t) / n, ind_t, ind)
    yc = y - mu
    var = _xdot(yc * yc, ind, ind_t) / n
    rstd = _xdot(lax.rsqrt(var + GN_EPS), ind_t, ind)
    yn = yc * rstd * ln_w + ln_b
    bonus = _xdot(_xdot(r * kp * rk, ind, ind_t), ind_t, ind) * v
    return (yn + bonus) * _silu(zb)


def _rwkv_post_fwd(cfg, y, r, kp, v, zb, ln_w, ln_b, rk):
    T, RW, tr = cfg.T, cfg.RW, cfg.tr
    ind, ind_t, _ = _head_indicators(cfg)

    def body(y_ref, r_ref, k_ref, v_ref, z_ref, lw_ref, lb_ref, rk_ref, ind_ref, indt_ref, ob_ref):
        ob_ref[...] = _post_fn(y_ref[...], r_ref[...], k_ref[...], v_ref[...], z_ref[...], lw_ref[...], lb_ref[...],
                               rk_ref[...], ind_ref[...], indt_ref[...]).astype(BF16)

    consts = [ln_w, ln_b, rk, ind, ind_t]
    return _pcall(body, name="rwkv_post_fwd", grid=(T // tr,),
                  in_specs=[_tile(tr, RW)] * 5 + [_const(c.shape) for c in consts],
                  out_specs=_tile(tr, RW), out_shape=jax.ShapeDtypeStruct((T, RW), BF16),
                  compiler_params=_cparams(("parallel",)))(y, r, kp, v, zb, *consts)


def _rwkv_post_bwd(cfg, y, r, kp, v, zb, ln_w, ln_b, rk, dob):
    T, RW = cfg.T, cfg.RW
    tr = min(128, T)
    ind, ind_t, _ = _head_indicators(cfg)

    def body(y_ref, r_ref, k_ref, v_ref, z_ref, lw_ref, lb_ref, rk_ref, ind_ref, indt_ref, dob_ref,
             dy_ref, dr_ref, dk_ref, dv_ref, dz_ref, dlw_ref, dlb_ref, drk_ref):
        fn = functools.partial(_post_fn, ind=ind_ref[...], ind_t=indt_ref[...])
        _, vjp = jax.vjp(fn, y_ref[...], r_ref[...], k_ref[...], v_ref[...], z_ref[...], lw_ref[...], lb_ref[...],
                         rk_ref[...])
        d = vjp(dob_ref[...])
        for ref, val in zip((dy_ref, dr_ref, dk_ref, dv_ref, dz_ref), d[:5]):
            ref[...] = val
        i = pl.program_id(0)
        for ref, val in zip((dlw_ref, dlb_ref, drk_ref), d[5:8]):
            _acc_store(i, ref, val)

    consts = [ln_w, ln_b, rk, ind, ind_t]
    vec = jax.ShapeDtypeStruct((1, RW), F32)
    return _pcall(body, name="rwkv_post_bwd", grid=(T // tr,),
                  in_specs=[_tile(tr, RW)] * 5 + [_const(c.shape) for c in consts] + [_tile(tr, RW)],
                  out_specs=[_tile(tr, RW)] * 5 + [_const((1, RW))] * 3,
                  out_shape=[jax.ShapeDtypeStruct((T, RW), F32)] * 5 + [vec] * 3,
                  compiler_params=_cparams(("arbitrary",)))(y, r, kp, v, zb, *consts, dob)


def _adamw_math(w, g, m, v):
    m = ADAM_B1 * m + (1.0 - ADAM_B1) * g
    v = ADAM_B2 * v + (1.0 - ADAM_B2) * (g * g)
    m_hat = m / (1.0 - ADAM_B1 ** ADAM_STEP)
    v_hat = v / (1.0 - ADAM_B2 ** ADAM_STEP)
    delta = -ADAM_LR * (m_hat / (jnp.sqrt(v_hat) + ADAM_EPS) + ADAM_WD * w)
    return delta, m, v


def _adamw(name, w, g, m, v, copy_grad=False):
    R, Cc = w.shape
    Rp = -(-R // 8) * 8
    tr = Rp
    for nb in range(1, Rp // 8 + 1):
        if (Rp // 8) % nb == 0 and (Rp // nb) * Cc * 4 <= 2 * 1024 * 1024:
            tr = Rp // nb
            break

    def body(w_ref, g_ref, m_ref, v_ref, d_ref, nm_ref, nv_ref, *g_out):
        g_v = g_ref[...]
        d, nm, nv = _adamw_math(w_ref[...], g_v, m_ref[...], v_ref[...])
        d_ref[...] = d
        nm_ref[...] = nm
        nv_ref[...] = nv
        if copy_grad:
            g_out[0][...] = g_v

    spec = _tile(tr, Cc)
    n_out = 4 if copy_grad else 3
    return _pcall(body, name=name, grid=(Rp // tr,), in_specs=[spec] * 4, out_specs=[spec] * n_out,
                  out_shape=[jax.ShapeDtypeStruct((R, Cc), F32)] * n_out,
                  compiler_params=_cparams(("parallel",)))(w, g, m, v)


def _row_tile(R, Cc, itemsize, budget=2 * 1024 * 1024):
    for nb in range(1, R // 16 + 1):
        if R % nb == 0 and (R // nb) % 16 == 0 and (R // nb) * Cc * itemsize <= budget:
            return R // nb
    return R


def _add_halves(name, gs, r1, c_idx):
    _, R, Cc = gs.shape
    half = R // 2
    tr = _row_tile(half, Cc, 4)
    nb = half // tr

    def body(c_ref, g_ref, r_ref, o_ref):
        o_ref[...] = (g_ref[...].astype(F32) + r_ref[...].astype(F32)).astype(BF16)

    grid_spec = pltpu.PrefetchScalarGridSpec(
        num_scalar_prefetch=1, grid=(N_CHIPS, nb),
        in_specs=[pl.BlockSpec((1, tr, Cc), lambda s, i, c: (s, c[0] * nb + i, 0)),
                  pl.BlockSpec((1, tr, Cc), lambda s, i, c: (s, i, 0))],
        out_specs=pl.BlockSpec((1, tr, Cc), lambda s, i, c: (s, i, 0)))
    return _pcall(body, name=name, grid_spec=grid_spec, out_shape=jax.ShapeDtypeStruct((N_CHIPS, half, Cc), BF16),
                  compiler_params=_cparams(("parallel", "parallel")))(c_idx, gs, r1)


def _sum_slots(name, r2):
    S, R, Cc = r2.shape
    tr = _row_tile(R, Cc, 4 * S // 2 if r2.dtype == BF16 else 4 * S)

    def body(r_ref, o_ref):
        acc = r_ref[0].astype(F32)
        for s in range(1, S):
            acc = acc + r_ref[s].astype(F32)
        o_ref[...] = acc

    return _pcall(body, name=name, grid=(R // tr,), in_specs=[pl.BlockSpec((S, tr, Cc), lambda i: (0, i, 0))],
                  out_specs=_tile(tr, Cc), out_shape=jax.ShapeDtypeStruct((R, Cc), F32),
                  compiler_params=_cparams(("parallel",)))(r2)


def _sum_chips(name, recv, own, place):
    S, H, Cc = recv.shape
    tr = _row_tile(H, Cc, 4, 1024 * 1024)
    nb = H // tr

    def body(p_ref, r_ref, own_ref, o_ref):
        s = pl.program_id(1)
        me = p_ref[0]

        @pl.when(s == 0)
        def _():
            o_ref[...] = jnp.zeros_like(o_ref)

        @pl.when(s == me)
        def _():
            o_ref[...] += own_ref[0].astype(F32)

        @pl.when(s != me)
        def _():
            o_ref[...] += r_ref[0].astype(F32)

    grid_spec = pltpu.PrefetchScalarGridSpec(
        num_scalar_prefetch=1, grid=(nb, S),
        in_specs=[pl.BlockSpec((1, tr, Cc), lambda i, s, p: (jnp.where(s == p[0], (s + 1) % S, s), i, 0)),
                  pl.BlockSpec((1, tr, Cc), lambda i, s, p: (p[0], i, 0))],
        out_specs=pl.BlockSpec((tr, Cc), lambda i, s, p: (p[1] * nb + i, 0)))
    return _pcall(body, name=name, grid_spec=grid_spec, out_shape=jax.ShapeDtypeStruct((2 * H, Cc), F32),
                  compiler_params=_cparams(("parallel", "arbitrary")))(place, recv, own)


def _cast_bf16(name, w):
    R, Cc = w.shape
    tr = _row_tile(R, Cc, 4)

    def body(w_ref, o_ref):
        o_ref[...] = w_ref[...].astype(BF16)

    return _pcall(body, name=name, grid=(R // tr,), in_specs=[_tile(tr, Cc)], out_specs=_tile(tr, Cc),
                  out_shape=jax.ShapeDtypeStruct((R, Cc), BF16), compiler_params=_cparams(("parallel",)))(w)


_ANY = pl.BlockSpec(memory_space=pl.ANY)


def _place():
    x, y, c = lax.axis_index("x"), lax.axis_index("y"), lax.axis_index("c")
    others = [(1 - x, y), (x, 1 - y), (1 - x, 1 - y)]
    return x, y, c, others


def _gather_weights(shards):
    n = len(shards)
    halves = [s.shape[0] // 2 for s in shards]

    def body(*refs):
        ins, outs = refs[:n], refs[n:2 * n]
        send_sems, recv_sems = refs[2 * n:]
        x, y, c, others = _place()
        me = 2 * x + y

        def rows(k, ref, chip, hc):
            return ref.at[chip, pl.ds(hc * halves[k], halves[k]), :]

        def remote(k, j, src, dst, to):
            return pltpu.make_async_remote_copy(src_ref=src, dst_ref=dst, send_sem=send_sems.at[6 * k + j],
                                                recv_sem=recv_sems.at[6 * k + j], device_id=to, device_id_type=MESH)

        first, passed = [], []
        for k in range(n):
            mine = ins[k].at[pl.ds(c * halves[k], halves[k]), :]
            for j, (px, py) in enumerate(others):
                cp = remote(k, j, mine, rows(k, outs[k], me, c), (px, py, c))
                cp.start()
                first.append(cp)
        for k in range(n):
            for j, (px, py) in enumerate(others):
                land = rows(k, outs[k], 2 * px + py, c)
                remote(k, j, land, land, (x, y, c)).wait_recv()
                cp = remote(k, 3 + j, land, land, (x, y, 1 - c))
                cp.start()
                passed.append(cp)
        for k in range(n):
            for j, (px, py) in enumerate(others):
                land = rows(k, outs[k], 2 * px + py, 1 - c)
                remote(k, 3 + j, land, land, (x, y, c)).wait_recv()
        for cp in first + passed:
            cp.wait_send()

    return _pcall(
        body, name="gather_weights", in_specs=[_ANY] * n, out_specs=[_ANY] * n,
        out_shape=[jax.ShapeDtypeStruct((N_CHIPS,) + s.shape, s.dtype) for s in shards],
        scratch_shapes=[pltpu.SemaphoreType.DMA((6 * n,)), pltpu.SemaphoreType.DMA((6 * n,))],
    )(*shards)


def _exchange_halves(grads):
    n = len(grads)
    halves = [g.shape[1] // 2 for g in grads]

    def body(*refs):
        ins, outs = refs[:n], refs[n:2 * n]
        send_sems, recv_sems = refs[2 * n:]
        x, y, c, _ = _place()
        cps = []
        for k in range(n):
            src = ins[k].at[:, pl.ds((1 - c) * halves[k], halves[k]), :]
            cp = pltpu.make_async_remote_copy(src_ref=src, dst_ref=outs[k], send_sem=send_sems.at[k],
                                              recv_sem=recv_sems.at[k], device_id=(x, y, 1 - c), device_id_type=MESH)
            cp.start()
            cps.append(cp)
        for cp in cps:
            cp.wait()

    return _pcall(
        body, name="exchange_halves", in_specs=[_ANY] * n, out_specs=[_ANY] * n,
        out_shape=[jax.ShapeDtypeStruct((N_CHIPS, h) + g.shape[2:], g.dtype) for g, h in zip(grads, halves)],
        scratch_shapes=[pltpu.SemaphoreType.DMA((n,)), pltpu.SemaphoreType.DMA((n,))],
    )(*grads)


def _scatter_to_owners(chip_sums, small):
    n = len(chip_sums)

    def body(*refs):
        ins, small_in = refs[:n], refs[n]
        outs, small_out = refs[n + 1:2 * n + 1], refs[2 * n + 1]
        send_sems, recv_sems, local_sem, ssend, srecv = refs[2 * n + 2:]
        x, y, c, others = _place()
        me = 2 * x + y
        dev = 2 * me + c
        local = pltpu.make_async_copy(small_in, small_out.at[dev], local_sem)
        local.start()
        sends = []
        for k in range(n):
            for j, (px, py) in enumerate(others):
                cp = pltpu.make_async_remote_copy(
                    src_ref=ins[k].at[2 * px + py], dst_ref=outs[k].at[me], send_sem=send_sems.at[3 * k + j],
                    recv_sem=recv_sems.at[3 * k + j], device_id=(px, py, c), device_id_type=MESH)
                cp.start()
                sends.append(cp)
        rel = [(dx, dy, dc) for dx in (0, 1) for dy in (0, 1) for dc in (0, 1)][1:]
        for r, (dx, dy, dc) in enumerate(rel):
            to = (x ^ dx, y ^ dy, c ^ dc)
            cp = pltpu.make_async_remote_copy(src_ref=small_in, dst_ref=small_out.at[dev], send_sem=ssend.at[r],
                                              recv_sem=srecv.at[r], device_id=to, device_id_type=MESH)
            cp.start()
            sends.append(cp)
        for k in range(n):
            for j, (px, py) in enumerate(others):
                land = outs[k].at[2 * px + py]
                pltpu.make_async_remote_copy(src_ref=land, dst_ref=land, send_sem=send_sems.at[3 * k + j],
                                             recv_sem=recv_sems.at[3 * k + j], device_id=(x, y, c),
                                             device_id_type=MESH).wait_recv()
        for r, (dx, dy, dc) in enumerate(rel):
            land = small_out.at[4 * (x ^ dx) + 2 * (y ^ dy) + (c ^ dc)]
            pltpu.make_async_remote_copy(src_ref=land, dst_ref=land, send_sem=ssend.at[r], recv_sem=srecv.at[r],
                                         device_id=(x, y, c), device_id_type=MESH).wait_recv()
        for cp in sends:
            cp.wait_send()
        local.wait()

    return _pcall(
        body, name="scatter_to_owners", in_specs=[_ANY] * (n + 1), out_specs=[_ANY] * (n + 1),
        out_shape=[jax.ShapeDtypeStruct(g.shape, g.dtype) for g in chip_sums]
        + [jax.ShapeDtypeStruct((N_DEV,) + small.shape, small.dtype)],
        scratch_shapes=[pltpu.SemaphoreType.DMA((3 * n,)), pltpu.SemaphoreType.DMA((3 * n,)),
                        pltpu.SemaphoreType.DMA, pltpu.SemaphoreType.DMA((7,)), pltpu.SemaphoreType.DMA((7,))],
    )(*chip_sums, small)


def _join_halves(fulls):
    n = len(fulls)
    hs = [f.shape[0] // 2 for f in fulls]

    def body(*refs):
        ins, outs = refs[:n], refs[n:2 * n]
        send_sems, recv_sems = refs[2 * n:]
        x, y, c, _ = _place()
        cps = []
        for k in range(n):
            mine = pl.ds(c * hs[k], hs[k])
            cp = pltpu.make_async_remote_copy(src_ref=ins[k].at[mine, :], dst_ref=outs[k].at[mine, :],
                                              send_sem=send_sems.at[k], recv_sem=recv_sems.at[k],
                                              device_id=(x, y, 1 - c), device_id_type=MESH)
            cp.start()
            cps.append(cp)
        for k in range(n):
            land = outs[k].at[pl.ds((1 - c) * hs[k], hs[k]), :]
            pltpu.make_async_remote_copy(src_ref=land, dst_ref=land, send_sem=send_sems.at[k],
                                         recv_sem=recv_sems.at[k], device_id=(x, y, c), device_id_type=MESH).wait_recv()
        for cp in cps:
            cp.wait_send()

    return _pcall(
        body, name="join_halves", in_specs=[_ANY] * n, out_specs=[_ANY] * n,
        out_shape=[jax.ShapeDtypeStruct(f.shape, f.dtype) for f in fulls],
        input_output_aliases={k: k for k in range(n)},
        scratch_shapes=[pltpu.SemaphoreType.DMA((n,)), pltpu.SemaphoreType.DMA((n,))],
    )(*fulls)


def _local_step(cfg, x2, target, norm_gain, w_my, fb, mu_g, w0, w2, a0, a2, k_k, k_a, r_k, ln_w, ln_b, wpf, wpr, wout,
                fng):
    T, D, FW, FH, RW, RH, LP, lora = cfg.T, cfg.D, cfg.FW, cfg.FH, cfg.RW, cfg.RH, cfg.LP, cfg.lora
    fb_p = jnp.pad(fb, ((0, 0), (0, LANES - FH)))
    mu = _rwkv_vec_to_my(cfg, mu_g)
    w2p = jnp.pad(w2, ((0, LP - lora), (0, 0)))
    a2p = jnp.pad(a2, ((0, LP - lora), (0, 0)))
    rk = r_k.reshape(1, RW)
    tm = min(1024, T)

    h = _rms_fwd(cfg, x2, norm_gain)
    u = _mm("in_proj", h, w_my, "nn", F32, tm, cfg.tn, 512)
    c_cols = _fox_prep(cfg, u, fb_p)
    c_rows = c_cols[:, :FH].T.reshape(FH, 1, T)
    o, lse = _attn_fwd(cfg, u, c_rows)
    oa = _gate_a_fwd(cfg, o, u)
    prep = _rwkv_prep_fwd(cfg, u, mu, w0, w2p, a0, a2p, k_k, k_a)
    r, lw, kp, v, an, b, zb = prep
    toks = [r, lw, kp, v, an, b]
    q_s, yloc, a_m, sloc = _scan_local_fwd(cfg, toks)
    y, ckpt = _scan_carry_fwd(cfg, q_s, yloc, a_m, sloc)
    ob = _rwkv_post_fwd(cfg, y, r, kp, v, zb, ln_w, ln_b, rk)
    pa = _mm("proj_fox", oa, wpf, "nn", F32, tm, 1024, 512)
    pb = _mm("proj_rwkv", ob, wpr, "nn", F32, tm, 1024, 512)
    m = _merge_fwd(cfg, pa, pb, u)
    mo = _mm("out_proj", m, wout, "nn", F32, tm, 1024, 512)
    loss8, dres, dres16, d_fng = _final(cfg, x2, mo, fng.reshape(1, D), target)

    dm = _mm("out_proj_dx", dres16, wout, "nt", F32, tm, 1024, 512)
    d_wout = _mm("out_proj_dw", m, dres16, "tn", BF16, 1024, 1024, 512)
    dpa, dpb, du = _merge_bwd(cfg, pa, pb, u, dm)
    doa = _mm("proj_fox_dx", dpa, wpf, "nt", F32, tm, 1024, 512)
    d_wpf = _mm("proj_fox_dw", oa, dpa, "tn", BF16, 1024, 1024, 512)
    dob = _mm("proj_rwkv_dx", dpb, wpr, "nt", F32, tm, 1024, 512)
    d_wpr = _mm("proj_rwkv_dw", ob, dpb, "tn", BF16, 1024, 1024, 512)

    do, du = _gate_a_bwd(cfg, o, u, doa, du)
    du, dcol = _attn_bwd(cfg, u, c_rows, lse, do, du)
    dc = jnp.pad(-dcol.reshape(FH, T).T, ((0, 0), (0, LANES - FH)))
    df, d_fb = _fox_prep_bwd(cfg, u, fb_p, dc)

    dy, dr_p, dk_p, dv_p, dzb, d_lnw, d_lnb, d_rk = _rwkv_post_bwd(cfg, y, r, kp, v, zb, ln_w, ln_b, rk, dob)
    dq_s, da_m, dsl = _scan_carry_bwd(cfg, q_s, a_m, ckpt, dy)
    cots = _scan_local_bwd(cfg, toks, dq_s, dy, da_m, dsl, [dr_p, dk_p, dv_p])
    dus, d_mu, d_w0, d_w2p, d_a0, d_a2p, d_kk, d_ka = _rwkv_prep_bwd(cfg, u, mu, w0, w2p, a0, a2p, k_k, k_a, cots, dzb)
    du = _shift_bwd(cfg, dus, mu, df, du)
    dh = _mm("in_proj_dx", du, w_my, "nt", F32, tm, 1024, cfg.tn)
    d_wmy = _mm("in_proj_dw", du, h, "tn", BF16, cfg.tn, 1024, 512)
    gx, d_ng = _rms_bwd(cfg, x2, norm_gain, dh, dres)

    small = dict(norm_gain=d_ng, fox_forget_bias=d_fb[:, :FH], rwkv_shift_mix=_rwkv_vec_from_my(cfg, d_mu),
                 rwkv_w0=d_w0, rwkv_a0=d_a0, rwkv_k_k=d_kk, rwkv_k_a=d_ka, rwkv_r_k=d_rk, rwkv_ln_w=d_lnw,
                 rwkv_ln_b=d_lnb, final_norm_gain=d_fng)
    big = dict(w_in=d_wmy, rwkv_w2=d_w2p[:lora], rwkv_a2=d_a2p[:lora], w_proj_fox=d_wpf, w_proj_rwkv=d_wpr,
               w_out=d_wout)
    return loss8[0, 0], gx, small, big


_SMALL = ["norm_gain", "fox_forget_bias", "rwkv_shift_mix", "rwkv_w0", "rwkv_a0", "rwkv_k_k", "rwkv_k_a", "rwkv_r_k",
          "rwkv_ln_w", "rwkv_ln_b", "final_norm_gain"]
_WEIGHTS = ["norm_gain", "w_in", "fox_forget_bias", "rwkv_shift_mix", "rwkv_w0", "rwkv_w2", "rwkv_a0", "rwkv_a2",
            "rwkv_k_k", "rwkv_k_a", "rwkv_r_k", "rwkv_ln_w", "rwkv_ln_b", "w_proj_fox", "w_proj_rwkv", "w_out",
            "final_norm_gain"]


def _pack_small(arrs):
    parts = []
    for a in arrs:
        f = a.reshape(-1)
        parts.append(jnp.pad(f, (0, (-f.shape[0]) % LANES)))
    flat = jnp.concatenate(parts)
    rows = flat.shape[0] // LANES
    flat = jnp.pad(flat, (0, ((-rows) % 8) * LANES))
    return flat.reshape(-1, LANES)


def _unpack_small(packed, shapes):
    flat = packed.reshape(-1)
    out, pos = [], 0
    for s in shapes:
        n = int(np.prod(s))
        out.append(flat[pos:pos + n].reshape(s))
        pos += n + ((-n) % LANES)
    return out


def _shard_major(a, axis):
    parts = jnp.split(a, N_CHIPS, axis=axis)
    return jnp.stack(parts, axis=0)


def kernel(x, norm_gain, w_in, fox_forget_bias, rwkv_shift_mix, rwkv_w0, rwkv_w2, rwkv_a0, rwkv_a2, rwkv_k_k, rwkv_k_a, rwkv_r_k, rwkv_ln_w, rwkv_ln_b, w_proj_fox, w_proj_rwkv, w_out, final_norm_gain, loss_target, m_norm_gain, m_w_in, m_fox_forget_bias, m_rwkv_shift_mix, m_rwkv_w0, m_rwkv_w2, m_rwkv_a0, m_rwkv_a2, m_rwkv_k_k, m_rwkv_k_a, m_rwkv_r_k, m_rwkv_ln_w, m_rwkv_ln_b, m_w_proj_fox, m_w_proj_rwkv, m_w_out, m_final_norm_gain, v_norm_gain, v_w_in, v_fox_forget_bias, v_rwkv_shift_mix, v_rwkv_w0, v_rwkv_w2, v_rwkv_a0, v_rwkv_a2, v_rwkv_k_k, v_rwkv_k_a, v_rwkv_r_k, v_rwkv_ln_w, v_rwkv_ln_b, v_w_proj_fox, v_w_proj_rwkv, v_w_out, v_final_norm_gain):
    args = dict(locals())
    T, D = x.shape[1], x.shape[2]
    lora = rwkv_w2.shape[1]
    cfg = _Cfg(T, D, lora)
    RW = cfg.RW
    c_idx = lax.axis_index("c").astype(jnp.int32).reshape(1)
    me_chip = (2 * lax.axis_index("x") + lax.axis_index("y")).astype(jnp.int32)
    place = jnp.concatenate([me_chip.reshape(1), c_idx])

    w_in_t, m_in_t, v_in_t = w_in[0].T, m_w_in[0].T, v_w_in[0].T
    w_in_s = w_in[0].astype(BF16)
    wp_s = jnp.concatenate([w_proj_fox[0], w_proj_rwkv[0]], axis=0)
    lora_s = jnp.concatenate([rwkv_w2[0], rwkv_a2[0]], axis=0)
    mine = [w_in_s, _cast_bf16("cast_w_proj", wp_s), _cast_bf16("cast_w_out", w_out[0]), lora_s]
    gathered = _gather_weights(mine)
    g_in, g_wp, g_out, g_lora = [lax.dynamic_update_slice(g, own[None], (me_chip, 0, 0))
                                 for g, own in zip(gathered, mine)]
    w_my = _shards_to_my_layout(cfg, g_in)
    wp = g_wp.transpose(1, 0, 2).reshape(2 * RW, D)
    wout = g_out.reshape(D, D)
    lo = g_lora.transpose(1, 0, 2).reshape(2 * lora, RW)

    loss_dev, gx, small, big = _local_step(
        cfg, x[0], loss_target[0], norm_gain, w_my, fox_forget_bias, rwkv_shift_mix, rwkv_w0, lo[:lora], rwkv_a0,
        lo[lora:], rwkv_k_k, rwkv_k_a, rwkv_r_k, rwkv_ln_w, rwkv_ln_b, wp[:RW], wp[RW:], wout, final_norm_gain)
    loss = lax.psum(loss_dev, ("x", "y", "c"))

    gs_in = _my_layout_to_shards(cfg, big["w_in"])
    gs_wp = _shard_major(jnp.concatenate([big["w_proj_fox"], big["w_proj_rwkv"]], axis=0), 1)
    gs_out = _shard_major(big["w_out"], 0)
    gs_lora = _shard_major(jnp.concatenate([big["rwkv_w2"], big["rwkv_a2"]], axis=0).astype(BF16), 1)
    gs = [gs_in, gs_wp, gs_out, gs_lora]
    names = ["w_in", "w_proj", "w_out", "lora"]
    recv1 = _exchange_halves(gs)
    chip_sums = [_add_halves("add_halves_" + nm, g, r, c_idx) for nm, g, r in zip(names, gs, recv1)]
    small_shapes = [args[nm].shape for nm in _SMALL]
    packed = _pack_small([small[nm] for nm in _SMALL])
    *recv2, small_all = _scatter_to_owners(chip_sums, packed)
    reduced = [_sum_chips("sum_chips_" + nm, r, own, place) for nm, r, own in zip(names, recv2, chip_sums)]
    g_small = _sum_slots("sum_small", small_all)
    g_in_f, g_wp_f, g_out_f, g_lora_f = _join_halves(reduced)

    grads = dict(zip(_SMALL, _unpack_small(g_small, small_shapes)))
    grads["w_proj_fox"] = g_wp_f[None, :RW]
    grads["w_proj_rwkv"] = g_wp_f[None, RW:]
    grads["w_out"] = g_out_f[None]
    grads["rwkv_w2"] = g_lora_f[None, :lora]
    grads["rwkv_a2"] = g_lora_f[None, lora:]

    delta, new_m, new_v = {}, {}, {}
    w_small = _pack_small([args[nm] for nm in _SMALL])
    m_small = _pack_small([args["m_" + nm] for nm in _SMALL])
    v_small = _pack_small([args["v_" + nm] for nm in _SMALL])
    d_s, m_s, v_s = _adamw("adamw_small", w_small, g_small, m_small, v_small)
    for tgt, pk in ((delta, d_s), (new_m, m_s), (new_v, v_s)):
        tgt.update(zip(_SMALL, _unpack_small(pk, small_shapes)))
    d_t, m_t, v_t, g_t = _adamw("adamw_w_in", w_in_t, g_in_f, m_in_t, v_in_t, copy_grad=True)
    grads["w_in"], delta["w_in"], new_m["w_in"], new_v["w_in"] = [t.T[None] for t in (g_t, d_t, m_t, v_t)]
    for nm in ("w_proj_fox", "w_proj_rwkv", "w_out", "rwkv_w2", "rwkv_a2"):
        shp = args[nm].shape
        two_d = (shp[1], shp[2])
        d_b, m_b, v_b = _adamw("adamw_" + nm, args[nm].reshape(two_d), grads[nm].reshape(two_d),
                               args["m_" + nm].reshape(two_d), args["v_" + nm].reshape(two_d))
        delta[nm], new_m[nm], new_v[nm] = d_b.reshape(shp), m_b.reshape(shp), v_b.reshape(shp)

    return (loss, gx[None], *[grads[n] for n in _WEIGHTS], *[delta[n] for n in _WEIGHTS],
            *[new_m[n] for n in _WEIGHTS], *[new_v[n] for n in _WEIGHTS])
```

```python
import functools

import numpy as np
import jax
import jax.numpy as jnp
from jax import lax
from jax.experimental import pallas as pl
from jax.experimental.pallas import tpu as pltpu

F32 = jnp.float32
BF16 = jnp.bfloat16
HI = lax.Precision.HIGHEST
MESH = pl.DeviceIdType.MESH

FOX_HEAD_DIM = 128
RWKV_HEAD_DIM = 64
RMS_EPS = 1e-6
GN_EPS = 64e-5
L2_EPS = 1e-12
ADAM_LR = 0.001
ADAM_B1 = 0.9
ADAM_B2 = 0.999
ADAM_EPS = 1e-08
ADAM_WD = 0.01
ADAM_STEP = 10

LANES = 128
VMEM_LIMIT = 56 * 1024 * 1024
SCAN_CHUNK = 64
SCAN_HEADS_PER_STEP = 8
SCAN_PASSES = (3, 1, 1)
N_CHIPS = 4
N_DEV = 8

_pcall = pl.pallas_call


def _cparams(sem=None):
    return pltpu.CompilerParams(dimension_semantics=sem, vmem_limit_bytes=VMEM_LIMIT)


def _softplus(x):
    return jnp.maximum(x, 0.0) + jnp.log(1.0 + jnp.exp(-jnp.abs(x)))


def _silu(z):
    return z * jax.nn.sigmoid(z)


def _rmsn(x, g):
    return x * lax.rsqrt(jnp.mean(x * x, axis=-1, keepdims=True) + RMS_EPS) * g


def _dot(a, b, dims="nn", precision=None):
    dn = {"nn": (((1,), (0,)), ((), ())), "nt": (((1,), (1,)), ((), ())), "tn": (((0,), (0,)), ((), ()))}[dims]
    return lax.dot_general(a, b, dn, precision=precision, preferred_element_type=F32)


def _split_bf16(x):
    hi = x.astype(BF16)
    return hi, (x - hi.astype(F32)).astype(BF16)


def _bdot_raw(a, b, ca, cb, passes):
    dn = (((ca,), (cb,)), ((0,), (0,)))
    mm = lambda p, q: lax.dot_general(p, q, dn, preferred_element_type=F32)
    if passes == 1:
        return mm(a.astype(BF16), b.astype(BF16))
    ah, al = _split_bf16(a)
    bh, bl = _split_bf16(b)
    return mm(ah, bh) + (mm(ah, bl) + mm(al, bh))


@functools.partial(jax.custom_vjp, nondiff_argnums=(2, 3, 4))
def _bdot_p(a, b, ca, cb, passes):
    return _bdot_raw(a, b, ca, cb, passes)


def _bdot_fwd(a, b, ca, cb, passes):
    return _bdot_raw(a, b, ca, cb, passes), (a, b)


def _bdot_bwd(ca, cb, passes, res, g):
    a, b = res
    if (ca, cb) == (2, 1):
        return _bdot_p(g, b, 2, 2, passes), _bdot_p(a, g, 1, 1, passes)
    if (ca, cb) == (2, 2):
        return _bdot_p(g, b, 2, 1, passes), _bdot_p(g, a, 1, 1, passes)
    assert (ca, cb) == (1, 1)
    return _bdot_p(b, g, 2, 2, passes), _bdot_p(a, g, 2, 1, passes)


_bdot_p.defvjp(_bdot_fwd, _bdot_bwd)


def _bdot(a, b, ca, cb, passes=3):
    return _bdot_p(a, b, ca, cb, passes)


def _dot3(a, b):
    return _bdot(a[None], b[None], 2, 1)[0]


@jax.custom_vjp
def _xdot(x, m, mt):
    hi, lo = _split_bf16(x)
    m16 = m.astype(BF16)
    return _dot(hi, m16) + _dot(lo, m16)


def _xdot_fwd(x, m, mt):
    return _xdot(x, m, mt), (m, mt)


def _xdot_bwd(res, g):
    m, mt = res
    return _xdot(g, mt, m), jnp.zeros_like(m), jnp.zeros_like(mt)


_xdot.defvjp(_xdot_fwd, _xdot_bwd)


class _Cfg:
    def __init__(self, T, D, lora):
        self.T, self.D, self.lora = T, D, lora
        self.FW = D // 2
        self.FH = self.FW // FOX_HEAD_DIM
        self.RW = D // 2
        self.RH = self.RW // RWKV_HEAD_DIM
        self.LP = -(-lora // LANES) * LANES
        self.o_fox = 0
        self.o_rwkv = 4 * self.FW
        self.o_gate = self.o_rwkv + 4 * self.RW
        self.o_f = self.o_gate + 2 * D
        self.o_wd = self.o_f + LANES
        self.o_ad = self.o_wd + self.LP
        end = self.o_ad + self.LP
        self.tn = 1280 if D >= 2048 else LANES
        self.ncol = -(-end // self.tn) * self.tn
        self.in_cols = 4 * self.FW + self.FH + 4 * self.RW + 2 * lora + 2 * D
        self.scp = -(-(self.in_cols // N_CHIPS) // LANES) * LANES
        self.rseg = 4 * self.RW + 2 * self.LP
        self.C = min(SCAN_CHUNK, T)
        self.tr = min(256, T)
        self.hb = min(SCAN_HEADS_PER_STEP, self.RH)

    def segments(self):
        FW, FH, RW, lo, D = self.FW, self.FH, self.RW, self.lora, self.D
        g_f = 4 * FW
        g_r = g_f + FH
        g_wd = g_r + 4 * RW
        g_ad = g_wd + lo
        g_g = g_ad + lo
        dh = FOX_HEAD_DIM
        qkv = [(j * FW + h * dh, dh, (3 * h + j) * dh) for h in range(FH) for j in range(3)]
        return qkv + [(3 * FW, FW, 3 * FW), (g_f, FH, self.o_f), (g_r, 4 * RW, self.o_rwkv), (g_wd, lo, self.o_wd),
                      (g_ad, lo, self.o_ad), (g_g, 2 * D, self.o_gate)]


def _to_my_layout(cfg, wg):
    R = wg.shape[0]
    segs = sorted(cfg.segments(), key=lambda s: s[2])
    parts, pos = [], 0
    for g0, w, m0 in segs:
        if m0 > pos:
            parts.append(jnp.zeros((R, m0 - pos), wg.dtype))
        parts.append(wg[:, g0:g0 + w])
        pos = m0 + w
    if cfg.ncol > pos:
        parts.append(jnp.zeros((R, cfg.ncol - pos), wg.dtype))
    return jnp.concatenate(parts, axis=1)


def _from_my_layout(cfg, wm):
    segs = sorted(cfg.segments(), key=lambda s: s[0])
    return jnp.concatenate([wm[:, m0:m0 + w] for g0, w, m0 in segs], axis=1)


def _shards_to_my_layout(cfg, g):
    R, sc = g.shape[1], g.shape[2]
    segs = sorted(cfg.segments(), key=lambda s: s[2])
    parts, pos = [], 0
    for g0, w, m0 in segs:
        if m0 > pos:
            parts.append(jnp.zeros((R, m0 - pos), g.dtype))
        for s in range(N_CHIPS):
            lo, hi = max(g0, s * sc), min(g0 + w, (s + 1) * sc)
            if lo < hi:
                parts.append(g[s, :, lo - s * sc:hi - s * sc])
        pos = m0 + w
    if cfg.ncol > pos:
        parts.append(jnp.zeros((R, cfg.ncol - pos), g.dtype))
    return jnp.concatenate(parts, axis=1)


def _my_layout_to_shards(cfg, wm):
    sc = cfg.in_cols // N_CHIPS
    segs = sorted(cfg.segments(), key=lambda s: s[0])
    shards = []
    for s in range(N_CHIPS):
        parts = []
        for g0, w, m0 in segs:
            lo, hi = max(g0, s * sc), min(g0 + w, (s + 1) * sc)
            if lo < hi:
                parts.append(wm[:, m0 + lo - g0:m0 + hi - g0])
        shards.append(jnp.concatenate(parts, axis=1))
    return jnp.stack(shards, axis=0)


def _rwkv_vec_to_my(cfg, v):
    RW4, lo, LP = 4 * cfg.RW, cfg.lora, cfg.LP
    z = jnp.zeros((1, LP - lo), v.dtype)
    return jnp.concatenate([v[:, :RW4], v[:, RW4:RW4 + lo], z, v[:, RW4 + lo:], z], axis=1)


def _rwkv_vec_from_my(cfg, v):
    RW4, lo, LP = 4 * cfg.RW, cfg.lora, cfg.LP
    return jnp.concatenate([v[:, :RW4], v[:, RW4:RW4 + lo], v[:, RW4 + LP:RW4 + LP + lo]], axis=1)


def _mm(name, a, b, dims, out_dtype, tm, tn, tk):
    (M, K) = a.shape if dims != "tn" else a.shape[::-1]
    N = b.shape[0] if dims == "nt" else b.shape[1]
    tm, tn, tk = min(tm, M), min(tn, N), min(tk, K)
    assert M % tm == 0 and N % tn == 0 and K % tk == 0, (name, M, N, K, tm, tn, tk)
    nk = K // tk
    if dims == "nn":
        a_spec = pl.BlockSpec((tm, tk), lambda i, j, k: (i, k))
        b_spec = pl.BlockSpec((tk, tn), lambda i, j, k: (k, j))
    elif dims == "nt":
        a_spec = pl.BlockSpec((tm, tk), lambda i, j, k: (i, k))
        b_spec = pl.BlockSpec((tn, tk), lambda i, j, k: (j, k))
    else:
        a_spec = pl.BlockSpec((tk, tm), lambda i, j, k: (k, i))
        b_spec = pl.BlockSpec((tk, tn), lambda i, j, k: (k, j))

    def body(a_ref, b_ref, o_ref, *acc):
        if nk == 1:
            o_ref[...] = _dot(a_ref[...], b_ref[...], dims).astype(o_ref.dtype)
            return
        acc_ref, k = acc[0], pl.program_id(2)

        @pl.when(k == 0)
        def _():
            acc_ref[...] = jnp.zeros_like(acc_ref)

        acc_ref[...] += _dot(a_ref[...], b_ref[...], dims)

        @pl.when(k == nk - 1)
        def _():
            o_ref[...] = acc_ref[...].astype(o_ref.dtype)

    return _pcall(
        body, name=name, grid=(M // tm, N // tn, nk),
        in_specs=[a_spec, b_spec], out_specs=pl.BlockSpec((tm, tn), lambda i, j, k: (i, j)),
        out_shape=jax.ShapeDtypeStruct((M, N), out_dtype),
        scratch_shapes=[pltpu.VMEM((tm, tn), F32)] if nk > 1 else [],
        compiler_params=_cparams(("parallel", "parallel", "arbitrary")),
    )(a, b)


def _tile(tr, w, cb=0):
    return pl.BlockSpec((tr, w), lambda i: (i, cb))


def _const(shape):
    nd = len(shape)
    return pl.BlockSpec(shape, lambda i: (0,) * nd)


def _acc_store(i, ref, val):
    @pl.when(i == 0)
    def _():
        ref[...] = val

    @pl.when(i > 0)
    def _():
        ref[...] += val


def _rms_fwd(cfg, x2, g):
    T, D, tr = cfg.T, cfg.D, cfg.tr

    def body(x_ref, g_ref, h_ref):
        h_ref[...] = _rmsn(x_ref[...], g_ref[...]).astype(BF16)

    return _pcall(body, name="rms_fwd", grid=(T // tr,), in_specs=[_tile(tr, D), _const((1, D))],
                  out_specs=_tile(tr, D), out_shape=jax.ShapeDtypeStruct((T, D), BF16),
                  compiler_params=_cparams(("parallel",)))(x2, g)


def _rms_bwd(cfg, x2, g, dh, dres):
    T, D, tr = cfg.T, cfg.D, cfg.tr

    def body(x_ref, g_ref, dh_ref, dres_ref, gx_ref, dg_ref):
        _, vjp = jax.vjp(_rmsn, x_ref[...], g_ref[...])
        dx, dg = vjp(dh_ref[...])
        gx_ref[...] = dx + dres_ref[...]
        _acc_store(pl.program_id(0), dg_ref, dg)

    return _pcall(body, name="rms_bwd", grid=(T // tr,),
                  in_specs=[_tile(tr, D), _const((1, D)), _tile(tr, D), _tile(tr, D)],
                  out_specs=[_tile(tr, D), _const((1, D))],
                  out_shape=[jax.ShapeDtypeStruct((T, D), F32), jax.ShapeDtypeStruct((1, D), F32)],
                  compiler_params=_cparams(("arbitrary",)))(x2, g, dh, dres)


def _final(cfg, x2, mo, fg, target):
    T, D, tr = cfg.T, cfg.D, cfg.tr

    def loss_fn(hres, g, tgt):
        err = _rmsn(hres, g) - tgt
        return 0.5 * jnp.sum(jnp.mean(err * err, axis=-1, keepdims=True), axis=0, keepdims=True)

    def body(x_ref, mo_ref, g_ref, t_ref, loss_ref, dres_ref, dres16_ref, dg_ref):
        hres = x_ref[...] + mo_ref[...]
        loss, vjp = jax.vjp(functools.partial(loss_fn, tgt=t_ref[...]), hres, g_ref[...])
        dres, dg = vjp(jnp.ones((1, 1), F32))
        dres_ref[...] = dres
        dres16_ref[...] = dres.astype(BF16)
        i = pl.program_id(0)
        _acc_store(i, dg_ref, dg)
        _acc_store(i, loss_ref, jnp.broadcast_to(loss, (8, LANES)))

    return _pcall(body, name="final_loss", grid=(T // tr,),
                  in_specs=[_tile(tr, D), _tile(tr, D), _const((1, D)), _tile(tr, D)],
                  out_specs=[_const((8, LANES)), _tile(tr, D), _tile(tr, D), _const((1, D))],
                  out_shape=[jax.ShapeDtypeStruct((8, LANES), F32), jax.ShapeDtypeStruct((T, D), F32),
                             jax.ShapeDtypeStruct((T, D), BF16), jax.ShapeDtypeStruct((1, D), F32)],
                  compiler_params=_cparams(("arbitrary",)))(x2, mo, fg, target)


def _merge_fn(pa, pb, ga, gb):
    return jax.nn.sigmoid(ga) * pa + jax.nn.sigmoid(gb) * pb


def _merge_fwd(cfg, pa, pb, u):
    T, D, tr = cfg.T, cfg.D, cfg.tr
    cga, cgb = cfg.o_gate // D, cfg.o_gate // D + 1

    def body(pa_ref, pb_ref, ga_ref, gb_ref, m_ref):
        m_ref[...] = _merge_fn(pa_ref[...], pb_ref[...], ga_ref[...], gb_ref[...]).astype(BF16)

    return _pcall(body, name="merge_fwd", grid=(T // tr,),
                  in_specs=[_tile(tr, D), _tile(tr, D), _tile(tr, D, cga), _tile(tr, D, cgb)],
                  out_specs=_tile(tr, D), out_shape=jax.ShapeDtypeStruct((T, D), BF16),
                  compiler_params=_cparams(("parallel",)))(pa, pb, u, u)


def _merge_bwd(cfg, pa, pb, u, dm):
    T, D, tr = cfg.T, cfg.D, cfg.tr
    cga, cgb = cfg.o_gate // D, cfg.o_gate // D + 1

    def body(pa_ref, pb_ref, ga_ref, gb_ref, dm_ref, dpa_ref, dpb_ref, dg_ref):
        _, vjp = jax.vjp(_merge_fn, pa_ref[...], pb_ref[...], ga_ref[...], gb_ref[...])
        dpa, dpb, dga, dgb = vjp(dm_ref[...])
        dpa_ref[...] = dpa.astype(BF16)
        dpb_ref[...] = dpb.astype(BF16)
        dg_ref[:, :D] = dga.astype(BF16)
        dg_ref[:, D:] = dgb.astype(BF16)

    return _pcall(body, name="merge_bwd", grid=(T // tr,),
                  in_specs=[_tile(tr, D), _tile(tr, D), _tile(tr, D, cga), _tile(tr, D, cgb), _tile(tr, D)],
                  out_specs=[_tile(tr, D), _tile(tr, D), _tile(tr, 2 * D, cfg.o_gate // (2 * D))],
                  out_shape=[jax.ShapeDtypeStruct((T, D), BF16), jax.ShapeDtypeStruct((T, D), BF16),
                             jax.ShapeDtypeStruct((T, cfg.ncol), BF16)],
                  compiler_params=_cparams(("parallel",)))(pa, pb, u, u, dm)


def _gate_fn(o, z):
    return o * _silu(z)


def _gate_a_fwd(cfg, o, u):
    T, FW, tr = cfg.T, cfg.FW, cfg.tr

    def body(o_ref, z_ref, oa_ref):
        oa_ref[...] = _gate_fn(o_ref[...], z_ref[...]).astype(BF16)

    return _pcall(body, name="gate_a_fwd", grid=(T // tr,), in_specs=[_tile(tr, FW), _tile(tr, FW, 3)],
                  out_specs=_tile(tr, FW), out_shape=jax.ShapeDtypeStruct((T, FW), BF16),
                  compiler_params=_cparams(("parallel",)))(o, u)


def _gate_a_bwd(cfg, o, u, doa, du):
    T, FW, tr = cfg.T, cfg.FW, cfg.tr

    def body(o_ref, z_ref, doa_ref, du_in, do_ref, dz_ref):
        _, vjp = jax.vjp(_gate_fn, o_ref[...], z_ref[...])
        do, dz = vjp(doa_ref[...])
        do_ref[...] = do
        dz_ref[...] = dz.astype(BF16)

    return _pcall(body, name="gate_a_bwd", grid=(T // tr,),
                  in_specs=[_tile(tr, FW), _tile(tr, FW, 3), _tile(tr, FW), _ANY],
                  out_specs=[_tile(tr, FW), _tile(tr, FW, 3)],
                  out_shape=[jax.ShapeDtypeStruct((T, FW), F32), jax.ShapeDtypeStruct(du.shape, BF16)],
                  input_output_aliases={3: 1},
                  compiler_params=_cparams(("parallel",)))(o, u, doa, du)


def _fox_prep(cfg, u, fb):
    T, tr = cfg.T, cfg.tr
    cf = cfg.o_f // LANES

    def body(f_ref, fb_ref, c_ref, carry_ref):
        i = pl.program_id(0)

        @pl.when(i == 0)
        def _():
            carry_ref[...] = jnp.zeros_like(carry_ref)

        lf = -_softplus(-(f_ref[...] + fb_ref[...]))
        r = lax.broadcasted_iota(jnp.int32, (tr, tr), 0)
        c = lax.broadcasted_iota(jnp.int32, (tr, tr), 1)
        tri = (r >= c).astype(F32)
        c_ref[...] = _dot(tri, lf, precision=HI) + carry_ref[...]
        carry_ref[...] += jnp.sum(lf, axis=0, keepdims=True)

    return _pcall(body, name="fox_prep", grid=(T // tr,), in_specs=[_tile(tr, LANES, cf), _const((1, LANES))],
                  out_specs=_tile(tr, LANES), out_shape=jax.ShapeDtypeStruct((T, LANES), F32),
                  scratch_shapes=[pltpu.VMEM((1, LANES), F32)], compiler_params=_cparams(("arbitrary",)))(u, fb)


def _fox_prep_bwd(cfg, u, fb, dc):
    T, tr = cfg.T, cfg.tr
    cf = cfg.o_f // LANES
    nb = T // tr

    def body(f_ref, fb_ref, dc_ref, df_ref, dfb_ref, carry_ref):
        i = pl.program_id(0)

        @pl.when(i == 0)
        def _():
            carry_ref[...] = jnp.zeros_like(carry_ref)

        dc = dc_ref[...]
        r = lax.broadcasted_iota(jnp.int32, (tr, tr), 0)
        c = lax.broadcasted_iota(jnp.int32, (tr, tr), 1)
        triu = (r <= c).astype(F32)
        dlf = _dot(triu, dc, precision=HI) + carry_ref[...]
        carry_ref[...] += jnp.sum(dc, axis=0, keepdims=True)
        dz = dlf * jax.nn.sigmoid(-(f_ref[...] + fb_ref[...]))
        df_ref[...] = dz.astype(BF16)
        _acc_store(i, dfb_ref, jnp.sum(dz, axis=0, keepdims=True))

    rev = lambda i: (nb - 1 - i, 0)
    return _pcall(body, name="fox_prep_bwd", grid=(nb,),
                  in_specs=[pl.BlockSpec((tr, LANES), lambda i: (nb - 1 - i, cf)), _const((1, LANES)),
                            pl.BlockSpec((tr, LANES), rev)],
                  out_specs=[pl.BlockSpec((tr, LANES), rev), _const((1, LANES))],
                  out_shape=[jax.ShapeDtypeStruct((T, LANES), BF16), jax.ShapeDtypeStruct((1, LANES), F32)],
                  scratch_shapes=[pltpu.VMEM((1, LANES), F32)], compiler_params=_cparams(("arbitrary",)))(u, fb, dc)


def _attn_logits(q_ref, k_ref, c_ref, i, tq, te):
    s = _dot(q_ref[...].astype(BF16), k_ref[0:te, :].astype(BF16), "nt") * (FOX_HEAD_DIM ** -0.5) - c_ref[0, :, 0:te]
    row = i * tq + lax.broadcasted_iota(jnp.int32, (tq, te), 0)
    col = lax.broadcasted_iota(jnp.int32, (tq, te), 1)
    return jnp.where(col <= row, s, -1e30)


def _per_query_tile(i, nq, tq, fn):
    for ii in range(nq):
        pl.when(i == ii)(functools.partial(fn, (ii + 1) * tq))


def _attn_fwd(cfg, u, c_rows):
    T, FW, FH = cfg.T, cfg.FW, cfg.FH
    tq = min(256, T)
    dh = FOX_HEAD_DIM

    def body(q_ref, k_ref, v_ref, c_ref, o_ref, lse_ref):
        i = pl.program_id(1)

        def tile(te):
            s = _attn_logits(q_ref, k_ref, c_ref, i, tq, te)
            m = jnp.max(s, axis=1, keepdims=True)
            p = jnp.exp(s - m)
            l = jnp.sum(p, axis=1, keepdims=True)
            o_ref[...] = _dot(p.astype(BF16), v_ref[0:te, :].astype(BF16)) / l
            lse_ref[0] = m + jnp.log(l)

        _per_query_tile(i, T // tq, tq, tile)

    return _pcall(
        body, name="fox_attn_fwd", grid=(FH, T // tq),
        in_specs=[pl.BlockSpec((tq, dh), lambda h, i: (i, 3 * h)), pl.BlockSpec((T, dh), lambda h, i: (0, 3 * h + 1)),
                  pl.BlockSpec((T, dh), lambda h, i: (0, 3 * h + 2)), pl.BlockSpec((1, 1, T), lambda h, i: (h, 0, 0))],
        out_specs=[pl.BlockSpec((tq, dh), lambda h, i: (i, h)), pl.BlockSpec((1, tq, 1), lambda h, i: (h, i, 0))],
        out_shape=[jax.ShapeDtypeStruct((T, FW), F32), jax.ShapeDtypeStruct((FH, T, 1), F32)],
        compiler_params=_cparams(("parallel", "arbitrary")),
    )(u, u, u, c_rows)


def _attn_bwd(cfg, u, c_rows, lse, do, du):
    T, FW, FH = cfg.T, cfg.FW, cfg.FH
    tq = min(256, T)
    nq = T // tq
    dh = FOX_HEAD_DIM
    scale = dh ** -0.5

    def body(q_ref, k_ref, v_ref, c_ref, lse_ref, do_ref, du_in, du_ref, dcol_ref, dk_acc, dv_acc):
        i = pl.program_id(1)

        @pl.when(i == 0)
        def _():
            dk_acc[...] = jnp.zeros_like(dk_acc)
            dv_acc[...] = jnp.zeros_like(dv_acc)
            dcol_ref[...] = jnp.zeros_like(dcol_ref)

        def tile(te):
            s = _attn_logits(q_ref, k_ref, c_ref, i, tq, te)
            p = jnp.exp(s - lse_ref[0])
            do_v = do_ref[...]
            dp = _dot(do_v.astype(BF16), v_ref[0:te, :].astype(BF16), "nt")
            delta = jnp.sum(p * dp, axis=1, keepdims=True)
            ds = p * (dp - delta)
            ds16 = ds.astype(BF16)
            du_ref[te - tq:te, 0:dh] = (_dot(ds16, k_ref[0:te, :].astype(BF16)) * scale).astype(BF16)
            dk_acc[0:te, :] += _dot(ds16, q_ref[...].astype(BF16), "tn") * scale
            dv_acc[0:te, :] += _dot(p.astype(BF16), do_v.astype(BF16), "tn")
            dcol_ref[0, :, 0:te] += jnp.sum(ds, axis=0, keepdims=True)

        _per_query_tile(i, nq, tq, tile)

        @pl.when(i == nq - 1)
        def _():
            du_ref[:, dh:2 * dh] = dk_acc[...].astype(BF16)
            du_ref[:, 2 * dh:3 * dh] = dv_acc[...].astype(BF16)

    return _pcall(
        body, name="fox_attn_bwd", grid=(FH, nq),
        in_specs=[pl.BlockSpec((tq, dh), lambda h, i: (i, 3 * h)), pl.BlockSpec((T, dh), lambda h, i: (0, 3 * h + 1)),
                  pl.BlockSpec((T, dh), lambda h, i: (0, 3 * h + 2)), pl.BlockSpec((1, 1, T), lambda h, i: (h, 0, 0)),
                  pl.BlockSpec((1, tq, 1), lambda h, i: (h, i, 0)), pl.BlockSpec((tq, dh), lambda h, i: (i, h)), _ANY],
        out_specs=[pl.BlockSpec((T, 3 * dh), lambda h, i: (0, h)), pl.BlockSpec((1, 1, T), lambda h, i: (h, 0, 0))],
        out_shape=[jax.ShapeDtypeStruct(du.shape, BF16), jax.ShapeDtypeStruct((FH, 1, T), F32)],
        scratch_shapes=[pltpu.VMEM((T, dh), F32), pltpu.VMEM((T, dh), F32)],
        input_output_aliases={6: 0},
        compiler_params=_cparams(("parallel", "arbitrary")),
    )(u, u, u, c_rows, lse, do, du)


def _head_indicators(cfg):
    ind = np.zeros((cfg.RW, LANES), np.float32)
    ind[np.arange(cfg.RW), np.arange(cfg.RW) // RWKV_HEAD_DIM] = 1.0
    pad = np.zeros((1, LANES), np.float32)
    pad[0, cfg.RH:] = 1.0
    return jnp.asarray(ind), jnp.asarray(ind.T.copy()), jnp.asarray(pad)


def _prep_fn(us_r, us_k, us_v, us_wd, us_ad, w0, w2p, a0, a2p, k_k, k_a, ind, ind_t, pad):
    wpre = w0 + _dot3(jnp.tanh(us_wd), w2p)
    w = -_softplus(-wpre) - 0.5
    lw = -jnp.exp(w)
    a = jax.nn.sigmoid(a0 + _dot3(us_ad, a2p))
    kk = us_k * k_k
    ss = _xdot(kk * kk, ind, ind_t) + pad
    inv = 1.0 / jnp.maximum(jnp.sqrt(ss), L2_EPS)
    kkn = kk * _xdot(inv, ind_t, ind)
    kp = us_k * (1.0 + (a - 1.0) * k_a)
    return us_r, lw, kp, us_v, -kkn, kkn * a


def _shifted(u, prev_row, mu, first):
    n = u.shape[0]
    rolled = pltpu.roll(u, 1, 0)
    row = lax.broadcasted_iota(jnp.int32, u.shape, 0)
    p0 = jnp.where(first, jnp.zeros_like(prev_row), prev_row)
    prev = jnp.where(row == 0, jnp.broadcast_to(p0, u.shape), rolled)
    return u + (prev - u) * mu, prev


def _rwkv_specs(cfg, tr):
    RW, LP = cfg.RW, cfg.LP
    base = cfg.o_rwkv // RW
    cols = [(RW, base), (RW, base + 1), (RW, base + 2), (RW, base + 3), (LP, cfg.o_wd // LP), (LP, cfg.o_ad // LP)]
    cur = [pl.BlockSpec((tr, w), (lambda i, cb=cb: (i, cb))) for w, cb in cols]
    prv = [pl.BlockSpec((8, w), (lambda i, cb=cb: (jnp.maximum(i * (tr // 8) - 1, 0), cb))) for w, cb in cols]
    return cols, cur, prv


def _mu_pieces(cfg, mu_ref):
    RW, LP = cfg.RW, cfg.LP
    offs = [0, RW, 2 * RW, 3 * RW, 4 * RW, 4 * RW + LP, 4 * RW + 2 * LP]
    return [mu_ref[:, offs[j]:offs[j + 1]] for j in range(6)]


def _rwkv_prep_fwd(cfg, u, mu, w0, w2p, a0, a2p, k_k, k_a):
    T, RW, LP, tr = cfg.T, cfg.RW, cfg.LP, cfg.tr
    ind, ind_t, pad = _head_indicators(cfg)
    cols, cur, prv = _rwkv_specs(cfg, tr)

    def body(*refs):
        u_refs, p_refs = refs[0:6], refs[6:12]
        mu_ref, w0_ref, w2_ref, a0_ref, a2_ref, kk_ref, ka_ref, ind_ref, indt_ref, pad_ref = refs[12:22]
        outs = refs[22:]
        first = pl.program_id(0) == 0
        mus = _mu_pieces(cfg, mu_ref)
        us = [_shifted(u_refs[j][...], p_refs[j][7:8, :], mus[j], first)[0] for j in range(6)]
        res = _prep_fn(us[0], us[1], us[2], us[4], us[5], w0_ref[...], w2_ref[...], a0_ref[...], a2_ref[...],
                       kk_ref[...], ka_ref[...], ind_ref[...], indt_ref[...], pad_ref[...])
        for j in range(6):
            outs[j][...] = res[j]
        outs[6][...] = us[3]

    consts = [mu, w0, w2p, a0, a2p, k_k, k_a, ind, ind_t, pad]
    return _pcall(body, name="rwkv_prep_fwd", grid=(T // tr,),
                  in_specs=cur + prv + [_const(c.shape) for c in consts],
                  out_specs=[_tile(tr, RW)] * 7, out_shape=[jax.ShapeDtypeStruct((T, RW), F32)] * 7,
                  compiler_params=_cparams(("parallel",)))(*([u] * 12), *consts)


def _rwkv_prep_bwd(cfg, u, mu, w0, w2p, a0, a2p, k_k, k_a, cots, dzb):
    T, RW, LP = cfg.T, cfg.RW, cfg.LP
    tr = min(128, T)
    ind, ind_t, pad = _head_indicators(cfg)
    cols, cur, prv = _rwkv_specs(cfg, tr)
    rseg = cfg.rseg

    def body(*refs):
        u_refs, p_refs = refs[0:6], refs[6:12]
        mu_ref, w0_ref, w2_ref, a0_ref, a2_ref, kk_ref, ka_ref, ind_ref, indt_ref, pad_ref = refs[12:22]
        cot_refs, dzb_ref = refs[22:28], refs[28]
        dus_ref, dmu_ref, dw0_ref, dw2_ref, da0_ref, da2_ref, dkk_ref, dka_ref = refs[29:]
        i = pl.program_id(0)
        first = i == 0
        mus = _mu_pieces(cfg, mu_ref)
        sh = [_shifted(u_refs[j][...], p_refs[j][7:8, :], mus[j], first) for j in range(6)]
        us = [s[0] for s in sh]
        fn = functools.partial(_prep_fn, ind=ind_ref[...], ind_t=indt_ref[...], pad=pad_ref[...])
        _, vjp = jax.vjp(fn, us[0], us[1], us[2], us[4], us[5], w0_ref[...], w2_ref[...], a0_ref[...], a2_ref[...],
                         kk_ref[...], ka_ref[...])
        d = vjp(tuple(c[...] for c in cot_refs))
        dus = [d[0], d[1], d[2], dzb_ref[...], d[3], d[4]]
        offs = [0, RW, 2 * RW, 3 * RW, 4 * RW, 4 * RW + LP, 4 * RW + 2 * LP]
        for j in range(6):
            dus_ref[:, offs[j]:offs[j + 1]] = dus[j]
            dmu_j = jnp.sum(dus[j] * (sh[j][1] - u_refs[j][...]), axis=0, keepdims=True)

            @pl.when(first)
            def _(j=j, dmu_j=dmu_j):
                dmu_ref[:, offs[j]:offs[j + 1]] = dmu_j

            @pl.when(i > 0)
            def _(j=j, dmu_j=dmu_j):
                dmu_ref[:, offs[j]:offs[j + 1]] += dmu_j
        for ref, val in zip((dw0_ref, dw2_ref, da0_ref, da2_ref, dkk_ref, dka_ref), d[5:11]):
            _acc_store(i, ref, val)

    consts = [mu, w0, w2p, a0, a2p, k_k, k_a, ind, ind_t, pad]
    vec = jax.ShapeDtypeStruct((1, RW), F32)
    mat = jax.ShapeDtypeStruct((LP, RW), F32)
    return _pcall(body, name="rwkv_prep_bwd", grid=(T // tr,),
                  in_specs=cur + prv + [_const(c.shape) for c in consts] + [_tile(tr, RW)] * 7,
                  out_specs=[_tile(tr, rseg), _const((1, rseg)), _const((1, RW)), _const((LP, RW)), _const((1, RW)),
                             _const((LP, RW)), _const((1, RW)), _const((1, RW))],
                  out_shape=[jax.ShapeDtypeStruct((T, rseg), F32), jax.ShapeDtypeStruct((1, rseg), F32),
                             vec, mat, vec, mat, vec, vec],
                  compiler_params=_cparams(("arbitrary",)))(*([u] * 12), *consts, *cots, dzb)


def _shift_bwd(cfg, dus, mu, df, du):
    T, tr, RW, LP = cfg.T, cfg.tr, cfg.RW, cfg.LP
    nb = T // tr
    tail = cfg.ncol - cfg.o_f
    assert cfg.o_rwkv % (4 * RW) == 0 and (4 * RW) % (2 * LP) == 0 and cfg.o_f % tail == 0

    def shifted(d_ref, n_ref, mu_ref):
        d = d_ref[...]
        rolled = pltpu.roll(d, tr - 1, 0)
        row = lax.broadcasted_iota(jnp.int32, d.shape, 0)
        n0 = jnp.where(pl.program_id(0) == nb - 1, jnp.zeros_like(n_ref[0:1, :]), n_ref[0:1, :])
        nxt = jnp.where(row == tr - 1, jnp.broadcast_to(n0, d.shape), rolled)
        mu_v = mu_ref[...]
        return (d * (1.0 - mu_v) + nxt * mu_v).astype(BF16)

    def main_body(d_ref, n_ref, mu_ref, du_in, du_ref):
        du_ref[...] = shifted(d_ref, n_ref, mu_ref)

    def tail_body(d_ref, n_ref, mu_ref, df_ref, du_in, du_ref):
        du_ref[:, 0:LANES] = df_ref[...]
        du_ref[:, LANES:LANES + 2 * LP] = shifted(d_ref, n_ref, mu_ref)
        if tail > LANES + 2 * LP:
            du_ref[:, LANES + 2 * LP:] = jnp.zeros((tr, tail - LANES - 2 * LP), BF16)

    def specs(w, cb):
        return [_tile(tr, w, cb),
                pl.BlockSpec((8, w), lambda i: (jnp.minimum((i + 1) * (tr // 8), T // 8 - 1), cb)),
                pl.BlockSpec((1, w), lambda i: (0, cb))]

    out = jax.ShapeDtypeStruct(du.shape, BF16)
    du = _pcall(main_body, name="shift_bwd_main", grid=(nb,), in_specs=specs(4 * RW, 0) + [_ANY],
                out_specs=_tile(tr, 4 * RW, cfg.o_rwkv // (4 * RW)), out_shape=out, input_output_aliases={3: 0},
                compiler_params=_cparams(("parallel",)))(dus, dus, mu, du)
    return _pcall(tail_body, name="shift_bwd_tail", grid=(nb,),
                  in_specs=specs(2 * LP, 4 * RW // (2 * LP)) + [_tile(tr, LANES), _ANY],
                  out_specs=_tile(tr, tail, cfg.o_f // tail), out_shape=out, input_output_aliases={4: 0},
                  compiler_params=_cparams(("parallel",)))(dus, dus, mu, df, du)


def _chunk_local(r, lw, k, v, a, b):
    H, C, K = r.shape
    row = lax.broadcasted_iota(jnp.int32, (C, C), 0)
    col = lax.broadcasted_iota(jnp.int32, (C, C), 1)
    incl = jnp.broadcast_to((row >= col).astype(F32)[None], (H, C, C))
    strict = (row > col)[None]
    lower = (row >= col)[None]
    eye = (row == col)[None]
    zero = jnp.zeros((), F32)
    L = _bdot(incl, lw, 2, 1)
    LC = jnp.sum(lw, axis=1, keepdims=True)
    eL = jnp.exp(L)
    eLn = jnp.exp(-L)
    at = a * jnp.exp(L - lw)
    rt = r * eL
    bt = b * eLn
    kt = k * eLn
    eR = jnp.exp(LC - L)
    bh = b * eR
    kh = k * eR
    gram = functools.partial(_bdot, passes=SCAN_PASSES[0])
    inv = functools.partial(_bdot, passes=SCAN_PASSES[1])
    app = functools.partial(_bdot, passes=SCAN_PASSES[2])
    n_ab = jnp.where(strict, gram(at, bt, 2, 2), zero)
    n_ak = jnp.where(strict, gram(at, kt, 2, 2), zero)
    m_rb = jnp.where(lower, gram(rt, bt, 2, 2), zero)
    m_rk = jnp.where(lower, gram(rt, kt, 2, 2), zero)
    M = n_ab
    P = jnp.where(eye, 1.0, zero) + n_ab
    for _ in range(1, max(1, int(np.ceil(np.log2(C))))):
        M = inv(M, M, 2, 1)
        P = P + inv(M, P, 2, 1)
    W = app(P, at, 2, 1)
    Uloc = app(P, app(n_ak, v, 2, 1), 2, 1)
    Q = rt + app(m_rb, W, 2, 1)
    Yloc = app(m_rb, Uloc, 2, 1) + app(m_rk, v, 2, 1)
    A = jnp.where(eye, jnp.exp(LC), zero) + app(W, bh, 1, 1)
    Sloc = app(Uloc, bh, 1, 1) + app(v, kh, 1, 1)
    return Q, Yloc, A, Sloc


def _split_heads(ref, n):
    N = RWKV_HEAD_DIM
    return jnp.stack([ref[:, h * N:(h + 1) * N] for h in range(n)], axis=0)


def _merge_heads(x):
    return jnp.concatenate([x[h] for h in range(x.shape[0])], axis=1)


def _scan_local_specs(cfg):
    N, HB = RWKV_HEAD_DIM, cfg.hb
    grid = (cfg.RH // HB, cfg.T // cfg.C)
    seq = pl.BlockSpec((HB, cfg.C, N), lambda h, j: (h, j, 0))
    mat = pl.BlockSpec((HB, 1, N, N), lambda h, j: (h, j, 0, 0))
    return grid, seq, mat


def _scan_local_fwd(cfg, seqs):
    T, RH, N = cfg.T, cfg.RH, RWKV_HEAD_DIM
    grid, seq, mat = _scan_local_specs(cfg)

    def body(r_ref, lw_ref, k_ref, v_ref, a_ref, b_ref, q_ref, yl_ref, a_out, sl_ref):
        Q, Yloc, A, Sloc = _chunk_local(*[_split_heads(ref, cfg.hb) for ref in (r_ref, lw_ref, k_ref, v_ref, a_ref, b_ref)])
        q_ref[...] = Q
        yl_ref[...] = Yloc
        a_out[:, 0] = A
        sl_ref[:, 0] = Sloc

    tok = pl.BlockSpec((cfg.C, cfg.hb * N), lambda h, j: (j, h))
    sq = jax.ShapeDtypeStruct((RH, T, N), F32)
    mt = jax.ShapeDtypeStruct((RH, T // cfg.C, N, N), F32)
    return _pcall(body, name="rwkv_scan_local_fwd", grid=grid, in_specs=[tok] * 6, out_specs=[seq, seq, mat, mat],
                  out_shape=[sq, sq, mt, mt], compiler_params=_cparams(("parallel", "parallel")))(*seqs)


def _scan_local_bwd(cfg, toks, dq, dy, da, dsl, extra):
    T, RW, N = cfg.T, cfg.RW, RWKV_HEAD_DIM
    grid, seq, mat = _scan_local_specs(cfg)

    def body(r_ref, lw_ref, k_ref, v_ref, a_ref, b_ref, dq_ref, dy_ref, da_ref, dsl_ref, xr_ref, xk_ref, xv_ref,
             *outs):
        ins = [_split_heads(ref, cfg.hb) for ref in (r_ref, lw_ref, k_ref, v_ref, a_ref, b_ref)]
        _, vjp = jax.vjp(_chunk_local, *ins)
        d = vjp((dq_ref[...], _split_heads(dy_ref, cfg.hb), da_ref[:, 0], dsl_ref[:, 0]))
        add = {0: xr_ref, 2: xk_ref, 3: xv_ref}
        for j in range(6):
            dj = _merge_heads(d[j])
            outs[j][...] = dj + add[j][...] if j in add else dj

    tok = pl.BlockSpec((cfg.C, cfg.hb * N), lambda h, j: (j, h))
    return _pcall(body, name="rwkv_scan_local_bwd", grid=grid, in_specs=[tok] * 6 + [seq, tok, mat, mat] + [tok] * 3,
                  out_specs=[tok] * 6, out_shape=[jax.ShapeDtypeStruct((T, RW), F32)] * 6,
                  compiler_params=_cparams(("parallel", "parallel")))(*toks, dq, dy, da, dsl, *extra)


def _scan_carry_specs(cfg, rev):
    N, RH, C, nc = RWKV_HEAD_DIM, cfg.RH, cfg.C, cfg.T // cfg.C
    at = (lambda j: nc - 1 - j) if rev else (lambda j: j)
    seq = pl.BlockSpec((RH, C, N), lambda j: (0, at(j), 0))
    mat = pl.BlockSpec((RH, 1, N, N), lambda j: (0, at(j), 0, 0))
    return nc, seq, mat


def _scan_carry_fwd(cfg, q, yloc, a, sloc):
    T, RH, N = cfg.T, cfg.RH, RWKV_HEAD_DIM
    nc, seq, mat = _scan_carry_specs(cfg, False)

    def body(q_ref, yl_ref, a_ref, sl_ref, y_ref, ck_ref, s_ref):
        @pl.when(pl.program_id(0) == 0)
        def _():
            s_ref[...] = jnp.zeros_like(s_ref)

        S = s_ref[...]
        ck_ref[:, 0] = S
        y_ref[...] = _merge_heads(_bdot(q_ref[...], S, 2, 2) + yl_ref[...])
        s_ref[...] = _bdot(S, a_ref[:, 0], 2, 1) + sl_ref[:, 0]

    tok = pl.BlockSpec((cfg.C, cfg.RW), lambda j: (j, 0))
    return _pcall(body, name="rwkv_scan_carry_fwd", grid=(nc,), in_specs=[seq, seq, mat, mat], out_specs=[tok, mat],
                  out_shape=[jax.ShapeDtypeStruct((T, cfg.RW), F32), jax.ShapeDtypeStruct((RH, nc, N, N), F32)],
                  scratch_shapes=[pltpu.VMEM((RH, N, N), F32)],
                  compiler_params=_cparams(("arbitrary",)))(q, yloc, a, sloc)


def _scan_carry_bwd(cfg, q, a, ckpt, dy):
    T, RH, N = cfg.T, cfg.RH, RWKV_HEAD_DIM
    nc, seq, mat = _scan_carry_specs(cfg, True)

    def body(q_ref, a_ref, ck_ref, dy_ref, dq_ref, da_ref, dsl_ref, ds_ref):
        @pl.when(pl.program_id(0) == 0)
        def _():
            ds_ref[...] = jnp.zeros_like(ds_ref)

        S, dS, dY = ck_ref[:, 0], ds_ref[...], _split_heads(dy_ref, RH)
        dq_ref[...] = _bdot(dY, S, 2, 1)
        da_ref[:, 0] = _bdot(S, dS, 1, 1)
        dsl_ref[:, 0] = dS
        ds_ref[...] = _bdot(dS, a_ref[:, 0], 2, 2) + _bdot(dY, q_ref[...], 1, 1)

    mt = jax.ShapeDtypeStruct((RH, nc, N, N), F32)
    tok = pl.BlockSpec((cfg.C, cfg.RW), lambda j: (nc - 1 - j, 0))
    return _pcall(body, name="rwkv_scan_carry_bwd", grid=(nc,), in_specs=[seq, mat, mat, tok],
                  out_specs=[seq, mat, mat], out_shape=[jax.ShapeDtypeStruct((RH, T, N), F32), mt, mt],
                  scratch_shapes=[pltpu.VMEM((RH, N, N), F32)],
                  compiler_params=_cparams(("arbitrary",)))(q, a, ckpt, dy)


def _post_fn(y, r, kp, v, zb, ln_w, ln_b, rk, ind, ind_t):
    n = float(RWKV_HEAD_DIM)
    mu = _xdot(_xdot(y, ind, ind_t) / n, ind_t, ind)
    yc = y - mu
    var = _xdot(yc * yc, ind, ind_t) / n
    rstd = _xdot(lax.rsqrt(var + GN_EPS), ind_t, ind)
    yn = yc * rstd * ln_w + ln_b
    bonus = _xdot(_xdot(r * kp * rk, ind, ind_t), ind_t, ind) * v
    return (yn + bonus) * _silu(zb)


def _rwkv_post_fwd(cfg, y, r, kp, v, zb, ln_w, ln_b, rk):
    T, RW, tr = cfg.T, cfg.RW, cfg.tr
    ind, ind_t, _ = _head_indicators(cfg)

    def body(y_ref, r_ref, k_ref, v_ref, z_ref, lw_ref, lb_ref, rk_ref, ind_ref, indt_ref, ob_ref):
        ob_ref[...] = _post_fn(y_ref[...], r_ref[...], k_ref[...], v_ref[...], z_ref[...], lw_ref[...], lb_ref[...],
                               rk_ref[...], ind_ref[...], indt_ref[...]).astype(BF16)

    consts = [ln_w, ln_b, rk, ind, ind_t]
    return _pcall(body, name="rwkv_post_fwd", grid=(T // tr,),
                  in_specs=[_tile(tr, RW)] * 5 + [_const(c.shape) for c in consts],
                  out_specs=_tile(tr, RW), out_shape=jax.ShapeDtypeStruct((T, RW), BF16),
                  compiler_params=_cparams(("parallel",)))(y, r, kp, v, zb, *consts)


def _rwkv_post_bwd(cfg, y, r, kp, v, zb, ln_w, ln_b, rk, dob):
    T, RW = cfg.T, cfg.RW
    tr = min(128, T)
    ind, ind_t, _ = _head_indicators(cfg)

    def body(y_ref, r_ref, k_ref, v_ref, z_ref, lw_ref, lb_ref, rk_ref, ind_ref, indt_ref, dob_ref,
             dy_ref, dr_ref, dk_ref, dv_ref, dz_ref, dlw_ref, dlb_ref, drk_ref):
        fn = functools.partial(_post_fn, ind=ind_ref[...], ind_t=indt_ref[...])
        _, vjp = jax.vjp(fn, y_ref[...], r_ref[...], k_ref[...], v_ref[...], z_ref[...], lw_ref[...], lb_ref[...],
                         rk_ref[...])
        d = vjp(dob_ref[...])
        for ref, val in zip((dy_ref, dr_ref, dk_ref, dv_ref, dz_ref), d[:5]):
            ref[...] = val
        i = pl.program_id(0)
        for ref, val in zip((dlw_ref, dlb_ref, drk_ref), d[5:8]):
            _acc_store(i, ref, val)

    consts = [ln_w, ln_b, rk, ind, ind_t]
    vec = jax.ShapeDtypeStruct((1, RW), F32)
    return _pcall(body, name="rwkv_post_bwd", grid=(T // tr,),
                  in_specs=[_tile(tr, RW)] * 5 + [_const(c.shape) for c in consts] + [_tile(tr, RW)],
                  out_specs=[_tile(tr, RW)] * 5 + [_const((1, RW))] * 3,
                  out_shape=[jax.ShapeDtypeStruct((T, RW), F32)] * 5 + [vec] * 3,
                  compiler_params=_cparams(("arbitrary",)))(y, r, kp, v, zb, *consts, dob)


def _adamw_math(w, g, m, v):
    m = ADAM_B1 * m + (1.0 - ADAM_B1) * g
    v = ADAM_B2 * v + (1.0 - ADAM_B2) * (g * g)
    m_hat = m / (1.0 - ADAM_B1 ** ADAM_STEP)
    v_hat = v / (1.0 - ADAM_B2 ** ADAM_STEP)
    delta = -ADAM_LR * (m_hat / (jnp.sqrt(v_hat) + ADAM_EPS) + ADAM_WD * w)
    return delta, m, v


def _adamw(name, w, g, m, v, copy_grad=False):
    R, Cc = w.shape
    Rp = -(-R // 8) * 8
    tr = Rp
    for nb in range(1, Rp // 8 + 1):
        if (Rp // 8) % nb == 0 and (Rp // nb) * Cc * 4 <= 2 * 1024 * 1024:
            tr = Rp // nb
            break

    def body(w_ref, g_ref, m_ref, v_ref, d_ref, nm_ref, nv_ref, *g_out):
        g_v = g_ref[...]
        d, nm, nv = _adamw_math(w_ref[...], g_v, m_ref[...], v_ref[...])
        d_ref[...] = d
        nm_ref[...] = nm
        nv_ref[...] = nv
        if copy_grad:
            g_out[0][...] = g_v

    spec = _tile(tr, Cc)
    n_out = 4 if copy_grad else 3
    return _pcall(body, name=name, grid=(Rp // tr,), in_specs=[spec] * 4, out_specs=[spec] * n_out,
                  out_shape=[jax.ShapeDtypeStruct((R, Cc), F32)] * n_out,
                  compiler_params=_cparams(("parallel",)))(w, g, m, v)


def _row_tile(R, Cc, itemsize, budget=2 * 1024 * 1024):
    for nb in range(1, R // 16 + 1):
        if R % nb == 0 and (R // nb) % 16 == 0 and (R // nb) * Cc * itemsize <= budget:
            return R // nb
    return R


def _add_halves(name, gs, r1, c_idx):
    _, R, Cc = gs.shape
    half = R // 2
    tr = _row_tile(half, Cc, 4)
    nb = half // tr

    def body(c_ref, g_ref, r_ref, o_ref):
        o_ref[...] = (g_ref[...].astype(F32) + r_ref[...].astype(F32)).astype(BF16)

    grid_spec = pltpu.PrefetchScalarGridSpec(
        num_scalar_prefetch=1, grid=(N_CHIPS, nb),
        in_specs=[pl.BlockSpec((1, tr, Cc), lambda s, i, c: (s, c[0] * nb + i, 0)),
                  pl.BlockSpec((1, tr, Cc), lambda s, i, c: (s, i, 0))],
        out_specs=pl.BlockSpec((1, tr, Cc), lambda s, i, c: (s, i, 0)))
    return _pcall(body, name=name, grid_spec=grid_spec, out_shape=jax.ShapeDtypeStruct((N_CHIPS, half, Cc), BF16),
                  compiler_params=_cparams(("parallel", "parallel")))(c_idx, gs, r1)


def _sum_slots(name, r2):
    S, R, Cc = r2.shape
    tr = _row_tile(R, Cc, 4 * S // 2 if r2.dtype == BF16 else 4 * S)

    def body(r_ref, o_ref):
        acc = r_ref[0].astype(F32)
        for s in range(1, S):
            acc = acc + r_ref[s].astype(F32)
        o_ref[...] = acc

    return _pcall(body, name=name, grid=(R // tr,), in_specs=[pl.BlockSpec((S, tr, Cc), lambda i: (0, i, 0))],
                  out_specs=_tile(tr, Cc), out_shape=jax.ShapeDtypeStruct((R, Cc), F32),
                  compiler_params=_cparams(("parallel",)))(r2)


def _sum_chips(name, recv, own, place):
    S, H, Cc = recv.shape
    tr = _row_tile(H, Cc, 4, 1024 * 1024)
    nb = H // tr

    def body(p_ref, r_ref, own_ref, o_ref):
        s = pl.program_id(1)
        me = p_ref[0]

        @pl.when(s == 0)
        def _():
            o_ref[...] = jnp.zeros_like(o_ref)

        @pl.when(s == me)
        def _():
            o_ref[...] += own_ref[0].astype(F32)

        @pl.when(s != me)
        def _():
            o_ref[...] += r_ref[0].astype(F32)

    grid_spec = pltpu.PrefetchScalarGridSpec(
        num_scalar_prefetch=1, grid=(nb, S),
        in_specs=[pl.BlockSpec((1, tr, Cc), lambda i, s, p: (jnp.where(s == p[0], (s + 1) % S, s), i, 0)),
                  pl.BlockSpec((1, tr, Cc), lambda i, s, p: (p[0], i, 0))],
        out_specs=pl.BlockSpec((tr, Cc), lambda i, s, p: (p[1] * nb + i, 0)))
    return _pcall(body, name=name, grid_spec=grid_spec, out_shape=jax.ShapeDtypeStruct((2 * H, Cc), F32),
                  compiler_params=_cparams(("parallel", "arbitrary")))(place, recv, own)


def _cast_bf16(name, w):
    R, Cc = w.shape
    tr = _row_tile(R, Cc, 4)

    def body(w_ref, o_ref):
        o_ref[...] = w_ref[...].astype(BF16)

    return _pcall(body, name=name, grid=(R // tr,), in_specs=[_tile(tr, Cc)], out_specs=_tile(tr, Cc),
                  out_shape=jax.ShapeDtypeStruct((R, Cc), BF16), compiler_params=_cparams(("parallel",)))(w)


_ANY = pl.BlockSpec(memory_space=pl.ANY)


def _place():
    x, y, c = lax.axis_index("x"), lax.axis_index("y"), lax.axis_index("c")
    others = [(1 - x, y), (x, 1 - y), (1 - x, 1 - y)]
    return x, y, c, others


def _gather_weights(shards):
    n = len(shards)
    halves = [s.shape[0] // 2 for s in shards]

    def body(*refs):
        ins, outs = refs[:n], refs[n:2 * n]
        send_sems, recv_sems = refs[2 * n:]
        x, y, c, _ = _place()
        me = 2 * x + y
        n1 = (x ^ (1 - c), y ^ c)
        n2 = (x ^ c, y ^ (1 - c))
        s1, s2, sd = 2 * n1[0] + n1[1], 2 * n2[0] + n2[1], 2 * (1 - x) + (1 - y)

        def rows(k, chip, hc):
            return outs[k].at[chip, pl.ds(hc * halves[k], halves[k]), :]

        def remote(k, j, src, dst, to):
            return pltpu.make_async_remote_copy(src_ref=src, dst_ref=dst, send_sem=send_sems.at[6 * k + j],
                                                recv_sem=recv_sems.at[6 * k + j], device_id=to, device_id_type=MESH)

        def arrived(k, j, land):
            remote(k, j, land, land, (x, y, c)).wait_recv()

        sent = []

        def send(k, j, src, dst, to):
            cp = remote(k, j, src, dst, to)
            cp.start()
            sent.append(cp)

        sib = (x, y, 1 - c)
        for k in range(n):
            mine = ins[k].at[pl.ds(c * halves[k], halves[k]), :]
            send(k, 0, mine, rows(k, me, c), (*n1, c))
            send(k, 1, mine, rows(k, me, c), (*n2, c))
        for k in range(n):
            from_n1, from_n2, from_d = rows(k, s1, c), rows(k, s2, c), rows(k, sd, c)
            arrived(k, 0, from_n1)
            send(k, 2, from_n1, from_n1, (*n2, c))
            send(k, 3, from_n1, from_n1, sib)
            arrived(k, 1, from_n2)
            send(k, 4, from_n2, from_n2, sib)
            arrived(k, 2, from_d)
            send(k, 5, from_d, from_d, sib)
        for k in range(n):
            arrived(k, 3, rows(k, s2, 1 - c))
            arrived(k, 4, rows(k, s1, 1 - c))
            arrived(k, 5, rows(k, sd, 1 - c))
        for cp in sent:
            cp.wait_send()

    return _pcall(
        body, name="gather_weights", in_specs=[_ANY] * n, out_specs=[_ANY] * n,
        out_shape=[jax.ShapeDtypeStruct((N_CHIPS,) + s.shape, s.dtype) for s in shards],
        scratch_shapes=[pltpu.SemaphoreType.DMA((6 * n,)), pltpu.SemaphoreType.DMA((6 * n,))],
    )(*shards)


def _exchange_halves(grads):
    n = len(grads)
    halves = [g.shape[1] // 2 for g in grads]

    def body(*refs):
        ins, outs = refs[:n], refs[n:2 * n]
        send_sems, recv_sems = refs[2 * n:]
        x, y, c, _ = _place()
        cps = []
        for k in range(n):
            src = ins[k].at[:, pl.ds((1 - c) * halves[k], halves[k]), :]
            cp = pltpu.make_async_remote_copy(src_ref=src, dst_ref=outs[k], send_sem=send_sems.at[k],
                                              recv_sem=recv_sems.at[k], device_id=(x, y, 1 - c), device_id_type=MESH)
            cp.start()
            cps.append(cp)
        for cp in cps:
            cp.wait()

    return _pcall(
        body, name="exchange_halves", in_specs=[_ANY] * n, out_specs=[_ANY] * n,
        out_shape=[jax.ShapeDtypeStruct((N_CHIPS, h) + g.shape[2:], g.dtype) for g, h in zip(grads, halves)],
        scratch_shapes=[pltpu.SemaphoreType.DMA((n,)), pltpu.SemaphoreType.DMA((n,))],
    )(*grads)


def _scatter_to_owners(chip_sums, small):
    n = len(chip_sums)

    def body(*refs):
        ins, small_in = refs[:n], refs[n]
        outs, small_out = refs[n + 1:2 * n + 1], refs[2 * n + 1]
        send_sems, recv_sems, local_sem, ssend, srecv = refs[2 * n + 2:]
        x, y, c, others = _place()
        me = 2 * x + y
        dev = 2 * me + c
        local = pltpu.make_async_copy(small_in, small_out.at[dev], local_sem)
        local.start()
        sends = []
        for k in range(n):
            for j, (px, py) in enumerate(others):
                cp = pltpu.make_async_remote_copy(
                    src_ref=ins[k].at[2 * px + py], dst_ref=outs[k].at[me], send_sem=send_sems.at[3 * k + j],
                    recv_sem=recv_sems.at[3 * k + j], device_id=(px, py, c), device_id_type=MESH)
                cp.start()
                sends.append(cp)
        rel = [(dx, dy, dc) for dx in (0, 1) for dy in (0, 1) for dc in (0, 1)][1:]
        for r, (dx, dy, dc) in enumerate(rel):
            to = (x ^ dx, y ^ dy, c ^ dc)
            cp = pltpu.make_async_remote_copy(src_ref=small_in, dst_ref=small_out.at[dev], send_sem=ssend.at[r],
                                              recv_sem=srecv.at[r], device_id=to, device_id_type=MESH)
            cp.start()
            sends.append(cp)
        for k in range(n):
            for j, (px, py) in enumerate(others):
                land = outs[k].at[2 * px + py]
                pltpu.make_async_remote_copy(src_ref=land, dst_ref=land, send_sem=send_sems.at[3 * k + j],
                                             recv_sem=recv_sems.at[3 * k + j], device_id=(x, y, c),
                                             device_id_type=MESH).wait_recv()
        for r, (dx, dy, dc) in enumerate(rel):
            land = small_out.at[4 * (x ^ dx) + 2 * (y ^ dy) + (c ^ dc)]
            pltpu.make_async_remote_copy(src_ref=land, dst_ref=land, send_sem=ssend.at[r], recv_sem=srecv.at[r],
                                         device_id=(x, y, c), device_id_type=MESH).wait_recv()
        for cp in sends:
            cp.wait_send()
        local.wait()

    return _pcall(
        body, name="scatter_to_owners", in_specs=[_ANY] * (n + 1), out_specs=[_ANY] * (n + 1),
        out_shape=[jax.ShapeDtypeStruct(g.shape, g.dtype) for g in chip_sums]
        + [jax.ShapeDtypeStruct((N_DEV,) + small.shape, small.dtype)],
        scratch_shapes=[pltpu.SemaphoreType.DMA((3 * n,)), pltpu.SemaphoreType.DMA((3 * n,)),
                        pltpu.SemaphoreType.DMA, pltpu.SemaphoreType.DMA((7,)), pltpu.SemaphoreType.DMA((7,))],
    )(*chip_sums, small)


def _join_halves(fulls):
    n = len(fulls)
    hs = [f.shape[0] // 2 for f in fulls]

    def body(*refs):
        ins, outs = refs[:n], refs[n:2 * n]
        send_sems, recv_sems = refs[2 * n:]
        x, y, c, _ = _place()
        cps = []
        for k in range(n):
            mine = pl.ds(c * hs[k], hs[k])
            cp = pltpu.make_async_remote_copy(src_ref=ins[k].at[mine, :], dst_ref=outs[k].at[mine, :],
                                              send_sem=send_sems.at[k], recv_sem=recv_sems.at[k],
                                              device_id=(x, y, 1 - c), device_id_type=MESH)
            cp.start()
            cps.append(cp)
        for k in range(n):
            land = outs[k].at[pl.ds((1 - c) * hs[k], hs[k]), :]
            pltpu.make_async_remote_copy(src_ref=land, dst_ref=land, send_sem=send_sems.at[k],
                                         recv_sem=recv_sems.at[k], device_id=(x, y, c), device_id_type=MESH).wait_recv()
        for cp in cps:
            cp.wait_send()

    return _pcall(
        body, name="join_halves", in_specs=[_ANY] * n, out_specs=[_ANY] * n,
        out_shape=[jax.ShapeDtypeStruct(f.shape, f.dtype) for f in fulls],
        input_output_aliases={k: k for k in range(n)},
        scratch_shapes=[pltpu.SemaphoreType.DMA((n,)), pltpu.SemaphoreType.DMA((n,))],
    )(*fulls)


def _local_step(cfg, x2, target, norm_gain, w_my, fb, mu_g, w0, w2, a0, a2, k_k, k_a, r_k, ln_w, ln_b, wpf, wpr, wout,
                fng):
    T, D, FW, FH, RW, RH, LP, lora = cfg.T, cfg.D, cfg.FW, cfg.FH, cfg.RW, cfg.RH, cfg.LP, cfg.lora
    fb_p = jnp.pad(fb, ((0, 0), (0, LANES - FH)))
    mu = _rwkv_vec_to_my(cfg, mu_g)
    w2p = jnp.pad(w2, ((0, LP - lora), (0, 0)))
    a2p = jnp.pad(a2, ((0, LP - lora), (0, 0)))
    rk = r_k.reshape(1, RW)
    tm = min(1024, T)

    h = _rms_fwd(cfg, x2, norm_gain)
    u = _mm("in_proj", h, w_my, "nn", F32, tm, cfg.tn, 2048)
    c_cols = _fox_prep(cfg, u, fb_p)
    c_rows = c_cols[:, :FH].T.reshape(FH, 1, T)
    o, lse = _attn_fwd(cfg, u, c_rows)
    oa = _gate_a_fwd(cfg, o, u)
    prep = _rwkv_prep_fwd(cfg, u, mu, w0, w2p, a0, a2p, k_k, k_a)
    r, lw, kp, v, an, b, zb = prep
    toks = [r, lw, kp, v, an, b]
    q_s, yloc, a_m, sloc = _scan_local_fwd(cfg, toks)
    y, ckpt = _scan_carry_fwd(cfg, q_s, yloc, a_m, sloc)
    ob = _rwkv_post_fwd(cfg, y, r, kp, v, zb, ln_w, ln_b, rk)
    pa = _mm("proj_fox", oa, wpf, "nn", F32, tm, 1024, 512)
    pb = _mm("proj_rwkv", ob, wpr, "nn", F32, tm, 1024, 512)
    m = _merge_fwd(cfg, pa, pb, u)
    mo = _mm("out_proj", m, wout, "nn", F32, tm, 1024, 512)
    loss8, dres, dres16, d_fng = _final(cfg, x2, mo, fng.reshape(1, D), target)

    dm = _mm("out_proj_dx", dres16, wout, "nt", F32, tm, 1024, 512)
    d_wout = _mm("out_proj_dw", m, dres16, "tn", BF16, 1024, 1024, 512)
    dpa, dpb, du = _merge_bwd(cfg, pa, pb, u, dm)
    doa = _mm("proj_fox_dx", dpa, wpf, "nt", F32, tm, 1024, 512)
    d_wpf = _mm("proj_fox_dw", oa, dpa, "tn", BF16, 1024, 1024, 512)
    dob = _mm("proj_rwkv_dx", dpb, wpr, "nt", F32, tm, 1024, 512)
    d_wpr = _mm("proj_rwkv_dw", ob, dpb, "tn", BF16, 1024, 1024, 512)

    do, du = _gate_a_bwd(cfg, o, u, doa, du)
    du, dcol = _attn_bwd(cfg, u, c_rows, lse, do, du)
    dc = jnp.pad(-dcol.reshape(FH, T).T, ((0, 0), (0, LANES - FH)))
    df, d_fb = _fox_prep_bwd(cfg, u, fb_p, dc)

    dy, dr_p, dk_p, dv_p, dzb, d_lnw, d_lnb, d_rk = _rwkv_post_bwd(cfg, y, r, kp, v, zb, ln_w, ln_b, rk, dob)
    dq_s, da_m, dsl = _scan_carry_bwd(cfg, q_s, a_m, ckpt, dy)
    cots = _scan_local_bwd(cfg, toks, dq_s, dy, da_m, dsl, [dr_p, dk_p, dv_p])
    dus, d_mu, d_w0, d_w2p, d_a0, d_a2p, d_kk, d_ka = _rwkv_prep_bwd(cfg, u, mu, w0, w2p, a0, a2p, k_k, k_a, cots, dzb)
    du = _shift_bwd(cfg, dus, mu, df, du)
    dh = _mm("in_proj_dx", du, w_my, "nt", F32, tm, 1024, cfg.tn)
    d_wmy = _mm("in_proj_dw", h, du, "tn", BF16, 1024, cfg.tn, 2048)
    gx, d_ng = _rms_bwd(cfg, x2, norm_gain, dh, dres)

    small = dict(norm_gain=d_ng, fox_forget_bias=d_fb[:, :FH], rwkv_shift_mix=_rwkv_vec_from_my(cfg, d_mu),
                 rwkv_w0=d_w0, rwkv_a0=d_a0, rwkv_k_k=d_kk, rwkv_k_a=d_ka, rwkv_r_k=d_rk, rwkv_ln_w=d_lnw,
                 rwkv_ln_b=d_lnb, final_norm_gain=d_fng)
    big = dict(w_in=d_wmy, rwkv_w2=d_w2p[:lora], rwkv_a2=d_a2p[:lora], w_proj_fox=d_wpf, w_proj_rwkv=d_wpr,
               w_out=d_wout)
    return loss8[0, 0], gx, small, big


_SMALL = ["norm_gain", "fox_forget_bias", "rwkv_shift_mix", "rwkv_w0", "rwkv_a0", "rwkv_k_k", "rwkv_k_a", "rwkv_r_k",
          "rwkv_ln_w", "rwkv_ln_b", "final_norm_gain"]
_WEIGHTS = ["norm_gain", "w_in", "fox_forget_bias", "rwkv_shift_mix", "rwkv_w0", "rwkv_w2", "rwkv_a0", "rwkv_a2",
            "rwkv_k_k", "rwkv_k_a", "rwkv_r_k", "rwkv_ln_w", "rwkv_ln_b", "w_proj_fox", "w_proj_rwkv", "w_out",
            "final_norm_gain"]


def _pack_small(arrs):
    parts = []
    for a in arrs:
        f = a.reshape(-1)
        parts.append(jnp.pad(f, (0, (-f.shape[0]) % LANES)))
    flat = jnp.concatenate(parts)
    rows = flat.shape[0] // LANES
    flat = jnp.pad(flat, (0, ((-rows) % 8) * LANES))
    return flat.reshape(-1, LANES)


def _unpack_small(packed, shapes):
    flat = packed.reshape(-1)
    out, pos = [], 0
    for s in shapes:
        n = int(np.prod(s))
        out.append(flat[pos:pos + n].reshape(s))
        pos += n + ((-n) % LANES)
    return out


def _shard_major(a, axis):
    parts = jnp.split(a, N_CHIPS, axis=axis)
    return jnp.stack(parts, axis=0)


def kernel(x, norm_gain, w_in, fox_forget_bias, rwkv_shift_mix, rwkv_w0, rwkv_w2, rwkv_a0, rwkv_a2, rwkv_k_k, rwkv_k_a, rwkv_r_k, rwkv_ln_w, rwkv_ln_b, w_proj_fox, w_proj_rwkv, w_out, final_norm_gain, loss_target, m_norm_gain, m_w_in, m_fox_forget_bias, m_rwkv_shift_mix, m_rwkv_w0, m_rwkv_w2, m_rwkv_a0, m_rwkv_a2, m_rwkv_k_k, m_rwkv_k_a, m_rwkv_r_k, m_rwkv_ln_w, m_rwkv_ln_b, m_w_proj_fox, m_w_proj_rwkv, m_w_out, m_final_norm_gain, v_norm_gain, v_w_in, v_fox_forget_bias, v_rwkv_shift_mix, v_rwkv_w0, v_rwkv_w2, v_rwkv_a0, v_rwkv_a2, v_rwkv_k_k, v_rwkv_k_a, v_rwkv_r_k, v_rwkv_ln_w, v_rwkv_ln_b, v_w_proj_fox, v_w_proj_rwkv, v_w_out, v_final_norm_gain):
    args = dict(locals())
    T, D = x.shape[1], x.shape[2]
    lora = rwkv_w2.shape[1]
    cfg = _Cfg(T, D, lora)
    RW = cfg.RW
    c_idx = lax.axis_index("c").astype(jnp.int32).reshape(1)
    me_chip = (2 * lax.axis_index("x") + lax.axis_index("y")).astype(jnp.int32)
    place = jnp.concatenate([me_chip.reshape(1), c_idx])

    w_in_s = w_in[0].astype(BF16)
    wp_s = jnp.concatenate([w_proj_fox[0], w_proj_rwkv[0]], axis=0)
    lora_s = jnp.concatenate([rwkv_w2[0], rwkv_a2[0]], axis=0)
    mine = [w_in_s, _cast_bf16("cast_w_proj", wp_s), _cast_bf16("cast_w_out", w_out[0]), lora_s]
    gathered = _gather_weights(mine)
    g_in, g_wp, g_out, g_lora = [lax.dynamic_update_slice(g, own[None], (me_chip, 0, 0))
                                 for g, own in zip(gathered, mine)]
    w_my = _shards_to_my_layout(cfg, g_in)
    wp = g_wp.transpose(1, 0, 2).reshape(2 * RW, D)
    wout = g_out.reshape(D, D)
    lo = g_lora.transpose(1, 0, 2).reshape(2 * lora, RW)

    loss_dev, gx, small, big = _local_step(
        cfg, x[0], loss_target[0], norm_gain, w_my, fox_forget_bias, rwkv_shift_mix, rwkv_w0, lo[:lora], rwkv_a0,
        lo[lora:], rwkv_k_k, rwkv_k_a, rwkv_r_k, rwkv_ln_w, rwkv_ln_b, wp[:RW], wp[RW:], wout, final_norm_gain)
    loss = lax.psum(loss_dev, ("x", "y", "c"))

    gs_in = _my_layout_to_shards(cfg, big["w_in"])
    gs_wp = _shard_major(jnp.concatenate([big["w_proj_fox"], big["w_proj_rwkv"]], axis=0), 1)
    gs_out = _shard_major(big["w_out"], 0)
    gs_lora = _shard_major(jnp.concatenate([big["rwkv_w2"], big["rwkv_a2"]], axis=0).astype(BF16), 1)
    gs = [gs_in, gs_wp, gs_out, gs_lora]
    names = ["w_in", "w_proj", "w_out", "lora"]
    recv1 = _exchange_halves(gs)
    chip_sums = [_add_halves("add_halves_" + nm, g, r, c_idx) for nm, g, r in zip(names, gs, recv1)]
    small_shapes = [args[nm].shape for nm in _SMALL]
    packed = _pack_small([small[nm] for nm in _SMALL])
    *recv2, small_all = _scatter_to_owners(chip_sums, packed)
    reduced = [_sum_chips("sum_chips_" + nm, r, own, place) for nm, r, own in zip(names, recv2, chip_sums)]
    g_small = _sum_slots("sum_small", small_all)
    g_in_f, g_wp_f, g_out_f, g_lora_f = _join_halves(reduced)

    grads = dict(zip(_SMALL, _unpack_small(g_small, small_shapes)))
    grads["w_in"] = g_in_f[None]
    grads["w_proj_fox"] = g_wp_f[None, :RW]
    grads["w_proj_rwkv"] = g_wp_f[None, RW:]
    grads["w_out"] = g_out_f[None]
    grads["rwkv_w2"] = g_lora_f[None, :lora]
    grads["rwkv_a2"] = g_lora_f[None, lora:]

    delta, new_m, new_v = {}, {}, {}
    w_small = _pack_small([args[nm] for nm in _SMALL])
    m_small = _pack_small([args["m_" + nm] for nm in _SMALL])
    v_small = _pack_small([args["v_" + nm] for nm in _SMALL])
    d_s, m_s, v_s = _adamw("adamw_small", w_small, g_small, m_small, v_small)
    for tgt, pk in ((delta, d_s), (new_m, m_s), (new_v, v_s)):
        tgt.update(zip(_SMALL, _unpack_small(pk, small_shapes)))
    for nm in ("w_in", "w_proj_fox", "w_proj_rwkv", "w_out", "rwkv_w2", "rwkv_a2"):
        shp = args[nm].shape
        two_d = (shp[1], shp[2])
        d_b, m_b, v_b = _adamw("adamw_" + nm, args[nm].reshape(two_d), grads[nm].reshape(two_d),
                               args["m_" + nm].reshape(two_d), args["v_" + nm].reshape(two_d))
        delta[nm], new_m[nm], new_v[nm] = d_b.reshape(shp), m_b.reshape(shp), v_b.reshape(shp)

    return (loss, gx[None], *[grads[n] for n in _WEIGHTS], *[delta[n] for n in _WEIGHTS],
            *[new_m[n] for n in _WEIGHTS], *[new_v[n] for n in _WEIGHTS])
```

```python
import functools

import numpy as np
import jax
import jax.numpy as jnp
from jax import lax
from jax.experimental import pallas as pl
from jax.experimental.pallas import tpu as pltpu

F32 = jnp.float32
BF16 = jnp.bfloat16
HI = lax.Precision.HIGHEST
MESH = pl.DeviceIdType.MESH

FOX_HEAD_DIM = 128
RWKV_HEAD_DIM = 64
RMS_EPS = 1e-6
GN_EPS = 64e-5
L2_EPS = 1e-12
ADAM_LR = 0.001
ADAM_B1 = 0.9
ADAM_B2 = 0.999
ADAM_EPS = 1e-08
ADAM_WD = 0.01
ADAM_STEP = 10

LANES = 128
VMEM_LIMIT = 56 * 1024 * 1024
SCAN_CHUNK = 64
SCAN_HEADS_PER_STEP = 8
SCAN_PASSES = (3, 1, 1)
N_CHIPS = 4
N_DEV = 8

_pcall = pl.pallas_call


def _cparams(sem=None):
    return pltpu.CompilerParams(dimension_semantics=sem, vmem_limit_bytes=VMEM_LIMIT)


def _softplus(x):
    return jnp.maximum(x, 0.0) + jnp.log(1.0 + jnp.exp(-jnp.abs(x)))


def _silu(z):
    return z * jax.nn.sigmoid(z)


def _rmsn(x, g):
    return x * lax.rsqrt(jnp.mean(x * x, axis=-1, keepdims=True) + RMS_EPS) * g


def _dot(a, b, dims="nn", precision=None):
    dn = {"nn": (((1,), (0,)), ((), ())), "nt": (((1,), (1,)), ((), ())), "tn": (((0,), (0,)), ((), ()))}[dims]
    return lax.dot_general(a, b, dn, precision=precision, preferred_element_type=F32)


def _split_bf16(x):
    hi = x.astype(BF16)
    return hi, (x - hi.astype(F32)).astype(BF16)


def _bdot_raw(a, b, ca, cb, passes):
    dn = (((ca,), (cb,)), ((0,), (0,)))
    mm = lambda p, q: lax.dot_general(p, q, dn, preferred_element_type=F32)
    if passes == 1:
        return mm(a.astype(BF16), b.astype(BF16))
    ah, al = _split_bf16(a)
    bh, bl = _split_bf16(b)
    return mm(ah, bh) + (mm(ah, bl) + mm(al, bh))


@functools.partial(jax.custom_vjp, nondiff_argnums=(2, 3, 4))
def _bdot_p(a, b, ca, cb, passes):
    return _bdot_raw(a, b, ca, cb, passes)


def _bdot_fwd(a, b, ca, cb, passes):
    return _bdot_raw(a, b, ca, cb, passes), (a, b)


def _bdot_bwd(ca, cb, passes, res, g):
    a, b = res
    if (ca, cb) == (2, 1):
        return _bdot_p(g, b, 2, 2, passes), _bdot_p(a, g, 1, 1, passes)
    if (ca, cb) == (2, 2):
        return _bdot_p(g, b, 2, 1, passes), _bdot_p(g, a, 1, 1, passes)
    assert (ca, cb) == (1, 1)
    return _bdot_p(b, g, 2, 2, passes), _bdot_p(a, g, 2, 1, passes)


_bdot_p.defvjp(_bdot_fwd, _bdot_bwd)


def _bdot(a, b, ca, cb, passes=3):
    return _bdot_p(a, b, ca, cb, passes)


def _dot3(a, b):
    return _bdot(a[None], b[None], 2, 1)[0]


@jax.custom_vjp
def _xdot(x, m, mt):
    hi, lo = _split_bf16(x)
    m16 = m.astype(BF16)
    return _dot(hi, m16) + _dot(lo, m16)


def _xdot_fwd(x, m, mt):
    return _xdot(x, m, mt), (m, mt)


def _xdot_bwd(res, g):
    m, mt = res
    return _xdot(g, mt, m), jnp.zeros_like(m), jnp.zeros_like(mt)


_xdot.defvjp(_xdot_fwd, _xdot_bwd)


class _Cfg:
    def __init__(self, T, D, lora):
        self.T, self.D, self.lora = T, D, lora
        self.FW = D // 2
        self.FH = self.FW // FOX_HEAD_DIM
        self.RW = D // 2
        self.RH = self.RW // RWKV_HEAD_DIM
        self.LP = -(-lora // LANES) * LANES
        self.o_fox = 0
        self.o_rwkv = 4 * self.FW
        self.o_gate = self.o_rwkv + 4 * self.RW
        self.o_f = self.o_gate + 2 * D
        self.o_wd = self.o_f + LANES
        self.o_ad = self.o_wd + self.LP
        end = self.o_ad + self.LP
        self.tn = 1280 if D >= 2048 else LANES
        self.ncol = -(-end // self.tn) * self.tn
        self.in_cols = 4 * self.FW + self.FH + 4 * self.RW + 2 * lora + 2 * D
        self.scp = -(-(self.in_cols // N_CHIPS) // LANES) * LANES
        self.rseg = 4 * self.RW + 2 * self.LP
        self.C = min(SCAN_CHUNK, T)
        self.tr = min(256, T)
        self.hb = min(SCAN_HEADS_PER_STEP, self.RH)

    def segments(self):
        FW, FH, RW, lo, D = self.FW, self.FH, self.RW, self.lora, self.D
        g_f = 4 * FW
        g_r = g_f + FH
        g_wd = g_r + 4 * RW
        g_ad = g_wd + lo
        g_g = g_ad + lo
        dh = FOX_HEAD_DIM
        qkv = [(j * FW + h * dh, dh, (3 * h + j) * dh) for h in range(FH) for j in range(3)]
        return qkv + [(3 * FW, FW, 3 * FW), (g_f, FH, self.o_f), (g_r, 4 * RW, self.o_rwkv), (g_wd, lo, self.o_wd),
                      (g_ad, lo, self.o_ad), (g_g, 2 * D, self.o_gate)]


def _to_my_layout(cfg, wg):
    R = wg.shape[0]
    segs = sorted(cfg.segments(), key=lambda s: s[2])
    parts, pos = [], 0
    for g0, w, m0 in segs:
        if m0 > pos:
            parts.append(jnp.zeros((R, m0 - pos), wg.dtype))
        parts.append(wg[:, g0:g0 + w])
        pos = m0 + w
    if cfg.ncol > pos:
        parts.append(jnp.zeros((R, cfg.ncol - pos), wg.dtype))
    return jnp.concatenate(parts, axis=1)


def _from_my_layout(cfg, wm):
    segs = sorted(cfg.segments(), key=lambda s: s[0])
    return jnp.concatenate([wm[:, m0:m0 + w] for g0, w, m0 in segs], axis=1)


def _shards_to_my_layout(cfg, g):
    R, sc = g.shape[1], g.shape[2]
    segs = sorted(cfg.segments(), key=lambda s: s[2])
    parts, pos = [], 0
    for g0, w, m0 in segs:
        if m0 > pos:
            parts.append(jnp.zeros((R, m0 - pos), g.dtype))
        for s in range(N_CHIPS):
            lo, hi = max(g0, s * sc), min(g0 + w, (s + 1) * sc)
            if lo < hi:
                parts.append(g[s, :, lo - s * sc:hi - s * sc])
        pos = m0 + w
    if cfg.ncol > pos:
        parts.append(jnp.zeros((R, cfg.ncol - pos), g.dtype))
    return jnp.concatenate(parts, axis=1)


def _my_layout_to_shards(cfg, wm):
    sc = cfg.in_cols // N_CHIPS
    segs = sorted(cfg.segments(), key=lambda s: s[0])
    shards = []
    for s in range(N_CHIPS):
        parts = []
        for g0, w, m0 in segs:
            lo, hi = max(g0, s * sc), min(g0 + w, (s + 1) * sc)
            if lo < hi:
                parts.append(wm[:, m0 + lo - g0:m0 + hi - g0])
        shards.append(jnp.concatenate(parts, axis=1))
    return jnp.stack(shards, axis=0)


def _rwkv_vec_to_my(cfg, v):
    RW4, lo, LP = 4 * cfg.RW, cfg.lora, cfg.LP
    z = jnp.zeros((1, LP - lo), v.dtype)
    return jnp.concatenate([v[:, :RW4], v[:, RW4:RW4 + lo], z, v[:, RW4 + lo:], z], axis=1)


def _rwkv_vec_from_my(cfg, v):
    RW4, lo, LP = 4 * cfg.RW, cfg.lora, cfg.LP
    return jnp.concatenate([v[:, :RW4], v[:, RW4:RW4 + lo], v[:, RW4 + LP:RW4 + LP + lo]], axis=1)


def _mm(name, a, b, dims, out_dtype, tm, tn, tk, comm=None):
    (M, K) = a.shape if dims != "tn" else a.shape[::-1]
    N = b.shape[0] if dims == "nt" else b.shape[1]
    tm, tn, tk = min(tm, M), min(tn, N), min(tk, K)
    assert M % tm == 0 and N % tn == 0 and K % tk == 0, (name, M, N, K, tm, tn, tk)
    nk = K // tk
    steps = (M // tm, N // tn, nk)
    c_in, c_out, c_scr, c_start, c_finish = comm if comm else ([], [], [], None, None)
    if dims == "nn":
        a_spec = pl.BlockSpec((tm, tk), lambda i, j, k: (i, k))
        b_spec = pl.BlockSpec((tk, tn), lambda i, j, k: (k, j))
    elif dims == "nt":
        a_spec = pl.BlockSpec((tm, tk), lambda i, j, k: (i, k))
        b_spec = pl.BlockSpec((tn, tk), lambda i, j, k: (j, k))
    else:
        a_spec = pl.BlockSpec((tk, tm), lambda i, j, k: (k, i))
        b_spec = pl.BlockSpec((tk, tn), lambda i, j, k: (k, j))

    n_acc = 1 if nk > 1 else 0

    def body(a_ref, b_ref, *rest):
        cin, o_ref = rest[:len(c_in)], rest[len(c_in)]
        cout = rest[len(c_in) + 1:len(c_in) + 1 + len(c_out)]
        scr = rest[len(c_in) + 1 + len(c_out):]
        ids = [pl.program_id(d) for d in range(3)]
        if comm:
            @pl.when((ids[0] == 0) & (ids[1] == 0) & (ids[2] == 0))
            def _():
                c_start(cin, cout, scr[n_acc:])

        if nk == 1:
            o_ref[...] = _dot(a_ref[...], b_ref[...], dims).astype(o_ref.dtype)
        else:
            acc_ref, k = scr[0], ids[2]

            @pl.when(k == 0)
            def _():
                acc_ref[...] = jnp.zeros_like(acc_ref)

            acc_ref[...] += _dot(a_ref[...], b_ref[...], dims)

            @pl.when(k == nk - 1)
            def _():
                o_ref[...] = acc_ref[...].astype(o_ref.dtype)

        if comm:
            @pl.when((ids[0] == steps[0] - 1) & (ids[1] == steps[1] - 1) & (ids[2] == steps[2] - 1))
            def _():
                c_finish(cin, cout, scr[n_acc:])

    res = _pcall(
        body, name=name, grid=steps,
        in_specs=[a_spec, b_spec] + [_ANY] * len(c_in),
        out_specs=[pl.BlockSpec((tm, tn), lambda i, j, k: (i, j))] + [_ANY] * len(c_out),
        out_shape=[jax.ShapeDtypeStruct((M, N), out_dtype)] + list(c_out),
        scratch_shapes=([pltpu.VMEM((tm, tn), F32)] if nk > 1 else []) + list(c_scr),
        compiler_params=_cparams(("arbitrary",) * 3 if comm else ("parallel", "parallel", "arbitrary")),
    )(a, b, *c_in)
    return res if comm else res[0]


def _tile(tr, w, cb=0):
    return pl.BlockSpec((tr, w), lambda i: (i, cb))


def _const(shape):
    nd = len(shape)
    return pl.BlockSpec(shape, lambda i: (0,) * nd)


def _acc_store(i, ref, val):
    @pl.when(i == 0)
    def _():
        ref[...] = val

    @pl.when(i > 0)
    def _():
        ref[...] += val


def _rms_fwd(cfg, x2, g):
    T, D, tr = cfg.T, cfg.D, cfg.tr

    def body(x_ref, g_ref, h_ref):
        h_ref[...] = _rmsn(x_ref[...], g_ref[...]).astype(BF16)

    return _pcall(body, name="rms_fwd", grid=(T // tr,), in_specs=[_tile(tr, D), _const((1, D))],
                  out_specs=_tile(tr, D), out_shape=jax.ShapeDtypeStruct((T, D), BF16),
                  compiler_params=_cparams(("parallel",)))(x2, g)


def _rms_bwd(cfg, x2, g, dh, dres):
    T, D, tr = cfg.T, cfg.D, cfg.tr

    def body(x_ref, g_ref, dh_ref, dres_ref, gx_ref, dg_ref):
        _, vjp = jax.vjp(_rmsn, x_ref[...], g_ref[...])
        dx, dg = vjp(dh_ref[...])
        gx_ref[...] = dx + dres_ref[...]
        _acc_store(pl.program_id(0), dg_ref, dg)

    return _pcall(body, name="rms_bwd", grid=(T // tr,),
                  in_specs=[_tile(tr, D), _const((1, D)), _tile(tr, D), _tile(tr, D)],
                  out_specs=[_tile(tr, D), _const((1, D))],
                  out_shape=[jax.ShapeDtypeStruct((T, D), F32), jax.ShapeDtypeStruct((1, D), F32)],
                  compiler_params=_cparams(("arbitrary",)))(x2, g, dh, dres)


def _final(cfg, x2, mo, fg, target):
    T, D, tr = cfg.T, cfg.D, cfg.tr

    def loss_fn(hres, g, tgt):
        err = _rmsn(hres, g) - tgt
        return 0.5 * jnp.sum(jnp.mean(err * err, axis=-1, keepdims=True), axis=0, keepdims=True)

    def body(x_ref, mo_ref, g_ref, t_ref, loss_ref, dres_ref, dres16_ref, dg_ref):
        hres = x_ref[...] + mo_ref[...]
        loss, vjp = jax.vjp(functools.partial(loss_fn, tgt=t_ref[...]), hres, g_ref[...])
        dres, dg = vjp(jnp.ones((1, 1), F32))
        dres_ref[...] = dres
        dres16_ref[...] = dres.astype(BF16)
        i = pl.program_id(0)
        _acc_store(i, dg_ref, dg)
        _acc_store(i, loss_ref, jnp.broadcast_to(loss, (8, LANES)))

    return _pcall(body, name="final_loss", grid=(T // tr,),
                  in_specs=[_tile(tr, D), _tile(tr, D), _const((1, D)), _tile(tr, D)],
                  out_specs=[_const((8, LANES)), _tile(tr, D), _tile(tr, D), _const((1, D))],
                  out_shape=[jax.ShapeDtypeStruct((8, LANES), F32), jax.ShapeDtypeStruct((T, D), F32),
                             jax.ShapeDtypeStruct((T, D), BF16), jax.ShapeDtypeStruct((1, D), F32)],
                  compiler_params=_cparams(("arbitrary",)))(x2, mo, fg, target)


def _merge_fn(pa, pb, ga, gb):
    return jax.nn.sigmoid(ga) * pa + jax.nn.sigmoid(gb) * pb


def _merge_fwd(cfg, pa, pb, u):
    T, D, tr = cfg.T, cfg.D, cfg.tr
    cga, cgb = cfg.o_gate // D, cfg.o_gate // D + 1

    def body(pa_ref, pb_ref, ga_ref, gb_ref, m_ref):
        m_ref[...] = _merge_fn(pa_ref[...], pb_ref[...], ga_ref[...], gb_ref[...]).astype(BF16)

    return _pcall(body, name="merge_fwd", grid=(T // tr,),
                  in_specs=[_tile(tr, D), _tile(tr, D), _tile(tr, D, cga), _tile(tr, D, cgb)],
                  out_specs=_tile(tr, D), out_shape=jax.ShapeDtypeStruct((T, D), BF16),
                  compiler_params=_cparams(("parallel",)))(pa, pb, u, u)


def _merge_bwd(cfg, pa, pb, u, dm):
    T, D, tr = cfg.T, cfg.D, cfg.tr
    cga, cgb = cfg.o_gate // D, cfg.o_gate // D + 1

    def body(pa_ref, pb_ref, ga_ref, gb_ref, dm_ref, dpa_ref, dpb_ref, dg_ref):
        _, vjp = jax.vjp(_merge_fn, pa_ref[...], pb_ref[...], ga_ref[...], gb_ref[...])
        dpa, dpb, dga, dgb = vjp(dm_ref[...])
        dpa_ref[...] = dpa.astype(BF16)
        dpb_ref[...] = dpb.astype(BF16)
        dg_ref[:, :D] = dga.astype(BF16)
        dg_ref[:, D:] = dgb.astype(BF16)

    return _pcall(body, name="merge_bwd", grid=(T // tr,),
                  in_specs=[_tile(tr, D), _tile(tr, D), _tile(tr, D, cga), _tile(tr, D, cgb), _tile(tr, D)],
                  out_specs=[_tile(tr, D), _tile(tr, D), _tile(tr, 2 * D, cfg.o_gate // (2 * D))],
                  out_shape=[jax.ShapeDtypeStruct((T, D), BF16), jax.ShapeDtypeStruct((T, D), BF16),
                             jax.ShapeDtypeStruct((T, cfg.ncol), BF16)],
                  compiler_params=_cparams(("parallel",)))(pa, pb, u, u, dm)


def _gate_fn(o, z):
    return o * _silu(z)


def _gate_a_fwd(cfg, o, u):
    T, FW, tr = cfg.T, cfg.FW, cfg.tr

    def body(o_ref, z_ref, oa_ref):
        oa_ref[...] = _gate_fn(o_ref[...], z_ref[...]).astype(BF16)

    return _pcall(body, name="gate_a_fwd", grid=(T // tr,), in_specs=[_tile(tr, FW), _tile(tr, FW, 3)],
                  out_specs=_tile(tr, FW), out_shape=jax.ShapeDtypeStruct((T, FW), BF16),
                  compiler_params=_cparams(("parallel",)))(o, u)


def _gate_a_bwd(cfg, o, u, doa, du):
    T, FW, tr = cfg.T, cfg.FW, cfg.tr

    def body(o_ref, z_ref, doa_ref, du_in, do_ref, dz_ref):
        _, vjp = jax.vjp(_gate_fn, o_ref[...], z_ref[...])
        do, dz = vjp(doa_ref[...])
        do_ref[...] = do
        dz_ref[...] = dz.astype(BF16)

    return _pcall(body, name="gate_a_bwd", grid=(T // tr,),
                  in_specs=[_tile(tr, FW), _tile(tr, FW, 3), _tile(tr, FW), _ANY],
                  out_specs=[_tile(tr, FW), _tile(tr, FW, 3)],
                  out_shape=[jax.ShapeDtypeStruct((T, FW), F32), jax.ShapeDtypeStruct(du.shape, BF16)],
                  input_output_aliases={3: 1},
                  compiler_params=_cparams(("parallel",)))(o, u, doa, du)


def _fox_prep(cfg, u, fb):
    T, tr = cfg.T, cfg.tr
    cf = cfg.o_f // LANES

    def body(f_ref, fb_ref, c_ref, carry_ref):
        i = pl.program_id(0)

        @pl.when(i == 0)
        def _():
            carry_ref[...] = jnp.zeros_like(carry_ref)

        lf = -_softplus(-(f_ref[...] + fb_ref[...]))
        r = lax.broadcasted_iota(jnp.int32, (tr, tr), 0)
        c = lax.broadcasted_iota(jnp.int32, (tr, tr), 1)
        tri = (r >= c).astype(F32)
        c_ref[...] = _dot(tri, lf, precision=HI) + carry_ref[...]
        carry_ref[...] += jnp.sum(lf, axis=0, keepdims=True)

    return _pcall(body, name="fox_prep", grid=(T // tr,), in_specs=[_tile(tr, LANES, cf), _const((1, LANES))],
                  out_specs=_tile(tr, LANES), out_shape=jax.ShapeDtypeStruct((T, LANES), F32),
                  scratch_shapes=[pltpu.VMEM((1, LANES), F32)], compiler_params=_cparams(("arbitrary",)))(u, fb)


def _fox_prep_bwd(cfg, u, fb, dc):
    T, tr = cfg.T, cfg.tr
    cf = cfg.o_f // LANES
    nb = T // tr

    def body(f_ref, fb_ref, dc_ref, df_ref, dfb_ref, carry_ref):
        i = pl.program_id(0)

        @pl.when(i == 0)
        def _():
            carry_ref[...] = jnp.zeros_like(carry_ref)

        dc = dc_ref[...]
        r = lax.broadcasted_iota(jnp.int32, (tr, tr), 0)
        c = lax.broadcasted_iota(jnp.int32, (tr, tr), 1)
        triu = (r <= c).astype(F32)
        dlf = _dot(triu, dc, precision=HI) + carry_ref[...]
        carry_ref[...] += jnp.sum(dc, axis=0, keepdims=True)
        dz = dlf * jax.nn.sigmoid(-(f_ref[...] + fb_ref[...]))
        df_ref[...] = dz.astype(BF16)
        _acc_store(i, dfb_ref, jnp.sum(dz, axis=0, keepdims=True))

    rev = lambda i: (nb - 1 - i, 0)
    return _pcall(body, name="fox_prep_bwd", grid=(nb,),
                  in_specs=[pl.BlockSpec((tr, LANES), lambda i: (nb - 1 - i, cf)), _const((1, LANES)),
                            pl.BlockSpec((tr, LANES), rev)],
                  out_specs=[pl.BlockSpec((tr, LANES), rev), _const((1, LANES))],
                  out_shape=[jax.ShapeDtypeStruct((T, LANES), BF16), jax.ShapeDtypeStruct((1, LANES), F32)],
                  scratch_shapes=[pltpu.VMEM((1, LANES), F32)], compiler_params=_cparams(("arbitrary",)))(u, fb, dc)


def _attn_logits(q_ref, k_ref, c_ref, i, tq, te):
    s = _dot(q_ref[...].astype(BF16), k_ref[0:te, :].astype(BF16), "nt") * (FOX_HEAD_DIM ** -0.5) - c_ref[0, :, 0:te]
    row = i * tq + lax.broadcasted_iota(jnp.int32, (tq, te), 0)
    col = lax.broadcasted_iota(jnp.int32, (tq, te), 1)
    return jnp.where(col <= row, s, -1e30)


def _per_query_tile(i, nq, tq, fn):
    for ii in range(nq):
        pl.when(i == ii)(functools.partial(fn, (ii + 1) * tq))


def _attn_fwd(cfg, u, c_rows):
    T, FW, FH = cfg.T, cfg.FW, cfg.FH
    tq = min(256, T)
    dh = FOX_HEAD_DIM

    def body(q_ref, k_ref, v_ref, c_ref, o_ref, lse_ref):
        i = pl.program_id(1)

        def tile(te):
            s = _attn_logits(q_ref, k_ref, c_ref, i, tq, te)
            m = jnp.max(s, axis=1, keepdims=True)
            p = jnp.exp(s - m)
            l = jnp.sum(p, axis=1, keepdims=True)
            o_ref[...] = _dot(p.astype(BF16), v_ref[0:te, :].astype(BF16)) / l
            lse_ref[0] = m + jnp.log(l)

        _per_query_tile(i, T // tq, tq, tile)

    return _pcall(
        body, name="fox_attn_fwd", grid=(FH, T // tq),
        in_specs=[pl.BlockSpec((tq, dh), lambda h, i: (i, 3 * h)), pl.BlockSpec((T, dh), lambda h, i: (0, 3 * h + 1)),
                  pl.BlockSpec((T, dh), lambda h, i: (0, 3 * h + 2)), pl.BlockSpec((1, 1, T), lambda h, i: (h, 0, 0))],
        out_specs=[pl.BlockSpec((tq, dh), lambda h, i: (i, h)), pl.BlockSpec((1, tq, 1), lambda h, i: (h, i, 0))],
        out_shape=[jax.ShapeDtypeStruct((T, FW), F32), jax.ShapeDtypeStruct((FH, T, 1), F32)],
        compiler_params=_cparams(("parallel", "arbitrary")),
    )(u, u, u, c_rows)


def _attn_bwd(cfg, u, c_rows, lse, do, du):
    T, FW, FH = cfg.T, cfg.FW, cfg.FH
    tq = min(256, T)
    nq = T // tq
    dh = FOX_HEAD_DIM
    scale = dh ** -0.5

    def body(q_ref, k_ref, v_ref, c_ref, lse_ref, do_ref, du_in, du_ref, dcol_ref, dk_acc, dv_acc):
        i = pl.program_id(1)

        @pl.when(i == 0)
        def _():
            dk_acc[...] = jnp.zeros_like(dk_acc)
            dv_acc[...] = jnp.zeros_like(dv_acc)
            dcol_ref[...] = jnp.zeros_like(dcol_ref)

        def tile(te):
            s = _attn_logits(q_ref, k_ref, c_ref, i, tq, te)
            p = jnp.exp(s - lse_ref[0])
            do_v = do_ref[...]
            dp = _dot(do_v.astype(BF16), v_ref[0:te, :].astype(BF16), "nt")
            delta = jnp.sum(p * dp, axis=1, keepdims=True)
            ds = p * (dp - delta)
            ds16 = ds.astype(BF16)
            du_ref[te - tq:te, 0:dh] = (_dot(ds16, k_ref[0:te, :].astype(BF16)) * scale).astype(BF16)
            dk_acc[0:te, :] += _dot(ds16, q_ref[...].astype(BF16), "tn") * scale
            dv_acc[0:te, :] += _dot(p.astype(BF16), do_v.astype(BF16), "tn")
            dcol_ref[0, :, 0:te] += jnp.sum(ds, axis=0, keepdims=True)

        _per_query_tile(i, nq, tq, tile)

        @pl.when(i == nq - 1)
        def _():
            du_ref[:, dh:2 * dh] = dk_acc[...].astype(BF16)
            du_ref[:, 2 * dh:3 * dh] = dv_acc[...].astype(BF16)

    return _pcall(
        body, name="fox_attn_bwd", grid=(FH, nq),
        in_specs=[pl.BlockSpec((tq, dh), lambda h, i: (i, 3 * h)), pl.BlockSpec((T, dh), lambda h, i: (0, 3 * h + 1)),
                  pl.BlockSpec((T, dh), lambda h, i: (0, 3 * h + 2)), pl.BlockSpec((1, 1, T), lambda h, i: (h, 0, 0)),
                  pl.BlockSpec((1, tq, 1), lambda h, i: (h, i, 0)), pl.BlockSpec((tq, dh), lambda h, i: (i, h)), _ANY],
        out_specs=[pl.BlockSpec((T, 3 * dh), lambda h, i: (0, h)), pl.BlockSpec((1, 1, T), lambda h, i: (h, 0, 0))],
        out_shape=[jax.ShapeDtypeStruct(du.shape, BF16), jax.ShapeDtypeStruct((FH, 1, T), F32)],
        scratch_shapes=[pltpu.VMEM((T, dh), F32), pltpu.VMEM((T, dh), F32)],
        input_output_aliases={6: 0},
        compiler_params=_cparams(("parallel", "arbitrary")),
    )(u, u, u, c_rows, lse, do, du)


def _head_indicators(cfg):
    ind = np.zeros((cfg.RW, LANES), np.float32)
    ind[np.arange(cfg.RW), np.arange(cfg.RW) // RWKV_HEAD_DIM] = 1.0
    pad = np.zeros((1, LANES), np.float32)
    pad[0, cfg.RH:] = 1.0
    return jnp.asarray(ind), jnp.asarray(ind.T.copy()), jnp.asarray(pad)


def _prep_fn(us_r, us_k, us_v, us_wd, us_ad, w0, w2p, a0, a2p, k_k, k_a, ind, ind_t, pad):
    wpre = w0 + _dot3(jnp.tanh(us_wd), w2p)
    w = -_softplus(-wpre) - 0.5
    lw = -jnp.exp(w)
    a = jax.nn.sigmoid(a0 + _dot3(us_ad, a2p))
    kk = us_k * k_k
    ss = _xdot(kk * kk, ind, ind_t) + pad
    inv = 1.0 / jnp.maximum(jnp.sqrt(ss), L2_EPS)
    kkn = kk * _xdot(inv, ind_t, ind)
    kp = us_k * (1.0 + (a - 1.0) * k_a)
    return us_r, lw, kp, us_v, -kkn, kkn * a


def _shifted(u, prev_row, mu, first):
    n = u.shape[0]
    rolled = pltpu.roll(u, 1, 0)
    row = lax.broadcasted_iota(jnp.int32, u.shape, 0)
    p0 = jnp.where(first, jnp.zeros_like(prev_row), prev_row)
    prev = jnp.where(row == 0, jnp.broadcast_to(p0, u.shape), rolled)
    return u + (prev - u) * mu, prev


def _rwkv_specs(cfg, tr):
    RW, LP = cfg.RW, cfg.LP
    base = cfg.o_rwkv // RW
    cols = [(RW, base), (RW, base + 1), (RW, base + 2), (RW, base + 3), (LP, cfg.o_wd // LP), (LP, cfg.o_ad // LP)]
    cur = [pl.BlockSpec((tr, w), (lambda i, cb=cb: (i, cb))) for w, cb in cols]
    prv = [pl.BlockSpec((8, w), (lambda i, cb=cb: (jnp.maximum(i * (tr // 8) - 1, 0), cb))) for w, cb in cols]
    return cols, cur, prv


def _mu_pieces(cfg, mu_ref):
    RW, LP = cfg.RW, cfg.LP
    offs = [0, RW, 2 * RW, 3 * RW, 4 * RW, 4 * RW + LP, 4 * RW + 2 * LP]
    return [mu_ref[:, offs[j]:offs[j + 1]] for j in range(6)]


def _rwkv_prep_fwd(cfg, u, mu, w0, w2p, a0, a2p, k_k, k_a):
    T, RW, LP, tr = cfg.T, cfg.RW, cfg.LP, cfg.tr
    ind, ind_t, pad = _head_indicators(cfg)
    cols, cur, prv = _rwkv_specs(cfg, tr)

    def body(*refs):
        u_refs, p_refs = refs[0:6], refs[6:12]
        mu_ref, w0_ref, w2_ref, a0_ref, a2_ref, kk_ref, ka_ref, ind_ref, indt_ref, pad_ref = refs[12:22]
        outs = refs[22:]
        first = pl.program_id(0) == 0
        mus = _mu_pieces(cfg, mu_ref)
        us = [_shifted(u_refs[j][...], p_refs[j][7:8, :], mus[j], first)[0] for j in range(6)]
        res = _prep_fn(us[0], us[1], us[2], us[4], us[5], w0_ref[...], w2_ref[...], a0_ref[...], a2_ref[...],
                       kk_ref[...], ka_ref[...], ind_ref[...], indt_ref[...], pad_ref[...])
        for j in range(6):
            outs[j][...] = res[j]
        outs[6][...] = us[3]

    consts = [mu, w0, w2p, a0, a2p, k_k, k_a, ind, ind_t, pad]
    return _pcall(body, name="rwkv_prep_fwd", grid=(T // tr,),
                  in_specs=cur + prv + [_const(c.shape) for c in consts],
                  out_specs=[_tile(tr, RW)] * 7, out_shape=[jax.ShapeDtypeStruct((T, RW), F32)] * 7,
                  compiler_params=_cparams(("parallel",)))(*([u] * 12), *consts)


def _rwkv_prep_bwd(cfg, u, mu, w0, w2p, a0, a2p, k_k, k_a, cots, dzb):
    T, RW, LP = cfg.T, cfg.RW, cfg.LP
    tr = min(128, T)
    ind, ind_t, pad = _head_indicators(cfg)
    cols, cur, prv = _rwkv_specs(cfg, tr)
    rseg = cfg.rseg

    def body(*refs):
        u_refs, p_refs = refs[0:6], refs[6:12]
        mu_ref, w0_ref, w2_ref, a0_ref, a2_ref, kk_ref, ka_ref, ind_ref, indt_ref, pad_ref = refs[12:22]
        cot_refs, dzb_ref = refs[22:28], refs[28]
        dus_ref, dmu_ref, dw0_ref, dw2_ref, da0_ref, da2_ref, dkk_ref, dka_ref = refs[29:]
        i = pl.program_id(0)
        first = i == 0
        mus = _mu_pieces(cfg, mu_ref)
        sh = [_shifted(u_refs[j][...], p_refs[j][7:8, :], mus[j], first) for j in range(6)]
        us = [s[0] for s in sh]
        fn = functools.partial(_prep_fn, ind=ind_ref[...], ind_t=indt_ref[...], pad=pad_ref[...])
        _, vjp = jax.vjp(fn, us[0], us[1], us[2], us[4], us[5], w0_ref[...], w2_ref[...], a0_ref[...], a2_ref[...],
                         kk_ref[...], ka_ref[...])
        d = vjp(tuple(c[...] for c in cot_refs))
        dus = [d[0], d[1], d[2], dzb_ref[...], d[3], d[4]]
        offs = [0, RW, 2 * RW, 3 * RW, 4 * RW, 4 * RW + LP, 4 * RW + 2 * LP]
        for j in range(6):
            dus_ref[:, offs[j]:offs[j + 1]] = dus[j]
            dmu_j = jnp.sum(dus[j] * (sh[j][1] - u_refs[j][...]), axis=0, keepdims=True)

            @pl.when(first)
            def _(j=j, dmu_j=dmu_j):
                dmu_ref[:, offs[j]:offs[j + 1]] = dmu_j

            @pl.when(i > 0)
            def _(j=j, dmu_j=dmu_j):
                dmu_ref[:, offs[j]:offs[j + 1]] += dmu_j
        for ref, val in zip((dw0_ref, dw2_ref, da0_ref, da2_ref, dkk_ref, dka_ref), d[5:11]):
            _acc_store(i, ref, val)

    consts = [mu, w0, w2p, a0, a2p, k_k, k_a, ind, ind_t, pad]
    vec = jax.ShapeDtypeStruct((1, RW), F32)
    mat = jax.ShapeDtypeStruct((LP, RW), F32)
    return _pcall(body, name="rwkv_prep_bwd", grid=(T // tr,),
                  in_specs=cur + prv + [_const(c.shape) for c in consts] + [_tile(tr, RW)] * 7,
                  out_specs=[_tile(tr, rseg), _const((1, rseg)), _const((1, RW)), _const((LP, RW)), _const((1, RW)),
                             _const((LP, RW)), _const((1, RW)), _const((1, RW))],
                  out_shape=[jax.ShapeDtypeStruct((T, rseg), F32), jax.ShapeDtypeStruct((1, rseg), F32),
                             vec, mat, vec, mat, vec, vec],
                  compiler_params=_cparams(("arbitrary",)))(*([u] * 12), *consts, *cots, dzb)


def _shift_bwd(cfg, dus, mu, df, du):
    T, tr, RW, LP = cfg.T, cfg.tr, cfg.RW, cfg.LP
    nb = T // tr
    tail = cfg.ncol - cfg.o_f
    assert cfg.o_rwkv % (4 * RW) == 0 and (4 * RW) % (2 * LP) == 0 and cfg.o_f % tail == 0

    def shifted(d_ref, n_ref, mu_ref):
        d = d_ref[...]
        rolled = pltpu.roll(d, tr - 1, 0)
        row = lax.broadcasted_iota(jnp.int32, d.shape, 0)
        n0 = jnp.where(pl.program_id(0) == nb - 1, jnp.zeros_like(n_ref[0:1, :]), n_ref[0:1, :])
        nxt = jnp.where(row == tr - 1, jnp.broadcast_to(n0, d.shape), rolled)
        mu_v = mu_ref[...]
        return (d * (1.0 - mu_v) + nxt * mu_v).astype(BF16)

    def main_body(d_ref, n_ref, mu_ref, du_in, du_ref):
        du_ref[...] = shifted(d_ref, n_ref, mu_ref)

    def tail_body(d_ref, n_ref, mu_ref, df_ref, du_in, du_ref):
        du_ref[:, 0:LANES] = df_ref[...]
        du_ref[:, LANES:LANES + 2 * LP] = shifted(d_ref, n_ref, mu_ref)
        if tail > LANES + 2 * LP:
            du_ref[:, LANES + 2 * LP:] = jnp.zeros((tr, tail - LANES - 2 * LP), BF16)

    def specs(w, cb):
        return [_tile(tr, w, cb),
                pl.BlockSpec((8, w), lambda i: (jnp.minimum((i + 1) * (tr // 8), T // 8 - 1), cb)),
                pl.BlockSpec((1, w), lambda i: (0, cb))]

    out = jax.ShapeDtypeStruct(du.shape, BF16)
    du = _pcall(main_body, name="shift_bwd_main", grid=(nb,), in_specs=specs(4 * RW, 0) + [_ANY],
                out_specs=_tile(tr, 4 * RW, cfg.o_rwkv // (4 * RW)), out_shape=out, input_output_aliases={3: 0},
                compiler_params=_cparams(("parallel",)))(dus, dus, mu, du)
    return _pcall(tail_body, name="shift_bwd_tail", grid=(nb,),
                  in_specs=specs(2 * LP, 4 * RW // (2 * LP)) + [_tile(tr, LANES), _ANY],
                  out_specs=_tile(tr, tail, cfg.o_f // tail), out_shape=out, input_output_aliases={4: 0},
                  compiler_params=_cparams(("parallel",)))(dus, dus, mu, df, du)


def _chunk_local(r, lw, k, v, a, b):
    H, C, K = r.shape
    row = lax.broadcasted_iota(jnp.int32, (C, C), 0)
    col = lax.broadcasted_iota(jnp.int32, (C, C), 1)
    incl = jnp.broadcast_to((row >= col).astype(F32)[None], (H, C, C))
    strict = (row > col)[None]
    lower = (row >= col)[None]
    eye = (row == col)[None]
    zero = jnp.zeros((), F32)
    L = _bdot(incl, lw, 2, 1)
    LC = jnp.sum(lw, axis=1, keepdims=True)
    eL = jnp.exp(L)
    eLn = jnp.exp(-L)
    at = a * jnp.exp(L - lw)
    rt = r * eL
    bt = b * eLn
    kt = k * eLn
    eR = jnp.exp(LC - L)
    bh = b * eR
    kh = k * eR
    gram = functools.partial(_bdot, passes=SCAN_PASSES[0])
    inv = functools.partial(_bdot, passes=SCAN_PASSES[1])
    app = functools.partial(_bdot, passes=SCAN_PASSES[2])
    n_ab = jnp.where(strict, gram(at, bt, 2, 2), zero)
    n_ak = jnp.where(strict, gram(at, kt, 2, 2), zero)
    m_rb = jnp.where(lower, gram(rt, bt, 2, 2), zero)
    m_rk = jnp.where(lower, gram(rt, kt, 2, 2), zero)
    M = n_ab
    P = jnp.where(eye, 1.0, zero) + n_ab
    for _ in range(1, max(1, int(np.ceil(np.log2(C))))):
        M = inv(M, M, 2, 1)
        P = P + inv(M, P, 2, 1)
    W = app(P, at, 2, 1)
    Uloc = app(P, app(n_ak, v, 2, 1), 2, 1)
    Q = rt + app(m_rb, W, 2, 1)
    Yloc = app(m_rb, Uloc, 2, 1) + app(m_rk, v, 2, 1)
    A = jnp.where(eye, jnp.exp(LC), zero) + app(W, bh, 1, 1)
    Sloc = app(Uloc, bh, 1, 1) + app(v, kh, 1, 1)
    return Q, Yloc, A, Sloc


def _split_heads(ref, n):
    N = RWKV_HEAD_DIM
    return jnp.stack([ref[:, h * N:(h + 1) * N] for h in range(n)], axis=0)


def _merge_heads(x):
    return jnp.concatenate([x[h] for h in range(x.shape[0])], axis=1)


def _scan_local_specs(cfg):
    N, HB = RWKV_HEAD_DIM, cfg.hb
    grid = (cfg.RH // HB, cfg.T // cfg.C)
    seq = pl.BlockSpec((HB, cfg.C, N), lambda h, j: (h, j, 0))
    mat = pl.BlockSpec((HB, 1, N, N), lambda h, j: (h, j, 0, 0))
    return grid, seq, mat


def _scan_local_fwd(cfg, seqs):
    T, RH, N = cfg.T, cfg.RH, RWKV_HEAD_DIM
    grid, seq, mat = _scan_local_specs(cfg)

    def body(r_ref, lw_ref, k_ref, v_ref, a_ref, b_ref, q_ref, yl_ref, a_out, sl_ref):
        Q, Yloc, A, Sloc = _chunk_local(*[_split_heads(ref, cfg.hb) for ref in (r_ref, lw_ref, k_ref, v_ref, a_ref, b_ref)])
        q_ref[...] = Q
        yl_ref[...] = Yloc
        a_out[:, 0] = A
        sl_ref[:, 0] = Sloc

    tok = pl.BlockSpec((cfg.C, cfg.hb * N), lambda h, j: (j, h))
    sq = jax.ShapeDtypeStruct((RH, T, N), F32)
    mt = jax.ShapeDtypeStruct((RH, T // cfg.C, N, N), F32)
    return _pcall(body, name="rwkv_scan_local_fwd", grid=grid, in_specs=[tok] * 6, out_specs=[seq, seq, mat, mat],
                  out_shape=[sq, sq, mt, mt], compiler_params=_cparams(("parallel", "parallel")))(*seqs)


def _scan_local_bwd(cfg, toks, dq, dy, da, dsl, extra):
    T, RW, N = cfg.T, cfg.RW, RWKV_HEAD_DIM
    grid, seq, mat = _scan_local_specs(cfg)

    def body(r_ref, lw_ref, k_ref, v_ref, a_ref, b_ref, dq_ref, dy_ref, da_ref, dsl_ref, xr_ref, xk_ref, xv_ref,
             *outs):
        ins = [_split_heads(ref, cfg.hb) for ref in (r_ref, lw_ref, k_ref, v_ref, a_ref, b_ref)]
        _, vjp = jax.vjp(_chunk_local, *ins)
        d = vjp((dq_ref[...], _split_heads(dy_ref, cfg.hb), da_ref[:, 0], dsl_ref[:, 0]))
        add = {0: xr_ref, 2: xk_ref, 3: xv_ref}
        for j in range(6):
            dj = _merge_heads(d[j])
            outs[j][...] = dj + add[j][...] if j in add else dj

    tok = pl.BlockSpec((cfg.C, cfg.hb * N), lambda h, j: (j, h))
    return _pcall(body, name="rwkv_scan_local_bwd", grid=grid, in_specs=[tok] * 6 + [seq, tok, mat, mat] + [tok] * 3,
                  out_specs=[tok] * 6, out_shape=[jax.ShapeDtypeStruct((T, RW), F32)] * 6,
                  compiler_params=_cparams(("parallel", "parallel")))(*toks, dq, dy, da, dsl, *extra)


def _scan_carry_specs(cfg, rev):
    N, RH, C, nc = RWKV_HEAD_DIM, cfg.RH, cfg.C, cfg.T // cfg.C
    at = (lambda j: nc - 1 - j) if rev else (lambda j: j)
    seq = pl.BlockSpec((RH, C, N), lambda j: (0, at(j), 0))
    mat = pl.BlockSpec((RH, 1, N, N), lambda j: (0, at(j), 0, 0))
    return nc, seq, mat


def _scan_carry_fwd(cfg, q, yloc, a, sloc):
    T, RH, N = cfg.T, cfg.RH, RWKV_HEAD_DIM
    nc, seq, mat = _scan_carry_specs(cfg, False)

    def body(q_ref, yl_ref, a_ref, sl_ref, y_ref, ck_ref, s_ref):
        @pl.when(pl.program_id(0) == 0)
        def _():
            s_ref[...] = jnp.zeros_like(s_ref)

        S = s_ref[...]
        ck_ref[:, 0] = S
        y_ref[...] = _merge_heads(_bdot(q_ref[...], S, 2, 2) + yl_ref[...])
        s_ref[...] = _bdot(S, a_ref[:, 0], 2, 1) + sl_ref[:, 0]

    tok = pl.BlockSpec((cfg.C, cfg.RW), lambda j: (j, 0))
    return _pcall(body, name="rwkv_scan_carry_fwd", grid=(nc,), in_specs=[seq, seq, mat, mat], out_specs=[tok, mat],
                  out_shape=[jax.ShapeDtypeStruct((T, cfg.RW), F32), jax.ShapeDtypeStruct((RH, nc, N, N), F32)],
                  scratch_shapes=[pltpu.VMEM((RH, N, N), F32)],
                  compiler_params=_cparams(("arbitrary",)))(q, yloc, a, sloc)


def _scan_carry_bwd(cfg, q, a, ckpt, dy):
    T, RH, N = cfg.T, cfg.RH, RWKV_HEAD_DIM
    nc, seq, mat = _scan_carry_specs(cfg, True)

    def body(q_ref, a_ref, ck_ref, dy_ref, dq_ref, da_ref, dsl_ref, ds_ref):
        @pl.when(pl.program_id(0) == 0)
        def _():
            ds_ref[...] = jnp.zeros_like(ds_ref)

        S, dS, dY = ck_ref[:, 0], ds_ref[...], _split_heads(dy_ref, RH)
        dq_ref[...] = _bdot(dY, S, 2, 1)
        da_ref[:, 0] = _bdot(S, dS, 1, 1)
        dsl_ref[:, 0] = dS
        ds_ref[...] = _bdot(dS, a_ref[:, 0], 2, 2) + _bdot(dY, q_ref[...], 1, 1)

    mt = jax.ShapeDtypeStruct((RH, nc, N, N), F32)
    tok = pl.BlockSpec((cfg.C, cfg.RW), lambda j: (nc - 1 - j, 0))
    return _pcall(body, name="rwkv_scan_carry_bwd", grid=(nc,), in_specs=[seq, mat, mat, tok],
                  out_specs=[seq, mat, mat], out_shape=[jax.ShapeDtypeStruct((RH, T, N), F32), mt, mt],
                  scratch_shapes=[pltpu.VMEM((RH, N, N), F32)],
                  compiler_params=_cparams(("arbitrary",)))(q, a, ckpt, dy)


def _post_fn(y, r, kp, v, zb, ln_w, ln_b, rk, ind, ind_t):
    n = float(RWKV_HEAD_DIM)
    mu = _xdot(_xdot(y, ind, ind_t) / n, ind_t, ind)
    yc = y - mu
    var = _xdot(yc * yc, ind, ind_t) / n
    rstd = _xdot(lax.rsqrt(var + GN_EPS), ind_t, ind)
    yn = yc * rstd * ln_w + ln_b
    bonus = _xdot(_xdot(r * kp * rk, ind, ind_t), ind_t, ind) * v
    return (yn + bonus) * _silu(zb)


def _rwkv_post_fwd(cfg, y, r, kp, v, zb, ln_w, ln_b, rk):
    T, RW, tr = cfg.T, cfg.RW, cfg.tr
    ind, ind_t, _ = _head_indicators(cfg)

    def body(y_ref, r_ref, k_ref, v_ref, z_ref, lw_ref, lb_ref, rk_ref, ind_ref, indt_ref, ob_ref):
        ob_ref[...] = _post_fn(y_ref[...], r_ref[...], k_ref[...], v_ref[...], z_ref[...], lw_ref[...], lb_ref[...],
                               rk_ref[...], ind_ref[...], indt_ref[...]).astype(BF16)

    consts = [ln_w, ln_b, rk, ind, ind_t]
    return _pcall(body, name="rwkv_post_fwd", grid=(T // tr,),
                  in_specs=[_tile(tr, RW)] * 5 + [_const(c.shape) for c in consts],
                  out_specs=_tile(tr, RW), out_shape=jax.ShapeDtypeStruct((T, RW), BF16),
                  compiler_params=_cparams(("parallel",)))(y, r, kp, v, zb, *consts)


def _rwkv_post_bwd(cfg, y, r, kp, v, zb, ln_w, ln_b, rk, dob):
    T, RW = cfg.T, cfg.RW
    tr = min(128, T)
    ind, ind_t, _ = _head_indicators(cfg)

    def body(y_ref, r_ref, k_ref, v_ref, z_ref, lw_ref, lb_ref, rk_ref, ind_ref, indt_ref, dob_ref,
             dy_ref, dr_ref, dk_ref, dv_ref, dz_ref, dlw_ref, dlb_ref, drk_ref):
        fn = functools.partial(_post_fn, ind=ind_ref[...], ind_t=indt_ref[...])
        _, vjp = jax.vjp(fn, y_ref[...], r_ref[...], k_ref[...], v_ref[...], z_ref[...], lw_ref[...], lb_ref[...],
                         rk_ref[...])
        d = vjp(dob_ref[...])
        for ref, val in zip((dy_ref, dr_ref, dk_ref, dv_ref, dz_ref), d[:5]):
            ref[...] = val
        i = pl.program_id(0)
        for ref, val in zip((dlw_ref, dlb_ref, drk_ref), d[5:8]):
            _acc_store(i, ref, val)

    consts = [ln_w, ln_b, rk, ind, ind_t]
    vec = jax.ShapeDtypeStruct((1, RW), F32)
    return _pcall(body, name="rwkv_post_bwd", grid=(T // tr,),
                  in_specs=[_tile(tr, RW)] * 5 + [_const(c.shape) for c in consts] + [_tile(tr, RW)],
                  out_specs=[_tile(tr, RW)] * 5 + [_const((1, RW))] * 3,
                  out_shape=[jax.ShapeDtypeStruct((T, RW), F32)] * 5 + [vec] * 3,
                  compiler_params=_cparams(("arbitrary",)))(y, r, kp, v, zb, *consts, dob)


def _adamw_math(w, g, m, v):
    m = ADAM_B1 * m + (1.0 - ADAM_B1) * g
    v = ADAM_B2 * v + (1.0 - ADAM_B2) * (g * g)
    m_hat = m / (1.0 - ADAM_B1 ** ADAM_STEP)
    v_hat = v / (1.0 - ADAM_B2 ** ADAM_STEP)
    delta = -ADAM_LR * (m_hat / (jnp.sqrt(v_hat) + ADAM_EPS) + ADAM_WD * w)
    return delta, m, v


def _adamw(name, w, g, m, v, copy_grad=False):
    R, Cc = w.shape
    Rp = -(-R // 8) * 8
    tr = Rp
    for nb in range(1, Rp // 8 + 1):
        if (Rp // 8) % nb == 0 and (Rp // nb) * Cc * 4 <= 2 * 1024 * 1024:
            tr = Rp // nb
            break

    def body(w_ref, g_ref, m_ref, v_ref, d_ref, nm_ref, nv_ref, *g_out):
        g_v = g_ref[...]
        d, nm, nv = _adamw_math(w_ref[...], g_v, m_ref[...], v_ref[...])
        d_ref[...] = d
        nm_ref[...] = nm
        nv_ref[...] = nv
        if copy_grad:
            g_out[0][...] = g_v

    spec = _tile(tr, Cc)
    n_out = 4 if copy_grad else 3
    return _pcall(body, name=name, grid=(Rp // tr,), in_specs=[spec] * 4, out_specs=[spec] * n_out,
                  out_shape=[jax.ShapeDtypeStruct((R, Cc), F32)] * n_out,
                  compiler_params=_cparams(("parallel",)))(w, g, m, v)


def _row_tile(R, Cc, itemsize, budget=2 * 1024 * 1024):
    for nb in range(1, R // 16 + 1):
        if R % nb == 0 and (R // nb) % 16 == 0 and (R // nb) * Cc * itemsize <= budget:
            return R // nb
    return R


def _add_halves(name, gs, r1, c_idx):
    _, R, Cc = gs.shape
    half = R // 2
    tr = _row_tile(half, Cc, 4)
    nb = half // tr

    def body(c_ref, g_ref, r_ref, o_ref):
        o_ref[...] = (g_ref[...].astype(F32) + r_ref[...].astype(F32)).astype(BF16)

    grid_spec = pltpu.PrefetchScalarGridSpec(
        num_scalar_prefetch=1, grid=(N_CHIPS, nb),
        in_specs=[pl.BlockSpec((1, tr, Cc), lambda s, i, c: (s, c[0] * nb + i, 0)),
                  pl.BlockSpec((1, tr, Cc), lambda s, i, c: (s, i, 0))],
        out_specs=pl.BlockSpec((1, tr, Cc), lambda s, i, c: (s, i, 0)))
    return _pcall(body, name=name, grid_spec=grid_spec, out_shape=jax.ShapeDtypeStruct((N_CHIPS, half, Cc), BF16),
                  compiler_params=_cparams(("parallel", "parallel")))(c_idx, gs, r1)


def _sum_slots(name, r2):
    S, R, Cc = r2.shape
    tr = _row_tile(R, Cc, 4 * S // 2 if r2.dtype == BF16 else 4 * S)

    def body(r_ref, o_ref):
        acc = r_ref[0].astype(F32)
        for s in range(1, S):
            acc = acc + r_ref[s].astype(F32)
        o_ref[...] = acc

    return _pcall(body, name=name, grid=(R // tr,), in_specs=[pl.BlockSpec((S, tr, Cc), lambda i: (0, i, 0))],
                  out_specs=_tile(tr, Cc), out_shape=jax.ShapeDtypeStruct((R, Cc), F32),
                  compiler_params=_cparams(("parallel",)))(r2)


def _sum_chips(name, recv, own, place):
    S, H, Cc = recv.shape
    tr = _row_tile(H, Cc, 4, 1024 * 1024)
    nb = H // tr

    def body(p_ref, r_ref, own_ref, o_ref):
        s = pl.program_id(1)
        me = p_ref[0]

        @pl.when(s == 0)
        def _():
            o_ref[...] = jnp.zeros_like(o_ref)

        @pl.when(s == me)
        def _():
            o_ref[...] += own_ref[0].astype(F32)

        @pl.when(s != me)
        def _():
            o_ref[...] += r_ref[0].astype(F32)

    grid_spec = pltpu.PrefetchScalarGridSpec(
        num_scalar_prefetch=1, grid=(nb, S),
        in_specs=[pl.BlockSpec((1, tr, Cc), lambda i, s, p: (jnp.where(s == p[0], (s + 1) % S, s), i, 0)),
                  pl.BlockSpec((1, tr, Cc), lambda i, s, p: (p[0], i, 0))],
        out_specs=pl.BlockSpec((tr, Cc), lambda i, s, p: (p[1] * nb + i, 0)))
    return _pcall(body, name=name, grid_spec=grid_spec, out_shape=jax.ShapeDtypeStruct((2 * H, Cc), F32),
                  compiler_params=_cparams(("parallel", "arbitrary")))(place, recv, own)


def _cast_bf16(name, w):
    R, Cc = w.shape
    tr = _row_tile(R, Cc, 4)

    def body(w_ref, o_ref):
        o_ref[...] = w_ref[...].astype(BF16)

    return _pcall(body, name=name, grid=(R // tr,), in_specs=[_tile(tr, Cc)], out_specs=_tile(tr, Cc),
                  out_shape=jax.ShapeDtypeStruct((R, Cc), BF16), compiler_params=_cparams(("parallel",)))(w)


_ANY = pl.BlockSpec(memory_space=pl.ANY)


def _place():
    x, y, c = lax.axis_index("x"), lax.axis_index("y"), lax.axis_index("c")
    others = [(1 - x, y), (x, 1 - y), (1 - x, 1 - y)]
    return x, y, c, others


def _gather_weights(shards):
    n = len(shards)
    halves = [s.shape[0] // 2 for s in shards]

    def body(*refs):
        ins, outs = refs[:n], refs[n:2 * n]
        send_sems, recv_sems = refs[2 * n:]
        x, y, c, _ = _place()
        me = 2 * x + y
        n1 = (x ^ (1 - c), y ^ c)
        n2 = (x ^ c, y ^ (1 - c))
        s1, s2, sd = 2 * n1[0] + n1[1], 2 * n2[0] + n2[1], 2 * (1 - x) + (1 - y)

        def rows(k, chip, hc):
            return outs[k].at[chip, pl.ds(hc * halves[k], halves[k]), :]

        def remote(k, j, src, dst, to):
            return pltpu.make_async_remote_copy(src_ref=src, dst_ref=dst, send_sem=send_sems.at[6 * k + j],
                                                recv_sem=recv_sems.at[6 * k + j], device_id=to, device_id_type=MESH)

        def arrived(k, j, land):
            remote(k, j, land, land, (x, y, c)).wait_recv()

        sent = []

        def send(k, j, src, dst, to):
            cp = remote(k, j, src, dst, to)
            cp.start()
            sent.append(cp)

        sib = (x, y, 1 - c)
        for k in range(n):
            mine = ins[k].at[pl.ds(c * halves[k], halves[k]), :]
            send(k, 0, mine, rows(k, me, c), (*n1, c))
            send(k, 1, mine, rows(k, me, c), (*n2, c))
        for k in range(n):
            from_n1, from_n2, from_d = rows(k, s1, c), rows(k, s2, c), rows(k, sd, c)
            arrived(k, 0, from_n1)
            send(k, 2, from_n1, from_n1, (*n2, c))
            send(k, 3, from_n1, from_n1, sib)
            arrived(k, 1, from_n2)
            send(k, 4, from_n2, from_n2, sib)
            arrived(k, 2, from_d)
            send(k, 5, from_d, from_d, sib)
        for k in range(n):
            arrived(k, 3, rows(k, s2, 1 - c))
            arrived(k, 4, rows(k, s1, 1 - c))
            arrived(k, 5, rows(k, sd, 1 - c))
        for cp in sent:
            cp.wait_send()

    return _pcall(
        body, name="gather_weights", in_specs=[_ANY] * n, out_specs=[_ANY] * n,
        out_shape=[jax.ShapeDtypeStruct((N_CHIPS,) + s.shape, s.dtype) for s in shards],
        scratch_shapes=[pltpu.SemaphoreType.DMA((6 * n,)), pltpu.SemaphoreType.DMA((6 * n,))],
    )(*shards)


def _exchange_halves(grads):
    n = len(grads)
    halves = [g.shape[1] // 2 for g in grads]

    def body(*refs):
        ins, outs = refs[:n], refs[n:2 * n]
        send_sems, recv_sems = refs[2 * n:]
        x, y, c, _ = _place()
        cps = []
        for k in range(n):
            src = ins[k].at[:, pl.ds((1 - c) * halves[k], halves[k]), :]
            cp = pltpu.make_async_remote_copy(src_ref=src, dst_ref=outs[k], send_sem=send_sems.at[k],
                                              recv_sem=recv_sems.at[k], device_id=(x, y, 1 - c), device_id_type=MESH)
            cp.start()
            cps.append(cp)
        for cp in cps:
            cp.wait()

    return _pcall(
        body, name="exchange_halves", in_specs=[_ANY] * n, out_specs=[_ANY] * n,
        out_shape=[jax.ShapeDtypeStruct((N_CHIPS, h) + g.shape[2:], g.dtype) for g, h in zip(grads, halves)],
        scratch_shapes=[pltpu.SemaphoreType.DMA((n,)), pltpu.SemaphoreType.DMA((n,))],
    )(*grads)


def _scatter_to_owners(chip_sums):
    n = len(chip_sums)

    def sends(ins, outs, sems):
        x, y, c, others = _place()
        me = 2 * x + y
        return [pltpu.make_async_remote_copy(
            src_ref=ins[k].at[2 * px + py], dst_ref=outs[k].at[me], send_sem=sems[0].at[3 * k + j],
            recv_sem=sems[1].at[3 * k + j], device_id=(px, py, c), device_id_type=MESH)
            for k in range(n) for j, (px, py) in enumerate(others)]

    def start(ins, outs, sems):
        for cp in sends(ins, outs, sems):
            cp.start()

    def finish(ins, outs, sems):
        x, y, c, others = _place()
        for k in range(n):
            for j, (px, py) in enumerate(others):
                land = outs[k].at[2 * px + py]
                pltpu.make_async_remote_copy(src_ref=land, dst_ref=land, send_sem=sems[0].at[3 * k + j],
                                             recv_sem=sems[1].at[3 * k + j], device_id=(x, y, c),
                                             device_id_type=MESH).wait_recv()
        for cp in sends(ins, outs, sems):
            cp.wait_send()

    out_shapes = [jax.ShapeDtypeStruct(g.shape, g.dtype) for g in chip_sums]
    scratch = [pltpu.SemaphoreType.DMA((3 * n,)), pltpu.SemaphoreType.DMA((3 * n,))]
    return list(chip_sums), out_shapes, scratch, start, finish


def _join_halves(fulls, small):
    n = len(fulls)
    hs = [f.shape[0] // 2 for f in fulls]
    rel = [(dx, dy, dc) for dx in (0, 1) for dy in (0, 1) for dc in (0, 1)][1:]

    def body(*refs):
        ins, small_in = refs[:n], refs[n]
        outs, small_out = refs[n + 1:2 * n + 1], refs[2 * n + 1]
        send_sems, recv_sems, ssend, srecv, local_sem = refs[2 * n + 2:]
        x, y, c, _ = _place()
        dev = 4 * x + 2 * y + c
        local = pltpu.make_async_copy(small_in, small_out.at[dev], local_sem)
        local.start()
        cps = []
        for k in range(n):
            mine = pl.ds(c * hs[k], hs[k])
            cp = pltpu.make_async_remote_copy(src_ref=ins[k].at[mine, :], dst_ref=outs[k].at[mine, :],
                                              send_sem=send_sems.at[k], recv_sem=recv_sems.at[k],
                                              device_id=(x, y, 1 - c), device_id_type=MESH)
            cp.start()
            cps.append(cp)
        for r, (dx, dy, dc) in enumerate(rel):
            cp = pltpu.make_async_remote_copy(src_ref=small_in, dst_ref=small_out.at[dev], send_sem=ssend.at[r],
                                              recv_sem=srecv.at[r], device_id=(x ^ dx, y ^ dy, c ^ dc),
                                              device_id_type=MESH)
            cp.start()
            cps.append(cp)
        for k in range(n):
            land = outs[k].at[pl.ds((1 - c) * hs[k], hs[k]), :]
            pltpu.make_async_remote_copy(src_ref=land, dst_ref=land, send_sem=send_sems.at[k],
                                         recv_sem=recv_sems.at[k], device_id=(x, y, c), device_id_type=MESH).wait_recv()
        for r, (dx, dy, dc) in enumerate(rel):
            land = small_out.at[4 * (x ^ dx) + 2 * (y ^ dy) + (c ^ dc)]
            pltpu.make_async_remote_copy(src_ref=land, dst_ref=land, send_sem=ssend.at[r], recv_sem=srecv.at[r],
                                         device_id=(x, y, c), device_id_type=MESH).wait_recv()
        for cp in cps:
            cp.wait_send()
        local.wait()

    return _pcall(
        body, name="join_halves", in_specs=[_ANY] * (n + 1), out_specs=[_ANY] * (n + 1),
        out_shape=[jax.ShapeDtypeStruct(f.shape, f.dtype) for f in fulls]
        + [jax.ShapeDtypeStruct((N_DEV,) + small.shape, small.dtype)],
        input_output_aliases={k: k for k in range(n)},
        scratch_shapes=[pltpu.SemaphoreType.DMA((n,)), pltpu.SemaphoreType.DMA((n,)), pltpu.SemaphoreType.DMA((7,)),
                        pltpu.SemaphoreType.DMA((7,)), pltpu.SemaphoreType.DMA],
    )(*fulls, small)


def _local_step(cfg, x2, target, norm_gain, w_my, fb, mu_g, w0, w2, a0, a2, k_k, k_a, r_k, ln_w, ln_b, wpf, wpr, wout,
                fng, exchange=None):
    T, D, FW, FH, RW, RH, LP, lora = cfg.T, cfg.D, cfg.FW, cfg.FH, cfg.RW, cfg.RH, cfg.LP, cfg.lora
    fb_p = jnp.pad(fb, ((0, 0), (0, LANES - FH)))
    mu = _rwkv_vec_to_my(cfg, mu_g)
    w2p = jnp.pad(w2, ((0, LP - lora), (0, 0)))
    a2p = jnp.pad(a2, ((0, LP - lora), (0, 0)))
    rk = r_k.reshape(1, RW)
    tm = min(1024, T)

    h = _rms_fwd(cfg, x2, norm_gain)
    u = _mm("in_proj", h, w_my, "nn", F32, tm, cfg.tn, 2048)
    c_cols = _fox_prep(cfg, u, fb_p)
    c_rows = c_cols[:, :FH].T.reshape(FH, 1, T)
    o, lse = _attn_fwd(cfg, u, c_rows)
    oa = _gate_a_fwd(cfg, o, u)
    prep = _rwkv_prep_fwd(cfg, u, mu, w0, w2p, a0, a2p, k_k, k_a)
    r, lw, kp, v, an, b, zb = prep
    toks = [r, lw, kp, v, an, b]
    q_s, yloc, a_m, sloc = _scan_local_fwd(cfg, toks)
    y, ckpt = _scan_carry_fwd(cfg, q_s, yloc, a_m, sloc)
    ob = _rwkv_post_fwd(cfg, y, r, kp, v, zb, ln_w, ln_b, rk)
    pa = _mm("proj_fox", oa, wpf, "nn", F32, tm, 1024, 2048)
    pb = _mm("proj_rwkv", ob, wpr, "nn", F32, tm, 1024, 2048)
    m = _merge_fwd(cfg, pa, pb, u)
    mo = _mm("out_proj", m, wout, "nn", F32, tm, 1024, 2048)
    loss8, dres, dres16, d_fng = _final(cfg, x2, mo, fng.reshape(1, D), target)

    dm = _mm("out_proj_dx", dres16, wout, "nt", F32, tm, 1024, 2048)
    d_wout = _mm("out_proj_dw", m, dres16, "tn", BF16, 1024, 1024, 2048)
    dpa, dpb, du = _merge_bwd(cfg, pa, pb, u, dm)
    doa = _mm("proj_fox_dx", dpa, wpf, "nt", F32, tm, 1024, 2048)
    d_wpf = _mm("proj_fox_dw", oa, dpa, "tn", BF16, 1024, 1024, 2048)
    dob = _mm("proj_rwkv_dx", dpb, wpr, "nt", F32, tm, 1024, 2048)
    d_wpr = _mm("proj_rwkv_dw", ob, dpb, "tn", BF16, 1024, 1024, 2048)

    do, du = _gate_a_bwd(cfg, o, u, doa, du)
    du, dcol = _attn_bwd(cfg, u, c_rows, lse, do, du)
    dc = jnp.pad(-dcol.reshape(FH, T).T, ((0, 0), (0, LANES - FH)))
    df, d_fb = _fox_prep_bwd(cfg, u, fb_p, dc)

    dy, dr_p, dk_p, dv_p, dzb, d_lnw, d_lnb, d_rk = _rwkv_post_bwd(cfg, y, r, kp, v, zb, ln_w, ln_b, rk, dob)
    dq_s, da_m, dsl = _scan_carry_bwd(cfg, q_s, a_m, ckpt, dy)
    cots = _scan_local_bwd(cfg, toks, dq_s, dy, da_m, dsl, [dr_p, dk_p, dv_p])
    dus, d_mu, d_w0, d_w2p, d_a0, d_a2p, d_kk, d_ka = _rwkv_prep_bwd(cfg, u, mu, w0, w2p, a0, a2p, k_k, k_a, cots, dzb)
    du = _shift_bwd(cfg, dus, mu, df, du)
    d_wmy = _mm("in_proj_dw", h, du, "tn", BF16, 1024, cfg.tn, 2048)
    big = dict(w_in=d_wmy, rwkv_w2=d_w2p[:lora], rwkv_a2=d_a2p[:lora], w_proj_fox=d_wpf, w_proj_rwkv=d_wpr,
               w_out=d_wout)
    tkx = 2 * cfg.tn if cfg.ncol % (2 * cfg.tn) == 0 else cfg.tn
    res = _mm("in_proj_dx", du, w_my, "nt", F32, tm, 1024, tkx, comm=exchange(big) if exchange else None)
    dh, received = (res[0], res[1:]) if exchange else (res, None)
    gx, d_ng = _rms_bwd(cfg, x2, norm_gain, dh, dres)

    small = dict(norm_gain=d_ng, fox_forget_bias=d_fb[:, :FH], rwkv_shift_mix=_rwkv_vec_from_my(cfg, d_mu),
                 rwkv_w0=d_w0, rwkv_a0=d_a0, rwkv_k_k=d_kk, rwkv_k_a=d_ka, rwkv_r_k=d_rk, rwkv_ln_w=d_lnw,
                 rwkv_ln_b=d_lnb, final_norm_gain=d_fng)
    return loss8[0, 0], gx, small, big, received


_SMALL = ["norm_gain", "fox_forget_bias", "rwkv_shift_mix", "rwkv_w0", "rwkv_a0", "rwkv_k_k", "rwkv_k_a", "rwkv_r_k",
          "rwkv_ln_w", "rwkv_ln_b", "final_norm_gain"]
_WEIGHTS = ["norm_gain", "w_in", "fox_forget_bias", "rwkv_shift_mix", "rwkv_w0", "rwkv_w2", "rwkv_a0", "rwkv_a2",
            "rwkv_k_k", "rwkv_k_a", "rwkv_r_k", "rwkv_ln_w", "rwkv_ln_b", "w_proj_fox", "w_proj_rwkv", "w_out",
            "final_norm_gain"]


def _pack_small(arrs):
    parts = []
    for a in arrs:
        f = a.reshape(-1)
        parts.append(jnp.pad(f, (0, (-f.shape[0]) % LANES)))
    flat = jnp.concatenate(parts)
    rows = flat.shape[0] // LANES
    flat = jnp.pad(flat, (0, ((-rows) % 8) * LANES))
    return flat.reshape(-1, LANES)


def _unpack_small(packed, shapes):
    flat = packed.reshape(-1)
    out, pos = [], 0
    for s in shapes:
        n = int(np.prod(s))
        out.append(flat[pos:pos + n].reshape(s))
        pos += n + ((-n) % LANES)
    return out


def _shard_major(a, axis):
    parts = jnp.split(a, N_CHIPS, axis=axis)
    return jnp.stack(parts, axis=0)


def kernel(x, norm_gain, w_in, fox_forget_bias, rwkv_shift_mix, rwkv_w0, rwkv_w2, rwkv_a0, rwkv_a2, rwkv_k_k, rwkv_k_a, rwkv_r_k, rwkv_ln_w, rwkv_ln_b, w_proj_fox, w_proj_rwkv, w_out, final_norm_gain, loss_target, m_norm_gain, m_w_in, m_fox_forget_bias, m_rwkv_shift_mix, m_rwkv_w0, m_rwkv_w2, m_rwkv_a0, m_rwkv_a2, m_rwkv_k_k, m_rwkv_k_a, m_rwkv_r_k, m_rwkv_ln_w, m_rwkv_ln_b, m_w_proj_fox, m_w_proj_rwkv, m_w_out, m_final_norm_gain, v_norm_gain, v_w_in, v_fox_forget_bias, v_rwkv_shift_mix, v_rwkv_w0, v_rwkv_w2, v_rwkv_a0, v_rwkv_a2, v_rwkv_k_k, v_rwkv_k_a, v_rwkv_r_k, v_rwkv_ln_w, v_rwkv_ln_b, v_w_proj_fox, v_w_proj_rwkv, v_w_out, v_final_norm_gain):
    args = dict(locals())
    T, D = x.shape[1], x.shape[2]
    lora = rwkv_w2.shape[1]
    cfg = _Cfg(T, D, lora)
    RW = cfg.RW
    c_idx = lax.axis_index("c").astype(jnp.int32).reshape(1)
    me_chip = (2 * lax.axis_index("x") + lax.axis_index("y")).astype(jnp.int32)
    place = jnp.concatenate([me_chip.reshape(1), c_idx])

    w_in_s = w_in[0].astype(BF16)
    lora_s = jnp.concatenate([rwkv_w2[0], rwkv_a2[0]], axis=0)
    mine = [w_in_s, _cast_bf16("cast_w_proj_fox", w_proj_fox[0]), _cast_bf16("cast_w_proj_rwkv", w_proj_rwkv[0]),
            _cast_bf16("cast_w_out", w_out[0]), lora_s]
    gathered = _gather_weights(mine)
    g_in, g_wpf, g_wpr, g_out, g_lora = [lax.dynamic_update_slice(g, own[None], (me_chip, 0, 0))
                                         for g, own in zip(gathered, mine)]
    w_my = _shards_to_my_layout(cfg, g_in)
    wpf = g_wpf.transpose(1, 0, 2).reshape(RW, D)
    wpr = g_wpr.transpose(1, 0, 2).reshape(RW, D)
    wout = g_out.reshape(D, D)
    lo = g_lora.transpose(1, 0, 2).reshape(2 * lora, RW)

    names = ["w_in", "w_proj_fox", "w_proj_rwkv", "w_out", "lora"]
    chip_sums = []

    def exchange(big):
        gs_in = _my_layout_to_shards(cfg, big["w_in"])
        gs_lora = _shard_major(jnp.concatenate([big["rwkv_w2"], big["rwkv_a2"]], axis=0).astype(BF16), 1)
        gs = [gs_in, _shard_major(big["w_proj_fox"], 1), _shard_major(big["w_proj_rwkv"], 1),
              _shard_major(big["w_out"], 0), gs_lora]
        recv1 = _exchange_halves(gs)
        chip_sums.extend(_add_halves("add_halves_" + nm, g, r, c_idx) for nm, g, r in zip(names, gs, recv1))
        return _scatter_to_owners(chip_sums)

    loss_dev, gx, small, _, recv2 = _local_step(
        cfg, x[0], loss_target[0], norm_gain, w_my, fox_forget_bias, rwkv_shift_mix, rwkv_w0, lo[:lora], rwkv_a0,
        lo[lora:], rwkv_k_k, rwkv_k_a, rwkv_r_k, rwkv_ln_w, rwkv_ln_b, wpf, wpr, wout, final_norm_gain, exchange)
    loss = lax.psum(loss_dev, ("x", "y", "c"))

    small_shapes = [args[nm].shape for nm in _SMALL]
    packed = _pack_small([small[nm] for nm in _SMALL])
    reduced = [_sum_chips("sum_chips_" + nm, r, own, place) for nm, r, own in zip(names, recv2, chip_sums)]
    g_in_f, g_wpf_f, g_wpr_f, g_out_f, g_lora_f, small_all = _join_halves(reduced, packed)
    g_small = _sum_slots("sum_small", small_all)

    grads = dict(zip(_SMALL, _unpack_small(g_small, small_shapes)))
    grads["w_in"] = g_in_f[None]
    grads["w_proj_fox"] = g_wpf_f[None]
    grads["w_proj_rwkv"] = g_wpr_f[None]
    grads["w_out"] = g_out_f[None]
    grads["rwkv_w2"] = g_lora_f[None, :lora]
    grads["rwkv_a2"] = g_lora_f[None, lora:]

    delta, new_m, new_v = {}, {}, {}
    w_small = _pack_small([args[nm] for nm in _SMALL])
    m_small = _pack_small([args["m_" + nm] for nm in _SMALL])
    v_small = _pack_small([args["v_" + nm] for nm in _SMALL])
    d_s, m_s, v_s = _adamw("adamw_small", w_small, g_small, m_small, v_small)
    for tgt, pk in ((delta, d_s), (new_m, m_s), (new_v, v_s)):
        tgt.update(zip(_SMALL, _unpack_small(pk, small_shapes)))
    for nm in ("w_in", "w_proj_fox", "w_proj_rwkv", "w_out", "rwkv_w2", "rwkv_a2"):
        shp = args[nm].shape
        two_d = (shp[1], shp[2])
        d_b, m_b, v_b = _adamw("adamw_" + nm, args[nm].reshape(two_d), grads[nm].reshape(two_d),
                               args["m_" + nm].reshape(two_d), args["v_" + nm].reshape(two_d))
        delta[nm], new_m[nm], new_v[nm] = d_b.reshape(shp), m_b.reshape(shp), v_b.reshape(shp)

    return (loss, gx[None], *[grads[n] for n in _WEIGHTS], *[delta[n] for n in _WEIGHTS],
            *[new_m[n] for n in _WEIGHTS], *[new_v[n] for n in _WEIGHTS])
```

```python
import functools

import numpy as np
import jax
import jax.numpy as jnp
from jax import lax
from jax.experimental import pallas as pl
from jax.experimental.pallas import tpu as pltpu

F32 = jnp.float32
BF16 = jnp.bfloat16
HI = lax.Precision.HIGHEST
MESH = pl.DeviceIdType.MESH

FOX_HEAD_DIM = 128
RWKV_HEAD_DIM = 64
RMS_EPS = 1e-6
GN_EPS = 64e-5
L2_EPS = 1e-12
ADAM_LR = 0.001
ADAM_B1 = 0.9
ADAM_B2 = 0.999
ADAM_EPS = 1e-08
ADAM_WD = 0.01
ADAM_STEP = 10

LANES = 128
VMEM_LIMIT = 56 * 1024 * 1024
SCAN_CHUNK = 64
SCAN_HEADS_PER_STEP = 8
SCAN_PASSES = (3, 1, 1)
N_CHIPS = 4
N_DEV = 8

_pcall = pl.pallas_call


def _cparams(sem=None):
    return pltpu.CompilerParams(dimension_semantics=sem, vmem_limit_bytes=VMEM_LIMIT)


def _softplus(x):
    return jnp.maximum(x, 0.0) + jnp.log(1.0 + jnp.exp(-jnp.abs(x)))


def _silu(z):
    return z * jax.nn.sigmoid(z)


def _rmsn(x, g):
    return x * lax.rsqrt(jnp.mean(x * x, axis=-1, keepdims=True) + RMS_EPS) * g


def _dot(a, b, dims="nn", precision=None):
    dn = {"nn": (((1,), (0,)), ((), ())), "nt": (((1,), (1,)), ((), ())), "tn": (((0,), (0,)), ((), ()))}[dims]
    return lax.dot_general(a, b, dn, precision=precision, preferred_element_type=F32)


def _split_bf16(x):
    hi = x.astype(BF16)
    return hi, (x - hi.astype(F32)).astype(BF16)


def _bdot_raw(a, b, ca, cb, passes):
    dn = (((ca,), (cb,)), ((0,), (0,)))
    mm = lambda p, q: lax.dot_general(p, q, dn, preferred_element_type=F32)
    if passes == 1:
        return mm(a.astype(BF16), b.astype(BF16))
    ah, al = _split_bf16(a)
    bh, bl = _split_bf16(b)
    return mm(ah, bh) + (mm(ah, bl) + mm(al, bh))


@functools.partial(jax.custom_vjp, nondiff_argnums=(2, 3, 4))
def _bdot_p(a, b, ca, cb, passes):
    return _bdot_raw(a, b, ca, cb, passes)


def _bdot_fwd(a, b, ca, cb, passes):
    return _bdot_raw(a, b, ca, cb, passes), (a, b)


def _bdot_bwd(ca, cb, passes, res, g):
    a, b = res
    if (ca, cb) == (2, 1):
        return _bdot_p(g, b, 2, 2, passes), _bdot_p(a, g, 1, 1, passes)
    if (ca, cb) == (2, 2):
        return _bdot_p(g, b, 2, 1, passes), _bdot_p(g, a, 1, 1, passes)
    assert (ca, cb) == (1, 1)
    return _bdot_p(b, g, 2, 2, passes), _bdot_p(a, g, 2, 1, passes)


_bdot_p.defvjp(_bdot_fwd, _bdot_bwd)


def _bdot(a, b, ca, cb, passes=3):
    return _bdot_p(a, b, ca, cb, passes)


def _dot3(a, b):
    return _bdot(a[None], b[None], 2, 1)[0]


@jax.custom_vjp
def _xdot(x, m, mt):
    hi, lo = _split_bf16(x)
    m16 = m.astype(BF16)
    return _dot(hi, m16) + _dot(lo, m16)


def _xdot_fwd(x, m, mt):
    return _xdot(x, m, mt), (m, mt)


def _xdot_bwd(res, g):
    m, mt = res
    return _xdot(g, mt, m), jnp.zeros_like(m), jnp.zeros_like(mt)


_xdot.defvjp(_xdot_fwd, _xdot_bwd)


class _Cfg:
    def __init__(self, T, D, lora):
        self.T, self.D, self.lora = T, D, lora
        self.FW = D // 2
        self.FH = self.FW // FOX_HEAD_DIM
        self.RW = D // 2
        self.RH = self.RW // RWKV_HEAD_DIM
        self.LP = -(-lora // LANES) * LANES
        self.o_fox = 0
        self.o_rwkv = 4 * self.FW
        self.o_gate = self.o_rwkv + 4 * self.RW
        self.o_f = self.o_gate + 2 * D
        self.o_wd = self.o_f + LANES
        self.o_ad = self.o_wd + self.LP
        end = self.o_ad + self.LP
        self.tn = 1280 if D >= 2048 else LANES
        self.ncol = -(-end // self.tn) * self.tn
        self.in_cols = 4 * self.FW + self.FH + 4 * self.RW + 2 * lora + 2 * D
        self.scp = -(-(self.in_cols // N_CHIPS) // LANES) * LANES
        self.rseg = 4 * self.RW + 2 * self.LP
        self.C = min(SCAN_CHUNK, T)
        self.tr = min(256, T)
        self.hb = min(SCAN_HEADS_PER_STEP, self.RH)

    def segments(self):
        FW, FH, RW, lo, D = self.FW, self.FH, self.RW, self.lora, self.D
        g_f = 4 * FW
        g_r = g_f + FH
        g_wd = g_r + 4 * RW
        g_ad = g_wd + lo
        g_g = g_ad + lo
        dh = FOX_HEAD_DIM
        qkv = [(j * FW + h * dh, dh, (3 * h + j) * dh) for h in range(FH) for j in range(3)]
        return qkv + [(3 * FW, FW, 3 * FW), (g_f, FH, self.o_f), (g_r, 4 * RW, self.o_rwkv), (g_wd, lo, self.o_wd),
                      (g_ad, lo, self.o_ad), (g_g, 2 * D, self.o_gate)]


def _to_my_layout(cfg, wg):
    R = wg.shape[0]
    segs = sorted(cfg.segments(), key=lambda s: s[2])
    parts, pos = [], 0
    for g0, w, m0 in segs:
        if m0 > pos:
            parts.append(jnp.zeros((R, m0 - pos), wg.dtype))
        parts.append(wg[:, g0:g0 + w])
        pos = m0 + w
    if cfg.ncol > pos:
        parts.append(jnp.zeros((R, cfg.ncol - pos), wg.dtype))
    return jnp.concatenate(parts, axis=1)


def _from_my_layout(cfg, wm):
    segs = sorted(cfg.segments(), key=lambda s: s[0])
    return jnp.concatenate([wm[:, m0:m0 + w] for g0, w, m0 in segs], axis=1)


def _shards_to_my_layout(cfg, g):
    R, sc = g.shape[1], g.shape[2]
    segs = sorted(cfg.segments(), key=lambda s: s[2])
    parts, pos = [], 0
    for g0, w, m0 in segs:
        if m0 > pos:
            parts.append(jnp.zeros((R, m0 - pos), g.dtype))
        for s in range(N_CHIPS):
            lo, hi = max(g0, s * sc), min(g0 + w, (s + 1) * sc)
            if lo < hi:
                parts.append(g[s, :, lo - s * sc:hi - s * sc])
        pos = m0 + w
    if cfg.ncol > pos:
        parts.append(jnp.zeros((R, cfg.ncol - pos), g.dtype))
    return jnp.concatenate(parts, axis=1)


def _my_layout_to_shards(cfg, wm):
    sc = cfg.in_cols // N_CHIPS
    segs = sorted(cfg.segments(), key=lambda s: s[0])
    shards = []
    for s in range(N_CHIPS):
        parts = []
        for g0, w, m0 in segs:
            lo, hi = max(g0, s * sc), min(g0 + w, (s + 1) * sc)
            if lo < hi:
                parts.append(wm[:, m0 + lo - g0:m0 + hi - g0])
        shards.append(jnp.concatenate(parts, axis=1))
    return jnp.stack(shards, axis=0)


def _rwkv_vec_to_my(cfg, v):
    RW4, lo, LP = 4 * cfg.RW, cfg.lora, cfg.LP
    z = jnp.zeros((1, LP - lo), v.dtype)
    return jnp.concatenate([v[:, :RW4], v[:, RW4:RW4 + lo], z, v[:, RW4 + lo:], z], axis=1)


def _rwkv_vec_from_my(cfg, v):
    RW4, lo, LP = 4 * cfg.RW, cfg.lora, cfg.LP
    return jnp.concatenate([v[:, :RW4], v[:, RW4:RW4 + lo], v[:, RW4 + LP:RW4 + LP + lo]], axis=1)


def _comm_at(comm, which, steps, cin, cout, scr):
    if not comm or len(comm) <= which:
        return
    lin, total = 0, 1
    for d, n in enumerate(steps):
        lin = lin * n + pl.program_id(d)
        total *= n
    pl.when(lin == {3: 0, 4: total - 1, 5: total // 2}[which])(lambda: comm[which](cin, cout, scr))


def _mm(name, a, b, dims, out_dtype, tm, tn, tk, comm=None):
    (M, K) = a.shape if dims != "tn" else a.shape[::-1]
    N = b.shape[0] if dims == "nt" else b.shape[1]
    tm, tn, tk = min(tm, M), min(tn, N), min(tk, K)
    assert M % tm == 0 and N % tn == 0 and K % tk == 0, (name, M, N, K, tm, tn, tk)
    nk = K // tk
    steps = (M // tm, N // tn, nk)
    c_in, c_out, c_scr = comm[:3] if comm else ([], [], [])
    if dims == "nn":
        a_spec = pl.BlockSpec((tm, tk), lambda i, j, k: (i, k))
        b_spec = pl.BlockSpec((tk, tn), lambda i, j, k: (k, j))
    elif dims == "nt":
        a_spec = pl.BlockSpec((tm, tk), lambda i, j, k: (i, k))
        b_spec = pl.BlockSpec((tn, tk), lambda i, j, k: (j, k))
    else:
        a_spec = pl.BlockSpec((tk, tm), lambda i, j, k: (k, i))
        b_spec = pl.BlockSpec((tk, tn), lambda i, j, k: (k, j))

    n_acc = 1 if nk > 1 else 0

    def body(a_ref, b_ref, *rest):
        cin, o_ref = rest[:len(c_in)], rest[len(c_in)]
        cout = rest[len(c_in) + 1:len(c_in) + 1 + len(c_out)]
        scr = rest[len(c_in) + 1 + len(c_out):]
        _comm_at(comm, 3, steps, cin, cout, scr[n_acc:])
        if nk == 1:
            o_ref[...] = _dot(a_ref[...], b_ref[...], dims).astype(o_ref.dtype)
        else:
            acc_ref, k = scr[0], pl.program_id(2)

            @pl.when(k == 0)
            def _():
                acc_ref[...] = jnp.zeros_like(acc_ref)

            acc_ref[...] += _dot(a_ref[...], b_ref[...], dims)

            @pl.when(k == nk - 1)
            def _():
                o_ref[...] = acc_ref[...].astype(o_ref.dtype)

        _comm_at(comm, 5, steps, cin, cout, scr[n_acc:])
        _comm_at(comm, 4, steps, cin, cout, scr[n_acc:])

    res = _pcall(
        body, name=name, grid=steps,
        in_specs=[a_spec, b_spec] + [_ANY] * len(c_in),
        out_specs=[pl.BlockSpec((tm, tn), lambda i, j, k: (i, j))] + [_ANY] * len(c_out),
        out_shape=[jax.ShapeDtypeStruct((M, N), out_dtype)] + list(c_out),
        scratch_shapes=([pltpu.VMEM((tm, tn), F32)] if nk > 1 else []) + list(c_scr),
        compiler_params=_cparams(("arbitrary",) * 3 if comm else ("parallel", "parallel", "arbitrary")),
    )(a, b, *c_in)
    return res if comm else res[0]


def _tile(tr, w, cb=0):
    return pl.BlockSpec((tr, w), lambda i: (i, cb))


def _const(shape):
    nd = len(shape)
    return pl.BlockSpec(shape, lambda i: (0,) * nd)


def _acc_store(i, ref, val):
    @pl.when(i == 0)
    def _():
        ref[...] = val

    @pl.when(i > 0)
    def _():
        ref[...] += val


def _rms_fwd(cfg, x2, g):
    T, D, tr = cfg.T, cfg.D, cfg.tr

    def body(x_ref, g_ref, h_ref):
        h_ref[...] = _rmsn(x_ref[...], g_ref[...]).astype(BF16)

    return _pcall(body, name="rms_fwd", grid=(T // tr,), in_specs=[_tile(tr, D), _const((1, D))],
                  out_specs=_tile(tr, D), out_shape=jax.ShapeDtypeStruct((T, D), BF16),
                  compiler_params=_cparams(("parallel",)))(x2, g)


def _rms_bwd(cfg, x2, g, dh, dres):
    T, D, tr = cfg.T, cfg.D, cfg.tr

    def body(x_ref, g_ref, dh_ref, dres_ref, gx_ref, dg_ref):
        _, vjp = jax.vjp(_rmsn, x_ref[...], g_ref[...])
        dx, dg = vjp(dh_ref[...])
        gx_ref[...] = dx + dres_ref[...]
        _acc_store(pl.program_id(0), dg_ref, dg)

    return _pcall(body, name="rms_bwd", grid=(T // tr,),
                  in_specs=[_tile(tr, D), _const((1, D)), _tile(tr, D), _tile(tr, D)],
                  out_specs=[_tile(tr, D), _const((1, D))],
                  out_shape=[jax.ShapeDtypeStruct((T, D), F32), jax.ShapeDtypeStruct((1, D), F32)],
                  compiler_params=_cparams(("arbitrary",)))(x2, g, dh, dres)


def _final(cfg, x2, mo, fg, target):
    T, D, tr = cfg.T, cfg.D, cfg.tr

    def loss_fn(hres, g, tgt):
        err = _rmsn(hres, g) - tgt
        return 0.5 * jnp.sum(jnp.mean(err * err, axis=-1, keepdims=True), axis=0, keepdims=True)

    def body(x_ref, mo_ref, g_ref, t_ref, loss_ref, dres_ref, dres16_ref, dg_ref):
        hres = x_ref[...] + mo_ref[...]
        loss, vjp = jax.vjp(functools.partial(loss_fn, tgt=t_ref[...]), hres, g_ref[...])
        dres, dg = vjp(jnp.ones((1, 1), F32))
        dres_ref[...] = dres
        dres16_ref[...] = dres.astype(BF16)
        i = pl.program_id(0)
        _acc_store(i, dg_ref, dg)
        _acc_store(i, loss_ref, jnp.broadcast_to(loss, (8, LANES)))

    return _pcall(body, name="final_loss", grid=(T // tr,),
                  in_specs=[_tile(tr, D), _tile(tr, D), _const((1, D)), _tile(tr, D)],
                  out_specs=[_const((8, LANES)), _tile(tr, D), _tile(tr, D), _const((1, D))],
                  out_shape=[jax.ShapeDtypeStruct((8, LANES), F32), jax.ShapeDtypeStruct((T, D), F32),
                             jax.ShapeDtypeStruct((T, D), BF16), jax.ShapeDtypeStruct((1, D), F32)],
                  compiler_params=_cparams(("arbitrary",)))(x2, mo, fg, target)


def _merge_fn(pa, pb, ga, gb):
    return jax.nn.sigmoid(ga) * pa + jax.nn.sigmoid(gb) * pb


def _merge_fwd(cfg, pa, pb, u):
    T, D, tr = cfg.T, cfg.D, cfg.tr
    cga, cgb = cfg.o_gate // D, cfg.o_gate // D + 1

    def body(pa_ref, pb_ref, ga_ref, gb_ref, m_ref):
        m_ref[...] = _merge_fn(pa_ref[...], pb_ref[...], ga_ref[...], gb_ref[...]).astype(BF16)

    return _pcall(body, name="merge_fwd", grid=(T // tr,),
                  in_specs=[_tile(tr, D), _tile(tr, D), _tile(tr, D, cga), _tile(tr, D, cgb)],
                  out_specs=_tile(tr, D), out_shape=jax.ShapeDtypeStruct((T, D), BF16),
                  compiler_params=_cparams(("parallel",)))(pa, pb, u, u)


def _merge_bwd(cfg, pa, pb, u, dm):
    T, D, tr = cfg.T, cfg.D, cfg.tr
    cga, cgb = cfg.o_gate // D, cfg.o_gate // D + 1

    def body(pa_ref, pb_ref, ga_ref, gb_ref, dm_ref, dpa_ref, dpb_ref, dg_ref):
        _, vjp = jax.vjp(_merge_fn, pa_ref[...], pb_ref[...], ga_ref[...], gb_ref[...])
        dpa, dpb, dga, dgb = vjp(dm_ref[...])
        dpa_ref[...] = dpa.astype(BF16)
        dpb_ref[...] = dpb.astype(BF16)
        dg_ref[:, :D] = dga.astype(BF16)
        dg_ref[:, D:] = dgb.astype(BF16)

    return _pcall(body, name="merge_bwd", grid=(T // tr,),
                  in_specs=[_tile(tr, D), _tile(tr, D), _tile(tr, D, cga), _tile(tr, D, cgb), _tile(tr, D)],
                  out_specs=[_tile(tr, D), _tile(tr, D), _tile(tr, 2 * D, cfg.o_gate // (2 * D))],
                  out_shape=[jax.ShapeDtypeStruct((T, D), BF16), jax.ShapeDtypeStruct((T, D), BF16),
                             jax.ShapeDtypeStruct((T, cfg.ncol), BF16)],
                  compiler_params=_cparams(("parallel",)))(pa, pb, u, u, dm)


def _gate_fn(o, z):
    return o * _silu(z)


def _gate_a_fwd(cfg, o, u):
    T, FW, tr = cfg.T, cfg.FW, cfg.tr

    def body(o_ref, z_ref, oa_ref):
        oa_ref[...] = _gate_fn(o_ref[...], z_ref[...]).astype(BF16)

    return _pcall(body, name="gate_a_fwd", grid=(T // tr,), in_specs=[_tile(tr, FW), _tile(tr, FW, 3)],
                  out_specs=_tile(tr, FW), out_shape=jax.ShapeDtypeStruct((T, FW), BF16),
                  compiler_params=_cparams(("parallel",)))(o, u)


def _gate_a_bwd(cfg, o, u, doa, du):
    T, FW, tr = cfg.T, cfg.FW, cfg.tr

    def body(o_ref, z_ref, doa_ref, du_in, do_ref, dz_ref):
        _, vjp = jax.vjp(_gate_fn, o_ref[...], z_ref[...])
        do, dz = vjp(doa_ref[...])
        do_ref[...] = do
        dz_ref[...] = dz.astype(BF16)

    return _pcall(body, name="gate_a_bwd", grid=(T // tr,),
                  in_specs=[_tile(tr, FW), _tile(tr, FW, 3), _tile(tr, FW), _ANY],
                  out_specs=[_tile(tr, FW), _tile(tr, FW, 3)],
                  out_shape=[jax.ShapeDtypeStruct((T, FW), F32), jax.ShapeDtypeStruct(du.shape, BF16)],
                  input_output_aliases={3: 1},
                  compiler_params=_cparams(("parallel",)))(o, u, doa, du)


def _fox_prep(cfg, u, fb):
    T, tr = cfg.T, cfg.tr
    cf = cfg.o_f // LANES

    def body(f_ref, fb_ref, c_ref, carry_ref):
        i = pl.program_id(0)

        @pl.when(i == 0)
        def _():
            carry_ref[...] = jnp.zeros_like(carry_ref)

        lf = -_softplus(-(f_ref[...] + fb_ref[...]))
        r = lax.broadcasted_iota(jnp.int32, (tr, tr), 0)
        c = lax.broadcasted_iota(jnp.int32, (tr, tr), 1)
        tri = (r >= c).astype(F32)
        c_ref[...] = _dot(tri, lf, precision=HI) + carry_ref[...]
        carry_ref[...] += jnp.sum(lf, axis=0, keepdims=True)

    return _pcall(body, name="fox_prep", grid=(T // tr,), in_specs=[_tile(tr, LANES, cf), _const((1, LANES))],
                  out_specs=_tile(tr, LANES), out_shape=jax.ShapeDtypeStruct((T, LANES), F32),
                  scratch_shapes=[pltpu.VMEM((1, LANES), F32)], compiler_params=_cparams(("arbitrary",)))(u, fb)


def _fox_prep_bwd(cfg, u, fb, dc):
    T, tr = cfg.T, cfg.tr
    cf = cfg.o_f // LANES
    nb = T // tr

    def body(f_ref, fb_ref, dc_ref, df_ref, dfb_ref, carry_ref):
        i = pl.program_id(0)

        @pl.when(i == 0)
        def _():
            carry_ref[...] = jnp.zeros_like(carry_ref)

        dc = dc_ref[...]
        r = lax.broadcasted_iota(jnp.int32, (tr, tr), 0)
        c = lax.broadcasted_iota(jnp.int32, (tr, tr), 1)
        triu = (r <= c).astype(F32)
        dlf = _dot(triu, dc, precision=HI) + carry_ref[...]
        carry_ref[...] += jnp.sum(dc, axis=0, keepdims=True)
        dz = dlf * jax.nn.sigmoid(-(f_ref[...] + fb_ref[...]))
        df_ref[...] = dz.astype(BF16)
        _acc_store(i, dfb_ref, jnp.sum(dz, axis=0, keepdims=True))

    rev = lambda i: (nb - 1 - i, 0)
    return _pcall(body, name="fox_prep_bwd", grid=(nb,),
                  in_specs=[pl.BlockSpec((tr, LANES), lambda i: (nb - 1 - i, cf)), _const((1, LANES)),
                            pl.BlockSpec((tr, LANES), rev)],
                  out_specs=[pl.BlockSpec((tr, LANES), rev), _const((1, LANES))],
                  out_shape=[jax.ShapeDtypeStruct((T, LANES), BF16), jax.ShapeDtypeStruct((1, LANES), F32)],
                  scratch_shapes=[pltpu.VMEM((1, LANES), F32)], compiler_params=_cparams(("arbitrary",)))(u, fb, dc)


def _attn_logits(q_ref, k_ref, c_ref, i, tq, te):
    s = _dot(q_ref[...].astype(BF16), k_ref[0:te, :].astype(BF16), "nt") * (FOX_HEAD_DIM ** -0.5) - c_ref[0, :, 0:te]
    row = i * tq + lax.broadcasted_iota(jnp.int32, (tq, te), 0)
    col = lax.broadcasted_iota(jnp.int32, (tq, te), 1)
    return jnp.where(col <= row, s, -1e30)


def _per_query_tile(i, nq, tq, fn):
    for ii in range(nq):
        pl.when(i == ii)(functools.partial(fn, (ii + 1) * tq))


def _attn_fwd(cfg, u, c_rows):
    T, FW, FH = cfg.T, cfg.FW, cfg.FH
    tq = min(256, T)
    dh = FOX_HEAD_DIM

    def body(q_ref, k_ref, v_ref, c_ref, o_ref, lse_ref):
        i = pl.program_id(1)

        def tile(te):
            s = _attn_logits(q_ref, k_ref, c_ref, i, tq, te)
            m = jnp.max(s, axis=1, keepdims=True)
            p = jnp.exp(s - m)
            l = jnp.sum(p, axis=1, keepdims=True)
            o_ref[...] = _dot(p.astype(BF16), v_ref[0:te, :].astype(BF16)) / l
            lse_ref[0] = m + jnp.log(l)

        _per_query_tile(i, T // tq, tq, tile)

    return _pcall(
        body, name="fox_attn_fwd", grid=(FH, T // tq),
        in_specs=[pl.BlockSpec((tq, dh), lambda h, i: (i, 3 * h)), pl.BlockSpec((T, dh), lambda h, i: (0, 3 * h + 1)),
                  pl.BlockSpec((T, dh), lambda h, i: (0, 3 * h + 2)), pl.BlockSpec((1, 1, T), lambda h, i: (h, 0, 0))],
        out_specs=[pl.BlockSpec((tq, dh), lambda h, i: (i, h)), pl.BlockSpec((1, tq, 1), lambda h, i: (h, i, 0))],
        out_shape=[jax.ShapeDtypeStruct((T, FW), F32), jax.ShapeDtypeStruct((FH, T, 1), F32)],
        compiler_params=_cparams(("parallel", "arbitrary")),
    )(u, u, u, c_rows)


def _attn_bwd(cfg, u, c_rows, lse, do, du):
    T, FW, FH = cfg.T, cfg.FW, cfg.FH
    tq = min(256, T)
    nq = T // tq
    dh = FOX_HEAD_DIM
    scale = dh ** -0.5

    def body(q_ref, k_ref, v_ref, c_ref, lse_ref, do_ref, du_in, du_ref, dcol_ref, dk_acc, dv_acc):
        i = pl.program_id(1)

        @pl.when(i == 0)
        def _():
            dk_acc[...] = jnp.zeros_like(dk_acc)
            dv_acc[...] = jnp.zeros_like(dv_acc)
            dcol_ref[...] = jnp.zeros_like(dcol_ref)

        def tile(te):
            s = _attn_logits(q_ref, k_ref, c_ref, i, tq, te)
            p = jnp.exp(s - lse_ref[0])
            do_v = do_ref[...]
            dp = _dot(do_v.astype(BF16), v_ref[0:te, :].astype(BF16), "nt")
            delta = jnp.sum(p * dp, axis=1, keepdims=True)
            ds = p * (dp - delta)
            ds16 = ds.astype(BF16)
            du_ref[te - tq:te, 0:dh] = (_dot(ds16, k_ref[0:te, :].astype(BF16)) * scale).astype(BF16)
            dk_acc[0:te, :] += _dot(ds16, q_ref[...].astype(BF16), "tn") * scale
            dv_acc[0:te, :] += _dot(p.astype(BF16), do_v.astype(BF16), "tn")
            dcol_ref[0, :, 0:te] += jnp.sum(ds, axis=0, keepdims=True)

        _per_query_tile(i, nq, tq, tile)

        @pl.when(i == nq - 1)
        def _():
            du_ref[:, dh:2 * dh] = dk_acc[...].astype(BF16)
            du_ref[:, 2 * dh:3 * dh] = dv_acc[...].astype(BF16)

    return _pcall(
        body, name="fox_attn_bwd", grid=(FH, nq),
        in_specs=[pl.BlockSpec((tq, dh), lambda h, i: (i, 3 * h)), pl.BlockSpec((T, dh), lambda h, i: (0, 3 * h + 1)),
                  pl.BlockSpec((T, dh), lambda h, i: (0, 3 * h + 2)), pl.BlockSpec((1, 1, T), lambda h, i: (h, 0, 0)),
                  pl.BlockSpec((1, tq, 1), lambda h, i: (h, i, 0)), pl.BlockSpec((tq, dh), lambda h, i: (i, h)), _ANY],
        out_specs=[pl.BlockSpec((T, 3 * dh), lambda h, i: (0, h)), pl.BlockSpec((1, 1, T), lambda h, i: (h, 0, 0))],
        out_shape=[jax.ShapeDtypeStruct(du.shape, BF16), jax.ShapeDtypeStruct((FH, 1, T), F32)],
        scratch_shapes=[pltpu.VMEM((T, dh), F32), pltpu.VMEM((T, dh), F32)],
        input_output_aliases={6: 0},
        compiler_params=_cparams(("parallel", "arbitrary")),
    )(u, u, u, c_rows, lse, do, du)


def _head_indicators(cfg):
    ind = np.zeros((cfg.RW, LANES), np.float32)
    ind[np.arange(cfg.RW), np.arange(cfg.RW) // RWKV_HEAD_DIM] = 1.0
    pad = np.zeros((1, LANES), np.float32)
    pad[0, cfg.RH:] = 1.0
    return jnp.asarray(ind), jnp.asarray(ind.T.copy()), jnp.asarray(pad)


def _prep_fn(us_r, us_k, us_v, us_wd, us_ad, w0, w2p, a0, a2p, k_k, k_a, ind, ind_t, pad):
    wpre = w0 + _dot3(jnp.tanh(us_wd), w2p)
    w = -_softplus(-wpre) - 0.5
    lw = -jnp.exp(w)
    a = jax.nn.sigmoid(a0 + _dot3(us_ad, a2p))
    kk = us_k * k_k
    ss = _xdot(kk * kk, ind, ind_t) + pad
    inv = 1.0 / jnp.maximum(jnp.sqrt(ss), L2_EPS)
    kkn = kk * _xdot(inv, ind_t, ind)
    kp = us_k * (1.0 + (a - 1.0) * k_a)
    return us_r, lw, kp, us_v, -kkn, kkn * a


def _shifted(u, prev_row, mu, first):
    n = u.shape[0]
    rolled = pltpu.roll(u, 1, 0)
    row = lax.broadcasted_iota(jnp.int32, u.shape, 0)
    p0 = jnp.where(first, jnp.zeros_like(prev_row), prev_row)
    prev = jnp.where(row == 0, jnp.broadcast_to(p0, u.shape), rolled)
    return u + (prev - u) * mu, prev


def _rwkv_specs(cfg, tr):
    RW, LP = cfg.RW, cfg.LP
    base = cfg.o_rwkv // RW
    cols = [(RW, base), (RW, base + 1), (RW, base + 2), (RW, base + 3), (LP, cfg.o_wd // LP), (LP, cfg.o_ad // LP)]
    cur = [pl.BlockSpec((tr, w), (lambda i, cb=cb: (i, cb))) for w, cb in cols]
    prv = [pl.BlockSpec((8, w), (lambda i, cb=cb: (jnp.maximum(i * (tr // 8) - 1, 0), cb))) for w, cb in cols]
    return cols, cur, prv


def _mu_pieces(cfg, mu_ref):
    RW, LP = cfg.RW, cfg.LP
    offs = [0, RW, 2 * RW, 3 * RW, 4 * RW, 4 * RW + LP, 4 * RW + 2 * LP]
    return [mu_ref[:, offs[j]:offs[j + 1]] for j in range(6)]


def _rwkv_prep_fwd(cfg, u, mu, w0, w2p, a0, a2p, k_k, k_a):
    T, RW, LP, tr = cfg.T, cfg.RW, cfg.LP, cfg.tr
    ind, ind_t, pad = _head_indicators(cfg)
    cols, cur, prv = _rwkv_specs(cfg, tr)

    def body(*refs):
        u_refs, p_refs = refs[0:6], refs[6:12]
        mu_ref, w0_ref, w2_ref, a0_ref, a2_ref, kk_ref, ka_ref, ind_ref, indt_ref, pad_ref = refs[12:22]
        outs = refs[22:]
        first = pl.program_id(0) == 0
        mus = _mu_pieces(cfg, mu_ref)
        us = [_shifted(u_refs[j][...], p_refs[j][7:8, :], mus[j], first)[0] for j in range(6)]
        res = _prep_fn(us[0], us[1], us[2], us[4], us[5], w0_ref[...], w2_ref[...], a0_ref[...], a2_ref[...],
                       kk_ref[...], ka_ref[...], ind_ref[...], indt_ref[...], pad_ref[...])
        for j in range(6):
            outs[j][...] = res[j]
        outs[6][...] = us[3]

    consts = [mu, w0, w2p, a0, a2p, k_k, k_a, ind, ind_t, pad]
    return _pcall(body, name="rwkv_prep_fwd", grid=(T // tr,),
                  in_specs=cur + prv + [_const(c.shape) for c in consts],
                  out_specs=[_tile(tr, RW)] * 7, out_shape=[jax.ShapeDtypeStruct((T, RW), F32)] * 7,
                  compiler_params=_cparams(("parallel",)))(*([u] * 12), *consts)


def _rwkv_prep_bwd(cfg, u, mu, w0, w2p, a0, a2p, k_k, k_a, cots, dzb):
    T, RW, LP = cfg.T, cfg.RW, cfg.LP
    tr = min(128, T)
    ind, ind_t, pad = _head_indicators(cfg)
    cols, cur, prv = _rwkv_specs(cfg, tr)
    rseg = cfg.rseg

    def body(*refs):
        u_refs, p_refs = refs[0:6], refs[6:12]
        mu_ref, w0_ref, w2_ref, a0_ref, a2_ref, kk_ref, ka_ref, ind_ref, indt_ref, pad_ref = refs[12:22]
        cot_refs, dzb_ref = refs[22:28], refs[28]
        dus_ref, dmu_ref, dw0_ref, dw2_ref, da0_ref, da2_ref, dkk_ref, dka_ref = refs[29:]
        i = pl.program_id(0)
        first = i == 0
        mus = _mu_pieces(cfg, mu_ref)
        sh = [_shifted(u_refs[j][...], p_refs[j][7:8, :], mus[j], first) for j in range(6)]
        us = [s[0] for s in sh]
        fn = functools.partial(_prep_fn, ind=ind_ref[...], ind_t=indt_ref[...], pad=pad_ref[...])
        _, vjp = jax.vjp(fn, us[0], us[1], us[2], us[4], us[5], w0_ref[...], w2_ref[...], a0_ref[...], a2_ref[...],
                         kk_ref[...], ka_ref[...])
        d = vjp(tuple(c[...] for c in cot_refs))
        dus = [d[0], d[1], d[2], dzb_ref[...], d[3], d[4]]
        offs = [0, RW, 2 * RW, 3 * RW, 4 * RW, 4 * RW + LP, 4 * RW + 2 * LP]
        for j in range(6):
            dus_ref[:, offs[j]:offs[j + 1]] = dus[j]
            dmu_j = jnp.sum(dus[j] * (sh[j][1] - u_refs[j][...]), axis=0, keepdims=True)

            @pl.when(first)
            def _(j=j, dmu_j=dmu_j):
                dmu_ref[:, offs[j]:offs[j + 1]] = dmu_j

            @pl.when(i > 0)
            def _(j=j, dmu_j=dmu_j):
                dmu_ref[:, offs[j]:offs[j + 1]] += dmu_j
        for ref, val in zip((dw0_ref, dw2_ref, da0_ref, da2_ref, dkk_ref, dka_ref), d[5:11]):
            _acc_store(i, ref, val)

    consts = [mu, w0, w2p, a0, a2p, k_k, k_a, ind, ind_t, pad]
    vec = jax.ShapeDtypeStruct((1, RW), F32)
    mat = jax.ShapeDtypeStruct((LP, RW), F32)
    return _pcall(body, name="rwkv_prep_bwd", grid=(T // tr,),
                  in_specs=cur + prv + [_const(c.shape) for c in consts] + [_tile(tr, RW)] * 7,
                  out_specs=[_tile(tr, rseg), _const((1, rseg)), _const((1, RW)), _const((LP, RW)), _const((1, RW)),
                             _const((LP, RW)), _const((1, RW)), _const((1, RW))],
                  out_shape=[jax.ShapeDtypeStruct((T, rseg), F32), jax.ShapeDtypeStruct((1, rseg), F32),
                             vec, mat, vec, mat, vec, vec],
                  compiler_params=_cparams(("arbitrary",)))(*([u] * 12), *consts, *cots, dzb)


def _shift_bwd(cfg, dus, mu, df, du):
    T, tr, RW, LP = cfg.T, cfg.tr, cfg.RW, cfg.LP
    nb = T // tr
    tail = cfg.ncol - cfg.o_f
    assert cfg.o_rwkv % (4 * RW) == 0 and (4 * RW) % (2 * LP) == 0 and cfg.o_f % tail == 0

    def shifted(d_ref, n_ref, mu_ref):
        d = d_ref[...]
        rolled = pltpu.roll(d, tr - 1, 0)
        row = lax.broadcasted_iota(jnp.int32, d.shape, 0)
        n0 = jnp.where(pl.program_id(0) == nb - 1, jnp.zeros_like(n_ref[0:1, :]), n_ref[0:1, :])
        nxt = jnp.where(row == tr - 1, jnp.broadcast_to(n0, d.shape), rolled)
        mu_v = mu_ref[...]
        return (d * (1.0 - mu_v) + nxt * mu_v).astype(BF16)

    def main_body(d_ref, n_ref, mu_ref, du_in, du_ref):
        du_ref[...] = shifted(d_ref, n_ref, mu_ref)

    def tail_body(d_ref, n_ref, mu_ref, df_ref, du_in, du_ref):
        du_ref[:, 0:LANES] = df_ref[...]
        du_ref[:, LANES:LANES + 2 * LP] = shifted(d_ref, n_ref, mu_ref)
        if tail > LANES + 2 * LP:
            du_ref[:, LANES + 2 * LP:] = jnp.zeros((tr, tail - LANES - 2 * LP), BF16)

    def specs(w, cb):
        return [_tile(tr, w, cb),
                pl.BlockSpec((8, w), lambda i: (jnp.minimum((i + 1) * (tr // 8), T // 8 - 1), cb)),
                pl.BlockSpec((1, w), lambda i: (0, cb))]

    out = jax.ShapeDtypeStruct(du.shape, BF16)
    du = _pcall(main_body, name="shift_bwd_main", grid=(nb,), in_specs=specs(4 * RW, 0) + [_ANY],
                out_specs=_tile(tr, 4 * RW, cfg.o_rwkv // (4 * RW)), out_shape=out, input_output_aliases={3: 0},
                compiler_params=_cparams(("parallel",)))(dus, dus, mu, du)
    return _pcall(tail_body, name="shift_bwd_tail", grid=(nb,),
                  in_specs=specs(2 * LP, 4 * RW // (2 * LP)) + [_tile(tr, LANES), _ANY],
                  out_specs=_tile(tr, tail, cfg.o_f // tail), out_shape=out, input_output_aliases={4: 0},
                  compiler_params=_cparams(("parallel",)))(dus, dus, mu, df, du)


def _chunk_local(r, lw, k, v, a, b):
    H, C, K = r.shape
    row = lax.broadcasted_iota(jnp.int32, (C, C), 0)
    col = lax.broadcasted_iota(jnp.int32, (C, C), 1)
    incl = jnp.broadcast_to((row >= col).astype(F32)[None], (H, C, C))
    strict = (row > col)[None]
    lower = (row >= col)[None]
    eye = (row == col)[None]
    zero = jnp.zeros((), F32)
    L = _bdot(incl, lw, 2, 1)
    LC = jnp.sum(lw, axis=1, keepdims=True)
    eL = jnp.exp(L)
    eLn = jnp.exp(-L)
    at = a * jnp.exp(L - lw)
    rt = r * eL
    bt = b * eLn
    kt = k * eLn
    eR = jnp.exp(LC - L)
    bh = b * eR
    kh = k * eR
    gram = functools.partial(_bdot, passes=SCAN_PASSES[0])
    inv = functools.partial(_bdot, passes=SCAN_PASSES[1])
    app = functools.partial(_bdot, passes=SCAN_PASSES[2])
    n_ab = jnp.where(strict, gram(at, bt, 2, 2), zero)
    n_ak = jnp.where(strict, gram(at, kt, 2, 2), zero)
    m_rb = jnp.where(lower, gram(rt, bt, 2, 2), zero)
    m_rk = jnp.where(lower, gram(rt, kt, 2, 2), zero)
    M = n_ab
    P = jnp.where(eye, 1.0, zero) + n_ab
    for _ in range(1, max(1, int(np.ceil(np.log2(C))))):
        M = inv(M, M, 2, 1)
        P = P + inv(M, P, 2, 1)
    W = app(P, at, 2, 1)
    Uloc = app(P, app(n_ak, v, 2, 1), 2, 1)
    Q = rt + app(m_rb, W, 2, 1)
    Yloc = app(m_rb, Uloc, 2, 1) + app(m_rk, v, 2, 1)
    A = jnp.where(eye, jnp.exp(LC), zero) + app(W, bh, 1, 1)
    Sloc = app(Uloc, bh, 1, 1) + app(v, kh, 1, 1)
    return Q, Yloc, A, Sloc


def _split_heads(ref, n):
    N = RWKV_HEAD_DIM
    return jnp.stack([ref[:, h * N:(h + 1) * N] for h in range(n)], axis=0)


def _merge_heads(x):
    return jnp.concatenate([x[h] for h in range(x.shape[0])], axis=1)


def _scan_local_specs(cfg):
    N, HB = RWKV_HEAD_DIM, cfg.hb
    grid = (cfg.RH // HB, cfg.T // cfg.C)
    seq = pl.BlockSpec((HB, cfg.C, N), lambda h, j: (h, j, 0))
    mat = pl.BlockSpec((HB, 1, N, N), lambda h, j: (h, j, 0, 0))
    return grid, seq, mat


def _scan_local_fwd(cfg, seqs):
    T, RH, N = cfg.T, cfg.RH, RWKV_HEAD_DIM
    grid, seq, mat = _scan_local_specs(cfg)

    def body(r_ref, lw_ref, k_ref, v_ref, a_ref, b_ref, q_ref, yl_ref, a_out, sl_ref):
        Q, Yloc, A, Sloc = _chunk_local(*[_split_heads(ref, cfg.hb) for ref in (r_ref, lw_ref, k_ref, v_ref, a_ref, b_ref)])
        q_ref[...] = Q
        yl_ref[...] = Yloc
        a_out[:, 0] = A
        sl_ref[:, 0] = Sloc

    tok = pl.BlockSpec((cfg.C, cfg.hb * N), lambda h, j: (j, h))
    sq = jax.ShapeDtypeStruct((RH, T, N), F32)
    mt = jax.ShapeDtypeStruct((RH, T // cfg.C, N, N), F32)
    return _pcall(body, name="rwkv_scan_local_fwd", grid=grid, in_specs=[tok] * 6, out_specs=[seq, seq, mat, mat],
                  out_shape=[sq, sq, mt, mt], compiler_params=_cparams(("parallel", "parallel")))(*seqs)


def _scan_local_bwd(cfg, toks, dq, dy, da, dsl, extra, comm=None):
    T, RW, N = cfg.T, cfg.RW, RWKV_HEAD_DIM
    grid, seq, mat = _scan_local_specs(cfg)
    c_in, c_out, c_scr = comm[:3] if comm else ([], [], [])

    def body(r_ref, lw_ref, k_ref, v_ref, a_ref, b_ref, dq_ref, dy_ref, da_ref, dsl_ref, xr_ref, xk_ref, xv_ref,
             *rest):
        cin, outs = rest[:len(c_in)], rest[len(c_in):len(c_in) + 6]
        cout, scr = rest[len(c_in) + 6:len(c_in) + 6 + len(c_out)], rest[len(c_in) + 6 + len(c_out):]
        _comm_at(comm, 3, grid, cin, cout, scr)
        ins = [_split_heads(ref, cfg.hb) for ref in (r_ref, lw_ref, k_ref, v_ref, a_ref, b_ref)]
        _, vjp = jax.vjp(_chunk_local, *ins)
        d = vjp((dq_ref[...], _split_heads(dy_ref, cfg.hb), da_ref[:, 0], dsl_ref[:, 0]))
        add = {0: xr_ref, 2: xk_ref, 3: xv_ref}
        for j in range(6):
            dj = _merge_heads(d[j])
            outs[j][...] = dj + add[j][...] if j in add else dj
        _comm_at(comm, 4, grid, cin, cout, scr)

    tok = pl.BlockSpec((cfg.C, cfg.hb * N), lambda h, j: (j, h))
    return _pcall(body, name="rwkv_scan_local_bwd", grid=grid,
                  in_specs=[tok] * 6 + [seq, tok, mat, mat] + [tok] * 3 + [_ANY] * len(c_in),
                  out_specs=[tok] * 6 + [_ANY] * len(c_out),
                  out_shape=[jax.ShapeDtypeStruct((T, RW), F32)] * 6 + list(c_out), scratch_shapes=list(c_scr),
                  compiler_params=_cparams(("arbitrary", "arbitrary") if comm else ("parallel", "parallel")),
                  )(*toks, dq, dy, da, dsl, *extra, *c_in)


def _scan_carry_specs(cfg, rev):
    N, RH, C, nc = RWKV_HEAD_DIM, cfg.RH, cfg.C, cfg.T // cfg.C
    at = (lambda j: nc - 1 - j) if rev else (lambda j: j)
    seq = pl.BlockSpec((RH, C, N), lambda j: (0, at(j), 0))
    mat = pl.BlockSpec((RH, 1, N, N), lambda j: (0, at(j), 0, 0))
    return nc, seq, mat


def _scan_carry_fwd(cfg, q, yloc, a, sloc):
    T, RH, N = cfg.T, cfg.RH, RWKV_HEAD_DIM
    nc, seq, mat = _scan_carry_specs(cfg, False)

    def body(q_ref, yl_ref, a_ref, sl_ref, y_ref, ck_ref, s_ref):
        @pl.when(pl.program_id(0) == 0)
        def _():
            s_ref[...] = jnp.zeros_like(s_ref)

        S = s_ref[...]
        ck_ref[:, 0] = S
        y_ref[...] = _merge_heads(_bdot(q_ref[...], S, 2, 2) + yl_ref[...])
        s_ref[...] = _bdot(S, a_ref[:, 0], 2, 1) + sl_ref[:, 0]

    tok = pl.BlockSpec((cfg.C, cfg.RW), lambda j: (j, 0))
    return _pcall(body, name="rwkv_scan_carry_fwd", grid=(nc,), in_specs=[seq, seq, mat, mat], out_specs=[tok, mat],
                  out_shape=[jax.ShapeDtypeStruct((T, cfg.RW), F32), jax.ShapeDtypeStruct((RH, nc, N, N), F32)],
                  scratch_shapes=[pltpu.VMEM((RH, N, N), F32)],
                  compiler_params=_cparams(("arbitrary",)))(q, yloc, a, sloc)


def _scan_carry_bwd(cfg, q, a, ckpt, dy):
    T, RH, N = cfg.T, cfg.RH, RWKV_HEAD_DIM
    nc, seq, mat = _scan_carry_specs(cfg, True)

    def body(q_ref, a_ref, ck_ref, dy_ref, dq_ref, da_ref, dsl_ref, ds_ref):
        @pl.when(pl.program_id(0) == 0)
        def _():
            ds_ref[...] = jnp.zeros_like(ds_ref)

        S, dS, dY = ck_ref[:, 0], ds_ref[...], _split_heads(dy_ref, RH)
        dq_ref[...] = _bdot(dY, S, 2, 1)
        da_ref[:, 0] = _bdot(S, dS, 1, 1)
        dsl_ref[:, 0] = dS
        ds_ref[...] = _bdot(dS, a_ref[:, 0], 2, 2) + _bdot(dY, q_ref[...], 1, 1)

    mt = jax.ShapeDtypeStruct((RH, nc, N, N), F32)
    tok = pl.BlockSpec((cfg.C, cfg.RW), lambda j: (nc - 1 - j, 0))
    return _pcall(body, name="rwkv_scan_carry_bwd", grid=(nc,), in_specs=[seq, mat, mat, tok],
                  out_specs=[seq, mat, mat], out_shape=[jax.ShapeDtypeStruct((RH, T, N), F32), mt, mt],
                  scratch_shapes=[pltpu.VMEM((RH, N, N), F32)],
                  compiler_params=_cparams(("arbitrary",)))(q, a, ckpt, dy)


def _post_fn(y, r, kp, v, zb, ln_w, ln_b, rk, ind, ind_t):
    n = float(RWKV_HEAD_DIM)
    mu = _xdot(_xdot(y, ind, ind_t) / n, ind_t, ind)
    yc = y - mu
    var = _xdot(yc * yc, ind, ind_t) / n
    rstd = _xdot(lax.rsqrt(var + GN_EPS), ind_t, ind)
    yn = yc * rstd * ln_w + ln_b
    bonus = _xdot(_xdot(r * kp * rk, ind, ind_t), ind_t, ind) * v
    return (yn + bonus) * _silu(zb)


def _rwkv_post_fwd(cfg, y, r, kp, v, zb, ln_w, ln_b, rk):
    T, RW, tr = cfg.T, cfg.RW, cfg.tr
    ind, ind_t, _ = _head_indicators(cfg)

    def body(y_ref, r_ref, k_ref, v_ref, z_ref, lw_ref, lb_ref, rk_ref, ind_ref, indt_ref, ob_ref):
        ob_ref[...] = _post_fn(y_ref[...], r_ref[...], k_ref[...], v_ref[...], z_ref[...], lw_ref[...], lb_ref[...],
                               rk_ref[...], ind_ref[...], indt_ref[...]).astype(BF16)

    consts = [ln_w, ln_b, rk, ind, ind_t]
    return _pcall(body, name="rwkv_post_fwd", grid=(T // tr,),
                  in_specs=[_tile(tr, RW)] * 5 + [_const(c.shape) for c in consts],
                  out_specs=_tile(tr, RW), out_shape=jax.ShapeDtypeStruct((T, RW), BF16),
                  compiler_params=_cparams(("parallel",)))(y, r, kp, v, zb, *consts)


def _rwkv_post_bwd(cfg, y, r, kp, v, zb, ln_w, ln_b, rk, dob):
    T, RW = cfg.T, cfg.RW
    tr = min(128, T)
    ind, ind_t, _ = _head_indicators(cfg)

    def body(y_ref, r_ref, k_ref, v_ref, z_ref, lw_ref, lb_ref, rk_ref, ind_ref, indt_ref, dob_ref,
             dy_ref, dr_ref, dk_ref, dv_ref, dz_ref, dlw_ref, dlb_ref, drk_ref):
        fn = functools.partial(_post_fn, ind=ind_ref[...], ind_t=indt_ref[...])
        _, vjp = jax.vjp(fn, y_ref[...], r_ref[...], k_ref[...], v_ref[...], z_ref[...], lw_ref[...], lb_ref[...],
                         rk_ref[...])
        d = vjp(dob_ref[...])
        for ref, val in zip((dy_ref, dr_ref, dk_ref, dv_ref, dz_ref), d[:5]):
            ref[...] = val
        i = pl.program_id(0)
        for ref, val in zip((dlw_ref, dlb_ref, drk_ref), d[5:8]):
            _acc_store(i, ref, val)

    consts = [ln_w, ln_b, rk, ind, ind_t]
    vec = jax.ShapeDtypeStruct((1, RW), F32)
    return _pcall(body, name="rwkv_post_bwd", grid=(T // tr,),
                  in_specs=[_tile(tr, RW)] * 5 + [_const(c.shape) for c in consts] + [_tile(tr, RW)],
                  out_specs=[_tile(tr, RW)] * 5 + [_const((1, RW))] * 3,
                  out_shape=[jax.ShapeDtypeStruct((T, RW), F32)] * 5 + [vec] * 3,
                  compiler_params=_cparams(("arbitrary",)))(y, r, kp, v, zb, *consts, dob)


def _adamw_math(w, g, m, v):
    m = ADAM_B1 * m + (1.0 - ADAM_B1) * g
    v = ADAM_B2 * v + (1.0 - ADAM_B2) * (g * g)
    m_hat = m / (1.0 - ADAM_B1 ** ADAM_STEP)
    v_hat = v / (1.0 - ADAM_B2 ** ADAM_STEP)
    delta = -ADAM_LR * (m_hat / (jnp.sqrt(v_hat) + ADAM_EPS) + ADAM_WD * w)
    return delta, m, v


def _adamw(name, w, g, m, v, copy_grad=False):
    R, Cc = w.shape
    Rp = -(-R // 8) * 8
    tr = Rp
    for nb in range(1, Rp // 8 + 1):
        if (Rp // 8) % nb == 0 and (Rp // nb) * Cc * 4 <= 2 * 1024 * 1024:
            tr = Rp // nb
            break

    def body(w_ref, g_ref, m_ref, v_ref, d_ref, nm_ref, nv_ref, *g_out):
        g_v = g_ref[...]
        d, nm, nv = _adamw_math(w_ref[...], g_v, m_ref[...], v_ref[...])
        d_ref[...] = d
        nm_ref[...] = nm
        nv_ref[...] = nv
        if copy_grad:
            g_out[0][...] = g_v

    spec = _tile(tr, Cc)
    n_out = 4 if copy_grad else 3
    return _pcall(body, name=name, grid=(Rp // tr,), in_specs=[spec] * 4, out_specs=[spec] * n_out,
                  out_shape=[jax.ShapeDtypeStruct((R, Cc), F32)] * n_out,
                  compiler_params=_cparams(("parallel",)))(w, g, m, v)


def _row_tile(R, Cc, itemsize, budget=2 * 1024 * 1024):
    for nb in range(1, R // 16 + 1):
        if R % nb == 0 and (R // nb) % 16 == 0 and (R // nb) * Cc * itemsize <= budget:
            return R // nb
    return R


def _add_halves(name, gs, r1, c_idx):
    _, R, Cc = gs.shape
    half = R // 2
    tr = _row_tile(half, Cc, 4)
    nb = half // tr

    def body(c_ref, g_ref, r_ref, o_ref):
        o_ref[...] = (g_ref[...].astype(F32) + r_ref[...].astype(F32)).astype(BF16)

    grid_spec = pltpu.PrefetchScalarGridSpec(
        num_scalar_prefetch=1, grid=(N_CHIPS, nb),
        in_specs=[pl.BlockSpec((1, tr, Cc), lambda s, i, c: (s, c[0] * nb + i, 0)),
                  pl.BlockSpec((1, tr, Cc), lambda s, i, c: (s, i, 0))],
        out_specs=pl.BlockSpec((1, tr, Cc), lambda s, i, c: (s, i, 0)))
    return _pcall(body, name=name, grid_spec=grid_spec, out_shape=jax.ShapeDtypeStruct((N_CHIPS, half, Cc), BF16),
                  compiler_params=_cparams(("parallel", "parallel")))(c_idx, gs, r1)


def _sum_slots(name, r2):
    S, R, Cc = r2.shape
    tr = _row_tile(R, Cc, 4 * S // 2 if r2.dtype == BF16 else 4 * S)

    def body(r_ref, o_ref):
        acc = r_ref[0].astype(F32)
        for s in range(1, S):
            acc = acc + r_ref[s].astype(F32)
        o_ref[...] = acc

    return _pcall(body, name=name, grid=(R // tr,), in_specs=[pl.BlockSpec((S, tr, Cc), lambda i: (0, i, 0))],
                  out_specs=_tile(tr, Cc), out_shape=jax.ShapeDtypeStruct((R, Cc), F32),
                  compiler_params=_cparams(("parallel",)))(r2)


def _sum_chips(name, recv, own, place):
    S, H, Cc = recv.shape
    tr = _row_tile(H, Cc, 4, 1024 * 1024)
    nb = H // tr

    def body(p_ref, r_ref, own_ref, o_ref):
        s = pl.program_id(1)
        me = p_ref[0]

        @pl.when(s == 0)
        def _():
            o_ref[...] = jnp.zeros_like(o_ref)

        @pl.when(s == me)
        def _():
            o_ref[...] += own_ref[0].astype(F32)

        @pl.when(s != me)
        def _():
            o_ref[...] += r_ref[0].astype(F32)

    grid_spec = pltpu.PrefetchScalarGridSpec(
        num_scalar_prefetch=1, grid=(nb, S),
        in_specs=[pl.BlockSpec((1, tr, Cc), lambda i, s, p: (jnp.where(s == p[0], (s + 1) % S, s), i, 0)),
                  pl.BlockSpec((1, tr, Cc), lambda i, s, p: (p[0], i, 0))],
        out_specs=pl.BlockSpec((tr, Cc), lambda i, s, p: (p[1] * nb + i, 0)))
    return _pcall(body, name=name, grid_spec=grid_spec, out_shape=jax.ShapeDtypeStruct((2 * H, Cc), F32),
                  compiler_params=_cparams(("parallel", "arbitrary")))(place, recv, own)


def _cast_bf16(name, w):
    R, Cc = w.shape
    tr = _row_tile(R, Cc, 4)

    def body(w_ref, o_ref):
        o_ref[...] = w_ref[...].astype(BF16)

    return _pcall(body, name=name, grid=(R // tr,), in_specs=[_tile(tr, Cc)], out_specs=_tile(tr, Cc),
                  out_shape=jax.ShapeDtypeStruct((R, Cc), BF16), compiler_params=_cparams(("parallel",)))(w)


_ANY = pl.BlockSpec(memory_space=pl.ANY)


def _place():
    x, y, c = lax.axis_index("x"), lax.axis_index("y"), lax.axis_index("c")
    others = [(1 - x, y), (x, 1 - y), (1 - x, 1 - y)]
    return x, y, c, others


def _gather_weights(shards):
    arrays, out_shapes, scratch, start, finish, middle = _gather_parts(shards)
    n = len(shards)

    def body(*refs):
        ins, outs, sems = refs[:n], refs[n:2 * n], refs[2 * n:]
        start(ins, outs, sems)
        middle(ins, outs, sems)
        finish(ins, outs, sems)

    return _pcall(body, name="gather_weights", in_specs=[_ANY] * n, out_specs=[_ANY] * n, out_shape=out_shapes,
                  scratch_shapes=scratch)(*arrays)


def _gather_parts(shards):
    n = len(shards)
    halves = [s.shape[0] // 2 for s in shards]

    def parts(ins, outs, sems):
        x, y, c, _ = _place()
        me = 2 * x + y
        n1 = (x ^ (1 - c), y ^ c)
        n2 = (x ^ c, y ^ (1 - c))
        s1, s2, sd = 2 * n1[0] + n1[1], 2 * n2[0] + n2[1], 2 * (1 - x) + (1 - y)
        sib = (x, y, 1 - c)

        def rows(k, chip, hc):
            return outs[k].at[chip, pl.ds(hc * halves[k], halves[k]), :]

        def remote(k, j, src, dst, to):
            return pltpu.make_async_remote_copy(src_ref=src, dst_ref=dst, send_sem=sems[0].at[6 * k + j],
                                                recv_sem=sems[1].at[6 * k + j], device_id=to, device_id_type=MESH)

        def copy(k, j):
            if j < 2:
                mine = ins[k].at[pl.ds(c * halves[k], halves[k]), :]
                return remote(k, j, mine, rows(k, me, c), (*(n1 if j == 0 else n2), c))
            land = rows(k, {2: s1, 3: s1, 4: s2, 5: sd}[j], c)
            return remote(k, j, land, land, (*n2, c) if j == 2 else sib)

        def arrived(k, j):
            hc = c if j < 3 else 1 - c
            land = rows(k, {0: s1, 1: s2, 2: sd, 3: s2, 4: s1, 5: sd}[j], hc)
            remote(k, j, land, land, (x, y, c)).wait_recv()

        return copy, arrived

    def start(ins, outs, sems):
        copy, _ = parts(ins, outs, sems)
        for k in range(n):
            copy(k, 0).start()
            copy(k, 1).start()

    def middle(ins, outs, sems):
        copy, arrived = parts(ins, outs, sems)
        for k in range(n):
            arrived(k, 0)
            copy(k, 2).start()
            copy(k, 3).start()
            arrived(k, 1)
            copy(k, 4).start()

    def finish(ins, outs, sems):
        copy, arrived = parts(ins, outs, sems)
        for k in range(n):
            arrived(k, 2)
            copy(k, 5).start()
        for k in range(n):
            for j in (3, 4, 5):
                arrived(k, j)
        for k in range(n):
            for j in range(6):
                copy(k, j).wait_send()

    out_shapes = [jax.ShapeDtypeStruct((N_CHIPS,) + s.shape, s.dtype) for s in shards]
    scratch = [pltpu.SemaphoreType.DMA((6 * n,)), pltpu.SemaphoreType.DMA((6 * n,))]
    return list(shards), out_shapes, scratch, start, finish, middle


def _exchange_halves(name, grads):
    n = len(grads)
    halves = [g.shape[1] // 2 for g in grads]

    def body(*refs):
        ins, outs = refs[:n], refs[n:2 * n]
        send_sems, recv_sems = refs[2 * n:]
        x, y, c, _ = _place()
        cps = []
        for k in range(n):
            src = ins[k].at[:, pl.ds((1 - c) * halves[k], halves[k]), :]
            cp = pltpu.make_async_remote_copy(src_ref=src, dst_ref=outs[k], send_sem=send_sems.at[k],
                                              recv_sem=recv_sems.at[k], device_id=(x, y, 1 - c), device_id_type=MESH)
            cp.start()
            cps.append(cp)
        for cp in cps:
            cp.wait()

    return _pcall(
        body, name=name, in_specs=[_ANY] * n, out_specs=[_ANY] * n,
        out_shape=[jax.ShapeDtypeStruct((N_CHIPS, h) + g.shape[2:], g.dtype) for g, h in zip(grads, halves)],
        scratch_shapes=[pltpu.SemaphoreType.DMA((n,)), pltpu.SemaphoreType.DMA((n,))],
    )(*grads)


def _scatter_to_owners(chip_sums):
    n = len(chip_sums)

    def sends(ins, outs, sems):
        x, y, c, others = _place()
        me = 2 * x + y
        return [pltpu.make_async_remote_copy(
            src_ref=ins[k].at[2 * px + py], dst_ref=outs[k].at[me], send_sem=sems[0].at[3 * k + j],
            recv_sem=sems[1].at[3 * k + j], device_id=(px, py, c), device_id_type=MESH)
            for k in range(n) for j, (px, py) in enumerate(others)]

    def start(ins, outs, sems):
        for cp in sends(ins, outs, sems):
            cp.start()

    def finish(ins, outs, sems):
        x, y, c, others = _place()
        for k in range(n):
            for j, (px, py) in enumerate(others):
                land = outs[k].at[2 * px + py]
                pltpu.make_async_remote_copy(src_ref=land, dst_ref=land, send_sem=sems[0].at[3 * k + j],
                                             recv_sem=sems[1].at[3 * k + j], device_id=(x, y, c),
                                             device_id_type=MESH).wait_recv()
        for cp in sends(ins, outs, sems):
            cp.wait_send()

    out_shapes = [jax.ShapeDtypeStruct(g.shape, g.dtype) for g in chip_sums]
    scratch = [pltpu.SemaphoreType.DMA((3 * n,)), pltpu.SemaphoreType.DMA((3 * n,))]
    return list(chip_sums), out_shapes, scratch, start, finish


def _join_halves(fulls, small):
    n = len(fulls)
    hs = [f.shape[0] // 2 for f in fulls]
    rel = [(dx, dy, dc) for dx in (0, 1) for dy in (0, 1) for dc in (0, 1)][1:]

    def body(*refs):
        ins, small_in = refs[:n], refs[n]
        outs, small_out = refs[n + 1:2 * n + 1], refs[2 * n + 1]
        send_sems, recv_sems, ssend, srecv, local_sem = refs[2 * n + 2:]
        x, y, c, _ = _place()
        dev = 4 * x + 2 * y + c
        local = pltpu.make_async_copy(small_in, small_out.at[dev], local_sem)
        local.start()
        cps = []
        for k in range(n):
            mine = pl.ds(c * hs[k], hs[k])
            cp = pltpu.make_async_remote_copy(src_ref=ins[k].at[mine, :], dst_ref=outs[k].at[mine, :],
                                              send_sem=send_sems.at[k], recv_sem=recv_sems.at[k],
                                              device_id=(x, y, 1 - c), device_id_type=MESH)
            cp.start()
            cps.append(cp)
        for r, (dx, dy, dc) in enumerate(rel):
            cp = pltpu.make_async_remote_copy(src_ref=small_in, dst_ref=small_out.at[dev], send_sem=ssend.at[r],
                                              recv_sem=srecv.at[r], device_id=(x ^ dx, y ^ dy, c ^ dc),
                                              device_id_type=MESH)
            cp.start()
            cps.append(cp)
        for k in range(n):
            land = outs[k].at[pl.ds((1 - c) * hs[k], hs[k]), :]
            pltpu.make_async_remote_copy(src_ref=land, dst_ref=land, send_sem=send_sems.at[k],
                                         recv_sem=recv_sems.at[k], device_id=(x, y, c), device_id_type=MESH).wait_recv()
        for r, (dx, dy, dc) in enumerate(rel):
            land = small_out.at[4 * (x ^ dx) + 2 * (y ^ dy) + (c ^ dc)]
            pltpu.make_async_remote_copy(src_ref=land, dst_ref=land, send_sem=ssend.at[r], recv_sem=srecv.at[r],
                                         device_id=(x, y, c), device_id_type=MESH).wait_recv()
        for cp in cps:
            cp.wait_send()
        local.wait()

    return _pcall(
        body, name="join_halves", in_specs=[_ANY] * (n + 1), out_specs=[_ANY] * (n + 1),
        out_shape=[jax.ShapeDtypeStruct(f.shape, f.dtype) for f in fulls]
        + [jax.ShapeDtypeStruct((N_DEV,) + small.shape, small.dtype)],
        input_output_aliases={k: k for k in range(n)},
        scratch_shapes=[pltpu.SemaphoreType.DMA((n,)), pltpu.SemaphoreType.DMA((n,)), pltpu.SemaphoreType.DMA((7,)),
                        pltpu.SemaphoreType.DMA((7,)), pltpu.SemaphoreType.DMA],
    )(*fulls, small)


def _local_step(cfg, x2, target, norm_gain, w_my, fb, mu_g, w0, a0, k_k, k_a, r_k, ln_w, ln_b, fng, rest,
                exchange=None):
    T, D, FW, FH, RW, RH, LP, lora = cfg.T, cfg.D, cfg.FW, cfg.FH, cfg.RW, cfg.RH, cfg.LP, cfg.lora
    fb_p = jnp.pad(fb, ((0, 0), (0, LANES - FH)))
    mu = _rwkv_vec_to_my(cfg, mu_g)
    rk = r_k.reshape(1, RW)
    tm = min(1024, T)

    h = _rms_fwd(cfg, x2, norm_gain)
    if len(rest) == 2:
        u, *got = _mm("in_proj", h, w_my, "nn", F32, tm, cfg.tn, 2048, comm=rest[0])
        rest = rest[1](got)
    else:
        u = _mm("in_proj", h, w_my, "nn", F32, tm, cfg.tn, 2048)
    w2, a2, wpf, wpr, wout = rest
    w2p = jnp.pad(w2, ((0, LP - lora), (0, 0)))
    a2p = jnp.pad(a2, ((0, LP - lora), (0, 0)))
    c_cols = _fox_prep(cfg, u, fb_p)
    c_rows = c_cols[:, :FH].T.reshape(FH, 1, T)
    o, lse = _attn_fwd(cfg, u, c_rows)
    oa = _gate_a_fwd(cfg, o, u)
    prep = _rwkv_prep_fwd(cfg, u, mu, w0, w2p, a0, a2p, k_k, k_a)
    r, lw, kp, v, an, b, zb = prep
    toks = [r, lw, kp, v, an, b]
    q_s, yloc, a_m, sloc = _scan_local_fwd(cfg, toks)
    y, ckpt = _scan_carry_fwd(cfg, q_s, yloc, a_m, sloc)
    ob = _rwkv_post_fwd(cfg, y, r, kp, v, zb, ln_w, ln_b, rk)
    pa = _mm("proj_fox", oa, wpf, "nn", F32, tm, 1024, 2048)
    pb = _mm("proj_rwkv", ob, wpr, "nn", F32, tm, 1024, 2048)
    m = _merge_fwd(cfg, pa, pb, u)
    mo = _mm("out_proj", m, wout, "nn", F32, tm, 1024, 2048)
    loss8, dres, dres16, d_fng = _final(cfg, x2, mo, fng.reshape(1, D), target)

    dm = _mm("out_proj_dx", dres16, wout, "nt", F32, tm, 1024, 2048)
    d_wout = _mm("out_proj_dw", m, dres16, "tn", BF16, 1024, 1024, 2048)
    dpa, dpb, du = _merge_bwd(cfg, pa, pb, u, dm)
    doa = _mm("proj_fox_dx", dpa, wpf, "nt", F32, tm, 1024, 2048)
    d_wpf = _mm("proj_fox_dw", oa, dpa, "tn", BF16, 1024, 1024, 2048)
    dob = _mm("proj_rwkv_dx", dpb, wpr, "nt", F32, tm, 1024, 2048)
    d_wpr = _mm("proj_rwkv_dw", ob, dpb, "tn", BF16, 1024, 1024, 2048)

    do, du = _gate_a_bwd(cfg, o, u, doa, du)
    du, dcol = _attn_bwd(cfg, u, c_rows, lse, do, du)
    dc = jnp.pad(-dcol.reshape(FH, T).T, ((0, 0), (0, LANES - FH)))
    df, d_fb = _fox_prep_bwd(cfg, u, fb_p, dc)

    dy, dr_p, dk_p, dv_p, dzb, d_lnw, d_lnb, d_rk = _rwkv_post_bwd(cfg, y, r, kp, v, zb, ln_w, ln_b, rk, dob)
    dq_s, da_m, dsl = _scan_carry_bwd(cfg, q_s, a_m, ckpt, dy)
    early = dict(w_proj_fox=d_wpf, w_proj_rwkv=d_wpr, w_out=d_wout)
    res = _scan_local_bwd(cfg, toks, dq_s, dy, da_m, dsl, [dr_p, dk_p, dv_p], exchange(early) if exchange else None)
    cots, received = res[:6], list(res[6:])
    dus, d_mu, d_w0, d_w2p, d_a0, d_a2p, d_kk, d_ka = _rwkv_prep_bwd(cfg, u, mu, w0, w2p, a0, a2p, k_k, k_a, cots, dzb)
    du = _shift_bwd(cfg, dus, mu, df, du)
    d_wmy = _mm("in_proj_dw", h, du, "tn", BF16, 1024, cfg.tn, 2048)
    late = dict(w_in=d_wmy, rwkv_w2=d_w2p[:lora], rwkv_a2=d_a2p[:lora])
    tkx = 2 * cfg.tn if cfg.ncol % (2 * cfg.tn) == 0 else cfg.tn
    res = _mm("in_proj_dx", du, w_my, "nt", F32, tm, 1024, tkx, comm=exchange(late) if exchange else None)
    dh = res[0] if exchange else res
    received += list(res[1:]) if exchange else []
    big = dict(early, **late)
    gx, d_ng = _rms_bwd(cfg, x2, norm_gain, dh, dres)

    small = dict(norm_gain=d_ng, fox_forget_bias=d_fb[:, :FH], rwkv_shift_mix=_rwkv_vec_from_my(cfg, d_mu),
                 rwkv_w0=d_w0, rwkv_a0=d_a0, rwkv_k_k=d_kk, rwkv_k_a=d_ka, rwkv_r_k=d_rk, rwkv_ln_w=d_lnw,
                 rwkv_ln_b=d_lnb, final_norm_gain=d_fng)
    return loss8[0, 0], gx, small, big, received


_SMALL = ["norm_gain", "fox_forget_bias", "rwkv_shift_mix", "rwkv_w0", "rwkv_a0", "rwkv_k_k", "rwkv_k_a", "rwkv_r_k",
          "rwkv_ln_w", "rwkv_ln_b", "final_norm_gain"]
_WEIGHTS = ["norm_gain", "w_in", "fox_forget_bias", "rwkv_shift_mix", "rwkv_w0", "rwkv_w2", "rwkv_a0", "rwkv_a2",
            "rwkv_k_k", "rwkv_k_a", "rwkv_r_k", "rwkv_ln_w", "rwkv_ln_b", "w_proj_fox", "w_proj_rwkv", "w_out",
            "final_norm_gain"]


def _pack_small(arrs):
    parts = []
    for a in arrs:
        f = a.reshape(-1)
        parts.append(jnp.pad(f, (0, (-f.shape[0]) % LANES)))
    flat = jnp.concatenate(parts)
    rows = flat.shape[0] // LANES
    flat = jnp.pad(flat, (0, ((-rows) % 8) * LANES))
    return flat.reshape(-1, LANES)


def _unpack_small(packed, shapes):
    flat = packed.reshape(-1)
    out, pos = [], 0
    for s in shapes:
        n = int(np.prod(s))
        out.append(flat[pos:pos + n].reshape(s))
        pos += n + ((-n) % LANES)
    return out


def _shard_major(a, axis):
    parts = jnp.split(a, N_CHIPS, axis=axis)
    return jnp.stack(parts, axis=0)


def kernel(x, norm_gain, w_in, fox_forget_bias, rwkv_shift_mix, rwkv_w0, rwkv_w2, rwkv_a0, rwkv_a2, rwkv_k_k, rwkv_k_a, rwkv_r_k, rwkv_ln_w, rwkv_ln_b, w_proj_fox, w_proj_rwkv, w_out, final_norm_gain, loss_target, m_norm_gain, m_w_in, m_fox_forget_bias, m_rwkv_shift_mix, m_rwkv_w0, m_rwkv_w2, m_rwkv_a0, m_rwkv_a2, m_rwkv_k_k, m_rwkv_k_a, m_rwkv_r_k, m_rwkv_ln_w, m_rwkv_ln_b, m_w_proj_fox, m_w_proj_rwkv, m_w_out, m_final_norm_gain, v_norm_gain, v_w_in, v_fox_forget_bias, v_rwkv_shift_mix, v_rwkv_w0, v_rwkv_w2, v_rwkv_a0, v_rwkv_a2, v_rwkv_k_k, v_rwkv_k_a, v_rwkv_r_k, v_rwkv_ln_w, v_rwkv_ln_b, v_w_proj_fox, v_w_proj_rwkv, v_w_out, v_final_norm_gain):
    args = dict(locals())
    T, D = x.shape[1], x.shape[2]
    lora = rwkv_w2.shape[1]
    cfg = _Cfg(T, D, lora)
    RW = cfg.RW
    c_idx = lax.axis_index("c").astype(jnp.int32).reshape(1)
    me_chip = (2 * lax.axis_index("x") + lax.axis_index("y")).astype(jnp.int32)
    place = jnp.concatenate([me_chip.reshape(1), c_idx])

    w_in_s = w_in[0].astype(BF16)
    lora_s = jnp.concatenate([rwkv_w2[0], rwkv_a2[0]], axis=0)
    own_slot = lambda g, own: lax.dynamic_update_slice(g, own[None], (me_chip, 0, 0))
    w_my = _shards_to_my_layout(cfg, own_slot(_gather_weights([w_in_s])[0], w_in_s))
    mine = [_cast_bf16("cast_w_proj_fox", w_proj_fox[0]), _cast_bf16("cast_w_proj_rwkv", w_proj_rwkv[0]),
            _cast_bf16("cast_w_out", w_out[0]), lora_s]

    def unpack(gathered):
        g_wpf, g_wpr, g_out, g_lora = [own_slot(g, own) for g, own in zip(gathered, mine)]
        lo = g_lora.transpose(1, 0, 2).reshape(2 * lora, RW)
        return (lo[:lora], lo[lora:], g_wpf.transpose(1, 0, 2).reshape(RW, D),
                g_wpr.transpose(1, 0, 2).reshape(RW, D), g_out.reshape(D, D))

    names, chip_sums = [], []

    def exchange(big):
        if "w_in" in big:
            group = ["w_in", "lora"]
            gs = [_my_layout_to_shards(cfg, big["w_in"]),
                  _shard_major(jnp.concatenate([big["rwkv_w2"], big["rwkv_a2"]], axis=0).astype(BF16), 1)]
        else:
            group = ["w_proj_fox", "w_proj_rwkv", "w_out"]
            gs = [_shard_major(big["w_proj_fox"], 1), _shard_major(big["w_proj_rwkv"], 1),
                  _shard_major(big["w_out"], 0)]
        recv1 = _exchange_halves("exchange_halves_" + group[0], gs)
        sums = [_add_halves("add_halves_" + nm, g, r, c_idx) for nm, g, r in zip(group, gs, recv1)]
        names.extend(group)
        chip_sums.extend(sums)
        return _scatter_to_owners(sums)

    loss_dev, gx, small, _, recv2 = _local_step(
        cfg, x[0], loss_target[0], norm_gain, w_my, fox_forget_bias, rwkv_shift_mix, rwkv_w0, rwkv_a0, rwkv_k_k,
        rwkv_k_a, rwkv_r_k, rwkv_ln_w, rwkv_ln_b, final_norm_gain, (_gather_parts(mine), unpack), exchange)
    loss = lax.psum(loss_dev, ("x", "y", "c"))

    small_shapes = [args[nm].shape for nm in _SMALL]
    packed = _pack_small([small[nm] for nm in _SMALL])
    reduced = [_sum_chips("sum_chips_" + nm, r, own, place) for nm, r, own in zip(names, recv2, chip_sums)]
    *joined, small_all = _join_halves(reduced, packed)
    g_small = _sum_slots("sum_small", small_all)

    grads = dict(zip(_SMALL, _unpack_small(g_small, small_shapes)))
    grads.update({nm: g[None] for nm, g in zip(names, joined) if nm != "lora"})
    g_lora_f = joined[names.index("lora")]
    grads["rwkv_w2"] = g_lora_f[None, :lora]
    grads["rwkv_a2"] = g_lora_f[None, lora:]

    delta, new_m, new_v = {}, {}, {}
    w_small = _pack_small([args[nm] for nm in _SMALL])
    m_small = _pack_small([args["m_" + nm] for nm in _SMALL])
    v_small = _pack_small([args["v_" + nm] for nm in _SMALL])
    d_s, m_s, v_s = _adamw("adamw_small", w_small, g_small, m_small, v_small)
    for tgt, pk in ((delta, d_s), (new_m, m_s), (new_v, v_s)):
        tgt.update(zip(_SMALL, _unpack_small(pk, small_shapes)))
    for nm in ("w_in", "w_proj_fox", "w_proj_rwkv", "w_out", "rwkv_w2", "rwkv_a2"):
        shp = args[nm].shape
        two_d = (shp[1], shp[2])
        d_b, m_b, v_b = _adamw("adamw_" + nm, args[nm].reshape(two_d), grads[nm].reshape(two_d),
                               args["m_" + nm].reshape(two_d), args["v_" + nm].reshape(two_d))
        delta[nm], new_m[nm], new_v[nm] = d_b.reshape(shp), m_b.reshape(shp), v_b.reshape(shp)

    return (loss, gx[None], *[grads[n] for n in _WEIGHTS], *[delta[n] for n in _WEIGHTS],
            *[new_m[n] for n in _WEIGHTS], *[new_v[n] for n in _WEIGHTS])
```

```python
import functools

import numpy as np
import jax
import jax.numpy as jnp
from jax import lax
from jax.experimental import pallas as pl
from jax.experimental.pallas import tpu as pltpu

F32 = jnp.float32
BF16 = jnp.bfloat16
HI = lax.Precision.HIGHEST
MESH = pl.DeviceIdType.MESH

FOX_HEAD_DIM = 128
RWKV_HEAD_DIM = 64
RMS_EPS = 1e-6
GN_EPS = 64e-5
L2_EPS = 1e-12
ADAM_LR = 0.001
ADAM_B1 = 0.9
ADAM_B2 = 0.999
ADAM_EPS = 1e-08
ADAM_WD = 0.01
ADAM_STEP = 10

LANES = 128
VMEM_LIMIT = 56 * 1024 * 1024
SCAN_CHUNK = 64
SCAN_HEADS_PER_STEP = 16
SCAN_PASSES = (3, 1, 1)
N_CHIPS = 4
N_DEV = 8

_pcall = pl.pallas_call


def _cparams(sem=None):
    return pltpu.CompilerParams(dimension_semantics=sem, vmem_limit_bytes=VMEM_LIMIT)


def _softplus(x):
    return jnp.maximum(x, 0.0) + jnp.log(1.0 + jnp.exp(-jnp.abs(x)))


def _silu(z):
    return z * jax.nn.sigmoid(z)


def _rmsn(x, g):
    return x * lax.rsqrt(jnp.mean(x * x, axis=-1, keepdims=True) + RMS_EPS) * g


def _dot(a, b, dims="nn", precision=None):
    dn = {"nn": (((1,), (0,)), ((), ())), "nt": (((1,), (1,)), ((), ())), "tn": (((0,), (0,)), ((), ()))}[dims]
    return lax.dot_general(a, b, dn, precision=precision, preferred_element_type=F32)


def _split_bf16(x):
    hi = x.astype(BF16)
    return hi, (x - hi.astype(F32)).astype(BF16)


def _bdot_raw(a, b, ca, cb, passes):
    dn = (((ca,), (cb,)), ((0,), (0,)))
    mm = lambda p, q: lax.dot_general(p, q, dn, preferred_element_type=F32)
    if passes == 1:
        return mm(a.astype(BF16), b.astype(BF16))
    ah, al = _split_bf16(a)
    bh, bl = _split_bf16(b)
    return mm(ah, bh) + (mm(ah, bl) + mm(al, bh))


@functools.partial(jax.custom_vjp, nondiff_argnums=(2, 3, 4))
def _bdot_p(a, b, ca, cb, passes):
    return _bdot_raw(a, b, ca, cb, passes)


def _bdot_fwd(a, b, ca, cb, passes):
    return _bdot_raw(a, b, ca, cb, passes), (a, b)


def _bdot_bwd(ca, cb, passes, res, g):
    a, b = res
    if (ca, cb) == (2, 1):
        return _bdot_p(g, b, 2, 2, passes), _bdot_p(a, g, 1, 1, passes)
    if (ca, cb) == (2, 2):
        return _bdot_p(g, b, 2, 1, passes), _bdot_p(g, a, 1, 1, passes)
    assert (ca, cb) == (1, 1)
    return _bdot_p(b, g, 2, 2, passes), _bdot_p(a, g, 2, 1, passes)


_bdot_p.defvjp(_bdot_fwd, _bdot_bwd)


def _bdot(a, b, ca, cb, passes=3):
    return _bdot_p(a, b, ca, cb, passes)


def _dot3(a, b):
    return _bdot(a[None], b[None], 2, 1)[0]


@jax.custom_vjp
def _xdot(x, m, mt):
    hi, lo = _split_bf16(x)
    m16 = m.astype(BF16)
    return _dot(hi, m16) + _dot(lo, m16)


def _xdot_fwd(x, m, mt):
    return _xdot(x, m, mt), (m, mt)


def _xdot_bwd(res, g):
    m, mt = res
    return _xdot(g, mt, m), jnp.zeros_like(m), jnp.zeros_like(mt)


_xdot.defvjp(_xdot_fwd, _xdot_bwd)


class _Cfg:
    def __init__(self, T, D, lora):
        self.T, self.D, self.lora = T, D, lora
        self.FW = D // 2
        self.FH = self.FW // FOX_HEAD_DIM
        self.RW = D // 2
        self.RH = self.RW // RWKV_HEAD_DIM
        self.LP = -(-lora // LANES) * LANES
        self.o_fox = 0
        self.o_rwkv = 4 * self.FW
        self.o_gate = self.o_rwkv + 4 * self.RW
        self.o_f = self.o_gate + 2 * D
        self.o_wd = self.o_f + LANES
        self.o_ad = self.o_wd + self.LP
        end = self.o_ad + self.LP
        self.tn = 1280 if D >= 2048 else LANES
        self.ncol = -(-end // self.tn) * self.tn
        self.in_cols = 4 * self.FW + self.FH + 4 * self.RW + 2 * lora + 2 * D
        self.scp = -(-(self.in_cols // N_CHIPS) // LANES) * LANES
        self.rseg = 4 * self.RW + 2 * self.LP
        self.C = min(SCAN_CHUNK, T)
        self.tr = min(256, T)
        self.hb = min(SCAN_HEADS_PER_STEP, self.RH)

    def segments(self):
        FW, FH, RW, lo, D = self.FW, self.FH, self.RW, self.lora, self.D
        g_f = 4 * FW
        g_r = g_f + FH
        g_wd = g_r + 4 * RW
        g_ad = g_wd + lo
        g_g = g_ad + lo
        dh = FOX_HEAD_DIM
        qkv = [(j * FW + h * dh, dh, (3 * h + j) * dh) for h in range(FH) for j in range(3)]
        return qkv + [(3 * FW, FW, 3 * FW), (g_f, FH, self.o_f), (g_r, 4 * RW, self.o_rwkv), (g_wd, lo, self.o_wd),
                      (g_ad, lo, self.o_ad), (g_g, 2 * D, self.o_gate)]


def _to_my_layout(cfg, wg):
    R = wg.shape[0]
    segs = sorted(cfg.segments(), key=lambda s: s[2])
    parts, pos = [], 0
    for g0, w, m0 in segs:
        if m0 > pos:
            parts.append(jnp.zeros((R, m0 - pos), wg.dtype))
        parts.append(wg[:, g0:g0 + w])
        pos = m0 + w
    if cfg.ncol > pos:
        parts.append(jnp.zeros((R, cfg.ncol - pos), wg.dtype))
    return jnp.concatenate(parts, axis=1)


def _from_my_layout(cfg, wm):
    segs = sorted(cfg.segments(), key=lambda s: s[0])
    return jnp.concatenate([wm[:, m0:m0 + w] for g0, w, m0 in segs], axis=1)


def _shards_to_my_layout(cfg, g):
    R, sc = g.shape[1], g.shape[2]
    segs = sorted(cfg.segments(), key=lambda s: s[2])
    parts, pos = [], 0
    for g0, w, m0 in segs:
        if m0 > pos:
            parts.append(jnp.zeros((R, m0 - pos), g.dtype))
        for s in range(N_CHIPS):
            lo, hi = max(g0, s * sc), min(g0 + w, (s + 1) * sc)
            if lo < hi:
                parts.append(g[s, :, lo - s * sc:hi - s * sc])
        pos = m0 + w
    if cfg.ncol > pos:
        parts.append(jnp.zeros((R, cfg.ncol - pos), g.dtype))
    return jnp.concatenate(parts, axis=1)


def _my_layout_to_shards(cfg, wm):
    sc = cfg.in_cols // N_CHIPS
    segs = sorted(cfg.segments(), key=lambda s: s[0])
    shards = []
    for s in range(N_CHIPS):
        parts = []
        for g0, w, m0 in segs:
            lo, hi = max(g0, s * sc), min(g0 + w, (s + 1) * sc)
            if lo < hi:
                parts.append(wm[:, m0 + lo - g0:m0 + hi - g0])
        shards.append(jnp.concatenate(parts, axis=1))
    return jnp.stack(shards, axis=0)


def _rwkv_vec_to_my(cfg, v):
    RW4, lo, LP = 4 * cfg.RW, cfg.lora, cfg.LP
    z = jnp.zeros((1, LP - lo), v.dtype)
    return jnp.concatenate([v[:, :RW4], v[:, RW4:RW4 + lo], z, v[:, RW4 + lo:], z], axis=1)


def _rwkv_vec_from_my(cfg, v):
    RW4, lo, LP = 4 * cfg.RW, cfg.lora, cfg.LP
    return jnp.concatenate([v[:, :RW4], v[:, RW4:RW4 + lo], v[:, RW4 + LP:RW4 + LP + lo]], axis=1)


def _comm_at(comm, which, steps, cin, cout, scr):
    if not comm or len(comm) <= which:
        return
    lin, total = 0, 1
    for d, n in enumerate(steps):
        lin = lin * n + pl.program_id(d)
        total *= n
    pl.when(lin == {3: 0, 4: total - 1, 5: total // 2}[which])(lambda: comm[which](cin, cout, scr))


def _mm(name, a, b, dims, out_dtype, tm, tn, tk, comm=None):
    (M, K) = a.shape if dims != "tn" else a.shape[::-1]
    N = b.shape[0] if dims == "nt" else b.shape[1]
    tm, tn, tk = min(tm, M), min(tn, N), min(tk, K)
    assert M % tm == 0 and N % tn == 0 and K % tk == 0, (name, M, N, K, tm, tn, tk)
    nk = K // tk
    steps = (M // tm, N // tn, nk)
    c_in, c_out, c_scr = comm[:3] if comm else ([], [], [])
    if dims == "nn":
        a_spec = pl.BlockSpec((tm, tk), lambda i, j, k: (i, k))
        b_spec = pl.BlockSpec((tk, tn), lambda i, j, k: (k, j))
    elif dims == "nt":
        a_spec = pl.BlockSpec((tm, tk), lambda i, j, k: (i, k))
        b_spec = pl.BlockSpec((tn, tk), lambda i, j, k: (j, k))
    else:
        a_spec = pl.BlockSpec((tk, tm), lambda i, j, k: (k, i))
        b_spec = pl.BlockSpec((tk, tn), lambda i, j, k: (k, j))

    n_acc = 1 if nk > 1 else 0

    def body(a_ref, b_ref, *rest):
        cin, o_ref = rest[:len(c_in)], rest[len(c_in)]
        cout = rest[len(c_in) + 1:len(c_in) + 1 + len(c_out)]
        scr = rest[len(c_in) + 1 + len(c_out):]
        _comm_at(comm, 3, steps, cin, cout, scr[n_acc:])
        if nk == 1:
            o_ref[...] = _dot(a_ref[...], b_ref[...], dims).astype(o_ref.dtype)
        else:
            acc_ref, k = scr[0], pl.program_id(2)

            @pl.when(k == 0)
            def _():
                acc_ref[...] = jnp.zeros_like(acc_ref)

            acc_ref[...] += _dot(a_ref[...], b_ref[...], dims)

            @pl.when(k == nk - 1)
            def _():
                o_ref[...] = acc_ref[...].astype(o_ref.dtype)

        _comm_at(comm, 5, steps, cin, cout, scr[n_acc:])
        _comm_at(comm, 4, steps, cin, cout, scr[n_acc:])

    res = _pcall(
        body, name=name, grid=steps,
        in_specs=[a_spec, b_spec] + [_ANY] * len(c_in),
        out_specs=[pl.BlockSpec((tm, tn), lambda i, j, k: (i, j))] + [_ANY] * len(c_out),
        out_shape=[jax.ShapeDtypeStruct((M, N), out_dtype)] + list(c_out),
        scratch_shapes=([pltpu.VMEM((tm, tn), F32)] if nk > 1 else []) + list(c_scr),
        compiler_params=_cparams(("arbitrary",) * 3 if comm else ("parallel", "parallel", "arbitrary")),
    )(a, b, *c_in)
    return res if comm else res[0]


def _tile(tr, w, cb=0):
    return pl.BlockSpec((tr, w), lambda i: (i, cb))


def _const(shape):
    nd = len(shape)
    return pl.BlockSpec(shape, lambda i: (0,) * nd)


def _acc_store(i, ref, val):
    @pl.when(i == 0)
    def _():
        ref[...] = val

    @pl.when(i > 0)
    def _():
        ref[...] += val


def _rms_fwd(cfg, x2, g):
    T, D, tr = cfg.T, cfg.D, cfg.tr

    def body(x_ref, g_ref, h_ref):
        h_ref[...] = _rmsn(x_ref[...], g_ref[...]).astype(BF16)

    return _pcall(body, name="rms_fwd", grid=(T // tr,), in_specs=[_tile(tr, D), _const((1, D))],
                  out_specs=_tile(tr, D), out_shape=jax.ShapeDtypeStruct((T, D), BF16),
                  compiler_params=_cparams(("parallel",)))(x2, g)


def _rms_bwd(cfg, x2, g, dh, dres, comm=None):
    T, D, tr = cfg.T, cfg.D, cfg.tr
    c_in, c_out, c_scr = comm[:3] if comm else ([], [], [])
    steps = (T // tr,)

    def body(x_ref, g_ref, dh_ref, dres_ref, *rest):
        cin, (gx_ref, dg_ref) = rest[:len(c_in)], rest[len(c_in):len(c_in) + 2]
        cout, scr = rest[len(c_in) + 2:len(c_in) + 2 + len(c_out)], rest[len(c_in) + 2 + len(c_out):]
        _comm_at(comm, 3, steps, cin, cout, scr)
        _, vjp = jax.vjp(_rmsn, x_ref[...], g_ref[...])
        dx, dg = vjp(dh_ref[...])
        gx_ref[...] = dx + dres_ref[...]
        _acc_store(pl.program_id(0), dg_ref, dg)
        _comm_at(comm, 4, steps, cin, cout, scr)

    return _pcall(body, name="rms_bwd", grid=steps,
                  in_specs=[_tile(tr, D), _const((1, D)), _tile(tr, D), _tile(tr, D)] + [_ANY] * len(c_in),
                  out_specs=[_tile(tr, D), _const((1, D))] + [_ANY] * len(c_out),
                  out_shape=[jax.ShapeDtypeStruct((T, D), F32), jax.ShapeDtypeStruct((1, D), F32)] + list(c_out),
                  scratch_shapes=list(c_scr), compiler_params=_cparams(("arbitrary",)))(x2, g, dh, dres, *c_in)


def _final(cfg, x2, mo, fg, target):
    T, D, tr = cfg.T, cfg.D, cfg.tr

    def loss_fn(hres, g, tgt):
        err = _rmsn(hres, g) - tgt
        return 0.5 * jnp.sum(jnp.mean(err * err, axis=-1, keepdims=True), axis=0, keepdims=True)

    def body(x_ref, mo_ref, g_ref, t_ref, loss_ref, dres_ref, dres16_ref, dg_ref):
        hres = x_ref[...] + mo_ref[...]
        loss, vjp = jax.vjp(functools.partial(loss_fn, tgt=t_ref[...]), hres, g_ref[...])
        dres, dg = vjp(jnp.ones((1, 1), F32))
        dres_ref[...] = dres
        dres16_ref[...] = dres.astype(BF16)
        i = pl.program_id(0)
        _acc_store(i, dg_ref, dg)
        _acc_store(i, loss_ref, jnp.broadcast_to(loss, (8, LANES)))

    return _pcall(body, name="final_loss", grid=(T // tr,),
                  in_specs=[_tile(tr, D), _tile(tr, D), _const((1, D)), _tile(tr, D)],
                  out_specs=[_const((8, LANES)), _tile(tr, D), _tile(tr, D), _const((1, D))],
                  out_shape=[jax.ShapeDtypeStruct((8, LANES), F32), jax.ShapeDtypeStruct((T, D), F32),
                             jax.ShapeDtypeStruct((T, D), BF16), jax.ShapeDtypeStruct((1, D), F32)],
                  compiler_params=_cparams(("arbitrary",)))(x2, mo, fg, target)


def _merge_fn(pa, pb, ga, gb):
    return jax.nn.sigmoid(ga) * pa + jax.nn.sigmoid(gb) * pb


def _merge_fwd(cfg, pa, pb, u):
    T, D, tr = cfg.T, cfg.D, cfg.tr
    cga, cgb = cfg.o_gate // D, cfg.o_gate // D + 1

    def body(pa_ref, pb_ref, ga_ref, gb_ref, m_ref):
        m_ref[...] = _merge_fn(pa_ref[...], pb_ref[...], ga_ref[...], gb_ref[...]).astype(BF16)

    return _pcall(body, name="merge_fwd", grid=(T // tr,),
                  in_specs=[_tile(tr, D), _tile(tr, D), _tile(tr, D, cga), _tile(tr, D, cgb)],
                  out_specs=_tile(tr, D), out_shape=jax.ShapeDtypeStruct((T, D), BF16),
                  compiler_params=_cparams(("parallel",)))(pa, pb, u, u)


def _merge_bwd(cfg, pa, pb, u, dm):
    T, D, tr = cfg.T, cfg.D, cfg.tr
    cga, cgb = cfg.o_gate // D, cfg.o_gate // D + 1

    def body(pa_ref, pb_ref, ga_ref, gb_ref, dm_ref, dpa_ref, dpb_ref, dg_ref):
        _, vjp = jax.vjp(_merge_fn, pa_ref[...], pb_ref[...], ga_ref[...], gb_ref[...])
        dpa, dpb, dga, dgb = vjp(dm_ref[...])
        dpa_ref[...] = dpa.astype(BF16)
        dpb_ref[...] = dpb.astype(BF16)
        dg_ref[:, :D] = dga.astype(BF16)
        dg_ref[:, D:] = dgb.astype(BF16)

    return _pcall(body, name="merge_bwd", grid=(T // tr,),
                  in_specs=[_tile(tr, D), _tile(tr, D), _tile(tr, D, cga), _tile(tr, D, cgb), _tile(tr, D)],
                  out_specs=[_tile(tr, D), _tile(tr, D), _tile(tr, 2 * D, cfg.o_gate // (2 * D))],
                  out_shape=[jax.ShapeDtypeStruct((T, D), BF16), jax.ShapeDtypeStruct((T, D), BF16),
                             jax.ShapeDtypeStruct((T, cfg.ncol), BF16)],
                  compiler_params=_cparams(("parallel",)))(pa, pb, u, u, dm)


def _gate_fn(o, z):
    return o * _silu(z)


def _gate_a_fwd(cfg, o, u):
    T, FW, tr = cfg.T, cfg.FW, cfg.tr

    def body(o_ref, z_ref, oa_ref):
        oa_ref[...] = _gate_fn(o_ref[...], z_ref[...]).astype(BF16)

    return _pcall(body, name="gate_a_fwd", grid=(T // tr,), in_specs=[_tile(tr, FW), _tile(tr, FW, 3)],
                  out_specs=_tile(tr, FW), out_shape=jax.ShapeDtypeStruct((T, FW), BF16),
                  compiler_params=_cparams(("parallel",)))(o, u)


def _gate_a_bwd(cfg, o, u, doa, du):
    T, FW, tr = cfg.T, cfg.FW, cfg.tr

    def body(o_ref, z_ref, doa_ref, du_in, do_ref, dz_ref):
        _, vjp = jax.vjp(_gate_fn, o_ref[...], z_ref[...])
        do, dz = vjp(doa_ref[...])
        do_ref[...] = do
        dz_ref[...] = dz.astype(BF16)

    return _pcall(body, name="gate_a_bwd", grid=(T // tr,),
                  in_specs=[_tile(tr, FW), _tile(tr, FW, 3), _tile(tr, FW), _ANY],
                  out_specs=[_tile(tr, FW), _tile(tr, FW, 3)],
                  out_shape=[jax.ShapeDtypeStruct((T, FW), F32), jax.ShapeDtypeStruct(du.shape, BF16)],
                  input_output_aliases={3: 1},
                  compiler_params=_cparams(("parallel",)))(o, u, doa, du)


def _fox_prep(cfg, u, fb):
    T, tr = cfg.T, cfg.tr
    cf = cfg.o_f // LANES

    def body(f_ref, fb_ref, c_ref, carry_ref):
        i = pl.program_id(0)

        @pl.when(i == 0)
        def _():
            carry_ref[...] = jnp.zeros_like(carry_ref)

        lf = -_softplus(-(f_ref[...] + fb_ref[...]))
        r = lax.broadcasted_iota(jnp.int32, (tr, tr), 0)
        c = lax.broadcasted_iota(jnp.int32, (tr, tr), 1)
        tri = (r >= c).astype(F32)
        c_ref[...] = _dot(tri, lf, precision=HI) + carry_ref[...]
        carry_ref[...] += jnp.sum(lf, axis=0, keepdims=True)

    return _pcall(body, name="fox_prep", grid=(T // tr,), in_specs=[_tile(tr, LANES, cf), _const((1, LANES))],
                  out_specs=_tile(tr, LANES), out_shape=jax.ShapeDtypeStruct((T, LANES), F32),
                  scratch_shapes=[pltpu.VMEM((1, LANES), F32)], compiler_params=_cparams(("arbitrary",)))(u, fb)


def _fox_prep_bwd(cfg, u, fb, dc):
    T, tr = cfg.T, cfg.tr
    cf = cfg.o_f // LANES
    nb = T // tr

    def body(f_ref, fb_ref, dc_ref, df_ref, dfb_ref, carry_ref):
        i = pl.program_id(0)

        @pl.when(i == 0)
        def _():
            carry_ref[...] = jnp.zeros_like(carry_ref)

        dc = dc_ref[...]
        r = lax.broadcasted_iota(jnp.int32, (tr, tr), 0)
        c = lax.broadcasted_iota(jnp.int32, (tr, tr), 1)
        triu = (r <= c).astype(F32)
        dlf = _dot(triu, dc, precision=HI) + carry_ref[...]
        carry_ref[...] += jnp.sum(dc, axis=0, keepdims=True)
        dz = dlf * jax.nn.sigmoid(-(f_ref[...] + fb_ref[...]))
        df_ref[...] = dz.astype(BF16)
        _acc_store(i, dfb_ref, jnp.sum(dz, axis=0, keepdims=True))

    rev = lambda i: (nb - 1 - i, 0)
    return _pcall(body, name="fox_prep_bwd", grid=(nb,),
                  in_specs=[pl.BlockSpec((tr, LANES), lambda i: (nb - 1 - i, cf)), _const((1, LANES)),
                            pl.BlockSpec((tr, LANES), rev)],
                  out_specs=[pl.BlockSpec((tr, LANES), rev), _const((1, LANES))],
                  out_shape=[jax.ShapeDtypeStruct((T, LANES), BF16), jax.ShapeDtypeStruct((1, LANES), F32)],
                  scratch_shapes=[pltpu.VMEM((1, LANES), F32)], compiler_params=_cparams(("arbitrary",)))(u, fb, dc)


def _attn_logits(q_ref, k_ref, c_ref, i, tq, te):
    s = _dot(q_ref[...].astype(BF16), k_ref[0:te, :].astype(BF16), "nt") * (FOX_HEAD_DIM ** -0.5) - c_ref[0, :, 0:te]
    row = i * tq + lax.broadcasted_iota(jnp.int32, (tq, te), 0)
    col = lax.broadcasted_iota(jnp.int32, (tq, te), 1)
    return jnp.where(col <= row, s, -1e30)


def _per_query_tile(i, nq, tq, fn):
    for ii in range(nq):
        pl.when(i == ii)(functools.partial(fn, (ii + 1) * tq))


def _attn_fwd(cfg, u, c_rows):
    T, FW, FH = cfg.T, cfg.FW, cfg.FH
    tq = min(256, T)
    dh = FOX_HEAD_DIM

    def body(q_ref, k_ref, v_ref, c_ref, o_ref, lse_ref):
        i = pl.program_id(1)

        def tile(te):
            s = _attn_logits(q_ref, k_ref, c_ref, i, tq, te)
            m = jnp.max(s, axis=1, keepdims=True)
            p = jnp.exp(s - m)
            l = jnp.sum(p, axis=1, keepdims=True)
            o_ref[...] = _dot(p.astype(BF16), v_ref[0:te, :].astype(BF16)) / l
            lse_ref[0] = m + jnp.log(l)

        _per_query_tile(i, T // tq, tq, tile)

    return _pcall(
        body, name="fox_attn_fwd", grid=(FH, T // tq),
        in_specs=[pl.BlockSpec((tq, dh), lambda h, i: (i, 3 * h)), pl.BlockSpec((T, dh), lambda h, i: (0, 3 * h + 1)),
                  pl.BlockSpec((T, dh), lambda h, i: (0, 3 * h + 2)), pl.BlockSpec((1, 1, T), lambda h, i: (h, 0, 0))],
        out_specs=[pl.BlockSpec((tq, dh), lambda h, i: (i, h)), pl.BlockSpec((1, tq, 1), lambda h, i: (h, i, 0))],
        out_shape=[jax.ShapeDtypeStruct((T, FW), F32), jax.ShapeDtypeStruct((FH, T, 1), F32)],
        compiler_params=_cparams(("parallel", "arbitrary")),
    )(u, u, u, c_rows)


def _attn_bwd(cfg, u, c_rows, lse, do, du):
    T, FW, FH = cfg.T, cfg.FW, cfg.FH
    tq = min(256, T)
    nq = T // tq
    dh = FOX_HEAD_DIM
    scale = dh ** -0.5

    def body(q_ref, k_ref, v_ref, c_ref, lse_ref, do_ref, du_in, du_ref, dcol_ref, dk_acc, dv_acc):
        i = pl.program_id(1)

        @pl.when(i == 0)
        def _():
            dk_acc[...] = jnp.zeros_like(dk_acc)
            dv_acc[...] = jnp.zeros_like(dv_acc)
            dcol_ref[...] = jnp.zeros_like(dcol_ref)

        def tile(te):
            s = _attn_logits(q_ref, k_ref, c_ref, i, tq, te)
            p = jnp.exp(s - lse_ref[0])
            do_v = do_ref[...]
            dp = _dot(do_v.astype(BF16), v_ref[0:te, :].astype(BF16), "nt")
            delta = jnp.sum(p * dp, axis=1, keepdims=True)
            ds = p * (dp - delta)
            ds16 = ds.astype(BF16)
            du_ref[te - tq:te, 0:dh] = (_dot(ds16, k_ref[0:te, :].astype(BF16)) * scale).astype(BF16)
            dk_acc[0:te, :] += _dot(ds16, q_ref[...].astype(BF16), "tn") * scale
            dv_acc[0:te, :] += _dot(p.astype(BF16), do_v.astype(BF16), "tn")
            dcol_ref[0, :, 0:te] += jnp.sum(ds, axis=0, keepdims=True)

        _per_query_tile(i, nq, tq, tile)

        @pl.when(i == nq - 1)
        def _():
            du_ref[:, dh:2 * dh] = dk_acc[...].astype(BF16)
            du_ref[:, 2 * dh:3 * dh] = dv_acc[...].astype(BF16)

    return _pcall(
        body, name="fox_attn_bwd", grid=(FH, nq),
        in_specs=[pl.BlockSpec((tq, dh), lambda h, i: (i, 3 * h)), pl.BlockSpec((T, dh), lambda h, i: (0, 3 * h + 1)),
                  pl.BlockSpec((T, dh), lambda h, i: (0, 3 * h + 2)), pl.BlockSpec((1, 1, T), lambda h, i: (h, 0, 0)),
                  pl.BlockSpec((1, tq, 1), lambda h, i: (h, i, 0)), pl.BlockSpec((tq, dh), lambda h, i: (i, h)), _ANY],
        out_specs=[pl.BlockSpec((T, 3 * dh), lambda h, i: (0, h)), pl.BlockSpec((1, 1, T), lambda h, i: (h, 0, 0))],
        out_shape=[jax.ShapeDtypeStruct(du.shape, BF16), jax.ShapeDtypeStruct((FH, 1, T), F32)],
        scratch_shapes=[pltpu.VMEM((T, dh), F32), pltpu.VMEM((T, dh), F32)],
        input_output_aliases={6: 0},
        compiler_params=_cparams(("parallel", "arbitrary")),
    )(u, u, u, c_rows, lse, do, du)


def _head_indicators(cfg):
    ind = np.zeros((cfg.RW, LANES), np.float32)
    ind[np.arange(cfg.RW), np.arange(cfg.RW) // RWKV_HEAD_DIM] = 1.0
    pad = np.zeros((1, LANES), np.float32)
    pad[0, cfg.RH:] = 1.0
    return jnp.asarray(ind), jnp.asarray(ind.T.copy()), jnp.asarray(pad)


def _prep_fn(us_r, us_k, us_v, us_wd, us_ad, w0, w2p, a0, a2p, k_k, k_a, ind, ind_t, pad):
    wpre = w0 + _dot3(jnp.tanh(us_wd), w2p)
    w = -_softplus(-wpre) - 0.5
    lw = -jnp.exp(w)
    a = jax.nn.sigmoid(a0 + _dot3(us_ad, a2p))
    kk = us_k * k_k
    ss = _xdot(kk * kk, ind, ind_t) + pad
    inv = 1.0 / jnp.maximum(jnp.sqrt(ss), L2_EPS)
    kkn = kk * _xdot(inv, ind_t, ind)
    kp = us_k * (1.0 + (a - 1.0) * k_a)
    return us_r, lw, kp, us_v, -kkn, kkn * a


def _shifted(u, prev_row, mu, first):
    n = u.shape[0]
    rolled = pltpu.roll(u, 1, 0)
    row = lax.broadcasted_iota(jnp.int32, u.shape, 0)
    p0 = jnp.where(first, jnp.zeros_like(prev_row), prev_row)
    prev = jnp.where(row == 0, jnp.broadcast_to(p0, u.shape), rolled)
    return u + (prev - u) * mu, prev


def _rwkv_specs(cfg, tr):
    RW, LP = cfg.RW, cfg.LP
    base = cfg.o_rwkv // RW
    cols = [(RW, base), (RW, base + 1), (RW, base + 2), (RW, base + 3), (LP, cfg.o_wd // LP), (LP, cfg.o_ad // LP)]
    cur = [pl.BlockSpec((tr, w), (lambda i, cb=cb: (i, cb))) for w, cb in cols]
    prv = [pl.BlockSpec((8, w), (lambda i, cb=cb: (jnp.maximum(i * (tr // 8) - 1, 0), cb))) for w, cb in cols]
    return cols, cur, prv


def _mu_pieces(cfg, mu_ref):
    RW, LP = cfg.RW, cfg.LP
    offs = [0, RW, 2 * RW, 3 * RW, 4 * RW, 4 * RW + LP, 4 * RW + 2 * LP]
    return [mu_ref[:, offs[j]:offs[j + 1]] for j in range(6)]


def _rwkv_prep_fwd(cfg, u, mu, w0, w2p, a0, a2p, k_k, k_a):
    T, RW, LP, tr = cfg.T, cfg.RW, cfg.LP, cfg.tr
    ind, ind_t, pad = _head_indicators(cfg)
    cols, cur, prv = _rwkv_specs(cfg, tr)

    def body(*refs):
        u_refs, p_refs = refs[0:6], refs[6:12]
        mu_ref, w0_ref, w2_ref, a0_ref, a2_ref, kk_ref, ka_ref, ind_ref, indt_ref, pad_ref = refs[12:22]
        outs = refs[22:]
        first = pl.program_id(0) == 0
        mus = _mu_pieces(cfg, mu_ref)
        us = [_shifted(u_refs[j][...], p_refs[j][7:8, :], mus[j], first)[0] for j in range(6)]
        res = _prep_fn(us[0], us[1], us[2], us[4], us[5], w0_ref[...], w2_ref[...], a0_ref[...], a2_ref[...],
                       kk_ref[...], ka_ref[...], ind_ref[...], indt_ref[...], pad_ref[...])
        for j in range(6):
            outs[j][...] = res[j]
        outs[6][...] = us[3]

    consts = [mu, w0, w2p, a0, a2p, k_k, k_a, ind, ind_t, pad]
    return _pcall(body, name="rwkv_prep_fwd", grid=(T // tr,),
                  in_specs=cur + prv + [_const(c.shape) for c in consts],
                  out_specs=[_tile(tr, RW)] * 7, out_shape=[jax.ShapeDtypeStruct((T, RW), F32)] * 7,
                  compiler_params=_cparams(("parallel",)))(*([u] * 12), *consts)


def _rwkv_prep_bwd(cfg, u, mu, w0, w2p, a0, a2p, k_k, k_a, cots, dzb):
    T, RW, LP = cfg.T, cfg.RW, cfg.LP
    tr = min(128, T)
    ind, ind_t, pad = _head_indicators(cfg)
    cols, cur, prv = _rwkv_specs(cfg, tr)
    rseg = cfg.rseg

    def body(*refs):
        u_refs, p_refs = refs[0:6], refs[6:12]
        mu_ref, w0_ref, w2_ref, a0_ref, a2_ref, kk_ref, ka_ref, ind_ref, indt_ref, pad_ref = refs[12:22]
        cot_refs, dzb_ref = refs[22:28], refs[28]
        dus_ref, dmu_ref, dw0_ref, dw2_ref, da0_ref, da2_ref, dkk_ref, dka_ref = refs[29:]
        i = pl.program_id(0)
        first = i == 0
        mus = _mu_pieces(cfg, mu_ref)
        sh = [_shifted(u_refs[j][...], p_refs[j][7:8, :], mus[j], first) for j in range(6)]
        us = [s[0] for s in sh]
        fn = functools.partial(_prep_fn, ind=ind_ref[...], ind_t=indt_ref[...], pad=pad_ref[...])
        _, vjp = jax.vjp(fn, us[0], us[1], us[2], us[4], us[5], w0_ref[...], w2_ref[...], a0_ref[...], a2_ref[...],
                         kk_ref[...], ka_ref[...])
        d = vjp(tuple(c[...] for c in cot_refs))
        dus = [d[0], d[1], d[2], dzb_ref[...], d[3], d[4]]
        offs = [0, RW, 2 * RW, 3 * RW, 4 * RW, 4 * RW + LP, 4 * RW + 2 * LP]
        for j in range(6):
            dus_ref[:, offs[j]:offs[j + 1]] = dus[j]
            dmu_j = jnp.sum(dus[j] * (sh[j][1] - u_refs[j][...]), axis=0, keepdims=True)

            @pl.when(first)
            def _(j=j, dmu_j=dmu_j):
                dmu_ref[:, offs[j]:offs[j + 1]] = dmu_j

            @pl.when(i > 0)
            def _(j=j, dmu_j=dmu_j):
                dmu_ref[:, offs[j]:offs[j + 1]] += dmu_j
        for ref, val in zip((dw0_ref, dw2_ref, da0_ref, da2_ref, dkk_ref, dka_ref), d[5:11]):
            _acc_store(i, ref, val)

    consts = [mu, w0, w2p, a0, a2p, k_k, k_a, ind, ind_t, pad]
    vec = jax.ShapeDtypeStruct((1, RW), F32)
    mat = jax.ShapeDtypeStruct((LP, RW), F32)
    return _pcall(body, name="rwkv_prep_bwd", grid=(T // tr,),
                  in_specs=cur + prv + [_const(c.shape) for c in consts] + [_tile(tr, RW)] * 7,
                  out_specs=[_tile(tr, rseg), _const((1, rseg)), _const((1, RW)), _const((LP, RW)), _const((1, RW)),
                             _const((LP, RW)), _const((1, RW)), _const((1, RW))],
                  out_shape=[jax.ShapeDtypeStruct((T, rseg), F32), jax.ShapeDtypeStruct((1, rseg), F32),
                             vec, mat, vec, mat, vec, vec],
                  compiler_params=_cparams(("arbitrary",)))(*([u] * 12), *consts, *cots, dzb)


def _shift_bwd(cfg, dus, mu, df, du):
    T, tr, RW, LP = cfg.T, cfg.tr, cfg.RW, cfg.LP
    nb = T // tr
    tail = cfg.ncol - cfg.o_f
    assert cfg.o_rwkv % (4 * RW) == 0 and (4 * RW) % (2 * LP) == 0 and cfg.o_f % tail == 0

    def shifted(d_ref, n_ref, mu_ref):
        d = d_ref[...]
        rolled = pltpu.roll(d, tr - 1, 0)
        row = lax.broadcasted_iota(jnp.int32, d.shape, 0)
        n0 = jnp.where(pl.program_id(0) == nb - 1, jnp.zeros_like(n_ref[0:1, :]), n_ref[0:1, :])
        nxt = jnp.where(row == tr - 1, jnp.broadcast_to(n0, d.shape), rolled)
        mu_v = mu_ref[...]
        return (d * (1.0 - mu_v) + nxt * mu_v).astype(BF16)

    def main_body(d_ref, n_ref, mu_ref, du_in, du_ref):
        du_ref[...] = shifted(d_ref, n_ref, mu_ref)

    def tail_body(d_ref, n_ref, mu_ref, df_ref, du_in, du_ref):
        du_ref[:, 0:LANES] = df_ref[...]
        du_ref[:, LANES:LANES + 2 * LP] = shifted(d_ref, n_ref, mu_ref)
        if tail > LANES + 2 * LP:
            du_ref[:, LANES + 2 * LP:] = jnp.zeros((tr, tail - LANES - 2 * LP), BF16)

    def specs(w, cb):
        return [_tile(tr, w, cb),
                pl.BlockSpec((8, w), lambda i: (jnp.minimum((i + 1) * (tr // 8), T // 8 - 1), cb)),
                pl.BlockSpec((1, w), lambda i: (0, cb))]

    out = jax.ShapeDtypeStruct(du.shape, BF16)
    du = _pcall(main_body, name="shift_bwd_main", grid=(nb,), in_specs=specs(4 * RW, 0) + [_ANY],
                out_specs=_tile(tr, 4 * RW, cfg.o_rwkv // (4 * RW)), out_shape=out, input_output_aliases={3: 0},
                compiler_params=_cparams(("parallel",)))(dus, dus, mu, du)
    return _pcall(tail_body, name="shift_bwd_tail", grid=(nb,),
                  in_specs=specs(2 * LP, 4 * RW // (2 * LP)) + [_tile(tr, LANES), _ANY],
                  out_specs=_tile(tr, tail, cfg.o_f // tail), out_shape=out, input_output_aliases={4: 0},
                  compiler_params=_cparams(("parallel",)))(dus, dus, mu, df, du)


def _chunk_local(r, lw, k, v, a, b):
    H, C, K = r.shape
    row = lax.broadcasted_iota(jnp.int32, (C, C), 0)
    col = lax.broadcasted_iota(jnp.int32, (C, C), 1)
    incl = jnp.broadcast_to((row >= col).astype(F32)[None], (H, C, C))
    strict = (row > col)[None]
    lower = (row >= col)[None]
    eye = (row == col)[None]
    zero = jnp.zeros((), F32)
    L = _bdot(incl, lw, 2, 1)
    LC = jnp.sum(lw, axis=1, keepdims=True)
    eL = jnp.exp(L)
    eLn = jnp.exp(-L)
    at = a * jnp.exp(L - lw)
    rt = r * eL
    bt = b * eLn
    kt = k * eLn
    eR = jnp.exp(LC - L)
    bh = b * eR
    kh = k * eR
    gram = functools.partial(_bdot, passes=SCAN_PASSES[0])
    inv = functools.partial(_bdot, passes=SCAN_PASSES[1])
    app = functools.partial(_bdot, passes=SCAN_PASSES[2])
    n_ab = jnp.where(strict, gram(at, bt, 2, 2), zero)
    n_ak = jnp.where(strict, gram(at, kt, 2, 2), zero)
    m_rb = jnp.where(lower, gram(rt, bt, 2, 2), zero)
    m_rk = jnp.where(lower, gram(rt, kt, 2, 2), zero)
    M = n_ab
    P = jnp.where(eye, 1.0, zero) + n_ab
    for _ in range(1, max(1, int(np.ceil(np.log2(C))))):
        M = inv(M, M, 2, 1)
        P = P + inv(M, P, 2, 1)
    W = app(P, at, 2, 1)
    Uloc = app(P, app(n_ak, v, 2, 1), 2, 1)
    Q = rt + app(m_rb, W, 2, 1)
    Yloc = app(m_rb, Uloc, 2, 1) + app(m_rk, v, 2, 1)
    A = jnp.where(eye, jnp.exp(LC), zero) + app(W, bh, 1, 1)
    Sloc = app(Uloc, bh, 1, 1) + app(v, kh, 1, 1)
    return Q, Yloc, A, Sloc


def _split_heads(ref, n):
    N = RWKV_HEAD_DIM
    return jnp.stack([ref[:, h * N:(h + 1) * N] for h in range(n)], axis=0)


def _merge_heads(x):
    return jnp.concatenate([x[h] for h in range(x.shape[0])], axis=1)


def _scan_local_specs(cfg):
    N, HB = RWKV_HEAD_DIM, cfg.hb
    grid = (cfg.RH // HB, cfg.T // cfg.C)
    seq = pl.BlockSpec((HB, cfg.C, N), lambda h, j: (h, j, 0))
    mat = pl.BlockSpec((HB, 1, N, N), lambda h, j: (h, j, 0, 0))
    return grid, seq, mat


def _scan_local_fwd(cfg, seqs):
    T, RH, N = cfg.T, cfg.RH, RWKV_HEAD_DIM
    grid, seq, mat = _scan_local_specs(cfg)

    def body(r_ref, lw_ref, k_ref, v_ref, a_ref, b_ref, q_ref, yl_ref, a_out, sl_ref):
        Q, Yloc, A, Sloc = _chunk_local(*[_split_heads(ref, cfg.hb) for ref in (r_ref, lw_ref, k_ref, v_ref, a_ref, b_ref)])
        q_ref[...] = Q
        yl_ref[...] = Yloc
        a_out[:, 0] = A
        sl_ref[:, 0] = Sloc

    tok = pl.BlockSpec((cfg.C, cfg.hb * N), lambda h, j: (j, h))
    sq = jax.ShapeDtypeStruct((RH, T, N), F32)
    mt = jax.ShapeDtypeStruct((RH, T // cfg.C, N, N), F32)
    return _pcall(body, name="rwkv_scan_local_fwd", grid=grid, in_specs=[tok] * 6, out_specs=[seq, seq, mat, mat],
                  out_shape=[sq, sq, mt, mt], compiler_params=_cparams(("parallel", "parallel")))(*seqs)


def _scan_local_bwd(cfg, toks, dq, dy, da, dsl, extra, comm=None):
    T, RW, N = cfg.T, cfg.RW, RWKV_HEAD_DIM
    grid, seq, mat = _scan_local_specs(cfg)
    c_in, c_out, c_scr = comm[:3] if comm else ([], [], [])

    def body(r_ref, lw_ref, k_ref, v_ref, a_ref, b_ref, dq_ref, dy_ref, da_ref, dsl_ref, xr_ref, xk_ref, xv_ref,
             *rest):
        cin, outs = rest[:len(c_in)], rest[len(c_in):len(c_in) + 6]
        cout, scr = rest[len(c_in) + 6:len(c_in) + 6 + len(c_out)], rest[len(c_in) + 6 + len(c_out):]
        _comm_at(comm, 3, grid, cin, cout, scr)
        ins = [_split_heads(ref, cfg.hb) for ref in (r_ref, lw_ref, k_ref, v_ref, a_ref, b_ref)]
        _, vjp = jax.vjp(_chunk_local, *ins)
        d = vjp((dq_ref[...], _split_heads(dy_ref, cfg.hb), da_ref[:, 0], dsl_ref[:, 0]))
        add = {0: xr_ref, 2: xk_ref, 3: xv_ref}
        for j in range(6):
            dj = _merge_heads(d[j])
            outs[j][...] = dj + add[j][...] if j in add else dj
        _comm_at(comm, 4, grid, cin, cout, scr)

    tok = pl.BlockSpec((cfg.C, cfg.hb * N), lambda h, j: (j, h))
    return _pcall(body, name="rwkv_scan_local_bwd", grid=grid,
                  in_specs=[tok] * 6 + [seq, tok, mat, mat] + [tok] * 3 + [_ANY] * len(c_in),
                  out_specs=[tok] * 6 + [_ANY] * len(c_out),
                  out_shape=[jax.ShapeDtypeStruct((T, RW), F32)] * 6 + list(c_out), scratch_shapes=list(c_scr),
                  compiler_params=_cparams(("arbitrary", "arbitrary") if comm else ("parallel", "parallel")),
                  )(*toks, dq, dy, da, dsl, *extra, *c_in)


def _scan_carry_specs(cfg, rev):
    N, RH, C, nc = RWKV_HEAD_DIM, cfg.RH, cfg.C, cfg.T // cfg.C
    at = (lambda j: nc - 1 - j) if rev else (lambda j: j)
    seq = pl.BlockSpec((RH, C, N), lambda j: (0, at(j), 0))
    mat = pl.BlockSpec((RH, 1, N, N), lambda j: (0, at(j), 0, 0))
    return nc, seq, mat


def _scan_carry_fwd(cfg, q, yloc, a, sloc):
    T, RH, N = cfg.T, cfg.RH, RWKV_HEAD_DIM
    nc, seq, mat = _scan_carry_specs(cfg, False)

    def body(q_ref, yl_ref, a_ref, sl_ref, y_ref, ck_ref, s_ref):
        @pl.when(pl.program_id(0) == 0)
        def _():
            s_ref[...] = jnp.zeros_like(s_ref)

        S = s_ref[...]
        ck_ref[:, 0] = S
        y_ref[...] = _merge_heads(_bdot(q_ref[...], S, 2, 2) + yl_ref[...])
        s_ref[...] = _bdot(S, a_ref[:, 0], 2, 1) + sl_ref[:, 0]

    tok = pl.BlockSpec((cfg.C, cfg.RW), lambda j: (j, 0))
    return _pcall(body, name="rwkv_scan_carry_fwd", grid=(nc,), in_specs=[seq, seq, mat, mat], out_specs=[tok, mat],
                  out_shape=[jax.ShapeDtypeStruct((T, cfg.RW), F32), jax.ShapeDtypeStruct((RH, nc, N, N), F32)],
                  scratch_shapes=[pltpu.VMEM((RH, N, N), F32)],
                  compiler_params=_cparams(("arbitrary",)))(q, yloc, a, sloc)


def _scan_carry_bwd(cfg, q, a, ckpt, dy):
    T, RH, N = cfg.T, cfg.RH, RWKV_HEAD_DIM
    nc, seq, mat = _scan_carry_specs(cfg, True)

    def body(q_ref, a_ref, ck_ref, dy_ref, dq_ref, da_ref, dsl_ref, ds_ref):
        @pl.when(pl.program_id(0) == 0)
        def _():
            ds_ref[...] = jnp.zeros_like(ds_ref)

        S, dS, dY = ck_ref[:, 0], ds_ref[...], _split_heads(dy_ref, RH)
        dq_ref[...] = _bdot(dY, S, 2, 1)
        da_ref[:, 0] = _bdot(S, dS, 1, 1)
        dsl_ref[:, 0] = dS
        ds_ref[...] = _bdot(dS, a_ref[:, 0], 2, 2) + _bdot(dY, q_ref[...], 1, 1)

    mt = jax.ShapeDtypeStruct((RH, nc, N, N), F32)
    tok = pl.BlockSpec((cfg.C, cfg.RW), lambda j: (nc - 1 - j, 0))
    return _pcall(body, name="rwkv_scan_carry_bwd", grid=(nc,), in_specs=[seq, mat, mat, tok],
                  out_specs=[seq, mat, mat], out_shape=[jax.ShapeDtypeStruct((RH, T, N), F32), mt, mt],
                  scratch_shapes=[pltpu.VMEM((RH, N, N), F32)],
                  compiler_params=_cparams(("arbitrary",)))(q, a, ckpt, dy)


def _post_fn(y, r, kp, v, zb, ln_w, ln_b, rk, ind, ind_t):
    n = float(RWKV_HEAD_DIM)
    mu = _xdot(_xdot(y, ind, ind_t) / n, ind_t, ind)
    yc = y - mu
    var = _xdot(yc * yc, ind, ind_t) / n
    rstd = _xdot(lax.rsqrt(var + GN_EPS), ind_t, ind)
    yn = yc * rstd * ln_w + ln_b
    bonus = _xdot(_xdot(r * kp * rk, ind, ind_t), ind_t, ind) * v
    return (yn + bonus) * _silu(zb)


def _rwkv_post_fwd(cfg, y, r, kp, v, zb, ln_w, ln_b, rk):
    T, RW, tr = cfg.T, cfg.RW, cfg.tr
    ind, ind_t, _ = _head_indicators(cfg)

    def body(y_ref, r_ref, k_ref, v_ref, z_ref, lw_ref, lb_ref, rk_ref, ind_ref, indt_ref, ob_ref):
        ob_ref[...] = _post_fn(y_ref[...], r_ref[...], k_ref[...], v_ref[...], z_ref[...], lw_ref[...], lb_ref[...],
                               rk_ref[...], ind_ref[...], indt_ref[...]).astype(BF16)

    consts = [ln_w, ln_b, rk, ind, ind_t]
    return _pcall(body, name="rwkv_post_fwd", grid=(T // tr,),
                  in_specs=[_tile(tr, RW)] * 5 + [_const(c.shape) for c in consts],
                  out_specs=_tile(tr, RW), out_shape=jax.ShapeDtypeStruct((T, RW), BF16),
                  compiler_params=_cparams(("parallel",)))(y, r, kp, v, zb, *consts)


def _rwkv_post_bwd(cfg, y, r, kp, v, zb, ln_w, ln_b, rk, dob):
    T, RW = cfg.T, cfg.RW
    tr = min(128, T)
    ind, ind_t, _ = _head_indicators(cfg)

    def body(y_ref, r_ref, k_ref, v_ref, z_ref, lw_ref, lb_ref, rk_ref, ind_ref, indt_ref, dob_ref,
             dy_ref, dr_ref, dk_ref, dv_ref, dz_ref, dlw_ref, dlb_ref, drk_ref):
        fn = functools.partial(_post_fn, ind=ind_ref[...], ind_t=indt_ref[...])
        _, vjp = jax.vjp(fn, y_ref[...], r_ref[...], k_ref[...], v_ref[...], z_ref[...], lw_ref[...], lb_ref[...],
                         rk_ref[...])
        d = vjp(dob_ref[...])
        for ref, val in zip((dy_ref, dr_ref, dk_ref, dv_ref, dz_ref), d[:5]):
            ref[...] = val
        i = pl.program_id(0)
        for ref, val in zip((dlw_ref, dlb_ref, drk_ref), d[5:8]):
            _acc_store(i, ref, val)

    consts = [ln_w, ln_b, rk, ind, ind_t]
    vec = jax.ShapeDtypeStruct((1, RW), F32)
    return _pcall(body, name="rwkv_post_bwd", grid=(T // tr,),
                  in_specs=[_tile(tr, RW)] * 5 + [_const(c.shape) for c in consts] + [_tile(tr, RW)],
                  out_specs=[_tile(tr, RW)] * 5 + [_const((1, RW))] * 3,
                  out_shape=[jax.ShapeDtypeStruct((T, RW), F32)] * 5 + [vec] * 3,
                  compiler_params=_cparams(("arbitrary",)))(y, r, kp, v, zb, *consts, dob)


def _adamw_math(w, g, m, v):
    m = ADAM_B1 * m + (1.0 - ADAM_B1) * g
    v = ADAM_B2 * v + (1.0 - ADAM_B2) * (g * g)
    m_hat = m / (1.0 - ADAM_B1 ** ADAM_STEP)
    v_hat = v / (1.0 - ADAM_B2 ** ADAM_STEP)
    delta = -ADAM_LR * (m_hat / (jnp.sqrt(v_hat) + ADAM_EPS) + ADAM_WD * w)
    return delta, m, v


def _adamw(name, w, g, m, v, copy_grad=False):
    R, Cc = w.shape
    Rp = -(-R // 8) * 8
    tr = Rp
    for nb in range(1, Rp // 8 + 1):
        if (Rp // 8) % nb == 0 and (Rp // nb) * Cc * 4 <= 2 * 1024 * 1024:
            tr = Rp // nb
            break

    def body(w_ref, g_ref, m_ref, v_ref, d_ref, nm_ref, nv_ref, *g_out):
        g_v = g_ref[...]
        d, nm, nv = _adamw_math(w_ref[...], g_v, m_ref[...], v_ref[...])
        d_ref[...] = d
        nm_ref[...] = nm
        nv_ref[...] = nv
        if copy_grad:
            g_out[0][...] = g_v

    spec = _tile(tr, Cc)
    n_out = 4 if copy_grad else 3
    return _pcall(body, name=name, grid=(Rp // tr,), in_specs=[spec] * 4, out_specs=[spec] * n_out,
                  out_shape=[jax.ShapeDtypeStruct((R, Cc), F32)] * n_out,
                  compiler_params=_cparams(("parallel",)))(w, g, m, v)


def _row_tile(R, Cc, itemsize, budget=2 * 1024 * 1024):
    for nb in range(1, R // 16 + 1):
        if R % nb == 0 and (R // nb) % 16 == 0 and (R // nb) * Cc * itemsize <= budget:
            return R // nb
    return R


def _add_halves(name, gs, r1, c_idx):
    _, R, Cc = gs.shape
    half = R // 2
    tr = _row_tile(half, Cc, 4)
    nb = half // tr

    def body(c_ref, g_ref, r_ref, o_ref):
        o_ref[...] = (g_ref[...].astype(F32) + r_ref[...].astype(F32)).astype(BF16)

    grid_spec = pltpu.PrefetchScalarGridSpec(
        num_scalar_prefetch=1, grid=(N_CHIPS, nb),
        in_specs=[pl.BlockSpec((1, tr, Cc), lambda s, i, c: (s, c[0] * nb + i, 0)),
                  pl.BlockSpec((1, tr, Cc), lambda s, i, c: (s, i, 0))],
        out_specs=pl.BlockSpec((1, tr, Cc), lambda s, i, c: (s, i, 0)))
    return _pcall(body, name=name, grid_spec=grid_spec, out_shape=jax.ShapeDtypeStruct((N_CHIPS, half, Cc), BF16),
                  compiler_params=_cparams(("parallel", "parallel")))(c_idx, gs, r1)


def _sum_slots(name, r2):
    S, R, Cc = r2.shape
    tr = _row_tile(R, Cc, 4 * S // 2 if r2.dtype == BF16 else 4 * S)

    def body(r_ref, o_ref):
        acc = r_ref[0].astype(F32)
        for s in range(1, S):
            acc = acc + r_ref[s].astype(F32)
        o_ref[...] = acc

    return _pcall(body, name=name, grid=(R // tr,), in_specs=[pl.BlockSpec((S, tr, Cc), lambda i: (0, i, 0))],
                  out_specs=_tile(tr, Cc), out_shape=jax.ShapeDtypeStruct((R, Cc), F32),
                  compiler_params=_cparams(("parallel",)))(r2)


def _sum_chips(name, recv, own, place):
    S, H, Cc = recv.shape
    tr = _row_tile(H, Cc, 4, 1024 * 1024)
    nb = H // tr

    def body(p_ref, r_ref, own_ref, o_ref):
        s = pl.program_id(1)
        me = p_ref[0]

        @pl.when(s == 0)
        def _():
            o_ref[...] = jnp.zeros_like(o_ref)

        @pl.when(s == me)
        def _():
            o_ref[...] += own_ref[0].astype(F32)

        @pl.when(s != me)
        def _():
            o_ref[...] += r_ref[0].astype(F32)

    grid_spec = pltpu.PrefetchScalarGridSpec(
        num_scalar_prefetch=1, grid=(nb, S),
        in_specs=[pl.BlockSpec((1, tr, Cc), lambda i, s, p: (jnp.where(s == p[0], (s + 1) % S, s), i, 0)),
                  pl.BlockSpec((1, tr, Cc), lambda i, s, p: (p[0], i, 0))],
        out_specs=pl.BlockSpec((tr, Cc), lambda i, s, p: (p[1] * nb + i, 0)))
    return _pcall(body, name=name, grid_spec=grid_spec, out_shape=jax.ShapeDtypeStruct((2 * H, Cc), F32),
                  compiler_params=_cparams(("parallel", "arbitrary")))(place, recv, own)


def _cast_bf16(name, w):
    R, Cc = w.shape
    tr = _row_tile(R, Cc, 4)

    def body(w_ref, o_ref):
        o_ref[...] = w_ref[...].astype(BF16)

    return _pcall(body, name=name, grid=(R // tr,), in_specs=[_tile(tr, Cc)], out_specs=_tile(tr, Cc),
                  out_shape=jax.ShapeDtypeStruct((R, Cc), BF16), compiler_params=_cparams(("parallel",)))(w)


_ANY = pl.BlockSpec(memory_space=pl.ANY)


def _place():
    x, y, c = lax.axis_index("x"), lax.axis_index("y"), lax.axis_index("c")
    others = [(1 - x, y), (x, 1 - y), (1 - x, 1 - y)]
    return x, y, c, others


def _gather_weights(shards):
    arrays, out_shapes, scratch, start, finish, middle = _gather_parts(shards)
    n = len(shards)

    def body(*refs):
        ins, outs, sems = refs[:n], refs[n:2 * n], refs[2 * n:]
        start(ins, outs, sems)
        middle(ins, outs, sems)
        finish(ins, outs, sems)

    return _pcall(body, name="gather_weights", in_specs=[_ANY] * n, out_specs=[_ANY] * n, out_shape=out_shapes,
                  scratch_shapes=scratch)(*arrays)


def _gather_parts(shards):
    n = len(shards)
    halves = [s.shape[0] // 2 for s in shards]

    def parts(ins, outs, sems):
        x, y, c, _ = _place()
        me = 2 * x + y
        n1 = (x ^ (1 - c), y ^ c)
        n2 = (x ^ c, y ^ (1 - c))
        s1, s2, sd = 2 * n1[0] + n1[1], 2 * n2[0] + n2[1], 2 * (1 - x) + (1 - y)
        sib = (x, y, 1 - c)

        def rows(k, chip, hc):
            return outs[k].at[chip, pl.ds(hc * halves[k], halves[k]), :]

        def remote(k, j, src, dst, to):
            return pltpu.make_async_remote_copy(src_ref=src, dst_ref=dst, send_sem=sems[0].at[6 * k + j],
                                                recv_sem=sems[1].at[6 * k + j], device_id=to, device_id_type=MESH)

        def copy(k, j):
            if j < 2:
                mine = ins[k].at[pl.ds(c * halves[k], halves[k]), :]
                return remote(k, j, mine, rows(k, me, c), (*(n1 if j == 0 else n2), c))
            land = rows(k, {2: s1, 3: s1, 4: s2, 5: sd}[j], c)
            return remote(k, j, land, land, (*n2, c) if j == 2 else sib)

        def arrived(k, j):
            hc = c if j < 3 else 1 - c
            land = rows(k, {0: s1, 1: s2, 2: sd, 3: s2, 4: s1, 5: sd}[j], hc)
            remote(k, j, land, land, (x, y, c)).wait_recv()

        return copy, arrived

    def start(ins, outs, sems):
        copy, _ = parts(ins, outs, sems)
        for k in range(n):
            copy(k, 0).start()
            copy(k, 1).start()

    def middle(ins, outs, sems):
        copy, arrived = parts(ins, outs, sems)
        for k in range(n):
            arrived(k, 0)
            copy(k, 2).start()
            copy(k, 3).start()
            arrived(k, 1)
            copy(k, 4).start()

    def finish(ins, outs, sems):
        copy, arrived = parts(ins, outs, sems)
        for k in range(n):
            arrived(k, 2)
            copy(k, 5).start()
        for k in range(n):
            for j in (3, 4, 5):
                arrived(k, j)
        for k in range(n):
            for j in range(6):
                copy(k, j).wait_send()

    out_shapes = [jax.ShapeDtypeStruct((N_CHIPS,) + s.shape, s.dtype) for s in shards]
    scratch = [pltpu.SemaphoreType.DMA((6 * n,)), pltpu.SemaphoreType.DMA((6 * n,))]
    return list(shards), out_shapes, scratch, start, finish, middle


def _exchange_halves(name, grads):
    n = len(grads)
    halves = [g.shape[1] // 2 for g in grads]

    def body(*refs):
        ins, outs = refs[:n], refs[n:2 * n]
        send_sems, recv_sems = refs[2 * n:]
        x, y, c, _ = _place()
        cps = []
        for k in range(n):
            src = ins[k].at[:, pl.ds((1 - c) * halves[k], halves[k]), :]
            cp = pltpu.make_async_remote_copy(src_ref=src, dst_ref=outs[k], send_sem=send_sems.at[k],
                                              recv_sem=recv_sems.at[k], device_id=(x, y, 1 - c), device_id_type=MESH)
            cp.start()
            cps.append(cp)
        for cp in cps:
            cp.wait()

    return _pcall(
        body, name=name, in_specs=[_ANY] * n, out_specs=[_ANY] * n,
        out_shape=[jax.ShapeDtypeStruct((N_CHIPS, h) + g.shape[2:], g.dtype) for g, h in zip(grads, halves)],
        scratch_shapes=[pltpu.SemaphoreType.DMA((n,)), pltpu.SemaphoreType.DMA((n,))],
    )(*grads)


def _scatter_to_owners(chip_sums):
    n = len(chip_sums)

    def sends(ins, outs, sems):
        x, y, c, others = _place()
        me = 2 * x + y
        return [pltpu.make_async_remote_copy(
            src_ref=ins[k].at[2 * px + py], dst_ref=outs[k].at[me], send_sem=sems[0].at[3 * k + j],
            recv_sem=sems[1].at[3 * k + j], device_id=(px, py, c), device_id_type=MESH)
            for k in range(n) for j, (px, py) in enumerate(others)]

    def start(ins, outs, sems):
        for cp in sends(ins, outs, sems):
            cp.start()

    def finish(ins, outs, sems):
        x, y, c, others = _place()
        for k in range(n):
            for j, (px, py) in enumerate(others):
                land = outs[k].at[2 * px + py]
                pltpu.make_async_remote_copy(src_ref=land, dst_ref=land, send_sem=sems[0].at[3 * k + j],
                                             recv_sem=sems[1].at[3 * k + j], device_id=(x, y, c),
                                             device_id_type=MESH).wait_recv()
        for cp in sends(ins, outs, sems):
            cp.wait_send()

    out_shapes = [jax.ShapeDtypeStruct(g.shape, g.dtype) for g in chip_sums]
    scratch = [pltpu.SemaphoreType.DMA((3 * n,)), pltpu.SemaphoreType.DMA((3 * n,))]
    return list(chip_sums), out_shapes, scratch, start, finish


def _second_neighbour():
    x, y, c, _ = _place()
    return (x, y, c), (x ^ c, y ^ (1 - c)), (x ^ (1 - c), y ^ c)


def _scatter_stage1(chip_sums):
    n = len(chip_sums)

    def copies(ins, outs, sems):
        (x, y, c), n2, n1 = _second_neighbour()
        diag = 2 * (1 - x) + (1 - y)
        return [pltpu.make_async_remote_copy(
            src_ref=ins[k].at[slot], dst_ref=outs[2 * k + j], send_sem=sems[0].at[2 * k + j],
            recv_sem=sems[1].at[2 * k + j], device_id=(*n2, c), device_id_type=MESH)
            for k in range(n) for j, slot in enumerate((2 * n2[0] + n2[1], diag))]

    def start(ins, outs, sems):
        for cp in copies(ins, outs, sems):
            cp.start()

    def finish(ins, outs, sems):
        for cp in copies(ins, outs, sems):
            cp.wait()

    out_shapes = [jax.ShapeDtypeStruct(g.shape[1:], g.dtype) for g in chip_sums for _ in range(2)]
    scratch = [pltpu.SemaphoreType.DMA((2 * n,)), pltpu.SemaphoreType.DMA((2 * n,))]
    return list(chip_sums), out_shapes, scratch, start, finish


def _scatter_stage2(passed):
    n = len(passed)

    def copies(ins, outs, sems):
        (x, y, c), n2, n1 = _second_neighbour()
        return [pltpu.make_async_remote_copy(src_ref=ins[k], dst_ref=outs[k], send_sem=sems[0].at[k],
                                             recv_sem=sems[1].at[k], device_id=(*n1, c), device_id_type=MESH)
                for k in range(n)]

    def start(ins, outs, sems):
        for cp in copies(ins, outs, sems):
            cp.start()

    def finish(ins, outs, sems):
        for cp in copies(ins, outs, sems):
            cp.wait()

    out_shapes = [jax.ShapeDtypeStruct(p.shape, p.dtype) for p in passed]
    scratch = [pltpu.SemaphoreType.DMA((n,)), pltpu.SemaphoreType.DMA((n,))]
    return list(passed), out_shapes, scratch, start, finish


def _add_passed(name, own, got, slot):
    _, H, Cc = own.shape
    tr = _row_tile(H, Cc, 4)

    def body(s_ref, o_ref, g_ref, out_ref):
        out_ref[...] = (o_ref[0].astype(F32) + g_ref[...].astype(F32)).astype(BF16)

    grid_spec = pltpu.PrefetchScalarGridSpec(
        num_scalar_prefetch=1, grid=(H // tr,),
        in_specs=[pl.BlockSpec((1, tr, Cc), lambda i, s: (s[0], i, 0)), pl.BlockSpec((tr, Cc), lambda i, s: (i, 0))],
        out_specs=pl.BlockSpec((tr, Cc), lambda i, s: (i, 0)))
    return _pcall(body, name=name, grid_spec=grid_spec, out_shape=jax.ShapeDtypeStruct((H, Cc), BF16),
                  compiler_params=_cparams(("parallel",)))(slot, own, got)


def _sum_stages(name, own, direct, via, place):
    _, H, Cc = own.shape
    tr = _row_tile(H, Cc, 4, 1024 * 1024)
    nb = H // tr

    def body(p_ref, own_ref, d_ref, v_ref, o_ref):
        o_ref[...] = (own_ref[0].astype(F32) + d_ref[...].astype(F32)) + v_ref[...].astype(F32)

    flat = pl.BlockSpec((tr, Cc), lambda i, p: (i, 0))
    grid_spec = pltpu.PrefetchScalarGridSpec(
        num_scalar_prefetch=1, grid=(nb,),
        in_specs=[pl.BlockSpec((1, tr, Cc), lambda i, p: (p[0], i, 0)), flat, flat],
        out_specs=pl.BlockSpec((tr, Cc), lambda i, p: (p[1] * nb + i, 0)))
    return _pcall(body, name=name, grid_spec=grid_spec, out_shape=jax.ShapeDtypeStruct((2 * H, Cc), F32),
                  compiler_params=_cparams(("parallel",)))(place, own, direct, via)


def _join_halves(fulls, small):
    n = len(fulls)
    hs = [f.shape[0] // 2 for f in fulls]
    rel = [(dx, dy, dc) for dx in (0, 1) for dy in (0, 1) for dc in (0, 1)][1:]

    def body(*refs):
        ins, small_in = refs[:n], refs[n]
        outs, small_out = refs[n + 1:2 * n + 1], refs[2 * n + 1]
        send_sems, recv_sems, ssend, srecv, local_sem = refs[2 * n + 2:]
        x, y, c, _ = _place()
        dev = 4 * x + 2 * y + c
        local = pltpu.make_async_copy(small_in, small_out.at[dev], local_sem)
        local.start()
        cps = []
        for k in range(n):
            mine = pl.ds(c * hs[k], hs[k])
            cp = pltpu.make_async_remote_copy(src_ref=ins[k].at[mine, :], dst_ref=outs[k].at[mine, :],
                                              send_sem=send_sems.at[k], recv_sem=recv_sems.at[k],
                                              device_id=(x, y, 1 - c), device_id_type=MESH)
            cp.start()
            cps.append(cp)
        for r, (dx, dy, dc) in enumerate(rel):
            cp = pltpu.make_async_remote_copy(src_ref=small_in, dst_ref=small_out.at[dev], send_sem=ssend.at[r],
                                              recv_sem=srecv.at[r], device_id=(x ^ dx, y ^ dy, c ^ dc),
                                              device_id_type=MESH)
            cp.start()
            cps.append(cp)
        for k in range(n):
            land = outs[k].at[pl.ds((1 - c) * hs[k], hs[k]), :]
            pltpu.make_async_remote_copy(src_ref=land, dst_ref=land, send_sem=send_sems.at[k],
                                         recv_sem=recv_sems.at[k], device_id=(x, y, c), device_id_type=MESH).wait_recv()
        for r, (dx, dy, dc) in enumerate(rel):
            land = small_out.at[4 * (x ^ dx) + 2 * (y ^ dy) + (c ^ dc)]
            pltpu.make_async_remote_copy(src_ref=land, dst_ref=land, send_sem=ssend.at[r], recv_sem=srecv.at[r],
                                         device_id=(x, y, c), device_id_type=MESH).wait_recv()
        for cp in cps:
            cp.wait_send()
        local.wait()

    return _pcall(
        body, name="join_halves", in_specs=[_ANY] * (n + 1), out_specs=[_ANY] * (n + 1),
        out_shape=[jax.ShapeDtypeStruct(f.shape, f.dtype) for f in fulls]
        + [jax.ShapeDtypeStruct((N_DEV,) + small.shape, small.dtype)],
        input_output_aliases={k: k for k in range(n)},
        scratch_shapes=[pltpu.SemaphoreType.DMA((n,)), pltpu.SemaphoreType.DMA((n,)), pltpu.SemaphoreType.DMA((7,)),
                        pltpu.SemaphoreType.DMA((7,)), pltpu.SemaphoreType.DMA],
    )(*fulls, small)


def _local_step(cfg, x2, target, norm_gain, w_my, fb, mu_g, w0, a0, k_k, k_a, r_k, ln_w, ln_b, fng, rest,
                exchange=None):
    T, D, FW, FH, RW, RH, LP, lora = cfg.T, cfg.D, cfg.FW, cfg.FH, cfg.RW, cfg.RH, cfg.LP, cfg.lora
    fb_p = jnp.pad(fb, ((0, 0), (0, LANES - FH)))
    mu = _rwkv_vec_to_my(cfg, mu_g)
    rk = r_k.reshape(1, RW)
    tm = min(1024, T)

    h = _rms_fwd(cfg, x2, norm_gain)
    if len(rest) == 2:
        u, *got = _mm("in_proj", h, w_my, "nn", F32, tm, cfg.tn, 2048, comm=rest[0])
        rest = rest[1](got)
    else:
        u = _mm("in_proj", h, w_my, "nn", F32, tm, cfg.tn, 2048)
    w2, a2, wpf, wpr, wout = rest
    w2p = jnp.pad(w2, ((0, LP - lora), (0, 0)))
    a2p = jnp.pad(a2, ((0, LP - lora), (0, 0)))
    c_cols = _fox_prep(cfg, u, fb_p)
    c_rows = c_cols[:, :FH].T.reshape(FH, 1, T)
    o, lse = _attn_fwd(cfg, u, c_rows)
    oa = _gate_a_fwd(cfg, o, u)
    prep = _rwkv_prep_fwd(cfg, u, mu, w0, w2p, a0, a2p, k_k, k_a)
    r, lw, kp, v, an, b, zb = prep
    toks = [r, lw, kp, v, an, b]
    q_s, yloc, a_m, sloc = _scan_local_fwd(cfg, toks)
    y, ckpt = _scan_carry_fwd(cfg, q_s, yloc, a_m, sloc)
    ob = _rwkv_post_fwd(cfg, y, r, kp, v, zb, ln_w, ln_b, rk)
    pa = _mm("proj_fox", oa, wpf, "nn", F32, tm, 1024, 2048)
    pb = _mm("proj_rwkv", ob, wpr, "nn", F32, tm, 1024, 2048)
    m = _merge_fwd(cfg, pa, pb, u)
    mo = _mm("out_proj", m, wout, "nn", F32, tm, 1024, 2048)
    loss8, dres, dres16, d_fng = _final(cfg, x2, mo, fng.reshape(1, D), target)

    dm = _mm("out_proj_dx", dres16, wout, "nt", F32, tm, 1024, 2048)
    d_wout = _mm("out_proj_dw", m, dres16, "tn", BF16, 1024, 1024, 2048)
    dpa, dpb, du = _merge_bwd(cfg, pa, pb, u, dm)
    doa = _mm("proj_fox_dx", dpa, wpf, "nt", F32, tm, 1024, 2048)
    d_wpf = _mm("proj_fox_dw", oa, dpa, "tn", BF16, 1024, 1024, 2048)
    dob = _mm("proj_rwkv_dx", dpb, wpr, "nt", F32, tm, 1024, 2048)
    d_wpr = _mm("proj_rwkv_dw", ob, dpb, "tn", BF16, 1024, 1024, 2048)

    do, du = _gate_a_bwd(cfg, o, u, doa, du)
    du, dcol = _attn_bwd(cfg, u, c_rows, lse, do, du)
    dc = jnp.pad(-dcol.reshape(FH, T).T, ((0, 0), (0, LANES - FH)))
    df, d_fb = _fox_prep_bwd(cfg, u, fb_p, dc)

    dy, dr_p, dk_p, dv_p, dzb, d_lnw, d_lnb, d_rk = _rwkv_post_bwd(cfg, y, r, kp, v, zb, ln_w, ln_b, rk, dob)
    dq_s, da_m, dsl = _scan_carry_bwd(cfg, q_s, a_m, ckpt, dy)
    early = dict(w_proj_fox=d_wpf, w_proj_rwkv=d_wpr, w_out=d_wout)
    res = _scan_local_bwd(cfg, toks, dq_s, dy, da_m, dsl, [dr_p, dk_p, dv_p], exchange(early) if exchange else None)
    cots, received = res[:6], list(res[6:])
    dus, d_mu, d_w0, d_w2p, d_a0, d_a2p, d_kk, d_ka = _rwkv_prep_bwd(cfg, u, mu, w0, w2p, a0, a2p, k_k, k_a, cots, dzb)
    du = _shift_bwd(cfg, dus, mu, df, du)
    d_wmy = _mm("in_proj_dw", h, du, "tn", BF16, 1024, cfg.tn, 2048)
    late = dict(w_in=d_wmy, rwkv_w2=d_w2p[:lora], rwkv_a2=d_a2p[:lora])
    tkx = 2 * cfg.tn if cfg.ncol % (2 * cfg.tn) == 0 else cfg.tn
    res = _mm("in_proj_dx", du, w_my, "nt", F32, tm, 1024, tkx, comm=exchange(late) if exchange else None)
    dh = res[0] if exchange else res
    big = dict(early, **late)
    res = _rms_bwd(cfg, x2, norm_gain, dh, dres, exchange(list(res[1:])) if exchange else None)
    gx, d_ng = res[:2]
    received += list(res[2:])

    small = dict(norm_gain=d_ng, fox_forget_bias=d_fb[:, :FH], rwkv_shift_mix=_rwkv_vec_from_my(cfg, d_mu),
                 rwkv_w0=d_w0, rwkv_a0=d_a0, rwkv_k_k=d_kk, rwkv_k_a=d_ka, rwkv_r_k=d_rk, rwkv_ln_w=d_lnw,
                 rwkv_ln_b=d_lnb, final_norm_gain=d_fng)
    return loss8[0, 0], gx, small, big, received


_SMALL = ["norm_gain", "fox_forget_bias", "rwkv_shift_mix", "rwkv_w0", "rwkv_a0", "rwkv_k_k", "rwkv_k_a", "rwkv_r_k",
          "rwkv_ln_w", "rwkv_ln_b", "final_norm_gain"]
_WEIGHTS = ["norm_gain", "w_in", "fox_forget_bias", "rwkv_shift_mix", "rwkv_w0", "rwkv_w2", "rwkv_a0", "rwkv_a2",
            "rwkv_k_k", "rwkv_k_a", "rwkv_r_k", "rwkv_ln_w", "rwkv_ln_b", "w_proj_fox", "w_proj_rwkv", "w_out",
            "final_norm_gain"]


def _pack_small(arrs):
    parts = []
    for a in arrs:
        f = a.reshape(-1)
        parts.append(jnp.pad(f, (0, (-f.shape[0]) % LANES)))
    flat = jnp.concatenate(parts)
    rows = flat.shape[0] // LANES
    flat = jnp.pad(flat, (0, ((-rows) % 8) * LANES))
    return flat.reshape(-1, LANES)


def _unpack_small(packed, shapes):
    flat = packed.reshape(-1)
    out, pos = [], 0
    for s in shapes:
        n = int(np.prod(s))
        out.append(flat[pos:pos + n].reshape(s))
        pos += n + ((-n) % LANES)
    return out


def _shard_major(a, axis):
    parts = jnp.split(a, N_CHIPS, axis=axis)
    return jnp.stack(parts, axis=0)


def kernel(x, norm_gain, w_in, fox_forget_bias, rwkv_shift_mix, rwkv_w0, rwkv_w2, rwkv_a0, rwkv_a2, rwkv_k_k, rwkv_k_a, rwkv_r_k, rwkv_ln_w, rwkv_ln_b, w_proj_fox, w_proj_rwkv, w_out, final_norm_gain, loss_target, m_norm_gain, m_w_in, m_fox_forget_bias, m_rwkv_shift_mix, m_rwkv_w0, m_rwkv_w2, m_rwkv_a0, m_rwkv_a2, m_rwkv_k_k, m_rwkv_k_a, m_rwkv_r_k, m_rwkv_ln_w, m_rwkv_ln_b, m_w_proj_fox, m_w_proj_rwkv, m_w_out, m_final_norm_gain, v_norm_gain, v_w_in, v_fox_forget_bias, v_rwkv_shift_mix, v_rwkv_w0, v_rwkv_w2, v_rwkv_a0, v_rwkv_a2, v_rwkv_k_k, v_rwkv_k_a, v_rwkv_r_k, v_rwkv_ln_w, v_rwkv_ln_b, v_w_proj_fox, v_w_proj_rwkv, v_w_out, v_final_norm_gain):
    args = dict(locals())
    T, D = x.shape[1], x.shape[2]
    lora = rwkv_w2.shape[1]
    cfg = _Cfg(T, D, lora)
    RW = cfg.RW
    c_idx = lax.axis_index("c").astype(jnp.int32).reshape(1)
    me_chip = (2 * lax.axis_index("x") + lax.axis_index("y")).astype(jnp.int32)
    place = jnp.concatenate([me_chip.reshape(1), c_idx])

    w_in_s = w_in[0].astype(BF16)
    lora_s = jnp.concatenate([rwkv_w2[0], rwkv_a2[0]], axis=0)
    own_slot = lambda g, own: lax.dynamic_update_slice(g, own[None], (me_chip, 0, 0))
    w_my = _shards_to_my_layout(cfg, own_slot(_gather_weights([w_in_s])[0], w_in_s))
    mine = [_cast_bf16("cast_w_proj_fox", w_proj_fox[0]), _cast_bf16("cast_w_proj_rwkv", w_proj_rwkv[0]),
            _cast_bf16("cast_w_out", w_out[0]), lora_s]

    def unpack(gathered):
        g_wpf, g_wpr, g_out, g_lora = [own_slot(g, own) for g, own in zip(gathered, mine)]
        lo = g_lora.transpose(1, 0, 2).reshape(2 * lora, RW)
        return (lo[:lora], lo[lora:], g_wpf.transpose(1, 0, 2).reshape(RW, D),
                g_wpr.transpose(1, 0, 2).reshape(RW, D), g_out.reshape(D, D))

    early, late = ["w_proj_fox", "w_proj_rwkv", "w_out"], ["w_in", "lora"]
    names = early + late
    chip_sums, direct = {}, {}
    n1_slot = (2 * (lax.axis_index("x") ^ (1 - lax.axis_index("c")))
               + (lax.axis_index("y") ^ lax.axis_index("c"))).astype(jnp.int32).reshape(1)

    def exchange(got):
        if isinstance(got, dict):
            if "w_in" in got:
                group, scatter = late, _scatter_stage1
                gs = [_my_layout_to_shards(cfg, got["w_in"]),
                      _shard_major(jnp.concatenate([got["rwkv_w2"], got["rwkv_a2"]], axis=0).astype(BF16), 1)]
            else:
                group, scatter = early, _scatter_to_owners
                gs = [_shard_major(got["w_proj_fox"], 1), _shard_major(got["w_proj_rwkv"], 1),
                      _shard_major(got["w_out"], 0)]
            recv1 = _exchange_halves("exchange_halves_" + group[0], gs)
            sums = [_add_halves("add_halves_" + nm, g, r, c_idx) for nm, g, r in zip(group, gs, recv1)]
            chip_sums.update(zip(group, sums))
            return scatter(sums)
        direct.update(zip(late, got[0::2]))
        return _scatter_stage2([_add_passed("add_passed_" + nm, chip_sums[nm], g, n1_slot)
                                for nm, g in zip(late, got[1::2])])

    loss_dev, gx, small, _, recv2 = _local_step(
        cfg, x[0], loss_target[0], norm_gain, w_my, fox_forget_bias, rwkv_shift_mix, rwkv_w0, rwkv_a0, rwkv_k_k,
        rwkv_k_a, rwkv_r_k, rwkv_ln_w, rwkv_ln_b, final_norm_gain, (_gather_parts(mine), unpack), exchange)
    loss = lax.psum(loss_dev, ("x", "y", "c"))

    small_shapes = [args[nm].shape for nm in _SMALL]
    packed = _pack_small([small[nm] for nm in _SMALL])
    reduced = [_sum_chips("sum_chips_" + nm, r, chip_sums[nm], place) for nm, r in zip(early, recv2[:3])]
    reduced += [_sum_stages("sum_stages_" + nm, chip_sums[nm], direct[nm], via, place)
                for nm, via in zip(late, recv2[3:])]
    *joined, small_all = _join_halves(reduced, packed)
    g_small = _sum_slots("sum_small", small_all)

    grads = dict(zip(_SMALL, _unpack_small(g_small, small_shapes)))
    grads.update({nm: g[None] for nm, g in zip(names, joined) if nm != "lora"})
    g_lora_f = joined[names.index("lora")]
    grads["rwkv_w2"] = g_lora_f[None, :lora]
    grads["rwkv_a2"] = g_lora_f[None, lora:]

    delta, new_m, new_v = {}, {}, {}
    w_small = _pack_small([args[nm] for nm in _SMALL])
    m_small = _pack_small([args["m_" + nm] for nm in _SMALL])
    v_small = _pack_small([args["v_" + nm] for nm in _SMALL])
    d_s, m_s, v_s = _adamw("adamw_small", w_small, g_small, m_small, v_small)
    for tgt, pk in ((delta, d_s), (new_m, m_s), (new_v, v_s)):
        tgt.update(zip(_SMALL, _unpack_small(pk, small_shapes)))
    for nm in ("w_in", "w_proj_fox", "w_proj_rwkv", "w_out", "rwkv_w2", "rwkv_a2"):
        shp = args[nm].shape
        two_d = (shp[1], shp[2])
        d_b, m_b, v_b = _adamw("adamw_" + nm, args[nm].reshape(two_d), grads[nm].reshape(two_d),
                               args["m_" + nm].reshape(two_d), args["v_" + nm].reshape(two_d))
        delta[nm], new_m[nm], new_v[nm] = d_b.reshape(shp), m_b.reshape(shp), v_b.reshape(shp)

    return (loss, gx[None], *[grads[n] for n in _WEIGHTS], *[delta[n] for n in _WEIGHTS],
            *[new_m[n] for n in _WEIGHTS], *[new_v[n] for n in _WEIGHTS])
```

```python
import functools

import numpy as np
import jax
import jax.numpy as jnp
from jax import lax
from jax.experimental import pallas as pl
from jax.experimental.pallas import tpu as pltpu

F32 = jnp.float32
BF16 = jnp.bfloat16
HI = lax.Precision.HIGHEST
MESH = pl.DeviceIdType.MESH

FOX_HEAD_DIM = 128
RWKV_HEAD_DIM = 64
RMS_EPS = 1e-6
GN_EPS = 64e-5
L2_EPS = 1e-12
ADAM_LR = 0.001
ADAM_B1 = 0.9
ADAM_B2 = 0.999
ADAM_EPS = 1e-08
ADAM_WD = 0.01
ADAM_STEP = 10

LANES = 128
VMEM_LIMIT = 56 * 1024 * 1024
SCAN_CHUNK = 64
SCAN_HEADS_PER_STEP = 16
SCAN_PASSES = (1, 1, 1)
N_CHIPS = 4
N_DEV = 8

_pcall = pl.pallas_call


def _cparams(sem=None):
    return pltpu.CompilerParams(dimension_semantics=sem, vmem_limit_bytes=VMEM_LIMIT)


def _softplus(x):
    return jnp.maximum(x, 0.0) + jnp.log(1.0 + jnp.exp(-jnp.abs(x)))


def _silu(z):
    return z * jax.nn.sigmoid(z)


def _rmsn(x, g):
    return x * lax.rsqrt(jnp.mean(x * x, axis=-1, keepdims=True) + RMS_EPS) * g


def _dot(a, b, dims="nn", precision=None):
    dn = {"nn": (((1,), (0,)), ((), ())), "nt": (((1,), (1,)), ((), ())), "tn": (((0,), (0,)), ((), ()))}[dims]
    return lax.dot_general(a, b, dn, precision=precision, preferred_element_type=F32)


def _split_bf16(x):
    hi = x.astype(BF16)
    return hi, (x - hi.astype(F32)).astype(BF16)


def _bdot_raw(a, b, ca, cb, passes):
    dn = (((ca,), (cb,)), ((0,), (0,)))
    mm = lambda p, q: lax.dot_general(p, q, dn, preferred_element_type=F32)
    if passes == 1:
        return mm(a.astype(BF16), b.astype(BF16))
    ah, al = _split_bf16(a)
    bh, bl = _split_bf16(b)
    return mm(ah, bh) + (mm(ah, bl) + mm(al, bh))


@functools.partial(jax.custom_vjp, nondiff_argnums=(2, 3, 4))
def _bdot_p(a, b, ca, cb, passes):
    return _bdot_raw(a, b, ca, cb, passes)


def _bdot_fwd(a, b, ca, cb, passes):
    return _bdot_raw(a, b, ca, cb, passes), (a, b)


def _bdot_bwd(ca, cb, passes, res, g):
    a, b = res
    if (ca, cb) == (2, 1):
        return _bdot_p(g, b, 2, 2, passes), _bdot_p(a, g, 1, 1, passes)
    if (ca, cb) == (2, 2):
        return _bdot_p(g, b, 2, 1, passes), _bdot_p(g, a, 1, 1, passes)
    assert (ca, cb) == (1, 1)
    return _bdot_p(b, g, 2, 2, passes), _bdot_p(a, g, 2, 1, passes)


_bdot_p.defvjp(_bdot_fwd, _bdot_bwd)


def _bdot(a, b, ca, cb, passes=3):
    return _bdot_p(a, b, ca, cb, passes)


def _dot3(a, b):
    return _bdot(a[None], b[None], 2, 1)[0]


@jax.custom_vjp
def _xdot(x, m, mt):
    hi, lo = _split_bf16(x)
    m16 = m.astype(BF16)
    return _dot(hi, m16) + _dot(lo, m16)


def _xdot_fwd(x, m, mt):
    return _xdot(x, m, mt), (m, mt)


def _xdot_bwd(res, g):
    m, mt = res
    return _xdot(g, mt, m), jnp.zeros_like(m), jnp.zeros_like(mt)


_xdot.defvjp(_xdot_fwd, _xdot_bwd)


class _Cfg:
    def __init__(self, T, D, lora):
        self.T, self.D, self.lora = T, D, lora
        self.FW = D // 2
        self.FH = self.FW // FOX_HEAD_DIM
        self.RW = D // 2
        self.RH = self.RW // RWKV_HEAD_DIM
        self.LP = -(-lora // LANES) * LANES
        self.o_fox = 0
        self.o_rwkv = 4 * self.FW
        self.o_gate = self.o_rwkv + 4 * self.RW
        self.o_f = self.o_gate + 2 * D
        self.o_wd = self.o_f + LANES
        self.o_ad = self.o_wd + self.LP
        end = self.o_ad + self.LP
        self.tn = 1280 if D >= 2048 else LANES
        self.ncol = -(-end // self.tn) * self.tn
        self.in_cols = 4 * self.FW + self.FH + 4 * self.RW + 2 * lora + 2 * D
        self.rseg = 4 * self.RW + 2 * self.LP
        self.C = min(SCAN_CHUNK, T)
        self.tr = min(256, T)
        self.hb = min(SCAN_HEADS_PER_STEP, self.RH)

    def segments(self):
        FW, FH, RW, lo, D = self.FW, self.FH, self.RW, self.lora, self.D
        g_f = 4 * FW
        g_r = g_f + FH
        g_wd = g_r + 4 * RW
        g_ad = g_wd + lo
        g_g = g_ad + lo
        dh = FOX_HEAD_DIM
        qkv = [(j * FW + h * dh, dh, (3 * h + j) * dh) for h in range(FH) for j in range(3)]
        return qkv + [(3 * FW, FW, 3 * FW), (g_f, FH, self.o_f), (g_r, 4 * RW, self.o_rwkv), (g_wd, lo, self.o_wd),
                      (g_ad, lo, self.o_ad), (g_g, 2 * D, self.o_gate)]


def _shards_to_my_layout(cfg, g):
    R, sc = g.shape[1], g.shape[2]
    segs = sorted(cfg.segments(), key=lambda s: s[2])
    parts, pos = [], 0
    for g0, w, m0 in segs:
        if m0 > pos:
            parts.append(jnp.zeros((R, m0 - pos), g.dtype))
        for s in range(N_CHIPS):
            lo, hi = max(g0, s * sc), min(g0 + w, (s + 1) * sc)
            if lo < hi:
                parts.append(g[s, :, lo - s * sc:hi - s * sc])
        pos = m0 + w
    if cfg.ncol > pos:
        parts.append(jnp.zeros((R, cfg.ncol - pos), g.dtype))
    return jnp.concatenate(parts, axis=1)


def _my_layout_to_shards(cfg, wm):
    sc = cfg.in_cols // N_CHIPS
    segs = sorted(cfg.segments(), key=lambda s: s[0])
    shards = []
    for s in range(N_CHIPS):
        parts = []
        for g0, w, m0 in segs:
            lo, hi = max(g0, s * sc), min(g0 + w, (s + 1) * sc)
            if lo < hi:
                parts.append(wm[:, m0 + lo - g0:m0 + hi - g0])
        shards.append(jnp.concatenate(parts, axis=1))
    return jnp.stack(shards, axis=0)


def _rwkv_vec_to_my(cfg, v):
    RW4, lo, LP = 4 * cfg.RW, cfg.lora, cfg.LP
    z = jnp.zeros((1, LP - lo), v.dtype)
    return jnp.concatenate([v[:, :RW4], v[:, RW4:RW4 + lo], z, v[:, RW4 + lo:], z], axis=1)


def _rwkv_vec_from_my(cfg, v):
    RW4, lo, LP = 4 * cfg.RW, cfg.lora, cfg.LP
    return jnp.concatenate([v[:, :RW4], v[:, RW4:RW4 + lo], v[:, RW4 + LP:RW4 + LP + lo]], axis=1)


def _comm_at(comm, which, steps, cin, cout, scr):
    if not comm or len(comm) <= which:
        return
    lin, total = 0, 1
    for d, n in enumerate(steps):
        lin = lin * n + pl.program_id(d)
        total *= n
    pl.when(lin == {3: 0, 4: total - 1, 5: total // 2}[which])(lambda: comm[which](cin, cout, scr))


def _mm(name, a, b, dims, out_dtype, tm, tn, tk, comm=None):
    (M, K) = a.shape if dims != "tn" else a.shape[::-1]
    N = b.shape[0] if dims == "nt" else b.shape[1]
    tm, tn, tk = min(tm, M), min(tn, N), min(tk, K)
    assert M % tm == 0 and N % tn == 0 and K % tk == 0, (name, M, N, K, tm, tn, tk)
    nk = K // tk
    steps = (M // tm, N // tn, nk)
    c_in, c_out, c_scr = comm[:3] if comm else ([], [], [])
    if dims == "nn":
        a_spec = pl.BlockSpec((tm, tk), lambda i, j, k: (i, k))
        b_spec = pl.BlockSpec((tk, tn), lambda i, j, k: (k, j))
    elif dims == "nt":
        a_spec = pl.BlockSpec((tm, tk), lambda i, j, k: (i, k))
        b_spec = pl.BlockSpec((tn, tk), lambda i, j, k: (j, k))
    else:
        a_spec = pl.BlockSpec((tk, tm), lambda i, j, k: (k, i))
        b_spec = pl.BlockSpec((tk, tn), lambda i, j, k: (k, j))

    n_acc = 1 if nk > 1 else 0

    def body(a_ref, b_ref, *rest):
        cin, o_ref = rest[:len(c_in)], rest[len(c_in)]
        cout = rest[len(c_in) + 1:len(c_in) + 1 + len(c_out)]
        scr = rest[len(c_in) + 1 + len(c_out):]
        _comm_at(comm, 3, steps, cin, cout, scr[n_acc:])
        if nk == 1:
            o_ref[...] = _dot(a_ref[...], b_ref[...], dims).astype(o_ref.dtype)
        else:
            acc_ref, k = scr[0], pl.program_id(2)

            @pl.when(k == 0)
            def _():
                acc_ref[...] = jnp.zeros_like(acc_ref)

            acc_ref[...] += _dot(a_ref[...], b_ref[...], dims)

            @pl.when(k == nk - 1)
            def _():
                o_ref[...] = acc_ref[...].astype(o_ref.dtype)

        _comm_at(comm, 5, steps, cin, cout, scr[n_acc:])
        _comm_at(comm, 4, steps, cin, cout, scr[n_acc:])

    res = _pcall(
        body, name=name, grid=steps,
        in_specs=[a_spec, b_spec] + [_ANY] * len(c_in),
        out_specs=[pl.BlockSpec((tm, tn), lambda i, j, k: (i, j))] + [_ANY] * len(c_out),
        out_shape=[jax.ShapeDtypeStruct((M, N), out_dtype)] + list(c_out),
        scratch_shapes=([pltpu.VMEM((tm, tn), F32)] if nk > 1 else []) + list(c_scr),
        compiler_params=_cparams(("arbitrary",) * 3 if comm else ("parallel", "parallel", "arbitrary")),
    )(a, b, *c_in)
    return res if comm else res[0]


def _tile(tr, w, cb=0):
    return pl.BlockSpec((tr, w), lambda i: (i, cb))


def _const(shape):
    nd = len(shape)
    return pl.BlockSpec(shape, lambda i: (0,) * nd)


def _acc_store(i, ref, val):
    @pl.when(i == 0)
    def _():
        ref[...] = val

    @pl.when(i > 0)
    def _():
        ref[...] += val


def _rms_fwd(cfg, x2, g):
    T, D, tr = cfg.T, cfg.D, cfg.tr

    def body(x_ref, g_ref, h_ref):
        h_ref[...] = _rmsn(x_ref[...], g_ref[...]).astype(BF16)

    return _pcall(body, name="rms_fwd", grid=(T // tr,), in_specs=[_tile(tr, D), _const((1, D))],
                  out_specs=_tile(tr, D), out_shape=jax.ShapeDtypeStruct((T, D), BF16),
                  compiler_params=_cparams(("parallel",)))(x2, g)


def _rms_bwd(cfg, x2, g, dh, dres, comm=None):
    T, D, tr = cfg.T, cfg.D, cfg.tr
    c_in, c_out, c_scr = comm[:3] if comm else ([], [], [])
    steps = (T // tr,)

    def body(x_ref, g_ref, dh_ref, dres_ref, *rest):
        cin, (gx_ref, dg_ref) = rest[:len(c_in)], rest[len(c_in):len(c_in) + 2]
        cout, scr = rest[len(c_in) + 2:len(c_in) + 2 + len(c_out)], rest[len(c_in) + 2 + len(c_out):]
        _comm_at(comm, 3, steps, cin, cout, scr)
        _, vjp = jax.vjp(_rmsn, x_ref[...], g_ref[...])
        dx, dg = vjp(dh_ref[...])
        gx_ref[...] = dx + dres_ref[...]
        _acc_store(pl.program_id(0), dg_ref, dg)
        _comm_at(comm, 4, steps, cin, cout, scr)

    return _pcall(body, name="rms_bwd", grid=steps,
                  in_specs=[_tile(tr, D), _const((1, D)), _tile(tr, D), _tile(tr, D)] + [_ANY] * len(c_in),
                  out_specs=[_tile(tr, D), _const((1, D))] + [_ANY] * len(c_out),
                  out_shape=[jax.ShapeDtypeStruct((T, D), F32), jax.ShapeDtypeStruct((1, D), F32)] + list(c_out),
                  scratch_shapes=list(c_scr), compiler_params=_cparams(("arbitrary",)))(x2, g, dh, dres, *c_in)


def _final(cfg, x2, mo, fg, target):
    T, D, tr = cfg.T, cfg.D, cfg.tr

    def loss_fn(hres, g, tgt):
        err = _rmsn(hres, g) - tgt
        return 0.5 * jnp.sum(jnp.mean(err * err, axis=-1, keepdims=True), axis=0, keepdims=True)

    def body(x_ref, mo_ref, g_ref, t_ref, loss_ref, dres_ref, dres16_ref, dg_ref):
        hres = x_ref[...] + mo_ref[...]
        loss, vjp = jax.vjp(functools.partial(loss_fn, tgt=t_ref[...]), hres, g_ref[...])
        dres, dg = vjp(jnp.ones((1, 1), F32))
        dres_ref[...] = dres
        dres16_ref[...] = dres.astype(BF16)
        i = pl.program_id(0)
        _acc_store(i, dg_ref, dg)
        _acc_store(i, loss_ref, jnp.broadcast_to(loss, (8, LANES)))

    return _pcall(body, name="final_loss", grid=(T // tr,),
                  in_specs=[_tile(tr, D), _tile(tr, D), _const((1, D)), _tile(tr, D)],
                  out_specs=[_const((8, LANES)), _tile(tr, D), _tile(tr, D), _const((1, D))],
                  out_shape=[jax.ShapeDtypeStruct((8, LANES), F32), jax.ShapeDtypeStruct((T, D), F32),
                             jax.ShapeDtypeStruct((T, D), BF16), jax.ShapeDtypeStruct((1, D), F32)],
                  compiler_params=_cparams(("arbitrary",)))(x2, mo, fg, target)


def _merge_fn(pa, pb, ga, gb):
    return jax.nn.sigmoid(ga) * pa + jax.nn.sigmoid(gb) * pb


def _merge_fwd(cfg, pa, pb, u):
    T, D, tr = cfg.T, cfg.D, cfg.tr
    cga, cgb = cfg.o_gate // D, cfg.o_gate // D + 1

    def body(pa_ref, pb_ref, ga_ref, gb_ref, m_ref):
        m_ref[...] = _merge_fn(pa_ref[...], pb_ref[...], ga_ref[...], gb_ref[...]).astype(BF16)

    return _pcall(body, name="merge_fwd", grid=(T // tr,),
                  in_specs=[_tile(tr, D), _tile(tr, D), _tile(tr, D, cga), _tile(tr, D, cgb)],
                  out_specs=_tile(tr, D), out_shape=jax.ShapeDtypeStruct((T, D), BF16),
                  compiler_params=_cparams(("parallel",)))(pa, pb, u, u)


def _merge_bwd(cfg, pa, pb, u, dm):
    T, D, tr = cfg.T, cfg.D, cfg.tr
    cga, cgb = cfg.o_gate // D, cfg.o_gate // D + 1

    def body(pa_ref, pb_ref, ga_ref, gb_ref, dm_ref, dpa_ref, dpb_ref, dg_ref):
        _, vjp = jax.vjp(_merge_fn, pa_ref[...], pb_ref[...], ga_ref[...], gb_ref[...])
        dpa, dpb, dga, dgb = vjp(dm_ref[...])
        dpa_ref[...] = dpa.astype(BF16)
        dpb_ref[...] = dpb.astype(BF16)
        dg_ref[:, :D] = dga.astype(BF16)
        dg_ref[:, D:] = dgb.astype(BF16)

    return _pcall(body, name="merge_bwd", grid=(T // tr,),
                  in_specs=[_tile(tr, D), _tile(tr, D), _tile(tr, D, cga), _tile(tr, D, cgb), _tile(tr, D)],
                  out_specs=[_tile(tr, D), _tile(tr, D), _tile(tr, 2 * D, cfg.o_gate // (2 * D))],
                  out_shape=[jax.ShapeDtypeStruct((T, D), BF16), jax.ShapeDtypeStruct((T, D), BF16),
                             jax.ShapeDtypeStruct((T, cfg.ncol), BF16)],
                  compiler_params=_cparams(("parallel",)))(pa, pb, u, u, dm)


def _gate_fn(o, z):
    return o * _silu(z)


def _gate_a_fwd(cfg, o, u):
    T, FW, tr = cfg.T, cfg.FW, cfg.tr

    def body(o_ref, z_ref, oa_ref):
        oa_ref[...] = _gate_fn(o_ref[...], z_ref[...]).astype(BF16)

    return _pcall(body, name="gate_a_fwd", grid=(T // tr,), in_specs=[_tile(tr, FW), _tile(tr, FW, 3)],
                  out_specs=_tile(tr, FW), out_shape=jax.ShapeDtypeStruct((T, FW), BF16),
                  compiler_params=_cparams(("parallel",)))(o, u)


def _gate_a_bwd(cfg, o, u, doa, du):
    T, FW, tr = cfg.T, cfg.FW, cfg.tr

    def body(o_ref, z_ref, doa_ref, du_in, do_ref, dz_ref):
        _, vjp = jax.vjp(_gate_fn, o_ref[...], z_ref[...])
        do, dz = vjp(doa_ref[...])
        do_ref[...] = do
        dz_ref[...] = dz.astype(BF16)

    return _pcall(body, name="gate_a_bwd", grid=(T // tr,),
                  in_specs=[_tile(tr, FW), _tile(tr, FW, 3), _tile(tr, FW), _ANY],
                  out_specs=[_tile(tr, FW), _tile(tr, FW, 3)],
                  out_shape=[jax.ShapeDtypeStruct((T, FW), F32), jax.ShapeDtypeStruct(du.shape, BF16)],
                  input_output_aliases={3: 1},
                  compiler_params=_cparams(("parallel",)))(o, u, doa, du)


def _fox_prep(cfg, u, fb):
    T, tr = cfg.T, cfg.tr
    cf = cfg.o_f // LANES

    def body(f_ref, fb_ref, c_ref, carry_ref):
        i = pl.program_id(0)

        @pl.when(i == 0)
        def _():
            carry_ref[...] = jnp.zeros_like(carry_ref)

        lf = -_softplus(-(f_ref[...] + fb_ref[...]))
        r = lax.broadcasted_iota(jnp.int32, (tr, tr), 0)
        c = lax.broadcasted_iota(jnp.int32, (tr, tr), 1)
        tri = (r >= c).astype(F32)
        c_ref[...] = _dot(tri, lf, precision=HI) + carry_ref[...]
        carry_ref[...] += jnp.sum(lf, axis=0, keepdims=True)

    return _pcall(body, name="fox_prep", grid=(T // tr,), in_specs=[_tile(tr, LANES, cf), _const((1, LANES))],
                  out_specs=_tile(tr, LANES), out_shape=jax.ShapeDtypeStruct((T, LANES), F32),
                  scratch_shapes=[pltpu.VMEM((1, LANES), F32)], compiler_params=_cparams(("arbitrary",)))(u, fb)


def _fox_prep_bwd(cfg, u, fb, dc):
    T, tr = cfg.T, cfg.tr
    cf = cfg.o_f // LANES
    nb = T // tr

    def body(f_ref, fb_ref, dc_ref, df_ref, dfb_ref, carry_ref):
        i = pl.program_id(0)

        @pl.when(i == 0)
        def _():
            carry_ref[...] = jnp.zeros_like(carry_ref)

        dc = dc_ref[...]
        r = lax.broadcasted_iota(jnp.int32, (tr, tr), 0)
        c = lax.broadcasted_iota(jnp.int32, (tr, tr), 1)
        triu = (r <= c).astype(F32)
        dlf = _dot(triu, dc, precision=HI) + carry_ref[...]
        carry_ref[...] += jnp.sum(dc, axis=0, keepdims=True)
        dz = dlf * jax.nn.sigmoid(-(f_ref[...] + fb_ref[...]))
        df_ref[...] = dz.astype(BF16)
        _acc_store(i, dfb_ref, jnp.sum(dz, axis=0, keepdims=True))

    rev = lambda i: (nb - 1 - i, 0)
    return _pcall(body, name="fox_prep_bwd", grid=(nb,),
                  in_specs=[pl.BlockSpec((tr, LANES), lambda i: (nb - 1 - i, cf)), _const((1, LANES)),
                            pl.BlockSpec((tr, LANES), rev)],
                  out_specs=[pl.BlockSpec((tr, LANES), rev), _const((1, LANES))],
                  out_shape=[jax.ShapeDtypeStruct((T, LANES), BF16), jax.ShapeDtypeStruct((1, LANES), F32)],
                  scratch_shapes=[pltpu.VMEM((1, LANES), F32)], compiler_params=_cparams(("arbitrary",)))(u, fb, dc)


def _attn_logits(q_ref, k_ref, c_ref, i, tq, te):
    s = _dot(q_ref[...].astype(BF16), k_ref[0:te, :].astype(BF16), "nt") * (FOX_HEAD_DIM ** -0.5) - c_ref[0, :, 0:te]
    row = i * tq + lax.broadcasted_iota(jnp.int32, (tq, te), 0)
    col = lax.broadcasted_iota(jnp.int32, (tq, te), 1)
    return jnp.where(col <= row, s, -1e30)


def _per_query_tile(i, nq, tq, fn):
    for ii in range(nq):
        pl.when(i == ii)(functools.partial(fn, (ii + 1) * tq))


def _attn_fwd(cfg, u, c_rows):
    T, FW, FH = cfg.T, cfg.FW, cfg.FH
    tq = min(256, T)
    dh = FOX_HEAD_DIM

    def body(q_ref, k_ref, v_ref, c_ref, o_ref, lse_ref):
        i = pl.program_id(1)

        def tile(te):
            s = _attn_logits(q_ref, k_ref, c_ref, i, tq, te)
            m = jnp.max(s, axis=1, keepdims=True)
            p = jnp.exp(s - m)
            l = jnp.sum(p, axis=1, keepdims=True)
            o_ref[...] = _dot(p.astype(BF16), v_ref[0:te, :].astype(BF16)) / l
            lse_ref[0] = m + jnp.log(l)

        _per_query_tile(i, T // tq, tq, tile)

    return _pcall(
        body, name="fox_attn_fwd", grid=(FH, T // tq),
        in_specs=[pl.BlockSpec((tq, dh), lambda h, i: (i, 3 * h)), pl.BlockSpec((T, dh), lambda h, i: (0, 3 * h + 1)),
                  pl.BlockSpec((T, dh), lambda h, i: (0, 3 * h + 2)), pl.BlockSpec((1, 1, T), lambda h, i: (h, 0, 0))],
        out_specs=[pl.BlockSpec((tq, dh), lambda h, i: (i, h)), pl.BlockSpec((1, tq, 1), lambda h, i: (h, i, 0))],
        out_shape=[jax.ShapeDtypeStruct((T, FW), F32), jax.ShapeDtypeStruct((FH, T, 1), F32)],
        compiler_params=_cparams(("parallel", "arbitrary")),
    )(u, u, u, c_rows)


def _attn_bwd(cfg, u, c_rows, lse, do, du):
    T, FW, FH = cfg.T, cfg.FW, cfg.FH
    tq = min(256, T)
    nq = T // tq
    dh = FOX_HEAD_DIM
    scale = dh ** -0.5

    def body(q_ref, k_ref, v_ref, c_ref, lse_ref, do_ref, du_in, du_ref, dcol_ref, dk_acc, dv_acc):
        i = pl.program_id(1)

        @pl.when(i == 0)
        def _():
            dk_acc[...] = jnp.zeros_like(dk_acc)
            dv_acc[...] = jnp.zeros_like(dv_acc)
            dcol_ref[...] = jnp.zeros_like(dcol_ref)

        def tile(te):
            s = _attn_logits(q_ref, k_ref, c_ref, i, tq, te)
            p = jnp.exp(s - lse_ref[0])
            do_v = do_ref[...]
            dp = _dot(do_v.astype(BF16), v_ref[0:te, :].astype(BF16), "nt")
            delta = jnp.sum(p * dp, axis=1, keepdims=True)
            ds = p * (dp - delta)
            ds16 = ds.astype(BF16)
            du_ref[te - tq:te, 0:dh] = (_dot(ds16, k_ref[0:te, :].astype(BF16)) * scale).astype(BF16)
            dk_acc[0:te, :] += _dot(ds16, q_ref[...].astype(BF16), "tn") * scale
            dv_acc[0:te, :] += _dot(p.astype(BF16), do_v.astype(BF16), "tn")
            dcol_ref[0, :, 0:te] += jnp.sum(ds, axis=0, keepdims=True)

        _per_query_tile(i, nq, tq, tile)

        @pl.when(i == nq - 1)
        def _():
            du_ref[:, dh:2 * dh] = dk_acc[...].astype(BF16)
            du_ref[:, 2 * dh:3 * dh] = dv_acc[...].astype(BF16)

    return _pcall(
        body, name="fox_attn_bwd", grid=(FH, nq),
        in_specs=[pl.BlockSpec((tq, dh), lambda h, i: (i, 3 * h)), pl.BlockSpec((T, dh), lambda h, i: (0, 3 * h + 1)),
                  pl.BlockSpec((T, dh), lambda h, i: (0, 3 * h + 2)), pl.BlockSpec((1, 1, T), lambda h, i: (h, 0, 0)),
                  pl.BlockSpec((1, tq, 1), lambda h, i: (h, i, 0)), pl.BlockSpec((tq, dh), lambda h, i: (i, h)), _ANY],
        out_specs=[pl.BlockSpec((T, 3 * dh), lambda h, i: (0, h)), pl.BlockSpec((1, 1, T), lambda h, i: (h, 0, 0))],
        out_shape=[jax.ShapeDtypeStruct(du.shape, BF16), jax.ShapeDtypeStruct((FH, 1, T), F32)],
        scratch_shapes=[pltpu.VMEM((T, dh), F32), pltpu.VMEM((T, dh), F32)],
        input_output_aliases={6: 0},
        compiler_params=_cparams(("parallel", "arbitrary")),
    )(u, u, u, c_rows, lse, do, du)


def _head_indicators(cfg):
    ind = np.zeros((cfg.RW, LANES), np.float32)
    ind[np.arange(cfg.RW), np.arange(cfg.RW) // RWKV_HEAD_DIM] = 1.0
    pad = np.zeros((1, LANES), np.float32)
    pad[0, cfg.RH:] = 1.0
    return jnp.asarray(ind), jnp.asarray(ind.T.copy()), jnp.asarray(pad)


def _prep_fn(us_r, us_k, us_v, us_wd, us_ad, w0, w2p, a0, a2p, k_k, k_a, ind, ind_t, pad):
    wpre = w0 + _dot3(jnp.tanh(us_wd), w2p)
    w = -_softplus(-wpre) - 0.5
    lw = -jnp.exp(w)
    a = jax.nn.sigmoid(a0 + _dot3(us_ad, a2p))
    kk = us_k * k_k
    ss = _xdot(kk * kk, ind, ind_t) + pad
    inv = 1.0 / jnp.maximum(jnp.sqrt(ss), L2_EPS)
    kkn = kk * _xdot(inv, ind_t, ind)
    kp = us_k * (1.0 + (a - 1.0) * k_a)
    return us_r, lw, kp, us_v, -kkn, kkn * a


def _shifted(u, prev_row, mu, first):
    n = u.shape[0]
    rolled = pltpu.roll(u, 1, 0)
    row = lax.broadcasted_iota(jnp.int32, u.shape, 0)
    p0 = jnp.where(first, jnp.zeros_like(prev_row), prev_row)
    prev = jnp.where(row == 0, jnp.broadcast_to(p0, u.shape), rolled)
    return u + (prev - u) * mu, prev


def _rwkv_specs(cfg, tr):
    RW, LP = cfg.RW, cfg.LP
    base = cfg.o_rwkv // RW
    cols = [(RW, base), (RW, base + 1), (RW, base + 2), (RW, base + 3), (LP, cfg.o_wd // LP), (LP, cfg.o_ad // LP)]
    cur = [pl.BlockSpec((tr, w), (lambda i, cb=cb: (i, cb))) for w, cb in cols]
    prv = [pl.BlockSpec((8, w), (lambda i, cb=cb: (jnp.maximum(i * (tr // 8) - 1, 0), cb))) for w, cb in cols]
    return cols, cur, prv


def _mu_pieces(cfg, mu_ref):
    RW, LP = cfg.RW, cfg.LP
    offs = [0, RW, 2 * RW, 3 * RW, 4 * RW, 4 * RW + LP, 4 * RW + 2 * LP]
    return [mu_ref[:, offs[j]:offs[j + 1]] for j in range(6)]


def _rwkv_prep_fwd(cfg, u, mu, w0, w2p, a0, a2p, k_k, k_a):
    T, RW, LP, tr = cfg.T, cfg.RW, cfg.LP, cfg.tr
    ind, ind_t, pad = _head_indicators(cfg)
    cols, cur, prv = _rwkv_specs(cfg, tr)

    def body(*refs):
        u_refs, p_refs = refs[0:6], refs[6:12]
        mu_ref, w0_ref, w2_ref, a0_ref, a2_ref, kk_ref, ka_ref, ind_ref, indt_ref, pad_ref = refs[12:22]
        outs = refs[22:]
        first = pl.program_id(0) == 0
        mus = _mu_pieces(cfg, mu_ref)
        us = [_shifted(u_refs[j][...], p_refs[j][7:8, :], mus[j], first)[0] for j in range(6)]
        res = _prep_fn(us[0], us[1], us[2], us[4], us[5], w0_ref[...], w2_ref[...], a0_ref[...], a2_ref[...],
                       kk_ref[...], ka_ref[...], ind_ref[...], indt_ref[...], pad_ref[...])
        for j in range(6):
            outs[j][...] = res[j]
        outs[6][...] = us[3]

    consts = [mu, w0, w2p, a0, a2p, k_k, k_a, ind, ind_t, pad]
    return _pcall(body, name="rwkv_prep_fwd", grid=(T // tr,),
                  in_specs=cur + prv + [_const(c.shape) for c in consts],
                  out_specs=[_tile(tr, RW)] * 7, out_shape=[jax.ShapeDtypeStruct((T, RW), F32)] * 7,
                  compiler_params=_cparams(("parallel",)))(*([u] * 12), *consts)


def _rwkv_prep_bwd(cfg, u, mu, w0, w2p, a0, a2p, k_k, k_a, cots, dzb):
    T, RW, LP = cfg.T, cfg.RW, cfg.LP
    tr = min(128, T)
    ind, ind_t, pad = _head_indicators(cfg)
    cols, cur, prv = _rwkv_specs(cfg, tr)
    rseg = cfg.rseg

    def body(*refs):
        u_refs, p_refs = refs[0:6], refs[6:12]
        mu_ref, w0_ref, w2_ref, a0_ref, a2_ref, kk_ref, ka_ref, ind_ref, indt_ref, pad_ref = refs[12:22]
        cot_refs, dzb_ref = refs[22:28], refs[28]
        dus_ref, dmu_ref, dw0_ref, dw2_ref, da0_ref, da2_ref, dkk_ref, dka_ref = refs[29:]
        i = pl.program_id(0)
        first = i == 0
        mus = _mu_pieces(cfg, mu_ref)
        sh = [_shifted(u_refs[j][...], p_refs[j][7:8, :], mus[j], first) for j in range(6)]
        us = [s[0] for s in sh]
        fn = functools.partial(_prep_fn, ind=ind_ref[...], ind_t=indt_ref[...], pad=pad_ref[...])
        _, vjp = jax.vjp(fn, us[0], us[1], us[2], us[4], us[5], w0_ref[...], w2_ref[...], a0_ref[...], a2_ref[...],
                         kk_ref[...], ka_ref[...])
        d = vjp(tuple(c[...] for c in cot_refs))
        dus = [d[0], d[1], d[2], dzb_ref[...], d[3], d[4]]
        offs = [0, RW, 2 * RW, 3 * RW, 4 * RW, 4 * RW + LP, 4 * RW + 2 * LP]
        for j in range(6):
            dus_ref[:, offs[j]:offs[j + 1]] = dus[j]
            dmu_j = jnp.sum(dus[j] * (sh[j][1] - u_refs[j][...]), axis=0, keepdims=True)

            @pl.when(first)
            def _(j=j, dmu_j=dmu_j):
                dmu_ref[:, offs[j]:offs[j + 1]] = dmu_j

            @pl.when(i > 0)
            def _(j=j, dmu_j=dmu_j):
                dmu_ref[:, offs[j]:offs[j + 1]] += dmu_j
        for ref, val in zip((dw0_ref, dw2_ref, da0_ref, da2_ref, dkk_ref, dka_ref), d[5:11]):
            _acc_store(i, ref, val)

    consts = [mu, w0, w2p, a0, a2p, k_k, k_a, ind, ind_t, pad]
    vec = jax.ShapeDtypeStruct((1, RW), F32)
    mat = jax.ShapeDtypeStruct((LP, RW), F32)
    return _pcall(body, name="rwkv_prep_bwd", grid=(T // tr,),
                  in_specs=cur + prv + [_const(c.shape) for c in consts] + [_tile(tr, RW)] * 7,
                  out_specs=[_tile(tr, rseg), _const((1, rseg)), _const((1, RW)), _const((LP, RW)), _const((1, RW)),
                             _const((LP, RW)), _const((1, RW)), _const((1, RW))],
                  out_shape=[jax.ShapeDtypeStruct((T, rseg), F32), jax.ShapeDtypeStruct((1, rseg), F32),
                             vec, mat, vec, mat, vec, vec],
                  compiler_params=_cparams(("arbitrary",)))(*([u] * 12), *consts, *cots, dzb)


def _shift_bwd(cfg, dus, mu, df, du):
    T, tr, RW, LP = cfg.T, cfg.tr, cfg.RW, cfg.LP
    nb = T // tr
    tail = cfg.ncol - cfg.o_f
    assert cfg.o_rwkv % (4 * RW) == 0 and (4 * RW) % (2 * LP) == 0 and cfg.o_f % tail == 0

    def shifted(d_ref, n_ref, mu_ref):
        d = d_ref[...]
        rolled = pltpu.roll(d, tr - 1, 0)
        row = lax.broadcasted_iota(jnp.int32, d.shape, 0)
        n0 = jnp.where(pl.program_id(0) == nb - 1, jnp.zeros_like(n_ref[0:1, :]), n_ref[0:1, :])
        nxt = jnp.where(row == tr - 1, jnp.broadcast_to(n0, d.shape), rolled)
        mu_v = mu_ref[...]
        return (d * (1.0 - mu_v) + nxt * mu_v).astype(BF16)

    def main_body(d_ref, n_ref, mu_ref, du_in, du_ref):
        du_ref[...] = shifted(d_ref, n_ref, mu_ref)

    def tail_body(d_ref, n_ref, mu_ref, df_ref, du_in, du_ref):
        du_ref[:, 0:LANES] = df_ref[...]
        du_ref[:, LANES:LANES + 2 * LP] = shifted(d_ref, n_ref, mu_ref)
        if tail > LANES + 2 * LP:
            du_ref[:, LANES + 2 * LP:] = jnp.zeros((tr, tail - LANES - 2 * LP), BF16)

    def specs(w, cb):
        return [_tile(tr, w, cb),
                pl.BlockSpec((8, w), lambda i: (jnp.minimum((i + 1) * (tr // 8), T // 8 - 1), cb)),
                pl.BlockSpec((1, w), lambda i: (0, cb))]

    out = jax.ShapeDtypeStruct(du.shape, BF16)
    du = _pcall(main_body, name="shift_bwd_main", grid=(nb,), in_specs=specs(4 * RW, 0) + [_ANY],
                out_specs=_tile(tr, 4 * RW, cfg.o_rwkv // (4 * RW)), out_shape=out, input_output_aliases={3: 0},
                compiler_params=_cparams(("parallel",)))(dus, dus, mu, du)
    return _pcall(tail_body, name="shift_bwd_tail", grid=(nb,),
                  in_specs=specs(2 * LP, 4 * RW // (2 * LP)) + [_tile(tr, LANES), _ANY],
                  out_specs=_tile(tr, tail, cfg.o_f // tail), out_shape=out, input_output_aliases={4: 0},
                  compiler_params=_cparams(("parallel",)))(dus, dus, mu, df, du)


def _chunk_local(r, lw, k, v, a, b):
    H, C, K = r.shape
    row = lax.broadcasted_iota(jnp.int32, (C, C), 0)
    col = lax.broadcasted_iota(jnp.int32, (C, C), 1)
    incl = jnp.broadcast_to((row >= col).astype(F32)[None], (H, C, C))
    strict = (row > col)[None]
    lower = (row >= col)[None]
    eye = (row == col)[None]
    zero = jnp.zeros((), F32)
    L = _bdot(incl, lw, 2, 1)
    LC = jnp.sum(lw, axis=1, keepdims=True)
    eL = jnp.exp(L)
    eLn = jnp.exp(-L)
    at = a * jnp.exp(L - lw)
    rt = r * eL
    bt = b * eLn
    kt = k * eLn
    eR = jnp.exp(LC - L)
    bh = b * eR
    kh = k * eR
    gram = functools.partial(_bdot, passes=SCAN_PASSES[0])
    inv = functools.partial(_bdot, passes=SCAN_PASSES[1])
    app = functools.partial(_bdot, passes=SCAN_PASSES[2])
    n_ab = jnp.where(strict, gram(at, bt, 2, 2), zero)
    n_ak = jnp.where(strict, gram(at, kt, 2, 2), zero)
    m_rb = jnp.where(lower, gram(rt, bt, 2, 2), zero)
    m_rk = jnp.where(lower, gram(rt, kt, 2, 2), zero)
    M = n_ab
    P = jnp.where(eye, 1.0, zero) + n_ab
    for _ in range(1, max(1, int(np.ceil(np.log2(C))))):
        M = inv(M, M, 2, 1)
        P = P + inv(M, P, 2, 1)
    W = app(P, at, 2, 1)
    Uloc = app(P, app(n_ak, v, 2, 1), 2, 1)
    Q = rt + app(m_rb, W, 2, 1)
    Yloc = app(m_rb, Uloc, 2, 1) + app(m_rk, v, 2, 1)
    A = jnp.where(eye, jnp.exp(LC), zero) + app(W, bh, 1, 1)
    Sloc = app(Uloc, bh, 1, 1) + app(v, kh, 1, 1)
    return Q, Yloc, A, Sloc


def _split_heads(ref, n):
    N = RWKV_HEAD_DIM
    return jnp.stack([ref[:, h * N:(h + 1) * N] for h in range(n)], axis=0)


def _merge_heads(x):
    return jnp.concatenate([x[h] for h in range(x.shape[0])], axis=1)


def _scan_local_specs(cfg):
    N, HB = RWKV_HEAD_DIM, cfg.hb
    grid = (cfg.RH // HB, cfg.T // cfg.C)
    seq = pl.BlockSpec((HB, cfg.C, N), lambda h, j: (h, j, 0))
    mat = pl.BlockSpec((HB, 1, N, N), lambda h, j: (h, j, 0, 0))
    return grid, seq, mat


def _scan_local_fwd(cfg, seqs):
    T, RH, N = cfg.T, cfg.RH, RWKV_HEAD_DIM
    grid, seq, mat = _scan_local_specs(cfg)

    def body(r_ref, lw_ref, k_ref, v_ref, a_ref, b_ref, q_ref, yl_ref, a_out, sl_ref):
        Q, Yloc, A, Sloc = _chunk_local(*[_split_heads(ref, cfg.hb) for ref in (r_ref, lw_ref, k_ref, v_ref, a_ref, b_ref)])
        q_ref[...] = Q
        yl_ref[...] = Yloc
        a_out[:, 0] = A
        sl_ref[:, 0] = Sloc

    tok = pl.BlockSpec((cfg.C, cfg.hb * N), lambda h, j: (j, h))
    sq = jax.ShapeDtypeStruct((RH, T, N), F32)
    mt = jax.ShapeDtypeStruct((RH, T // cfg.C, N, N), F32)
    return _pcall(body, name="rwkv_scan_local_fwd", grid=grid, in_specs=[tok] * 6, out_specs=[seq, seq, mat, mat],
                  out_shape=[sq, sq, mt, mt], compiler_params=_cparams(("parallel", "parallel")))(*seqs)


def _scan_local_bwd(cfg, toks, dq, dy, da, dsl, extra, comm=None):
    T, RW, N = cfg.T, cfg.RW, RWKV_HEAD_DIM
    grid, seq, mat = _scan_local_specs(cfg)
    c_in, c_out, c_scr = comm[:3] if comm else ([], [], [])

    def body(r_ref, lw_ref, k_ref, v_ref, a_ref, b_ref, dq_ref, dy_ref, da_ref, dsl_ref, xr_ref, xk_ref, xv_ref,
             *rest):
        cin, outs = rest[:len(c_in)], rest[len(c_in):len(c_in) + 6]
        cout, scr = rest[len(c_in) + 6:len(c_in) + 6 + len(c_out)], rest[len(c_in) + 6 + len(c_out):]
        _comm_at(comm, 3, grid, cin, cout, scr)
        ins = [_split_heads(ref, cfg.hb) for ref in (r_ref, lw_ref, k_ref, v_ref, a_ref, b_ref)]
        _, vjp = jax.vjp(_chunk_local, *ins)
        d = vjp((dq_ref[...], _split_heads(dy_ref, cfg.hb), da_ref[:, 0], dsl_ref[:, 0]))
        add = {0: xr_ref, 2: xk_ref, 3: xv_ref}
        for j in range(6):
            dj = _merge_heads(d[j])
            outs[j][...] = dj + add[j][...] if j in add else dj
        _comm_at(comm, 4, grid, cin, cout, scr)

    tok = pl.BlockSpec((cfg.C, cfg.hb * N), lambda h, j: (j, h))
    return _pcall(body, name="rwkv_scan_local_bwd", grid=grid,
                  in_specs=[tok] * 6 + [seq, tok, mat, mat] + [tok] * 3 + [_ANY] * len(c_in),
                  out_specs=[tok] * 6 + [_ANY] * len(c_out),
                  out_shape=[jax.ShapeDtypeStruct((T, RW), F32)] * 6 + list(c_out), scratch_shapes=list(c_scr),
                  compiler_params=_cparams(("arbitrary", "arbitrary") if comm else ("parallel", "parallel")),
                  )(*toks, dq, dy, da, dsl, *extra, *c_in)


def _scan_carry_specs(cfg, rev):
    N, RH, C, nc = RWKV_HEAD_DIM, cfg.RH, cfg.C, cfg.T // cfg.C
    at = (lambda j: nc - 1 - j) if rev else (lambda j: j)
    seq = pl.BlockSpec((RH, C, N), lambda j: (0, at(j), 0))
    mat = pl.BlockSpec((RH, 1, N, N), lambda j: (0, at(j), 0, 0))
    return nc, seq, mat


def _scan_carry_fwd(cfg, q, yloc, a, sloc):
    T, RH, N = cfg.T, cfg.RH, RWKV_HEAD_DIM
    nc, seq, mat = _scan_carry_specs(cfg, False)

    def body(q_ref, yl_ref, a_ref, sl_ref, y_ref, ck_ref, s_ref):
        @pl.when(pl.program_id(0) == 0)
        def _():
            s_ref[...] = jnp.zeros_like(s_ref)

        S = s_ref[...]
        ck_ref[:, 0] = S
        y_ref[...] = _merge_heads(_bdot(q_ref[...], S, 2, 2) + yl_ref[...])
        s_ref[...] = _bdot(S, a_ref[:, 0], 2, 1) + sl_ref[:, 0]

    tok = pl.BlockSpec((cfg.C, cfg.RW), lambda j: (j, 0))
    return _pcall(body, name="rwkv_scan_carry_fwd", grid=(nc,), in_specs=[seq, seq, mat, mat], out_specs=[tok, mat],
                  out_shape=[jax.ShapeDtypeStruct((T, cfg.RW), F32), jax.ShapeDtypeStruct((RH, nc, N, N), F32)],
                  scratch_shapes=[pltpu.VMEM((RH, N, N), F32)],
                  compiler_params=_cparams(("arbitrary",)))(q, yloc, a, sloc)


def _scan_carry_bwd(cfg, q, a, ckpt, dy):
    T, RH, N = cfg.T, cfg.RH, RWKV_HEAD_DIM
    nc, seq, mat = _scan_carry_specs(cfg, True)

    def body(q_ref, a_ref, ck_ref, dy_ref, dq_ref, da_ref, dsl_ref, ds_ref):
        @pl.when(pl.program_id(0) == 0)
        def _():
            ds_ref[...] = jnp.zeros_like(ds_ref)

        S, dS, dY = ck_ref[:, 0], ds_ref[...], _split_heads(dy_ref, RH)
        dq_ref[...] = _bdot(dY, S, 2, 1)
        da_ref[:, 0] = _bdot(S, dS, 1, 1)
        dsl_ref[:, 0] = dS
        ds_ref[...] = _bdot(dS, a_ref[:, 0], 2, 2) + _bdot(dY, q_ref[...], 1, 1)

    mt = jax.ShapeDtypeStruct((RH, nc, N, N), F32)
    tok = pl.BlockSpec((cfg.C, cfg.RW), lambda j: (nc - 1 - j, 0))
    return _pcall(body, name="rwkv_scan_carry_bwd", grid=(nc,), in_specs=[seq, mat, mat, tok],
                  out_specs=[seq, mat, mat], out_shape=[jax.ShapeDtypeStruct((RH, T, N), F32), mt, mt],
                  scratch_shapes=[pltpu.VMEM((RH, N, N), F32)],
                  compiler_params=_cparams(("arbitrary",)))(q, a, ckpt, dy)


def _post_fn(y, r, kp, v, zb, ln_w, ln_b, rk, ind, ind_t):
    n = float(RWKV_HEAD_DIM)
    mu = _xdot(_xdot(y, ind, ind_t) / n, ind_t, ind)
    yc = y - mu
    var = _xdot(yc * yc, ind, ind_t) / n
    rstd = _xdot(lax.rsqrt(var + GN_EPS), ind_t, ind)
    yn = yc * rstd * ln_w + ln_b
    bonus = _xdot(_xdot(r * kp * rk, ind, ind_t), ind_t, ind) * v
    return (yn + bonus) * _silu(zb)


def _rwkv_post_fwd(cfg, y, r, kp, v, zb, ln_w, ln_b, rk):
    T, RW, tr = cfg.T, cfg.RW, cfg.tr
    ind, ind_t, _ = _head_indicators(cfg)

    def body(y_ref, r_ref, k_ref, v_ref, z_ref, lw_ref, lb_ref, rk_ref, ind_ref, indt_ref, ob_ref):
        ob_ref[...] = _post_fn(y_ref[...], r_ref[...], k_ref[...], v_ref[...], z_ref[...], lw_ref[...], lb_ref[...],
                               rk_ref[...], ind_ref[...], indt_ref[...]).astype(BF16)

    consts = [ln_w, ln_b, rk, ind, ind_t]
    return _pcall(body, name="rwkv_post_fwd", grid=(T // tr,),
                  in_specs=[_tile(tr, RW)] * 5 + [_const(c.shape) for c in consts],
                  out_specs=_tile(tr, RW), out_shape=jax.ShapeDtypeStruct((T, RW), BF16),
                  compiler_params=_cparams(("parallel",)))(y, r, kp, v, zb, *consts)


def _rwkv_post_bwd(cfg, y, r, kp, v, zb, ln_w, ln_b, rk, dob):
    T, RW = cfg.T, cfg.RW
    tr = min(128, T)
    ind, ind_t, _ = _head_indicators(cfg)

    def body(y_ref, r_ref, k_ref, v_ref, z_ref, lw_ref, lb_ref, rk_ref, ind_ref, indt_ref, dob_ref,
             dy_ref, dr_ref, dk_ref, dv_ref, dz_ref, dlw_ref, dlb_ref, drk_ref):
        fn = functools.partial(_post_fn, ind=ind_ref[...], ind_t=indt_ref[...])
        _, vjp = jax.vjp(fn, y_ref[...], r_ref[...], k_ref[...], v_ref[...], z_ref[...], lw_ref[...], lb_ref[...],
                         rk_ref[...])
        d = vjp(dob_ref[...])
        for ref, val in zip((dy_ref, dr_ref, dk_ref, dv_ref, dz_ref), d[:5]):
            ref[...] = val
        i = pl.program_id(0)
        for ref, val in zip((dlw_ref, dlb_ref, drk_ref), d[5:8]):
            _acc_store(i, ref, val)

    consts = [ln_w, ln_b, rk, ind, ind_t]
    vec = jax.ShapeDtypeStruct((1, RW), F32)
    return _pcall(body, name="rwkv_post_bwd", grid=(T // tr,),
                  in_specs=[_tile(tr, RW)] * 5 + [_const(c.shape) for c in consts] + [_tile(tr, RW)],
                  out_specs=[_tile(tr, RW)] * 5 + [_const((1, RW))] * 3,
                  out_shape=[jax.ShapeDtypeStruct((T, RW), F32)] * 5 + [vec] * 3,
                  compiler_params=_cparams(("arbitrary",)))(y, r, kp, v, zb, *consts, dob)


def _adamw_math(w, g, m, v):
    m = ADAM_B1 * m + (1.0 - ADAM_B1) * g
    v = ADAM_B2 * v + (1.0 - ADAM_B2) * (g * g)
    m_hat = m / (1.0 - ADAM_B1 ** ADAM_STEP)
    v_hat = v / (1.0 - ADAM_B2 ** ADAM_STEP)
    delta = -ADAM_LR * (m_hat / (jnp.sqrt(v_hat) + ADAM_EPS) + ADAM_WD * w)
    return delta, m, v


def _adamw(name, w, g, m, v):
    R, Cc = w.shape
    tr = R
    for nb in range(1, R // 8 + 1):
        if R % nb == 0 and (R // nb) % 8 == 0 and (R // nb) * Cc * 4 <= 2 * 1024 * 1024:
            tr = R // nb
            break

    def body(w_ref, g_ref, m_ref, v_ref, d_ref, nm_ref, nv_ref):
        d, nm, nv = _adamw_math(w_ref[...], g_ref[...], m_ref[...], v_ref[...])
        d_ref[...] = d
        nm_ref[...] = nm
        nv_ref[...] = nv

    spec = _tile(tr, Cc)
    return _pcall(body, name=name, grid=(R // tr,), in_specs=[spec] * 4, out_specs=[spec] * 3,
                  out_shape=[jax.ShapeDtypeStruct((R, Cc), F32)] * 3,
                  compiler_params=_cparams(("parallel",)))(w, g, m, v)


def _row_tile(R, Cc, itemsize, budget=2 * 1024 * 1024):
    for nb in range(1, R // 16 + 1):
        if R % nb == 0 and (R // nb) % 16 == 0 and (R // nb) * Cc * itemsize <= budget:
            return R // nb
    return R


def _add_halves(name, gs, r1, c_idx):
    S, R, Cc = gs.shape
    half = R // 2
    tr = _row_tile(half, Cc, 4)
    nb = half // tr

    def body(c_ref, g_ref, r_ref, o_ref):
        o_ref[...] = (g_ref[...].astype(F32) + r_ref[...].astype(F32)).astype(BF16)

    grid_spec = pltpu.PrefetchScalarGridSpec(
        num_scalar_prefetch=1, grid=(S, nb),
        in_specs=[pl.BlockSpec((1, tr, Cc), lambda s, i, c: (s, c[0] * nb + i, 0)),
                  pl.BlockSpec((1, tr, Cc), lambda s, i, c: (s, i, 0))],
        out_specs=pl.BlockSpec((1, tr, Cc), lambda s, i, c: (s, i, 0)))
    return _pcall(body, name=name, grid_spec=grid_spec, out_shape=jax.ShapeDtypeStruct((S, half, Cc), BF16),
                  compiler_params=_cparams(("parallel", "parallel")))(c_idx, gs, r1)


def _sum_slots(name, r2):
    S, R, Cc = r2.shape
    tr = _row_tile(R, Cc, 4 * S // 2 if r2.dtype == BF16 else 4 * S)

    def body(r_ref, o_ref):
        acc = r_ref[0].astype(F32)
        for s in range(1, S):
            acc = acc + r_ref[s].astype(F32)
        o_ref[...] = acc

    return _pcall(body, name=name, grid=(R // tr,), in_specs=[pl.BlockSpec((S, tr, Cc), lambda i: (0, i, 0))],
                  out_specs=_tile(tr, Cc), out_shape=jax.ShapeDtypeStruct((R, Cc), F32),
                  compiler_params=_cparams(("parallel",)))(r2)


def _sum_chips(name, recv, own, place):
    S, H, Cc = recv.shape
    tr = _row_tile(H, Cc, 4, 1024 * 1024)
    nb = H // tr

    def body(p_ref, r_ref, own_ref, o_ref):
        s = pl.program_id(1)
        me = p_ref[0]

        @pl.when(s == 0)
        def _():
            o_ref[...] = jnp.zeros_like(o_ref)

        @pl.when(s == me)
        def _():
            o_ref[...] += own_ref[0].astype(F32)

        @pl.when(s != me)
        def _():
            o_ref[...] += r_ref[0].astype(F32)

    grid_spec = pltpu.PrefetchScalarGridSpec(
        num_scalar_prefetch=1, grid=(nb, S),
        in_specs=[pl.BlockSpec((1, tr, Cc), lambda i, s, p: (jnp.where(s == p[0], (s + 1) % S, s), i, 0)),
                  pl.BlockSpec((1, tr, Cc), lambda i, s, p: (p[0], i, 0))],
        out_specs=pl.BlockSpec((tr, Cc), lambda i, s, p: (p[1] * nb + i, 0)))
    return _pcall(body, name=name, grid_spec=grid_spec, out_shape=jax.ShapeDtypeStruct((2 * H, Cc), F32),
                  compiler_params=_cparams(("parallel", "arbitrary")))(place, recv, own)


def _cast_bf16(name, w):
    R, Cc = w.shape
    tr = _row_tile(R, Cc, 4)

    def body(w_ref, o_ref):
        o_ref[...] = w_ref[...].astype(BF16)

    return _pcall(body, name=name, grid=(R // tr,), in_specs=[_tile(tr, Cc)], out_specs=_tile(tr, Cc),
                  out_shape=jax.ShapeDtypeStruct((R, Cc), BF16), compiler_params=_cparams(("parallel",)))(w)


_ANY = pl.BlockSpec(memory_space=pl.ANY)


def _place():
    x, y, c = lax.axis_index("x"), lax.axis_index("y"), lax.axis_index("c")
    others = [(1 - x, y), (x, 1 - y), (1 - x, 1 - y)]
    return x, y, c, others


def _gather_weights(shards):
    arrays, out_shapes, scratch, start, finish, middle = _gather_parts(shards)
    n = len(shards)

    def body(*refs):
        ins, outs, sems = refs[:n], refs[n:2 * n], refs[2 * n:]
        start(ins, outs, sems)
        middle(ins, outs, sems)
        finish(ins, outs, sems)

    return _pcall(body, name="gather_weights", in_specs=[_ANY] * n, out_specs=[_ANY] * n, out_shape=out_shapes,
                  scratch_shapes=scratch)(*arrays)


def _gather_parts(shards):
    n = len(shards)
    halves = [s.shape[0] // 2 for s in shards]

    def parts(ins, outs, sems):
        x, y, c, _ = _place()
        me = 2 * x + y
        n1 = (x ^ (1 - c), y ^ c)
        n2 = (x ^ c, y ^ (1 - c))
        s1, s2, sd = 2 * n1[0] + n1[1], 2 * n2[0] + n2[1], 2 * (1 - x) + (1 - y)
        sib = (x, y, 1 - c)

        def rows(k, chip, hc):
            return outs[k].at[chip, pl.ds(hc * halves[k], halves[k]), :]

        def remote(k, j, src, dst, to):
            return pltpu.make_async_remote_copy(src_ref=src, dst_ref=dst, send_sem=sems[0].at[6 * k + j],
                                                recv_sem=sems[1].at[6 * k + j], device_id=to, device_id_type=MESH)

        def copy(k, j):
            if j < 2:
                mine = ins[k].at[pl.ds(c * halves[k], halves[k]), :]
                return remote(k, j, mine, rows(k, me, c), (*(n1 if j == 0 else n2), c))
            land = rows(k, {2: s1, 3: s1, 4: s2, 5: sd}[j], c)
            return remote(k, j, land, land, (*n2, c) if j == 2 else sib)

        def arrived(k, j):
            hc = c if j < 3 else 1 - c
            land = rows(k, {0: s1, 1: s2, 2: sd, 3: s2, 4: s1, 5: sd}[j], hc)
            remote(k, j, land, land, (x, y, c)).wait_recv()

        return copy, arrived

    def start(ins, outs, sems):
        copy, _ = parts(ins, outs, sems)
        for k in range(n):
            copy(k, 0).start()
            copy(k, 1).start()

    def middle(ins, outs, sems):
        copy, arrived = parts(ins, outs, sems)
        for k in range(n):
            arrived(k, 0)
            copy(k, 2).start()
            copy(k, 3).start()
            arrived(k, 1)
            copy(k, 4).start()

    def finish(ins, outs, sems):
        copy, arrived = parts(ins, outs, sems)
        for k in range(n):
            arrived(k, 2)
            copy(k, 5).start()
        for k in range(n):
            for j in (3, 4, 5):
                arrived(k, j)
        for k in range(n):
            for j in range(6):
                copy(k, j).wait_send()

    out_shapes = [jax.ShapeDtypeStruct((N_CHIPS,) + s.shape, s.dtype) for s in shards]
    scratch = [pltpu.SemaphoreType.DMA((6 * n,)), pltpu.SemaphoreType.DMA((6 * n,))]
    return list(shards), out_shapes, scratch, start, finish, middle


def _exchange_halves(name, grads):
    n = len(grads)
    halves = [g.shape[1] // 2 for g in grads]

    def body(*refs):
        ins, outs = refs[:n], refs[n:2 * n]
        send_sems, recv_sems = refs[2 * n:]
        x, y, c, _ = _place()
        cps = []
        for k in range(n):
            src = ins[k].at[:, pl.ds((1 - c) * halves[k], halves[k]), :]
            cp = pltpu.make_async_remote_copy(src_ref=src, dst_ref=outs[k], send_sem=send_sems.at[k],
                                              recv_sem=recv_sems.at[k], device_id=(x, y, 1 - c), device_id_type=MESH)
            cp.start()
            cps.append(cp)
        for cp in cps:
            cp.wait()

    return _pcall(
        body, name=name, in_specs=[_ANY] * n, out_specs=[_ANY] * n,
        out_shape=[jax.ShapeDtypeStruct((g.shape[0], h) + g.shape[2:], g.dtype) for g, h in zip(grads, halves)],
        scratch_shapes=[pltpu.SemaphoreType.DMA((n,)), pltpu.SemaphoreType.DMA((n,))],
    )(*grads)


def _scatter_to_owners(chip_sums):
    n = len(chip_sums)

    def sends(ins, outs, sems):
        x, y, c, others = _place()
        me = 2 * x + y
        return [pltpu.make_async_remote_copy(
            src_ref=ins[k].at[2 * px + py], dst_ref=outs[k].at[me], send_sem=sems[0].at[3 * k + j],
            recv_sem=sems[1].at[3 * k + j], device_id=(px, py, c), device_id_type=MESH)
            for k in range(n) for j, (px, py) in enumerate(others)]

    def start(ins, outs, sems):
        for cp in sends(ins, outs, sems):
            cp.start()

    def finish(ins, outs, sems):
        x, y, c, others = _place()
        for k in range(n):
            for j, (px, py) in enumerate(others):
                land = outs[k].at[2 * px + py]
                pltpu.make_async_remote_copy(src_ref=land, dst_ref=land, send_sem=sems[0].at[3 * k + j],
                                             recv_sem=sems[1].at[3 * k + j], device_id=(x, y, c),
                                             device_id_type=MESH).wait_recv()
        for cp in sends(ins, outs, sems):
            cp.wait_send()

    out_shapes = [jax.ShapeDtypeStruct(g.shape, g.dtype) for g in chip_sums]
    scratch = [pltpu.SemaphoreType.DMA((3 * n,)), pltpu.SemaphoreType.DMA((3 * n,))]
    return list(chip_sums), out_shapes, scratch, start, finish


def _second_neighbour():
    x, y, c, _ = _place()
    return (x, y, c), (x ^ c, y ^ (1 - c)), (x ^ (1 - c), y ^ c)


def _scatter_stage1(chip_sums):
    n = len(chip_sums)

    def copies(ins, outs, sems):
        (x, y, c), n2, n1 = _second_neighbour()
        diag = 2 * (1 - x) + (1 - y)
        return [pltpu.make_async_remote_copy(
            src_ref=ins[k].at[slot], dst_ref=outs[2 * k + j], send_sem=sems[0].at[2 * k + j],
            recv_sem=sems[1].at[2 * k + j], device_id=(*n2, c), device_id_type=MESH)
            for k in range(n) for j, slot in enumerate((2 * n2[0] + n2[1], diag))]

    def start(ins, outs, sems):
        for cp in copies(ins, outs, sems):
            cp.start()

    def finish(ins, outs, sems):
        for cp in copies(ins, outs, sems):
            cp.wait()

    out_shapes = [jax.ShapeDtypeStruct(g.shape[1:], g.dtype) for g in chip_sums for _ in range(2)]
    scratch = [pltpu.SemaphoreType.DMA((2 * n,)), pltpu.SemaphoreType.DMA((2 * n,))]
    return list(chip_sums), out_shapes, scratch, start, finish


def _scatter_stage2(passed):
    n = len(passed)

    def copies(ins, outs, sems):
        (x, y, c), n2, n1 = _second_neighbour()
        return [pltpu.make_async_remote_copy(src_ref=ins[k], dst_ref=outs[k], send_sem=sems[0].at[k],
                                             recv_sem=sems[1].at[k], device_id=(*n1, c), device_id_type=MESH)
                for k in range(n)]

    def start(ins, outs, sems):
        for cp in copies(ins, outs, sems):
            cp.start()

    def finish(ins, outs, sems):
        for cp in copies(ins, outs, sems):
            cp.wait()

    out_shapes = [jax.ShapeDtypeStruct(p.shape, p.dtype) for p in passed]
    scratch = [pltpu.SemaphoreType.DMA((n,)), pltpu.SemaphoreType.DMA((n,))]
    return list(passed), out_shapes, scratch, start, finish


def _add_passed(name, own, got, slot):
    _, H, Cc = own.shape
    tr = _row_tile(H, Cc, 4)

    def body(s_ref, o_ref, g_ref, out_ref):
        out_ref[...] = (o_ref[0].astype(F32) + g_ref[...].astype(F32)).astype(BF16)

    grid_spec = pltpu.PrefetchScalarGridSpec(
        num_scalar_prefetch=1, grid=(H // tr,),
        in_specs=[pl.BlockSpec((1, tr, Cc), lambda i, s: (s[0], i, 0)), pl.BlockSpec((tr, Cc), lambda i, s: (i, 0))],
        out_specs=pl.BlockSpec((tr, Cc), lambda i, s: (i, 0)))
    return _pcall(body, name=name, grid_spec=grid_spec, out_shape=jax.ShapeDtypeStruct((H, Cc), BF16),
                  compiler_params=_cparams(("parallel",)))(slot, own, got)


def _sum_stages(name, own, direct, via, place):
    _, H, Cc = own.shape
    tr = _row_tile(H, Cc, 4, 1024 * 1024)
    nb = H // tr

    def body(p_ref, own_ref, d_ref, v_ref, o_ref):
        o_ref[...] = (own_ref[0].astype(F32) + d_ref[...].astype(F32)) + v_ref[...].astype(F32)

    flat = pl.BlockSpec((tr, Cc), lambda i, p: (i, 0))
    grid_spec = pltpu.PrefetchScalarGridSpec(
        num_scalar_prefetch=1, grid=(nb,),
        in_specs=[pl.BlockSpec((1, tr, Cc), lambda i, p: (p[0], i, 0)), flat, flat],
        out_specs=pl.BlockSpec((tr, Cc), lambda i, p: (p[1] * nb + i, 0)))
    return _pcall(body, name=name, grid_spec=grid_spec, out_shape=jax.ShapeDtypeStruct((2 * H, Cc), F32),
                  compiler_params=_cparams(("parallel",)))(place, own, direct, via)


def _join_halves(fulls, small):
    n = len(fulls)
    hs = [f.shape[0] // 2 for f in fulls]
    rel = [(dx, dy, dc) for dx in (0, 1) for dy in (0, 1) for dc in (0, 1)][1:]

    def body(*refs):
        ins, small_in = refs[:n], refs[n]
        outs, small_out = refs[n + 1:2 * n + 1], refs[2 * n + 1]
        send_sems, recv_sems, ssend, srecv, local_sem = refs[2 * n + 2:]
        x, y, c, _ = _place()
        dev = 4 * x + 2 * y + c
        local = pltpu.make_async_copy(small_in, small_out.at[dev], local_sem)
        local.start()
        cps = []
        for k in range(n):
            mine = pl.ds(c * hs[k], hs[k])
            cp = pltpu.make_async_remote_copy(src_ref=ins[k].at[mine, :], dst_ref=outs[k].at[mine, :],
                                              send_sem=send_sems.at[k], recv_sem=recv_sems.at[k],
                                              device_id=(x, y, 1 - c), device_id_type=MESH)
            cp.start()
            cps.append(cp)
        for r, (dx, dy, dc) in enumerate(rel):
            cp = pltpu.make_async_remote_copy(src_ref=small_in, dst_ref=small_out.at[dev], send_sem=ssend.at[r],
                                              recv_sem=srecv.at[r], device_id=(x ^ dx, y ^ dy, c ^ dc),
                                              device_id_type=MESH)
            cp.start()
            cps.append(cp)
        for k in range(n):
            land = outs[k].at[pl.ds((1 - c) * hs[k], hs[k]), :]
            pltpu.make_async_remote_copy(src_ref=land, dst_ref=land, send_sem=send_sems.at[k],
                                         recv_sem=recv_sems.at[k], device_id=(x, y, c), device_id_type=MESH).wait_recv()
        for r, (dx, dy, dc) in enumerate(rel):
            land = small_out.at[4 * (x ^ dx) + 2 * (y ^ dy) + (c ^ dc)]
            pltpu.make_async_remote_copy(src_ref=land, dst_ref=land, send_sem=ssend.at[r], recv_sem=srecv.at[r],
                                         device_id=(x, y, c), device_id_type=MESH).wait_recv()
        for cp in cps:
            cp.wait_send()
        local.wait()

    return _pcall(
        body, name="join_halves", in_specs=[_ANY] * (n + 1), out_specs=[_ANY] * (n + 1),
        out_shape=[jax.ShapeDtypeStruct(f.shape, f.dtype) for f in fulls]
        + [jax.ShapeDtypeStruct((N_DEV,) + small.shape, small.dtype)],
        input_output_aliases={k: k for k in range(n)},
        scratch_shapes=[pltpu.SemaphoreType.DMA((n,)), pltpu.SemaphoreType.DMA((n,)), pltpu.SemaphoreType.DMA((7,)),
                        pltpu.SemaphoreType.DMA((7,)), pltpu.SemaphoreType.DMA],
    )(*fulls, small)


def _local_step(cfg, x2, target, norm_gain, w_my, fb, mu_g, w0, a0, k_k, k_a, r_k, ln_w, ln_b, fng, rest,
                exchange=None):
    T, D, FW, FH, RW, RH, LP, lora = cfg.T, cfg.D, cfg.FW, cfg.FH, cfg.RW, cfg.RH, cfg.LP, cfg.lora
    fb_p = jnp.pad(fb, ((0, 0), (0, LANES - FH)))
    mu = _rwkv_vec_to_my(cfg, mu_g)
    rk = r_k.reshape(1, RW)
    tm = min(1024, T)

    h = _rms_fwd(cfg, x2, norm_gain)
    if len(rest) == 2:
        u, *got = _mm("in_proj", h, w_my, "nn", F32, tm, cfg.tn, 2048, comm=rest[0])
        rest = rest[1](got)
    else:
        u = _mm("in_proj", h, w_my, "nn", F32, tm, cfg.tn, 2048)
    w2, a2, wpf, wpr, wout = rest
    w2p = jnp.pad(w2, ((0, LP - lora), (0, 0)))
    a2p = jnp.pad(a2, ((0, LP - lora), (0, 0)))
    c_cols = _fox_prep(cfg, u, fb_p)
    c_rows = c_cols[:, :FH].T.reshape(FH, 1, T)
    o, lse = _attn_fwd(cfg, u, c_rows)
    oa = _gate_a_fwd(cfg, o, u)
    prep = _rwkv_prep_fwd(cfg, u, mu, w0, w2p, a0, a2p, k_k, k_a)
    r, lw, kp, v, an, b, zb = prep
    toks = [r, lw, kp, v, an, b]
    q_s, yloc, a_m, sloc = _scan_local_fwd(cfg, toks)
    y, ckpt = _scan_carry_fwd(cfg, q_s, yloc, a_m, sloc)
    ob = _rwkv_post_fwd(cfg, y, r, kp, v, zb, ln_w, ln_b, rk)
    pa = _mm("proj_fox", oa, wpf, "nn", F32, tm, 1024, 2048)
    pb = _mm("proj_rwkv", ob, wpr, "nn", F32, tm, 1024, 2048)
    m = _merge_fwd(cfg, pa, pb, u)
    mo = _mm("out_proj", m, wout, "nn", F32, tm, 1024, 2048)
    loss8, dres, dres16, d_fng = _final(cfg, x2, mo, fng.reshape(1, D), target)

    dm = _mm("out_proj_dx", dres16, wout, "nt", F32, tm, 1024, 2048)
    d_wout = _mm("out_proj_dw", m, dres16, "tn", BF16, 1024, 1024, 2048)
    dpa, dpb, du = _merge_bwd(cfg, pa, pb, u, dm)
    doa = _mm("proj_fox_dx", dpa, wpf, "nt", F32, tm, 1024, 2048)
    d_wpf = _mm("proj_fox_dw", oa, dpa, "tn", BF16, 1024, 1024, 2048)
    dob = _mm("proj_rwkv_dx", dpb, wpr, "nt", F32, tm, 1024, 2048)
    d_wpr = _mm("proj_rwkv_dw", ob, dpb, "tn", BF16, 1024, 1024, 2048)

    do, du = _gate_a_bwd(cfg, o, u, doa, du)
    du, dcol = _attn_bwd(cfg, u, c_rows, lse, do, du)
    dc = jnp.pad(-dcol.reshape(FH, T).T, ((0, 0), (0, LANES - FH)))
    df, d_fb = _fox_prep_bwd(cfg, u, fb_p, dc)

    dy, dr_p, dk_p, dv_p, dzb, d_lnw, d_lnb, d_rk = _rwkv_post_bwd(cfg, y, r, kp, v, zb, ln_w, ln_b, rk, dob)
    dq_s, da_m, dsl = _scan_carry_bwd(cfg, q_s, a_m, ckpt, dy)
    early = dict(w_proj_fox=d_wpf, w_proj_rwkv=d_wpr, w_out=d_wout)
    res = _scan_local_bwd(cfg, toks, dq_s, dy, da_m, dsl, [dr_p, dk_p, dv_p], exchange(early) if exchange else None)
    cots, received = res[:6], list(res[6:])
    dus, d_mu, d_w0, d_w2p, d_a0, d_a2p, d_kk, d_ka = _rwkv_prep_bwd(cfg, u, mu, w0, w2p, a0, a2p, k_k, k_a, cots, dzb)
    du = _shift_bwd(cfg, dus, mu, df, du)
    d_wmy = _mm("in_proj_dw", h, du, "tn", BF16, 1024, cfg.tn, 2048)
    late = dict(w_in=d_wmy, rwkv_w2=d_w2p[:lora], rwkv_a2=d_a2p[:lora])
    tkx = 2 * cfg.tn if cfg.ncol % (2 * cfg.tn) == 0 else cfg.tn
    res = _mm("in_proj_dx", du, w_my, "nt", F32, tm, 1024, tkx, comm=exchange(late) if exchange else None)
    dh = res[0] if exchange else res
    big = dict(early, **late)
    res = _rms_bwd(cfg, x2, norm_gain, dh, dres, exchange(list(res[1:])) if exchange else None)
    gx, d_ng = res[:2]
    received += list(res[2:])

    small = dict(norm_gain=d_ng, fox_forget_bias=d_fb[:, :FH], rwkv_shift_mix=_rwkv_vec_from_my(cfg, d_mu),
                 rwkv_w0=d_w0, rwkv_a0=d_a0, rwkv_k_k=d_kk, rwkv_k_a=d_ka, rwkv_r_k=d_rk, rwkv_ln_w=d_lnw,
                 rwkv_ln_b=d_lnb, final_norm_gain=d_fng)
    return loss8[0, 0], gx, small, big, received


_SMALL = ["norm_gain", "fox_forget_bias", "rwkv_shift_mix", "rwkv_w0", "rwkv_a0", "rwkv_k_k", "rwkv_k_a", "rwkv_r_k",
          "rwkv_ln_w", "rwkv_ln_b", "final_norm_gain"]
_WEIGHTS = ["norm_gain", "w_in", "fox_forget_bias", "rwkv_shift_mix", "rwkv_w0", "rwkv_w2", "rwkv_a0", "rwkv_a2",
            "rwkv_k_k", "rwkv_k_a", "rwkv_r_k", "rwkv_ln_w", "rwkv_ln_b", "w_proj_fox", "w_proj_rwkv", "w_out",
            "final_norm_gain"]


def _pack_small(arrs):
    parts = []
    for a in arrs:
        f = a.reshape(-1)
        parts.append(jnp.pad(f, (0, (-f.shape[0]) % LANES)))
    flat = jnp.concatenate(parts)
    rows = flat.shape[0] // LANES
    flat = jnp.pad(flat, (0, ((-rows) % 8) * LANES))
    return flat.reshape(-1, LANES)


def _unpack_small(packed, shapes):
    flat = packed.reshape(-1)
    out, pos = [], 0
    for s in shapes:
        n = int(np.prod(s))
        out.append(flat[pos:pos + n].reshape(s))
        pos += n + ((-n) % LANES)
    return out


def _shard_major(a, axis):
    parts = jnp.split(a, N_CHIPS, axis=axis)
    return jnp.stack(parts, axis=0)


def kernel(x, norm_gain, w_in, fox_forget_bias, rwkv_shift_mix, rwkv_w0, rwkv_w2, rwkv_a0, rwkv_a2, rwkv_k_k, rwkv_k_a, rwkv_r_k, rwkv_ln_w, rwkv_ln_b, w_proj_fox, w_proj_rwkv, w_out, final_norm_gain, loss_target, m_norm_gain, m_w_in, m_fox_forget_bias, m_rwkv_shift_mix, m_rwkv_w0, m_rwkv_w2, m_rwkv_a0, m_rwkv_a2, m_rwkv_k_k, m_rwkv_k_a, m_rwkv_r_k, m_rwkv_ln_w, m_rwkv_ln_b, m_w_proj_fox, m_w_proj_rwkv, m_w_out, m_final_norm_gain, v_norm_gain, v_w_in, v_fox_forget_bias, v_rwkv_shift_mix, v_rwkv_w0, v_rwkv_w2, v_rwkv_a0, v_rwkv_a2, v_rwkv_k_k, v_rwkv_k_a, v_rwkv_r_k, v_rwkv_ln_w, v_rwkv_ln_b, v_w_proj_fox, v_w_proj_rwkv, v_w_out, v_final_norm_gain):
    args = dict(locals())
    T, D = x.shape[1], x.shape[2]
    lora = rwkv_w2.shape[1]
    cfg = _Cfg(T, D, lora)
    RW = cfg.RW
    c_idx = lax.axis_index("c").astype(jnp.int32).reshape(1)
    me_chip = (2 * lax.axis_index("x") + lax.axis_index("y")).astype(jnp.int32)
    place = jnp.concatenate([me_chip.reshape(1), c_idx])

    w_in_s = w_in[0].astype(BF16)
    lora_s = jnp.concatenate([rwkv_w2[0], rwkv_a2[0]], axis=0)
    own_slot = lambda g, own: lax.dynamic_update_slice(g, own[None], (me_chip, 0, 0))
    w_my = _shards_to_my_layout(cfg, own_slot(_gather_weights([w_in_s])[0], w_in_s))
    mine = [_cast_bf16("cast_w_proj_fox", w_proj_fox[0]), _cast_bf16("cast_w_proj_rwkv", w_proj_rwkv[0]),
            _cast_bf16("cast_w_out", w_out[0]), lora_s]

    def unpack(gathered):
        g_wpf, g_wpr, g_out, g_lora = [own_slot(g, own) for g, own in zip(gathered, mine)]
        lo = g_lora.transpose(1, 0, 2).reshape(2 * lora, RW)
        return (lo[:lora], lo[lora:], g_wpf.transpose(1, 0, 2).reshape(RW, D),
                g_wpr.transpose(1, 0, 2).reshape(RW, D), g_out.reshape(D, D))

    early, late = ["w_proj_fox", "w_proj_rwkv", "w_out"], ["w_in", "lora"]
    names = early + late
    chip_sums, direct = {}, {}
    n1_slot = (2 * (lax.axis_index("x") ^ (1 - lax.axis_index("c")))
               + (lax.axis_index("y") ^ lax.axis_index("c"))).astype(jnp.int32).reshape(1)

    def exchange(got):
        if isinstance(got, dict):
            if "w_in" in got:
                group, scatter = late, _scatter_stage1
                gs = [got["w_in"][None],
                      _shard_major(jnp.concatenate([got["rwkv_w2"], got["rwkv_a2"]], axis=0).astype(BF16), 1)]
            else:
                group, scatter = early, _scatter_to_owners
                gs = [_shard_major(got["w_proj_fox"], 1), _shard_major(got["w_proj_rwkv"], 1),
                      _shard_major(got["w_out"], 0)]
            recv1 = _exchange_halves("exchange_halves_" + group[0], gs)
            sums = [_add_halves("add_halves_" + nm, g, r, c_idx) for nm, g, r in zip(group, gs, recv1)]
            if group is late:
                sums[0] = _my_layout_to_shards(cfg, sums[0][0])
            chip_sums.update(zip(group, sums))
            return scatter(sums)
        direct.update(zip(late, got[0::2]))
        return _scatter_stage2([_add_passed("add_passed_" + nm, chip_sums[nm], g, n1_slot)
                                for nm, g in zip(late, got[1::2])])

    loss_dev, gx, small, _, recv2 = _local_step(
        cfg, x[0], loss_target[0], norm_gain, w_my, fox_forget_bias, rwkv_shift_mix, rwkv_w0, rwkv_a0, rwkv_k_k,
        rwkv_k_a, rwkv_r_k, rwkv_ln_w, rwkv_ln_b, final_norm_gain, (_gather_parts(mine), unpack), exchange)
    loss = lax.psum(loss_dev, ("x", "y", "c"))

    small_shapes = [args[nm].shape for nm in _SMALL]
    packed = _pack_small([small[nm] for nm in _SMALL])
    reduced = [_sum_chips("sum_chips_" + nm, r, chip_sums[nm], place) for nm, r in zip(early, recv2[:3])]
    reduced += [_sum_stages("sum_stages_" + nm, chip_sums[nm], direct[nm], via, place)
                for nm, via in zip(late, recv2[3:])]
    *joined, small_all = _join_halves(reduced, packed)
    g_small = _sum_slots("sum_small", small_all)

    grads = dict(zip(_SMALL, _unpack_small(g_small, small_shapes)))
    grads.update({nm: g[None] for nm, g in zip(names, joined) if nm != "lora"})
    g_lora_f = joined[names.index("lora")]
    grads["rwkv_w2"] = g_lora_f[None, :lora]
    grads["rwkv_a2"] = g_lora_f[None, lora:]

    delta, new_m, new_v = {}, {}, {}
    w_small = _pack_small([args[nm] for nm in _SMALL])
    m_small = _pack_small([args["m_" + nm] for nm in _SMALL])
    v_small = _pack_small([args["v_" + nm] for nm in _SMALL])
    d_s, m_s, v_s = _adamw("adamw_small", w_small, g_small, m_small, v_small)
    for tgt, pk in ((delta, d_s), (new_m, m_s), (new_v, v_s)):
        tgt.update(zip(_SMALL, _unpack_small(pk, small_shapes)))
    for nm in ("w_in", "w_proj_fox", "w_proj_rwkv", "w_out", "rwkv_w2", "rwkv_a2"):
        shp = args[nm].shape
        two_d = (shp[1], shp[2])
        d_b, m_b, v_b = _adamw("adamw_" + nm, args[nm].reshape(two_d), grads[nm].reshape(two_d),
                               args["m_" + nm].reshape(two_d), args["v_" + nm].reshape(two_d))
        delta[nm], new_m[nm], new_v[nm] = d_b.reshape(shp), m_b.reshape(shp), v_b.reshape(shp)

    return (loss, gx[None], *[grads[n] for n in _WEIGHTS], *[delta[n] for n in _WEIGHTS],
            *[new_m[n] for n in _WEIGHTS], *[new_v[n] for n in _WEIGHTS])
```

```python
import functools

import numpy as np
import jax
import jax.numpy as jnp
from jax import lax
from jax.experimental import pallas as pl
from jax.experimental.pallas import tpu as pltpu

F32 = jnp.float32
BF16 = jnp.bfloat16
HI = lax.Precision.HIGHEST
MESH = pl.DeviceIdType.MESH

FOX_HEAD_DIM = 128
RWKV_HEAD_DIM = 64
RMS_EPS = 1e-6
GN_EPS = 64e-5
L2_EPS = 1e-12
ADAM_LR = 0.001
ADAM_B1 = 0.9
ADAM_B2 = 0.999
ADAM_EPS = 1e-08
ADAM_WD = 0.01
ADAM_STEP = 10

LANES = 128
VMEM_LIMIT = 56 * 1024 * 1024
SCAN_CHUNK = 64
SCAN_HEADS_PER_STEP = 16
SCAN_PASSES = (1, 1, 1)
N_CHIPS = 4
N_DEV = 8

_pcall = pl.pallas_call


def _cparams(sem=None):
    return pltpu.CompilerParams(dimension_semantics=sem, vmem_limit_bytes=VMEM_LIMIT)


def _softplus(x):
    return jnp.maximum(x, 0.0) + jnp.log(1.0 + jnp.exp(-jnp.abs(x)))


def _silu(z):
    return z * jax.nn.sigmoid(z)


def _rmsn(x, g):
    return x * lax.rsqrt(jnp.mean(x * x, axis=-1, keepdims=True) + RMS_EPS) * g


def _dot(a, b, dims="nn", precision=None):
    dn = {"nn": (((1,), (0,)), ((), ())), "nt": (((1,), (1,)), ((), ())), "tn": (((0,), (0,)), ((), ()))}[dims]
    return lax.dot_general(a, b, dn, precision=precision, preferred_element_type=F32)


def _split_bf16(x):
    hi = x.astype(BF16)
    return hi, (x - hi.astype(F32)).astype(BF16)


def _bdot_raw(a, b, ca, cb, passes):
    dn = (((ca,), (cb,)), ((0,), (0,)))
    mm = lambda p, q: lax.dot_general(p, q, dn, preferred_element_type=F32)
    if passes == 1:
        return mm(a.astype(BF16), b.astype(BF16))
    ah, al = _split_bf16(a)
    bh, bl = _split_bf16(b)
    return mm(ah, bh) + (mm(ah, bl) + mm(al, bh))


@functools.partial(jax.custom_vjp, nondiff_argnums=(2, 3, 4))
def _bdot_p(a, b, ca, cb, passes):
    return _bdot_raw(a, b, ca, cb, passes)


def _bdot_fwd(a, b, ca, cb, passes):
    return _bdot_raw(a, b, ca, cb, passes), (a, b)


def _bdot_bwd(ca, cb, passes, res, g):
    a, b = res
    if (ca, cb) == (2, 1):
        return _bdot_p(g, b, 2, 2, passes), _bdot_p(a, g, 1, 1, passes)
    if (ca, cb) == (2, 2):
        return _bdot_p(g, b, 2, 1, passes), _bdot_p(g, a, 1, 1, passes)
    assert (ca, cb) == (1, 1)
    return _bdot_p(b, g, 2, 2, passes), _bdot_p(a, g, 2, 1, passes)


_bdot_p.defvjp(_bdot_fwd, _bdot_bwd)


def _bdot(a, b, ca, cb, passes=3):
    return _bdot_p(a, b, ca, cb, passes)


def _dot3(a, b):
    return _bdot(a[None], b[None], 2, 1)[0]


@jax.custom_vjp
def _xdot(x, m, mt):
    hi, lo = _split_bf16(x)
    m16 = m.astype(BF16)
    return _dot(hi, m16) + _dot(lo, m16)


def _xdot_fwd(x, m, mt):
    return _xdot(x, m, mt), (m, mt)


def _xdot_bwd(res, g):
    m, mt = res
    return _xdot(g, mt, m), jnp.zeros_like(m), jnp.zeros_like(mt)


_xdot.defvjp(_xdot_fwd, _xdot_bwd)


class _Cfg:
    def __init__(self, T, D, lora):
        self.T, self.D, self.lora = T, D, lora
        self.FW = D // 2
        self.FH = self.FW // FOX_HEAD_DIM
        self.RW = D // 2
        self.RH = self.RW // RWKV_HEAD_DIM
        self.LP = -(-lora // LANES) * LANES
        self.o_fox = 0
        self.o_rwkv = 4 * self.FW
        self.o_gate = self.o_rwkv + 4 * self.RW
        self.o_f = self.o_gate + 2 * D
        self.o_wd = self.o_f + LANES
        self.o_ad = self.o_wd + self.LP
        end = self.o_ad + self.LP
        self.tn = 1280 if D >= 2048 else LANES
        self.ncol = -(-end // self.tn) * self.tn
        self.in_cols = 4 * self.FW + self.FH + 4 * self.RW + 2 * lora + 2 * D
        self.rseg = 4 * self.RW + 2 * self.LP
        self.C = min(SCAN_CHUNK, T)
        self.tr = min(256, T)
        self.hb = min(SCAN_HEADS_PER_STEP, self.RH)

    def segments(self):
        FW, FH, RW, lo, D = self.FW, self.FH, self.RW, self.lora, self.D
        g_f = 4 * FW
        g_r = g_f + FH
        g_wd = g_r + 4 * RW
        g_ad = g_wd + lo
        g_g = g_ad + lo
        dh = FOX_HEAD_DIM
        qkv = [(j * FW + h * dh, dh, (3 * h + j) * dh) for h in range(FH) for j in range(3)]
        return qkv + [(3 * FW, FW, 3 * FW), (g_f, FH, self.o_f), (g_r, 4 * RW, self.o_rwkv), (g_wd, lo, self.o_wd),
                      (g_ad, lo, self.o_ad), (g_g, 2 * D, self.o_gate)]


def _shards_to_my_layout(cfg, g):
    R, sc = g.shape[1], g.shape[2]
    segs = sorted(cfg.segments(), key=lambda s: s[2])
    parts, pos = [], 0
    for g0, w, m0 in segs:
        if m0 > pos:
            parts.append(jnp.zeros((R, m0 - pos), g.dtype))
        for s in range(N_CHIPS):
            lo, hi = max(g0, s * sc), min(g0 + w, (s + 1) * sc)
            if lo < hi:
                parts.append(g[s, :, lo - s * sc:hi - s * sc])
        pos = m0 + w
    if cfg.ncol > pos:
        parts.append(jnp.zeros((R, cfg.ncol - pos), g.dtype))
    return jnp.concatenate(parts, axis=1)


def _my_layout_to_shards(cfg, wm):
    sc = cfg.in_cols // N_CHIPS
    segs = sorted(cfg.segments(), key=lambda s: s[0])
    shards = []
    for s in range(N_CHIPS):
        parts = []
        for g0, w, m0 in segs:
            lo, hi = max(g0, s * sc), min(g0 + w, (s + 1) * sc)
            if lo < hi:
                parts.append(wm[:, m0 + lo - g0:m0 + hi - g0])
        shards.append(jnp.concatenate(parts, axis=1))
    return jnp.stack(shards, axis=0)


def _rwkv_vec_to_my(cfg, v):
    RW4, lo, LP = 4 * cfg.RW, cfg.lora, cfg.LP
    z = jnp.zeros((1, LP - lo), v.dtype)
    return jnp.concatenate([v[:, :RW4], v[:, RW4:RW4 + lo], z, v[:, RW4 + lo:], z], axis=1)


def _rwkv_vec_from_my(cfg, v):
    RW4, lo, LP = 4 * cfg.RW, cfg.lora, cfg.LP
    return jnp.concatenate([v[:, :RW4], v[:, RW4:RW4 + lo], v[:, RW4 + LP:RW4 + LP + lo]], axis=1)


def _comm_at(comm, which, steps, cin, cout, scr):
    if not comm or len(comm) <= which:
        return
    lin, total = 0, 1
    for d, n in enumerate(steps):
        lin = lin * n + pl.program_id(d)
        total *= n
    pl.when(lin == {3: 0, 4: total - 1, 5: total // 2}[which])(lambda: comm[which](cin, cout, scr))


def _mm(name, a, b, dims, out_dtype, tm, tn, tk, comm=None):
    (M, K) = a.shape if dims != "tn" else a.shape[::-1]
    N = b.shape[0] if dims == "nt" else b.shape[1]
    tm, tn, tk = min(tm, M), min(tn, N), min(tk, K)
    assert M % tm == 0 and N % tn == 0 and K % tk == 0, (name, M, N, K, tm, tn, tk)
    nk = K // tk
    steps = (M // tm, N // tn, nk)
    c_in, c_out, c_scr = comm[:3] if comm else ([], [], [])
    if dims == "nn":
        a_spec = pl.BlockSpec((tm, tk), lambda i, j, k: (i, k))
        b_spec = pl.BlockSpec((tk, tn), lambda i, j, k: (k, j))
    elif dims == "nt":
        a_spec = pl.BlockSpec((tm, tk), lambda i, j, k: (i, k))
        b_spec = pl.BlockSpec((tn, tk), lambda i, j, k: (j, k))
    else:
        a_spec = pl.BlockSpec((tk, tm), lambda i, j, k: (k, i))
        b_spec = pl.BlockSpec((tk, tn), lambda i, j, k: (k, j))

    n_acc = 1 if nk > 1 else 0

    def body(a_ref, b_ref, *rest):
        cin, o_ref = rest[:len(c_in)], rest[len(c_in)]
        cout = rest[len(c_in) + 1:len(c_in) + 1 + len(c_out)]
        scr = rest[len(c_in) + 1 + len(c_out):]
        _comm_at(comm, 3, steps, cin, cout, scr[n_acc:])
        if nk == 1:
            o_ref[...] = _dot(a_ref[...], b_ref[...], dims).astype(o_ref.dtype)
        else:
            acc_ref, k = scr[0], pl.program_id(2)

            @pl.when(k == 0)
            def _():
                acc_ref[...] = jnp.zeros_like(acc_ref)

            acc_ref[...] += _dot(a_ref[...], b_ref[...], dims)

            @pl.when(k == nk - 1)
            def _():
                o_ref[...] = acc_ref[...].astype(o_ref.dtype)

        _comm_at(comm, 5, steps, cin, cout, scr[n_acc:])
        _comm_at(comm, 4, steps, cin, cout, scr[n_acc:])

    res = _pcall(
        body, name=name, grid=steps,
        in_specs=[a_spec, b_spec] + [_ANY] * len(c_in),
        out_specs=[pl.BlockSpec((tm, tn), lambda i, j, k: (i, j))] + [_ANY] * len(c_out),
        out_shape=[jax.ShapeDtypeStruct((M, N), out_dtype)] + list(c_out),
        scratch_shapes=([pltpu.VMEM((tm, tn), F32)] if nk > 1 else []) + list(c_scr),
        compiler_params=_cparams(("arbitrary",) * 3 if comm else ("parallel", "parallel", "arbitrary")),
    )(a, b, *c_in)
    return res if comm else res[0]


def _tile(tr, w, cb=0):
    return pl.BlockSpec((tr, w), lambda i: (i, cb))


def _const(shape):
    nd = len(shape)
    return pl.BlockSpec(shape, lambda i: (0,) * nd)


def _acc_store(i, ref, val):
    @pl.when(i == 0)
    def _():
        ref[...] = val

    @pl.when(i > 0)
    def _():
        ref[...] += val


def _rms_fwd(cfg, x2, g):
    T, D, tr = cfg.T, cfg.D, cfg.tr

    def body(x_ref, g_ref, h_ref):
        h_ref[...] = _rmsn(x_ref[...], g_ref[...]).astype(BF16)

    return _pcall(body, name="rms_fwd", grid=(T // tr,), in_specs=[_tile(tr, D), _const((1, D))],
                  out_specs=_tile(tr, D), out_shape=jax.ShapeDtypeStruct((T, D), BF16),
                  compiler_params=_cparams(("parallel",)))(x2, g)


def _rms_bwd(cfg, x2, g, dh, dres, comm=None):
    T, D, tr = cfg.T, cfg.D, cfg.tr
    c_in, c_out, c_scr = comm[:3] if comm else ([], [], [])
    steps = (T // tr,)

    def body(x_ref, g_ref, dh_ref, dres_ref, *rest):
        cin, (gx_ref, dg_ref) = rest[:len(c_in)], rest[len(c_in):len(c_in) + 2]
        cout, scr = rest[len(c_in) + 2:len(c_in) + 2 + len(c_out)], rest[len(c_in) + 2 + len(c_out):]
        _comm_at(comm, 3, steps, cin, cout, scr)
        _, vjp = jax.vjp(_rmsn, x_ref[...], g_ref[...])
        dx, dg = vjp(dh_ref[...])
        gx_ref[...] = dx + dres_ref[...]
        _acc_store(pl.program_id(0), dg_ref, dg)
        _comm_at(comm, 4, steps, cin, cout, scr)

    return _pcall(body, name="rms_bwd", grid=steps,
                  in_specs=[_tile(tr, D), _const((1, D)), _tile(tr, D), _tile(tr, D)] + [_ANY] * len(c_in),
                  out_specs=[_tile(tr, D), _const((1, D))] + [_ANY] * len(c_out),
                  out_shape=[jax.ShapeDtypeStruct((T, D), F32), jax.ShapeDtypeStruct((1, D), F32)] + list(c_out),
                  scratch_shapes=list(c_scr), compiler_params=_cparams(("arbitrary",)))(x2, g, dh, dres, *c_in)


def _final(cfg, x2, mo, fg, target):
    T, D, tr = cfg.T, cfg.D, cfg.tr

    def loss_fn(hres, g, tgt):
        err = _rmsn(hres, g) - tgt
        return 0.5 * jnp.sum(jnp.mean(err * err, axis=-1, keepdims=True), axis=0, keepdims=True)

    def body(x_ref, mo_ref, g_ref, t_ref, loss_ref, dres_ref, dres16_ref, dg_ref):
        hres = x_ref[...] + mo_ref[...]
        loss, vjp = jax.vjp(functools.partial(loss_fn, tgt=t_ref[...]), hres, g_ref[...])
        dres, dg = vjp(jnp.ones((1, 1), F32))
        dres_ref[...] = dres
        dres16_ref[...] = dres.astype(BF16)
        i = pl.program_id(0)
        _acc_store(i, dg_ref, dg)
        _acc_store(i, loss_ref, jnp.broadcast_to(loss, (8, LANES)))

    return _pcall(body, name="final_loss", grid=(T // tr,),
                  in_specs=[_tile(tr, D), _tile(tr, D), _const((1, D)), _tile(tr, D)],
                  out_specs=[_const((8, LANES)), _tile(tr, D), _tile(tr, D), _const((1, D))],
                  out_shape=[jax.ShapeDtypeStruct((8, LANES), F32), jax.ShapeDtypeStruct((T, D), F32),
                             jax.ShapeDtypeStruct((T, D), BF16), jax.ShapeDtypeStruct((1, D), F32)],
                  compiler_params=_cparams(("arbitrary",)))(x2, mo, fg, target)


def _merge_fn(pa, pb, ga, gb):
    return jax.nn.sigmoid(ga) * pa + jax.nn.sigmoid(gb) * pb


def _merge_fwd(cfg, pa, pb, u):
    T, D, tr = cfg.T, cfg.D, cfg.tr
    cga, cgb = cfg.o_gate // D, cfg.o_gate // D + 1

    def body(pa_ref, pb_ref, ga_ref, gb_ref, m_ref):
        m_ref[...] = _merge_fn(pa_ref[...], pb_ref[...], ga_ref[...], gb_ref[...]).astype(BF16)

    return _pcall(body, name="merge_fwd", grid=(T // tr,),
                  in_specs=[_tile(tr, D), _tile(tr, D), _tile(tr, D, cga), _tile(tr, D, cgb)],
                  out_specs=_tile(tr, D), out_shape=jax.ShapeDtypeStruct((T, D), BF16),
                  compiler_params=_cparams(("parallel",)))(pa, pb, u, u)


def _merge_bwd(cfg, pa, pb, u, dm):
    T, D, tr = cfg.T, cfg.D, cfg.tr
    cga, cgb = cfg.o_gate // D, cfg.o_gate // D + 1

    def body(pa_ref, pb_ref, ga_ref, gb_ref, dm_ref, dpa_ref, dpb_ref, dg_ref):
        _, vjp = jax.vjp(_merge_fn, pa_ref[...], pb_ref[...], ga_ref[...], gb_ref[...])
        dpa, dpb, dga, dgb = vjp(dm_ref[...])
        dpa_ref[...] = dpa.astype(BF16)
        dpb_ref[...] = dpb.astype(BF16)
        dg_ref[:, :D] = dga.astype(BF16)
        dg_ref[:, D:] = dgb.astype(BF16)

    return _pcall(body, name="merge_bwd", grid=(T // tr,),
                  in_specs=[_tile(tr, D), _tile(tr, D), _tile(tr, D, cga), _tile(tr, D, cgb), _tile(tr, D)],
                  out_specs=[_tile(tr, D), _tile(tr, D), _tile(tr, 2 * D, cfg.o_gate // (2 * D))],
                  out_shape=[jax.ShapeDtypeStruct((T, D), BF16), jax.ShapeDtypeStruct((T, D), BF16),
                             jax.ShapeDtypeStruct((T, cfg.ncol), BF16)],
                  compiler_params=_cparams(("parallel",)))(pa, pb, u, u, dm)


def _gate_fn(o, z):
    return o * _silu(z)


def _gate_a_fwd(cfg, o, u):
    T, FW, tr = cfg.T, cfg.FW, cfg.tr

    def body(o_ref, z_ref, oa_ref):
        oa_ref[...] = _gate_fn(o_ref[...], z_ref[...]).astype(BF16)

    return _pcall(body, name="gate_a_fwd", grid=(T // tr,), in_specs=[_tile(tr, FW), _tile(tr, FW, 3)],
                  out_specs=_tile(tr, FW), out_shape=jax.ShapeDtypeStruct((T, FW), BF16),
                  compiler_params=_cparams(("parallel",)))(o, u)


def _gate_a_bwd(cfg, o, u, doa, du):
    T, FW, tr = cfg.T, cfg.FW, cfg.tr

    def body(o_ref, z_ref, doa_ref, du_in, do_ref, dz_ref):
        _, vjp = jax.vjp(_gate_fn, o_ref[...], z_ref[...])
        do, dz = vjp(doa_ref[...])
        do_ref[...] = do
        dz_ref[...] = dz.astype(BF16)

    return _pcall(body, name="gate_a_bwd", grid=(T // tr,),
                  in_specs=[_tile(tr, FW), _tile(tr, FW, 3), _tile(tr, FW), _ANY],
                  out_specs=[_tile(tr, FW), _tile(tr, FW, 3)],
                  out_shape=[jax.ShapeDtypeStruct((T, FW), F32), jax.ShapeDtypeStruct(du.shape, BF16)],
                  input_output_aliases={3: 1},
                  compiler_params=_cparams(("parallel",)))(o, u, doa, du)


def _fox_prep(cfg, u, fb):
    T, tr = cfg.T, cfg.tr
    cf = cfg.o_f // LANES

    def body(f_ref, fb_ref, c_ref, carry_ref):
        i = pl.program_id(0)

        @pl.when(i == 0)
        def _():
            carry_ref[...] = jnp.zeros_like(carry_ref)

        lf = -_softplus(-(f_ref[...] + fb_ref[...]))
        r = lax.broadcasted_iota(jnp.int32, (tr, tr), 0)
        c = lax.broadcasted_iota(jnp.int32, (tr, tr), 1)
        tri = (r >= c).astype(F32)
        c_ref[...] = _dot(tri, lf, precision=HI) + carry_ref[...]
        carry_ref[...] += jnp.sum(lf, axis=0, keepdims=True)

    return _pcall(body, name="fox_prep", grid=(T // tr,), in_specs=[_tile(tr, LANES, cf), _const((1, LANES))],
                  out_specs=_tile(tr, LANES), out_shape=jax.ShapeDtypeStruct((T, LANES), F32),
                  scratch_shapes=[pltpu.VMEM((1, LANES), F32)], compiler_params=_cparams(("arbitrary",)))(u, fb)


def _fox_prep_bwd(cfg, u, fb, dc):
    T, tr = cfg.T, cfg.tr
    cf = cfg.o_f // LANES
    nb = T // tr

    def body(f_ref, fb_ref, dc_ref, df_ref, dfb_ref, carry_ref):
        i = pl.program_id(0)

        @pl.when(i == 0)
        def _():
            carry_ref[...] = jnp.zeros_like(carry_ref)

        dc = dc_ref[...]
        r = lax.broadcasted_iota(jnp.int32, (tr, tr), 0)
        c = lax.broadcasted_iota(jnp.int32, (tr, tr), 1)
        triu = (r <= c).astype(F32)
        dlf = _dot(triu, dc, precision=HI) + carry_ref[...]
        carry_ref[...] += jnp.sum(dc, axis=0, keepdims=True)
        dz = dlf * jax.nn.sigmoid(-(f_ref[...] + fb_ref[...]))
        df_ref[...] = dz.astype(BF16)
        _acc_store(i, dfb_ref, jnp.sum(dz, axis=0, keepdims=True))

    rev = lambda i: (nb - 1 - i, 0)
    return _pcall(body, name="fox_prep_bwd", grid=(nb,),
                  in_specs=[pl.BlockSpec((tr, LANES), lambda i: (nb - 1 - i, cf)), _const((1, LANES)),
                            pl.BlockSpec((tr, LANES), rev)],
                  out_specs=[pl.BlockSpec((tr, LANES), rev), _const((1, LANES))],
                  out_shape=[jax.ShapeDtypeStruct((T, LANES), BF16), jax.ShapeDtypeStruct((1, LANES), F32)],
                  scratch_shapes=[pltpu.VMEM((1, LANES), F32)], compiler_params=_cparams(("arbitrary",)))(u, fb, dc)


def _attn_logits(q_ref, k_ref, c_ref, i, tq, te):
    s = _dot(q_ref[...].astype(BF16), k_ref[0:te, :].astype(BF16), "nt") * (FOX_HEAD_DIM ** -0.5) - c_ref[0, :, 0:te]
    row = i * tq + lax.broadcasted_iota(jnp.int32, (tq, te), 0)
    col = lax.broadcasted_iota(jnp.int32, (tq, te), 1)
    return jnp.where(col <= row, s, -1e30)


def _per_query_tile(i, nq, tq, fn):
    for ii in range(nq):
        pl.when(i == ii)(functools.partial(fn, (ii + 1) * tq))


def _attn_fwd(cfg, u, c_rows):
    T, FW, FH = cfg.T, cfg.FW, cfg.FH
    tq = min(256, T)
    dh = FOX_HEAD_DIM

    def body(q_ref, k_ref, v_ref, c_ref, o_ref, lse_ref):
        i = pl.program_id(1)

        def tile(te):
            s = _attn_logits(q_ref, k_ref, c_ref, i, tq, te)
            m = jnp.max(s, axis=1, keepdims=True)
            p = jnp.exp(s - m)
            l = jnp.sum(p, axis=1, keepdims=True)
            o_ref[...] = _dot(p.astype(BF16), v_ref[0:te, :].astype(BF16)) / l
            lse_ref[0] = m + jnp.log(l)

        _per_query_tile(i, T // tq, tq, tile)

    return _pcall(
        body, name="fox_attn_fwd", grid=(FH, T // tq),
        in_specs=[pl.BlockSpec((tq, dh), lambda h, i: (i, 3 * h)), pl.BlockSpec((T, dh), lambda h, i: (0, 3 * h + 1)),
                  pl.BlockSpec((T, dh), lambda h, i: (0, 3 * h + 2)), pl.BlockSpec((1, 1, T), lambda h, i: (h, 0, 0))],
        out_specs=[pl.BlockSpec((tq, dh), lambda h, i: (i, h)), pl.BlockSpec((1, tq, 1), lambda h, i: (h, i, 0))],
        out_shape=[jax.ShapeDtypeStruct((T, FW), F32), jax.ShapeDtypeStruct((FH, T, 1), F32)],
        compiler_params=_cparams(("parallel", "arbitrary")),
    )(u, u, u, c_rows)


def _attn_bwd(cfg, u, c_rows, lse, do, du):
    T, FW, FH = cfg.T, cfg.FW, cfg.FH
    tq = min(256, T)
    nq = T // tq
    dh = FOX_HEAD_DIM
    scale = dh ** -0.5

    def body(q_ref, k_ref, v_ref, c_ref, lse_ref, do_ref, du_in, du_ref, dcol_ref, dk_acc, dv_acc):
        i = pl.program_id(1)

        @pl.when(i == 0)
        def _():
            dk_acc[...] = jnp.zeros_like(dk_acc)
            dv_acc[...] = jnp.zeros_like(dv_acc)
            dcol_ref[...] = jnp.zeros_like(dcol_ref)

        def tile(te):
            s = _attn_logits(q_ref, k_ref, c_ref, i, tq, te)
            p = jnp.exp(s - lse_ref[0])
            do_v = do_ref[...]
            dp = _dot(do_v.astype(BF16), v_ref[0:te, :].astype(BF16), "nt")
            delta = jnp.sum(p * dp, axis=1, keepdims=True)
            ds = p * (dp - delta)
            ds16 = ds.astype(BF16)
            du_ref[te - tq:te, 0:dh] = (_dot(ds16, k_ref[0:te, :].astype(BF16)) * scale).astype(BF16)
            dk_acc[0:te, :] += _dot(ds16, q_ref[...].astype(BF16), "tn") * scale
            dv_acc[0:te, :] += _dot(p.astype(BF16), do_v.astype(BF16), "tn")
            dcol_ref[0, :, 0:te] += jnp.sum(ds, axis=0, keepdims=True)

        _per_query_tile(i, nq, tq, tile)

        @pl.when(i == nq - 1)
        def _():
            du_ref[:, dh:2 * dh] = dk_acc[...].astype(BF16)
            du_ref[:, 2 * dh:3 * dh] = dv_acc[...].astype(BF16)

    return _pcall(
        body, name="fox_attn_bwd", grid=(FH, nq),
        in_specs=[pl.BlockSpec((tq, dh), lambda h, i: (i, 3 * h)), pl.BlockSpec((T, dh), lambda h, i: (0, 3 * h + 1)),
                  pl.BlockSpec((T, dh), lambda h, i: (0, 3 * h + 2)), pl.BlockSpec((1, 1, T), lambda h, i: (h, 0, 0)),
                  pl.BlockSpec((1, tq, 1), lambda h, i: (h, i, 0)), pl.BlockSpec((tq, dh), lambda h, i: (i, h)), _ANY],
        out_specs=[pl.BlockSpec((T, 3 * dh), lambda h, i: (0, h)), pl.BlockSpec((1, 1, T), lambda h, i: (h, 0, 0))],
        out_shape=[jax.ShapeDtypeStruct(du.shape, BF16), jax.ShapeDtypeStruct((FH, 1, T), F32)],
        scratch_shapes=[pltpu.VMEM((T, dh), F32), pltpu.VMEM((T, dh), F32)],
        input_output_aliases={6: 0},
        compiler_params=_cparams(("parallel", "arbitrary")),
    )(u, u, u, c_rows, lse, do, du)


def _head_indicators(cfg):
    ind = np.zeros((cfg.RW, LANES), np.float32)
    ind[np.arange(cfg.RW), np.arange(cfg.RW) // RWKV_HEAD_DIM] = 1.0
    pad = np.zeros((1, LANES), np.float32)
    pad[0, cfg.RH:] = 1.0
    return jnp.asarray(ind), jnp.asarray(ind.T.copy()), jnp.asarray(pad)


def _prep_fn(us_r, us_k, us_v, us_wd, us_ad, w0, w2p, a0, a2p, k_k, k_a, ind, ind_t, pad):
    wpre = w0 + _dot3(jnp.tanh(us_wd), w2p)
    w = -_softplus(-wpre) - 0.5
    lw = -jnp.exp(w)
    a = jax.nn.sigmoid(a0 + _dot3(us_ad, a2p))
    kk = us_k * k_k
    ss = _xdot(kk * kk, ind, ind_t) + pad
    inv = 1.0 / jnp.maximum(jnp.sqrt(ss), L2_EPS)
    kkn = kk * _xdot(inv, ind_t, ind)
    kp = us_k * (1.0 + (a - 1.0) * k_a)
    return us_r, lw, kp, us_v, -kkn, kkn * a


def _shifted(u, prev_row, mu, first):
    n = u.shape[0]
    rolled = pltpu.roll(u, 1, 0)
    row = lax.broadcasted_iota(jnp.int32, u.shape, 0)
    p0 = jnp.where(first, jnp.zeros_like(prev_row), prev_row)
    prev = jnp.where(row == 0, jnp.broadcast_to(p0, u.shape), rolled)
    return u + (prev - u) * mu, prev


def _rwkv_specs(cfg, tr):
    RW, LP = cfg.RW, cfg.LP
    base = cfg.o_rwkv // RW
    cols = [(RW, base), (RW, base + 1), (RW, base + 2), (RW, base + 3), (LP, cfg.o_wd // LP), (LP, cfg.o_ad // LP)]
    cur = [pl.BlockSpec((tr, w), (lambda i, cb=cb: (i, cb))) for w, cb in cols]
    prv = [pl.BlockSpec((8, w), (lambda i, cb=cb: (jnp.maximum(i * (tr // 8) - 1, 0), cb))) for w, cb in cols]
    return cols, cur, prv


def _mu_pieces(cfg, mu_ref):
    RW, LP = cfg.RW, cfg.LP
    offs = [0, RW, 2 * RW, 3 * RW, 4 * RW, 4 * RW + LP, 4 * RW + 2 * LP]
    return [mu_ref[:, offs[j]:offs[j + 1]] for j in range(6)]


def _rwkv_prep_fwd(cfg, u, mu, w0, w2p, a0, a2p, k_k, k_a):
    T, RW, LP, tr = cfg.T, cfg.RW, cfg.LP, cfg.tr
    ind, ind_t, pad = _head_indicators(cfg)
    cols, cur, prv = _rwkv_specs(cfg, tr)

    def body(*refs):
        u_refs, p_refs = refs[0:6], refs[6:12]
        mu_ref, w0_ref, w2_ref, a0_ref, a2_ref, kk_ref, ka_ref, ind_ref, indt_ref, pad_ref = refs[12:22]
        outs = refs[22:]
        first = pl.program_id(0) == 0
        mus = _mu_pieces(cfg, mu_ref)
        us = [_shifted(u_refs[j][...], p_refs[j][7:8, :], mus[j], first)[0] for j in range(6)]
        res = _prep_fn(us[0], us[1], us[2], us[4], us[5], w0_ref[...], w2_ref[...], a0_ref[...], a2_ref[...],
                       kk_ref[...], ka_ref[...], ind_ref[...], indt_ref[...], pad_ref[...])
        for j in range(6):
            outs[j][...] = res[j]
        outs[6][...] = us[3]

    consts = [mu, w0, w2p, a0, a2p, k_k, k_a, ind, ind_t, pad]
    return _pcall(body, name="rwkv_prep_fwd", grid=(T // tr,),
                  in_specs=cur + prv + [_const(c.shape) for c in consts],
                  out_specs=[_tile(tr, RW)] * 7, out_shape=[jax.ShapeDtypeStruct((T, RW), F32)] * 7,
                  compiler_params=_cparams(("parallel",)))(*([u] * 12), *consts)


def _rwkv_prep_bwd(cfg, u, mu, w0, w2p, a0, a2p, k_k, k_a, cots, dzb):
    T, RW, LP = cfg.T, cfg.RW, cfg.LP
    tr = min(128, T)
    ind, ind_t, pad = _head_indicators(cfg)
    cols, cur, prv = _rwkv_specs(cfg, tr)
    rseg = cfg.rseg

    def body(*refs):
        u_refs, p_refs = refs[0:6], refs[6:12]
        mu_ref, w0_ref, w2_ref, a0_ref, a2_ref, kk_ref, ka_ref, ind_ref, indt_ref, pad_ref = refs[12:22]
        cot_refs, dzb_ref = refs[22:28], refs[28]
        dus_ref, dmu_ref, dw0_ref, dw2_ref, da0_ref, da2_ref, dkk_ref, dka_ref = refs[29:]
        i = pl.program_id(0)
        first = i == 0
        mus = _mu_pieces(cfg, mu_ref)
        sh = [_shifted(u_refs[j][...], p_refs[j][7:8, :], mus[j], first) for j in range(6)]
        us = [s[0] for s in sh]
        fn = functools.partial(_prep_fn, ind=ind_ref[...], ind_t=indt_ref[...], pad=pad_ref[...])
        _, vjp = jax.vjp(fn, us[0], us[1], us[2], us[4], us[5], w0_ref[...], w2_ref[...], a0_ref[...], a2_ref[...],
                         kk_ref[...], ka_ref[...])
        d = vjp(tuple(c[...] for c in cot_refs))
        dus = [d[0], d[1], d[2], dzb_ref[...], d[3], d[4]]
        offs = [0, RW, 2 * RW, 3 * RW, 4 * RW, 4 * RW + LP, 4 * RW + 2 * LP]
        for j in range(6):
            dus_ref[:, offs[j]:offs[j + 1]] = dus[j]
            dmu_j = jnp.sum(dus[j] * (sh[j][1] - u_refs[j][...]), axis=0, keepdims=True)

            @pl.when(first)
            def _(j=j, dmu_j=dmu_j):
                dmu_ref[:, offs[j]:offs[j + 1]] = dmu_j

            @pl.when(i > 0)
            def _(j=j, dmu_j=dmu_j):
                dmu_ref[:, offs[j]:offs[j + 1]] += dmu_j
        for ref, val in zip((dw0_ref, dw2_ref, da0_ref, da2_ref, dkk_ref, dka_ref), d[5:11]):
            _acc_store(i, ref, val)

    consts = [mu, w0, w2p, a0, a2p, k_k, k_a, ind, ind_t, pad]
    vec = jax.ShapeDtypeStruct((1, RW), F32)
    mat = jax.ShapeDtypeStruct((LP, RW), F32)
    return _pcall(body, name="rwkv_prep_bwd", grid=(T // tr,),
                  in_specs=cur + prv + [_const(c.shape) for c in consts] + [_tile(tr, RW)] * 7,
                  out_specs=[_tile(tr, rseg), _const((1, rseg)), _const((1, RW)), _const((LP, RW)), _const((1, RW)),
                             _const((LP, RW)), _const((1, RW)), _const((1, RW))],
                  out_shape=[jax.ShapeDtypeStruct((T, rseg), F32), jax.ShapeDtypeStruct((1, rseg), F32),
                             vec, mat, vec, mat, vec, vec],
                  compiler_params=_cparams(("arbitrary",)))(*([u] * 12), *consts, *cots, dzb)


def _shift_bwd(cfg, dus, mu, df, du):
    T, tr, RW, LP = cfg.T, cfg.tr, cfg.RW, cfg.LP
    nb = T // tr
    tail = cfg.ncol - cfg.o_f
    assert cfg.o_rwkv % (4 * RW) == 0 and (4 * RW) % (2 * LP) == 0 and cfg.o_f % tail == 0

    def shifted(d_ref, n_ref, mu_ref):
        d = d_ref[...]
        rolled = pltpu.roll(d, tr - 1, 0)
        row = lax.broadcasted_iota(jnp.int32, d.shape, 0)
        n0 = jnp.where(pl.program_id(0) == nb - 1, jnp.zeros_like(n_ref[0:1, :]), n_ref[0:1, :])
        nxt = jnp.where(row == tr - 1, jnp.broadcast_to(n0, d.shape), rolled)
        mu_v = mu_ref[...]
        return (d * (1.0 - mu_v) + nxt * mu_v).astype(BF16)

    def main_body(d_ref, n_ref, mu_ref, du_in, du_ref):
        du_ref[...] = shifted(d_ref, n_ref, mu_ref)

    def tail_body(d_ref, n_ref, mu_ref, df_ref, du_in, du_ref):
        du_ref[:, 0:LANES] = df_ref[...]
        du_ref[:, LANES:LANES + 2 * LP] = shifted(d_ref, n_ref, mu_ref)
        if tail > LANES + 2 * LP:
            du_ref[:, LANES + 2 * LP:] = jnp.zeros((tr, tail - LANES - 2 * LP), BF16)

    def specs(w, cb):
        return [_tile(tr, w, cb),
                pl.BlockSpec((8, w), lambda i: (jnp.minimum((i + 1) * (tr // 8), T // 8 - 1), cb)),
                pl.BlockSpec((1, w), lambda i: (0, cb))]

    out = jax.ShapeDtypeStruct(du.shape, BF16)
    du = _pcall(main_body, name="shift_bwd_main", grid=(nb,), in_specs=specs(4 * RW, 0) + [_ANY],
                out_specs=_tile(tr, 4 * RW, cfg.o_rwkv // (4 * RW)), out_shape=out, input_output_aliases={3: 0},
                compiler_params=_cparams(("parallel",)))(dus, dus, mu, du)
    return _pcall(tail_body, name="shift_bwd_tail", grid=(nb,),
                  in_specs=specs(2 * LP, 4 * RW // (2 * LP)) + [_tile(tr, LANES), _ANY],
                  out_specs=_tile(tr, tail, cfg.o_f // tail), out_shape=out, input_output_aliases={4: 0},
                  compiler_params=_cparams(("parallel",)))(dus, dus, mu, df, du)


def _chunk_local(r, lw, k, v, a, b):
    H, C, K = r.shape
    row = lax.broadcasted_iota(jnp.int32, (C, C), 0)
    col = lax.broadcasted_iota(jnp.int32, (C, C), 1)
    incl = jnp.broadcast_to((row >= col).astype(F32)[None], (H, C, C))
    strict = (row > col)[None]
    lower = (row >= col)[None]
    eye = (row == col)[None]
    zero = jnp.zeros((), F32)
    L = _bdot(incl, lw, 2, 1)
    LC = jnp.sum(lw, axis=1, keepdims=True)
    eL = jnp.exp(L)
    eLn = jnp.exp(-L)
    at = a * jnp.exp(L - lw)
    rt = r * eL
    bt = b * eLn
    kt = k * eLn
    eR = jnp.exp(LC - L)
    bh = b * eR
    kh = k * eR
    gram = functools.partial(_bdot, passes=SCAN_PASSES[0])
    inv = functools.partial(_bdot, passes=SCAN_PASSES[1])
    app = functools.partial(_bdot, passes=SCAN_PASSES[2])
    n_ab = jnp.where(strict, gram(at, bt, 2, 2), zero)
    n_ak = jnp.where(strict, gram(at, kt, 2, 2), zero)
    m_rb = jnp.where(lower, gram(rt, bt, 2, 2), zero)
    m_rk = jnp.where(lower, gram(rt, kt, 2, 2), zero)
    M = n_ab
    P = jnp.where(eye, 1.0, zero) + n_ab
    for _ in range(1, max(1, int(np.ceil(np.log2(C))))):
        M = inv(M, M, 2, 1)
        P = P + inv(M, P, 2, 1)
    W = app(P, at, 2, 1)
    Uloc = app(P, app(n_ak, v, 2, 1), 2, 1)
    Q = rt + app(m_rb, W, 2, 1)
    Yloc = app(m_rb, Uloc, 2, 1) + app(m_rk, v, 2, 1)
    A = jnp.where(eye, jnp.exp(LC), zero) + app(W, bh, 1, 1)
    Sloc = app(Uloc, bh, 1, 1) + app(v, kh, 1, 1)
    return Q, Yloc, A, Sloc


def _split_heads(ref, n):
    N = RWKV_HEAD_DIM
    return jnp.stack([ref[:, h * N:(h + 1) * N] for h in range(n)], axis=0)


def _merge_heads(x):
    return jnp.concatenate([x[h] for h in range(x.shape[0])], axis=1)


def _scan_local_specs(cfg):
    N, HB = RWKV_HEAD_DIM, cfg.hb
    grid = (cfg.RH // HB, cfg.T // cfg.C)
    seq = pl.BlockSpec((HB, cfg.C, N), lambda h, j: (h, j, 0))
    mat = pl.BlockSpec((HB, 1, N, N), lambda h, j: (h, j, 0, 0))
    return grid, seq, mat


def _scan_local_fwd(cfg, seqs):
    T, RH, N = cfg.T, cfg.RH, RWKV_HEAD_DIM
    grid, seq, mat = _scan_local_specs(cfg)

    def body(r_ref, lw_ref, k_ref, v_ref, a_ref, b_ref, q_ref, yl_ref, a_out, sl_ref):
        Q, Yloc, A, Sloc = _chunk_local(*[_split_heads(ref, cfg.hb) for ref in (r_ref, lw_ref, k_ref, v_ref, a_ref, b_ref)])
        q_ref[...] = Q
        yl_ref[...] = Yloc
        a_out[:, 0] = A
        sl_ref[:, 0] = Sloc

    tok = pl.BlockSpec((cfg.C, cfg.hb * N), lambda h, j: (j, h))
    sq = jax.ShapeDtypeStruct((RH, T, N), F32)
    mt = jax.ShapeDtypeStruct((RH, T // cfg.C, N, N), F32)
    return _pcall(body, name="rwkv_scan_local_fwd", grid=grid, in_specs=[tok] * 6, out_specs=[seq, seq, mat, mat],
                  out_shape=[sq, sq, mt, mt], compiler_params=_cparams(("parallel", "parallel")))(*seqs)


def _scan_local_bwd(cfg, toks, dq, dy, da, dsl, extra, comm=None):
    T, RW, N = cfg.T, cfg.RW, RWKV_HEAD_DIM
    grid, seq, mat = _scan_local_specs(cfg)
    c_in, c_out, c_scr = comm[:3] if comm else ([], [], [])

    def body(r_ref, lw_ref, k_ref, v_ref, a_ref, b_ref, dq_ref, dy_ref, da_ref, dsl_ref, xr_ref, xk_ref, xv_ref,
             *rest):
        cin, outs = rest[:len(c_in)], rest[len(c_in):len(c_in) + 6]
        cout, scr = rest[len(c_in) + 6:len(c_in) + 6 + len(c_out)], rest[len(c_in) + 6 + len(c_out):]
        _comm_at(comm, 3, grid, cin, cout, scr)
        ins = [_split_heads(ref, cfg.hb) for ref in (r_ref, lw_ref, k_ref, v_ref, a_ref, b_ref)]
        _, vjp = jax.vjp(_chunk_local, *ins)
        d = vjp((dq_ref[...], _split_heads(dy_ref, cfg.hb), da_ref[:, 0], dsl_ref[:, 0]))
        add = {0: xr_ref, 2: xk_ref, 3: xv_ref}
        for j in range(6):
            dj = _merge_heads(d[j])
            outs[j][...] = dj + add[j][...] if j in add else dj
        _comm_at(comm, 4, grid, cin, cout, scr)

    tok = pl.BlockSpec((cfg.C, cfg.hb * N), lambda h, j: (j, h))
    return _pcall(body, name="rwkv_scan_local_bwd", grid=grid,
                  in_specs=[tok] * 6 + [seq, tok, mat, mat] + [tok] * 3 + [_ANY] * len(c_in),
                  out_specs=[tok] * 6 + [_ANY] * len(c_out),
                  out_shape=[jax.ShapeDtypeStruct((T, RW), F32)] * 6 + list(c_out), scratch_shapes=list(c_scr),
                  compiler_params=_cparams(("arbitrary", "arbitrary") if comm else ("parallel", "parallel")),
                  )(*toks, dq, dy, da, dsl, *extra, *c_in)


def _scan_carry_specs(cfg, rev):
    N, RH, C, nc = RWKV_HEAD_DIM, cfg.RH, cfg.C, cfg.T // cfg.C
    at = (lambda j: nc - 1 - j) if rev else (lambda j: j)
    seq = pl.BlockSpec((RH, C, N), lambda j: (0, at(j), 0))
    mat = pl.BlockSpec((RH, 1, N, N), lambda j: (0, at(j), 0, 0))
    return nc, seq, mat


def _scan_carry_fwd(cfg, q, yloc, a, sloc):
    T, RH, N = cfg.T, cfg.RH, RWKV_HEAD_DIM
    nc, seq, mat = _scan_carry_specs(cfg, False)

    def body(q_ref, yl_ref, a_ref, sl_ref, y_ref, ck_ref, s_ref):
        @pl.when(pl.program_id(0) == 0)
        def _():
            s_ref[...] = jnp.zeros_like(s_ref)

        S = s_ref[...]
        ck_ref[:, 0] = S
        y_ref[...] = _merge_heads(_bdot(q_ref[...], S, 2, 2, SCAN_PASSES[2]) + yl_ref[...])
        s_ref[...] = _bdot(S, a_ref[:, 0], 2, 1) + sl_ref[:, 0]

    tok = pl.BlockSpec((cfg.C, cfg.RW), lambda j: (j, 0))
    return _pcall(body, name="rwkv_scan_carry_fwd", grid=(nc,), in_specs=[seq, seq, mat, mat], out_specs=[tok, mat],
                  out_shape=[jax.ShapeDtypeStruct((T, cfg.RW), F32), jax.ShapeDtypeStruct((RH, nc, N, N), F32)],
                  scratch_shapes=[pltpu.VMEM((RH, N, N), F32)],
                  compiler_params=_cparams(("arbitrary",)))(q, yloc, a, sloc)


def _scan_carry_bwd(cfg, q, a, ckpt, dy):
    T, RH, N = cfg.T, cfg.RH, RWKV_HEAD_DIM
    nc, seq, mat = _scan_carry_specs(cfg, True)

    def body(q_ref, a_ref, ck_ref, dy_ref, dq_ref, da_ref, dsl_ref, ds_ref):
        @pl.when(pl.program_id(0) == 0)
        def _():
            ds_ref[...] = jnp.zeros_like(ds_ref)

        S, dS, dY = ck_ref[:, 0], ds_ref[...], _split_heads(dy_ref, RH)
        dq_ref[...] = _bdot(dY, S, 2, 1, SCAN_PASSES[2])
        da_ref[:, 0] = _bdot(S, dS, 1, 1, SCAN_PASSES[2])
        dsl_ref[:, 0] = dS
        ds_ref[...] = _bdot(dS, a_ref[:, 0], 2, 2) + _bdot(dY, q_ref[...], 1, 1, SCAN_PASSES[2])

    mt = jax.ShapeDtypeStruct((RH, nc, N, N), F32)
    tok = pl.BlockSpec((cfg.C, cfg.RW), lambda j: (nc - 1 - j, 0))
    return _pcall(body, name="rwkv_scan_carry_bwd", grid=(nc,), in_specs=[seq, mat, mat, tok],
                  out_specs=[seq, mat, mat], out_shape=[jax.ShapeDtypeStruct((RH, T, N), F32), mt, mt],
                  scratch_shapes=[pltpu.VMEM((RH, N, N), F32)],
                  compiler_params=_cparams(("arbitrary",)))(q, a, ckpt, dy)


def _post_fn(y, r, kp, v, zb, ln_w, ln_b, rk, ind, ind_t):
    n = float(RWKV_HEAD_DIM)
    mu = _xdot(_xdot(y, ind, ind_t) / n, ind_t, ind)
    yc = y - mu
    var = _xdot(yc * yc, ind, ind_t) / n
    rstd = _xdot(lax.rsqrt(var + GN_EPS), ind_t, ind)
    yn = yc * rstd * ln_w + ln_b
    bonus = _xdot(_xdot(r * kp * rk, ind, ind_t), ind_t, ind) * v
    return (yn + bonus) * _silu(zb)


def _rwkv_post_fwd(cfg, y, r, kp, v, zb, ln_w, ln_b, rk):
    T, RW, tr = cfg.T, cfg.RW, cfg.tr
    ind, ind_t, _ = _head_indicators(cfg)

    def body(y_ref, r_ref, k_ref, v_ref, z_ref, lw_ref, lb_ref, rk_ref, ind_ref, indt_ref, ob_ref):
        ob_ref[...] = _post_fn(y_ref[...], r_ref[...], k_ref[...], v_ref[...], z_ref[...], lw_ref[...], lb_ref[...],
                               rk_ref[...], ind_ref[...], indt_ref[...]).astype(BF16)

    consts = [ln_w, ln_b, rk, ind, ind_t]
    return _pcall(body, name="rwkv_post_fwd", grid=(T // tr,),
                  in_specs=[_tile(tr, RW)] * 5 + [_const(c.shape) for c in consts],
                  out_specs=_tile(tr, RW), out_shape=jax.ShapeDtypeStruct((T, RW), BF16),
                  compiler_params=_cparams(("parallel",)))(y, r, kp, v, zb, *consts)


def _rwkv_post_bwd(cfg, y, r, kp, v, zb, ln_w, ln_b, rk, dob):
    T, RW = cfg.T, cfg.RW
    tr = min(128, T)
    ind, ind_t, _ = _head_indicators(cfg)

    def body(y_ref, r_ref, k_ref, v_ref, z_ref, lw_ref, lb_ref, rk_ref, ind_ref, indt_ref, dob_ref,
             dy_ref, dr_ref, dk_ref, dv_ref, dz_ref, dlw_ref, dlb_ref, drk_ref):
        fn = functools.partial(_post_fn, ind=ind_ref[...], ind_t=indt_ref[...])
        _, vjp = jax.vjp(fn, y_ref[...], r_ref[...], k_ref[...], v_ref[...], z_ref[...], lw_ref[...], lb_ref[...],
                         rk_ref[...])
        d = vjp(dob_ref[...])
        for ref, val in zip((dy_ref, dr_ref, dk_ref, dv_ref, dz_ref), d[:5]):
            ref[...] = val
        i = pl.program_id(0)
        for ref, val in zip((dlw_ref, dlb_ref, drk_ref), d[5:8]):
            _acc_store(i, ref, val)

    consts = [ln_w, ln_b, rk, ind, ind_t]
    vec = jax.ShapeDtypeStruct((1, RW), F32)
    return _pcall(body, name="rwkv_post_bwd", grid=(T // tr,),
                  in_specs=[_tile(tr, RW)] * 5 + [_const(c.shape) for c in consts] + [_tile(tr, RW)],
                  out_specs=[_tile(tr, RW)] * 5 + [_const((1, RW))] * 3,
                  out_shape=[jax.ShapeDtypeStruct((T, RW), F32)] * 5 + [vec] * 3,
                  compiler_params=_cparams(("arbitrary",)))(y, r, kp, v, zb, *consts, dob)


def _adamw_math(w, g, m, v):
    m = ADAM_B1 * m + (1.0 - ADAM_B1) * g
    v = ADAM_B2 * v + (1.0 - ADAM_B2) * (g * g)
    m_hat = m / (1.0 - ADAM_B1 ** ADAM_STEP)
    v_hat = v / (1.0 - ADAM_B2 ** ADAM_STEP)
    delta = -ADAM_LR * (m_hat / (jnp.sqrt(v_hat) + ADAM_EPS) + ADAM_WD * w)
    return delta, m, v


def _adamw(name, w, g, m, v):
    R, Cc = w.shape
    tr = R
    for nb in range(1, R // 8 + 1):
        if R % nb == 0 and (R // nb) % 8 == 0 and (R // nb) * Cc * 4 <= 2 * 1024 * 1024:
            tr = R // nb
            break

    def body(w_ref, g_ref, m_ref, v_ref, d_ref, nm_ref, nv_ref):
        d, nm, nv = _adamw_math(w_ref[...], g_ref[...], m_ref[...], v_ref[...])
        d_ref[...] = d
        nm_ref[...] = nm
        nv_ref[...] = nv

    spec = _tile(tr, Cc)
    return _pcall(body, name=name, grid=(R // tr,), in_specs=[spec] * 4, out_specs=[spec] * 3,
                  out_shape=[jax.ShapeDtypeStruct((R, Cc), F32)] * 3,
                  compiler_params=_cparams(("parallel",)))(w, g, m, v)


def _row_tile(R, Cc, itemsize, budget=2 * 1024 * 1024):
    for nb in range(1, R // 16 + 1):
        if R % nb == 0 and (R // nb) % 16 == 0 and (R // nb) * Cc * itemsize <= budget:
            return R // nb
    return R


def _add_halves(name, gs, r1, c_idx):
    S, R, Cc = gs.shape
    half = R // 2
    tr = _row_tile(half, Cc, 4)
    nb = half // tr

    def body(c_ref, g_ref, r_ref, o_ref):
        o_ref[...] = (g_ref[...].astype(F32) + r_ref[...].astype(F32)).astype(BF16)

    grid_spec = pltpu.PrefetchScalarGridSpec(
        num_scalar_prefetch=1, grid=(S, nb),
        in_specs=[pl.BlockSpec((1, tr, Cc), lambda s, i, c: (s, c[0] * nb + i, 0)),
                  pl.BlockSpec((1, tr, Cc), lambda s, i, c: (s, i, 0))],
        out_specs=pl.BlockSpec((1, tr, Cc), lambda s, i, c: (s, i, 0)))
    return _pcall(body, name=name, grid_spec=grid_spec, out_shape=jax.ShapeDtypeStruct((S, half, Cc), BF16),
                  compiler_params=_cparams(("parallel", "parallel")))(c_idx, gs, r1)


def _sum_slots(name, r2):
    S, R, Cc = r2.shape
    tr = _row_tile(R, Cc, 4 * S // 2 if r2.dtype == BF16 else 4 * S)

    def body(r_ref, o_ref):
        acc = r_ref[0].astype(F32)
        for s in range(1, S):
            acc = acc + r_ref[s].astype(F32)
        o_ref[...] = acc

    return _pcall(body, name=name, grid=(R // tr,), in_specs=[pl.BlockSpec((S, tr, Cc), lambda i: (0, i, 0))],
                  out_specs=_tile(tr, Cc), out_shape=jax.ShapeDtypeStruct((R, Cc), F32),
                  compiler_params=_cparams(("parallel",)))(r2)


def _sum_chips(name, recv, own, place):
    S, H, Cc = recv.shape
    tr = _row_tile(H, Cc, 4, 1024 * 1024)
    nb = H // tr

    def body(p_ref, r_ref, own_ref, o_ref):
        s = pl.program_id(1)
        me = p_ref[0]

        @pl.when(s == 0)
        def _():
            o_ref[...] = jnp.zeros_like(o_ref)

        @pl.when(s == me)
        def _():
            o_ref[...] += own_ref[0].astype(F32)

        @pl.when(s != me)
        def _():
            o_ref[...] += r_ref[0].astype(F32)

    grid_spec = pltpu.PrefetchScalarGridSpec(
        num_scalar_prefetch=1, grid=(nb, S),
        in_specs=[pl.BlockSpec((1, tr, Cc), lambda i, s, p: (jnp.where(s == p[0], (s + 1) % S, s), i, 0)),
                  pl.BlockSpec((1, tr, Cc), lambda i, s, p: (p[0], i, 0))],
        out_specs=pl.BlockSpec((tr, Cc), lambda i, s, p: (p[1] * nb + i, 0)))
    return _pcall(body, name=name, grid_spec=grid_spec, out_shape=jax.ShapeDtypeStruct((2 * H, Cc), F32),
                  compiler_params=_cparams(("parallel", "arbitrary")))(place, recv, own)


def _cast_bf16(name, w):
    R, Cc = w.shape
    tr = _row_tile(R, Cc, 4)

    def body(w_ref, o_ref):
        o_ref[...] = w_ref[...].astype(BF16)

    return _pcall(body, name=name, grid=(R // tr,), in_specs=[_tile(tr, Cc)], out_specs=_tile(tr, Cc),
                  out_shape=jax.ShapeDtypeStruct((R, Cc), BF16), compiler_params=_cparams(("parallel",)))(w)


_ANY = pl.BlockSpec(memory_space=pl.ANY)


def _place():
    x, y, c = lax.axis_index("x"), lax.axis_index("y"), lax.axis_index("c")
    others = [(1 - x, y), (x, 1 - y), (1 - x, 1 - y)]
    return x, y, c, others


def _gather_weights(shards):
    arrays, out_shapes, scratch, start, finish, middle = _gather_parts(shards)
    n = len(shards)

    def body(*refs):
        ins, outs, sems = refs[:n], refs[n:2 * n], refs[2 * n:]
        start(ins, outs, sems)
        middle(ins, outs, sems)
        finish(ins, outs, sems)

    return _pcall(body, name="gather_weights", in_specs=[_ANY] * n, out_specs=[_ANY] * n, out_shape=out_shapes,
                  scratch_shapes=scratch)(*arrays)


def _gather_parts(shards):
    n = len(shards)
    halves = [s.shape[0] // 2 for s in shards]

    def parts(ins, outs, sems):
        x, y, c, _ = _place()
        me = 2 * x + y
        n1 = (x ^ (1 - c), y ^ c)
        n2 = (x ^ c, y ^ (1 - c))
        s1, s2, sd = 2 * n1[0] + n1[1], 2 * n2[0] + n2[1], 2 * (1 - x) + (1 - y)
        sib = (x, y, 1 - c)

        def rows(k, chip, hc):
            return outs[k].at[chip, pl.ds(hc * halves[k], halves[k]), :]

        def remote(k, j, src, dst, to):
            return pltpu.make_async_remote_copy(src_ref=src, dst_ref=dst, send_sem=sems[0].at[6 * k + j],
                                                recv_sem=sems[1].at[6 * k + j], device_id=to, device_id_type=MESH)

        def copy(k, j):
            if j < 2:
                mine = ins[k].at[pl.ds(c * halves[k], halves[k]), :]
                return remote(k, j, mine, rows(k, me, c), (*(n1 if j == 0 else n2), c))
            land = rows(k, {2: s1, 3: s1, 4: s2, 5: sd}[j], c)
            return remote(k, j, land, land, (*n2, c) if j == 2 else sib)

        def arrived(k, j):
            hc = c if j < 3 else 1 - c
            land = rows(k, {0: s1, 1: s2, 2: sd, 3: s2, 4: s1, 5: sd}[j], hc)
            remote(k, j, land, land, (x, y, c)).wait_recv()

        return copy, arrived

    def start(ins, outs, sems):
        copy, _ = parts(ins, outs, sems)
        for k in range(n):
            copy(k, 0).start()
            copy(k, 1).start()

    def middle(ins, outs, sems):
        copy, arrived = parts(ins, outs, sems)
        for k in range(n):
            arrived(k, 0)
            copy(k, 2).start()
            copy(k, 3).start()
            arrived(k, 1)
            copy(k, 4).start()

    def finish(ins, outs, sems):
        copy, arrived = parts(ins, outs, sems)
        for k in range(n):
            arrived(k, 2)
            copy(k, 5).start()
        for k in range(n):
            for j in (3, 4, 5):
                arrived(k, j)
        for k in range(n):
            for j in range(6):
                copy(k, j).wait_send()

    out_shapes = [jax.ShapeDtypeStruct((N_CHIPS,) + s.shape, s.dtype) for s in shards]
    scratch = [pltpu.SemaphoreType.DMA((6 * n,)), pltpu.SemaphoreType.DMA((6 * n,))]
    return list(shards), out_shapes, scratch, start, finish, middle


def _exchange_halves(name, grads):
    n = len(grads)
    halves = [g.shape[1] // 2 for g in grads]

    def body(*refs):
        ins, outs = refs[:n], refs[n:2 * n]
        send_sems, recv_sems = refs[2 * n:]
        x, y, c, _ = _place()
        cps = []
        for k in range(n):
            src = ins[k].at[:, pl.ds((1 - c) * halves[k], halves[k]), :]
            cp = pltpu.make_async_remote_copy(src_ref=src, dst_ref=outs[k], send_sem=send_sems.at[k],
                                              recv_sem=recv_sems.at[k], device_id=(x, y, 1 - c), device_id_type=MESH)
            cp.start()
            cps.append(cp)
        for cp in cps:
            cp.wait()

    return _pcall(
        body, name=name, in_specs=[_ANY] * n, out_specs=[_ANY] * n,
        out_shape=[jax.ShapeDtypeStruct((g.shape[0], h) + g.shape[2:], g.dtype) for g, h in zip(grads, halves)],
        scratch_shapes=[pltpu.SemaphoreType.DMA((n,)), pltpu.SemaphoreType.DMA((n,))],
    )(*grads)


def _scatter_to_owners(chip_sums):
    n = len(chip_sums)

    def sends(ins, outs, sems):
        x, y, c, others = _place()
        me = 2 * x + y
        return [pltpu.make_async_remote_copy(
            src_ref=ins[k].at[2 * px + py], dst_ref=outs[k].at[me], send_sem=sems[0].at[3 * k + j],
            recv_sem=sems[1].at[3 * k + j], device_id=(px, py, c), device_id_type=MESH)
            for k in range(n) for j, (px, py) in enumerate(others)]

    def start(ins, outs, sems):
        for cp in sends(ins, outs, sems):
            cp.start()

    def finish(ins, outs, sems):
        x, y, c, others = _place()
        for k in range(n):
            for j, (px, py) in enumerate(others):
                land = outs[k].at[2 * px + py]
                pltpu.make_async_remote_copy(src_ref=land, dst_ref=land, send_sem=sems[0].at[3 * k + j],
                                             recv_sem=sems[1].at[3 * k + j], device_id=(x, y, c),
                                             device_id_type=MESH).wait_recv()
        for cp in sends(ins, outs, sems):
            cp.wait_send()

    out_shapes = [jax.ShapeDtypeStruct(g.shape, g.dtype) for g in chip_sums]
    scratch = [pltpu.SemaphoreType.DMA((3 * n,)), pltpu.SemaphoreType.DMA((3 * n,))]
    return list(chip_sums), out_shapes, scratch, start, finish


def _swap_with_sibling(arrays):
    n = len(arrays)

    def copies(ins, outs, sems):
        x, y, c, _ = _place()
        return [pltpu.make_async_remote_copy(src_ref=ins[k], dst_ref=outs[k], send_sem=sems[0].at[k],
                                             recv_sem=sems[1].at[k], device_id=(x, y, 1 - c), device_id_type=MESH)
                for k in range(n)]

    def start(ins, outs, sems):
        for cp in copies(ins, outs, sems):
            cp.start()

    def finish(ins, outs, sems):
        for cp in copies(ins, outs, sems):
            cp.wait()

    out_shapes = [jax.ShapeDtypeStruct(a.shape, a.dtype) for a in arrays]
    scratch = [pltpu.SemaphoreType.DMA((n,)), pltpu.SemaphoreType.DMA((n,))]
    return list(arrays), out_shapes, scratch, start, finish


def _add_pair(name, a, b):
    R, Cc = a.shape
    tr = _row_tile(R, Cc, 4)

    def body(a_ref, b_ref, o_ref):
        o_ref[...] = (a_ref[...].astype(F32) + b_ref[...].astype(F32)).astype(BF16)

    return _pcall(body, name=name, grid=(R // tr,), in_specs=[_tile(tr, Cc)] * 2, out_specs=_tile(tr, Cc),
                  out_shape=jax.ShapeDtypeStruct((R, Cc), BF16), compiler_params=_cparams(("parallel",)))(a, b)


def _second_neighbour():
    x, y, c, _ = _place()
    return (x, y, c), (x ^ c, y ^ (1 - c)), (x ^ (1 - c), y ^ c)


def _scatter_stage1(chip_sums):
    n = len(chip_sums)

    def copies(ins, outs, sems):
        (x, y, c), n2, n1 = _second_neighbour()
        diag = 2 * (1 - x) + (1 - y)
        return [pltpu.make_async_remote_copy(
            src_ref=ins[k].at[slot], dst_ref=outs[2 * k + j], send_sem=sems[0].at[2 * k + j],
            recv_sem=sems[1].at[2 * k + j], device_id=(*n2, c), device_id_type=MESH)
            for k in range(n) for j, slot in enumerate((2 * n2[0] + n2[1], diag))]

    def start(ins, outs, sems):
        for cp in copies(ins, outs, sems):
            cp.start()

    def finish(ins, outs, sems):
        for cp in copies(ins, outs, sems):
            cp.wait()

    out_shapes = [jax.ShapeDtypeStruct(g.shape[1:], g.dtype) for g in chip_sums for _ in range(2)]
    scratch = [pltpu.SemaphoreType.DMA((2 * n,)), pltpu.SemaphoreType.DMA((2 * n,))]
    return list(chip_sums), out_shapes, scratch, start, finish


def _scatter_stage2(passed):
    n = len(passed)

    def copies(ins, outs, sems):
        (x, y, c), n2, n1 = _second_neighbour()
        return [pltpu.make_async_remote_copy(src_ref=ins[k], dst_ref=outs[k], send_sem=sems[0].at[k],
                                             recv_sem=sems[1].at[k], device_id=(*n1, c), device_id_type=MESH)
                for k in range(n)]

    def start(ins, outs, sems):
        for cp in copies(ins, outs, sems):
            cp.start()

    def finish(ins, outs, sems):
        for cp in copies(ins, outs, sems):
            cp.wait()

    out_shapes = [jax.ShapeDtypeStruct(p.shape, p.dtype) for p in passed]
    scratch = [pltpu.SemaphoreType.DMA((n,)), pltpu.SemaphoreType.DMA((n,))]
    return list(passed), out_shapes, scratch, start, finish


def _add_passed(name, own, got, slot):
    _, H, Cc = own.shape
    tr = _row_tile(H, Cc, 4)

    def body(s_ref, o_ref, g_ref, out_ref):
        out_ref[...] = (o_ref[0].astype(F32) + g_ref[...].astype(F32)).astype(BF16)

    grid_spec = pltpu.PrefetchScalarGridSpec(
        num_scalar_prefetch=1, grid=(H // tr,),
        in_specs=[pl.BlockSpec((1, tr, Cc), lambda i, s: (s[0], i, 0)), pl.BlockSpec((tr, Cc), lambda i, s: (i, 0))],
        out_specs=pl.BlockSpec((tr, Cc), lambda i, s: (i, 0)))
    return _pcall(body, name=name, grid_spec=grid_spec, out_shape=jax.ShapeDtypeStruct((H, Cc), BF16),
                  compiler_params=_cparams(("parallel",)))(slot, own, got)


def _sum_stages(name, own, direct, via, place):
    _, H, Cc = own.shape
    tr = _row_tile(H, Cc, 4, 1024 * 1024)
    nb = H // tr

    def body(p_ref, own_ref, d_ref, v_ref, o_ref):
        o_ref[...] = (own_ref[0].astype(F32) + d_ref[...].astype(F32)) + v_ref[...].astype(F32)

    flat = pl.BlockSpec((tr, Cc), lambda i, p: (i, 0))
    grid_spec = pltpu.PrefetchScalarGridSpec(
        num_scalar_prefetch=1, grid=(nb,),
        in_specs=[pl.BlockSpec((1, tr, Cc), lambda i, p: (p[0], i, 0)), flat, flat],
        out_specs=pl.BlockSpec((tr, Cc), lambda i, p: (p[1] * nb + i, 0)))
    return _pcall(body, name=name, grid_spec=grid_spec, out_shape=jax.ShapeDtypeStruct((2 * H, Cc), F32),
                  compiler_params=_cparams(("parallel",)))(place, own, direct, via)


def _join_halves(fulls, small):
    n = len(fulls)
    hs = [f.shape[0] // 2 for f in fulls]
    rel = [(dx, dy, dc) for dx in (0, 1) for dy in (0, 1) for dc in (0, 1)][1:]

    def body(*refs):
        ins, small_in = refs[:n], refs[n]
        outs, small_out = refs[n + 1:2 * n + 1], refs[2 * n + 1]
        send_sems, recv_sems, ssend, srecv, local_sem = refs[2 * n + 2:]
        x, y, c, _ = _place()
        dev = 4 * x + 2 * y + c
        local = pltpu.make_async_copy(small_in, small_out.at[dev], local_sem)
        local.start()
        cps = []
        for k in range(n):
            mine = pl.ds(c * hs[k], hs[k])
            cp = pltpu.make_async_remote_copy(src_ref=ins[k].at[mine, :], dst_ref=outs[k].at[mine, :],
                                              send_sem=send_sems.at[k], recv_sem=recv_sems.at[k],
                                              device_id=(x, y, 1 - c), device_id_type=MESH)
            cp.start()
            cps.append(cp)
        for r, (dx, dy, dc) in enumerate(rel):
            cp = pltpu.make_async_remote_copy(src_ref=small_in, dst_ref=small_out.at[dev], send_sem=ssend.at[r],
                                              recv_sem=srecv.at[r], device_id=(x ^ dx, y ^ dy, c ^ dc),
                                              device_id_type=MESH)
            cp.start()
            cps.append(cp)
        for k in range(n):
            land = outs[k].at[pl.ds((1 - c) * hs[k], hs[k]), :]
            pltpu.make_async_remote_copy(src_ref=land, dst_ref=land, send_sem=send_sems.at[k],
                                         recv_sem=recv_sems.at[k], device_id=(x, y, c), device_id_type=MESH).wait_recv()
        for r, (dx, dy, dc) in enumerate(rel):
            land = small_out.at[4 * (x ^ dx) + 2 * (y ^ dy) + (c ^ dc)]
            pltpu.make_async_remote_copy(src_ref=land, dst_ref=land, send_sem=ssend.at[r], recv_sem=srecv.at[r],
                                         device_id=(x, y, c), device_id_type=MESH).wait_recv()
        for cp in cps:
            cp.wait_send()
        local.wait()

    return _pcall(
        body, name="join_halves", in_specs=[_ANY] * (n + 1), out_specs=[_ANY] * (n + 1),
        out_shape=[jax.ShapeDtypeStruct(f.shape, f.dtype) for f in fulls]
        + [jax.ShapeDtypeStruct((N_DEV,) + small.shape, small.dtype)],
        input_output_aliases={k: k for k in range(n)},
        scratch_shapes=[pltpu.SemaphoreType.DMA((n,)), pltpu.SemaphoreType.DMA((n,)), pltpu.SemaphoreType.DMA((7,)),
                        pltpu.SemaphoreType.DMA((7,)), pltpu.SemaphoreType.DMA],
    )(*fulls, small)


def _local_step(cfg, x2, target, norm_gain, w_my, fb, mu_g, w0, a0, k_k, k_a, r_k, ln_w, ln_b, fng, rest,
                exchange=None):
    T, D, FW, FH, RW, RH, LP, lora = cfg.T, cfg.D, cfg.FW, cfg.FH, cfg.RW, cfg.RH, cfg.LP, cfg.lora
    fb_p = jnp.pad(fb, ((0, 0), (0, LANES - FH)))
    mu = _rwkv_vec_to_my(cfg, mu_g)
    rk = r_k.reshape(1, RW)
    tm = min(1024, T)

    h = _rms_fwd(cfg, x2, norm_gain)
    if len(rest) == 2:
        u, *got = _mm("in_proj", h, w_my, "nn", F32, tm, cfg.tn, 2048, comm=rest[0])
        rest = rest[1](got)
    else:
        u = _mm("in_proj", h, w_my, "nn", F32, tm, cfg.tn, 2048)
    w2, a2, wpf, wpr, wout = rest
    w2p = jnp.pad(w2, ((0, LP - lora), (0, 0)))
    a2p = jnp.pad(a2, ((0, LP - lora), (0, 0)))
    c_cols = _fox_prep(cfg, u, fb_p)
    c_rows = c_cols[:, :FH].T.reshape(FH, 1, T)
    o, lse = _attn_fwd(cfg, u, c_rows)
    oa = _gate_a_fwd(cfg, o, u)
    prep = _rwkv_prep_fwd(cfg, u, mu, w0, w2p, a0, a2p, k_k, k_a)
    r, lw, kp, v, an, b, zb = prep
    toks = [r, lw, kp, v, an, b]
    q_s, yloc, a_m, sloc = _scan_local_fwd(cfg, toks)
    y, ckpt = _scan_carry_fwd(cfg, q_s, yloc, a_m, sloc)
    ob = _rwkv_post_fwd(cfg, y, r, kp, v, zb, ln_w, ln_b, rk)
    pa = _mm("proj_fox", oa, wpf, "nn", F32, tm, 1024, 2048)
    pb = _mm("proj_rwkv", ob, wpr, "nn", F32, tm, 1024, 2048)
    m = _merge_fwd(cfg, pa, pb, u)
    mo = _mm("out_proj", m, wout, "nn", F32, tm, 1024, 2048)
    loss8, dres, dres16, d_fng = _final(cfg, x2, mo, fng.reshape(1, D), target)

    dm = _mm("out_proj_dx", dres16, wout, "nt", F32, tm, 1024, 2048)
    d_wout = _mm("out_proj_dw", m, dres16, "tn", BF16, 1024, 1024, 2048)
    dpa, dpb, du = _merge_bwd(cfg, pa, pb, u, dm)
    doa = _mm("proj_fox_dx", dpa, wpf, "nt", F32, tm, 1024, 2048)
    d_wpf = _mm("proj_fox_dw", oa, dpa, "tn", BF16, 1024, 1024, 2048)
    dob = _mm("proj_rwkv_dx", dpb, wpr, "nt", F32, tm, 1024, 2048)
    d_wpr = _mm("proj_rwkv_dw", ob, dpb, "tn", BF16, 1024, 1024, 2048)

    do, du = _gate_a_bwd(cfg, o, u, doa, du)
    du, dcol = _attn_bwd(cfg, u, c_rows, lse, do, du)
    dc = jnp.pad(-dcol.reshape(FH, T).T, ((0, 0), (0, LANES - FH)))
    df, d_fb = _fox_prep_bwd(cfg, u, fb_p, dc)

    dy, dr_p, dk_p, dv_p, dzb, d_lnw, d_lnb, d_rk = _rwkv_post_bwd(cfg, y, r, kp, v, zb, ln_w, ln_b, rk, dob)
    dq_s, da_m, dsl = _scan_carry_bwd(cfg, q_s, a_m, ckpt, dy)
    early = dict(w_proj_fox=d_wpf, w_proj_rwkv=d_wpr, w_out=d_wout)
    res = _scan_local_bwd(cfg, toks, dq_s, dy, da_m, dsl, [dr_p, dk_p, dv_p], exchange(early) if exchange else None)
    cots, received = res[:6], list(res[6:])
    dus, d_mu, d_w0, d_w2p, d_a0, d_a2p, d_kk, d_ka = _rwkv_prep_bwd(cfg, u, mu, w0, w2p, a0, a2p, k_k, k_a, cots, dzb)
    du = _shift_bwd(cfg, dus, mu, df, du)
    if exchange:
        late = dict(w_in=exchange((h, du, d_w2p[:lora], d_a2p[:lora])))
    else:
        late = dict(w_in=_mm("in_proj_dw", h, du, "tn", BF16, 1024, cfg.tn, 2048), rwkv_w2=d_w2p[:lora],
                    rwkv_a2=d_a2p[:lora])
    tkx = 2 * cfg.tn if cfg.ncol % (2 * cfg.tn) == 0 else cfg.tn
    res = _mm("in_proj_dx", du, w_my, "nt", F32, tm, 1024, tkx, comm=exchange(late) if exchange else None)
    dh = res[0] if exchange else res
    big = dict(early, **late)
    res = _rms_bwd(cfg, x2, norm_gain, dh, dres, exchange(list(res[1:])) if exchange else None)
    gx, d_ng = res[:2]
    received += list(res[2:])

    small = dict(norm_gain=d_ng, fox_forget_bias=d_fb[:, :FH], rwkv_shift_mix=_rwkv_vec_from_my(cfg, d_mu),
                 rwkv_w0=d_w0, rwkv_a0=d_a0, rwkv_k_k=d_kk, rwkv_k_a=d_ka, rwkv_r_k=d_rk, rwkv_ln_w=d_lnw,
                 rwkv_ln_b=d_lnb, final_norm_gain=d_fng)
    return loss8[0, 0], gx, small, big, received


_SMALL = ["norm_gain", "fox_forget_bias", "rwkv_shift_mix", "rwkv_w0", "rwkv_a0", "rwkv_k_k", "rwkv_k_a", "rwkv_r_k",
          "rwkv_ln_w", "rwkv_ln_b", "final_norm_gain"]
_WEIGHTS = ["norm_gain", "w_in", "fox_forget_bias", "rwkv_shift_mix", "rwkv_w0", "rwkv_w2", "rwkv_a0", "rwkv_a2",
            "rwkv_k_k", "rwkv_k_a", "rwkv_r_k", "rwkv_ln_w", "rwkv_ln_b", "w_proj_fox", "w_proj_rwkv", "w_out",
            "final_norm_gain"]


def _pack_small(arrs):
    parts = []
    for a in arrs:
        f = a.reshape(-1)
        parts.append(jnp.pad(f, (0, (-f.shape[0]) % LANES)))
    flat = jnp.concatenate(parts)
    rows = flat.shape[0] // LANES
    flat = jnp.pad(flat, (0, ((-rows) % 8) * LANES))
    return flat.reshape(-1, LANES)


def _unpack_small(packed, shapes):
    flat = packed.reshape(-1)
    out, pos = [], 0
    for s in shapes:
        n = int(np.prod(s))
        out.append(flat[pos:pos + n].reshape(s))
        pos += n + ((-n) % LANES)
    return out


def _shard_major(a, axis):
    parts = jnp.split(a, N_CHIPS, axis=axis)
    return jnp.stack(parts, axis=0)


def kernel(x, norm_gain, w_in, fox_forget_bias, rwkv_shift_mix, rwkv_w0, rwkv_w2, rwkv_a0, rwkv_a2, rwkv_k_k, rwkv_k_a, rwkv_r_k, rwkv_ln_w, rwkv_ln_b, w_proj_fox, w_proj_rwkv, w_out, final_norm_gain, loss_target, m_norm_gain, m_w_in, m_fox_forget_bias, m_rwkv_shift_mix, m_rwkv_w0, m_rwkv_w2, m_rwkv_a0, m_rwkv_a2, m_rwkv_k_k, m_rwkv_k_a, m_rwkv_r_k, m_rwkv_ln_w, m_rwkv_ln_b, m_w_proj_fox, m_w_proj_rwkv, m_w_out, m_final_norm_gain, v_norm_gain, v_w_in, v_fox_forget_bias, v_rwkv_shift_mix, v_rwkv_w0, v_rwkv_w2, v_rwkv_a0, v_rwkv_a2, v_rwkv_k_k, v_rwkv_k_a, v_rwkv_r_k, v_rwkv_ln_w, v_rwkv_ln_b, v_w_proj_fox, v_w_proj_rwkv, v_w_out, v_final_norm_gain):
    args = dict(locals())
    T, D = x.shape[1], x.shape[2]
    lora = rwkv_w2.shape[1]
    cfg = _Cfg(T, D, lora)
    RW = cfg.RW
    c_idx = lax.axis_index("c").astype(jnp.int32).reshape(1)
    me_chip = (2 * lax.axis_index("x") + lax.axis_index("y")).astype(jnp.int32)
    place = jnp.concatenate([me_chip.reshape(1), c_idx])

    w_in_s = w_in[0].astype(BF16)
    lora_s = jnp.concatenate([rwkv_w2[0], rwkv_a2[0]], axis=0)
    own_slot = lambda g, own: lax.dynamic_update_slice(g, own[None], (me_chip, 0, 0))
    w_my = _shards_to_my_layout(cfg, own_slot(_gather_weights([w_in_s])[0], w_in_s))
    mine = [_cast_bf16("cast_w_proj_fox", w_proj_fox[0]), _cast_bf16("cast_w_proj_rwkv", w_proj_rwkv[0]),
            _cast_bf16("cast_w_out", w_out[0]), lora_s]

    def unpack(gathered):
        g_wpf, g_wpr, g_out, g_lora = [own_slot(g, own) for g, own in zip(gathered, mine)]
        lo = g_lora.transpose(1, 0, 2).reshape(2 * lora, RW)
        return (lo[:lora], lo[lora:], g_wpf.transpose(1, 0, 2).reshape(RW, D),
                g_wpr.transpose(1, 0, 2).reshape(RW, D), g_out.reshape(D, D))

    early, late = ["w_proj_fox", "w_proj_rwkv", "w_out"], ["w_in", "lora"]
    names = early + late
    chip_sums, direct = {}, {}
    n1_slot = (2 * (lax.axis_index("x") ^ (1 - lax.axis_index("c")))
               + (lax.axis_index("y") ^ lax.axis_index("c"))).astype(jnp.int32).reshape(1)

    def exchange(got):
        if isinstance(got, tuple):
            h, du, d_w2, d_a2 = got
            c, half = lax.axis_index("c"), D // 2
            cols = lambda base: lax.dynamic_slice_in_dim(h, base * half, half, axis=1)
            lora_g = _shard_major(jnp.concatenate([d_w2, d_a2], axis=0).astype(BF16), 1)
            lora_rows = lambda base: lax.dynamic_slice_in_dim(lora_g, base * lora, lora, axis=1).reshape(-1, RW // 4)
            tiles = (BF16, min(1024, half), cfg.tn, 2048)
            sent = _mm("in_proj_dw_sibling", cols(1 - c), du, "tn", *tiles)
            kept, got_w, got_l = _mm("in_proj_dw", cols(c), du, "tn", *tiles,
                                     comm=_swap_with_sibling([sent, lora_rows(1 - c)]))
            return (_add_pair("add_halves_w_in", kept, got_w),
                    _add_pair("add_halves_lora", lora_rows(c), got_l).reshape(N_CHIPS, lora, RW // 4))
        if isinstance(got, dict):
            if "w_in" in got:
                sums = [_my_layout_to_shards(cfg, got["w_in"][0]), got["w_in"][1]]
                chip_sums.update(zip(late, sums))
                return _scatter_stage1(sums)
            gs = [_shard_major(got["w_proj_fox"], 1), _shard_major(got["w_proj_rwkv"], 1),
                  _shard_major(got["w_out"], 0)]
            recv1 = _exchange_halves("exchange_halves_" + early[0], gs)
            sums = [_add_halves("add_halves_" + nm, g, r, c_idx) for nm, g, r in zip(early, gs, recv1)]
            chip_sums.update(zip(early, sums))
            return _scatter_to_owners(sums)
        direct.update(zip(late, got[0::2]))
        return _scatter_stage2([_add_passed("add_passed_" + nm, chip_sums[nm], g, n1_slot)
                                for nm, g in zip(late, got[1::2])])

    loss_dev, gx, small, _, recv2 = _local_step(
        cfg, x[0], loss_target[0], norm_gain, w_my, fox_forget_bias, rwkv_shift_mix, rwkv_w0, rwkv_a0, rwkv_k_k,
        rwkv_k_a, rwkv_r_k, rwkv_ln_w, rwkv_ln_b, final_norm_gain, (_gather_parts(mine), unpack), exchange)
    loss = lax.psum(loss_dev, ("x", "y", "c"))

    small_shapes = [args[nm].shape for nm in _SMALL]
    packed = _pack_small([small[nm] for nm in _SMALL])
    reduced = [_sum_chips("sum_chips_" + nm, r, chip_sums[nm], place) for nm, r in zip(early, recv2[:3])]
    reduced += [_sum_stages("sum_stages_" + nm, chip_sums[nm], direct[nm], via, place)
                for nm, via in zip(late, recv2[3:])]
    *joined, small_all = _join_halves(reduced, packed)
    g_small = _sum_slots("sum_small", small_all)

    grads = dict(zip(_SMALL, _unpack_small(g_small, small_shapes)))
    grads.update({nm: g[None] for nm, g in zip(names, joined) if nm != "lora"})
    g_lora_f = joined[names.index("lora")]
    grads["rwkv_w2"] = g_lora_f[None, :lora]
    grads["rwkv_a2"] = g_lora_f[None, lora:]

    delta, new_m, new_v = {}, {}, {}
    w_small = _pack_small([args[nm] for nm in _SMALL])
    m_small = _pack_small([args["m_" + nm] for nm in _SMALL])
    v_small = _pack_small([args["v_" + nm] for nm in _SMALL])
    d_s, m_s, v_s = _adamw("adamw_small", w_small, g_small, m_small, v_small)
    for tgt, pk in ((delta, d_s), (new_m, m_s), (new_v, v_s)):
        tgt.update(zip(_SMALL, _unpack_small(pk, small_shapes)))
    for nm in ("w_in", "w_proj_fox", "w_proj_rwkv", "w_out", "rwkv_w2", "rwkv_a2"):
        shp = args[nm].shape
        two_d = (shp[1], shp[2])
        d_b, m_b, v_b = _adamw("adamw_" + nm, args[nm].reshape(two_d), grads[nm].reshape(two_d),
                               args["m_" + nm].reshape(two_d), args["v_" + nm].reshape(two_d))
        delta[nm], new_m[nm], new_v[nm] = d_b.reshape(shp), m_b.reshape(shp), v_b.reshape(shp)

    return (loss, gx[None], *[grads[n] for n in _WEIGHTS], *[delta[n] for n in _WEIGHTS],
            *[new_m[n] for n in _WEIGHTS], *[new_v[n] for n in _WEIGHTS])
```

```python
import functools

import numpy as np
import jax
import jax.numpy as jnp
from jax import lax
from jax.experimental import pallas as pl
from jax.experimental.pallas import tpu as pltpu

F32 = jnp.float32
BF16 = jnp.bfloat16
HI = lax.Precision.HIGHEST
MESH = pl.DeviceIdType.MESH

FOX_HEAD_DIM = 128
RWKV_HEAD_DIM = 64
RMS_EPS = 1e-6
GN_EPS = 64e-5
L2_EPS = 1e-12
ADAM_LR = 0.001
ADAM_B1 = 0.9
ADAM_B2 = 0.999
ADAM_EPS = 1e-08
ADAM_WD = 0.01
ADAM_STEP = 10

LANES = 128
VMEM_LIMIT = 56 * 1024 * 1024
SCAN_CHUNK = 64
SCAN_HEADS_PER_STEP = 16
SCAN_PASSES = (1, 1, 1)
N_CHIPS = 4
N_DEV = 8

_pcall = pl.pallas_call


def _cparams(sem=None):
    return pltpu.CompilerParams(dimension_semantics=sem, vmem_limit_bytes=VMEM_LIMIT)


def _softplus(x):
    return jnp.maximum(x, 0.0) + jnp.log(1.0 + jnp.exp(-jnp.abs(x)))


def _silu(z):
    return z * jax.nn.sigmoid(z)


def _rmsn(x, g):
    return x * lax.rsqrt(jnp.mean(x * x, axis=-1, keepdims=True) + RMS_EPS) * g


def _dot(a, b, dims="nn", precision=None):
    dn = {"nn": (((1,), (0,)), ((), ())), "nt": (((1,), (1,)), ((), ())), "tn": (((0,), (0,)), ((), ()))}[dims]
    return lax.dot_general(a, b, dn, precision=precision, preferred_element_type=F32)


def _split_bf16(x):
    hi = x.astype(BF16)
    return hi, (x - hi.astype(F32)).astype(BF16)


def _bdot_raw(a, b, ca, cb, passes):
    dn = (((ca,), (cb,)), ((0,), (0,)))
    mm = lambda p, q: lax.dot_general(p, q, dn, preferred_element_type=F32)
    if passes == 1:
        return mm(a.astype(BF16), b.astype(BF16))
    ah, al = _split_bf16(a)
    bh, bl = _split_bf16(b)
    return mm(ah, bh) + (mm(ah, bl) + mm(al, bh))


@functools.partial(jax.custom_vjp, nondiff_argnums=(2, 3, 4))
def _bdot_p(a, b, ca, cb, passes):
    return _bdot_raw(a, b, ca, cb, passes)


def _bdot_fwd(a, b, ca, cb, passes):
    return _bdot_raw(a, b, ca, cb, passes), (a, b)


def _bdot_bwd(ca, cb, passes, res, g):
    a, b = res
    if (ca, cb) == (2, 1):
        return _bdot_p(g, b, 2, 2, passes), _bdot_p(a, g, 1, 1, passes)
    if (ca, cb) == (2, 2):
        return _bdot_p(g, b, 2, 1, passes), _bdot_p(g, a, 1, 1, passes)
    assert (ca, cb) == (1, 1)
    return _bdot_p(b, g, 2, 2, passes), _bdot_p(a, g, 2, 1, passes)


_bdot_p.defvjp(_bdot_fwd, _bdot_bwd)


def _bdot(a, b, ca, cb, passes=3):
    return _bdot_p(a, b, ca, cb, passes)


def _dot3(a, b):
    return _bdot(a[None], b[None], 2, 1)[0]


@jax.custom_vjp
def _xdot(x, m, mt):
    hi, lo = _split_bf16(x)
    m16 = m.astype(BF16)
    return _dot(hi, m16) + _dot(lo, m16)


def _xdot_fwd(x, m, mt):
    return _xdot(x, m, mt), (m, mt)


def _xdot_bwd(res, g):
    m, mt = res
    return _xdot(g, mt, m), jnp.zeros_like(m), jnp.zeros_like(mt)


_xdot.defvjp(_xdot_fwd, _xdot_bwd)


class _Cfg:
    def __init__(self, T, D, lora):
        self.T, self.D, self.lora = T, D, lora
        self.FW = D // 2
        self.FH = self.FW // FOX_HEAD_DIM
        self.RW = D // 2
        self.RH = self.RW // RWKV_HEAD_DIM
        self.LP = -(-lora // LANES) * LANES
        self.o_fox = 0
        self.o_rwkv = 4 * self.FW
        self.o_gate = self.o_rwkv + 4 * self.RW
        self.o_f = self.o_gate + 2 * D
        self.o_wd = self.o_f + LANES
        self.o_ad = self.o_wd + self.LP
        end = self.o_ad + self.LP
        self.tn = 1280 if D >= 2048 else LANES
        self.ncol = -(-end // self.tn) * self.tn
        self.in_cols = 4 * self.FW + self.FH + 4 * self.RW + 2 * lora + 2 * D
        self.scp = -(-(self.in_cols // N_CHIPS) // LANES) * LANES
        self.rseg = 4 * self.RW + 2 * self.LP
        self.C = min(SCAN_CHUNK, T)
        self.tr = min(256, T)
        self.hb = min(SCAN_HEADS_PER_STEP, self.RH)

    def segments(self):
        FW, FH, RW, lo, D = self.FW, self.FH, self.RW, self.lora, self.D
        g_f = 4 * FW
        g_r = g_f + FH
        g_wd = g_r + 4 * RW
        g_ad = g_wd + lo
        g_g = g_ad + lo
        dh = FOX_HEAD_DIM
        qkv = [(j * FW + h * dh, dh, (3 * h + j) * dh) for h in range(FH) for j in range(3)]
        return qkv + [(3 * FW, FW, 3 * FW), (g_f, FH, self.o_f), (g_r, 4 * RW, self.o_rwkv), (g_wd, lo, self.o_wd),
                      (g_ad, lo, self.o_ad), (g_g, 2 * D, self.o_gate)]


def _shards_to_my_layout(cfg, g):
    R, sc = g.shape[1], g.shape[2]
    segs = sorted(cfg.segments(), key=lambda s: s[2])
    parts, pos = [], 0
    for g0, w, m0 in segs:
        if m0 > pos:
            parts.append(jnp.zeros((R, m0 - pos), g.dtype))
        for s in range(N_CHIPS):
            lo, hi = max(g0, s * sc), min(g0 + w, (s + 1) * sc)
            if lo < hi:
                parts.append(g[s, :, lo - s * sc:hi - s * sc])
        pos = m0 + w
    if cfg.ncol > pos:
        parts.append(jnp.zeros((R, cfg.ncol - pos), g.dtype))
    return jnp.concatenate(parts, axis=1)


def _my_layout_to_shards(cfg, wm):
    sc, R = cfg.in_cols // N_CHIPS, wm.shape[0]
    segs = sorted(cfg.segments(), key=lambda s: s[0])
    shards = []
    for s in range(N_CHIPS):
        parts = []
        for g0, w, m0 in segs:
            lo, hi = max(g0, s * sc), min(g0 + w, (s + 1) * sc)
            if lo < hi:
                parts.append(wm[:, m0 + lo - g0:m0 + hi - g0])
        parts.append(jnp.zeros((R, cfg.scp - sc), wm.dtype))
        shards.append(jnp.concatenate(parts, axis=1))
    return jnp.stack(shards, axis=0)


def _rwkv_vec_to_my(cfg, v):
    RW4, lo, LP = 4 * cfg.RW, cfg.lora, cfg.LP
    z = jnp.zeros((1, LP - lo), v.dtype)
    return jnp.concatenate([v[:, :RW4], v[:, RW4:RW4 + lo], z, v[:, RW4 + lo:], z], axis=1)


def _rwkv_vec_from_my(cfg, v):
    RW4, lo, LP = 4 * cfg.RW, cfg.lora, cfg.LP
    return jnp.concatenate([v[:, :RW4], v[:, RW4:RW4 + lo], v[:, RW4 + LP:RW4 + LP + lo]], axis=1)


def _comm_at(comm, which, steps, cin, cout, scr):
    if not comm or len(comm) <= which:
        return
    lin, total = 0, 1
    for d, n in enumerate(steps):
        lin = lin * n + pl.program_id(d)
        total *= n
    pl.when(lin == {3: 0, 4: total - 1, 5: total // 2}[which])(lambda: comm[which](cin, cout, scr))


def _mm(name, a, b, dims, out_dtype, tm, tn, tk, comm=None):
    (M, K) = a.shape if dims != "tn" else a.shape[::-1]
    N = b.shape[0] if dims == "nt" else b.shape[1]
    tm, tn, tk = min(tm, M), min(tn, N), min(tk, K)
    assert M % tm == 0 and N % tn == 0 and K % tk == 0, (name, M, N, K, tm, tn, tk)
    nk = K // tk
    steps = (M // tm, N // tn, nk)
    c_in, c_out, c_scr = comm[:3] if comm else ([], [], [])
    if dims == "nn":
        a_spec = pl.BlockSpec((tm, tk), lambda i, j, k: (i, k))
        b_spec = pl.BlockSpec((tk, tn), lambda i, j, k: (k, j))
    elif dims == "nt":
        a_spec = pl.BlockSpec((tm, tk), lambda i, j, k: (i, k))
        b_spec = pl.BlockSpec((tn, tk), lambda i, j, k: (j, k))
    else:
        a_spec = pl.BlockSpec((tk, tm), lambda i, j, k: (k, i))
        b_spec = pl.BlockSpec((tk, tn), lambda i, j, k: (k, j))

    n_acc = 1 if nk > 1 else 0

    def body(a_ref, b_ref, *rest):
        cin, o_ref = rest[:len(c_in)], rest[len(c_in)]
        cout = rest[len(c_in) + 1:len(c_in) + 1 + len(c_out)]
        scr = rest[len(c_in) + 1 + len(c_out):]
        _comm_at(comm, 3, steps, cin, cout, scr[n_acc:])
        if nk == 1:
            o_ref[...] = _dot(a_ref[...], b_ref[...], dims).astype(o_ref.dtype)
        else:
            acc_ref, k = scr[0], pl.program_id(2)

            @pl.when(k == 0)
            def _():
                acc_ref[...] = jnp.zeros_like(acc_ref)

            acc_ref[...] += _dot(a_ref[...], b_ref[...], dims)

            @pl.when(k == nk - 1)
            def _():
                o_ref[...] = acc_ref[...].astype(o_ref.dtype)

        _comm_at(comm, 5, steps, cin, cout, scr[n_acc:])
        _comm_at(comm, 4, steps, cin, cout, scr[n_acc:])

    res = _pcall(
        body, name=name, grid=steps,
        in_specs=[a_spec, b_spec] + [_ANY] * len(c_in),
        out_specs=[pl.BlockSpec((tm, tn), lambda i, j, k: (i, j))] + [_ANY] * len(c_out),
        out_shape=[jax.ShapeDtypeStruct((M, N), out_dtype)] + list(c_out),
        scratch_shapes=([pltpu.VMEM((tm, tn), F32)] if nk > 1 else []) + list(c_scr),
        compiler_params=_cparams(("arbitrary",) * 3 if comm else ("parallel", "parallel", "arbitrary")),
    )(a, b, *c_in)
    return res if comm else res[0]


def _tile(tr, w, cb=0):
    return pl.BlockSpec((tr, w), lambda i: (i, cb))


def _const(shape):
    nd = len(shape)
    return pl.BlockSpec(shape, lambda i: (0,) * nd)


def _acc_store(i, ref, val):
    @pl.when(i == 0)
    def _():
        ref[...] = val

    @pl.when(i > 0)
    def _():
        ref[...] += val


def _rms_fwd(cfg, x2, g):
    T, D, tr = cfg.T, cfg.D, cfg.tr

    def body(x_ref, g_ref, h_ref):
        h_ref[...] = _rmsn(x_ref[...], g_ref[...]).astype(BF16)

    return _pcall(body, name="rms_fwd", grid=(T // tr,), in_specs=[_tile(tr, D), _const((1, D))],
                  out_specs=_tile(tr, D), out_shape=jax.ShapeDtypeStruct((T, D), BF16),
                  compiler_params=_cparams(("parallel",)))(x2, g)


def _rms_bwd(cfg, x2, g, dh, dres, comm=None):
    T, D, tr = cfg.T, cfg.D, cfg.tr
    c_in, c_out, c_scr = comm[:3] if comm else ([], [], [])
    steps = (T // tr,)

    def body(x_ref, g_ref, dh_ref, dres_ref, *rest):
        cin, (gx_ref, dg_ref) = rest[:len(c_in)], rest[len(c_in):len(c_in) + 2]
        cout, scr = rest[len(c_in) + 2:len(c_in) + 2 + len(c_out)], rest[len(c_in) + 2 + len(c_out):]
        _comm_at(comm, 3, steps, cin, cout, scr)
        _, vjp = jax.vjp(_rmsn, x_ref[...], g_ref[...])
        dx, dg = vjp(dh_ref[...])
        gx_ref[...] = dx + dres_ref[...]
        _acc_store(pl.program_id(0), dg_ref, dg)
        _comm_at(comm, 4, steps, cin, cout, scr)

    return _pcall(body, name="rms_bwd", grid=steps,
                  in_specs=[_tile(tr, D), _const((1, D)), _tile(tr, D), _tile(tr, D)] + [_ANY] * len(c_in),
                  out_specs=[_tile(tr, D), _const((1, D))] + [_ANY] * len(c_out),
                  out_shape=[jax.ShapeDtypeStruct((T, D), F32), jax.ShapeDtypeStruct((1, D), F32)] + list(c_out),
                  scratch_shapes=list(c_scr), compiler_params=_cparams(("arbitrary",)))(x2, g, dh, dres, *c_in)


def _final(cfg, x2, mo, fg, target):
    T, D, tr = cfg.T, cfg.D, cfg.tr

    def loss_fn(hres, g, tgt):
        err = _rmsn(hres, g) - tgt
        return 0.5 * jnp.sum(jnp.mean(err * err, axis=-1, keepdims=True), axis=0, keepdims=True)

    def body(x_ref, mo_ref, g_ref, t_ref, loss_ref, dres_ref, dres16_ref, dg_ref):
        hres = x_ref[...] + mo_ref[...]
        loss, vjp = jax.vjp(functools.partial(loss_fn, tgt=t_ref[...]), hres, g_ref[...])
        dres, dg = vjp(jnp.ones((1, 1), F32))
        dres_ref[...] = dres
        dres16_ref[...] = dres.astype(BF16)
        i = pl.program_id(0)
        _acc_store(i, dg_ref, dg)
        _acc_store(i, loss_ref, jnp.broadcast_to(loss, (8, LANES)))

    return _pcall(body, name="final_loss", grid=(T // tr,),
                  in_specs=[_tile(tr, D), _tile(tr, D), _const((1, D)), _tile(tr, D)],
                  out_specs=[_const((8, LANES)), _tile(tr, D), _tile(tr, D), _const((1, D))],
                  out_shape=[jax.ShapeDtypeStruct((8, LANES), F32), jax.ShapeDtypeStruct((T, D), F32),
                             jax.ShapeDtypeStruct((T, D), BF16), jax.ShapeDtypeStruct((1, D), F32)],
                  compiler_params=_cparams(("arbitrary",)))(x2, mo, fg, target)


def _merge_fn(pa, pb, ga, gb):
    return jax.nn.sigmoid(ga) * pa + jax.nn.sigmoid(gb) * pb


def _merge_fwd(cfg, pa, pb, u):
    T, D, tr = cfg.T, cfg.D, cfg.tr
    cga, cgb = cfg.o_gate // D, cfg.o_gate // D + 1

    def body(pa_ref, pb_ref, ga_ref, gb_ref, m_ref):
        m_ref[...] = _merge_fn(pa_ref[...], pb_ref[...], ga_ref[...], gb_ref[...]).astype(BF16)

    return _pcall(body, name="merge_fwd", grid=(T // tr,),
                  in_specs=[_tile(tr, D), _tile(tr, D), _tile(tr, D, cga), _tile(tr, D, cgb)],
                  out_specs=_tile(tr, D), out_shape=jax.ShapeDtypeStruct((T, D), BF16),
                  compiler_params=_cparams(("parallel",)))(pa, pb, u, u)


def _merge_bwd(cfg, pa, pb, u, dm):
    T, D, tr = cfg.T, cfg.D, cfg.tr
    cga, cgb = cfg.o_gate // D, cfg.o_gate // D + 1

    def body(pa_ref, pb_ref, ga_ref, gb_ref, dm_ref, dpa_ref, dpb_ref, dg_ref):
        _, vjp = jax.vjp(_merge_fn, pa_ref[...], pb_ref[...], ga_ref[...], gb_ref[...])
        dpa, dpb, dga, dgb = vjp(dm_ref[...])
        dpa_ref[...] = dpa.astype(BF16)
        dpb_ref[...] = dpb.astype(BF16)
        dg_ref[:, :D] = dga.astype(BF16)
        dg_ref[:, D:] = dgb.astype(BF16)

    return _pcall(body, name="merge_bwd", grid=(T // tr,),
                  in_specs=[_tile(tr, D), _tile(tr, D), _tile(tr, D, cga), _tile(tr, D, cgb), _tile(tr, D)],
                  out_specs=[_tile(tr, D), _tile(tr, D), _tile(tr, 2 * D, cfg.o_gate // (2 * D))],
                  out_shape=[jax.ShapeDtypeStruct((T, D), BF16), jax.ShapeDtypeStruct((T, D), BF16),
                             jax.ShapeDtypeStruct((T, cfg.ncol), BF16)],
                  compiler_params=_cparams(("parallel",)))(pa, pb, u, u, dm)


def _gate_fn(o, z):
    return o * _silu(z)


def _gate_a_fwd(cfg, o, u):
    T, FW, tr = cfg.T, cfg.FW, cfg.tr

    def body(o_ref, z_ref, oa_ref):
        oa_ref[...] = _gate_fn(o_ref[...], z_ref[...]).astype(BF16)

    return _pcall(body, name="gate_a_fwd", grid=(T // tr,), in_specs=[_tile(tr, FW), _tile(tr, FW, 3)],
                  out_specs=_tile(tr, FW), out_shape=jax.ShapeDtypeStruct((T, FW), BF16),
                  compiler_params=_cparams(("parallel",)))(o, u)


def _gate_a_bwd(cfg, o, u, doa, du):
    T, FW, tr = cfg.T, cfg.FW, cfg.tr

    def body(o_ref, z_ref, doa_ref, du_in, do_ref, dz_ref):
        _, vjp = jax.vjp(_gate_fn, o_ref[...], z_ref[...])
        do, dz = vjp(doa_ref[...])
        do_ref[...] = do
        dz_ref[...] = dz.astype(BF16)

    return _pcall(body, name="gate_a_bwd", grid=(T // tr,),
                  in_specs=[_tile(tr, FW), _tile(tr, FW, 3), _tile(tr, FW), _ANY],
                  out_specs=[_tile(tr, FW), _tile(tr, FW, 3)],
                  out_shape=[jax.ShapeDtypeStruct((T, FW), F32), jax.ShapeDtypeStruct(du.shape, BF16)],
                  input_output_aliases={3: 1},
                  compiler_params=_cparams(("parallel",)))(o, u, doa, du)


def _fox_prep(cfg, u, fb):
    T, tr = cfg.T, cfg.tr
    cf = cfg.o_f // LANES

    def body(f_ref, fb_ref, c_ref, carry_ref):
        i = pl.program_id(0)

        @pl.when(i == 0)
        def _():
            carry_ref[...] = jnp.zeros_like(carry_ref)

        lf = -_softplus(-(f_ref[...] + fb_ref[...]))
        r = lax.broadcasted_iota(jnp.int32, (tr, tr), 0)
        c = lax.broadcasted_iota(jnp.int32, (tr, tr), 1)
        tri = (r >= c).astype(F32)
        c_ref[...] = _dot(tri, lf, precision=HI) + carry_ref[...]
        carry_ref[...] += jnp.sum(lf, axis=0, keepdims=True)

    return _pcall(body, name="fox_prep", grid=(T // tr,), in_specs=[_tile(tr, LANES, cf), _const((1, LANES))],
                  out_specs=_tile(tr, LANES), out_shape=jax.ShapeDtypeStruct((T, LANES), F32),
                  scratch_shapes=[pltpu.VMEM((1, LANES), F32)], compiler_params=_cparams(("arbitrary",)))(u, fb)


def _fox_prep_bwd(cfg, u, fb, dc):
    T, tr = cfg.T, cfg.tr
    cf = cfg.o_f // LANES
    nb = T // tr

    def body(f_ref, fb_ref, dc_ref, df_ref, dfb_ref, carry_ref):
        i = pl.program_id(0)

        @pl.when(i == 0)
        def _():
            carry_ref[...] = jnp.zeros_like(carry_ref)

        dc = dc_ref[...]
        r = lax.broadcasted_iota(jnp.int32, (tr, tr), 0)
        c = lax.broadcasted_iota(jnp.int32, (tr, tr), 1)
        triu = (r <= c).astype(F32)
        dlf = _dot(triu, dc, precision=HI) + carry_ref[...]
        carry_ref[...] += jnp.sum(dc, axis=0, keepdims=True)
        dz = dlf * jax.nn.sigmoid(-(f_ref[...] + fb_ref[...]))
        df_ref[...] = dz.astype(BF16)
        _acc_store(i, dfb_ref, jnp.sum(dz, axis=0, keepdims=True))

    rev = lambda i: (nb - 1 - i, 0)
    return _pcall(body, name="fox_prep_bwd", grid=(nb,),
                  in_specs=[pl.BlockSpec((tr, LANES), lambda i: (nb - 1 - i, cf)), _const((1, LANES)),
                            pl.BlockSpec((tr, LANES), rev)],
                  out_specs=[pl.BlockSpec((tr, LANES), rev), _const((1, LANES))],
                  out_shape=[jax.ShapeDtypeStruct((T, LANES), BF16), jax.ShapeDtypeStruct((1, LANES), F32)],
                  scratch_shapes=[pltpu.VMEM((1, LANES), F32)], compiler_params=_cparams(("arbitrary",)))(u, fb, dc)


def _attn_logits(q_ref, k_ref, c_ref, i, tq, te):
    s = _dot(q_ref[...].astype(BF16), k_ref[0:te, :].astype(BF16), "nt") * (FOX_HEAD_DIM ** -0.5) - c_ref[0, :, 0:te]
    row = i * tq + lax.broadcasted_iota(jnp.int32, (tq, te), 0)
    col = lax.broadcasted_iota(jnp.int32, (tq, te), 1)
    return jnp.where(col <= row, s, -1e30)


def _per_query_tile(i, nq, tq, fn):
    for ii in range(nq):
        pl.when(i == ii)(functools.partial(fn, (ii + 1) * tq))


def _attn_fwd(cfg, u, c_rows):
    T, FW, FH = cfg.T, cfg.FW, cfg.FH
    tq = min(256, T)
    dh = FOX_HEAD_DIM

    def body(q_ref, k_ref, v_ref, c_ref, o_ref, lse_ref):
        i = pl.program_id(1)

        def tile(te):
            s = _attn_logits(q_ref, k_ref, c_ref, i, tq, te)
            m = jnp.max(s, axis=1, keepdims=True)
            p = jnp.exp(s - m)
            l = jnp.sum(p, axis=1, keepdims=True)
            o_ref[...] = _dot(p.astype(BF16), v_ref[0:te, :].astype(BF16)) / l
            lse_ref[0] = m + jnp.log(l)

        _per_query_tile(i, T // tq, tq, tile)

    return _pcall(
        body, name="fox_attn_fwd", grid=(FH, T // tq),
        in_specs=[pl.BlockSpec((tq, dh), lambda h, i: (i, 3 * h)), pl.BlockSpec((T, dh), lambda h, i: (0, 3 * h + 1)),
                  pl.BlockSpec((T, dh), lambda h, i: (0, 3 * h + 2)), pl.BlockSpec((1, 1, T), lambda h, i: (h, 0, 0))],
        out_specs=[pl.BlockSpec((tq, dh), lambda h, i: (i, h)), pl.BlockSpec((1, tq, 1), lambda h, i: (h, i, 0))],
        out_shape=[jax.ShapeDtypeStruct((T, FW), F32), jax.ShapeDtypeStruct((FH, T, 1), F32)],
        compiler_params=_cparams(("parallel", "arbitrary")),
    )(u, u, u, c_rows)


def _attn_bwd(cfg, u, c_rows, lse, do, du):
    T, FW, FH = cfg.T, cfg.FW, cfg.FH
    tq = min(256, T)
    nq = T // tq
    dh = FOX_HEAD_DIM
    scale = dh ** -0.5

    def body(q_ref, k_ref, v_ref, c_ref, lse_ref, do_ref, du_in, du_ref, dcol_ref, dk_acc, dv_acc):
        i = pl.program_id(1)

        @pl.when(i == 0)
        def _():
            dk_acc[...] = jnp.zeros_like(dk_acc)
            dv_acc[...] = jnp.zeros_like(dv_acc)
            dcol_ref[...] = jnp.zeros_like(dcol_ref)

        def tile(te):
            s = _attn_logits(q_ref, k_ref, c_ref, i, tq, te)
            p = jnp.exp(s - lse_ref[0])
            do_v = do_ref[...]
            dp = _dot(do_v.astype(BF16), v_ref[0:te, :].astype(BF16), "nt")
            delta = jnp.sum(p * dp, axis=1, keepdims=True)
            ds = p * (dp - delta)
            ds16 = ds.astype(BF16)
            du_ref[te - tq:te, 0:dh] = (_dot(ds16, k_ref[0:te, :].astype(BF16)) * scale).astype(BF16)
            dk_acc[0:te, :] += _dot(ds16, q_ref[...].astype(BF16), "tn") * scale
            dv_acc[0:te, :] += _dot(p.astype(BF16), do_v.astype(BF16), "tn")
            dcol_ref[0, :, 0:te] += jnp.sum(ds, axis=0, keepdims=True)

        _per_query_tile(i, nq, tq, tile)

        @pl.when(i == nq - 1)
        def _():
            du_ref[:, dh:2 * dh] = dk_acc[...].astype(BF16)
            du_ref[:, 2 * dh:3 * dh] = dv_acc[...].astype(BF16)

    return _pcall(
        body, name="fox_attn_bwd", grid=(FH, nq),
        in_specs=[pl.BlockSpec((tq, dh), lambda h, i: (i, 3 * h)), pl.BlockSpec((T, dh), lambda h, i: (0, 3 * h + 1)),
                  pl.BlockSpec((T, dh), lambda h, i: (0, 3 * h + 2)), pl.BlockSpec((1, 1, T), lambda h, i: (h, 0, 0)),
                  pl.BlockSpec((1, tq, 1), lambda h, i: (h, i, 0)), pl.BlockSpec((tq, dh), lambda h, i: (i, h)), _ANY],
        out_specs=[pl.BlockSpec((T, 3 * dh), lambda h, i: (0, h)), pl.BlockSpec((1, 1, T), lambda h, i: (h, 0, 0))],
        out_shape=[jax.ShapeDtypeStruct(du.shape, BF16), jax.ShapeDtypeStruct((FH, 1, T), F32)],
        scratch_shapes=[pltpu.VMEM((T, dh), F32), pltpu.VMEM((T, dh), F32)],
        input_output_aliases={6: 0},
        compiler_params=_cparams(("parallel", "arbitrary")),
    )(u, u, u, c_rows, lse, do, du)


def _head_indicators(cfg):
    ind = np.zeros((cfg.RW, LANES), np.float32)
    ind[np.arange(cfg.RW), np.arange(cfg.RW) // RWKV_HEAD_DIM] = 1.0
    pad = np.zeros((1, LANES), np.float32)
    pad[0, cfg.RH:] = 1.0
    return jnp.asarray(ind), jnp.asarray(ind.T.copy()), jnp.asarray(pad)


def _prep_fn(us_r, us_k, us_v, us_wd, us_ad, w0, w2p, a0, a2p, k_k, k_a, ind, ind_t, pad):
    wpre = w0 + _dot3(jnp.tanh(us_wd), w2p)
    w = -_softplus(-wpre) - 0.5
    lw = -jnp.exp(w)
    a = jax.nn.sigmoid(a0 + _dot3(us_ad, a2p))
    kk = us_k * k_k
    ss = _xdot(kk * kk, ind, ind_t) + pad
    inv = 1.0 / jnp.maximum(jnp.sqrt(ss), L2_EPS)
    kkn = kk * _xdot(inv, ind_t, ind)
    kp = us_k * (1.0 + (a - 1.0) * k_a)
    return us_r, lw, kp, us_v, -kkn, kkn * a


def _shifted(u, prev_row, mu, first):
    n = u.shape[0]
    rolled = pltpu.roll(u, 1, 0)
    row = lax.broadcasted_iota(jnp.int32, u.shape, 0)
    p0 = jnp.where(first, jnp.zeros_like(prev_row), prev_row)
    prev = jnp.where(row == 0, jnp.broadcast_to(p0, u.shape), rolled)
    return u + (prev - u) * mu, prev


def _rwkv_specs(cfg, tr):
    RW, LP = cfg.RW, cfg.LP
    base = cfg.o_rwkv // RW
    cols = [(RW, base), (RW, base + 1), (RW, base + 2), (RW, base + 3), (LP, cfg.o_wd // LP), (LP, cfg.o_ad // LP)]
    cur = [pl.BlockSpec((tr, w), (lambda i, cb=cb: (i, cb))) for w, cb in cols]
    prv = [pl.BlockSpec((8, w), (lambda i, cb=cb: (jnp.maximum(i * (tr // 8) - 1, 0), cb))) for w, cb in cols]
    return cols, cur, prv


def _mu_pieces(cfg, mu_ref):
    RW, LP = cfg.RW, cfg.LP
    offs = [0, RW, 2 * RW, 3 * RW, 4 * RW, 4 * RW + LP, 4 * RW + 2 * LP]
    return [mu_ref[:, offs[j]:offs[j + 1]] for j in range(6)]


def _rwkv_prep_fwd(cfg, u, mu, w0, w2p, a0, a2p, k_k, k_a):
    T, RW, LP, tr = cfg.T, cfg.RW, cfg.LP, cfg.tr
    ind, ind_t, pad = _head_indicators(cfg)
    cols, cur, prv = _rwkv_specs(cfg, tr)

    def body(*refs):
        u_refs, p_refs = refs[0:6], refs[6:12]
        mu_ref, w0_ref, w2_ref, a0_ref, a2_ref, kk_ref, ka_ref, ind_ref, indt_ref, pad_ref = refs[12:22]
        outs = refs[22:]
        first = pl.program_id(0) == 0
        mus = _mu_pieces(cfg, mu_ref)
        us = [_shifted(u_refs[j][...], p_refs[j][7:8, :], mus[j], first)[0] for j in range(6)]
        res = _prep_fn(us[0], us[1], us[2], us[4], us[5], w0_ref[...], w2_ref[...], a0_ref[...], a2_ref[...],
                       kk_ref[...], ka_ref[...], ind_ref[...], indt_ref[...], pad_ref[...])
        for j in range(6):
            outs[j][...] = res[j]
        outs[6][...] = us[3]

    consts = [mu, w0, w2p, a0, a2p, k_k, k_a, ind, ind_t, pad]
    return _pcall(body, name="rwkv_prep_fwd", grid=(T // tr,),
                  in_specs=cur + prv + [_const(c.shape) for c in consts],
                  out_specs=[_tile(tr, RW)] * 7, out_shape=[jax.ShapeDtypeStruct((T, RW), F32)] * 7,
                  compiler_params=_cparams(("parallel",)))(*([u] * 12), *consts)


def _rwkv_prep_bwd(cfg, u, mu, w0, w2p, a0, a2p, k_k, k_a, cots, dzb):
    T, RW, LP = cfg.T, cfg.RW, cfg.LP
    tr = min(128, T)
    ind, ind_t, pad = _head_indicators(cfg)
    cols, cur, prv = _rwkv_specs(cfg, tr)
    rseg = cfg.rseg

    def body(*refs):
        u_refs, p_refs = refs[0:6], refs[6:12]
        mu_ref, w0_ref, w2_ref, a0_ref, a2_ref, kk_ref, ka_ref, ind_ref, indt_ref, pad_ref = refs[12:22]
        cot_refs, dzb_ref = refs[22:28], refs[28]
        dus_ref, dmu_ref, dw0_ref, dw2_ref, da0_ref, da2_ref, dkk_ref, dka_ref = refs[29:]
        i = pl.program_id(0)
        first = i == 0
        mus = _mu_pieces(cfg, mu_ref)
        sh = [_shifted(u_refs[j][...], p_refs[j][7:8, :], mus[j], first) for j in range(6)]
        us = [s[0] for s in sh]
        fn = functools.partial(_prep_fn, ind=ind_ref[...], ind_t=indt_ref[...], pad=pad_ref[...])
        _, vjp = jax.vjp(fn, us[0], us[1], us[2], us[4], us[5], w0_ref[...], w2_ref[...], a0_ref[...], a2_ref[...],
                         kk_ref[...], ka_ref[...])
        d = vjp(tuple(c[...] for c in cot_refs))
        dus = [d[0], d[1], d[2], dzb_ref[...], d[3], d[4]]
        offs = [0, RW, 2 * RW, 3 * RW, 4 * RW, 4 * RW + LP, 4 * RW + 2 * LP]
        for j in range(6):
            dus_ref[:, offs[j]:offs[j + 1]] = dus[j]
            dmu_j = jnp.sum(dus[j] * (sh[j][1] - u_refs[j][...]), axis=0, keepdims=True)

            @pl.when(first)
            def _(j=j, dmu_j=dmu_j):
                dmu_ref[:, offs[j]:offs[j + 1]] = dmu_j

            @pl.when(i > 0)
            def _(j=j, dmu_j=dmu_j):
                dmu_ref[:, offs[j]:offs[j + 1]] += dmu_j
        for ref, val in zip((dw0_ref, dw2_ref, da0_ref, da2_ref, dkk_ref, dka_ref), d[5:11]):
            _acc_store(i, ref, val)

    consts = [mu, w0, w2p, a0, a2p, k_k, k_a, ind, ind_t, pad]
    vec = jax.ShapeDtypeStruct((1, RW), F32)
    mat = jax.ShapeDtypeStruct((LP, RW), F32)
    return _pcall(body, name="rwkv_prep_bwd", grid=(T // tr,),
                  in_specs=cur + prv + [_const(c.shape) for c in consts] + [_tile(tr, RW)] * 7,
                  out_specs=[_tile(tr, rseg), _const((1, rseg)), _const((1, RW)), _const((LP, RW)), _const((1, RW)),
                             _const((LP, RW)), _const((1, RW)), _const((1, RW))],
                  out_shape=[jax.ShapeDtypeStruct((T, rseg), F32), jax.ShapeDtypeStruct((1, rseg), F32),
                             vec, mat, vec, mat, vec, vec],
                  compiler_params=_cparams(("arbitrary",)))(*([u] * 12), *consts, *cots, dzb)


def _shift_bwd(cfg, dus, mu, df, du):
    T, tr, RW, LP = cfg.T, cfg.tr, cfg.RW, cfg.LP
    nb = T // tr
    tail = cfg.ncol - cfg.o_f
    assert cfg.o_rwkv % (4 * RW) == 0 and (4 * RW) % (2 * LP) == 0 and cfg.o_f % tail == 0

    def shifted(d_ref, n_ref, mu_ref):
        d = d_ref[...]
        rolled = pltpu.roll(d, tr - 1, 0)
        row = lax.broadcasted_iota(jnp.int32, d.shape, 0)
        n0 = jnp.where(pl.program_id(0) == nb - 1, jnp.zeros_like(n_ref[0:1, :]), n_ref[0:1, :])
        nxt = jnp.where(row == tr - 1, jnp.broadcast_to(n0, d.shape), rolled)
        mu_v = mu_ref[...]
        return (d * (1.0 - mu_v) + nxt * mu_v).astype(BF16)

    def main_body(d_ref, n_ref, mu_ref, du_in, du_ref):
        du_ref[...] = shifted(d_ref, n_ref, mu_ref)

    def tail_body(d_ref, n_ref, mu_ref, df_ref, du_in, du_ref):
        du_ref[:, 0:LANES] = df_ref[...]
        du_ref[:, LANES:LANES + 2 * LP] = shifted(d_ref, n_ref, mu_ref)
        if tail > LANES + 2 * LP:
            du_ref[:, LANES + 2 * LP:] = jnp.zeros((tr, tail - LANES - 2 * LP), BF16)

    def specs(w, cb):
        return [_tile(tr, w, cb),
                pl.BlockSpec((8, w), lambda i: (jnp.minimum((i + 1) * (tr // 8), T // 8 - 1), cb)),
                pl.BlockSpec((1, w), lambda i: (0, cb))]

    out = jax.ShapeDtypeStruct(du.shape, BF16)
    du = _pcall(main_body, name="shift_bwd_main", grid=(nb,), in_specs=specs(4 * RW, 0) + [_ANY],
                out_specs=_tile(tr, 4 * RW, cfg.o_rwkv // (4 * RW)), out_shape=out, input_output_aliases={3: 0},
                compiler_params=_cparams(("parallel",)))(dus, dus, mu, du)
    return _pcall(tail_body, name="shift_bwd_tail", grid=(nb,),
                  in_specs=specs(2 * LP, 4 * RW // (2 * LP)) + [_tile(tr, LANES), _ANY],
                  out_specs=_tile(tr, tail, cfg.o_f // tail), out_shape=out, input_output_aliases={4: 0},
                  compiler_params=_cparams(("parallel",)))(dus, dus, mu, df, du)


def _chunk_local(r, lw, k, v, a, b):
    H, C, K = r.shape
    row = lax.broadcasted_iota(jnp.int32, (C, C), 0)
    col = lax.broadcasted_iota(jnp.int32, (C, C), 1)
    incl = jnp.broadcast_to((row >= col).astype(F32)[None], (H, C, C))
    strict = (row > col)[None]
    lower = (row >= col)[None]
    eye = (row == col)[None]
    zero = jnp.zeros((), F32)
    L = _bdot(incl, lw, 2, 1)
    LC = jnp.sum(lw, axis=1, keepdims=True)
    eL = jnp.exp(L)
    eLn = jnp.exp(-L)
    at = a * jnp.exp(L - lw)
    rt = r * eL
    bt = b * eLn
    kt = k * eLn
    eR = jnp.exp(LC - L)
    bh = b * eR
    kh = k * eR
    gram = functools.partial(_bdot, passes=SCAN_PASSES[0])
    inv = functools.partial(_bdot, passes=SCAN_PASSES[1])
    app = functools.partial(_bdot, passes=SCAN_PASSES[2])
    n_ab = jnp.where(strict, gram(at, bt, 2, 2), zero)
    n_ak = jnp.where(strict, gram(at, kt, 2, 2), zero)
    m_rb = jnp.where(lower, gram(rt, bt, 2, 2), zero)
    m_rk = jnp.where(lower, gram(rt, kt, 2, 2), zero)
    M = n_ab
    P = jnp.where(eye, 1.0, zero) + n_ab
    for _ in range(1, max(1, int(np.ceil(np.log2(C))))):
        M = inv(M, M, 2, 1)
        P = P + inv(M, P, 2, 1)
    W = app(P, at, 2, 1)
    Uloc = app(P, app(n_ak, v, 2, 1), 2, 1)
    Q = rt + app(m_rb, W, 2, 1)
    Yloc = app(m_rb, Uloc, 2, 1) + app(m_rk, v, 2, 1)
    A = jnp.where(eye, jnp.exp(LC), zero) + app(W, bh, 1, 1)
    Sloc = app(Uloc, bh, 1, 1) + app(v, kh, 1, 1)
    return Q, Yloc, A, Sloc


def _split_heads(ref, n):
    N = RWKV_HEAD_DIM
    return jnp.stack([ref[:, h * N:(h + 1) * N] for h in range(n)], axis=0)


def _merge_heads(x):
    return jnp.concatenate([x[h] for h in range(x.shape[0])], axis=1)


def _scan_local_specs(cfg):
    N, HB = RWKV_HEAD_DIM, cfg.hb
    grid = (cfg.RH // HB, cfg.T // cfg.C)
    seq = pl.BlockSpec((HB, cfg.C, N), lambda h, j: (h, j, 0))
    mat = pl.BlockSpec((HB, 1, N, N), lambda h, j: (h, j, 0, 0))
    return grid, seq, mat


def _scan_local_fwd(cfg, seqs):
    T, RH, N = cfg.T, cfg.RH, RWKV_HEAD_DIM
    grid, seq, mat = _scan_local_specs(cfg)

    def body(r_ref, lw_ref, k_ref, v_ref, a_ref, b_ref, q_ref, yl_ref, a_out, sl_ref):
        Q, Yloc, A, Sloc = _chunk_local(*[_split_heads(ref, cfg.hb) for ref in (r_ref, lw_ref, k_ref, v_ref, a_ref, b_ref)])
        q_ref[...] = Q
        yl_ref[...] = Yloc
        a_out[:, 0] = A
        sl_ref[:, 0] = Sloc

    tok = pl.BlockSpec((cfg.C, cfg.hb * N), lambda h, j: (j, h))
    sq = jax.ShapeDtypeStruct((RH, T, N), F32)
    mt = jax.ShapeDtypeStruct((RH, T // cfg.C, N, N), F32)
    return _pcall(body, name="rwkv_scan_local_fwd", grid=grid, in_specs=[tok] * 6, out_specs=[seq, seq, mat, mat],
                  out_shape=[sq, sq, mt, mt], compiler_params=_cparams(("parallel", "parallel")))(*seqs)


def _scan_local_bwd(cfg, toks, dq, dy, da, dsl, extra, comm=None):
    T, RW, N = cfg.T, cfg.RW, RWKV_HEAD_DIM
    grid, seq, mat = _scan_local_specs(cfg)
    c_in, c_out, c_scr = comm[:3] if comm else ([], [], [])

    def body(r_ref, lw_ref, k_ref, v_ref, a_ref, b_ref, dq_ref, dy_ref, da_ref, dsl_ref, xr_ref, xk_ref, xv_ref,
             *rest):
        cin, outs = rest[:len(c_in)], rest[len(c_in):len(c_in) + 6]
        cout, scr = rest[len(c_in) + 6:len(c_in) + 6 + len(c_out)], rest[len(c_in) + 6 + len(c_out):]
        _comm_at(comm, 3, grid, cin, cout, scr)
        ins = [_split_heads(ref, cfg.hb) for ref in (r_ref, lw_ref, k_ref, v_ref, a_ref, b_ref)]
        _, vjp = jax.vjp(_chunk_local, *ins)
        d = vjp((dq_ref[...], _split_heads(dy_ref, cfg.hb), da_ref[:, 0], dsl_ref[:, 0]))
        add = {0: xr_ref, 2: xk_ref, 3: xv_ref}
        for j in range(6):
            dj = _merge_heads(d[j])
            outs[j][...] = dj + add[j][...] if j in add else dj
        _comm_at(comm, 4, grid, cin, cout, scr)

    tok = pl.BlockSpec((cfg.C, cfg.hb * N), lambda h, j: (j, h))
    return _pcall(body, name="rwkv_scan_local_bwd", grid=grid,
                  in_specs=[tok] * 6 + [seq, tok, mat, mat] + [tok] * 3 + [_ANY] * len(c_in),
                  out_specs=[tok] * 6 + [_ANY] * len(c_out),
                  out_shape=[jax.ShapeDtypeStruct((T, RW), F32)] * 6 + list(c_out), scratch_shapes=list(c_scr),
                  compiler_params=_cparams(("arbitrary", "arbitrary") if comm else ("parallel", "parallel")),
                  )(*toks, dq, dy, da, dsl, *extra, *c_in)


def _scan_carry_specs(cfg, rev):
    N, RH, C, nc = RWKV_HEAD_DIM, cfg.RH, cfg.C, cfg.T // cfg.C
    at = (lambda j: nc - 1 - j) if rev else (lambda j: j)
    seq = pl.BlockSpec((RH, C, N), lambda j: (0, at(j), 0))
    mat = pl.BlockSpec((RH, 1, N, N), lambda j: (0, at(j), 0, 0))
    return nc, seq, mat


def _scan_carry_fwd(cfg, q, yloc, a, sloc):
    T, RH, N = cfg.T, cfg.RH, RWKV_HEAD_DIM
    nc, seq, mat = _scan_carry_specs(cfg, False)

    def body(q_ref, yl_ref, a_ref, sl_ref, y_ref, ck_ref, s_ref):
        @pl.when(pl.program_id(0) == 0)
        def _():
            s_ref[...] = jnp.zeros_like(s_ref)

        S = s_ref[...]
        ck_ref[:, 0] = S
        y_ref[...] = _merge_heads(_bdot(q_ref[...], S, 2, 2, SCAN_PASSES[2]) + yl_ref[...])
        s_ref[...] = _bdot(S, a_ref[:, 0], 2, 1) + sl_ref[:, 0]

    tok = pl.BlockSpec((cfg.C, cfg.RW), lambda j: (j, 0))
    return _pcall(body, name="rwkv_scan_carry_fwd", grid=(nc,), in_specs=[seq, seq, mat, mat], out_specs=[tok, mat],
                  out_shape=[jax.ShapeDtypeStruct((T, cfg.RW), F32), jax.ShapeDtypeStruct((RH, nc, N, N), F32)],
                  scratch_shapes=[pltpu.VMEM((RH, N, N), F32)],
                  compiler_params=_cparams(("arbitrary",)))(q, yloc, a, sloc)


def _scan_carry_bwd(cfg, q, a, ckpt, dy):
    T, RH, N = cfg.T, cfg.RH, RWKV_HEAD_DIM
    nc, seq, mat = _scan_carry_specs(cfg, True)

    def body(q_ref, a_ref, ck_ref, dy_ref, dq_ref, da_ref, dsl_ref, ds_ref):
        @pl.when(pl.program_id(0) == 0)
        def _():
            ds_ref[...] = jnp.zeros_like(ds_ref)

        S, dS, dY = ck_ref[:, 0], ds_ref[...], _split_heads(dy_ref, RH)
        dq_ref[...] = _bdot(dY, S, 2, 1, SCAN_PASSES[2])
        da_ref[:, 0] = _bdot(S, dS, 1, 1, SCAN_PASSES[2])
        dsl_ref[:, 0] = dS
        ds_ref[...] = _bdot(dS, a_ref[:, 0], 2, 2) + _bdot(dY, q_ref[...], 1, 1, SCAN_PASSES[2])

    mt = jax.ShapeDtypeStruct((RH, nc, N, N), F32)
    tok = pl.BlockSpec((cfg.C, cfg.RW), lambda j: (nc - 1 - j, 0))
    return _pcall(body, name="rwkv_scan_carry_bwd", grid=(nc,), in_specs=[seq, mat, mat, tok],
                  out_specs=[seq, mat, mat], out_shape=[jax.ShapeDtypeStruct((RH, T, N), F32), mt, mt],
                  scratch_shapes=[pltpu.VMEM((RH, N, N), F32)],
                  compiler_params=_cparams(("arbitrary",)))(q, a, ckpt, dy)


def _post_fn(y, r, kp, v, zb, ln_w, ln_b, rk, ind, ind_t):
    n = float(RWKV_HEAD_DIM)
    mu = _xdot(_xdot(y, ind, ind_t) / n, ind_t, ind)
    yc = y - mu
    var = _xdot(yc * yc, ind, ind_t) / n
    rstd = _xdot(lax.rsqrt(var + GN_EPS), ind_t, ind)
    yn = yc * rstd * ln_w + ln_b
    bonus = _xdot(_xdot(r * kp * rk, ind, ind_t), ind_t, ind) * v
    return (yn + bonus) * _silu(zb)


def _rwkv_post_fwd(cfg, y, r, kp, v, zb, ln_w, ln_b, rk):
    T, RW, tr = cfg.T, cfg.RW, cfg.tr
    ind, ind_t, _ = _head_indicators(cfg)

    def body(y_ref, r_ref, k_ref, v_ref, z_ref, lw_ref, lb_ref, rk_ref, ind_ref, indt_ref, ob_ref):
        ob_ref[...] = _post_fn(y_ref[...], r_ref[...], k_ref[...], v_ref[...], z_ref[...], lw_ref[...], lb_ref[...],
                               rk_ref[...], ind_ref[...], indt_ref[...]).astype(BF16)

    consts = [ln_w, ln_b, rk, ind, ind_t]
    return _pcall(body, name="rwkv_post_fwd", grid=(T // tr,),
                  in_specs=[_tile(tr, RW)] * 5 + [_const(c.shape) for c in consts],
                  out_specs=_tile(tr, RW), out_shape=jax.ShapeDtypeStruct((T, RW), BF16),
                  compiler_params=_cparams(("parallel",)))(y, r, kp, v, zb, *consts)


def _rwkv_post_bwd(cfg, y, r, kp, v, zb, ln_w, ln_b, rk, dob):
    T, RW = cfg.T, cfg.RW
    tr = min(128, T)
    ind, ind_t, _ = _head_indicators(cfg)

    def body(y_ref, r_ref, k_ref, v_ref, z_ref, lw_ref, lb_ref, rk_ref, ind_ref, indt_ref, dob_ref,
             dy_ref, dr_ref, dk_ref, dv_ref, dz_ref, dlw_ref, dlb_ref, drk_ref):
        fn = functools.partial(_post_fn, ind=ind_ref[...], ind_t=indt_ref[...])
        _, vjp = jax.vjp(fn, y_ref[...], r_ref[...], k_ref[...], v_ref[...], z_ref[...], lw_ref[...], lb_ref[...],
                         rk_ref[...])
        d = vjp(dob_ref[...])
        for ref, val in zip((dy_ref, dr_ref, dk_ref, dv_ref, dz_ref), d[:5]):
            ref[...] = val
        i = pl.program_id(0)
        for ref, val in zip((dlw_ref, dlb_ref, drk_ref), d[5:8]):
            _acc_store(i, ref, val)

    consts = [ln_w, ln_b, rk, ind, ind_t]
    vec = jax.ShapeDtypeStruct((1, RW), F32)
    return _pcall(body, name="rwkv_post_bwd", grid=(T // tr,),
                  in_specs=[_tile(tr, RW)] * 5 + [_const(c.shape) for c in consts] + [_tile(tr, RW)],
                  out_specs=[_tile(tr, RW)] * 5 + [_const((1, RW))] * 3,
                  out_shape=[jax.ShapeDtypeStruct((T, RW), F32)] * 5 + [vec] * 3,
                  compiler_params=_cparams(("arbitrary",)))(y, r, kp, v, zb, *consts, dob)


def _adamw_math(w, g, m, v):
    m = ADAM_B1 * m + (1.0 - ADAM_B1) * g
    v = ADAM_B2 * v + (1.0 - ADAM_B2) * (g * g)
    m_hat = m / (1.0 - ADAM_B1 ** ADAM_STEP)
    v_hat = v / (1.0 - ADAM_B2 ** ADAM_STEP)
    delta = -ADAM_LR * (m_hat / (jnp.sqrt(v_hat) + ADAM_EPS) + ADAM_WD * w)
    return delta, m, v


def _adamw(name, w, g, m, v, copy_grad=False):
    R, Cc = w.shape
    Rp = -(-R // 8) * 8
    tr = Rp
    for nb in range(1, Rp // 8 + 1):
        if (Rp // 8) % nb == 0 and (Rp // nb) * Cc * 4 <= 2 * 1024 * 1024:
            tr = Rp // nb
            break

    def body(w_ref, g_ref, m_ref, v_ref, d_ref, nm_ref, nv_ref, *g_out):
        g_v = g_ref[...]
        d, nm, nv = _adamw_math(w_ref[...], g_v, m_ref[...], v_ref[...])
        d_ref[...] = d
        nm_ref[...] = nm
        nv_ref[...] = nv
        if copy_grad:
            g_out[0][...] = g_v

    spec = _tile(tr, Cc)
    n_out = 4 if copy_grad else 3
    return _pcall(body, name=name, grid=(Rp // tr,), in_specs=[spec] * 4, out_specs=[spec] * n_out,
                  out_shape=[jax.ShapeDtypeStruct((R, Cc), F32)] * n_out,
                  compiler_params=_cparams(("parallel",)))(w, g, m, v)


def _row_tile(R, Cc, itemsize, budget=2 * 1024 * 1024):
    for nb in range(1, R // 16 + 1):
        if R % nb == 0 and (R // nb) % 16 == 0 and (R // nb) * Cc * itemsize <= budget:
            return R // nb
    return R


def _add_halves(name, gs, r1, c_idx):
    S, R, Cc = gs.shape
    half = R // 2
    tr = _row_tile(half, Cc, 4)
    nb = half // tr

    def body(c_ref, g_ref, r_ref, o_ref):
        o_ref[...] = (g_ref[...].astype(F32) + r_ref[...].astype(F32)).astype(BF16)

    grid_spec = pltpu.PrefetchScalarGridSpec(
        num_scalar_prefetch=1, grid=(S, nb),
        in_specs=[pl.BlockSpec((1, tr, Cc), lambda s, i, c: (s, c[0] * nb + i, 0)),
                  pl.BlockSpec((1, tr, Cc), lambda s, i, c: (s, i, 0))],
        out_specs=pl.BlockSpec((1, tr, Cc), lambda s, i, c: (s, i, 0)))
    return _pcall(body, name=name, grid_spec=grid_spec, out_shape=jax.ShapeDtypeStruct((S, half, Cc), BF16),
                  compiler_params=_cparams(("parallel", "parallel")))(c_idx, gs, r1)


def _sum_slots(name, r2):
    S, R, Cc = r2.shape
    tr = _row_tile(R, Cc, 4 * S // 2 if r2.dtype == BF16 else 4 * S)

    def body(r_ref, o_ref):
        acc = r_ref[0].astype(F32)
        for s in range(1, S):
            acc = acc + r_ref[s].astype(F32)
        o_ref[...] = acc

    return _pcall(body, name=name, grid=(R // tr,), in_specs=[pl.BlockSpec((S, tr, Cc), lambda i: (0, i, 0))],
                  out_specs=_tile(tr, Cc), out_shape=jax.ShapeDtypeStruct((R, Cc), F32),
                  compiler_params=_cparams(("parallel",)))(r2)


def _sum_chips(name, recv, own, place):
    S, H, Cc = recv.shape
    tr = _row_tile(H, Cc, 4, 1024 * 1024)
    nb = H // tr

    def body(p_ref, r_ref, own_ref, o_ref):
        s = pl.program_id(1)
        me = p_ref[0]

        @pl.when(s == 0)
        def _():
            o_ref[...] = jnp.zeros_like(o_ref)

        @pl.when(s == me)
        def _():
            o_ref[...] += own_ref[0].astype(F32)

        @pl.when(s != me)
        def _():
            o_ref[...] += r_ref[0].astype(F32)

    grid_spec = pltpu.PrefetchScalarGridSpec(
        num_scalar_prefetch=1, grid=(nb, S),
        in_specs=[pl.BlockSpec((1, tr, Cc), lambda i, s, p: (jnp.where(s == p[0], (s + 1) % S, s), i, 0)),
                  pl.BlockSpec((1, tr, Cc), lambda i, s, p: (p[0], i, 0))],
        out_specs=pl.BlockSpec((tr, Cc), lambda i, s, p: (p[1] * nb + i, 0)))
    return _pcall(body, name=name, grid_spec=grid_spec, out_shape=jax.ShapeDtypeStruct((2 * H, Cc), F32),
                  compiler_params=_cparams(("parallel", "arbitrary")))(place, recv, own)


def _cast_bf16(name, w):
    R, Cc = w.shape
    tr = _row_tile(R, Cc, 4)

    def body(w_ref, o_ref):
        o_ref[...] = w_ref[...].astype(BF16)

    return _pcall(body, name=name, grid=(R // tr,), in_specs=[_tile(tr, Cc)], out_specs=_tile(tr, Cc),
                  out_shape=jax.ShapeDtypeStruct((R, Cc), BF16), compiler_params=_cparams(("parallel",)))(w)


_ANY = pl.BlockSpec(memory_space=pl.ANY)


def _place():
    x, y, c = lax.axis_index("x"), lax.axis_index("y"), lax.axis_index("c")
    others = [(1 - x, y), (x, 1 - y), (1 - x, 1 - y)]
    return x, y, c, others


def _gather_weights(shards):
    arrays, out_shapes, scratch, start, finish, middle = _gather_parts(shards)
    n = len(shards)

    def body(*refs):
        ins, outs, sems = refs[:n], refs[n:2 * n], refs[2 * n:]
        start(ins, outs, sems)
        middle(ins, outs, sems)
        finish(ins, outs, sems)

    return _pcall(body, name="gather_weights", in_specs=[_ANY] * n, out_specs=[_ANY] * n, out_shape=out_shapes,
                  scratch_shapes=scratch)(*arrays)


def _gather_parts(shards):
    n = len(shards)
    halves = [s.shape[0] // 2 for s in shards]

    def parts(ins, outs, sems):
        x, y, c, _ = _place()
        me = 2 * x + y
        n1 = (x ^ (1 - c), y ^ c)
        n2 = (x ^ c, y ^ (1 - c))
        s1, s2, sd = 2 * n1[0] + n1[1], 2 * n2[0] + n2[1], 2 * (1 - x) + (1 - y)
        sib = (x, y, 1 - c)

        def rows(k, chip, hc):
            return outs[k].at[chip, pl.ds(hc * halves[k], halves[k]), :]

        def remote(k, j, src, dst, to):
            return pltpu.make_async_remote_copy(src_ref=src, dst_ref=dst, send_sem=sems[0].at[6 * k + j],
                                                recv_sem=sems[1].at[6 * k + j], device_id=to, device_id_type=MESH)

        def copy(k, j):
            if j < 2:
                mine = ins[k].at[pl.ds(c * halves[k], halves[k]), :]
                return remote(k, j, mine, rows(k, me, c), (*(n1 if j == 0 else n2), c))
            land = rows(k, {2: s1, 3: s1, 4: s2, 5: sd}[j], c)
            return remote(k, j, land, land, (*n2, c) if j == 2 else sib)

        def arrived(k, j):
            hc = c if j < 3 else 1 - c
            land = rows(k, {0: s1, 1: s2, 2: sd, 3: s2, 4: s1, 5: sd}[j], hc)
            remote(k, j, land, land, (x, y, c)).wait_recv()

        return copy, arrived

    def start(ins, outs, sems):
        copy, _ = parts(ins, outs, sems)
        for k in range(n):
            copy(k, 0).start()
            copy(k, 1).start()

    def middle(ins, outs, sems):
        copy, arrived = parts(ins, outs, sems)
        for k in range(n):
            arrived(k, 0)
            copy(k, 2).start()
            copy(k, 3).start()
            arrived(k, 1)
            copy(k, 4).start()

    def finish(ins, outs, sems):
        copy, arrived = parts(ins, outs, sems)
        for k in range(n):
            arrived(k, 2)
            copy(k, 5).start()
        for k in range(n):
            for j in (3, 4, 5):
                arrived(k, j)
        for k in range(n):
            for j in range(6):
                copy(k, j).wait_send()

    out_shapes = [jax.ShapeDtypeStruct((N_CHIPS,) + s.shape, s.dtype) for s in shards]
    scratch = [pltpu.SemaphoreType.DMA((6 * n,)), pltpu.SemaphoreType.DMA((6 * n,))]
    return list(shards), out_shapes, scratch, start, finish, middle


def _exchange_halves(name, grads):
    n = len(grads)
    halves = [g.shape[1] // 2 for g in grads]

    def body(*refs):
        ins, outs = refs[:n], refs[n:2 * n]
        send_sems, recv_sems = refs[2 * n:]
        x, y, c, _ = _place()
        cps = []
        for k in range(n):
            src = ins[k].at[:, pl.ds((1 - c) * halves[k], halves[k]), :]
            cp = pltpu.make_async_remote_copy(src_ref=src, dst_ref=outs[k], send_sem=send_sems.at[k],
                                              recv_sem=recv_sems.at[k], device_id=(x, y, 1 - c), device_id_type=MESH)
            cp.start()
            cps.append(cp)
        for cp in cps:
            cp.wait()

    return _pcall(
        body, name=name, in_specs=[_ANY] * n, out_specs=[_ANY] * n,
        out_shape=[jax.ShapeDtypeStruct((g.shape[0], h) + g.shape[2:], g.dtype) for g, h in zip(grads, halves)],
        scratch_shapes=[pltpu.SemaphoreType.DMA((n,)), pltpu.SemaphoreType.DMA((n,))],
    )(*grads)


def _scatter_to_owners(chip_sums):
    n = len(chip_sums)

    def sends(ins, outs, sems):
        x, y, c, others = _place()
        me = 2 * x + y
        return [pltpu.make_async_remote_copy(
            src_ref=ins[k].at[2 * px + py], dst_ref=outs[k].at[me], send_sem=sems[0].at[3 * k + j],
            recv_sem=sems[1].at[3 * k + j], device_id=(px, py, c), device_id_type=MESH)
            for k in range(n) for j, (px, py) in enumerate(others)]

    def start(ins, outs, sems):
        for cp in sends(ins, outs, sems):
            cp.start()

    def finish(ins, outs, sems):
        x, y, c, others = _place()
        for k in range(n):
            for j, (px, py) in enumerate(others):
                land = outs[k].at[2 * px + py]
                pltpu.make_async_remote_copy(src_ref=land, dst_ref=land, send_sem=sems[0].at[3 * k + j],
                                             recv_sem=sems[1].at[3 * k + j], device_id=(x, y, c),
                                             device_id_type=MESH).wait_recv()
        for cp in sends(ins, outs, sems):
            cp.wait_send()

    out_shapes = [jax.ShapeDtypeStruct(g.shape, g.dtype) for g in chip_sums]
    scratch = [pltpu.SemaphoreType.DMA((3 * n,)), pltpu.SemaphoreType.DMA((3 * n,))]
    return list(chip_sums), out_shapes, scratch, start, finish


def _swap_with_sibling(arrays):
    n = len(arrays)

    def copies(ins, outs, sems):
        x, y, c, _ = _place()
        return [pltpu.make_async_remote_copy(src_ref=ins[k], dst_ref=outs[k], send_sem=sems[0].at[k],
                                             recv_sem=sems[1].at[k], device_id=(x, y, 1 - c), device_id_type=MESH)
                for k in range(n)]

    def start(ins, outs, sems):
        for cp in copies(ins, outs, sems):
            cp.start()

    def finish(ins, outs, sems):
        for cp in copies(ins, outs, sems):
            cp.wait()

    out_shapes = [jax.ShapeDtypeStruct(a.shape, a.dtype) for a in arrays]
    scratch = [pltpu.SemaphoreType.DMA((n,)), pltpu.SemaphoreType.DMA((n,))]
    return list(arrays), out_shapes, scratch, start, finish


def _add_pair(name, a, b):
    R, Cc = a.shape
    tr = _row_tile(R, Cc, 4)

    def body(a_ref, b_ref, o_ref):
        o_ref[...] = (a_ref[...].astype(F32) + b_ref[...].astype(F32)).astype(BF16)

    return _pcall(body, name=name, grid=(R // tr,), in_specs=[_tile(tr, Cc)] * 2, out_specs=_tile(tr, Cc),
                  out_shape=jax.ShapeDtypeStruct((R, Cc), BF16), compiler_params=_cparams(("parallel",)))(a, b)


def _second_neighbour():
    x, y, c, _ = _place()
    return (x, y, c), (x ^ c, y ^ (1 - c)), (x ^ (1 - c), y ^ c)


def _scatter_stage1(chip_sums):
    n = len(chip_sums)

    def copies(ins, outs, sems):
        (x, y, c), n2, n1 = _second_neighbour()
        diag = 2 * (1 - x) + (1 - y)
        return [pltpu.make_async_remote_copy(
            src_ref=ins[k].at[slot], dst_ref=outs[2 * k + j], send_sem=sems[0].at[2 * k + j],
            recv_sem=sems[1].at[2 * k + j], device_id=(*n2, c), device_id_type=MESH)
            for k in range(n) for j, slot in enumerate((2 * n2[0] + n2[1], diag))]

    def start(ins, outs, sems):
        for cp in copies(ins, outs, sems):
            cp.start()

    def finish(ins, outs, sems):
        for cp in copies(ins, outs, sems):
            cp.wait()

    out_shapes = [jax.ShapeDtypeStruct(g.shape[1:], g.dtype) for g in chip_sums for _ in range(2)]
    scratch = [pltpu.SemaphoreType.DMA((2 * n,)), pltpu.SemaphoreType.DMA((2 * n,))]
    return list(chip_sums), out_shapes, scratch, start, finish


def _scatter_stage2(passed):
    n = len(passed)

    def copies(ins, outs, sems):
        (x, y, c), n2, n1 = _second_neighbour()
        return [pltpu.make_async_remote_copy(src_ref=ins[k], dst_ref=outs[k], send_sem=sems[0].at[k],
                                             recv_sem=sems[1].at[k], device_id=(*n1, c), device_id_type=MESH)
                for k in range(n)]

    def start(ins, outs, sems):
        for cp in copies(ins, outs, sems):
            cp.start()

    def finish(ins, outs, sems):
        for cp in copies(ins, outs, sems):
            cp.wait()

    out_shapes = [jax.ShapeDtypeStruct(p.shape, p.dtype) for p in passed]
    scratch = [pltpu.SemaphoreType.DMA((n,)), pltpu.SemaphoreType.DMA((n,))]
    return list(passed), out_shapes, scratch, start, finish


def _add_passed(name, own, got, slot):
    _, H, Cc = own.shape
    tr = _row_tile(H, Cc, 4)

    def body(s_ref, o_ref, g_ref, out_ref):
        out_ref[...] = (o_ref[0].astype(F32) + g_ref[...].astype(F32)).astype(BF16)

    grid_spec = pltpu.PrefetchScalarGridSpec(
        num_scalar_prefetch=1, grid=(H // tr,),
        in_specs=[pl.BlockSpec((1, tr, Cc), lambda i, s: (s[0], i, 0)), pl.BlockSpec((tr, Cc), lambda i, s: (i, 0))],
        out_specs=pl.BlockSpec((tr, Cc), lambda i, s: (i, 0)))
    return _pcall(body, name=name, grid_spec=grid_spec, out_shape=jax.ShapeDtypeStruct((H, Cc), BF16),
                  compiler_params=_cparams(("parallel",)))(slot, own, got)


def _sum_stages(name, own, direct, via, place, transposed=False):
    _, H, Cc = own.shape
    tr = LANES if transposed else _row_tile(H, Cc, 4, 1024 * 1024)
    nb = H // tr

    def body(p_ref, own_ref, d_ref, v_ref, o_ref):
        acc = (own_ref[0].astype(F32) + d_ref[...].astype(F32)) + v_ref[...].astype(F32)
        o_ref[...] = acc.T if transposed else acc

    flat = pl.BlockSpec((tr, Cc), lambda i, p: (i, 0))
    out_spec = (pl.BlockSpec((Cc, tr), lambda i, p: (0, p[1] * nb + i)) if transposed
                else pl.BlockSpec((tr, Cc), lambda i, p: (p[1] * nb + i, 0)))
    grid_spec = pltpu.PrefetchScalarGridSpec(
        num_scalar_prefetch=1, grid=(nb,),
        in_specs=[pl.BlockSpec((1, tr, Cc), lambda i, p: (p[0], i, 0)), flat, flat], out_specs=out_spec)
    return _pcall(body, name=name, grid_spec=grid_spec,
                  out_shape=jax.ShapeDtypeStruct((Cc, 2 * H) if transposed else (2 * H, Cc), F32),
                  compiler_params=_cparams(("parallel",)))(place, own, direct, via)


def _join_halves(fulls, axes, small):
    n = len(fulls)
    hs = [f.shape[ax] // 2 for f, ax in zip(fulls, axes)]
    rel = [(dx, dy, dc) for dx in (0, 1) for dy in (0, 1) for dc in (0, 1)][1:]

    def half(ref, k, hc):
        part = pl.ds(hc * hs[k], hs[k])
        return ref.at[:, part] if axes[k] else ref.at[part, :]

    def body(*refs):
        ins, small_in = refs[:n], refs[n]
        outs, small_out = refs[n + 1:2 * n + 1], refs[2 * n + 1]
        send_sems, recv_sems, ssend, srecv, local_sem = refs[2 * n + 2:]
        x, y, c, _ = _place()
        dev = 4 * x + 2 * y + c
        local = pltpu.make_async_copy(small_in, small_out.at[dev], local_sem)
        local.start()
        cps = []
        for k in range(n):
            cp = pltpu.make_async_remote_copy(src_ref=half(ins[k], k, c), dst_ref=half(outs[k], k, c),
                                              send_sem=send_sems.at[k], recv_sem=recv_sems.at[k],
                                              device_id=(x, y, 1 - c), device_id_type=MESH)
            cp.start()
            cps.append(cp)
        for r, (dx, dy, dc) in enumerate(rel):
            cp = pltpu.make_async_remote_copy(src_ref=small_in, dst_ref=small_out.at[dev], send_sem=ssend.at[r],
                                              recv_sem=srecv.at[r], device_id=(x ^ dx, y ^ dy, c ^ dc),
                                              device_id_type=MESH)
            cp.start()
            cps.append(cp)
        for k in range(n):
            land = half(outs[k], k, 1 - c)
            pltpu.make_async_remote_copy(src_ref=land, dst_ref=land, send_sem=send_sems.at[k],
                                         recv_sem=recv_sems.at[k], device_id=(x, y, c), device_id_type=MESH).wait_recv()
        for r, (dx, dy, dc) in enumerate(rel):
            land = small_out.at[4 * (x ^ dx) + 2 * (y ^ dy) + (c ^ dc)]
            pltpu.make_async_remote_copy(src_ref=land, dst_ref=land, send_sem=ssend.at[r], recv_sem=srecv.at[r],
                                         device_id=(x, y, c), device_id_type=MESH).wait_recv()
        for cp in cps:
            cp.wait_send()
        local.wait()

    return _pcall(
        body, name="join_halves", in_specs=[_ANY] * (n + 1), out_specs=[_ANY] * (n + 1),
        out_shape=[jax.ShapeDtypeStruct(f.shape, f.dtype) for f in fulls]
        + [jax.ShapeDtypeStruct((N_DEV,) + small.shape, small.dtype)],
        input_output_aliases={k: k for k in range(n)},
        scratch_shapes=[pltpu.SemaphoreType.DMA((n,)), pltpu.SemaphoreType.DMA((n,)), pltpu.SemaphoreType.DMA((7,)),
                        pltpu.SemaphoreType.DMA((7,)), pltpu.SemaphoreType.DMA],
    )(*fulls, small)


def _local_step(cfg, x2, target, norm_gain, w_my, fb, mu_g, w0, a0, k_k, k_a, r_k, ln_w, ln_b, fng, rest,
                exchange=None):
    T, D, FW, FH, RW, RH, LP, lora = cfg.T, cfg.D, cfg.FW, cfg.FH, cfg.RW, cfg.RH, cfg.LP, cfg.lora
    fb_p = jnp.pad(fb, ((0, 0), (0, LANES - FH)))
    mu = _rwkv_vec_to_my(cfg, mu_g)
    rk = r_k.reshape(1, RW)
    tm = min(1024, T)

    h = _rms_fwd(cfg, x2, norm_gain)
    if len(rest) == 2:
        u, *got = _mm("in_proj", h, w_my, "nn", F32, tm, cfg.tn, 2048, comm=rest[0])
        rest = rest[1](got)
    else:
        u = _mm("in_proj", h, w_my, "nn", F32, tm, cfg.tn, 2048)
    w2, a2, wpf, wpr, wout = rest
    w2p = jnp.pad(w2, ((0, LP - lora), (0, 0)))
    a2p = jnp.pad(a2, ((0, LP - lora), (0, 0)))
    c_cols = _fox_prep(cfg, u, fb_p)
    c_rows = c_cols[:, :FH].T.reshape(FH, 1, T)
    o, lse = _attn_fwd(cfg, u, c_rows)
    oa = _gate_a_fwd(cfg, o, u)
    prep = _rwkv_prep_fwd(cfg, u, mu, w0, w2p, a0, a2p, k_k, k_a)
    r, lw, kp, v, an, b, zb = prep
    toks = [r, lw, kp, v, an, b]
    q_s, yloc, a_m, sloc = _scan_local_fwd(cfg, toks)
    y, ckpt = _scan_carry_fwd(cfg, q_s, yloc, a_m, sloc)
    ob = _rwkv_post_fwd(cfg, y, r, kp, v, zb, ln_w, ln_b, rk)
    pa = _mm("proj_fox", oa, wpf, "nn", F32, tm, 1024, 2048)
    pb = _mm("proj_rwkv", ob, wpr, "nn", F32, tm, 1024, 2048)
    m = _merge_fwd(cfg, pa, pb, u)
    mo = _mm("out_proj", m, wout, "nn", F32, tm, 1024, 2048)
    loss8, dres, dres16, d_fng = _final(cfg, x2, mo, fng.reshape(1, D), target)

    dm = _mm("out_proj_dx", dres16, wout, "nt", F32, tm, 1024, 2048)
    d_wout = _mm("out_proj_dw", m, dres16, "tn", BF16, 1024, 1024, 2048)
    dpa, dpb, du = _merge_bwd(cfg, pa, pb, u, dm)
    doa = _mm("proj_fox_dx", dpa, wpf, "nt", F32, tm, 1024, 2048)
    d_wpf = _mm("proj_fox_dw", oa, dpa, "tn", BF16, 1024, 1024, 2048)
    dob = _mm("proj_rwkv_dx", dpb, wpr, "nt", F32, tm, 1024, 2048)
    d_wpr = _mm("proj_rwkv_dw", ob, dpb, "tn", BF16, 1024, 1024, 2048)

    do, du = _gate_a_bwd(cfg, o, u, doa, du)
    du, dcol = _attn_bwd(cfg, u, c_rows, lse, do, du)
    dc = jnp.pad(-dcol.reshape(FH, T).T, ((0, 0), (0, LANES - FH)))
    df, d_fb = _fox_prep_bwd(cfg, u, fb_p, dc)

    dy, dr_p, dk_p, dv_p, dzb, d_lnw, d_lnb, d_rk = _rwkv_post_bwd(cfg, y, r, kp, v, zb, ln_w, ln_b, rk, dob)
    dq_s, da_m, dsl = _scan_carry_bwd(cfg, q_s, a_m, ckpt, dy)
    early = dict(w_proj_fox=d_wpf, w_proj_rwkv=d_wpr, w_out=d_wout)
    res = _scan_local_bwd(cfg, toks, dq_s, dy, da_m, dsl, [dr_p, dk_p, dv_p], exchange(early) if exchange else None)
    cots, received = res[:6], list(res[6:])
    dus, d_mu, d_w0, d_w2p, d_a0, d_a2p, d_kk, d_ka = _rwkv_prep_bwd(cfg, u, mu, w0, w2p, a0, a2p, k_k, k_a, cots, dzb)
    du = _shift_bwd(cfg, dus, mu, df, du)
    if exchange:
        late = dict(w_in=exchange((h, du, d_w2p[:lora], d_a2p[:lora])))
    else:
        late = dict(w_in=_mm("in_proj_dw", h, du, "tn", BF16, 1024, cfg.tn, 2048), rwkv_w2=d_w2p[:lora],
                    rwkv_a2=d_a2p[:lora])
    tkx = 2 * cfg.tn if cfg.ncol % (2 * cfg.tn) == 0 else cfg.tn
    res = _mm("in_proj_dx", du, w_my, "nt", F32, tm, 1024, tkx, comm=exchange(late) if exchange else None)
    dh = res[0] if exchange else res
    big = dict(early, **late)
    res = _rms_bwd(cfg, x2, norm_gain, dh, dres, exchange(list(res[1:])) if exchange else None)
    gx, d_ng = res[:2]
    received += list(res[2:])

    small = dict(norm_gain=d_ng, fox_forget_bias=d_fb[:, :FH], rwkv_shift_mix=_rwkv_vec_from_my(cfg, d_mu),
                 rwkv_w0=d_w0, rwkv_a0=d_a0, rwkv_k_k=d_kk, rwkv_k_a=d_ka, rwkv_r_k=d_rk, rwkv_ln_w=d_lnw,
                 rwkv_ln_b=d_lnb, final_norm_gain=d_fng)
    return loss8[0, 0], gx, small, big, received


_SMALL = ["norm_gain", "fox_forget_bias", "rwkv_shift_mix", "rwkv_w0", "rwkv_a0", "rwkv_k_k", "rwkv_k_a", "rwkv_r_k",
          "rwkv_ln_w", "rwkv_ln_b", "final_norm_gain"]
_WEIGHTS = ["norm_gain", "w_in", "fox_forget_bias", "rwkv_shift_mix", "rwkv_w0", "rwkv_w2", "rwkv_a0", "rwkv_a2",
            "rwkv_k_k", "rwkv_k_a", "rwkv_r_k", "rwkv_ln_w", "rwkv_ln_b", "w_proj_fox", "w_proj_rwkv", "w_out",
            "final_norm_gain"]


def _pack_small(arrs):
    parts = []
    for a in arrs:
        f = a.reshape(-1)
        parts.append(jnp.pad(f, (0, (-f.shape[0]) % LANES)))
    flat = jnp.concatenate(parts)
    rows = flat.shape[0] // LANES
    flat = jnp.pad(flat, (0, ((-rows) % 8) * LANES))
    return flat.reshape(-1, LANES)


def _unpack_small(packed, shapes):
    flat = packed.reshape(-1)
    out, pos = [], 0
    for s in shapes:
        n = int(np.prod(s))
        out.append(flat[pos:pos + n].reshape(s))
        pos += n + ((-n) % LANES)
    return out


def _shard_major(a, axis):
    parts = jnp.split(a, N_CHIPS, axis=axis)
    return jnp.stack(parts, axis=0)


def kernel(x, norm_gain, w_in, fox_forget_bias, rwkv_shift_mix, rwkv_w0, rwkv_w2, rwkv_a0, rwkv_a2, rwkv_k_k, rwkv_k_a, rwkv_r_k, rwkv_ln_w, rwkv_ln_b, w_proj_fox, w_proj_rwkv, w_out, final_norm_gain, loss_target, m_norm_gain, m_w_in, m_fox_forget_bias, m_rwkv_shift_mix, m_rwkv_w0, m_rwkv_w2, m_rwkv_a0, m_rwkv_a2, m_rwkv_k_k, m_rwkv_k_a, m_rwkv_r_k, m_rwkv_ln_w, m_rwkv_ln_b, m_w_proj_fox, m_w_proj_rwkv, m_w_out, m_final_norm_gain, v_norm_gain, v_w_in, v_fox_forget_bias, v_rwkv_shift_mix, v_rwkv_w0, v_rwkv_w2, v_rwkv_a0, v_rwkv_a2, v_rwkv_k_k, v_rwkv_k_a, v_rwkv_r_k, v_rwkv_ln_w, v_rwkv_ln_b, v_w_proj_fox, v_w_proj_rwkv, v_w_out, v_final_norm_gain):
    args = dict(locals())
    T, D = x.shape[1], x.shape[2]
    lora = rwkv_w2.shape[1]
    cfg = _Cfg(T, D, lora)
    RW = cfg.RW
    c_idx = lax.axis_index("c").astype(jnp.int32).reshape(1)
    me_chip = (2 * lax.axis_index("x") + lax.axis_index("y")).astype(jnp.int32)
    place = jnp.concatenate([me_chip.reshape(1), c_idx])

    w_in_s = w_in[0].astype(BF16)
    lora_s = jnp.concatenate([rwkv_w2[0], rwkv_a2[0]], axis=0)
    own_slot = lambda g, own: lax.dynamic_update_slice(g, own[None], (me_chip, 0, 0))
    w_my = _shards_to_my_layout(cfg, own_slot(_gather_weights([w_in_s])[0], w_in_s))
    mine = [_cast_bf16("cast_w_proj_fox", w_proj_fox[0]), _cast_bf16("cast_w_proj_rwkv", w_proj_rwkv[0]),
            _cast_bf16("cast_w_out", w_out[0]), lora_s]

    def unpack(gathered):
        g_wpf, g_wpr, g_out, g_lora = [own_slot(g, own) for g, own in zip(gathered, mine)]
        lo = g_lora.transpose(1, 0, 2).reshape(2 * lora, RW)
        return (lo[:lora], lo[lora:], g_wpf.transpose(1, 0, 2).reshape(RW, D),
                g_wpr.transpose(1, 0, 2).reshape(RW, D), g_out.reshape(D, D))

    early, late = ["w_proj_fox", "w_proj_rwkv", "w_out"], ["w_in", "lora"]
    names = early + late
    chip_sums, direct = {}, {}
    n1_slot = (2 * (lax.axis_index("x") ^ (1 - lax.axis_index("c")))
               + (lax.axis_index("y") ^ lax.axis_index("c"))).astype(jnp.int32).reshape(1)

    def exchange(got):
        if isinstance(got, tuple):
            h, du, d_w2, d_a2 = got
            c, half = lax.axis_index("c"), D // 2
            cols = lambda base: lax.dynamic_slice_in_dim(h, base * half, half, axis=1)
            lora_g = _shard_major(jnp.concatenate([d_w2, d_a2], axis=0).astype(BF16), 1)
            lora_rows = lambda base: lax.dynamic_slice_in_dim(lora_g, base * lora, lora, axis=1).reshape(-1, RW // 4)
            tiles = (BF16, min(1024, half), cfg.tn, 2048)
            sent = _mm("in_proj_dw_sibling", cols(1 - c), du, "tn", *tiles)
            kept, got_w, got_l = _mm("in_proj_dw", cols(c), du, "tn", *tiles,
                                     comm=_swap_with_sibling([sent, lora_rows(1 - c)]))
            return (_add_pair("add_halves_w_in", kept, got_w),
                    _add_pair("add_halves_lora", lora_rows(c), got_l).reshape(N_CHIPS, lora, RW // 4))
        if isinstance(got, dict):
            if "w_in" in got:
                sums = [_my_layout_to_shards(cfg, got["w_in"][0]), got["w_in"][1]]
                chip_sums.update(zip(late, sums))
                return _scatter_stage1(sums)
            gs = [_shard_major(got["w_proj_fox"], 1), _shard_major(got["w_proj_rwkv"], 1),
                  _shard_major(got["w_out"], 0)]
            recv1 = _exchange_halves("exchange_halves_" + early[0], gs)
            sums = [_add_halves("add_halves_" + nm, g, r, c_idx) for nm, g, r in zip(early, gs, recv1)]
            chip_sums.update(zip(early, sums))
            return _scatter_to_owners(sums)
        direct.update(zip(late, got[0::2]))
        return _scatter_stage2([_add_passed("add_passed_" + nm, chip_sums[nm], g, n1_slot)
                                for nm, g in zip(late, got[1::2])])

    loss_dev, gx, small, _, recv2 = _local_step(
        cfg, x[0], loss_target[0], norm_gain, w_my, fox_forget_bias, rwkv_shift_mix, rwkv_w0, rwkv_a0, rwkv_k_k,
        rwkv_k_a, rwkv_r_k, rwkv_ln_w, rwkv_ln_b, final_norm_gain, (_gather_parts(mine), unpack), exchange)
    loss = lax.psum(loss_dev, ("x", "y", "c"))

    small_shapes = [args[nm].shape for nm in _SMALL]
    packed = _pack_small([small[nm] for nm in _SMALL])
    reduced = [_sum_chips("sum_chips_" + nm, r, chip_sums[nm], place) for nm, r in zip(early, recv2[:3])]
    reduced += [_sum_stages("sum_stages_" + nm, chip_sums[nm], direct[nm], via, place, transposed=nm == "w_in")
                for nm, via in zip(late, recv2[3:])]
    *joined, small_all = _join_halves(reduced, [int(nm == "w_in") for nm in names], packed)
    g_small = _sum_slots("sum_small", small_all)

    grads = dict(zip(_SMALL, _unpack_small(g_small, small_shapes)))
    grads.update({nm: g[None] for nm, g in zip(names, joined) if nm not in ("lora", "w_in")})
    g_lora_f = joined[names.index("lora")]
    grads["rwkv_w2"] = g_lora_f[None, :lora]
    grads["rwkv_a2"] = g_lora_f[None, lora:]

    delta, new_m, new_v = {}, {}, {}
    w_small = _pack_small([args[nm] for nm in _SMALL])
    m_small = _pack_small([args["m_" + nm] for nm in _SMALL])
    v_small = _pack_small([args["v_" + nm] for nm in _SMALL])
    d_s, m_s, v_s = _adamw("adamw_small", w_small, g_small, m_small, v_small)
    for tgt, pk in ((delta, d_s), (new_m, m_s), (new_v, v_s)):
        tgt.update(zip(_SMALL, _unpack_small(pk, small_shapes)))
    t_out = _adamw("adamw_w_in", w_in[0].T, joined[names.index("w_in")], m_w_in[0].T, v_w_in[0].T, copy_grad=True)
    delta["w_in"], new_m["w_in"], new_v["w_in"], grads["w_in"] = [t.T[None] for t in t_out]
    for nm in ("w_proj_fox", "w_proj_rwkv", "w_out", "rwkv_w2", "rwkv_a2"):
        shp = args[nm].shape
        two_d = (shp[1], shp[2])
        d_b, m_b, v_b = _adamw("adamw_" + nm, args[nm].reshape(two_d), grads[nm].reshape(two_d),
                               args["m_" + nm].reshape(two_d), args["v_" + nm].reshape(two_d))
        delta[nm], new_m[nm], new_v[nm] = d_b.reshape(shp), m_b.reshape(shp), v_b.reshape(shp)

    return (loss, gx[None], *[grads[n] for n in _WEIGHTS], *[delta[n] for n in _WEIGHTS],
            *[new_m[n] for n in _WEIGHTS], *[new_v[n] for n in _WEIGHTS])
```

```python
import functools

import numpy as np
import jax
import jax.numpy as jnp
from jax import lax
from jax.experimental import pallas as pl
from jax.experimental.pallas import tpu as pltpu

F32 = jnp.float32
BF16 = jnp.bfloat16
HI = lax.Precision.HIGHEST
MESH = pl.DeviceIdType.MESH

FOX_HEAD_DIM = 128
RWKV_HEAD_DIM = 64
RMS_EPS = 1e-6
GN_EPS = 64e-5
L2_EPS = 1e-12
ADAM_LR = 0.001
ADAM_B1 = 0.9
ADAM_B2 = 0.999
ADAM_EPS = 1e-08
ADAM_WD = 0.01
ADAM_STEP = 10

LANES = 128
VMEM_LIMIT = 56 * 1024 * 1024
SCAN_CHUNK = 64
SCAN_HEADS_PER_STEP = 16
SCAN_PASSES = (3, 1, 1)
N_CHIPS = 4
N_DEV = 8

_pcall = pl.pallas_call


def _cparams(sem=None):
    return pltpu.CompilerParams(dimension_semantics=sem, vmem_limit_bytes=VMEM_LIMIT)


def _softplus(x):
    return jnp.maximum(x, 0.0) + jnp.log(1.0 + jnp.exp(-jnp.abs(x)))


def _silu(z):
    return z * jax.nn.sigmoid(z)


def _rmsn(x, g):
    return x * lax.rsqrt(jnp.mean(x * x, axis=-1, keepdims=True) + RMS_EPS) * g


def _dot(a, b, dims="nn", precision=None):
    dn = {"nn": (((1,), (0,)), ((), ())), "nt": (((1,), (1,)), ((), ())), "tn": (((0,), (0,)), ((), ()))}[dims]
    return lax.dot_general(a, b, dn, precision=precision, preferred_element_type=F32)


def _split_bf16(x):
    hi = x.astype(BF16)
    return hi, (x - hi.astype(F32)).astype(BF16)


def _bdot_raw(a, b, ca, cb, passes):
    dn = (((ca,), (cb,)), ((0,), (0,)))
    mm = lambda p, q: lax.dot_general(p, q, dn, preferred_element_type=F32)
    if passes == 1:
        return mm(a.astype(BF16), b.astype(BF16))
    ah, al = _split_bf16(a)
    bh, bl = _split_bf16(b)
    return mm(ah, bh) + (mm(ah, bl) + mm(al, bh))


@functools.partial(jax.custom_vjp, nondiff_argnums=(2, 3, 4))
def _bdot_p(a, b, ca, cb, passes):
    return _bdot_raw(a, b, ca, cb, passes)


def _bdot_fwd(a, b, ca, cb, passes):
    return _bdot_raw(a, b, ca, cb, passes), (a, b)


def _bdot_bwd(ca, cb, passes, res, g):
    a, b = res
    if (ca, cb) == (2, 1):
        return _bdot_p(g, b, 2, 2, passes), _bdot_p(a, g, 1, 1, passes)
    if (ca, cb) == (2, 2):
        return _bdot_p(g, b, 2, 1, passes), _bdot_p(g, a, 1, 1, passes)
    assert (ca, cb) == (1, 1)
    return _bdot_p(b, g, 2, 2, passes), _bdot_p(a, g, 2, 1, passes)


_bdot_p.defvjp(_bdot_fwd, _bdot_bwd)


def _bdot(a, b, ca, cb, passes=3):
    return _bdot_p(a, b, ca, cb, passes)


def _dot3(a, b):
    return _bdot(a[None], b[None], 2, 1)[0]


@jax.custom_vjp
def _xdot(x, m, mt):
    hi, lo = _split_bf16(x)
    m16 = m.astype(BF16)
    return _dot(hi, m16) + _dot(lo, m16)


def _xdot_fwd(x, m, mt):
    return _xdot(x, m, mt), (m, mt)


def _xdot_bwd(res, g):
    m, mt = res
    return _xdot(g, mt, m), jnp.zeros_like(m), jnp.zeros_like(mt)


_xdot.defvjp(_xdot_fwd, _xdot_bwd)


class _Cfg:
    def __init__(self, T, D, lora):
        self.T, self.D, self.lora = T, D, lora
        self.FW = D // 2
        self.FH = self.FW // FOX_HEAD_DIM
        self.RW = D // 2
        self.RH = self.RW // RWKV_HEAD_DIM
        self.LP = -(-lora // LANES) * LANES
        self.o_fox = 0
        self.o_rwkv = 4 * self.FW
        self.o_gate = self.o_rwkv + 4 * self.RW
        self.o_f = self.o_gate + 2 * D
        self.o_wd = self.o_f + LANES
        self.o_ad = self.o_wd + self.LP
        end = self.o_ad + self.LP
        self.tn = 1280 if D >= 2048 else LANES
        self.ncol = -(-end // self.tn) * self.tn
        self.in_cols = 4 * self.FW + self.FH + 4 * self.RW + 2 * lora + 2 * D
        self.scp = -(-(self.in_cols // N_CHIPS) // LANES) * LANES
        self.rseg = 4 * self.RW + 2 * self.LP
        self.C = min(SCAN_CHUNK, T)
        self.tr = min(256, T)
        self.hb = min(SCAN_HEADS_PER_STEP, self.RH)

    def segments(self):
        FW, FH, RW, lo, D = self.FW, self.FH, self.RW, self.lora, self.D
        g_f = 4 * FW
        g_r = g_f + FH
        g_wd = g_r + 4 * RW
        g_ad = g_wd + lo
        g_g = g_ad + lo
        dh = FOX_HEAD_DIM
        qkv = [(j * FW + h * dh, dh, (3 * h + j) * dh) for h in range(FH) for j in range(3)]
        return qkv + [(3 * FW, FW, 3 * FW), (g_f, FH, self.o_f), (g_r, 4 * RW, self.o_rwkv), (g_wd, lo, self.o_wd),
                      (g_ad, lo, self.o_ad), (g_g, 2 * D, self.o_gate)]


def _shards_to_my_layout(cfg, g):
    R, sc = g.shape[1], g.shape[2]
    segs = sorted(cfg.segments(), key=lambda s: s[2])
    parts, pos = [], 0
    for g0, w, m0 in segs:
        if m0 > pos:
            parts.append(jnp.zeros((R, m0 - pos), g.dtype))
        for s in range(N_CHIPS):
            lo, hi = max(g0, s * sc), min(g0 + w, (s + 1) * sc)
            if lo < hi:
                parts.append(g[s, :, lo - s * sc:hi - s * sc])
        pos = m0 + w
    if cfg.ncol > pos:
        parts.append(jnp.zeros((R, cfg.ncol - pos), g.dtype))
    return jnp.concatenate(parts, axis=1)


def _my_layout_to_shards(cfg, wm):
    sc, R = cfg.in_cols // N_CHIPS, wm.shape[0]
    segs = sorted(cfg.segments(), key=lambda s: s[0])
    shards = []
    for s in range(N_CHIPS):
        parts = []
        for g0, w, m0 in segs:
            lo, hi = max(g0, s * sc), min(g0 + w, (s + 1) * sc)
            if lo < hi:
                parts.append(wm[:, m0 + lo - g0:m0 + hi - g0])
        parts.append(jnp.zeros((R, cfg.scp - sc), wm.dtype))
        shards.append(jnp.concatenate(parts, axis=1))
    return jnp.stack(shards, axis=0)


def _rwkv_vec_to_my(cfg, v):
    RW4, lo, LP = 4 * cfg.RW, cfg.lora, cfg.LP
    z = jnp.zeros((1, LP - lo), v.dtype)
    return jnp.concatenate([v[:, :RW4], v[:, RW4:RW4 + lo], z, v[:, RW4 + lo:], z], axis=1)


def _rwkv_vec_from_my(cfg, v):
    RW4, lo, LP = 4 * cfg.RW, cfg.lora, cfg.LP
    return jnp.concatenate([v[:, :RW4], v[:, RW4:RW4 + lo], v[:, RW4 + LP:RW4 + LP + lo]], axis=1)


def _comm_at(comm, which, steps, cin, cout, scr):
    if not comm or len(comm) <= which:
        return
    lin, total = 0, 1
    for d, n in enumerate(steps):
        lin = lin * n + pl.program_id(d)
        total *= n
    pl.when(lin == {3: 0, 4: total - 1, 5: total // 2}[which])(lambda: comm[which](cin, cout, scr))


def _mm(name, a, b, dims, out_dtype, tm, tn, tk, comm=None):
    (M, K) = a.shape if dims != "tn" else a.shape[::-1]
    N = b.shape[0] if dims == "nt" else b.shape[1]
    tm, tn, tk = min(tm, M), min(tn, N), min(tk, K)
    assert M % tm == 0 and N % tn == 0 and K % tk == 0, (name, M, N, K, tm, tn, tk)
    nk = K // tk
    steps = (M // tm, N // tn, nk)
    c_in, c_out, c_scr = comm[:3] if comm else ([], [], [])
    if dims == "nn":
        a_spec = pl.BlockSpec((tm, tk), lambda i, j, k: (i, k))
        b_spec = pl.BlockSpec((tk, tn), lambda i, j, k: (k, j))
    elif dims == "nt":
        a_spec = pl.BlockSpec((tm, tk), lambda i, j, k: (i, k))
        b_spec = pl.BlockSpec((tn, tk), lambda i, j, k: (j, k))
    else:
        a_spec = pl.BlockSpec((tk, tm), lambda i, j, k: (k, i))
        b_spec = pl.BlockSpec((tk, tn), lambda i, j, k: (k, j))

    n_acc = 1 if nk > 1 else 0

    def body(a_ref, b_ref, *rest):
        cin, o_ref = rest[:len(c_in)], rest[len(c_in)]
        cout = rest[len(c_in) + 1:len(c_in) + 1 + len(c_out)]
        scr = rest[len(c_in) + 1 + len(c_out):]
        _comm_at(comm, 3, steps, cin, cout, scr[n_acc:])
        if nk == 1:
            o_ref[...] = _dot(a_ref[...], b_ref[...], dims).astype(o_ref.dtype)
        else:
            acc_ref, k = scr[0], pl.program_id(2)

            @pl.when(k == 0)
            def _():
                acc_ref[...] = jnp.zeros_like(acc_ref)

            acc_ref[...] += _dot(a_ref[...], b_ref[...], dims)

            @pl.when(k == nk - 1)
            def _():
                o_ref[...] = acc_ref[...].astype(o_ref.dtype)

        _comm_at(comm, 5, steps, cin, cout, scr[n_acc:])
        _comm_at(comm, 4, steps, cin, cout, scr[n_acc:])

    res = _pcall(
        body, name=name, grid=steps,
        in_specs=[a_spec, b_spec] + [_ANY] * len(c_in),
        out_specs=[pl.BlockSpec((tm, tn), lambda i, j, k: (i, j))] + [_ANY] * len(c_out),
        out_shape=[jax.ShapeDtypeStruct((M, N), out_dtype)] + list(c_out),
        scratch_shapes=([pltpu.VMEM((tm, tn), F32)] if nk > 1 else []) + list(c_scr),
        compiler_params=_cparams(("arbitrary",) * 3 if comm else ("parallel", "parallel", "arbitrary")),
    )(a, b, *c_in)
    return res if comm else res[0]


def _tile(tr, w, cb=0):
    return pl.BlockSpec((tr, w), lambda i: (i, cb))


def _const(shape):
    nd = len(shape)
    return pl.BlockSpec(shape, lambda i: (0,) * nd)


def _acc_store(i, ref, val):
    @pl.when(i == 0)
    def _():
        ref[...] = val

    @pl.when(i > 0)
    def _():
        ref[...] += val


def _rms_fwd(cfg, x2, g):
    T, D, tr = cfg.T, cfg.D, cfg.tr

    def body(x_ref, g_ref, h_ref):
        h_ref[...] = _rmsn(x_ref[...], g_ref[...]).astype(BF16)

    return _pcall(body, name="rms_fwd", grid=(T // tr,), in_specs=[_tile(tr, D), _const((1, D))],
                  out_specs=_tile(tr, D), out_shape=jax.ShapeDtypeStruct((T, D), BF16),
                  compiler_params=_cparams(("parallel",)))(x2, g)


def _rms_bwd(cfg, x2, g, dh, dres, comm=None):
    T, D, tr = cfg.T, cfg.D, cfg.tr
    c_in, c_out, c_scr = comm[:3] if comm else ([], [], [])
    steps = (T // tr,)

    def body(x_ref, g_ref, dh_ref, dres_ref, *rest):
        cin, (gx_ref, dg_ref) = rest[:len(c_in)], rest[len(c_in):len(c_in) + 2]
        cout, scr = rest[len(c_in) + 2:len(c_in) + 2 + len(c_out)], rest[len(c_in) + 2 + len(c_out):]
        _comm_at(comm, 3, steps, cin, cout, scr)
        _, vjp = jax.vjp(_rmsn, x_ref[...], g_ref[...])
        dx, dg = vjp(dh_ref[...])
        gx_ref[...] = dx + dres_ref[...]
        _acc_store(pl.program_id(0), dg_ref, dg)
        _comm_at(comm, 4, steps, cin, cout, scr)

    return _pcall(body, name="rms_bwd", grid=steps,
                  in_specs=[_tile(tr, D), _const((1, D)), _tile(tr, D), _tile(tr, D)] + [_ANY] * len(c_in),
                  out_specs=[_tile(tr, D), _const((1, D))] + [_ANY] * len(c_out),
                  out_shape=[jax.ShapeDtypeStruct((T, D), F32), jax.ShapeDtypeStruct((1, D), F32)] + list(c_out),
                  scratch_shapes=list(c_scr), compiler_params=_cparams(("arbitrary",)))(x2, g, dh, dres, *c_in)


def _final(cfg, x2, mo, fg, target):
    T, D, tr = cfg.T, cfg.D, cfg.tr

    def loss_fn(hres, g, tgt):
        err = _rmsn(hres, g) - tgt
        return 0.5 * jnp.sum(jnp.mean(err * err, axis=-1, keepdims=True), axis=0, keepdims=True)

    def body(x_ref, mo_ref, g_ref, t_ref, loss_ref, dres_ref, dres16_ref, dg_ref):
        hres = x_ref[...] + mo_ref[...]
        loss, vjp = jax.vjp(functools.partial(loss_fn, tgt=t_ref[...]), hres, g_ref[...])
        dres, dg = vjp(jnp.ones((1, 1), F32))
        dres_ref[...] = dres
        dres16_ref[...] = dres.astype(BF16)
        i = pl.program_id(0)
        _acc_store(i, dg_ref, dg)
        _acc_store(i, loss_ref, jnp.broadcast_to(loss, (8, LANES)))

    return _pcall(body, name="final_loss", grid=(T // tr,),
                  in_specs=[_tile(tr, D), _tile(tr, D), _const((1, D)), _tile(tr, D)],
                  out_specs=[_const((8, LANES)), _tile(tr, D), _tile(tr, D), _const((1, D))],
                  out_shape=[jax.ShapeDtypeStruct((8, LANES), F32), jax.ShapeDtypeStruct((T, D), F32),
                             jax.ShapeDtypeStruct((T, D), BF16), jax.ShapeDtypeStruct((1, D), F32)],
                  compiler_params=_cparams(("arbitrary",)))(x2, mo, fg, target)


def _merge_fn(pa, pb, ga, gb):
    return jax.nn.sigmoid(ga) * pa + jax.nn.sigmoid(gb) * pb


def _merge_fwd(cfg, pa, pb, u):
    T, D, tr = cfg.T, cfg.D, cfg.tr
    cga, cgb = cfg.o_gate // D, cfg.o_gate // D + 1

    def body(pa_ref, pb_ref, ga_ref, gb_ref, m_ref):
        m_ref[...] = _merge_fn(pa_ref[...], pb_ref[...], ga_ref[...], gb_ref[...]).astype(BF16)

    return _pcall(body, name="merge_fwd", grid=(T // tr,),
                  in_specs=[_tile(tr, D), _tile(tr, D), _tile(tr, D, cga), _tile(tr, D, cgb)],
                  out_specs=_tile(tr, D), out_shape=jax.ShapeDtypeStruct((T, D), BF16),
                  compiler_params=_cparams(("parallel",)))(pa, pb, u, u)


def _merge_bwd(cfg, pa, pb, u, dm):
    T, D, tr = cfg.T, cfg.D, cfg.tr
    cga, cgb = cfg.o_gate // D, cfg.o_gate // D + 1

    def body(pa_ref, pb_ref, ga_ref, gb_ref, dm_ref, dpa_ref, dpb_ref, dg_ref):
        _, vjp = jax.vjp(_merge_fn, pa_ref[...], pb_ref[...], ga_ref[...], gb_ref[...])
        dpa, dpb, dga, dgb = vjp(dm_ref[...])
        dpa_ref[...] = dpa.astype(BF16)
        dpb_ref[...] = dpb.astype(BF16)
        dg_ref[:, :D] = dga.astype(BF16)
        dg_ref[:, D:] = dgb.astype(BF16)

    return _pcall(body, name="merge_bwd", grid=(T // tr,),
                  in_specs=[_tile(tr, D), _tile(tr, D), _tile(tr, D, cga), _tile(tr, D, cgb), _tile(tr, D)],
                  out_specs=[_tile(tr, D), _tile(tr, D), _tile(tr, 2 * D, cfg.o_gate // (2 * D))],
                  out_shape=[jax.ShapeDtypeStruct((T, D), BF16), jax.ShapeDtypeStruct((T, D), BF16),
                             jax.ShapeDtypeStruct((T, cfg.ncol), BF16)],
                  compiler_params=_cparams(("parallel",)))(pa, pb, u, u, dm)


def _gate_fn(o, z):
    return o * _silu(z)


def _gate_a_fwd(cfg, o, u):
    T, FW, tr = cfg.T, cfg.FW, cfg.tr

    def body(o_ref, z_ref, oa_ref):
        oa_ref[...] = _gate_fn(o_ref[...], z_ref[...]).astype(BF16)

    return _pcall(body, name="gate_a_fwd", grid=(T // tr,), in_specs=[_tile(tr, FW), _tile(tr, FW, 3)],
                  out_specs=_tile(tr, FW), out_shape=jax.ShapeDtypeStruct((T, FW), BF16),
                  compiler_params=_cparams(("parallel",)))(o, u)


def _gate_a_bwd(cfg, o, u, doa, du):
    T, FW, tr = cfg.T, cfg.FW, cfg.tr

    def body(o_ref, z_ref, doa_ref, du_in, do_ref, dz_ref):
        _, vjp = jax.vjp(_gate_fn, o_ref[...], z_ref[...])
        do, dz = vjp(doa_ref[...])
        do_ref[...] = do
        dz_ref[...] = dz.astype(BF16)

    return _pcall(body, name="gate_a_bwd", grid=(T // tr,),
                  in_specs=[_tile(tr, FW), _tile(tr, FW, 3), _tile(tr, FW), _ANY],
                  out_specs=[_tile(tr, FW), _tile(tr, FW, 3)],
                  out_shape=[jax.ShapeDtypeStruct((T, FW), F32), jax.ShapeDtypeStruct(du.shape, BF16)],
                  input_output_aliases={3: 1},
                  compiler_params=_cparams(("parallel",)))(o, u, doa, du)


def _fox_prep(cfg, u, fb):
    T, tr = cfg.T, cfg.tr
    cf = cfg.o_f // LANES

    def body(f_ref, fb_ref, c_ref, carry_ref):
        i = pl.program_id(0)

        @pl.when(i == 0)
        def _():
            carry_ref[...] = jnp.zeros_like(carry_ref)

        lf = -_softplus(-(f_ref[...] + fb_ref[...]))
        r = lax.broadcasted_iota(jnp.int32, (tr, tr), 0)
        c = lax.broadcasted_iota(jnp.int32, (tr, tr), 1)
        tri = (r >= c).astype(F32)
        c_ref[...] = _dot(tri, lf, precision=HI) + carry_ref[...]
        carry_ref[...] += jnp.sum(lf, axis=0, keepdims=True)

    return _pcall(body, name="fox_prep", grid=(T // tr,), in_specs=[_tile(tr, LANES, cf), _const((1, LANES))],
                  out_specs=_tile(tr, LANES), out_shape=jax.ShapeDtypeStruct((T, LANES), F32),
                  scratch_shapes=[pltpu.VMEM((1, LANES), F32)], compiler_params=_cparams(("arbitrary",)))(u, fb)


def _fox_prep_bwd(cfg, u, fb, dc):
    T, tr = cfg.T, cfg.tr
    cf = cfg.o_f // LANES
    nb = T // tr

    def body(f_ref, fb_ref, dc_ref, df_ref, dfb_ref, carry_ref):
        i = pl.program_id(0)

        @pl.when(i == 0)
        def _():
            carry_ref[...] = jnp.zeros_like(carry_ref)

        dc = dc_ref[...]
        r = lax.broadcasted_iota(jnp.int32, (tr, tr), 0)
        c = lax.broadcasted_iota(jnp.int32, (tr, tr), 1)
        triu = (r <= c).astype(F32)
        dlf = _dot(triu, dc, precision=HI) + carry_ref[...]
        carry_ref[...] += jnp.sum(dc, axis=0, keepdims=True)
        dz = dlf * jax.nn.sigmoid(-(f_ref[...] + fb_ref[...]))
        df_ref[...] = dz.astype(BF16)
        _acc_store(i, dfb_ref, jnp.sum(dz, axis=0, keepdims=True))

    rev = lambda i: (nb - 1 - i, 0)
    return _pcall(body, name="fox_prep_bwd", grid=(nb,),
                  in_specs=[pl.BlockSpec((tr, LANES), lambda i: (nb - 1 - i, cf)), _const((1, LANES)),
                            pl.BlockSpec((tr, LANES), rev)],
                  out_specs=[pl.BlockSpec((tr, LANES), rev), _const((1, LANES))],
                  out_shape=[jax.ShapeDtypeStruct((T, LANES), BF16), jax.ShapeDtypeStruct((1, LANES), F32)],
                  scratch_shapes=[pltpu.VMEM((1, LANES), F32)], compiler_params=_cparams(("arbitrary",)))(u, fb, dc)


def _attn_logits(q_ref, k_ref, c_ref, i, tq, te):
    s = _dot(q_ref[...].astype(BF16), k_ref[0:te, :].astype(BF16), "nt") * (FOX_HEAD_DIM ** -0.5) - c_ref[0, :, 0:te]
    row = i * tq + lax.broadcasted_iota(jnp.int32, (tq, te), 0)
    col = lax.broadcasted_iota(jnp.int32, (tq, te), 1)
    return jnp.where(col <= row, s, -1e30)


def _per_query_tile(i, nq, tq, fn):
    for ii in range(nq):
        pl.when(i == ii)(functools.partial(fn, (ii + 1) * tq))


def _attn_fwd(cfg, u, c_rows):
    T, FW, FH = cfg.T, cfg.FW, cfg.FH
    tq = min(256, T)
    dh = FOX_HEAD_DIM

    def body(q_ref, k_ref, v_ref, c_ref, o_ref, lse_ref):
        i = pl.program_id(1)

        def tile(te):
            s = _attn_logits(q_ref, k_ref, c_ref, i, tq, te)
            m = jnp.max(s, axis=1, keepdims=True)
            p = jnp.exp(s - m)
            l = jnp.sum(p, axis=1, keepdims=True)
            o_ref[...] = _dot(p.astype(BF16), v_ref[0:te, :].astype(BF16)) / l
            lse_ref[0] = m + jnp.log(l)

        _per_query_tile(i, T // tq, tq, tile)

    return _pcall(
        body, name="fox_attn_fwd", grid=(FH, T // tq),
        in_specs=[pl.BlockSpec((tq, dh), lambda h, i: (i, 3 * h)), pl.BlockSpec((T, dh), lambda h, i: (0, 3 * h + 1)),
                  pl.BlockSpec((T, dh), lambda h, i: (0, 3 * h + 2)), pl.BlockSpec((1, 1, T), lambda h, i: (h, 0, 0))],
        out_specs=[pl.BlockSpec((tq, dh), lambda h, i: (i, h)), pl.BlockSpec((1, tq, 1), lambda h, i: (h, i, 0))],
        out_shape=[jax.ShapeDtypeStruct((T, FW), F32), jax.ShapeDtypeStruct((FH, T, 1), F32)],
        compiler_params=_cparams(("parallel", "arbitrary")),
    )(u, u, u, c_rows)


def _attn_bwd(cfg, u, c_rows, lse, do, du):
    T, FW, FH = cfg.T, cfg.FW, cfg.FH
    tq = min(256, T)
    nq = T // tq
    dh = FOX_HEAD_DIM
    scale = dh ** -0.5

    def body(q_ref, k_ref, v_ref, c_ref, lse_ref, do_ref, du_in, du_ref, dcol_ref, dk_acc, dv_acc):
        i = pl.program_id(1)

        @pl.when(i == 0)
        def _():
            dk_acc[...] = jnp.zeros_like(dk_acc)
            dv_acc[...] = jnp.zeros_like(dv_acc)
            dcol_ref[...] = jnp.zeros_like(dcol_ref)

        def tile(te):
            s = _attn_logits(q_ref, k_ref, c_ref, i, tq, te)
            p = jnp.exp(s - lse_ref[0])
            do_v = do_ref[...]
            dp = _dot(do_v.astype(BF16), v_ref[0:te, :].astype(BF16), "nt")
            delta = jnp.sum(p * dp, axis=1, keepdims=True)
            ds = p * (dp - delta)
            ds16 = ds.astype(BF16)
            du_ref[te - tq:te, 0:dh] = (_dot(ds16, k_ref[0:te, :].astype(BF16)) * scale).astype(BF16)
            dk_acc[0:te, :] += _dot(ds16, q_ref[...].astype(BF16), "tn") * scale
            dv_acc[0:te, :] += _dot(p.astype(BF16), do_v.astype(BF16), "tn")
            dcol_ref[0, :, 0:te] += jnp.sum(ds, axis=0, keepdims=True)

        _per_query_tile(i, nq, tq, tile)

        @pl.when(i == nq - 1)
        def _():
            du_ref[:, dh:2 * dh] = dk_acc[...].astype(BF16)
            du_ref[:, 2 * dh:3 * dh] = dv_acc[...].astype(BF16)

    return _pcall(
        body, name="fox_attn_bwd", grid=(FH, nq),
        in_specs=[pl.BlockSpec((tq, dh), lambda h, i: (i, 3 * h)), pl.BlockSpec((T, dh), lambda h, i: (0, 3 * h + 1)),
                  pl.BlockSpec((T, dh), lambda h, i: (0, 3 * h + 2)), pl.BlockSpec((1, 1, T), lambda h, i: (h, 0, 0)),
                  pl.BlockSpec((1, tq, 1), lambda h, i: (h, i, 0)), pl.BlockSpec((tq, dh), lambda h, i: (i, h)), _ANY],
        out_specs=[pl.BlockSpec((T, 3 * dh), lambda h, i: (0, h)), pl.BlockSpec((1, 1, T), lambda h, i: (h, 0, 0))],
        out_shape=[jax.ShapeDtypeStruct(du.shape, BF16), jax.ShapeDtypeStruct((FH, 1, T), F32)],
        scratch_shapes=[pltpu.VMEM((T, dh), F32), pltpu.VMEM((T, dh), F32)],
        input_output_aliases={6: 0},
        compiler_params=_cparams(("parallel", "arbitrary")),
    )(u, u, u, c_rows, lse, do, du)


def _head_indicators(cfg):
    ind = np.zeros((cfg.RW, LANES), np.float32)
    ind[np.arange(cfg.RW), np.arange(cfg.RW) // RWKV_HEAD_DIM] = 1.0
    pad = np.zeros((1, LANES), np.float32)
    pad[0, cfg.RH:] = 1.0
    return jnp.asarray(ind), jnp.asarray(ind.T.copy()), jnp.asarray(pad)


def _prep_fn(us_r, us_k, us_v, us_wd, us_ad, w0, w2p, a0, a2p, k_k, k_a, ind, ind_t, pad):
    wpre = w0 + _dot3(jnp.tanh(us_wd), w2p)
    w = -_softplus(-wpre) - 0.5
    lw = -jnp.exp(w)
    a = jax.nn.sigmoid(a0 + _dot3(us_ad, a2p))
    kk = us_k * k_k
    ss = _xdot(kk * kk, ind, ind_t) + pad
    inv = 1.0 / jnp.maximum(jnp.sqrt(ss), L2_EPS)
    kkn = kk * _xdot(inv, ind_t, ind)
    kp = us_k * (1.0 + (a - 1.0) * k_a)
    return us_r, lw, kp, us_v, -kkn, kkn * a


def _shifted(u, prev_row, mu, first):
    n = u.shape[0]
    rolled = pltpu.roll(u, 1, 0)
    row = lax.broadcasted_iota(jnp.int32, u.shape, 0)
    p0 = jnp.where(first, jnp.zeros_like(prev_row), prev_row)
    prev = jnp.where(row == 0, jnp.broadcast_to(p0, u.shape), rolled)
    return u + (prev - u) * mu, prev


def _rwkv_specs(cfg, tr):
    RW, LP = cfg.RW, cfg.LP
    base = cfg.o_rwkv // RW
    cols = [(RW, base), (RW, base + 1), (RW, base + 2), (RW, base + 3), (LP, cfg.o_wd // LP), (LP, cfg.o_ad // LP)]
    cur = [pl.BlockSpec((tr, w), (lambda i, cb=cb: (i, cb))) for w, cb in cols]
    prv = [pl.BlockSpec((8, w), (lambda i, cb=cb: (jnp.maximum(i * (tr // 8) - 1, 0), cb))) for w, cb in cols]
    return cols, cur, prv


def _mu_pieces(cfg, mu_ref):
    RW, LP = cfg.RW, cfg.LP
    offs = [0, RW, 2 * RW, 3 * RW, 4 * RW, 4 * RW + LP, 4 * RW + 2 * LP]
    return [mu_ref[:, offs[j]:offs[j + 1]] for j in range(6)]


def _rwkv_prep_fwd(cfg, u, mu, w0, w2p, a0, a2p, k_k, k_a):
    T, RW, LP, tr = cfg.T, cfg.RW, cfg.LP, cfg.tr
    ind, ind_t, pad = _head_indicators(cfg)
    cols, cur, prv = _rwkv_specs(cfg, tr)

    def body(*refs):
        u_refs, p_refs = refs[0:6], refs[6:12]
        mu_ref, w0_ref, w2_ref, a0_ref, a2_ref, kk_ref, ka_ref, ind_ref, indt_ref, pad_ref = refs[12:22]
        outs = refs[22:]
        first = pl.program_id(0) == 0
        mus = _mu_pieces(cfg, mu_ref)
        us = [_shifted(u_refs[j][...], p_refs[j][7:8, :], mus[j], first)[0] for j in range(6)]
        res = _prep_fn(us[0], us[1], us[2], us[4], us[5], w0_ref[...], w2_ref[...], a0_ref[...], a2_ref[...],
                       kk_ref[...], ka_ref[...], ind_ref[...], indt_ref[...], pad_ref[...])
        for j in range(6):
            outs[j][...] = res[j]
        outs[6][...] = us[3]

    consts = [mu, w0, w2p, a0, a2p, k_k, k_a, ind, ind_t, pad]
    return _pcall(body, name="rwkv_prep_fwd", grid=(T // tr,),
                  in_specs=cur + prv + [_const(c.shape) for c in consts],
                  out_specs=[_tile(tr, RW)] * 7, out_shape=[jax.ShapeDtypeStruct((T, RW), F32)] * 7,
                  compiler_params=_cparams(("parallel",)))(*([u] * 12), *consts)


def _rwkv_prep_bwd(cfg, u, mu, w0, w2p, a0, a2p, k_k, k_a, cots, dzb):
    T, RW, LP = cfg.T, cfg.RW, cfg.LP
    tr = min(128, T)
    ind, ind_t, pad = _head_indicators(cfg)
    cols, cur, prv = _rwkv_specs(cfg, tr)
    rseg = cfg.rseg

    def body(*refs):
        u_refs, p_refs = refs[0:6], refs[6:12]
        mu_ref, w0_ref, w2_ref, a0_ref, a2_ref, kk_ref, ka_ref, ind_ref, indt_ref, pad_ref = refs[12:22]
        cot_refs, dzb_ref = refs[22:28], refs[28]
        dus_ref, dmu_ref, dw0_ref, dw2_ref, da0_ref, da2_ref, dkk_ref, dka_ref = refs[29:]
        i = pl.program_id(0)
        first = i == 0
        mus = _mu_pieces(cfg, mu_ref)
        sh = [_shifted(u_refs[j][...], p_refs[j][7:8, :], mus[j], first) for j in range(6)]
        us = [s[0] for s in sh]
        fn = functools.partial(_prep_fn, ind=ind_ref[...], ind_t=indt_ref[...], pad=pad_ref[...])
        _, vjp = jax.vjp(fn, us[0], us[1], us[2], us[4], us[5], w0_ref[...], w2_ref[...], a0_ref[...], a2_ref[...],
                         kk_ref[...], ka_ref[...])
        d = vjp(tuple(c[...] for c in cot_refs))
        dus = [d[0], d[1], d[2], dzb_ref[...], d[3], d[4]]
        offs = [0, RW, 2 * RW, 3 * RW, 4 * RW, 4 * RW + LP, 4 * RW + 2 * LP]
        for j in range(6):
            dus_ref[:, offs[j]:offs[j + 1]] = dus[j]
            dmu_j = jnp.sum(dus[j] * (sh[j][1] - u_refs[j][...]), axis=0, keepdims=True)

            @pl.when(first)
            def _(j=j, dmu_j=dmu_j):
                dmu_ref[:, offs[j]:offs[j + 1]] = dmu_j

            @pl.when(i > 0)
            def _(j=j, dmu_j=dmu_j):
                dmu_ref[:, offs[j]:offs[j + 1]] += dmu_j
        for ref, val in zip((dw0_ref, dw2_ref, da0_ref, da2_ref, dkk_ref, dka_ref), d[5:11]):
            _acc_store(i, ref, val)

    consts = [mu, w0, w2p, a0, a2p, k_k, k_a, ind, ind_t, pad]
    vec = jax.ShapeDtypeStruct((1, RW), F32)
    mat = jax.ShapeDtypeStruct((LP, RW), F32)
    return _pcall(body, name="rwkv_prep_bwd", grid=(T // tr,),
                  in_specs=cur + prv + [_const(c.shape) for c in consts] + [_tile(tr, RW)] * 7,
                  out_specs=[_tile(tr, rseg), _const((1, rseg)), _const((1, RW)), _const((LP, RW)), _const((1, RW)),
                             _const((LP, RW)), _const((1, RW)), _const((1, RW))],
                  out_shape=[jax.ShapeDtypeStruct((T, rseg), F32), jax.ShapeDtypeStruct((1, rseg), F32),
                             vec, mat, vec, mat, vec, vec],
                  compiler_params=_cparams(("arbitrary",)))(*([u] * 12), *consts, *cots, dzb)


def _shift_bwd(cfg, dus, mu, df, du):
    T, tr, RW, LP = cfg.T, cfg.tr, cfg.RW, cfg.LP
    nb = T // tr
    tail = cfg.ncol - cfg.o_f
    assert cfg.o_rwkv % (4 * RW) == 0 and (4 * RW) % (2 * LP) == 0 and cfg.o_f % tail == 0

    def shifted(d_ref, n_ref, mu_ref):
        d = d_ref[...]
        rolled = pltpu.roll(d, tr - 1, 0)
        row = lax.broadcasted_iota(jnp.int32, d.shape, 0)
        n0 = jnp.where(pl.program_id(0) == nb - 1, jnp.zeros_like(n_ref[0:1, :]), n_ref[0:1, :])
        nxt = jnp.where(row == tr - 1, jnp.broadcast_to(n0, d.shape), rolled)
        mu_v = mu_ref[...]
        return (d * (1.0 - mu_v) + nxt * mu_v).astype(BF16)

    def main_body(d_ref, n_ref, mu_ref, du_in, du_ref):
        du_ref[...] = shifted(d_ref, n_ref, mu_ref)

    def tail_body(d_ref, n_ref, mu_ref, df_ref, du_in, du_ref):
        du_ref[:, 0:LANES] = df_ref[...]
        du_ref[:, LANES:LANES + 2 * LP] = shifted(d_ref, n_ref, mu_ref)
        if tail > LANES + 2 * LP:
            du_ref[:, LANES + 2 * LP:] = jnp.zeros((tr, tail - LANES - 2 * LP), BF16)

    def specs(w, cb):
        return [_tile(tr, w, cb),
                pl.BlockSpec((8, w), lambda i: (jnp.minimum((i + 1) * (tr // 8), T // 8 - 1), cb)),
                pl.BlockSpec((1, w), lambda i: (0, cb))]

    out = jax.ShapeDtypeStruct(du.shape, BF16)
    du = _pcall(main_body, name="shift_bwd_main", grid=(nb,), in_specs=specs(4 * RW, 0) + [_ANY],
                out_specs=_tile(tr, 4 * RW, cfg.o_rwkv // (4 * RW)), out_shape=out, input_output_aliases={3: 0},
                compiler_params=_cparams(("parallel",)))(dus, dus, mu, du)
    return _pcall(tail_body, name="shift_bwd_tail", grid=(nb,),
                  in_specs=specs(2 * LP, 4 * RW // (2 * LP)) + [_tile(tr, LANES), _ANY],
                  out_specs=_tile(tr, tail, cfg.o_f // tail), out_shape=out, input_output_aliases={4: 0},
                  compiler_params=_cparams(("parallel",)))(dus, dus, mu, df, du)


def _chunk_local(r, lw, k, v, a, b):
    H, C, K = r.shape
    row = lax.broadcasted_iota(jnp.int32, (C, C), 0)
    col = lax.broadcasted_iota(jnp.int32, (C, C), 1)
    incl = jnp.broadcast_to((row >= col).astype(F32)[None], (H, C, C))
    strict = (row > col)[None]
    lower = (row >= col)[None]
    eye = (row == col)[None]
    zero = jnp.zeros((), F32)
    L = _bdot(incl, lw, 2, 1)
    LC = jnp.sum(lw, axis=1, keepdims=True)
    eL = jnp.exp(L)
    eLn = jnp.exp(-L)
    at = a * jnp.exp(L - lw)
    rt = r * eL
    bt = b * eLn
    kt = k * eLn
    eR = jnp.exp(LC - L)
    bh = b * eR
    kh = k * eR
    gram = functools.partial(_bdot, passes=SCAN_PASSES[0])
    inv = functools.partial(_bdot, passes=SCAN_PASSES[1])
    app = functools.partial(_bdot, passes=SCAN_PASSES[2])
    n_ab = jnp.where(strict, gram(at, bt, 2, 2), zero)
    n_ak = jnp.where(strict, gram(at, kt, 2, 2), zero)
    m_rb = jnp.where(lower, gram(rt, bt, 2, 2), zero)
    m_rk = jnp.where(lower, gram(rt, kt, 2, 2), zero)
    M = n_ab
    P = jnp.where(eye, 1.0, zero) + n_ab
    for _ in range(1, max(1, int(np.ceil(np.log2(C))))):
        M = inv(M, M, 2, 1)
        P = P + inv(M, P, 2, 1)
    W = app(P, at, 2, 1)
    Uloc = app(P, app(n_ak, v, 2, 1), 2, 1)
    Q = rt + app(m_rb, W, 2, 1)
    Yloc = app(m_rb, Uloc, 2, 1) + app(m_rk, v, 2, 1)
    A = jnp.where(eye, jnp.exp(LC), zero) + app(W, bh, 1, 1)
    Sloc = app(Uloc, bh, 1, 1) + app(v, kh, 1, 1)
    return Q, Yloc, A, Sloc


def _split_heads(ref, n):
    N = RWKV_HEAD_DIM
    return jnp.stack([ref[:, h * N:(h + 1) * N] for h in range(n)], axis=0)


def _merge_heads(x):
    return jnp.concatenate([x[h] for h in range(x.shape[0])], axis=1)


def _scan_local_specs(cfg):
    N, HB = RWKV_HEAD_DIM, cfg.hb
    grid = (cfg.RH // HB, cfg.T // cfg.C)
    seq = pl.BlockSpec((HB, cfg.C, N), lambda h, j: (h, j, 0))
    mat = pl.BlockSpec((HB, 1, N, N), lambda h, j: (h, j, 0, 0))
    return grid, seq, mat


def _scan_local_fwd(cfg, seqs):
    T, RH, N = cfg.T, cfg.RH, RWKV_HEAD_DIM
    grid, seq, mat = _scan_local_specs(cfg)

    def body(r_ref, lw_ref, k_ref, v_ref, a_ref, b_ref, q_ref, yl_ref, a_out, sl_ref):
        Q, Yloc, A, Sloc = _chunk_local(*[_split_heads(ref, cfg.hb) for ref in (r_ref, lw_ref, k_ref, v_ref, a_ref, b_ref)])
        q_ref[...] = Q
        yl_ref[...] = Yloc
        a_out[:, 0] = A
        sl_ref[:, 0] = Sloc

    tok = pl.BlockSpec((cfg.C, cfg.hb * N), lambda h, j: (j, h))
    sq = jax.ShapeDtypeStruct((RH, T, N), F32)
    mt = jax.ShapeDtypeStruct((RH, T // cfg.C, N, N), F32)
    return _pcall(body, name="rwkv_scan_local_fwd", grid=grid, in_specs=[tok] * 6, out_specs=[seq, seq, mat, mat],
                  out_shape=[sq, sq, mt, mt], compiler_params=_cparams(("parallel", "parallel")))(*seqs)


def _scan_local_bwd(cfg, toks, dq, dy, da, dsl, extra, comm=None):
    T, RW, N = cfg.T, cfg.RW, RWKV_HEAD_DIM
    grid, seq, mat = _scan_local_specs(cfg)
    c_in, c_out, c_scr = comm[:3] if comm else ([], [], [])

    def body(r_ref, lw_ref, k_ref, v_ref, a_ref, b_ref, dq_ref, dy_ref, da_ref, dsl_ref, xr_ref, xk_ref, xv_ref,
             *rest):
        cin, outs = rest[:len(c_in)], rest[len(c_in):len(c_in) + 6]
        cout, scr = rest[len(c_in) + 6:len(c_in) + 6 + len(c_out)], rest[len(c_in) + 6 + len(c_out):]
        _comm_at(comm, 3, grid, cin, cout, scr)
        ins = [_split_heads(ref, cfg.hb) for ref in (r_ref, lw_ref, k_ref, v_ref, a_ref, b_ref)]
        _, vjp = jax.vjp(_chunk_local, *ins)
        d = vjp((dq_ref[...], _split_heads(dy_ref, cfg.hb), da_ref[:, 0], dsl_ref[:, 0]))
        add = {0: xr_ref, 2: xk_ref, 3: xv_ref}
        for j in range(6):
            dj = _merge_heads(d[j])
            outs[j][...] = dj + add[j][...] if j in add else dj
        _comm_at(comm, 4, grid, cin, cout, scr)

    tok = pl.BlockSpec((cfg.C, cfg.hb * N), lambda h, j: (j, h))
    return _pcall(body, name="rwkv_scan_local_bwd", grid=grid,
                  in_specs=[tok] * 6 + [seq, tok, mat, mat] + [tok] * 3 + [_ANY] * len(c_in),
                  out_specs=[tok] * 6 + [_ANY] * len(c_out),
                  out_shape=[jax.ShapeDtypeStruct((T, RW), F32)] * 6 + list(c_out), scratch_shapes=list(c_scr),
                  compiler_params=_cparams(("arbitrary", "arbitrary") if comm else ("parallel", "parallel")),
                  )(*toks, dq, dy, da, dsl, *extra, *c_in)


def _scan_carry_specs(cfg, rev):
    N, RH, C, nc = RWKV_HEAD_DIM, cfg.RH, cfg.C, cfg.T // cfg.C
    at = (lambda j: nc - 1 - j) if rev else (lambda j: j)
    seq = pl.BlockSpec((RH, C, N), lambda j: (0, at(j), 0))
    mat = pl.BlockSpec((RH, 1, N, N), lambda j: (0, at(j), 0, 0))
    return nc, seq, mat


def _scan_carry_fwd(cfg, q, yloc, a, sloc):
    T, RH, N = cfg.T, cfg.RH, RWKV_HEAD_DIM
    nc, seq, mat = _scan_carry_specs(cfg, False)

    def body(q_ref, yl_ref, a_ref, sl_ref, y_ref, ck_ref, s_ref):
        @pl.when(pl.program_id(0) == 0)
        def _():
            s_ref[...] = jnp.zeros_like(s_ref)

        S = s_ref[...]
        ck_ref[:, 0] = S
        y_ref[...] = _merge_heads(_bdot(q_ref[...], S, 2, 2, SCAN_PASSES[2]) + yl_ref[...])
        s_ref[...] = _bdot(S, a_ref[:, 0], 2, 1) + sl_ref[:, 0]

    tok = pl.BlockSpec((cfg.C, cfg.RW), lambda j: (j, 0))
    return _pcall(body, name="rwkv_scan_carry_fwd", grid=(nc,), in_specs=[seq, seq, mat, mat], out_specs=[tok, mat],
                  out_shape=[jax.ShapeDtypeStruct((T, cfg.RW), F32), jax.ShapeDtypeStruct((RH, nc, N, N), F32)],
                  scratch_shapes=[pltpu.VMEM((RH, N, N), F32)],
                  compiler_params=_cparams(("arbitrary",)))(q, yloc, a, sloc)


def _scan_carry_bwd(cfg, q, a, ckpt, dy):
    T, RH, N = cfg.T, cfg.RH, RWKV_HEAD_DIM
    nc, seq, mat = _scan_carry_specs(cfg, True)

    def body(q_ref, a_ref, ck_ref, dy_ref, dq_ref, da_ref, dsl_ref, ds_ref):
        @pl.when(pl.program_id(0) == 0)
        def _():
            ds_ref[...] = jnp.zeros_like(ds_ref)

        S, dS, dY = ck_ref[:, 0], ds_ref[...], _split_heads(dy_ref, RH)
        dq_ref[...] = _bdot(dY, S, 2, 1, SCAN_PASSES[2])
        da_ref[:, 0] = _bdot(S, dS, 1, 1, SCAN_PASSES[2])
        dsl_ref[:, 0] = dS
        ds_ref[...] = _bdot(dS, a_ref[:, 0], 2, 2) + _bdot(dY, q_ref[...], 1, 1, SCAN_PASSES[2])

    mt = jax.ShapeDtypeStruct((RH, nc, N, N), F32)
    tok = pl.BlockSpec((cfg.C, cfg.RW), lambda j: (nc - 1 - j, 0))
    return _pcall(body, name="rwkv_scan_carry_bwd", grid=(nc,), in_specs=[seq, mat, mat, tok],
                  out_specs=[seq, mat, mat], out_shape=[jax.ShapeDtypeStruct((RH, T, N), F32), mt, mt],
                  scratch_shapes=[pltpu.VMEM((RH, N, N), F32)],
                  compiler_params=_cparams(("arbitrary",)))(q, a, ckpt, dy)


def _post_fn(y, r, kp, v, zb, ln_w, ln_b, rk, ind, ind_t):
    n = float(RWKV_HEAD_DIM)
    mu = _xdot(_xdot(y, ind, ind_t) / n, ind_t, ind)
    yc = y - mu
    var = _xdot(yc * yc, ind, ind_t) / n
    rstd = _xdot(lax.rsqrt(var + GN_EPS), ind_t, ind)
    yn = yc * rstd * ln_w + ln_b
    bonus = _xdot(_xdot(r * kp * rk, ind, ind_t), ind_t, ind) * v
    return (yn + bonus) * _silu(zb)


def _rwkv_post_fwd(cfg, y, r, kp, v, zb, ln_w, ln_b, rk):
    T, RW, tr = cfg.T, cfg.RW, cfg.tr
    ind, ind_t, _ = _head_indicators(cfg)

    def body(y_ref, r_ref, k_ref, v_ref, z_ref, lw_ref, lb_ref, rk_ref, ind_ref, indt_ref, ob_ref):
        ob_ref[...] = _post_fn(y_ref[...], r_ref[...], k_ref[...], v_ref[...], z_ref[...], lw_ref[...], lb_ref[...],
                               rk_ref[...], ind_ref[...], indt_ref[...]).astype(BF16)

    consts = [ln_w, ln_b, rk, ind, ind_t]
    return _pcall(body, name="rwkv_post_fwd", grid=(T // tr,),
                  in_specs=[_tile(tr, RW)] * 5 + [_const(c.shape) for c in consts],
                  out_specs=_tile(tr, RW), out_shape=jax.ShapeDtypeStruct((T, RW), BF16),
                  compiler_params=_cparams(("parallel",)))(y, r, kp, v, zb, *consts)


def _rwkv_post_bwd(cfg, y, r, kp, v, zb, ln_w, ln_b, rk, dob):
    T, RW = cfg.T, cfg.RW
    tr = min(128, T)
    ind, ind_t, _ = _head_indicators(cfg)

    def body(y_ref, r_ref, k_ref, v_ref, z_ref, lw_ref, lb_ref, rk_ref, ind_ref, indt_ref, dob_ref,
             dy_ref, dr_ref, dk_ref, dv_ref, dz_ref, dlw_ref, dlb_ref, drk_ref):
        fn = functools.partial(_post_fn, ind=ind_ref[...], ind_t=indt_ref[...])
        _, vjp = jax.vjp(fn, y_ref[...], r_ref[...], k_ref[...], v_ref[...], z_ref[...], lw_ref[...], lb_ref[...],
                         rk_ref[...])
        d = vjp(dob_ref[...])
        for ref, val in zip((dy_ref, dr_ref, dk_ref, dv_ref, dz_ref), d[:5]):
            ref[...] = val
        i = pl.program_id(0)
        for ref, val in zip((dlw_ref, dlb_ref, drk_ref), d[5:8]):
            _acc_store(i, ref, val)

    consts = [ln_w, ln_b, rk, ind, ind_t]
    vec = jax.ShapeDtypeStruct((1, RW), F32)
    return _pcall(body, name="rwkv_post_bwd", grid=(T // tr,),
                  in_specs=[_tile(tr, RW)] * 5 + [_const(c.shape) for c in consts] + [_tile(tr, RW)],
                  out_specs=[_tile(tr, RW)] * 5 + [_const((1, RW))] * 3,
                  out_shape=[jax.ShapeDtypeStruct((T, RW), F32)] * 5 + [vec] * 3,
                  compiler_params=_cparams(("arbitrary",)))(y, r, kp, v, zb, *consts, dob)


def _adamw_math(w, g, m, v):
    m = ADAM_B1 * m + (1.0 - ADAM_B1) * g
    v = ADAM_B2 * v + (1.0 - ADAM_B2) * (g * g)
    m_hat = m / (1.0 - ADAM_B1 ** ADAM_STEP)
    v_hat = v / (1.0 - ADAM_B2 ** ADAM_STEP)
    delta = -ADAM_LR * (m_hat / (jnp.sqrt(v_hat) + ADAM_EPS) + ADAM_WD * w)
    return delta, m, v


def _adamw(name, w, g, m, v, copy_grad=False):
    R, Cc = w.shape
    Rp = -(-R // 8) * 8
    tr = Rp
    for nb in range(1, Rp // 8 + 1):
        if (Rp // 8) % nb == 0 and (Rp // nb) * Cc * 4 <= 2 * 1024 * 1024:
            tr = Rp // nb
            break

    def body(w_ref, g_ref, m_ref, v_ref, d_ref, nm_ref, nv_ref, *g_out):
        g_v = g_ref[...]
        d, nm, nv = _adamw_math(w_ref[...], g_v, m_ref[...], v_ref[...])
        d_ref[...] = d
        nm_ref[...] = nm
        nv_ref[...] = nv
        if copy_grad:
            g_out[0][...] = g_v

    spec = _tile(tr, Cc)
    n_out = 4 if copy_grad else 3
    return _pcall(body, name=name, grid=(Rp // tr,), in_specs=[spec] * 4, out_specs=[spec] * n_out,
                  out_shape=[jax.ShapeDtypeStruct((R, Cc), F32)] * n_out,
                  compiler_params=_cparams(("parallel",)))(w, g, m, v)


def _row_tile(R, Cc, itemsize, budget=2 * 1024 * 1024):
    for nb in range(1, R // 16 + 1):
        if R % nb == 0 and (R // nb) % 16 == 0 and (R // nb) * Cc * itemsize <= budget:
            return R // nb
    return R


def _add_halves(name, gs, r1, c_idx):
    S, R, Cc = gs.shape
    half = R // 2
    tr = _row_tile(half, Cc, 4)
    nb = half // tr

    def body(c_ref, g_ref, r_ref, o_ref):
        o_ref[...] = (g_ref[...].astype(F32) + r_ref[...].astype(F32)).astype(BF16)

    grid_spec = pltpu.PrefetchScalarGridSpec(
        num_scalar_prefetch=1, grid=(S, nb),
        in_specs=[pl.BlockSpec((1, tr, Cc), lambda s, i, c: (s, c[0] * nb + i, 0)),
                  pl.BlockSpec((1, tr, Cc), lambda s, i, c: (s, i, 0))],
        out_specs=pl.BlockSpec((1, tr, Cc), lambda s, i, c: (s, i, 0)))
    return _pcall(body, name=name, grid_spec=grid_spec, out_shape=jax.ShapeDtypeStruct((S, half, Cc), BF16),
                  compiler_params=_cparams(("parallel", "parallel")))(c_idx, gs, r1)


def _sum_slots(name, r2):
    S, R, Cc = r2.shape
    tr = _row_tile(R, Cc, 4 * S // 2 if r2.dtype == BF16 else 4 * S)

    def body(r_ref, o_ref):
        acc = r_ref[0].astype(F32)
        for s in range(1, S):
            acc = acc + r_ref[s].astype(F32)
        o_ref[...] = acc

    return _pcall(body, name=name, grid=(R // tr,), in_specs=[pl.BlockSpec((S, tr, Cc), lambda i: (0, i, 0))],
                  out_specs=_tile(tr, Cc), out_shape=jax.ShapeDtypeStruct((R, Cc), F32),
                  compiler_params=_cparams(("parallel",)))(r2)


def _sum_chips(name, recv, own, place):
    S, H, Cc = recv.shape
    tr = _row_tile(H, Cc, 4, 1024 * 1024)
    nb = H // tr

    def body(p_ref, r_ref, own_ref, o_ref):
        s = pl.program_id(1)
        me = p_ref[0]

        @pl.when(s == 0)
        def _():
            o_ref[...] = jnp.zeros_like(o_ref)

        @pl.when(s == me)
        def _():
            o_ref[...] += own_ref[0].astype(F32)

        @pl.when(s != me)
        def _():
            o_ref[...] += r_ref[0].astype(F32)

    grid_spec = pltpu.PrefetchScalarGridSpec(
        num_scalar_prefetch=1, grid=(nb, S),
        in_specs=[pl.BlockSpec((1, tr, Cc), lambda i, s, p: (jnp.where(s == p[0], (s + 1) % S, s), i, 0)),
                  pl.BlockSpec((1, tr, Cc), lambda i, s, p: (p[0], i, 0))],
        out_specs=pl.BlockSpec((tr, Cc), lambda i, s, p: (p[1] * nb + i, 0)))
    return _pcall(body, name=name, grid_spec=grid_spec, out_shape=jax.ShapeDtypeStruct((2 * H, Cc), F32),
                  compiler_params=_cparams(("parallel", "arbitrary")))(place, recv, own)


def _cast_bf16(name, w):
    R, Cc = w.shape
    tr = _row_tile(R, Cc, 4)

    def body(w_ref, o_ref):
        o_ref[...] = w_ref[...].astype(BF16)

    return _pcall(body, name=name, grid=(R // tr,), in_specs=[_tile(tr, Cc)], out_specs=_tile(tr, Cc),
                  out_shape=jax.ShapeDtypeStruct((R, Cc), BF16), compiler_params=_cparams(("parallel",)))(w)


_ANY = pl.BlockSpec(memory_space=pl.ANY)


def _place():
    x, y, c = lax.axis_index("x"), lax.axis_index("y"), lax.axis_index("c")
    others = [(1 - x, y), (x, 1 - y), (1 - x, 1 - y)]
    return x, y, c, others


def _gather_weights(shards):
    arrays, out_shapes, scratch, start, finish, middle = _gather_parts(shards)
    n = len(shards)

    def body(*refs):
        ins, outs, sems = refs[:n], refs[n:2 * n], refs[2 * n:]
        start(ins, outs, sems)
        middle(ins, outs, sems)
        finish(ins, outs, sems)

    return _pcall(body, name="gather_weights", in_specs=[_ANY] * n, out_specs=[_ANY] * n, out_shape=out_shapes,
                  scratch_shapes=scratch)(*arrays)


def _gather_parts(shards):
    n = len(shards)
    halves = [s.shape[0] // 2 for s in shards]

    def parts(ins, outs, sems):
        x, y, c, _ = _place()
        me = 2 * x + y
        n1 = (x ^ (1 - c), y ^ c)
        n2 = (x ^ c, y ^ (1 - c))
        s1, s2, sd = 2 * n1[0] + n1[1], 2 * n2[0] + n2[1], 2 * (1 - x) + (1 - y)
        sib = (x, y, 1 - c)

        def rows(k, chip, hc):
            return outs[k].at[chip, pl.ds(hc * halves[k], halves[k]), :]

        def remote(k, j, src, dst, to):
            return pltpu.make_async_remote_copy(src_ref=src, dst_ref=dst, send_sem=sems[0].at[6 * k + j],
                                                recv_sem=sems[1].at[6 * k + j], device_id=to, device_id_type=MESH)

        def copy(k, j):
            if j < 2:
                mine = ins[k].at[pl.ds(c * halves[k], halves[k]), :]
                return remote(k, j, mine, rows(k, me, c), (*(n1 if j == 0 else n2), c))
            land = rows(k, {2: s1, 3: s1, 4: s2, 5: sd}[j], c)
            return remote(k, j, land, land, (*n2, c) if j == 2 else sib)

        def arrived(k, j):
            hc = c if j < 3 else 1 - c
            land = rows(k, {0: s1, 1: s2, 2: sd, 3: s2, 4: s1, 5: sd}[j], hc)
            remote(k, j, land, land, (x, y, c)).wait_recv()

        return copy, arrived

    def start(ins, outs, sems):
        copy, _ = parts(ins, outs, sems)
        for k in range(n):
            copy(k, 0).start()
            copy(k, 1).start()

    def middle(ins, outs, sems):
        copy, arrived = parts(ins, outs, sems)
        for k in range(n):
            arrived(k, 0)
            copy(k, 2).start()
            copy(k, 3).start()
            arrived(k, 1)
            copy(k, 4).start()

    def finish(ins, outs, sems):
        copy, arrived = parts(ins, outs, sems)
        for k in range(n):
            arrived(k, 2)
            copy(k, 5).start()
        for k in range(n):
            for j in (3, 4, 5):
                arrived(k, j)
        for k in range(n):
            for j in range(6):
                copy(k, j).wait_send()

    out_shapes = [jax.ShapeDtypeStruct((N_CHIPS,) + s.shape, s.dtype) for s in shards]
    scratch = [pltpu.SemaphoreType.DMA((6 * n,)), pltpu.SemaphoreType.DMA((6 * n,))]
    return list(shards), out_shapes, scratch, start, finish, middle


def _exchange_halves(name, grads):
    n = len(grads)
    halves = [g.shape[1] // 2 for g in grads]

    def body(*refs):
        ins, outs = refs[:n], refs[n:2 * n]
        send_sems, recv_sems = refs[2 * n:]
        x, y, c, _ = _place()
        cps = []
        for k in range(n):
            src = ins[k].at[:, pl.ds((1 - c) * halves[k], halves[k]), :]
            cp = pltpu.make_async_remote_copy(src_ref=src, dst_ref=outs[k], send_sem=send_sems.at[k],
                                              recv_sem=recv_sems.at[k], device_id=(x, y, 1 - c), device_id_type=MESH)
            cp.start()
            cps.append(cp)
        for cp in cps:
            cp.wait()

    return _pcall(
        body, name=name, in_specs=[_ANY] * n, out_specs=[_ANY] * n,
        out_shape=[jax.ShapeDtypeStruct((g.shape[0], h) + g.shape[2:], g.dtype) for g, h in zip(grads, halves)],
        scratch_shapes=[pltpu.SemaphoreType.DMA((n,)), pltpu.SemaphoreType.DMA((n,))],
    )(*grads)


def _scatter_to_owners(chip_sums):
    n = len(chip_sums)

    def sends(ins, outs, sems):
        x, y, c, others = _place()
        me = 2 * x + y
        return [pltpu.make_async_remote_copy(
            src_ref=ins[k].at[2 * px + py], dst_ref=outs[k].at[me], send_sem=sems[0].at[3 * k + j],
            recv_sem=sems[1].at[3 * k + j], device_id=(px, py, c), device_id_type=MESH)
            for k in range(n) for j, (px, py) in enumerate(others)]

    def start(ins, outs, sems):
        for cp in sends(ins, outs, sems):
            cp.start()

    def finish(ins, outs, sems):
        x, y, c, others = _place()
        for k in range(n):
            for j, (px, py) in enumerate(others):
                land = outs[k].at[2 * px + py]
                pltpu.make_async_remote_copy(src_ref=land, dst_ref=land, send_sem=sems[0].at[3 * k + j],
                                             recv_sem=sems[1].at[3 * k + j], device_id=(x, y, c),
                                             device_id_type=MESH).wait_recv()
        for cp in sends(ins, outs, sems):
            cp.wait_send()

    out_shapes = [jax.ShapeDtypeStruct(g.shape, g.dtype) for g in chip_sums]
    scratch = [pltpu.SemaphoreType.DMA((3 * n,)), pltpu.SemaphoreType.DMA((3 * n,))]
    return list(chip_sums), out_shapes, scratch, start, finish


def _swap_with_sibling(arrays):
    n = len(arrays)

    def copies(ins, outs, sems):
        x, y, c, _ = _place()
        return [pltpu.make_async_remote_copy(src_ref=ins[k], dst_ref=outs[k], send_sem=sems[0].at[k],
                                             recv_sem=sems[1].at[k], device_id=(x, y, 1 - c), device_id_type=MESH)
                for k in range(n)]

    def start(ins, outs, sems):
        for cp in copies(ins, outs, sems):
            cp.start()

    def finish(ins, outs, sems):
        for cp in copies(ins, outs, sems):
            cp.wait()

    out_shapes = [jax.ShapeDtypeStruct(a.shape, a.dtype) for a in arrays]
    scratch = [pltpu.SemaphoreType.DMA((n,)), pltpu.SemaphoreType.DMA((n,))]
    return list(arrays), out_shapes, scratch, start, finish


def _add_pair(name, a, b):
    R, Cc = a.shape
    tr = _row_tile(R, Cc, 4)

    def body(a_ref, b_ref, o_ref):
        o_ref[...] = (a_ref[...].astype(F32) + b_ref[...].astype(F32)).astype(BF16)

    return _pcall(body, name=name, grid=(R // tr,), in_specs=[_tile(tr, Cc)] * 2, out_specs=_tile(tr, Cc),
                  out_shape=jax.ShapeDtypeStruct((R, Cc), BF16), compiler_params=_cparams(("parallel",)))(a, b)


def _second_neighbour():
    x, y, c, _ = _place()
    return (x, y, c), (x ^ c, y ^ (1 - c)), (x ^ (1 - c), y ^ c)


def _scatter_stage1(chip_sums):
    n = len(chip_sums)

    def copies(ins, outs, sems):
        (x, y, c), n2, n1 = _second_neighbour()
        diag = 2 * (1 - x) + (1 - y)
        return [pltpu.make_async_remote_copy(
            src_ref=ins[k].at[slot], dst_ref=outs[2 * k + j], send_sem=sems[0].at[2 * k + j],
            recv_sem=sems[1].at[2 * k + j], device_id=(*n2, c), device_id_type=MESH)
            for k in range(n) for j, slot in enumerate((2 * n2[0] + n2[1], diag))]

    def start(ins, outs, sems):
        for cp in copies(ins, outs, sems):
            cp.start()

    def finish(ins, outs, sems):
        for cp in copies(ins, outs, sems):
            cp.wait()

    out_shapes = [jax.ShapeDtypeStruct(g.shape[1:], g.dtype) for g in chip_sums for _ in range(2)]
    scratch = [pltpu.SemaphoreType.DMA((2 * n,)), pltpu.SemaphoreType.DMA((2 * n,))]
    return list(chip_sums), out_shapes, scratch, start, finish


def _scatter_stage2(passed):
    n = len(passed)

    def copies(ins, outs, sems):
        (x, y, c), n2, n1 = _second_neighbour()
        return [pltpu.make_async_remote_copy(src_ref=ins[k], dst_ref=outs[k], send_sem=sems[0].at[k],
                                             recv_sem=sems[1].at[k], device_id=(*n1, c), device_id_type=MESH)
                for k in range(n)]

    def start(ins, outs, sems):
        for cp in copies(ins, outs, sems):
            cp.start()

    def finish(ins, outs, sems):
        for cp in copies(ins, outs, sems):
            cp.wait()

    out_shapes = [jax.ShapeDtypeStruct(p.shape, p.dtype) for p in passed]
    scratch = [pltpu.SemaphoreType.DMA((n,)), pltpu.SemaphoreType.DMA((n,))]
    return list(passed), out_shapes, scratch, start, finish


def _add_passed(name, own, got, slot):
    _, H, Cc = own.shape
    tr = _row_tile(H, Cc, 4)

    def body(s_ref, o_ref, g_ref, out_ref):
        out_ref[...] = (o_ref[0].astype(F32) + g_ref[...].astype(F32)).astype(BF16)

    grid_spec = pltpu.PrefetchScalarGridSpec(
        num_scalar_prefetch=1, grid=(H // tr,),
        in_specs=[pl.BlockSpec((1, tr, Cc), lambda i, s: (s[0], i, 0)), pl.BlockSpec((tr, Cc), lambda i, s: (i, 0))],
        out_specs=pl.BlockSpec((tr, Cc), lambda i, s: (i, 0)))
    return _pcall(body, name=name, grid_spec=grid_spec, out_shape=jax.ShapeDtypeStruct((H, Cc), BF16),
                  compiler_params=_cparams(("parallel",)))(slot, own, got)


def _sum_stages(name, own, direct, via, place, transposed=False):
    _, H, Cc = own.shape
    tr = LANES if transposed else _row_tile(H, Cc, 4, 1024 * 1024)
    nb = H // tr

    def body(p_ref, own_ref, d_ref, v_ref, o_ref):
        acc = (own_ref[0].astype(F32) + d_ref[...].astype(F32)) + v_ref[...].astype(F32)
        o_ref[...] = acc.T if transposed else acc

    flat = pl.BlockSpec((tr, Cc), lambda i, p: (i, 0))
    out_spec = (pl.BlockSpec((Cc, tr), lambda i, p: (0, p[1] * nb + i)) if transposed
                else pl.BlockSpec((tr, Cc), lambda i, p: (p[1] * nb + i, 0)))
    grid_spec = pltpu.PrefetchScalarGridSpec(
        num_scalar_prefetch=1, grid=(nb,),
        in_specs=[pl.BlockSpec((1, tr, Cc), lambda i, p: (p[0], i, 0)), flat, flat], out_specs=out_spec)
    return _pcall(body, name=name, grid_spec=grid_spec,
                  out_shape=jax.ShapeDtypeStruct((Cc, 2 * H) if transposed else (2 * H, Cc), F32),
                  compiler_params=_cparams(("parallel",)))(place, own, direct, via)


def _join_halves(fulls, axes, small):
    n = len(fulls)
    hs = [f.shape[ax] // 2 for f, ax in zip(fulls, axes)]
    rel = [(dx, dy, dc) for dx in (0, 1) for dy in (0, 1) for dc in (0, 1)][1:]

    def half(ref, k, hc):
        part = pl.ds(hc * hs[k], hs[k])
        return ref.at[:, part] if axes[k] else ref.at[part, :]

    def body(*refs):
        ins, small_in = refs[:n], refs[n]
        outs, small_out = refs[n + 1:2 * n + 1], refs[2 * n + 1]
        send_sems, recv_sems, ssend, srecv, local_sem = refs[2 * n + 2:]
        x, y, c, _ = _place()
        dev = 4 * x + 2 * y + c
        local = pltpu.make_async_copy(small_in, small_out.at[dev], local_sem)
        local.start()
        cps = []
        for k in range(n):
            cp = pltpu.make_async_remote_copy(src_ref=half(ins[k], k, c), dst_ref=half(outs[k], k, c),
                                              send_sem=send_sems.at[k], recv_sem=recv_sems.at[k],
                                              device_id=(x, y, 1 - c), device_id_type=MESH)
            cp.start()
            cps.append(cp)
        for r, (dx, dy, dc) in enumerate(rel):
            cp = pltpu.make_async_remote_copy(src_ref=small_in, dst_ref=small_out.at[dev], send_sem=ssend.at[r],
                                              recv_sem=srecv.at[r], device_id=(x ^ dx, y ^ dy, c ^ dc),
                                              device_id_type=MESH)
            cp.start()
            cps.append(cp)
        for k in range(n):
            land = half(outs[k], k, 1 - c)
            pltpu.make_async_remote_copy(src_ref=land, dst_ref=land, send_sem=send_sems.at[k],
                                         recv_sem=recv_sems.at[k], device_id=(x, y, c), device_id_type=MESH).wait_recv()
        for r, (dx, dy, dc) in enumerate(rel):
            land = small_out.at[4 * (x ^ dx) + 2 * (y ^ dy) + (c ^ dc)]
            pltpu.make_async_remote_copy(src_ref=land, dst_ref=land, send_sem=ssend.at[r], recv_sem=srecv.at[r],
                                         device_id=(x, y, c), device_id_type=MESH).wait_recv()
        for cp in cps:
            cp.wait_send()
        local.wait()

    return _pcall(
        body, name="join_halves", in_specs=[_ANY] * (n + 1), out_specs=[_ANY] * (n + 1),
        out_shape=[jax.ShapeDtypeStruct(f.shape, f.dtype) for f in fulls]
        + [jax.ShapeDtypeStruct((N_DEV,) + small.shape, small.dtype)],
        input_output_aliases={k: k for k in range(n)},
        scratch_shapes=[pltpu.SemaphoreType.DMA((n,)), pltpu.SemaphoreType.DMA((n,)), pltpu.SemaphoreType.DMA((7,)),
                        pltpu.SemaphoreType.DMA((7,)), pltpu.SemaphoreType.DMA],
    )(*fulls, small)


def _local_step(cfg, x2, target, norm_gain, w_my, fb, mu_g, w0, a0, k_k, k_a, r_k, ln_w, ln_b, fng, rest,
                exchange=None):
    T, D, FW, FH, RW, RH, LP, lora = cfg.T, cfg.D, cfg.FW, cfg.FH, cfg.RW, cfg.RH, cfg.LP, cfg.lora
    fb_p = jnp.pad(fb, ((0, 0), (0, LANES - FH)))
    mu = _rwkv_vec_to_my(cfg, mu_g)
    rk = r_k.reshape(1, RW)
    tm = min(1024, T)

    h = _rms_fwd(cfg, x2, norm_gain)
    if len(rest) == 2:
        u, *got = _mm("in_proj", h, w_my, "nn", F32, tm, cfg.tn, 2048, comm=rest[0])
        rest = rest[1](got)
    else:
        u = _mm("in_proj", h, w_my, "nn", F32, tm, cfg.tn, 2048)
    w2, a2, wpf, wpr, wout = rest
    w2p = jnp.pad(w2, ((0, LP - lora), (0, 0)))
    a2p = jnp.pad(a2, ((0, LP - lora), (0, 0)))
    c_cols = _fox_prep(cfg, u, fb_p)
    c_rows = c_cols[:, :FH].T.reshape(FH, 1, T)
    o, lse = _attn_fwd(cfg, u, c_rows)
    oa = _gate_a_fwd(cfg, o, u)
    prep = _rwkv_prep_fwd(cfg, u, mu, w0, w2p, a0, a2p, k_k, k_a)
    r, lw, kp, v, an, b, zb = prep
    toks = [r, lw, kp, v, an, b]
    q_s, yloc, a_m, sloc = _scan_local_fwd(cfg, toks)
    y, ckpt = _scan_carry_fwd(cfg, q_s, yloc, a_m, sloc)
    ob = _rwkv_post_fwd(cfg, y, r, kp, v, zb, ln_w, ln_b, rk)
    pa = _mm("proj_fox", oa, wpf, "nn", F32, tm, 1024, 2048)
    pb = _mm("proj_rwkv", ob, wpr, "nn", F32, tm, 1024, 2048)
    m = _merge_fwd(cfg, pa, pb, u)
    mo = _mm("out_proj", m, wout, "nn", F32, tm, 1024, 2048)
    loss8, dres, dres16, d_fng = _final(cfg, x2, mo, fng.reshape(1, D), target)

    dm = _mm("out_proj_dx", dres16, wout, "nt", F32, tm, 1024, 2048)
    d_wout = _mm("out_proj_dw", m, dres16, "tn", BF16, 1024, 1024, 2048)
    dpa, dpb, du = _merge_bwd(cfg, pa, pb, u, dm)
    doa = _mm("proj_fox_dx", dpa, wpf, "nt", F32, tm, 1024, 2048)
    d_wpf = _mm("proj_fox_dw", oa, dpa, "tn", BF16, 1024, 1024, 2048)
    dob = _mm("proj_rwkv_dx", dpb, wpr, "nt", F32, tm, 1024, 2048)
    d_wpr = _mm("proj_rwkv_dw", ob, dpb, "tn", BF16, 1024, 1024, 2048)

    do, du = _gate_a_bwd(cfg, o, u, doa, du)
    du, dcol = _attn_bwd(cfg, u, c_rows, lse, do, du)
    dc = jnp.pad(-dcol.reshape(FH, T).T, ((0, 0), (0, LANES - FH)))
    df, d_fb = _fox_prep_bwd(cfg, u, fb_p, dc)

    dy, dr_p, dk_p, dv_p, dzb, d_lnw, d_lnb, d_rk = _rwkv_post_bwd(cfg, y, r, kp, v, zb, ln_w, ln_b, rk, dob)
    dq_s, da_m, dsl = _scan_carry_bwd(cfg, q_s, a_m, ckpt, dy)
    early = dict(w_proj_fox=d_wpf, w_proj_rwkv=d_wpr, w_out=d_wout)
    res = _scan_local_bwd(cfg, toks, dq_s, dy, da_m, dsl, [dr_p, dk_p, dv_p], exchange(early) if exchange else None)
    cots, received = res[:6], list(res[6:])
    dus, d_mu, d_w0, d_w2p, d_a0, d_a2p, d_kk, d_ka = _rwkv_prep_bwd(cfg, u, mu, w0, w2p, a0, a2p, k_k, k_a, cots, dzb)
    du = _shift_bwd(cfg, dus, mu, df, du)
    if exchange:
        late = dict(w_in=exchange((h, du, d_w2p[:lora], d_a2p[:lora])))
    else:
        late = dict(w_in=_mm("in_proj_dw", h, du, "tn", BF16, 1024, cfg.tn, 2048), rwkv_w2=d_w2p[:lora],
                    rwkv_a2=d_a2p[:lora])
    tkx = 2 * cfg.tn if cfg.ncol % (2 * cfg.tn) == 0 else cfg.tn
    res = _mm("in_proj_dx", du, w_my, "nt", F32, tm, 1024, tkx, comm=exchange(late) if exchange else None)
    dh = res[0] if exchange else res
    big = dict(early, **late)
    res = _rms_bwd(cfg, x2, norm_gain, dh, dres, exchange(list(res[1:])) if exchange else None)
    gx, d_ng = res[:2]
    received += list(res[2:])

    small = dict(norm_gain=d_ng, fox_forget_bias=d_fb[:, :FH], rwkv_shift_mix=_rwkv_vec_from_my(cfg, d_mu),
                 rwkv_w0=d_w0, rwkv_a0=d_a0, rwkv_k_k=d_kk, rwkv_k_a=d_ka, rwkv_r_k=d_rk, rwkv_ln_w=d_lnw,
                 rwkv_ln_b=d_lnb, final_norm_gain=d_fng)
    return loss8[0, 0], gx, small, big, received


_SMALL = ["norm_gain", "fox_forget_bias", "rwkv_shift_mix", "rwkv_w0", "rwkv_a0", "rwkv_k_k", "rwkv_k_a", "rwkv_r_k",
          "rwkv_ln_w", "rwkv_ln_b", "final_norm_gain"]
_WEIGHTS = ["norm_gain", "w_in", "fox_forget_bias", "rwkv_shift_mix", "rwkv_w0", "rwkv_w2", "rwkv_a0", "rwkv_a2",
            "rwkv_k_k", "rwkv_k_a", "rwkv_r_k", "rwkv_ln_w", "rwkv_ln_b", "w_proj_fox", "w_proj_rwkv", "w_out",
            "final_norm_gain"]


def _pack_small(arrs):
    parts = []
    for a in arrs:
        f = a.reshape(-1)
        parts.append(jnp.pad(f, (0, (-f.shape[0]) % LANES)))
    flat = jnp.concatenate(parts)
    rows = flat.shape[0] // LANES
    flat = jnp.pad(flat, (0, ((-rows) % 8) * LANES))
    return flat.reshape(-1, LANES)


def _unpack_small(packed, shapes):
    flat = packed.reshape(-1)
    out, pos = [], 0
    for s in shapes:
        n = int(np.prod(s))
        out.append(flat[pos:pos + n].reshape(s))
        pos += n + ((-n) % LANES)
    return out


def _shard_major(a, axis):
    parts = jnp.split(a, N_CHIPS, axis=axis)
    return jnp.stack(parts, axis=0)


def kernel(x, norm_gain, w_in, fox_forget_bias, rwkv_shift_mix, rwkv_w0, rwkv_w2, rwkv_a0, rwkv_a2, rwkv_k_k, rwkv_k_a, rwkv_r_k, rwkv_ln_w, rwkv_ln_b, w_proj_fox, w_proj_rwkv, w_out, final_norm_gain, loss_target, m_norm_gain, m_w_in, m_fox_forget_bias, m_rwkv_shift_mix, m_rwkv_w0, m_rwkv_w2, m_rwkv_a0, m_rwkv_a2, m_rwkv_k_k, m_rwkv_k_a, m_rwkv_r_k, m_rwkv_ln_w, m_rwkv_ln_b, m_w_proj_fox, m_w_proj_rwkv, m_w_out, m_final_norm_gain, v_norm_gain, v_w_in, v_fox_forget_bias, v_rwkv_shift_mix, v_rwkv_w0, v_rwkv_w2, v_rwkv_a0, v_rwkv_a2, v_rwkv_k_k, v_rwkv_k_a, v_rwkv_r_k, v_rwkv_ln_w, v_rwkv_ln_b, v_w_proj_fox, v_w_proj_rwkv, v_w_out, v_final_norm_gain):
    args = dict(locals())
    T, D = x.shape[1], x.shape[2]
    lora = rwkv_w2.shape[1]
    cfg = _Cfg(T, D, lora)
    RW = cfg.RW
    c_idx = lax.axis_index("c").astype(jnp.int32).reshape(1)
    me_chip = (2 * lax.axis_index("x") + lax.axis_index("y")).astype(jnp.int32)
    place = jnp.concatenate([me_chip.reshape(1), c_idx])

    w_in_s = w_in[0].astype(BF16)
    lora_s = jnp.concatenate([rwkv_w2[0], rwkv_a2[0]], axis=0)
    own_slot = lambda g, own: lax.dynamic_update_slice(g, own[None], (me_chip, 0, 0))
    w_my = _shards_to_my_layout(cfg, own_slot(_gather_weights([w_in_s])[0], w_in_s))
    mine = [_cast_bf16("cast_w_proj_fox", w_proj_fox[0]), _cast_bf16("cast_w_proj_rwkv", w_proj_rwkv[0]),
            _cast_bf16("cast_w_out", w_out[0]), lora_s]

    def unpack(gathered):
        g_wpf, g_wpr, g_out, g_lora = [own_slot(g, own) for g, own in zip(gathered, mine)]
        lo = g_lora.transpose(1, 0, 2).reshape(2 * lora, RW)
        return (lo[:lora], lo[lora:], g_wpf.transpose(1, 0, 2).reshape(RW, D),
                g_wpr.transpose(1, 0, 2).reshape(RW, D), g_out.reshape(D, D))

    early, late = ["w_proj_fox", "w_proj_rwkv", "w_out"], ["w_in", "lora"]
    names = early + late
    chip_sums, direct = {}, {}
    n1_slot = (2 * (lax.axis_index("x") ^ (1 - lax.axis_index("c")))
               + (lax.axis_index("y") ^ lax.axis_index("c"))).astype(jnp.int32).reshape(1)

    def exchange(got):
        if isinstance(got, tuple):
            h, du, d_w2, d_a2 = got
            c, half = lax.axis_index("c"), D // 2
            cols = lambda base: lax.dynamic_slice_in_dim(h, base * half, half, axis=1)
            lora_g = _shard_major(jnp.concatenate([d_w2, d_a2], axis=0).astype(BF16), 1)
            lora_rows = lambda base: lax.dynamic_slice_in_dim(lora_g, base * lora, lora, axis=1).reshape(-1, RW // 4)
            tiles = (BF16, min(1024, half), cfg.tn, 2048)
            sent = _mm("in_proj_dw_sibling", cols(1 - c), du, "tn", *tiles)
            kept, got_w, got_l = _mm("in_proj_dw", cols(c), du, "tn", *tiles,
                                     comm=_swap_with_sibling([sent, lora_rows(1 - c)]))
            return (_add_pair("add_halves_w_in", kept, got_w),
                    _add_pair("add_halves_lora", lora_rows(c), got_l).reshape(N_CHIPS, lora, RW // 4))
        if isinstance(got, dict):
            if "w_in" in got:
                sums = [_my_layout_to_shards(cfg, got["w_in"][0]), got["w_in"][1]]
                chip_sums.update(zip(late, sums))
                return _scatter_stage1(sums)
            gs = [_shard_major(got["w_proj_fox"], 1), _shard_major(got["w_proj_rwkv"], 1),
                  _shard_major(got["w_out"], 0)]
            recv1 = _exchange_halves("exchange_halves_" + early[0], gs)
            sums = [_add_halves("add_halves_" + nm, g, r, c_idx) for nm, g, r in zip(early, gs, recv1)]
            chip_sums.update(zip(early, sums))
            return _scatter_to_owners(sums)
        direct.update(zip(late, got[0::2]))
        return _scatter_stage2([_add_passed("add_passed_" + nm, chip_sums[nm], g, n1_slot)
                                for nm, g in zip(late, got[1::2])])

    loss_dev, gx, small, _, recv2 = _local_step(
        cfg, x[0], loss_target[0], norm_gain, w_my, fox_forget_bias, rwkv_shift_mix, rwkv_w0, rwkv_a0, rwkv_k_k,
        rwkv_k_a, rwkv_r_k, rwkv_ln_w, rwkv_ln_b, final_norm_gain, (_gather_parts(mine), unpack), exchange)
    loss = lax.psum(loss_dev, ("x", "y", "c"))

    small_shapes = [args[nm].shape for nm in _SMALL]
    packed = _pack_small([small[nm] for nm in _SMALL])
    reduced = [_sum_chips("sum_chips_" + nm, r, chip_sums[nm], place) for nm, r in zip(early, recv2[:3])]
    reduced += [_sum_stages("sum_stages_" + nm, chip_sums[nm], direct[nm], via, place, transposed=nm == "w_in")
                for nm, via in zip(late, recv2[3:])]
    *joined, small_all = _join_halves(reduced, [int(nm == "w_in") for nm in names], packed)
    g_small = _sum_slots("sum_small", small_all)

    grads = dict(zip(_SMALL, _unpack_small(g_small, small_shapes)))
    grads.update({nm: g[None] for nm, g in zip(names, joined) if nm not in ("lora", "w_in")})
    g_lora_f = joined[names.index("lora")]
    grads["rwkv_w2"] = g_lora_f[None, :lora]
    grads["rwkv_a2"] = g_lora_f[None, lora:]

    delta, new_m, new_v = {}, {}, {}
    w_small = _pack_small([args[nm] for nm in _SMALL])
    m_small = _pack_small([args["m_" + nm] for nm in _SMALL])
    v_small = _pack_small([args["v_" + nm] for nm in _SMALL])
    d_s, m_s, v_s = _adamw("adamw_small", w_small, g_small, m_small, v_small)
    for tgt, pk in ((delta, d_s), (new_m, m_s), (new_v, v_s)):
        tgt.update(zip(_SMALL, _unpack_small(pk, small_shapes)))
    t_out = _adamw("adamw_w_in", w_in[0].T, joined[names.index("w_in")], m_w_in[0].T, v_w_in[0].T, copy_grad=True)
    delta["w_in"], new_m["w_in"], new_v["w_in"], grads["w_in"] = [t.T[None] for t in t_out]
    for nm in ("w_proj_fox", "w_proj_rwkv", "w_out", "rwkv_w2", "rwkv_a2"):
        shp = args[nm].shape
        two_d = (shp[1], shp[2])
        d_b, m_b, v_b = _adamw("adamw_" + nm, args[nm].reshape(two_d), grads[nm].reshape(two_d),
                               args["m_" + nm].reshape(two_d), args["v_" + nm].reshape(two_d))
        delta[nm], new_m[nm], new_v[nm] = d_b.reshape(shp), m_b.reshape(shp), v_b.reshape(shp)

    return (loss, gx[None], *[grads[n] for n in _WEIGHTS], *[delta[n] for n in _WEIGHTS],
            *[new_m[n] for n in _WEIGHTS], *[new_v[n] for n in _WEIGHTS])
```

```python
import functools

import numpy as np
import jax
import jax.numpy as jnp
from jax import lax
from jax.experimental import pallas as pl
from jax.experimental.pallas import tpu as pltpu

F32 = jnp.float32
BF16 = jnp.bfloat16
HI = lax.Precision.HIGHEST
MESH = pl.DeviceIdType.MESH

FOX_HEAD_DIM = 128
RWKV_HEAD_DIM = 64
RMS_EPS = 1e-6
GN_EPS = 64e-5
L2_EPS = 1e-12
ADAM_LR = 0.001
ADAM_B1 = 0.9
ADAM_B2 = 0.999
ADAM_EPS = 1e-08
ADAM_WD = 0.01
ADAM_STEP = 10

LANES = 128
VMEM_LIMIT = 56 * 1024 * 1024
SCAN_CHUNK = 64
SCAN_HEADS_PER_STEP = 16
SCAN_PASSES = (3, 1, 1)
N_CHIPS = 4
N_DEV = 8

_pcall = pl.pallas_call


def _cparams(sem=None):
    return pltpu.CompilerParams(dimension_semantics=sem, vmem_limit_bytes=VMEM_LIMIT)


def _softplus(x):
    return jnp.maximum(x, 0.0) + jnp.log(1.0 + jnp.exp(-jnp.abs(x)))


def _silu(z):
    return z * jax.nn.sigmoid(z)


def _rmsn(x, g):
    return x * lax.rsqrt(jnp.mean(x * x, axis=-1, keepdims=True) + RMS_EPS) * g


def _dot(a, b, dims="nn", precision=None):
    dn = {"nn": (((1,), (0,)), ((), ())), "nt": (((1,), (1,)), ((), ())), "tn": (((0,), (0,)), ((), ()))}[dims]
    return lax.dot_general(a, b, dn, precision=precision, preferred_element_type=F32)


def _split_bf16(x):
    hi = x.astype(BF16)
    return hi, (x - hi.astype(F32)).astype(BF16)


def _bdot_raw(a, b, ca, cb, passes):
    dn = (((ca,), (cb,)), ((0,), (0,)))
    mm = lambda p, q: lax.dot_general(p, q, dn, preferred_element_type=F32)
    if passes == 1:
        return mm(a.astype(BF16), b.astype(BF16))
    ah, al = _split_bf16(a)
    bh, bl = _split_bf16(b)
    return mm(ah, bh) + (mm(ah, bl) + mm(al, bh))


@functools.partial(jax.custom_vjp, nondiff_argnums=(2, 3, 4))
def _bdot_p(a, b, ca, cb, passes):
    return _bdot_raw(a, b, ca, cb, passes)


def _bdot_fwd(a, b, ca, cb, passes):
    return _bdot_raw(a, b, ca, cb, passes), (a, b)


def _bdot_bwd(ca, cb, passes, res, g):
    a, b = res
    if (ca, cb) == (2, 1):
        return _bdot_p(g, b, 2, 2, passes), _bdot_p(a, g, 1, 1, passes)
    if (ca, cb) == (2, 2):
        return _bdot_p(g, b, 2, 1, passes), _bdot_p(g, a, 1, 1, passes)
    assert (ca, cb) == (1, 1)
    return _bdot_p(b, g, 2, 2, passes), _bdot_p(a, g, 2, 1, passes)


_bdot_p.defvjp(_bdot_fwd, _bdot_bwd)


def _bdot(a, b, ca, cb, passes=3):
    return _bdot_p(a, b, ca, cb, passes)


def _dot3(a, b):
    return _bdot(a[None], b[None], 2, 1)[0]


@jax.custom_vjp
def _xdot(x, m, mt):
    hi, lo = _split_bf16(x)
    m16 = m.astype(BF16)
    return _dot(hi, m16) + _dot(lo, m16)


def _xdot_fwd(x, m, mt):
    return _xdot(x, m, mt), (m, mt)


def _xdot_bwd(res, g):
    m, mt = res
    return _xdot(g, mt, m), jnp.zeros_like(m), jnp.zeros_like(mt)


_xdot.defvjp(_xdot_fwd, _xdot_bwd)


class _Cfg:
    def __init__(self, T, D, lora):
        self.T, self.D, self.lora = T, D, lora
        self.FW = D // 2
        self.FH = self.FW // FOX_HEAD_DIM
        self.RW = D // 2
        self.RH = self.RW // RWKV_HEAD_DIM
        self.LP = -(-lora // LANES) * LANES
        self.o_fox = 0
        self.o_rwkv = 4 * self.FW
        self.o_gate = self.o_rwkv + 4 * self.RW
        self.o_f = self.o_gate + 2 * D
        self.o_wd = self.o_f + LANES
        self.o_ad = self.o_wd + self.LP
        end = self.o_ad + self.LP
        self.tn = 1280 if D >= 2048 else LANES
        self.ncol = -(-end // self.tn) * self.tn
        self.in_cols = 4 * self.FW + self.FH + 4 * self.RW + 2 * lora + 2 * D
        self.scp = -(-(self.in_cols // N_CHIPS) // LANES) * LANES
        self.rseg = 4 * self.RW + 2 * self.LP
        self.C = min(SCAN_CHUNK, T)
        self.tr = min(256, T)
        self.hb = min(SCAN_HEADS_PER_STEP, self.RH)

    def segments(self):
        FW, FH, RW, lo, D = self.FW, self.FH, self.RW, self.lora, self.D
        g_f = 4 * FW
        g_r = g_f + FH
        g_wd = g_r + 4 * RW
        g_ad = g_wd + lo
        g_g = g_ad + lo
        dh = FOX_HEAD_DIM
        qkv = [(j * FW + h * dh, dh, (3 * h + j) * dh) for h in range(FH) for j in range(3)]
        return qkv + [(3 * FW, FW, 3 * FW), (g_f, FH, self.o_f), (g_r, 4 * RW, self.o_rwkv), (g_wd, lo, self.o_wd),
                      (g_ad, lo, self.o_ad), (g_g, 2 * D, self.o_gate)]


def _shards_to_my_layout(cfg, g):
    R, sc = g.shape[1], g.shape[2]
    segs = sorted(cfg.segments(), key=lambda s: s[2])
    parts, pos = [], 0
    for g0, w, m0 in segs:
        if m0 > pos:
            parts.append(jnp.zeros((R, m0 - pos), g.dtype))
        for s in range(N_CHIPS):
            lo, hi = max(g0, s * sc), min(g0 + w, (s + 1) * sc)
            if lo < hi:
                parts.append(g[s, :, lo - s * sc:hi - s * sc])
        pos = m0 + w
    if cfg.ncol > pos:
        parts.append(jnp.zeros((R, cfg.ncol - pos), g.dtype))
    return jnp.concatenate(parts, axis=1)


def _my_layout_to_shards(cfg, wm):
    sc, R = cfg.in_cols // N_CHIPS, wm.shape[0]
    segs = sorted(cfg.segments(), key=lambda s: s[0])
    shards = []
    for s in range(N_CHIPS):
        parts = []
        for g0, w, m0 in segs:
            lo, hi = max(g0, s * sc), min(g0 + w, (s + 1) * sc)
            if lo < hi:
                parts.append(wm[:, m0 + lo - g0:m0 + hi - g0])
        parts.append(jnp.zeros((R, cfg.scp - sc), wm.dtype))
        shards.append(jnp.concatenate(parts, axis=1))
    return jnp.stack(shards, axis=0)


def _rwkv_vec_to_my(cfg, v):
    RW4, lo, LP = 4 * cfg.RW, cfg.lora, cfg.LP
    z = jnp.zeros((1, LP - lo), v.dtype)
    return jnp.concatenate([v[:, :RW4], v[:, RW4:RW4 + lo], z, v[:, RW4 + lo:], z], axis=1)


def _rwkv_vec_from_my(cfg, v):
    RW4, lo, LP = 4 * cfg.RW, cfg.lora, cfg.LP
    return jnp.concatenate([v[:, :RW4], v[:, RW4:RW4 + lo], v[:, RW4 + LP:RW4 + LP + lo]], axis=1)


def _comm_at(comm, which, steps, cin, cout, scr):
    if not comm or len(comm) <= which:
        return
    lin, total = 0, 1
    for d, n in enumerate(steps):
        lin = lin * n + pl.program_id(d)
        total *= n
    pl.when(lin == {3: 0, 4: total - 1, 5: total // 2}[which])(lambda: comm[which](cin, cout, scr))


def _mm(name, a, b, dims, out_dtype, tm, tn, tk, comm=None):
    (M, K) = a.shape if dims != "tn" else a.shape[::-1]
    N = b.shape[0] if dims == "nt" else b.shape[1]
    tm, tn, tk = min(tm, M), min(tn, N), min(tk, K)
    assert M % tm == 0 and N % tn == 0 and K % tk == 0, (name, M, N, K, tm, tn, tk)
    nk = K // tk
    steps = (M // tm, N // tn, nk)
    c_in, c_out, c_scr = comm[:3] if comm else ([], [], [])
    if dims == "nn":
        a_spec = pl.BlockSpec((tm, tk), lambda i, j, k: (i, k))
        b_spec = pl.BlockSpec((tk, tn), lambda i, j, k: (k, j))
    elif dims == "nt":
        a_spec = pl.BlockSpec((tm, tk), lambda i, j, k: (i, k))
        b_spec = pl.BlockSpec((tn, tk), lambda i, j, k: (j, k))
    else:
        a_spec = pl.BlockSpec((tk, tm), lambda i, j, k: (k, i))
        b_spec = pl.BlockSpec((tk, tn), lambda i, j, k: (k, j))

    n_acc = 1 if nk > 1 else 0

    def body(a_ref, b_ref, *rest):
        cin, o_ref = rest[:len(c_in)], rest[len(c_in)]
        cout = rest[len(c_in) + 1:len(c_in) + 1 + len(c_out)]
        scr = rest[len(c_in) + 1 + len(c_out):]
        _comm_at(comm, 3, steps, cin, cout, scr[n_acc:])
        if nk == 1:
            o_ref[...] = _dot(a_ref[...], b_ref[...], dims).astype(o_ref.dtype)
        else:
            acc_ref, k = scr[0], pl.program_id(2)

            @pl.when(k == 0)
            def _():
                acc_ref[...] = jnp.zeros_like(acc_ref)

            acc_ref[...] += _dot(a_ref[...], b_ref[...], dims)

            @pl.when(k == nk - 1)
            def _():
                o_ref[...] = acc_ref[...].astype(o_ref.dtype)

        _comm_at(comm, 5, steps, cin, cout, scr[n_acc:])
        _comm_at(comm, 4, steps, cin, cout, scr[n_acc:])

    res = _pcall(
        body, name=name, grid=steps,
        in_specs=[a_spec, b_spec] + [_ANY] * len(c_in),
        out_specs=[pl.BlockSpec((tm, tn), lambda i, j, k: (i, j))] + [_ANY] * len(c_out),
        out_shape=[jax.ShapeDtypeStruct((M, N), out_dtype)] + list(c_out),
        scratch_shapes=([pltpu.VMEM((tm, tn), F32)] if nk > 1 else []) + list(c_scr),
        compiler_params=_cparams(("arbitrary",) * 3 if comm else ("parallel", "parallel", "arbitrary")),
    )(a, b, *c_in)
    return res if comm else res[0]


def _tile(tr, w, cb=0):
    return pl.BlockSpec((tr, w), lambda i: (i, cb))


def _const(shape):
    nd = len(shape)
    return pl.BlockSpec(shape, lambda i: (0,) * nd)


def _acc_store(i, ref, val):
    @pl.when(i == 0)
    def _():
        ref[...] = val

    @pl.when(i > 0)
    def _():
        ref[...] += val


def _rms_fwd(cfg, x2, g):
    T, D, tr = cfg.T, cfg.D, cfg.tr

    def body(x_ref, g_ref, h_ref):
        h_ref[...] = _rmsn(x_ref[...], g_ref[...]).astype(BF16)

    return _pcall(body, name="rms_fwd", grid=(T // tr,), in_specs=[_tile(tr, D), _const((1, D))],
                  out_specs=_tile(tr, D), out_shape=jax.ShapeDtypeStruct((T, D), BF16),
                  compiler_params=_cparams(("parallel",)))(x2, g)


def _rms_bwd(cfg, x2, g, dh, dres, comm=None):
    T, D, tr = cfg.T, cfg.D, cfg.tr
    c_in, c_out, c_scr = comm[:3] if comm else ([], [], [])
    steps = (T // tr,)

    def body(x_ref, g_ref, dh_ref, dres_ref, *rest):
        cin, (gx_ref, dg_ref) = rest[:len(c_in)], rest[len(c_in):len(c_in) + 2]
        cout, scr = rest[len(c_in) + 2:len(c_in) + 2 + len(c_out)], rest[len(c_in) + 2 + len(c_out):]
        _comm_at(comm, 3, steps, cin, cout, scr)
        _, vjp = jax.vjp(_rmsn, x_ref[...], g_ref[...])
        dx, dg = vjp(dh_ref[...])
        gx_ref[...] = dx + dres_ref[...]
        _acc_store(pl.program_id(0), dg_ref, dg)
        _comm_at(comm, 4, steps, cin, cout, scr)

    return _pcall(body, name="rms_bwd", grid=steps,
                  in_specs=[_tile(tr, D), _const((1, D)), _tile(tr, D), _tile(tr, D)] + [_ANY] * len(c_in),
                  out_specs=[_tile(tr, D), _const((1, D))] + [_ANY] * len(c_out),
                  out_shape=[jax.ShapeDtypeStruct((T, D), F32), jax.ShapeDtypeStruct((1, D), F32)] + list(c_out),
                  scratch_shapes=list(c_scr), compiler_params=_cparams(("arbitrary",)))(x2, g, dh, dres, *c_in)


def _final(cfg, x2, mo, fg, target):
    T, D, tr = cfg.T, cfg.D, cfg.tr

    def loss_fn(hres, g, tgt):
        err = _rmsn(hres, g) - tgt
        return 0.5 * jnp.sum(jnp.mean(err * err, axis=-1, keepdims=True), axis=0, keepdims=True)

    def body(x_ref, mo_ref, g_ref, t_ref, loss_ref, dres_ref, dres16_ref, dg_ref):
        hres = x_ref[...] + mo_ref[...]
        loss, vjp = jax.vjp(functools.partial(loss_fn, tgt=t_ref[...]), hres, g_ref[...])
        dres, dg = vjp(jnp.ones((1, 1), F32))
        dres_ref[...] = dres
        dres16_ref[...] = dres.astype(BF16)
        i = pl.program_id(0)
        _acc_store(i, dg_ref, dg)
        _acc_store(i, loss_ref, jnp.broadcast_to(loss, (8, LANES)))

    return _pcall(body, name="final_loss", grid=(T // tr,),
                  in_specs=[_tile(tr, D), _tile(tr, D), _const((1, D)), _tile(tr, D)],
                  out_specs=[_const((8, LANES)), _tile(tr, D), _tile(tr, D), _const((1, D))],
                  out_shape=[jax.ShapeDtypeStruct((8, LANES), F32), jax.ShapeDtypeStruct((T, D), F32),
                             jax.ShapeDtypeStruct((T, D), BF16), jax.ShapeDtypeStruct((1, D), F32)],
                  compiler_params=_cparams(("arbitrary",)))(x2, mo, fg, target)


def _merge_fn(pa, pb, ga, gb):
    return jax.nn.sigmoid(ga) * pa + jax.nn.sigmoid(gb) * pb


def _merge_fwd(cfg, pa, pb, u):
    T, D, tr = cfg.T, cfg.D, cfg.tr
    cga, cgb = cfg.o_gate // D, cfg.o_gate // D + 1

    def body(pa_ref, pb_ref, ga_ref, gb_ref, m_ref):
        m_ref[...] = _merge_fn(pa_ref[...], pb_ref[...], ga_ref[...], gb_ref[...]).astype(BF16)

    return _pcall(body, name="merge_fwd", grid=(T // tr,),
                  in_specs=[_tile(tr, D), _tile(tr, D), _tile(tr, D, cga), _tile(tr, D, cgb)],
                  out_specs=_tile(tr, D), out_shape=jax.ShapeDtypeStruct((T, D), BF16),
                  compiler_params=_cparams(("parallel",)))(pa, pb, u, u)


def _merge_bwd(cfg, pa, pb, u, dm):
    T, D, tr = cfg.T, cfg.D, cfg.tr
    cga, cgb = cfg.o_gate // D, cfg.o_gate // D + 1

    def body(pa_ref, pb_ref, ga_ref, gb_ref, dm_ref, dpa_ref, dpb_ref, dg_ref):
        _, vjp = jax.vjp(_merge_fn, pa_ref[...], pb_ref[...], ga_ref[...], gb_ref[...])
        dpa, dpb, dga, dgb = vjp(dm_ref[...])
        dpa_ref[...] = dpa.astype(BF16)
        dpb_ref[...] = dpb.astype(BF16)
        dg_ref[:, :D] = dga.astype(BF16)
        dg_ref[:, D:] = dgb.astype(BF16)

    return _pcall(body, name="merge_bwd", grid=(T // tr,),
                  in_specs=[_tile(tr, D), _tile(tr, D), _tile(tr, D, cga), _tile(tr, D, cgb), _tile(tr, D)],
                  out_specs=[_tile(tr, D), _tile(tr, D), _tile(tr, 2 * D, cfg.o_gate // (2 * D))],
                  out_shape=[jax.ShapeDtypeStruct((T, D), BF16), jax.ShapeDtypeStruct((T, D), BF16),
                             jax.ShapeDtypeStruct((T, cfg.ncol), BF16)],
                  compiler_params=_cparams(("parallel",)))(pa, pb, u, u, dm)


def _gate_fn(o, z):
    return o * _silu(z)


def _gate_a_fwd(cfg, o, u):
    T, FW, tr = cfg.T, cfg.FW, cfg.tr

    def body(o_ref, z_ref, oa_ref):
        oa_ref[...] = _gate_fn(o_ref[...], z_ref[...]).astype(BF16)

    return _pcall(body, name="gate_a_fwd", grid=(T // tr,), in_specs=[_tile(tr, FW), _tile(tr, FW, 3)],
                  out_specs=_tile(tr, FW), out_shape=jax.ShapeDtypeStruct((T, FW), BF16),
                  compiler_params=_cparams(("parallel",)))(o, u)


def _gate_a_bwd(cfg, o, u, doa, du):
    T, FW, tr = cfg.T, cfg.FW, cfg.tr

    def body(o_ref, z_ref, doa_ref, du_in, do_ref, dz_ref):
        _, vjp = jax.vjp(_gate_fn, o_ref[...], z_ref[...])
        do, dz = vjp(doa_ref[...])
        do_ref[...] = do
        dz_ref[...] = dz.astype(BF16)

    return _pcall(body, name="gate_a_bwd", grid=(T // tr,),
                  in_specs=[_tile(tr, FW), _tile(tr, FW, 3), _tile(tr, FW), _ANY],
                  out_specs=[_tile(tr, FW), _tile(tr, FW, 3)],
                  out_shape=[jax.ShapeDtypeStruct((T, FW), F32), jax.ShapeDtypeStruct(du.shape, BF16)],
                  input_output_aliases={3: 1},
                  compiler_params=_cparams(("parallel",)))(o, u, doa, du)


def _fox_prep(cfg, u, fb):
    T, tr = cfg.T, cfg.tr
    cf = cfg.o_f // LANES

    def body(f_ref, fb_ref, c_ref, carry_ref):
        i = pl.program_id(0)

        @pl.when(i == 0)
        def _():
            carry_ref[...] = jnp.zeros_like(carry_ref)

        lf = -_softplus(-(f_ref[...] + fb_ref[...]))
        r = lax.broadcasted_iota(jnp.int32, (tr, tr), 0)
        c = lax.broadcasted_iota(jnp.int32, (tr, tr), 1)
        tri = (r >= c).astype(F32)
        c_ref[...] = _dot(tri, lf, precision=HI) + carry_ref[...]
        carry_ref[...] += jnp.sum(lf, axis=0, keepdims=True)

    return _pcall(body, name="fox_prep", grid=(T // tr,), in_specs=[_tile(tr, LANES, cf), _const((1, LANES))],
                  out_specs=_tile(tr, LANES), out_shape=jax.ShapeDtypeStruct((T, LANES), F32),
                  scratch_shapes=[pltpu.VMEM((1, LANES), F32)], compiler_params=_cparams(("arbitrary",)))(u, fb)


def _fox_prep_bwd(cfg, u, fb, dc):
    T, tr = cfg.T, cfg.tr
    cf = cfg.o_f // LANES
    nb = T // tr

    def body(f_ref, fb_ref, dc_ref, df_ref, dfb_ref, carry_ref):
        i = pl.program_id(0)

        @pl.when(i == 0)
        def _():
            carry_ref[...] = jnp.zeros_like(carry_ref)

        dc = dc_ref[...]
        r = lax.broadcasted_iota(jnp.int32, (tr, tr), 0)
        c = lax.broadcasted_iota(jnp.int32, (tr, tr), 1)
        triu = (r <= c).astype(F32)
        dlf = _dot(triu, dc, precision=HI) + carry_ref[...]
        carry_ref[...] += jnp.sum(dc, axis=0, keepdims=True)
        dz = dlf * jax.nn.sigmoid(-(f_ref[...] + fb_ref[...]))
        df_ref[...] = dz.astype(BF16)
        _acc_store(i, dfb_ref, jnp.sum(dz, axis=0, keepdims=True))

    rev = lambda i: (nb - 1 - i, 0)
    return _pcall(body, name="fox_prep_bwd", grid=(nb,),
                  in_specs=[pl.BlockSpec((tr, LANES), lambda i: (nb - 1 - i, cf)), _const((1, LANES)),
                            pl.BlockSpec((tr, LANES), rev)],
                  out_specs=[pl.BlockSpec((tr, LANES), rev), _const((1, LANES))],
                  out_shape=[jax.ShapeDtypeStruct((T, LANES), BF16), jax.ShapeDtypeStruct((1, LANES), F32)],
                  scratch_shapes=[pltpu.VMEM((1, LANES), F32)], compiler_params=_cparams(("arbitrary",)))(u, fb, dc)


def _attn_logits(q_ref, k_ref, c_ref, tq, te):
    q = q_ref[...].astype(BF16)
    scale = FOX_HEAD_DIM ** -0.5
    part = lambda k0, k1: _dot(q, k_ref[k0:k1, :].astype(BF16), "nt") * scale - c_ref[0, :, k0:k1]
    row = lax.broadcasted_iota(jnp.int32, (tq, tq), 0)
    col = lax.broadcasted_iota(jnp.int32, (tq, tq), 1)
    own = ((te - tq, te), jnp.where(col <= row, part(te - tq, te), -1e30))
    return [((0, te - tq), part(0, te - tq)), own] if te > tq else [own]


def _per_query_tile(i, nq, tq, fn):
    for ii in range(nq):
        pl.when(i == ii)(functools.partial(fn, (ii + 1) * tq))


def _attn_fwd(cfg, u, c_rows):
    T, FW, FH = cfg.T, cfg.FW, cfg.FH
    tq = min(256, T)
    dh = FOX_HEAD_DIM

    def body(q_ref, k_ref, v_ref, c_ref, o_ref, lse_ref):
        i = pl.program_id(1)

        def tile(te):
            parts = _attn_logits(q_ref, k_ref, c_ref, tq, te)
            m = functools.reduce(jnp.maximum, [jnp.max(s, axis=1, keepdims=True) for _, s in parts])
            l, acc = 0.0, 0.0
            for (k0, k1), s in parts:
                p = jnp.exp(s - m)
                l = l + jnp.sum(p, axis=1, keepdims=True)
                acc = acc + _dot(p.astype(BF16), v_ref[k0:k1, :].astype(BF16))
            o_ref[...] = acc / l
            lse_ref[0] = m + jnp.log(l)

        _per_query_tile(i, T // tq, tq, tile)

    return _pcall(
        body, name="fox_attn_fwd", grid=(FH, T // tq),
        in_specs=[pl.BlockSpec((tq, dh), lambda h, i: (i, 3 * h)), pl.BlockSpec((T, dh), lambda h, i: (0, 3 * h + 1)),
                  pl.BlockSpec((T, dh), lambda h, i: (0, 3 * h + 2)), pl.BlockSpec((1, 1, T), lambda h, i: (h, 0, 0))],
        out_specs=[pl.BlockSpec((tq, dh), lambda h, i: (i, h)), pl.BlockSpec((1, tq, 1), lambda h, i: (h, i, 0))],
        out_shape=[jax.ShapeDtypeStruct((T, FW), F32), jax.ShapeDtypeStruct((FH, T, 1), F32)],
        compiler_params=_cparams(("parallel", "arbitrary")),
    )(u, u, u, c_rows)


def _attn_bwd(cfg, u, c_rows, lse, do, du):
    T, FW, FH = cfg.T, cfg.FW, cfg.FH
    tq = min(256, T)
    nq = T // tq
    dh = FOX_HEAD_DIM
    scale = dh ** -0.5

    def body(q_ref, k_ref, v_ref, c_ref, lse_ref, do_ref, du_in, du_ref, dcol_ref, dk_acc, dv_acc):
        i = pl.program_id(1)

        @pl.when(i == 0)
        def _():
            dk_acc[...] = jnp.zeros_like(dk_acc)
            dv_acc[...] = jnp.zeros_like(dv_acc)
            dcol_ref[...] = jnp.zeros_like(dcol_ref)

        def tile(te):
            lse, q16, do16 = lse_ref[0], q_ref[...].astype(BF16), do_ref[...].astype(BF16)
            probs = [(ks, jnp.exp(s - lse)) for ks, s in _attn_logits(q_ref, k_ref, c_ref, tq, te)]
            dps = [_dot(do16, v_ref[k0:k1, :].astype(BF16), "nt") for (k0, k1), _ in probs]
            delta = sum(jnp.sum(p * dp, axis=1, keepdims=True) for (_, p), dp in zip(probs, dps))
            dq = 0.0
            for ((k0, k1), p), dp in zip(probs, dps):
                ds = p * (dp - delta)
                ds16 = ds.astype(BF16)
                dq = dq + _dot(ds16, k_ref[k0:k1, :].astype(BF16))
                dk_acc[k0:k1, :] += _dot(ds16, q16, "tn") * scale
                dv_acc[k0:k1, :] += _dot(p.astype(BF16), do16, "tn")
                dcol_ref[0, :, k0:k1] += jnp.sum(ds, axis=0, keepdims=True)
            du_ref[te - tq:te, 0:dh] = (dq * scale).astype(BF16)

        _per_query_tile(i, nq, tq, tile)

        @pl.when(i == nq - 1)
        def _():
            du_ref[:, dh:2 * dh] = dk_acc[...].astype(BF16)
            du_ref[:, 2 * dh:3 * dh] = dv_acc[...].astype(BF16)

    return _pcall(
        body, name="fox_attn_bwd", grid=(FH, nq),
        in_specs=[pl.BlockSpec((tq, dh), lambda h, i: (i, 3 * h)), pl.BlockSpec((T, dh), lambda h, i: (0, 3 * h + 1)),
                  pl.BlockSpec((T, dh), lambda h, i: (0, 3 * h + 2)), pl.BlockSpec((1, 1, T), lambda h, i: (h, 0, 0)),
                  pl.BlockSpec((1, tq, 1), lambda h, i: (h, i, 0)), pl.BlockSpec((tq, dh), lambda h, i: (i, h)), _ANY],
        out_specs=[pl.BlockSpec((T, 3 * dh), lambda h, i: (0, h)), pl.BlockSpec((1, 1, T), lambda h, i: (h, 0, 0))],
        out_shape=[jax.ShapeDtypeStruct(du.shape, BF16), jax.ShapeDtypeStruct((FH, 1, T), F32)],
        scratch_shapes=[pltpu.VMEM((T, dh), F32), pltpu.VMEM((T, dh), F32)],
        input_output_aliases={6: 0},
        compiler_params=_cparams(("parallel", "arbitrary")),
    )(u, u, u, c_rows, lse, do, du)


def _head_indicators(cfg):
    ind = np.zeros((cfg.RW, LANES), np.float32)
    ind[np.arange(cfg.RW), np.arange(cfg.RW) // RWKV_HEAD_DIM] = 1.0
    pad = np.zeros((1, LANES), np.float32)
    pad[0, cfg.RH:] = 1.0
    return jnp.asarray(ind), jnp.asarray(ind.T.copy()), jnp.asarray(pad)


def _prep_fn(us_r, us_k, us_v, us_wd, us_ad, w0, w2p, a0, a2p, k_k, k_a, ind, ind_t, pad):
    wpre = w0 + _dot3(jnp.tanh(us_wd), w2p)
    w = -_softplus(-wpre) - 0.5
    lw = -jnp.exp(w)
    a = jax.nn.sigmoid(a0 + _dot3(us_ad, a2p))
    kk = us_k * k_k
    ss = _xdot(kk * kk, ind, ind_t) + pad
    inv = 1.0 / jnp.maximum(jnp.sqrt(ss), L2_EPS)
    kkn = kk * _xdot(inv, ind_t, ind)
    kp = us_k * (1.0 + (a - 1.0) * k_a)
    return us_r, lw, kp, us_v, -kkn, kkn * a


def _shifted(u, prev_row, mu, first):
    n = u.shape[0]
    rolled = pltpu.roll(u, 1, 0)
    row = lax.broadcasted_iota(jnp.int32, u.shape, 0)
    p0 = jnp.where(first, jnp.zeros_like(prev_row), prev_row)
    prev = jnp.where(row == 0, jnp.broadcast_to(p0, u.shape), rolled)
    return u + (prev - u) * mu, prev


def _rwkv_specs(cfg, tr):
    RW, LP = cfg.RW, cfg.LP
    base = cfg.o_rwkv // RW
    cols = [(RW, base), (RW, base + 1), (RW, base + 2), (RW, base + 3), (LP, cfg.o_wd // LP), (LP, cfg.o_ad // LP)]
    cur = [pl.BlockSpec((tr, w), (lambda i, cb=cb: (i, cb))) for w, cb in cols]
    prv = [pl.BlockSpec((8, w), (lambda i, cb=cb: (jnp.maximum(i * (tr // 8) - 1, 0), cb))) for w, cb in cols]
    return cols, cur, prv


def _mu_pieces(cfg, mu_ref):
    RW, LP = cfg.RW, cfg.LP
    offs = [0, RW, 2 * RW, 3 * RW, 4 * RW, 4 * RW + LP, 4 * RW + 2 * LP]
    return [mu_ref[:, offs[j]:offs[j + 1]] for j in range(6)]


def _rwkv_prep_fwd(cfg, u, mu, w0, w2p, a0, a2p, k_k, k_a):
    T, RW, LP, tr = cfg.T, cfg.RW, cfg.LP, cfg.tr
    ind, ind_t, pad = _head_indicators(cfg)
    cols, cur, prv = _rwkv_specs(cfg, tr)

    def body(*refs):
        u_refs, p_refs = refs[0:6], refs[6:12]
        mu_ref, w0_ref, w2_ref, a0_ref, a2_ref, kk_ref, ka_ref, ind_ref, indt_ref, pad_ref = refs[12:22]
        outs = refs[22:]
        first = pl.program_id(0) == 0
        mus = _mu_pieces(cfg, mu_ref)
        us = [_shifted(u_refs[j][...], p_refs[j][7:8, :], mus[j], first)[0] for j in range(6)]
        res = _prep_fn(us[0], us[1], us[2], us[4], us[5], w0_ref[...], w2_ref[...], a0_ref[...], a2_ref[...],
                       kk_ref[...], ka_ref[...], ind_ref[...], indt_ref[...], pad_ref[...])
        for j in range(6):
            outs[j][...] = res[j]
        outs[6][...] = us[3]

    consts = [mu, w0, w2p, a0, a2p, k_k, k_a, ind, ind_t, pad]
    return _pcall(body, name="rwkv_prep_fwd", grid=(T // tr,),
                  in_specs=cur + prv + [_const(c.shape) for c in consts],
                  out_specs=[_tile(tr, RW)] * 7, out_shape=[jax.ShapeDtypeStruct((T, RW), F32)] * 7,
                  compiler_params=_cparams(("parallel",)))(*([u] * 12), *consts)


def _rwkv_prep_bwd(cfg, u, mu, w0, w2p, a0, a2p, k_k, k_a, cots, dzb):
    T, RW, LP = cfg.T, cfg.RW, cfg.LP
    tr = min(128, T)
    ind, ind_t, pad = _head_indicators(cfg)
    cols, cur, prv = _rwkv_specs(cfg, tr)
    rseg = cfg.rseg

    def body(*refs):
        u_refs, p_refs = refs[0:6], refs[6:12]
        mu_ref, w0_ref, w2_ref, a0_ref, a2_ref, kk_ref, ka_ref, ind_ref, indt_ref, pad_ref = refs[12:22]
        cot_refs, dzb_ref = refs[22:28], refs[28]
        dus_ref, dmu_ref, dw0_ref, dw2_ref, da0_ref, da2_ref, dkk_ref, dka_ref = refs[29:]
        i = pl.program_id(0)
        first = i == 0
        mus = _mu_pieces(cfg, mu_ref)
        sh = [_shifted(u_refs[j][...], p_refs[j][7:8, :], mus[j], first) for j in range(6)]
        us = [s[0] for s in sh]
        fn = functools.partial(_prep_fn, ind=ind_ref[...], ind_t=indt_ref[...], pad=pad_ref[...])
        _, vjp = jax.vjp(fn, us[0], us[1], us[2], us[4], us[5], w0_ref[...], w2_ref[...], a0_ref[...], a2_ref[...],
                         kk_ref[...], ka_ref[...])
        d = vjp(tuple(c[...] for c in cot_refs))
        dus = [d[0], d[1], d[2], dzb_ref[...], d[3], d[4]]
        offs = [0, RW, 2 * RW, 3 * RW, 4 * RW, 4 * RW + LP, 4 * RW + 2 * LP]
        for j in range(6):
            dus_ref[:, offs[j]:offs[j + 1]] = dus[j]
            dmu_j = jnp.sum(dus[j] * (sh[j][1] - u_refs[j][...]), axis=0, keepdims=True)

            @pl.when(first)
            def _(j=j, dmu_j=dmu_j):
                dmu_ref[:, offs[j]:offs[j + 1]] = dmu_j

            @pl.when(i > 0)
            def _(j=j, dmu_j=dmu_j):
                dmu_ref[:, offs[j]:offs[j + 1]] += dmu_j
        for ref, val in zip((dw0_ref, dw2_ref, da0_ref, da2_ref, dkk_ref, dka_ref), d[5:11]):
            _acc_store(i, ref, val)

    consts = [mu, w0, w2p, a0, a2p, k_k, k_a, ind, ind_t, pad]
    vec = jax.ShapeDtypeStruct((1, RW), F32)
    mat = jax.ShapeDtypeStruct((LP, RW), F32)
    return _pcall(body, name="rwkv_prep_bwd", grid=(T // tr,),
                  in_specs=cur + prv + [_const(c.shape) for c in consts] + [_tile(tr, RW)] * 7,
                  out_specs=[_tile(tr, rseg), _const((1, rseg)), _const((1, RW)), _const((LP, RW)), _const((1, RW)),
                             _const((LP, RW)), _const((1, RW)), _const((1, RW))],
                  out_shape=[jax.ShapeDtypeStruct((T, rseg), F32), jax.ShapeDtypeStruct((1, rseg), F32),
                             vec, mat, vec, mat, vec, vec],
                  compiler_params=_cparams(("arbitrary",)))(*([u] * 12), *consts, *cots, dzb)


def _shift_bwd(cfg, dus, mu, df, du):
    T, tr, RW, LP = cfg.T, cfg.tr, cfg.RW, cfg.LP
    nb = T // tr
    tail = cfg.ncol - cfg.o_f
    assert cfg.o_rwkv % (4 * RW) == 0 and (4 * RW) % (2 * LP) == 0 and cfg.o_f % tail == 0

    def shifted(d_ref, n_ref, mu_ref):
        d = d_ref[...]
        rolled = pltpu.roll(d, tr - 1, 0)
        row = lax.broadcasted_iota(jnp.int32, d.shape, 0)
        n0 = jnp.where(pl.program_id(0) == nb - 1, jnp.zeros_like(n_ref[0:1, :]), n_ref[0:1, :])
        nxt = jnp.where(row == tr - 1, jnp.broadcast_to(n0, d.shape), rolled)
        mu_v = mu_ref[...]
        return (d * (1.0 - mu_v) + nxt * mu_v).astype(BF16)

    def main_body(d_ref, n_ref, mu_ref, du_in, du_ref):
        du_ref[...] = shifted(d_ref, n_ref, mu_ref)

    def tail_body(d_ref, n_ref, mu_ref, df_ref, du_in, du_ref):
        du_ref[:, 0:LANES] = df_ref[...]
        du_ref[:, LANES:LANES + 2 * LP] = shifted(d_ref, n_ref, mu_ref)
        if tail > LANES + 2 * LP:
            du_ref[:, LANES + 2 * LP:] = jnp.zeros((tr, tail - LANES - 2 * LP), BF16)

    def specs(w, cb):
        return [_tile(tr, w, cb),
                pl.BlockSpec((8, w), lambda i: (jnp.minimum((i + 1) * (tr // 8), T // 8 - 1), cb)),
                pl.BlockSpec((1, w), lambda i: (0, cb))]

    out = jax.ShapeDtypeStruct(du.shape, BF16)
    du = _pcall(main_body, name="shift_bwd_main", grid=(nb,), in_specs=specs(4 * RW, 0) + [_ANY],
                out_specs=_tile(tr, 4 * RW, cfg.o_rwkv // (4 * RW)), out_shape=out, input_output_aliases={3: 0},
                compiler_params=_cparams(("parallel",)))(dus, dus, mu, du)
    return _pcall(tail_body, name="shift_bwd_tail", grid=(nb,),
                  in_specs=specs(2 * LP, 4 * RW // (2 * LP)) + [_tile(tr, LANES), _ANY],
                  out_specs=_tile(tr, tail, cfg.o_f // tail), out_shape=out, input_output_aliases={4: 0},
                  compiler_params=_cparams(("parallel",)))(dus, dus, mu, df, du)


def _chunk_local(r, lw, k, v, a, b):
    H, C, K = r.shape
    row = lax.broadcasted_iota(jnp.int32, (C, C), 0)
    col = lax.broadcasted_iota(jnp.int32, (C, C), 1)
    incl = jnp.broadcast_to((row >= col).astype(F32)[None], (H, C, C))
    strict = (row > col)[None]
    lower = (row >= col)[None]
    eye = (row == col)[None]
    zero = jnp.zeros((), F32)
    L = _bdot(incl, lw, 2, 1)
    LC = jnp.sum(lw, axis=1, keepdims=True)
    eL = jnp.exp(L)
    eLn = jnp.exp(-L)
    at = a * jnp.exp(L - lw)
    rt = r * eL
    bt = b * eLn
    kt = k * eLn
    eR = jnp.exp(LC - L)
    bh = b * eR
    kh = k * eR
    gram = functools.partial(_bdot, passes=SCAN_PASSES[0])
    inv = functools.partial(_bdot, passes=SCAN_PASSES[1])
    app = functools.partial(_bdot, passes=SCAN_PASSES[2])
    ar = jnp.concatenate([at, rt], axis=1)
    g_b = gram(ar, bt, 2, 2)
    g_k = gram(ar, kt, 2, 2)
    n_ab = jnp.where(strict, g_b[:, :C], zero)
    n_ak = jnp.where(strict, g_k[:, :C], zero)
    m_rb = jnp.where(lower, g_b[:, C:], zero)
    m_rk = jnp.where(lower, g_k[:, C:], zero)
    M = n_ab
    P = jnp.where(eye, 1.0, zero) + n_ab
    for _ in range(1, max(1, int(np.ceil(np.log2(C))))):
        M = inv(M, M, 2, 1)
        P = P + inv(M, P, 2, 1)
    W = app(P, at, 2, 1)
    Uloc = app(P, app(n_ak, v, 2, 1), 2, 1)
    Q = rt + app(m_rb, W, 2, 1)
    Yloc = app(m_rb, Uloc, 2, 1) + app(m_rk, v, 2, 1)
    A = jnp.where(eye, jnp.exp(LC), zero) + app(W, bh, 1, 1)
    Sloc = app(Uloc, bh, 1, 1) + app(v, kh, 1, 1)
    return Q, Yloc, A, Sloc


def _split_heads(ref, n):
    N = RWKV_HEAD_DIM
    return jnp.stack([ref[:, h * N:(h + 1) * N] for h in range(n)], axis=0)


def _merge_heads(x):
    return jnp.concatenate([x[h] for h in range(x.shape[0])], axis=1)


def _scan_local_specs(cfg):
    N, HB = RWKV_HEAD_DIM, cfg.hb
    grid = (cfg.RH // HB, cfg.T // cfg.C)
    seq = pl.BlockSpec((HB, cfg.C, N), lambda h, j: (h, j, 0))
    mat = pl.BlockSpec((HB, 1, N, N), lambda h, j: (h, j, 0, 0))
    return grid, seq, mat


def _scan_local_fwd(cfg, seqs):
    T, RH, N = cfg.T, cfg.RH, RWKV_HEAD_DIM
    grid, seq, mat = _scan_local_specs(cfg)

    def body(r_ref, lw_ref, k_ref, v_ref, a_ref, b_ref, q_ref, yl_ref, a_out, sl_ref):
        Q, Yloc, A, Sloc = _chunk_local(*[_split_heads(ref, cfg.hb) for ref in (r_ref, lw_ref, k_ref, v_ref, a_ref, b_ref)])
        q_ref[...] = Q
        yl_ref[...] = Yloc
        a_out[:, 0] = A
        sl_ref[:, 0] = Sloc

    tok = pl.BlockSpec((cfg.C, cfg.hb * N), lambda h, j: (j, h))
    sq = jax.ShapeDtypeStruct((RH, T, N), F32)
    mt = jax.ShapeDtypeStruct((RH, T // cfg.C, N, N), F32)
    return _pcall(body, name="rwkv_scan_local_fwd", grid=grid, in_specs=[tok] * 6, out_specs=[seq, seq, mat, mat],
                  out_shape=[sq, sq, mt, mt], compiler_params=_cparams(("parallel", "parallel")))(*seqs)


def _scan_local_bwd(cfg, toks, dq, dy, da, dsl, extra, comm=None):
    T, RW, N = cfg.T, cfg.RW, RWKV_HEAD_DIM
    grid, seq, mat = _scan_local_specs(cfg)
    c_in, c_out, c_scr = comm[:3] if comm else ([], [], [])

    def body(r_ref, lw_ref, k_ref, v_ref, a_ref, b_ref, dq_ref, dy_ref, da_ref, dsl_ref, xr_ref, xk_ref, xv_ref,
             *rest):
        cin, outs = rest[:len(c_in)], rest[len(c_in):len(c_in) + 6]
        cout, scr = rest[len(c_in) + 6:len(c_in) + 6 + len(c_out)], rest[len(c_in) + 6 + len(c_out):]
        _comm_at(comm, 3, grid, cin, cout, scr)
        ins = [_split_heads(ref, cfg.hb) for ref in (r_ref, lw_ref, k_ref, v_ref, a_ref, b_ref)]
        _, vjp = jax.vjp(_chunk_local, *ins)
        d = vjp((dq_ref[...], _split_heads(dy_ref, cfg.hb), da_ref[:, 0], dsl_ref[:, 0]))
        add = {0: xr_ref, 2: xk_ref, 3: xv_ref}
        for j in range(6):
            dj = _merge_heads(d[j])
            outs[j][...] = dj + add[j][...] if j in add else dj
        _comm_at(comm, 4, grid, cin, cout, scr)

    tok = pl.BlockSpec((cfg.C, cfg.hb * N), lambda h, j: (j, h))
    return _pcall(body, name="rwkv_scan_local_bwd", grid=grid,
                  in_specs=[tok] * 6 + [seq, tok, mat, mat] + [tok] * 3 + [_ANY] * len(c_in),
                  out_specs=[tok] * 6 + [_ANY] * len(c_out),
                  out_shape=[jax.ShapeDtypeStruct((T, RW), F32)] * 6 + list(c_out), scratch_shapes=list(c_scr),
                  compiler_params=_cparams(("arbitrary", "arbitrary") if comm else ("parallel", "parallel")),
                  )(*toks, dq, dy, da, dsl, *extra, *c_in)


def _scan_carry_specs(cfg, rev):
    N, RH, C, nc = RWKV_HEAD_DIM, cfg.RH, cfg.C, cfg.T // cfg.C
    at = (lambda j: nc - 1 - j) if rev else (lambda j: j)
    seq = pl.BlockSpec((RH, C, N), lambda j: (0, at(j), 0))
    mat = pl.BlockSpec((RH, 1, N, N), lambda j: (0, at(j), 0, 0))
    return nc, seq, mat


def _scan_carry_fwd(cfg, q, yloc, a, sloc):
    T, RH, N = cfg.T, cfg.RH, RWKV_HEAD_DIM
    nc, seq, mat = _scan_carry_specs(cfg, False)

    def body(q_ref, yl_ref, a_ref, sl_ref, y_ref, ck_ref, s_ref):
        @pl.when(pl.program_id(0) == 0)
        def _():
            s_ref[...] = jnp.zeros_like(s_ref)

        S = s_ref[...]
        ck_ref[:, 0] = S
        y_ref[...] = _merge_heads(_bdot(q_ref[...], S, 2, 2, SCAN_PASSES[2]) + yl_ref[...])
        s_ref[...] = _bdot(S, a_ref[:, 0], 2, 1) + sl_ref[:, 0]

    tok = pl.BlockSpec((cfg.C, cfg.RW), lambda j: (j, 0))
    return _pcall(body, name="rwkv_scan_carry_fwd", grid=(nc,), in_specs=[seq, seq, mat, mat], out_specs=[tok, mat],
                  out_shape=[jax.ShapeDtypeStruct((T, cfg.RW), F32), jax.ShapeDtypeStruct((RH, nc, N, N), F32)],
                  scratch_shapes=[pltpu.VMEM((RH, N, N), F32)],
                  compiler_params=_cparams(("arbitrary",)))(q, yloc, a, sloc)


def _scan_carry_bwd(cfg, q, a, ckpt, dy):
    T, RH, N = cfg.T, cfg.RH, RWKV_HEAD_DIM
    nc, seq, mat = _scan_carry_specs(cfg, True)

    def body(q_ref, a_ref, ck_ref, dy_ref, dq_ref, da_ref, dsl_ref, ds_ref):
        @pl.when(pl.program_id(0) == 0)
        def _():
            ds_ref[...] = jnp.zeros_like(ds_ref)

        S, dS, dY = ck_ref[:, 0], ds_ref[...], _split_heads(dy_ref, RH)
        dq_ref[...] = _bdot(dY, S, 2, 1, SCAN_PASSES[2])
        da_ref[:, 0] = _bdot(S, dS, 1, 1, SCAN_PASSES[2])
        dsl_ref[:, 0] = dS
        ds_ref[...] = _bdot(dS, a_ref[:, 0], 2, 2) + _bdot(dY, q_ref[...], 1, 1, SCAN_PASSES[2])

    mt = jax.ShapeDtypeStruct((RH, nc, N, N), F32)
    tok = pl.BlockSpec((cfg.C, cfg.RW), lambda j: (nc - 1 - j, 0))
    return _pcall(body, name="rwkv_scan_carry_bwd", grid=(nc,), in_specs=[seq, mat, mat, tok],
                  out_specs=[seq, mat, mat], out_shape=[jax.ShapeDtypeStruct((RH, T, N), F32), mt, mt],
                  scratch_shapes=[pltpu.VMEM((RH, N, N), F32)],
                  compiler_params=_cparams(("arbitrary",)))(q, a, ckpt, dy)


def _post_fn(y, r, kp, v, zb, ln_w, ln_b, rk, ind, ind_t):
    n = float(RWKV_HEAD_DIM)
    mu = _xdot(_xdot(y, ind, ind_t) / n, ind_t, ind)
    yc = y - mu
    var = _xdot(yc * yc, ind, ind_t) / n
    rstd = _xdot(lax.rsqrt(var + GN_EPS), ind_t, ind)
    yn = yc * rstd * ln_w + ln_b
    bonus = _xdot(_xdot(r * kp * rk, ind, ind_t), ind_t, ind) * v
    return (yn + bonus) * _silu(zb)


def _rwkv_post_fwd(cfg, y, r, kp, v, zb, ln_w, ln_b, rk):
    T, RW, tr = cfg.T, cfg.RW, cfg.tr
    ind, ind_t, _ = _head_indicators(cfg)

    def body(y_ref, r_ref, k_ref, v_ref, z_ref, lw_ref, lb_ref, rk_ref, ind_ref, indt_ref, ob_ref):
        ob_ref[...] = _post_fn(y_ref[...], r_ref[...], k_ref[...], v_ref[...], z_ref[...], lw_ref[...], lb_ref[...],
                               rk_ref[...], ind_ref[...], indt_ref[...]).astype(BF16)

    consts = [ln_w, ln_b, rk, ind, ind_t]
    return _pcall(body, name="rwkv_post_fwd", grid=(T // tr,),
                  in_specs=[_tile(tr, RW)] * 5 + [_const(c.shape) for c in consts],
                  out_specs=_tile(tr, RW), out_shape=jax.ShapeDtypeStruct((T, RW), BF16),
                  compiler_params=_cparams(("parallel",)))(y, r, kp, v, zb, *consts)


def _rwkv_post_bwd(cfg, y, r, kp, v, zb, ln_w, ln_b, rk, dob):
    T, RW = cfg.T, cfg.RW
    tr = min(128, T)
    ind, ind_t, _ = _head_indicators(cfg)

    def body(y_ref, r_ref, k_ref, v_ref, z_ref, lw_ref, lb_ref, rk_ref, ind_ref, indt_ref, dob_ref,
             dy_ref, dr_ref, dk_ref, dv_ref, dz_ref, dlw_ref, dlb_ref, drk_ref):
        fn = functools.partial(_post_fn, ind=ind_ref[...], ind_t=indt_ref[...])
        _, vjp = jax.vjp(fn, y_ref[...], r_ref[...], k_ref[...], v_ref[...], z_ref[...], lw_ref[...], lb_ref[...],
                         rk_ref[...])
        d = vjp(dob_ref[...])
        for ref, val in zip((dy_ref, dr_ref, dk_ref, dv_ref, dz_ref), d[:5]):
            ref[...] = val
        i = pl.program_id(0)
        for ref, val in zip((dlw_ref, dlb_ref, drk_ref), d[5:8]):
            _acc_store(i, ref, val)

    consts = [ln_w, ln_b, rk, ind, ind_t]
    vec = jax.ShapeDtypeStruct((1, RW), F32)
    return _pcall(body, name="rwkv_post_bwd", grid=(T // tr,),
                  in_specs=[_tile(tr, RW)] * 5 + [_const(c.shape) for c in consts] + [_tile(tr, RW)],
                  out_specs=[_tile(tr, RW)] * 5 + [_const((1, RW))] * 3,
                  out_shape=[jax.ShapeDtypeStruct((T, RW), F32)] * 5 + [vec] * 3,
                  compiler_params=_cparams(("arbitrary",)))(y, r, kp, v, zb, *consts, dob)


def _adamw_math(w, g, m, v):
    m = ADAM_B1 * m + (1.0 - ADAM_B1) * g
    v = ADAM_B2 * v + (1.0 - ADAM_B2) * (g * g)
    m_hat = m / (1.0 - ADAM_B1 ** ADAM_STEP)
    v_hat = v / (1.0 - ADAM_B2 ** ADAM_STEP)
    delta = -ADAM_LR * (m_hat / (jnp.sqrt(v_hat) + ADAM_EPS) + ADAM_WD * w)
    return delta, m, v


def _adamw(name, w, g, m, v, copy_grad=False):
    R, Cc = w.shape
    Rp = -(-R // 8) * 8
    tr = Rp
    for nb in range(1, Rp // 8 + 1):
        if (Rp // 8) % nb == 0 and (Rp // nb) * Cc * 4 <= 2 * 1024 * 1024:
            tr = Rp // nb
            break

    def body(w_ref, g_ref, m_ref, v_ref, d_ref, nm_ref, nv_ref, *g_out):
        g_v = g_ref[...]
        d, nm, nv = _adamw_math(w_ref[...], g_v, m_ref[...], v_ref[...])
        d_ref[...] = d
        nm_ref[...] = nm
        nv_ref[...] = nv
        if copy_grad:
            g_out[0][...] = g_v

    spec = _tile(tr, Cc)
    n_out = 4 if copy_grad else 3
    return _pcall(body, name=name, grid=(Rp // tr,), in_specs=[spec] * 4, out_specs=[spec] * n_out,
                  out_shape=[jax.ShapeDtypeStruct((R, Cc), F32)] * n_out,
                  compiler_params=_cparams(("parallel",)))(w, g, m, v)


def _row_tile(R, Cc, itemsize, budget=2 * 1024 * 1024):
    for nb in range(1, R // 16 + 1):
        if R % nb == 0 and (R // nb) % 16 == 0 and (R // nb) * Cc * itemsize <= budget:
            return R // nb
    return R


def _add_halves(name, gs, r1, c_idx):
    S, R, Cc = gs.shape
    half = R // 2
    tr = _row_tile(half, Cc, 4)
    nb = half // tr

    def body(c_ref, g_ref, r_ref, o_ref):
        o_ref[...] = (g_ref[...].astype(F32) + r_ref[...].astype(F32)).astype(BF16)

    grid_spec = pltpu.PrefetchScalarGridSpec(
        num_scalar_prefetch=1, grid=(S, nb),
        in_specs=[pl.BlockSpec((1, tr, Cc), lambda s, i, c: (s, c[0] * nb + i, 0)),
                  pl.BlockSpec((1, tr, Cc), lambda s, i, c: (s, i, 0))],
        out_specs=pl.BlockSpec((1, tr, Cc), lambda s, i, c: (s, i, 0)))
    return _pcall(body, name=name, grid_spec=grid_spec, out_shape=jax.ShapeDtypeStruct((S, half, Cc), BF16),
                  compiler_params=_cparams(("parallel", "parallel")))(c_idx, gs, r1)


def _sum_slots(name, r2):
    S, R, Cc = r2.shape
    tr = _row_tile(R, Cc, 4 * S // 2 if r2.dtype == BF16 else 4 * S)

    def body(r_ref, o_ref):
        acc = r_ref[0].astype(F32)
        for s in range(1, S):
            acc = acc + r_ref[s].astype(F32)
        o_ref[...] = acc

    return _pcall(body, name=name, grid=(R // tr,), in_specs=[pl.BlockSpec((S, tr, Cc), lambda i: (0, i, 0))],
                  out_specs=_tile(tr, Cc), out_shape=jax.ShapeDtypeStruct((R, Cc), F32),
                  compiler_params=_cparams(("parallel",)))(r2)


def _sum_chips(name, recv, own, place):
    S, H, Cc = recv.shape
    tr = _row_tile(H, Cc, 4, 1024 * 1024)
    nb = H // tr

    def body(p_ref, r_ref, own_ref, o_ref):
        s = pl.program_id(1)
        me = p_ref[0]

        @pl.when(s == 0)
        def _():
            o_ref[...] = jnp.zeros_like(o_ref)

        @pl.when(s == me)
        def _():
            o_ref[...] += own_ref[0].astype(F32)

        @pl.when(s != me)
        def _():
            o_ref[...] += r_ref[0].astype(F32)

    grid_spec = pltpu.PrefetchScalarGridSpec(
        num_scalar_prefetch=1, grid=(nb, S),
        in_specs=[pl.BlockSpec((1, tr, Cc), lambda i, s, p: (jnp.where(s == p[0], (s + 1) % S, s), i, 0)),
                  pl.BlockSpec((1, tr, Cc), lambda i, s, p: (p[0], i, 0))],
        out_specs=pl.BlockSpec((tr, Cc), lambda i, s, p: (p[1] * nb + i, 0)))
    return _pcall(body, name=name, grid_spec=grid_spec, out_shape=jax.ShapeDtypeStruct((2 * H, Cc), F32),
                  compiler_params=_cparams(("parallel", "arbitrary")))(place, recv, own)


def _cast_bf16(name, w):
    R, Cc = w.shape
    tr = _row_tile(R, Cc, 4)

    def body(w_ref, o_ref):
        o_ref[...] = w_ref[...].astype(BF16)

    return _pcall(body, name=name, grid=(R // tr,), in_specs=[_tile(tr, Cc)], out_specs=_tile(tr, Cc),
                  out_shape=jax.ShapeDtypeStruct((R, Cc), BF16), compiler_params=_cparams(("parallel",)))(w)


_ANY = pl.BlockSpec(memory_space=pl.ANY)


def _place():
    x, y, c = lax.axis_index("x"), lax.axis_index("y"), lax.axis_index("c")
    others = [(1 - x, y), (x, 1 - y), (1 - x, 1 - y)]
    return x, y, c, others


def _gather_weights(shards):
    arrays, out_shapes, scratch, start, finish, middle = _gather_parts(shards)
    n = len(shards)

    def body(*refs):
        ins, outs, sems = refs[:n], refs[n:2 * n], refs[2 * n:]
        start(ins, outs, sems)
        middle(ins, outs, sems)
        finish(ins, outs, sems)

    return _pcall(body, name="gather_weights", in_specs=[_ANY] * n, out_specs=[_ANY] * n, out_shape=out_shapes,
                  scratch_shapes=scratch)(*arrays)


def _gather_parts(shards):
    n = len(shards)
    halves = [s.shape[0] // 2 for s in shards]

    def parts(ins, outs, sems):
        x, y, c, _ = _place()
        me = 2 * x + y
        n1 = (x ^ (1 - c), y ^ c)
        n2 = (x ^ c, y ^ (1 - c))
        s1, s2, sd = 2 * n1[0] + n1[1], 2 * n2[0] + n2[1], 2 * (1 - x) + (1 - y)
        sib = (x, y, 1 - c)

        def rows(k, chip, hc):
            return outs[k].at[chip, pl.ds(hc * halves[k], halves[k]), :]

        def remote(k, j, src, dst, to):
            return pltpu.make_async_remote_copy(src_ref=src, dst_ref=dst, send_sem=sems[0].at[6 * k + j],
                                                recv_sem=sems[1].at[6 * k + j], device_id=to, device_id_type=MESH)

        def copy(k, j):
            if j < 2:
                mine = ins[k].at[pl.ds(c * halves[k], halves[k]), :]
                return remote(k, j, mine, rows(k, me, c), (*(n1 if j == 0 else n2), c))
            land = rows(k, {2: s1, 3: s1, 4: s2, 5: sd}[j], c)
            return remote(k, j, land, land, (*n2, c) if j == 2 else sib)

        def arrived(k, j):
            hc = c if j < 3 else 1 - c
            land = rows(k, {0: s1, 1: s2, 2: sd, 3: s2, 4: s1, 5: sd}[j], hc)
            remote(k, j, land, land, (x, y, c)).wait_recv()

        return copy, arrived

    def start(ins, outs, sems):
        copy, _ = parts(ins, outs, sems)
        for k in range(n):
            copy(k, 0).start()
            copy(k, 1).start()

    def middle(ins, outs, sems):
        copy, arrived = parts(ins, outs, sems)
        for k in range(n):
            arrived(k, 0)
            copy(k, 2).start()
            copy(k, 3).start()
            arrived(k, 1)
            copy(k, 4).start()

    def finish(ins, outs, sems):
        copy, arrived = parts(ins, outs, sems)
        for k in range(n):
            arrived(k, 2)
            copy(k, 5).start()
        for k in range(n):
            for j in (3, 4, 5):
                arrived(k, j)
        for k in range(n):
            for j in range(6):
                copy(k, j).wait_send()

    out_shapes = [jax.ShapeDtypeStruct((N_CHIPS,) + s.shape, s.dtype) for s in shards]
    scratch = [pltpu.SemaphoreType.DMA((6 * n,)), pltpu.SemaphoreType.DMA((6 * n,))]
    return list(shards), out_shapes, scratch, start, finish, middle


def _exchange_halves(name, grads):
    n = len(grads)
    halves = [g.shape[1] // 2 for g in grads]

    def body(*refs):
        ins, outs = refs[:n], refs[n:2 * n]
        send_sems, recv_sems = refs[2 * n:]
        x, y, c, _ = _place()
        cps = []
        for k in range(n):
            src = ins[k].at[:, pl.ds((1 - c) * halves[k], halves[k]), :]
            cp = pltpu.make_async_remote_copy(src_ref=src, dst_ref=outs[k], send_sem=send_sems.at[k],
                                              recv_sem=recv_sems.at[k], device_id=(x, y, 1 - c), device_id_type=MESH)
            cp.start()
            cps.append(cp)
        for cp in cps:
            cp.wait()

    return _pcall(
        body, name=name, in_specs=[_ANY] * n, out_specs=[_ANY] * n,
        out_shape=[jax.ShapeDtypeStruct((g.shape[0], h) + g.shape[2:], g.dtype) for g, h in zip(grads, halves)],
        scratch_shapes=[pltpu.SemaphoreType.DMA((n,)), pltpu.SemaphoreType.DMA((n,))],
    )(*grads)


def _scatter_to_owners(chip_sums):
    n = len(chip_sums)

    def sends(ins, outs, sems):
        x, y, c, others = _place()
        me = 2 * x + y
        return [pltpu.make_async_remote_copy(
            src_ref=ins[k].at[2 * px + py], dst_ref=outs[k].at[me], send_sem=sems[0].at[3 * k + j],
            recv_sem=sems[1].at[3 * k + j], device_id=(px, py, c), device_id_type=MESH)
            for k in range(n) for j, (px, py) in enumerate(others)]

    def start(ins, outs, sems):
        for cp in sends(ins, outs, sems):
            cp.start()

    def finish(ins, outs, sems):
        x, y, c, others = _place()
        for k in range(n):
            for j, (px, py) in enumerate(others):
                land = outs[k].at[2 * px + py]
                pltpu.make_async_remote_copy(src_ref=land, dst_ref=land, send_sem=sems[0].at[3 * k + j],
                                             recv_sem=sems[1].at[3 * k + j], device_id=(x, y, c),
                                             device_id_type=MESH).wait_recv()
        for cp in sends(ins, outs, sems):
            cp.wait_send()

    out_shapes = [jax.ShapeDtypeStruct(g.shape, g.dtype) for g in chip_sums]
    scratch = [pltpu.SemaphoreType.DMA((3 * n,)), pltpu.SemaphoreType.DMA((3 * n,))]
    return list(chip_sums), out_shapes, scratch, start, finish


def _swap_with_sibling(arrays):
    n = len(arrays)

    def copies(ins, outs, sems):
        x, y, c, _ = _place()
        return [pltpu.make_async_remote_copy(src_ref=ins[k], dst_ref=outs[k], send_sem=sems[0].at[k],
                                             recv_sem=sems[1].at[k], device_id=(x, y, 1 - c), device_id_type=MESH)
                for k in range(n)]

    def start(ins, outs, sems):
        for cp in copies(ins, outs, sems):
            cp.start()

    def finish(ins, outs, sems):
        for cp in copies(ins, outs, sems):
            cp.wait()

    out_shapes = [jax.ShapeDtypeStruct(a.shape, a.dtype) for a in arrays]
    scratch = [pltpu.SemaphoreType.DMA((n,)), pltpu.SemaphoreType.DMA((n,))]
    return list(arrays), out_shapes, scratch, start, finish


def _add_pair(name, a, b):
    R, Cc = a.shape
    tr = _row_tile(R, Cc, 4)

    def body(a_ref, b_ref, o_ref):
        o_ref[...] = (a_ref[...].astype(F32) + b_ref[...].astype(F32)).astype(BF16)

    return _pcall(body, name=name, grid=(R // tr,), in_specs=[_tile(tr, Cc)] * 2, out_specs=_tile(tr, Cc),
                  out_shape=jax.ShapeDtypeStruct((R, Cc), BF16), compiler_params=_cparams(("parallel",)))(a, b)


def _second_neighbour():
    x, y, c, _ = _place()
    return (x, y, c), (x ^ c, y ^ (1 - c)), (x ^ (1 - c), y ^ c)


def _scatter_stage1(chip_sums):
    n = len(chip_sums)

    def copies(ins, outs, sems):
        (x, y, c), n2, n1 = _second_neighbour()
        diag = 2 * (1 - x) + (1 - y)
        return [pltpu.make_async_remote_copy(
            src_ref=ins[k].at[slot], dst_ref=outs[2 * k + j], send_sem=sems[0].at[2 * k + j],
            recv_sem=sems[1].at[2 * k + j], device_id=(*n2, c), device_id_type=MESH)
            for k in range(n) for j, slot in enumerate((2 * n2[0] + n2[1], diag))]

    def start(ins, outs, sems):
        for cp in copies(ins, outs, sems):
            cp.start()

    def finish(ins, outs, sems):
        for cp in copies(ins, outs, sems):
            cp.wait()

    out_shapes = [jax.ShapeDtypeStruct(g.shape[1:], g.dtype) for g in chip_sums for _ in range(2)]
    scratch = [pltpu.SemaphoreType.DMA((2 * n,)), pltpu.SemaphoreType.DMA((2 * n,))]
    return list(chip_sums), out_shapes, scratch, start, finish


def _scatter_stage2(passed):
    n = len(passed)

    def copies(ins, outs, sems):
        (x, y, c), n2, n1 = _second_neighbour()
        return [pltpu.make_async_remote_copy(src_ref=ins[k], dst_ref=outs[k], send_sem=sems[0].at[k],
                                             recv_sem=sems[1].at[k], device_id=(*n1, c), device_id_type=MESH)
                for k in range(n)]

    def start(ins, outs, sems):
        for cp in copies(ins, outs, sems):
            cp.start()

    def finish(ins, outs, sems):
        for cp in copies(ins, outs, sems):
            cp.wait()

    out_shapes = [jax.ShapeDtypeStruct(p.shape, p.dtype) for p in passed]
    scratch = [pltpu.SemaphoreType.DMA((n,)), pltpu.SemaphoreType.DMA((n,))]
    return list(passed), out_shapes, scratch, start, finish


def _add_passed(name, own, got, slot):
    _, H, Cc = own.shape
    tr = _row_tile(H, Cc, 4)

    def body(s_ref, o_ref, g_ref, out_ref):
        out_ref[...] = (o_ref[0].astype(F32) + g_ref[...].astype(F32)).astype(BF16)

    grid_spec = pltpu.PrefetchScalarGridSpec(
        num_scalar_prefetch=1, grid=(H // tr,),
        in_specs=[pl.BlockSpec((1, tr, Cc), lambda i, s: (s[0], i, 0)), pl.BlockSpec((tr, Cc), lambda i, s: (i, 0))],
        out_specs=pl.BlockSpec((tr, Cc), lambda i, s: (i, 0)))
    return _pcall(body, name=name, grid_spec=grid_spec, out_shape=jax.ShapeDtypeStruct((H, Cc), BF16),
                  compiler_params=_cparams(("parallel",)))(slot, own, got)


def _sum_stages(name, own, direct, via, place, transposed=False):
    _, H, Cc = own.shape
    tr = LANES if transposed else _row_tile(H, Cc, 4, 1024 * 1024)
    nb = H // tr

    def body(p_ref, own_ref, d_ref, v_ref, o_ref):
        acc = (own_ref[0].astype(F32) + d_ref[...].astype(F32)) + v_ref[...].astype(F32)
        o_ref[...] = acc.T if transposed else acc

    flat = pl.BlockSpec((tr, Cc), lambda i, p: (i, 0))
    out_spec = (pl.BlockSpec((Cc, tr), lambda i, p: (0, p[1] * nb + i)) if transposed
                else pl.BlockSpec((tr, Cc), lambda i, p: (p[1] * nb + i, 0)))
    grid_spec = pltpu.PrefetchScalarGridSpec(
        num_scalar_prefetch=1, grid=(nb,),
        in_specs=[pl.BlockSpec((1, tr, Cc), lambda i, p: (p[0], i, 0)), flat, flat], out_specs=out_spec)
    return _pcall(body, name=name, grid_spec=grid_spec,
                  out_shape=jax.ShapeDtypeStruct((Cc, 2 * H) if transposed else (2 * H, Cc), F32),
                  compiler_params=_cparams(("parallel",)))(place, own, direct, via)


def _join_halves(fulls, axes, small):
    n = len(fulls)
    hs = [f.shape[ax] // 2 for f, ax in zip(fulls, axes)]
    rel = [(dx, dy, dc) for dx in (0, 1) for dy in (0, 1) for dc in (0, 1)][1:]

    def half(ref, k, hc):
        part = pl.ds(hc * hs[k], hs[k])
        return ref.at[:, part] if axes[k] else ref.at[part, :]

    def body(*refs):
        ins, small_in = refs[:n], refs[n]
        outs, small_out = refs[n + 1:2 * n + 1], refs[2 * n + 1]
        send_sems, recv_sems, ssend, srecv, local_sem = refs[2 * n + 2:]
        x, y, c, _ = _place()
        dev = 4 * x + 2 * y + c
        local = pltpu.make_async_copy(small_in, small_out.at[dev], local_sem)
        local.start()
        cps = []
        for k in range(n):
            cp = pltpu.make_async_remote_copy(src_ref=half(ins[k], k, c), dst_ref=half(outs[k], k, c),
                                              send_sem=send_sems.at[k], recv_sem=recv_sems.at[k],
                                              device_id=(x, y, 1 - c), device_id_type=MESH)
            cp.start()
            cps.append(cp)
        for r, (dx, dy, dc) in enumerate(rel):
            cp = pltpu.make_async_remote_copy(src_ref=small_in, dst_ref=small_out.at[dev], send_sem=ssend.at[r],
                                              recv_sem=srecv.at[r], device_id=(x ^ dx, y ^ dy, c ^ dc),
                                              device_id_type=MESH)
            cp.start()
            cps.append(cp)
        for k in range(n):
            land = half(outs[k], k, 1 - c)
            pltpu.make_async_remote_copy(src_ref=land, dst_ref=land, send_sem=send_sems.at[k],
                                         recv_sem=recv_sems.at[k], device_id=(x, y, c), device_id_type=MESH).wait_recv()
        for r, (dx, dy, dc) in enumerate(rel):
            land = small_out.at[4 * (x ^ dx) + 2 * (y ^ dy) + (c ^ dc)]
            pltpu.make_async_remote_copy(src_ref=land, dst_ref=land, send_sem=ssend.at[r], recv_sem=srecv.at[r],
                                         device_id=(x, y, c), device_id_type=MESH).wait_recv()
        for cp in cps:
            cp.wait_send()
        local.wait()

    return _pcall(
        body, name="join_halves", in_specs=[_ANY] * (n + 1), out_specs=[_ANY] * (n + 1),
        out_shape=[jax.ShapeDtypeStruct(f.shape, f.dtype) for f in fulls]
        + [jax.ShapeDtypeStruct((N_DEV,) + small.shape, small.dtype)],
        input_output_aliases={k: k for k in range(n)},
        scratch_shapes=[pltpu.SemaphoreType.DMA((n,)), pltpu.SemaphoreType.DMA((n,)), pltpu.SemaphoreType.DMA((7,)),
                        pltpu.SemaphoreType.DMA((7,)), pltpu.SemaphoreType.DMA],
    )(*fulls, small)


def _local_step(cfg, x2, target, norm_gain, w_my, fb, mu_g, w0, a0, k_k, k_a, r_k, ln_w, ln_b, fng, rest,
                exchange=None):
    T, D, FW, FH, RW, RH, LP, lora = cfg.T, cfg.D, cfg.FW, cfg.FH, cfg.RW, cfg.RH, cfg.LP, cfg.lora
    fb_p = jnp.pad(fb, ((0, 0), (0, LANES - FH)))
    mu = _rwkv_vec_to_my(cfg, mu_g)
    rk = r_k.reshape(1, RW)
    tm = min(1024, T)

    h = _rms_fwd(cfg, x2, norm_gain)
    if len(rest) == 2:
        u, *got = _mm("in_proj", h, w_my, "nn", F32, tm, cfg.tn, 2048, comm=rest[0])
        rest = rest[1](got)
    else:
        u = _mm("in_proj", h, w_my, "nn", F32, tm, cfg.tn, 2048)
    w2, a2, wpf, wpr, wout = rest
    w2p = jnp.pad(w2, ((0, LP - lora), (0, 0)))
    a2p = jnp.pad(a2, ((0, LP - lora), (0, 0)))
    c_cols = _fox_prep(cfg, u, fb_p)
    c_rows = c_cols[:, :FH].T.reshape(FH, 1, T)
    o, lse = _attn_fwd(cfg, u, c_rows)
    oa = _gate_a_fwd(cfg, o, u)
    prep = _rwkv_prep_fwd(cfg, u, mu, w0, w2p, a0, a2p, k_k, k_a)
    r, lw, kp, v, an, b, zb = prep
    toks = [r, lw, kp, v, an, b]
    q_s, yloc, a_m, sloc = _scan_local_fwd(cfg, toks)
    y, ckpt = _scan_carry_fwd(cfg, q_s, yloc, a_m, sloc)
    ob = _rwkv_post_fwd(cfg, y, r, kp, v, zb, ln_w, ln_b, rk)
    pa = _mm("proj_fox", oa, wpf, "nn", F32, tm, 1024, 2048)
    pb = _mm("proj_rwkv", ob, wpr, "nn", F32, tm, 1024, 2048)
    m = _merge_fwd(cfg, pa, pb, u)
    mo = _mm("out_proj", m, wout, "nn", F32, tm, 1024, 2048)
    loss8, dres, dres16, d_fng = _final(cfg, x2, mo, fng.reshape(1, D), target)

    dm = _mm("out_proj_dx", dres16, wout, "nt", F32, tm, 1024, 2048)
    d_wout = _mm("out_proj_dw", m, dres16, "tn", BF16, 1024, 1024, 2048)
    dpa, dpb, du = _merge_bwd(cfg, pa, pb, u, dm)
    doa = _mm("proj_fox_dx", dpa, wpf, "nt", F32, tm, 1024, 2048)
    d_wpf = _mm("proj_fox_dw", oa, dpa, "tn", BF16, 1024, 1024, 2048)
    dob = _mm("proj_rwkv_dx", dpb, wpr, "nt", F32, tm, 1024, 2048)
    d_wpr = _mm("proj_rwkv_dw", ob, dpb, "tn", BF16, 1024, 1024, 2048)

    do, du = _gate_a_bwd(cfg, o, u, doa, du)
    du, dcol = _attn_bwd(cfg, u, c_rows, lse, do, du)
    dc = jnp.pad(-dcol.reshape(FH, T).T, ((0, 0), (0, LANES - FH)))
    df, d_fb = _fox_prep_bwd(cfg, u, fb_p, dc)

    dy, dr_p, dk_p, dv_p, dzb, d_lnw, d_lnb, d_rk = _rwkv_post_bwd(cfg, y, r, kp, v, zb, ln_w, ln_b, rk, dob)
    dq_s, da_m, dsl = _scan_carry_bwd(cfg, q_s, a_m, ckpt, dy)
    early = dict(w_proj_fox=d_wpf, w_proj_rwkv=d_wpr, w_out=d_wout)
    res = _scan_local_bwd(cfg, toks, dq_s, dy, da_m, dsl, [dr_p, dk_p, dv_p], exchange(early) if exchange else None)
    cots, received = res[:6], list(res[6:])
    dus, d_mu, d_w0, d_w2p, d_a0, d_a2p, d_kk, d_ka = _rwkv_prep_bwd(cfg, u, mu, w0, w2p, a0, a2p, k_k, k_a, cots, dzb)
    du = _shift_bwd(cfg, dus, mu, df, du)
    if exchange:
        late = dict(w_in=exchange((h, du, d_w2p[:lora], d_a2p[:lora])))
    else:
        late = dict(w_in=_mm("in_proj_dw", h, du, "tn", BF16, 1024, cfg.tn, 2048), rwkv_w2=d_w2p[:lora],
                    rwkv_a2=d_a2p[:lora])
    tkx = 2 * cfg.tn if cfg.ncol % (2 * cfg.tn) == 0 else cfg.tn
    res = _mm("in_proj_dx", du, w_my, "nt", F32, tm, 1024, tkx, comm=exchange(late) if exchange else None)
    dh = res[0] if exchange else res
    big = dict(early, **late)
    res = _rms_bwd(cfg, x2, norm_gain, dh, dres, exchange(list(res[1:])) if exchange else None)
    gx, d_ng = res[:2]
    received += list(res[2:])

    small = dict(norm_gain=d_ng, fox_forget_bias=d_fb[:, :FH], rwkv_shift_mix=_rwkv_vec_from_my(cfg, d_mu),
                 rwkv_w0=d_w0, rwkv_a0=d_a0, rwkv_k_k=d_kk, rwkv_k_a=d_ka, rwkv_r_k=d_rk, rwkv_ln_w=d_lnw,
                 rwkv_ln_b=d_lnb, final_norm_gain=d_fng)
    return loss8[0, 0], gx, small, big, received


_SMALL = ["norm_gain", "fox_forget_bias", "rwkv_shift_mix", "rwkv_w0", "rwkv_a0", "rwkv_k_k", "rwkv_k_a", "rwkv_r_k",
          "rwkv_ln_w", "rwkv_ln_b", "final_norm_gain"]
_WEIGHTS = ["norm_gain", "w_in", "fox_forget_bias", "rwkv_shift_mix", "rwkv_w0", "rwkv_w2", "rwkv_a0", "rwkv_a2",
            "rwkv_k_k", "rwkv_k_a", "rwkv_r_k", "rwkv_ln_w", "rwkv_ln_b", "w_proj_fox", "w_proj_rwkv", "w_out",
            "final_norm_gain"]


def _pack_small(arrs):
    parts = []
    for a in arrs:
        f = a.reshape(-1)
        parts.append(jnp.pad(f, (0, (-f.shape[0]) % LANES)))
    flat = jnp.concatenate(parts)
    rows = flat.shape[0] // LANES
    flat = jnp.pad(flat, (0, ((-rows) % 8) * LANES))
    return flat.reshape(-1, LANES)


def _unpack_small(packed, shapes):
    flat = packed.reshape(-1)
    out, pos = [], 0
    for s in shapes:
        n = int(np.prod(s))
        out.append(flat[pos:pos + n].reshape(s))
        pos += n + ((-n) % LANES)
    return out


def _shard_major(a, axis):
    parts = jnp.split(a, N_CHIPS, axis=axis)
    return jnp.stack(parts, axis=0)


def kernel(x, norm_gain, w_in, fox_forget_bias, rwkv_shift_mix, rwkv_w0, rwkv_w2, rwkv_a0, rwkv_a2, rwkv_k_k, rwkv_k_a, rwkv_r_k, rwkv_ln_w, rwkv_ln_b, w_proj_fox, w_proj_rwkv, w_out, final_norm_gain, loss_target, m_norm_gain, m_w_in, m_fox_forget_bias, m_rwkv_shift_mix, m_rwkv_w0, m_rwkv_w2, m_rwkv_a0, m_rwkv_a2, m_rwkv_k_k, m_rwkv_k_a, m_rwkv_r_k, m_rwkv_ln_w, m_rwkv_ln_b, m_w_proj_fox, m_w_proj_rwkv, m_w_out, m_final_norm_gain, v_norm_gain, v_w_in, v_fox_forget_bias, v_rwkv_shift_mix, v_rwkv_w0, v_rwkv_w2, v_rwkv_a0, v_rwkv_a2, v_rwkv_k_k, v_rwkv_k_a, v_rwkv_r_k, v_rwkv_ln_w, v_rwkv_ln_b, v_w_proj_fox, v_w_proj_rwkv, v_w_out, v_final_norm_gain):
    args = dict(locals())
    T, D = x.shape[1], x.shape[2]
    lora = rwkv_w2.shape[1]
    cfg = _Cfg(T, D, lora)
    RW = cfg.RW
    c_idx = lax.axis_index("c").astype(jnp.int32).reshape(1)
    me_chip = (2 * lax.axis_index("x") + lax.axis_index("y")).astype(jnp.int32)
    place = jnp.concatenate([me_chip.reshape(1), c_idx])

    w_in_s = w_in[0].astype(BF16)
    lora_s = jnp.concatenate([rwkv_w2[0], rwkv_a2[0]], axis=0)
    own_slot = lambda g, own: lax.dynamic_update_slice(g, own[None], (me_chip, 0, 0))
    w_my = _shards_to_my_layout(cfg, own_slot(_gather_weights([w_in_s])[0], w_in_s))
    mine = [_cast_bf16("cast_w_proj_fox", w_proj_fox[0]), _cast_bf16("cast_w_proj_rwkv", w_proj_rwkv[0]),
            _cast_bf16("cast_w_out", w_out[0]), lora_s]

    def unpack(gathered):
        g_wpf, g_wpr, g_out, g_lora = [own_slot(g, own) for g, own in zip(gathered, mine)]
        lo = g_lora.transpose(1, 0, 2).reshape(2 * lora, RW)
        return (lo[:lora], lo[lora:], g_wpf.transpose(1, 0, 2).reshape(RW, D),
                g_wpr.transpose(1, 0, 2).reshape(RW, D), g_out.reshape(D, D))

    early, late = ["w_proj_fox", "w_proj_rwkv", "w_out"], ["w_in", "lora"]
    names = early + late
    chip_sums, direct = {}, {}
    n1_slot = (2 * (lax.axis_index("x") ^ (1 - lax.axis_index("c")))
               + (lax.axis_index("y") ^ lax.axis_index("c"))).astype(jnp.int32).reshape(1)

    def exchange(got):
        if isinstance(got, tuple):
            h, du, d_w2, d_a2 = got
            c, half = lax.axis_index("c"), D // 2
            cols = lambda base: lax.dynamic_slice_in_dim(h, base * half, half, axis=1)
            lora_g = _shard_major(jnp.concatenate([d_w2, d_a2], axis=0).astype(BF16), 1)
            lora_rows = lambda base: lax.dynamic_slice_in_dim(lora_g, base * lora, lora, axis=1).reshape(-1, RW // 4)
            tiles = (BF16, min(1024, half), cfg.tn, 2048)
            sent = _mm("in_proj_dw_sibling", cols(1 - c), du, "tn", *tiles)
            kept, got_w, got_l = _mm("in_proj_dw", cols(c), du, "tn", *tiles,
                                     comm=_swap_with_sibling([sent, lora_rows(1 - c)]))
            return (_add_pair("add_halves_w_in", kept, got_w),
                    _add_pair("add_halves_lora", lora_rows(c), got_l).reshape(N_CHIPS, lora, RW // 4))
        if isinstance(got, dict):
            if "w_in" in got:
                sums = [_my_layout_to_shards(cfg, got["w_in"][0]), got["w_in"][1]]
                chip_sums.update(zip(late, sums))
                return _scatter_stage1(sums)
            gs = [_shard_major(got["w_proj_fox"], 1), _shard_major(got["w_proj_rwkv"], 1),
                  _shard_major(got["w_out"], 0)]
            recv1 = _exchange_halves("exchange_halves_" + early[0], gs)
            sums = [_add_halves("add_halves_" + nm, g, r, c_idx) for nm, g, r in zip(early, gs, recv1)]
            chip_sums.update(zip(early, sums))
            return _scatter_to_owners(sums)
        direct.update(zip(late, got[0::2]))
        return _scatter_stage2([_add_passed("add_passed_" + nm, chip_sums[nm], g, n1_slot)
                                for nm, g in zip(late, got[1::2])])

    loss_dev, gx, small, _, recv2 = _local_step(
        cfg, x[0], loss_target[0], norm_gain, w_my, fox_forget_bias, rwkv_shift_mix, rwkv_w0, rwkv_a0, rwkv_k_k,
        rwkv_k_a, rwkv_r_k, rwkv_ln_w, rwkv_ln_b, final_norm_gain, (_gather_parts(mine), unpack), exchange)
    loss = lax.psum(loss_dev, ("x", "y", "c"))

    small_shapes = [args[nm].shape for nm in _SMALL]
    packed = _pack_small([small[nm] for nm in _SMALL])
    reduced = [_sum_chips("sum_chips_" + nm, r, chip_sums[nm], place) for nm, r in zip(early, recv2[:3])]
    reduced += [_sum_stages("sum_stages_" + nm, chip_sums[nm], direct[nm], via, place, transposed=nm == "w_in")
                for nm, via in zip(late, recv2[3:])]
    *joined, small_all = _join_halves(reduced, [int(nm == "w_in") for nm in names], packed)
    g_small = _sum_slots("sum_small", small_all)

    grads = dict(zip(_SMALL, _unpack_small(g_small, small_shapes)))
    grads.update({nm: g[None] for nm, g in zip(names, joined) if nm not in ("lora", "w_in")})
    g_lora_f = joined[names.index("lora")]
    grads["rwkv_w2"] = g_lora_f[None, :lora]
    grads["rwkv_a2"] = g_lora_f[None, lora:]

    delta, new_m, new_v = {}, {}, {}
    w_small = _pack_small([args[nm] for nm in _SMALL])
    m_small = _pack_small([args["m_" + nm] for nm in _SMALL])
    v_small = _pack_small([args["v_" + nm] for nm in _SMALL])
    d_s, m_s, v_s = _adamw("adamw_small", w_small, g_small, m_small, v_small)
    for tgt, pk in ((delta, d_s), (new_m, m_s), (new_v, v_s)):
        tgt.update(zip(_SMALL, _unpack_small(pk, small_shapes)))
    t_out = _adamw("adamw_w_in", w_in[0].T, joined[names.index("w_in")], m_w_in[0].T, v_w_in[0].T, copy_grad=True)
    delta["w_in"], new_m["w_in"], new_v["w_in"], grads["w_in"] = [t.T[None] for t in t_out]
    for nm in ("w_proj_fox", "w_proj_rwkv", "w_out", "rwkv_w2", "rwkv_a2"):
        shp = args[nm].shape
        two_d = (shp[1], shp[2])
        d_b, m_b, v_b = _adamw("adamw_" + nm, args[nm].reshape(two_d), grads[nm].reshape(two_d),
                               args["m_" + nm].reshape(two_d), args["v_" + nm].reshape(two_d))
        delta[nm], new_m[nm], new_v[nm] = d_b.reshape(shp), m_b.reshape(shp), v_b.reshape(shp)

    return (loss, gx[None], *[grads[n] for n in _WEIGHTS], *[delta[n] for n in _WEIGHTS],
            *[new_m[n] for n in _WEIGHTS], *[new_v[n] for n in _WEIGHTS])
```

```python
import functools

import numpy as np
import jax
import jax.numpy as jnp
from jax import lax
from jax.experimental import pallas as pl
from jax.experimental.pallas import tpu as pltpu

F32 = jnp.float32
BF16 = jnp.bfloat16
HI = lax.Precision.HIGHEST
MESH = pl.DeviceIdType.MESH

FOX_HEAD_DIM = 128
RWKV_HEAD_DIM = 64
RMS_EPS = 1e-6
GN_EPS = 64e-5
L2_EPS = 1e-12
ADAM_LR = 0.001
ADAM_B1 = 0.9
ADAM_B2 = 0.999
ADAM_EPS = 1e-08
ADAM_WD = 0.01
ADAM_STEP = 10

LANES = 128
VMEM_LIMIT = 56 * 1024 * 1024
SCAN_CHUNK = 64
SCAN_HEADS_PER_STEP = 16
SCAN_PASSES = (3, 1, 1)
N_CHIPS = 4
N_DEV = 8

_pcall = pl.pallas_call


def _cparams(sem=None):
    return pltpu.CompilerParams(dimension_semantics=sem, vmem_limit_bytes=VMEM_LIMIT)


def _softplus(x):
    return jnp.maximum(x, 0.0) + jnp.log(1.0 + jnp.exp(-jnp.abs(x)))


def _silu(z):
    return z * jax.nn.sigmoid(z)


def _rmsn(x, g):
    return x * lax.rsqrt(jnp.mean(x * x, axis=-1, keepdims=True) + RMS_EPS) * g


def _dot(a, b, dims="nn", precision=None):
    dn = {"nn": (((1,), (0,)), ((), ())), "nt": (((1,), (1,)), ((), ())), "tn": (((0,), (0,)), ((), ()))}[dims]
    return lax.dot_general(a, b, dn, precision=precision, preferred_element_type=F32)


def _split_bf16(x):
    hi = x.astype(BF16)
    return hi, (x - hi.astype(F32)).astype(BF16)


def _bdot_raw(a, b, ca, cb, passes):
    dn = (((ca,), (cb,)), ((0,), (0,)))
    mm = lambda p, q: lax.dot_general(p, q, dn, preferred_element_type=F32)
    if passes == 1:
        return mm(a.astype(BF16), b.astype(BF16))
    ah, al = _split_bf16(a)
    bh, bl = _split_bf16(b)
    return mm(ah, bh) + (mm(ah, bl) + mm(al, bh))


@functools.partial(jax.custom_vjp, nondiff_argnums=(2, 3, 4))
def _bdot_p(a, b, ca, cb, passes):
    return _bdot_raw(a, b, ca, cb, passes)


def _bdot_fwd(a, b, ca, cb, passes):
    return _bdot_raw(a, b, ca, cb, passes), (a, b)


def _bdot_bwd(ca, cb, passes, res, g):
    a, b = res
    if (ca, cb) == (2, 1):
        return _bdot_p(g, b, 2, 2, passes), _bdot_p(a, g, 1, 1, passes)
    if (ca, cb) == (2, 2):
        return _bdot_p(g, b, 2, 1, passes), _bdot_p(g, a, 1, 1, passes)
    assert (ca, cb) == (1, 1)
    return _bdot_p(b, g, 2, 2, passes), _bdot_p(a, g, 2, 1, passes)


_bdot_p.defvjp(_bdot_fwd, _bdot_bwd)


def _bdot(a, b, ca, cb, passes=3):
    return _bdot_p(a, b, ca, cb, passes)


def _dot3(a, b):
    return _bdot(a[None], b[None], 2, 1)[0]


@jax.custom_vjp
def _xdot(x, m, mt):
    hi, lo = _split_bf16(x)
    m16 = m.astype(BF16)
    return _dot(hi, m16) + _dot(lo, m16)


def _xdot_fwd(x, m, mt):
    return _xdot(x, m, mt), (m, mt)


def _xdot_bwd(res, g):
    m, mt = res
    return _xdot(g, mt, m), jnp.zeros_like(m), jnp.zeros_like(mt)


_xdot.defvjp(_xdot_fwd, _xdot_bwd)


class _Cfg:
    def __init__(self, T, D, lora):
        self.T, self.D, self.lora = T, D, lora
        self.FW = D // 2
        self.FH = self.FW // FOX_HEAD_DIM
        self.RW = D // 2
        self.RH = self.RW // RWKV_HEAD_DIM
        self.LP = -(-lora // LANES) * LANES
        self.o_fox = 0
        self.o_rwkv = 4 * self.FW
        self.o_gate = self.o_rwkv + 4 * self.RW
        self.o_f = self.o_gate + 2 * D
        self.o_wd = self.o_f + LANES
        self.o_ad = self.o_wd + self.LP
        end = self.o_ad + self.LP
        self.tn = 1280 if D >= 2048 else LANES
        self.ncol = -(-end // self.tn) * self.tn
        self.in_cols = 4 * self.FW + self.FH + 4 * self.RW + 2 * lora + 2 * D
        self.scp = -(-(self.in_cols // N_CHIPS) // LANES) * LANES
        self.rseg = 4 * self.RW + 2 * self.LP
        self.C = min(SCAN_CHUNK, T)
        self.tr = min(256, T)
        self.hb = min(SCAN_HEADS_PER_STEP, self.RH)

    def segments(self):
        FW, FH, RW, lo, D = self.FW, self.FH, self.RW, self.lora, self.D
        g_f = 4 * FW
        g_r = g_f + FH
        g_wd = g_r + 4 * RW
        g_ad = g_wd + lo
        g_g = g_ad + lo
        dh = FOX_HEAD_DIM
        qkv = [(j * FW + h * dh, dh, (3 * h + j) * dh) for h in range(FH) for j in range(3)]
        return qkv + [(3 * FW, FW, 3 * FW), (g_f, FH, self.o_f), (g_r, 4 * RW, self.o_rwkv), (g_wd, lo, self.o_wd),
                      (g_ad, lo, self.o_ad), (g_g, 2 * D, self.o_gate)]


def _shards_to_my_layout(cfg, g):
    R, sc = g.shape[1], g.shape[2]
    segs = sorted(cfg.segments(), key=lambda s: s[2])
    parts, pos = [], 0
    for g0, w, m0 in segs:
        if m0 > pos:
            parts.append(jnp.zeros((R, m0 - pos), g.dtype))
        for s in range(N_CHIPS):
            lo, hi = max(g0, s * sc), min(g0 + w, (s + 1) * sc)
            if lo < hi:
                parts.append(g[s, :, lo - s * sc:hi - s * sc])
        pos = m0 + w
    if cfg.ncol > pos:
        parts.append(jnp.zeros((R, cfg.ncol - pos), g.dtype))
    return jnp.concatenate(parts, axis=1)


def _my_layout_to_shards(cfg, wm):
    sc, R = cfg.in_cols // N_CHIPS, wm.shape[0]
    segs = sorted(cfg.segments(), key=lambda s: s[0])
    shards = []
    for s in range(N_CHIPS):
        parts = []
        for g0, w, m0 in segs:
            lo, hi = max(g0, s * sc), min(g0 + w, (s + 1) * sc)
            if lo < hi:
                parts.append(wm[:, m0 + lo - g0:m0 + hi - g0])
        parts.append(jnp.zeros((R, cfg.scp - sc), wm.dtype))
        shards.append(jnp.concatenate(parts, axis=1))
    return jnp.stack(shards, axis=0)


def _rwkv_vec_to_my(cfg, v):
    RW4, lo, LP = 4 * cfg.RW, cfg.lora, cfg.LP
    z = jnp.zeros((1, LP - lo), v.dtype)
    return jnp.concatenate([v[:, :RW4], v[:, RW4:RW4 + lo], z, v[:, RW4 + lo:], z], axis=1)


def _rwkv_vec_from_my(cfg, v):
    RW4, lo, LP = 4 * cfg.RW, cfg.lora, cfg.LP
    return jnp.concatenate([v[:, :RW4], v[:, RW4:RW4 + lo], v[:, RW4 + LP:RW4 + LP + lo]], axis=1)


def _comm_at(comm, which, steps, cin, cout, scr):
    if not comm or len(comm) <= which:
        return
    lin, total = 0, 1
    for d, n in enumerate(steps):
        lin = lin * n + pl.program_id(d)
        total *= n
    pl.when(lin == {3: 0, 4: total - 1, 5: total // 2}[which])(lambda: comm[which](cin, cout, scr))


def _mm(name, a, b, dims, out_dtype, tm, tn, tk, comm=None):
    (M, K) = a.shape if dims != "tn" else a.shape[::-1]
    N = b.shape[0] if dims == "nt" else b.shape[1]
    tm, tn, tk = min(tm, M), min(tn, N), min(tk, K)
    assert M % tm == 0 and N % tn == 0 and K % tk == 0, (name, M, N, K, tm, tn, tk)
    nk = K // tk
    steps = (M // tm, N // tn, nk)
    c_in, c_out, c_scr = comm[:3] if comm else ([], [], [])
    if dims == "nn":
        a_spec = pl.BlockSpec((tm, tk), lambda i, j, k: (i, k))
        b_spec = pl.BlockSpec((tk, tn), lambda i, j, k: (k, j))
    elif dims == "nt":
        a_spec = pl.BlockSpec((tm, tk), lambda i, j, k: (i, k))
        b_spec = pl.BlockSpec((tn, tk), lambda i, j, k: (j, k))
    else:
        a_spec = pl.BlockSpec((tk, tm), lambda i, j, k: (k, i))
        b_spec = pl.BlockSpec((tk, tn), lambda i, j, k: (k, j))

    n_acc = 1 if nk > 1 else 0

    def body(a_ref, b_ref, *rest):
        cin, o_ref = rest[:len(c_in)], rest[len(c_in)]
        cout = rest[len(c_in) + 1:len(c_in) + 1 + len(c_out)]
        scr = rest[len(c_in) + 1 + len(c_out):]
        _comm_at(comm, 3, steps, cin, cout, scr[n_acc:])
        if nk == 1:
            o_ref[...] = _dot(a_ref[...], b_ref[...], dims).astype(o_ref.dtype)
        else:
            acc_ref, k = scr[0], pl.program_id(2)

            @pl.when(k == 0)
            def _():
                acc_ref[...] = jnp.zeros_like(acc_ref)

            acc_ref[...] += _dot(a_ref[...], b_ref[...], dims)

            @pl.when(k == nk - 1)
            def _():
                o_ref[...] = acc_ref[...].astype(o_ref.dtype)

        _comm_at(comm, 5, steps, cin, cout, scr[n_acc:])
        _comm_at(comm, 4, steps, cin, cout, scr[n_acc:])

    res = _pcall(
        body, name=name, grid=steps,
        in_specs=[a_spec, b_spec] + [_ANY] * len(c_in),
        out_specs=[pl.BlockSpec((tm, tn), lambda i, j, k: (i, j))] + [_ANY] * len(c_out),
        out_shape=[jax.ShapeDtypeStruct((M, N), out_dtype)] + list(c_out),
        scratch_shapes=([pltpu.VMEM((tm, tn), F32)] if nk > 1 else []) + list(c_scr),
        compiler_params=_cparams(("arbitrary",) * 3 if comm else ("parallel", "parallel", "arbitrary")),
    )(a, b, *c_in)
    return res if comm else res[0]


def _tile(tr, w, cb=0):
    return pl.BlockSpec((tr, w), lambda i: (i, cb))


def _const(shape):
    nd = len(shape)
    return pl.BlockSpec(shape, lambda i: (0,) * nd)


def _acc_store(i, ref, val):
    @pl.when(i == 0)
    def _():
        ref[...] = val

    @pl.when(i > 0)
    def _():
        ref[...] += val


def _rms_fwd(cfg, x2, g):
    T, D, tr = cfg.T, cfg.D, cfg.tr

    def body(x_ref, g_ref, h_ref):
        h_ref[...] = _rmsn(x_ref[...], g_ref[...]).astype(BF16)

    return _pcall(body, name="rms_fwd", grid=(T // tr,), in_specs=[_tile(tr, D), _const((1, D))],
                  out_specs=_tile(tr, D), out_shape=jax.ShapeDtypeStruct((T, D), BF16),
                  compiler_params=_cparams(("parallel",)))(x2, g)


def _rms_bwd(cfg, x2, g, dh, dres, comm=None):
    T, D, tr = cfg.T, cfg.D, cfg.tr
    c_in, c_out, c_scr = comm[:3] if comm else ([], [], [])
    steps = (T // tr,)

    def body(x_ref, g_ref, dh_ref, dres_ref, *rest):
        cin, (gx_ref, dg_ref) = rest[:len(c_in)], rest[len(c_in):len(c_in) + 2]
        cout, scr = rest[len(c_in) + 2:len(c_in) + 2 + len(c_out)], rest[len(c_in) + 2 + len(c_out):]
        _comm_at(comm, 3, steps, cin, cout, scr)
        _, vjp = jax.vjp(_rmsn, x_ref[...], g_ref[...])
        dx, dg = vjp(dh_ref[...])
        gx_ref[...] = dx + dres_ref[...]
        _acc_store(pl.program_id(0), dg_ref, dg)
        _comm_at(comm, 4, steps, cin, cout, scr)

    return _pcall(body, name="rms_bwd", grid=steps,
                  in_specs=[_tile(tr, D), _const((1, D)), _tile(tr, D), _tile(tr, D)] + [_ANY] * len(c_in),
                  out_specs=[_tile(tr, D), _const((1, D))] + [_ANY] * len(c_out),
                  out_shape=[jax.ShapeDtypeStruct((T, D), F32), jax.ShapeDtypeStruct((1, D), F32)] + list(c_out),
                  scratch_shapes=list(c_scr), compiler_params=_cparams(("arbitrary",)))(x2, g, dh, dres, *c_in)


def _final(cfg, x2, mo, fg, target):
    T, D, tr = cfg.T, cfg.D, cfg.tr

    def loss_fn(hres, g, tgt):
        err = _rmsn(hres, g) - tgt
        return 0.5 * jnp.sum(jnp.mean(err * err, axis=-1, keepdims=True), axis=0, keepdims=True)

    def body(x_ref, mo_ref, g_ref, t_ref, loss_ref, dres_ref, dres16_ref, dg_ref):
        hres = x_ref[...] + mo_ref[...]
        loss, vjp = jax.vjp(functools.partial(loss_fn, tgt=t_ref[...]), hres, g_ref[...])
        dres, dg = vjp(jnp.ones((1, 1), F32))
        dres_ref[...] = dres
        dres16_ref[...] = dres.astype(BF16)
        i = pl.program_id(0)
        _acc_store(i, dg_ref, dg)
        _acc_store(i, loss_ref, jnp.broadcast_to(loss, (8, LANES)))

    return _pcall(body, name="final_loss", grid=(T // tr,),
                  in_specs=[_tile(tr, D), _tile(tr, D), _const((1, D)), _tile(tr, D)],
                  out_specs=[_const((8, LANES)), _tile(tr, D), _tile(tr, D), _const((1, D))],
                  out_shape=[jax.ShapeDtypeStruct((8, LANES), F32), jax.ShapeDtypeStruct((T, D), F32),
                             jax.ShapeDtypeStruct((T, D), BF16), jax.ShapeDtypeStruct((1, D), F32)],
                  compiler_params=_cparams(("arbitrary",)))(x2, mo, fg, target)


def _merge_fn(pa, pb, ga, gb):
    return jax.nn.sigmoid(ga) * pa + jax.nn.sigmoid(gb) * pb


def _merge_fwd(cfg, pa, pb, u):
    T, D, tr = cfg.T, cfg.D, cfg.tr
    cga, cgb = cfg.o_gate // D, cfg.o_gate // D + 1

    def body(pa_ref, pb_ref, ga_ref, gb_ref, m_ref):
        m_ref[...] = _merge_fn(pa_ref[...], pb_ref[...], ga_ref[...], gb_ref[...]).astype(BF16)

    return _pcall(body, name="merge_fwd", grid=(T // tr,),
                  in_specs=[_tile(tr, D), _tile(tr, D), _tile(tr, D, cga), _tile(tr, D, cgb)],
                  out_specs=_tile(tr, D), out_shape=jax.ShapeDtypeStruct((T, D), BF16),
                  compiler_params=_cparams(("parallel",)))(pa, pb, u, u)


def _merge_bwd(cfg, pa, pb, u, dm):
    T, D, tr = cfg.T, cfg.D, cfg.tr
    cga, cgb = cfg.o_gate // D, cfg.o_gate // D + 1

    def body(pa_ref, pb_ref, ga_ref, gb_ref, dm_ref, dpa_ref, dpb_ref, dg_ref):
        _, vjp = jax.vjp(_merge_fn, pa_ref[...], pb_ref[...], ga_ref[...], gb_ref[...])
        dpa, dpb, dga, dgb = vjp(dm_ref[...])
        dpa_ref[...] = dpa.astype(BF16)
        dpb_ref[...] = dpb.astype(BF16)
        dg_ref[:, :D] = dga.astype(BF16)
        dg_ref[:, D:] = dgb.astype(BF16)

    return _pcall(body, name="merge_bwd", grid=(T // tr,),
                  in_specs=[_tile(tr, D), _tile(tr, D), _tile(tr, D, cga), _tile(tr, D, cgb), _tile(tr, D)],
                  out_specs=[_tile(tr, D), _tile(tr, D), _tile(tr, 2 * D, cfg.o_gate // (2 * D))],
                  out_shape=[jax.ShapeDtypeStruct((T, D), BF16), jax.ShapeDtypeStruct((T, D), BF16),
                             jax.ShapeDtypeStruct((T, cfg.ncol), BF16)],
                  compiler_params=_cparams(("parallel",)))(pa, pb, u, u, dm)


def _gate_fn(o, z):
    return o * _silu(z)


def _gate_a_fwd(cfg, o, u):
    T, FW, tr = cfg.T, cfg.FW, cfg.tr

    def body(o_ref, z_ref, oa_ref):
        oa_ref[...] = _gate_fn(o_ref[...], z_ref[...]).astype(BF16)

    return _pcall(body, name="gate_a_fwd", grid=(T // tr,), in_specs=[_tile(tr, FW), _tile(tr, FW, 3)],
                  out_specs=_tile(tr, FW), out_shape=jax.ShapeDtypeStruct((T, FW), BF16),
                  compiler_params=_cparams(("parallel",)))(o, u)


def _gate_a_bwd(cfg, o, u, doa, du):
    T, FW, tr = cfg.T, cfg.FW, cfg.tr

    def body(o_ref, z_ref, doa_ref, du_in, do_ref, dz_ref):
        _, vjp = jax.vjp(_gate_fn, o_ref[...], z_ref[...])
        do, dz = vjp(doa_ref[...])
        do_ref[...] = do
        dz_ref[...] = dz.astype(BF16)

    return _pcall(body, name="gate_a_bwd", grid=(T // tr,),
                  in_specs=[_tile(tr, FW), _tile(tr, FW, 3), _tile(tr, FW), _ANY],
                  out_specs=[_tile(tr, FW), _tile(tr, FW, 3)],
                  out_shape=[jax.ShapeDtypeStruct((T, FW), F32), jax.ShapeDtypeStruct(du.shape, BF16)],
                  input_output_aliases={3: 1},
                  compiler_params=_cparams(("parallel",)))(o, u, doa, du)


def _fox_prep(cfg, u, fb):
    T, tr = cfg.T, cfg.tr
    cf = cfg.o_f // LANES

    def body(f_ref, fb_ref, c_ref, carry_ref):
        i = pl.program_id(0)

        @pl.when(i == 0)
        def _():
            carry_ref[...] = jnp.zeros_like(carry_ref)

        lf = -_softplus(-(f_ref[...] + fb_ref[...]))
        r = lax.broadcasted_iota(jnp.int32, (tr, tr), 0)
        c = lax.broadcasted_iota(jnp.int32, (tr, tr), 1)
        tri = (r >= c).astype(F32)
        c_ref[...] = _dot(tri, lf, precision=HI) + carry_ref[...]
        carry_ref[...] += jnp.sum(lf, axis=0, keepdims=True)

    return _pcall(body, name="fox_prep", grid=(T // tr,), in_specs=[_tile(tr, LANES, cf), _const((1, LANES))],
                  out_specs=_tile(tr, LANES), out_shape=jax.ShapeDtypeStruct((T, LANES), F32),
                  scratch_shapes=[pltpu.VMEM((1, LANES), F32)], compiler_params=_cparams(("arbitrary",)))(u, fb)


def _fox_prep_bwd(cfg, u, fb, dc):
    T, tr = cfg.T, cfg.tr
    cf = cfg.o_f // LANES
    nb = T // tr

    def body(f_ref, fb_ref, dc_ref, df_ref, dfb_ref, carry_ref):
        i = pl.program_id(0)

        @pl.when(i == 0)
        def _():
            carry_ref[...] = jnp.zeros_like(carry_ref)

        dc = dc_ref[...]
        r = lax.broadcasted_iota(jnp.int32, (tr, tr), 0)
        c = lax.broadcasted_iota(jnp.int32, (tr, tr), 1)
        triu = (r <= c).astype(F32)
        dlf = _dot(triu, dc, precision=HI) + carry_ref[...]
        carry_ref[...] += jnp.sum(dc, axis=0, keepdims=True)
        dz = dlf * jax.nn.sigmoid(-(f_ref[...] + fb_ref[...]))
        df_ref[...] = dz.astype(BF16)
        _acc_store(i, dfb_ref, jnp.sum(dz, axis=0, keepdims=True))

    rev = lambda i: (nb - 1 - i, 0)
    return _pcall(body, name="fox_prep_bwd", grid=(nb,),
                  in_specs=[pl.BlockSpec((tr, LANES), lambda i: (nb - 1 - i, cf)), _const((1, LANES)),
                            pl.BlockSpec((tr, LANES), rev)],
                  out_specs=[pl.BlockSpec((tr, LANES), rev), _const((1, LANES))],
                  out_shape=[jax.ShapeDtypeStruct((T, LANES), BF16), jax.ShapeDtypeStruct((1, LANES), F32)],
                  scratch_shapes=[pltpu.VMEM((1, LANES), F32)], compiler_params=_cparams(("arbitrary",)))(u, fb, dc)


def _attn_logits(q_ref, k_ref, c_ref, tq, te):
    q = q_ref[...].astype(BF16)
    scale = FOX_HEAD_DIM ** -0.5
    part = lambda k0, k1: _dot(q, k_ref[k0:k1, :].astype(BF16), "nt") * scale - c_ref[0, :, k0:k1]
    row = lax.broadcasted_iota(jnp.int32, (tq, tq), 0)
    col = lax.broadcasted_iota(jnp.int32, (tq, tq), 1)
    own = ((te - tq, te), jnp.where(col <= row, part(te - tq, te), -1e30))
    return [((0, te - tq), part(0, te - tq)), own] if te > tq else [own]


def _per_query_tile(i, nq, tq, fn):
    for ii in range(nq):
        pl.when(i == ii)(functools.partial(fn, (ii + 1) * tq))


def _attn_fwd(cfg, u, c_rows):
    T, FW, FH = cfg.T, cfg.FW, cfg.FH
    tq = min(256, T)
    dh = FOX_HEAD_DIM

    def body(q_ref, k_ref, v_ref, c_ref, o_ref, lse_ref):
        i = pl.program_id(1)

        def tile(te):
            parts = _attn_logits(q_ref, k_ref, c_ref, tq, te)
            m = functools.reduce(jnp.maximum, [jnp.max(s, axis=1, keepdims=True) for _, s in parts])
            l, acc = 0.0, 0.0
            for (k0, k1), s in parts:
                p = jnp.exp(s - m)
                l = l + jnp.sum(p, axis=1, keepdims=True)
                acc = acc + _dot(p.astype(BF16), v_ref[k0:k1, :].astype(BF16))
            o_ref[...] = acc / l
            lse_ref[0] = m + jnp.log(l)

        _per_query_tile(i, T // tq, tq, tile)

    return _pcall(
        body, name="fox_attn_fwd", grid=(FH, T // tq),
        in_specs=[pl.BlockSpec((tq, dh), lambda h, i: (i, 3 * h)), pl.BlockSpec((T, dh), lambda h, i: (0, 3 * h + 1)),
                  pl.BlockSpec((T, dh), lambda h, i: (0, 3 * h + 2)), pl.BlockSpec((1, 1, T), lambda h, i: (h, 0, 0))],
        out_specs=[pl.BlockSpec((tq, dh), lambda h, i: (i, h)), pl.BlockSpec((1, tq, 1), lambda h, i: (h, i, 0))],
        out_shape=[jax.ShapeDtypeStruct((T, FW), F32), jax.ShapeDtypeStruct((FH, T, 1), F32)],
        compiler_params=_cparams(("parallel", "arbitrary")),
    )(u, u, u, c_rows)


def _attn_bwd(cfg, u, c_rows, lse, do, du):
    T, FW, FH = cfg.T, cfg.FW, cfg.FH
    tq = min(256, T)
    nq = T // tq
    dh = FOX_HEAD_DIM
    scale = dh ** -0.5

    def body(q_ref, k_ref, v_ref, c_ref, lse_ref, do_ref, du_in, du_ref, dcol_ref, dk_acc, dv_acc):
        i = pl.program_id(1)

        @pl.when(i == 0)
        def _():
            dk_acc[...] = jnp.zeros_like(dk_acc)
            dv_acc[...] = jnp.zeros_like(dv_acc)
            dcol_ref[...] = jnp.zeros_like(dcol_ref)

        def tile(te):
            lse, q16, do16 = lse_ref[0], q_ref[...].astype(BF16), do_ref[...].astype(BF16)
            probs = [(ks, jnp.exp(s - lse)) for ks, s in _attn_logits(q_ref, k_ref, c_ref, tq, te)]
            dps = [_dot(do16, v_ref[k0:k1, :].astype(BF16), "nt") for (k0, k1), _ in probs]
            delta = sum(jnp.sum(p * dp, axis=1, keepdims=True) for (_, p), dp in zip(probs, dps))
            dq = 0.0
            for ((k0, k1), p), dp in zip(probs, dps):
                ds = p * (dp - delta)
                ds16 = ds.astype(BF16)
                dq = dq + _dot(ds16, k_ref[k0:k1, :].astype(BF16))
                dk_acc[k0:k1, :] += _dot(ds16, q16, "tn") * scale
                dv_acc[k0:k1, :] += _dot(p.astype(BF16), do16, "tn")
                dcol_ref[0, :, k0:k1] += jnp.sum(ds, axis=0, keepdims=True)
            du_ref[te - tq:te, 0:dh] = (dq * scale).astype(BF16)

        _per_query_tile(i, nq, tq, tile)

        @pl.when(i == nq - 1)
        def _():
            du_ref[:, dh:2 * dh] = dk_acc[...].astype(BF16)
            du_ref[:, 2 * dh:3 * dh] = dv_acc[...].astype(BF16)

    return _pcall(
        body, name="fox_attn_bwd", grid=(FH, nq),
        in_specs=[pl.BlockSpec((tq, dh), lambda h, i: (i, 3 * h)), pl.BlockSpec((T, dh), lambda h, i: (0, 3 * h + 1)),
                  pl.BlockSpec((T, dh), lambda h, i: (0, 3 * h + 2)), pl.BlockSpec((1, 1, T), lambda h, i: (h, 0, 0)),
                  pl.BlockSpec((1, tq, 1), lambda h, i: (h, i, 0)), pl.BlockSpec((tq, dh), lambda h, i: (i, h)), _ANY],
        out_specs=[pl.BlockSpec((T, 3 * dh), lambda h, i: (0, h)), pl.BlockSpec((1, 1, T), lambda h, i: (h, 0, 0))],
        out_shape=[jax.ShapeDtypeStruct(du.shape, BF16), jax.ShapeDtypeStruct((FH, 1, T), F32)],
        scratch_shapes=[pltpu.VMEM((T, dh), F32), pltpu.VMEM((T, dh), F32)],
        input_output_aliases={6: 0},
        compiler_params=_cparams(("parallel", "arbitrary")),
    )(u, u, u, c_rows, lse, do, du)


def _head_indicators(cfg):
    ind = np.zeros((cfg.RW, LANES), np.float32)
    ind[np.arange(cfg.RW), np.arange(cfg.RW) // RWKV_HEAD_DIM] = 1.0
    pad = np.zeros((1, LANES), np.float32)
    pad[0, cfg.RH:] = 1.0
    return jnp.asarray(ind), jnp.asarray(ind.T.copy()), jnp.asarray(pad)


def _prep_fn(us_r, us_k, us_v, us_wd, us_ad, w0, w2p, a0, a2p, k_k, k_a, ind, ind_t, pad):
    wpre = w0 + _dot3(jnp.tanh(us_wd), w2p)
    w = -_softplus(-wpre) - 0.5
    lw = -jnp.exp(w)
    a = jax.nn.sigmoid(a0 + _dot3(us_ad, a2p))
    kk = us_k * k_k
    ss = _xdot(kk * kk, ind, ind_t) + pad
    inv = 1.0 / jnp.maximum(jnp.sqrt(ss), L2_EPS)
    kkn = kk * _xdot(inv, ind_t, ind)
    kp = us_k * (1.0 + (a - 1.0) * k_a)
    return us_r, lw, kp, us_v, -kkn, kkn * a


def _shifted(u, prev_row, mu, first):
    n = u.shape[0]
    rolled = pltpu.roll(u, 1, 0)
    row = lax.broadcasted_iota(jnp.int32, u.shape, 0)
    p0 = jnp.where(first, jnp.zeros_like(prev_row), prev_row)
    prev = jnp.where(row == 0, jnp.broadcast_to(p0, u.shape), rolled)
    return u + (prev - u) * mu, prev


def _rwkv_specs(cfg, tr):
    RW, LP = cfg.RW, cfg.LP
    base = cfg.o_rwkv // RW
    cols = [(RW, base), (RW, base + 1), (RW, base + 2), (RW, base + 3), (LP, cfg.o_wd // LP), (LP, cfg.o_ad // LP)]
    cur = [pl.BlockSpec((tr, w), (lambda i, cb=cb: (i, cb))) for w, cb in cols]
    prv = [pl.BlockSpec((8, w), (lambda i, cb=cb: (jnp.maximum(i * (tr // 8) - 1, 0), cb))) for w, cb in cols]
    return cols, cur, prv


def _mu_pieces(cfg, mu_ref):
    RW, LP = cfg.RW, cfg.LP
    offs = [0, RW, 2 * RW, 3 * RW, 4 * RW, 4 * RW + LP, 4 * RW + 2 * LP]
    return [mu_ref[:, offs[j]:offs[j + 1]] for j in range(6)]


def _rwkv_prep_fwd(cfg, u, mu, w0, w2p, a0, a2p, k_k, k_a):
    T, RW, LP, tr = cfg.T, cfg.RW, cfg.LP, cfg.tr
    ind, ind_t, pad = _head_indicators(cfg)
    cols, cur, prv = _rwkv_specs(cfg, tr)

    def body(*refs):
        u_refs, p_refs = refs[0:6], refs[6:12]
        mu_ref, w0_ref, w2_ref, a0_ref, a2_ref, kk_ref, ka_ref, ind_ref, indt_ref, pad_ref = refs[12:22]
        outs = refs[22:]
        first = pl.program_id(0) == 0
        mus = _mu_pieces(cfg, mu_ref)
        us = [_shifted(u_refs[j][...], p_refs[j][7:8, :], mus[j], first)[0] for j in range(6)]
        res = _prep_fn(us[0], us[1], us[2], us[4], us[5], w0_ref[...], w2_ref[...], a0_ref[...], a2_ref[...],
                       kk_ref[...], ka_ref[...], ind_ref[...], indt_ref[...], pad_ref[...])
        for j in range(6):
            outs[j][...] = res[j]
        outs[6][...] = us[3]

    consts = [mu, w0, w2p, a0, a2p, k_k, k_a, ind, ind_t, pad]
    return _pcall(body, name="rwkv_prep_fwd", grid=(T // tr,),
                  in_specs=cur + prv + [_const(c.shape) for c in consts],
                  out_specs=[_tile(tr, RW)] * 7, out_shape=[jax.ShapeDtypeStruct((T, RW), F32)] * 7,
                  compiler_params=_cparams(("parallel",)))(*([u] * 12), *consts)


def _rwkv_prep_bwd(cfg, u, mu, w0, w2p, a0, a2p, k_k, k_a, cots, dzb):
    T, RW, LP = cfg.T, cfg.RW, cfg.LP
    tr = min(128, T)
    ind, ind_t, pad = _head_indicators(cfg)
    cols, cur, prv = _rwkv_specs(cfg, tr)
    rseg = cfg.rseg

    def body(*refs):
        u_refs, p_refs = refs[0:6], refs[6:12]
        mu_ref, w0_ref, w2_ref, a0_ref, a2_ref, kk_ref, ka_ref, ind_ref, indt_ref, pad_ref = refs[12:22]
        cot_refs, dzb_ref = refs[22:28], refs[28]
        dus_ref, dmu_ref, dw0_ref, dw2_ref, da0_ref, da2_ref, dkk_ref, dka_ref = refs[29:]
        i = pl.program_id(0)
        first = i == 0
        mus = _mu_pieces(cfg, mu_ref)
        sh = [_shifted(u_refs[j][...], p_refs[j][7:8, :], mus[j], first) for j in range(6)]
        us = [s[0] for s in sh]
        fn = functools.partial(_prep_fn, ind=ind_ref[...], ind_t=indt_ref[...], pad=pad_ref[...])
        _, vjp = jax.vjp(fn, us[0], us[1], us[2], us[4], us[5], w0_ref[...], w2_ref[...], a0_ref[...], a2_ref[...],
                         kk_ref[...], ka_ref[...])
        d = vjp(tuple(c[...] for c in cot_refs))
        dus = [d[0], d[1], d[2], dzb_ref[...], d[3], d[4]]
        offs = [0, RW, 2 * RW, 3 * RW, 4 * RW, 4 * RW + LP, 4 * RW + 2 * LP]
        for j in range(6):
            dus_ref[:, offs[j]:offs[j + 1]] = dus[j]
            dmu_j = jnp.sum(dus[j] * (sh[j][1] - u_refs[j][...]), axis=0, keepdims=True)

            @pl.when(first)
            def _(j=j, dmu_j=dmu_j):
                dmu_ref[:, offs[j]:offs[j + 1]] = dmu_j

            @pl.when(i > 0)
            def _(j=j, dmu_j=dmu_j):
                dmu_ref[:, offs[j]:offs[j + 1]] += dmu_j
        for ref, val in zip((dw0_ref, dw2_ref, da0_ref, da2_ref, dkk_ref, dka_ref), d[5:11]):
            _acc_store(i, ref, val)

    consts = [mu, w0, w2p, a0, a2p, k_k, k_a, ind, ind_t, pad]
    vec = jax.ShapeDtypeStruct((1, RW), F32)
    mat = jax.ShapeDtypeStruct((LP, RW), F32)
    return _pcall(body, name="rwkv_prep_bwd", grid=(T // tr,),
                  in_specs=cur + prv + [_const(c.shape) for c in consts] + [_tile(tr, RW)] * 7,
                  out_specs=[_tile(tr, rseg), _const((1, rseg)), _const((1, RW)), _const((LP, RW)), _const((1, RW)),
                             _const((LP, RW)), _const((1, RW)), _const((1, RW))],
                  out_shape=[jax.ShapeDtypeStruct((T, rseg), F32), jax.ShapeDtypeStruct((1, rseg), F32),
                             vec, mat, vec, mat, vec, vec],
                  compiler_params=_cparams(("arbitrary",)))(*([u] * 12), *consts, *cots, dzb)


def _shift_bwd(cfg, dus, mu, df, du):
    T, tr, RW, LP = cfg.T, cfg.tr, cfg.RW, cfg.LP
    nb = T // tr
    tail = cfg.ncol - cfg.o_f
    assert cfg.o_rwkv % (4 * RW) == 0 and (4 * RW) % (2 * LP) == 0 and cfg.o_f % tail == 0

    def shifted(d_ref, n_ref, mu_ref):
        d = d_ref[...]
        rolled = pltpu.roll(d, tr - 1, 0)
        row = lax.broadcasted_iota(jnp.int32, d.shape, 0)
        n0 = jnp.where(pl.program_id(0) == nb - 1, jnp.zeros_like(n_ref[0:1, :]), n_ref[0:1, :])
        nxt = jnp.where(row == tr - 1, jnp.broadcast_to(n0, d.shape), rolled)
        mu_v = mu_ref[...]
        return (d * (1.0 - mu_v) + nxt * mu_v).astype(BF16)

    def main_body(d_ref, n_ref, mu_ref, du_in, du_ref):
        du_ref[...] = shifted(d_ref, n_ref, mu_ref)

    def tail_body(d_ref, n_ref, mu_ref, df_ref, du_in, du_ref):
        du_ref[:, 0:LANES] = df_ref[...]
        du_ref[:, LANES:LANES + 2 * LP] = shifted(d_ref, n_ref, mu_ref)
        if tail > LANES + 2 * LP:
            du_ref[:, LANES + 2 * LP:] = jnp.zeros((tr, tail - LANES - 2 * LP), BF16)

    def specs(w, cb):
        return [_tile(tr, w, cb),
                pl.BlockSpec((8, w), lambda i: (jnp.minimum((i + 1) * (tr // 8), T // 8 - 1), cb)),
                pl.BlockSpec((1, w), lambda i: (0, cb))]

    out = jax.ShapeDtypeStruct(du.shape, BF16)
    du = _pcall(main_body, name="shift_bwd_main", grid=(nb,), in_specs=specs(4 * RW, 0) + [_ANY],
                out_specs=_tile(tr, 4 * RW, cfg.o_rwkv // (4 * RW)), out_shape=out, input_output_aliases={3: 0},
                compiler_params=_cparams(("parallel",)))(dus, dus, mu, du)
    return _pcall(tail_body, name="shift_bwd_tail", grid=(nb,),
                  in_specs=specs(2 * LP, 4 * RW // (2 * LP)) + [_tile(tr, LANES), _ANY],
                  out_specs=_tile(tr, tail, cfg.o_f // tail), out_shape=out, input_output_aliases={4: 0},
                  compiler_params=_cparams(("parallel",)))(dus, dus, mu, df, du)


def _chunk_local(r, lw, k, v, a, b):
    H, C, K = r.shape
    row = lax.broadcasted_iota(jnp.int32, (C, C), 0)
    col = lax.broadcasted_iota(jnp.int32, (C, C), 1)
    incl = jnp.broadcast_to((row >= col).astype(F32)[None], (H, C, C))
    strict = (row > col)[None]
    lower = (row >= col)[None]
    eye = (row == col)[None]
    zero = jnp.zeros((), F32)
    L = _bdot(incl, lw, 2, 1)
    LC = jnp.sum(lw, axis=1, keepdims=True)
    eL = jnp.exp(L)
    eLn = jnp.exp(-L)
    at = a * jnp.exp(L - lw)
    rt = r * eL
    bt = b * eLn
    kt = k * eLn
    eR = jnp.exp(LC - L)
    bh = b * eR
    kh = k * eR
    gram = functools.partial(_bdot, passes=SCAN_PASSES[0])
    inv = functools.partial(_bdot, passes=SCAN_PASSES[1])
    app = functools.partial(_bdot, passes=SCAN_PASSES[2])
    ar = jnp.concatenate([at, rt], axis=1)
    g_b = app(ar, bt, 2, 2)
    g_k = gram(ar, kt, 2, 2)
    n_ab = jnp.where(strict, g_b[:, :C], zero)
    n_ak = jnp.where(strict, g_k[:, :C], zero)
    m_rb = jnp.where(lower, g_b[:, C:], zero)
    m_rk = jnp.where(lower, g_k[:, C:], zero)
    M = n_ab
    P = jnp.where(eye, 1.0, zero) + n_ab
    for _ in range(1, max(1, int(np.ceil(np.log2(C))))):
        M = inv(M, M, 2, 1)
        P = P + inv(M, P, 2, 1)
    W = app(P, at, 2, 1)
    Uloc = app(P, app(n_ak, v, 2, 1), 2, 1)
    Q = rt + app(m_rb, W, 2, 1)
    Yloc = app(m_rb, Uloc, 2, 1) + app(m_rk, v, 2, 1)
    A = jnp.where(eye, jnp.exp(LC), zero) + app(W, bh, 1, 1)
    Sloc = app(Uloc, bh, 1, 1) + app(v, kh, 1, 1)
    return Q, Yloc, A, Sloc


def _split_heads(ref, n):
    N = RWKV_HEAD_DIM
    return jnp.stack([ref[:, h * N:(h + 1) * N] for h in range(n)], axis=0)


def _merge_heads(x):
    return jnp.concatenate([x[h] for h in range(x.shape[0])], axis=1)


def _scan_local_specs(cfg):
    N, HB = RWKV_HEAD_DIM, cfg.hb
    grid = (cfg.RH // HB, cfg.T // cfg.C)
    seq = pl.BlockSpec((HB, cfg.C, N), lambda h, j: (h, j, 0))
    mat = pl.BlockSpec((HB, 1, N, N), lambda h, j: (h, j, 0, 0))
    return grid, seq, mat


def _scan_local_fwd(cfg, seqs):
    T, RH, N = cfg.T, cfg.RH, RWKV_HEAD_DIM
    grid, seq, mat = _scan_local_specs(cfg)

    def body(r_ref, lw_ref, k_ref, v_ref, a_ref, b_ref, q_ref, yl_ref, a_out, sl_ref):
        Q, Yloc, A, Sloc = _chunk_local(*[_split_heads(ref, cfg.hb) for ref in (r_ref, lw_ref, k_ref, v_ref, a_ref, b_ref)])
        q_ref[...] = Q
        yl_ref[...] = Yloc
        a_out[:, 0] = A
        sl_ref[:, 0] = Sloc

    tok = pl.BlockSpec((cfg.C, cfg.hb * N), lambda h, j: (j, h))
    sq = jax.ShapeDtypeStruct((RH, T, N), F32)
    mt = jax.ShapeDtypeStruct((RH, T // cfg.C, N, N), F32)
    return _pcall(body, name="rwkv_scan_local_fwd", grid=grid, in_specs=[tok] * 6, out_specs=[seq, seq, mat, mat],
                  out_shape=[sq, sq, mt, mt], compiler_params=_cparams(("parallel", "parallel")))(*seqs)


def _scan_local_bwd(cfg, toks, dq, dy, da, dsl, extra, comm=None):
    T, RW, N = cfg.T, cfg.RW, RWKV_HEAD_DIM
    grid, seq, mat = _scan_local_specs(cfg)
    c_in, c_out, c_scr = comm[:3] if comm else ([], [], [])

    def body(r_ref, lw_ref, k_ref, v_ref, a_ref, b_ref, dq_ref, dy_ref, da_ref, dsl_ref, xr_ref, xk_ref, xv_ref,
             *rest):
        cin, outs = rest[:len(c_in)], rest[len(c_in):len(c_in) + 6]
        cout, scr = rest[len(c_in) + 6:len(c_in) + 6 + len(c_out)], rest[len(c_in) + 6 + len(c_out):]
        _comm_at(comm, 3, grid, cin, cout, scr)
        ins = [_split_heads(ref, cfg.hb) for ref in (r_ref, lw_ref, k_ref, v_ref, a_ref, b_ref)]
        _, vjp = jax.vjp(_chunk_local, *ins)
        d = vjp((dq_ref[...], _split_heads(dy_ref, cfg.hb), da_ref[:, 0], dsl_ref[:, 0]))
        add = {0: xr_ref, 2: xk_ref, 3: xv_ref}
        for j in range(6):
            dj = _merge_heads(d[j])
            outs[j][...] = dj + add[j][...] if j in add else dj
        _comm_at(comm, 4, grid, cin, cout, scr)

    tok = pl.BlockSpec((cfg.C, cfg.hb * N), lambda h, j: (j, h))
    return _pcall(body, name="rwkv_scan_local_bwd", grid=grid,
                  in_specs=[tok] * 6 + [seq, tok, mat, mat] + [tok] * 3 + [_ANY] * len(c_in),
                  out_specs=[tok] * 6 + [_ANY] * len(c_out),
                  out_shape=[jax.ShapeDtypeStruct((T, RW), F32)] * 6 + list(c_out), scratch_shapes=list(c_scr),
                  compiler_params=_cparams(("arbitrary", "arbitrary") if comm else ("parallel", "parallel")),
                  )(*toks, dq, dy, da, dsl, *extra, *c_in)


def _scan_carry_specs(cfg, rev):
    N, RH, C, nc = RWKV_HEAD_DIM, cfg.RH, cfg.C, cfg.T // cfg.C
    at = (lambda j: nc - 1 - j) if rev else (lambda j: j)
    seq = pl.BlockSpec((RH, C, N), lambda j: (0, at(j), 0))
    mat = pl.BlockSpec((RH, 1, N, N), lambda j: (0, at(j), 0, 0))
    return nc, seq, mat


def _scan_carry_fwd(cfg, q, yloc, a, sloc):
    T, RH, N = cfg.T, cfg.RH, RWKV_HEAD_DIM
    nc, seq, mat = _scan_carry_specs(cfg, False)

    def body(q_ref, yl_ref, a_ref, sl_ref, y_ref, ck_ref, s_ref):
        @pl.when(pl.program_id(0) == 0)
        def _():
            s_ref[...] = jnp.zeros_like(s_ref)

        S = s_ref[...]
        ck_ref[:, 0] = S
        y_ref[...] = _merge_heads(_bdot(q_ref[...], S, 2, 2, SCAN_PASSES[2]) + yl_ref[...])
        s_ref[...] = _bdot(S, a_ref[:, 0], 2, 1) + sl_ref[:, 0]

    tok = pl.BlockSpec((cfg.C, cfg.RW), lambda j: (j, 0))
    return _pcall(body, name="rwkv_scan_carry_fwd", grid=(nc,), in_specs=[seq, seq, mat, mat], out_specs=[tok, mat],
                  out_shape=[jax.ShapeDtypeStruct((T, cfg.RW), F32), jax.ShapeDtypeStruct((RH, nc, N, N), F32)],
                  scratch_shapes=[pltpu.VMEM((RH, N, N), F32)],
                  compiler_params=_cparams(("arbitrary",)))(q, yloc, a, sloc)


def _scan_carry_bwd(cfg, q, a, ckpt, dy):
    T, RH, N = cfg.T, cfg.RH, RWKV_HEAD_DIM
    nc, seq, mat = _scan_carry_specs(cfg, True)

    def body(q_ref, a_ref, ck_ref, dy_ref, dq_ref, da_ref, dsl_ref, ds_ref):
        @pl.when(pl.program_id(0) == 0)
        def _():
            ds_ref[...] = jnp.zeros_like(ds_ref)

        S, dS, dY = ck_ref[:, 0], ds_ref[...], _split_heads(dy_ref, RH)
        dq_ref[...] = _bdot(dY, S, 2, 1, SCAN_PASSES[2])
        da_ref[:, 0] = _bdot(S, dS, 1, 1, SCAN_PASSES[2])
        dsl_ref[:, 0] = dS
        ds_ref[...] = _bdot(dS, a_ref[:, 0], 2, 2) + _bdot(dY, q_ref[...], 1, 1, SCAN_PASSES[2])

    mt = jax.ShapeDtypeStruct((RH, nc, N, N), F32)
    tok = pl.BlockSpec((cfg.C, cfg.RW), lambda j: (nc - 1 - j, 0))
    return _pcall(body, name="rwkv_scan_carry_bwd", grid=(nc,), in_specs=[seq, mat, mat, tok],
                  out_specs=[seq, mat, mat], out_shape=[jax.ShapeDtypeStruct((RH, T, N), F32), mt, mt],
                  scratch_shapes=[pltpu.VMEM((RH, N, N), F32)],
                  compiler_params=_cparams(("arbitrary",)))(q, a, ckpt, dy)


def _post_fn(y, r, kp, v, zb, ln_w, ln_b, rk, ind, ind_t):
    n = float(RWKV_HEAD_DIM)
    mu = _xdot(_xdot(y, ind, ind_t) / n, ind_t, ind)
    yc = y - mu
    var = _xdot(yc * yc, ind, ind_t) / n
    rstd = _xdot(lax.rsqrt(var + GN_EPS), ind_t, ind)
    yn = yc * rstd * ln_w + ln_b
    bonus = _xdot(_xdot(r * kp * rk, ind, ind_t), ind_t, ind) * v
    return (yn + bonus) * _silu(zb)


def _rwkv_post_fwd(cfg, y, r, kp, v, zb, ln_w, ln_b, rk):
    T, RW, tr = cfg.T, cfg.RW, cfg.tr
    ind, ind_t, _ = _head_indicators(cfg)

    def body(y_ref, r_ref, k_ref, v_ref, z_ref, lw_ref, lb_ref, rk_ref, ind_ref, indt_ref, ob_ref):
        ob_ref[...] = _post_fn(y_ref[...], r_ref[...], k_ref[...], v_ref[...], z_ref[...], lw_ref[...], lb_ref[...],
                               rk_ref[...], ind_ref[...], indt_ref[...]).astype(BF16)

    consts = [ln_w, ln_b, rk, ind, ind_t]
    return _pcall(body, name="rwkv_post_fwd", grid=(T // tr,),
                  in_specs=[_tile(tr, RW)] * 5 + [_const(c.shape) for c in consts],
                  out_specs=_tile(tr, RW), out_shape=jax.ShapeDtypeStruct((T, RW), BF16),
                  compiler_params=_cparams(("parallel",)))(y, r, kp, v, zb, *consts)


def _rwkv_post_bwd(cfg, y, r, kp, v, zb, ln_w, ln_b, rk, dob):
    T, RW = cfg.T, cfg.RW
    tr = min(128, T)
    ind, ind_t, _ = _head_indicators(cfg)

    def body(y_ref, r_ref, k_ref, v_ref, z_ref, lw_ref, lb_ref, rk_ref, ind_ref, indt_ref, dob_ref,
             dy_ref, dr_ref, dk_ref, dv_ref, dz_ref, dlw_ref, dlb_ref, drk_ref):
        fn = functools.partial(_post_fn, ind=ind_ref[...], ind_t=indt_ref[...])
        _, vjp = jax.vjp(fn, y_ref[...], r_ref[...], k_ref[...], v_ref[...], z_ref[...], lw_ref[...], lb_ref[...],
                         rk_ref[...])
        d = vjp(dob_ref[...])
        for ref, val in zip((dy_ref, dr_ref, dk_ref, dv_ref, dz_ref), d[:5]):
            ref[...] = val
        i = pl.program_id(0)
        for ref, val in zip((dlw_ref, dlb_ref, drk_ref), d[5:8]):
            _acc_store(i, ref, val)

    consts = [ln_w, ln_b, rk, ind, ind_t]
    vec = jax.ShapeDtypeStruct((1, RW), F32)
    return _pcall(body, name="rwkv_post_bwd", grid=(T // tr,),
                  in_specs=[_tile(tr, RW)] * 5 + [_const(c.shape) for c in consts] + [_tile(tr, RW)],
                  out_specs=[_tile(tr, RW)] * 5 + [_const((1, RW))] * 3,
                  out_shape=[jax.ShapeDtypeStruct((T, RW), F32)] * 5 + [vec] * 3,
                  compiler_params=_cparams(("arbitrary",)))(y, r, kp, v, zb, *consts, dob)


def _adamw_math(w, g, m, v):
    m = ADAM_B1 * m + (1.0 - ADAM_B1) * g
    v = ADAM_B2 * v + (1.0 - ADAM_B2) * (g * g)
    m_hat = m / (1.0 - ADAM_B1 ** ADAM_STEP)
    v_hat = v / (1.0 - ADAM_B2 ** ADAM_STEP)
    delta = -ADAM_LR * (m_hat / (jnp.sqrt(v_hat) + ADAM_EPS) + ADAM_WD * w)
    return delta, m, v


def _adamw(name, w, g, m, v, copy_grad=False):
    R, Cc = w.shape
    Rp = -(-R // 8) * 8
    tr = Rp
    for nb in range(1, Rp // 8 + 1):
        if (Rp // 8) % nb == 0 and (Rp // nb) * Cc * 4 <= 2 * 1024 * 1024:
            tr = Rp // nb
            break

    def body(w_ref, g_ref, m_ref, v_ref, d_ref, nm_ref, nv_ref, *g_out):
        g_v = g_ref[...]
        d, nm, nv = _adamw_math(w_ref[...], g_v, m_ref[...], v_ref[...])
        d_ref[...] = d
        nm_ref[...] = nm
        nv_ref[...] = nv
        if copy_grad:
            g_out[0][...] = g_v

    spec = _tile(tr, Cc)
    n_out = 4 if copy_grad else 3
    return _pcall(body, name=name, grid=(Rp // tr,), in_specs=[spec] * 4, out_specs=[spec] * n_out,
                  out_shape=[jax.ShapeDtypeStruct((R, Cc), F32)] * n_out,
                  compiler_params=_cparams(("parallel",)))(w, g, m, v)


def _row_tile(R, Cc, itemsize, budget=2 * 1024 * 1024):
    for nb in range(1, R // 16 + 1):
        if R % nb == 0 and (R // nb) % 16 == 0 and (R // nb) * Cc * itemsize <= budget:
            return R // nb
    return R


def _add_halves(name, gs, r1, c_idx):
    S, R, Cc = gs.shape
    half = R // 2
    tr = _row_tile(half, Cc, 4)
    nb = half // tr

    def body(c_ref, g_ref, r_ref, o_ref):
        o_ref[...] = (g_ref[...].astype(F32) + r_ref[...].astype(F32)).astype(BF16)

    grid_spec = pltpu.PrefetchScalarGridSpec(
        num_scalar_prefetch=1, grid=(S, nb),
        in_specs=[pl.BlockSpec((1, tr, Cc), lambda s, i, c: (s, c[0] * nb + i, 0)),
                  pl.BlockSpec((1, tr, Cc), lambda s, i, c: (s, i, 0))],
        out_specs=pl.BlockSpec((1, tr, Cc), lambda s, i, c: (s, i, 0)))
    return _pcall(body, name=name, grid_spec=grid_spec, out_shape=jax.ShapeDtypeStruct((S, half, Cc), BF16),
                  compiler_params=_cparams(("parallel", "parallel")))(c_idx, gs, r1)


def _sum_slots(name, r2):
    S, R, Cc = r2.shape
    tr = _row_tile(R, Cc, 4 * S // 2 if r2.dtype == BF16 else 4 * S)

    def body(r_ref, o_ref):
        acc = r_ref[0].astype(F32)
        for s in range(1, S):
            acc = acc + r_ref[s].astype(F32)
        o_ref[...] = acc

    return _pcall(body, name=name, grid=(R // tr,), in_specs=[pl.BlockSpec((S, tr, Cc), lambda i: (0, i, 0))],
                  out_specs=_tile(tr, Cc), out_shape=jax.ShapeDtypeStruct((R, Cc), F32),
                  compiler_params=_cparams(("parallel",)))(r2)


def _sum_chips(name, recv, own, place):
    S, H, Cc = recv.shape
    tr = _row_tile(H, Cc, 4, 1024 * 1024)
    nb = H // tr

    def body(p_ref, r_ref, own_ref, o_ref):
        s = pl.program_id(1)
        me = p_ref[0]

        @pl.when(s == 0)
        def _():
            o_ref[...] = jnp.zeros_like(o_ref)

        @pl.when(s == me)
        def _():
            o_ref[...] += own_ref[0].astype(F32)

        @pl.when(s != me)
        def _():
            o_ref[...] += r_ref[0].astype(F32)

    grid_spec = pltpu.PrefetchScalarGridSpec(
        num_scalar_prefetch=1, grid=(nb, S),
        in_specs=[pl.BlockSpec((1, tr, Cc), lambda i, s, p: (jnp.where(s == p[0], (s + 1) % S, s), i, 0)),
                  pl.BlockSpec((1, tr, Cc), lambda i, s, p: (p[0], i, 0))],
        out_specs=pl.BlockSpec((tr, Cc), lambda i, s, p: (p[1] * nb + i, 0)))
    return _pcall(body, name=name, grid_spec=grid_spec, out_shape=jax.ShapeDtypeStruct((2 * H, Cc), F32),
                  compiler_params=_cparams(("parallel", "arbitrary")))(place, recv, own)


def _cast_bf16(name, w):
    R, Cc = w.shape
    tr = _row_tile(R, Cc, 4)

    def body(w_ref, o_ref):
        o_ref[...] = w_ref[...].astype(BF16)

    return _pcall(body, name=name, grid=(R // tr,), in_specs=[_tile(tr, Cc)], out_specs=_tile(tr, Cc),
                  out_shape=jax.ShapeDtypeStruct((R, Cc), BF16), compiler_params=_cparams(("parallel",)))(w)


_ANY = pl.BlockSpec(memory_space=pl.ANY)


def _place():
    x, y, c = lax.axis_index("x"), lax.axis_index("y"), lax.axis_index("c")
    others = [(1 - x, y), (x, 1 - y), (1 - x, 1 - y)]
    return x, y, c, others


def _gather_weights(shards):
    arrays, out_shapes, scratch, start, finish, middle = _gather_parts(shards)
    n = len(shards)

    def body(*refs):
        ins, outs, sems = refs[:n], refs[n:2 * n], refs[2 * n:]
        start(ins, outs, sems)
        middle(ins, outs, sems)
        finish(ins, outs, sems)

    return _pcall(body, name="gather_weights", in_specs=[_ANY] * n, out_specs=[_ANY] * n, out_shape=out_shapes,
                  scratch_shapes=scratch)(*arrays)


def _gather_parts(shards):
    n = len(shards)
    halves = [s.shape[0] // 2 for s in shards]

    def parts(ins, outs, sems):
        x, y, c, _ = _place()
        me = 2 * x + y
        n1 = (x ^ (1 - c), y ^ c)
        n2 = (x ^ c, y ^ (1 - c))
        s1, s2, sd = 2 * n1[0] + n1[1], 2 * n2[0] + n2[1], 2 * (1 - x) + (1 - y)
        sib = (x, y, 1 - c)

        def rows(k, chip, hc):
            return outs[k].at[chip, pl.ds(hc * halves[k], halves[k]), :]

        def remote(k, j, src, dst, to):
            return pltpu.make_async_remote_copy(src_ref=src, dst_ref=dst, send_sem=sems[0].at[6 * k + j],
                                                recv_sem=sems[1].at[6 * k + j], device_id=to, device_id_type=MESH)

        def copy(k, j):
            if j < 2:
                mine = ins[k].at[pl.ds(c * halves[k], halves[k]), :]
                return remote(k, j, mine, rows(k, me, c), (*(n1 if j == 0 else n2), c))
            land = rows(k, {2: s1, 3: s1, 4: s2, 5: sd}[j], c)
            return remote(k, j, land, land, (*n2, c) if j == 2 else sib)

        def arrived(k, j):
            hc = c if j < 3 else 1 - c
            land = rows(k, {0: s1, 1: s2, 2: sd, 3: s2, 4: s1, 5: sd}[j], hc)
            remote(k, j, land, land, (x, y, c)).wait_recv()

        return copy, arrived

    def start(ins, outs, sems):
        copy, _ = parts(ins, outs, sems)
        for k in range(n):
            copy(k, 0).start()
            copy(k, 1).start()

    def middle(ins, outs, sems):
        copy, arrived = parts(ins, outs, sems)
        for k in range(n):
            arrived(k, 0)
            copy(k, 2).start()
            copy(k, 3).start()
            arrived(k, 1)
            copy(k, 4).start()

    def finish(ins, outs, sems):
        copy, arrived = parts(ins, outs, sems)
        for k in range(n):
            arrived(k, 2)
            copy(k, 5).start()
        for k in range(n):
            for j in (3, 4, 5):
                arrived(k, j)
        for k in range(n):
            for j in range(6):
                copy(k, j).wait_send()

    out_shapes = [jax.ShapeDtypeStruct((N_CHIPS,) + s.shape, s.dtype) for s in shards]
    scratch = [pltpu.SemaphoreType.DMA((6 * n,)), pltpu.SemaphoreType.DMA((6 * n,))]
    return list(shards), out_shapes, scratch, start, finish, middle


def _exchange_halves(name, grads):
    n = len(grads)
    halves = [g.shape[1] // 2 for g in grads]

    def body(*refs):
        ins, outs = refs[:n], refs[n:2 * n]
        send_sems, recv_sems = refs[2 * n:]
        x, y, c, _ = _place()
        cps = []
        for k in range(n):
            src = ins[k].at[:, pl.ds((1 - c) * halves[k], halves[k]), :]
            cp = pltpu.make_async_remote_copy(src_ref=src, dst_ref=outs[k], send_sem=send_sems.at[k],
                                              recv_sem=recv_sems.at[k], device_id=(x, y, 1 - c), device_id_type=MESH)
            cp.start()
            cps.append(cp)
        for cp in cps:
            cp.wait()

    return _pcall(
        body, name=name, in_specs=[_ANY] * n, out_specs=[_ANY] * n,
        out_shape=[jax.ShapeDtypeStruct((g.shape[0], h) + g.shape[2:], g.dtype) for g, h in zip(grads, halves)],
        scratch_shapes=[pltpu.SemaphoreType.DMA((n,)), pltpu.SemaphoreType.DMA((n,))],
    )(*grads)


def _scatter_to_owners(chip_sums):
    n = len(chip_sums)

    def sends(ins, outs, sems):
        x, y, c, others = _place()
        me = 2 * x + y
        return [pltpu.make_async_remote_copy(
            src_ref=ins[k].at[2 * px + py], dst_ref=outs[k].at[me], send_sem=sems[0].at[3 * k + j],
            recv_sem=sems[1].at[3 * k + j], device_id=(px, py, c), device_id_type=MESH)
            for k in range(n) for j, (px, py) in enumerate(others)]

    def start(ins, outs, sems):
        for cp in sends(ins, outs, sems):
            cp.start()

    def finish(ins, outs, sems):
        x, y, c, others = _place()
        for k in range(n):
            for j, (px, py) in enumerate(others):
                land = outs[k].at[2 * px + py]
                pltpu.make_async_remote_copy(src_ref=land, dst_ref=land, send_sem=sems[0].at[3 * k + j],
                                             recv_sem=sems[1].at[3 * k + j], device_id=(x, y, c),
                                             device_id_type=MESH).wait_recv()
        for cp in sends(ins, outs, sems):
            cp.wait_send()

    out_shapes = [jax.ShapeDtypeStruct(g.shape, g.dtype) for g in chip_sums]
    scratch = [pltpu.SemaphoreType.DMA((3 * n,)), pltpu.SemaphoreType.DMA((3 * n,))]
    return list(chip_sums), out_shapes, scratch, start, finish


def _swap_with_sibling(arrays):
    n = len(arrays)

    def copies(ins, outs, sems):
        x, y, c, _ = _place()
        return [pltpu.make_async_remote_copy(src_ref=ins[k], dst_ref=outs[k], send_sem=sems[0].at[k],
                                             recv_sem=sems[1].at[k], device_id=(x, y, 1 - c), device_id_type=MESH)
                for k in range(n)]

    def start(ins, outs, sems):
        for cp in copies(ins, outs, sems):
            cp.start()

    def finish(ins, outs, sems):
        for cp in copies(ins, outs, sems):
            cp.wait()

    out_shapes = [jax.ShapeDtypeStruct(a.shape, a.dtype) for a in arrays]
    scratch = [pltpu.SemaphoreType.DMA((n,)), pltpu.SemaphoreType.DMA((n,))]
    return list(arrays), out_shapes, scratch, start, finish


def _add_pair(name, a, b):
    R, Cc = a.shape
    tr = _row_tile(R, Cc, 4)

    def body(a_ref, b_ref, o_ref):
        o_ref[...] = (a_ref[...].astype(F32) + b_ref[...].astype(F32)).astype(BF16)

    return _pcall(body, name=name, grid=(R // tr,), in_specs=[_tile(tr, Cc)] * 2, out_specs=_tile(tr, Cc),
                  out_shape=jax.ShapeDtypeStruct((R, Cc), BF16), compiler_params=_cparams(("parallel",)))(a, b)


def _second_neighbour():
    x, y, c, _ = _place()
    return (x, y, c), (x ^ c, y ^ (1 - c)), (x ^ (1 - c), y ^ c)


def _scatter_stage1(chip_sums):
    n = len(chip_sums)

    def copies(ins, outs, sems):
        (x, y, c), n2, n1 = _second_neighbour()
        diag = 2 * (1 - x) + (1 - y)
        return [pltpu.make_async_remote_copy(
            src_ref=ins[k].at[slot], dst_ref=outs[2 * k + j], send_sem=sems[0].at[2 * k + j],
            recv_sem=sems[1].at[2 * k + j], device_id=(*n2, c), device_id_type=MESH)
            for k in range(n) for j, slot in enumerate((2 * n2[0] + n2[1], diag))]

    def start(ins, outs, sems):
        for cp in copies(ins, outs, sems):
            cp.start()

    def finish(ins, outs, sems):
        for cp in copies(ins, outs, sems):
            cp.wait()

    out_shapes = [jax.ShapeDtypeStruct(g.shape[1:], g.dtype) for g in chip_sums for _ in range(2)]
    scratch = [pltpu.SemaphoreType.DMA((2 * n,)), pltpu.SemaphoreType.DMA((2 * n,))]
    return list(chip_sums), out_shapes, scratch, start, finish


def _scatter_stage2(passed):
    n = len(passed)

    def copies(ins, outs, sems):
        (x, y, c), n2, n1 = _second_neighbour()
        return [pltpu.make_async_remote_copy(src_ref=ins[k], dst_ref=outs[k], send_sem=sems[0].at[k],
                                             recv_sem=sems[1].at[k], device_id=(*n1, c), device_id_type=MESH)
                for k in range(n)]

    def start(ins, outs, sems):
        for cp in copies(ins, outs, sems):
            cp.start()

    def finish(ins, outs, sems):
        for cp in copies(ins, outs, sems):
            cp.wait()

    out_shapes = [jax.ShapeDtypeStruct(p.shape, p.dtype) for p in passed]
    scratch = [pltpu.SemaphoreType.DMA((n,)), pltpu.SemaphoreType.DMA((n,))]
    return list(passed), out_shapes, scratch, start, finish


def _add_passed(name, own, got, slot):
    _, H, Cc = own.shape
    tr = _row_tile(H, Cc, 4)

    def body(s_ref, o_ref, g_ref, out_ref):
        out_ref[...] = (o_ref[0].astype(F32) + g_ref[...].astype(F32)).astype(BF16)

    grid_spec = pltpu.PrefetchScalarGridSpec(
        num_scalar_prefetch=1, grid=(H // tr,),
        in_specs=[pl.BlockSpec((1, tr, Cc), lambda i, s: (s[0], i, 0)), pl.BlockSpec((tr, Cc), lambda i, s: (i, 0))],
        out_specs=pl.BlockSpec((tr, Cc), lambda i, s: (i, 0)))
    return _pcall(body, name=name, grid_spec=grid_spec, out_shape=jax.ShapeDtypeStruct((H, Cc), BF16),
                  compiler_params=_cparams(("parallel",)))(slot, own, got)


def _sum_stages(name, own, direct, via, place, transposed=False):
    _, H, Cc = own.shape
    tr = LANES if transposed else _row_tile(H, Cc, 4, 1024 * 1024)
    nb = H // tr

    def body(p_ref, own_ref, d_ref, v_ref, o_ref):
        acc = (own_ref[0].astype(F32) + d_ref[...].astype(F32)) + v_ref[...].astype(F32)
        o_ref[...] = acc.T if transposed else acc

    flat = pl.BlockSpec((tr, Cc), lambda i, p: (i, 0))
    out_spec = (pl.BlockSpec((Cc, tr), lambda i, p: (0, p[1] * nb + i)) if transposed
                else pl.BlockSpec((tr, Cc), lambda i, p: (p[1] * nb + i, 0)))
    grid_spec = pltpu.PrefetchScalarGridSpec(
        num_scalar_prefetch=1, grid=(nb,),
        in_specs=[pl.BlockSpec((1, tr, Cc), lambda i, p: (p[0], i, 0)), flat, flat], out_specs=out_spec)
    return _pcall(body, name=name, grid_spec=grid_spec,
                  out_shape=jax.ShapeDtypeStruct((Cc, 2 * H) if transposed else (2 * H, Cc), F32),
                  compiler_params=_cparams(("parallel",)))(place, own, direct, via)


def _join_halves(fulls, axes, small):
    n = len(fulls)
    hs = [f.shape[ax] // 2 for f, ax in zip(fulls, axes)]
    rel = [(dx, dy, dc) for dx in (0, 1) for dy in (0, 1) for dc in (0, 1)][1:]

    def half(ref, k, hc):
        part = pl.ds(hc * hs[k], hs[k])
        return ref.at[:, part] if axes[k] else ref.at[part, :]

    def body(*refs):
        ins, small_in = refs[:n], refs[n]
        outs, small_out = refs[n + 1:2 * n + 1], refs[2 * n + 1]
        send_sems, recv_sems, ssend, srecv, local_sem = refs[2 * n + 2:]
        x, y, c, _ = _place()
        dev = 4 * x + 2 * y + c
        local = pltpu.make_async_copy(small_in, small_out.at[dev], local_sem)
        local.start()
        cps = []
        for k in range(n):
            cp = pltpu.make_async_remote_copy(src_ref=half(ins[k], k, c), dst_ref=half(outs[k], k, c),
                                              send_sem=send_sems.at[k], recv_sem=recv_sems.at[k],
                                              device_id=(x, y, 1 - c), device_id_type=MESH)
            cp.start()
            cps.append(cp)
        for r, (dx, dy, dc) in enumerate(rel):
            cp = pltpu.make_async_remote_copy(src_ref=small_in, dst_ref=small_out.at[dev], send_sem=ssend.at[r],
                                              recv_sem=srecv.at[r], device_id=(x ^ dx, y ^ dy, c ^ dc),
                                              device_id_type=MESH)
            cp.start()
            cps.append(cp)
        for k in range(n):
            land = half(outs[k], k, 1 - c)
            pltpu.make_async_remote_copy(src_ref=land, dst_ref=land, send_sem=send_sems.at[k],
                                         recv_sem=recv_sems.at[k], device_id=(x, y, c), device_id_type=MESH).wait_recv()
        for r, (dx, dy, dc) in enumerate(rel):
            land = small_out.at[4 * (x ^ dx) + 2 * (y ^ dy) + (c ^ dc)]
            pltpu.make_async_remote_copy(src_ref=land, dst_ref=land, send_sem=ssend.at[r], recv_sem=srecv.at[r],
                                         device_id=(x, y, c), device_id_type=MESH).wait_recv()
        for cp in cps:
            cp.wait_send()
        local.wait()

    return _pcall(
        body, name="join_halves", in_specs=[_ANY] * (n + 1), out_specs=[_ANY] * (n + 1),
        out_shape=[jax.ShapeDtypeStruct(f.shape, f.dtype) for f in fulls]
        + [jax.ShapeDtypeStruct((N_DEV,) + small.shape, small.dtype)],
        input_output_aliases={k: k for k in range(n)},
        scratch_shapes=[pltpu.SemaphoreType.DMA((n,)), pltpu.SemaphoreType.DMA((n,)), pltpu.SemaphoreType.DMA((7,)),
                        pltpu.SemaphoreType.DMA((7,)), pltpu.SemaphoreType.DMA],
    )(*fulls, small)


def _local_step(cfg, x2, target, norm_gain, w_my, fb, mu_g, w0, a0, k_k, k_a, r_k, ln_w, ln_b, fng, rest,
                exchange=None):
    T, D, FW, FH, RW, RH, LP, lora = cfg.T, cfg.D, cfg.FW, cfg.FH, cfg.RW, cfg.RH, cfg.LP, cfg.lora
    fb_p = jnp.pad(fb, ((0, 0), (0, LANES - FH)))
    mu = _rwkv_vec_to_my(cfg, mu_g)
    rk = r_k.reshape(1, RW)
    tm = min(1024, T)

    h = _rms_fwd(cfg, x2, norm_gain)
    if len(rest) == 2:
        u, *got = _mm("in_proj", h, w_my, "nn", F32, tm, cfg.tn, 2048, comm=rest[0])
        rest = rest[1](got)
    else:
        u = _mm("in_proj", h, w_my, "nn", F32, tm, cfg.tn, 2048)
    w2, a2, wpf, wpr, wout = rest
    w2p = jnp.pad(w2, ((0, LP - lora), (0, 0)))
    a2p = jnp.pad(a2, ((0, LP - lora), (0, 0)))
    c_cols = _fox_prep(cfg, u, fb_p)
    c_rows = c_cols[:, :FH].T.reshape(FH, 1, T)
    o, lse = _attn_fwd(cfg, u, c_rows)
    oa = _gate_a_fwd(cfg, o, u)
    prep = _rwkv_prep_fwd(cfg, u, mu, w0, w2p, a0, a2p, k_k, k_a)
    r, lw, kp, v, an, b, zb = prep
    toks = [r, lw, kp, v, an, b]
    q_s, yloc, a_m, sloc = _scan_local_fwd(cfg, toks)
    y, ckpt = _scan_carry_fwd(cfg, q_s, yloc, a_m, sloc)
    ob = _rwkv_post_fwd(cfg, y, r, kp, v, zb, ln_w, ln_b, rk)
    pa = _mm("proj_fox", oa, wpf, "nn", F32, tm, 1024, 2048)
    pb = _mm("proj_rwkv", ob, wpr, "nn", F32, tm, 1024, 2048)
    m = _merge_fwd(cfg, pa, pb, u)
    mo = _mm("out_proj", m, wout, "nn", F32, tm, 1024, 2048)
    loss8, dres, dres16, d_fng = _final(cfg, x2, mo, fng.reshape(1, D), target)

    dm = _mm("out_proj_dx", dres16, wout, "nt", F32, tm, 1024, 2048)
    d_wout = _mm("out_proj_dw", m, dres16, "tn", BF16, 1024, 1024, 2048)
    dpa, dpb, du = _merge_bwd(cfg, pa, pb, u, dm)
    doa = _mm("proj_fox_dx", dpa, wpf, "nt", F32, tm, 1024, 2048)
    d_wpf = _mm("proj_fox_dw", oa, dpa, "tn", BF16, 1024, 1024, 2048)
    dob = _mm("proj_rwkv_dx", dpb, wpr, "nt", F32, tm, 1024, 2048)
    d_wpr = _mm("proj_rwkv_dw", ob, dpb, "tn", BF16, 1024, 1024, 2048)

    do, du = _gate_a_bwd(cfg, o, u, doa, du)
    du, dcol = _attn_bwd(cfg, u, c_rows, lse, do, du)
    dc = jnp.pad(-dcol.reshape(FH, T).T, ((0, 0), (0, LANES - FH)))
    df, d_fb = _fox_prep_bwd(cfg, u, fb_p, dc)

    dy, dr_p, dk_p, dv_p, dzb, d_lnw, d_lnb, d_rk = _rwkv_post_bwd(cfg, y, r, kp, v, zb, ln_w, ln_b, rk, dob)
    dq_s, da_m, dsl = _scan_carry_bwd(cfg, q_s, a_m, ckpt, dy)
    early = dict(w_proj_fox=d_wpf, w_proj_rwkv=d_wpr, w_out=d_wout)
    res = _scan_local_bwd(cfg, toks, dq_s, dy, da_m, dsl, [dr_p, dk_p, dv_p], exchange(early) if exchange else None)
    cots, received = res[:6], list(res[6:])
    dus, d_mu, d_w0, d_w2p, d_a0, d_a2p, d_kk, d_ka = _rwkv_prep_bwd(cfg, u, mu, w0, w2p, a0, a2p, k_k, k_a, cots, dzb)
    du = _shift_bwd(cfg, dus, mu, df, du)
    if exchange:
        late = dict(w_in=exchange((h, du, d_w2p[:lora], d_a2p[:lora])))
    else:
        late = dict(w_in=_mm("in_proj_dw", h, du, "tn", BF16, 1024, cfg.tn, 2048), rwkv_w2=d_w2p[:lora],
                    rwkv_a2=d_a2p[:lora])
    tkx = 2 * cfg.tn if cfg.ncol % (2 * cfg.tn) == 0 else cfg.tn
    res = _mm("in_proj_dx", du, w_my, "nt", F32, tm, 1024, tkx, comm=exchange(late) if exchange else None)
    dh = res[0] if exchange else res
    big = dict(early, **late)
    res = _rms_bwd(cfg, x2, norm_gain, dh, dres, exchange(list(res[1:])) if exchange else None)
    gx, d_ng = res[:2]
    received += list(res[2:])

    small = dict(norm_gain=d_ng, fox_forget_bias=d_fb[:, :FH], rwkv_shift_mix=_rwkv_vec_from_my(cfg, d_mu),
                 rwkv_w0=d_w0, rwkv_a0=d_a0, rwkv_k_k=d_kk, rwkv_k_a=d_ka, rwkv_r_k=d_rk, rwkv_ln_w=d_lnw,
                 rwkv_ln_b=d_lnb, final_norm_gain=d_fng)
    return loss8[0, 0], gx, small, big, received


_SMALL = ["norm_gain", "fox_forget_bias", "rwkv_shift_mix", "rwkv_w0", "rwkv_a0", "rwkv_k_k", "rwkv_k_a", "rwkv_r_k",
          "rwkv_ln_w", "rwkv_ln_b", "final_norm_gain"]
_WEIGHTS = ["norm_gain", "w_in", "fox_forget_bias", "rwkv_shift_mix", "rwkv_w0", "rwkv_w2", "rwkv_a0", "rwkv_a2",
            "rwkv_k_k", "rwkv_k_a", "rwkv_r_k", "rwkv_ln_w", "rwkv_ln_b", "w_proj_fox", "w_proj_rwkv", "w_out",
            "final_norm_gain"]


def _pack_small(arrs):
    parts = []
    for a in arrs:
        f = a.reshape(-1)
        parts.append(jnp.pad(f, (0, (-f.shape[0]) % LANES)))
    flat = jnp.concatenate(parts)
    rows = flat.shape[0] // LANES
    flat = jnp.pad(flat, (0, ((-rows) % 8) * LANES))
    return flat.reshape(-1, LANES)


def _unpack_small(packed, shapes):
    flat = packed.reshape(-1)
    out, pos = [], 0
    for s in shapes:
        n = int(np.prod(s))
        out.append(flat[pos:pos + n].reshape(s))
        pos += n + ((-n) % LANES)
    return out


def _shard_major(a, axis):
    parts = jnp.split(a, N_CHIPS, axis=axis)
    return jnp.stack(parts, axis=0)


def kernel(x, norm_gain, w_in, fox_forget_bias, rwkv_shift_mix, rwkv_w0, rwkv_w2, rwkv_a0, rwkv_a2, rwkv_k_k, rwkv_k_a, rwkv_r_k, rwkv_ln_w, rwkv_ln_b, w_proj_fox, w_proj_rwkv, w_out, final_norm_gain, loss_target, m_norm_gain, m_w_in, m_fox_forget_bias, m_rwkv_shift_mix, m_rwkv_w0, m_rwkv_w2, m_rwkv_a0, m_rwkv_a2, m_rwkv_k_k, m_rwkv_k_a, m_rwkv_r_k, m_rwkv_ln_w, m_rwkv_ln_b, m_w_proj_fox, m_w_proj_rwkv, m_w_out, m_final_norm_gain, v_norm_gain, v_w_in, v_fox_forget_bias, v_rwkv_shift_mix, v_rwkv_w0, v_rwkv_w2, v_rwkv_a0, v_rwkv_a2, v_rwkv_k_k, v_rwkv_k_a, v_rwkv_r_k, v_rwkv_ln_w, v_rwkv_ln_b, v_w_proj_fox, v_w_proj_rwkv, v_w_out, v_final_norm_gain):
    args = dict(locals())
    T, D = x.shape[1], x.shape[2]
    lora = rwkv_w2.shape[1]
    cfg = _Cfg(T, D, lora)
    RW = cfg.RW
    c_idx = lax.axis_index("c").astype(jnp.int32).reshape(1)
    me_chip = (2 * lax.axis_index("x") + lax.axis_index("y")).astype(jnp.int32)
    place = jnp.concatenate([me_chip.reshape(1), c_idx])

    w_in_s = w_in[0].astype(BF16)
    lora_s = jnp.concatenate([rwkv_w2[0], rwkv_a2[0]], axis=0)
    own_slot = lambda g, own: lax.dynamic_update_slice(g, own[None], (me_chip, 0, 0))
    w_my = _shards_to_my_layout(cfg, own_slot(_gather_weights([w_in_s])[0], w_in_s))
    mine = [_cast_bf16("cast_w_proj_fox", w_proj_fox[0]), _cast_bf16("cast_w_proj_rwkv", w_proj_rwkv[0]),
            _cast_bf16("cast_w_out", w_out[0]), lora_s]

    def unpack(gathered):
        g_wpf, g_wpr, g_out, g_lora = [own_slot(g, own) for g, own in zip(gathered, mine)]
        lo = g_lora.transpose(1, 0, 2).reshape(2 * lora, RW)
        return (lo[:lora], lo[lora:], g_wpf.transpose(1, 0, 2).reshape(RW, D),
                g_wpr.transpose(1, 0, 2).reshape(RW, D), g_out.reshape(D, D))

    early, late = ["w_proj_fox", "w_proj_rwkv", "w_out"], ["w_in", "lora"]
    names = early + late
    chip_sums, direct = {}, {}
    n1_slot = (2 * (lax.axis_index("x") ^ (1 - lax.axis_index("c")))
               + (lax.axis_index("y") ^ lax.axis_index("c"))).astype(jnp.int32).reshape(1)

    def exchange(got):
        if isinstance(got, tuple):
            h, du, d_w2, d_a2 = got
            c, half = lax.axis_index("c"), D // 2
            cols = lambda base: lax.dynamic_slice_in_dim(h, base * half, half, axis=1)
            lora_g = _shard_major(jnp.concatenate([d_w2, d_a2], axis=0).astype(BF16), 1)
            lora_rows = lambda base: lax.dynamic_slice_in_dim(lora_g, base * lora, lora, axis=1).reshape(-1, RW // 4)
            tiles = (BF16, min(1024, half), cfg.tn, 2048)
            sent = _mm("in_proj_dw_sibling", cols(1 - c), du, "tn", *tiles)
            kept, got_w, got_l = _mm("in_proj_dw", cols(c), du, "tn", *tiles,
                                     comm=_swap_with_sibling([sent, lora_rows(1 - c)]))
            return (_add_pair("add_halves_w_in", kept, got_w),
                    _add_pair("add_halves_lora", lora_rows(c), got_l).reshape(N_CHIPS, lora, RW // 4))
        if isinstance(got, dict):
            if "w_in" in got:
                sums = [_my_layout_to_shards(cfg, got["w_in"][0]), got["w_in"][1]]
                chip_sums.update(zip(late, sums))
                return _scatter_stage1(sums)
            gs = [_shard_major(got["w_proj_fox"], 1), _shard_major(got["w_proj_rwkv"], 1),
                  _shard_major(got["w_out"], 0)]
            recv1 = _exchange_halves("exchange_halves_" + early[0], gs)
            sums = [_add_halves("add_halves_" + nm, g, r, c_idx) for nm, g, r in zip(early, gs, recv1)]
            chip_sums.update(zip(early, sums))
            return _scatter_to_owners(sums)
        direct.update(zip(late, got[0::2]))
        return _scatter_stage2([_add_passed("add_passed_" + nm, chip_sums[nm], g, n1_slot)
                                for nm, g in zip(late, got[1::2])])

    loss_dev, gx, small, _, recv2 = _local_step(
        cfg, x[0], loss_target[0], norm_gain, w_my, fox_forget_bias, rwkv_shift_mix, rwkv_w0, rwkv_a0, rwkv_k_k,
        rwkv_k_a, rwkv_r_k, rwkv_ln_w, rwkv_ln_b, final_norm_gain, (_gather_parts(mine), unpack), exchange)
    loss = lax.psum(loss_dev, ("x", "y", "c"))

    small_shapes = [args[nm].shape for nm in _SMALL]
    packed = _pack_small([small[nm] for nm in _SMALL])
    reduced = [_sum_chips("sum_chips_" + nm, r, chip_sums[nm], place) for nm, r in zip(early, recv2[:3])]
    reduced += [_sum_stages("sum_stages_" + nm, chip_sums[nm], direct[nm], via, place, transposed=nm == "w_in")
                for nm, via in zip(late, recv2[3:])]
    *joined, small_all = _join_halves(reduced, [int(nm == "w_in") for nm in names], packed)
    g_small = _sum_slots("sum_small", small_all)

    grads = dict(zip(_SMALL, _unpack_small(g_small, small_shapes)))
    grads.update({nm: g[None] for nm, g in zip(names, joined) if nm not in ("lora", "w_in")})
    g_lora_f = joined[names.index("lora")]
    grads["rwkv_w2"] = g_lora_f[None, :lora]
    grads["rwkv_a2"] = g_lora_f[None, lora:]

    delta, new_m, new_v = {}, {}, {}
    w_small = _pack_small([args[nm] for nm in _SMALL])
    m_small = _pack_small([args["m_" + nm] for nm in _SMALL])
    v_small = _pack_small([args["v_" + nm] for nm in _SMALL])
    d_s, m_s, v_s = _adamw("adamw_small", w_small, g_small, m_small, v_small)
    for tgt, pk in ((delta, d_s), (new_m, m_s), (new_v, v_s)):
        tgt.update(zip(_SMALL, _unpack_small(pk, small_shapes)))
    t_out = _adamw("adamw_w_in", w_in[0].T, joined[names.index("w_in")], m_w_in[0].T, v_w_in[0].T, copy_grad=True)
    delta["w_in"], new_m["w_in"], new_v["w_in"], grads["w_in"] = [t.T[None] for t in t_out]
    for nm in ("w_proj_fox", "w_proj_rwkv", "w_out", "rwkv_w2", "rwkv_a2"):
        shp = args[nm].shape
        two_d = (shp[1], shp[2])
        d_b, m_b, v_b = _adamw("adamw_" + nm, args[nm].reshape(two_d), grads[nm].reshape(two_d),
                               args["m_" + nm].reshape(two_d), args["v_" + nm].reshape(two_d))
        delta[nm], new_m[nm], new_v[nm] = d_b.reshape(shp), m_b.reshape(shp), v_b.reshape(shp)

    return (loss, gx[None], *[grads[n] for n in _WEIGHTS], *[delta[n] for n in _WEIGHTS],
            *[new_m[n] for n in _WEIGHTS], *[new_v[n] for n in _WEIGHTS])
```

```python
import functools

import numpy as np
import jax
import jax.numpy as jnp
from jax import lax
from jax.experimental import pallas as pl
from jax.experimental.pallas import tpu as pltpu

F32 = jnp.float32
BF16 = jnp.bfloat16
HI = lax.Precision.HIGHEST
MESH = pl.DeviceIdType.MESH

FOX_HEAD_DIM = 128
RWKV_HEAD_DIM = 64
RMS_EPS = 1e-6
GN_EPS = 64e-5
L2_EPS = 1e-12
ADAM_LR = 0.001
ADAM_B1 = 0.9
ADAM_B2 = 0.999
ADAM_EPS = 1e-08
ADAM_WD = 0.01
ADAM_STEP = 10

LANES = 128
VMEM_LIMIT = 56 * 1024 * 1024
SCAN_CHUNK = 64
SCAN_HEADS_PER_STEP = 16
SCAN_PASSES = (3, 1, 1)
N_CHIPS = 4
N_DEV = 8

_pcall = pl.pallas_call


def _cparams(sem=None):
    return pltpu.CompilerParams(dimension_semantics=sem, vmem_limit_bytes=VMEM_LIMIT)


def _softplus(x):
    return jnp.maximum(x, 0.0) + jnp.log(1.0 + jnp.exp(-jnp.abs(x)))


def _silu(z):
    return z * jax.nn.sigmoid(z)


def _rmsn(x, g):
    return x * lax.rsqrt(jnp.mean(x * x, axis=-1, keepdims=True) + RMS_EPS) * g


def _dot(a, b, dims="nn", precision=None):
    dn = {"nn": (((1,), (0,)), ((), ())), "nt": (((1,), (1,)), ((), ())), "tn": (((0,), (0,)), ((), ()))}[dims]
    return lax.dot_general(a, b, dn, precision=precision, preferred_element_type=F32)


def _split_bf16(x):
    hi = x.astype(BF16)
    return hi, (x - hi.astype(F32)).astype(BF16)


def _bdot_raw(a, b, ca, cb, passes):
    dn = (((ca,), (cb,)), ((0,), (0,)))
    mm = lambda p, q: lax.dot_general(p, q, dn, preferred_element_type=F32)
    if passes == 1:
        return mm(a.astype(BF16), b.astype(BF16))
    ah, al = _split_bf16(a)
    bh, bl = _split_bf16(b)
    return mm(ah, bh) + (mm(ah, bl) + mm(al, bh))


@functools.partial(jax.custom_vjp, nondiff_argnums=(2, 3, 4))
def _bdot_p(a, b, ca, cb, passes):
    return _bdot_raw(a, b, ca, cb, passes)


def _bdot_fwd(a, b, ca, cb, passes):
    return _bdot_raw(a, b, ca, cb, passes), (a, b)


def _bdot_bwd(ca, cb, passes, res, g):
    a, b = res
    if (ca, cb) == (2, 1):
        return _bdot_p(g, b, 2, 2, passes), _bdot_p(a, g, 1, 1, passes)
    if (ca, cb) == (2, 2):
        return _bdot_p(g, b, 2, 1, passes), _bdot_p(g, a, 1, 1, passes)
    assert (ca, cb) == (1, 1)
    return _bdot_p(b, g, 2, 2, passes), _bdot_p(a, g, 2, 1, passes)


_bdot_p.defvjp(_bdot_fwd, _bdot_bwd)


def _bdot(a, b, ca, cb, passes=3):
    return _bdot_p(a, b, ca, cb, passes)


def _dot3(a, b):
    return _bdot(a[None], b[None], 2, 1)[0]


@jax.custom_vjp
def _xdot(x, m, mt):
    hi, lo = _split_bf16(x)
    m16 = m.astype(BF16)
    return _dot(hi, m16) + _dot(lo, m16)


def _xdot_fwd(x, m, mt):
    return _xdot(x, m, mt), (m, mt)


def _xdot_bwd(res, g):
    m, mt = res
    return _xdot(g, mt, m), jnp.zeros_like(m), jnp.zeros_like(mt)


_xdot.defvjp(_xdot_fwd, _xdot_bwd)


class _Cfg:
    def __init__(self, T, D, lora):
        self.T, self.D, self.lora = T, D, lora
        self.FW = D // 2
        self.FH = self.FW // FOX_HEAD_DIM
        self.RW = D // 2
        self.RH = self.RW // RWKV_HEAD_DIM
        self.LP = -(-lora // LANES) * LANES
        self.o_fox = 0
        self.o_rwkv = 4 * self.FW
        self.o_gate = self.o_rwkv + 4 * self.RW
        self.o_f = self.o_gate + 2 * D
        self.o_wd = self.o_f + LANES
        self.o_ad = self.o_wd + self.LP
        end = self.o_ad + self.LP
        self.tn = 1280 if D >= 2048 else LANES
        self.ncol = -(-end // self.tn) * self.tn
        self.in_cols = 4 * self.FW + self.FH + 4 * self.RW + 2 * lora + 2 * D
        self.scp = -(-(self.in_cols // N_CHIPS) // LANES) * LANES
        self.rseg = 4 * self.RW + 2 * self.LP
        self.C = min(SCAN_CHUNK, T)
        self.tr = min(256, T)
        self.hb = min(SCAN_HEADS_PER_STEP, self.RH)

    def segments(self):
        FW, FH, RW, lo, D = self.FW, self.FH, self.RW, self.lora, self.D
        g_f = 4 * FW
        g_r = g_f + FH
        g_wd = g_r + 4 * RW
        g_ad = g_wd + lo
        g_g = g_ad + lo
        dh = FOX_HEAD_DIM
        qkv = [(j * FW + h * dh, dh, (3 * h + j) * dh) for h in range(FH) for j in range(3)]
        return qkv + [(3 * FW, FW, 3 * FW), (g_f, FH, self.o_f), (g_r, 4 * RW, self.o_rwkv), (g_wd, lo, self.o_wd),
                      (g_ad, lo, self.o_ad), (g_g, 2 * D, self.o_gate)]


def _shards_to_my_layout(cfg, g):
    R, sc = g.shape[1], g.shape[2]
    segs = sorted(cfg.segments(), key=lambda s: s[2])
    parts, pos = [], 0
    for g0, w, m0 in segs:
        if m0 > pos:
            parts.append(jnp.zeros((R, m0 - pos), g.dtype))
        for s in range(N_CHIPS):
            lo, hi = max(g0, s * sc), min(g0 + w, (s + 1) * sc)
            if lo < hi:
                parts.append(g[s, :, lo - s * sc:hi - s * sc])
        pos = m0 + w
    if cfg.ncol > pos:
        parts.append(jnp.zeros((R, cfg.ncol - pos), g.dtype))
    return jnp.concatenate(parts, axis=1)


def _my_layout_to_shards(cfg, wm):
    sc, R = cfg.in_cols // N_CHIPS, wm.shape[0]
    segs = sorted(cfg.segments(), key=lambda s: s[0])
    shards = []
    for s in range(N_CHIPS):
        parts = []
        for g0, w, m0 in segs:
            lo, hi = max(g0, s * sc), min(g0 + w, (s + 1) * sc)
            if lo < hi:
                parts.append(wm[:, m0 + lo - g0:m0 + hi - g0])
        parts.append(jnp.zeros((R, cfg.scp - sc), wm.dtype))
        shards.append(jnp.concatenate(parts, axis=1))
    return jnp.stack(shards, axis=0)


def _rwkv_vec_to_my(cfg, v):
    RW4, lo, LP = 4 * cfg.RW, cfg.lora, cfg.LP
    z = jnp.zeros((1, LP - lo), v.dtype)
    return jnp.concatenate([v[:, :RW4], v[:, RW4:RW4 + lo], z, v[:, RW4 + lo:], z], axis=1)


def _rwkv_vec_from_my(cfg, v):
    RW4, lo, LP = 4 * cfg.RW, cfg.lora, cfg.LP
    return jnp.concatenate([v[:, :RW4], v[:, RW4:RW4 + lo], v[:, RW4 + LP:RW4 + LP + lo]], axis=1)


def _comm_at(comm, which, steps, cin, cout, scr):
    if not comm or len(comm) <= which:
        return
    lin, total = 0, 1
    for d, n in enumerate(steps):
        lin = lin * n + pl.program_id(d)
        total *= n
    pl.when(lin == {3: 0, 4: total - 1, 5: total // 2}[which])(lambda: comm[which](cin, cout, scr))


def _hosted(body, n_in, n_out, steps, comm):
    if not comm:
        return body, [], [], [], []
    ci, co, cs = len(comm[0]), len(comm[1]), len(comm[2])

    def wrapped(*refs):
        ins, cin = refs[:n_in], refs[n_in:n_in + ci]
        outs, cout = refs[n_in + ci:n_in + ci + n_out], refs[n_in + ci + n_out:n_in + ci + n_out + co]
        cscr, scr = refs[n_in + ci + n_out + co:n_in + ci + n_out + co + cs], refs[n_in + ci + n_out + co + cs:]
        _comm_at(comm, 3, steps, cin, cout, cscr)
        body(*ins, *outs, *scr)
        _comm_at(comm, 5, steps, cin, cout, cscr)
        _comm_at(comm, 4, steps, cin, cout, cscr)

    return wrapped, [_ANY] * ci, [_ANY] * co, list(comm[1]), list(comm[2])


def _mm(name, a, b, dims, out_dtype, tm, tn, tk, comm=None):
    (M, K) = a.shape if dims != "tn" else a.shape[::-1]
    N = b.shape[0] if dims == "nt" else b.shape[1]
    tm, tn, tk = min(tm, M), min(tn, N), min(tk, K)
    assert M % tm == 0 and N % tn == 0 and K % tk == 0, (name, M, N, K, tm, tn, tk)
    nk = K // tk
    steps = (M // tm, N // tn, nk)
    c_in, c_out, c_scr = comm[:3] if comm else ([], [], [])
    if dims == "nn":
        a_spec = pl.BlockSpec((tm, tk), lambda i, j, k: (i, k))
        b_spec = pl.BlockSpec((tk, tn), lambda i, j, k: (k, j))
    elif dims == "nt":
        a_spec = pl.BlockSpec((tm, tk), lambda i, j, k: (i, k))
        b_spec = pl.BlockSpec((tn, tk), lambda i, j, k: (j, k))
    else:
        a_spec = pl.BlockSpec((tk, tm), lambda i, j, k: (k, i))
        b_spec = pl.BlockSpec((tk, tn), lambda i, j, k: (k, j))

    n_acc = 1 if nk > 1 else 0

    def body(a_ref, b_ref, *rest):
        cin, o_ref = rest[:len(c_in)], rest[len(c_in)]
        cout = rest[len(c_in) + 1:len(c_in) + 1 + len(c_out)]
        scr = rest[len(c_in) + 1 + len(c_out):]
        _comm_at(comm, 3, steps, cin, cout, scr[n_acc:])
        if nk == 1:
            o_ref[...] = _dot(a_ref[...], b_ref[...], dims).astype(o_ref.dtype)
        else:
            acc_ref, k = scr[0], pl.program_id(2)

            @pl.when(k == 0)
            def _():
                acc_ref[...] = jnp.zeros_like(acc_ref)

            acc_ref[...] += _dot(a_ref[...], b_ref[...], dims)

            @pl.when(k == nk - 1)
            def _():
                o_ref[...] = acc_ref[...].astype(o_ref.dtype)

        _comm_at(comm, 5, steps, cin, cout, scr[n_acc:])
        _comm_at(comm, 4, steps, cin, cout, scr[n_acc:])

    res = _pcall(
        body, name=name, grid=steps,
        in_specs=[a_spec, b_spec] + [_ANY] * len(c_in),
        out_specs=[pl.BlockSpec((tm, tn), lambda i, j, k: (i, j))] + [_ANY] * len(c_out),
        out_shape=[jax.ShapeDtypeStruct((M, N), out_dtype)] + list(c_out),
        scratch_shapes=([pltpu.VMEM((tm, tn), F32)] if nk > 1 else []) + list(c_scr),
        compiler_params=_cparams(("arbitrary",) * 3 if comm else ("parallel", "parallel", "arbitrary")),
    )(a, b, *c_in)
    return res if comm else res[0]


def _tile(tr, w, cb=0):
    return pl.BlockSpec((tr, w), lambda i: (i, cb))


def _const(shape):
    nd = len(shape)
    return pl.BlockSpec(shape, lambda i: (0,) * nd)


def _acc_store(i, ref, val):
    @pl.when(i == 0)
    def _():
        ref[...] = val

    @pl.when(i > 0)
    def _():
        ref[...] += val


def _rms_fwd(cfg, x2, g, comm=None):
    T, D, tr = cfg.T, cfg.D, cfg.tr
    steps = (T // tr,)

    def body(x_ref, g_ref, h_ref):
        h_ref[...] = _rmsn(x_ref[...], g_ref[...]).astype(BF16)

    body, c_in, c_out, c_shapes, c_scr = _hosted(body, 2, 1, steps, comm)
    res = _pcall(body, name="rms_fwd", grid=steps, in_specs=[_tile(tr, D), _const((1, D))] + c_in,
                 out_specs=[_tile(tr, D)] + c_out, out_shape=[jax.ShapeDtypeStruct((T, D), BF16)] + c_shapes,
                 scratch_shapes=c_scr, compiler_params=_cparams(("arbitrary",) if comm else ("parallel",)),
                 )(x2, g, *(comm[0] if comm else []))
    return res if comm else res[0]


def _rms_bwd(cfg, x2, g, dh, dres, comm=None):
    T, D, tr = cfg.T, cfg.D, cfg.tr
    c_in, c_out, c_scr = comm[:3] if comm else ([], [], [])
    steps = (T // tr,)

    def body(x_ref, g_ref, dh_ref, dres_ref, *rest):
        cin, (gx_ref, dg_ref) = rest[:len(c_in)], rest[len(c_in):len(c_in) + 2]
        cout, scr = rest[len(c_in) + 2:len(c_in) + 2 + len(c_out)], rest[len(c_in) + 2 + len(c_out):]
        _comm_at(comm, 3, steps, cin, cout, scr)
        _, vjp = jax.vjp(_rmsn, x_ref[...], g_ref[...])
        dx, dg = vjp(dh_ref[...])
        gx_ref[...] = dx + dres_ref[...]
        _acc_store(pl.program_id(0), dg_ref, dg)
        _comm_at(comm, 4, steps, cin, cout, scr)

    return _pcall(body, name="rms_bwd", grid=steps,
                  in_specs=[_tile(tr, D), _const((1, D)), _tile(tr, D), _tile(tr, D)] + [_ANY] * len(c_in),
                  out_specs=[_tile(tr, D), _const((1, D))] + [_ANY] * len(c_out),
                  out_shape=[jax.ShapeDtypeStruct((T, D), F32), jax.ShapeDtypeStruct((1, D), F32)] + list(c_out),
                  scratch_shapes=list(c_scr), compiler_params=_cparams(("arbitrary",)))(x2, g, dh, dres, *c_in)


def _final(cfg, x2, mo, fg, target):
    T, D, tr = cfg.T, cfg.D, cfg.tr

    def loss_fn(hres, g, tgt):
        err = _rmsn(hres, g) - tgt
        return 0.5 * jnp.sum(jnp.mean(err * err, axis=-1, keepdims=True), axis=0, keepdims=True)

    def body(x_ref, mo_ref, g_ref, t_ref, loss_ref, dres_ref, dres16_ref, dg_ref):
        hres = x_ref[...] + mo_ref[...]
        loss, vjp = jax.vjp(functools.partial(loss_fn, tgt=t_ref[...]), hres, g_ref[...])
        dres, dg = vjp(jnp.ones((1, 1), F32))
        dres_ref[...] = dres
        dres16_ref[...] = dres.astype(BF16)
        i = pl.program_id(0)
        _acc_store(i, dg_ref, dg)
        _acc_store(i, loss_ref, jnp.broadcast_to(loss, (8, LANES)))

    return _pcall(body, name="final_loss", grid=(T // tr,),
                  in_specs=[_tile(tr, D), _tile(tr, D), _const((1, D)), _tile(tr, D)],
                  out_specs=[_const((8, LANES)), _tile(tr, D), _tile(tr, D), _const((1, D))],
                  out_shape=[jax.ShapeDtypeStruct((8, LANES), F32), jax.ShapeDtypeStruct((T, D), F32),
                             jax.ShapeDtypeStruct((T, D), BF16), jax.ShapeDtypeStruct((1, D), F32)],
                  compiler_params=_cparams(("arbitrary",)))(x2, mo, fg, target)


def _merge_fn(pa, pb, ga, gb):
    return jax.nn.sigmoid(ga) * pa + jax.nn.sigmoid(gb) * pb


def _merge_fwd(cfg, pa, pb, u):
    T, D, tr = cfg.T, cfg.D, cfg.tr
    cga, cgb = cfg.o_gate // D, cfg.o_gate // D + 1

    def body(pa_ref, pb_ref, ga_ref, gb_ref, m_ref):
        m_ref[...] = _merge_fn(pa_ref[...], pb_ref[...], ga_ref[...], gb_ref[...]).astype(BF16)

    return _pcall(body, name="merge_fwd", grid=(T // tr,),
                  in_specs=[_tile(tr, D), _tile(tr, D), _tile(tr, D, cga), _tile(tr, D, cgb)],
                  out_specs=_tile(tr, D), out_shape=jax.ShapeDtypeStruct((T, D), BF16),
                  compiler_params=_cparams(("parallel",)))(pa, pb, u, u)


def _merge_bwd(cfg, pa, pb, u, dm):
    T, D, tr = cfg.T, cfg.D, cfg.tr
    cga, cgb = cfg.o_gate // D, cfg.o_gate // D + 1

    def body(pa_ref, pb_ref, ga_ref, gb_ref, dm_ref, dpa_ref, dpb_ref, dg_ref):
        _, vjp = jax.vjp(_merge_fn, pa_ref[...], pb_ref[...], ga_ref[...], gb_ref[...])
        dpa, dpb, dga, dgb = vjp(dm_ref[...])
        dpa_ref[...] = dpa.astype(BF16)
        dpb_ref[...] = dpb.astype(BF16)
        dg_ref[:, :D] = dga.astype(BF16)
        dg_ref[:, D:] = dgb.astype(BF16)

    return _pcall(body, name="merge_bwd", grid=(T // tr,),
                  in_specs=[_tile(tr, D), _tile(tr, D), _tile(tr, D, cga), _tile(tr, D, cgb), _tile(tr, D)],
                  out_specs=[_tile(tr, D), _tile(tr, D), _tile(tr, 2 * D, cfg.o_gate // (2 * D))],
                  out_shape=[jax.ShapeDtypeStruct((T, D), BF16), jax.ShapeDtypeStruct((T, D), BF16),
                             jax.ShapeDtypeStruct((T, cfg.ncol), BF16)],
                  compiler_params=_cparams(("parallel",)))(pa, pb, u, u, dm)


def _gate_fn(o, z):
    return o * _silu(z)


def _gate_a_fwd(cfg, o, u):
    T, FW, tr = cfg.T, cfg.FW, cfg.tr

    def body(o_ref, z_ref, oa_ref):
        oa_ref[...] = _gate_fn(o_ref[...], z_ref[...]).astype(BF16)

    return _pcall(body, name="gate_a_fwd", grid=(T // tr,), in_specs=[_tile(tr, FW), _tile(tr, FW, 3)],
                  out_specs=_tile(tr, FW), out_shape=jax.ShapeDtypeStruct((T, FW), BF16),
                  compiler_params=_cparams(("parallel",)))(o, u)


def _gate_a_bwd(cfg, o, u, doa, du):
    T, FW, tr = cfg.T, cfg.FW, cfg.tr

    def body(o_ref, z_ref, doa_ref, du_in, do_ref, dz_ref):
        _, vjp = jax.vjp(_gate_fn, o_ref[...], z_ref[...])
        do, dz = vjp(doa_ref[...])
        do_ref[...] = do
        dz_ref[...] = dz.astype(BF16)

    return _pcall(body, name="gate_a_bwd", grid=(T // tr,),
                  in_specs=[_tile(tr, FW), _tile(tr, FW, 3), _tile(tr, FW), _ANY],
                  out_specs=[_tile(tr, FW), _tile(tr, FW, 3)],
                  out_shape=[jax.ShapeDtypeStruct((T, FW), F32), jax.ShapeDtypeStruct(du.shape, BF16)],
                  input_output_aliases={3: 1},
                  compiler_params=_cparams(("parallel",)))(o, u, doa, du)


def _fox_prep(cfg, u, fb):
    T, tr = cfg.T, cfg.tr
    cf = cfg.o_f // LANES

    def body(f_ref, fb_ref, c_ref, carry_ref):
        i = pl.program_id(0)

        @pl.when(i == 0)
        def _():
            carry_ref[...] = jnp.zeros_like(carry_ref)

        lf = -_softplus(-(f_ref[...] + fb_ref[...]))
        r = lax.broadcasted_iota(jnp.int32, (tr, tr), 0)
        c = lax.broadcasted_iota(jnp.int32, (tr, tr), 1)
        tri = (r >= c).astype(F32)
        c_ref[...] = _dot(tri, lf, precision=HI) + carry_ref[...]
        carry_ref[...] += jnp.sum(lf, axis=0, keepdims=True)

    return _pcall(body, name="fox_prep", grid=(T // tr,), in_specs=[_tile(tr, LANES, cf), _const((1, LANES))],
                  out_specs=_tile(tr, LANES), out_shape=jax.ShapeDtypeStruct((T, LANES), F32),
                  scratch_shapes=[pltpu.VMEM((1, LANES), F32)], compiler_params=_cparams(("arbitrary",)))(u, fb)


def _fox_prep_bwd(cfg, u, fb, dc):
    T, tr = cfg.T, cfg.tr
    cf = cfg.o_f // LANES
    nb = T // tr

    def body(f_ref, fb_ref, dc_ref, df_ref, dfb_ref, carry_ref):
        i = pl.program_id(0)

        @pl.when(i == 0)
        def _():
            carry_ref[...] = jnp.zeros_like(carry_ref)

        dc = dc_ref[...]
        r = lax.broadcasted_iota(jnp.int32, (tr, tr), 0)
        c = lax.broadcasted_iota(jnp.int32, (tr, tr), 1)
        triu = (r <= c).astype(F32)
        dlf = _dot(triu, dc, precision=HI) + carry_ref[...]
        carry_ref[...] += jnp.sum(dc, axis=0, keepdims=True)
        dz = dlf * jax.nn.sigmoid(-(f_ref[...] + fb_ref[...]))
        df_ref[...] = dz.astype(BF16)
        _acc_store(i, dfb_ref, jnp.sum(dz, axis=0, keepdims=True))

    rev = lambda i: (nb - 1 - i, 0)
    return _pcall(body, name="fox_prep_bwd", grid=(nb,),
                  in_specs=[pl.BlockSpec((tr, LANES), lambda i: (nb - 1 - i, cf)), _const((1, LANES)),
                            pl.BlockSpec((tr, LANES), rev)],
                  out_specs=[pl.BlockSpec((tr, LANES), rev), _const((1, LANES))],
                  out_shape=[jax.ShapeDtypeStruct((T, LANES), BF16), jax.ShapeDtypeStruct((1, LANES), F32)],
                  scratch_shapes=[pltpu.VMEM((1, LANES), F32)], compiler_params=_cparams(("arbitrary",)))(u, fb, dc)


def _attn_logits(q_ref, k_ref, c_ref, tq, te):
    q = q_ref[...].astype(BF16)
    scale = FOX_HEAD_DIM ** -0.5
    part = lambda k0, k1: _dot(q, k_ref[k0:k1, :].astype(BF16), "nt") * scale - c_ref[0, :, k0:k1]
    row = lax.broadcasted_iota(jnp.int32, (tq, tq), 0)
    col = lax.broadcasted_iota(jnp.int32, (tq, tq), 1)
    own = ((te - tq, te), jnp.where(col <= row, part(te - tq, te), -1e30))
    return [((0, te - tq), part(0, te - tq)), own] if te > tq else [own]


def _per_query_tile(i, nq, tq, fn):
    for ii in range(nq):
        pl.when(i == ii)(functools.partial(fn, (ii + 1) * tq))


def _attn_fwd(cfg, u, c_rows):
    T, FW, FH = cfg.T, cfg.FW, cfg.FH
    tq = min(256, T)
    dh = FOX_HEAD_DIM

    def body(q_ref, k_ref, v_ref, c_ref, o_ref, lse_ref):
        i = pl.program_id(1)

        def tile(te):
            parts = _attn_logits(q_ref, k_ref, c_ref, tq, te)
            m = functools.reduce(jnp.maximum, [jnp.max(s, axis=1, keepdims=True) for _, s in parts])
            l, acc = 0.0, 0.0
            for (k0, k1), s in parts:
                p = jnp.exp(s - m)
                l = l + jnp.sum(p, axis=1, keepdims=True)
                acc = acc + _dot(p.astype(BF16), v_ref[k0:k1, :].astype(BF16))
            o_ref[...] = acc / l
            lse_ref[0] = m + jnp.log(l)

        _per_query_tile(i, T // tq, tq, tile)

    return _pcall(
        body, name="fox_attn_fwd", grid=(FH, T // tq),
        in_specs=[pl.BlockSpec((tq, dh), lambda h, i: (i, 3 * h)), pl.BlockSpec((T, dh), lambda h, i: (0, 3 * h + 1)),
                  pl.BlockSpec((T, dh), lambda h, i: (0, 3 * h + 2)), pl.BlockSpec((1, 1, T), lambda h, i: (h, 0, 0))],
        out_specs=[pl.BlockSpec((tq, dh), lambda h, i: (i, h)), pl.BlockSpec((1, tq, 1), lambda h, i: (h, i, 0))],
        out_shape=[jax.ShapeDtypeStruct((T, FW), F32), jax.ShapeDtypeStruct((FH, T, 1), F32)],
        compiler_params=_cparams(("parallel", "arbitrary")),
    )(u, u, u, c_rows)


def _attn_bwd(cfg, u, c_rows, lse, do, du):
    T, FW, FH = cfg.T, cfg.FW, cfg.FH
    tq = min(256, T)
    nq = T // tq
    dh = FOX_HEAD_DIM
    scale = dh ** -0.5

    def body(q_ref, k_ref, v_ref, c_ref, lse_ref, do_ref, du_in, du_ref, dcol_ref, dk_acc, dv_acc):
        i = pl.program_id(1)

        @pl.when(i == 0)
        def _():
            dk_acc[...] = jnp.zeros_like(dk_acc)
            dv_acc[...] = jnp.zeros_like(dv_acc)
            dcol_ref[...] = jnp.zeros_like(dcol_ref)

        def tile(te):
            lse, q16, do16 = lse_ref[0], q_ref[...].astype(BF16), do_ref[...].astype(BF16)
            probs = [(ks, jnp.exp(s - lse)) for ks, s in _attn_logits(q_ref, k_ref, c_ref, tq, te)]
            dps = [_dot(do16, v_ref[k0:k1, :].astype(BF16), "nt") for (k0, k1), _ in probs]
            delta = sum(jnp.sum(p * dp, axis=1, keepdims=True) for (_, p), dp in zip(probs, dps))
            dq = 0.0
            for ((k0, k1), p), dp in zip(probs, dps):
                ds = p * (dp - delta)
                ds16 = ds.astype(BF16)
                dq = dq + _dot(ds16, k_ref[k0:k1, :].astype(BF16))
                dk_acc[k0:k1, :] += _dot(ds16, q16, "tn") * scale
                dv_acc[k0:k1, :] += _dot(p.astype(BF16), do16, "tn")
                dcol_ref[0, :, k0:k1] += jnp.sum(ds, axis=0, keepdims=True)
            du_ref[te - tq:te, 0:dh] = (dq * scale).astype(BF16)

        _per_query_tile(i, nq, tq, tile)

        @pl.when(i == nq - 1)
        def _():
            du_ref[:, dh:2 * dh] = dk_acc[...].astype(BF16)
            du_ref[:, 2 * dh:3 * dh] = dv_acc[...].astype(BF16)

    return _pcall(
        body, name="fox_attn_bwd", grid=(FH, nq),
        in_specs=[pl.BlockSpec((tq, dh), lambda h, i: (i, 3 * h)), pl.BlockSpec((T, dh), lambda h, i: (0, 3 * h + 1)),
                  pl.BlockSpec((T, dh), lambda h, i: (0, 3 * h + 2)), pl.BlockSpec((1, 1, T), lambda h, i: (h, 0, 0)),
                  pl.BlockSpec((1, tq, 1), lambda h, i: (h, i, 0)), pl.BlockSpec((tq, dh), lambda h, i: (i, h)), _ANY],
        out_specs=[pl.BlockSpec((T, 3 * dh), lambda h, i: (0, h)), pl.BlockSpec((1, 1, T), lambda h, i: (h, 0, 0))],
        out_shape=[jax.ShapeDtypeStruct(du.shape, BF16), jax.ShapeDtypeStruct((FH, 1, T), F32)],
        scratch_shapes=[pltpu.VMEM((T, dh), F32), pltpu.VMEM((T, dh), F32)],
        input_output_aliases={6: 0},
        compiler_params=_cparams(("parallel", "arbitrary")),
    )(u, u, u, c_rows, lse, do, du)


def _head_indicators(cfg):
    ind = np.zeros((cfg.RW, LANES), np.float32)
    ind[np.arange(cfg.RW), np.arange(cfg.RW) // RWKV_HEAD_DIM] = 1.0
    pad = np.zeros((1, LANES), np.float32)
    pad[0, cfg.RH:] = 1.0
    return jnp.asarray(ind), jnp.asarray(ind.T.copy()), jnp.asarray(pad)


def _prep_fn(us_r, us_k, us_v, us_wd, us_ad, w0, w2p, a0, a2p, k_k, k_a, ind, ind_t, pad):
    wpre = w0 + _dot3(jnp.tanh(us_wd), w2p)
    w = -_softplus(-wpre) - 0.5
    lw = -jnp.exp(w)
    a = jax.nn.sigmoid(a0 + _dot3(us_ad, a2p))
    kk = us_k * k_k
    ss = _xdot(kk * kk, ind, ind_t) + pad
    inv = 1.0 / jnp.maximum(jnp.sqrt(ss), L2_EPS)
    kkn = kk * _xdot(inv, ind_t, ind)
    kp = us_k * (1.0 + (a - 1.0) * k_a)
    return us_r, lw, kp, us_v, -kkn, kkn * a


def _shifted(u, prev_row, mu, first):
    n = u.shape[0]
    rolled = pltpu.roll(u, 1, 0)
    row = lax.broadcasted_iota(jnp.int32, u.shape, 0)
    p0 = jnp.where(first, jnp.zeros_like(prev_row), prev_row)
    prev = jnp.where(row == 0, jnp.broadcast_to(p0, u.shape), rolled)
    return u + (prev - u) * mu, prev


def _rwkv_specs(cfg, tr):
    RW, LP = cfg.RW, cfg.LP
    base = cfg.o_rwkv // RW
    cols = [(RW, base), (RW, base + 1), (RW, base + 2), (RW, base + 3), (LP, cfg.o_wd // LP), (LP, cfg.o_ad // LP)]
    cur = [pl.BlockSpec((tr, w), (lambda i, cb=cb: (i, cb))) for w, cb in cols]
    prv = [pl.BlockSpec((8, w), (lambda i, cb=cb: (jnp.maximum(i * (tr // 8) - 1, 0), cb))) for w, cb in cols]
    return cols, cur, prv


def _mu_pieces(cfg, mu_ref):
    RW, LP = cfg.RW, cfg.LP
    offs = [0, RW, 2 * RW, 3 * RW, 4 * RW, 4 * RW + LP, 4 * RW + 2 * LP]
    return [mu_ref[:, offs[j]:offs[j + 1]] for j in range(6)]


def _rwkv_prep_fwd(cfg, u, mu, w0, w2p, a0, a2p, k_k, k_a):
    T, RW, LP, tr = cfg.T, cfg.RW, cfg.LP, cfg.tr
    ind, ind_t, pad = _head_indicators(cfg)
    cols, cur, prv = _rwkv_specs(cfg, tr)

    def body(*refs):
        u_refs, p_refs = refs[0:6], refs[6:12]
        mu_ref, w0_ref, w2_ref, a0_ref, a2_ref, kk_ref, ka_ref, ind_ref, indt_ref, pad_ref = refs[12:22]
        outs = refs[22:]
        first = pl.program_id(0) == 0
        mus = _mu_pieces(cfg, mu_ref)
        us = [_shifted(u_refs[j][...], p_refs[j][7:8, :], mus[j], first)[0] for j in range(6)]
        res = _prep_fn(us[0], us[1], us[2], us[4], us[5], w0_ref[...], w2_ref[...], a0_ref[...], a2_ref[...],
                       kk_ref[...], ka_ref[...], ind_ref[...], indt_ref[...], pad_ref[...])
        for j in range(6):
            outs[j][...] = res[j]
        outs[6][...] = us[3]

    consts = [mu, w0, w2p, a0, a2p, k_k, k_a, ind, ind_t, pad]
    return _pcall(body, name="rwkv_prep_fwd", grid=(T // tr,),
                  in_specs=cur + prv + [_const(c.shape) for c in consts],
                  out_specs=[_tile(tr, RW)] * 7, out_shape=[jax.ShapeDtypeStruct((T, RW), F32)] * 7,
                  compiler_params=_cparams(("parallel",)))(*([u] * 12), *consts)


def _rwkv_prep_bwd(cfg, u, mu, w0, w2p, a0, a2p, k_k, k_a, cots, dzb):
    T, RW, LP = cfg.T, cfg.RW, cfg.LP
    tr = min(128, T)
    ind, ind_t, pad = _head_indicators(cfg)
    cols, cur, prv = _rwkv_specs(cfg, tr)
    rseg = cfg.rseg

    def body(*refs):
        u_refs, p_refs = refs[0:6], refs[6:12]
        mu_ref, w0_ref, w2_ref, a0_ref, a2_ref, kk_ref, ka_ref, ind_ref, indt_ref, pad_ref = refs[12:22]
        cot_refs, dzb_ref = refs[22:28], refs[28]
        dus_ref, dmu_ref, dw0_ref, dw2_ref, da0_ref, da2_ref, dkk_ref, dka_ref = refs[29:]
        i = pl.program_id(0)
        first = i == 0
        mus = _mu_pieces(cfg, mu_ref)
        sh = [_shifted(u_refs[j][...], p_refs[j][7:8, :], mus[j], first) for j in range(6)]
        us = [s[0] for s in sh]
        fn = functools.partial(_prep_fn, ind=ind_ref[...], ind_t=indt_ref[...], pad=pad_ref[...])
        _, vjp = jax.vjp(fn, us[0], us[1], us[2], us[4], us[5], w0_ref[...], w2_ref[...], a0_ref[...], a2_ref[...],
                         kk_ref[...], ka_ref[...])
        d = vjp(tuple(c[...] for c in cot_refs))
        dus = [d[0], d[1], d[2], dzb_ref[...], d[3], d[4]]
        offs = [0, RW, 2 * RW, 3 * RW, 4 * RW, 4 * RW + LP, 4 * RW + 2 * LP]
        for j in range(6):
            dus_ref[:, offs[j]:offs[j + 1]] = dus[j]
            dmu_j = jnp.sum(dus[j] * (sh[j][1] - u_refs[j][...]), axis=0, keepdims=True)

            @pl.when(first)
            def _(j=j, dmu_j=dmu_j):
                dmu_ref[:, offs[j]:offs[j + 1]] = dmu_j

            @pl.when(i > 0)
            def _(j=j, dmu_j=dmu_j):
                dmu_ref[:, offs[j]:offs[j + 1]] += dmu_j
        for ref, val in zip((dw0_ref, dw2_ref, da0_ref, da2_ref, dkk_ref, dka_ref), d[5:11]):
            _acc_store(i, ref, val)

    consts = [mu, w0, w2p, a0, a2p, k_k, k_a, ind, ind_t, pad]
    vec = jax.ShapeDtypeStruct((1, RW), F32)
    mat = jax.ShapeDtypeStruct((LP, RW), F32)
    return _pcall(body, name="rwkv_prep_bwd", grid=(T // tr,),
                  in_specs=cur + prv + [_const(c.shape) for c in consts] + [_tile(tr, RW)] * 7,
                  out_specs=[_tile(tr, rseg), _const((1, rseg)), _const((1, RW)), _const((LP, RW)), _const((1, RW)),
                             _const((LP, RW)), _const((1, RW)), _const((1, RW))],
                  out_shape=[jax.ShapeDtypeStruct((T, rseg), F32), jax.ShapeDtypeStruct((1, rseg), F32),
                             vec, mat, vec, mat, vec, vec],
                  compiler_params=_cparams(("arbitrary",)))(*([u] * 12), *consts, *cots, dzb)


def _shift_bwd(cfg, dus, mu, df, du):
    T, tr, RW, LP = cfg.T, cfg.tr, cfg.RW, cfg.LP
    nb = T // tr
    tail = cfg.ncol - cfg.o_f
    assert cfg.o_rwkv % (4 * RW) == 0 and (4 * RW) % (2 * LP) == 0 and cfg.o_f % tail == 0

    def shifted(d_ref, n_ref, mu_ref):
        d = d_ref[...]
        rolled = pltpu.roll(d, tr - 1, 0)
        row = lax.broadcasted_iota(jnp.int32, d.shape, 0)
        n0 = jnp.where(pl.program_id(0) == nb - 1, jnp.zeros_like(n_ref[0:1, :]), n_ref[0:1, :])
        nxt = jnp.where(row == tr - 1, jnp.broadcast_to(n0, d.shape), rolled)
        mu_v = mu_ref[...]
        return (d * (1.0 - mu_v) + nxt * mu_v).astype(BF16)

    def main_body(d_ref, n_ref, mu_ref, du_in, du_ref):
        du_ref[...] = shifted(d_ref, n_ref, mu_ref)

    def tail_body(d_ref, n_ref, mu_ref, df_ref, du_in, du_ref):
        du_ref[:, 0:LANES] = df_ref[...]
        du_ref[:, LANES:LANES + 2 * LP] = shifted(d_ref, n_ref, mu_ref)
        if tail > LANES + 2 * LP:
            du_ref[:, LANES + 2 * LP:] = jnp.zeros((tr, tail - LANES - 2 * LP), BF16)

    def specs(w, cb):
        return [_tile(tr, w, cb),
                pl.BlockSpec((8, w), lambda i: (jnp.minimum((i + 1) * (tr // 8), T // 8 - 1), cb)),
                pl.BlockSpec((1, w), lambda i: (0, cb))]

    out = jax.ShapeDtypeStruct(du.shape, BF16)
    du = _pcall(main_body, name="shift_bwd_main", grid=(nb,), in_specs=specs(4 * RW, 0) + [_ANY],
                out_specs=_tile(tr, 4 * RW, cfg.o_rwkv // (4 * RW)), out_shape=out, input_output_aliases={3: 0},
                compiler_params=_cparams(("parallel",)))(dus, dus, mu, du)
    return _pcall(tail_body, name="shift_bwd_tail", grid=(nb,),
                  in_specs=specs(2 * LP, 4 * RW // (2 * LP)) + [_tile(tr, LANES), _ANY],
                  out_specs=_tile(tr, tail, cfg.o_f // tail), out_shape=out, input_output_aliases={4: 0},
                  compiler_params=_cparams(("parallel",)))(dus, dus, mu, df, du)


def _chunk_local(r, lw, k, v, a, b):
    H, C, K = r.shape
    row = lax.broadcasted_iota(jnp.int32, (C, C), 0)
    col = lax.broadcasted_iota(jnp.int32, (C, C), 1)
    incl = jnp.broadcast_to((row >= col).astype(F32)[None], (H, C, C))
    strict = (row > col)[None]
    lower = (row >= col)[None]
    eye = (row == col)[None]
    zero = jnp.zeros((), F32)
    L = _bdot(incl, lw, 2, 1)
    LC = jnp.sum(lw, axis=1, keepdims=True)
    eL = jnp.exp(L)
    eLn = jnp.exp(-L)
    at = a * jnp.exp(L - lw)
    rt = r * eL
    bt = b * eLn
    kt = k * eLn
    eR = jnp.exp(LC - L)
    bh = b * eR
    kh = k * eR
    keys = functools.partial(_bdot, passes=SCAN_PASSES[0])
    inv = functools.partial(_bdot, passes=SCAN_PASSES[1])
    app = functools.partial(_bdot, passes=SCAN_PASSES[2])
    ar = jnp.concatenate([at, rt], axis=1)
    g_b = app(ar, bt, 2, 2)
    g_k = keys(ar, kt, 2, 2)
    n_ab = jnp.where(strict, g_b[:, :C], zero)
    n_ak = jnp.where(strict, g_k[:, :C], zero)
    m_rb = jnp.where(lower, g_b[:, C:], zero)
    m_rk = jnp.where(lower, g_k[:, C:], zero)
    M = n_ab
    P = jnp.where(eye, 1.0, zero) + n_ab
    for _ in range(1, max(1, int(np.ceil(np.log2(C))))):
        M = inv(M, M, 2, 1)
        P = P + inv(M, P, 2, 1)
    W = app(P, at, 2, 1)
    Uloc = app(P, app(n_ak, v, 2, 1), 2, 1)
    Q = rt + app(m_rb, W, 2, 1)
    Yloc = app(m_rb, Uloc, 2, 1) + app(m_rk, v, 2, 1)
    A = jnp.where(eye, jnp.exp(LC), zero) + app(W, bh, 1, 1)
    Sloc = app(Uloc, bh, 1, 1) + app(v, kh, 1, 1)
    return Q, Yloc, A, Sloc


def _split_heads(ref, n):
    N = RWKV_HEAD_DIM
    return jnp.stack([ref[:, h * N:(h + 1) * N] for h in range(n)], axis=0)


def _merge_heads(x):
    return jnp.concatenate([x[h] for h in range(x.shape[0])], axis=1)


def _scan_local_specs(cfg):
    N, HB = RWKV_HEAD_DIM, cfg.hb
    grid = (cfg.RH // HB, cfg.T // cfg.C)
    seq = pl.BlockSpec((HB, cfg.C, N), lambda h, j: (h, j, 0))
    mat = pl.BlockSpec((HB, 1, N, N), lambda h, j: (h, j, 0, 0))
    return grid, seq, mat


def _scan_local_fwd(cfg, seqs):
    T, RH, N = cfg.T, cfg.RH, RWKV_HEAD_DIM
    grid, seq, mat = _scan_local_specs(cfg)

    def body(r_ref, lw_ref, k_ref, v_ref, a_ref, b_ref, q_ref, yl_ref, a_out, sl_ref):
        Q, Yloc, A, Sloc = _chunk_local(*[_split_heads(ref, cfg.hb) for ref in (r_ref, lw_ref, k_ref, v_ref, a_ref, b_ref)])
        q_ref[...] = Q
        yl_ref[...] = Yloc
        a_out[:, 0] = A
        sl_ref[:, 0] = Sloc

    tok = pl.BlockSpec((cfg.C, cfg.hb * N), lambda h, j: (j, h))
    sq = jax.ShapeDtypeStruct((RH, T, N), F32)
    mt = jax.ShapeDtypeStruct((RH, T // cfg.C, N, N), F32)
    return _pcall(body, name="rwkv_scan_local_fwd", grid=grid, in_specs=[tok] * 6, out_specs=[seq, seq, mat, mat],
                  out_shape=[sq, sq, mt, mt], compiler_params=_cparams(("parallel", "parallel")))(*seqs)


def _scan_local_bwd(cfg, toks, dq, dy, da, dsl, extra, comm=None):
    T, RW, N = cfg.T, cfg.RW, RWKV_HEAD_DIM
    grid, seq, mat = _scan_local_specs(cfg)
    c_in, c_out, c_scr = comm[:3] if comm else ([], [], [])

    def body(r_ref, lw_ref, k_ref, v_ref, a_ref, b_ref, dq_ref, dy_ref, da_ref, dsl_ref, xr_ref, xk_ref, xv_ref,
             *rest):
        cin, outs = rest[:len(c_in)], rest[len(c_in):len(c_in) + 6]
        cout, scr = rest[len(c_in) + 6:len(c_in) + 6 + len(c_out)], rest[len(c_in) + 6 + len(c_out):]
        _comm_at(comm, 3, grid, cin, cout, scr)
        ins = [_split_heads(ref, cfg.hb) for ref in (r_ref, lw_ref, k_ref, v_ref, a_ref, b_ref)]
        _, vjp = jax.vjp(_chunk_local, *ins)
        d = vjp((dq_ref[...], _split_heads(dy_ref, cfg.hb), da_ref[:, 0], dsl_ref[:, 0]))
        add = {0: xr_ref, 2: xk_ref, 3: xv_ref}
        for j in range(6):
            dj = _merge_heads(d[j])
            outs[j][...] = dj + add[j][...] if j in add else dj
        _comm_at(comm, 4, grid, cin, cout, scr)

    tok = pl.BlockSpec((cfg.C, cfg.hb * N), lambda h, j: (j, h))
    return _pcall(body, name="rwkv_scan_local_bwd", grid=grid,
                  in_specs=[tok] * 6 + [seq, tok, mat, mat] + [tok] * 3 + [_ANY] * len(c_in),
                  out_specs=[tok] * 6 + [_ANY] * len(c_out),
                  out_shape=[jax.ShapeDtypeStruct((T, RW), F32)] * 6 + list(c_out), scratch_shapes=list(c_scr),
                  compiler_params=_cparams(("arbitrary", "arbitrary") if comm else ("parallel", "parallel")),
                  )(*toks, dq, dy, da, dsl, *extra, *c_in)


def _scan_carry_specs(cfg, rev):
    N, RH, C, nc = RWKV_HEAD_DIM, cfg.RH, cfg.C, cfg.T // cfg.C
    at = (lambda j: nc - 1 - j) if rev else (lambda j: j)
    seq = pl.BlockSpec((RH, C, N), lambda j: (0, at(j), 0))
    mat = pl.BlockSpec((RH, 1, N, N), lambda j: (0, at(j), 0, 0))
    return nc, seq, mat


def _scan_carry_fwd(cfg, q, yloc, a, sloc):
    T, RH, N = cfg.T, cfg.RH, RWKV_HEAD_DIM
    nc, seq, mat = _scan_carry_specs(cfg, False)

    def body(q_ref, yl_ref, a_ref, sl_ref, y_ref, ck_ref, s_ref):
        @pl.when(pl.program_id(0) == 0)
        def _():
            s_ref[...] = jnp.zeros_like(s_ref)

        S = s_ref[...]
        ck_ref[:, 0] = S
        y_ref[...] = _merge_heads(_bdot(q_ref[...], S, 2, 2, SCAN_PASSES[2]) + yl_ref[...])
        s_ref[...] = _bdot(S, a_ref[:, 0], 2, 1) + sl_ref[:, 0]

    tok = pl.BlockSpec((cfg.C, cfg.RW), lambda j: (j, 0))
    return _pcall(body, name="rwkv_scan_carry_fwd", grid=(nc,), in_specs=[seq, seq, mat, mat], out_specs=[tok, mat],
                  out_shape=[jax.ShapeDtypeStruct((T, cfg.RW), F32), jax.ShapeDtypeStruct((RH, nc, N, N), F32)],
                  scratch_shapes=[pltpu.VMEM((RH, N, N), F32)],
                  compiler_params=_cparams(("arbitrary",)))(q, yloc, a, sloc)


def _scan_carry_bwd(cfg, q, a, ckpt, dy, comm=None):
    T, RH, N = cfg.T, cfg.RH, RWKV_HEAD_DIM
    nc, seq, mat = _scan_carry_specs(cfg, True)

    def body(q_ref, a_ref, ck_ref, dy_ref, dq_ref, da_ref, dsl_ref, ds_ref):
        @pl.when(pl.program_id(0) == 0)
        def _():
            ds_ref[...] = jnp.zeros_like(ds_ref)

        S, dS, dY = ck_ref[:, 0], ds_ref[...], _split_heads(dy_ref, RH)
        dq_ref[...] = _bdot(dY, S, 2, 1, SCAN_PASSES[2])
        da_ref[:, 0] = _bdot(S, dS, 1, 1, SCAN_PASSES[2])
        dsl_ref[:, 0] = dS
        ds_ref[...] = _bdot(dS, a_ref[:, 0], 2, 2) + _bdot(dY, q_ref[...], 1, 1, SCAN_PASSES[2])

    mt = jax.ShapeDtypeStruct((RH, nc, N, N), F32)
    tok = pl.BlockSpec((cfg.C, cfg.RW), lambda j: (nc - 1 - j, 0))
    body, c_in, c_out, c_shapes, c_scr = _hosted(body, 4, 3, (nc,), comm)
    return _pcall(body, name="rwkv_scan_carry_bwd", grid=(nc,), in_specs=[seq, mat, mat, tok] + c_in,
                  out_specs=[seq, mat, mat] + c_out,
                  out_shape=[jax.ShapeDtypeStruct((RH, T, N), F32), mt, mt] + c_shapes,
                  scratch_shapes=c_scr + [pltpu.VMEM((RH, N, N), F32)],
                  compiler_params=_cparams(("arbitrary",)))(q, a, ckpt, dy, *(comm[0] if comm else []))


def _post_fn(y, r, kp, v, zb, ln_w, ln_b, rk, ind, ind_t):
    n = float(RWKV_HEAD_DIM)
    mu = _xdot(_xdot(y, ind, ind_t) / n, ind_t, ind)
    yc = y - mu
    var = _xdot(yc * yc, ind, ind_t) / n
    rstd = _xdot(lax.rsqrt(var + GN_EPS), ind_t, ind)
    yn = yc * rstd * ln_w + ln_b
    bonus = _xdot(_xdot(r * kp * rk, ind, ind_t), ind_t, ind) * v
    return (yn + bonus) * _silu(zb)


def _rwkv_post_fwd(cfg, y, r, kp, v, zb, ln_w, ln_b, rk):
    T, RW, tr = cfg.T, cfg.RW, cfg.tr
    ind, ind_t, _ = _head_indicators(cfg)

    def body(y_ref, r_ref, k_ref, v_ref, z_ref, lw_ref, lb_ref, rk_ref, ind_ref, indt_ref, ob_ref):
        ob_ref[...] = _post_fn(y_ref[...], r_ref[...], k_ref[...], v_ref[...], z_ref[...], lw_ref[...], lb_ref[...],
                               rk_ref[...], ind_ref[...], indt_ref[...]).astype(BF16)

    consts = [ln_w, ln_b, rk, ind, ind_t]
    return _pcall(body, name="rwkv_post_fwd", grid=(T // tr,),
                  in_specs=[_tile(tr, RW)] * 5 + [_const(c.shape) for c in consts],
                  out_specs=_tile(tr, RW), out_shape=jax.ShapeDtypeStruct((T, RW), BF16),
                  compiler_params=_cparams(("parallel",)))(y, r, kp, v, zb, *consts)


def _rwkv_post_bwd(cfg, y, r, kp, v, zb, ln_w, ln_b, rk, dob):
    T, RW = cfg.T, cfg.RW
    tr = min(128, T)
    ind, ind_t, _ = _head_indicators(cfg)

    def body(y_ref, r_ref, k_ref, v_ref, z_ref, lw_ref, lb_ref, rk_ref, ind_ref, indt_ref, dob_ref,
             dy_ref, dr_ref, dk_ref, dv_ref, dz_ref, dlw_ref, dlb_ref, drk_ref):
        fn = functools.partial(_post_fn, ind=ind_ref[...], ind_t=indt_ref[...])
        _, vjp = jax.vjp(fn, y_ref[...], r_ref[...], k_ref[...], v_ref[...], z_ref[...], lw_ref[...], lb_ref[...],
                         rk_ref[...])
        d = vjp(dob_ref[...])
        for ref, val in zip((dy_ref, dr_ref, dk_ref, dv_ref, dz_ref), d[:5]):
            ref[...] = val
        i = pl.program_id(0)
        for ref, val in zip((dlw_ref, dlb_ref, drk_ref), d[5:8]):
            _acc_store(i, ref, val)

    consts = [ln_w, ln_b, rk, ind, ind_t]
    vec = jax.ShapeDtypeStruct((1, RW), F32)
    return _pcall(body, name="rwkv_post_bwd", grid=(T // tr,),
                  in_specs=[_tile(tr, RW)] * 5 + [_const(c.shape) for c in consts] + [_tile(tr, RW)],
                  out_specs=[_tile(tr, RW)] * 5 + [_const((1, RW))] * 3,
                  out_shape=[jax.ShapeDtypeStruct((T, RW), F32)] * 5 + [vec] * 3,
                  compiler_params=_cparams(("arbitrary",)))(y, r, kp, v, zb, *consts, dob)


def _adamw_math(w, g, m, v):
    m = ADAM_B1 * m + (1.0 - ADAM_B1) * g
    v = ADAM_B2 * v + (1.0 - ADAM_B2) * (g * g)
    m_hat = m / (1.0 - ADAM_B1 ** ADAM_STEP)
    v_hat = v / (1.0 - ADAM_B2 ** ADAM_STEP)
    delta = -ADAM_LR * (m_hat / (jnp.sqrt(v_hat) + ADAM_EPS) + ADAM_WD * w)
    return delta, m, v


def _adamw(name, w, g, m, v, copy_grad=False, comm=None):
    R, Cc = w.shape
    Rp = -(-R // 8) * 8
    tr = Rp
    for nb in range(1, Rp // 8 + 1):
        if (Rp // 8) % nb == 0 and (Rp // nb) * Cc * 4 <= 2 * 1024 * 1024:
            tr = Rp // nb
            break

    def body(w_ref, g_ref, m_ref, v_ref, d_ref, nm_ref, nv_ref, *g_out):
        g_v = g_ref[...]
        d, nm, nv = _adamw_math(w_ref[...], g_v, m_ref[...], v_ref[...])
        d_ref[...] = d
        nm_ref[...] = nm
        nv_ref[...] = nv
        if copy_grad:
            g_out[0][...] = g_v

    spec = _tile(tr, Cc)
    n_out = 4 if copy_grad else 3
    body, c_in, c_out, c_shapes, c_scr = _hosted(body, 4, n_out, (Rp // tr,), comm)
    return _pcall(body, name=name, grid=(Rp // tr,), in_specs=[spec] * 4 + c_in, out_specs=[spec] * n_out + c_out,
                  out_shape=[jax.ShapeDtypeStruct((R, Cc), F32)] * n_out + c_shapes, scratch_shapes=c_scr,
                  compiler_params=_cparams(("arbitrary",) if comm else ("parallel",)),
                  )(w, g, m, v, *(comm[0] if comm else []))


def _row_tile(R, Cc, itemsize, budget=2 * 1024 * 1024):
    for nb in range(1, R // 16 + 1):
        if R % nb == 0 and (R // nb) % 16 == 0 and (R // nb) * Cc * itemsize <= budget:
            return R // nb
    return R


def _add_halves(name, gs, r1, c_idx):
    S, R, Cc = gs.shape
    half = R // 2
    tr = _row_tile(half, Cc, 4)
    nb = half // tr

    def body(c_ref, g_ref, r_ref, o_ref):
        o_ref[...] = (g_ref[...].astype(F32) + r_ref[...].astype(F32)).astype(BF16)

    grid_spec = pltpu.PrefetchScalarGridSpec(
        num_scalar_prefetch=1, grid=(S, nb),
        in_specs=[pl.BlockSpec((1, tr, Cc), lambda s, i, c: (s, c[0] * nb + i, 0)),
                  pl.BlockSpec((1, tr, Cc), lambda s, i, c: (s, i, 0))],
        out_specs=pl.BlockSpec((1, tr, Cc), lambda s, i, c: (s, i, 0)))
    return _pcall(body, name=name, grid_spec=grid_spec, out_shape=jax.ShapeDtypeStruct((S, half, Cc), BF16),
                  compiler_params=_cparams(("parallel", "parallel")))(c_idx, gs, r1)


def _sum_slots(name, r2):
    S, R, Cc = r2.shape
    tr = _row_tile(R, Cc, 4 * S // 2 if r2.dtype == BF16 else 4 * S)

    def body(r_ref, o_ref):
        acc = r_ref[0].astype(F32)
        for s in range(1, S):
            acc = acc + r_ref[s].astype(F32)
        o_ref[...] = acc

    return _pcall(body, name=name, grid=(R // tr,), in_specs=[pl.BlockSpec((S, tr, Cc), lambda i: (0, i, 0))],
                  out_specs=_tile(tr, Cc), out_shape=jax.ShapeDtypeStruct((R, Cc), F32),
                  compiler_params=_cparams(("parallel",)))(r2)


def _sum_chips(name, recv, own, place):
    S, H, Cc = recv.shape
    tr = _row_tile(H, Cc, 4, 1024 * 1024)
    nb = H // tr

    def body(p_ref, r_ref, own_ref, o_ref):
        s = pl.program_id(1)
        me = p_ref[0]

        @pl.when(s == 0)
        def _():
            o_ref[...] = jnp.zeros_like(o_ref)

        @pl.when(s == me)
        def _():
            o_ref[...] += own_ref[0].astype(F32)

        @pl.when(s != me)
        def _():
            o_ref[...] += r_ref[0].astype(F32)

    grid_spec = pltpu.PrefetchScalarGridSpec(
        num_scalar_prefetch=1, grid=(nb, S),
        in_specs=[pl.BlockSpec((1, tr, Cc), lambda i, s, p: (jnp.where(s == p[0], (s + 1) % S, s), i, 0)),
                  pl.BlockSpec((1, tr, Cc), lambda i, s, p: (p[0], i, 0))],
        out_specs=pl.BlockSpec((tr, Cc), lambda i, s, p: (p[1] * nb + i, 0)))
    return _pcall(body, name=name, grid_spec=grid_spec, out_shape=jax.ShapeDtypeStruct((2 * H, Cc), F32),
                  compiler_params=_cparams(("parallel", "arbitrary")))(place, recv, own)


def _cast_bf16(name, w):
    R, Cc = w.shape
    tr = _row_tile(R, Cc, 4)

    def body(w_ref, o_ref):
        o_ref[...] = w_ref[...].astype(BF16)

    return _pcall(body, name=name, grid=(R // tr,), in_specs=[_tile(tr, Cc)], out_specs=_tile(tr, Cc),
                  out_shape=jax.ShapeDtypeStruct((R, Cc), BF16), compiler_params=_cparams(("parallel",)))(w)


_ANY = pl.BlockSpec(memory_space=pl.ANY)


def _place():
    x, y, c = lax.axis_index("x"), lax.axis_index("y"), lax.axis_index("c")
    others = [(1 - x, y), (x, 1 - y), (1 - x, 1 - y)]
    return x, y, c, others


def _gather_parts(shards):
    n = len(shards)
    halves = [s.shape[0] // 2 for s in shards]

    def parts(ins, outs, sems):
        x, y, c, _ = _place()
        me = 2 * x + y
        n1 = (x ^ (1 - c), y ^ c)
        n2 = (x ^ c, y ^ (1 - c))
        s1, s2, sd = 2 * n1[0] + n1[1], 2 * n2[0] + n2[1], 2 * (1 - x) + (1 - y)
        sib = (x, y, 1 - c)

        def rows(k, chip, hc):
            return outs[k].at[chip, pl.ds(hc * halves[k], halves[k]), :]

        def remote(k, j, src, dst, to):
            return pltpu.make_async_remote_copy(src_ref=src, dst_ref=dst, send_sem=sems[0].at[6 * k + j],
                                                recv_sem=sems[1].at[6 * k + j], device_id=to, device_id_type=MESH)

        def copy(k, j):
            if j < 2:
                mine = ins[k].at[pl.ds(c * halves[k], halves[k]), :]
                return remote(k, j, mine, rows(k, me, c), (*(n1 if j == 0 else n2), c))
            land = rows(k, {2: s1, 3: s1, 4: s2, 5: sd}[j], c)
            return remote(k, j, land, land, (*n2, c) if j == 2 else sib)

        def arrived(k, j):
            hc = c if j < 3 else 1 - c
            land = rows(k, {0: s1, 1: s2, 2: sd, 3: s2, 4: s1, 5: sd}[j], hc)
            remote(k, j, land, land, (x, y, c)).wait_recv()

        return copy, arrived

    def start(ins, outs, sems):
        copy, _ = parts(ins, outs, sems)
        for k in range(n):
            copy(k, 0).start()
            copy(k, 1).start()

    def middle(ins, outs, sems):
        copy, arrived = parts(ins, outs, sems)
        for k in range(n):
            arrived(k, 0)
            copy(k, 2).start()
            copy(k, 3).start()
            arrived(k, 1)
            copy(k, 4).start()

    def finish(ins, outs, sems):
        copy, arrived = parts(ins, outs, sems)
        for k in range(n):
            arrived(k, 2)
            copy(k, 5).start()
        for k in range(n):
            for j in (3, 4, 5):
                arrived(k, j)
        for k in range(n):
            for j in range(6):
                copy(k, j).wait_send()

    out_shapes = [jax.ShapeDtypeStruct((N_CHIPS,) + s.shape, s.dtype) for s in shards]
    scratch = [pltpu.SemaphoreType.DMA((6 * n,)), pltpu.SemaphoreType.DMA((6 * n,))]
    return list(shards), out_shapes, scratch, start, finish, middle


def _swap_halves(grads):
    n = len(grads)
    halves = [g.shape[1] // 2 for g in grads]

    def copies(ins, outs, sems):
        x, y, c, _ = _place()
        return [pltpu.make_async_remote_copy(
            src_ref=ins[k].at[:, pl.ds((1 - c) * halves[k], halves[k]), :], dst_ref=outs[k], send_sem=sems[0].at[k],
            recv_sem=sems[1].at[k], device_id=(x, y, 1 - c), device_id_type=MESH) for k in range(n)]

    def start(ins, outs, sems):
        for cp in copies(ins, outs, sems):
            cp.start()

    def finish(ins, outs, sems):
        for cp in copies(ins, outs, sems):
            cp.wait()

    out_shapes = [jax.ShapeDtypeStruct((g.shape[0], h) + g.shape[2:], g.dtype) for g, h in zip(grads, halves)]
    scratch = [pltpu.SemaphoreType.DMA((n,)), pltpu.SemaphoreType.DMA((n,))]
    return list(grads), out_shapes, scratch, start, finish


def _scatter_to_owners(chip_sums):
    n = len(chip_sums)

    def sends(ins, outs, sems):
        x, y, c, others = _place()
        me = 2 * x + y
        return [pltpu.make_async_remote_copy(
            src_ref=ins[k].at[2 * px + py], dst_ref=outs[k].at[me], send_sem=sems[0].at[3 * k + j],
            recv_sem=sems[1].at[3 * k + j], device_id=(px, py, c), device_id_type=MESH)
            for k in range(n) for j, (px, py) in enumerate(others)]

    def start(ins, outs, sems):
        for cp in sends(ins, outs, sems):
            cp.start()

    def finish(ins, outs, sems):
        x, y, c, others = _place()
        for k in range(n):
            for j, (px, py) in enumerate(others):
                land = outs[k].at[2 * px + py]
                pltpu.make_async_remote_copy(src_ref=land, dst_ref=land, send_sem=sems[0].at[3 * k + j],
                                             recv_sem=sems[1].at[3 * k + j], device_id=(x, y, c),
                                             device_id_type=MESH).wait_recv()
        for cp in sends(ins, outs, sems):
            cp.wait_send()

    out_shapes = [jax.ShapeDtypeStruct(g.shape, g.dtype) for g in chip_sums]
    scratch = [pltpu.SemaphoreType.DMA((3 * n,)), pltpu.SemaphoreType.DMA((3 * n,))]
    return list(chip_sums), out_shapes, scratch, start, finish


def _swap_with_sibling(arrays):
    n = len(arrays)

    def copies(ins, outs, sems):
        x, y, c, _ = _place()
        return [pltpu.make_async_remote_copy(src_ref=ins[k], dst_ref=outs[k], send_sem=sems[0].at[k],
                                             recv_sem=sems[1].at[k], device_id=(x, y, 1 - c), device_id_type=MESH)
                for k in range(n)]

    def start(ins, outs, sems):
        for cp in copies(ins, outs, sems):
            cp.start()

    def finish(ins, outs, sems):
        for cp in copies(ins, outs, sems):
            cp.wait()

    out_shapes = [jax.ShapeDtypeStruct(a.shape, a.dtype) for a in arrays]
    scratch = [pltpu.SemaphoreType.DMA((n,)), pltpu.SemaphoreType.DMA((n,))]
    return list(arrays), out_shapes, scratch, start, finish


def _add_pair(name, a, b):
    R, Cc = a.shape
    tr = _row_tile(R, Cc, 4)

    def body(a_ref, b_ref, o_ref):
        o_ref[...] = (a_ref[...].astype(F32) + b_ref[...].astype(F32)).astype(BF16)

    return _pcall(body, name=name, grid=(R // tr,), in_specs=[_tile(tr, Cc)] * 2, out_specs=_tile(tr, Cc),
                  out_shape=jax.ShapeDtypeStruct((R, Cc), BF16), compiler_params=_cparams(("parallel",)))(a, b)


def _second_neighbour():
    x, y, c, _ = _place()
    return (x, y, c), (x ^ c, y ^ (1 - c)), (x ^ (1 - c), y ^ c)


def _scatter_stage1(chip_sums):
    n = len(chip_sums)

    def copies(ins, outs, sems):
        (x, y, c), n2, n1 = _second_neighbour()
        diag = 2 * (1 - x) + (1 - y)
        return [pltpu.make_async_remote_copy(
            src_ref=ins[k].at[slot], dst_ref=outs[2 * k + j], send_sem=sems[0].at[2 * k + j],
            recv_sem=sems[1].at[2 * k + j], device_id=(*n2, c), device_id_type=MESH)
            for k in range(n) for j, slot in enumerate((2 * n2[0] + n2[1], diag))]

    def start(ins, outs, sems):
        for cp in copies(ins, outs, sems):
            cp.start()

    def finish(ins, outs, sems):
        for cp in copies(ins, outs, sems):
            cp.wait()

    out_shapes = [jax.ShapeDtypeStruct(g.shape[1:], g.dtype) for g in chip_sums for _ in range(2)]
    scratch = [pltpu.SemaphoreType.DMA((2 * n,)), pltpu.SemaphoreType.DMA((2 * n,))]
    return list(chip_sums), out_shapes, scratch, start, finish


def _scatter_stage2(passed):
    n = len(passed)

    def copies(ins, outs, sems):
        (x, y, c), n2, n1 = _second_neighbour()
        return [pltpu.make_async_remote_copy(src_ref=ins[k], dst_ref=outs[k], send_sem=sems[0].at[k],
                                             recv_sem=sems[1].at[k], device_id=(*n1, c), device_id_type=MESH)
                for k in range(n)]

    def start(ins, outs, sems):
        for cp in copies(ins, outs, sems):
            cp.start()

    def finish(ins, outs, sems):
        for cp in copies(ins, outs, sems):
            cp.wait()

    out_shapes = [jax.ShapeDtypeStruct(p.shape, p.dtype) for p in passed]
    scratch = [pltpu.SemaphoreType.DMA((n,)), pltpu.SemaphoreType.DMA((n,))]
    return list(passed), out_shapes, scratch, start, finish


def _add_passed(name, own, got, slot):
    _, H, Cc = own.shape
    tr = _row_tile(H, Cc, 4)

    def body(s_ref, o_ref, g_ref, out_ref):
        out_ref[...] = (o_ref[0].astype(F32) + g_ref[...].astype(F32)).astype(BF16)

    grid_spec = pltpu.PrefetchScalarGridSpec(
        num_scalar_prefetch=1, grid=(H // tr,),
        in_specs=[pl.BlockSpec((1, tr, Cc), lambda i, s: (s[0], i, 0)), pl.BlockSpec((tr, Cc), lambda i, s: (i, 0))],
        out_specs=pl.BlockSpec((tr, Cc), lambda i, s: (i, 0)))
    return _pcall(body, name=name, grid_spec=grid_spec, out_shape=jax.ShapeDtypeStruct((H, Cc), BF16),
                  compiler_params=_cparams(("parallel",)))(slot, own, got)


def _sum_stages(name, own, direct, via, place, transposed=False):
    _, H, Cc = own.shape
    tr = LANES if transposed else _row_tile(H, Cc, 4, 1024 * 1024)
    nb = H // tr

    def body(p_ref, own_ref, d_ref, v_ref, o_ref):
        acc = (own_ref[0].astype(F32) + d_ref[...].astype(F32)) + v_ref[...].astype(F32)
        o_ref[...] = acc.T if transposed else acc

    flat = pl.BlockSpec((tr, Cc), lambda i, p: (i, 0))
    out_spec = (pl.BlockSpec((Cc, tr), lambda i, p: (0, p[1] * nb + i)) if transposed
                else pl.BlockSpec((tr, Cc), lambda i, p: (p[1] * nb + i, 0)))
    grid_spec = pltpu.PrefetchScalarGridSpec(
        num_scalar_prefetch=1, grid=(nb,),
        in_specs=[pl.BlockSpec((1, tr, Cc), lambda i, p: (p[0], i, 0)), flat, flat], out_specs=out_spec)
    return _pcall(body, name=name, grid_spec=grid_spec,
                  out_shape=jax.ShapeDtypeStruct((Cc, 2 * H) if transposed else (2 * H, Cc), F32),
                  compiler_params=_cparams(("parallel",)))(place, own, direct, via)


def _join_halves(fulls, axes, small):
    n = len(fulls)
    hs = [f.shape[ax] // 2 for f, ax in zip(fulls, axes)]
    rel = [(dx, dy, dc) for dx in (0, 1) for dy in (0, 1) for dc in (0, 1)][1:]

    def half(ref, k, hc):
        part = pl.ds(hc * hs[k], hs[k])
        return ref.at[:, part] if axes[k] else ref.at[part, :]

    def body(*refs):
        ins, small_in = refs[:n], refs[n]
        outs, small_out = refs[n + 1:2 * n + 1], refs[2 * n + 1]
        send_sems, recv_sems, ssend, srecv, local_sem = refs[2 * n + 2:]
        x, y, c, _ = _place()
        dev = 4 * x + 2 * y + c
        local = pltpu.make_async_copy(small_in, small_out.at[dev], local_sem)
        local.start()
        cps = []
        for k in range(n):
            cp = pltpu.make_async_remote_copy(src_ref=half(ins[k], k, c), dst_ref=half(outs[k], k, c),
                                              send_sem=send_sems.at[k], recv_sem=recv_sems.at[k],
                                              device_id=(x, y, 1 - c), device_id_type=MESH)
            cp.start()
            cps.append(cp)
        for r, (dx, dy, dc) in enumerate(rel):
            cp = pltpu.make_async_remote_copy(src_ref=small_in, dst_ref=small_out.at[dev], send_sem=ssend.at[r],
                                              recv_sem=srecv.at[r], device_id=(x ^ dx, y ^ dy, c ^ dc),
                                              device_id_type=MESH)
            cp.start()
            cps.append(cp)
        for k in range(n):
            land = half(outs[k], k, 1 - c)
            pltpu.make_async_remote_copy(src_ref=land, dst_ref=land, send_sem=send_sems.at[k],
                                         recv_sem=recv_sems.at[k], device_id=(x, y, c), device_id_type=MESH).wait_recv()
        for r, (dx, dy, dc) in enumerate(rel):
            land = small_out.at[4 * (x ^ dx) + 2 * (y ^ dy) + (c ^ dc)]
            pltpu.make_async_remote_copy(src_ref=land, dst_ref=land, send_sem=ssend.at[r], recv_sem=srecv.at[r],
                                         device_id=(x, y, c), device_id_type=MESH).wait_recv()
        for cp in cps:
            cp.wait_send()
        local.wait()

    return _pcall(
        body, name="join_halves", in_specs=[_ANY] * (n + 1), out_specs=[_ANY] * (n + 1),
        out_shape=[jax.ShapeDtypeStruct(f.shape, f.dtype) for f in fulls]
        + [jax.ShapeDtypeStruct((N_DEV,) + small.shape, small.dtype)],
        input_output_aliases={k: k for k in range(n)},
        scratch_shapes=[pltpu.SemaphoreType.DMA((n,)), pltpu.SemaphoreType.DMA((n,)), pltpu.SemaphoreType.DMA((7,)),
                        pltpu.SemaphoreType.DMA((7,)), pltpu.SemaphoreType.DMA],
    )(*fulls, small)


def _local_step(cfg, x2, target, norm_gain, w_my, fb, mu_g, w0, a0, k_k, k_a, r_k, ln_w, ln_b, fng, rest,
                exchange=None, h=None):
    T, D, FW, FH, RW, RH, LP, lora = cfg.T, cfg.D, cfg.FW, cfg.FH, cfg.RW, cfg.RH, cfg.LP, cfg.lora
    fb_p = jnp.pad(fb, ((0, 0), (0, LANES - FH)))
    mu = _rwkv_vec_to_my(cfg, mu_g)
    rk = r_k.reshape(1, RW)
    tm = min(1024, T)

    if h is None:
        h = _rms_fwd(cfg, x2, norm_gain)
    if len(rest) == 2:
        u, *got = _mm("in_proj", h, w_my, "nn", F32, tm, cfg.tn, 2048, comm=rest[0])
        rest = rest[1](got)
    else:
        u = _mm("in_proj", h, w_my, "nn", F32, tm, cfg.tn, 2048)
    w2, a2, wpf, wpr, wout = rest
    w2p = jnp.pad(w2, ((0, LP - lora), (0, 0)))
    a2p = jnp.pad(a2, ((0, LP - lora), (0, 0)))
    c_cols = _fox_prep(cfg, u, fb_p)
    c_rows = c_cols[:, :FH].T.reshape(FH, 1, T)
    o, lse = _attn_fwd(cfg, u, c_rows)
    oa = _gate_a_fwd(cfg, o, u)
    prep = _rwkv_prep_fwd(cfg, u, mu, w0, w2p, a0, a2p, k_k, k_a)
    r, lw, kp, v, an, b, zb = prep
    toks = [r, lw, kp, v, an, b]
    q_s, yloc, a_m, sloc = _scan_local_fwd(cfg, toks)
    y, ckpt = _scan_carry_fwd(cfg, q_s, yloc, a_m, sloc)
    ob = _rwkv_post_fwd(cfg, y, r, kp, v, zb, ln_w, ln_b, rk)
    pa = _mm("proj_fox", oa, wpf, "nn", F32, tm, 1024, 2048)
    pb = _mm("proj_rwkv", ob, wpr, "nn", F32, tm, 1024, 2048)
    m = _merge_fwd(cfg, pa, pb, u)
    mo = _mm("out_proj", m, wout, "nn", F32, tm, 1024, 2048)
    loss8, dres, dres16, d_fng = _final(cfg, x2, mo, fng.reshape(1, D), target)

    dm = _mm("out_proj_dx", dres16, wout, "nt", F32, tm, 1024, 2048)
    d_wout = _mm("out_proj_dw", m, dres16, "tn", BF16, 1024, 1024, 2048)
    dpa, dpb, du = _merge_bwd(cfg, pa, pb, u, dm)
    doa = _mm("proj_fox_dx", dpa, wpf, "nt", F32, tm, 1024, 2048)
    d_wpf = _mm("proj_fox_dw", oa, dpa, "tn", BF16, 1024, 1024, 2048)
    dob = _mm("proj_rwkv_dx", dpb, wpr, "nt", F32, tm, 1024, 2048)
    d_wpr = _mm("proj_rwkv_dw", ob, dpb, "tn", BF16, 1024, 1024, 2048)

    do, du = _gate_a_bwd(cfg, o, u, doa, du)
    du, dcol = _attn_bwd(cfg, u, c_rows, lse, do, du)
    dc = jnp.pad(-dcol.reshape(FH, T).T, ((0, 0), (0, LANES - FH)))
    df, d_fb = _fox_prep_bwd(cfg, u, fb_p, dc)

    dy, dr_p, dk_p, dv_p, dzb, d_lnw, d_lnb, d_rk = _rwkv_post_bwd(cfg, y, r, kp, v, zb, ln_w, ln_b, rk, dob)
    early = dict(w_proj_fox=d_wpf, w_proj_rwkv=d_wpr, w_out=d_wout)
    res = _scan_carry_bwd(cfg, q_s, a_m, ckpt, dy, exchange(early) if exchange else None)
    dq_s, da_m, dsl = res[:3]
    res = _scan_local_bwd(cfg, toks, dq_s, dy, da_m, dsl, [dr_p, dk_p, dv_p],
                          exchange(("swapped", list(res[3:]))) if exchange else None)
    cots, received = res[:6], list(res[6:])
    dus, d_mu, d_w0, d_w2p, d_a0, d_a2p, d_kk, d_ka = _rwkv_prep_bwd(cfg, u, mu, w0, w2p, a0, a2p, k_k, k_a, cots, dzb)
    du = _shift_bwd(cfg, dus, mu, df, du)
    if exchange:
        late = dict(w_in=exchange((h, du, d_w2p[:lora], d_a2p[:lora])))
    else:
        late = dict(w_in=_mm("in_proj_dw", h, du, "tn", BF16, 1024, cfg.tn, 2048), rwkv_w2=d_w2p[:lora],
                    rwkv_a2=d_a2p[:lora])
    tkx = 2 * cfg.tn if cfg.ncol % (2 * cfg.tn) == 0 else cfg.tn
    res = _mm("in_proj_dx", du, w_my, "nt", F32, tm, 1024, tkx, comm=exchange(late) if exchange else None)
    dh = res[0] if exchange else res
    big = dict(early, **late)
    res = _rms_bwd(cfg, x2, norm_gain, dh, dres, exchange(list(res[1:])) if exchange else None)
    gx, d_ng = res[:2]
    received += list(res[2:])

    small = dict(norm_gain=d_ng, fox_forget_bias=d_fb[:, :FH], rwkv_shift_mix=_rwkv_vec_from_my(cfg, d_mu),
                 rwkv_w0=d_w0, rwkv_a0=d_a0, rwkv_k_k=d_kk, rwkv_k_a=d_ka, rwkv_r_k=d_rk, rwkv_ln_w=d_lnw,
                 rwkv_ln_b=d_lnb, final_norm_gain=d_fng)
    return loss8[0, 0], gx, small, big, received


_SMALL = ["norm_gain", "fox_forget_bias", "rwkv_shift_mix", "rwkv_w0", "rwkv_a0", "rwkv_k_k", "rwkv_k_a", "rwkv_r_k",
          "rwkv_ln_w", "rwkv_ln_b", "final_norm_gain"]
_WEIGHTS = ["norm_gain", "w_in", "fox_forget_bias", "rwkv_shift_mix", "rwkv_w0", "rwkv_w2", "rwkv_a0", "rwkv_a2",
            "rwkv_k_k", "rwkv_k_a", "rwkv_r_k", "rwkv_ln_w", "rwkv_ln_b", "w_proj_fox", "w_proj_rwkv", "w_out",
            "final_norm_gain"]


def _pack_small(arrs):
    parts = []
    for a in arrs:
        f = a.reshape(-1)
        parts.append(jnp.pad(f, (0, (-f.shape[0]) % LANES)))
    flat = jnp.concatenate(parts)
    rows = flat.shape[0] // LANES
    flat = jnp.pad(flat, (0, ((-rows) % 8) * LANES))
    return flat.reshape(-1, LANES)


def _unpack_small(packed, shapes):
    flat = packed.reshape(-1)
    out, pos = [], 0
    for s in shapes:
        n = int(np.prod(s))
        out.append(flat[pos:pos + n].reshape(s))
        pos += n + ((-n) % LANES)
    return out


def _shard_major(a, axis):
    parts = jnp.split(a, N_CHIPS, axis=axis)
    return jnp.stack(parts, axis=0)


def kernel(x, norm_gain, w_in, fox_forget_bias, rwkv_shift_mix, rwkv_w0, rwkv_w2, rwkv_a0, rwkv_a2, rwkv_k_k, rwkv_k_a, rwkv_r_k, rwkv_ln_w, rwkv_ln_b, w_proj_fox, w_proj_rwkv, w_out, final_norm_gain, loss_target, m_norm_gain, m_w_in, m_fox_forget_bias, m_rwkv_shift_mix, m_rwkv_w0, m_rwkv_w2, m_rwkv_a0, m_rwkv_a2, m_rwkv_k_k, m_rwkv_k_a, m_rwkv_r_k, m_rwkv_ln_w, m_rwkv_ln_b, m_w_proj_fox, m_w_proj_rwkv, m_w_out, m_final_norm_gain, v_norm_gain, v_w_in, v_fox_forget_bias, v_rwkv_shift_mix, v_rwkv_w0, v_rwkv_w2, v_rwkv_a0, v_rwkv_a2, v_rwkv_k_k, v_rwkv_k_a, v_rwkv_r_k, v_rwkv_ln_w, v_rwkv_ln_b, v_w_proj_fox, v_w_proj_rwkv, v_w_out, v_final_norm_gain):
    args = dict(locals())
    T, D = x.shape[1], x.shape[2]
    lora = rwkv_w2.shape[1]
    cfg = _Cfg(T, D, lora)
    RW = cfg.RW
    c_idx = lax.axis_index("c").astype(jnp.int32).reshape(1)
    me_chip = (2 * lax.axis_index("x") + lax.axis_index("y")).astype(jnp.int32)
    place = jnp.concatenate([me_chip.reshape(1), c_idx])

    w_in_s = w_in[0].astype(BF16)
    lora_s = jnp.concatenate([rwkv_w2[0], rwkv_a2[0]], axis=0)
    own_slot = lambda g, own: lax.dynamic_update_slice(g, own[None], (me_chip, 0, 0))
    h, g_in = _rms_fwd(cfg, x[0], norm_gain, _gather_parts([w_in_s]))
    w_my = _shards_to_my_layout(cfg, own_slot(g_in, w_in_s))
    mine = [_cast_bf16("cast_w_proj_fox", w_proj_fox[0]), _cast_bf16("cast_w_proj_rwkv", w_proj_rwkv[0]),
            _cast_bf16("cast_w_out", w_out[0]), lora_s]

    def unpack(gathered):
        g_wpf, g_wpr, g_out, g_lora = [own_slot(g, own) for g, own in zip(gathered, mine)]
        lo = g_lora.transpose(1, 0, 2).reshape(2 * lora, RW)
        return (lo[:lora], lo[lora:], g_wpf.transpose(1, 0, 2).reshape(RW, D),
                g_wpr.transpose(1, 0, 2).reshape(RW, D), g_out.reshape(D, D))

    early, late = ["w_proj_fox", "w_proj_rwkv", "w_out"], ["w_in", "lora"]
    names = early + late
    chip_sums, direct, shard_major = {}, {}, []
    n1_slot = (2 * (lax.axis_index("x") ^ (1 - lax.axis_index("c")))
               + (lax.axis_index("y") ^ lax.axis_index("c"))).astype(jnp.int32).reshape(1)

    def exchange(got):
        if isinstance(got, tuple) and len(got) == 4:
            h, du, d_w2, d_a2 = got
            c, half = lax.axis_index("c"), D // 2
            cols = lambda base: lax.dynamic_slice_in_dim(h, base * half, half, axis=1)
            lora_g = _shard_major(jnp.concatenate([d_w2, d_a2], axis=0).astype(BF16), 1)
            lora_rows = lambda base: lax.dynamic_slice_in_dim(lora_g, base * lora, lora, axis=1).reshape(-1, RW // 4)
            tiles = (BF16, min(1024, half), cfg.tn, 2048)
            sent = _mm("in_proj_dw_sibling", cols(1 - c), du, "tn", *tiles)
            kept, got_w, got_l = _mm("in_proj_dw", cols(c), du, "tn", *tiles,
                                     comm=_swap_with_sibling([sent, lora_rows(1 - c)]))
            return (_add_pair("add_halves_w_in", kept, got_w),
                    _add_pair("add_halves_lora", lora_rows(c), got_l).reshape(N_CHIPS, lora, RW // 4))
        if isinstance(got, dict):
            if "w_in" in got:
                sums = [_my_layout_to_shards(cfg, got["w_in"][0]), got["w_in"][1]]
                chip_sums.update(zip(late, sums))
                return _scatter_stage1(sums)
            shard_major.extend([_shard_major(got["w_proj_fox"], 1), _shard_major(got["w_proj_rwkv"], 1),
                                _shard_major(got["w_out"], 0)])
            return _swap_halves(shard_major)
        if got[0] == "swapped":
            sums = [_add_halves("add_halves_" + nm, g, r, c_idx) for nm, g, r in zip(early, shard_major, got[1])]
            chip_sums.update(zip(early, sums))
            return _scatter_to_owners(sums)
        direct.update(zip(late, got[0::2]))
        return _scatter_stage2([_add_passed("add_passed_" + nm, chip_sums[nm], g, n1_slot)
                                for nm, g in zip(late, got[1::2])])

    loss_dev, gx, small, _, recv2 = _local_step(
        cfg, x[0], loss_target[0], norm_gain, w_my, fox_forget_bias, rwkv_shift_mix, rwkv_w0, rwkv_a0, rwkv_k_k,
        rwkv_k_a, rwkv_r_k, rwkv_ln_w, rwkv_ln_b, final_norm_gain, (_gather_parts(mine), unpack), exchange, h)
    loss = lax.psum(loss_dev, ("x", "y", "c"))

    small_shapes = [args[nm].shape for nm in _SMALL]
    packed = _pack_small([small[nm] for nm in _SMALL])
    reduced = [_sum_chips("sum_chips_" + nm, r, chip_sums[nm], place) for nm, r in zip(early, recv2[:3])]
    reduced += [_sum_stages("sum_stages_" + nm, chip_sums[nm], direct[nm], via, place, transposed=nm == "w_in")
                for nm, via in zip(late, recv2[3:])]
    *joined, small_all = _join_halves(reduced, [int(nm == "w_in") for nm in names], packed)
    g_small = _sum_slots("sum_small", small_all)

    grads = dict(zip(_SMALL, _unpack_small(g_small, small_shapes)))
    grads.update({nm: g[None] for nm, g in zip(names, joined) if nm not in ("lora", "w_in")})
    g_lora_f = joined[names.index("lora")]
    grads["rwkv_w2"] = g_lora_f[None, :lora]
    grads["rwkv_a2"] = g_lora_f[None, lora:]

    delta, new_m, new_v = {}, {}, {}
    w_small = _pack_small([args[nm] for nm in _SMALL])
    m_small = _pack_small([args["m_" + nm] for nm in _SMALL])
    v_small = _pack_small([args["v_" + nm] for nm in _SMALL])
    d_s, m_s, v_s = _adamw("adamw_small", w_small, g_small, m_small, v_small)
    for tgt, pk in ((delta, d_s), (new_m, m_s), (new_v, v_s)):
        tgt.update(zip(_SMALL, _unpack_small(pk, small_shapes)))
    t_out = _adamw("adamw_w_in", w_in[0].T, joined[names.index("w_in")], m_w_in[0].T, v_w_in[0].T, copy_grad=True)
    delta["w_in"], new_m["w_in"], new_v["w_in"], grads["w_in"] = [t.T[None] for t in t_out]
    for nm in ("w_proj_fox", "w_proj_rwkv", "w_out", "rwkv_w2", "rwkv_a2"):
        shp = args[nm].shape
        two_d = (shp[1], shp[2])
        d_b, m_b, v_b = _adamw("adamw_" + nm, args[nm].reshape(two_d), grads[nm].reshape(two_d),
                               args["m_" + nm].reshape(two_d), args["v_" + nm].reshape(two_d))
        delta[nm], new_m[nm], new_v[nm] = d_b.reshape(shp), m_b.reshape(shp), v_b.reshape(shp)

    return (loss, gx[None], *[grads[n] for n in _WEIGHTS], *[delta[n] for n in _WEIGHTS],
            *[new_m[n] for n in _WEIGHTS], *[new_v[n] for n in _WEIGHTS])
```

```python
import functools

import numpy as np
import jax
import jax.numpy as jnp
from jax import lax
from jax.experimental import pallas as pl
from jax.experimental.pallas import tpu as pltpu

F32 = jnp.float32
BF16 = jnp.bfloat16
HI = lax.Precision.HIGHEST
MESH = pl.DeviceIdType.MESH

FOX_HEAD_DIM = 128
RWKV_HEAD_DIM = 64
RMS_EPS = 1e-6
GN_EPS = 64e-5
L2_EPS = 1e-12
ADAM_LR = 0.001
ADAM_B1 = 0.9
ADAM_B2 = 0.999
ADAM_EPS = 1e-08
ADAM_WD = 0.01
ADAM_STEP = 10

LANES = 128
VMEM_LIMIT = 56 * 1024 * 1024
SCAN_CHUNK = 64
SCAN_HEADS_PER_STEP = 16
SCAN_PASSES = (3, 1, 1)
N_CHIPS = 4
N_DEV = 8

_pcall = pl.pallas_call


def _cparams(sem=None):
    return pltpu.CompilerParams(dimension_semantics=sem, vmem_limit_bytes=VMEM_LIMIT)


def _softplus(x):
    return jnp.maximum(x, 0.0) + jnp.log(1.0 + jnp.exp(-jnp.abs(x)))


def _silu(z):
    return z * jax.nn.sigmoid(z)


def _rmsn(x, g):
    return x * lax.rsqrt(jnp.mean(x * x, axis=-1, keepdims=True) + RMS_EPS) * g


def _dot(a, b, dims="nn", precision=None):
    dn = {"nn": (((1,), (0,)), ((), ())), "nt": (((1,), (1,)), ((), ())), "tn": (((0,), (0,)), ((), ()))}[dims]
    return lax.dot_general(a, b, dn, precision=precision, preferred_element_type=F32)


def _split_bf16(x):
    hi = x.astype(BF16)
    return hi, (x - hi.astype(F32)).astype(BF16)


def _bdot_raw(a, b, ca, cb, passes):
    dn = (((ca,), (cb,)), ((0,), (0,)))
    mm = lambda p, q: lax.dot_general(p, q, dn, preferred_element_type=F32)
    if passes == 1:
        return mm(a.astype(BF16), b.astype(BF16))
    ah, al = _split_bf16(a)
    bh, bl = _split_bf16(b)
    return mm(ah, bh) + (mm(ah, bl) + mm(al, bh))


@functools.partial(jax.custom_vjp, nondiff_argnums=(2, 3, 4))
def _bdot_p(a, b, ca, cb, passes):
    return _bdot_raw(a, b, ca, cb, passes)


def _bdot_fwd(a, b, ca, cb, passes):
    return _bdot_raw(a, b, ca, cb, passes), (a, b)


def _bdot_bwd(ca, cb, passes, res, g):
    a, b = res
    if (ca, cb) == (2, 1):
        return _bdot_p(g, b, 2, 2, passes), _bdot_p(a, g, 1, 1, passes)
    if (ca, cb) == (2, 2):
        return _bdot_p(g, b, 2, 1, passes), _bdot_p(g, a, 1, 1, passes)
    assert (ca, cb) == (1, 1)
    return _bdot_p(b, g, 2, 2, passes), _bdot_p(a, g, 2, 1, passes)


_bdot_p.defvjp(_bdot_fwd, _bdot_bwd)


def _bdot(a, b, ca, cb, passes=3):
    return _bdot_p(a, b, ca, cb, passes)


def _dot3(a, b):
    return _bdot(a[None], b[None], 2, 1)[0]


@jax.custom_vjp
def _xdot(x, m, mt):
    hi, lo = _split_bf16(x)
    m16 = m.astype(BF16)
    return _dot(hi, m16) + _dot(lo, m16)


def _xdot_fwd(x, m, mt):
    return _xdot(x, m, mt), (m, mt)


def _xdot_bwd(res, g):
    m, mt = res
    return _xdot(g, mt, m), jnp.zeros_like(m), jnp.zeros_like(mt)


_xdot.defvjp(_xdot_fwd, _xdot_bwd)


class _Cfg:
    def __init__(self, T, D, lora):
        self.T, self.D, self.lora = T, D, lora
        self.FW = D // 2
        self.FH = self.FW // FOX_HEAD_DIM
        self.RW = D // 2
        self.RH = self.RW // RWKV_HEAD_DIM
        self.LP = -(-lora // LANES) * LANES
        self.o_fox = 0
        self.o_rwkv = 4 * self.FW
        self.o_gate = self.o_rwkv + 4 * self.RW
        self.o_f = self.o_gate + 2 * D
        self.o_wd = self.o_f + LANES
        self.o_ad = self.o_wd + self.LP
        end = self.o_ad + self.LP
        self.tn = 1280 if D >= 2048 else LANES
        self.ncol = -(-end // self.tn) * self.tn
        self.in_cols = 4 * self.FW + self.FH + 4 * self.RW + 2 * lora + 2 * D
        self.scp = -(-(self.in_cols // N_CHIPS) // LANES) * LANES
        self.rseg = 4 * self.RW + 2 * self.LP
        self.C = min(SCAN_CHUNK, T)
        self.tr = min(256, T)
        self.hb = min(SCAN_HEADS_PER_STEP, self.RH)

    def segments(self):
        FW, FH, RW, lo, D = self.FW, self.FH, self.RW, self.lora, self.D
        g_f = 4 * FW
        g_r = g_f + FH
        g_wd = g_r + 4 * RW
        g_ad = g_wd + lo
        g_g = g_ad + lo
        dh = FOX_HEAD_DIM
        qkv = [(j * FW + h * dh, dh, (3 * h + j) * dh) for h in range(FH) for j in range(3)]
        return qkv + [(3 * FW, FW, 3 * FW), (g_f, FH, self.o_f), (g_r, 4 * RW, self.o_rwkv), (g_wd, lo, self.o_wd),
                      (g_ad, lo, self.o_ad), (g_g, 2 * D, self.o_gate)]


def _shards_to_my_layout(cfg, g):
    R, sc = g.shape[1], g.shape[2]
    segs = sorted(cfg.segments(), key=lambda s: s[2])
    parts, pos = [], 0
    for g0, w, m0 in segs:
        if m0 > pos:
            parts.append(jnp.zeros((R, m0 - pos), g.dtype))
        for s in range(N_CHIPS):
            lo, hi = max(g0, s * sc), min(g0 + w, (s + 1) * sc)
            if lo < hi:
                parts.append(g[s, :, lo - s * sc:hi - s * sc])
        pos = m0 + w
    if cfg.ncol > pos:
        parts.append(jnp.zeros((R, cfg.ncol - pos), g.dtype))
    return jnp.concatenate(parts, axis=1)


def _my_layout_to_shards(cfg, wm):
    sc, R = cfg.in_cols // N_CHIPS, wm.shape[0]
    segs = sorted(cfg.segments(), key=lambda s: s[0])
    shards = []
    for s in range(N_CHIPS):
        parts = []
        for g0, w, m0 in segs:
            lo, hi = max(g0, s * sc), min(g0 + w, (s + 1) * sc)
            if lo < hi:
                parts.append(wm[:, m0 + lo - g0:m0 + hi - g0])
        parts.append(jnp.zeros((R, cfg.scp - sc), wm.dtype))
        shards.append(jnp.concatenate(parts, axis=1))
    return jnp.stack(shards, axis=0)


def _rwkv_vec_to_my(cfg, v):
    RW4, lo, LP = 4 * cfg.RW, cfg.lora, cfg.LP
    z = jnp.zeros((1, LP - lo), v.dtype)
    return jnp.concatenate([v[:, :RW4], v[:, RW4:RW4 + lo], z, v[:, RW4 + lo:], z], axis=1)


def _rwkv_vec_from_my(cfg, v):
    RW4, lo, LP = 4 * cfg.RW, cfg.lora, cfg.LP
    return jnp.concatenate([v[:, :RW4], v[:, RW4:RW4 + lo], v[:, RW4 + LP:RW4 + LP + lo]], axis=1)


def _comm_at(comm, which, steps, cin, cout, scr):
    if not comm or len(comm) <= which:
        return
    lin, total = 0, 1
    for d, n in enumerate(steps):
        lin = lin * n + pl.program_id(d)
        total *= n
    pl.when(lin == {3: 0, 4: total - 1, 5: total // 2}[which])(lambda: comm[which](cin, cout, scr))


def _hosted(body, n_in, n_out, steps, comm):
    if not comm:
        return body, [], [], [], []
    ci, co, cs = len(comm[0]), len(comm[1]), len(comm[2])

    def wrapped(*refs):
        ins, cin = refs[:n_in], refs[n_in:n_in + ci]
        outs, cout = refs[n_in + ci:n_in + ci + n_out], refs[n_in + ci + n_out:n_in + ci + n_out + co]
        cscr, scr = refs[n_in + ci + n_out + co:n_in + ci + n_out + co + cs], refs[n_in + ci + n_out + co + cs:]
        _comm_at(comm, 3, steps, cin, cout, cscr)
        body(*ins, *outs, *scr)
        _comm_at(comm, 5, steps, cin, cout, cscr)
        _comm_at(comm, 4, steps, cin, cout, cscr)

    return wrapped, [_ANY] * ci, [_ANY] * co, list(comm[1]), list(comm[2])


def _mm(name, a, b, dims, out_dtype, tm, tn, tk, comm=None):
    (M, K) = a.shape if dims != "tn" else a.shape[::-1]
    N = b.shape[0] if dims == "nt" else b.shape[1]
    tm, tn, tk = min(tm, M), min(tn, N), min(tk, K)
    assert M % tm == 0 and N % tn == 0 and K % tk == 0, (name, M, N, K, tm, tn, tk)
    nk = K // tk
    steps = (M // tm, N // tn, nk)
    c_in, c_out, c_scr = comm[:3] if comm else ([], [], [])
    if dims == "nn":
        a_spec = pl.BlockSpec((tm, tk), lambda i, j, k: (i, k))
        b_spec = pl.BlockSpec((tk, tn), lambda i, j, k: (k, j))
    elif dims == "nt":
        a_spec = pl.BlockSpec((tm, tk), lambda i, j, k: (i, k))
        b_spec = pl.BlockSpec((tn, tk), lambda i, j, k: (j, k))
    else:
        a_spec = pl.BlockSpec((tk, tm), lambda i, j, k: (k, i))
        b_spec = pl.BlockSpec((tk, tn), lambda i, j, k: (k, j))

    n_acc = 1 if nk > 1 else 0

    def body(a_ref, b_ref, *rest):
        cin, o_ref = rest[:len(c_in)], rest[len(c_in)]
        cout = rest[len(c_in) + 1:len(c_in) + 1 + len(c_out)]
        scr = rest[len(c_in) + 1 + len(c_out):]
        _comm_at(comm, 3, steps, cin, cout, scr[n_acc:])
        if nk == 1:
            o_ref[...] = _dot(a_ref[...], b_ref[...], dims).astype(o_ref.dtype)
        else:
            acc_ref, k = scr[0], pl.program_id(2)

            @pl.when(k == 0)
            def _():
                acc_ref[...] = jnp.zeros_like(acc_ref)

            acc_ref[...] += _dot(a_ref[...], b_ref[...], dims)

            @pl.when(k == nk - 1)
            def _():
                o_ref[...] = acc_ref[...].astype(o_ref.dtype)

        _comm_at(comm, 5, steps, cin, cout, scr[n_acc:])
        _comm_at(comm, 4, steps, cin, cout, scr[n_acc:])

    res = _pcall(
        body, name=name, grid=steps,
        in_specs=[a_spec, b_spec] + [_ANY] * len(c_in),
        out_specs=[pl.BlockSpec((tm, tn), lambda i, j, k: (i, j))] + [_ANY] * len(c_out),
        out_shape=[jax.ShapeDtypeStruct((M, N), out_dtype)] + list(c_out),
        scratch_shapes=([pltpu.VMEM((tm, tn), F32)] if nk > 1 else []) + list(c_scr),
        compiler_params=_cparams(("arbitrary",) * 3 if comm else ("parallel", "parallel", "arbitrary")),
    )(a, b, *c_in)
    return res if comm else res[0]


def _tile(tr, w, cb=0):
    return pl.BlockSpec((tr, w), lambda i: (i, cb))


def _const(shape):
    nd = len(shape)
    return pl.BlockSpec(shape, lambda i: (0,) * nd)


def _acc_store(i, ref, val):
    @pl.when(i == 0)
    def _():
        ref[...] = val

    @pl.when(i > 0)
    def _():
        ref[...] += val


def _rms_fwd(cfg, x2, g, comm=None):
    T, D, tr = cfg.T, cfg.D, cfg.tr
    steps = (T // tr,)

    def body(x_ref, g_ref, h_ref):
        h_ref[...] = _rmsn(x_ref[...], g_ref[...]).astype(BF16)

    body, c_in, c_out, c_shapes, c_scr = _hosted(body, 2, 1, steps, comm)
    res = _pcall(body, name="rms_fwd", grid=steps, in_specs=[_tile(tr, D), _const((1, D))] + c_in,
                 out_specs=[_tile(tr, D)] + c_out, out_shape=[jax.ShapeDtypeStruct((T, D), BF16)] + c_shapes,
                 scratch_shapes=c_scr, compiler_params=_cparams(("arbitrary",) if comm else ("parallel",)),
                 )(x2, g, *(comm[0] if comm else []))
    return res if comm else res[0]


def _rms_bwd(cfg, x2, g, dh, dres, comm=None):
    T, D, tr = cfg.T, cfg.D, cfg.tr
    c_in, c_out, c_scr = comm[:3] if comm else ([], [], [])
    steps = (T // tr,)

    def body(x_ref, g_ref, dh_ref, dres_ref, *rest):
        cin, (gx_ref, dg_ref) = rest[:len(c_in)], rest[len(c_in):len(c_in) + 2]
        cout, scr = rest[len(c_in) + 2:len(c_in) + 2 + len(c_out)], rest[len(c_in) + 2 + len(c_out):]
        _comm_at(comm, 3, steps, cin, cout, scr)
        _, vjp = jax.vjp(_rmsn, x_ref[...], g_ref[...])
        dx, dg = vjp(dh_ref[...])
        gx_ref[...] = dx + dres_ref[...]
        _acc_store(pl.program_id(0), dg_ref, dg)
        _comm_at(comm, 4, steps, cin, cout, scr)

    return _pcall(body, name="rms_bwd", grid=steps,
                  in_specs=[_tile(tr, D), _const((1, D)), _tile(tr, D), _tile(tr, D)] + [_ANY] * len(c_in),
                  out_specs=[_tile(tr, D), _const((1, D))] + [_ANY] * len(c_out),
                  out_shape=[jax.ShapeDtypeStruct((T, D), F32), jax.ShapeDtypeStruct((1, D), F32)] + list(c_out),
                  scratch_shapes=list(c_scr), compiler_params=_cparams(("arbitrary",)))(x2, g, dh, dres, *c_in)


def _final(cfg, x2, mo, fg, target):
    T, D, tr = cfg.T, cfg.D, cfg.tr

    def loss_fn(hres, g, tgt):
        err = _rmsn(hres, g) - tgt
        return 0.5 * jnp.sum(jnp.mean(err * err, axis=-1, keepdims=True), axis=0, keepdims=True)

    def body(x_ref, mo_ref, g_ref, t_ref, loss_ref, dres_ref, dres16_ref, dg_ref):
        hres = x_ref[...] + mo_ref[...]
        loss, vjp = jax.vjp(functools.partial(loss_fn, tgt=t_ref[...]), hres, g_ref[...])
        dres, dg = vjp(jnp.ones((1, 1), F32))
        dres_ref[...] = dres
        dres16_ref[...] = dres.astype(BF16)
        i = pl.program_id(0)
        _acc_store(i, dg_ref, dg)
        _acc_store(i, loss_ref, jnp.broadcast_to(loss, (8, LANES)))

    return _pcall(body, name="final_loss", grid=(T // tr,),
                  in_specs=[_tile(tr, D), _tile(tr, D), _const((1, D)), _tile(tr, D)],
                  out_specs=[_const((8, LANES)), _tile(tr, D), _tile(tr, D), _const((1, D))],
                  out_shape=[jax.ShapeDtypeStruct((8, LANES), F32), jax.ShapeDtypeStruct((T, D), F32),
                             jax.ShapeDtypeStruct((T, D), BF16), jax.ShapeDtypeStruct((1, D), F32)],
                  compiler_params=_cparams(("arbitrary",)))(x2, mo, fg, target)


def _merge_fn(pa, pb, ga, gb):
    return jax.nn.sigmoid(ga) * pa + jax.nn.sigmoid(gb) * pb


def _merge_fwd(cfg, pa, pb, u):
    T, D, tr = cfg.T, cfg.D, cfg.tr
    cga, cgb = cfg.o_gate // D, cfg.o_gate // D + 1

    def body(pa_ref, pb_ref, ga_ref, gb_ref, m_ref):
        m_ref[...] = _merge_fn(pa_ref[...], pb_ref[...], ga_ref[...], gb_ref[...]).astype(BF16)

    return _pcall(body, name="merge_fwd", grid=(T // tr,),
                  in_specs=[_tile(tr, D), _tile(tr, D), _tile(tr, D, cga), _tile(tr, D, cgb)],
                  out_specs=_tile(tr, D), out_shape=jax.ShapeDtypeStruct((T, D), BF16),
                  compiler_params=_cparams(("parallel",)))(pa, pb, u, u)


def _merge_bwd(cfg, pa, pb, u, dm):
    T, D, tr = cfg.T, cfg.D, cfg.tr
    cga, cgb = cfg.o_gate // D, cfg.o_gate // D + 1

    def body(pa_ref, pb_ref, ga_ref, gb_ref, dm_ref, dpa_ref, dpb_ref, dg_ref):
        _, vjp = jax.vjp(_merge_fn, pa_ref[...], pb_ref[...], ga_ref[...], gb_ref[...])
        dpa, dpb, dga, dgb = vjp(dm_ref[...])
        dpa_ref[...] = dpa.astype(BF16)
        dpb_ref[...] = dpb.astype(BF16)
        dg_ref[:, :D] = dga.astype(BF16)
        dg_ref[:, D:] = dgb.astype(BF16)

    return _pcall(body, name="merge_bwd", grid=(T // tr,),
                  in_specs=[_tile(tr, D), _tile(tr, D), _tile(tr, D, cga), _tile(tr, D, cgb), _tile(tr, D)],
                  out_specs=[_tile(tr, D), _tile(tr, D), _tile(tr, 2 * D, cfg.o_gate // (2 * D))],
                  out_shape=[jax.ShapeDtypeStruct((T, D), BF16), jax.ShapeDtypeStruct((T, D), BF16),
                             jax.ShapeDtypeStruct((T, cfg.ncol), BF16)],
                  compiler_params=_cparams(("parallel",)))(pa, pb, u, u, dm)


def _gate_fn(o, z):
    return o * _silu(z)


def _gate_a_fwd(cfg, o, u):
    T, FW, tr = cfg.T, cfg.FW, cfg.tr

    def body(o_ref, z_ref, oa_ref):
        oa_ref[...] = _gate_fn(o_ref[...], z_ref[...]).astype(BF16)

    return _pcall(body, name="gate_a_fwd", grid=(T // tr,), in_specs=[_tile(tr, FW), _tile(tr, FW, 3)],
                  out_specs=_tile(tr, FW), out_shape=jax.ShapeDtypeStruct((T, FW), BF16),
                  compiler_params=_cparams(("parallel",)))(o, u)


def _gate_a_bwd(cfg, o, u, doa, du):
    T, FW, tr = cfg.T, cfg.FW, cfg.tr

    def body(o_ref, z_ref, doa_ref, du_in, do_ref, dz_ref):
        _, vjp = jax.vjp(_gate_fn, o_ref[...], z_ref[...])
        do, dz = vjp(doa_ref[...])
        do_ref[...] = do
        dz_ref[...] = dz.astype(BF16)

    return _pcall(body, name="gate_a_bwd", grid=(T // tr,),
                  in_specs=[_tile(tr, FW), _tile(tr, FW, 3), _tile(tr, FW), _ANY],
                  out_specs=[_tile(tr, FW), _tile(tr, FW, 3)],
                  out_shape=[jax.ShapeDtypeStruct((T, FW), F32), jax.ShapeDtypeStruct(du.shape, BF16)],
                  input_output_aliases={3: 1},
                  compiler_params=_cparams(("parallel",)))(o, u, doa, du)


def _fox_prep(cfg, u, fb):
    T, tr = cfg.T, cfg.tr
    cf = cfg.o_f // LANES

    def body(f_ref, fb_ref, c_ref, carry_ref):
        i = pl.program_id(0)

        @pl.when(i == 0)
        def _():
            carry_ref[...] = jnp.zeros_like(carry_ref)

        lf = -_softplus(-(f_ref[...] + fb_ref[...]))
        r = lax.broadcasted_iota(jnp.int32, (tr, tr), 0)
        c = lax.broadcasted_iota(jnp.int32, (tr, tr), 1)
        tri = (r >= c).astype(F32)
        c_ref[...] = _dot(tri, lf, precision=HI) + carry_ref[...]
        carry_ref[...] += jnp.sum(lf, axis=0, keepdims=True)

    return _pcall(body, name="fox_prep", grid=(T // tr,), in_specs=[_tile(tr, LANES, cf), _const((1, LANES))],
                  out_specs=_tile(tr, LANES), out_shape=jax.ShapeDtypeStruct((T, LANES), F32),
                  scratch_shapes=[pltpu.VMEM((1, LANES), F32)], compiler_params=_cparams(("arbitrary",)))(u, fb)


def _fox_prep_bwd(cfg, u, fb, dc):
    T, tr = cfg.T, cfg.tr
    cf = cfg.o_f // LANES
    nb = T // tr

    def body(f_ref, fb_ref, dc_ref, df_ref, dfb_ref, carry_ref):
        i = pl.program_id(0)

        @pl.when(i == 0)
        def _():
            carry_ref[...] = jnp.zeros_like(carry_ref)

        dc = dc_ref[...]
        r = lax.broadcasted_iota(jnp.int32, (tr, tr), 0)
        c = lax.broadcasted_iota(jnp.int32, (tr, tr), 1)
        triu = (r <= c).astype(F32)
        dlf = _dot(triu, dc, precision=HI) + carry_ref[...]
        carry_ref[...] += jnp.sum(dc, axis=0, keepdims=True)
        dz = dlf * jax.nn.sigmoid(-(f_ref[...] + fb_ref[...]))
        df_ref[...] = dz.astype(BF16)
        _acc_store(i, dfb_ref, jnp.sum(dz, axis=0, keepdims=True))

    rev = lambda i: (nb - 1 - i, 0)
    return _pcall(body, name="fox_prep_bwd", grid=(nb,),
                  in_specs=[pl.BlockSpec((tr, LANES), lambda i: (nb - 1 - i, cf)), _const((1, LANES)),
                            pl.BlockSpec((tr, LANES), rev)],
                  out_specs=[pl.BlockSpec((tr, LANES), rev), _const((1, LANES))],
                  out_shape=[jax.ShapeDtypeStruct((T, LANES), BF16), jax.ShapeDtypeStruct((1, LANES), F32)],
                  scratch_shapes=[pltpu.VMEM((1, LANES), F32)], compiler_params=_cparams(("arbitrary",)))(u, fb, dc)


def _attn_logits(q_ref, k_ref, c_ref, tq, te):
    q = q_ref[...].astype(BF16)
    scale = FOX_HEAD_DIM ** -0.5
    part = lambda k0, k1: _dot(q, k_ref[k0:k1, :].astype(BF16), "nt") * scale - c_ref[0, :, k0:k1]
    row = lax.broadcasted_iota(jnp.int32, (tq, tq), 0)
    col = lax.broadcasted_iota(jnp.int32, (tq, tq), 1)
    own = ((te - tq, te), jnp.where(col <= row, part(te - tq, te), -1e30))
    return [((0, te - tq), part(0, te - tq)), own] if te > tq else [own]


def _per_query_tile(i, nq, tq, fn):
    for ii in range(nq):
        pl.when(i == ii)(functools.partial(fn, (ii + 1) * tq))


def _attn_fwd(cfg, u, c_rows):
    T, FW, FH = cfg.T, cfg.FW, cfg.FH
    tq = min(256, T)
    dh = FOX_HEAD_DIM

    def body(q_ref, k_ref, v_ref, c_ref, o_ref, lse_ref):
        i = pl.program_id(1)

        def tile(te):
            parts = _attn_logits(q_ref, k_ref, c_ref, tq, te)
            m = functools.reduce(jnp.maximum, [jnp.max(s, axis=1, keepdims=True) for _, s in parts])
            l, acc = 0.0, 0.0
            for (k0, k1), s in parts:
                p = jnp.exp(s - m)
                l = l + jnp.sum(p, axis=1, keepdims=True)
                acc = acc + _dot(p.astype(BF16), v_ref[k0:k1, :].astype(BF16))
            o_ref[...] = acc / l
            lse_ref[0] = m + jnp.log(l)

        _per_query_tile(i, T // tq, tq, tile)

    return _pcall(
        body, name="fox_attn_fwd", grid=(FH, T // tq),
        in_specs=[pl.BlockSpec((tq, dh), lambda h, i: (i, 3 * h)), pl.BlockSpec((T, dh), lambda h, i: (0, 3 * h + 1)),
                  pl.BlockSpec((T, dh), lambda h, i: (0, 3 * h + 2)), pl.BlockSpec((1, 1, T), lambda h, i: (h, 0, 0))],
        out_specs=[pl.BlockSpec((tq, dh), lambda h, i: (i, h)), pl.BlockSpec((1, tq, 1), lambda h, i: (h, i, 0))],
        out_shape=[jax.ShapeDtypeStruct((T, FW), F32), jax.ShapeDtypeStruct((FH, T, 1), F32)],
        compiler_params=_cparams(("parallel", "arbitrary")),
    )(u, u, u, c_rows)


def _attn_bwd(cfg, u, c_rows, lse, do, du):
    T, FW, FH = cfg.T, cfg.FW, cfg.FH
    tq = min(256, T)
    nq = T // tq
    dh = FOX_HEAD_DIM
    scale = dh ** -0.5

    def body(q_ref, k_ref, v_ref, c_ref, lse_ref, do_ref, du_in, du_ref, dcol_ref, dk_acc, dv_acc):
        i = pl.program_id(1)

        @pl.when(i == 0)
        def _():
            dk_acc[...] = jnp.zeros_like(dk_acc)
            dv_acc[...] = jnp.zeros_like(dv_acc)
            dcol_ref[...] = jnp.zeros_like(dcol_ref)

        def tile(te):
            lse, q16, do16 = lse_ref[0], q_ref[...].astype(BF16), do_ref[...].astype(BF16)
            probs = [(ks, jnp.exp(s - lse)) for ks, s in _attn_logits(q_ref, k_ref, c_ref, tq, te)]
            dps = [_dot(do16, v_ref[k0:k1, :].astype(BF16), "nt") for (k0, k1), _ in probs]
            delta = sum(jnp.sum(p * dp, axis=1, keepdims=True) for (_, p), dp in zip(probs, dps))
            dq = 0.0
            for ((k0, k1), p), dp in zip(probs, dps):
                ds = p * (dp - delta)
                ds16 = ds.astype(BF16)
                dq = dq + _dot(ds16, k_ref[k0:k1, :].astype(BF16))
                dk_acc[k0:k1, :] += _dot(ds16, q16, "tn") * scale
                dv_acc[k0:k1, :] += _dot(p.astype(BF16), do16, "tn")
                dcol_ref[0, :, k0:k1] += jnp.sum(ds, axis=0, keepdims=True)
            du_ref[te - tq:te, 0:dh] = (dq * scale).astype(BF16)

        _per_query_tile(i, nq, tq, tile)

        @pl.when(i == nq - 1)
        def _():
            du_ref[:, dh:2 * dh] = dk_acc[...].astype(BF16)
            du_ref[:, 2 * dh:3 * dh] = dv_acc[...].astype(BF16)

    return _pcall(
        body, name="fox_attn_bwd", grid=(FH, nq),
        in_specs=[pl.BlockSpec((tq, dh), lambda h, i: (i, 3 * h)), pl.BlockSpec((T, dh), lambda h, i: (0, 3 * h + 1)),
                  pl.BlockSpec((T, dh), lambda h, i: (0, 3 * h + 2)), pl.BlockSpec((1, 1, T), lambda h, i: (h, 0, 0)),
                  pl.BlockSpec((1, tq, 1), lambda h, i: (h, i, 0)), pl.BlockSpec((tq, dh), lambda h, i: (i, h)), _ANY],
        out_specs=[pl.BlockSpec((T, 3 * dh), lambda h, i: (0, h)), pl.BlockSpec((1, 1, T), lambda h, i: (h, 0, 0))],
        out_shape=[jax.ShapeDtypeStruct(du.shape, BF16), jax.ShapeDtypeStruct((FH, 1, T), F32)],
        scratch_shapes=[pltpu.VMEM((T, dh), F32), pltpu.VMEM((T, dh), F32)],
        input_output_aliases={6: 0},
        compiler_params=_cparams(("parallel", "arbitrary")),
    )(u, u, u, c_rows, lse, do, du)


def _head_indicators(cfg):
    ind = np.zeros((cfg.RW, LANES), np.float32)
    ind[np.arange(cfg.RW), np.arange(cfg.RW) // RWKV_HEAD_DIM] = 1.0
    pad = np.zeros((1, LANES), np.float32)
    pad[0, cfg.RH:] = 1.0
    return jnp.asarray(ind), jnp.asarray(ind.T.copy()), jnp.asarray(pad)


def _prep_fn(us_r, us_k, us_v, us_wd, us_ad, w0, w2p, a0, a2p, k_k, k_a, ind, ind_t, pad):
    wpre = w0 + _dot3(jnp.tanh(us_wd), w2p)
    w = -_softplus(-wpre) - 0.5
    lw = -jnp.exp(w)
    a = jax.nn.sigmoid(a0 + _dot3(us_ad, a2p))
    kk = us_k * k_k
    ss = _xdot(kk * kk, ind, ind_t) + pad
    inv = 1.0 / jnp.maximum(jnp.sqrt(ss), L2_EPS)
    kkn = kk * _xdot(inv, ind_t, ind)
    kp = us_k * (1.0 + (a - 1.0) * k_a)
    return us_r, lw, kp, us_v, -kkn, kkn * a


def _shifted(u, prev_row, mu, first):
    n = u.shape[0]
    rolled = pltpu.roll(u, 1, 0)
    row = lax.broadcasted_iota(jnp.int32, u.shape, 0)
    p0 = jnp.where(first, jnp.zeros_like(prev_row), prev_row)
    prev = jnp.where(row == 0, jnp.broadcast_to(p0, u.shape), rolled)
    return u + (prev - u) * mu, prev


def _rwkv_specs(cfg, tr):
    RW, LP = cfg.RW, cfg.LP
    base = cfg.o_rwkv // RW
    cols = [(RW, base), (RW, base + 1), (RW, base + 2), (RW, base + 3), (LP, cfg.o_wd // LP), (LP, cfg.o_ad // LP)]
    cur = [pl.BlockSpec((tr, w), (lambda i, cb=cb: (i, cb))) for w, cb in cols]
    prv = [pl.BlockSpec((8, w), (lambda i, cb=cb: (jnp.maximum(i * (tr // 8) - 1, 0), cb))) for w, cb in cols]
    return cols, cur, prv


def _mu_pieces(cfg, mu_ref):
    RW, LP = cfg.RW, cfg.LP
    offs = [0, RW, 2 * RW, 3 * RW, 4 * RW, 4 * RW + LP, 4 * RW + 2 * LP]
    return [mu_ref[:, offs[j]:offs[j + 1]] for j in range(6)]


def _rwkv_prep_fwd(cfg, u, mu, w0, w2p, a0, a2p, k_k, k_a):
    T, RW, LP, tr = cfg.T, cfg.RW, cfg.LP, cfg.tr
    ind, ind_t, pad = _head_indicators(cfg)
    cols, cur, prv = _rwkv_specs(cfg, tr)

    def body(*refs):
        u_refs, p_refs = refs[0:6], refs[6:12]
        mu_ref, w0_ref, w2_ref, a0_ref, a2_ref, kk_ref, ka_ref, ind_ref, indt_ref, pad_ref = refs[12:22]
        outs = refs[22:]
        first = pl.program_id(0) == 0
        mus = _mu_pieces(cfg, mu_ref)
        us = [_shifted(u_refs[j][...], p_refs[j][7:8, :], mus[j], first)[0] for j in range(6)]
        res = _prep_fn(us[0], us[1], us[2], us[4], us[5], w0_ref[...], w2_ref[...], a0_ref[...], a2_ref[...],
                       kk_ref[...], ka_ref[...], ind_ref[...], indt_ref[...], pad_ref[...])
        for j in range(6):
            outs[j][...] = res[j]
        outs[6][...] = us[3]

    consts = [mu, w0, w2p, a0, a2p, k_k, k_a, ind, ind_t, pad]
    return _pcall(body, name="rwkv_prep_fwd", grid=(T // tr,),
                  in_specs=cur + prv + [_const(c.shape) for c in consts],
                  out_specs=[_tile(tr, RW)] * 7, out_shape=[jax.ShapeDtypeStruct((T, RW), F32)] * 7,
                  compiler_params=_cparams(("parallel",)))(*([u] * 12), *consts)


def _rwkv_prep_bwd(cfg, u, mu, w0, w2p, a0, a2p, k_k, k_a, cots, dzb):
    T, RW, LP = cfg.T, cfg.RW, cfg.LP
    tr = min(128, T)
    ind, ind_t, pad = _head_indicators(cfg)
    cols, cur, prv = _rwkv_specs(cfg, tr)
    rseg = cfg.rseg

    def body(*refs):
        u_refs, p_refs = refs[0:6], refs[6:12]
        mu_ref, w0_ref, w2_ref, a0_ref, a2_ref, kk_ref, ka_ref, ind_ref, indt_ref, pad_ref = refs[12:22]
        cot_refs, dzb_ref = refs[22:28], refs[28]
        dus_ref, dmu_ref, dw0_ref, dw2_ref, da0_ref, da2_ref, dkk_ref, dka_ref = refs[29:]
        i = pl.program_id(0)
        first = i == 0
        mus = _mu_pieces(cfg, mu_ref)
        sh = [_shifted(u_refs[j][...], p_refs[j][7:8, :], mus[j], first) for j in range(6)]
        us = [s[0] for s in sh]
        fn = functools.partial(_prep_fn, ind=ind_ref[...], ind_t=indt_ref[...], pad=pad_ref[...])
        _, vjp = jax.vjp(fn, us[0], us[1], us[2], us[4], us[5], w0_ref[...], w2_ref[...], a0_ref[...], a2_ref[...],
                         kk_ref[...], ka_ref[...])
        d = vjp(tuple(c[...] for c in cot_refs))
        dus = [d[0], d[1], d[2], dzb_ref[...], d[3], d[4]]
        offs = [0, RW, 2 * RW, 3 * RW, 4 * RW, 4 * RW + LP, 4 * RW + 2 * LP]
        for j in range(6):
            dus_ref[:, offs[j]:offs[j + 1]] = dus[j]
            dmu_j = jnp.sum(dus[j] * (sh[j][1] - u_refs[j][...]), axis=0, keepdims=True)

            @pl.when(first)
            def _(j=j, dmu_j=dmu_j):
                dmu_ref[:, offs[j]:offs[j + 1]] = dmu_j

            @pl.when(i > 0)
            def _(j=j, dmu_j=dmu_j):
                dmu_ref[:, offs[j]:offs[j + 1]] += dmu_j
        for ref, val in zip((dw0_ref, dw2_ref, da0_ref, da2_ref, dkk_ref, dka_ref), d[5:11]):
            _acc_store(i, ref, val)

    consts = [mu, w0, w2p, a0, a2p, k_k, k_a, ind, ind_t, pad]
    vec = jax.ShapeDtypeStruct((1, RW), F32)
    mat = jax.ShapeDtypeStruct((LP, RW), F32)
    return _pcall(body, name="rwkv_prep_bwd", grid=(T // tr,),
                  in_specs=cur + prv + [_const(c.shape) for c in consts] + [_tile(tr, RW)] * 7,
                  out_specs=[_tile(tr, rseg), _const((1, rseg)), _const((1, RW)), _const((LP, RW)), _const((1, RW)),
                             _const((LP, RW)), _const((1, RW)), _const((1, RW))],
                  out_shape=[jax.ShapeDtypeStruct((T, rseg), F32), jax.ShapeDtypeStruct((1, rseg), F32),
                             vec, mat, vec, mat, vec, vec],
                  compiler_params=_cparams(("arbitrary",)))(*([u] * 12), *consts, *cots, dzb)


def _shift_bwd(cfg, dus, mu, df, du):
    T, tr, RW, LP = cfg.T, cfg.tr, cfg.RW, cfg.LP
    nb = T // tr
    tail = cfg.ncol - cfg.o_f
    assert cfg.o_rwkv % (4 * RW) == 0 and (4 * RW) % (2 * LP) == 0 and cfg.o_f % tail == 0

    def shifted(d_ref, n_ref, mu_ref):
        d = d_ref[...]
        rolled = pltpu.roll(d, tr - 1, 0)
        row = lax.broadcasted_iota(jnp.int32, d.shape, 0)
        n0 = jnp.where(pl.program_id(0) == nb - 1, jnp.zeros_like(n_ref[0:1, :]), n_ref[0:1, :])
        nxt = jnp.where(row == tr - 1, jnp.broadcast_to(n0, d.shape), rolled)
        mu_v = mu_ref[...]
        return (d * (1.0 - mu_v) + nxt * mu_v).astype(BF16)

    def main_body(d_ref, n_ref, mu_ref, du_in, du_ref):
        du_ref[...] = shifted(d_ref, n_ref, mu_ref)

    def tail_body(d_ref, n_ref, mu_ref, df_ref, du_in, du_ref):
        du_ref[:, 0:LANES] = df_ref[...]
        du_ref[:, LANES:LANES + 2 * LP] = shifted(d_ref, n_ref, mu_ref)
        if tail > LANES + 2 * LP:
            du_ref[:, LANES + 2 * LP:] = jnp.zeros((tr, tail - LANES - 2 * LP), BF16)

    def specs(w, cb):
        return [_tile(tr, w, cb),
                pl.BlockSpec((8, w), lambda i: (jnp.minimum((i + 1) * (tr // 8), T // 8 - 1), cb)),
                pl.BlockSpec((1, w), lambda i: (0, cb))]

    out = jax.ShapeDtypeStruct(du.shape, BF16)
    du = _pcall(main_body, name="shift_bwd_main", grid=(nb,), in_specs=specs(4 * RW, 0) + [_ANY],
                out_specs=_tile(tr, 4 * RW, cfg.o_rwkv // (4 * RW)), out_shape=out, input_output_aliases={3: 0},
                compiler_params=_cparams(("parallel",)))(dus, dus, mu, du)
    return _pcall(tail_body, name="shift_bwd_tail", grid=(nb,),
                  in_specs=specs(2 * LP, 4 * RW // (2 * LP)) + [_tile(tr, LANES), _ANY],
                  out_specs=_tile(tr, tail, cfg.o_f // tail), out_shape=out, input_output_aliases={4: 0},
                  compiler_params=_cparams(("parallel",)))(dus, dus, mu, df, du)


def _chunk_local(r, lw, k, v, a, b):
    H, C, K = r.shape
    row = lax.broadcasted_iota(jnp.int32, (C, C), 0)
    col = lax.broadcasted_iota(jnp.int32, (C, C), 1)
    incl = jnp.broadcast_to((row >= col).astype(F32)[None], (H, C, C))
    strict = (row > col)[None]
    lower = (row >= col)[None]
    eye = (row == col)[None]
    zero = jnp.zeros((), F32)
    L = _bdot(incl, lw, 2, 1)
    LC = jnp.sum(lw, axis=1, keepdims=True)
    eL = jnp.exp(L)
    eLn = jnp.exp(-L)
    at = a * jnp.exp(L - lw)
    rt = r * eL
    bt = b * eLn
    kt = k * eLn
    eR = jnp.exp(LC - L)
    bh = b * eR
    kh = k * eR
    keys = functools.partial(_bdot, passes=SCAN_PASSES[0])
    inv = functools.partial(_bdot, passes=SCAN_PASSES[1])
    app = functools.partial(_bdot, passes=SCAN_PASSES[2])
    ar = jnp.concatenate([at, rt], axis=1)
    g_b = app(ar, bt, 2, 2)
    g_k = keys(ar, kt, 2, 2)
    n_ab = jnp.where(strict, g_b[:, :C], zero)
    n_ak = jnp.where(strict, g_k[:, :C], zero)
    m_rb = jnp.where(lower, g_b[:, C:], zero)
    m_rk = jnp.where(lower, g_k[:, C:], zero)
    M = n_ab
    P = jnp.where(eye, 1.0, zero) + n_ab
    for _ in range(1, max(1, int(np.ceil(np.log2(C))))):
        M = inv(M, M, 2, 1)
        P = P + inv(M, P, 2, 1)
    W = app(P, at, 2, 1)
    Uloc = app(P, app(n_ak, v, 2, 1), 2, 1)
    Q = rt + app(m_rb, W, 2, 1)
    Yloc = app(m_rb, Uloc, 2, 1) + app(m_rk, v, 2, 1)
    A = jnp.where(eye, jnp.exp(LC), zero) + app(W, bh, 1, 1)
    Sloc = app(Uloc, bh, 1, 1) + app(v, kh, 1, 1)
    return Q, Yloc, A, Sloc


def _split_heads(ref, n):
    N = RWKV_HEAD_DIM
    return jnp.stack([ref[:, h * N:(h + 1) * N] for h in range(n)], axis=0)


def _merge_heads(x):
    return jnp.concatenate([x[h] for h in range(x.shape[0])], axis=1)


def _scan_local_specs(cfg):
    N, HB = RWKV_HEAD_DIM, cfg.hb
    grid = (cfg.RH // HB, cfg.T // cfg.C)
    seq = pl.BlockSpec((HB, cfg.C, N), lambda h, j: (h, j, 0))
    mat = pl.BlockSpec((HB, 1, N, N), lambda h, j: (h, j, 0, 0))
    return grid, seq, mat


def _scan_local_fwd(cfg, seqs):
    T, RH, N = cfg.T, cfg.RH, RWKV_HEAD_DIM
    grid, seq, mat = _scan_local_specs(cfg)

    def body(r_ref, lw_ref, k_ref, v_ref, a_ref, b_ref, q_ref, yl_ref, a_out, sl_ref):
        Q, Yloc, A, Sloc = _chunk_local(*[_split_heads(ref, cfg.hb) for ref in (r_ref, lw_ref, k_ref, v_ref, a_ref, b_ref)])
        q_ref[...] = Q
        yl_ref[...] = Yloc
        a_out[:, 0] = A
        sl_ref[:, 0] = Sloc

    tok = pl.BlockSpec((cfg.C, cfg.hb * N), lambda h, j: (j, h))
    sq = jax.ShapeDtypeStruct((RH, T, N), F32)
    mt = jax.ShapeDtypeStruct((RH, T // cfg.C, N, N), F32)
    return _pcall(body, name="rwkv_scan_local_fwd", grid=grid, in_specs=[tok] * 6, out_specs=[seq, seq, mat, mat],
                  out_shape=[sq, sq, mt, mt], compiler_params=_cparams(("parallel", "parallel")))(*seqs)


def _scan_local_bwd(cfg, toks, dq, dy, da, dsl, extra, comm=None):
    T, RW, N = cfg.T, cfg.RW, RWKV_HEAD_DIM
    grid, seq, mat = _scan_local_specs(cfg)
    c_in, c_out, c_scr = comm[:3] if comm else ([], [], [])

    def body(r_ref, lw_ref, k_ref, v_ref, a_ref, b_ref, dq_ref, dy_ref, da_ref, dsl_ref, xr_ref, xk_ref, xv_ref,
             *rest):
        cin, outs = rest[:len(c_in)], rest[len(c_in):len(c_in) + 6]
        cout, scr = rest[len(c_in) + 6:len(c_in) + 6 + len(c_out)], rest[len(c_in) + 6 + len(c_out):]
        _comm_at(comm, 3, grid, cin, cout, scr)
        ins = [_split_heads(ref, cfg.hb) for ref in (r_ref, lw_ref, k_ref, v_ref, a_ref, b_ref)]
        _, vjp = jax.vjp(_chunk_local, *ins)
        d = vjp((dq_ref[...], _split_heads(dy_ref, cfg.hb), da_ref[:, 0], dsl_ref[:, 0]))
        add = {0: xr_ref, 2: xk_ref, 3: xv_ref}
        for j in range(6):
            dj = _merge_heads(d[j])
            outs[j][...] = dj + add[j][...] if j in add else dj
        _comm_at(comm, 4, grid, cin, cout, scr)

    tok = pl.BlockSpec((cfg.C, cfg.hb * N), lambda h, j: (j, h))
    return _pcall(body, name="rwkv_scan_local_bwd", grid=grid,
                  in_specs=[tok] * 6 + [seq, tok, mat, mat] + [tok] * 3 + [_ANY] * len(c_in),
                  out_specs=[tok] * 6 + [_ANY] * len(c_out),
                  out_shape=[jax.ShapeDtypeStruct((T, RW), F32)] * 6 + list(c_out), scratch_shapes=list(c_scr),
                  compiler_params=_cparams(("arbitrary", "arbitrary") if comm else ("parallel", "parallel")),
                  )(*toks, dq, dy, da, dsl, *extra, *c_in)


def _scan_carry_specs(cfg, rev):
    N, RH, C, nc = RWKV_HEAD_DIM, cfg.RH, cfg.C, cfg.T // cfg.C
    at = (lambda j: nc - 1 - j) if rev else (lambda j: j)
    seq = pl.BlockSpec((RH, C, N), lambda j: (0, at(j), 0))
    mat = pl.BlockSpec((RH, 1, N, N), lambda j: (0, at(j), 0, 0))
    return nc, seq, mat


def _scan_carry_fwd(cfg, q, yloc, a, sloc):
    T, RH, N = cfg.T, cfg.RH, RWKV_HEAD_DIM
    nc, seq, mat = _scan_carry_specs(cfg, False)

    def body(q_ref, yl_ref, a_ref, sl_ref, y_ref, ck_ref, s_ref):
        @pl.when(pl.program_id(0) == 0)
        def _():
            s_ref[...] = jnp.zeros_like(s_ref)

        S = s_ref[...]
        ck_ref[:, 0] = S
        y_ref[...] = _merge_heads(_bdot(q_ref[...], S, 2, 2, SCAN_PASSES[2]) + yl_ref[...])
        s_ref[...] = _bdot(S, a_ref[:, 0], 2, 1) + sl_ref[:, 0]

    tok = pl.BlockSpec((cfg.C, cfg.RW), lambda j: (j, 0))
    return _pcall(body, name="rwkv_scan_carry_fwd", grid=(nc,), in_specs=[seq, seq, mat, mat], out_specs=[tok, mat],
                  out_shape=[jax.ShapeDtypeStruct((T, cfg.RW), F32), jax.ShapeDtypeStruct((RH, nc, N, N), F32)],
                  scratch_shapes=[pltpu.VMEM((RH, N, N), F32)],
                  compiler_params=_cparams(("arbitrary",)))(q, yloc, a, sloc)


def _scan_carry_bwd(cfg, q, a, ckpt, dy, comm=None):
    T, RH, N = cfg.T, cfg.RH, RWKV_HEAD_DIM
    nc, seq, mat = _scan_carry_specs(cfg, True)

    def body(q_ref, a_ref, ck_ref, dy_ref, dq_ref, da_ref, dsl_ref, ds_ref):
        @pl.when(pl.program_id(0) == 0)
        def _():
            ds_ref[...] = jnp.zeros_like(ds_ref)

        S, dS, dY = ck_ref[:, 0], ds_ref[...], _split_heads(dy_ref, RH)
        dq_ref[...] = _bdot(dY, S, 2, 1, SCAN_PASSES[2])
        da_ref[:, 0] = _bdot(S, dS, 1, 1, SCAN_PASSES[2])
        dsl_ref[:, 0] = dS
        ds_ref[...] = _bdot(dS, a_ref[:, 0], 2, 2) + _bdot(dY, q_ref[...], 1, 1, SCAN_PASSES[2])

    mt = jax.ShapeDtypeStruct((RH, nc, N, N), F32)
    tok = pl.BlockSpec((cfg.C, cfg.RW), lambda j: (nc - 1 - j, 0))
    body, c_in, c_out, c_shapes, c_scr = _hosted(body, 4, 3, (nc,), comm)
    return _pcall(body, name="rwkv_scan_carry_bwd", grid=(nc,), in_specs=[seq, mat, mat, tok] + c_in,
                  out_specs=[seq, mat, mat] + c_out,
                  out_shape=[jax.ShapeDtypeStruct((RH, T, N), F32), mt, mt] + c_shapes,
                  scratch_shapes=c_scr + [pltpu.VMEM((RH, N, N), F32)],
                  compiler_params=_cparams(("arbitrary",)))(q, a, ckpt, dy, *(comm[0] if comm else []))


def _post_fn(y, r, kp, v, zb, ln_w, ln_b, rk, ind, ind_t):
    n = float(RWKV_HEAD_DIM)
    mu = _xdot(_xdot(y, ind, ind_t) / n, ind_t, ind)
    yc = y - mu
    var = _xdot(yc * yc, ind, ind_t) / n
    rstd = _xdot(lax.rsqrt(var + GN_EPS), ind_t, ind)
    yn = yc * rstd * ln_w + ln_b
    bonus = _xdot(_xdot(r * kp * rk, ind, ind_t), ind_t, ind) * v
    return (yn + bonus) * _silu(zb)


def _rwkv_post_fwd(cfg, y, r, kp, v, zb, ln_w, ln_b, rk):
    T, RW, tr = cfg.T, cfg.RW, cfg.tr
    ind, ind_t, _ = _head_indicators(cfg)

    def body(y_ref, r_ref, k_ref, v_ref, z_ref, lw_ref, lb_ref, rk_ref, ind_ref, indt_ref, ob_ref):
        ob_ref[...] = _post_fn(y_ref[...], r_ref[...], k_ref[...], v_ref[...], z_ref[...], lw_ref[...], lb_ref[...],
                               rk_ref[...], ind_ref[...], indt_ref[...]).astype(BF16)

    consts = [ln_w, ln_b, rk, ind, ind_t]
    return _pcall(body, name="rwkv_post_fwd", grid=(T // tr,),
                  in_specs=[_tile(tr, RW)] * 5 + [_const(c.shape) for c in consts],
                  out_specs=_tile(tr, RW), out_shape=jax.ShapeDtypeStruct((T, RW), BF16),
                  compiler_params=_cparams(("parallel",)))(y, r, kp, v, zb, *consts)


def _rwkv_post_bwd(cfg, y, r, kp, v, zb, ln_w, ln_b, rk, dob):
    T, RW = cfg.T, cfg.RW
    tr = min(128, T)
    ind, ind_t, _ = _head_indicators(cfg)

    def body(y_ref, r_ref, k_ref, v_ref, z_ref, lw_ref, lb_ref, rk_ref, ind_ref, indt_ref, dob_ref,
             dy_ref, dr_ref, dk_ref, dv_ref, dz_ref, dlw_ref, dlb_ref, drk_ref):
        fn = functools.partial(_post_fn, ind=ind_ref[...], ind_t=indt_ref[...])
        _, vjp = jax.vjp(fn, y_ref[...], r_ref[...], k_ref[...], v_ref[...], z_ref[...], lw_ref[...], lb_ref[...],
                         rk_ref[...])
        d = vjp(dob_ref[...])
        for ref, val in zip((dy_ref, dr_ref, dk_ref, dv_ref, dz_ref), d[:5]):
            ref[...] = val
        i = pl.program_id(0)
        for ref, val in zip((dlw_ref, dlb_ref, drk_ref), d[5:8]):
            _acc_store(i, ref, val)

    consts = [ln_w, ln_b, rk, ind, ind_t]
    vec = jax.ShapeDtypeStruct((1, RW), F32)
    return _pcall(body, name="rwkv_post_bwd", grid=(T // tr,),
                  in_specs=[_tile(tr, RW)] * 5 + [_const(c.shape) for c in consts] + [_tile(tr, RW)],
                  out_specs=[_tile(tr, RW)] * 5 + [_const((1, RW))] * 3,
                  out_shape=[jax.ShapeDtypeStruct((T, RW), F32)] * 5 + [vec] * 3,
                  compiler_params=_cparams(("arbitrary",)))(y, r, kp, v, zb, *consts, dob)


def _adamw_math(w, g, m, v):
    m = ADAM_B1 * m + (1.0 - ADAM_B1) * g
    v = ADAM_B2 * v + (1.0 - ADAM_B2) * (g * g)
    m_hat = m / (1.0 - ADAM_B1 ** ADAM_STEP)
    v_hat = v / (1.0 - ADAM_B2 ** ADAM_STEP)
    delta = -ADAM_LR * (m_hat / (jnp.sqrt(v_hat) + ADAM_EPS) + ADAM_WD * w)
    return delta, m, v


def _adamw(name, w, g, m, v, copy_grad=False, comm=None):
    R, Cc = w.shape
    Rp = -(-R // 8) * 8
    tr = Rp
    for nb in range(1, Rp // 8 + 1):
        if (Rp // 8) % nb == 0 and (Rp // nb) * Cc * 4 <= 2 * 1024 * 1024:
            tr = Rp // nb
            break

    def body(w_ref, g_ref, m_ref, v_ref, d_ref, nm_ref, nv_ref, *g_out):
        g_v = g_ref[...]
        d, nm, nv = _adamw_math(w_ref[...], g_v, m_ref[...], v_ref[...])
        d_ref[...] = d
        nm_ref[...] = nm
        nv_ref[...] = nv
        if copy_grad:
            g_out[0][...] = g_v

    spec = _tile(tr, Cc)
    n_out = 4 if copy_grad else 3
    body, c_in, c_out, c_shapes, c_scr = _hosted(body, 4, n_out, (Rp // tr,), comm)
    return _pcall(body, name=name, grid=(Rp // tr,), in_specs=[spec] * 4 + c_in, out_specs=[spec] * n_out + c_out,
                  out_shape=[jax.ShapeDtypeStruct((R, Cc), F32)] * n_out + c_shapes, scratch_shapes=c_scr,
                  compiler_params=_cparams(("arbitrary",) if comm else ("parallel",)),
                  )(w, g, m, v, *(comm[0] if comm else []))


def _row_tile(R, Cc, itemsize, budget=2 * 1024 * 1024):
    for nb in range(1, R // 16 + 1):
        if R % nb == 0 and (R // nb) % 16 == 0 and (R // nb) * Cc * itemsize <= budget:
            return R // nb
    return R


def _add_halves(name, gs, r1, c_idx):
    S, R, Cc = gs.shape
    half = R // 2
    tr = _row_tile(half, Cc, 4)
    nb = half // tr

    def body(c_ref, g_ref, r_ref, o_ref):
        o_ref[...] = (g_ref[...].astype(F32) + r_ref[...].astype(F32)).astype(BF16)

    grid_spec = pltpu.PrefetchScalarGridSpec(
        num_scalar_prefetch=1, grid=(S, nb),
        in_specs=[pl.BlockSpec((1, tr, Cc), lambda s, i, c: (s, c[0] * nb + i, 0)),
                  pl.BlockSpec((1, tr, Cc), lambda s, i, c: (s, i, 0))],
        out_specs=pl.BlockSpec((1, tr, Cc), lambda s, i, c: (s, i, 0)))
    return _pcall(body, name=name, grid_spec=grid_spec, out_shape=jax.ShapeDtypeStruct((S, half, Cc), BF16),
                  compiler_params=_cparams(("parallel", "parallel")))(c_idx, gs, r1)


def _sum_slots(name, r2):
    S, R, Cc = r2.shape
    tr = _row_tile(R, Cc, 4 * S // 2 if r2.dtype == BF16 else 4 * S)

    def body(r_ref, o_ref):
        acc = r_ref[0].astype(F32)
        for s in range(1, S):
            acc = acc + r_ref[s].astype(F32)
        o_ref[...] = acc

    return _pcall(body, name=name, grid=(R // tr,), in_specs=[pl.BlockSpec((S, tr, Cc), lambda i: (0, i, 0))],
                  out_specs=_tile(tr, Cc), out_shape=jax.ShapeDtypeStruct((R, Cc), F32),
                  compiler_params=_cparams(("parallel",)))(r2)


def _sum_chips(name, recv, own, place):
    S, H, Cc = recv.shape
    tr = _row_tile(H, Cc, 4, 1024 * 1024)
    nb = H // tr

    def body(p_ref, r_ref, own_ref, o_ref):
        s = pl.program_id(1)
        me = p_ref[0]

        @pl.when(s == 0)
        def _():
            o_ref[...] = jnp.zeros_like(o_ref)

        @pl.when(s == me)
        def _():
            o_ref[...] += own_ref[0].astype(F32)

        @pl.when(s != me)
        def _():
            o_ref[...] += r_ref[0].astype(F32)

    grid_spec = pltpu.PrefetchScalarGridSpec(
        num_scalar_prefetch=1, grid=(nb, S),
        in_specs=[pl.BlockSpec((1, tr, Cc), lambda i, s, p: (jnp.where(s == p[0], (s + 1) % S, s), i, 0)),
                  pl.BlockSpec((1, tr, Cc), lambda i, s, p: (p[0], i, 0))],
        out_specs=pl.BlockSpec((tr, Cc), lambda i, s, p: (p[1] * nb + i, 0)))
    return _pcall(body, name=name, grid_spec=grid_spec, out_shape=jax.ShapeDtypeStruct((2 * H, Cc), F32),
                  compiler_params=_cparams(("parallel", "arbitrary")))(place, recv, own)


def _cast_bf16(name, w):
    R, Cc = w.shape
    tr = _row_tile(R, Cc, 4)

    def body(w_ref, o_ref):
        o_ref[...] = w_ref[...].astype(BF16)

    return _pcall(body, name=name, grid=(R // tr,), in_specs=[_tile(tr, Cc)], out_specs=_tile(tr, Cc),
                  out_shape=jax.ShapeDtypeStruct((R, Cc), BF16), compiler_params=_cparams(("parallel",)))(w)


_ANY = pl.BlockSpec(memory_space=pl.ANY)


def _place():
    x, y, c = lax.axis_index("x"), lax.axis_index("y"), lax.axis_index("c")
    others = [(1 - x, y), (x, 1 - y), (1 - x, 1 - y)]
    return x, y, c, others


def _gather_parts(shards):
    n = len(shards)
    halves = [s.shape[0] // 2 for s in shards]

    def parts(ins, outs, sems):
        x, y, c, _ = _place()
        me = 2 * x + y
        n1 = (x ^ (1 - c), y ^ c)
        n2 = (x ^ c, y ^ (1 - c))
        s1, s2, sd = 2 * n1[0] + n1[1], 2 * n2[0] + n2[1], 2 * (1 - x) + (1 - y)
        sib = (x, y, 1 - c)

        def rows(k, chip, hc):
            return outs[k].at[chip, pl.ds(hc * halves[k], halves[k]), :]

        def remote(k, j, src, dst, to):
            return pltpu.make_async_remote_copy(src_ref=src, dst_ref=dst, send_sem=sems[0].at[7 * k + j],
                                                recv_sem=sems[1].at[7 * k + j], device_id=to, device_id_type=MESH)

        def copy(k, j):
            if j == 6:
                return remote(k, j, ins[k], outs[k].at[me], sib)
            if j < 2:
                mine = ins[k].at[pl.ds(c * halves[k], halves[k]), :]
                return remote(k, j, mine, rows(k, me, c), (*(n1 if j == 0 else n2), c))
            land = rows(k, {2: s1, 3: s1, 4: s2, 5: sd}[j], c)
            return remote(k, j, land, land, (*n2, c) if j == 2 else sib)

        def arrived(k, j):
            hc = c if j < 3 else 1 - c
            land = outs[k].at[me] if j == 6 else rows(k, {0: s1, 1: s2, 2: sd, 3: s2, 4: s1, 5: sd}[j], hc)
            remote(k, j, land, land, (x, y, c)).wait_recv()

        return copy, arrived

    def start(ins, outs, sems):
        copy, _ = parts(ins, outs, sems)
        for k in range(n):
            copy(k, 0).start()
            copy(k, 1).start()
            copy(k, 6).start()

    def middle(ins, outs, sems):
        copy, arrived = parts(ins, outs, sems)
        for k in range(n):
            arrived(k, 0)
            copy(k, 2).start()
            copy(k, 3).start()
            arrived(k, 1)
            copy(k, 4).start()

    def finish(ins, outs, sems):
        copy, arrived = parts(ins, outs, sems)
        for k in range(n):
            arrived(k, 2)
            copy(k, 5).start()
        for k in range(n):
            for j in (3, 4, 5, 6):
                arrived(k, j)
        for k in range(n):
            for j in range(7):
                copy(k, j).wait_send()

    out_shapes = [jax.ShapeDtypeStruct((N_CHIPS,) + s.shape, s.dtype) for s in shards]
    scratch = [pltpu.SemaphoreType.DMA((7 * n,)), pltpu.SemaphoreType.DMA((7 * n,))]
    return list(shards), out_shapes, scratch, start, finish, middle


def _swap_halves(grads):
    n = len(grads)
    halves = [g.shape[1] // 2 for g in grads]

    def copies(ins, outs, sems):
        x, y, c, _ = _place()
        return [pltpu.make_async_remote_copy(
            src_ref=ins[k].at[:, pl.ds((1 - c) * halves[k], halves[k]), :], dst_ref=outs[k], send_sem=sems[0].at[k],
            recv_sem=sems[1].at[k], device_id=(x, y, 1 - c), device_id_type=MESH) for k in range(n)]

    def start(ins, outs, sems):
        for cp in copies(ins, outs, sems):
            cp.start()

    def finish(ins, outs, sems):
        for cp in copies(ins, outs, sems):
            cp.wait()

    out_shapes = [jax.ShapeDtypeStruct((g.shape[0], h) + g.shape[2:], g.dtype) for g, h in zip(grads, halves)]
    scratch = [pltpu.SemaphoreType.DMA((n,)), pltpu.SemaphoreType.DMA((n,))]
    return list(grads), out_shapes, scratch, start, finish


def _scatter_to_owners(chip_sums):
    n = len(chip_sums)

    def sends(ins, outs, sems):
        x, y, c, others = _place()
        me = 2 * x + y
        return [pltpu.make_async_remote_copy(
            src_ref=ins[k].at[2 * px + py], dst_ref=outs[k].at[me], send_sem=sems[0].at[3 * k + j],
            recv_sem=sems[1].at[3 * k + j], device_id=(px, py, c), device_id_type=MESH)
            for k in range(n) for j, (px, py) in enumerate(others)]

    def start(ins, outs, sems):
        for cp in sends(ins, outs, sems):
            cp.start()

    def finish(ins, outs, sems):
        x, y, c, others = _place()
        for k in range(n):
            for j, (px, py) in enumerate(others):
                land = outs[k].at[2 * px + py]
                pltpu.make_async_remote_copy(src_ref=land, dst_ref=land, send_sem=sems[0].at[3 * k + j],
                                             recv_sem=sems[1].at[3 * k + j], device_id=(x, y, c),
                                             device_id_type=MESH).wait_recv()
        for cp in sends(ins, outs, sems):
            cp.wait_send()

    out_shapes = [jax.ShapeDtypeStruct(g.shape, g.dtype) for g in chip_sums]
    scratch = [pltpu.SemaphoreType.DMA((3 * n,)), pltpu.SemaphoreType.DMA((3 * n,))]
    return list(chip_sums), out_shapes, scratch, start, finish


def _swap_with_sibling(arrays):
    n = len(arrays)

    def copies(ins, outs, sems):
        x, y, c, _ = _place()
        return [pltpu.make_async_remote_copy(src_ref=ins[k], dst_ref=outs[k], send_sem=sems[0].at[k],
                                             recv_sem=sems[1].at[k], device_id=(x, y, 1 - c), device_id_type=MESH)
                for k in range(n)]

    def start(ins, outs, sems):
        for cp in copies(ins, outs, sems):
            cp.start()

    def finish(ins, outs, sems):
        for cp in copies(ins, outs, sems):
            cp.wait()

    out_shapes = [jax.ShapeDtypeStruct(a.shape, a.dtype) for a in arrays]
    scratch = [pltpu.SemaphoreType.DMA((n,)), pltpu.SemaphoreType.DMA((n,))]
    return list(arrays), out_shapes, scratch, start, finish


def _add_pair(name, a, b):
    R, Cc = a.shape
    tr = _row_tile(R, Cc, 4)

    def body(a_ref, b_ref, o_ref):
        o_ref[...] = (a_ref[...].astype(F32) + b_ref[...].astype(F32)).astype(BF16)

    return _pcall(body, name=name, grid=(R // tr,), in_specs=[_tile(tr, Cc)] * 2, out_specs=_tile(tr, Cc),
                  out_shape=jax.ShapeDtypeStruct((R, Cc), BF16), compiler_params=_cparams(("parallel",)))(a, b)


def _second_neighbour():
    x, y, c, _ = _place()
    return (x, y, c), (x ^ c, y ^ (1 - c)), (x ^ (1 - c), y ^ c)


def _scatter_stage1(chip_sums):
    n = len(chip_sums)

    def copies(ins, outs, sems):
        (x, y, c), n2, n1 = _second_neighbour()
        diag = 2 * (1 - x) + (1 - y)
        return [pltpu.make_async_remote_copy(
            src_ref=ins[k].at[slot], dst_ref=outs[2 * k + j], send_sem=sems[0].at[2 * k + j],
            recv_sem=sems[1].at[2 * k + j], device_id=(*n2, c), device_id_type=MESH)
            for k in range(n) for j, slot in enumerate((2 * n2[0] + n2[1], diag))]

    def start(ins, outs, sems):
        for cp in copies(ins, outs, sems):
            cp.start()

    def finish(ins, outs, sems):
        for cp in copies(ins, outs, sems):
            cp.wait()

    out_shapes = [jax.ShapeDtypeStruct(g.shape[1:], g.dtype) for g in chip_sums for _ in range(2)]
    scratch = [pltpu.SemaphoreType.DMA((2 * n,)), pltpu.SemaphoreType.DMA((2 * n,))]
    return list(chip_sums), out_shapes, scratch, start, finish


def _scatter_stage2(passed):
    n = len(passed)

    def copies(ins, outs, sems):
        (x, y, c), n2, n1 = _second_neighbour()
        return [pltpu.make_async_remote_copy(src_ref=ins[k], dst_ref=outs[k], send_sem=sems[0].at[k],
                                             recv_sem=sems[1].at[k], device_id=(*n1, c), device_id_type=MESH)
                for k in range(n)]

    def start(ins, outs, sems):
        for cp in copies(ins, outs, sems):
            cp.start()

    def finish(ins, outs, sems):
        for cp in copies(ins, outs, sems):
            cp.wait()

    out_shapes = [jax.ShapeDtypeStruct(p.shape, p.dtype) for p in passed]
    scratch = [pltpu.SemaphoreType.DMA((n,)), pltpu.SemaphoreType.DMA((n,))]
    return list(passed), out_shapes, scratch, start, finish


def _add_passed(name, own, got, slot):
    _, H, Cc = own.shape
    tr = _row_tile(H, Cc, 4)

    def body(s_ref, o_ref, g_ref, out_ref):
        out_ref[...] = (o_ref[0].astype(F32) + g_ref[...].astype(F32)).astype(BF16)

    grid_spec = pltpu.PrefetchScalarGridSpec(
        num_scalar_prefetch=1, grid=(H // tr,),
        in_specs=[pl.BlockSpec((1, tr, Cc), lambda i, s: (s[0], i, 0)), pl.BlockSpec((tr, Cc), lambda i, s: (i, 0))],
        out_specs=pl.BlockSpec((tr, Cc), lambda i, s: (i, 0)))
    return _pcall(body, name=name, grid_spec=grid_spec, out_shape=jax.ShapeDtypeStruct((H, Cc), BF16),
                  compiler_params=_cparams(("parallel",)))(slot, own, got)


def _sum_stages(name, own, direct, via, place, transposed=False):
    _, H, Cc = own.shape
    tr = LANES if transposed else _row_tile(H, Cc, 4, 1024 * 1024)
    nb = H // tr

    def body(p_ref, own_ref, d_ref, v_ref, o_ref):
        acc = (own_ref[0].astype(F32) + d_ref[...].astype(F32)) + v_ref[...].astype(F32)
        o_ref[...] = acc.T if transposed else acc

    flat = pl.BlockSpec((tr, Cc), lambda i, p: (i, 0))
    out_spec = (pl.BlockSpec((Cc, tr), lambda i, p: (0, p[1] * nb + i)) if transposed
                else pl.BlockSpec((tr, Cc), lambda i, p: (p[1] * nb + i, 0)))
    grid_spec = pltpu.PrefetchScalarGridSpec(
        num_scalar_prefetch=1, grid=(nb,),
        in_specs=[pl.BlockSpec((1, tr, Cc), lambda i, p: (p[0], i, 0)), flat, flat], out_specs=out_spec)
    return _pcall(body, name=name, grid_spec=grid_spec,
                  out_shape=jax.ShapeDtypeStruct((Cc, 2 * H) if transposed else (2 * H, Cc), F32),
                  compiler_params=_cparams(("parallel",)))(place, own, direct, via)


def _join_halves(fulls, axes, small):
    n = len(fulls)
    hs = [f.shape[ax] // 2 for f, ax in zip(fulls, axes)]
    rel = [(dx, dy, dc) for dx in (0, 1) for dy in (0, 1) for dc in (0, 1)][1:]

    def half(ref, k, hc):
        part = pl.ds(hc * hs[k], hs[k])
        return ref.at[:, part] if axes[k] else ref.at[part, :]

    def body(*refs):
        ins, small_in = refs[:n], refs[n]
        outs, small_out = refs[n + 1:2 * n + 1], refs[2 * n + 1]
        send_sems, recv_sems, ssend, srecv, local_sem = refs[2 * n + 2:]
        x, y, c, _ = _place()
        dev = 4 * x + 2 * y + c
        local = pltpu.make_async_copy(small_in, small_out.at[dev], local_sem)
        local.start()
        cps = []
        for k in range(n):
            cp = pltpu.make_async_remote_copy(src_ref=half(ins[k], k, c), dst_ref=half(outs[k], k, c),
                                              send_sem=send_sems.at[k], recv_sem=recv_sems.at[k],
                                              device_id=(x, y, 1 - c), device_id_type=MESH)
            cp.start()
            cps.append(cp)
        for r, (dx, dy, dc) in enumerate(rel):
            cp = pltpu.make_async_remote_copy(src_ref=small_in, dst_ref=small_out.at[dev], send_sem=ssend.at[r],
                                              recv_sem=srecv.at[r], device_id=(x ^ dx, y ^ dy, c ^ dc),
                                              device_id_type=MESH)
            cp.start()
            cps.append(cp)
        for k in range(n):
            land = half(outs[k], k, 1 - c)
            pltpu.make_async_remote_copy(src_ref=land, dst_ref=land, send_sem=send_sems.at[k],
                                         recv_sem=recv_sems.at[k], device_id=(x, y, c), device_id_type=MESH).wait_recv()
        for r, (dx, dy, dc) in enumerate(rel):
            land = small_out.at[4 * (x ^ dx) + 2 * (y ^ dy) + (c ^ dc)]
            pltpu.make_async_remote_copy(src_ref=land, dst_ref=land, send_sem=ssend.at[r], recv_sem=srecv.at[r],
                                         device_id=(x, y, c), device_id_type=MESH).wait_recv()
        for cp in cps:
            cp.wait_send()
        local.wait()

    return _pcall(
        body, name="join_halves", in_specs=[_ANY] * (n + 1), out_specs=[_ANY] * (n + 1),
        out_shape=[jax.ShapeDtypeStruct(f.shape, f.dtype) for f in fulls]
        + [jax.ShapeDtypeStruct((N_DEV,) + small.shape, small.dtype)],
        input_output_aliases={k: k for k in range(n)},
        scratch_shapes=[pltpu.SemaphoreType.DMA((n,)), pltpu.SemaphoreType.DMA((n,)), pltpu.SemaphoreType.DMA((7,)),
                        pltpu.SemaphoreType.DMA((7,)), pltpu.SemaphoreType.DMA],
    )(*fulls, small)


def _local_step(cfg, x2, target, norm_gain, w_my, fb, mu_g, w0, a0, k_k, k_a, r_k, ln_w, ln_b, fng, rest,
                exchange=None, h=None):
    T, D, FW, FH, RW, RH, LP, lora = cfg.T, cfg.D, cfg.FW, cfg.FH, cfg.RW, cfg.RH, cfg.LP, cfg.lora
    fb_p = jnp.pad(fb, ((0, 0), (0, LANES - FH)))
    mu = _rwkv_vec_to_my(cfg, mu_g)
    rk = r_k.reshape(1, RW)
    tm = min(1024, T)

    if h is None:
        h = _rms_fwd(cfg, x2, norm_gain)
    if len(rest) == 2:
        u, *got = _mm("in_proj", h, w_my, "nn", F32, tm, cfg.tn, 2048, comm=rest[0])
        rest = rest[1](got)
    else:
        u = _mm("in_proj", h, w_my, "nn", F32, tm, cfg.tn, 2048)
    w2, a2, wpf, wpr, wout = rest
    w2p = jnp.pad(w2, ((0, LP - lora), (0, 0)))
    a2p = jnp.pad(a2, ((0, LP - lora), (0, 0)))
    c_cols = _fox_prep(cfg, u, fb_p)
    c_rows = c_cols[:, :FH].T.reshape(FH, 1, T)
    o, lse = _attn_fwd(cfg, u, c_rows)
    oa = _gate_a_fwd(cfg, o, u)
    prep = _rwkv_prep_fwd(cfg, u, mu, w0, w2p, a0, a2p, k_k, k_a)
    r, lw, kp, v, an, b, zb = prep
    toks = [r, lw, kp, v, an, b]
    q_s, yloc, a_m, sloc = _scan_local_fwd(cfg, toks)
    y, ckpt = _scan_carry_fwd(cfg, q_s, yloc, a_m, sloc)
    ob = _rwkv_post_fwd(cfg, y, r, kp, v, zb, ln_w, ln_b, rk)
    pa = _mm("proj_fox", oa, wpf, "nn", F32, tm, 1024, 2048)
    pb = _mm("proj_rwkv", ob, wpr, "nn", F32, tm, 1024, 2048)
    m = _merge_fwd(cfg, pa, pb, u)
    mo = _mm("out_proj", m, wout, "nn", F32, tm, 1024, 2048)
    loss8, dres, dres16, d_fng = _final(cfg, x2, mo, fng.reshape(1, D), target)

    dm = _mm("out_proj_dx", dres16, wout, "nt", F32, tm, 1024, 2048)
    d_wout = _mm("out_proj_dw", m, dres16, "tn", BF16, 1024, 1024, 2048)
    dpa, dpb, du = _merge_bwd(cfg, pa, pb, u, dm)
    doa = _mm("proj_fox_dx", dpa, wpf, "nt", F32, tm, 1024, 2048)
    d_wpf = _mm("proj_fox_dw", oa, dpa, "tn", BF16, 1024, 1024, 2048)
    dob = _mm("proj_rwkv_dx", dpb, wpr, "nt", F32, tm, 1024, 2048)
    d_wpr = _mm("proj_rwkv_dw", ob, dpb, "tn", BF16, 1024, 1024, 2048)

    do, du = _gate_a_bwd(cfg, o, u, doa, du)
    du, dcol = _attn_bwd(cfg, u, c_rows, lse, do, du)
    dc = jnp.pad(-dcol.reshape(FH, T).T, ((0, 0), (0, LANES - FH)))
    df, d_fb = _fox_prep_bwd(cfg, u, fb_p, dc)

    dy, dr_p, dk_p, dv_p, dzb, d_lnw, d_lnb, d_rk = _rwkv_post_bwd(cfg, y, r, kp, v, zb, ln_w, ln_b, rk, dob)
    early = dict(w_proj_fox=d_wpf, w_proj_rwkv=d_wpr, w_out=d_wout)
    res = _scan_carry_bwd(cfg, q_s, a_m, ckpt, dy, exchange(early) if exchange else None)
    dq_s, da_m, dsl = res[:3]
    res = _scan_local_bwd(cfg, toks, dq_s, dy, da_m, dsl, [dr_p, dk_p, dv_p],
                          exchange(("swapped", list(res[3:]))) if exchange else None)
    cots, received = res[:6], list(res[6:])
    dus, d_mu, d_w0, d_w2p, d_a0, d_a2p, d_kk, d_ka = _rwkv_prep_bwd(cfg, u, mu, w0, w2p, a0, a2p, k_k, k_a, cots, dzb)
    du = _shift_bwd(cfg, dus, mu, df, du)
    if exchange:
        late = dict(w_in=exchange((h, du, d_w2p[:lora], d_a2p[:lora])))
    else:
        late = dict(w_in=_mm("in_proj_dw", h, du, "tn", BF16, 1024, cfg.tn, 2048), rwkv_w2=d_w2p[:lora],
                    rwkv_a2=d_a2p[:lora])
    tkx = 2 * cfg.tn if cfg.ncol % (2 * cfg.tn) == 0 else cfg.tn
    res = _mm("in_proj_dx", du, w_my, "nt", F32, tm, 1024, tkx, comm=exchange(late) if exchange else None)
    dh = res[0] if exchange else res
    big = dict(early, **late)
    res = _rms_bwd(cfg, x2, norm_gain, dh, dres, exchange(list(res[1:])) if exchange else None)
    gx, d_ng = res[:2]
    received += list(res[2:])

    small = dict(norm_gain=d_ng, fox_forget_bias=d_fb[:, :FH], rwkv_shift_mix=_rwkv_vec_from_my(cfg, d_mu),
                 rwkv_w0=d_w0, rwkv_a0=d_a0, rwkv_k_k=d_kk, rwkv_k_a=d_ka, rwkv_r_k=d_rk, rwkv_ln_w=d_lnw,
                 rwkv_ln_b=d_lnb, final_norm_gain=d_fng)
    return loss8[0, 0], gx, small, big, received


_SMALL = ["norm_gain", "fox_forget_bias", "rwkv_shift_mix", "rwkv_w0", "rwkv_a0", "rwkv_k_k", "rwkv_k_a", "rwkv_r_k",
          "rwkv_ln_w", "rwkv_ln_b", "final_norm_gain"]
_WEIGHTS = ["norm_gain", "w_in", "fox_forget_bias", "rwkv_shift_mix", "rwkv_w0", "rwkv_w2", "rwkv_a0", "rwkv_a2",
            "rwkv_k_k", "rwkv_k_a", "rwkv_r_k", "rwkv_ln_w", "rwkv_ln_b", "w_proj_fox", "w_proj_rwkv", "w_out",
            "final_norm_gain"]


def _pack_small(arrs):
    parts = []
    for a in arrs:
        f = a.reshape(-1)
        parts.append(jnp.pad(f, (0, (-f.shape[0]) % LANES)))
    flat = jnp.concatenate(parts)
    rows = flat.shape[0] // LANES
    flat = jnp.pad(flat, (0, ((-rows) % 8) * LANES))
    return flat.reshape(-1, LANES)


def _unpack_small(packed, shapes):
    flat = packed.reshape(-1)
    out, pos = [], 0
    for s in shapes:
        n = int(np.prod(s))
        out.append(flat[pos:pos + n].reshape(s))
        pos += n + ((-n) % LANES)
    return out


def _shard_major(a, axis):
    parts = jnp.split(a, N_CHIPS, axis=axis)
    return jnp.stack(parts, axis=0)


def kernel(x, norm_gain, w_in, fox_forget_bias, rwkv_shift_mix, rwkv_w0, rwkv_w2, rwkv_a0, rwkv_a2, rwkv_k_k, rwkv_k_a, rwkv_r_k, rwkv_ln_w, rwkv_ln_b, w_proj_fox, w_proj_rwkv, w_out, final_norm_gain, loss_target, m_norm_gain, m_w_in, m_fox_forget_bias, m_rwkv_shift_mix, m_rwkv_w0, m_rwkv_w2, m_rwkv_a0, m_rwkv_a2, m_rwkv_k_k, m_rwkv_k_a, m_rwkv_r_k, m_rwkv_ln_w, m_rwkv_ln_b, m_w_proj_fox, m_w_proj_rwkv, m_w_out, m_final_norm_gain, v_norm_gain, v_w_in, v_fox_forget_bias, v_rwkv_shift_mix, v_rwkv_w0, v_rwkv_w2, v_rwkv_a0, v_rwkv_a2, v_rwkv_k_k, v_rwkv_k_a, v_rwkv_r_k, v_rwkv_ln_w, v_rwkv_ln_b, v_w_proj_fox, v_w_proj_rwkv, v_w_out, v_final_norm_gain):
    args = dict(locals())
    T, D = x.shape[1], x.shape[2]
    lora = rwkv_w2.shape[1]
    cfg = _Cfg(T, D, lora)
    RW = cfg.RW
    c_idx = lax.axis_index("c").astype(jnp.int32).reshape(1)
    me_chip = (2 * lax.axis_index("x") + lax.axis_index("y")).astype(jnp.int32)
    place = jnp.concatenate([me_chip.reshape(1), c_idx])

    w_in_s = w_in[0].astype(BF16)
    lora_s = jnp.concatenate([rwkv_w2[0], rwkv_a2[0]], axis=0)
    h, g_in = _rms_fwd(cfg, x[0], norm_gain, _gather_parts([w_in_s]))
    w_my = _shards_to_my_layout(cfg, g_in)
    mine = [_cast_bf16("cast_w_proj_fox", w_proj_fox[0]), _cast_bf16("cast_w_proj_rwkv", w_proj_rwkv[0]),
            _cast_bf16("cast_w_out", w_out[0]), lora_s]

    def unpack(gathered):
        g_wpf, g_wpr, g_out, g_lora = gathered
        lo = g_lora.transpose(1, 0, 2).reshape(2 * lora, RW)
        return (lo[:lora], lo[lora:], g_wpf.transpose(1, 0, 2).reshape(RW, D),
                g_wpr.transpose(1, 0, 2).reshape(RW, D), g_out.reshape(D, D))

    early, late = ["w_proj_fox", "w_proj_rwkv", "w_out"], ["w_in", "lora"]
    names = early + late
    chip_sums, direct, shard_major = {}, {}, []
    n1_slot = (2 * (lax.axis_index("x") ^ (1 - lax.axis_index("c")))
               + (lax.axis_index("y") ^ lax.axis_index("c"))).astype(jnp.int32).reshape(1)

    def exchange(got):
        if isinstance(got, tuple) and len(got) == 4:
            h, du, d_w2, d_a2 = got
            c, half = lax.axis_index("c"), D // 2
            cols = lambda base: lax.dynamic_slice_in_dim(h, base * half, half, axis=1)
            lora_g = _shard_major(jnp.concatenate([d_w2, d_a2], axis=0).astype(BF16), 1)
            lora_rows = lambda base: lax.dynamic_slice_in_dim(lora_g, base * lora, lora, axis=1).reshape(-1, RW // 4)
            tiles = (BF16, min(1024, half), cfg.tn, 2048)
            sent = _mm("in_proj_dw_sibling", cols(1 - c), du, "tn", *tiles)
            kept, got_w, got_l = _mm("in_proj_dw", cols(c), du, "tn", *tiles,
                                     comm=_swap_with_sibling([sent, lora_rows(1 - c)]))
            return (_add_pair("add_halves_w_in", kept, got_w),
                    _add_pair("add_halves_lora", lora_rows(c), got_l).reshape(N_CHIPS, lora, RW // 4))
        if isinstance(got, dict):
            if "w_in" in got:
                sums = [_my_layout_to_shards(cfg, got["w_in"][0]), got["w_in"][1]]
                chip_sums.update(zip(late, sums))
                return _scatter_stage1(sums)
            shard_major.extend([_shard_major(got["w_proj_fox"], 1), _shard_major(got["w_proj_rwkv"], 1),
                                _shard_major(got["w_out"], 0)])
            return _swap_halves(shard_major)
        if got[0] == "swapped":
            sums = [_add_halves("add_halves_" + nm, g, r, c_idx) for nm, g, r in zip(early, shard_major, got[1])]
            chip_sums.update(zip(early, sums))
            return _scatter_to_owners(sums)
        direct.update(zip(late, got[0::2]))
        return _scatter_stage2([_add_passed("add_passed_" + nm, chip_sums[nm], g, n1_slot)
                                for nm, g in zip(late, got[1::2])])

    loss_dev, gx, small, _, recv2 = _local_step(
        cfg, x[0], loss_target[0], norm_gain, w_my, fox_forget_bias, rwkv_shift_mix, rwkv_w0, rwkv_a0, rwkv_k_k,
        rwkv_k_a, rwkv_r_k, rwkv_ln_w, rwkv_ln_b, final_norm_gain, (_gather_parts(mine), unpack), exchange, h)
    loss = lax.psum(loss_dev, ("x", "y", "c"))

    small_shapes = [args[nm].shape for nm in _SMALL]
    packed = _pack_small([small[nm] for nm in _SMALL])
    reduced = [_sum_chips("sum_chips_" + nm, r, chip_sums[nm], place) for nm, r in zip(early, recv2[:3])]
    reduced += [_sum_stages("sum_stages_" + nm, chip_sums[nm], direct[nm], via, place, transposed=nm == "w_in")
                for nm, via in zip(late, recv2[3:])]
    *joined, small_all = _join_halves(reduced, [int(nm == "w_in") for nm in names], packed)
    g_small = _sum_slots("sum_small", small_all)

    grads = dict(zip(_SMALL, _unpack_small(g_small, small_shapes)))
    grads.update({nm: g[None] for nm, g in zip(names, joined) if nm not in ("lora", "w_in")})
    g_lora_f = joined[names.index("lora")]
    grads["rwkv_w2"] = g_lora_f[None, :lora]
    grads["rwkv_a2"] = g_lora_f[None, lora:]

    delta, new_m, new_v = {}, {}, {}
    w_small = _pack_small([args[nm] for nm in _SMALL])
    m_small = _pack_small([args["m_" + nm] for nm in _SMALL])
    v_small = _pack_small([args["v_" + nm] for nm in _SMALL])
    d_s, m_s, v_s = _adamw("adamw_small", w_small, g_small, m_small, v_small)
    for tgt, pk in ((delta, d_s), (new_m, m_s), (new_v, v_s)):
        tgt.update(zip(_SMALL, _unpack_small(pk, small_shapes)))
    t_out = _adamw("adamw_w_in", w_in[0].T, joined[names.index("w_in")], m_w_in[0].T, v_w_in[0].T, copy_grad=True)
    delta["w_in"], new_m["w_in"], new_v["w_in"], grads["w_in"] = [t.T[None] for t in t_out]
    for nm in ("w_proj_fox", "w_proj_rwkv", "w_out", "rwkv_w2", "rwkv_a2"):
        shp = args[nm].shape
        two_d = (shp[1], shp[2])
        d_b, m_b, v_b = _adamw("adamw_" + nm, args[nm].reshape(two_d), grads[nm].reshape(two_d),
                               args["m_" + nm].reshape(two_d), args["v_" + nm].reshape(two_d))
        delta[nm], new_m[nm], new_v[nm] = d_b.reshape(shp), m_b.reshape(shp), v_b.reshape(shp)

    return (loss, gx[None], *[grads[n] for n in _WEIGHTS], *[delta[n] for n in _WEIGHTS],
            *[new_m[n] for n in _WEIGHTS], *[new_v[n] for n in _WEIGHTS])
```

```python
import functools

import numpy as np
import jax
import jax.numpy as jnp
from jax import lax
from jax.experimental import pallas as pl
from jax.experimental.pallas import tpu as pltpu

F32 = jnp.float32
BF16 = jnp.bfloat16
HI = lax.Precision.HIGHEST
MESH = pl.DeviceIdType.MESH

FOX_HEAD_DIM = 128
RWKV_HEAD_DIM = 64
RMS_EPS = 1e-6
GN_EPS = 64e-5
L2_EPS = 1e-12
ADAM_LR = 0.001
ADAM_B1 = 0.9
ADAM_B2 = 0.999
ADAM_EPS = 1e-08
ADAM_WD = 0.01
ADAM_STEP = 10

LANES = 128
VMEM_LIMIT = 56 * 1024 * 1024
SCAN_CHUNK = 64
SCAN_HEADS_PER_STEP = 16
SCAN_PASSES = (3, 1, 1)
N_CHIPS = 4
N_DEV = 8

_pcall = pl.pallas_call


def _cparams(sem=None):
    return pltpu.CompilerParams(dimension_semantics=sem, vmem_limit_bytes=VMEM_LIMIT)


def _softplus(x):
    return jnp.maximum(x, 0.0) + jnp.log(1.0 + jnp.exp(-jnp.abs(x)))


def _silu(z):
    return z * jax.nn.sigmoid(z)


def _rmsn(x, g):
    return x * lax.rsqrt(jnp.mean(x * x, axis=-1, keepdims=True) + RMS_EPS) * g


def _dot(a, b, dims="nn", precision=None):
    dn = {"nn": (((1,), (0,)), ((), ())), "nt": (((1,), (1,)), ((), ())), "tn": (((0,), (0,)), ((), ()))}[dims]
    return lax.dot_general(a, b, dn, precision=precision, preferred_element_type=F32)


def _split_bf16(x):
    hi = x.astype(BF16)
    return hi, (x - hi.astype(F32)).astype(BF16)


def _bdot_raw(a, b, ca, cb, passes):
    dn = (((ca,), (cb,)), ((0,), (0,)))
    mm = lambda p, q: lax.dot_general(p, q, dn, preferred_element_type=F32)
    if passes == 1:
        return mm(a.astype(BF16), b.astype(BF16))
    ah, al = _split_bf16(a)
    bh, bl = _split_bf16(b)
    return mm(ah, bh) + (mm(ah, bl) + mm(al, bh))


@functools.partial(jax.custom_vjp, nondiff_argnums=(2, 3, 4))
def _bdot_p(a, b, ca, cb, passes):
    return _bdot_raw(a, b, ca, cb, passes)


def _bdot_fwd(a, b, ca, cb, passes):
    return _bdot_raw(a, b, ca, cb, passes), (a, b)


def _bdot_bwd(ca, cb, passes, res, g):
    a, b = res
    if (ca, cb) == (2, 1):
        return _bdot_p(g, b, 2, 2, passes), _bdot_p(a, g, 1, 1, passes)
    if (ca, cb) == (2, 2):
        return _bdot_p(g, b, 2, 1, passes), _bdot_p(g, a, 1, 1, passes)
    assert (ca, cb) == (1, 1)
    return _bdot_p(b, g, 2, 2, passes), _bdot_p(a, g, 2, 1, passes)


_bdot_p.defvjp(_bdot_fwd, _bdot_bwd)


def _bdot(a, b, ca, cb, passes=3):
    return _bdot_p(a, b, ca, cb, passes)


def _dot3(a, b):
    return _bdot(a[None], b[None], 2, 1)[0]


@jax.custom_vjp
def _xdot(x, m, mt):
    hi, lo = _split_bf16(x)
    m16 = m.astype(BF16)
    return _dot(hi, m16) + _dot(lo, m16)


def _xdot_fwd(x, m, mt):
    return _xdot(x, m, mt), (m, mt)


def _xdot_bwd(res, g):
    m, mt = res
    return _xdot(g, mt, m), jnp.zeros_like(m), jnp.zeros_like(mt)


_xdot.defvjp(_xdot_fwd, _xdot_bwd)


class _Cfg:
    def __init__(self, T, D, lora):
        self.T, self.D, self.lora = T, D, lora
        self.FW = D // 2
        self.FH = self.FW // FOX_HEAD_DIM
        self.RW = D // 2
        self.RH = self.RW // RWKV_HEAD_DIM
        self.LP = -(-lora // LANES) * LANES
        self.o_fox = 0
        self.o_rwkv = 4 * self.FW
        self.o_gate = self.o_rwkv + 4 * self.RW
        self.o_f = self.o_gate + 2 * D
        self.o_wd = self.o_f + LANES
        self.o_ad = self.o_wd + self.LP
        end = self.o_ad + self.LP
        self.tn = 1280 if D >= 2048 else LANES
        self.ncol = -(-end // self.tn) * self.tn
        self.in_cols = 4 * self.FW + self.FH + 4 * self.RW + 2 * lora + 2 * D
        self.scp = -(-(self.in_cols // N_CHIPS) // LANES) * LANES
        self.rseg = 4 * self.RW + 2 * self.LP
        self.C = min(SCAN_CHUNK, T)
        self.tr = min(256, T)
        self.hb = min(SCAN_HEADS_PER_STEP, self.RH)

    def segments(self):
        FW, FH, RW, lo, D = self.FW, self.FH, self.RW, self.lora, self.D
        g_f = 4 * FW
        g_r = g_f + FH
        g_wd = g_r + 4 * RW
        g_ad = g_wd + lo
        g_g = g_ad + lo
        dh = FOX_HEAD_DIM
        qkv = [(j * FW + h * dh, dh, (3 * h + j) * dh) for h in range(FH) for j in range(3)]
        return qkv + [(3 * FW, FW, 3 * FW), (g_f, FH, self.o_f), (g_r, 4 * RW, self.o_rwkv), (g_wd, lo, self.o_wd),
                      (g_ad, lo, self.o_ad), (g_g, 2 * D, self.o_gate)]


def _shards_to_my_layout(cfg, g):
    R, sc = g.shape[1], g.shape[2]
    segs = sorted(cfg.segments(), key=lambda s: s[2])
    parts, pos = [], 0
    for g0, w, m0 in segs:
        if m0 > pos:
            parts.append(jnp.zeros((R, m0 - pos), g.dtype))
        for s in range(N_CHIPS):
            lo, hi = max(g0, s * sc), min(g0 + w, (s + 1) * sc)
            if lo < hi:
                parts.append(g[s, :, lo - s * sc:hi - s * sc])
        pos = m0 + w
    if cfg.ncol > pos:
        parts.append(jnp.zeros((R, cfg.ncol - pos), g.dtype))
    return jnp.concatenate(parts, axis=1)


def _my_layout_to_shards(cfg, wm):
    sc, R = cfg.in_cols // N_CHIPS, wm.shape[0]
    segs = sorted(cfg.segments(), key=lambda s: s[0])
    shards = []
    for s in range(N_CHIPS):
        parts = []
        for g0, w, m0 in segs:
            lo, hi = max(g0, s * sc), min(g0 + w, (s + 1) * sc)
            if lo < hi:
                parts.append(wm[:, m0 + lo - g0:m0 + hi - g0])
        parts.append(jnp.zeros((R, cfg.scp - sc), wm.dtype))
        shards.append(jnp.concatenate(parts, axis=1))
    return jnp.stack(shards, axis=0)


def _rwkv_vec_to_my(cfg, v):
    RW4, lo, LP = 4 * cfg.RW, cfg.lora, cfg.LP
    z = jnp.zeros((1, LP - lo), v.dtype)
    return jnp.concatenate([v[:, :RW4], v[:, RW4:RW4 + lo], z, v[:, RW4 + lo:], z], axis=1)


def _rwkv_vec_from_my(cfg, v):
    RW4, lo, LP = 4 * cfg.RW, cfg.lora, cfg.LP
    return jnp.concatenate([v[:, :RW4], v[:, RW4:RW4 + lo], v[:, RW4 + LP:RW4 + LP + lo]], axis=1)


def _comm_at(comm, which, steps, cin, cout, scr):
    if not comm or len(comm) <= which or comm[which] is None:
        return
    lin, total = 0, 1
    for d, n in enumerate(steps):
        lin = lin * n + pl.program_id(d)
        total *= n
    pl.when(lin == {3: 0, 4: total - 1, 5: total // 2}[which])(lambda: comm[which](cin, cout, scr))


def _hosted(body, n_in, n_out, steps, comm):
    if not comm:
        return body, [], [], [], []
    ci, co, cs = len(comm[0]), len(comm[1]), len(comm[2])

    def wrapped(*refs):
        ins, cin = refs[:n_in], refs[n_in:n_in + ci]
        outs, cout = refs[n_in + ci:n_in + ci + n_out], refs[n_in + ci + n_out:n_in + ci + n_out + co]
        cscr, scr = refs[n_in + ci + n_out + co:n_in + ci + n_out + co + cs], refs[n_in + ci + n_out + co + cs:]
        _comm_at(comm, 3, steps, cin, cout, cscr)
        body(*ins, *outs, *scr)
        _comm_at(comm, 5, steps, cin, cout, cscr)
        _comm_at(comm, 4, steps, cin, cout, cscr)

    return wrapped, [_ANY] * ci, [_ANY] * co, list(comm[1]), list(comm[2])


def _mm(name, a, b, dims, out_dtype, tm, tn, tk, comm=None):
    (M, K) = a.shape if dims != "tn" else a.shape[::-1]
    N = b.shape[0] if dims == "nt" else b.shape[1]
    tm, tn, tk = min(tm, M), min(tn, N), min(tk, K)
    assert M % tm == 0 and N % tn == 0 and K % tk == 0, (name, M, N, K, tm, tn, tk)
    nk = K // tk
    steps = (M // tm, N // tn, nk)
    c_in, c_out, c_scr = comm[:3] if comm else ([], [], [])
    if dims == "nn":
        a_spec = pl.BlockSpec((tm, tk), lambda i, j, k: (i, k))
        b_spec = pl.BlockSpec((tk, tn), lambda i, j, k: (k, j))
    elif dims == "nt":
        a_spec = pl.BlockSpec((tm, tk), lambda i, j, k: (i, k))
        b_spec = pl.BlockSpec((tn, tk), lambda i, j, k: (j, k))
    else:
        a_spec = pl.BlockSpec((tk, tm), lambda i, j, k: (k, i))
        b_spec = pl.BlockSpec((tk, tn), lambda i, j, k: (k, j))

    n_acc = 1 if nk > 1 else 0

    def body(a_ref, b_ref, *rest):
        cin, o_ref = rest[:len(c_in)], rest[len(c_in)]
        cout = rest[len(c_in) + 1:len(c_in) + 1 + len(c_out)]
        scr = rest[len(c_in) + 1 + len(c_out):]
        _comm_at(comm, 3, steps, cin, cout, scr[n_acc:])
        if nk == 1:
            o_ref[...] = _dot(a_ref[...], b_ref[...], dims).astype(o_ref.dtype)
        else:
            acc_ref, k = scr[0], pl.program_id(2)

            @pl.when(k == 0)
            def _():
                acc_ref[...] = jnp.zeros_like(acc_ref)

            acc_ref[...] += _dot(a_ref[...], b_ref[...], dims)

            @pl.when(k == nk - 1)
            def _():
                o_ref[...] = acc_ref[...].astype(o_ref.dtype)

        _comm_at(comm, 5, steps, cin, cout, scr[n_acc:])
        _comm_at(comm, 4, steps, cin, cout, scr[n_acc:])

    res = _pcall(
        body, name=name, grid=steps,
        in_specs=[a_spec, b_spec] + [_ANY] * len(c_in),
        out_specs=[pl.BlockSpec((tm, tn), lambda i, j, k: (i, j))] + [_ANY] * len(c_out),
        out_shape=[jax.ShapeDtypeStruct((M, N), out_dtype)] + list(c_out),
        scratch_shapes=([pltpu.VMEM((tm, tn), F32)] if nk > 1 else []) + list(c_scr),
        compiler_params=_cparams(("arbitrary",) * 3 if comm else ("parallel", "parallel", "arbitrary")),
    )(a, b, *c_in)
    return res if comm else res[0]


def _tile(tr, w, cb=0):
    return pl.BlockSpec((tr, w), lambda i: (i, cb))


def _const(shape):
    nd = len(shape)
    return pl.BlockSpec(shape, lambda i: (0,) * nd)


def _acc_store(i, ref, val):
    @pl.when(i == 0)
    def _():
        ref[...] = val

    @pl.when(i > 0)
    def _():
        ref[...] += val


def _rms_fwd(cfg, x2, g, comm=None):
    T, D, tr = cfg.T, cfg.D, cfg.tr
    steps = (T // tr,)

    def body(x_ref, g_ref, h_ref):
        h_ref[...] = _rmsn(x_ref[...], g_ref[...]).astype(BF16)

    body, c_in, c_out, c_shapes, c_scr = _hosted(body, 2, 1, steps, comm)
    res = _pcall(body, name="rms_fwd", grid=steps, in_specs=[_tile(tr, D), _const((1, D))] + c_in,
                 out_specs=[_tile(tr, D)] + c_out, out_shape=[jax.ShapeDtypeStruct((T, D), BF16)] + c_shapes,
                 scratch_shapes=c_scr, compiler_params=_cparams(("arbitrary",) if comm else ("parallel",)),
                 )(x2, g, *(comm[0] if comm else []))
    return res if comm else res[0]


def _rms_bwd(cfg, x2, g, dh, dres, comm=None):
    T, D, tr = cfg.T, cfg.D, cfg.tr
    c_in, c_out, c_scr = comm[:3] if comm else ([], [], [])
    steps = (T // tr,)

    def body(x_ref, g_ref, dh_ref, dres_ref, *rest):
        cin, (gx_ref, dg_ref) = rest[:len(c_in)], rest[len(c_in):len(c_in) + 2]
        cout, scr = rest[len(c_in) + 2:len(c_in) + 2 + len(c_out)], rest[len(c_in) + 2 + len(c_out):]
        _comm_at(comm, 3, steps, cin, cout, scr)
        _, vjp = jax.vjp(_rmsn, x_ref[...], g_ref[...])
        dx, dg = vjp(dh_ref[...])
        gx_ref[...] = dx + dres_ref[...]
        _acc_store(pl.program_id(0), dg_ref, dg)
        _comm_at(comm, 4, steps, cin, cout, scr)

    return _pcall(body, name="rms_bwd", grid=steps,
                  in_specs=[_tile(tr, D), _const((1, D)), _tile(tr, D), _tile(tr, D)] + [_ANY] * len(c_in),
                  out_specs=[_tile(tr, D), _const((1, D))] + [_ANY] * len(c_out),
                  out_shape=[jax.ShapeDtypeStruct((T, D), F32), jax.ShapeDtypeStruct((1, D), F32)] + list(c_out),
                  scratch_shapes=list(c_scr), compiler_params=_cparams(("arbitrary",)))(x2, g, dh, dres, *c_in)


def _final(cfg, x2, mo, fg, target):
    T, D, tr = cfg.T, cfg.D, cfg.tr

    def loss_fn(hres, g, tgt):
        err = _rmsn(hres, g) - tgt
        return 0.5 * jnp.sum(jnp.mean(err * err, axis=-1, keepdims=True), axis=0, keepdims=True)

    def body(x_ref, mo_ref, g_ref, t_ref, loss_ref, dres_ref, dres16_ref, dg_ref):
        hres = x_ref[...] + mo_ref[...]
        loss, vjp = jax.vjp(functools.partial(loss_fn, tgt=t_ref[...]), hres, g_ref[...])
        dres, dg = vjp(jnp.ones((1, 1), F32))
        dres_ref[...] = dres
        dres16_ref[...] = dres.astype(BF16)
        i = pl.program_id(0)
        _acc_store(i, dg_ref, dg)
        _acc_store(i, loss_ref, jnp.broadcast_to(loss, (8, LANES)))

    return _pcall(body, name="final_loss", grid=(T // tr,),
                  in_specs=[_tile(tr, D), _tile(tr, D), _const((1, D)), _tile(tr, D)],
                  out_specs=[_const((8, LANES)), _tile(tr, D), _tile(tr, D), _const((1, D))],
                  out_shape=[jax.ShapeDtypeStruct((8, LANES), F32), jax.ShapeDtypeStruct((T, D), F32),
                             jax.ShapeDtypeStruct((T, D), BF16), jax.ShapeDtypeStruct((1, D), F32)],
                  compiler_params=_cparams(("arbitrary",)))(x2, mo, fg, target)


def _merge_fn(pa, pb, ga, gb):
    return jax.nn.sigmoid(ga) * pa + jax.nn.sigmoid(gb) * pb


def _merge_fwd(cfg, pa, pb, u):
    T, D, tr = cfg.T, cfg.D, cfg.tr
    cga, cgb = cfg.o_gate // D, cfg.o_gate // D + 1

    def body(pa_ref, pb_ref, ga_ref, gb_ref, m_ref):
        m_ref[...] = _merge_fn(pa_ref[...], pb_ref[...], ga_ref[...], gb_ref[...]).astype(BF16)

    return _pcall(body, name="merge_fwd", grid=(T // tr,),
                  in_specs=[_tile(tr, D), _tile(tr, D), _tile(tr, D, cga), _tile(tr, D, cgb)],
                  out_specs=_tile(tr, D), out_shape=jax.ShapeDtypeStruct((T, D), BF16),
                  compiler_params=_cparams(("parallel",)))(pa, pb, u, u)


def _merge_bwd(cfg, pa, pb, u, dm):
    T, D, tr = cfg.T, cfg.D, cfg.tr
    cga, cgb = cfg.o_gate // D, cfg.o_gate // D + 1

    def body(pa_ref, pb_ref, ga_ref, gb_ref, dm_ref, dpa_ref, dpb_ref, dg_ref):
        _, vjp = jax.vjp(_merge_fn, pa_ref[...], pb_ref[...], ga_ref[...], gb_ref[...])
        dpa, dpb, dga, dgb = vjp(dm_ref[...])
        dpa_ref[...] = dpa.astype(BF16)
        dpb_ref[...] = dpb.astype(BF16)
        dg_ref[:, :D] = dga.astype(BF16)
        dg_ref[:, D:] = dgb.astype(BF16)

    return _pcall(body, name="merge_bwd", grid=(T // tr,),
                  in_specs=[_tile(tr, D), _tile(tr, D), _tile(tr, D, cga), _tile(tr, D, cgb), _tile(tr, D)],
                  out_specs=[_tile(tr, D), _tile(tr, D), _tile(tr, 2 * D, cfg.o_gate // (2 * D))],
                  out_shape=[jax.ShapeDtypeStruct((T, D), BF16), jax.ShapeDtypeStruct((T, D), BF16),
                             jax.ShapeDtypeStruct((T, cfg.ncol), BF16)],
                  compiler_params=_cparams(("parallel",)))(pa, pb, u, u, dm)


def _gate_fn(o, z):
    return o * _silu(z)


def _gate_a_fwd(cfg, o, u):
    T, FW, tr = cfg.T, cfg.FW, cfg.tr

    def body(o_ref, z_ref, oa_ref):
        oa_ref[...] = _gate_fn(o_ref[...], z_ref[...]).astype(BF16)

    return _pcall(body, name="gate_a_fwd", grid=(T // tr,), in_specs=[_tile(tr, FW), _tile(tr, FW, 3)],
                  out_specs=_tile(tr, FW), out_shape=jax.ShapeDtypeStruct((T, FW), BF16),
                  compiler_params=_cparams(("parallel",)))(o, u)


def _gate_a_bwd(cfg, o, u, doa, du):
    T, FW, tr = cfg.T, cfg.FW, cfg.tr

    def body(o_ref, z_ref, doa_ref, du_in, do_ref, dz_ref):
        _, vjp = jax.vjp(_gate_fn, o_ref[...], z_ref[...])
        do, dz = vjp(doa_ref[...])
        do_ref[...] = do
        dz_ref[...] = dz.astype(BF16)

    return _pcall(body, name="gate_a_bwd", grid=(T // tr,),
                  in_specs=[_tile(tr, FW), _tile(tr, FW, 3), _tile(tr, FW), _ANY],
                  out_specs=[_tile(tr, FW), _tile(tr, FW, 3)],
                  out_shape=[jax.ShapeDtypeStruct((T, FW), F32), jax.ShapeDtypeStruct(du.shape, BF16)],
                  input_output_aliases={3: 1},
                  compiler_params=_cparams(("parallel",)))(o, u, doa, du)


def _fox_prep(cfg, u, fb):
    T, tr = cfg.T, cfg.tr
    cf = cfg.o_f // LANES

    def body(f_ref, fb_ref, c_ref, carry_ref):
        i = pl.program_id(0)

        @pl.when(i == 0)
        def _():
            carry_ref[...] = jnp.zeros_like(carry_ref)

        lf = -_softplus(-(f_ref[...] + fb_ref[...]))
        r = lax.broadcasted_iota(jnp.int32, (tr, tr), 0)
        c = lax.broadcasted_iota(jnp.int32, (tr, tr), 1)
        tri = (r >= c).astype(F32)
        c_ref[...] = _dot(tri, lf, precision=HI) + carry_ref[...]
        carry_ref[...] += jnp.sum(lf, axis=0, keepdims=True)

    return _pcall(body, name="fox_prep", grid=(T // tr,), in_specs=[_tile(tr, LANES, cf), _const((1, LANES))],
                  out_specs=_tile(tr, LANES), out_shape=jax.ShapeDtypeStruct((T, LANES), F32),
                  scratch_shapes=[pltpu.VMEM((1, LANES), F32)], compiler_params=_cparams(("arbitrary",)))(u, fb)


def _fox_prep_bwd(cfg, u, fb, dc):
    T, tr = cfg.T, cfg.tr
    cf = cfg.o_f // LANES
    nb = T // tr

    def body(f_ref, fb_ref, dc_ref, df_ref, dfb_ref, carry_ref):
        i = pl.program_id(0)

        @pl.when(i == 0)
        def _():
            carry_ref[...] = jnp.zeros_like(carry_ref)

        dc = dc_ref[...]
        r = lax.broadcasted_iota(jnp.int32, (tr, tr), 0)
        c = lax.broadcasted_iota(jnp.int32, (tr, tr), 1)
        triu = (r <= c).astype(F32)
        dlf = _dot(triu, dc, precision=HI) + carry_ref[...]
        carry_ref[...] += jnp.sum(dc, axis=0, keepdims=True)
        dz = dlf * jax.nn.sigmoid(-(f_ref[...] + fb_ref[...]))
        df_ref[...] = dz.astype(BF16)
        _acc_store(i, dfb_ref, jnp.sum(dz, axis=0, keepdims=True))

    rev = lambda i: (nb - 1 - i, 0)
    return _pcall(body, name="fox_prep_bwd", grid=(nb,),
                  in_specs=[pl.BlockSpec((tr, LANES), lambda i: (nb - 1 - i, cf)), _const((1, LANES)),
                            pl.BlockSpec((tr, LANES), rev)],
                  out_specs=[pl.BlockSpec((tr, LANES), rev), _const((1, LANES))],
                  out_shape=[jax.ShapeDtypeStruct((T, LANES), BF16), jax.ShapeDtypeStruct((1, LANES), F32)],
                  scratch_shapes=[pltpu.VMEM((1, LANES), F32)], compiler_params=_cparams(("arbitrary",)))(u, fb, dc)


def _scaled_q(q_ref):
    return (q_ref[...] * (FOX_HEAD_DIM ** -0.5)).astype(BF16)


def _attn_logits(q_ref, k_ref, c_ref, tq, te):
    q = _scaled_q(q_ref)
    part = lambda k0, k1: _dot(q, k_ref[k0:k1, :].astype(BF16), "nt") - c_ref[0, :, k0:k1]
    row = lax.broadcasted_iota(jnp.int32, (tq, tq), 0)
    col = lax.broadcasted_iota(jnp.int32, (tq, tq), 1)
    own = ((te - tq, te), jnp.where(col <= row, part(te - tq, te), -1e30))
    return [((0, te - tq), part(0, te - tq)), own] if te > tq else [own]


def _per_query_tile(i, nq, tq, fn):
    for ii in range(nq):
        pl.when(i == ii)(functools.partial(fn, (ii + 1) * tq))


def _attn_fwd(cfg, u, c_rows):
    T, FW, FH = cfg.T, cfg.FW, cfg.FH
    tq = min(256, T)
    dh = FOX_HEAD_DIM

    def body(q_ref, k_ref, v_ref, c_ref, o_ref, lse_ref):
        i = pl.program_id(1)

        def tile(te):
            parts = _attn_logits(q_ref, k_ref, c_ref, tq, te)
            m = functools.reduce(jnp.maximum, [jnp.max(s, axis=1, keepdims=True) for _, s in parts])
            l, acc = 0.0, 0.0
            for (k0, k1), s in parts:
                p = jnp.exp(s - m)
                l = l + jnp.sum(p, axis=1, keepdims=True)
                acc = acc + _dot(p.astype(BF16), v_ref[k0:k1, :].astype(BF16))
            o_ref[...] = acc / l
            lse_ref[0] = m + jnp.log(l)

        _per_query_tile(i, T // tq, tq, tile)

    return _pcall(
        body, name="fox_attn_fwd", grid=(FH, T // tq),
        in_specs=[pl.BlockSpec((tq, dh), lambda h, i: (i, 3 * h)), pl.BlockSpec((T, dh), lambda h, i: (0, 3 * h + 1)),
                  pl.BlockSpec((T, dh), lambda h, i: (0, 3 * h + 2)), pl.BlockSpec((1, 1, T), lambda h, i: (h, 0, 0))],
        out_specs=[pl.BlockSpec((tq, dh), lambda h, i: (i, h)), pl.BlockSpec((1, tq, 1), lambda h, i: (h, i, 0))],
        out_shape=[jax.ShapeDtypeStruct((T, FW), F32), jax.ShapeDtypeStruct((FH, T, 1), F32)],
        compiler_params=_cparams(("parallel", "arbitrary")),
    )(u, u, u, c_rows)


def _attn_bwd(cfg, u, c_rows, lse, do, du):
    T, FW, FH = cfg.T, cfg.FW, cfg.FH
    tq = min(256, T)
    nq = T // tq
    dh = FOX_HEAD_DIM
    scale = dh ** -0.5

    def body(q_ref, k_ref, v_ref, c_ref, lse_ref, do_ref, du_in, du_ref, dcol_ref, dk_acc, dv_acc):
        i = pl.program_id(1)

        @pl.when(i == 0)
        def _():
            dk_acc[...] = jnp.zeros_like(dk_acc)
            dv_acc[...] = jnp.zeros_like(dv_acc)
            dcol_ref[...] = jnp.zeros_like(dcol_ref)

        def tile(te):
            lse, q16, do16 = lse_ref[0], _scaled_q(q_ref), do_ref[...].astype(BF16)
            probs = [(ks, jnp.exp(s - lse)) for ks, s in _attn_logits(q_ref, k_ref, c_ref, tq, te)]
            dps = [_dot(do16, v_ref[k0:k1, :].astype(BF16), "nt") for (k0, k1), _ in probs]
            delta = sum(jnp.sum(p * dp, axis=1, keepdims=True) for (_, p), dp in zip(probs, dps))
            dq = 0.0
            for ((k0, k1), p), dp in zip(probs, dps):
                ds = p * (dp - delta)
                ds16 = ds.astype(BF16)
                dq = dq + _dot(ds16, k_ref[k0:k1, :].astype(BF16))
                dk_acc[k0:k1, :] += _dot(ds16, q16, "tn")
                dv_acc[k0:k1, :] += _dot(p.astype(BF16), do16, "tn")
                dcol_ref[0, :, k0:k1] += jnp.sum(ds, axis=0, keepdims=True)
            du_ref[te - tq:te, 0:dh] = (dq * scale).astype(BF16)

        _per_query_tile(i, nq, tq, tile)

        @pl.when(i == nq - 1)
        def _():
            du_ref[:, dh:2 * dh] = dk_acc[...].astype(BF16)
            du_ref[:, 2 * dh:3 * dh] = dv_acc[...].astype(BF16)

    return _pcall(
        body, name="fox_attn_bwd", grid=(FH, nq),
        in_specs=[pl.BlockSpec((tq, dh), lambda h, i: (i, 3 * h)), pl.BlockSpec((T, dh), lambda h, i: (0, 3 * h + 1)),
                  pl.BlockSpec((T, dh), lambda h, i: (0, 3 * h + 2)), pl.BlockSpec((1, 1, T), lambda h, i: (h, 0, 0)),
                  pl.BlockSpec((1, tq, 1), lambda h, i: (h, i, 0)), pl.BlockSpec((tq, dh), lambda h, i: (i, h)), _ANY],
        out_specs=[pl.BlockSpec((T, 3 * dh), lambda h, i: (0, h)), pl.BlockSpec((1, 1, T), lambda h, i: (h, 0, 0))],
        out_shape=[jax.ShapeDtypeStruct(du.shape, BF16), jax.ShapeDtypeStruct((FH, 1, T), F32)],
        scratch_shapes=[pltpu.VMEM((T, dh), F32), pltpu.VMEM((T, dh), F32)],
        input_output_aliases={6: 0},
        compiler_params=_cparams(("parallel", "arbitrary")),
    )(u, u, u, c_rows, lse, do, du)


def _head_indicators(cfg):
    ind = np.zeros((cfg.RW, LANES), np.float32)
    ind[np.arange(cfg.RW), np.arange(cfg.RW) // RWKV_HEAD_DIM] = 1.0
    pad = np.zeros((1, LANES), np.float32)
    pad[0, cfg.RH:] = 1.0
    return jnp.asarray(ind), jnp.asarray(ind.T.copy()), jnp.asarray(pad)


def _prep_fn(us_r, us_k, us_v, us_wd, us_ad, w0, w2p, a0, a2p, k_k, k_a, ind, ind_t, pad):
    wpre = w0 + _dot3(jnp.tanh(us_wd), w2p)
    w = -_softplus(-wpre) - 0.5
    lw = -jnp.exp(w)
    a = jax.nn.sigmoid(a0 + _dot3(us_ad, a2p))
    kk = us_k * k_k
    ss = _xdot(kk * kk, ind, ind_t) + pad
    inv = 1.0 / jnp.maximum(jnp.sqrt(ss), L2_EPS)
    kkn = kk * _xdot(inv, ind_t, ind)
    kp = us_k * (1.0 + (a - 1.0) * k_a)
    return us_r, lw, kp, us_v, -kkn, kkn * a


def _shifted(u, prev_row, mu, first):
    n = u.shape[0]
    rolled = pltpu.roll(u, 1, 0)
    row = lax.broadcasted_iota(jnp.int32, u.shape, 0)
    p0 = jnp.where(first, jnp.zeros_like(prev_row), prev_row)
    prev = jnp.where(row == 0, jnp.broadcast_to(p0, u.shape), rolled)
    return u + (prev - u) * mu, prev


def _rwkv_specs(cfg, tr):
    RW, LP = cfg.RW, cfg.LP
    base = cfg.o_rwkv // RW
    cols = [(RW, base), (RW, base + 1), (RW, base + 2), (RW, base + 3), (LP, cfg.o_wd // LP), (LP, cfg.o_ad // LP)]
    cur = [pl.BlockSpec((tr, w), (lambda i, cb=cb: (i, cb))) for w, cb in cols]
    prv = [pl.BlockSpec((8, w), (lambda i, cb=cb: (jnp.maximum(i * (tr // 8) - 1, 0), cb))) for w, cb in cols]
    return cols, cur, prv


def _mu_pieces(cfg, mu_ref):
    RW, LP = cfg.RW, cfg.LP
    offs = [0, RW, 2 * RW, 3 * RW, 4 * RW, 4 * RW + LP, 4 * RW + 2 * LP]
    return [mu_ref[:, offs[j]:offs[j + 1]] for j in range(6)]


def _rwkv_prep_fwd(cfg, u, mu, w0, w2p, a0, a2p, k_k, k_a):
    T, RW, LP, tr = cfg.T, cfg.RW, cfg.LP, cfg.tr
    ind, ind_t, pad = _head_indicators(cfg)
    cols, cur, prv = _rwkv_specs(cfg, tr)

    def body(*refs):
        u_refs, p_refs = refs[0:6], refs[6:12]
        mu_ref, w0_ref, w2_ref, a0_ref, a2_ref, kk_ref, ka_ref, ind_ref, indt_ref, pad_ref = refs[12:22]
        outs = refs[22:]
        first = pl.program_id(0) == 0
        mus = _mu_pieces(cfg, mu_ref)
        us = [_shifted(u_refs[j][...], p_refs[j][7:8, :], mus[j], first)[0] for j in range(6)]
        res = _prep_fn(us[0], us[1], us[2], us[4], us[5], w0_ref[...], w2_ref[...], a0_ref[...], a2_ref[...],
                       kk_ref[...], ka_ref[...], ind_ref[...], indt_ref[...], pad_ref[...])
        for j in range(6):
            outs[j][...] = res[j]
        outs[6][...] = us[3]

    consts = [mu, w0, w2p, a0, a2p, k_k, k_a, ind, ind_t, pad]
    return _pcall(body, name="rwkv_prep_fwd", grid=(T // tr,),
                  in_specs=cur + prv + [_const(c.shape) for c in consts],
                  out_specs=[_tile(tr, RW)] * 7, out_shape=[jax.ShapeDtypeStruct((T, RW), F32)] * 7,
                  compiler_params=_cparams(("parallel",)))(*([u] * 12), *consts)


def _rwkv_prep_bwd(cfg, u, mu, w0, w2p, a0, a2p, k_k, k_a, cots, dzb):
    T, RW, LP = cfg.T, cfg.RW, cfg.LP
    tr = min(128, T)
    ind, ind_t, pad = _head_indicators(cfg)
    cols, cur, prv = _rwkv_specs(cfg, tr)
    rseg = cfg.rseg

    def body(*refs):
        u_refs, p_refs = refs[0:6], refs[6:12]
        mu_ref, w0_ref, w2_ref, a0_ref, a2_ref, kk_ref, ka_ref, ind_ref, indt_ref, pad_ref = refs[12:22]
        cot_refs, dzb_ref = refs[22:28], refs[28]
        dus_ref, dmu_ref, dw0_ref, dw2_ref, da0_ref, da2_ref, dkk_ref, dka_ref = refs[29:]
        i = pl.program_id(0)
        first = i == 0
        mus = _mu_pieces(cfg, mu_ref)
        sh = [_shifted(u_refs[j][...], p_refs[j][7:8, :], mus[j], first) for j in range(6)]
        us = [s[0] for s in sh]
        fn = functools.partial(_prep_fn, ind=ind_ref[...], ind_t=indt_ref[...], pad=pad_ref[...])
        _, vjp = jax.vjp(fn, us[0], us[1], us[2], us[4], us[5], w0_ref[...], w2_ref[...], a0_ref[...], a2_ref[...],
                         kk_ref[...], ka_ref[...])
        d = vjp(tuple(c[...] for c in cot_refs))
        dus = [d[0], d[1], d[2], dzb_ref[...], d[3], d[4]]
        offs = [0, RW, 2 * RW, 3 * RW, 4 * RW, 4 * RW + LP, 4 * RW + 2 * LP]
        for j in range(6):
            dus_ref[:, offs[j]:offs[j + 1]] = dus[j]
            dmu_j = jnp.sum(dus[j] * (sh[j][1] - u_refs[j][...]), axis=0, keepdims=True)

            @pl.when(first)
            def _(j=j, dmu_j=dmu_j):
                dmu_ref[:, offs[j]:offs[j + 1]] = dmu_j

            @pl.when(i > 0)
            def _(j=j, dmu_j=dmu_j):
                dmu_ref[:, offs[j]:offs[j + 1]] += dmu_j
        for ref, val in zip((dw0_ref, dw2_ref, da0_ref, da2_ref, dkk_ref, dka_ref), d[5:11]):
            _acc_store(i, ref, val)

    consts = [mu, w0, w2p, a0, a2p, k_k, k_a, ind, ind_t, pad]
    vec = jax.ShapeDtypeStruct((1, RW), F32)
    mat = jax.ShapeDtypeStruct((LP, RW), F32)
    return _pcall(body, name="rwkv_prep_bwd", grid=(T // tr,),
                  in_specs=cur + prv + [_const(c.shape) for c in consts] + [_tile(tr, RW)] * 7,
                  out_specs=[_tile(tr, rseg), _const((1, rseg)), _const((1, RW)), _const((LP, RW)), _const((1, RW)),
                             _const((LP, RW)), _const((1, RW)), _const((1, RW))],
                  out_shape=[jax.ShapeDtypeStruct((T, rseg), F32), jax.ShapeDtypeStruct((1, rseg), F32),
                             vec, mat, vec, mat, vec, vec],
                  compiler_params=_cparams(("arbitrary",)))(*([u] * 12), *consts, *cots, dzb)


def _shift_bwd(cfg, dus, mu, df, du):
    T, tr, RW, LP = cfg.T, cfg.tr, cfg.RW, cfg.LP
    nb = T // tr
    tail = cfg.ncol - cfg.o_f
    assert cfg.o_rwkv % (4 * RW) == 0 and (4 * RW) % (2 * LP) == 0 and cfg.o_f % tail == 0

    def shifted(d_ref, n_ref, mu_ref):
        d = d_ref[...]
        rolled = pltpu.roll(d, tr - 1, 0)
        row = lax.broadcasted_iota(jnp.int32, d.shape, 0)
        n0 = jnp.where(pl.program_id(0) == nb - 1, jnp.zeros_like(n_ref[0:1, :]), n_ref[0:1, :])
        nxt = jnp.where(row == tr - 1, jnp.broadcast_to(n0, d.shape), rolled)
        mu_v = mu_ref[...]
        return (d * (1.0 - mu_v) + nxt * mu_v).astype(BF16)

    def main_body(d_ref, n_ref, mu_ref, du_in, du_ref):
        du_ref[...] = shifted(d_ref, n_ref, mu_ref)

    def tail_body(d_ref, n_ref, mu_ref, df_ref, du_in, du_ref):
        du_ref[:, 0:LANES] = df_ref[...]
        du_ref[:, LANES:LANES + 2 * LP] = shifted(d_ref, n_ref, mu_ref)
        if tail > LANES + 2 * LP:
            du_ref[:, LANES + 2 * LP:] = jnp.zeros((tr, tail - LANES - 2 * LP), BF16)

    def specs(w, cb):
        return [_tile(tr, w, cb),
                pl.BlockSpec((8, w), lambda i: (jnp.minimum((i + 1) * (tr // 8), T // 8 - 1), cb)),
                pl.BlockSpec((1, w), lambda i: (0, cb))]

    out = jax.ShapeDtypeStruct(du.shape, BF16)
    du = _pcall(main_body, name="shift_bwd_main", grid=(nb,), in_specs=specs(4 * RW, 0) + [_ANY],
                out_specs=_tile(tr, 4 * RW, cfg.o_rwkv // (4 * RW)), out_shape=out, input_output_aliases={3: 0},
                compiler_params=_cparams(("parallel",)))(dus, dus, mu, du)
    return _pcall(tail_body, name="shift_bwd_tail", grid=(nb,),
                  in_specs=specs(2 * LP, 4 * RW // (2 * LP)) + [_tile(tr, LANES), _ANY],
                  out_specs=_tile(tr, tail, cfg.o_f // tail), out_shape=out, input_output_aliases={4: 0},
                  compiler_params=_cparams(("parallel",)))(dus, dus, mu, df, du)


def _chunk_local(r, lw, k, v, a, b):
    H, C, K = r.shape
    row = lax.broadcasted_iota(jnp.int32, (C, C), 0)
    col = lax.broadcasted_iota(jnp.int32, (C, C), 1)
    incl = jnp.broadcast_to((row >= col).astype(F32)[None], (H, C, C))
    strict = (row > col)[None]
    lower = (row >= col)[None]
    eye = (row == col)[None]
    zero = jnp.zeros((), F32)
    L = _bdot(incl, lw, 2, 1)
    LC = jnp.sum(lw, axis=1, keepdims=True)
    eL = jnp.exp(L)
    eLn = jnp.exp(-L)
    at = a * jnp.exp(L - lw)
    rt = r * eL
    bt = b * eLn
    kt = k * eLn
    eR = jnp.exp(LC - L)
    bh = b * eR
    kh = k * eR
    keys = functools.partial(_bdot, passes=SCAN_PASSES[0])
    inv = functools.partial(_bdot, passes=SCAN_PASSES[1])
    app = functools.partial(_bdot, passes=SCAN_PASSES[2])
    ar = jnp.concatenate([at, rt], axis=1)
    g_b = app(ar, bt, 2, 2)
    g_k = keys(ar, kt, 2, 2)
    n_ab = jnp.where(strict, g_b[:, :C], zero)
    n_ak = jnp.where(strict, g_k[:, :C], zero)
    m_rb = jnp.where(lower, g_b[:, C:], zero)
    m_rk = jnp.where(lower, g_k[:, C:], zero)
    M = n_ab
    P = jnp.where(eye, 1.0, zero) + n_ab
    for _ in range(1, max(1, int(np.ceil(np.log2(C))))):
        M = inv(M, M, 2, 1)
        P = P + inv(M, P, 2, 1)
    W = app(P, at, 2, 1)
    Uloc = app(P, app(n_ak, v, 2, 1), 2, 1)
    Q = rt + app(m_rb, W, 2, 1)
    Yloc = app(m_rb, Uloc, 2, 1) + app(m_rk, v, 2, 1)
    A = jnp.where(eye, jnp.exp(LC), zero) + app(W, bh, 1, 1)
    Sloc = app(Uloc, bh, 1, 1) + app(v, kh, 1, 1)
    return Q, Yloc, A, Sloc


def _split_heads(ref, n):
    N = RWKV_HEAD_DIM
    return jnp.stack([ref[:, h * N:(h + 1) * N] for h in range(n)], axis=0)


def _merge_heads(x):
    return jnp.concatenate([x[h] for h in range(x.shape[0])], axis=1)


def _scan_local_specs(cfg):
    N, HB = RWKV_HEAD_DIM, cfg.hb
    grid = (cfg.RH // HB, cfg.T // cfg.C)
    seq = pl.BlockSpec((HB, cfg.C, N), lambda h, j: (h, j, 0))
    mat = pl.BlockSpec((HB, 1, N, N), lambda h, j: (h, j, 0, 0))
    return grid, seq, mat


def _scan_local_fwd(cfg, seqs):
    T, RH, N = cfg.T, cfg.RH, RWKV_HEAD_DIM
    grid, seq, mat = _scan_local_specs(cfg)

    def body(r_ref, lw_ref, k_ref, v_ref, a_ref, b_ref, q_ref, yl_ref, a_out, sl_ref):
        Q, Yloc, A, Sloc = _chunk_local(*[_split_heads(ref, cfg.hb) for ref in (r_ref, lw_ref, k_ref, v_ref, a_ref, b_ref)])
        q_ref[...] = Q
        yl_ref[...] = Yloc
        a_out[:, 0] = A
        sl_ref[:, 0] = Sloc

    tok = pl.BlockSpec((cfg.C, cfg.hb * N), lambda h, j: (j, h))
    sq = jax.ShapeDtypeStruct((RH, T, N), F32)
    mt = jax.ShapeDtypeStruct((RH, T // cfg.C, N, N), F32)
    return _pcall(body, name="rwkv_scan_local_fwd", grid=grid, in_specs=[tok] * 6, out_specs=[seq, seq, mat, mat],
                  out_shape=[sq, sq, mt, mt], compiler_params=_cparams(("parallel", "parallel")))(*seqs)


def _scan_local_bwd(cfg, toks, dq, dy, da, dsl, extra, comm=None):
    T, RW, N = cfg.T, cfg.RW, RWKV_HEAD_DIM
    grid, seq, mat = _scan_local_specs(cfg)
    c_in, c_out, c_scr = comm[:3] if comm else ([], [], [])

    def body(r_ref, lw_ref, k_ref, v_ref, a_ref, b_ref, dq_ref, dy_ref, da_ref, dsl_ref, xr_ref, xk_ref, xv_ref,
             *rest):
        cin, outs = rest[:len(c_in)], rest[len(c_in):len(c_in) + 6]
        cout, scr = rest[len(c_in) + 6:len(c_in) + 6 + len(c_out)], rest[len(c_in) + 6 + len(c_out):]
        _comm_at(comm, 3, grid, cin, cout, scr)
        ins = [_split_heads(ref, cfg.hb) for ref in (r_ref, lw_ref, k_ref, v_ref, a_ref, b_ref)]
        _, vjp = jax.vjp(_chunk_local, *ins)
        d = vjp((dq_ref[...], _split_heads(dy_ref, cfg.hb), da_ref[:, 0], dsl_ref[:, 0]))
        add = {0: xr_ref, 2: xk_ref, 3: xv_ref}
        for j in range(6):
            dj = _merge_heads(d[j])
            outs[j][...] = dj + add[j][...] if j in add else dj
        _comm_at(comm, 4, grid, cin, cout, scr)

    tok = pl.BlockSpec((cfg.C, cfg.hb * N), lambda h, j: (j, h))
    return _pcall(body, name="rwkv_scan_local_bwd", grid=grid,
                  in_specs=[tok] * 6 + [seq, tok, mat, mat] + [tok] * 3 + [_ANY] * len(c_in),
                  out_specs=[tok] * 6 + [_ANY] * len(c_out),
                  out_shape=[jax.ShapeDtypeStruct((T, RW), F32)] * 6 + list(c_out), scratch_shapes=list(c_scr),
                  compiler_params=_cparams(("arbitrary", "arbitrary") if comm else ("parallel", "parallel")),
                  )(*toks, dq, dy, da, dsl, *extra, *c_in)


def _scan_carry_specs(cfg, rev):
    N, RH, C, nc = RWKV_HEAD_DIM, cfg.RH, cfg.C, cfg.T // cfg.C
    at = (lambda j: nc - 1 - j) if rev else (lambda j: j)
    seq = pl.BlockSpec((RH, C, N), lambda j: (0, at(j), 0))
    mat = pl.BlockSpec((RH, 1, N, N), lambda j: (0, at(j), 0, 0))
    return nc, seq, mat


def _scan_carry_fwd(cfg, q, yloc, a, sloc):
    T, RH, N = cfg.T, cfg.RH, RWKV_HEAD_DIM
    nc, seq, mat = _scan_carry_specs(cfg, False)

    def body(q_ref, yl_ref, a_ref, sl_ref, y_ref, ck_ref, s_ref):
        @pl.when(pl.program_id(0) == 0)
        def _():
            s_ref[...] = jnp.zeros_like(s_ref)

        S = s_ref[...]
        ck_ref[:, 0] = S
        y_ref[...] = _merge_heads(_bdot(q_ref[...], S, 2, 2, SCAN_PASSES[2]) + yl_ref[...])
        s_ref[...] = _bdot(S, a_ref[:, 0], 2, 1) + sl_ref[:, 0]

    tok = pl.BlockSpec((cfg.C, cfg.RW), lambda j: (j, 0))
    return _pcall(body, name="rwkv_scan_carry_fwd", grid=(nc,), in_specs=[seq, seq, mat, mat], out_specs=[tok, mat],
                  out_shape=[jax.ShapeDtypeStruct((T, cfg.RW), F32), jax.ShapeDtypeStruct((RH, nc, N, N), F32)],
                  scratch_shapes=[pltpu.VMEM((RH, N, N), F32)],
                  compiler_params=_cparams(("arbitrary",)))(q, yloc, a, sloc)


def _scan_carry_bwd(cfg, q, a, ckpt, dy, comm=None):
    T, RH, N = cfg.T, cfg.RH, RWKV_HEAD_DIM
    nc, seq, mat = _scan_carry_specs(cfg, True)

    def body(q_ref, a_ref, ck_ref, dy_ref, dq_ref, da_ref, dsl_ref, ds_ref):
        @pl.when(pl.program_id(0) == 0)
        def _():
            ds_ref[...] = jnp.zeros_like(ds_ref)

        S, dS, dY = ck_ref[:, 0], ds_ref[...], _split_heads(dy_ref, RH)
        dq_ref[...] = _bdot(dY, S, 2, 1, SCAN_PASSES[2])
        da_ref[:, 0] = _bdot(S, dS, 1, 1, SCAN_PASSES[2])
        dsl_ref[:, 0] = dS
        ds_ref[...] = _bdot(dS, a_ref[:, 0], 2, 2) + _bdot(dY, q_ref[...], 1, 1, SCAN_PASSES[2])

    mt = jax.ShapeDtypeStruct((RH, nc, N, N), F32)
    tok = pl.BlockSpec((cfg.C, cfg.RW), lambda j: (nc - 1 - j, 0))
    body, c_in, c_out, c_shapes, c_scr = _hosted(body, 4, 3, (nc,), comm)
    return _pcall(body, name="rwkv_scan_carry_bwd", grid=(nc,), in_specs=[seq, mat, mat, tok] + c_in,
                  out_specs=[seq, mat, mat] + c_out,
                  out_shape=[jax.ShapeDtypeStruct((RH, T, N), F32), mt, mt] + c_shapes,
                  scratch_shapes=c_scr + [pltpu.VMEM((RH, N, N), F32)],
                  compiler_params=_cparams(("arbitrary",)))(q, a, ckpt, dy, *(comm[0] if comm else []))


def _post_fn(y, r, kp, v, zb, ln_w, ln_b, rk, ind, ind_t):
    n = float(RWKV_HEAD_DIM)
    mu = _xdot(_xdot(y, ind, ind_t) / n, ind_t, ind)
    yc = y - mu
    var = _xdot(yc * yc, ind, ind_t) / n
    rstd = _xdot(lax.rsqrt(var + GN_EPS), ind_t, ind)
    yn = yc * rstd * ln_w + ln_b
    bonus = _xdot(_xdot(r * kp * rk, ind, ind_t), ind_t, ind) * v
    return (yn + bonus) * _silu(zb)


def _rwkv_post_fwd(cfg, y, r, kp, v, zb, ln_w, ln_b, rk):
    T, RW, tr = cfg.T, cfg.RW, cfg.tr
    ind, ind_t, _ = _head_indicators(cfg)

    def body(y_ref, r_ref, k_ref, v_ref, z_ref, lw_ref, lb_ref, rk_ref, ind_ref, indt_ref, ob_ref):
        ob_ref[...] = _post_fn(y_ref[...], r_ref[...], k_ref[...], v_ref[...], z_ref[...], lw_ref[...], lb_ref[...],
                               rk_ref[...], ind_ref[...], indt_ref[...]).astype(BF16)

    consts = [ln_w, ln_b, rk, ind, ind_t]
    return _pcall(body, name="rwkv_post_fwd", grid=(T // tr,),
                  in_specs=[_tile(tr, RW)] * 5 + [_const(c.shape) for c in consts],
                  out_specs=_tile(tr, RW), out_shape=jax.ShapeDtypeStruct((T, RW), BF16),
                  compiler_params=_cparams(("parallel",)))(y, r, kp, v, zb, *consts)


def _rwkv_post_bwd(cfg, y, r, kp, v, zb, ln_w, ln_b, rk, dob):
    T, RW = cfg.T, cfg.RW
    tr = min(128, T)
    ind, ind_t, _ = _head_indicators(cfg)

    def body(y_ref, r_ref, k_ref, v_ref, z_ref, lw_ref, lb_ref, rk_ref, ind_ref, indt_ref, dob_ref,
             dy_ref, dr_ref, dk_ref, dv_ref, dz_ref, dlw_ref, dlb_ref, drk_ref):
        fn = functools.partial(_post_fn, ind=ind_ref[...], ind_t=indt_ref[...])
        _, vjp = jax.vjp(fn, y_ref[...], r_ref[...], k_ref[...], v_ref[...], z_ref[...], lw_ref[...], lb_ref[...],
                         rk_ref[...])
        d = vjp(dob_ref[...])
        for ref, val in zip((dy_ref, dr_ref, dk_ref, dv_ref, dz_ref), d[:5]):
            ref[...] = val
        i = pl.program_id(0)
        for ref, val in zip((dlw_ref, dlb_ref, drk_ref), d[5:8]):
            _acc_store(i, ref, val)

    consts = [ln_w, ln_b, rk, ind, ind_t]
    vec = jax.ShapeDtypeStruct((1, RW), F32)
    return _pcall(body, name="rwkv_post_bwd", grid=(T // tr,),
                  in_specs=[_tile(tr, RW)] * 5 + [_const(c.shape) for c in consts] + [_tile(tr, RW)],
                  out_specs=[_tile(tr, RW)] * 5 + [_const((1, RW))] * 3,
                  out_shape=[jax.ShapeDtypeStruct((T, RW), F32)] * 5 + [vec] * 3,
                  compiler_params=_cparams(("arbitrary",)))(y, r, kp, v, zb, *consts, dob)


def _adamw_math(w, g, m, v):
    m = ADAM_B1 * m + (1.0 - ADAM_B1) * g
    v = ADAM_B2 * v + (1.0 - ADAM_B2) * (g * g)
    m_hat = m / (1.0 - ADAM_B1 ** ADAM_STEP)
    v_hat = v / (1.0 - ADAM_B2 ** ADAM_STEP)
    delta = -ADAM_LR * (m_hat / (jnp.sqrt(v_hat) + ADAM_EPS) + ADAM_WD * w)
    return delta, m, v


def _adamw(name, w, g, m, v, copy_grad=False, comm=None):
    R, Cc = w.shape
    Rp = -(-R // 8) * 8
    tr = Rp
    for nb in range(1, Rp // 8 + 1):
        if (Rp // 8) % nb == 0 and (Rp // nb) * Cc * 4 <= 2 * 1024 * 1024:
            tr = Rp // nb
            break

    def body(w_ref, g_ref, m_ref, v_ref, d_ref, nm_ref, nv_ref, *g_out):
        g_v = g_ref[...]
        d, nm, nv = _adamw_math(w_ref[...], g_v, m_ref[...], v_ref[...])
        d_ref[...] = d
        nm_ref[...] = nm
        nv_ref[...] = nv
        if copy_grad:
            g_out[0][...] = g_v

    spec = _tile(tr, Cc)
    n_out = 4 if copy_grad else 3
    body, c_in, c_out, c_shapes, c_scr = _hosted(body, 4, n_out, (Rp // tr,), comm)
    aliases = {4 + k: n_out + k for k in range(len(c_in))} if comm and len(comm) > 6 and comm[6] else {}
    return _pcall(body, name=name, grid=(Rp // tr,), in_specs=[spec] * 4 + c_in, out_specs=[spec] * n_out + c_out,
                  out_shape=[jax.ShapeDtypeStruct((R, Cc), F32)] * n_out + c_shapes, scratch_shapes=c_scr,
                  input_output_aliases=aliases,
                  compiler_params=_cparams(("arbitrary",) if comm else ("parallel",)),
                  )(w, g, m, v, *(comm[0] if comm else []))


def _row_tile(R, Cc, itemsize, budget=2 * 1024 * 1024):
    for nb in range(1, R // 16 + 1):
        if R % nb == 0 and (R // nb) % 16 == 0 and (R // nb) * Cc * itemsize <= budget:
            return R // nb
    return R


def _add_halves(name, gs, r1, c_idx):
    S, R, Cc = gs.shape
    half = R // 2
    tr = _row_tile(half, Cc, 4)
    nb = half // tr

    def body(c_ref, g_ref, r_ref, o_ref):
        o_ref[...] = (g_ref[...].astype(F32) + r_ref[...].astype(F32)).astype(BF16)

    grid_spec = pltpu.PrefetchScalarGridSpec(
        num_scalar_prefetch=1, grid=(S, nb),
        in_specs=[pl.BlockSpec((1, tr, Cc), lambda s, i, c: (s, c[0] * nb + i, 0)),
                  pl.BlockSpec((1, tr, Cc), lambda s, i, c: (s, i, 0))],
        out_specs=pl.BlockSpec((1, tr, Cc), lambda s, i, c: (s, i, 0)))
    return _pcall(body, name=name, grid_spec=grid_spec, out_shape=jax.ShapeDtypeStruct((S, half, Cc), BF16),
                  compiler_params=_cparams(("parallel", "parallel")))(c_idx, gs, r1)


def _sum_slots(name, r2):
    S, R, Cc = r2.shape
    tr = _row_tile(R, Cc, 4 * S // 2 if r2.dtype == BF16 else 4 * S)

    def body(r_ref, o_ref):
        acc = r_ref[0].astype(F32)
        for s in range(1, S):
            acc = acc + r_ref[s].astype(F32)
        o_ref[...] = acc

    return _pcall(body, name=name, grid=(R // tr,), in_specs=[pl.BlockSpec((S, tr, Cc), lambda i: (0, i, 0))],
                  out_specs=_tile(tr, Cc), out_shape=jax.ShapeDtypeStruct((R, Cc), F32),
                  compiler_params=_cparams(("parallel",)))(r2)


def _sum_chips(name, recv, own, place):
    S, H, Cc = recv.shape
    tr = _row_tile(H, Cc, 4, 1024 * 1024)
    nb = H // tr

    def body(p_ref, r_ref, own_ref, o_ref):
        s = pl.program_id(1)
        me = p_ref[0]

        @pl.when(s == 0)
        def _():
            o_ref[...] = jnp.zeros_like(o_ref)

        @pl.when(s == me)
        def _():
            o_ref[...] += own_ref[0].astype(F32)

        @pl.when(s != me)
        def _():
            o_ref[...] += r_ref[0].astype(F32)

    grid_spec = pltpu.PrefetchScalarGridSpec(
        num_scalar_prefetch=1, grid=(nb, S),
        in_specs=[pl.BlockSpec((1, tr, Cc), lambda i, s, p: (jnp.where(s == p[0], (s + 1) % S, s), i, 0)),
                  pl.BlockSpec((1, tr, Cc), lambda i, s, p: (p[0], i, 0))],
        out_specs=pl.BlockSpec((tr, Cc), lambda i, s, p: (p[1] * nb + i, 0)))
    return _pcall(body, name=name, grid_spec=grid_spec, out_shape=jax.ShapeDtypeStruct((2 * H, Cc), F32),
                  compiler_params=_cparams(("parallel", "arbitrary")))(place, recv, own)


def _cast_bf16(name, w):
    R, Cc = w.shape
    tr = _row_tile(R, Cc, 4)

    def body(w_ref, o_ref):
        o_ref[...] = w_ref[...].astype(BF16)

    return _pcall(body, name=name, grid=(R // tr,), in_specs=[_tile(tr, Cc)], out_specs=_tile(tr, Cc),
                  out_shape=jax.ShapeDtypeStruct((R, Cc), BF16), compiler_params=_cparams(("parallel",)))(w)


_ANY = pl.BlockSpec(memory_space=pl.ANY)


def _place():
    x, y, c = lax.axis_index("x"), lax.axis_index("y"), lax.axis_index("c")
    others = [(1 - x, y), (x, 1 - y), (1 - x, 1 - y)]
    return x, y, c, others


def _gather_parts(shards):
    n = len(shards)
    halves = [s.shape[0] // 2 for s in shards]

    def parts(ins, outs, sems):
        x, y, c, _ = _place()
        me = 2 * x + y
        n1 = (x ^ (1 - c), y ^ c)
        n2 = (x ^ c, y ^ (1 - c))
        s1, s2, sd = 2 * n1[0] + n1[1], 2 * n2[0] + n2[1], 2 * (1 - x) + (1 - y)
        sib = (x, y, 1 - c)

        def rows(k, chip, hc):
            return outs[k].at[chip, pl.ds(hc * halves[k], halves[k]), :]

        def remote(k, j, src, dst, to):
            return pltpu.make_async_remote_copy(src_ref=src, dst_ref=dst, send_sem=sems[0].at[7 * k + j],
                                                recv_sem=sems[1].at[7 * k + j], device_id=to, device_id_type=MESH)

        def copy(k, j):
            if j == 6:
                return remote(k, j, ins[k], outs[k].at[me], sib)
            if j < 2:
                mine = ins[k].at[pl.ds(c * halves[k], halves[k]), :]
                return remote(k, j, mine, rows(k, me, c), (*(n1 if j == 0 else n2), c))
            land = rows(k, {2: s1, 3: s1, 4: s2, 5: sd}[j], c)
            return remote(k, j, land, land, (*n2, c) if j == 2 else sib)

        def arrived(k, j):
            hc = c if j < 3 else 1 - c
            land = outs[k].at[me] if j == 6 else rows(k, {0: s1, 1: s2, 2: sd, 3: s2, 4: s1, 5: sd}[j], hc)
            remote(k, j, land, land, (x, y, c)).wait_recv()

        return copy, arrived

    def start(ins, outs, sems):
        copy, _ = parts(ins, outs, sems)
        for k in range(n):
            copy(k, 0).start()
            copy(k, 1).start()
            copy(k, 6).start()

    def middle(ins, outs, sems):
        copy, arrived = parts(ins, outs, sems)
        for k in range(n):
            arrived(k, 0)
            copy(k, 2).start()
            copy(k, 3).start()
            arrived(k, 1)
            copy(k, 4).start()

    def finish(ins, outs, sems):
        copy, arrived = parts(ins, outs, sems)
        for k in range(n):
            arrived(k, 2)
            copy(k, 5).start()
        for k in range(n):
            for j in (3, 4, 5, 6):
                arrived(k, j)
        for k in range(n):
            for j in range(7):
                copy(k, j).wait_send()

    out_shapes = [jax.ShapeDtypeStruct((N_CHIPS,) + s.shape, s.dtype) for s in shards]
    scratch = [pltpu.SemaphoreType.DMA((7 * n,)), pltpu.SemaphoreType.DMA((7 * n,))]
    return list(shards), out_shapes, scratch, start, finish, middle


def _swap_halves(grads):
    n = len(grads)
    halves = [g.shape[1] // 2 for g in grads]

    def copies(ins, outs, sems):
        x, y, c, _ = _place()
        return [pltpu.make_async_remote_copy(
            src_ref=ins[k].at[:, pl.ds((1 - c) * halves[k], halves[k]), :], dst_ref=outs[k], send_sem=sems[0].at[k],
            recv_sem=sems[1].at[k], device_id=(x, y, 1 - c), device_id_type=MESH) for k in range(n)]

    def start(ins, outs, sems):
        for cp in copies(ins, outs, sems):
            cp.start()

    def finish(ins, outs, sems):
        for cp in copies(ins, outs, sems):
            cp.wait()

    out_shapes = [jax.ShapeDtypeStruct((g.shape[0], h) + g.shape[2:], g.dtype) for g, h in zip(grads, halves)]
    scratch = [pltpu.SemaphoreType.DMA((n,)), pltpu.SemaphoreType.DMA((n,))]
    return list(grads), out_shapes, scratch, start, finish


def _scatter_to_owners(chip_sums):
    n = len(chip_sums)

    def sends(ins, outs, sems):
        x, y, c, others = _place()
        me = 2 * x + y
        return [pltpu.make_async_remote_copy(
            src_ref=ins[k].at[2 * px + py], dst_ref=outs[k].at[me], send_sem=sems[0].at[3 * k + j],
            recv_sem=sems[1].at[3 * k + j], device_id=(px, py, c), device_id_type=MESH)
            for k in range(n) for j, (px, py) in enumerate(others)]

    def start(ins, outs, sems):
        for cp in sends(ins, outs, sems):
            cp.start()

    def finish(ins, outs, sems):
        x, y, c, others = _place()
        for k in range(n):
            for j, (px, py) in enumerate(others):
                land = outs[k].at[2 * px + py]
                pltpu.make_async_remote_copy(src_ref=land, dst_ref=land, send_sem=sems[0].at[3 * k + j],
                                             recv_sem=sems[1].at[3 * k + j], device_id=(x, y, c),
                                             device_id_type=MESH).wait_recv()
        for cp in sends(ins, outs, sems):
            cp.wait_send()

    out_shapes = [jax.ShapeDtypeStruct(g.shape, g.dtype) for g in chip_sums]
    scratch = [pltpu.SemaphoreType.DMA((3 * n,)), pltpu.SemaphoreType.DMA((3 * n,))]
    return list(chip_sums), out_shapes, scratch, start, finish


def _swap_with_sibling(arrays):
    n = len(arrays)

    def copies(ins, outs, sems):
        x, y, c, _ = _place()
        return [pltpu.make_async_remote_copy(src_ref=ins[k], dst_ref=outs[k], send_sem=sems[0].at[k],
                                             recv_sem=sems[1].at[k], device_id=(x, y, 1 - c), device_id_type=MESH)
                for k in range(n)]

    def start(ins, outs, sems):
        for cp in copies(ins, outs, sems):
            cp.start()

    def finish(ins, outs, sems):
        for cp in copies(ins, outs, sems):
            cp.wait()

    out_shapes = [jax.ShapeDtypeStruct(a.shape, a.dtype) for a in arrays]
    scratch = [pltpu.SemaphoreType.DMA((n,)), pltpu.SemaphoreType.DMA((n,))]
    return list(arrays), out_shapes, scratch, start, finish


def _add_pair(name, a, b):
    R, Cc = a.shape
    tr = _row_tile(R, Cc, 4)

    def body(a_ref, b_ref, o_ref):
        o_ref[...] = (a_ref[...].astype(F32) + b_ref[...].astype(F32)).astype(BF16)

    return _pcall(body, name=name, grid=(R // tr,), in_specs=[_tile(tr, Cc)] * 2, out_specs=_tile(tr, Cc),
                  out_shape=jax.ShapeDtypeStruct((R, Cc), BF16), compiler_params=_cparams(("parallel",)))(a, b)


def _second_neighbour():
    x, y, c, _ = _place()
    return (x, y, c), (x ^ c, y ^ (1 - c)), (x ^ (1 - c), y ^ c)


def _scatter_stage1(chip_sums):
    n = len(chip_sums)

    def copies(ins, outs, sems):
        (x, y, c), n2, n1 = _second_neighbour()
        diag = 2 * (1 - x) + (1 - y)
        return [pltpu.make_async_remote_copy(
            src_ref=ins[k].at[slot], dst_ref=outs[2 * k + j], send_sem=sems[0].at[2 * k + j],
            recv_sem=sems[1].at[2 * k + j], device_id=(*n2, c), device_id_type=MESH)
            for k in range(n) for j, slot in enumerate((2 * n2[0] + n2[1], diag))]

    def start(ins, outs, sems):
        for cp in copies(ins, outs, sems):
            cp.start()

    def finish(ins, outs, sems):
        for cp in copies(ins, outs, sems):
            cp.wait()

    out_shapes = [jax.ShapeDtypeStruct(g.shape[1:], g.dtype) for g in chip_sums for _ in range(2)]
    scratch = [pltpu.SemaphoreType.DMA((2 * n,)), pltpu.SemaphoreType.DMA((2 * n,))]
    return list(chip_sums), out_shapes, scratch, start, finish


def _scatter_stage2(passed):
    n = len(passed)

    def copies(ins, outs, sems):
        (x, y, c), n2, n1 = _second_neighbour()
        return [pltpu.make_async_remote_copy(src_ref=ins[k], dst_ref=outs[k], send_sem=sems[0].at[k],
                                             recv_sem=sems[1].at[k], device_id=(*n1, c), device_id_type=MESH)
                for k in range(n)]

    def start(ins, outs, sems):
        for cp in copies(ins, outs, sems):
            cp.start()

    def finish(ins, outs, sems):
        for cp in copies(ins, outs, sems):
            cp.wait()

    out_shapes = [jax.ShapeDtypeStruct(p.shape, p.dtype) for p in passed]
    scratch = [pltpu.SemaphoreType.DMA((n,)), pltpu.SemaphoreType.DMA((n,))]
    return list(passed), out_shapes, scratch, start, finish


def _add_passed(name, own, got, slot):
    _, H, Cc = own.shape
    tr = _row_tile(H, Cc, 4)

    def body(s_ref, o_ref, g_ref, out_ref):
        out_ref[...] = (o_ref[0].astype(F32) + g_ref[...].astype(F32)).astype(BF16)

    grid_spec = pltpu.PrefetchScalarGridSpec(
        num_scalar_prefetch=1, grid=(H // tr,),
        in_specs=[pl.BlockSpec((1, tr, Cc), lambda i, s: (s[0], i, 0)), pl.BlockSpec((tr, Cc), lambda i, s: (i, 0))],
        out_specs=pl.BlockSpec((tr, Cc), lambda i, s: (i, 0)))
    return _pcall(body, name=name, grid_spec=grid_spec, out_shape=jax.ShapeDtypeStruct((H, Cc), BF16),
                  compiler_params=_cparams(("parallel",)))(slot, own, got)


def _sum_stages(name, own, direct, via, place, transposed=False):
    _, H, Cc = own.shape
    tr = LANES if transposed else _row_tile(H, Cc, 4, 1024 * 1024)
    nb = H // tr

    def body(p_ref, own_ref, d_ref, v_ref, o_ref):
        acc = (own_ref[0].astype(F32) + d_ref[...].astype(F32)) + v_ref[...].astype(F32)
        o_ref[...] = acc.T if transposed else acc

    flat = pl.BlockSpec((tr, Cc), lambda i, p: (i, 0))
    out_spec = (pl.BlockSpec((Cc, tr), lambda i, p: (0, p[1] * nb + i)) if transposed
                else pl.BlockSpec((tr, Cc), lambda i, p: (p[1] * nb + i, 0)))
    grid_spec = pltpu.PrefetchScalarGridSpec(
        num_scalar_prefetch=1, grid=(nb,),
        in_specs=[pl.BlockSpec((1, tr, Cc), lambda i, p: (p[0], i, 0)), flat, flat], out_specs=out_spec)
    return _pcall(body, name=name, grid_spec=grid_spec,
                  out_shape=jax.ShapeDtypeStruct((Cc, 2 * H) if transposed else (2 * H, Cc), F32),
                  compiler_params=_cparams(("parallel",)))(place, own, direct, via)


def _join_parts(fulls, axes):
    n = len(fulls)
    hs = [f.shape[ax] // 2 for f, ax in zip(fulls, axes)]

    def half(ref, k, hc):
        part = pl.ds(hc * hs[k], hs[k])
        return ref.at[:, part] if axes[k] else ref.at[part, :]

    def copies(ins, outs, sems):
        x, y, c, _ = _place()
        return [pltpu.make_async_remote_copy(src_ref=half(ins[k], k, c), dst_ref=half(outs[k], k, c),
                                             send_sem=sems[0].at[k], recv_sem=sems[1].at[k],
                                             device_id=(x, y, 1 - c), device_id_type=MESH) for k in range(n)]

    def start(ins, outs, sems):
        for cp in copies(ins, outs, sems):
            cp.start()

    def finish(ins, outs, sems):
        x, y, c, _ = _place()
        for k in range(n):
            land = half(outs[k], k, 1 - c)
            pltpu.make_async_remote_copy(src_ref=land, dst_ref=land, send_sem=sems[0].at[k], recv_sem=sems[1].at[k],
                                         device_id=(x, y, c), device_id_type=MESH).wait_recv()
        for cp in copies(ins, outs, sems):
            cp.wait_send()

    out_shapes = [jax.ShapeDtypeStruct(f.shape, f.dtype) for f in fulls]
    scratch = [pltpu.SemaphoreType.DMA((n,)), pltpu.SemaphoreType.DMA((n,))]
    return list(fulls), out_shapes, scratch, start, finish, None, True


def _join_halves(fulls, axes, small):
    n = len(fulls)
    hs = [f.shape[ax] // 2 for f, ax in zip(fulls, axes)]
    rel = [(dx, dy, dc) for dx in (0, 1) for dy in (0, 1) for dc in (0, 1)][1:]

    def half(ref, k, hc):
        part = pl.ds(hc * hs[k], hs[k])
        return ref.at[:, part] if axes[k] else ref.at[part, :]

    def body(*refs):
        ins, small_in = refs[:n], refs[n]
        outs, small_out = refs[n + 1:2 * n + 1], refs[2 * n + 1]
        send_sems, recv_sems, ssend, srecv, local_sem = refs[2 * n + 2:]
        x, y, c, _ = _place()
        dev = 4 * x + 2 * y + c
        local = pltpu.make_async_copy(small_in, small_out.at[dev], local_sem)
        local.start()
        cps = []
        for k in range(n):
            cp = pltpu.make_async_remote_copy(src_ref=half(ins[k], k, c), dst_ref=half(outs[k], k, c),
                                              send_sem=send_sems.at[k], recv_sem=recv_sems.at[k],
                                              device_id=(x, y, 1 - c), device_id_type=MESH)
            cp.start()
            cps.append(cp)
        for r, (dx, dy, dc) in enumerate(rel):
            cp = pltpu.make_async_remote_copy(src_ref=small_in, dst_ref=small_out.at[dev], send_sem=ssend.at[r],
                                              recv_sem=srecv.at[r], device_id=(x ^ dx, y ^ dy, c ^ dc),
                                              device_id_type=MESH)
            cp.start()
            cps.append(cp)
        for k in range(n):
            land = half(outs[k], k, 1 - c)
            pltpu.make_async_remote_copy(src_ref=land, dst_ref=land, send_sem=send_sems.at[k],
                                         recv_sem=recv_sems.at[k], device_id=(x, y, c), device_id_type=MESH).wait_recv()
        for r, (dx, dy, dc) in enumerate(rel):
            land = small_out.at[4 * (x ^ dx) + 2 * (y ^ dy) + (c ^ dc)]
            pltpu.make_async_remote_copy(src_ref=land, dst_ref=land, send_sem=ssend.at[r], recv_sem=srecv.at[r],
                                         device_id=(x, y, c), device_id_type=MESH).wait_recv()
        for cp in cps:
            cp.wait_send()
        local.wait()

    return _pcall(
        body, name="join_halves", in_specs=[_ANY] * (n + 1), out_specs=[_ANY] * (n + 1),
        out_shape=[jax.ShapeDtypeStruct(f.shape, f.dtype) for f in fulls]
        + [jax.ShapeDtypeStruct((N_DEV,) + small.shape, small.dtype)],
        input_output_aliases={k: k for k in range(n)},
        scratch_shapes=[pltpu.SemaphoreType.DMA((n,)), pltpu.SemaphoreType.DMA((n,)), pltpu.SemaphoreType.DMA((7,)),
                        pltpu.SemaphoreType.DMA((7,)), pltpu.SemaphoreType.DMA],
    )(*fulls, small)


def _local_step(cfg, x2, target, norm_gain, w_my, fb, mu_g, w0, a0, k_k, k_a, r_k, ln_w, ln_b, fng, rest,
                exchange=None, h=None):
    T, D, FW, FH, RW, RH, LP, lora = cfg.T, cfg.D, cfg.FW, cfg.FH, cfg.RW, cfg.RH, cfg.LP, cfg.lora
    fb_p = jnp.pad(fb, ((0, 0), (0, LANES - FH)))
    mu = _rwkv_vec_to_my(cfg, mu_g)
    rk = r_k.reshape(1, RW)
    tm = min(1024, T)

    if h is None:
        h = _rms_fwd(cfg, x2, norm_gain)
    if len(rest) == 2:
        u, *got = _mm("in_proj", h, w_my, "nn", F32, tm, cfg.tn, 2048, comm=rest[0])
        rest = rest[1](got)
    else:
        u = _mm("in_proj", h, w_my, "nn", F32, tm, cfg.tn, 2048)
    w2, a2, wpf, wpr, wout = rest
    w2p = jnp.pad(w2, ((0, LP - lora), (0, 0)))
    a2p = jnp.pad(a2, ((0, LP - lora), (0, 0)))
    c_cols = _fox_prep(cfg, u, fb_p)
    c_rows = c_cols[:, :FH].T.reshape(FH, 1, T)
    o, lse = _attn_fwd(cfg, u, c_rows)
    oa = _gate_a_fwd(cfg, o, u)
    prep = _rwkv_prep_fwd(cfg, u, mu, w0, w2p, a0, a2p, k_k, k_a)
    r, lw, kp, v, an, b, zb = prep
    toks = [r, lw, kp, v, an, b]
    q_s, yloc, a_m, sloc = _scan_local_fwd(cfg, toks)
    y, ckpt = _scan_carry_fwd(cfg, q_s, yloc, a_m, sloc)
    ob = _rwkv_post_fwd(cfg, y, r, kp, v, zb, ln_w, ln_b, rk)
    pa = _mm("proj_fox", oa, wpf, "nn", F32, tm, 1024, 2048)
    pb = _mm("proj_rwkv", ob, wpr, "nn", F32, tm, 1024, 2048)
    m = _merge_fwd(cfg, pa, pb, u)
    mo = _mm("out_proj", m, wout, "nn", F32, tm, 1024, 2048)
    loss8, dres, dres16, d_fng = _final(cfg, x2, mo, fng.reshape(1, D), target)

    dm = _mm("out_proj_dx", dres16, wout, "nt", F32, tm, 1024, 2048)
    d_wout = _mm("out_proj_dw", m, dres16, "tn", BF16, 1024, 1024, 2048)
    dpa, dpb, du = _merge_bwd(cfg, pa, pb, u, dm)
    doa = _mm("proj_fox_dx", dpa, wpf, "nt", F32, tm, 1024, 2048)
    d_wpf = _mm("proj_fox_dw", oa, dpa, "tn", BF16, 1024, 1024, 2048)
    dob = _mm("proj_rwkv_dx", dpb, wpr, "nt", F32, tm, 1024, 2048)
    d_wpr = _mm("proj_rwkv_dw", ob, dpb, "tn", BF16, 1024, 1024, 2048)

    do, du = _gate_a_bwd(cfg, o, u, doa, du)
    du, dcol = _attn_bwd(cfg, u, c_rows, lse, do, du)
    dc = jnp.pad(-dcol.reshape(FH, T).T, ((0, 0), (0, LANES - FH)))
    df, d_fb = _fox_prep_bwd(cfg, u, fb_p, dc)

    dy, dr_p, dk_p, dv_p, dzb, d_lnw, d_lnb, d_rk = _rwkv_post_bwd(cfg, y, r, kp, v, zb, ln_w, ln_b, rk, dob)
    early = dict(w_proj_fox=d_wpf, w_proj_rwkv=d_wpr, w_out=d_wout)
    res = _scan_carry_bwd(cfg, q_s, a_m, ckpt, dy, exchange(early) if exchange else None)
    dq_s, da_m, dsl = res[:3]
    res = _scan_local_bwd(cfg, toks, dq_s, dy, da_m, dsl, [dr_p, dk_p, dv_p],
                          exchange(("swapped", list(res[3:]))) if exchange else None)
    cots, received = res[:6], list(res[6:])
    dus, d_mu, d_w0, d_w2p, d_a0, d_a2p, d_kk, d_ka = _rwkv_prep_bwd(cfg, u, mu, w0, w2p, a0, a2p, k_k, k_a, cots, dzb)
    du = _shift_bwd(cfg, dus, mu, df, du)
    if exchange:
        late = dict(w_in=exchange((h, du, d_w2p[:lora], d_a2p[:lora])))
    else:
        late = dict(w_in=_mm("in_proj_dw", h, du, "tn", BF16, 1024, cfg.tn, 2048), rwkv_w2=d_w2p[:lora],
                    rwkv_a2=d_a2p[:lora])
    tkx = 2 * cfg.tn if cfg.ncol % (2 * cfg.tn) == 0 else cfg.tn
    res = _mm("in_proj_dx", du, w_my, "nt", F32, tm, 1024, tkx, comm=exchange(late) if exchange else None)
    dh = res[0] if exchange else res
    big = dict(early, **late)
    res = _rms_bwd(cfg, x2, norm_gain, dh, dres, exchange(list(res[1:])) if exchange else None)
    gx, d_ng = res[:2]
    received += list(res[2:])

    small = dict(norm_gain=d_ng, fox_forget_bias=d_fb[:, :FH], rwkv_shift_mix=_rwkv_vec_from_my(cfg, d_mu),
                 rwkv_w0=d_w0, rwkv_a0=d_a0, rwkv_k_k=d_kk, rwkv_k_a=d_ka, rwkv_r_k=d_rk, rwkv_ln_w=d_lnw,
                 rwkv_ln_b=d_lnb, final_norm_gain=d_fng)
    return loss8[0, 0], gx, small, big, received


_SMALL = ["norm_gain", "fox_forget_bias", "rwkv_shift_mix", "rwkv_w0", "rwkv_a0", "rwkv_k_k", "rwkv_k_a", "rwkv_r_k",
          "rwkv_ln_w", "rwkv_ln_b", "final_norm_gain"]
_WEIGHTS = ["norm_gain", "w_in", "fox_forget_bias", "rwkv_shift_mix", "rwkv_w0", "rwkv_w2", "rwkv_a0", "rwkv_a2",
            "rwkv_k_k", "rwkv_k_a", "rwkv_r_k", "rwkv_ln_w", "rwkv_ln_b", "w_proj_fox", "w_proj_rwkv", "w_out",
            "final_norm_gain"]


def _pack_small(arrs):
    parts = []
    for a in arrs:
        f = a.reshape(-1)
        parts.append(jnp.pad(f, (0, (-f.shape[0]) % LANES)))
    flat = jnp.concatenate(parts)
    rows = flat.shape[0] // LANES
    flat = jnp.pad(flat, (0, ((-rows) % 8) * LANES))
    return flat.reshape(-1, LANES)


def _unpack_small(packed, shapes):
    flat = packed.reshape(-1)
    out, pos = [], 0
    for s in shapes:
        n = int(np.prod(s))
        out.append(flat[pos:pos + n].reshape(s))
        pos += n + ((-n) % LANES)
    return out


def _shard_major(a, axis):
    parts = jnp.split(a, N_CHIPS, axis=axis)
    return jnp.stack(parts, axis=0)


def kernel(x, norm_gain, w_in, fox_forget_bias, rwkv_shift_mix, rwkv_w0, rwkv_w2, rwkv_a0, rwkv_a2, rwkv_k_k, rwkv_k_a, rwkv_r_k, rwkv_ln_w, rwkv_ln_b, w_proj_fox, w_proj_rwkv, w_out, final_norm_gain, loss_target, m_norm_gain, m_w_in, m_fox_forget_bias, m_rwkv_shift_mix, m_rwkv_w0, m_rwkv_w2, m_rwkv_a0, m_rwkv_a2, m_rwkv_k_k, m_rwkv_k_a, m_rwkv_r_k, m_rwkv_ln_w, m_rwkv_ln_b, m_w_proj_fox, m_w_proj_rwkv, m_w_out, m_final_norm_gain, v_norm_gain, v_w_in, v_fox_forget_bias, v_rwkv_shift_mix, v_rwkv_w0, v_rwkv_w2, v_rwkv_a0, v_rwkv_a2, v_rwkv_k_k, v_rwkv_k_a, v_rwkv_r_k, v_rwkv_ln_w, v_rwkv_ln_b, v_w_proj_fox, v_w_proj_rwkv, v_w_out, v_final_norm_gain):
    args = dict(locals())
    T, D = x.shape[1], x.shape[2]
    lora = rwkv_w2.shape[1]
    cfg = _Cfg(T, D, lora)
    RW = cfg.RW
    c_idx = lax.axis_index("c").astype(jnp.int32).reshape(1)
    me_chip = (2 * lax.axis_index("x") + lax.axis_index("y")).astype(jnp.int32)
    place = jnp.concatenate([me_chip.reshape(1), c_idx])

    w_in_s = w_in[0].astype(BF16)
    lora_s = jnp.concatenate([rwkv_w2[0], rwkv_a2[0]], axis=0)
    h, g_in = _rms_fwd(cfg, x[0], norm_gain, _gather_parts([w_in_s]))
    w_my = _shards_to_my_layout(cfg, g_in)
    mine = [_cast_bf16("cast_w_proj_fox", w_proj_fox[0]), _cast_bf16("cast_w_proj_rwkv", w_proj_rwkv[0]),
            _cast_bf16("cast_w_out", w_out[0]), lora_s]

    def unpack(gathered):
        g_wpf, g_wpr, g_out, g_lora = gathered
        lo = g_lora.transpose(1, 0, 2).reshape(2 * lora, RW)
        return (lo[:lora], lo[lora:], g_wpf.transpose(1, 0, 2).reshape(RW, D),
                g_wpr.transpose(1, 0, 2).reshape(RW, D), g_out.reshape(D, D))

    early, late = ["w_proj_fox", "w_proj_rwkv", "w_out"], ["w_in", "lora"]
    names = early + late
    chip_sums, direct, shard_major = {}, {}, []
    n1_slot = (2 * (lax.axis_index("x") ^ (1 - lax.axis_index("c")))
               + (lax.axis_index("y") ^ lax.axis_index("c"))).astype(jnp.int32).reshape(1)

    def exchange(got):
        if isinstance(got, tuple) and len(got) == 4:
            h, du, d_w2, d_a2 = got
            c, half = lax.axis_index("c"), D // 2
            cols = lambda base: lax.dynamic_slice_in_dim(h, base * half, half, axis=1)
            lora_g = _shard_major(jnp.concatenate([d_w2, d_a2], axis=0).astype(BF16), 1)
            lora_rows = lambda base: lax.dynamic_slice_in_dim(lora_g, base * lora, lora, axis=1).reshape(-1, RW // 4)
            tiles = (BF16, min(1024, half), cfg.tn, 2048)
            sent = _mm("in_proj_dw_sibling", cols(1 - c), du, "tn", *tiles)
            kept, got_w, got_l = _mm("in_proj_dw", cols(c), du, "tn", *tiles,
                                     comm=_swap_with_sibling([sent, lora_rows(1 - c)]))
            return (_add_pair("add_halves_w_in", kept, got_w),
                    _add_pair("add_halves_lora", lora_rows(c), got_l).reshape(N_CHIPS, lora, RW // 4))
        if isinstance(got, dict):
            if "w_in" in got:
                sums = [_my_layout_to_shards(cfg, got["w_in"][0]), got["w_in"][1]]
                chip_sums.update(zip(late, sums))
                return _scatter_stage1(sums)
            shard_major.extend([_shard_major(got["w_proj_fox"], 1), _shard_major(got["w_proj_rwkv"], 1),
                                _shard_major(got["w_out"], 0)])
            return _swap_halves(shard_major)
        if got[0] == "swapped":
            sums = [_add_halves("add_halves_" + nm, g, r, c_idx) for nm, g, r in zip(early, shard_major, got[1])]
            chip_sums.update(zip(early, sums))
            return _scatter_to_owners(sums)
        direct.update(zip(late, got[0::2]))
        return _scatter_stage2([_add_passed("add_passed_" + nm, chip_sums[nm], g, n1_slot)
                                for nm, g in zip(late, got[1::2])])

    loss_dev, gx, small, _, recv2 = _local_step(
        cfg, x[0], loss_target[0], norm_gain, w_my, fox_forget_bias, rwkv_shift_mix, rwkv_w0, rwkv_a0, rwkv_k_k,
        rwkv_k_a, rwkv_r_k, rwkv_ln_w, rwkv_ln_b, final_norm_gain, (_gather_parts(mine), unpack), exchange, h)
    loss = lax.psum(loss_dev, ("x", "y", "c"))

    small_shapes = [args[nm].shape for nm in _SMALL]
    packed = _pack_small([small[nm] for nm in _SMALL])
    reduced = [_sum_chips("sum_chips_" + nm, r, chip_sums[nm], place) for nm, r in zip(early, recv2[:3])]
    half_late = [_sum_stages("sum_stages_" + nm, chip_sums[nm], direct[nm], via, place, transposed=nm == "w_in")
                 for nm, via in zip(late, recv2[3:])]
    *joined, small_all = _join_halves(reduced, [0] * len(early), packed)
    g_small = _sum_slots("sum_small", small_all)
    grads = dict(zip(_SMALL, _unpack_small(g_small, small_shapes)))
    grads.update({nm: g[None] for nm, g in zip(early, joined)})

    delta, new_m, new_v = {}, {}, {}

    def adamw(nm, g2d, comm=None):
        shp = args[nm].shape
        two_d = (shp[1], shp[2])
        res = _adamw("adamw_" + nm, args[nm].reshape(two_d), g2d, args["m_" + nm].reshape(two_d),
                     args["v_" + nm].reshape(two_d), comm=comm)
        delta[nm], new_m[nm], new_v[nm] = [t.reshape(shp) for t in res[:3]]
        return res[3:]

    g_in_t, g_lora_f = adamw(early[0], grads[early[0]][0], _join_parts(half_late, [1, 0]))
    for nm in early[1:]:
        adamw(nm, grads[nm][0])
    grads["rwkv_w2"], grads["rwkv_a2"] = g_lora_f[None, :lora], g_lora_f[None, lora:]
    adamw("rwkv_w2", g_lora_f[:lora])
    adamw("rwkv_a2", g_lora_f[lora:])
    t_out = _adamw("adamw_w_in", w_in[0].T, g_in_t, m_w_in[0].T, v_w_in[0].T, copy_grad=True)
    delta["w_in"], new_m["w_in"], new_v["w_in"], grads["w_in"] = [t.T[None] for t in t_out]
    w_small = _pack_small([args[nm] for nm in _SMALL])
    m_small = _pack_small([args["m_" + nm] for nm in _SMALL])
    v_small = _pack_small([args["v_" + nm] for nm in _SMALL])
    d_s, m_s, v_s = _adamw("adamw_small", w_small, g_small, m_small, v_small)
    for tgt, pk in ((delta, d_s), (new_m, m_s), (new_v, v_s)):
        tgt.update(zip(_SMALL, _unpack_small(pk, small_shapes)))

    return (loss, gx[None], *[grads[n] for n in _WEIGHTS], *[delta[n] for n in _WEIGHTS],
            *[new_m[n] for n in _WEIGHTS], *[new_v[n] for n in _WEIGHTS])
```

```python
import functools

import numpy as np
import jax
import jax.numpy as jnp
from jax import lax
from jax.experimental import pallas as pl
from jax.experimental.pallas import tpu as pltpu

F32 = jnp.float32
BF16 = jnp.bfloat16
HI = lax.Precision.HIGHEST
MESH = pl.DeviceIdType.MESH

FOX_HEAD_DIM = 128
RWKV_HEAD_DIM = 64
RMS_EPS = 1e-6
GN_EPS = 64e-5
L2_EPS = 1e-12
ADAM_LR = 0.001
ADAM_B1 = 0.9
ADAM_B2 = 0.999
ADAM_EPS = 1e-08
ADAM_WD = 0.01
ADAM_STEP = 10

LANES = 128
VMEM_LIMIT = 56 * 1024 * 1024
SCAN_CHUNK = 64
SCAN_HEADS_PER_STEP = 16
SCAN_PASSES = (3, 1, 1)
N_CHIPS = 4
N_DEV = 8

_pcall = pl.pallas_call


def _cparams(sem=None):
    return pltpu.CompilerParams(dimension_semantics=sem, vmem_limit_bytes=VMEM_LIMIT)


def _softplus(x):
    return jnp.maximum(x, 0.0) + jnp.log(1.0 + jnp.exp(-jnp.abs(x)))


def _silu(z):
    return z * jax.nn.sigmoid(z)


def _rmsn(x, g):
    return x * lax.rsqrt(jnp.mean(x * x, axis=-1, keepdims=True) + RMS_EPS) * g


def _dot(a, b, dims="nn", precision=None):
    dn = {"nn": (((1,), (0,)), ((), ())), "nt": (((1,), (1,)), ((), ())), "tn": (((0,), (0,)), ((), ()))}[dims]
    return lax.dot_general(a, b, dn, precision=precision, preferred_element_type=F32)


def _split_bf16(x):
    hi = x.astype(BF16)
    return hi, (x - hi.astype(F32)).astype(BF16)


def _bdot_raw(a, b, ca, cb, passes):
    dn = (((ca,), (cb,)), ((0,), (0,)))
    mm = lambda p, q: lax.dot_general(p, q, dn, preferred_element_type=F32)
    if passes == 1:
        return mm(a.astype(BF16), b.astype(BF16))
    ah, al = _split_bf16(a)
    bh, bl = _split_bf16(b)
    return mm(ah, bh) + (mm(ah, bl) + mm(al, bh))


@functools.partial(jax.custom_vjp, nondiff_argnums=(2, 3, 4))
def _bdot_p(a, b, ca, cb, passes):
    return _bdot_raw(a, b, ca, cb, passes)


def _bdot_fwd(a, b, ca, cb, passes):
    return _bdot_raw(a, b, ca, cb, passes), (a, b)


def _bdot_bwd(ca, cb, passes, res, g):
    a, b = res
    if (ca, cb) == (2, 1):
        return _bdot_p(g, b, 2, 2, passes), _bdot_p(a, g, 1, 1, passes)
    if (ca, cb) == (2, 2):
        return _bdot_p(g, b, 2, 1, passes), _bdot_p(g, a, 1, 1, passes)
    assert (ca, cb) == (1, 1)
    return _bdot_p(b, g, 2, 2, passes), _bdot_p(a, g, 2, 1, passes)


_bdot_p.defvjp(_bdot_fwd, _bdot_bwd)


def _bdot(a, b, ca, cb, passes=3):
    return _bdot_p(a, b, ca, cb, passes)


def _dot3(a, b):
    return _bdot(a[None], b[None], 2, 1)[0]


@jax.custom_vjp
def _xdot(x, m, mt):
    hi, lo = _split_bf16(x)
    m16 = m.astype(BF16)
    return _dot(hi, m16) + _dot(lo, m16)


def _xdot_fwd(x, m, mt):
    return _xdot(x, m, mt), (m, mt)


def _xdot_bwd(res, g):
    m, mt = res
    return _xdot(g, mt, m), jnp.zeros_like(m), jnp.zeros_like(mt)


_xdot.defvjp(_xdot_fwd, _xdot_bwd)


class _Cfg:
    def __init__(self, T, D, lora):
        self.T, self.D, self.lora = T, D, lora
        self.FW = D // 2
        self.FH = self.FW // FOX_HEAD_DIM
        self.RW = D // 2
        self.RH = self.RW // RWKV_HEAD_DIM
        self.LP = -(-lora // LANES) * LANES
        self.o_fox = 0
        self.o_rwkv = 4 * self.FW
        self.o_gate = self.o_rwkv + 4 * self.RW
        self.o_f = self.o_gate + 2 * D
        self.o_wd = self.o_f + LANES
        self.o_ad = self.o_wd + self.LP
        end = self.o_ad + self.LP
        self.tn = 1280 if D >= 2048 else LANES
        self.ncol = -(-end // self.tn) * self.tn
        self.in_cols = 4 * self.FW + self.FH + 4 * self.RW + 2 * lora + 2 * D
        self.scp = -(-(self.in_cols // N_CHIPS) // LANES) * LANES
        self.rseg = 4 * self.RW + 2 * self.LP
        self.C = min(SCAN_CHUNK, T)
        self.tr = min(256, T)
        self.hb = min(SCAN_HEADS_PER_STEP, self.RH)

    def segments(self):
        FW, FH, RW, lo, D = self.FW, self.FH, self.RW, self.lora, self.D
        g_f = 4 * FW
        g_r = g_f + FH
        g_wd = g_r + 4 * RW
        g_ad = g_wd + lo
        g_g = g_ad + lo
        dh = FOX_HEAD_DIM
        qkv = [(j * FW + h * dh, dh, (3 * h + j) * dh) for h in range(FH) for j in range(3)]
        return qkv + [(3 * FW, FW, 3 * FW), (g_f, FH, self.o_f), (g_r, 4 * RW, self.o_rwkv), (g_wd, lo, self.o_wd),
                      (g_ad, lo, self.o_ad), (g_g, 2 * D, self.o_gate)]


def _shards_to_my_layout(cfg, g):
    R, sc = g.shape[1], g.shape[2]
    segs = sorted(cfg.segments(), key=lambda s: s[2])
    parts, pos = [], 0
    for g0, w, m0 in segs:
        if m0 > pos:
            parts.append(jnp.zeros((R, m0 - pos), g.dtype))
        for s in range(N_CHIPS):
            lo, hi = max(g0, s * sc), min(g0 + w, (s + 1) * sc)
            if lo < hi:
                parts.append(g[s, :, lo - s * sc:hi - s * sc])
        pos = m0 + w
    if cfg.ncol > pos:
        parts.append(jnp.zeros((R, cfg.ncol - pos), g.dtype))
    return jnp.concatenate(parts, axis=1)


def _my_layout_to_shards(cfg, wm):
    sc, R = cfg.in_cols // N_CHIPS, wm.shape[0]
    segs = sorted(cfg.segments(), key=lambda s: s[0])
    shards = []
    for s in range(N_CHIPS):
        parts = []
        for g0, w, m0 in segs:
            lo, hi = max(g0, s * sc), min(g0 + w, (s + 1) * sc)
            if lo < hi:
                parts.append(wm[:, m0 + lo - g0:m0 + hi - g0])
        parts.append(jnp.zeros((R, cfg.scp - sc), wm.dtype))
        shards.append(jnp.concatenate(parts, axis=1))
    return jnp.stack(shards, axis=0)


def _rwkv_vec_to_my(cfg, v):
    RW4, lo, LP = 4 * cfg.RW, cfg.lora, cfg.LP
    z = jnp.zeros((1, LP - lo), v.dtype)
    return jnp.concatenate([v[:, :RW4], v[:, RW4:RW4 + lo], z, v[:, RW4 + lo:], z], axis=1)


def _rwkv_vec_from_my(cfg, v):
    RW4, lo, LP = 4 * cfg.RW, cfg.lora, cfg.LP
    return jnp.concatenate([v[:, :RW4], v[:, RW4:RW4 + lo], v[:, RW4 + LP:RW4 + LP + lo]], axis=1)


def _comm_at(comm, which, steps, cin, cout, scr):
    if not comm or len(comm) <= which:
        return
    lin, total = 0, 1
    for d, n in enumerate(steps):
        lin = lin * n + pl.program_id(d)
        total *= n
    pl.when(lin == {3: 0, 4: total - 1, 5: total // 2}[which])(lambda: comm[which](cin, cout, scr))


def _hosted(body, n_in, n_out, steps, comm):
    if not comm:
        return body, [], [], [], []
    ci, co, cs = len(comm[0]), len(comm[1]), len(comm[2])

    def wrapped(*refs):
        ins, cin = refs[:n_in], refs[n_in:n_in + ci]
        outs, cout = refs[n_in + ci:n_in + ci + n_out], refs[n_in + ci + n_out:n_in + ci + n_out + co]
        cscr, scr = refs[n_in + ci + n_out + co:n_in + ci + n_out + co + cs], refs[n_in + ci + n_out + co + cs:]
        _comm_at(comm, 3, steps, cin, cout, cscr)
        body(*ins, *outs, *scr)
        _comm_at(comm, 5, steps, cin, cout, cscr)
        _comm_at(comm, 4, steps, cin, cout, cscr)

    return wrapped, [_ANY] * ci, [_ANY] * co, list(comm[1]), list(comm[2])


def _mm(name, a, b, dims, out_dtype, tm, tn, tk, comm=None):
    (M, K) = a.shape if dims != "tn" else a.shape[::-1]
    N = b.shape[0] if dims == "nt" else b.shape[1]
    tm, tn, tk = min(tm, M), min(tn, N), min(tk, K)
    assert M % tm == 0 and N % tn == 0 and K % tk == 0, (name, M, N, K, tm, tn, tk)
    nk = K // tk
    steps = (M // tm, N // tn, nk)
    c_in, c_out, c_scr = comm[:3] if comm else ([], [], [])
    if dims == "nn":
        a_spec = pl.BlockSpec((tm, tk), lambda i, j, k: (i, k))
        b_spec = pl.BlockSpec((tk, tn), lambda i, j, k: (k, j))
    elif dims == "nt":
        a_spec = pl.BlockSpec((tm, tk), lambda i, j, k: (i, k))
        b_spec = pl.BlockSpec((tn, tk), lambda i, j, k: (j, k))
    else:
        a_spec = pl.BlockSpec((tk, tm), lambda i, j, k: (k, i))
        b_spec = pl.BlockSpec((tk, tn), lambda i, j, k: (k, j))

    n_acc = 1 if nk > 1 else 0

    def body(a_ref, b_ref, *rest):
        cin, o_ref = rest[:len(c_in)], rest[len(c_in)]
        cout = rest[len(c_in) + 1:len(c_in) + 1 + len(c_out)]
        scr = rest[len(c_in) + 1 + len(c_out):]
        _comm_at(comm, 3, steps, cin, cout, scr[n_acc:])
        if nk == 1:
            o_ref[...] = _dot(a_ref[...], b_ref[...], dims).astype(o_ref.dtype)
        else:
            acc_ref, k = scr[0], pl.program_id(2)

            @pl.when(k == 0)
            def _():
                acc_ref[...] = jnp.zeros_like(acc_ref)

            acc_ref[...] += _dot(a_ref[...], b_ref[...], dims)

            @pl.when(k == nk - 1)
            def _():
                o_ref[...] = acc_ref[...].astype(o_ref.dtype)

        _comm_at(comm, 5, steps, cin, cout, scr[n_acc:])
        _comm_at(comm, 4, steps, cin, cout, scr[n_acc:])

    res = _pcall(
        body, name=name, grid=steps,
        in_specs=[a_spec, b_spec] + [_ANY] * len(c_in),
        out_specs=[pl.BlockSpec((tm, tn), lambda i, j, k: (i, j))] + [_ANY] * len(c_out),
        out_shape=[jax.ShapeDtypeStruct((M, N), out_dtype)] + list(c_out),
        scratch_shapes=([pltpu.VMEM((tm, tn), F32)] if nk > 1 else []) + list(c_scr),
        compiler_params=_cparams(("arbitrary",) * 3 if comm else ("parallel", "parallel", "arbitrary")),
    )(a, b, *c_in)
    return res if comm else res[0]


def _tile(tr, w, cb=0):
    return pl.BlockSpec((tr, w), lambda i: (i, cb))


def _const(shape):
    nd = len(shape)
    return pl.BlockSpec(shape, lambda i: (0,) * nd)


def _acc_store(i, ref, val):
    @pl.when(i == 0)
    def _():
        ref[...] = val

    @pl.when(i > 0)
    def _():
        ref[...] += val


def _rms_fwd(cfg, x2, g, comm=None):
    T, D, tr = cfg.T, cfg.D, cfg.tr
    steps = (T // tr,)

    def body(x_ref, g_ref, h_ref):
        h_ref[...] = _rmsn(x_ref[...], g_ref[...]).astype(BF16)

    body, c_in, c_out, c_shapes, c_scr = _hosted(body, 2, 1, steps, comm)
    res = _pcall(body, name="rms_fwd", grid=steps, in_specs=[_tile(tr, D), _const((1, D))] + c_in,
                 out_specs=[_tile(tr, D)] + c_out, out_shape=[jax.ShapeDtypeStruct((T, D), BF16)] + c_shapes,
                 scratch_shapes=c_scr, compiler_params=_cparams(("arbitrary",) if comm else ("parallel",)),
                 )(x2, g, *(comm[0] if comm else []))
    return res if comm else res[0]


def _rms_bwd(cfg, x2, g, dh, dres, comm=None):
    T, D, tr = cfg.T, cfg.D, cfg.tr
    c_in, c_out, c_scr = comm[:3] if comm else ([], [], [])
    steps = (T // tr,)

    def body(x_ref, g_ref, dh_ref, dres_ref, *rest):
        cin, (gx_ref, dg_ref) = rest[:len(c_in)], rest[len(c_in):len(c_in) + 2]
        cout, scr = rest[len(c_in) + 2:len(c_in) + 2 + len(c_out)], rest[len(c_in) + 2 + len(c_out):]
        _comm_at(comm, 3, steps, cin, cout, scr)
        _, vjp = jax.vjp(_rmsn, x_ref[...], g_ref[...])
        dx, dg = vjp(dh_ref[...])
        gx_ref[...] = dx + dres_ref[...]
        _acc_store(pl.program_id(0), dg_ref, dg)
        _comm_at(comm, 4, steps, cin, cout, scr)

    return _pcall(body, name="rms_bwd", grid=steps,
                  in_specs=[_tile(tr, D), _const((1, D)), _tile(tr, D), _tile(tr, D)] + [_ANY] * len(c_in),
                  out_specs=[_tile(tr, D), _const((1, D))] + [_ANY] * len(c_out),
                  out_shape=[jax.ShapeDtypeStruct((T, D), F32), jax.ShapeDtypeStruct((1, D), F32)] + list(c_out),
                  scratch_shapes=list(c_scr), compiler_params=_cparams(("arbitrary",)))(x2, g, dh, dres, *c_in)


def _final(cfg, x2, mo, fg, target):
    T, D, tr = cfg.T, cfg.D, cfg.tr

    def loss_fn(hres, g, tgt):
        err = _rmsn(hres, g) - tgt
        return 0.5 * jnp.sum(jnp.mean(err * err, axis=-1, keepdims=True), axis=0, keepdims=True)

    def body(x_ref, mo_ref, g_ref, t_ref, loss_ref, dres_ref, dres16_ref, dg_ref):
        hres = x_ref[...] + mo_ref[...]
        loss, vjp = jax.vjp(functools.partial(loss_fn, tgt=t_ref[...]), hres, g_ref[...])
        dres, dg = vjp(jnp.ones((1, 1), F32))
        dres_ref[...] = dres
        dres16_ref[...] = dres.astype(BF16)
        i = pl.program_id(0)
        _acc_store(i, dg_ref, dg)
        _acc_store(i, loss_ref, jnp.broadcast_to(loss, (8, LANES)))

    return _pcall(body, name="final_loss", grid=(T // tr,),
                  in_specs=[_tile(tr, D), _tile(tr, D), _const((1, D)), _tile(tr, D)],
                  out_specs=[_const((8, LANES)), _tile(tr, D), _tile(tr, D), _const((1, D))],
                  out_shape=[jax.ShapeDtypeStruct((8, LANES), F32), jax.ShapeDtypeStruct((T, D), F32),
                             jax.ShapeDtypeStruct((T, D), BF16), jax.ShapeDtypeStruct((1, D), F32)],
                  compiler_params=_cparams(("arbitrary",)))(x2, mo, fg, target)


def _merge_fn(pa, pb, ga, gb):
    return jax.nn.sigmoid(ga) * pa + jax.nn.sigmoid(gb) * pb


def _merge_fwd(cfg, pa, pb, u):
    T, D, tr = cfg.T, cfg.D, cfg.tr
    cga, cgb = cfg.o_gate // D, cfg.o_gate // D + 1

    def body(pa_ref, pb_ref, ga_ref, gb_ref, m_ref):
        m_ref[...] = _merge_fn(pa_ref[...], pb_ref[...], ga_ref[...], gb_ref[...]).astype(BF16)

    return _pcall(body, name="merge_fwd", grid=(T // tr,),
                  in_specs=[_tile(tr, D), _tile(tr, D), _tile(tr, D, cga), _tile(tr, D, cgb)],
                  out_specs=_tile(tr, D), out_shape=jax.ShapeDtypeStruct((T, D), BF16),
                  compiler_params=_cparams(("parallel",)))(pa, pb, u, u)


def _merge_bwd(cfg, pa, pb, u, dm):
    T, D, tr = cfg.T, cfg.D, cfg.tr
    cga, cgb = cfg.o_gate // D, cfg.o_gate // D + 1

    def body(pa_ref, pb_ref, ga_ref, gb_ref, dm_ref, dpa_ref, dpb_ref, dg_ref):
        _, vjp = jax.vjp(_merge_fn, pa_ref[...], pb_ref[...], ga_ref[...], gb_ref[...])
        dpa, dpb, dga, dgb = vjp(dm_ref[...])
        dpa_ref[...] = dpa.astype(BF16)
        dpb_ref[...] = dpb.astype(BF16)
        dg_ref[:, :D] = dga.astype(BF16)
        dg_ref[:, D:] = dgb.astype(BF16)

    return _pcall(body, name="merge_bwd", grid=(T // tr,),
                  in_specs=[_tile(tr, D), _tile(tr, D), _tile(tr, D, cga), _tile(tr, D, cgb), _tile(tr, D)],
                  out_specs=[_tile(tr, D), _tile(tr, D), _tile(tr, 2 * D, cfg.o_gate // (2 * D))],
                  out_shape=[jax.ShapeDtypeStruct((T, D), BF16), jax.ShapeDtypeStruct((T, D), BF16),
                             jax.ShapeDtypeStruct((T, cfg.ncol), BF16)],
                  compiler_params=_cparams(("parallel",)))(pa, pb, u, u, dm)


def _gate_fn(o, z):
    return o * _silu(z)


def _gate_a_fwd(cfg, o, u):
    T, FW, tr = cfg.T, cfg.FW, cfg.tr

    def body(o_ref, z_ref, oa_ref):
        oa_ref[...] = _gate_fn(o_ref[...], z_ref[...]).astype(BF16)

    return _pcall(body, name="gate_a_fwd", grid=(T // tr,), in_specs=[_tile(tr, FW), _tile(tr, FW, 3)],
                  out_specs=_tile(tr, FW), out_shape=jax.ShapeDtypeStruct((T, FW), BF16),
                  compiler_params=_cparams(("parallel",)))(o, u)


def _gate_a_bwd(cfg, o, u, doa, du):
    T, FW, tr = cfg.T, cfg.FW, cfg.tr

    def body(o_ref, z_ref, doa_ref, du_in, do_ref, dz_ref):
        _, vjp = jax.vjp(_gate_fn, o_ref[...], z_ref[...])
        do, dz = vjp(doa_ref[...])
        do_ref[...] = do
        dz_ref[...] = dz.astype(BF16)

    return _pcall(body, name="gate_a_bwd", grid=(T // tr,),
                  in_specs=[_tile(tr, FW), _tile(tr, FW, 3), _tile(tr, FW), _ANY],
                  out_specs=[_tile(tr, FW), _tile(tr, FW, 3)],
                  out_shape=[jax.ShapeDtypeStruct((T, FW), F32), jax.ShapeDtypeStruct(du.shape, BF16)],
                  input_output_aliases={3: 1},
                  compiler_params=_cparams(("parallel",)))(o, u, doa, du)


def _fox_prep(cfg, u, fb):
    T, tr = cfg.T, cfg.tr
    cf = cfg.o_f // LANES

    def body(f_ref, fb_ref, c_ref, carry_ref):
        i = pl.program_id(0)

        @pl.when(i == 0)
        def _():
            carry_ref[...] = jnp.zeros_like(carry_ref)

        lf = -_softplus(-(f_ref[...] + fb_ref[...]))
        r = lax.broadcasted_iota(jnp.int32, (tr, tr), 0)
        c = lax.broadcasted_iota(jnp.int32, (tr, tr), 1)
        tri = (r >= c).astype(F32)
        c_ref[...] = _dot(tri, lf, precision=HI) + carry_ref[...]
        carry_ref[...] += jnp.sum(lf, axis=0, keepdims=True)

    return _pcall(body, name="fox_prep", grid=(T // tr,), in_specs=[_tile(tr, LANES, cf), _const((1, LANES))],
                  out_specs=_tile(tr, LANES), out_shape=jax.ShapeDtypeStruct((T, LANES), F32),
                  scratch_shapes=[pltpu.VMEM((1, LANES), F32)], compiler_params=_cparams(("arbitrary",)))(u, fb)


def _fox_prep_bwd(cfg, u, fb, dc):
    T, tr = cfg.T, cfg.tr
    cf = cfg.o_f // LANES
    nb = T // tr

    def body(f_ref, fb_ref, dc_ref, df_ref, dfb_ref, carry_ref):
        i = pl.program_id(0)

        @pl.when(i == 0)
        def _():
            carry_ref[...] = jnp.zeros_like(carry_ref)

        dc = dc_ref[...]
        r = lax.broadcasted_iota(jnp.int32, (tr, tr), 0)
        c = lax.broadcasted_iota(jnp.int32, (tr, tr), 1)
        triu = (r <= c).astype(F32)
        dlf = _dot(triu, dc, precision=HI) + carry_ref[...]
        carry_ref[...] += jnp.sum(dc, axis=0, keepdims=True)
        dz = dlf * jax.nn.sigmoid(-(f_ref[...] + fb_ref[...]))
        df_ref[...] = dz.astype(BF16)
        _acc_store(i, dfb_ref, jnp.sum(dz, axis=0, keepdims=True))

    rev = lambda i: (nb - 1 - i, 0)
    return _pcall(body, name="fox_prep_bwd", grid=(nb,),
                  in_specs=[pl.BlockSpec((tr, LANES), lambda i: (nb - 1 - i, cf)), _const((1, LANES)),
                            pl.BlockSpec((tr, LANES), rev)],
                  out_specs=[pl.BlockSpec((tr, LANES), rev), _const((1, LANES))],
                  out_shape=[jax.ShapeDtypeStruct((T, LANES), BF16), jax.ShapeDtypeStruct((1, LANES), F32)],
                  scratch_shapes=[pltpu.VMEM((1, LANES), F32)], compiler_params=_cparams(("arbitrary",)))(u, fb, dc)


def _attn_logits(q_ref, k_ref, c_ref, tq, te):
    q = q_ref[...].astype(BF16)
    scale = FOX_HEAD_DIM ** -0.5
    part = lambda k0, k1: _dot(q, k_ref[k0:k1, :].astype(BF16), "nt") * scale - c_ref[0, :, k0:k1]
    row = lax.broadcasted_iota(jnp.int32, (tq, tq), 0)
    col = lax.broadcasted_iota(jnp.int32, (tq, tq), 1)
    own = ((te - tq, te), jnp.where(col <= row, part(te - tq, te), -1e30))
    return [((0, te - tq), part(0, te - tq)), own] if te > tq else [own]


def _per_query_tile(i, nq, tq, fn):
    for ii in range(nq):
        pl.when(i == ii)(functools.partial(fn, (ii + 1) * tq))


def _attn_fwd(cfg, u, c_rows):
    T, FW, FH = cfg.T, cfg.FW, cfg.FH
    tq = min(256, T)
    dh = FOX_HEAD_DIM

    def body(q_ref, k_ref, v_ref, c_ref, o_ref, lse_ref):
        i = pl.program_id(1)

        def tile(te):
            parts = _attn_logits(q_ref, k_ref, c_ref, tq, te)
            m = functools.reduce(jnp.maximum, [jnp.max(s, axis=1, keepdims=True) for _, s in parts])
            l, acc = 0.0, 0.0
            for (k0, k1), s in parts:
                p = jnp.exp(s - m)
                l = l + jnp.sum(p, axis=1, keepdims=True)
                acc = acc + _dot(p.astype(BF16), v_ref[k0:k1, :].astype(BF16))
            o_ref[...] = acc / l
            lse_ref[0] = m + jnp.log(l)

        _per_query_tile(i, T // tq, tq, tile)

    return _pcall(
        body, name="fox_attn_fwd", grid=(FH, T // tq),
        in_specs=[pl.BlockSpec((tq, dh), lambda h, i: (i, 3 * h)), pl.BlockSpec((T, dh), lambda h, i: (0, 3 * h + 1)),
                  pl.BlockSpec((T, dh), lambda h, i: (0, 3 * h + 2)), pl.BlockSpec((1, 1, T), lambda h, i: (h, 0, 0))],
        out_specs=[pl.BlockSpec((tq, dh), lambda h, i: (i, h)), pl.BlockSpec((1, tq, 1), lambda h, i: (h, i, 0))],
        out_shape=[jax.ShapeDtypeStruct((T, FW), F32), jax.ShapeDtypeStruct((FH, T, 1), F32)],
        compiler_params=_cparams(("parallel", "arbitrary")),
    )(u, u, u, c_rows)


def _attn_bwd(cfg, u, c_rows, lse, do, du):
    T, FW, FH = cfg.T, cfg.FW, cfg.FH
    tq = min(256, T)
    nq = T // tq
    dh = FOX_HEAD_DIM
    scale = dh ** -0.5

    def body(q_ref, k_ref, v_ref, c_ref, lse_ref, do_ref, du_in, du_ref, dcol_ref, dk_acc, dv_acc):
        i = pl.program_id(1)

        @pl.when(i == 0)
        def _():
            dk_acc[...] = jnp.zeros_like(dk_acc)
            dv_acc[...] = jnp.zeros_like(dv_acc)
            dcol_ref[...] = jnp.zeros_like(dcol_ref)

        def tile(te):
            lse, q16, do16 = lse_ref[0], q_ref[...].astype(BF16), do_ref[...].astype(BF16)
            probs = [(ks, jnp.exp(s - lse)) for ks, s in _attn_logits(q_ref, k_ref, c_ref, tq, te)]
            dps = [_dot(do16, v_ref[k0:k1, :].astype(BF16), "nt") for (k0, k1), _ in probs]
            delta = sum(jnp.sum(p * dp, axis=1, keepdims=True) for (_, p), dp in zip(probs, dps))
            dq = 0.0
            for ((k0, k1), p), dp in zip(probs, dps):
                ds = p * (dp - delta)
                ds16 = ds.astype(BF16)
                dq = dq + _dot(ds16, k_ref[k0:k1, :].astype(BF16))
                dk_acc[k0:k1, :] += _dot(ds16, q16, "tn") * scale
                dv_acc[k0:k1, :] += _dot(p.astype(BF16), do16, "tn")
                dcol_ref[0, :, k0:k1] += jnp.sum(ds, axis=0, keepdims=True)
            du_ref[te - tq:te, 0:dh] = (dq * scale).astype(BF16)

        _per_query_tile(i, nq, tq, tile)

        @pl.when(i == nq - 1)
        def _():
            du_ref[:, dh:2 * dh] = dk_acc[...].astype(BF16)
            du_ref[:, 2 * dh:3 * dh] = dv_acc[...].astype(BF16)

    return _pcall(
        body, name="fox_attn_bwd", grid=(FH, nq),
        in_specs=[pl.BlockSpec((tq, dh), lambda h, i: (i, 3 * h)), pl.BlockSpec((T, dh), lambda h, i: (0, 3 * h + 1)),
                  pl.BlockSpec((T, dh), lambda h, i: (0, 3 * h + 2)), pl.BlockSpec((1, 1, T), lambda h, i: (h, 0, 0)),
                  pl.BlockSpec((1, tq, 1), lambda h, i: (h, i, 0)), pl.BlockSpec((tq, dh), lambda h, i: (i, h)), _ANY],
        out_specs=[pl.BlockSpec((T, 3 * dh), lambda h, i: (0, h)), pl.BlockSpec((1, 1, T), lambda h, i: (h, 0, 0))],
        out_shape=[jax.ShapeDtypeStruct(du.shape, BF16), jax.ShapeDtypeStruct((FH, 1, T), F32)],
        scratch_shapes=[pltpu.VMEM((T, dh), F32), pltpu.VMEM((T, dh), F32)],
        input_output_aliases={6: 0},
        compiler_params=_cparams(("parallel", "arbitrary")),
    )(u, u, u, c_rows, lse, do, du)


def _head_indicators(cfg):
    ind = np.zeros((cfg.RW, LANES), np.float32)
    ind[np.arange(cfg.RW), np.arange(cfg.RW) // RWKV_HEAD_DIM] = 1.0
    pad = np.zeros((1, LANES), np.float32)
    pad[0, cfg.RH:] = 1.0
    return jnp.asarray(ind), jnp.asarray(ind.T.copy()), jnp.asarray(pad)


def _prep_fn(us_r, us_k, us_v, us_wd, us_ad, w0, w2p, a0, a2p, k_k, k_a, ind, ind_t, pad):
    wpre = w0 + _dot3(jnp.tanh(us_wd), w2p)
    w = -_softplus(-wpre) - 0.5
    lw = -jnp.exp(w)
    a = jax.nn.sigmoid(a0 + _dot3(us_ad, a2p))
    kk = us_k * k_k
    ss = _xdot(kk * kk, ind, ind_t) + pad
    inv = 1.0 / jnp.maximum(jnp.sqrt(ss), L2_EPS)
    kkn = kk * _xdot(inv, ind_t, ind)
    kp = us_k * (1.0 + (a - 1.0) * k_a)
    return us_r, lw, kp, us_v, -kkn, kkn * a


def _shifted(u, prev_row, mu, first):
    n = u.shape[0]
    rolled = pltpu.roll(u, 1, 0)
    row = lax.broadcasted_iota(jnp.int32, u.shape, 0)
    p0 = jnp.where(first, jnp.zeros_like(prev_row), prev_row)
    prev = jnp.where(row == 0, jnp.broadcast_to(p0, u.shape), rolled)
    return u + (prev - u) * mu, prev


def _rwkv_specs(cfg, tr):
    RW, LP = cfg.RW, cfg.LP
    base = cfg.o_rwkv // RW
    cols = [(RW, base), (RW, base + 1), (RW, base + 2), (RW, base + 3), (LP, cfg.o_wd // LP), (LP, cfg.o_ad // LP)]
    cur = [pl.BlockSpec((tr, w), (lambda i, cb=cb: (i, cb))) for w, cb in cols]
    prv = [pl.BlockSpec((8, w), (lambda i, cb=cb: (jnp.maximum(i * (tr // 8) - 1, 0), cb))) for w, cb in cols]
    return cols, cur, prv


def _mu_pieces(cfg, mu_ref):
    RW, LP = cfg.RW, cfg.LP
    offs = [0, RW, 2 * RW, 3 * RW, 4 * RW, 4 * RW + LP, 4 * RW + 2 * LP]
    return [mu_ref[:, offs[j]:offs[j + 1]] for j in range(6)]


def _rwkv_prep_fwd(cfg, u, mu, w0, w2p, a0, a2p, k_k, k_a):
    T, RW, LP, tr = cfg.T, cfg.RW, cfg.LP, cfg.tr
    ind, ind_t, pad = _head_indicators(cfg)
    cols, cur, prv = _rwkv_specs(cfg, tr)

    def body(*refs):
        u_refs, p_refs = refs[0:6], refs[6:12]
        mu_ref, w0_ref, w2_ref, a0_ref, a2_ref, kk_ref, ka_ref, ind_ref, indt_ref, pad_ref = refs[12:22]
        outs = refs[22:]
        first = pl.program_id(0) == 0
        mus = _mu_pieces(cfg, mu_ref)
        us = [_shifted(u_refs[j][...], p_refs[j][7:8, :], mus[j], first)[0] for j in range(6)]
        res = _prep_fn(us[0], us[1], us[2], us[4], us[5], w0_ref[...], w2_ref[...], a0_ref[...], a2_ref[...],
                       kk_ref[...], ka_ref[...], ind_ref[...], indt_ref[...], pad_ref[...])
        for j in range(6):
            outs[j][...] = res[j]
        outs[6][...] = us[3]

    consts = [mu, w0, w2p, a0, a2p, k_k, k_a, ind, ind_t, pad]
    return _pcall(body, name="rwkv_prep_fwd", grid=(T // tr,),
                  in_specs=cur + prv + [_const(c.shape) for c in consts],
                  out_specs=[_tile(tr, RW)] * 7, out_shape=[jax.ShapeDtypeStruct((T, RW), F32)] * 7,
                  compiler_params=_cparams(("parallel",)))(*([u] * 12), *consts)


def _rwkv_prep_bwd(cfg, u, mu, w0, w2p, a0, a2p, k_k, k_a, cots, dzb):
    T, RW, LP = cfg.T, cfg.RW, cfg.LP
    tr = min(128, T)
    ind, ind_t, pad = _head_indicators(cfg)
    cols, cur, prv = _rwkv_specs(cfg, tr)
    rseg = cfg.rseg

    def body(*refs):
        u_refs, p_refs = refs[0:6], refs[6:12]
        mu_ref, w0_ref, w2_ref, a0_ref, a2_ref, kk_ref, ka_ref, ind_ref, indt_ref, pad_ref = refs[12:22]
        cot_refs, dzb_ref = refs[22:28], refs[28]
        dus_ref, dmu_ref, dw0_ref, dw2_ref, da0_ref, da2_ref, dkk_ref, dka_ref = refs[29:]
        i = pl.program_id(0)
        first = i == 0
        mus = _mu_pieces(cfg, mu_ref)
        sh = [_shifted(u_refs[j][...], p_refs[j][7:8, :], mus[j], first) for j in range(6)]
        us = [s[0] for s in sh]
        fn = functools.partial(_prep_fn, ind=ind_ref[...], ind_t=indt_ref[...], pad=pad_ref[...])
        _, vjp = jax.vjp(fn, us[0], us[1], us[2], us[4], us[5], w0_ref[...], w2_ref[...], a0_ref[...], a2_ref[...],
                         kk_ref[...], ka_ref[...])
        d = vjp(tuple(c[...] for c in cot_refs))
        dus = [d[0], d[1], d[2], dzb_ref[...], d[3], d[4]]
        offs = [0, RW, 2 * RW, 3 * RW, 4 * RW, 4 * RW + LP, 4 * RW + 2 * LP]
        for j in range(6):
            dus_ref[:, offs[j]:offs[j + 1]] = dus[j]
            dmu_j = jnp.sum(dus[j] * (sh[j][1] - u_refs[j][...]), axis=0, keepdims=True)

            @pl.when(first)
            def _(j=j, dmu_j=dmu_j):
                dmu_ref[:, offs[j]:offs[j + 1]] = dmu_j

            @pl.when(i > 0)
            def _(j=j, dmu_j=dmu_j):
                dmu_ref[:, offs[j]:offs[j + 1]] += dmu_j
        for ref, val in zip((dw0_ref, dw2_ref, da0_ref, da2_ref, dkk_ref, dka_ref), d[5:11]):
            _acc_store(i, ref, val)

    consts = [mu, w0, w2p, a0, a2p, k_k, k_a, ind, ind_t, pad]
    vec = jax.ShapeDtypeStruct((1, RW), F32)
    mat = jax.ShapeDtypeStruct((LP, RW), F32)
    return _pcall(body, name="rwkv_prep_bwd", grid=(T // tr,),
                  in_specs=cur + prv + [_const(c.shape) for c in consts] + [_tile(tr, RW)] * 7,
                  out_specs=[_tile(tr, rseg), _const((1, rseg)), _const((1, RW)), _const((LP, RW)), _const((1, RW)),
                             _const((LP, RW)), _const((1, RW)), _const((1, RW))],
                  out_shape=[jax.ShapeDtypeStruct((T, rseg), F32), jax.ShapeDtypeStruct((1, rseg), F32),
                             vec, mat, vec, mat, vec, vec],
                  compiler_params=_cparams(("arbitrary",)))(*([u] * 12), *consts, *cots, dzb)


def _shift_bwd(cfg, dus, mu, df, du):
    T, tr, RW, LP = cfg.T, cfg.tr, cfg.RW, cfg.LP
    nb = T // tr
    tail = cfg.ncol - cfg.o_f
    assert cfg.o_rwkv % (4 * RW) == 0 and (4 * RW) % (2 * LP) == 0 and cfg.o_f % tail == 0

    def shifted(d_ref, n_ref, mu_ref):
        d = d_ref[...]
        rolled = pltpu.roll(d, tr - 1, 0)
        row = lax.broadcasted_iota(jnp.int32, d.shape, 0)
        n0 = jnp.where(pl.program_id(0) == nb - 1, jnp.zeros_like(n_ref[0:1, :]), n_ref[0:1, :])
        nxt = jnp.where(row == tr - 1, jnp.broadcast_to(n0, d.shape), rolled)
        mu_v = mu_ref[...]
        return (d * (1.0 - mu_v) + nxt * mu_v).astype(BF16)

    def main_body(d_ref, n_ref, mu_ref, du_in, du_ref):
        du_ref[...] = shifted(d_ref, n_ref, mu_ref)

    def tail_body(d_ref, n_ref, mu_ref, df_ref, du_in, du_ref):
        du_ref[:, 0:LANES] = df_ref[...]
        du_ref[:, LANES:LANES + 2 * LP] = shifted(d_ref, n_ref, mu_ref)
        if tail > LANES + 2 * LP:
            du_ref[:, LANES + 2 * LP:] = jnp.zeros((tr, tail - LANES - 2 * LP), BF16)

    def specs(w, cb):
        return [_tile(tr, w, cb),
                pl.BlockSpec((8, w), lambda i: (jnp.minimum((i + 1) * (tr // 8), T // 8 - 1), cb)),
                pl.BlockSpec((1, w), lambda i: (0, cb))]

    out = jax.ShapeDtypeStruct(du.shape, BF16)
    du = _pcall(main_body, name="shift_bwd_main", grid=(nb,), in_specs=specs(4 * RW, 0) + [_ANY],
                out_specs=_tile(tr, 4 * RW, cfg.o_rwkv // (4 * RW)), out_shape=out, input_output_aliases={3: 0},
                compiler_params=_cparams(("parallel",)))(dus, dus, mu, du)
    return _pcall(tail_body, name="shift_bwd_tail", grid=(nb,),
                  in_specs=specs(2 * LP, 4 * RW // (2 * LP)) + [_tile(tr, LANES), _ANY],
                  out_specs=_tile(tr, tail, cfg.o_f // tail), out_shape=out, input_output_aliases={4: 0},
                  compiler_params=_cparams(("parallel",)))(dus, dus, mu, df, du)


def _chunk_local(r, lw, k, v, a, b):
    H, C, K = r.shape
    row = lax.broadcasted_iota(jnp.int32, (C, C), 0)
    col = lax.broadcasted_iota(jnp.int32, (C, C), 1)
    incl = jnp.broadcast_to((row >= col).astype(F32)[None], (H, C, C))
    strict = (row > col)[None]
    lower = (row >= col)[None]
    eye = (row == col)[None]
    zero = jnp.zeros((), F32)
    L = _bdot(incl, lw, 2, 1)
    LC = jnp.sum(lw, axis=1, keepdims=True)
    eL = jnp.exp(L)
    eLn = jnp.exp(-L)
    at = a * jnp.exp(L - lw)
    rt = r * eL
    bt = b * eLn
    kt = k * eLn
    eR = jnp.exp(LC - L)
    bh = b * eR
    kh = k * eR
    keys = functools.partial(_bdot, passes=SCAN_PASSES[0])
    inv = functools.partial(_bdot, passes=SCAN_PASSES[1])
    app = functools.partial(_bdot, passes=SCAN_PASSES[2])
    ar = jnp.concatenate([at, rt], axis=1)
    g_b = app(ar, bt, 2, 2)
    g_k = keys(ar, kt, 2, 2)
    n_ab = jnp.where(strict, g_b[:, :C], zero)
    n_ak = jnp.where(strict, g_k[:, :C], zero)
    m_rb = jnp.where(lower, g_b[:, C:], zero)
    m_rk = jnp.where(lower, g_k[:, C:], zero)
    M = n_ab
    P = jnp.where(eye, 1.0, zero) + n_ab
    for _ in range(1, max(1, int(np.ceil(np.log2(C))))):
        M = inv(M, M, 2, 1)
        P = P + inv(M, P, 2, 1)
    W = app(P, at, 2, 1)
    Uloc = app(P, app(n_ak, v, 2, 1), 2, 1)
    Q = rt + app(m_rb, W, 2, 1)
    Yloc = app(m_rb, Uloc, 2, 1) + app(m_rk, v, 2, 1)
    A = jnp.where(eye, jnp.exp(LC), zero) + app(W, bh, 1, 1)
    Sloc = app(Uloc, bh, 1, 1) + app(v, kh, 1, 1)
    return Q, Yloc, A, Sloc


def _split_heads(ref, n):
    N = RWKV_HEAD_DIM
    return jnp.stack([ref[:, h * N:(h + 1) * N] for h in range(n)], axis=0)


def _merge_heads(x):
    return jnp.concatenate([x[h] for h in range(x.shape[0])], axis=1)


def _scan_local_specs(cfg):
    N, HB = RWKV_HEAD_DIM, cfg.hb
    grid = (cfg.RH // HB, cfg.T // cfg.C)
    seq = pl.BlockSpec((HB, cfg.C, N), lambda h, j: (h, j, 0))
    mat = pl.BlockSpec((HB, 1, N, N), lambda h, j: (h, j, 0, 0))
    return grid, seq, mat


def _scan_local_fwd(cfg, seqs):
    T, RH, N = cfg.T, cfg.RH, RWKV_HEAD_DIM
    grid, seq, mat = _scan_local_specs(cfg)

    def body(r_ref, lw_ref, k_ref, v_ref, a_ref, b_ref, q_ref, yl_ref, a_out, sl_ref):
        Q, Yloc, A, Sloc = _chunk_local(*[_split_heads(ref, cfg.hb) for ref in (r_ref, lw_ref, k_ref, v_ref, a_ref, b_ref)])
        q_ref[...] = Q
        yl_ref[...] = Yloc
        a_out[:, 0] = A
        sl_ref[:, 0] = Sloc

    tok = pl.BlockSpec((cfg.C, cfg.hb * N), lambda h, j: (j, h))
    sq = jax.ShapeDtypeStruct((RH, T, N), F32)
    mt = jax.ShapeDtypeStruct((RH, T // cfg.C, N, N), F32)
    return _pcall(body, name="rwkv_scan_local_fwd", grid=grid, in_specs=[tok] * 6, out_specs=[seq, seq, mat, mat],
                  out_shape=[sq, sq, mt, mt], compiler_params=_cparams(("parallel", "parallel")))(*seqs)


def _scan_local_bwd(cfg, toks, dq, dy, da, dsl, extra, comm=None):
    T, RW, N = cfg.T, cfg.RW, RWKV_HEAD_DIM
    grid, seq, mat = _scan_local_specs(cfg)
    c_in, c_out, c_scr = comm[:3] if comm else ([], [], [])

    def body(r_ref, lw_ref, k_ref, v_ref, a_ref, b_ref, dq_ref, dy_ref, da_ref, dsl_ref, xr_ref, xk_ref, xv_ref,
             *rest):
        cin, outs = rest[:len(c_in)], rest[len(c_in):len(c_in) + 6]
        cout, scr = rest[len(c_in) + 6:len(c_in) + 6 + len(c_out)], rest[len(c_in) + 6 + len(c_out):]
        _comm_at(comm, 3, grid, cin, cout, scr)
        ins = [_split_heads(ref, cfg.hb) for ref in (r_ref, lw_ref, k_ref, v_ref, a_ref, b_ref)]
        _, vjp = jax.vjp(_chunk_local, *ins)
        d = vjp((dq_ref[...], _split_heads(dy_ref, cfg.hb), da_ref[:, 0], dsl_ref[:, 0]))
        add = {0: xr_ref, 2: xk_ref, 3: xv_ref}
        for j in range(6):
            dj = _merge_heads(d[j])
            outs[j][...] = dj + add[j][...] if j in add else dj
        _comm_at(comm, 4, grid, cin, cout, scr)

    tok = pl.BlockSpec((cfg.C, cfg.hb * N), lambda h, j: (j, h))
    return _pcall(body, name="rwkv_scan_local_bwd", grid=grid,
                  in_specs=[tok] * 6 + [seq, tok, mat, mat] + [tok] * 3 + [_ANY] * len(c_in),
                  out_specs=[tok] * 6 + [_ANY] * len(c_out),
                  out_shape=[jax.ShapeDtypeStruct((T, RW), F32)] * 6 + list(c_out), scratch_shapes=list(c_scr),
                  compiler_params=_cparams(("arbitrary", "arbitrary") if comm else ("parallel", "parallel")),
                  )(*toks, dq, dy, da, dsl, *extra, *c_in)


def _scan_carry_specs(cfg, rev):
    N, RH, C, nc = RWKV_HEAD_DIM, cfg.RH, cfg.C, cfg.T // cfg.C
    at = (lambda j: nc - 1 - j) if rev else (lambda j: j)
    seq = pl.BlockSpec((RH, C, N), lambda j: (0, at(j), 0))
    mat = pl.BlockSpec((RH, 1, N, N), lambda j: (0, at(j), 0, 0))
    return nc, seq, mat


def _scan_carry_fwd(cfg, q, yloc, a, sloc):
    T, RH, N = cfg.T, cfg.RH, RWKV_HEAD_DIM
    nc, seq, mat = _scan_carry_specs(cfg, False)

    def body(q_ref, yl_ref, a_ref, sl_ref, y_ref, ck_ref, s_ref):
        @pl.when(pl.program_id(0) == 0)
        def _():
            s_ref[...] = jnp.zeros_like(s_ref)

        S = s_ref[...]
        ck_ref[:, 0] = S
        y_ref[...] = _merge_heads(_bdot(q_ref[...], S, 2, 2, SCAN_PASSES[2]) + yl_ref[...])
        s_ref[...] = _bdot(S, a_ref[:, 0], 2, 1) + sl_ref[:, 0]

    tok = pl.BlockSpec((cfg.C, cfg.RW), lambda j: (j, 0))
    return _pcall(body, name="rwkv_scan_carry_fwd", grid=(nc,), in_specs=[seq, seq, mat, mat], out_specs=[tok, mat],
                  out_shape=[jax.ShapeDtypeStruct((T, cfg.RW), F32), jax.ShapeDtypeStruct((RH, nc, N, N), F32)],
                  scratch_shapes=[pltpu.VMEM((RH, N, N), F32)],
                  compiler_params=_cparams(("arbitrary",)))(q, yloc, a, sloc)


def _scan_carry_bwd(cfg, q, a, ckpt, dy, comm=None):
    T, RH, N = cfg.T, cfg.RH, RWKV_HEAD_DIM
    nc, seq, mat = _scan_carry_specs(cfg, True)

    def body(q_ref, a_ref, ck_ref, dy_ref, dq_ref, da_ref, dsl_ref, ds_ref):
        @pl.when(pl.program_id(0) == 0)
        def _():
            ds_ref[...] = jnp.zeros_like(ds_ref)

        S, dS, dY = ck_ref[:, 0], ds_ref[...], _split_heads(dy_ref, RH)
        dq_ref[...] = _bdot(dY, S, 2, 1, SCAN_PASSES[2])
        da_ref[:, 0] = _bdot(S, dS, 1, 1, SCAN_PASSES[2])
        dsl_ref[:, 0] = dS
        ds_ref[...] = _bdot(dS, a_ref[:, 0], 2, 2) + _bdot(dY, q_ref[...], 1, 1, SCAN_PASSES[2])

    mt = jax.ShapeDtypeStruct((RH, nc, N, N), F32)
    tok = pl.BlockSpec((cfg.C, cfg.RW), lambda j: (nc - 1 - j, 0))
    body, c_in, c_out, c_shapes, c_scr = _hosted(body, 4, 3, (nc,), comm)
    return _pcall(body, name="rwkv_scan_carry_bwd", grid=(nc,), in_specs=[seq, mat, mat, tok] + c_in,
                  out_specs=[seq, mat, mat] + c_out,
                  out_shape=[jax.ShapeDtypeStruct((RH, T, N), F32), mt, mt] + c_shapes,
                  scratch_shapes=c_scr + [pltpu.VMEM((RH, N, N), F32)],
                  compiler_params=_cparams(("arbitrary",)))(q, a, ckpt, dy, *(comm[0] if comm else []))


def _post_fn(y, r, kp, v, zb, ln_w, ln_b, rk, ind, ind_t):
    n = float(RWKV_HEAD_DIM)
    mu = _xdot(_xdot(y, ind, ind_t) / n, ind_t, ind)
    yc = y - mu
    var = _xdot(yc * yc, ind, ind_t) / n
    rstd = _xdot(lax.rsqrt(var + GN_EPS), ind_t, ind)
    yn = yc * rstd * ln_w + ln_b
    bonus = _xdot(_xdot(r * kp * rk, ind, ind_t), ind_t, ind) * v
    return (yn + bonus) * _silu(zb)


def _rwkv_post_fwd(cfg, y, r, kp, v, zb, ln_w, ln_b, rk):
    T, RW, tr = cfg.T, cfg.RW, cfg.tr
    ind, ind_t, _ = _head_indicators(cfg)

    def body(y_ref, r_ref, k_ref, v_ref, z_ref, lw_ref, lb_ref, rk_ref, ind_ref, indt_ref, ob_ref):
        ob_ref[...] = _post_fn(y_ref[...], r_ref[...], k_ref[...], v_ref[...], z_ref[...], lw_ref[...], lb_ref[...],
                               rk_ref[...], ind_ref[...], indt_ref[...]).astype(BF16)

    consts = [ln_w, ln_b, rk, ind, ind_t]
    return _pcall(body, name="rwkv_post_fwd", grid=(T // tr,),
                  in_specs=[_tile(tr, RW)] * 5 + [_const(c.shape) for c in consts],
                  out_specs=_tile(tr, RW), out_shape=jax.ShapeDtypeStruct((T, RW), BF16),
                  compiler_params=_cparams(("parallel",)))(y, r, kp, v, zb, *consts)


def _rwkv_post_bwd(cfg, y, r, kp, v, zb, ln_w, ln_b, rk, dob):
    T, RW = cfg.T, cfg.RW
    tr = min(128, T)
    ind, ind_t, _ = _head_indicators(cfg)

    def body(y_ref, r_ref, k_ref, v_ref, z_ref, lw_ref, lb_ref, rk_ref, ind_ref, indt_ref, dob_ref,
             dy_ref, dr_ref, dk_ref, dv_ref, dz_ref, dlw_ref, dlb_ref, drk_ref):
        fn = functools.partial(_post_fn, ind=ind_ref[...], ind_t=indt_ref[...])
        _, vjp = jax.vjp(fn, y_ref[...], r_ref[...], k_ref[...], v_ref[...], z_ref[...], lw_ref[...], lb_ref[...],
                         rk_ref[...])
        d = vjp(dob_ref[...])
        for ref, val in zip((dy_ref, dr_ref, dk_ref, dv_ref, dz_ref), d[:5]):
            ref[...] = val
        i = pl.program_id(0)
        for ref, val in zip((dlw_ref, dlb_ref, drk_ref), d[5:8]):
            _acc_store(i, ref, val)

    consts = [ln_w, ln_b, rk, ind, ind_t]
    vec = jax.ShapeDtypeStruct((1, RW), F32)
    return _pcall(body, name="rwkv_post_bwd", grid=(T // tr,),
                  in_specs=[_tile(tr, RW)] * 5 + [_const(c.shape) for c in consts] + [_tile(tr, RW)],
                  out_specs=[_tile(tr, RW)] * 5 + [_const((1, RW))] * 3,
                  out_shape=[jax.ShapeDtypeStruct((T, RW), F32)] * 5 + [vec] * 3,
                  compiler_params=_cparams(("arbitrary",)))(y, r, kp, v, zb, *consts, dob)


def _adamw_math(w, g, m, v):
    m = ADAM_B1 * m + (1.0 - ADAM_B1) * g
    v = ADAM_B2 * v + (1.0 - ADAM_B2) * (g * g)
    m_hat = m / (1.0 - ADAM_B1 ** ADAM_STEP)
    v_hat = v / (1.0 - ADAM_B2 ** ADAM_STEP)
    delta = -ADAM_LR * (m_hat / (jnp.sqrt(v_hat) + ADAM_EPS) + ADAM_WD * w)
    return delta, m, v


def _adamw(name, w, g, m, v, copy_grad=False, comm=None):
    R, Cc = w.shape
    Rp = -(-R // 8) * 8
    tr = Rp
    for nb in range(1, Rp // 8 + 1):
        if (Rp // 8) % nb == 0 and (Rp // nb) * Cc * 4 <= 2 * 1024 * 1024:
            tr = Rp // nb
            break

    def body(w_ref, g_ref, m_ref, v_ref, d_ref, nm_ref, nv_ref, *g_out):
        g_v = g_ref[...]
        d, nm, nv = _adamw_math(w_ref[...], g_v, m_ref[...], v_ref[...])
        d_ref[...] = d
        nm_ref[...] = nm
        nv_ref[...] = nv
        if copy_grad:
            g_out[0][...] = g_v

    spec = _tile(tr, Cc)
    n_out = 4 if copy_grad else 3
    body, c_in, c_out, c_shapes, c_scr = _hosted(body, 4, n_out, (Rp // tr,), comm)
    return _pcall(body, name=name, grid=(Rp // tr,), in_specs=[spec] * 4 + c_in, out_specs=[spec] * n_out + c_out,
                  out_shape=[jax.ShapeDtypeStruct((R, Cc), F32)] * n_out + c_shapes, scratch_shapes=c_scr,
                  compiler_params=_cparams(("arbitrary",) if comm else ("parallel",)),
                  )(w, g, m, v, *(comm[0] if comm else []))


def _row_tile(R, Cc, itemsize, budget=2 * 1024 * 1024):
    for nb in range(1, R // 16 + 1):
        if R % nb == 0 and (R // nb) % 16 == 0 and (R // nb) * Cc * itemsize <= budget:
            return R // nb
    return R


def _add_halves(name, gs, r1, c_idx):
    S, R, Cc = gs.shape
    half = R // 2
    tr = _row_tile(half, Cc, 4)
    nb = half // tr

    def body(c_ref, g_ref, r_ref, o_ref):
        o_ref[...] = (g_ref[...].astype(F32) + r_ref[...].astype(F32)).astype(BF16)

    grid_spec = pltpu.PrefetchScalarGridSpec(
        num_scalar_prefetch=1, grid=(S, nb),
        in_specs=[pl.BlockSpec((1, tr, Cc), lambda s, i, c: (s, c[0] * nb + i, 0)),
                  pl.BlockSpec((1, tr, Cc), lambda s, i, c: (s, i, 0))],
        out_specs=pl.BlockSpec((1, tr, Cc), lambda s, i, c: (s, i, 0)))
    return _pcall(body, name=name, grid_spec=grid_spec, out_shape=jax.ShapeDtypeStruct((S, half, Cc), BF16),
                  compiler_params=_cparams(("parallel", "parallel")))(c_idx, gs, r1)


def _sum_slots(name, r2):
    S, R, Cc = r2.shape
    tr = _row_tile(R, Cc, 4 * S // 2 if r2.dtype == BF16 else 4 * S)

    def body(r_ref, o_ref):
        acc = r_ref[0].astype(F32)
        for s in range(1, S):
            acc = acc + r_ref[s].astype(F32)
        o_ref[...] = acc

    return _pcall(body, name=name, grid=(R // tr,), in_specs=[pl.BlockSpec((S, tr, Cc), lambda i: (0, i, 0))],
                  out_specs=_tile(tr, Cc), out_shape=jax.ShapeDtypeStruct((R, Cc), F32),
                  compiler_params=_cparams(("parallel",)))(r2)


def _sum_chips(name, recv, own, place):
    S, H, Cc = recv.shape
    tr = _row_tile(H, Cc, 4, 1024 * 1024)
    nb = H // tr

    def body(p_ref, r_ref, own_ref, o_ref):
        s = pl.program_id(1)
        me = p_ref[0]

        @pl.when(s == 0)
        def _():
            o_ref[...] = jnp.zeros_like(o_ref)

        @pl.when(s == me)
        def _():
            o_ref[...] += own_ref[0].astype(F32)

        @pl.when(s != me)
        def _():
            o_ref[...] += r_ref[0].astype(F32)

    grid_spec = pltpu.PrefetchScalarGridSpec(
        num_scalar_prefetch=1, grid=(nb, S),
        in_specs=[pl.BlockSpec((1, tr, Cc), lambda i, s, p: (jnp.where(s == p[0], (s + 1) % S, s), i, 0)),
                  pl.BlockSpec((1, tr, Cc), lambda i, s, p: (p[0], i, 0))],
        out_specs=pl.BlockSpec((tr, Cc), lambda i, s, p: (p[1] * nb + i, 0)))
    return _pcall(body, name=name, grid_spec=grid_spec, out_shape=jax.ShapeDtypeStruct((2 * H, Cc), F32),
                  compiler_params=_cparams(("parallel", "arbitrary")))(place, recv, own)


def _cast_bf16(name, w):
    R, Cc = w.shape
    tr = _row_tile(R, Cc, 4)

    def body(w_ref, o_ref):
        o_ref[...] = w_ref[...].astype(BF16)

    return _pcall(body, name=name, grid=(R // tr,), in_specs=[_tile(tr, Cc)], out_specs=_tile(tr, Cc),
                  out_shape=jax.ShapeDtypeStruct((R, Cc), BF16), compiler_params=_cparams(("parallel",)))(w)


_ANY = pl.BlockSpec(memory_space=pl.ANY)


def _place():
    x, y, c = lax.axis_index("x"), lax.axis_index("y"), lax.axis_index("c")
    others = [(1 - x, y), (x, 1 - y), (1 - x, 1 - y)]
    return x, y, c, others


def _gather_parts(shards):
    n = len(shards)
    halves = [s.shape[0] // 2 for s in shards]

    def parts(ins, outs, sems):
        x, y, c, _ = _place()
        me = 2 * x + y
        n1 = (x ^ (1 - c), y ^ c)
        n2 = (x ^ c, y ^ (1 - c))
        s1, s2, sd = 2 * n1[0] + n1[1], 2 * n2[0] + n2[1], 2 * (1 - x) + (1 - y)
        sib = (x, y, 1 - c)

        def rows(k, chip, hc):
            return outs[k].at[chip, pl.ds(hc * halves[k], halves[k]), :]

        def remote(k, j, src, dst, to):
            return pltpu.make_async_remote_copy(src_ref=src, dst_ref=dst, send_sem=sems[0].at[7 * k + j],
                                                recv_sem=sems[1].at[7 * k + j], device_id=to, device_id_type=MESH)

        def copy(k, j):
            if j == 6:
                return remote(k, j, ins[k], outs[k].at[me], sib)
            if j < 2:
                mine = ins[k].at[pl.ds(c * halves[k], halves[k]), :]
                return remote(k, j, mine, rows(k, me, c), (*(n1 if j == 0 else n2), c))
            land = rows(k, {2: s1, 3: s1, 4: s2, 5: sd}[j], c)
            return remote(k, j, land, land, (*n2, c) if j == 2 else sib)

        def arrived(k, j):
            hc = c if j < 3 else 1 - c
            land = outs[k].at[me] if j == 6 else rows(k, {0: s1, 1: s2, 2: sd, 3: s2, 4: s1, 5: sd}[j], hc)
            remote(k, j, land, land, (x, y, c)).wait_recv()

        return copy, arrived

    def start(ins, outs, sems):
        copy, _ = parts(ins, outs, sems)
        for k in range(n):
            copy(k, 0).start()
            copy(k, 1).start()
            copy(k, 6).start()

    def middle(ins, outs, sems):
        copy, arrived = parts(ins, outs, sems)
        for k in range(n):
            arrived(k, 0)
            copy(k, 2).start()
            copy(k, 3).start()
            arrived(k, 1)
            copy(k, 4).start()

    def finish(ins, outs, sems):
        copy, arrived = parts(ins, outs, sems)
        for k in range(n):
            arrived(k, 2)
            copy(k, 5).start()
        for k in range(n):
            for j in (3, 4, 5, 6):
                arrived(k, j)
        for k in range(n):
            for j in range(7):
                copy(k, j).wait_send()

    out_shapes = [jax.ShapeDtypeStruct((N_CHIPS,) + s.shape, s.dtype) for s in shards]
    scratch = [pltpu.SemaphoreType.DMA((7 * n,)), pltpu.SemaphoreType.DMA((7 * n,))]
    return list(shards), out_shapes, scratch, start, finish, middle


def _swap_halves(grads):
    n = len(grads)
    halves = [g.shape[1] // 2 for g in grads]

    def copies(ins, outs, sems):
        x, y, c, _ = _place()
        return [pltpu.make_async_remote_copy(
            src_ref=ins[k].at[:, pl.ds((1 - c) * halves[k], halves[k]), :], dst_ref=outs[k], send_sem=sems[0].at[k],
            recv_sem=sems[1].at[k], device_id=(x, y, 1 - c), device_id_type=MESH) for k in range(n)]

    def start(ins, outs, sems):
        for cp in copies(ins, outs, sems):
            cp.start()

    def finish(ins, outs, sems):
        for cp in copies(ins, outs, sems):
            cp.wait()

    out_shapes = [jax.ShapeDtypeStruct((g.shape[0], h) + g.shape[2:], g.dtype) for g, h in zip(grads, halves)]
    scratch = [pltpu.SemaphoreType.DMA((n,)), pltpu.SemaphoreType.DMA((n,))]
    return list(grads), out_shapes, scratch, start, finish


def _scatter_to_owners(chip_sums):
    n = len(chip_sums)

    def sends(ins, outs, sems):
        x, y, c, others = _place()
        me = 2 * x + y
        return [pltpu.make_async_remote_copy(
            src_ref=ins[k].at[2 * px + py], dst_ref=outs[k].at[me], send_sem=sems[0].at[3 * k + j],
            recv_sem=sems[1].at[3 * k + j], device_id=(px, py, c), device_id_type=MESH)
            for k in range(n) for j, (px, py) in enumerate(others)]

    def start(ins, outs, sems):
        for cp in sends(ins, outs, sems):
            cp.start()

    def finish(ins, outs, sems):
        x, y, c, others = _place()
        for k in range(n):
            for j, (px, py) in enumerate(others):
                land = outs[k].at[2 * px + py]
                pltpu.make_async_remote_copy(src_ref=land, dst_ref=land, send_sem=sems[0].at[3 * k + j],
                                             recv_sem=sems[1].at[3 * k + j], device_id=(x, y, c),
                                             device_id_type=MESH).wait_recv()
        for cp in sends(ins, outs, sems):
            cp.wait_send()

    out_shapes = [jax.ShapeDtypeStruct(g.shape, g.dtype) for g in chip_sums]
    scratch = [pltpu.SemaphoreType.DMA((3 * n,)), pltpu.SemaphoreType.DMA((3 * n,))]
    return list(chip_sums), out_shapes, scratch, start, finish


def _swap_with_sibling(arrays):
    n = len(arrays)

    def copies(ins, outs, sems):
        x, y, c, _ = _place()
        return [pltpu.make_async_remote_copy(src_ref=ins[k], dst_ref=outs[k], send_sem=sems[0].at[k],
                                             recv_sem=sems[1].at[k], device_id=(x, y, 1 - c), device_id_type=MESH)
                for k in range(n)]

    def start(ins, outs, sems):
        for cp in copies(ins, outs, sems):
            cp.start()

    def finish(ins, outs, sems):
        for cp in copies(ins, outs, sems):
            cp.wait()

    out_shapes = [jax.ShapeDtypeStruct(a.shape, a.dtype) for a in arrays]
    scratch = [pltpu.SemaphoreType.DMA((n,)), pltpu.SemaphoreType.DMA((n,))]
    return list(arrays), out_shapes, scratch, start, finish


def _add_pair(name, a, b):
    R, Cc = a.shape
    tr = _row_tile(R, Cc, 4)

    def body(a_ref, b_ref, o_ref):
        o_ref[...] = (a_ref[...].astype(F32) + b_ref[...].astype(F32)).astype(BF16)

    return _pcall(body, name=name, grid=(R // tr,), in_specs=[_tile(tr, Cc)] * 2, out_specs=_tile(tr, Cc),
                  out_shape=jax.ShapeDtypeStruct((R, Cc), BF16), compiler_params=_cparams(("parallel",)))(a, b)


def _second_neighbour():
    x, y, c, _ = _place()
    return (x, y, c), (x ^ c, y ^ (1 - c)), (x ^ (1 - c), y ^ c)


def _scatter_stage1(chip_sums):
    n = len(chip_sums)

    def copies(ins, outs, sems):
        (x, y, c), n2, n1 = _second_neighbour()
        diag = 2 * (1 - x) + (1 - y)
        return [pltpu.make_async_remote_copy(
            src_ref=ins[k].at[slot], dst_ref=outs[2 * k + j], send_sem=sems[0].at[2 * k + j],
            recv_sem=sems[1].at[2 * k + j], device_id=(*n2, c), device_id_type=MESH)
            for k in range(n) for j, slot in enumerate((2 * n2[0] + n2[1], diag))]

    def start(ins, outs, sems):
        for cp in copies(ins, outs, sems):
            cp.start()

    def finish(ins, outs, sems):
        for cp in copies(ins, outs, sems):
            cp.wait()

    out_shapes = [jax.ShapeDtypeStruct(g.shape[1:], g.dtype) for g in chip_sums for _ in range(2)]
    scratch = [pltpu.SemaphoreType.DMA((2 * n,)), pltpu.SemaphoreType.DMA((2 * n,))]
    return list(chip_sums), out_shapes, scratch, start, finish


def _scatter_stage2(passed):
    n = len(passed)

    def copies(ins, outs, sems):
        (x, y, c), n2, n1 = _second_neighbour()
        return [pltpu.make_async_remote_copy(src_ref=ins[k], dst_ref=outs[k], send_sem=sems[0].at[k],
                                             recv_sem=sems[1].at[k], device_id=(*n1, c), device_id_type=MESH)
                for k in range(n)]

    def start(ins, outs, sems):
        for cp in copies(ins, outs, sems):
            cp.start()

    def finish(ins, outs, sems):
        for cp in copies(ins, outs, sems):
            cp.wait()

    out_shapes = [jax.ShapeDtypeStruct(p.shape, p.dtype) for p in passed]
    scratch = [pltpu.SemaphoreType.DMA((n,)), pltpu.SemaphoreType.DMA((n,))]
    return list(passed), out_shapes, scratch, start, finish


def _add_passed(name, own, got, slot):
    _, H, Cc = own.shape
    tr = _row_tile(H, Cc, 4)

    def body(s_ref, o_ref, g_ref, out_ref):
        out_ref[...] = (o_ref[0].astype(F32) + g_ref[...].astype(F32)).astype(BF16)

    grid_spec = pltpu.PrefetchScalarGridSpec(
        num_scalar_prefetch=1, grid=(H // tr,),
        in_specs=[pl.BlockSpec((1, tr, Cc), lambda i, s: (s[0], i, 0)), pl.BlockSpec((tr, Cc), lambda i, s: (i, 0))],
        out_specs=pl.BlockSpec((tr, Cc), lambda i, s: (i, 0)))
    return _pcall(body, name=name, grid_spec=grid_spec, out_shape=jax.ShapeDtypeStruct((H, Cc), BF16),
                  compiler_params=_cparams(("parallel",)))(slot, own, got)


def _sum_stages(name, own, direct, via, place, transposed=False):
    _, H, Cc = own.shape
    tr = LANES if transposed else _row_tile(H, Cc, 4, 1024 * 1024)
    nb = H // tr

    def body(p_ref, own_ref, d_ref, v_ref, o_ref):
        acc = (own_ref[0].astype(F32) + d_ref[...].astype(F32)) + v_ref[...].astype(F32)
        o_ref[...] = acc.T if transposed else acc

    flat = pl.BlockSpec((tr, Cc), lambda i, p: (i, 0))
    out_spec = (pl.BlockSpec((Cc, tr), lambda i, p: (0, p[1] * nb + i)) if transposed
                else pl.BlockSpec((tr, Cc), lambda i, p: (p[1] * nb + i, 0)))
    grid_spec = pltpu.PrefetchScalarGridSpec(
        num_scalar_prefetch=1, grid=(nb,),
        in_specs=[pl.BlockSpec((1, tr, Cc), lambda i, p: (p[0], i, 0)), flat, flat], out_specs=out_spec)
    return _pcall(body, name=name, grid_spec=grid_spec,
                  out_shape=jax.ShapeDtypeStruct((Cc, 2 * H) if transposed else (2 * H, Cc), F32),
                  compiler_params=_cparams(("parallel",)))(place, own, direct, via)


def _join_halves(fulls, axes, small):
    n = len(fulls)
    hs = [f.shape[ax] // 2 for f, ax in zip(fulls, axes)]
    rel = [(dx, dy, dc) for dx in (0, 1) for dy in (0, 1) for dc in (0, 1)][1:]

    def half(ref, k, hc):
        part = pl.ds(hc * hs[k], hs[k])
        return ref.at[:, part] if axes[k] else ref.at[part, :]

    def body(*refs):
        ins, small_in = refs[:n], refs[n]
        outs, small_out = refs[n + 1:2 * n + 1], refs[2 * n + 1]
        send_sems, recv_sems, ssend, srecv, local_sem = refs[2 * n + 2:]
        x, y, c, _ = _place()
        dev = 4 * x + 2 * y + c
        local = pltpu.make_async_copy(small_in, small_out.at[dev], local_sem)
        local.start()
        cps = []
        for k in range(n):
            cp = pltpu.make_async_remote_copy(src_ref=half(ins[k], k, c), dst_ref=half(outs[k], k, c),
                                              send_sem=send_sems.at[k], recv_sem=recv_sems.at[k],
                                              device_id=(x, y, 1 - c), device_id_type=MESH)
            cp.start()
            cps.append(cp)
        for r, (dx, dy, dc) in enumerate(rel):
            cp = pltpu.make_async_remote_copy(src_ref=small_in, dst_ref=small_out.at[dev], send_sem=ssend.at[r],
                                              recv_sem=srecv.at[r], device_id=(x ^ dx, y ^ dy, c ^ dc),
                                              device_id_type=MESH)
            cp.start()
            cps.append(cp)
        for k in range(n):
            land = half(outs[k], k, 1 - c)
            pltpu.make_async_remote_copy(src_ref=land, dst_ref=land, send_sem=send_sems.at[k],
                                         recv_sem=recv_sems.at[k], device_id=(x, y, c), device_id_type=MESH).wait_recv()
        for r, (dx, dy, dc) in enumerate(rel):
            land = small_out.at[4 * (x ^ dx) + 2 * (y ^ dy) + (c ^ dc)]
            pltpu.make_async_remote_copy(src_ref=land, dst_ref=land, send_sem=ssend.at[r], recv_sem=srecv.at[r],
                                         device_id=(x, y, c), device_id_type=MESH).wait_recv()
        for cp in cps:
            cp.wait_send()
        local.wait()

    return _pcall(
        body, name="join_halves", in_specs=[_ANY] * (n + 1), out_specs=[_ANY] * (n + 1),
        out_shape=[jax.ShapeDtypeStruct(f.shape, f.dtype) for f in fulls]
        + [jax.ShapeDtypeStruct((N_DEV,) + small.shape, small.dtype)],
        input_output_aliases={k: k for k in range(n)},
        scratch_shapes=[pltpu.SemaphoreType.DMA((n,)), pltpu.SemaphoreType.DMA((n,)), pltpu.SemaphoreType.DMA((7,)),
                        pltpu.SemaphoreType.DMA((7,)), pltpu.SemaphoreType.DMA],
    )(*fulls, small)


def _local_step(cfg, x2, target, norm_gain, w_my, fb, mu_g, w0, a0, k_k, k_a, r_k, ln_w, ln_b, fng, rest,
                exchange=None, h=None):
    T, D, FW, FH, RW, RH, LP, lora = cfg.T, cfg.D, cfg.FW, cfg.FH, cfg.RW, cfg.RH, cfg.LP, cfg.lora
    fb_p = jnp.pad(fb, ((0, 0), (0, LANES - FH)))
    mu = _rwkv_vec_to_my(cfg, mu_g)
    rk = r_k.reshape(1, RW)
    tm = min(1024, T)

    if h is None:
        h = _rms_fwd(cfg, x2, norm_gain)
    if len(rest) == 2:
        u, *got = _mm("in_proj", h, w_my, "nn", F32, tm, cfg.tn, 2048, comm=rest[0])
        rest = rest[1](got)
    else:
        u = _mm("in_proj", h, w_my, "nn", F32, tm, cfg.tn, 2048)
    w2, a2, wpf, wpr, wout = rest
    w2p = jnp.pad(w2, ((0, LP - lora), (0, 0)))
    a2p = jnp.pad(a2, ((0, LP - lora), (0, 0)))
    c_cols = _fox_prep(cfg, u, fb_p)
    c_rows = c_cols[:, :FH].T.reshape(FH, 1, T)
    o, lse = _attn_fwd(cfg, u, c_rows)
    oa = _gate_a_fwd(cfg, o, u)
    prep = _rwkv_prep_fwd(cfg, u, mu, w0, w2p, a0, a2p, k_k, k_a)
    r, lw, kp, v, an, b, zb = prep
    toks = [r, lw, kp, v, an, b]
    q_s, yloc, a_m, sloc = _scan_local_fwd(cfg, toks)
    y, ckpt = _scan_carry_fwd(cfg, q_s, yloc, a_m, sloc)
    ob = _rwkv_post_fwd(cfg, y, r, kp, v, zb, ln_w, ln_b, rk)
    pa = _mm("proj_fox", oa, wpf, "nn", F32, tm, 1024, 2048)
    pb = _mm("proj_rwkv", ob, wpr, "nn", F32, tm, 1024, 2048)
    m = _merge_fwd(cfg, pa, pb, u)
    mo = _mm("out_proj", m, wout, "nn", F32, tm, 1024, 2048)
    loss8, dres, dres16, d_fng = _final(cfg, x2, mo, fng.reshape(1, D), target)

    dm = _mm("out_proj_dx", dres16, wout, "nt", F32, tm, 1024, 2048)
    d_wout = _mm("out_proj_dw", m, dres16, "tn", BF16, 1024, 1024, 2048)
    dpa, dpb, du = _merge_bwd(cfg, pa, pb, u, dm)
    doa = _mm("proj_fox_dx", dpa, wpf, "nt", F32, tm, 1024, 2048)
    d_wpf = _mm("proj_fox_dw", oa, dpa, "tn", BF16, 1024, 1024, 2048)
    dob = _mm("proj_rwkv_dx", dpb, wpr, "nt", F32, tm, 1024, 2048)
    d_wpr = _mm("proj_rwkv_dw", ob, dpb, "tn", BF16, 1024, 1024, 2048)

    do, du = _gate_a_bwd(cfg, o, u, doa, du)
    du, dcol = _attn_bwd(cfg, u, c_rows, lse, do, du)
    dc = jnp.pad(-dcol.reshape(FH, T).T, ((0, 0), (0, LANES - FH)))
    df, d_fb = _fox_prep_bwd(cfg, u, fb_p, dc)

    dy, dr_p, dk_p, dv_p, dzb, d_lnw, d_lnb, d_rk = _rwkv_post_bwd(cfg, y, r, kp, v, zb, ln_w, ln_b, rk, dob)
    early = dict(w_proj_fox=d_wpf, w_proj_rwkv=d_wpr, w_out=d_wout)
    res = _scan_carry_bwd(cfg, q_s, a_m, ckpt, dy, exchange(early) if exchange else None)
    dq_s, da_m, dsl = res[:3]
    res = _scan_local_bwd(cfg, toks, dq_s, dy, da_m, dsl, [dr_p, dk_p, dv_p],
                          exchange(("swapped", list(res[3:]))) if exchange else None)
    cots, received = res[:6], list(res[6:])
    dus, d_mu, d_w0, d_w2p, d_a0, d_a2p, d_kk, d_ka = _rwkv_prep_bwd(cfg, u, mu, w0, w2p, a0, a2p, k_k, k_a, cots, dzb)
    du = _shift_bwd(cfg, dus, mu, df, du)
    if exchange:
        late = dict(w_in=exchange((h, du, d_w2p[:lora], d_a2p[:lora])))
    else:
        late = dict(w_in=_mm("in_proj_dw", h, du, "tn", BF16, 1024, cfg.tn, 2048), rwkv_w2=d_w2p[:lora],
                    rwkv_a2=d_a2p[:lora])
    tkx = 2 * cfg.tn if cfg.ncol % (2 * cfg.tn) == 0 else cfg.tn
    res = _mm("in_proj_dx", du, w_my, "nt", F32, tm, 1024, tkx, comm=exchange(late) if exchange else None)
    dh = res[0] if exchange else res
    big = dict(early, **late)
    res = _rms_bwd(cfg, x2, norm_gain, dh, dres, exchange(list(res[1:])) if exchange else None)
    gx, d_ng = res[:2]
    received += list(res[2:])

    small = dict(norm_gain=d_ng, fox_forget_bias=d_fb[:, :FH], rwkv_shift_mix=_rwkv_vec_from_my(cfg, d_mu),
                 rwkv_w0=d_w0, rwkv_a0=d_a0, rwkv_k_k=d_kk, rwkv_k_a=d_ka, rwkv_r_k=d_rk, rwkv_ln_w=d_lnw,
                 rwkv_ln_b=d_lnb, final_norm_gain=d_fng)
    return loss8[0, 0], gx, small, big, received


_SMALL = ["norm_gain", "fox_forget_bias", "rwkv_shift_mix", "rwkv_w0", "rwkv_a0", "rwkv_k_k", "rwkv_k_a", "rwkv_r_k",
          "rwkv_ln_w", "rwkv_ln_b", "final_norm_gain"]
_WEIGHTS = ["norm_gain", "w_in", "fox_forget_bias", "rwkv_shift_mix", "rwkv_w0", "rwkv_w2", "rwkv_a0", "rwkv_a2",
            "rwkv_k_k", "rwkv_k_a", "rwkv_r_k", "rwkv_ln_w", "rwkv_ln_b", "w_proj_fox", "w_proj_rwkv", "w_out",
            "final_norm_gain"]


def _pack_small(arrs):
    parts, n = [], 0
    for a in arrs:
        f = a.reshape(-1)
        fill = (-f.shape[0]) % LANES
        parts += [f] + ([jnp.zeros((fill,), f.dtype)] if fill else [])
        n += f.shape[0] + fill
    tail = ((-(n // LANES)) % 8) * LANES
    return jnp.concatenate(parts + ([jnp.zeros((tail,), parts[0].dtype)] if tail else [])).reshape(-1, LANES)


def _unpack_small(packed, shapes):
    flat = packed.reshape(-1)
    out, pos = [], 0
    for s in shapes:
        n = int(np.prod(s))
        out.append(flat[pos:pos + n].reshape(s))
        pos += n + ((-n) % LANES)
    return out


def _shard_major(a, axis):
    parts = jnp.split(a, N_CHIPS, axis=axis)
    return jnp.stack(parts, axis=0)


def kernel(x, norm_gain, w_in, fox_forget_bias, rwkv_shift_mix, rwkv_w0, rwkv_w2, rwkv_a0, rwkv_a2, rwkv_k_k, rwkv_k_a, rwkv_r_k, rwkv_ln_w, rwkv_ln_b, w_proj_fox, w_proj_rwkv, w_out, final_norm_gain, loss_target, m_norm_gain, m_w_in, m_fox_forget_bias, m_rwkv_shift_mix, m_rwkv_w0, m_rwkv_w2, m_rwkv_a0, m_rwkv_a2, m_rwkv_k_k, m_rwkv_k_a, m_rwkv_r_k, m_rwkv_ln_w, m_rwkv_ln_b, m_w_proj_fox, m_w_proj_rwkv, m_w_out, m_final_norm_gain, v_norm_gain, v_w_in, v_fox_forget_bias, v_rwkv_shift_mix, v_rwkv_w0, v_rwkv_w2, v_rwkv_a0, v_rwkv_a2, v_rwkv_k_k, v_rwkv_k_a, v_rwkv_r_k, v_rwkv_ln_w, v_rwkv_ln_b, v_w_proj_fox, v_w_proj_rwkv, v_w_out, v_final_norm_gain):
    args = dict(locals())
    T, D = x.shape[1], x.shape[2]
    lora = rwkv_w2.shape[1]
    cfg = _Cfg(T, D, lora)
    RW = cfg.RW
    c_idx = lax.axis_index("c").astype(jnp.int32).reshape(1)
    me_chip = (2 * lax.axis_index("x") + lax.axis_index("y")).astype(jnp.int32)
    place = jnp.concatenate([me_chip.reshape(1), c_idx])

    w_in_s = w_in[0].astype(BF16)
    lora_s = jnp.concatenate([rwkv_w2[0], rwkv_a2[0]], axis=0)
    h, g_in = _rms_fwd(cfg, x[0], norm_gain, _gather_parts([w_in_s]))
    w_my = _shards_to_my_layout(cfg, g_in)
    mine = [_cast_bf16("cast_w_proj_fox", w_proj_fox[0]), _cast_bf16("cast_w_proj_rwkv", w_proj_rwkv[0]),
            _cast_bf16("cast_w_out", w_out[0]), lora_s]

    def unpack(gathered):
        g_wpf, g_wpr, g_out, g_lora = gathered
        lo = g_lora.transpose(1, 0, 2).reshape(2 * lora, RW)
        return (lo[:lora], lo[lora:], g_wpf.transpose(1, 0, 2).reshape(RW, D),
                g_wpr.transpose(1, 0, 2).reshape(RW, D), g_out.reshape(D, D))

    early, late = ["w_proj_fox", "w_proj_rwkv", "w_out"], ["w_in", "lora"]
    names = early + late
    chip_sums, direct, shard_major = {}, {}, []
    n1_slot = (2 * (lax.axis_index("x") ^ (1 - lax.axis_index("c")))
               + (lax.axis_index("y") ^ lax.axis_index("c"))).astype(jnp.int32).reshape(1)

    def exchange(got):
        if isinstance(got, tuple) and len(got) == 4:
            h, du, d_w2, d_a2 = got
            c, half = lax.axis_index("c"), D // 2
            cols = lambda base: lax.dynamic_slice_in_dim(h, base * half, half, axis=1)
            lora_g = _shard_major(jnp.concatenate([d_w2, d_a2], axis=0).astype(BF16), 1)
            lora_rows = lambda base: lax.dynamic_slice_in_dim(lora_g, base * lora, lora, axis=1).reshape(-1, RW // 4)
            tiles = (BF16, min(1024, half), cfg.tn, 2048)
            sent = _mm("in_proj_dw_sibling", cols(1 - c), du, "tn", *tiles)
            kept, got_w, got_l = _mm("in_proj_dw", cols(c), du, "tn", *tiles,
                                     comm=_swap_with_sibling([sent, lora_rows(1 - c)]))
            return (_add_pair("add_halves_w_in", kept, got_w),
                    _add_pair("add_halves_lora", lora_rows(c), got_l).reshape(N_CHIPS, lora, RW // 4))
        if isinstance(got, dict):
            if "w_in" in got:
                sums = [_my_layout_to_shards(cfg, got["w_in"][0]), got["w_in"][1]]
                chip_sums.update(zip(late, sums))
                return _scatter_stage1(sums)
            shard_major.extend([_shard_major(got["w_proj_fox"], 1), _shard_major(got["w_proj_rwkv"], 1),
                                _shard_major(got["w_out"], 0)])
            return _swap_halves(shard_major)
        if got[0] == "swapped":
            sums = [_add_halves("add_halves_" + nm, g, r, c_idx) for nm, g, r in zip(early, shard_major, got[1])]
            chip_sums.update(zip(early, sums))
            return _scatter_to_owners(sums)
        direct.update(zip(late, got[0::2]))
        return _scatter_stage2([_add_passed("add_passed_" + nm, chip_sums[nm], g, n1_slot)
                                for nm, g in zip(late, got[1::2])])

    loss_dev, gx, small, _, recv2 = _local_step(
        cfg, x[0], loss_target[0], norm_gain, w_my, fox_forget_bias, rwkv_shift_mix, rwkv_w0, rwkv_a0, rwkv_k_k,
        rwkv_k_a, rwkv_r_k, rwkv_ln_w, rwkv_ln_b, final_norm_gain, (_gather_parts(mine), unpack), exchange, h)
    loss = lax.psum(loss_dev, ("x", "y", "c"))

    small_shapes = [args[nm].shape for nm in _SMALL]
    packed = _pack_small([small[nm] for nm in _SMALL])
    reduced = [_sum_chips("sum_chips_" + nm, r, chip_sums[nm], place) for nm, r in zip(early, recv2[:3])]
    reduced += [_sum_stages("sum_stages_" + nm, chip_sums[nm], direct[nm], via, place, transposed=nm == "w_in")
                for nm, via in zip(late, recv2[3:])]
    *joined, small_all = _join_halves(reduced, [int(nm == "w_in") for nm in names], packed)
    g_small = _sum_slots("sum_small", small_all)

    grads = dict(zip(_SMALL, _unpack_small(g_small, small_shapes)))
    grads.update({nm: g[None] for nm, g in zip(names, joined) if nm not in ("lora", "w_in")})
    g_lora_f = joined[names.index("lora")]
    grads["rwkv_w2"] = g_lora_f[None, :lora]
    grads["rwkv_a2"] = g_lora_f[None, lora:]

    delta, new_m, new_v = {}, {}, {}
    w_small = _pack_small([args[nm] for nm in _SMALL])
    m_small = _pack_small([args["m_" + nm] for nm in _SMALL])
    v_small = _pack_small([args["v_" + nm] for nm in _SMALL])
    d_s, m_s, v_s = _adamw("adamw_small", w_small, g_small, m_small, v_small)
    for tgt, pk in ((delta, d_s), (new_m, m_s), (new_v, v_s)):
        tgt.update(zip(_SMALL, _unpack_small(pk, small_shapes)))
    t_out = _adamw("adamw_w_in", w_in[0].T, joined[names.index("w_in")], m_w_in[0].T, v_w_in[0].T, copy_grad=True)
    delta["w_in"], new_m["w_in"], new_v["w_in"], grads["w_in"] = [t.T[None] for t in t_out]
    for nm in ("w_proj_fox", "w_proj_rwkv", "w_out", "rwkv_w2", "rwkv_a2"):
        shp = args[nm].shape
        two_d = (shp[1], shp[2])
        d_b, m_b, v_b = _adamw("adamw_" + nm, args[nm].reshape(two_d), grads[nm].reshape(two_d),
                               args["m_" + nm].reshape(two_d), args["v_" + nm].reshape(two_d))
        delta[nm], new_m[nm], new_v[nm] = d_b.reshape(shp), m_b.reshape(shp), v_b.reshape(shp)

    return (loss, gx[None], *[grads[n] for n in _WEIGHTS], *[delta[n] for n in _WEIGHTS],
            *[new_m[n] for n in _WEIGHTS], *[new_v[n] for n in _WEIGHTS])
```

```python
import functools

import numpy as np
import jax
import jax.numpy as jnp
from jax import lax
from jax.experimental import pallas as pl
from jax.experimental.pallas import tpu as pltpu

F32 = jnp.float32
BF16 = jnp.bfloat16
HI = lax.Precision.HIGHEST
MESH = pl.DeviceIdType.MESH

FOX_HEAD_DIM = 128
RWKV_HEAD_DIM = 64
RMS_EPS = 1e-6
GN_EPS = 64e-5
L2_EPS = 1e-12
ADAM_LR = 0.001
ADAM_B1 = 0.9
ADAM_B2 = 0.999
ADAM_EPS = 1e-08
ADAM_WD = 0.01
ADAM_STEP = 10

LANES = 128
VMEM_LIMIT = 56 * 1024 * 1024
SCAN_CHUNK = 64
SCAN_HEADS_PER_STEP = 16
SCAN_CHUNKS_PER_STEP = 2
SCAN_PASSES = (3, 1, 1)
N_CHIPS = 4
N_DEV = 8

_pcall = pl.pallas_call


def _cparams(sem=None):
    return pltpu.CompilerParams(dimension_semantics=sem, vmem_limit_bytes=VMEM_LIMIT)


def _softplus(x):
    return jnp.maximum(x, 0.0) + jnp.log(1.0 + jnp.exp(-jnp.abs(x)))


def _silu(z):
    return z * jax.nn.sigmoid(z)


def _rmsn(x, g):
    return x * lax.rsqrt(jnp.mean(x * x, axis=-1, keepdims=True) + RMS_EPS) * g


def _dot(a, b, dims="nn", precision=None):
    dn = {"nn": (((1,), (0,)), ((), ())), "nt": (((1,), (1,)), ((), ())), "tn": (((0,), (0,)), ((), ()))}[dims]
    return lax.dot_general(a, b, dn, precision=precision, preferred_element_type=F32)


def _split_bf16(x):
    hi = x.astype(BF16)
    return hi, (x - hi.astype(F32)).astype(BF16)


def _bdot_raw(a, b, ca, cb, passes):
    dn = (((ca,), (cb,)), ((0,), (0,)))
    mm = lambda p, q: lax.dot_general(p, q, dn, preferred_element_type=F32)
    if passes == 1:
        return mm(a.astype(BF16), b.astype(BF16))
    ah, al = _split_bf16(a)
    bh, bl = _split_bf16(b)
    return mm(ah, bh) + (mm(ah, bl) + mm(al, bh))


@functools.partial(jax.custom_vjp, nondiff_argnums=(2, 3, 4))
def _bdot_p(a, b, ca, cb, passes):
    return _bdot_raw(a, b, ca, cb, passes)


def _bdot_fwd(a, b, ca, cb, passes):
    return _bdot_raw(a, b, ca, cb, passes), (a, b)


def _bdot_bwd(ca, cb, passes, res, g):
    a, b = res
    if (ca, cb) == (2, 1):
        return _bdot_p(g, b, 2, 2, passes), _bdot_p(a, g, 1, 1, passes)
    if (ca, cb) == (2, 2):
        return _bdot_p(g, b, 2, 1, passes), _bdot_p(g, a, 1, 1, passes)
    assert (ca, cb) == (1, 1)
    return _bdot_p(b, g, 2, 2, passes), _bdot_p(a, g, 2, 1, passes)


_bdot_p.defvjp(_bdot_fwd, _bdot_bwd)


def _bdot(a, b, ca, cb, passes=3):
    return _bdot_p(a, b, ca, cb, passes)


def _dot3(a, b):
    return _bdot(a[None], b[None], 2, 1)[0]


@jax.custom_vjp
def _xdot(x, m, mt):
    hi, lo = _split_bf16(x)
    m16 = m.astype(BF16)
    return _dot(hi, m16) + _dot(lo, m16)


def _xdot_fwd(x, m, mt):
    return _xdot(x, m, mt), (m, mt)


def _xdot_bwd(res, g):
    m, mt = res
    return _xdot(g, mt, m), jnp.zeros_like(m), jnp.zeros_like(mt)


_xdot.defvjp(_xdot_fwd, _xdot_bwd)


class _Cfg:
    def __init__(self, T, D, lora):
        self.T, self.D, self.lora = T, D, lora
        self.FW = D // 2
        self.FH = self.FW // FOX_HEAD_DIM
        self.RW = D // 2
        self.RH = self.RW // RWKV_HEAD_DIM
        self.LP = -(-lora // LANES) * LANES
        self.o_fox = 0
        self.o_rwkv = 4 * self.FW
        self.o_gate = self.o_rwkv + 4 * self.RW
        self.o_f = self.o_gate + 2 * D
        self.o_wd = self.o_f + LANES
        self.o_ad = self.o_wd + self.LP
        end = self.o_ad + self.LP
        self.tn = 1280 if D >= 2048 else LANES
        self.ncol = -(-end // self.tn) * self.tn
        self.in_cols = 4 * self.FW + self.FH + 4 * self.RW + 2 * lora + 2 * D
        self.scp = -(-(self.in_cols // N_CHIPS) // LANES) * LANES
        self.rseg = 4 * self.RW + 2 * self.LP
        self.C = min(SCAN_CHUNK, T)
        self.tr = min(256, T)
        self.hb = min(SCAN_HEADS_PER_STEP, self.RH)
        self.cb = SCAN_CHUNKS_PER_STEP if (T // self.C) % SCAN_CHUNKS_PER_STEP == 0 else 1

    def segments(self):
        FW, FH, RW, lo, D = self.FW, self.FH, self.RW, self.lora, self.D
        g_f = 4 * FW
        g_r = g_f + FH
        g_wd = g_r + 4 * RW
        g_ad = g_wd + lo
        g_g = g_ad + lo
        dh = FOX_HEAD_DIM
        qkv = [(j * FW + h * dh, dh, (3 * h + j) * dh) for h in range(FH) for j in range(3)]
        return qkv + [(3 * FW, FW, 3 * FW), (g_f, FH, self.o_f), (g_r, 4 * RW, self.o_rwkv), (g_wd, lo, self.o_wd),
                      (g_ad, lo, self.o_ad), (g_g, 2 * D, self.o_gate)]


def _shards_to_my_layout(cfg, g):
    R, sc = g.shape[1], g.shape[2]
    segs = sorted(cfg.segments(), key=lambda s: s[2])
    parts, pos = [], 0
    for g0, w, m0 in segs:
        if m0 > pos:
            parts.append(jnp.zeros((R, m0 - pos), g.dtype))
        for s in range(N_CHIPS):
            lo, hi = max(g0, s * sc), min(g0 + w, (s + 1) * sc)
            if lo < hi:
                parts.append(g[s, :, lo - s * sc:hi - s * sc])
        pos = m0 + w
    if cfg.ncol > pos:
        parts.append(jnp.zeros((R, cfg.ncol - pos), g.dtype))
    return jnp.concatenate(parts, axis=1)


def _my_layout_to_shards(cfg, wm):
    sc, R = cfg.in_cols // N_CHIPS, wm.shape[0]
    segs = sorted(cfg.segments(), key=lambda s: s[0])
    shards = []
    for s in range(N_CHIPS):
        parts = []
        for g0, w, m0 in segs:
            lo, hi = max(g0, s * sc), min(g0 + w, (s + 1) * sc)
            if lo < hi:
                parts.append(wm[:, m0 + lo - g0:m0 + hi - g0])
        parts.append(jnp.zeros((R, cfg.scp - sc), wm.dtype))
        shards.append(jnp.concatenate(parts, axis=1))
    return jnp.stack(shards, axis=0)


def _rwkv_vec_to_my(cfg, v):
    RW4, lo, LP = 4 * cfg.RW, cfg.lora, cfg.LP
    z = jnp.zeros((1, LP - lo), v.dtype)
    return jnp.concatenate([v[:, :RW4], v[:, RW4:RW4 + lo], z, v[:, RW4 + lo:], z], axis=1)


def _rwkv_vec_from_my(cfg, v):
    RW4, lo, LP = 4 * cfg.RW, cfg.lora, cfg.LP
    return jnp.concatenate([v[:, :RW4], v[:, RW4:RW4 + lo], v[:, RW4 + LP:RW4 + LP + lo]], axis=1)


def _comm_at(comm, which, steps, cin, cout, scr):
    if not comm or len(comm) <= which:
        return
    lin, total = 0, 1
    for d, n in enumerate(steps):
        lin = lin * n + pl.program_id(d)
        total *= n
    pl.when(lin == {3: 0, 4: total - 1, 5: total // 2}[which])(lambda: comm[which](cin, cout, scr))


def _hosted(body, n_in, n_out, steps, comm):
    if not comm:
        return body, [], [], [], []
    ci, co, cs = len(comm[0]), len(comm[1]), len(comm[2])

    def wrapped(*refs):
        ins, cin = refs[:n_in], refs[n_in:n_in + ci]
        outs, cout = refs[n_in + ci:n_in + ci + n_out], refs[n_in + ci + n_out:n_in + ci + n_out + co]
        cscr, scr = refs[n_in + ci + n_out + co:n_in + ci + n_out + co + cs], refs[n_in + ci + n_out + co + cs:]
        _comm_at(comm, 3, steps, cin, cout, cscr)
        body(*ins, *outs, *scr)
        _comm_at(comm, 5, steps, cin, cout, cscr)
        _comm_at(comm, 4, steps, cin, cout, cscr)

    return wrapped, [_ANY] * ci, [_ANY] * co, list(comm[1]), list(comm[2])


def _mm(name, a, b, dims, out_dtype, tm, tn, tk, comm=None):
    (M, K) = a.shape if dims != "tn" else a.shape[::-1]
    N = b.shape[0] if dims == "nt" else b.shape[1]
    tm, tn, tk = min(tm, M), min(tn, N), min(tk, K)
    assert M % tm == 0 and N % tn == 0 and K % tk == 0, (name, M, N, K, tm, tn, tk)
    nk = K // tk
    steps = (M // tm, N // tn, nk)
    c_in, c_out, c_scr = comm[:3] if comm else ([], [], [])
    if dims == "nn":
        a_spec = pl.BlockSpec((tm, tk), lambda i, j, k: (i, k))
        b_spec = pl.BlockSpec((tk, tn), lambda i, j, k: (k, j))
    elif dims == "nt":
        a_spec = pl.BlockSpec((tm, tk), lambda i, j, k: (i, k))
        b_spec = pl.BlockSpec((tn, tk), lambda i, j, k: (j, k))
    else:
        a_spec = pl.BlockSpec((tk, tm), lambda i, j, k: (k, i))
        b_spec = pl.BlockSpec((tk, tn), lambda i, j, k: (k, j))

    n_acc = 1 if nk > 1 else 0

    def body(a_ref, b_ref, *rest):
        cin, o_ref = rest[:len(c_in)], rest[len(c_in)]
        cout = rest[len(c_in) + 1:len(c_in) + 1 + len(c_out)]
        scr = rest[len(c_in) + 1 + len(c_out):]
        _comm_at(comm, 3, steps, cin, cout, scr[n_acc:])
        if nk == 1:
            o_ref[...] = _dot(a_ref[...], b_ref[...], dims).astype(o_ref.dtype)
        else:
            acc_ref, k = scr[0], pl.program_id(2)

            @pl.when(k == 0)
            def _():
                acc_ref[...] = jnp.zeros_like(acc_ref)

            acc_ref[...] += _dot(a_ref[...], b_ref[...], dims)

            @pl.when(k == nk - 1)
            def _():
                o_ref[...] = acc_ref[...].astype(o_ref.dtype)

        _comm_at(comm, 5, steps, cin, cout, scr[n_acc:])
        _comm_at(comm, 4, steps, cin, cout, scr[n_acc:])

    res = _pcall(
        body, name=name, grid=steps,
        in_specs=[a_spec, b_spec] + [_ANY] * len(c_in),
        out_specs=[pl.BlockSpec((tm, tn), lambda i, j, k: (i, j))] + [_ANY] * len(c_out),
        out_shape=[jax.ShapeDtypeStruct((M, N), out_dtype)] + list(c_out),
        scratch_shapes=([pltpu.VMEM((tm, tn), F32)] if nk > 1 else []) + list(c_scr),
        compiler_params=_cparams(("arbitrary",) * 3 if comm else ("parallel", "parallel", "arbitrary")),
    )(a, b, *c_in)
    return res if comm else res[0]


def _tile(tr, w, cb=0):
    return pl.BlockSpec((tr, w), lambda i: (i, cb))


def _const(shape):
    nd = len(shape)
    return pl.BlockSpec(shape, lambda i: (0,) * nd)


def _acc_store(i, ref, val):
    @pl.when(i == 0)
    def _():
        ref[...] = val

    @pl.when(i > 0)
    def _():
        ref[...] += val


def _rms_fwd(cfg, x2, g, comm=None):
    T, D, tr = cfg.T, cfg.D, cfg.tr
    steps = (T // tr,)

    def body(x_ref, g_ref, h_ref):
        h_ref[...] = _rmsn(x_ref[...], g_ref[...]).astype(BF16)

    body, c_in, c_out, c_shapes, c_scr = _hosted(body, 2, 1, steps, comm)
    res = _pcall(body, name="rms_fwd", grid=steps, in_specs=[_tile(tr, D), _const((1, D))] + c_in,
                 out_specs=[_tile(tr, D)] + c_out, out_shape=[jax.ShapeDtypeStruct((T, D), BF16)] + c_shapes,
                 scratch_shapes=c_scr, compiler_params=_cparams(("arbitrary",) if comm else ("parallel",)),
                 )(x2, g, *(comm[0] if comm else []))
    return res if comm else res[0]


def _rms_bwd(cfg, x2, g, dh, dres, comm=None):
    T, D, tr = cfg.T, cfg.D, cfg.tr
    c_in, c_out, c_scr = comm[:3] if comm else ([], [], [])
    steps = (T // tr,)

    def body(x_ref, g_ref, dh_ref, dres_ref, *rest):
        cin, (gx_ref, dg_ref) = rest[:len(c_in)], rest[len(c_in):len(c_in) + 2]
        cout, scr = rest[len(c_in) + 2:len(c_in) + 2 + len(c_out)], rest[len(c_in) + 2 + len(c_out):]
        _comm_at(comm, 3, steps, cin, cout, scr)
        _, vjp = jax.vjp(_rmsn, x_ref[...], g_ref[...])
        dx, dg = vjp(dh_ref[...])
        gx_ref[...] = dx + dres_ref[...]
        _acc_store(pl.program_id(0), dg_ref, dg)
        _comm_at(comm, 4, steps, cin, cout, scr)

    return _pcall(body, name="rms_bwd", grid=steps,
                  in_specs=[_tile(tr, D), _const((1, D)), _tile(tr, D), _tile(tr, D)] + [_ANY] * len(c_in),
                  out_specs=[_tile(tr, D), _const((1, D))] + [_ANY] * len(c_out),
                  out_shape=[jax.ShapeDtypeStruct((T, D), F32), jax.ShapeDtypeStruct((1, D), F32)] + list(c_out),
                  scratch_shapes=list(c_scr), compiler_params=_cparams(("arbitrary",)))(x2, g, dh, dres, *c_in)


def _final(cfg, x2, mo, fg, target):
    T, D, tr = cfg.T, cfg.D, cfg.tr

    def loss_fn(hres, g, tgt):
        err = _rmsn(hres, g) - tgt
        return 0.5 * jnp.sum(jnp.mean(err * err, axis=-1, keepdims=True), axis=0, keepdims=True)

    def body(x_ref, mo_ref, g_ref, t_ref, loss_ref, dres_ref, dres16_ref, dg_ref):
        hres = x_ref[...] + mo_ref[...]
        loss, vjp = jax.vjp(functools.partial(loss_fn, tgt=t_ref[...]), hres, g_ref[...])
        dres, dg = vjp(jnp.ones((1, 1), F32))
        dres_ref[...] = dres
        dres16_ref[...] = dres.astype(BF16)
        i = pl.program_id(0)
        _acc_store(i, dg_ref, dg)
        _acc_store(i, loss_ref, jnp.broadcast_to(loss, (8, LANES)))

    return _pcall(body, name="final_loss", grid=(T // tr,),
                  in_specs=[_tile(tr, D), _tile(tr, D), _const((1, D)), _tile(tr, D)],
                  out_specs=[_const((8, LANES)), _tile(tr, D), _tile(tr, D), _const((1, D))],
                  out_shape=[jax.ShapeDtypeStruct((8, LANES), F32), jax.ShapeDtypeStruct((T, D), F32),
                             jax.ShapeDtypeStruct((T, D), BF16), jax.ShapeDtypeStruct((1, D), F32)],
                  compiler_params=_cparams(("arbitrary",)))(x2, mo, fg, target)


def _merge_fn(pa, pb, ga, gb):
    return jax.nn.sigmoid(ga) * pa + jax.nn.sigmoid(gb) * pb


def _merge_fwd(cfg, pa, pb, u):
    T, D, tr = cfg.T, cfg.D, cfg.tr
    cga, cgb = cfg.o_gate // D, cfg.o_gate // D + 1

    def body(pa_ref, pb_ref, ga_ref, gb_ref, m_ref):
        m_ref[...] = _merge_fn(pa_ref[...], pb_ref[...], ga_ref[...], gb_ref[...]).astype(BF16)

    return _pcall(body, name="merge_fwd", grid=(T // tr,),
                  in_specs=[_tile(tr, D), _tile(tr, D), _tile(tr, D, cga), _tile(tr, D, cgb)],
                  out_specs=_tile(tr, D), out_shape=jax.ShapeDtypeStruct((T, D), BF16),
                  compiler_params=_cparams(("parallel",)))(pa, pb, u, u)


def _merge_bwd(cfg, pa, pb, u, dm):
    T, D, tr = cfg.T, cfg.D, cfg.tr
    cga, cgb = cfg.o_gate // D, cfg.o_gate // D + 1

    def body(pa_ref, pb_ref, ga_ref, gb_ref, dm_ref, dpa_ref, dpb_ref, dg_ref):
        _, vjp = jax.vjp(_merge_fn, pa_ref[...], pb_ref[...], ga_ref[...], gb_ref[...])
        dpa, dpb, dga, dgb = vjp(dm_ref[...])
        dpa_ref[...] = dpa.astype(BF16)
        dpb_ref[...] = dpb.astype(BF16)
        dg_ref[:, :D] = dga.astype(BF16)
        dg_ref[:, D:] = dgb.astype(BF16)

    return _pcall(body, name="merge_bwd", grid=(T // tr,),
                  in_specs=[_tile(tr, D), _tile(tr, D), _tile(tr, D, cga), _tile(tr, D, cgb), _tile(tr, D)],
                  out_specs=[_tile(tr, D), _tile(tr, D), _tile(tr, 2 * D, cfg.o_gate // (2 * D))],
                  out_shape=[jax.ShapeDtypeStruct((T, D), BF16), jax.ShapeDtypeStruct((T, D), BF16),
                             jax.ShapeDtypeStruct((T, cfg.ncol), BF16)],
                  compiler_params=_cparams(("parallel",)))(pa, pb, u, u, dm)


def _gate_fn(o, z):
    return o * _silu(z)


def _gate_a_fwd(cfg, o, u):
    T, FW, tr = cfg.T, cfg.FW, cfg.tr

    def body(o_ref, z_ref, oa_ref):
        oa_ref[...] = _gate_fn(o_ref[...], z_ref[...]).astype(BF16)

    return _pcall(body, name="gate_a_fwd", grid=(T // tr,), in_specs=[_tile(tr, FW), _tile(tr, FW, 3)],
                  out_specs=_tile(tr, FW), out_shape=jax.ShapeDtypeStruct((T, FW), BF16),
                  compiler_params=_cparams(("parallel",)))(o, u)


def _gate_a_bwd(cfg, o, u, doa, du):
    T, FW, tr = cfg.T, cfg.FW, cfg.tr

    def body(o_ref, z_ref, doa_ref, du_in, do_ref, dz_ref):
        _, vjp = jax.vjp(_gate_fn, o_ref[...], z_ref[...])
        do, dz = vjp(doa_ref[...])
        do_ref[...] = do
        dz_ref[...] = dz.astype(BF16)

    return _pcall(body, name="gate_a_bwd", grid=(T // tr,),
                  in_specs=[_tile(tr, FW), _tile(tr, FW, 3), _tile(tr, FW), _ANY],
                  out_specs=[_tile(tr, FW), _tile(tr, FW, 3)],
                  out_shape=[jax.ShapeDtypeStruct((T, FW), F32), jax.ShapeDtypeStruct(du.shape, BF16)],
                  input_output_aliases={3: 1},
                  compiler_params=_cparams(("parallel",)))(o, u, doa, du)


def _fox_prep(cfg, u, fb):
    T, tr = cfg.T, cfg.tr
    cf = cfg.o_f // LANES

    def body(f_ref, fb_ref, c_ref, carry_ref):
        i = pl.program_id(0)

        @pl.when(i == 0)
        def _():
            carry_ref[...] = jnp.zeros_like(carry_ref)

        lf = -_softplus(-(f_ref[...] + fb_ref[...]))
        r = lax.broadcasted_iota(jnp.int32, (tr, tr), 0)
        c = lax.broadcasted_iota(jnp.int32, (tr, tr), 1)
        tri = (r >= c).astype(F32)
        c_ref[...] = _dot(tri, lf, precision=HI) + carry_ref[...]
        carry_ref[...] += jnp.sum(lf, axis=0, keepdims=True)

    return _pcall(body, name="fox_prep", grid=(T // tr,), in_specs=[_tile(tr, LANES, cf), _const((1, LANES))],
                  out_specs=_tile(tr, LANES), out_shape=jax.ShapeDtypeStruct((T, LANES), F32),
                  scratch_shapes=[pltpu.VMEM((1, LANES), F32)], compiler_params=_cparams(("arbitrary",)))(u, fb)


def _fox_prep_bwd(cfg, u, fb, dc):
    T, tr = cfg.T, cfg.tr
    cf = cfg.o_f // LANES
    nb = T // tr

    def body(f_ref, fb_ref, dc_ref, df_ref, dfb_ref, carry_ref):
        i = pl.program_id(0)

        @pl.when(i == 0)
        def _():
            carry_ref[...] = jnp.zeros_like(carry_ref)

        dc = dc_ref[...]
        r = lax.broadcasted_iota(jnp.int32, (tr, tr), 0)
        c = lax.broadcasted_iota(jnp.int32, (tr, tr), 1)
        triu = (r <= c).astype(F32)
        dlf = _dot(triu, dc, precision=HI) + carry_ref[...]
        carry_ref[...] += jnp.sum(dc, axis=0, keepdims=True)
        dz = dlf * jax.nn.sigmoid(-(f_ref[...] + fb_ref[...]))
        df_ref[...] = dz.astype(BF16)
        _acc_store(i, dfb_ref, jnp.sum(dz, axis=0, keepdims=True))

    rev = lambda i: (nb - 1 - i, 0)
    return _pcall(body, name="fox_prep_bwd", grid=(nb,),
                  in_specs=[pl.BlockSpec((tr, LANES), lambda i: (nb - 1 - i, cf)), _const((1, LANES)),
                            pl.BlockSpec((tr, LANES), rev)],
                  out_specs=[pl.BlockSpec((tr, LANES), rev), _const((1, LANES))],
                  out_shape=[jax.ShapeDtypeStruct((T, LANES), BF16), jax.ShapeDtypeStruct((1, LANES), F32)],
                  scratch_shapes=[pltpu.VMEM((1, LANES), F32)], compiler_params=_cparams(("arbitrary",)))(u, fb, dc)


def _attn_logits(q_ref, k_ref, c_ref, tq, te):
    q = q_ref[...].astype(BF16)
    scale = FOX_HEAD_DIM ** -0.5
    part = lambda k0, k1: _dot(q, k_ref[k0:k1, :].astype(BF16), "nt") * scale - c_ref[0, :, k0:k1]
    row = lax.broadcasted_iota(jnp.int32, (tq, tq), 0)
    col = lax.broadcasted_iota(jnp.int32, (tq, tq), 1)
    own = ((te - tq, te), jnp.where(col <= row, part(te - tq, te), -1e30))
    return [((0, te - tq), part(0, te - tq)), own] if te > tq else [own]


def _per_query_tile(i, nq, tq, fn):
    for ii in range(nq):
        pl.when(i == ii)(functools.partial(fn, (ii + 1) * tq))


def _attn_fwd(cfg, u, c_rows):
    T, FW, FH = cfg.T, cfg.FW, cfg.FH
    tq = min(256, T)
    dh = FOX_HEAD_DIM

    def body(q_ref, k_ref, v_ref, c_ref, o_ref, lse_ref):
        i = pl.program_id(1)

        def tile(te):
            parts = _attn_logits(q_ref, k_ref, c_ref, tq, te)
            m = functools.reduce(jnp.maximum, [jnp.max(s, axis=1, keepdims=True) for _, s in parts])
            l, acc = 0.0, 0.0
            for (k0, k1), s in parts:
                p = jnp.exp(s - m)
                l = l + jnp.sum(p, axis=1, keepdims=True)
                acc = acc + _dot(p.astype(BF16), v_ref[k0:k1, :].astype(BF16))
            o_ref[...] = acc / l
            lse_ref[0] = m + jnp.log(l)

        _per_query_tile(i, T // tq, tq, tile)

    return _pcall(
        body, name="fox_attn_fwd", grid=(FH, T // tq),
        in_specs=[pl.BlockSpec((tq, dh), lambda h, i: (i, 3 * h)), pl.BlockSpec((T, dh), lambda h, i: (0, 3 * h + 1)),
                  pl.BlockSpec((T, dh), lambda h, i: (0, 3 * h + 2)), pl.BlockSpec((1, 1, T), lambda h, i: (h, 0, 0))],
        out_specs=[pl.BlockSpec((tq, dh), lambda h, i: (i, h)), pl.BlockSpec((1, tq, 1), lambda h, i: (h, i, 0))],
        out_shape=[jax.ShapeDtypeStruct((T, FW), F32), jax.ShapeDtypeStruct((FH, T, 1), F32)],
        compiler_params=_cparams(("parallel", "arbitrary")),
    )(u, u, u, c_rows)


def _attn_bwd(cfg, u, c_rows, lse, do, du):
    T, FW, FH = cfg.T, cfg.FW, cfg.FH
    tq = min(256, T)
    nq = T // tq
    dh = FOX_HEAD_DIM
    scale = dh ** -0.5

    def body(q_ref, k_ref, v_ref, c_ref, lse_ref, do_ref, du_in, du_ref, dcol_ref, dk_acc, dv_acc):
        i = pl.program_id(1)

        @pl.when(i == 0)
        def _():
            dk_acc[...] = jnp.zeros_like(dk_acc)
            dv_acc[...] = jnp.zeros_like(dv_acc)
            dcol_ref[...] = jnp.zeros_like(dcol_ref)

        def tile(te):
            lse, q16, do16 = lse_ref[0], q_ref[...].astype(BF16), do_ref[...].astype(BF16)
            probs = [(ks, jnp.exp(s - lse)) for ks, s in _attn_logits(q_ref, k_ref, c_ref, tq, te)]
            dps = [_dot(do16, v_ref[k0:k1, :].astype(BF16), "nt") for (k0, k1), _ in probs]
            delta = sum(jnp.sum(p * dp, axis=1, keepdims=True) for (_, p), dp in zip(probs, dps))
            dq = 0.0
            for ((k0, k1), p), dp in zip(probs, dps):
                ds = p * (dp - delta)
                ds16 = ds.astype(BF16)
                dq = dq + _dot(ds16, k_ref[k0:k1, :].astype(BF16))
                dk_acc[k0:k1, :] += _dot(ds16, q16, "tn") * scale
                dv_acc[k0:k1, :] += _dot(p.astype(BF16), do16, "tn")
                dcol_ref[0, :, k0:k1] += jnp.sum(ds, axis=0, keepdims=True)
            du_ref[te - tq:te, 0:dh] = (dq * scale).astype(BF16)

        _per_query_tile(i, nq, tq, tile)

        @pl.when(i == nq - 1)
        def _():
            du_ref[:, dh:2 * dh] = dk_acc[...].astype(BF16)
            du_ref[:, 2 * dh:3 * dh] = dv_acc[...].astype(BF16)

    return _pcall(
        body, name="fox_attn_bwd", grid=(FH, nq),
        in_specs=[pl.BlockSpec((tq, dh), lambda h, i: (i, 3 * h)), pl.BlockSpec((T, dh), lambda h, i: (0, 3 * h + 1)),
                  pl.BlockSpec((T, dh), lambda h, i: (0, 3 * h + 2)), pl.BlockSpec((1, 1, T), lambda h, i: (h, 0, 0)),
                  pl.BlockSpec((1, tq, 1), lambda h, i: (h, i, 0)), pl.BlockSpec((tq, dh), lambda h, i: (i, h)), _ANY],
        out_specs=[pl.BlockSpec((T, 3 * dh), lambda h, i: (0, h)), pl.BlockSpec((1, 1, T), lambda h, i: (h, 0, 0))],
        out_shape=[jax.ShapeDtypeStruct(du.shape, BF16), jax.ShapeDtypeStruct((FH, 1, T), F32)],
        scratch_shapes=[pltpu.VMEM((T, dh), F32), pltpu.VMEM((T, dh), F32)],
        input_output_aliases={6: 0},
        compiler_params=_cparams(("parallel", "arbitrary")),
    )(u, u, u, c_rows, lse, do, du)


def _head_indicators(cfg):
    ind = np.zeros((cfg.RW, LANES), np.float32)
    ind[np.arange(cfg.RW), np.arange(cfg.RW) // RWKV_HEAD_DIM] = 1.0
    pad = np.zeros((1, LANES), np.float32)
    pad[0, cfg.RH:] = 1.0
    return jnp.asarray(ind), jnp.asarray(ind.T.copy()), jnp.asarray(pad)


def _prep_fn(us_r, us_k, us_v, us_wd, us_ad, w0, w2p, a0, a2p, k_k, k_a, ind, ind_t, pad):
    wpre = w0 + _dot3(jnp.tanh(us_wd), w2p)
    w = -_softplus(-wpre) - 0.5
    lw = -jnp.exp(w)
    a = jax.nn.sigmoid(a0 + _dot3(us_ad, a2p))
    kk = us_k * k_k
    ss = _xdot(kk * kk, ind, ind_t) + pad
    inv = 1.0 / jnp.maximum(jnp.sqrt(ss), L2_EPS)
    kkn = kk * _xdot(inv, ind_t, ind)
    kp = us_k * (1.0 + (a - 1.0) * k_a)
    return us_r, lw, kp, us_v, -kkn, kkn * a


def _shifted(u, prev_row, mu, first):
    n = u.shape[0]
    rolled = pltpu.roll(u, 1, 0)
    row = lax.broadcasted_iota(jnp.int32, u.shape, 0)
    p0 = jnp.where(first, jnp.zeros_like(prev_row), prev_row)
    prev = jnp.where(row == 0, jnp.broadcast_to(p0, u.shape), rolled)
    return u + (prev - u) * mu, prev


def _rwkv_specs(cfg, tr):
    RW, LP = cfg.RW, cfg.LP
    base = cfg.o_rwkv // RW
    cols = [(RW, base), (RW, base + 1), (RW, base + 2), (RW, base + 3), (LP, cfg.o_wd // LP), (LP, cfg.o_ad // LP)]
    cur = [pl.BlockSpec((tr, w), (lambda i, cb=cb: (i, cb))) for w, cb in cols]
    prv = [pl.BlockSpec((8, w), (lambda i, cb=cb: (jnp.maximum(i * (tr // 8) - 1, 0), cb))) for w, cb in cols]
    return cols, cur, prv


def _mu_pieces(cfg, mu_ref):
    RW, LP = cfg.RW, cfg.LP
    offs = [0, RW, 2 * RW, 3 * RW, 4 * RW, 4 * RW + LP, 4 * RW + 2 * LP]
    return [mu_ref[:, offs[j]:offs[j + 1]] for j in range(6)]


def _rwkv_prep_fwd(cfg, u, mu, w0, w2p, a0, a2p, k_k, k_a):
    T, RW, LP, tr = cfg.T, cfg.RW, cfg.LP, cfg.tr
    ind, ind_t, pad = _head_indicators(cfg)
    cols, cur, prv = _rwkv_specs(cfg, tr)

    def body(*refs):
        u_refs, p_refs = refs[0:6], refs[6:12]
        mu_ref, w0_ref, w2_ref, a0_ref, a2_ref, kk_ref, ka_ref, ind_ref, indt_ref, pad_ref = refs[12:22]
        outs = refs[22:]
        first = pl.program_id(0) == 0
        mus = _mu_pieces(cfg, mu_ref)
        us = [_shifted(u_refs[j][...], p_refs[j][7:8, :], mus[j], first)[0] for j in range(6)]
        res = _prep_fn(us[0], us[1], us[2], us[4], us[5], w0_ref[...], w2_ref[...], a0_ref[...], a2_ref[...],
                       kk_ref[...], ka_ref[...], ind_ref[...], indt_ref[...], pad_ref[...])
        for j in range(6):
            outs[j][...] = res[j]
        outs[6][...] = us[3]

    consts = [mu, w0, w2p, a0, a2p, k_k, k_a, ind, ind_t, pad]
    return _pcall(body, name="rwkv_prep_fwd", grid=(T // tr,),
                  in_specs=cur + prv + [_const(c.shape) for c in consts],
                  out_specs=[_tile(tr, RW)] * 7, out_shape=[jax.ShapeDtypeStruct((T, RW), F32)] * 7,
                  compiler_params=_cparams(("parallel",)))(*([u] * 12), *consts)


def _rwkv_prep_bwd(cfg, u, mu, w0, w2p, a0, a2p, k_k, k_a, cots, dzb):
    T, RW, LP = cfg.T, cfg.RW, cfg.LP
    tr = min(128, T)
    ind, ind_t, pad = _head_indicators(cfg)
    cols, cur, prv = _rwkv_specs(cfg, tr)
    rseg = cfg.rseg

    def body(*refs):
        u_refs, p_refs = refs[0:6], refs[6:12]
        mu_ref, w0_ref, w2_ref, a0_ref, a2_ref, kk_ref, ka_ref, ind_ref, indt_ref, pad_ref = refs[12:22]
        cot_refs, dzb_ref = refs[22:28], refs[28]
        dus_ref, dmu_ref, dw0_ref, dw2_ref, da0_ref, da2_ref, dkk_ref, dka_ref = refs[29:]
        i = pl.program_id(0)
        first = i == 0
        mus = _mu_pieces(cfg, mu_ref)
        sh = [_shifted(u_refs[j][...], p_refs[j][7:8, :], mus[j], first) for j in range(6)]
        us = [s[0] for s in sh]
        fn = functools.partial(_prep_fn, ind=ind_ref[...], ind_t=indt_ref[...], pad=pad_ref[...])
        _, vjp = jax.vjp(fn, us[0], us[1], us[2], us[4], us[5], w0_ref[...], w2_ref[...], a0_ref[...], a2_ref[...],
                         kk_ref[...], ka_ref[...])
        d = vjp(tuple(c[...] for c in cot_refs))
        dus = [d[0], d[1], d[2], dzb_ref[...], d[3], d[4]]
        offs = [0, RW, 2 * RW, 3 * RW, 4 * RW, 4 * RW + LP, 4 * RW + 2 * LP]
        for j in range(6):
            dus_ref[:, offs[j]:offs[j + 1]] = dus[j]
            dmu_j = jnp.sum(dus[j] * (sh[j][1] - u_refs[j][...]), axis=0, keepdims=True)

            @pl.when(first)
            def _(j=j, dmu_j=dmu_j):
                dmu_ref[:, offs[j]:offs[j + 1]] = dmu_j

            @pl.when(i > 0)
            def _(j=j, dmu_j=dmu_j):
                dmu_ref[:, offs[j]:offs[j + 1]] += dmu_j
        for ref, val in zip((dw0_ref, dw2_ref, da0_ref, da2_ref, dkk_ref, dka_ref), d[5:11]):
            _acc_store(i, ref, val)

    consts = [mu, w0, w2p, a0, a2p, k_k, k_a, ind, ind_t, pad]
    vec = jax.ShapeDtypeStruct((1, RW), F32)
    mat = jax.ShapeDtypeStruct((LP, RW), F32)
    return _pcall(body, name="rwkv_prep_bwd", grid=(T // tr,),
                  in_specs=cur + prv + [_const(c.shape) for c in consts] + [_tile(tr, RW)] * 7,
                  out_specs=[_tile(tr, rseg), _const((1, rseg)), _const((1, RW)), _const((LP, RW)), _const((1, RW)),
                             _const((LP, RW)), _const((1, RW)), _const((1, RW))],
                  out_shape=[jax.ShapeDtypeStruct((T, rseg), F32), jax.ShapeDtypeStruct((1, rseg), F32),
                             vec, mat, vec, mat, vec, vec],
                  compiler_params=_cparams(("arbitrary",)))(*([u] * 12), *consts, *cots, dzb)


def _shift_bwd(cfg, dus, mu, df, du):
    T, tr, RW, LP = cfg.T, cfg.tr, cfg.RW, cfg.LP
    nb = T // tr
    tail = cfg.ncol - cfg.o_f
    assert cfg.o_rwkv % (4 * RW) == 0 and (4 * RW) % (2 * LP) == 0 and cfg.o_f % tail == 0

    def shifted(d_ref, n_ref, mu_ref):
        d = d_ref[...]
        rolled = pltpu.roll(d, tr - 1, 0)
        row = lax.broadcasted_iota(jnp.int32, d.shape, 0)
        n0 = jnp.where(pl.program_id(0) == nb - 1, jnp.zeros_like(n_ref[0:1, :]), n_ref[0:1, :])
        nxt = jnp.where(row == tr - 1, jnp.broadcast_to(n0, d.shape), rolled)
        mu_v = mu_ref[...]
        return (d * (1.0 - mu_v) + nxt * mu_v).astype(BF16)

    def main_body(d_ref, n_ref, mu_ref, du_in, du_ref):
        du_ref[...] = shifted(d_ref, n_ref, mu_ref)

    def tail_body(d_ref, n_ref, mu_ref, df_ref, du_in, du_ref):
        du_ref[:, 0:LANES] = df_ref[...]
        du_ref[:, LANES:LANES + 2 * LP] = shifted(d_ref, n_ref, mu_ref)
        if tail > LANES + 2 * LP:
            du_ref[:, LANES + 2 * LP:] = jnp.zeros((tr, tail - LANES - 2 * LP), BF16)

    def specs(w, cb):
        return [_tile(tr, w, cb),
                pl.BlockSpec((8, w), lambda i: (jnp.minimum((i + 1) * (tr // 8), T // 8 - 1), cb)),
                pl.BlockSpec((1, w), lambda i: (0, cb))]

    out = jax.ShapeDtypeStruct(du.shape, BF16)
    du = _pcall(main_body, name="shift_bwd_main", grid=(nb,), in_specs=specs(4 * RW, 0) + [_ANY],
                out_specs=_tile(tr, 4 * RW, cfg.o_rwkv // (4 * RW)), out_shape=out, input_output_aliases={3: 0},
                compiler_params=_cparams(("parallel",)))(dus, dus, mu, du)
    return _pcall(tail_body, name="shift_bwd_tail", grid=(nb,),
                  in_specs=specs(2 * LP, 4 * RW // (2 * LP)) + [_tile(tr, LANES), _ANY],
                  out_specs=_tile(tr, tail, cfg.o_f // tail), out_shape=out, input_output_aliases={4: 0},
                  compiler_params=_cparams(("parallel",)))(dus, dus, mu, df, du)


def _chunk_local(r, lw, k, v, a, b):
    H, C, K = r.shape
    row = lax.broadcasted_iota(jnp.int32, (C, C), 0)
    col = lax.broadcasted_iota(jnp.int32, (C, C), 1)
    incl = jnp.broadcast_to((row >= col).astype(F32)[None], (H, C, C))
    strict = (row > col)[None]
    lower = (row >= col)[None]
    eye = (row == col)[None]
    zero = jnp.zeros((), F32)
    L = _bdot(incl, lw, 2, 1)
    LC = jnp.sum(lw, axis=1, keepdims=True)
    eL = jnp.exp(L)
    eLn = jnp.exp(-L)
    at = a * jnp.exp(L - lw)
    rt = r * eL
    bt = b * eLn
    kt = k * eLn
    eR = jnp.exp(LC - L)
    bh = b * eR
    kh = k * eR
    keys = functools.partial(_bdot, passes=SCAN_PASSES[0])
    inv = functools.partial(_bdot, passes=SCAN_PASSES[1])
    app = functools.partial(_bdot, passes=SCAN_PASSES[2])
    ar = jnp.concatenate([at, rt], axis=1)
    g_b = app(ar, bt, 2, 2)
    g_k = keys(ar, kt, 2, 2)
    n_ab = jnp.where(strict, g_b[:, :C], zero)
    n_ak = jnp.where(strict, g_k[:, :C], zero)
    m_rb = jnp.where(lower, g_b[:, C:], zero)
    m_rk = jnp.where(lower, g_k[:, C:], zero)
    M = n_ab
    P = jnp.where(eye, 1.0, zero) + n_ab
    for _ in range(1, max(1, int(np.ceil(np.log2(C))))):
        M = inv(M, M, 2, 1)
        P = P + inv(M, P, 2, 1)
    W = app(P, at, 2, 1)
    Uloc = app(P, app(n_ak, v, 2, 1), 2, 1)
    Q = rt + app(m_rb, W, 2, 1)
    Yloc = app(m_rb, Uloc, 2, 1) + app(m_rk, v, 2, 1)
    A = jnp.where(eye, jnp.exp(LC), zero) + app(W, bh, 1, 1)
    Sloc = app(Uloc, bh, 1, 1) + app(v, kh, 1, 1)
    return Q, Yloc, A, Sloc


def _split_heads(ref, n):
    N = RWKV_HEAD_DIM
    return jnp.stack([ref[:, h * N:(h + 1) * N] for h in range(n)], axis=0)


def _merge_heads(x):
    return jnp.concatenate([x[h] for h in range(x.shape[0])], axis=1)


def _chains(x, cb):
    hb = x.shape[0]
    return x.reshape(hb, cb, -1, x.shape[-1]).reshape(hb * cb, -1, x.shape[-1])


def _unchains(x, cb, seq):
    hb = x.shape[0] // cb
    x = x.reshape(hb, cb, x.shape[1], x.shape[2])
    return x.reshape(hb, cb * x.shape[2], x.shape[3]) if seq else x


def _scan_local_specs(cfg):
    N, HB, CB = RWKV_HEAD_DIM, cfg.hb, cfg.cb
    grid = (cfg.RH // HB, cfg.T // (CB * cfg.C))
    seq = pl.BlockSpec((HB, CB * cfg.C, N), lambda h, j: (h, j, 0))
    mat = pl.BlockSpec((HB, CB, N, N), lambda h, j: (h, j, 0, 0))
    return grid, seq, mat


def _scan_local_fwd(cfg, seqs):
    T, RH, N = cfg.T, cfg.RH, RWKV_HEAD_DIM
    grid, seq, mat = _scan_local_specs(cfg)

    def body(r_ref, lw_ref, k_ref, v_ref, a_ref, b_ref, q_ref, yl_ref, a_out, sl_ref):
        ins = [_chains(_split_heads(ref, cfg.hb), cfg.cb) for ref in (r_ref, lw_ref, k_ref, v_ref, a_ref, b_ref)]
        Q, Yloc, A, Sloc = _chunk_local(*ins)
        q_ref[...] = _unchains(Q, cfg.cb, True)
        yl_ref[...] = _unchains(Yloc, cfg.cb, True)
        a_out[...] = _unchains(A, cfg.cb, False)
        sl_ref[...] = _unchains(Sloc, cfg.cb, False)

    tok = pl.BlockSpec((cfg.cb * cfg.C, cfg.hb * N), lambda h, j: (j, h))
    sq = jax.ShapeDtypeStruct((RH, T, N), F32)
    mt = jax.ShapeDtypeStruct((RH, T // cfg.C, N, N), F32)
    return _pcall(body, name="rwkv_scan_local_fwd", grid=grid, in_specs=[tok] * 6, out_specs=[seq, seq, mat, mat],
                  out_shape=[sq, sq, mt, mt], compiler_params=_cparams(("parallel", "parallel")))(*seqs)


def _scan_local_bwd(cfg, toks, dq, dy, da, dsl, extra, comm=None):
    T, RW, N = cfg.T, cfg.RW, RWKV_HEAD_DIM
    grid, seq, mat = _scan_local_specs(cfg)
    c_in, c_out, c_scr = comm[:3] if comm else ([], [], [])

    def body(r_ref, lw_ref, k_ref, v_ref, a_ref, b_ref, dq_ref, dy_ref, da_ref, dsl_ref, xr_ref, xk_ref, xv_ref,
             *rest):
        cin, outs = rest[:len(c_in)], rest[len(c_in):len(c_in) + 6]
        cout, scr = rest[len(c_in) + 6:len(c_in) + 6 + len(c_out)], rest[len(c_in) + 6 + len(c_out):]
        _comm_at(comm, 3, grid, cin, cout, scr)
        ins = [_chains(_split_heads(ref, cfg.hb), cfg.cb) for ref in (r_ref, lw_ref, k_ref, v_ref, a_ref, b_ref)]
        _, vjp = jax.vjp(_chunk_local, *ins)
        d = vjp((_chains(dq_ref[...], cfg.cb), _chains(_split_heads(dy_ref, cfg.hb), cfg.cb),
                 _chains(da_ref[...], cfg.cb), _chains(dsl_ref[...], cfg.cb)))
        add = {0: xr_ref, 2: xk_ref, 3: xv_ref}
        for j in range(6):
            dj = _merge_heads(_unchains(d[j], cfg.cb, True))
            outs[j][...] = dj + add[j][...] if j in add else dj
        _comm_at(comm, 4, grid, cin, cout, scr)

    tok = pl.BlockSpec((cfg.cb * cfg.C, cfg.hb * N), lambda h, j: (j, h))
    return _pcall(body, name="rwkv_scan_local_bwd", grid=grid,
                  in_specs=[tok] * 6 + [seq, tok, mat, mat] + [tok] * 3 + [_ANY] * len(c_in),
                  out_specs=[tok] * 6 + [_ANY] * len(c_out),
                  out_shape=[jax.ShapeDtypeStruct((T, RW), F32)] * 6 + list(c_out), scratch_shapes=list(c_scr),
                  compiler_params=_cparams(("arbitrary", "arbitrary") if comm else ("parallel", "parallel")),
                  )(*toks, dq, dy, da, dsl, *extra, *c_in)


def _scan_carry_specs(cfg, rev):
    N, RH, C, nc = RWKV_HEAD_DIM, cfg.RH, cfg.C, cfg.T // cfg.C
    at = (lambda j: nc - 1 - j) if rev else (lambda j: j)
    seq = pl.BlockSpec((RH, C, N), lambda j: (0, at(j), 0))
    mat = pl.BlockSpec((RH, 1, N, N), lambda j: (0, at(j), 0, 0))
    return nc, seq, mat


def _scan_carry_fwd(cfg, q, yloc, a, sloc):
    T, RH, N = cfg.T, cfg.RH, RWKV_HEAD_DIM
    nc, seq, mat = _scan_carry_specs(cfg, False)

    def body(q_ref, yl_ref, a_ref, sl_ref, y_ref, ck_ref, s_ref):
        @pl.when(pl.program_id(0) == 0)
        def _():
            s_ref[...] = jnp.zeros_like(s_ref)

        S = s_ref[...]
        ck_ref[:, 0] = S
        y_ref[...] = _merge_heads(_bdot(q_ref[...], S, 2, 2, SCAN_PASSES[2]) + yl_ref[...])
        s_ref[...] = _bdot(S, a_ref[:, 0], 2, 1) + sl_ref[:, 0]

    tok = pl.BlockSpec((cfg.C, cfg.RW), lambda j: (j, 0))
    return _pcall(body, name="rwkv_scan_carry_fwd", grid=(nc,), in_specs=[seq, seq, mat, mat], out_specs=[tok, mat],
                  out_shape=[jax.ShapeDtypeStruct((T, cfg.RW), F32), jax.ShapeDtypeStruct((RH, nc, N, N), F32)],
                  scratch_shapes=[pltpu.VMEM((RH, N, N), F32)],
                  compiler_params=_cparams(("arbitrary",)))(q, yloc, a, sloc)


def _scan_carry_bwd(cfg, q, a, ckpt, dy, comm=None):
    T, RH, N = cfg.T, cfg.RH, RWKV_HEAD_DIM
    nc, seq, mat = _scan_carry_specs(cfg, True)

    def body(q_ref, a_ref, ck_ref, dy_ref, dq_ref, da_ref, dsl_ref, ds_ref):
        @pl.when(pl.program_id(0) == 0)
        def _():
            ds_ref[...] = jnp.zeros_like(ds_ref)

        S, dS, dY = ck_ref[:, 0], ds_ref[...], _split_heads(dy_ref, RH)
        dq_ref[...] = _bdot(dY, S, 2, 1, SCAN_PASSES[2])
        da_ref[:, 0] = _bdot(S, dS, 1, 1, SCAN_PASSES[2])
        dsl_ref[:, 0] = dS
        ds_ref[...] = _bdot(dS, a_ref[:, 0], 2, 2) + _bdot(dY, q_ref[...], 1, 1, SCAN_PASSES[2])

    mt = jax.ShapeDtypeStruct((RH, nc, N, N), F32)
    tok = pl.BlockSpec((cfg.C, cfg.RW), lambda j: (nc - 1 - j, 0))
    body, c_in, c_out, c_shapes, c_scr = _hosted(body, 4, 3, (nc,), comm)
    return _pcall(body, name="rwkv_scan_carry_bwd", grid=(nc,), in_specs=[seq, mat, mat, tok] + c_in,
                  out_specs=[seq, mat, mat] + c_out,
                  out_shape=[jax.ShapeDtypeStruct((RH, T, N), F32), mt, mt] + c_shapes,
                  scratch_shapes=c_scr + [pltpu.VMEM((RH, N, N), F32)],
                  compiler_params=_cparams(("arbitrary",)))(q, a, ckpt, dy, *(comm[0] if comm else []))


def _post_fn(y, r, kp, v, zb, ln_w, ln_b, rk, ind, ind_t):
    n = float(RWKV_HEAD_DIM)
    mu = _xdot(_xdot(y, ind, ind_t) / n, ind_t, ind)
    yc = y - mu
    var = _xdot(yc * yc, ind, ind_t) / n
    rstd = _xdot(lax.rsqrt(var + GN_EPS), ind_t, ind)
    yn = yc * rstd * ln_w + ln_b
    bonus = _xdot(_xdot(r * kp * rk, ind, ind_t), ind_t, ind) * v
    return (yn + bonus) * _silu(zb)


def _rwkv_post_fwd(cfg, y, r, kp, v, zb, ln_w, ln_b, rk):
    T, RW, tr = cfg.T, cfg.RW, cfg.tr
    ind, ind_t, _ = _head_indicators(cfg)

    def body(y_ref, r_ref, k_ref, v_ref, z_ref, lw_ref, lb_ref, rk_ref, ind_ref, indt_ref, ob_ref):
        ob_ref[...] = _post_fn(y_ref[...], r_ref[...], k_ref[...], v_ref[...], z_ref[...], lw_ref[...], lb_ref[...],
                               rk_ref[...], ind_ref[...], indt_ref[...]).astype(BF16)

    consts = [ln_w, ln_b, rk, ind, ind_t]
    return _pcall(body, name="rwkv_post_fwd", grid=(T // tr,),
                  in_specs=[_tile(tr, RW)] * 5 + [_const(c.shape) for c in consts],
                  out_specs=_tile(tr, RW), out_shape=jax.ShapeDtypeStruct((T, RW), BF16),
                  compiler_params=_cparams(("parallel",)))(y, r, kp, v, zb, *consts)


def _rwkv_post_bwd(cfg, y, r, kp, v, zb, ln_w, ln_b, rk, dob):
    T, RW = cfg.T, cfg.RW
    tr = min(128, T)
    ind, ind_t, _ = _head_indicators(cfg)

    def body(y_ref, r_ref, k_ref, v_ref, z_ref, lw_ref, lb_ref, rk_ref, ind_ref, indt_ref, dob_ref,
             dy_ref, dr_ref, dk_ref, dv_ref, dz_ref, dlw_ref, dlb_ref, drk_ref):
        fn = functools.partial(_post_fn, ind=ind_ref[...], ind_t=indt_ref[...])
        _, vjp = jax.vjp(fn, y_ref[...], r_ref[...], k_ref[...], v_ref[...], z_ref[...], lw_ref[...], lb_ref[...],
                         rk_ref[...])
        d = vjp(dob_ref[...])
        for ref, val in zip((dy_ref, dr_ref, dk_ref, dv_ref, dz_ref), d[:5]):
            ref[...] = val
        i = pl.program_id(0)
        for ref, val in zip((dlw_ref, dlb_ref, drk_ref), d[5:8]):
            _acc_store(i, ref, val)

    consts = [ln_w, ln_b, rk, ind, ind_t]
    vec = jax.ShapeDtypeStruct((1, RW), F32)
    return _pcall(body, name="rwkv_post_bwd", grid=(T // tr,),
                  in_specs=[_tile(tr, RW)] * 5 + [_const(c.shape) for c in consts] + [_tile(tr, RW)],
                  out_specs=[_tile(tr, RW)] * 5 + [_const((1, RW))] * 3,
                  out_shape=[jax.ShapeDtypeStruct((T, RW), F32)] * 5 + [vec] * 3,
                  compiler_params=_cparams(("arbitrary",)))(y, r, kp, v, zb, *consts, dob)


def _adamw_math(w, g, m, v):
    m = ADAM_B1 * m + (1.0 - ADAM_B1) * g
    v = ADAM_B2 * v + (1.0 - ADAM_B2) * (g * g)
    m_hat = m / (1.0 - ADAM_B1 ** ADAM_STEP)
    v_hat = v / (1.0 - ADAM_B2 ** ADAM_STEP)
    delta = -ADAM_LR * (m_hat / (jnp.sqrt(v_hat) + ADAM_EPS) + ADAM_WD * w)
    return delta, m, v


def _adamw(name, w, g, m, v, copy_grad=False, comm=None):
    R, Cc = w.shape
    Rp = -(-R // 8) * 8
    tr = Rp
    for nb in range(1, Rp // 8 + 1):
        if (Rp // 8) % nb == 0 and (Rp // nb) * Cc * 4 <= 2 * 1024 * 1024:
            tr = Rp // nb
            break

    def body(w_ref, g_ref, m_ref, v_ref, d_ref, nm_ref, nv_ref, *g_out):
        g_v = g_ref[...]
        d, nm, nv = _adamw_math(w_ref[...], g_v, m_ref[...], v_ref[...])
        d_ref[...] = d
        nm_ref[...] = nm
        nv_ref[...] = nv
        if copy_grad:
            g_out[0][...] = g_v

    spec = _tile(tr, Cc)
    n_out = 4 if copy_grad else 3
    body, c_in, c_out, c_shapes, c_scr = _hosted(body, 4, n_out, (Rp // tr,), comm)
    return _pcall(body, name=name, grid=(Rp // tr,), in_specs=[spec] * 4 + c_in, out_specs=[spec] * n_out + c_out,
                  out_shape=[jax.ShapeDtypeStruct((R, Cc), F32)] * n_out + c_shapes, scratch_shapes=c_scr,
                  compiler_params=_cparams(("arbitrary",) if comm else ("parallel",)),
                  )(w, g, m, v, *(comm[0] if comm else []))


def _row_tile(R, Cc, itemsize, budget=2 * 1024 * 1024):
    for nb in range(1, R // 16 + 1):
        if R % nb == 0 and (R // nb) % 16 == 0 and (R // nb) * Cc * itemsize <= budget:
            return R // nb
    return R


def _add_halves(name, gs, r1, c_idx):
    S, R, Cc = gs.shape
    half = R // 2
    tr = _row_tile(half, Cc, 4)
    nb = half // tr

    def body(c_ref, g_ref, r_ref, o_ref):
        o_ref[...] = (g_ref[...].astype(F32) + r_ref[...].astype(F32)).astype(BF16)

    grid_spec = pltpu.PrefetchScalarGridSpec(
        num_scalar_prefetch=1, grid=(S, nb),
        in_specs=[pl.BlockSpec((1, tr, Cc), lambda s, i, c: (s, c[0] * nb + i, 0)),
                  pl.BlockSpec((1, tr, Cc), lambda s, i, c: (s, i, 0))],
        out_specs=pl.BlockSpec((1, tr, Cc), lambda s, i, c: (s, i, 0)))
    return _pcall(body, name=name, grid_spec=grid_spec, out_shape=jax.ShapeDtypeStruct((S, half, Cc), BF16),
                  compiler_params=_cparams(("parallel", "parallel")))(c_idx, gs, r1)


def _sum_slots(name, r2):
    S, R, Cc = r2.shape
    tr = _row_tile(R, Cc, 4 * S // 2 if r2.dtype == BF16 else 4 * S)

    def body(r_ref, o_ref):
        acc = r_ref[0].astype(F32)
        for s in range(1, S):
            acc = acc + r_ref[s].astype(F32)
        o_ref[...] = acc

    return _pcall(body, name=name, grid=(R // tr,), in_specs=[pl.BlockSpec((S, tr, Cc), lambda i: (0, i, 0))],
                  out_specs=_tile(tr, Cc), out_shape=jax.ShapeDtypeStruct((R, Cc), F32),
                  compiler_params=_cparams(("parallel",)))(r2)


def _sum_chips(name, recv, own, place):
    S, H, Cc = recv.shape
    tr = _row_tile(H, Cc, 4, 1024 * 1024)
    nb = H // tr

    def body(p_ref, r_ref, own_ref, o_ref):
        s = pl.program_id(1)
        me = p_ref[0]

        @pl.when(s == 0)
        def _():
            o_ref[...] = jnp.zeros_like(o_ref)

        @pl.when(s == me)
        def _():
            o_ref[...] += own_ref[0].astype(F32)

        @pl.when(s != me)
        def _():
            o_ref[...] += r_ref[0].astype(F32)

    grid_spec = pltpu.PrefetchScalarGridSpec(
        num_scalar_prefetch=1, grid=(nb, S),
        in_specs=[pl.BlockSpec((1, tr, Cc), lambda i, s, p: (jnp.where(s == p[0], (s + 1) % S, s), i, 0)),
                  pl.BlockSpec((1, tr, Cc), lambda i, s, p: (p[0], i, 0))],
        out_specs=pl.BlockSpec((tr, Cc), lambda i, s, p: (p[1] * nb + i, 0)))
    return _pcall(body, name=name, grid_spec=grid_spec, out_shape=jax.ShapeDtypeStruct((2 * H, Cc), F32),
                  compiler_params=_cparams(("parallel", "arbitrary")))(place, recv, own)


def _cast_bf16(name, w):
    R, Cc = w.shape
    tr = _row_tile(R, Cc, 4)

    def body(w_ref, o_ref):
        o_ref[...] = w_ref[...].astype(BF16)

    return _pcall(body, name=name, grid=(R // tr,), in_specs=[_tile(tr, Cc)], out_specs=_tile(tr, Cc),
                  out_shape=jax.ShapeDtypeStruct((R, Cc), BF16), compiler_params=_cparams(("parallel",)))(w)


_ANY = pl.BlockSpec(memory_space=pl.ANY)


def _place():
    x, y, c = lax.axis_index("x"), lax.axis_index("y"), lax.axis_index("c")
    others = [(1 - x, y), (x, 1 - y), (1 - x, 1 - y)]
    return x, y, c, others


def _gather_parts(shards):
    n = len(shards)
    halves = [s.shape[0] // 2 for s in shards]

    def parts(ins, outs, sems):
        x, y, c, _ = _place()
        me = 2 * x + y
        n1 = (x ^ (1 - c), y ^ c)
        n2 = (x ^ c, y ^ (1 - c))
        s1, s2, sd = 2 * n1[0] + n1[1], 2 * n2[0] + n2[1], 2 * (1 - x) + (1 - y)
        sib = (x, y, 1 - c)

        def rows(k, chip, hc):
            return outs[k].at[chip, pl.ds(hc * halves[k], halves[k]), :]

        def remote(k, j, src, dst, to):
            return pltpu.make_async_remote_copy(src_ref=src, dst_ref=dst, send_sem=sems[0].at[7 * k + j],
                                                recv_sem=sems[1].at[7 * k + j], device_id=to, device_id_type=MESH)

        def copy(k, j):
            if j == 6:
                return remote(k, j, ins[k], outs[k].at[me], sib)
            if j < 2:
                mine = ins[k].at[pl.ds(c * halves[k], halves[k]), :]
                return remote(k, j, mine, rows(k, me, c), (*(n1 if j == 0 else n2), c))
            land = rows(k, {2: s1, 3: s1, 4: s2, 5: sd}[j], c)
            return remote(k, j, land, land, (*n2, c) if j == 2 else sib)

        def arrived(k, j):
            hc = c if j < 3 else 1 - c
            land = outs[k].at[me] if j == 6 else rows(k, {0: s1, 1: s2, 2: sd, 3: s2, 4: s1, 5: sd}[j], hc)
            remote(k, j, land, land, (x, y, c)).wait_recv()

        return copy, arrived

    def start(ins, outs, sems):
        copy, _ = parts(ins, outs, sems)
        for k in range(n):
            copy(k, 0).start()
            copy(k, 1).start()
            copy(k, 6).start()

    def middle(ins, outs, sems):
        copy, arrived = parts(ins, outs, sems)
        for k in range(n):
            arrived(k, 0)
            copy(k, 2).start()
            copy(k, 3).start()
            arrived(k, 1)
            copy(k, 4).start()

    def finish(ins, outs, sems):
        copy, arrived = parts(ins, outs, sems)
        for k in range(n):
            arrived(k, 2)
            copy(k, 5).start()
        for k in range(n):
            for j in (3, 4, 5, 6):
                arrived(k, j)
        for k in range(n):
            for j in range(7):
                copy(k, j).wait_send()

    out_shapes = [jax.ShapeDtypeStruct((N_CHIPS,) + s.shape, s.dtype) for s in shards]
    scratch = [pltpu.SemaphoreType.DMA((7 * n,)), pltpu.SemaphoreType.DMA((7 * n,))]
    return list(shards), out_shapes, scratch, start, finish, middle


def _swap_halves(grads):
    n = len(grads)
    halves = [g.shape[1] // 2 for g in grads]

    def copies(ins, outs, sems):
        x, y, c, _ = _place()
        return [pltpu.make_async_remote_copy(
            src_ref=ins[k].at[:, pl.ds((1 - c) * halves[k], halves[k]), :], dst_ref=outs[k], send_sem=sems[0].at[k],
            recv_sem=sems[1].at[k], device_id=(x, y, 1 - c), device_id_type=MESH) for k in range(n)]

    def start(ins, outs, sems):
        for cp in copies(ins, outs, sems):
            cp.start()

    def finish(ins, outs, sems):
        for cp in copies(ins, outs, sems):
            cp.wait()

    out_shapes = [jax.ShapeDtypeStruct((g.shape[0], h) + g.shape[2:], g.dtype) for g, h in zip(grads, halves)]
    scratch = [pltpu.SemaphoreType.DMA((n,)), pltpu.SemaphoreType.DMA((n,))]
    return list(grads), out_shapes, scratch, start, finish


def _scatter_to_owners(chip_sums):
    n = len(chip_sums)

    def sends(ins, outs, sems):
        x, y, c, others = _place()
        me = 2 * x + y
        return [pltpu.make_async_remote_copy(
            src_ref=ins[k].at[2 * px + py], dst_ref=outs[k].at[me], send_sem=sems[0].at[3 * k + j],
            recv_sem=sems[1].at[3 * k + j], device_id=(px, py, c), device_id_type=MESH)
            for k in range(n) for j, (px, py) in enumerate(others)]

    def start(ins, outs, sems):
        for cp in sends(ins, outs, sems):
            cp.start()

    def finish(ins, outs, sems):
        x, y, c, others = _place()
        for k in range(n):
            for j, (px, py) in enumerate(others):
                land = outs[k].at[2 * px + py]
                pltpu.make_async_remote_copy(src_ref=land, dst_ref=land, send_sem=sems[0].at[3 * k + j],
                                             recv_sem=sems[1].at[3 * k + j], device_id=(x, y, c),
                                             device_id_type=MESH).wait_recv()
        for cp in sends(ins, outs, sems):
            cp.wait_send()

    out_shapes = [jax.ShapeDtypeStruct(g.shape, g.dtype) for g in chip_sums]
    scratch = [pltpu.SemaphoreType.DMA((3 * n,)), pltpu.SemaphoreType.DMA((3 * n,))]
    return list(chip_sums), out_shapes, scratch, start, finish


def _swap_with_sibling(arrays):
    n = len(arrays)

    def copies(ins, outs, sems):
        x, y, c, _ = _place()
        return [pltpu.make_async_remote_copy(src_ref=ins[k], dst_ref=outs[k], send_sem=sems[0].at[k],
                                             recv_sem=sems[1].at[k], device_id=(x, y, 1 - c), device_id_type=MESH)
                for k in range(n)]

    def start(ins, outs, sems):
        for cp in copies(ins, outs, sems):
            cp.start()

    def finish(ins, outs, sems):
        for cp in copies(ins, outs, sems):
            cp.wait()

    out_shapes = [jax.ShapeDtypeStruct(a.shape, a.dtype) for a in arrays]
    scratch = [pltpu.SemaphoreType.DMA((n,)), pltpu.SemaphoreType.DMA((n,))]
    return list(arrays), out_shapes, scratch, start, finish


def _add_pair(name, a, b):
    R, Cc = a.shape
    tr = _row_tile(R, Cc, 4)

    def body(a_ref, b_ref, o_ref):
        o_ref[...] = (a_ref[...].astype(F32) + b_ref[...].astype(F32)).astype(BF16)

    return _pcall(body, name=name, grid=(R // tr,), in_specs=[_tile(tr, Cc)] * 2, out_specs=_tile(tr, Cc),
                  out_shape=jax.ShapeDtypeStruct((R, Cc), BF16), compiler_params=_cparams(("parallel",)))(a, b)


def _second_neighbour():
    x, y, c, _ = _place()
    return (x, y, c), (x ^ c, y ^ (1 - c)), (x ^ (1 - c), y ^ c)


def _scatter_stage1(chip_sums):
    n = len(chip_sums)

    def copies(ins, outs, sems):
        (x, y, c), n2, n1 = _second_neighbour()
        diag = 2 * (1 - x) + (1 - y)
        return [pltpu.make_async_remote_copy(
            src_ref=ins[k].at[slot], dst_ref=outs[2 * k + j], send_sem=sems[0].at[2 * k + j],
            recv_sem=sems[1].at[2 * k + j], device_id=(*n2, c), device_id_type=MESH)
            for k in range(n) for j, slot in enumerate((2 * n2[0] + n2[1], diag))]

    def start(ins, outs, sems):
        for cp in copies(ins, outs, sems):
            cp.start()

    def finish(ins, outs, sems):
        for cp in copies(ins, outs, sems):
            cp.wait()

    out_shapes = [jax.ShapeDtypeStruct(g.shape[1:], g.dtype) for g in chip_sums for _ in range(2)]
    scratch = [pltpu.SemaphoreType.DMA((2 * n,)), pltpu.SemaphoreType.DMA((2 * n,))]
    return list(chip_sums), out_shapes, scratch, start, finish


def _scatter_stage2(passed):
    n = len(passed)

    def copies(ins, outs, sems):
        (x, y, c), n2, n1 = _second_neighbour()
        return [pltpu.make_async_remote_copy(src_ref=ins[k], dst_ref=outs[k], send_sem=sems[0].at[k],
                                             recv_sem=sems[1].at[k], device_id=(*n1, c), device_id_type=MESH)
                for k in range(n)]

    def start(ins, outs, sems):
        for cp in copies(ins, outs, sems):
            cp.start()

    def finish(ins, outs, sems):
        for cp in copies(ins, outs, sems):
            cp.wait()

    out_shapes = [jax.ShapeDtypeStruct(p.shape, p.dtype) for p in passed]
    scratch = [pltpu.SemaphoreType.DMA((n,)), pltpu.SemaphoreType.DMA((n,))]
    return list(passed), out_shapes, scratch, start, finish


def _add_passed(name, own, got, slot):
    _, H, Cc = own.shape
    tr = _row_tile(H, Cc, 4)

    def body(s_ref, o_ref, g_ref, out_ref):
        out_ref[...] = (o_ref[0].astype(F32) + g_ref[...].astype(F32)).astype(BF16)

    grid_spec = pltpu.PrefetchScalarGridSpec(
        num_scalar_prefetch=1, grid=(H // tr,),
        in_specs=[pl.BlockSpec((1, tr, Cc), lambda i, s: (s[0], i, 0)), pl.BlockSpec((tr, Cc), lambda i, s: (i, 0))],
        out_specs=pl.BlockSpec((tr, Cc), lambda i, s: (i, 0)))
    return _pcall(body, name=name, grid_spec=grid_spec, out_shape=jax.ShapeDtypeStruct((H, Cc), BF16),
                  compiler_params=_cparams(("parallel",)))(slot, own, got)


def _sum_stages(name, own, direct, via, place, transposed=False):
    _, H, Cc = own.shape
    tr = LANES if transposed else _row_tile(H, Cc, 4, 1024 * 1024)
    nb = H // tr

    def body(p_ref, own_ref, d_ref, v_ref, o_ref):
        acc = (own_ref[0].astype(F32) + d_ref[...].astype(F32)) + v_ref[...].astype(F32)
        o_ref[...] = acc.T if transposed else acc

    flat = pl.BlockSpec((tr, Cc), lambda i, p: (i, 0))
    out_spec = (pl.BlockSpec((Cc, tr), lambda i, p: (0, p[1] * nb + i)) if transposed
                else pl.BlockSpec((tr, Cc), lambda i, p: (p[1] * nb + i, 0)))
    grid_spec = pltpu.PrefetchScalarGridSpec(
        num_scalar_prefetch=1, grid=(nb,),
        in_specs=[pl.BlockSpec((1, tr, Cc), lambda i, p: (p[0], i, 0)), flat, flat], out_specs=out_spec)
    return _pcall(body, name=name, grid_spec=grid_spec,
                  out_shape=jax.ShapeDtypeStruct((Cc, 2 * H) if transposed else (2 * H, Cc), F32),
                  compiler_params=_cparams(("parallel",)))(place, own, direct, via)


def _join_halves(fulls, axes, small):
    n = len(fulls)
    hs = [f.shape[ax] // 2 for f, ax in zip(fulls, axes)]
    rel = [(dx, dy, dc) for dx in (0, 1) for dy in (0, 1) for dc in (0, 1)][1:]

    def half(ref, k, hc):
        part = pl.ds(hc * hs[k], hs[k])
        return ref.at[:, part] if axes[k] else ref.at[part, :]

    def body(*refs):
        ins, small_in = refs[:n], refs[n]
        outs, small_out = refs[n + 1:2 * n + 1], refs[2 * n + 1]
        send_sems, recv_sems, ssend, srecv, local_sem = refs[2 * n + 2:]
        x, y, c, _ = _place()
        dev = 4 * x + 2 * y + c
        local = pltpu.make_async_copy(small_in, small_out.at[dev], local_sem)
        local.start()
        cps = []
        for k in range(n):
            cp = pltpu.make_async_remote_copy(src_ref=half(ins[k], k, c), dst_ref=half(outs[k], k, c),
                                              send_sem=send_sems.at[k], recv_sem=recv_sems.at[k],
                                              device_id=(x, y, 1 - c), device_id_type=MESH)
            cp.start()
            cps.append(cp)
        for r, (dx, dy, dc) in enumerate(rel):
            cp = pltpu.make_async_remote_copy(src_ref=small_in, dst_ref=small_out.at[dev], send_sem=ssend.at[r],
                                              recv_sem=srecv.at[r], device_id=(x ^ dx, y ^ dy, c ^ dc),
                                              device_id_type=MESH)
            cp.start()
            cps.append(cp)
        for k in range(n):
            land = half(outs[k], k, 1 - c)
            pltpu.make_async_remote_copy(src_ref=land, dst_ref=land, send_sem=send_sems.at[k],
                                         recv_sem=recv_sems.at[k], device_id=(x, y, c), device_id_type=MESH).wait_recv()
        for r, (dx, dy, dc) in enumerate(rel):
            land = small_out.at[4 * (x ^ dx) + 2 * (y ^ dy) + (c ^ dc)]
            pltpu.make_async_remote_copy(src_ref=land, dst_ref=land, send_sem=ssend.at[r], recv_sem=srecv.at[r],
                                         device_id=(x, y, c), device_id_type=MESH).wait_recv()
        for cp in cps:
            cp.wait_send()
        local.wait()

    return _pcall(
        body, name="join_halves", in_specs=[_ANY] * (n + 1), out_specs=[_ANY] * (n + 1),
        out_shape=[jax.ShapeDtypeStruct(f.shape, f.dtype) for f in fulls]
        + [jax.ShapeDtypeStruct((N_DEV,) + small.shape, small.dtype)],
        input_output_aliases={k: k for k in range(n)},
        scratch_shapes=[pltpu.SemaphoreType.DMA((n,)), pltpu.SemaphoreType.DMA((n,)), pltpu.SemaphoreType.DMA((7,)),
                        pltpu.SemaphoreType.DMA((7,)), pltpu.SemaphoreType.DMA],
    )(*fulls, small)


def _local_step(cfg, x2, target, norm_gain, w_my, fb, mu_g, w0, a0, k_k, k_a, r_k, ln_w, ln_b, fng, rest,
                exchange=None, h=None):
    T, D, FW, FH, RW, RH, LP, lora = cfg.T, cfg.D, cfg.FW, cfg.FH, cfg.RW, cfg.RH, cfg.LP, cfg.lora
    fb_p = jnp.pad(fb, ((0, 0), (0, LANES - FH)))
    mu = _rwkv_vec_to_my(cfg, mu_g)
    rk = r_k.reshape(1, RW)
    tm = min(1024, T)

    if h is None:
        h = _rms_fwd(cfg, x2, norm_gain)
    if len(rest) == 2:
        u, *got = _mm("in_proj", h, w_my, "nn", F32, tm, cfg.tn, 2048, comm=rest[0])
        rest = rest[1](got)
    else:
        u = _mm("in_proj", h, w_my, "nn", F32, tm, cfg.tn, 2048)
    w2, a2, wpf, wpr, wout = rest
    w2p = jnp.pad(w2, ((0, LP - lora), (0, 0)))
    a2p = jnp.pad(a2, ((0, LP - lora), (0, 0)))
    c_cols = _fox_prep(cfg, u, fb_p)
    c_rows = c_cols[:, :FH].T.reshape(FH, 1, T)
    o, lse = _attn_fwd(cfg, u, c_rows)
    oa = _gate_a_fwd(cfg, o, u)
    prep = _rwkv_prep_fwd(cfg, u, mu, w0, w2p, a0, a2p, k_k, k_a)
    r, lw, kp, v, an, b, zb = prep
    toks = [r, lw, kp, v, an, b]
    q_s, yloc, a_m, sloc = _scan_local_fwd(cfg, toks)
    y, ckpt = _scan_carry_fwd(cfg, q_s, yloc, a_m, sloc)
    ob = _rwkv_post_fwd(cfg, y, r, kp, v, zb, ln_w, ln_b, rk)
    pa = _mm("proj_fox", oa, wpf, "nn", F32, tm, 1024, 2048)
    pb = _mm("proj_rwkv", ob, wpr, "nn", F32, tm, 1024, 2048)
    m = _merge_fwd(cfg, pa, pb, u)
    mo = _mm("out_proj", m, wout, "nn", F32, tm, 1024, 2048)
    loss8, dres, dres16, d_fng = _final(cfg, x2, mo, fng.reshape(1, D), target)

    dm = _mm("out_proj_dx", dres16, wout, "nt", F32, tm, 1024, 2048)
    d_wout = _mm("out_proj_dw", m, dres16, "tn", BF16, 1024, 1024, 2048)
    dpa, dpb, du = _merge_bwd(cfg, pa, pb, u, dm)
    doa = _mm("proj_fox_dx", dpa, wpf, "nt", F32, tm, 1024, 2048)
    d_wpf = _mm("proj_fox_dw", oa, dpa, "tn", BF16, 1024, 1024, 2048)
    dob = _mm("proj_rwkv_dx", dpb, wpr, "nt", F32, tm, 1024, 2048)
    d_wpr = _mm("proj_rwkv_dw", ob, dpb, "tn", BF16, 1024, 1024, 2048)

    do, du = _gate_a_bwd(cfg, o, u, doa, du)
    du, dcol = _attn_bwd(cfg, u, c_rows, lse, do, du)
    dc = jnp.pad(-dcol.reshape(FH, T).T, ((0, 0), (0, LANES - FH)))
    df, d_fb = _fox_prep_bwd(cfg, u, fb_p, dc)

    dy, dr_p, dk_p, dv_p, dzb, d_lnw, d_lnb, d_rk = _rwkv_post_bwd(cfg, y, r, kp, v, zb, ln_w, ln_b, rk, dob)
    early = dict(w_proj_fox=d_wpf, w_proj_rwkv=d_wpr, w_out=d_wout)
    res = _scan_carry_bwd(cfg, q_s, a_m, ckpt, dy, exchange(early) if exchange else None)
    dq_s, da_m, dsl = res[:3]
    res = _scan_local_bwd(cfg, toks, dq_s, dy, da_m, dsl, [dr_p, dk_p, dv_p],
                          exchange(("swapped", list(res[3:]))) if exchange else None)
    cots, received = res[:6], list(res[6:])
    dus, d_mu, d_w0, d_w2p, d_a0, d_a2p, d_kk, d_ka = _rwkv_prep_bwd(cfg, u, mu, w0, w2p, a0, a2p, k_k, k_a, cots, dzb)
    du = _shift_bwd(cfg, dus, mu, df, du)
    if exchange:
        late = dict(w_in=exchange((h, du, d_w2p[:lora], d_a2p[:lora])))
    else:
        late = dict(w_in=_mm("in_proj_dw", h, du, "tn", BF16, 1024, cfg.tn, 2048), rwkv_w2=d_w2p[:lora],
                    rwkv_a2=d_a2p[:lora])
    tkx = 2 * cfg.tn if cfg.ncol % (2 * cfg.tn) == 0 else cfg.tn
    res = _mm("in_proj_dx", du, w_my, "nt", F32, tm, 1024, tkx, comm=exchange(late) if exchange else None)
    dh = res[0] if exchange else res
    big = dict(early, **late)
    res = _rms_bwd(cfg, x2, norm_gain, dh, dres, exchange(list(res[1:])) if exchange else None)
    gx, d_ng = res[:2]
    received += list(res[2:])

    small = dict(norm_gain=d_ng, fox_forget_bias=d_fb[:, :FH], rwkv_shift_mix=_rwkv_vec_from_my(cfg, d_mu),
                 rwkv_w0=d_w0, rwkv_a0=d_a0, rwkv_k_k=d_kk, rwkv_k_a=d_ka, rwkv_r_k=d_rk, rwkv_ln_w=d_lnw,
                 rwkv_ln_b=d_lnb, final_norm_gain=d_fng)
    return loss8[0, 0], gx, small, big, received


_SMALL = ["norm_gain", "fox_forget_bias", "rwkv_shift_mix", "rwkv_w0", "rwkv_a0", "rwkv_k_k", "rwkv_k_a", "rwkv_r_k",
          "rwkv_ln_w", "rwkv_ln_b", "final_norm_gain"]
_WEIGHTS = ["norm_gain", "w_in", "fox_forget_bias", "rwkv_shift_mix", "rwkv_w0", "rwkv_w2", "rwkv_a0", "rwkv_a2",
            "rwkv_k_k", "rwkv_k_a", "rwkv_r_k", "rwkv_ln_w", "rwkv_ln_b", "w_proj_fox", "w_proj_rwkv", "w_out",
            "final_norm_gain"]


def _pack_small(arrs):
    parts, n = [], 0
    for a in arrs:
        f = a.reshape(-1)
        fill = (-f.shape[0]) % LANES
        parts += [f] + ([jnp.zeros((fill,), f.dtype)] if fill else [])
        n += f.shape[0] + fill
    tail = ((-(n // LANES)) % 8) * LANES
    return jnp.concatenate(parts + ([jnp.zeros((tail,), parts[0].dtype)] if tail else [])).reshape(-1, LANES)


def _unpack_small(packed, shapes):
    flat = packed.reshape(-1)
    out, pos = [], 0
    for s in shapes:
        n = int(np.prod(s))
        out.append(flat[pos:pos + n].reshape(s))
        pos += n + ((-n) % LANES)
    return out


def _shard_major(a, axis):
    parts = jnp.split(a, N_CHIPS, axis=axis)
    return jnp.stack(parts, axis=0)


def kernel(x, norm_gain, w_in, fox_forget_bias, rwkv_shift_mix, rwkv_w0, rwkv_w2, rwkv_a0, rwkv_a2, rwkv_k_k, rwkv_k_a, rwkv_r_k, rwkv_ln_w, rwkv_ln_b, w_proj_fox, w_proj_rwkv, w_out, final_norm_gain, loss_target, m_norm_gain, m_w_in, m_fox_forget_bias, m_rwkv_shift_mix, m_rwkv_w0, m_rwkv_w2, m_rwkv_a0, m_rwkv_a2, m_rwkv_k_k, m_rwkv_k_a, m_rwkv_r_k, m_rwkv_ln_w, m_rwkv_ln_b, m_w_proj_fox, m_w_proj_rwkv, m_w_out, m_final_norm_gain, v_norm_gain, v_w_in, v_fox_forget_bias, v_rwkv_shift_mix, v_rwkv_w0, v_rwkv_w2, v_rwkv_a0, v_rwkv_a2, v_rwkv_k_k, v_rwkv_k_a, v_rwkv_r_k, v_rwkv_ln_w, v_rwkv_ln_b, v_w_proj_fox, v_w_proj_rwkv, v_w_out, v_final_norm_gain):
    args = dict(locals())
    T, D = x.shape[1], x.shape[2]
    lora = rwkv_w2.shape[1]
    cfg = _Cfg(T, D, lora)
    RW = cfg.RW
    c_idx = lax.axis_index("c").astype(jnp.int32).reshape(1)
    me_chip = (2 * lax.axis_index("x") + lax.axis_index("y")).astype(jnp.int32)
    place = jnp.concatenate([me_chip.reshape(1), c_idx])

    w_in_s = w_in[0].astype(BF16)
    lora_s = jnp.concatenate([rwkv_w2[0], rwkv_a2[0]], axis=0)
    h, g_in = _rms_fwd(cfg, x[0], norm_gain, _gather_parts([w_in_s]))
    w_my = _shards_to_my_layout(cfg, g_in)
    mine = [_cast_bf16("cast_w_proj_fox", w_proj_fox[0]), _cast_bf16("cast_w_proj_rwkv", w_proj_rwkv[0]),
            _cast_bf16("cast_w_out", w_out[0]), lora_s]

    def unpack(gathered):
        g_wpf, g_wpr, g_out, g_lora = gathered
        lo = g_lora.transpose(1, 0, 2).reshape(2 * lora, RW)
        return (lo[:lora], lo[lora:], g_wpf.transpose(1, 0, 2).reshape(RW, D),
                g_wpr.transpose(1, 0, 2).reshape(RW, D), g_out.reshape(D, D))

    early, late = ["w_proj_fox", "w_proj_rwkv", "w_out"], ["w_in", "lora"]
    names = early + late
    chip_sums, direct, shard_major = {}, {}, []
    n1_slot = (2 * (lax.axis_index("x") ^ (1 - lax.axis_index("c")))
               + (lax.axis_index("y") ^ lax.axis_index("c"))).astype(jnp.int32).reshape(1)

    def exchange(got):
        if isinstance(got, tuple) and len(got) == 4:
            h, du, d_w2, d_a2 = got
            c, half = lax.axis_index("c"), D // 2
            cols = lambda base: lax.dynamic_slice_in_dim(h, base * half, half, axis=1)
            lora_g = _shard_major(jnp.concatenate([d_w2, d_a2], axis=0).astype(BF16), 1)
            lora_rows = lambda base: lax.dynamic_slice_in_dim(lora_g, base * lora, lora, axis=1).reshape(-1, RW // 4)
            tiles = (BF16, min(1024, half), cfg.tn, 2048)
            sent = _mm("in_proj_dw_sibling", cols(1 - c), du, "tn", *tiles)
            kept, got_w, got_l = _mm("in_proj_dw", cols(c), du, "tn", *tiles,
                                     comm=_swap_with_sibling([sent, lora_rows(1 - c)]))
            return (_add_pair("add_halves_w_in", kept, got_w),
                    _add_pair("add_halves_lora", lora_rows(c), got_l).reshape(N_CHIPS, lora, RW // 4))
        if isinstance(got, dict):
            if "w_in" in got:
                sums = [_my_layout_to_shards(cfg, got["w_in"][0]), got["w_in"][1]]
                chip_sums.update(zip(late, sums))
                return _scatter_stage1(sums)
            shard_major.extend([_shard_major(got["w_proj_fox"], 1), _shard_major(got["w_proj_rwkv"], 1),
                                _shard_major(got["w_out"], 0)])
            return _swap_halves(shard_major)
        if got[0] == "swapped":
            sums = [_add_halves("add_halves_" + nm, g, r, c_idx) for nm, g, r in zip(early, shard_major, got[1])]
            chip_sums.update(zip(early, sums))
            return _scatter_to_owners(sums)
        direct.update(zip(late, got[0::2]))
        return _scatter_stage2([_add_passed("add_passed_" + nm, chip_sums[nm], g, n1_slot)
                                for nm, g in zip(late, got[1::2])])

    loss_dev, gx, small, _, recv2 = _local_step(
        cfg, x[0], loss_target[0], norm_gain, w_my, fox_forget_bias, rwkv_shift_mix, rwkv_w0, rwkv_a0, rwkv_k_k,
        rwkv_k_a, rwkv_r_k, rwkv_ln_w, rwkv_ln_b, final_norm_gain, (_gather_parts(mine), unpack), exchange, h)
    loss = lax.psum(loss_dev, ("x", "y", "c"))

    small_shapes = [args[nm].shape for nm in _SMALL]
    packed = _pack_small([small[nm] for nm in _SMALL])
    reduced = [_sum_chips("sum_chips_" + nm, r, chip_sums[nm], place) for nm, r in zip(early, recv2[:3])]
    reduced += [_sum_stages("sum_stages_" + nm, chip_sums[nm], direct[nm], via, place, transposed=nm == "w_in")
                for nm, via in zip(late, recv2[3:])]
    *joined, small_all = _join_halves(reduced, [int(nm == "w_in") for nm in names], packed)
    g_small = _sum_slots("sum_small", small_all)

    grads = dict(zip(_SMALL, _unpack_small(g_small, small_shapes)))
    grads.update({nm: g[None] for nm, g in zip(names, joined) if nm not in ("lora", "w_in")})
    g_lora_f = joined[names.index("lora")]
    grads["rwkv_w2"] = g_lora_f[None, :lora]
    grads["rwkv_a2"] = g_lora_f[None, lora:]

    delta, new_m, new_v = {}, {}, {}
    w_small = _pack_small([args[nm] for nm in _SMALL])
    m_small = _pack_small([args["m_" + nm] for nm in _SMALL])
    v_small = _pack_small([args["v_" + nm] for nm in _SMALL])
    d_s, m_s, v_s = _adamw("adamw_small", w_small, g_small, m_small, v_small)
    for tgt, pk in ((delta, d_s), (new_m, m_s), (new_v, v_s)):
        tgt.update(zip(_SMALL, _unpack_small(pk, small_shapes)))
    t_out = _adamw("adamw_w_in", w_in[0].T, joined[names.index("w_in")], m_w_in[0].T, v_w_in[0].T, copy_grad=True)
    delta["w_in"], new_m["w_in"], new_v["w_in"], grads["w_in"] = [t.T[None] for t in t_out]
    for nm in ("w_proj_fox", "w_proj_rwkv", "w_out", "rwkv_w2", "rwkv_a2"):
        shp = args[nm].shape
        two_d = (shp[1], shp[2])
        d_b, m_b, v_b = _adamw("adamw_" + nm, args[nm].reshape(two_d), grads[nm].reshape(two_d),
                               args["m_" + nm].reshape(two_d), args["v_" + nm].reshape(two_d))
        delta[nm], new_m[nm], new_v[nm] = d_b.reshape(shp), m_b.reshape(shp), v_b.reshape(shp)

    return (loss, gx[None], *[grads[n] for n in _WEIGHTS], *[delta[n] for n in _WEIGHTS],
            *[new_m[n] for n in _WEIGHTS], *[new_v[n] for n in _WEIGHTS])
```

```python
import functools

import numpy as np
import jax
import jax.numpy as jnp
from jax import lax
from jax.experimental import pallas as pl
from jax.experimental.pallas import tpu as pltpu

F32 = jnp.float32
BF16 = jnp.bfloat16
HI = lax.Precision.HIGHEST
MESH = pl.DeviceIdType.MESH

FOX_HEAD_DIM = 128
RWKV_HEAD_DIM = 64
RMS_EPS = 1e-6
GN_EPS = 64e-5
L2_EPS = 1e-12
ADAM_LR = 0.001
ADAM_B1 = 0.9
ADAM_B2 = 0.999
ADAM_EPS = 1e-08
ADAM_WD = 0.01
ADAM_STEP = 10

LANES = 128
VMEM_LIMIT = 56 * 1024 * 1024
SCAN_CHUNK = 64
SCAN_HEADS_PER_STEP = 16
SCAN_CHUNKS_PER_STEP = 2
SCAN_PASSES = ((3, 1), 1, 1)
N_CHIPS = 4
N_DEV = 8

_pcall = pl.pallas_call


def _cparams(sem=None):
    return pltpu.CompilerParams(dimension_semantics=sem, vmem_limit_bytes=VMEM_LIMIT)


def _softplus(x):
    return jnp.maximum(x, 0.0) + jnp.log(1.0 + jnp.exp(-jnp.abs(x)))


def _silu(z):
    return z * jax.nn.sigmoid(z)


def _rmsn(x, g):
    return x * lax.rsqrt(jnp.mean(x * x, axis=-1, keepdims=True) + RMS_EPS) * g


def _dot(a, b, dims="nn", precision=None):
    dn = {"nn": (((1,), (0,)), ((), ())), "nt": (((1,), (1,)), ((), ())), "tn": (((0,), (0,)), ((), ()))}[dims]
    return lax.dot_general(a, b, dn, precision=precision, preferred_element_type=F32)


def _split_bf16(x):
    hi = x.astype(BF16)
    return hi, (x - hi.astype(F32)).astype(BF16)


def _bdot_raw(a, b, ca, cb, passes):
    dn = (((ca,), (cb,)), ((0,), (0,)))
    mm = lambda p, q: lax.dot_general(p, q, dn, preferred_element_type=F32)
    passes = passes[0] if isinstance(passes, tuple) else passes
    if passes == 1:
        return mm(a.astype(BF16), b.astype(BF16))
    ah, al = _split_bf16(a)
    bh, bl = _split_bf16(b)
    return mm(ah, bh) + (mm(ah, bl) + mm(al, bh))


@functools.partial(jax.custom_vjp, nondiff_argnums=(2, 3, 4))
def _bdot_p(a, b, ca, cb, passes):
    return _bdot_raw(a, b, ca, cb, passes)


def _bdot_fwd(a, b, ca, cb, passes):
    return _bdot_raw(a, b, ca, cb, passes), (a, b)


def _bdot_bwd(ca, cb, passes, res, g):
    a, b = res
    passes = passes[1] if isinstance(passes, tuple) else passes
    if (ca, cb) == (2, 1):
        return _bdot_p(g, b, 2, 2, passes), _bdot_p(a, g, 1, 1, passes)
    if (ca, cb) == (2, 2):
        return _bdot_p(g, b, 2, 1, passes), _bdot_p(g, a, 1, 1, passes)
    assert (ca, cb) == (1, 1)
    return _bdot_p(b, g, 2, 2, passes), _bdot_p(a, g, 2, 1, passes)


_bdot_p.defvjp(_bdot_fwd, _bdot_bwd)


def _bdot(a, b, ca, cb, passes=3):
    return _bdot_p(a, b, ca, cb, passes)


def _dot3(a, b):
    return _bdot(a[None], b[None], 2, 1)[0]


@jax.custom_vjp
def _xdot(x, m, mt):
    hi, lo = _split_bf16(x)
    m16 = m.astype(BF16)
    return _dot(hi, m16) + _dot(lo, m16)


def _xdot_fwd(x, m, mt):
    return _xdot(x, m, mt), (m, mt)


def _xdot_bwd(res, g):
    m, mt = res
    return _xdot(g, mt, m), jnp.zeros_like(m), jnp.zeros_like(mt)


_xdot.defvjp(_xdot_fwd, _xdot_bwd)


class _Cfg:
    def __init__(self, T, D, lora):
        self.T, self.D, self.lora = T, D, lora
        self.FW = D // 2
        self.FH = self.FW // FOX_HEAD_DIM
        self.RW = D // 2
        self.RH = self.RW // RWKV_HEAD_DIM
        self.LP = -(-lora // LANES) * LANES
        self.o_fox = 0
        self.o_rwkv = 4 * self.FW
        self.o_gate = self.o_rwkv + 4 * self.RW
        self.o_f = self.o_gate + 2 * D
        self.o_wd = self.o_f + LANES
        self.o_ad = self.o_wd + self.LP
        end = self.o_ad + self.LP
        self.tn = 1280 if D >= 2048 else LANES
        self.ncol = -(-end // self.tn) * self.tn
        self.in_cols = 4 * self.FW + self.FH + 4 * self.RW + 2 * lora + 2 * D
        self.scp = -(-(self.in_cols // N_CHIPS) // LANES) * LANES
        self.rseg = 4 * self.RW + 2 * self.LP
        self.C = min(SCAN_CHUNK, T)
        self.tr = min(256, T)
        self.hb = min(SCAN_HEADS_PER_STEP, self.RH)
        self.cb = SCAN_CHUNKS_PER_STEP if (T // self.C) % SCAN_CHUNKS_PER_STEP == 0 else 1

    def segments(self):
        FW, FH, RW, lo, D = self.FW, self.FH, self.RW, self.lora, self.D
        g_f = 4 * FW
        g_r = g_f + FH
        g_wd = g_r + 4 * RW
        g_ad = g_wd + lo
        g_g = g_ad + lo
        dh = FOX_HEAD_DIM
        qkv = [(j * FW + h * dh, dh, (3 * h + j) * dh) for h in range(FH) for j in range(3)]
        return qkv + [(3 * FW, FW, 3 * FW), (g_f, FH, self.o_f), (g_r, 4 * RW, self.o_rwkv), (g_wd, lo, self.o_wd),
                      (g_ad, lo, self.o_ad), (g_g, 2 * D, self.o_gate)]


def _shards_to_my_layout(cfg, g):
    R, sc = g.shape[1], g.shape[2]
    segs = sorted(cfg.segments(), key=lambda s: s[2])
    parts, pos = [], 0
    for g0, w, m0 in segs:
        if m0 > pos:
            parts.append(jnp.zeros((R, m0 - pos), g.dtype))
        for s in range(N_CHIPS):
            lo, hi = max(g0, s * sc), min(g0 + w, (s + 1) * sc)
            if lo < hi:
                parts.append(g[s, :, lo - s * sc:hi - s * sc])
        pos = m0 + w
    if cfg.ncol > pos:
        parts.append(jnp.zeros((R, cfg.ncol - pos), g.dtype))
    return jnp.concatenate(parts, axis=1)


def _my_layout_to_shards(cfg, wm):
    sc, R = cfg.in_cols // N_CHIPS, wm.shape[0]
    segs = sorted(cfg.segments(), key=lambda s: s[0])
    shards = []
    for s in range(N_CHIPS):
        parts = []
        for g0, w, m0 in segs:
            lo, hi = max(g0, s * sc), min(g0 + w, (s + 1) * sc)
            if lo < hi:
                parts.append(wm[:, m0 + lo - g0:m0 + hi - g0])
        parts.append(jnp.zeros((R, cfg.scp - sc), wm.dtype))
        shards.append(jnp.concatenate(parts, axis=1))
    return jnp.stack(shards, axis=0)


def _rwkv_vec_to_my(cfg, v):
    RW4, lo, LP = 4 * cfg.RW, cfg.lora, cfg.LP
    z = jnp.zeros((1, LP - lo), v.dtype)
    return jnp.concatenate([v[:, :RW4], v[:, RW4:RW4 + lo], z, v[:, RW4 + lo:], z], axis=1)


def _rwkv_vec_from_my(cfg, v):
    RW4, lo, LP = 4 * cfg.RW, cfg.lora, cfg.LP
    return jnp.concatenate([v[:, :RW4], v[:, RW4:RW4 + lo], v[:, RW4 + LP:RW4 + LP + lo]], axis=1)


def _comm_at(comm, which, steps, cin, cout, scr):
    if not comm or len(comm) <= which:
        return
    lin, total = 0, 1
    for d, n in enumerate(steps):
        lin = lin * n + pl.program_id(d)
        total *= n
    pl.when(lin == {3: 0, 4: total - 1, 5: total // 2}[which])(lambda: comm[which](cin, cout, scr))


def _hosted(body, n_in, n_out, steps, comm):
    if not comm:
        return body, [], [], [], []
    ci, co, cs = len(comm[0]), len(comm[1]), len(comm[2])

    def wrapped(*refs):
        ins, cin = refs[:n_in], refs[n_in:n_in + ci]
        outs, cout = refs[n_in + ci:n_in + ci + n_out], refs[n_in + ci + n_out:n_in + ci + n_out + co]
        cscr, scr = refs[n_in + ci + n_out + co:n_in + ci + n_out + co + cs], refs[n_in + ci + n_out + co + cs:]
        _comm_at(comm, 3, steps, cin, cout, cscr)
        body(*ins, *outs, *scr)
        _comm_at(comm, 5, steps, cin, cout, cscr)
        _comm_at(comm, 4, steps, cin, cout, cscr)

    return wrapped, [_ANY] * ci, [_ANY] * co, list(comm[1]), list(comm[2])


def _mm(name, a, b, dims, out_dtype, tm, tn, tk, comm=None):
    (M, K) = a.shape if dims != "tn" else a.shape[::-1]
    N = b.shape[0] if dims == "nt" else b.shape[1]
    tm, tn, tk = min(tm, M), min(tn, N), min(tk, K)
    assert M % tm == 0 and N % tn == 0 and K % tk == 0, (name, M, N, K, tm, tn, tk)
    nk = K // tk
    steps = (M // tm, N // tn, nk)
    c_in, c_out, c_scr = comm[:3] if comm else ([], [], [])
    if dims == "nn":
        a_spec = pl.BlockSpec((tm, tk), lambda i, j, k: (i, k))
        b_spec = pl.BlockSpec((tk, tn), lambda i, j, k: (k, j))
    elif dims == "nt":
        a_spec = pl.BlockSpec((tm, tk), lambda i, j, k: (i, k))
        b_spec = pl.BlockSpec((tn, tk), lambda i, j, k: (j, k))
    else:
        a_spec = pl.BlockSpec((tk, tm), lambda i, j, k: (k, i))
        b_spec = pl.BlockSpec((tk, tn), lambda i, j, k: (k, j))

    n_acc = 1 if nk > 1 else 0

    def body(a_ref, b_ref, *rest):
        cin, o_ref = rest[:len(c_in)], rest[len(c_in)]
        cout = rest[len(c_in) + 1:len(c_in) + 1 + len(c_out)]
        scr = rest[len(c_in) + 1 + len(c_out):]
        _comm_at(comm, 3, steps, cin, cout, scr[n_acc:])
        if nk == 1:
            o_ref[...] = _dot(a_ref[...], b_ref[...], dims).astype(o_ref.dtype)
        else:
            acc_ref, k = scr[0], pl.program_id(2)

            @pl.when(k == 0)
            def _():
                acc_ref[...] = jnp.zeros_like(acc_ref)

            acc_ref[...] += _dot(a_ref[...], b_ref[...], dims)

            @pl.when(k == nk - 1)
            def _():
                o_ref[...] = acc_ref[...].astype(o_ref.dtype)

        _comm_at(comm, 5, steps, cin, cout, scr[n_acc:])
        _comm_at(comm, 4, steps, cin, cout, scr[n_acc:])

    res = _pcall(
        body, name=name, grid=steps,
        in_specs=[a_spec, b_spec] + [_ANY] * len(c_in),
        out_specs=[pl.BlockSpec((tm, tn), lambda i, j, k: (i, j))] + [_ANY] * len(c_out),
        out_shape=[jax.ShapeDtypeStruct((M, N), out_dtype)] + list(c_out),
        scratch_shapes=([pltpu.VMEM((tm, tn), F32)] if nk > 1 else []) + list(c_scr),
        compiler_params=_cparams(("arbitrary",) * 3 if comm else ("parallel", "parallel", "arbitrary")),
    )(a, b, *c_in)
    return res if comm else res[0]


def _tile(tr, w, cb=0):
    return pl.BlockSpec((tr, w), lambda i: (i, cb))


def _const(shape):
    nd = len(shape)
    return pl.BlockSpec(shape, lambda i: (0,) * nd)


def _acc_store(i, ref, val):
    @pl.when(i == 0)
    def _():
        ref[...] = val

    @pl.when(i > 0)
    def _():
        ref[...] += val


def _rms_fwd(cfg, x2, g, comm=None):
    T, D, tr = cfg.T, cfg.D, cfg.tr
    steps = (T // tr,)

    def body(x_ref, g_ref, h_ref):
        h_ref[...] = _rmsn(x_ref[...], g_ref[...]).astype(BF16)

    body, c_in, c_out, c_shapes, c_scr = _hosted(body, 2, 1, steps, comm)
    res = _pcall(body, name="rms_fwd", grid=steps, in_specs=[_tile(tr, D), _const((1, D))] + c_in,
                 out_specs=[_tile(tr, D)] + c_out, out_shape=[jax.ShapeDtypeStruct((T, D), BF16)] + c_shapes,
                 scratch_shapes=c_scr, compiler_params=_cparams(("arbitrary",) if comm else ("parallel",)),
                 )(x2, g, *(comm[0] if comm else []))
    return res if comm else res[0]


def _rms_bwd(cfg, x2, g, dh, dres, comm=None):
    T, D, tr = cfg.T, cfg.D, cfg.tr
    c_in, c_out, c_scr = comm[:3] if comm else ([], [], [])
    steps = (T // tr,)

    def body(x_ref, g_ref, dh_ref, dres_ref, *rest):
        cin, (gx_ref, dg_ref) = rest[:len(c_in)], rest[len(c_in):len(c_in) + 2]
        cout, scr = rest[len(c_in) + 2:len(c_in) + 2 + len(c_out)], rest[len(c_in) + 2 + len(c_out):]
        _comm_at(comm, 3, steps, cin, cout, scr)
        _, vjp = jax.vjp(_rmsn, x_ref[...], g_ref[...])
        dx, dg = vjp(dh_ref[...])
        gx_ref[...] = dx + dres_ref[...]
        _acc_store(pl.program_id(0), dg_ref, dg)
        _comm_at(comm, 4, steps, cin, cout, scr)

    return _pcall(body, name="rms_bwd", grid=steps,
                  in_specs=[_tile(tr, D), _const((1, D)), _tile(tr, D), _tile(tr, D)] + [_ANY] * len(c_in),
                  out_specs=[_tile(tr, D), _const((1, D))] + [_ANY] * len(c_out),
                  out_shape=[jax.ShapeDtypeStruct((T, D), F32), jax.ShapeDtypeStruct((1, D), F32)] + list(c_out),
                  scratch_shapes=list(c_scr), compiler_params=_cparams(("arbitrary",)))(x2, g, dh, dres, *c_in)


def _final(cfg, x2, mo, fg, target):
    T, D, tr = cfg.T, cfg.D, cfg.tr

    def loss_fn(hres, g, tgt):
        err = _rmsn(hres, g) - tgt
        return 0.5 * jnp.sum(jnp.mean(err * err, axis=-1, keepdims=True), axis=0, keepdims=True)

    def body(x_ref, mo_ref, g_ref, t_ref, loss_ref, dres_ref, dres16_ref, dg_ref):
        hres = x_ref[...] + mo_ref[...]
        loss, vjp = jax.vjp(functools.partial(loss_fn, tgt=t_ref[...]), hres, g_ref[...])
        dres, dg = vjp(jnp.ones((1, 1), F32))
        dres_ref[...] = dres
        dres16_ref[...] = dres.astype(BF16)
        i = pl.program_id(0)
        _acc_store(i, dg_ref, dg)
        _acc_store(i, loss_ref, jnp.broadcast_to(loss, (8, LANES)))

    return _pcall(body, name="final_loss", grid=(T // tr,),
                  in_specs=[_tile(tr, D), _tile(tr, D), _const((1, D)), _tile(tr, D)],
                  out_specs=[_const((8, LANES)), _tile(tr, D), _tile(tr, D), _const((1, D))],
                  out_shape=[jax.ShapeDtypeStruct((8, LANES), F32), jax.ShapeDtypeStruct((T, D), F32),
                             jax.ShapeDtypeStruct((T, D), BF16), jax.ShapeDtypeStruct((1, D), F32)],
                  compiler_params=_cparams(("arbitrary",)))(x2, mo, fg, target)


def _merge_fn(pa, pb, ga, gb):
    return jax.nn.sigmoid(ga) * pa + jax.nn.sigmoid(gb) * pb


def _merge_fwd(cfg, pa, pb, u):
    T, D, tr = cfg.T, cfg.D, cfg.tr
    cga, cgb = cfg.o_gate // D, cfg.o_gate // D + 1

    def body(pa_ref, pb_ref, ga_ref, gb_ref, m_ref):
        m_ref[...] = _merge_fn(pa_ref[...], pb_ref[...], ga_ref[...], gb_ref[...]).astype(BF16)

    return _pcall(body, name="merge_fwd", grid=(T // tr,),
                  in_specs=[_tile(tr, D), _tile(tr, D), _tile(tr, D, cga), _tile(tr, D, cgb)],
                  out_specs=_tile(tr, D), out_shape=jax.ShapeDtypeStruct((T, D), BF16),
                  compiler_params=_cparams(("parallel",)))(pa, pb, u, u)


def _merge_bwd(cfg, pa, pb, u, dm):
    T, D, tr = cfg.T, cfg.D, cfg.tr
    cga, cgb = cfg.o_gate // D, cfg.o_gate // D + 1

    def body(pa_ref, pb_ref, ga_ref, gb_ref, dm_ref, dpa_ref, dpb_ref, dg_ref):
        _, vjp = jax.vjp(_merge_fn, pa_ref[...], pb_ref[...], ga_ref[...], gb_ref[...])
        dpa, dpb, dga, dgb = vjp(dm_ref[...])
        dpa_ref[...] = dpa.astype(BF16)
        dpb_ref[...] = dpb.astype(BF16)
        dg_ref[:, :D] = dga.astype(BF16)
        dg_ref[:, D:] = dgb.astype(BF16)

    return _pcall(body, name="merge_bwd", grid=(T // tr,),
                  in_specs=[_tile(tr, D), _tile(tr, D), _tile(tr, D, cga), _tile(tr, D, cgb), _tile(tr, D)],
                  out_specs=[_tile(tr, D), _tile(tr, D), _tile(tr, 2 * D, cfg.o_gate // (2 * D))],
                  out_shape=[jax.ShapeDtypeStruct((T, D), BF16), jax.ShapeDtypeStruct((T, D), BF16),
                             jax.ShapeDtypeStruct((T, cfg.ncol), BF16)],
                  compiler_params=_cparams(("parallel",)))(pa, pb, u, u, dm)


def _gate_fn(o, z):
    return o * _silu(z)


def _gate_a_fwd(cfg, o, u):
    T, FW, tr = cfg.T, cfg.FW, cfg.tr

    def body(o_ref, z_ref, oa_ref):
        oa_ref[...] = _gate_fn(o_ref[...], z_ref[...]).astype(BF16)

    return _pcall(body, name="gate_a_fwd", grid=(T // tr,), in_specs=[_tile(tr, FW), _tile(tr, FW, 3)],
                  out_specs=_tile(tr, FW), out_shape=jax.ShapeDtypeStruct((T, FW), BF16),
                  compiler_params=_cparams(("parallel",)))(o, u)


def _gate_a_bwd(cfg, o, u, doa, du):
    T, FW, tr = cfg.T, cfg.FW, cfg.tr

    def body(o_ref, z_ref, doa_ref, du_in, do_ref, dz_ref):
        _, vjp = jax.vjp(_gate_fn, o_ref[...], z_ref[...])
        do, dz = vjp(doa_ref[...])
        do_ref[...] = do
        dz_ref[...] = dz.astype(BF16)

    return _pcall(body, name="gate_a_bwd", grid=(T // tr,),
                  in_specs=[_tile(tr, FW), _tile(tr, FW, 3), _tile(tr, FW), _ANY],
                  out_specs=[_tile(tr, FW), _tile(tr, FW, 3)],
                  out_shape=[jax.ShapeDtypeStruct((T, FW), F32), jax.ShapeDtypeStruct(du.shape, BF16)],
                  input_output_aliases={3: 1},
                  compiler_params=_cparams(("parallel",)))(o, u, doa, du)


def _fox_prep(cfg, u, fb):
    T, tr = cfg.T, cfg.tr
    cf = cfg.o_f // LANES

    def body(f_ref, fb_ref, c_ref, carry_ref):
        i = pl.program_id(0)

        @pl.when(i == 0)
        def _():
            carry_ref[...] = jnp.zeros_like(carry_ref)

        lf = -_softplus(-(f_ref[...] + fb_ref[...]))
        r = lax.broadcasted_iota(jnp.int32, (tr, tr), 0)
        c = lax.broadcasted_iota(jnp.int32, (tr, tr), 1)
        tri = (r >= c).astype(F32)
        c_ref[...] = _dot(tri, lf, precision=HI) + carry_ref[...]
        carry_ref[...] += jnp.sum(lf, axis=0, keepdims=True)

    return _pcall(body, name="fox_prep", grid=(T // tr,), in_specs=[_tile(tr, LANES, cf), _const((1, LANES))],
                  out_specs=_tile(tr, LANES), out_shape=jax.ShapeDtypeStruct((T, LANES), F32),
                  scratch_shapes=[pltpu.VMEM((1, LANES), F32)], compiler_params=_cparams(("arbitrary",)))(u, fb)


def _fox_prep_bwd(cfg, u, fb, dc):
    T, tr = cfg.T, cfg.tr
    cf = cfg.o_f // LANES
    nb = T // tr

    def body(f_ref, fb_ref, dc_ref, df_ref, dfb_ref, carry_ref):
        i = pl.program_id(0)

        @pl.when(i == 0)
        def _():
            carry_ref[...] = jnp.zeros_like(carry_ref)

        dc = dc_ref[...]
        r = lax.broadcasted_iota(jnp.int32, (tr, tr), 0)
        c = lax.broadcasted_iota(jnp.int32, (tr, tr), 1)
        triu = (r <= c).astype(F32)
        dlf = _dot(triu, dc, precision=HI) + carry_ref[...]
        carry_ref[...] += jnp.sum(dc, axis=0, keepdims=True)
        dz = dlf * jax.nn.sigmoid(-(f_ref[...] + fb_ref[...]))
        df_ref[...] = dz.astype(BF16)
        _acc_store(i, dfb_ref, jnp.sum(dz, axis=0, keepdims=True))

    rev = lambda i: (nb - 1 - i, 0)
    return _pcall(body, name="fox_prep_bwd", grid=(nb,),
                  in_specs=[pl.BlockSpec((tr, LANES), lambda i: (nb - 1 - i, cf)), _const((1, LANES)),
                            pl.BlockSpec((tr, LANES), rev)],
                  out_specs=[pl.BlockSpec((tr, LANES), rev), _const((1, LANES))],
                  out_shape=[jax.ShapeDtypeStruct((T, LANES), BF16), jax.ShapeDtypeStruct((1, LANES), F32)],
                  scratch_shapes=[pltpu.VMEM((1, LANES), F32)], compiler_params=_cparams(("arbitrary",)))(u, fb, dc)


def _attn_logits(q_ref, k_ref, c_ref, tq, te):
    q = q_ref[...].astype(BF16)
    scale = FOX_HEAD_DIM ** -0.5
    part = lambda k0, k1: _dot(q, k_ref[k0:k1, :].astype(BF16), "nt") * scale - c_ref[0, :, k0:k1]
    row = lax.broadcasted_iota(jnp.int32, (tq, tq), 0)
    col = lax.broadcasted_iota(jnp.int32, (tq, tq), 1)
    own = ((te - tq, te), jnp.where(col <= row, part(te - tq, te), -1e30))
    return [((0, te - tq), part(0, te - tq)), own] if te > tq else [own]


def _per_query_tile(i, nq, tq, fn):
    for ii in range(nq):
        pl.when(i == ii)(functools.partial(fn, (ii + 1) * tq))


def _attn_fwd(cfg, u, c_rows):
    T, FW, FH = cfg.T, cfg.FW, cfg.FH
    tq = min(256, T)
    dh = FOX_HEAD_DIM

    def body(q_ref, k_ref, v_ref, c_ref, o_ref, lse_ref):
        i = pl.program_id(1)

        def tile(te):
            parts = _attn_logits(q_ref, k_ref, c_ref, tq, te)
            m = functools.reduce(jnp.maximum, [jnp.max(s, axis=1, keepdims=True) for _, s in parts])
            l, acc = 0.0, 0.0
            for (k0, k1), s in parts:
                p = jnp.exp(s - m)
                l = l + jnp.sum(p, axis=1, keepdims=True)
                acc = acc + _dot(p.astype(BF16), v_ref[k0:k1, :].astype(BF16))
            o_ref[...] = acc / l
            lse_ref[0] = m + jnp.log(l)

        _per_query_tile(i, T // tq, tq, tile)

    return _pcall(
        body, name="fox_attn_fwd", grid=(FH, T // tq),
        in_specs=[pl.BlockSpec((tq, dh), lambda h, i: (i, 3 * h)), pl.BlockSpec((T, dh), lambda h, i: (0, 3 * h + 1)),
                  pl.BlockSpec((T, dh), lambda h, i: (0, 3 * h + 2)), pl.BlockSpec((1, 1, T), lambda h, i: (h, 0, 0))],
        out_specs=[pl.BlockSpec((tq, dh), lambda h, i: (i, h)), pl.BlockSpec((1, tq, 1), lambda h, i: (h, i, 0))],
        out_shape=[jax.ShapeDtypeStruct((T, FW), F32), jax.ShapeDtypeStruct((FH, T, 1), F32)],
        compiler_params=_cparams(("parallel", "arbitrary")),
    )(u, u, u, c_rows)


def _attn_bwd(cfg, u, c_rows, lse, do, du):
    T, FW, FH = cfg.T, cfg.FW, cfg.FH
    tq = min(256, T)
    nq = T // tq
    dh = FOX_HEAD_DIM
    scale = dh ** -0.5

    def body(q_ref, k_ref, v_ref, c_ref, lse_ref, do_ref, du_in, du_ref, dcol_ref, dk_acc, dv_acc):
        i = pl.program_id(1)

        @pl.when(i == 0)
        def _():
            dk_acc[...] = jnp.zeros_like(dk_acc)
            dv_acc[...] = jnp.zeros_like(dv_acc)
            dcol_ref[...] = jnp.zeros_like(dcol_ref)

        def tile(te):
            lse, q16, do16 = lse_ref[0], q_ref[...].astype(BF16), do_ref[...].astype(BF16)
            probs = [(ks, jnp.exp(s - lse)) for ks, s in _attn_logits(q_ref, k_ref, c_ref, tq, te)]
            dps = [_dot(do16, v_ref[k0:k1, :].astype(BF16), "nt") for (k0, k1), _ in probs]
            delta = sum(jnp.sum(p * dp, axis=1, keepdims=True) for (_, p), dp in zip(probs, dps))
            dq = 0.0
            for ((k0, k1), p), dp in zip(probs, dps):
                ds = p * (dp - delta)
                ds16 = ds.astype(BF16)
                dq = dq + _dot(ds16, k_ref[k0:k1, :].astype(BF16))
                dk_acc[k0:k1, :] += _dot(ds16, q16, "tn") * scale
                dv_acc[k0:k1, :] += _dot(p.astype(BF16), do16, "tn")
                dcol_ref[0, :, k0:k1] += jnp.sum(ds, axis=0, keepdims=True)
            du_ref[te - tq:te, 0:dh] = (dq * scale).astype(BF16)

        _per_query_tile(i, nq, tq, tile)

        @pl.when(i == nq - 1)
        def _():
            du_ref[:, dh:2 * dh] = dk_acc[...].astype(BF16)
            du_ref[:, 2 * dh:3 * dh] = dv_acc[...].astype(BF16)

    return _pcall(
        body, name="fox_attn_bwd", grid=(FH, nq),
        in_specs=[pl.BlockSpec((tq, dh), lambda h, i: (i, 3 * h)), pl.BlockSpec((T, dh), lambda h, i: (0, 3 * h + 1)),
                  pl.BlockSpec((T, dh), lambda h, i: (0, 3 * h + 2)), pl.BlockSpec((1, 1, T), lambda h, i: (h, 0, 0)),
                  pl.BlockSpec((1, tq, 1), lambda h, i: (h, i, 0)), pl.BlockSpec((tq, dh), lambda h, i: (i, h)), _ANY],
        out_specs=[pl.BlockSpec((T, 3 * dh), lambda h, i: (0, h)), pl.BlockSpec((1, 1, T), lambda h, i: (h, 0, 0))],
        out_shape=[jax.ShapeDtypeStruct(du.shape, BF16), jax.ShapeDtypeStruct((FH, 1, T), F32)],
        scratch_shapes=[pltpu.VMEM((T, dh), F32), pltpu.VMEM((T, dh), F32)],
        input_output_aliases={6: 0},
        compiler_params=_cparams(("parallel", "arbitrary")),
    )(u, u, u, c_rows, lse, do, du)


def _head_indicators(cfg):
    ind = np.zeros((cfg.RW, LANES), np.float32)
    ind[np.arange(cfg.RW), np.arange(cfg.RW) // RWKV_HEAD_DIM] = 1.0
    pad = np.zeros((1, LANES), np.float32)
    pad[0, cfg.RH:] = 1.0
    return jnp.asarray(ind), jnp.asarray(ind.T.copy()), jnp.asarray(pad)


def _prep_fn(us_r, us_k, us_v, us_wd, us_ad, w0, w2p, a0, a2p, k_k, k_a, ind, ind_t, pad):
    wpre = w0 + _dot3(jnp.tanh(us_wd), w2p)
    w = -_softplus(-wpre) - 0.5
    lw = -jnp.exp(w)
    a = jax.nn.sigmoid(a0 + _dot3(us_ad, a2p))
    kk = us_k * k_k
    ss = _xdot(kk * kk, ind, ind_t) + pad
    inv = 1.0 / jnp.maximum(jnp.sqrt(ss), L2_EPS)
    kkn = kk * _xdot(inv, ind_t, ind)
    kp = us_k * (1.0 + (a - 1.0) * k_a)
    return us_r, lw, kp, us_v, -kkn, kkn * a


def _shifted(u, prev_row, mu, first):
    n = u.shape[0]
    rolled = pltpu.roll(u, 1, 0)
    row = lax.broadcasted_iota(jnp.int32, u.shape, 0)
    p0 = jnp.where(first, jnp.zeros_like(prev_row), prev_row)
    prev = jnp.where(row == 0, jnp.broadcast_to(p0, u.shape), rolled)
    return u + (prev - u) * mu, prev


def _rwkv_specs(cfg, tr):
    RW, LP = cfg.RW, cfg.LP
    base = cfg.o_rwkv // RW
    cols = [(RW, base), (RW, base + 1), (RW, base + 2), (RW, base + 3), (LP, cfg.o_wd // LP), (LP, cfg.o_ad // LP)]
    cur = [pl.BlockSpec((tr, w), (lambda i, cb=cb: (i, cb))) for w, cb in cols]
    prv = [pl.BlockSpec((8, w), (lambda i, cb=cb: (jnp.maximum(i * (tr // 8) - 1, 0), cb))) for w, cb in cols]
    return cols, cur, prv


def _mu_pieces(cfg, mu_ref):
    RW, LP = cfg.RW, cfg.LP
    offs = [0, RW, 2 * RW, 3 * RW, 4 * RW, 4 * RW + LP, 4 * RW + 2 * LP]
    return [mu_ref[:, offs[j]:offs[j + 1]] for j in range(6)]


def _rwkv_prep_fwd(cfg, u, mu, w0, w2p, a0, a2p, k_k, k_a):
    T, RW, LP, tr = cfg.T, cfg.RW, cfg.LP, cfg.tr
    ind, ind_t, pad = _head_indicators(cfg)
    cols, cur, prv = _rwkv_specs(cfg, tr)

    def body(*refs):
        u_refs, p_refs = refs[0:6], refs[6:12]
        mu_ref, w0_ref, w2_ref, a0_ref, a2_ref, kk_ref, ka_ref, ind_ref, indt_ref, pad_ref = refs[12:22]
        outs = refs[22:]
        first = pl.program_id(0) == 0
        mus = _mu_pieces(cfg, mu_ref)
        us = [_shifted(u_refs[j][...], p_refs[j][7:8, :], mus[j], first)[0] for j in range(6)]
        res = _prep_fn(us[0], us[1], us[2], us[4], us[5], w0_ref[...], w2_ref[...], a0_ref[...], a2_ref[...],
                       kk_ref[...], ka_ref[...], ind_ref[...], indt_ref[...], pad_ref[...])
        for j in range(6):
            outs[j][...] = res[j]
        outs[6][...] = us[3]

    consts = [mu, w0, w2p, a0, a2p, k_k, k_a, ind, ind_t, pad]
    return _pcall(body, name="rwkv_prep_fwd", grid=(T // tr,),
                  in_specs=cur + prv + [_const(c.shape) for c in consts],
                  out_specs=[_tile(tr, RW)] * 7, out_shape=[jax.ShapeDtypeStruct((T, RW), F32)] * 7,
                  compiler_params=_cparams(("parallel",)))(*([u] * 12), *consts)


def _rwkv_prep_bwd(cfg, u, mu, w0, w2p, a0, a2p, k_k, k_a, cots, dzb):
    T, RW, LP = cfg.T, cfg.RW, cfg.LP
    tr = min(128, T)
    ind, ind_t, pad = _head_indicators(cfg)
    cols, cur, prv = _rwkv_specs(cfg, tr)
    rseg = cfg.rseg

    def body(*refs):
        u_refs, p_refs = refs[0:6], refs[6:12]
        mu_ref, w0_ref, w2_ref, a0_ref, a2_ref, kk_ref, ka_ref, ind_ref, indt_ref, pad_ref = refs[12:22]
        cot_refs, dzb_ref = refs[22:28], refs[28]
        dus_ref, dmu_ref, dw0_ref, dw2_ref, da0_ref, da2_ref, dkk_ref, dka_ref = refs[29:]
        i = pl.program_id(0)
        first = i == 0
        mus = _mu_pieces(cfg, mu_ref)
        sh = [_shifted(u_refs[j][...], p_refs[j][7:8, :], mus[j], first) for j in range(6)]
        us = [s[0] for s in sh]
        fn = functools.partial(_prep_fn, ind=ind_ref[...], ind_t=indt_ref[...], pad=pad_ref[...])
        _, vjp = jax.vjp(fn, us[0], us[1], us[2], us[4], us[5], w0_ref[...], w2_ref[...], a0_ref[...], a2_ref[...],
                         kk_ref[...], ka_ref[...])
        d = vjp(tuple(c[...] for c in cot_refs))
        dus = [d[0], d[1], d[2], dzb_ref[...], d[3], d[4]]
        offs = [0, RW, 2 * RW, 3 * RW, 4 * RW, 4 * RW + LP, 4 * RW + 2 * LP]
        for j in range(6):
            dus_ref[:, offs[j]:offs[j + 1]] = dus[j]
            dmu_j = jnp.sum(dus[j] * (sh[j][1] - u_refs[j][...]), axis=0, keepdims=True)

            @pl.when(first)
            def _(j=j, dmu_j=dmu_j):
                dmu_ref[:, offs[j]:offs[j + 1]] = dmu_j

            @pl.when(i > 0)
            def _(j=j, dmu_j=dmu_j):
                dmu_ref[:, offs[j]:offs[j + 1]] += dmu_j
        for ref, val in zip((dw0_ref, dw2_ref, da0_ref, da2_ref, dkk_ref, dka_ref), d[5:11]):
            _acc_store(i, ref, val)

    consts = [mu, w0, w2p, a0, a2p, k_k, k_a, ind, ind_t, pad]
    vec = jax.ShapeDtypeStruct((1, RW), F32)
    mat = jax.ShapeDtypeStruct((LP, RW), F32)
    return _pcall(body, name="rwkv_prep_bwd", grid=(T // tr,),
                  in_specs=cur + prv + [_const(c.shape) for c in consts] + [_tile(tr, RW)] * 7,
                  out_specs=[_tile(tr, rseg), _const((1, rseg)), _const((1, RW)), _const((LP, RW)), _const((1, RW)),
                             _const((LP, RW)), _const((1, RW)), _const((1, RW))],
                  out_shape=[jax.ShapeDtypeStruct((T, rseg), F32), jax.ShapeDtypeStruct((1, rseg), F32),
                             vec, mat, vec, mat, vec, vec],
                  compiler_params=_cparams(("arbitrary",)))(*([u] * 12), *consts, *cots, dzb)


def _shift_bwd(cfg, dus, mu, df, du):
    T, tr, RW, LP = cfg.T, cfg.tr, cfg.RW, cfg.LP
    nb = T // tr
    tail = cfg.ncol - cfg.o_f
    assert cfg.o_rwkv % (4 * RW) == 0 and (4 * RW) % (2 * LP) == 0 and cfg.o_f % tail == 0

    def shifted(d_ref, n_ref, mu_ref):
        d = d_ref[...]
        rolled = pltpu.roll(d, tr - 1, 0)
        row = lax.broadcasted_iota(jnp.int32, d.shape, 0)
        n0 = jnp.where(pl.program_id(0) == nb - 1, jnp.zeros_like(n_ref[0:1, :]), n_ref[0:1, :])
        nxt = jnp.where(row == tr - 1, jnp.broadcast_to(n0, d.shape), rolled)
        mu_v = mu_ref[...]
        return (d * (1.0 - mu_v) + nxt * mu_v).astype(BF16)

    def main_body(d_ref, n_ref, mu_ref, du_in, du_ref):
        du_ref[...] = shifted(d_ref, n_ref, mu_ref)

    def tail_body(d_ref, n_ref, mu_ref, df_ref, du_in, du_ref):
        du_ref[:, 0:LANES] = df_ref[...]
        du_ref[:, LANES:LANES + 2 * LP] = shifted(d_ref, n_ref, mu_ref)
        if tail > LANES + 2 * LP:
            du_ref[:, LANES + 2 * LP:] = jnp.zeros((tr, tail - LANES - 2 * LP), BF16)

    def specs(w, cb):
        return [_tile(tr, w, cb),
                pl.BlockSpec((8, w), lambda i: (jnp.minimum((i + 1) * (tr // 8), T // 8 - 1), cb)),
                pl.BlockSpec((1, w), lambda i: (0, cb))]

    out = jax.ShapeDtypeStruct(du.shape, BF16)
    du = _pcall(main_body, name="shift_bwd_main", grid=(nb,), in_specs=specs(4 * RW, 0) + [_ANY],
                out_specs=_tile(tr, 4 * RW, cfg.o_rwkv // (4 * RW)), out_shape=out, input_output_aliases={3: 0},
                compiler_params=_cparams(("parallel",)))(dus, dus, mu, du)
    return _pcall(tail_body, name="shift_bwd_tail", grid=(nb,),
                  in_specs=specs(2 * LP, 4 * RW // (2 * LP)) + [_tile(tr, LANES), _ANY],
                  out_specs=_tile(tr, tail, cfg.o_f // tail), out_shape=out, input_output_aliases={4: 0},
                  compiler_params=_cparams(("parallel",)))(dus, dus, mu, df, du)


def _chunk_local(r, lw, k, v, a, b):
    H, C, K = r.shape
    row = lax.broadcasted_iota(jnp.int32, (C, C), 0)
    col = lax.broadcasted_iota(jnp.int32, (C, C), 1)
    incl = jnp.broadcast_to((row >= col).astype(F32)[None], (H, C, C))
    strict = (row > col)[None]
    lower = (row >= col)[None]
    eye = (row == col)[None]
    zero = jnp.zeros((), F32)
    L = _bdot(incl, lw, 2, 1)
    LC = jnp.sum(lw, axis=1, keepdims=True)
    eL = jnp.exp(L)
    eLn = jnp.exp(-L)
    at = a * jnp.exp(L - lw)
    rt = r * eL
    bt = b * eLn
    kt = k * eLn
    eR = jnp.exp(LC - L)
    bh = b * eR
    kh = k * eR
    keys = functools.partial(_bdot, passes=SCAN_PASSES[0])
    inv = functools.partial(_bdot, passes=SCAN_PASSES[1])
    app = functools.partial(_bdot, passes=SCAN_PASSES[2])
    ar = jnp.concatenate([at, rt], axis=1)
    g_b = app(ar, bt, 2, 2)
    g_k = keys(ar, kt, 2, 2)
    n_ab = jnp.where(strict, g_b[:, :C], zero)
    n_ak = jnp.where(strict, g_k[:, :C], zero)
    m_rb = jnp.where(lower, g_b[:, C:], zero)
    m_rk = jnp.where(lower, g_k[:, C:], zero)
    M = n_ab
    P = jnp.where(eye, 1.0, zero) + n_ab
    for _ in range(1, max(1, int(np.ceil(np.log2(C))))):
        M = inv(M, M, 2, 1)
        P = P + inv(M, P, 2, 1)
    W = app(P, at, 2, 1)
    Uloc = app(P, app(n_ak, v, 2, 1), 2, 1)
    Q = rt + app(m_rb, W, 2, 1)
    Yloc = app(m_rb, Uloc, 2, 1) + app(m_rk, v, 2, 1)
    A = jnp.where(eye, jnp.exp(LC), zero) + app(W, bh, 1, 1)
    Sloc = app(Uloc, bh, 1, 1) + app(v, kh, 1, 1)
    return Q, Yloc, A, Sloc


def _split_heads(ref, n):
    N = RWKV_HEAD_DIM
    return jnp.stack([ref[:, h * N:(h + 1) * N] for h in range(n)], axis=0)


def _merge_heads(x):
    return jnp.concatenate([x[h] for h in range(x.shape[0])], axis=1)


def _chains(x, cb):
    hb = x.shape[0]
    return x.reshape(hb, cb, -1, x.shape[-1]).reshape(hb * cb, -1, x.shape[-1])


def _unchains(x, cb, seq):
    hb = x.shape[0] // cb
    x = x.reshape(hb, cb, x.shape[1], x.shape[2])
    return x.reshape(hb, cb * x.shape[2], x.shape[3]) if seq else x


def _scan_local_specs(cfg):
    N, HB, CB = RWKV_HEAD_DIM, cfg.hb, cfg.cb
    grid = (cfg.RH // HB, cfg.T // (CB * cfg.C))
    seq = pl.BlockSpec((HB, CB * cfg.C, N), lambda h, j: (h, j, 0))
    mat = pl.BlockSpec((HB, CB, N, N), lambda h, j: (h, j, 0, 0))
    return grid, seq, mat


def _scan_local_fwd(cfg, seqs):
    T, RH, N = cfg.T, cfg.RH, RWKV_HEAD_DIM
    grid, seq, mat = _scan_local_specs(cfg)

    def body(r_ref, lw_ref, k_ref, v_ref, a_ref, b_ref, q_ref, yl_ref, a_out, sl_ref):
        ins = [_chains(_split_heads(ref, cfg.hb), cfg.cb) for ref in (r_ref, lw_ref, k_ref, v_ref, a_ref, b_ref)]
        Q, Yloc, A, Sloc = _chunk_local(*ins)
        q_ref[...] = _unchains(Q, cfg.cb, True)
        yl_ref[...] = _unchains(Yloc, cfg.cb, True)
        a_out[...] = _unchains(A, cfg.cb, False)
        sl_ref[...] = _unchains(Sloc, cfg.cb, False)

    tok = pl.BlockSpec((cfg.cb * cfg.C, cfg.hb * N), lambda h, j: (j, h))
    sq = jax.ShapeDtypeStruct((RH, T, N), F32)
    mt = jax.ShapeDtypeStruct((RH, T // cfg.C, N, N), F32)
    return _pcall(body, name="rwkv_scan_local_fwd", grid=grid, in_specs=[tok] * 6, out_specs=[seq, seq, mat, mat],
                  out_shape=[sq, sq, mt, mt], compiler_params=_cparams(("parallel", "parallel")))(*seqs)


def _scan_local_bwd(cfg, toks, dq, dy, da, dsl, extra, comm=None):
    T, RW, N = cfg.T, cfg.RW, RWKV_HEAD_DIM
    grid, seq, mat = _scan_local_specs(cfg)
    c_in, c_out, c_scr = comm[:3] if comm else ([], [], [])

    def body(r_ref, lw_ref, k_ref, v_ref, a_ref, b_ref, dq_ref, dy_ref, da_ref, dsl_ref, xr_ref, xk_ref, xv_ref,
             *rest):
        cin, outs = rest[:len(c_in)], rest[len(c_in):len(c_in) + 6]
        cout, scr = rest[len(c_in) + 6:len(c_in) + 6 + len(c_out)], rest[len(c_in) + 6 + len(c_out):]
        _comm_at(comm, 3, grid, cin, cout, scr)
        ins = [_chains(_split_heads(ref, cfg.hb), cfg.cb) for ref in (r_ref, lw_ref, k_ref, v_ref, a_ref, b_ref)]
        _, vjp = jax.vjp(_chunk_local, *ins)
        d = vjp((_chains(dq_ref[...], cfg.cb), _chains(_split_heads(dy_ref, cfg.hb), cfg.cb),
                 _chains(da_ref[...], cfg.cb), _chains(dsl_ref[...], cfg.cb)))
        add = {0: xr_ref, 2: xk_ref, 3: xv_ref}
        for j in range(6):
            dj = _merge_heads(_unchains(d[j], cfg.cb, True))
            outs[j][...] = dj + add[j][...] if j in add else dj
        _comm_at(comm, 4, grid, cin, cout, scr)

    tok = pl.BlockSpec((cfg.cb * cfg.C, cfg.hb * N), lambda h, j: (j, h))
    return _pcall(body, name="rwkv_scan_local_bwd", grid=grid,
                  in_specs=[tok] * 6 + [seq, tok, mat, mat] + [tok] * 3 + [_ANY] * len(c_in),
                  out_specs=[tok] * 6 + [_ANY] * len(c_out),
                  out_shape=[jax.ShapeDtypeStruct((T, RW), F32)] * 6 + list(c_out), scratch_shapes=list(c_scr),
                  compiler_params=_cparams(("arbitrary", "arbitrary") if comm else ("parallel", "parallel")),
                  )(*toks, dq, dy, da, dsl, *extra, *c_in)


def _scan_carry_specs(cfg, rev):
    N, RH, C, nc = RWKV_HEAD_DIM, cfg.RH, cfg.C, cfg.T // cfg.C
    at = (lambda j: nc - 1 - j) if rev else (lambda j: j)
    seq = pl.BlockSpec((RH, C, N), lambda j: (0, at(j), 0))
    mat = pl.BlockSpec((RH, 1, N, N), lambda j: (0, at(j), 0, 0))
    return nc, seq, mat


def _scan_carry_fwd(cfg, q, yloc, a, sloc):
    T, RH, N = cfg.T, cfg.RH, RWKV_HEAD_DIM
    nc, seq, mat = _scan_carry_specs(cfg, False)

    def body(q_ref, yl_ref, a_ref, sl_ref, y_ref, ck_ref, s_ref):
        @pl.when(pl.program_id(0) == 0)
        def _():
            s_ref[...] = jnp.zeros_like(s_ref)

        S = s_ref[...]
        ck_ref[:, 0] = S
        y_ref[...] = _merge_heads(_bdot(q_ref[...], S, 2, 2, SCAN_PASSES[2]) + yl_ref[...])
        s_ref[...] = _bdot(S, a_ref[:, 0], 2, 1) + sl_ref[:, 0]

    tok = pl.BlockSpec((cfg.C, cfg.RW), lambda j: (j, 0))
    return _pcall(body, name="rwkv_scan_carry_fwd", grid=(nc,), in_specs=[seq, seq, mat, mat], out_specs=[tok, mat],
                  out_shape=[jax.ShapeDtypeStruct((T, cfg.RW), F32), jax.ShapeDtypeStruct((RH, nc, N, N), F32)],
                  scratch_shapes=[pltpu.VMEM((RH, N, N), F32)],
                  compiler_params=_cparams(("arbitrary",)))(q, yloc, a, sloc)


def _scan_carry_bwd(cfg, q, a, ckpt, dy, comm=None):
    T, RH, N = cfg.T, cfg.RH, RWKV_HEAD_DIM
    nc, seq, mat = _scan_carry_specs(cfg, True)

    def body(q_ref, a_ref, ck_ref, dy_ref, dq_ref, da_ref, dsl_ref, ds_ref):
        @pl.when(pl.program_id(0) == 0)
        def _():
            ds_ref[...] = jnp.zeros_like(ds_ref)

        S, dS, dY = ck_ref[:, 0], ds_ref[...], _split_heads(dy_ref, RH)
        dq_ref[...] = _bdot(dY, S, 2, 1, SCAN_PASSES[2])
        da_ref[:, 0] = _bdot(S, dS, 1, 1, SCAN_PASSES[2])
        dsl_ref[:, 0] = dS
        ds_ref[...] = _bdot(dS, a_ref[:, 0], 2, 2) + _bdot(dY, q_ref[...], 1, 1, SCAN_PASSES[2])

    mt = jax.ShapeDtypeStruct((RH, nc, N, N), F32)
    tok = pl.BlockSpec((cfg.C, cfg.RW), lambda j: (nc - 1 - j, 0))
    body, c_in, c_out, c_shapes, c_scr = _hosted(body, 4, 3, (nc,), comm)
    return _pcall(body, name="rwkv_scan_carry_bwd", grid=(nc,), in_specs=[seq, mat, mat, tok] + c_in,
                  out_specs=[seq, mat, mat] + c_out,
                  out_shape=[jax.ShapeDtypeStruct((RH, T, N), F32), mt, mt] + c_shapes,
                  scratch_shapes=c_scr + [pltpu.VMEM((RH, N, N), F32)],
                  compiler_params=_cparams(("arbitrary",)))(q, a, ckpt, dy, *(comm[0] if comm else []))


def _post_fn(y, r, kp, v, zb, ln_w, ln_b, rk, ind, ind_t):
    n = float(RWKV_HEAD_DIM)
    mu = _xdot(_xdot(y, ind, ind_t) / n, ind_t, ind)
    yc = y - mu
    var = _xdot(yc * yc, ind, ind_t) / n
    rstd = _xdot(lax.rsqrt(var + GN_EPS), ind_t, ind)
    yn = yc * rstd * ln_w + ln_b
    bonus = _xdot(_xdot(r * kp * rk, ind, ind_t), ind_t, ind) * v
    return (yn + bonus) * _silu(zb)


def _rwkv_post_fwd(cfg, y, r, kp, v, zb, ln_w, ln_b, rk):
    T, RW, tr = cfg.T, cfg.RW, cfg.tr
    ind, ind_t, _ = _head_indicators(cfg)

    def body(y_ref, r_ref, k_ref, v_ref, z_ref, lw_ref, lb_ref, rk_ref, ind_ref, indt_ref, ob_ref):
        ob_ref[...] = _post_fn(y_ref[...], r_ref[...], k_ref[...], v_ref[...], z_ref[...], lw_ref[...], lb_ref[...],
                               rk_ref[...], ind_ref[...], indt_ref[...]).astype(BF16)

    consts = [ln_w, ln_b, rk, ind, ind_t]
    return _pcall(body, name="rwkv_post_fwd", grid=(T // tr,),
                  in_specs=[_tile(tr, RW)] * 5 + [_const(c.shape) for c in consts],
                  out_specs=_tile(tr, RW), out_shape=jax.ShapeDtypeStruct((T, RW), BF16),
                  compiler_params=_cparams(("parallel",)))(y, r, kp, v, zb, *consts)


def _rwkv_post_bwd(cfg, y, r, kp, v, zb, ln_w, ln_b, rk, dob):
    T, RW = cfg.T, cfg.RW
    tr = min(128, T)
    ind, ind_t, _ = _head_indicators(cfg)

    def body(y_ref, r_ref, k_ref, v_ref, z_ref, lw_ref, lb_ref, rk_ref, ind_ref, indt_ref, dob_ref,
             dy_ref, dr_ref, dk_ref, dv_ref, dz_ref, dlw_ref, dlb_ref, drk_ref):
        fn = functools.partial(_post_fn, ind=ind_ref[...], ind_t=indt_ref[...])
        _, vjp = jax.vjp(fn, y_ref[...], r_ref[...], k_ref[...], v_ref[...], z_ref[...], lw_ref[...], lb_ref[...],
                         rk_ref[...])
        d = vjp(dob_ref[...])
        for ref, val in zip((dy_ref, dr_ref, dk_ref, dv_ref, dz_ref), d[:5]):
            ref[...] = val
        i = pl.program_id(0)
        for ref, val in zip((dlw_ref, dlb_ref, drk_ref), d[5:8]):
            _acc_store(i, ref, val)

    consts = [ln_w, ln_b, rk, ind, ind_t]
    vec = jax.ShapeDtypeStruct((1, RW), F32)
    return _pcall(body, name="rwkv_post_bwd", grid=(T // tr,),
                  in_specs=[_tile(tr, RW)] * 5 + [_const(c.shape) for c in consts] + [_tile(tr, RW)],
                  out_specs=[_tile(tr, RW)] * 5 + [_const((1, RW))] * 3,
                  out_shape=[jax.ShapeDtypeStruct((T, RW), F32)] * 5 + [vec] * 3,
                  compiler_params=_cparams(("arbitrary",)))(y, r, kp, v, zb, *consts, dob)


def _adamw_math(w, g, m, v):
    m = ADAM_B1 * m + (1.0 - ADAM_B1) * g
    v = ADAM_B2 * v + (1.0 - ADAM_B2) * (g * g)
    m_hat = m / (1.0 - ADAM_B1 ** ADAM_STEP)
    v_hat = v / (1.0 - ADAM_B2 ** ADAM_STEP)
    delta = -ADAM_LR * (m_hat / (jnp.sqrt(v_hat) + ADAM_EPS) + ADAM_WD * w)
    return delta, m, v


def _adamw(name, w, g, m, v, copy_grad=False, comm=None):
    R, Cc = w.shape
    Rp = -(-R // 8) * 8
    tr = Rp
    for nb in range(1, Rp // 8 + 1):
        if (Rp // 8) % nb == 0 and (Rp // nb) * Cc * 4 <= 2 * 1024 * 1024:
            tr = Rp // nb
            break

    def body(w_ref, g_ref, m_ref, v_ref, d_ref, nm_ref, nv_ref, *g_out):
        g_v = g_ref[...]
        d, nm, nv = _adamw_math(w_ref[...], g_v, m_ref[...], v_ref[...])
        d_ref[...] = d
        nm_ref[...] = nm
        nv_ref[...] = nv
        if copy_grad:
            g_out[0][...] = g_v

    spec = _tile(tr, Cc)
    n_out = 4 if copy_grad else 3
    body, c_in, c_out, c_shapes, c_scr = _hosted(body, 4, n_out, (Rp // tr,), comm)
    return _pcall(body, name=name, grid=(Rp // tr,), in_specs=[spec] * 4 + c_in, out_specs=[spec] * n_out + c_out,
                  out_shape=[jax.ShapeDtypeStruct((R, Cc), F32)] * n_out + c_shapes, scratch_shapes=c_scr,
                  compiler_params=_cparams(("arbitrary",) if comm else ("parallel",)),
                  )(w, g, m, v, *(comm[0] if comm else []))


def _row_tile(R, Cc, itemsize, budget=2 * 1024 * 1024):
    for nb in range(1, R // 16 + 1):
        if R % nb == 0 and (R // nb) % 16 == 0 and (R // nb) * Cc * itemsize <= budget:
            return R // nb
    return R


def _add_halves(name, gs, r1, c_idx):
    S, R, Cc = gs.shape
    half = R // 2
    tr = _row_tile(half, Cc, 4)
    nb = half // tr

    def body(c_ref, g_ref, r_ref, o_ref):
        o_ref[...] = (g_ref[...].astype(F32) + r_ref[...].astype(F32)).astype(BF16)

    grid_spec = pltpu.PrefetchScalarGridSpec(
        num_scalar_prefetch=1, grid=(S, nb),
        in_specs=[pl.BlockSpec((1, tr, Cc), lambda s, i, c: (s, c[0] * nb + i, 0)),
                  pl.BlockSpec((1, tr, Cc), lambda s, i, c: (s, i, 0))],
        out_specs=pl.BlockSpec((1, tr, Cc), lambda s, i, c: (s, i, 0)))
    return _pcall(body, name=name, grid_spec=grid_spec, out_shape=jax.ShapeDtypeStruct((S, half, Cc), BF16),
                  compiler_params=_cparams(("parallel", "parallel")))(c_idx, gs, r1)


def _sum_slots(name, r2):
    S, R, Cc = r2.shape
    tr = _row_tile(R, Cc, 4 * S // 2 if r2.dtype == BF16 else 4 * S)

    def body(r_ref, o_ref):
        acc = r_ref[0].astype(F32)
        for s in range(1, S):
            acc = acc + r_ref[s].astype(F32)
        o_ref[...] = acc

    return _pcall(body, name=name, grid=(R // tr,), in_specs=[pl.BlockSpec((S, tr, Cc), lambda i: (0, i, 0))],
                  out_specs=_tile(tr, Cc), out_shape=jax.ShapeDtypeStruct((R, Cc), F32),
                  compiler_params=_cparams(("parallel",)))(r2)


def _sum_chips(name, recv, own, place):
    S, H, Cc = recv.shape
    tr = _row_tile(H, Cc, 4, 1024 * 1024)
    nb = H // tr

    def body(p_ref, r_ref, own_ref, o_ref):
        s = pl.program_id(1)
        me = p_ref[0]

        @pl.when(s == 0)
        def _():
            o_ref[...] = jnp.zeros_like(o_ref)

        @pl.when(s == me)
        def _():
            o_ref[...] += own_ref[0].astype(F32)

        @pl.when(s != me)
        def _():
            o_ref[...] += r_ref[0].astype(F32)

    grid_spec = pltpu.PrefetchScalarGridSpec(
        num_scalar_prefetch=1, grid=(nb, S),
        in_specs=[pl.BlockSpec((1, tr, Cc), lambda i, s, p: (jnp.where(s == p[0], (s + 1) % S, s), i, 0)),
                  pl.BlockSpec((1, tr, Cc), lambda i, s, p: (p[0], i, 0))],
        out_specs=pl.BlockSpec((tr, Cc), lambda i, s, p: (p[1] * nb + i, 0)))
    return _pcall(body, name=name, grid_spec=grid_spec, out_shape=jax.ShapeDtypeStruct((2 * H, Cc), F32),
                  compiler_params=_cparams(("parallel", "arbitrary")))(place, recv, own)


def _cast_bf16(name, w):
    R, Cc = w.shape
    tr = _row_tile(R, Cc, 4)

    def body(w_ref, o_ref):
        o_ref[...] = w_ref[...].astype(BF16)

    return _pcall(body, name=name, grid=(R // tr,), in_specs=[_tile(tr, Cc)], out_specs=_tile(tr, Cc),
                  out_shape=jax.ShapeDtypeStruct((R, Cc), BF16), compiler_params=_cparams(("parallel",)))(w)


_ANY = pl.BlockSpec(memory_space=pl.ANY)


def _place():
    x, y, c = lax.axis_index("x"), lax.axis_index("y"), lax.axis_index("c")
    others = [(1 - x, y), (x, 1 - y), (1 - x, 1 - y)]
    return x, y, c, others


def _gather_parts(shards):
    n = len(shards)
    halves = [s.shape[0] // 2 for s in shards]

    def parts(ins, outs, sems):
        x, y, c, _ = _place()
        me = 2 * x + y
        n1 = (x ^ (1 - c), y ^ c)
        n2 = (x ^ c, y ^ (1 - c))
        s1, s2, sd = 2 * n1[0] + n1[1], 2 * n2[0] + n2[1], 2 * (1 - x) + (1 - y)
        sib = (x, y, 1 - c)

        def rows(k, chip, hc):
            return outs[k].at[chip, pl.ds(hc * halves[k], halves[k]), :]

        def remote(k, j, src, dst, to):
            return pltpu.make_async_remote_copy(src_ref=src, dst_ref=dst, send_sem=sems[0].at[7 * k + j],
                                                recv_sem=sems[1].at[7 * k + j], device_id=to, device_id_type=MESH)

        def copy(k, j):
            if j == 6:
                return remote(k, j, ins[k], outs[k].at[me], sib)
            if j < 2:
                mine = ins[k].at[pl.ds(c * halves[k], halves[k]), :]
                return remote(k, j, mine, rows(k, me, c), (*(n1 if j == 0 else n2), c))
            land = rows(k, {2: s1, 3: s1, 4: s2, 5: sd}[j], c)
            return remote(k, j, land, land, (*n2, c) if j == 2 else sib)

        def arrived(k, j):
            hc = c if j < 3 else 1 - c
            land = outs[k].at[me] if j == 6 else rows(k, {0: s1, 1: s2, 2: sd, 3: s2, 4: s1, 5: sd}[j], hc)
            remote(k, j, land, land, (x, y, c)).wait_recv()

        return copy, arrived

    def start(ins, outs, sems):
        copy, _ = parts(ins, outs, sems)
        for k in range(n):
            copy(k, 0).start()
            copy(k, 1).start()
            copy(k, 6).start()

    def middle(ins, outs, sems):
        copy, arrived = parts(ins, outs, sems)
        for k in range(n):
            arrived(k, 0)
            copy(k, 2).start()
            copy(k, 3).start()
            arrived(k, 1)
            copy(k, 4).start()

    def finish(ins, outs, sems):
        copy, arrived = parts(ins, outs, sems)
        for k in range(n):
            arrived(k, 2)
            copy(k, 5).start()
        for k in range(n):
            for j in (3, 4, 5, 6):
                arrived(k, j)
        for k in range(n):
            for j in range(7):
                copy(k, j).wait_send()

    out_shapes = [jax.ShapeDtypeStruct((N_CHIPS,) + s.shape, s.dtype) for s in shards]
    scratch = [pltpu.SemaphoreType.DMA((7 * n,)), pltpu.SemaphoreType.DMA((7 * n,))]
    return list(shards), out_shapes, scratch, start, finish, middle


def _swap_halves(grads):
    n = len(grads)
    halves = [g.shape[1] // 2 for g in grads]

    def copies(ins, outs, sems):
        x, y, c, _ = _place()
        return [pltpu.make_async_remote_copy(
            src_ref=ins[k].at[:, pl.ds((1 - c) * halves[k], halves[k]), :], dst_ref=outs[k], send_sem=sems[0].at[k],
            recv_sem=sems[1].at[k], device_id=(x, y, 1 - c), device_id_type=MESH) for k in range(n)]

    def start(ins, outs, sems):
        for cp in copies(ins, outs, sems):
            cp.start()

    def finish(ins, outs, sems):
        for cp in copies(ins, outs, sems):
            cp.wait()

    out_shapes = [jax.ShapeDtypeStruct((g.shape[0], h) + g.shape[2:], g.dtype) for g, h in zip(grads, halves)]
    scratch = [pltpu.SemaphoreType.DMA((n,)), pltpu.SemaphoreType.DMA((n,))]
    return list(grads), out_shapes, scratch, start, finish


def _scatter_to_owners(chip_sums):
    n = len(chip_sums)

    def sends(ins, outs, sems):
        x, y, c, others = _place()
        me = 2 * x + y
        return [pltpu.make_async_remote_copy(
            src_ref=ins[k].at[2 * px + py], dst_ref=outs[k].at[me], send_sem=sems[0].at[3 * k + j],
            recv_sem=sems[1].at[3 * k + j], device_id=(px, py, c), device_id_type=MESH)
            for k in range(n) for j, (px, py) in enumerate(others)]

    def start(ins, outs, sems):
        for cp in sends(ins, outs, sems):
            cp.start()

    def finish(ins, outs, sems):
        x, y, c, others = _place()
        for k in range(n):
            for j, (px, py) in enumerate(others):
                land = outs[k].at[2 * px + py]
                pltpu.make_async_remote_copy(src_ref=land, dst_ref=land, send_sem=sems[0].at[3 * k + j],
                                             recv_sem=sems[1].at[3 * k + j], device_id=(x, y, c),
                                             device_id_type=MESH).wait_recv()
        for cp in sends(ins, outs, sems):
            cp.wait_send()

    out_shapes = [jax.ShapeDtypeStruct(g.shape, g.dtype) for g in chip_sums]
    scratch = [pltpu.SemaphoreType.DMA((3 * n,)), pltpu.SemaphoreType.DMA((3 * n,))]
    return list(chip_sums), out_shapes, scratch, start, finish


def _swap_with_sibling(arrays):
    n = len(arrays)

    def copies(ins, outs, sems):
        x, y, c, _ = _place()
        return [pltpu.make_async_remote_copy(src_ref=ins[k], dst_ref=outs[k], send_sem=sems[0].at[k],
                                             recv_sem=sems[1].at[k], device_id=(x, y, 1 - c), device_id_type=MESH)
                for k in range(n)]

    def start(ins, outs, sems):
        for cp in copies(ins, outs, sems):
            cp.start()

    def finish(ins, outs, sems):
        for cp in copies(ins, outs, sems):
            cp.wait()

    out_shapes = [jax.ShapeDtypeStruct(a.shape, a.dtype) for a in arrays]
    scratch = [pltpu.SemaphoreType.DMA((n,)), pltpu.SemaphoreType.DMA((n,))]
    return list(arrays), out_shapes, scratch, start, finish


def _add_pair(name, a, b):
    R, Cc = a.shape
    tr = _row_tile(R, Cc, 4)

    def body(a_ref, b_ref, o_ref):
        o_ref[...] = (a_ref[...].astype(F32) + b_ref[...].astype(F32)).astype(BF16)

    return _pcall(body, name=name, grid=(R // tr,), in_specs=[_tile(tr, Cc)] * 2, out_specs=_tile(tr, Cc),
                  out_shape=jax.ShapeDtypeStruct((R, Cc), BF16), compiler_params=_cparams(("parallel",)))(a, b)


def _second_neighbour():
    x, y, c, _ = _place()
    return (x, y, c), (x ^ c, y ^ (1 - c)), (x ^ (1 - c), y ^ c)


def _scatter_stage1(chip_sums):
    n = len(chip_sums)

    def copies(ins, outs, sems):
        (x, y, c), n2, n1 = _second_neighbour()
        diag = 2 * (1 - x) + (1 - y)
        return [pltpu.make_async_remote_copy(
            src_ref=ins[k].at[slot], dst_ref=outs[2 * k + j], send_sem=sems[0].at[2 * k + j],
            recv_sem=sems[1].at[2 * k + j], device_id=(*n2, c), device_id_type=MESH)
            for k in range(n) for j, slot in enumerate((2 * n2[0] + n2[1], diag))]

    def start(ins, outs, sems):
        for cp in copies(ins, outs, sems):
            cp.start()

    def finish(ins, outs, sems):
        for cp in copies(ins, outs, sems):
            cp.wait()

    out_shapes = [jax.ShapeDtypeStruct(g.shape[1:], g.dtype) for g in chip_sums for _ in range(2)]
    scratch = [pltpu.SemaphoreType.DMA((2 * n,)), pltpu.SemaphoreType.DMA((2 * n,))]
    return list(chip_sums), out_shapes, scratch, start, finish


def _scatter_stage2(passed):
    n = len(passed)

    def copies(ins, outs, sems):
        (x, y, c), n2, n1 = _second_neighbour()
        return [pltpu.make_async_remote_copy(src_ref=ins[k], dst_ref=outs[k], send_sem=sems[0].at[k],
                                             recv_sem=sems[1].at[k], device_id=(*n1, c), device_id_type=MESH)
                for k in range(n)]

    def start(ins, outs, sems):
        for cp in copies(ins, outs, sems):
            cp.start()

    def finish(ins, outs, sems):
        for cp in copies(ins, outs, sems):
            cp.wait()

    out_shapes = [jax.ShapeDtypeStruct(p.shape, p.dtype) for p in passed]
    scratch = [pltpu.SemaphoreType.DMA((n,)), pltpu.SemaphoreType.DMA((n,))]
    return list(passed), out_shapes, scratch, start, finish


def _add_passed(name, own, got, slot):
    _, H, Cc = own.shape
    tr = _row_tile(H, Cc, 4)

    def body(s_ref, o_ref, g_ref, out_ref):
        out_ref[...] = (o_ref[0].astype(F32) + g_ref[...].astype(F32)).astype(BF16)

    grid_spec = pltpu.PrefetchScalarGridSpec(
        num_scalar_prefetch=1, grid=(H // tr,),
        in_specs=[pl.BlockSpec((1, tr, Cc), lambda i, s: (s[0], i, 0)), pl.BlockSpec((tr, Cc), lambda i, s: (i, 0))],
        out_specs=pl.BlockSpec((tr, Cc), lambda i, s: (i, 0)))
    return _pcall(body, name=name, grid_spec=grid_spec, out_shape=jax.ShapeDtypeStruct((H, Cc), BF16),
                  compiler_params=_cparams(("parallel",)))(slot, own, got)


def _sum_stages(name, own, direct, via, place, transposed=False):
    _, H, Cc = own.shape
    tr = LANES if transposed else _row_tile(H, Cc, 4, 1024 * 1024)
    nb = H // tr

    def body(p_ref, own_ref, d_ref, v_ref, o_ref):
        acc = (own_ref[0].astype(F32) + d_ref[...].astype(F32)) + v_ref[...].astype(F32)
        o_ref[...] = acc.T if transposed else acc

    flat = pl.BlockSpec((tr, Cc), lambda i, p: (i, 0))
    out_spec = (pl.BlockSpec((Cc, tr), lambda i, p: (0, p[1] * nb + i)) if transposed
                else pl.BlockSpec((tr, Cc), lambda i, p: (p[1] * nb + i, 0)))
    grid_spec = pltpu.PrefetchScalarGridSpec(
        num_scalar_prefetch=1, grid=(nb,),
        in_specs=[pl.BlockSpec((1, tr, Cc), lambda i, p: (p[0], i, 0)), flat, flat], out_specs=out_spec)
    return _pcall(body, name=name, grid_spec=grid_spec,
                  out_shape=jax.ShapeDtypeStruct((Cc, 2 * H) if transposed else (2 * H, Cc), F32),
                  compiler_params=_cparams(("parallel",)))(place, own, direct, via)


def _join_halves(fulls, axes, small):
    n = len(fulls)
    hs = [f.shape[ax] // 2 for f, ax in zip(fulls, axes)]
    rel = [(dx, dy, dc) for dx in (0, 1) for dy in (0, 1) for dc in (0, 1)][1:]

    def half(ref, k, hc):
        part = pl.ds(hc * hs[k], hs[k])
        return ref.at[:, part] if axes[k] else ref.at[part, :]

    def body(*refs):
        ins, small_in = refs[:n], refs[n]
        outs, small_out = refs[n + 1:2 * n + 1], refs[2 * n + 1]
        send_sems, recv_sems, ssend, srecv, local_sem = refs[2 * n + 2:]
        x, y, c, _ = _place()
        dev = 4 * x + 2 * y + c
        local = pltpu.make_async_copy(small_in, small_out.at[dev], local_sem)
        local.start()
        cps = []
        for k in range(n):
            cp = pltpu.make_async_remote_copy(src_ref=half(ins[k], k, c), dst_ref=half(outs[k], k, c),
                                              send_sem=send_sems.at[k], recv_sem=recv_sems.at[k],
                                              device_id=(x, y, 1 - c), device_id_type=MESH)
            cp.start()
            cps.append(cp)
        for r, (dx, dy, dc) in enumerate(rel):
            cp = pltpu.make_async_remote_copy(src_ref=small_in, dst_ref=small_out.at[dev], send_sem=ssend.at[r],
                                              recv_sem=srecv.at[r], device_id=(x ^ dx, y ^ dy, c ^ dc),
                                              device_id_type=MESH)
            cp.start()
            cps.append(cp)
        for k in range(n):
            land = half(outs[k], k, 1 - c)
            pltpu.make_async_remote_copy(src_ref=land, dst_ref=land, send_sem=send_sems.at[k],
                                         recv_sem=recv_sems.at[k], device_id=(x, y, c), device_id_type=MESH).wait_recv()
        for r, (dx, dy, dc) in enumerate(rel):
            land = small_out.at[4 * (x ^ dx) + 2 * (y ^ dy) + (c ^ dc)]
            pltpu.make_async_remote_copy(src_ref=land, dst_ref=land, send_sem=ssend.at[r], recv_sem=srecv.at[r],
                                         device_id=(x, y, c), device_id_type=MESH).wait_recv()
        for cp in cps:
            cp.wait_send()
        local.wait()

    return _pcall(
        body, name="join_halves", in_specs=[_ANY] * (n + 1), out_specs=[_ANY] * (n + 1),
        out_shape=[jax.ShapeDtypeStruct(f.shape, f.dtype) for f in fulls]
        + [jax.ShapeDtypeStruct((N_DEV,) + small.shape, small.dtype)],
        input_output_aliases={k: k for k in range(n)},
        scratch_shapes=[pltpu.SemaphoreType.DMA((n,)), pltpu.SemaphoreType.DMA((n,)), pltpu.SemaphoreType.DMA((7,)),
                        pltpu.SemaphoreType.DMA((7,)), pltpu.SemaphoreType.DMA],
    )(*fulls, small)


def _local_step(cfg, x2, target, norm_gain, w_my, fb, mu_g, w0, a0, k_k, k_a, r_k, ln_w, ln_b, fng, rest,
                exchange=None, h=None):
    T, D, FW, FH, RW, RH, LP, lora = cfg.T, cfg.D, cfg.FW, cfg.FH, cfg.RW, cfg.RH, cfg.LP, cfg.lora
    fb_p = jnp.pad(fb, ((0, 0), (0, LANES - FH)))
    mu = _rwkv_vec_to_my(cfg, mu_g)
    rk = r_k.reshape(1, RW)
    tm = min(1024, T)

    if h is None:
        h = _rms_fwd(cfg, x2, norm_gain)
    if len(rest) == 2:
        u, *got = _mm("in_proj", h, w_my, "nn", F32, tm, cfg.tn, 2048, comm=rest[0])
        rest = rest[1](got)
    else:
        u = _mm("in_proj", h, w_my, "nn", F32, tm, cfg.tn, 2048)
    w2, a2, wpf, wpr, wout = rest
    w2p = jnp.pad(w2, ((0, LP - lora), (0, 0)))
    a2p = jnp.pad(a2, ((0, LP - lora), (0, 0)))
    c_cols = _fox_prep(cfg, u, fb_p)
    c_rows = c_cols[:, :FH].T.reshape(FH, 1, T)
    o, lse = _attn_fwd(cfg, u, c_rows)
    oa = _gate_a_fwd(cfg, o, u)
    prep = _rwkv_prep_fwd(cfg, u, mu, w0, w2p, a0, a2p, k_k, k_a)
    r, lw, kp, v, an, b, zb = prep
    toks = [r, lw, kp, v, an, b]
    q_s, yloc, a_m, sloc = _scan_local_fwd(cfg, toks)
    y, ckpt = _scan_carry_fwd(cfg, q_s, yloc, a_m, sloc)
    ob = _rwkv_post_fwd(cfg, y, r, kp, v, zb, ln_w, ln_b, rk)
    pa = _mm("proj_fox", oa, wpf, "nn", F32, tm, 1024, 2048)
    pb = _mm("proj_rwkv", ob, wpr, "nn", F32, tm, 1024, 2048)
    m = _merge_fwd(cfg, pa, pb, u)
    mo = _mm("out_proj", m, wout, "nn", F32, tm, 1024, 2048)
    loss8, dres, dres16, d_fng = _final(cfg, x2, mo, fng.reshape(1, D), target)

    dm = _mm("out_proj_dx", dres16, wout, "nt", F32, tm, 1024, 2048)
    d_wout = _mm("out_proj_dw", m, dres16, "tn", BF16, 1024, 1024, 2048)
    dpa, dpb, du = _merge_bwd(cfg, pa, pb, u, dm)
    doa = _mm("proj_fox_dx", dpa, wpf, "nt", F32, tm, 1024, 2048)
    d_wpf = _mm("proj_fox_dw", oa, dpa, "tn", BF16, 1024, 1024, 2048)
    dob = _mm("proj_rwkv_dx", dpb, wpr, "nt", F32, tm, 1024, 2048)
    d_wpr = _mm("proj_rwkv_dw", ob, dpb, "tn", BF16, 1024, 1024, 2048)

    do, du = _gate_a_bwd(cfg, o, u, doa, du)
    du, dcol = _attn_bwd(cfg, u, c_rows, lse, do, du)
    dc = jnp.pad(-dcol.reshape(FH, T).T, ((0, 0), (0, LANES - FH)))
    df, d_fb = _fox_prep_bwd(cfg, u, fb_p, dc)

    dy, dr_p, dk_p, dv_p, dzb, d_lnw, d_lnb, d_rk = _rwkv_post_bwd(cfg, y, r, kp, v, zb, ln_w, ln_b, rk, dob)
    early = dict(w_proj_fox=d_wpf, w_proj_rwkv=d_wpr, w_out=d_wout)
    res = _scan_carry_bwd(cfg, q_s, a_m, ckpt, dy, exchange(early) if exchange else None)
    dq_s, da_m, dsl = res[:3]
    res = _scan_local_bwd(cfg, toks, dq_s, dy, da_m, dsl, [dr_p, dk_p, dv_p],
                          exchange(("swapped", list(res[3:]))) if exchange else None)
    cots, received = res[:6], list(res[6:])
    dus, d_mu, d_w0, d_w2p, d_a0, d_a2p, d_kk, d_ka = _rwkv_prep_bwd(cfg, u, mu, w0, w2p, a0, a2p, k_k, k_a, cots, dzb)
    du = _shift_bwd(cfg, dus, mu, df, du)
    if exchange:
        late = dict(w_in=exchange((h, du, d_w2p[:lora], d_a2p[:lora])))
    else:
        late = dict(w_in=_mm("in_proj_dw", h, du, "tn", BF16, 1024, cfg.tn, 2048), rwkv_w2=d_w2p[:lora],
                    rwkv_a2=d_a2p[:lora])
    tkx = 2 * cfg.tn if cfg.ncol % (2 * cfg.tn) == 0 else cfg.tn
    res = _mm("in_proj_dx", du, w_my, "nt", F32, tm, 1024, tkx, comm=exchange(late) if exchange else None)
    dh = res[0] if exchange else res
    big = dict(early, **late)
    res = _rms_bwd(cfg, x2, norm_gain, dh, dres, exchange(list(res[1:])) if exchange else None)
    gx, d_ng = res[:2]
    received += list(res[2:])

    small = dict(norm_gain=d_ng, fox_forget_bias=d_fb[:, :FH], rwkv_shift_mix=_rwkv_vec_from_my(cfg, d_mu),
                 rwkv_w0=d_w0, rwkv_a0=d_a0, rwkv_k_k=d_kk, rwkv_k_a=d_ka, rwkv_r_k=d_rk, rwkv_ln_w=d_lnw,
                 rwkv_ln_b=d_lnb, final_norm_gain=d_fng)
    return loss8[0, 0], gx, small, big, received


_SMALL = ["norm_gain", "fox_forget_bias", "rwkv_shift_mix", "rwkv_w0", "rwkv_a0", "rwkv_k_k", "rwkv_k_a", "rwkv_r_k",
          "rwkv_ln_w", "rwkv_ln_b", "final_norm_gain"]
_WEIGHTS = ["norm_gain", "w_in", "fox_forget_bias", "rwkv_shift_mix", "rwkv_w0", "rwkv_w2", "rwkv_a0", "rwkv_a2",
            "rwkv_k_k", "rwkv_k_a", "rwkv_r_k", "rwkv_ln_w", "rwkv_ln_b", "w_proj_fox", "w_proj_rwkv", "w_out",
            "final_norm_gain"]


def _pack_small(arrs):
    parts, n = [], 0
    for a in arrs:
        f = a.reshape(-1)
        fill = (-f.shape[0]) % LANES
        parts += [f] + ([jnp.zeros((fill,), f.dtype)] if fill else [])
        n += f.shape[0] + fill
    tail = ((-(n // LANES)) % 8) * LANES
    return jnp.concatenate(parts + ([jnp.zeros((tail,), parts[0].dtype)] if tail else [])).reshape(-1, LANES)


def _unpack_small(packed, shapes):
    flat = packed.reshape(-1)
    out, pos = [], 0
    for s in shapes:
        n = int(np.prod(s))
        out.append(flat[pos:pos + n].reshape(s))
        pos += n + ((-n) % LANES)
    return out


def _shard_major(a, axis):
    parts = jnp.split(a, N_CHIPS, axis=axis)
    return jnp.stack(parts, axis=0)


def kernel(x, norm_gain, w_in, fox_forget_bias, rwkv_shift_mix, rwkv_w0, rwkv_w2, rwkv_a0, rwkv_a2, rwkv_k_k, rwkv_k_a, rwkv_r_k, rwkv_ln_w, rwkv_ln_b, w_proj_fox, w_proj_rwkv, w_out, final_norm_gain, loss_target, m_norm_gain, m_w_in, m_fox_forget_bias, m_rwkv_shift_mix, m_rwkv_w0, m_rwkv_w2, m_rwkv_a0, m_rwkv_a2, m_rwkv_k_k, m_rwkv_k_a, m_rwkv_r_k, m_rwkv_ln_w, m_rwkv_ln_b, m_w_proj_fox, m_w_proj_rwkv, m_w_out, m_final_norm_gain, v_norm_gain, v_w_in, v_fox_forget_bias, v_rwkv_shift_mix, v_rwkv_w0, v_rwkv_w2, v_rwkv_a0, v_rwkv_a2, v_rwkv_k_k, v_rwkv_k_a, v_rwkv_r_k, v_rwkv_ln_w, v_rwkv_ln_b, v_w_proj_fox, v_w_proj_rwkv, v_w_out, v_final_norm_gain):
    args = dict(locals())
    T, D = x.shape[1], x.shape[2]
    lora = rwkv_w2.shape[1]
    cfg = _Cfg(T, D, lora)
    RW = cfg.RW
    c_idx = lax.axis_index("c").astype(jnp.int32).reshape(1)
    me_chip = (2 * lax.axis_index("x") + lax.axis_index("y")).astype(jnp.int32)
    place = jnp.concatenate([me_chip.reshape(1), c_idx])

    w_in_s = w_in[0].astype(BF16)
    lora_s = jnp.concatenate([rwkv_w2[0], rwkv_a2[0]], axis=0)
    h, g_in = _rms_fwd(cfg, x[0], norm_gain, _gather_parts([w_in_s]))
    w_my = _shards_to_my_layout(cfg, g_in)
    mine = [_cast_bf16("cast_w_proj_fox", w_proj_fox[0]), _cast_bf16("cast_w_proj_rwkv", w_proj_rwkv[0]),
            _cast_bf16("cast_w_out", w_out[0]), lora_s]

    def unpack(gathered):
        g_wpf, g_wpr, g_out, g_lora = gathered
        lo = g_lora.transpose(1, 0, 2).reshape(2 * lora, RW)
        return (lo[:lora], lo[lora:], g_wpf.transpose(1, 0, 2).reshape(RW, D),
                g_wpr.transpose(1, 0, 2).reshape(RW, D), g_out.reshape(D, D))

    early, late = ["w_proj_fox", "w_proj_rwkv", "w_out"], ["w_in", "lora"]
    names = early + late
    chip_sums, direct, shard_major = {}, {}, []
    n1_slot = (2 * (lax.axis_index("x") ^ (1 - lax.axis_index("c")))
               + (lax.axis_index("y") ^ lax.axis_index("c"))).astype(jnp.int32).reshape(1)

    def exchange(got):
        if isinstance(got, tuple) and len(got) == 4:
            h, du, d_w2, d_a2 = got
            c, half = lax.axis_index("c"), D // 2
            cols = lambda base: lax.dynamic_slice_in_dim(h, base * half, half, axis=1)
            lora_g = _shard_major(jnp.concatenate([d_w2, d_a2], axis=0).astype(BF16), 1)
            lora_rows = lambda base: lax.dynamic_slice_in_dim(lora_g, base * lora, lora, axis=1).reshape(-1, RW // 4)
            tiles = (BF16, min(1024, half), cfg.tn, 2048)
            sent = _mm("in_proj_dw_sibling", cols(1 - c), du, "tn", *tiles)
            kept, got_w, got_l = _mm("in_proj_dw", cols(c), du, "tn", *tiles,
                                     comm=_swap_with_sibling([sent, lora_rows(1 - c)]))
            return (_add_pair("add_halves_w_in", kept, got_w),
                    _add_pair("add_halves_lora", lora_rows(c), got_l).reshape(N_CHIPS, lora, RW // 4))
        if isinstance(got, dict):
            if "w_in" in got:
                sums = [_my_layout_to_shards(cfg, got["w_in"][0]), got["w_in"][1]]
                chip_sums.update(zip(late, sums))
                return _scatter_stage1(sums)
            shard_major.extend([_shard_major(got["w_proj_fox"], 1), _shard_major(got["w_proj_rwkv"], 1),
                                _shard_major(got["w_out"], 0)])
            return _swap_halves(shard_major)
        if got[0] == "swapped":
            sums = [_add_halves("add_halves_" + nm, g, r, c_idx) for nm, g, r in zip(early, shard_major, got[1])]
            chip_sums.update(zip(early, sums))
            return _scatter_to_owners(sums)
        direct.update(zip(late, got[0::2]))
        return _scatter_stage2([_add_passed("add_passed_" + nm, chip_sums[nm], g, n1_slot)
                                for nm, g in zip(late, got[1::2])])

    loss_dev, gx, small, _, recv2 = _local_step(
        cfg, x[0], loss_target[0], norm_gain, w_my, fox_forget_bias, rwkv_shift_mix, rwkv_w0, rwkv_a0, rwkv_k_k,
        rwkv_k_a, rwkv_r_k, rwkv_ln_w, rwkv_ln_b, final_norm_gain, (_gather_parts(mine), unpack), exchange, h)
    loss = lax.psum(loss_dev, ("x", "y", "c"))

    small_shapes = [args[nm].shape for nm in _SMALL]
    packed = _pack_small([small[nm] for nm in _SMALL])
    reduced = [_sum_chips("sum_chips_" + nm, r, chip_sums[nm], place) for nm, r in zip(early, recv2[:3])]
    reduced += [_sum_stages("sum_stages_" + nm, chip_sums[nm], direct[nm], via, place, transposed=nm == "w_in")
                for nm, via in zip(late, recv2[3:])]
    *joined, small_all = _join_halves(reduced, [int(nm == "w_in") for nm in names], packed)
    g_small = _sum_slots("sum_small", small_all)

    grads = dict(zip(_SMALL, _unpack_small(g_small, small_shapes)))
    grads.update({nm: g[None] for nm, g in zip(names, joined) if nm not in ("lora", "w_in")})
    g_lora_f = joined[names.index("lora")]
    grads["rwkv_w2"] = g_lora_f[None, :lora]
    grads["rwkv_a2"] = g_lora_f[None, lora:]

    delta, new_m, new_v = {}, {}, {}
    w_small = _pack_small([args[nm] for nm in _SMALL])
    m_small = _pack_small([args["m_" + nm] for nm in _SMALL])
    v_small = _pack_small([args["v_" + nm] for nm in _SMALL])
    d_s, m_s, v_s = _adamw("adamw_small", w_small, g_small, m_small, v_small)
    for tgt, pk in ((delta, d_s), (new_m, m_s), (new_v, v_s)):
        tgt.update(zip(_SMALL, _unpack_small(pk, small_shapes)))
    t_out = _adamw("adamw_w_in", w_in[0].T, joined[names.index("w_in")], m_w_in[0].T, v_w_in[0].T, copy_grad=True)
    delta["w_in"], new_m["w_in"], new_v["w_in"], grads["w_in"] = [t.T[None] for t in t_out]
    for nm in ("w_proj_fox", "w_proj_rwkv", "w_out", "rwkv_w2", "rwkv_a2"):
        shp = args[nm].shape
        two_d = (shp[1], shp[2])
        d_b, m_b, v_b = _adamw("adamw_" + nm, args[nm].reshape(two_d), grads[nm].reshape(two_d),
                               args["m_" + nm].reshape(two_d), args["v_" + nm].reshape(two_d))
        delta[nm], new_m[nm], new_v[nm] = d_b.reshape(shp), m_b.reshape(shp), v_b.reshape(shp)

    return (loss, gx[None], *[grads[n] for n in _WEIGHTS], *[delta[n] for n in _WEIGHTS],
            *[new_m[n] for n in _WEIGHTS], *[new_v[n] for n in _WEIGHTS])
```

```python
import functools

import numpy as np
import jax
import jax.numpy as jnp
from jax import lax
from jax.experimental import pallas as pl
from jax.experimental.pallas import tpu as pltpu

F32 = jnp.float32
BF16 = jnp.bfloat16
HI = lax.Precision.HIGHEST
MESH = pl.DeviceIdType.MESH

FOX_HEAD_DIM = 128
RWKV_HEAD_DIM = 64
RMS_EPS = 1e-6
GN_EPS = 64e-5
L2_EPS = 1e-12
ADAM_LR = 0.001
ADAM_B1 = 0.9
ADAM_B2 = 0.999
ADAM_EPS = 1e-08
ADAM_WD = 0.01
ADAM_STEP = 10

LANES = 128
VMEM_LIMIT = 56 * 1024 * 1024
SCAN_CHUNK = 64
SCAN_HEADS_PER_STEP = 16
SCAN_CHUNKS_PER_STEP = 2
SCAN_PASSES = ((3, 1), 1, 1)
N_CHIPS = 4
N_DEV = 8

_pcall = pl.pallas_call


def _cparams(sem=None):
    return pltpu.CompilerParams(dimension_semantics=sem, vmem_limit_bytes=VMEM_LIMIT)


def _softplus(x):
    return jnp.maximum(x, 0.0) + jnp.log(1.0 + jnp.exp(-jnp.abs(x)))


def _silu(z):
    return z * jax.nn.sigmoid(z)


def _rmsn(x, g):
    return x * lax.rsqrt(jnp.mean(x * x, axis=-1, keepdims=True) + RMS_EPS) * g


def _dot(a, b, dims="nn", precision=None):
    dn = {"nn": (((1,), (0,)), ((), ())), "nt": (((1,), (1,)), ((), ())), "tn": (((0,), (0,)), ((), ()))}[dims]
    return lax.dot_general(a, b, dn, precision=precision, preferred_element_type=F32)


def _split_bf16(x):
    hi = x.astype(BF16)
    return hi, (x - hi.astype(F32)).astype(BF16)


def _bdot_raw(a, b, ca, cb, passes):
    dn = (((ca,), (cb,)), ((0,), (0,)))
    mm = lambda p, q: lax.dot_general(p, q, dn, preferred_element_type=F32)
    passes = passes[0] if isinstance(passes, tuple) else passes
    if passes == 1:
        return mm(a.astype(BF16), b.astype(BF16))
    ah, al = _split_bf16(a)
    bh, bl = _split_bf16(b)
    return mm(ah, bh) + (mm(ah, bl) + mm(al, bh))


@functools.partial(jax.custom_vjp, nondiff_argnums=(2, 3, 4))
def _bdot_p(a, b, ca, cb, passes):
    return _bdot_raw(a, b, ca, cb, passes)


def _bdot_fwd(a, b, ca, cb, passes):
    return _bdot_raw(a, b, ca, cb, passes), (a, b)


def _bdot_bwd(ca, cb, passes, res, g):
    a, b = res
    passes = passes[1] if isinstance(passes, tuple) else passes
    if (ca, cb) == (2, 1):
        return _bdot_p(g, b, 2, 2, passes), _bdot_p(a, g, 1, 1, passes)
    if (ca, cb) == (2, 2):
        return _bdot_p(g, b, 2, 1, passes), _bdot_p(g, a, 1, 1, passes)
    assert (ca, cb) == (1, 1)
    return _bdot_p(b, g, 2, 2, passes), _bdot_p(a, g, 2, 1, passes)


_bdot_p.defvjp(_bdot_fwd, _bdot_bwd)


def _bdot(a, b, ca, cb, passes=3):
    return _bdot_p(a, b, ca, cb, passes)


def _dot3(a, b):
    return _bdot(a[None], b[None], 2, 1)[0]


@jax.custom_vjp
def _xdot(x, m, mt):
    hi, lo = _split_bf16(x)
    m16 = m.astype(BF16)
    return _dot(hi, m16) + _dot(lo, m16)


def _xdot_fwd(x, m, mt):
    return _xdot(x, m, mt), (m, mt)


def _xdot_bwd(res, g):
    m, mt = res
    return _xdot(g, mt, m), jnp.zeros_like(m), jnp.zeros_like(mt)


_xdot.defvjp(_xdot_fwd, _xdot_bwd)


class _Cfg:
    def __init__(self, T, D, lora):
        self.T, self.D, self.lora = T, D, lora
        self.FW = D // 2
        self.FH = self.FW // FOX_HEAD_DIM
        self.RW = D // 2
        self.RH = self.RW // RWKV_HEAD_DIM
        self.LP = -(-lora // LANES) * LANES
        self.o_fox = 0
        self.o_rwkv = 4 * self.FW
        self.o_gate = self.o_rwkv + 4 * self.RW
        self.o_f = self.o_gate + 2 * D
        self.o_wd = self.o_f + LANES
        self.o_ad = self.o_wd + self.LP
        end = self.o_ad + self.LP
        self.tn = 1280 if D >= 2048 else LANES
        self.ncol = -(-end // self.tn) * self.tn
        self.in_cols = 4 * self.FW + self.FH + 4 * self.RW + 2 * lora + 2 * D
        self.scp = -(-(self.in_cols // N_CHIPS) // LANES) * LANES
        self.rseg = 4 * self.RW + 2 * self.LP
        self.C = min(SCAN_CHUNK, T)
        self.tr = min(256, T)
        self.hb = min(SCAN_HEADS_PER_STEP, self.RH)
        self.cb = SCAN_CHUNKS_PER_STEP if (T // self.C) % SCAN_CHUNKS_PER_STEP == 0 else 1

    def segments(self):
        FW, FH, RW, lo, D = self.FW, self.FH, self.RW, self.lora, self.D
        g_f = 4 * FW
        g_r = g_f + FH
        g_wd = g_r + 4 * RW
        g_ad = g_wd + lo
        g_g = g_ad + lo
        dh = FOX_HEAD_DIM
        qkv = [(j * FW + h * dh, dh, (3 * h + j) * dh) for h in range(FH) for j in range(3)]
        return qkv + [(3 * FW, FW, 3 * FW), (g_f, FH, self.o_f), (g_r, 4 * RW, self.o_rwkv), (g_wd, lo, self.o_wd),
                      (g_ad, lo, self.o_ad), (g_g, 2 * D, self.o_gate)]


def _shards_to_my_layout(cfg, g):
    R, sc = g.shape[1], g.shape[2]
    segs = sorted(cfg.segments(), key=lambda s: s[2])
    parts, pos = [], 0
    for g0, w, m0 in segs:
        if m0 > pos:
            parts.append(jnp.zeros((R, m0 - pos), g.dtype))
        for s in range(N_CHIPS):
            lo, hi = max(g0, s * sc), min(g0 + w, (s + 1) * sc)
            if lo < hi:
                parts.append(g[s, :, lo - s * sc:hi - s * sc])
        pos = m0 + w
    if cfg.ncol > pos:
        parts.append(jnp.zeros((R, cfg.ncol - pos), g.dtype))
    return jnp.concatenate(parts, axis=1)


def _my_layout_to_shards(cfg, wm):
    sc, R = cfg.in_cols // N_CHIPS, wm.shape[0]
    segs = sorted(cfg.segments(), key=lambda s: s[0])
    shards = []
    for s in range(N_CHIPS):
        parts = []
        for g0, w, m0 in segs:
            lo, hi = max(g0, s * sc), min(g0 + w, (s + 1) * sc)
            if lo < hi:
                parts.append(wm[:, m0 + lo - g0:m0 + hi - g0])
        parts.append(jnp.zeros((R, cfg.scp - sc), wm.dtype))
        shards.append(jnp.concatenate(parts, axis=1))
    return jnp.stack(shards, axis=0)


def _rwkv_vec_to_my(cfg, v):
    RW4, lo, LP = 4 * cfg.RW, cfg.lora, cfg.LP
    z = jnp.zeros((1, LP - lo), v.dtype)
    return jnp.concatenate([v[:, :RW4], v[:, RW4:RW4 + lo], z, v[:, RW4 + lo:], z], axis=1)


def _rwkv_vec_from_my(cfg, v):
    RW4, lo, LP = 4 * cfg.RW, cfg.lora, cfg.LP
    return jnp.concatenate([v[:, :RW4], v[:, RW4:RW4 + lo], v[:, RW4 + LP:RW4 + LP + lo]], axis=1)


def _comm_at(comm, which, steps, cin, cout, scr):
    if not comm or len(comm) <= which:
        return
    lin, total = 0, 1
    for d, n in enumerate(steps):
        lin = lin * n + pl.program_id(d)
        total *= n
    pl.when(lin == {3: 0, 4: total - 1, 5: total // 2}[which])(lambda: comm[which](cin, cout, scr))


def _hosted(body, n_in, n_out, steps, comm):
    if not comm:
        return body, [], [], [], []
    ci, co, cs = len(comm[0]), len(comm[1]), len(comm[2])

    def wrapped(*refs):
        ins, cin = refs[:n_in], refs[n_in:n_in + ci]
        outs, cout = refs[n_in + ci:n_in + ci + n_out], refs[n_in + ci + n_out:n_in + ci + n_out + co]
        cscr, scr = refs[n_in + ci + n_out + co:n_in + ci + n_out + co + cs], refs[n_in + ci + n_out + co + cs:]
        _comm_at(comm, 3, steps, cin, cout, cscr)
        body(*ins, *outs, *scr)
        _comm_at(comm, 5, steps, cin, cout, cscr)
        _comm_at(comm, 4, steps, cin, cout, cscr)

    return wrapped, [_ANY] * ci, [_ANY] * co, list(comm[1]), list(comm[2])


def _mm(name, a, b, dims, out_dtype, tm, tn, tk, comm=None):
    (M, K) = a.shape if dims != "tn" else a.shape[::-1]
    N = b.shape[0] if dims == "nt" else b.shape[1]
    tm, tn, tk = min(tm, M), min(tn, N), min(tk, K)
    assert M % tm == 0 and N % tn == 0 and K % tk == 0, (name, M, N, K, tm, tn, tk)
    nk = K // tk
    steps = (M // tm, N // tn, nk)
    c_in, c_out, c_scr = comm[:3] if comm else ([], [], [])
    if dims == "nn":
        a_spec = pl.BlockSpec((tm, tk), lambda i, j, k: (i, k))
        b_spec = pl.BlockSpec((tk, tn), lambda i, j, k: (k, j))
    elif dims == "nt":
        a_spec = pl.BlockSpec((tm, tk), lambda i, j, k: (i, k))
        b_spec = pl.BlockSpec((tn, tk), lambda i, j, k: (j, k))
    else:
        a_spec = pl.BlockSpec((tk, tm), lambda i, j, k: (k, i))
        b_spec = pl.BlockSpec((tk, tn), lambda i, j, k: (k, j))

    n_acc = 1 if nk > 1 else 0

    def body(a_ref, b_ref, *rest):
        cin, o_ref = rest[:len(c_in)], rest[len(c_in)]
        cout = rest[len(c_in) + 1:len(c_in) + 1 + len(c_out)]
        scr = rest[len(c_in) + 1 + len(c_out):]
        _comm_at(comm, 3, steps, cin, cout, scr[n_acc:])
        if nk == 1:
            o_ref[...] = _dot(a_ref[...], b_ref[...], dims).astype(o_ref.dtype)
        else:
            acc_ref, k = scr[0], pl.program_id(2)

            @pl.when(k == 0)
            def _():
                acc_ref[...] = jnp.zeros_like(acc_ref)

            acc_ref[...] += _dot(a_ref[...], b_ref[...], dims)

            @pl.when(k == nk - 1)
            def _():
                o_ref[...] = acc_ref[...].astype(o_ref.dtype)

        _comm_at(comm, 5, steps, cin, cout, scr[n_acc:])
        _comm_at(comm, 4, steps, cin, cout, scr[n_acc:])

    res = _pcall(
        body, name=name, grid=steps,
        in_specs=[a_spec, b_spec] + [_ANY] * len(c_in),
        out_specs=[pl.BlockSpec((tm, tn), lambda i, j, k: (i, j))] + [_ANY] * len(c_out),
        out_shape=[jax.ShapeDtypeStruct((M, N), out_dtype)] + list(c_out),
        scratch_shapes=([pltpu.VMEM((tm, tn), F32)] if nk > 1 else []) + list(c_scr),
        compiler_params=_cparams(("arbitrary",) * 3 if comm else ("parallel", "parallel", "arbitrary")),
    )(a, b, *c_in)
    return res if comm else res[0]


def _tile(tr, w, cb=0):
    return pl.BlockSpec((tr, w), lambda i: (i, cb))


def _const(shape):
    nd = len(shape)
    return pl.BlockSpec(shape, lambda i: (0,) * nd)


def _acc_store(i, ref, val):
    @pl.when(i == 0)
    def _():
        ref[...] = val

    @pl.when(i > 0)
    def _():
        ref[...] += val


def _rms_fwd(cfg, x2, g, comm=None):
    T, D, tr = cfg.T, cfg.D, cfg.tr
    steps = (T // tr,)

    def body(x_ref, g_ref, h_ref):
        h_ref[...] = _rmsn(x_ref[...], g_ref[...]).astype(BF16)

    body, c_in, c_out, c_shapes, c_scr = _hosted(body, 2, 1, steps, comm)
    res = _pcall(body, name="rms_fwd", grid=steps, in_specs=[_tile(tr, D), _const((1, D))] + c_in,
                 out_specs=[_tile(tr, D)] + c_out, out_shape=[jax.ShapeDtypeStruct((T, D), BF16)] + c_shapes,
                 scratch_shapes=c_scr, compiler_params=_cparams(("arbitrary",) if comm else ("parallel",)),
                 )(x2, g, *(comm[0] if comm else []))
    return res if comm else res[0]


def _rms_bwd(cfg, x2, g, dh, dres, comm=None):
    T, D, tr = cfg.T, cfg.D, cfg.tr
    c_in, c_out, c_scr = comm[:3] if comm else ([], [], [])
    steps = (T // tr,)

    def body(x_ref, g_ref, dh_ref, dres_ref, *rest):
        cin, (gx_ref, dg_ref) = rest[:len(c_in)], rest[len(c_in):len(c_in) + 2]
        cout, scr = rest[len(c_in) + 2:len(c_in) + 2 + len(c_out)], rest[len(c_in) + 2 + len(c_out):]
        _comm_at(comm, 3, steps, cin, cout, scr)
        _, vjp = jax.vjp(_rmsn, x_ref[...], g_ref[...])
        dx, dg = vjp(dh_ref[...])
        gx_ref[...] = dx + dres_ref[...]
        _acc_store(pl.program_id(0), dg_ref, dg)
        _comm_at(comm, 4, steps, cin, cout, scr)

    return _pcall(body, name="rms_bwd", grid=steps,
                  in_specs=[_tile(tr, D), _const((1, D)), _tile(tr, D), _tile(tr, D)] + [_ANY] * len(c_in),
                  out_specs=[_tile(tr, D), _const((1, D))] + [_ANY] * len(c_out),
                  out_shape=[jax.ShapeDtypeStruct((T, D), F32), jax.ShapeDtypeStruct((1, D), F32)] + list(c_out),
                  scratch_shapes=list(c_scr), compiler_params=_cparams(("arbitrary",)))(x2, g, dh, dres, *c_in)


def _final(cfg, x2, mo, fg, target):
    T, D, tr = cfg.T, cfg.D, cfg.tr

    def loss_fn(hres, g, tgt):
        err = _rmsn(hres, g) - tgt
        return 0.5 * jnp.sum(jnp.mean(err * err, axis=-1, keepdims=True), axis=0, keepdims=True)

    def body(x_ref, mo_ref, g_ref, t_ref, loss_ref, dres_ref, dres16_ref, dg_ref):
        hres = x_ref[...] + mo_ref[...]
        loss, vjp = jax.vjp(functools.partial(loss_fn, tgt=t_ref[...]), hres, g_ref[...])
        dres, dg = vjp(jnp.ones((1, 1), F32))
        dres_ref[...] = dres
        dres16_ref[...] = dres.astype(BF16)
        i = pl.program_id(0)
        _acc_store(i, dg_ref, dg)
        _acc_store(i, loss_ref, jnp.broadcast_to(loss, (8, LANES)))

    return _pcall(body, name="final_loss", grid=(T // tr,),
                  in_specs=[_tile(tr, D), _tile(tr, D), _const((1, D)), _tile(tr, D)],
                  out_specs=[_const((8, LANES)), _tile(tr, D), _tile(tr, D), _const((1, D))],
                  out_shape=[jax.ShapeDtypeStruct((8, LANES), F32), jax.ShapeDtypeStruct((T, D), F32),
                             jax.ShapeDtypeStruct((T, D), BF16), jax.ShapeDtypeStruct((1, D), F32)],
                  compiler_params=_cparams(("arbitrary",)))(x2, mo, fg, target)


def _merge_fn(pa, pb, ga, gb):
    return jax.nn.sigmoid(ga) * pa + jax.nn.sigmoid(gb) * pb


def _merge_fwd(cfg, pa, pb, u):
    T, D, tr = cfg.T, cfg.D, cfg.tr
    cga, cgb = cfg.o_gate // D, cfg.o_gate // D + 1

    def body(pa_ref, pb_ref, ga_ref, gb_ref, m_ref):
        m_ref[...] = _merge_fn(pa_ref[...], pb_ref[...], ga_ref[...], gb_ref[...]).astype(BF16)

    return _pcall(body, name="merge_fwd", grid=(T // tr,),
                  in_specs=[_tile(tr, D), _tile(tr, D), _tile(tr, D, cga), _tile(tr, D, cgb)],
                  out_specs=_tile(tr, D), out_shape=jax.ShapeDtypeStruct((T, D), BF16),
                  compiler_params=_cparams(("parallel",)))(pa, pb, u, u)


def _merge_bwd(cfg, pa, pb, u, dm):
    T, D, tr = cfg.T, cfg.D, cfg.tr
    cga, cgb = cfg.o_gate // D, cfg.o_gate // D + 1

    def body(pa_ref, pb_ref, ga_ref, gb_ref, dm_ref, dpa_ref, dpb_ref, dg_ref):
        _, vjp = jax.vjp(_merge_fn, pa_ref[...], pb_ref[...], ga_ref[...], gb_ref[...])
        dpa, dpb, dga, dgb = vjp(dm_ref[...])
        dpa_ref[...] = dpa.astype(BF16)
        dpb_ref[...] = dpb.astype(BF16)
        dg_ref[:, :D] = dga.astype(BF16)
        dg_ref[:, D:] = dgb.astype(BF16)

    return _pcall(body, name="merge_bwd", grid=(T // tr,),
                  in_specs=[_tile(tr, D), _tile(tr, D), _tile(tr, D, cga), _tile(tr, D, cgb), _tile(tr, D)],
                  out_specs=[_tile(tr, D), _tile(tr, D), _tile(tr, 2 * D, cfg.o_gate // (2 * D))],
                  out_shape=[jax.ShapeDtypeStruct((T, D), BF16), jax.ShapeDtypeStruct((T, D), BF16),
                             jax.ShapeDtypeStruct((T, cfg.ncol), BF16)],
                  compiler_params=_cparams(("parallel",)))(pa, pb, u, u, dm)


def _gate_fn(o, z):
    return o * _silu(z)


def _gate_a_fwd(cfg, o, u):
    T, FW, tr = cfg.T, cfg.FW, cfg.tr

    def body(o_ref, z_ref, oa_ref):
        oa_ref[...] = _gate_fn(o_ref[...], z_ref[...]).astype(BF16)

    return _pcall(body, name="gate_a_fwd", grid=(T // tr,), in_specs=[_tile(tr, FW), _tile(tr, FW, 3)],
                  out_specs=_tile(tr, FW), out_shape=jax.ShapeDtypeStruct((T, FW), BF16),
                  compiler_params=_cparams(("parallel",)))(o, u)


def _gate_a_bwd(cfg, o, u, doa, du):
    T, FW, tr = cfg.T, cfg.FW, cfg.tr

    def body(o_ref, z_ref, doa_ref, du_in, do_ref, dz_ref):
        _, vjp = jax.vjp(_gate_fn, o_ref[...], z_ref[...])
        do, dz = vjp(doa_ref[...])
        do_ref[...] = do
        dz_ref[...] = dz.astype(BF16)

    return _pcall(body, name="gate_a_bwd", grid=(T // tr,),
                  in_specs=[_tile(tr, FW), _tile(tr, FW, 3), _tile(tr, FW), _ANY],
                  out_specs=[_tile(tr, FW), _tile(tr, FW, 3)],
                  out_shape=[jax.ShapeDtypeStruct((T, FW), F32), jax.ShapeDtypeStruct(du.shape, BF16)],
                  input_output_aliases={3: 1},
                  compiler_params=_cparams(("parallel",)))(o, u, doa, du)


def _fox_prep(cfg, u, fb):
    T, tr = cfg.T, cfg.tr
    cf = cfg.o_f // LANES

    def body(f_ref, fb_ref, c_ref, carry_ref):
        i = pl.program_id(0)

        @pl.when(i == 0)
        def _():
            carry_ref[...] = jnp.zeros_like(carry_ref)

        lf = -_softplus(-(f_ref[...] + fb_ref[...]))
        r = lax.broadcasted_iota(jnp.int32, (tr, tr), 0)
        c = lax.broadcasted_iota(jnp.int32, (tr, tr), 1)
        tri = (r >= c).astype(F32)
        c_ref[...] = _dot(tri, lf, precision=HI) + carry_ref[...]
        carry_ref[...] += jnp.sum(lf, axis=0, keepdims=True)

    return _pcall(body, name="fox_prep", grid=(T // tr,), in_specs=[_tile(tr, LANES, cf), _const((1, LANES))],
                  out_specs=_tile(tr, LANES), out_shape=jax.ShapeDtypeStruct((T, LANES), F32),
                  scratch_shapes=[pltpu.VMEM((1, LANES), F32)], compiler_params=_cparams(("arbitrary",)))(u, fb)


def _fox_prep_bwd(cfg, u, fb, dc):
    T, tr = cfg.T, cfg.tr
    cf = cfg.o_f // LANES
    nb = T // tr

    def body(f_ref, fb_ref, dc_ref, df_ref, dfb_ref, carry_ref):
        i = pl.program_id(0)

        @pl.when(i == 0)
        def _():
            carry_ref[...] = jnp.zeros_like(carry_ref)

        dc = dc_ref[...]
        r = lax.broadcasted_iota(jnp.int32, (tr, tr), 0)
        c = lax.broadcasted_iota(jnp.int32, (tr, tr), 1)
        triu = (r <= c).astype(F32)
        dlf = _dot(triu, dc, precision=HI) + carry_ref[...]
        carry_ref[...] += jnp.sum(dc, axis=0, keepdims=True)
        dz = dlf * jax.nn.sigmoid(-(f_ref[...] + fb_ref[...]))
        df_ref[...] = dz.astype(BF16)
        _acc_store(i, dfb_ref, jnp.sum(dz, axis=0, keepdims=True))

    rev = lambda i: (nb - 1 - i, 0)
    return _pcall(body, name="fox_prep_bwd", grid=(nb,),
                  in_specs=[pl.BlockSpec((tr, LANES), lambda i: (nb - 1 - i, cf)), _const((1, LANES)),
                            pl.BlockSpec((tr, LANES), rev)],
                  out_specs=[pl.BlockSpec((tr, LANES), rev), _const((1, LANES))],
                  out_shape=[jax.ShapeDtypeStruct((T, LANES), BF16), jax.ShapeDtypeStruct((1, LANES), F32)],
                  scratch_shapes=[pltpu.VMEM((1, LANES), F32)], compiler_params=_cparams(("arbitrary",)))(u, fb, dc)


def _attn_logits(q_ref, k_ref, c_ref, tq, te):
    q = q_ref[...].astype(BF16)
    scale = FOX_HEAD_DIM ** -0.5
    part = lambda k0, k1: _dot(q, k_ref[k0:k1, :].astype(BF16), "nt") * scale - c_ref[0, :, k0:k1]
    row = lax.broadcasted_iota(jnp.int32, (tq, tq), 0)
    col = lax.broadcasted_iota(jnp.int32, (tq, tq), 1)
    own = ((te - tq, te), jnp.where(col <= row, part(te - tq, te), -1e30))
    return [((0, te - tq), part(0, te - tq)), own] if te > tq else [own]


def _per_query_tile(i, nq, tq, fn):
    for ii in range(nq):
        pl.when(i == ii)(functools.partial(fn, (ii + 1) * tq))


def _attn_fwd(cfg, u, c_rows):
    T, FW, FH = cfg.T, cfg.FW, cfg.FH
    tq = min(256, T)
    dh = FOX_HEAD_DIM

    def body(q_ref, k_ref, v_ref, c_ref, o_ref, lse_ref):
        i = pl.program_id(1)

        def tile(te):
            parts = _attn_logits(q_ref, k_ref, c_ref, tq, te)
            m = functools.reduce(jnp.maximum, [jnp.max(s, axis=1, keepdims=True) for _, s in parts])
            l, acc = 0.0, 0.0
            for (k0, k1), s in parts:
                p = jnp.exp(s - m)
                l = l + jnp.sum(p, axis=1, keepdims=True)
                acc = acc + _dot(p.astype(BF16), v_ref[k0:k1, :].astype(BF16))
            o_ref[...] = acc / l
            lse_ref[0] = m + jnp.log(l)

        _per_query_tile(i, T // tq, tq, tile)

    return _pcall(
        body, name="fox_attn_fwd", grid=(FH, T // tq),
        in_specs=[pl.BlockSpec((tq, dh), lambda h, i: (i, 3 * h)), pl.BlockSpec((T, dh), lambda h, i: (0, 3 * h + 1)),
                  pl.BlockSpec((T, dh), lambda h, i: (0, 3 * h + 2)), pl.BlockSpec((1, 1, T), lambda h, i: (h, 0, 0))],
        out_specs=[pl.BlockSpec((tq, dh), lambda h, i: (i, h)), pl.BlockSpec((1, tq, 1), lambda h, i: (h, i, 0))],
        out_shape=[jax.ShapeDtypeStruct((T, FW), F32), jax.ShapeDtypeStruct((FH, T, 1), F32)],
        compiler_params=_cparams(("parallel", "arbitrary")),
    )(u, u, u, c_rows)


def _attn_bwd(cfg, u, c_rows, lse, do, du):
    T, FW, FH = cfg.T, cfg.FW, cfg.FH
    tq = min(256, T)
    nq = T // tq
    dh = FOX_HEAD_DIM
    scale = dh ** -0.5

    def body(q_ref, k_ref, v_ref, c_ref, lse_ref, do_ref, du_in, du_ref, dcol_ref, dk_acc, dv_acc):
        i = pl.program_id(1)

        @pl.when(i == 0)
        def _():
            dk_acc[...] = jnp.zeros_like(dk_acc)
            dv_acc[...] = jnp.zeros_like(dv_acc)
            dcol_ref[...] = jnp.zeros_like(dcol_ref)

        def tile(te):
            lse, q16, do16 = lse_ref[0], q_ref[...].astype(BF16), do_ref[...].astype(BF16)
            probs = [(ks, jnp.exp(s - lse)) for ks, s in _attn_logits(q_ref, k_ref, c_ref, tq, te)]
            dps = [_dot(do16, v_ref[k0:k1, :].astype(BF16), "nt") for (k0, k1), _ in probs]
            delta = sum(jnp.sum(p * dp, axis=1, keepdims=True) for (_, p), dp in zip(probs, dps))
            dq = 0.0
            for ((k0, k1), p), dp in zip(probs, dps):
                ds = p * (dp - delta)
                ds16 = ds.astype(BF16)
                dq = dq + _dot(ds16, k_ref[k0:k1, :].astype(BF16))
                dk_acc[k0:k1, :] += _dot(ds16, q16, "tn") * scale
                dv_acc[k0:k1, :] += _dot(p.astype(BF16), do16, "tn")
                dcol_ref[0, :, k0:k1] += jnp.sum(ds, axis=0, keepdims=True)
            du_ref[te - tq:te, 0:dh] = (dq * scale).astype(BF16)

        _per_query_tile(i, nq, tq, tile)

        @pl.when(i == nq - 1)
        def _():
            du_ref[:, dh:2 * dh] = dk_acc[...].astype(BF16)
            du_ref[:, 2 * dh:3 * dh] = dv_acc[...].astype(BF16)

    return _pcall(
        body, name="fox_attn_bwd", grid=(FH, nq),
        in_specs=[pl.BlockSpec((tq, dh), lambda h, i: (i, 3 * h)), pl.BlockSpec((T, dh), lambda h, i: (0, 3 * h + 1)),
                  pl.BlockSpec((T, dh), lambda h, i: (0, 3 * h + 2)), pl.BlockSpec((1, 1, T), lambda h, i: (h, 0, 0)),
                  pl.BlockSpec((1, tq, 1), lambda h, i: (h, i, 0)), pl.BlockSpec((tq, dh), lambda h, i: (i, h)), _ANY],
        out_specs=[pl.BlockSpec((T, 3 * dh), lambda h, i: (0, h)), pl.BlockSpec((1, 1, T), lambda h, i: (h, 0, 0))],
        out_shape=[jax.ShapeDtypeStruct(du.shape, BF16), jax.ShapeDtypeStruct((FH, 1, T), F32)],
        scratch_shapes=[pltpu.VMEM((T, dh), F32), pltpu.VMEM((T, dh), F32)],
        input_output_aliases={6: 0},
        compiler_params=_cparams(("parallel", "arbitrary")),
    )(u, u, u, c_rows, lse, do, du)


def _head_indicators(cfg):
    ind = np.zeros((cfg.RW, LANES), np.float32)
    ind[np.arange(cfg.RW), np.arange(cfg.RW) // RWKV_HEAD_DIM] = 1.0
    pad = np.zeros((1, LANES), np.float32)
    pad[0, cfg.RH:] = 1.0
    return jnp.asarray(ind), jnp.asarray(ind.T.copy()), jnp.asarray(pad)


def _prep_fn(us_r, us_k, us_v, us_wd, us_ad, w0, w2p, a0, a2p, k_k, k_a, ind, ind_t, pad):
    wpre = w0 + _dot3(jnp.tanh(us_wd), w2p)
    w = -_softplus(-wpre) - 0.5
    lw = -jnp.exp(w)
    a = jax.nn.sigmoid(a0 + _dot3(us_ad, a2p))
    kk = us_k * k_k
    ss = _xdot(kk * kk, ind, ind_t) + pad
    inv = 1.0 / jnp.maximum(jnp.sqrt(ss), L2_EPS)
    kkn = kk * _xdot(inv, ind_t, ind)
    kp = us_k * (1.0 + (a - 1.0) * k_a)
    return us_r, lw, kp, us_v, -kkn, kkn * a


def _shifted(u, prev_row, mu, first):
    n = u.shape[0]
    rolled = pltpu.roll(u, 1, 0)
    row = lax.broadcasted_iota(jnp.int32, u.shape, 0)
    p0 = jnp.where(first, jnp.zeros_like(prev_row), prev_row)
    prev = jnp.where(row == 0, jnp.broadcast_to(p0, u.shape), rolled)
    return u + (prev - u) * mu, prev


def _rwkv_specs(cfg, tr):
    RW, LP = cfg.RW, cfg.LP
    base = cfg.o_rwkv // RW
    cols = [(RW, base), (RW, base + 1), (RW, base + 2), (RW, base + 3), (LP, cfg.o_wd // LP), (LP, cfg.o_ad // LP)]
    cur = [pl.BlockSpec((tr, w), (lambda i, cb=cb: (i, cb))) for w, cb in cols]
    prv = [pl.BlockSpec((8, w), (lambda i, cb=cb: (jnp.maximum(i * (tr // 8) - 1, 0), cb))) for w, cb in cols]
    return cols, cur, prv


def _mu_pieces(cfg, mu_ref):
    RW, LP = cfg.RW, cfg.LP
    offs = [0, RW, 2 * RW, 3 * RW, 4 * RW, 4 * RW + LP, 4 * RW + 2 * LP]
    return [mu_ref[:, offs[j]:offs[j + 1]] for j in range(6)]


def _rwkv_prep_fwd(cfg, u, mu, w0, w2p, a0, a2p, k_k, k_a):
    T, RW, LP, tr = cfg.T, cfg.RW, cfg.LP, cfg.tr
    ind, ind_t, pad = _head_indicators(cfg)
    cols, cur, prv = _rwkv_specs(cfg, tr)

    def body(*refs):
        u_refs, p_refs = refs[0:6], refs[6:12]
        mu_ref, w0_ref, w2_ref, a0_ref, a2_ref, kk_ref, ka_ref, ind_ref, indt_ref, pad_ref = refs[12:22]
        outs = refs[22:]
        first = pl.program_id(0) == 0
        mus = _mu_pieces(cfg, mu_ref)
        us = [_shifted(u_refs[j][...], p_refs[j][7:8, :], mus[j], first)[0] for j in range(6)]
        res = _prep_fn(us[0], us[1], us[2], us[4], us[5], w0_ref[...], w2_ref[...], a0_ref[...], a2_ref[...],
                       kk_ref[...], ka_ref[...], ind_ref[...], indt_ref[...], pad_ref[...])
        for j in range(6):
            outs[j][...] = res[j]
        outs[6][...] = us[3]

    consts = [mu, w0, w2p, a0, a2p, k_k, k_a, ind, ind_t, pad]
    return _pcall(body, name="rwkv_prep_fwd", grid=(T // tr,),
                  in_specs=cur + prv + [_const(c.shape) for c in consts],
                  out_specs=[_tile(tr, RW)] * 7, out_shape=[jax.ShapeDtypeStruct((T, RW), F32)] * 7,
                  compiler_params=_cparams(("parallel",)))(*([u] * 12), *consts)


def _rwkv_prep_bwd(cfg, u, mu, w0, w2p, a0, a2p, k_k, k_a, cots, dzb):
    T, RW, LP = cfg.T, cfg.RW, cfg.LP
    tr = min(128, T)
    ind, ind_t, pad = _head_indicators(cfg)
    cols, cur, prv = _rwkv_specs(cfg, tr)
    rseg = cfg.rseg

    def body(*refs):
        u_refs, p_refs = refs[0:6], refs[6:12]
        mu_ref, w0_ref, w2_ref, a0_ref, a2_ref, kk_ref, ka_ref, ind_ref, indt_ref, pad_ref = refs[12:22]
        cot_refs, dzb_ref = refs[22:28], refs[28]
        dus_ref, dmu_ref, dw0_ref, dw2_ref, da0_ref, da2_ref, dkk_ref, dka_ref = refs[29:]
        i = pl.program_id(0)
        first = i == 0
        mus = _mu_pieces(cfg, mu_ref)
        sh = [_shifted(u_refs[j][...], p_refs[j][7:8, :], mus[j], first) for j in range(6)]
        us = [s[0] for s in sh]
        fn = functools.partial(_prep_fn, ind=ind_ref[...], ind_t=indt_ref[...], pad=pad_ref[...])
        _, vjp = jax.vjp(fn, us[0], us[1], us[2], us[4], us[5], w0_ref[...], w2_ref[...], a0_ref[...], a2_ref[...],
                         kk_ref[...], ka_ref[...])
        d = vjp(tuple(c[...] for c in cot_refs))
        dus = [d[0], d[1], d[2], dzb_ref[...], d[3], d[4]]
        offs = [0, RW, 2 * RW, 3 * RW, 4 * RW, 4 * RW + LP, 4 * RW + 2 * LP]
        for j in range(6):
            dus_ref[:, offs[j]:offs[j + 1]] = dus[j]
            dmu_j = jnp.sum(dus[j] * (sh[j][1] - u_refs[j][...]), axis=0, keepdims=True)

            @pl.when(first)
            def _(j=j, dmu_j=dmu_j):
                dmu_ref[:, offs[j]:offs[j + 1]] = dmu_j

            @pl.when(i > 0)
            def _(j=j, dmu_j=dmu_j):
                dmu_ref[:, offs[j]:offs[j + 1]] += dmu_j
        for ref, val in zip((dw0_ref, dw2_ref, da0_ref, da2_ref, dkk_ref, dka_ref), d[5:11]):
            _acc_store(i, ref, val)

    consts = [mu, w0, w2p, a0, a2p, k_k, k_a, ind, ind_t, pad]
    vec = jax.ShapeDtypeStruct((1, RW), F32)
    mat = jax.ShapeDtypeStruct((LP, RW), F32)
    return _pcall(body, name="rwkv_prep_bwd", grid=(T // tr,),
                  in_specs=cur + prv + [_const(c.shape) for c in consts] + [_tile(tr, RW)] * 7,
                  out_specs=[_tile(tr, rseg), _const((1, rseg)), _const((1, RW)), _const((LP, RW)), _const((1, RW)),
                             _const((LP, RW)), _const((1, RW)), _const((1, RW))],
                  out_shape=[jax.ShapeDtypeStruct((T, rseg), F32), jax.ShapeDtypeStruct((1, rseg), F32),
                             vec, mat, vec, mat, vec, vec],
                  compiler_params=_cparams(("arbitrary",)))(*([u] * 12), *consts, *cots, dzb)


def _shift_bwd(cfg, dus, mu, df, du):
    T, tr, RW, LP = cfg.T, cfg.tr, cfg.RW, cfg.LP
    nb = T // tr
    tail = cfg.ncol - cfg.o_f
    assert cfg.o_rwkv % (4 * RW) == 0 and (4 * RW) % (2 * LP) == 0 and cfg.o_f % tail == 0

    def shifted(d_ref, n_ref, mu_ref):
        d = d_ref[...]
        rolled = pltpu.roll(d, tr - 1, 0)
        row = lax.broadcasted_iota(jnp.int32, d.shape, 0)
        n0 = jnp.where(pl.program_id(0) == nb - 1, jnp.zeros_like(n_ref[0:1, :]), n_ref[0:1, :])
        nxt = jnp.where(row == tr - 1, jnp.broadcast_to(n0, d.shape), rolled)
        mu_v = mu_ref[...]
        return (d * (1.0 - mu_v) + nxt * mu_v).astype(BF16)

    def main_body(d_ref, n_ref, mu_ref, du_in, du_ref):
        du_ref[...] = shifted(d_ref, n_ref, mu_ref)

    def tail_body(d_ref, n_ref, mu_ref, df_ref, du_in, du_ref):
        du_ref[:, 0:LANES] = df_ref[...]
        du_ref[:, LANES:LANES + 2 * LP] = shifted(d_ref, n_ref, mu_ref)
        if tail > LANES + 2 * LP:
            du_ref[:, LANES + 2 * LP:] = jnp.zeros((tr, tail - LANES - 2 * LP), BF16)

    def specs(w, cb):
        return [_tile(tr, w, cb),
                pl.BlockSpec((8, w), lambda i: (jnp.minimum((i + 1) * (tr // 8), T // 8 - 1), cb)),
                pl.BlockSpec((1, w), lambda i: (0, cb))]

    out = jax.ShapeDtypeStruct(du.shape, BF16)
    du = _pcall(main_body, name="shift_bwd_main", grid=(nb,), in_specs=specs(4 * RW, 0) + [_ANY],
                out_specs=_tile(tr, 4 * RW, cfg.o_rwkv // (4 * RW)), out_shape=out, input_output_aliases={3: 0},
                compiler_params=_cparams(("parallel",)))(dus, dus, mu, du)
    return _pcall(tail_body, name="shift_bwd_tail", grid=(nb,),
                  in_specs=specs(2 * LP, 4 * RW // (2 * LP)) + [_tile(tr, LANES), _ANY],
                  out_specs=_tile(tr, tail, cfg.o_f // tail), out_shape=out, input_output_aliases={4: 0},
                  compiler_params=_cparams(("parallel",)))(dus, dus, mu, df, du)


def _chunk_local(r, lw, k, v, a, b):
    H, C, K = r.shape
    row = lax.broadcasted_iota(jnp.int32, (C, C), 0)
    col = lax.broadcasted_iota(jnp.int32, (C, C), 1)
    incl = jnp.broadcast_to((row >= col).astype(F32)[None], (H, C, C))
    strict = (row > col)[None]
    lower = (row >= col)[None]
    eye = (row == col)[None]
    zero = jnp.zeros((), F32)
    L = _bdot(incl, lw, 2, 1)
    LC = jnp.sum(lw, axis=1, keepdims=True)
    eL = jnp.exp(L)
    eLn = jnp.exp(-L)
    at = a * jnp.exp(L - lw)
    rt = r * eL
    bt = b * eLn
    kt = k * eLn
    eR = jnp.exp(LC - L)
    bh = b * eR
    kh = k * eR
    keys = functools.partial(_bdot, passes=SCAN_PASSES[0])
    inv = functools.partial(_bdot, passes=SCAN_PASSES[1])
    app = functools.partial(_bdot, passes=SCAN_PASSES[2])
    ar = jnp.concatenate([at, rt], axis=1)
    g_b = app(ar, bt, 2, 2)
    g_k = keys(ar, kt, 2, 2)
    n_ab = jnp.where(strict, g_b[:, :C], zero)
    n_ak = jnp.where(strict, g_k[:, :C], zero)
    m_rb = jnp.where(lower, g_b[:, C:], zero)
    m_rk = jnp.where(lower, g_k[:, C:], zero)
    P = jnp.where(eye, 1.0, zero) + n_ab
    squarings = max(1, int(np.ceil(np.log2(C)))) - 1
    if squarings:
        M = inv(n_ab, n_ab, 2, 1)
        for _ in range(squarings - 1):
            PM = inv(M, jnp.concatenate([P, M], axis=2), 2, 1)
            P, M = P + PM[:, :, :C], PM[:, :, C:]
        P = P + inv(M, P, 2, 1)
    W = app(P, at, 2, 1)
    Uloc = app(P, app(n_ak, v, 2, 1), 2, 1)
    Q = rt + app(m_rb, W, 2, 1)
    Yloc = app(m_rb, Uloc, 2, 1) + app(m_rk, v, 2, 1)
    A = jnp.where(eye, jnp.exp(LC), zero) + app(W, bh, 1, 1)
    Sloc = app(Uloc, bh, 1, 1) + app(v, kh, 1, 1)
    return Q, Yloc, A, Sloc


def _split_heads(ref, n):
    N = RWKV_HEAD_DIM
    return jnp.stack([ref[:, h * N:(h + 1) * N] for h in range(n)], axis=0)


def _merge_heads(x):
    return jnp.concatenate([x[h] for h in range(x.shape[0])], axis=1)


def _chains(x, cb):
    hb = x.shape[0]
    return x.reshape(hb, cb, -1, x.shape[-1]).reshape(hb * cb, -1, x.shape[-1])


def _unchains(x, cb, seq):
    hb = x.shape[0] // cb
    x = x.reshape(hb, cb, x.shape[1], x.shape[2])
    return x.reshape(hb, cb * x.shape[2], x.shape[3]) if seq else x


def _scan_local_specs(cfg):
    N, HB, CB = RWKV_HEAD_DIM, cfg.hb, cfg.cb
    grid = (cfg.RH // HB, cfg.T // (CB * cfg.C))
    seq = pl.BlockSpec((HB, CB * cfg.C, N), lambda h, j: (h, j, 0))
    mat = pl.BlockSpec((HB, CB, N, N), lambda h, j: (h, j, 0, 0))
    return grid, seq, mat


def _scan_local_fwd(cfg, seqs):
    T, RH, N = cfg.T, cfg.RH, RWKV_HEAD_DIM
    grid, seq, mat = _scan_local_specs(cfg)

    def body(r_ref, lw_ref, k_ref, v_ref, a_ref, b_ref, q_ref, yl_ref, a_out, sl_ref):
        ins = [_chains(_split_heads(ref, cfg.hb), cfg.cb) for ref in (r_ref, lw_ref, k_ref, v_ref, a_ref, b_ref)]
        Q, Yloc, A, Sloc = _chunk_local(*ins)
        q_ref[...] = _unchains(Q, cfg.cb, True)
        yl_ref[...] = _unchains(Yloc, cfg.cb, True)
        a_out[...] = _unchains(A, cfg.cb, False)
        sl_ref[...] = _unchains(Sloc, cfg.cb, False)

    tok = pl.BlockSpec((cfg.cb * cfg.C, cfg.hb * N), lambda h, j: (j, h))
    sq = jax.ShapeDtypeStruct((RH, T, N), F32)
    mt = jax.ShapeDtypeStruct((RH, T // cfg.C, N, N), F32)
    return _pcall(body, name="rwkv_scan_local_fwd", grid=grid, in_specs=[tok] * 6, out_specs=[seq, seq, mat, mat],
                  out_shape=[sq, sq, mt, mt], compiler_params=_cparams(("parallel", "parallel")))(*seqs)


def _scan_local_bwd(cfg, toks, dq, dy, da, dsl, extra, comm=None):
    T, RW, N = cfg.T, cfg.RW, RWKV_HEAD_DIM
    grid, seq, mat = _scan_local_specs(cfg)
    c_in, c_out, c_scr = comm[:3] if comm else ([], [], [])

    def body(r_ref, lw_ref, k_ref, v_ref, a_ref, b_ref, dq_ref, dy_ref, da_ref, dsl_ref, xr_ref, xk_ref, xv_ref,
             *rest):
        cin, outs = rest[:len(c_in)], rest[len(c_in):len(c_in) + 6]
        cout, scr = rest[len(c_in) + 6:len(c_in) + 6 + len(c_out)], rest[len(c_in) + 6 + len(c_out):]
        _comm_at(comm, 3, grid, cin, cout, scr)
        ins = [_chains(_split_heads(ref, cfg.hb), cfg.cb) for ref in (r_ref, lw_ref, k_ref, v_ref, a_ref, b_ref)]
        _, vjp = jax.vjp(_chunk_local, *ins)
        d = vjp((_chains(dq_ref[...], cfg.cb), _chains(_split_heads(dy_ref, cfg.hb), cfg.cb),
                 _chains(da_ref[...], cfg.cb), _chains(dsl_ref[...], cfg.cb)))
        add = {0: xr_ref, 2: xk_ref, 3: xv_ref}
        for j in range(6):
            dj = _merge_heads(_unchains(d[j], cfg.cb, True))
            outs[j][...] = dj + add[j][...] if j in add else dj
        _comm_at(comm, 4, grid, cin, cout, scr)

    tok = pl.BlockSpec((cfg.cb * cfg.C, cfg.hb * N), lambda h, j: (j, h))
    return _pcall(body, name="rwkv_scan_local_bwd", grid=grid,
                  in_specs=[tok] * 6 + [seq, tok, mat, mat] + [tok] * 3 + [_ANY] * len(c_in),
                  out_specs=[tok] * 6 + [_ANY] * len(c_out),
                  out_shape=[jax.ShapeDtypeStruct((T, RW), F32)] * 6 + list(c_out), scratch_shapes=list(c_scr),
                  compiler_params=_cparams(("arbitrary", "arbitrary") if comm else ("parallel", "parallel")),
                  )(*toks, dq, dy, da, dsl, *extra, *c_in)


def _scan_carry_specs(cfg, rev):
    N, RH, C, nc = RWKV_HEAD_DIM, cfg.RH, cfg.C, cfg.T // cfg.C
    at = (lambda j: nc - 1 - j) if rev else (lambda j: j)
    seq = pl.BlockSpec((RH, C, N), lambda j: (0, at(j), 0))
    mat = pl.BlockSpec((RH, 1, N, N), lambda j: (0, at(j), 0, 0))
    return nc, seq, mat


def _scan_carry_fwd(cfg, q, yloc, a, sloc):
    T, RH, N = cfg.T, cfg.RH, RWKV_HEAD_DIM
    nc, seq, mat = _scan_carry_specs(cfg, False)

    def body(q_ref, yl_ref, a_ref, sl_ref, y_ref, ck_ref, s_ref):
        @pl.when(pl.program_id(0) == 0)
        def _():
            s_ref[...] = jnp.zeros_like(s_ref)

        S = s_ref[...]
        ck_ref[:, 0] = S
        y_ref[...] = _merge_heads(_bdot(q_ref[...], S, 2, 2, SCAN_PASSES[2]) + yl_ref[...])
        s_ref[...] = _bdot(S, a_ref[:, 0], 2, 1) + sl_ref[:, 0]

    tok = pl.BlockSpec((cfg.C, cfg.RW), lambda j: (j, 0))
    return _pcall(body, name="rwkv_scan_carry_fwd", grid=(nc,), in_specs=[seq, seq, mat, mat], out_specs=[tok, mat],
                  out_shape=[jax.ShapeDtypeStruct((T, cfg.RW), F32), jax.ShapeDtypeStruct((RH, nc, N, N), F32)],
                  scratch_shapes=[pltpu.VMEM((RH, N, N), F32)],
                  compiler_params=_cparams(("arbitrary",)))(q, yloc, a, sloc)


def _scan_carry_bwd(cfg, q, a, ckpt, dy, comm=None):
    T, RH, N = cfg.T, cfg.RH, RWKV_HEAD_DIM
    nc, seq, mat = _scan_carry_specs(cfg, True)

    def body(q_ref, a_ref, ck_ref, dy_ref, dq_ref, da_ref, dsl_ref, ds_ref):
        @pl.when(pl.program_id(0) == 0)
        def _():
            ds_ref[...] = jnp.zeros_like(ds_ref)

        S, dS, dY = ck_ref[:, 0], ds_ref[...], _split_heads(dy_ref, RH)
        dq_ref[...] = _bdot(dY, S, 2, 1, SCAN_PASSES[2])
        da_ref[:, 0] = _bdot(S, dS, 1, 1, SCAN_PASSES[2])
        dsl_ref[:, 0] = dS
        ds_ref[...] = _bdot(dS, a_ref[:, 0], 2, 2) + _bdot(dY, q_ref[...], 1, 1, SCAN_PASSES[2])

    mt = jax.ShapeDtypeStruct((RH, nc, N, N), F32)
    tok = pl.BlockSpec((cfg.C, cfg.RW), lambda j: (nc - 1 - j, 0))
    body, c_in, c_out, c_shapes, c_scr = _hosted(body, 4, 3, (nc,), comm)
    return _pcall(body, name="rwkv_scan_carry_bwd", grid=(nc,), in_specs=[seq, mat, mat, tok] + c_in,
                  out_specs=[seq, mat, mat] + c_out,
                  out_shape=[jax.ShapeDtypeStruct((RH, T, N), F32), mt, mt] + c_shapes,
                  scratch_shapes=c_scr + [pltpu.VMEM((RH, N, N), F32)],
                  compiler_params=_cparams(("arbitrary",)))(q, a, ckpt, dy, *(comm[0] if comm else []))


def _post_fn(y, r, kp, v, zb, ln_w, ln_b, rk, ind, ind_t):
    n = float(RWKV_HEAD_DIM)
    mu = _xdot(_xdot(y, ind, ind_t) / n, ind_t, ind)
    yc = y - mu
    var = _xdot(yc * yc, ind, ind_t) / n
    rstd = _xdot(lax.rsqrt(var + GN_EPS), ind_t, ind)
    yn = yc * rstd * ln_w + ln_b
    bonus = _xdot(_xdot(r * kp * rk, ind, ind_t), ind_t, ind) * v
    return (yn + bonus) * _silu(zb)


def _rwkv_post_fwd(cfg, y, r, kp, v, zb, ln_w, ln_b, rk):
    T, RW, tr = cfg.T, cfg.RW, cfg.tr
    ind, ind_t, _ = _head_indicators(cfg)

    def body(y_ref, r_ref, k_ref, v_ref, z_ref, lw_ref, lb_ref, rk_ref, ind_ref, indt_ref, ob_ref):
        ob_ref[...] = _post_fn(y_ref[...], r_ref[...], k_ref[...], v_ref[...], z_ref[...], lw_ref[...], lb_ref[...],
                               rk_ref[...], ind_ref[...], indt_ref[...]).astype(BF16)

    consts = [ln_w, ln_b, rk, ind, ind_t]
    return _pcall(body, name="rwkv_post_fwd", grid=(T // tr,),
                  in_specs=[_tile(tr, RW)] * 5 + [_const(c.shape) for c in consts],
                  out_specs=_tile(tr, RW), out_shape=jax.ShapeDtypeStruct((T, RW), BF16),
                  compiler_params=_cparams(("parallel",)))(y, r, kp, v, zb, *consts)


def _rwkv_post_bwd(cfg, y, r, kp, v, zb, ln_w, ln_b, rk, dob):
    T, RW = cfg.T, cfg.RW
    tr = min(128, T)
    ind, ind_t, _ = _head_indicators(cfg)

    def body(y_ref, r_ref, k_ref, v_ref, z_ref, lw_ref, lb_ref, rk_ref, ind_ref, indt_ref, dob_ref,
             dy_ref, dr_ref, dk_ref, dv_ref, dz_ref, dlw_ref, dlb_ref, drk_ref):
        fn = functools.partial(_post_fn, ind=ind_ref[...], ind_t=indt_ref[...])
        _, vjp = jax.vjp(fn, y_ref[...], r_ref[...], k_ref[...], v_ref[...], z_ref[...], lw_ref[...], lb_ref[...],
                         rk_ref[...])
        d = vjp(dob_ref[...])
        for ref, val in zip((dy_ref, dr_ref, dk_ref, dv_ref, dz_ref), d[:5]):
            ref[...] = val
        i = pl.program_id(0)
        for ref, val in zip((dlw_ref, dlb_ref, drk_ref), d[5:8]):
            _acc_store(i, ref, val)

    consts = [ln_w, ln_b, rk, ind, ind_t]
    vec = jax.ShapeDtypeStruct((1, RW), F32)
    return _pcall(body, name="rwkv_post_bwd", grid=(T // tr,),
                  in_specs=[_tile(tr, RW)] * 5 + [_const(c.shape) for c in consts] + [_tile(tr, RW)],
                  out_specs=[_tile(tr, RW)] * 5 + [_const((1, RW))] * 3,
                  out_shape=[jax.ShapeDtypeStruct((T, RW), F32)] * 5 + [vec] * 3,
                  compiler_params=_cparams(("arbitrary",)))(y, r, kp, v, zb, *consts, dob)


def _adamw_math(w, g, m, v):
    m = ADAM_B1 * m + (1.0 - ADAM_B1) * g
    v = ADAM_B2 * v + (1.0 - ADAM_B2) * (g * g)
    m_hat = m / (1.0 - ADAM_B1 ** ADAM_STEP)
    v_hat = v / (1.0 - ADAM_B2 ** ADAM_STEP)
    delta = -ADAM_LR * (m_hat / (jnp.sqrt(v_hat) + ADAM_EPS) + ADAM_WD * w)
    return delta, m, v


def _adamw(name, w, g, m, v, copy_grad=False, comm=None):
    R, Cc = w.shape
    Rp = -(-R // 8) * 8
    tr = Rp
    for nb in range(1, Rp // 8 + 1):
        if (Rp // 8) % nb == 0 and (Rp // nb) * Cc * 4 <= 2 * 1024 * 1024:
            tr = Rp // nb
            break

    def body(w_ref, g_ref, m_ref, v_ref, d_ref, nm_ref, nv_ref, *g_out):
        g_v = g_ref[...]
        d, nm, nv = _adamw_math(w_ref[...], g_v, m_ref[...], v_ref[...])
        d_ref[...] = d
        nm_ref[...] = nm
        nv_ref[...] = nv
        if copy_grad:
            g_out[0][...] = g_v

    spec = _tile(tr, Cc)
    n_out = 4 if copy_grad else 3
    body, c_in, c_out, c_shapes, c_scr = _hosted(body, 4, n_out, (Rp // tr,), comm)
    return _pcall(body, name=name, grid=(Rp // tr,), in_specs=[spec] * 4 + c_in, out_specs=[spec] * n_out + c_out,
                  out_shape=[jax.ShapeDtypeStruct((R, Cc), F32)] * n_out + c_shapes, scratch_shapes=c_scr,
                  compiler_params=_cparams(("arbitrary",) if comm else ("parallel",)),
                  )(w, g, m, v, *(comm[0] if comm else []))


def _row_tile(R, Cc, itemsize, budget=2 * 1024 * 1024):
    for nb in range(1, R // 16 + 1):
        if R % nb == 0 and (R // nb) % 16 == 0 and (R // nb) * Cc * itemsize <= budget:
            return R // nb
    return R


def _add_halves(name, gs, r1, c_idx):
    S, R, Cc = gs.shape
    half = R // 2
    tr = _row_tile(half, Cc, 4)
    nb = half // tr

    def body(c_ref, g_ref, r_ref, o_ref):
        o_ref[...] = (g_ref[...].astype(F32) + r_ref[...].astype(F32)).astype(BF16)

    grid_spec = pltpu.PrefetchScalarGridSpec(
        num_scalar_prefetch=1, grid=(S, nb),
        in_specs=[pl.BlockSpec((1, tr, Cc), lambda s, i, c: (s, c[0] * nb + i, 0)),
                  pl.BlockSpec((1, tr, Cc), lambda s, i, c: (s, i, 0))],
        out_specs=pl.BlockSpec((1, tr, Cc), lambda s, i, c: (s, i, 0)))
    return _pcall(body, name=name, grid_spec=grid_spec, out_shape=jax.ShapeDtypeStruct((S, half, Cc), BF16),
                  compiler_params=_cparams(("parallel", "parallel")))(c_idx, gs, r1)


def _sum_slots(name, r2):
    S, R, Cc = r2.shape
    tr = _row_tile(R, Cc, 4 * S // 2 if r2.dtype == BF16 else 4 * S)

    def body(r_ref, o_ref):
        acc = r_ref[0].astype(F32)
        for s in range(1, S):
            acc = acc + r_ref[s].astype(F32)
        o_ref[...] = acc

    return _pcall(body, name=name, grid=(R // tr,), in_specs=[pl.BlockSpec((S, tr, Cc), lambda i: (0, i, 0))],
                  out_specs=_tile(tr, Cc), out_shape=jax.ShapeDtypeStruct((R, Cc), F32),
                  compiler_params=_cparams(("parallel",)))(r2)


def _sum_chips(name, recv, own, place):
    S, H, Cc = recv.shape
    tr = _row_tile(H, Cc, 4, 1024 * 1024)
    nb = H // tr

    def body(p_ref, r_ref, own_ref, o_ref):
        s = pl.program_id(1)
        me = p_ref[0]

        @pl.when(s == 0)
        def _():
            o_ref[...] = jnp.zeros_like(o_ref)

        @pl.when(s == me)
        def _():
            o_ref[...] += own_ref[0].astype(F32)

        @pl.when(s != me)
        def _():
            o_ref[...] += r_ref[0].astype(F32)

    grid_spec = pltpu.PrefetchScalarGridSpec(
        num_scalar_prefetch=1, grid=(nb, S),
        in_specs=[pl.BlockSpec((1, tr, Cc), lambda i, s, p: (jnp.where(s == p[0], (s + 1) % S, s), i, 0)),
                  pl.BlockSpec((1, tr, Cc), lambda i, s, p: (p[0], i, 0))],
        out_specs=pl.BlockSpec((tr, Cc), lambda i, s, p: (p[1] * nb + i, 0)))
    return _pcall(body, name=name, grid_spec=grid_spec, out_shape=jax.ShapeDtypeStruct((2 * H, Cc), F32),
                  compiler_params=_cparams(("parallel", "arbitrary")))(place, recv, own)


def _cast_bf16(name, w):
    R, Cc = w.shape
    tr = _row_tile(R, Cc, 4)

    def body(w_ref, o_ref):
        o_ref[...] = w_ref[...].astype(BF16)

    return _pcall(body, name=name, grid=(R // tr,), in_specs=[_tile(tr, Cc)], out_specs=_tile(tr, Cc),
                  out_shape=jax.ShapeDtypeStruct((R, Cc), BF16), compiler_params=_cparams(("parallel",)))(w)


_ANY = pl.BlockSpec(memory_space=pl.ANY)


def _place():
    x, y, c = lax.axis_index("x"), lax.axis_index("y"), lax.axis_index("c")
    others = [(1 - x, y), (x, 1 - y), (1 - x, 1 - y)]
    return x, y, c, others


def _gather_parts(shards):
    n = len(shards)
    halves = [s.shape[0] // 2 for s in shards]

    def parts(ins, outs, sems):
        x, y, c, _ = _place()
        me = 2 * x + y
        n1 = (x ^ (1 - c), y ^ c)
        n2 = (x ^ c, y ^ (1 - c))
        s1, s2, sd = 2 * n1[0] + n1[1], 2 * n2[0] + n2[1], 2 * (1 - x) + (1 - y)
        sib = (x, y, 1 - c)

        def rows(k, chip, hc):
            return outs[k].at[chip, pl.ds(hc * halves[k], halves[k]), :]

        def remote(k, j, src, dst, to):
            return pltpu.make_async_remote_copy(src_ref=src, dst_ref=dst, send_sem=sems[0].at[7 * k + j],
                                                recv_sem=sems[1].at[7 * k + j], device_id=to, device_id_type=MESH)

        def copy(k, j):
            if j == 6:
                return remote(k, j, ins[k], outs[k].at[me], sib)
            if j < 2:
                mine = ins[k].at[pl.ds(c * halves[k], halves[k]), :]
                return remote(k, j, mine, rows(k, me, c), (*(n1 if j == 0 else n2), c))
            land = rows(k, {2: s1, 3: s1, 4: s2, 5: sd}[j], c)
            return remote(k, j, land, land, (*n2, c) if j == 2 else sib)

        def arrived(k, j):
            hc = c if j < 3 else 1 - c
            land = outs[k].at[me] if j == 6 else rows(k, {0: s1, 1: s2, 2: sd, 3: s2, 4: s1, 5: sd}[j], hc)
            remote(k, j, land, land, (x, y, c)).wait_recv()

        return copy, arrived

    def start(ins, outs, sems):
        copy, _ = parts(ins, outs, sems)
        for k in range(n):
            copy(k, 0).start()
            copy(k, 1).start()
            copy(k, 6).start()

    def middle(ins, outs, sems):
        copy, arrived = parts(ins, outs, sems)
        for k in range(n):
            arrived(k, 0)
            copy(k, 2).start()
            copy(k, 3).start()
            arrived(k, 1)
            copy(k, 4).start()

    def finish(ins, outs, sems):
        copy, arrived = parts(ins, outs, sems)
        for k in range(n):
            arrived(k, 2)
            copy(k, 5).start()
        for k in range(n):
            for j in (3, 4, 5, 6):
                arrived(k, j)
        for k in range(n):
            for j in range(7):
                copy(k, j).wait_send()

    out_shapes = [jax.ShapeDtypeStruct((N_CHIPS,) + s.shape, s.dtype) for s in shards]
    scratch = [pltpu.SemaphoreType.DMA((7 * n,)), pltpu.SemaphoreType.DMA((7 * n,))]
    return list(shards), out_shapes, scratch, start, finish, middle


def _swap_halves(grads):
    n = len(grads)
    halves = [g.shape[1] // 2 for g in grads]

    def copies(ins, outs, sems):
        x, y, c, _ = _place()
        return [pltpu.make_async_remote_copy(
            src_ref=ins[k].at[:, pl.ds((1 - c) * halves[k], halves[k]), :], dst_ref=outs[k], send_sem=sems[0].at[k],
            recv_sem=sems[1].at[k], device_id=(x, y, 1 - c), device_id_type=MESH) for k in range(n)]

    def start(ins, outs, sems):
        for cp in copies(ins, outs, sems):
            cp.start()

    def finish(ins, outs, sems):
        for cp in copies(ins, outs, sems):
            cp.wait()

    out_shapes = [jax.ShapeDtypeStruct((g.shape[0], h) + g.shape[2:], g.dtype) for g, h in zip(grads, halves)]
    scratch = [pltpu.SemaphoreType.DMA((n,)), pltpu.SemaphoreType.DMA((n,))]
    return list(grads), out_shapes, scratch, start, finish


def _scatter_to_owners(chip_sums):
    n = len(chip_sums)

    def sends(ins, outs, sems):
        x, y, c, others = _place()
        me = 2 * x + y
        return [pltpu.make_async_remote_copy(
            src_ref=ins[k].at[2 * px + py], dst_ref=outs[k].at[me], send_sem=sems[0].at[3 * k + j],
            recv_sem=sems[1].at[3 * k + j], device_id=(px, py, c), device_id_type=MESH)
            for k in range(n) for j, (px, py) in enumerate(others)]

    def start(ins, outs, sems):
        for cp in sends(ins, outs, sems):
            cp.start()

    def finish(ins, outs, sems):
        x, y, c, others = _place()
        for k in range(n):
            for j, (px, py) in enumerate(others):
                land = outs[k].at[2 * px + py]
                pltpu.make_async_remote_copy(src_ref=land, dst_ref=land, send_sem=sems[0].at[3 * k + j],
                                             recv_sem=sems[1].at[3 * k + j], device_id=(x, y, c),
                                             device_id_type=MESH).wait_recv()
        for cp in sends(ins, outs, sems):
            cp.wait_send()

    out_shapes = [jax.ShapeDtypeStruct(g.shape, g.dtype) for g in chip_sums]
    scratch = [pltpu.SemaphoreType.DMA((3 * n,)), pltpu.SemaphoreType.DMA((3 * n,))]
    return list(chip_sums), out_shapes, scratch, start, finish


def _swap_with_sibling(arrays):
    n = len(arrays)

    def copies(ins, outs, sems):
        x, y, c, _ = _place()
        return [pltpu.make_async_remote_copy(src_ref=ins[k], dst_ref=outs[k], send_sem=sems[0].at[k],
                                             recv_sem=sems[1].at[k], device_id=(x, y, 1 - c), device_id_type=MESH)
                for k in range(n)]

    def start(ins, outs, sems):
        for cp in copies(ins, outs, sems):
            cp.start()

    def finish(ins, outs, sems):
        for cp in copies(ins, outs, sems):
            cp.wait()

    out_shapes = [jax.ShapeDtypeStruct(a.shape, a.dtype) for a in arrays]
    scratch = [pltpu.SemaphoreType.DMA((n,)), pltpu.SemaphoreType.DMA((n,))]
    return list(arrays), out_shapes, scratch, start, finish


def _add_pair(name, a, b):
    R, Cc = a.shape
    tr = _row_tile(R, Cc, 4)

    def body(a_ref, b_ref, o_ref):
        o_ref[...] = (a_ref[...].astype(F32) + b_ref[...].astype(F32)).astype(BF16)

    return _pcall(body, name=name, grid=(R // tr,), in_specs=[_tile(tr, Cc)] * 2, out_specs=_tile(tr, Cc),
                  out_shape=jax.ShapeDtypeStruct((R, Cc), BF16), compiler_params=_cparams(("parallel",)))(a, b)


def _second_neighbour():
    x, y, c, _ = _place()
    return (x, y, c), (x ^ c, y ^ (1 - c)), (x ^ (1 - c), y ^ c)


def _scatter_stage1(chip_sums):
    n = len(chip_sums)

    def copies(ins, outs, sems):
        (x, y, c), n2, n1 = _second_neighbour()
        diag = 2 * (1 - x) + (1 - y)
        return [pltpu.make_async_remote_copy(
            src_ref=ins[k].at[slot], dst_ref=outs[2 * k + j], send_sem=sems[0].at[2 * k + j],
            recv_sem=sems[1].at[2 * k + j], device_id=(*n2, c), device_id_type=MESH)
            for k in range(n) for j, slot in enumerate((2 * n2[0] + n2[1], diag))]

    def start(ins, outs, sems):
        for cp in copies(ins, outs, sems):
            cp.start()

    def finish(ins, outs, sems):
        for cp in copies(ins, outs, sems):
            cp.wait()

    out_shapes = [jax.ShapeDtypeStruct(g.shape[1:], g.dtype) for g in chip_sums for _ in range(2)]
    scratch = [pltpu.SemaphoreType.DMA((2 * n,)), pltpu.SemaphoreType.DMA((2 * n,))]
    return list(chip_sums), out_shapes, scratch, start, finish


def _scatter_stage2(passed):
    n = len(passed)

    def copies(ins, outs, sems):
        (x, y, c), n2, n1 = _second_neighbour()
        return [pltpu.make_async_remote_copy(src_ref=ins[k], dst_ref=outs[k], send_sem=sems[0].at[k],
                                             recv_sem=sems[1].at[k], device_id=(*n1, c), device_id_type=MESH)
                for k in range(n)]

    def start(ins, outs, sems):
        for cp in copies(ins, outs, sems):
            cp.start()

    def finish(ins, outs, sems):
        for cp in copies(ins, outs, sems):
            cp.wait()

    out_shapes = [jax.ShapeDtypeStruct(p.shape, p.dtype) for p in passed]
    scratch = [pltpu.SemaphoreType.DMA((n,)), pltpu.SemaphoreType.DMA((n,))]
    return list(passed), out_shapes, scratch, start, finish


def _add_passed(name, own, got, slot):
    _, H, Cc = own.shape
    tr = _row_tile(H, Cc, 4)

    def body(s_ref, o_ref, g_ref, out_ref):
        out_ref[...] = (o_ref[0].astype(F32) + g_ref[...].astype(F32)).astype(BF16)

    grid_spec = pltpu.PrefetchScalarGridSpec(
        num_scalar_prefetch=1, grid=(H // tr,),
        in_specs=[pl.BlockSpec((1, tr, Cc), lambda i, s: (s[0], i, 0)), pl.BlockSpec((tr, Cc), lambda i, s: (i, 0))],
        out_specs=pl.BlockSpec((tr, Cc), lambda i, s: (i, 0)))
    return _pcall(body, name=name, grid_spec=grid_spec, out_shape=jax.ShapeDtypeStruct((H, Cc), BF16),
                  compiler_params=_cparams(("parallel",)))(slot, own, got)


def _sum_stages(name, own, direct, via, place, transposed=False):
    _, H, Cc = own.shape
    tr = LANES if transposed else _row_tile(H, Cc, 4, 1024 * 1024)
    nb = H // tr

    def body(p_ref, own_ref, d_ref, v_ref, o_ref):
        acc = (own_ref[0].astype(F32) + d_ref[...].astype(F32)) + v_ref[...].astype(F32)
        o_ref[...] = acc.T if transposed else acc

    flat = pl.BlockSpec((tr, Cc), lambda i, p: (i, 0))
    out_spec = (pl.BlockSpec((Cc, tr), lambda i, p: (0, p[1] * nb + i)) if transposed
                else pl.BlockSpec((tr, Cc), lambda i, p: (p[1] * nb + i, 0)))
    grid_spec = pltpu.PrefetchScalarGridSpec(
        num_scalar_prefetch=1, grid=(nb,),
        in_specs=[pl.BlockSpec((1, tr, Cc), lambda i, p: (p[0], i, 0)), flat, flat], out_specs=out_spec)
    return _pcall(body, name=name, grid_spec=grid_spec,
                  out_shape=jax.ShapeDtypeStruct((Cc, 2 * H) if transposed else (2 * H, Cc), F32),
                  compiler_params=_cparams(("parallel",)))(place, own, direct, via)


def _join_halves(fulls, axes, small):
    n = len(fulls)
    hs = [f.shape[ax] // 2 for f, ax in zip(fulls, axes)]
    rel = [(dx, dy, dc) for dx in (0, 1) for dy in (0, 1) for dc in (0, 1)][1:]

    def half(ref, k, hc):
        part = pl.ds(hc * hs[k], hs[k])
        return ref.at[:, part] if axes[k] else ref.at[part, :]

    def body(*refs):
        ins, small_in = refs[:n], refs[n]
        outs, small_out = refs[n + 1:2 * n + 1], refs[2 * n + 1]
        send_sems, recv_sems, ssend, srecv, local_sem = refs[2 * n + 2:]
        x, y, c, _ = _place()
        dev = 4 * x + 2 * y + c
        local = pltpu.make_async_copy(small_in, small_out.at[dev], local_sem)
        local.start()
        cps = []
        for k in range(n):
            cp = pltpu.make_async_remote_copy(src_ref=half(ins[k], k, c), dst_ref=half(outs[k], k, c),
                                              send_sem=send_sems.at[k], recv_sem=recv_sems.at[k],
                                              device_id=(x, y, 1 - c), device_id_type=MESH)
            cp.start()
            cps.append(cp)
        for r, (dx, dy, dc) in enumerate(rel):
            cp = pltpu.make_async_remote_copy(src_ref=small_in, dst_ref=small_out.at[dev], send_sem=ssend.at[r],
                                              recv_sem=srecv.at[r], device_id=(x ^ dx, y ^ dy, c ^ dc),
                                              device_id_type=MESH)
            cp.start()
            cps.append(cp)
        for k in range(n):
            land = half(outs[k], k, 1 - c)
            pltpu.make_async_remote_copy(src_ref=land, dst_ref=land, send_sem=send_sems.at[k],
                                         recv_sem=recv_sems.at[k], device_id=(x, y, c), device_id_type=MESH).wait_recv()
        for r, (dx, dy, dc) in enumerate(rel):
            land = small_out.at[4 * (x ^ dx) + 2 * (y ^ dy) + (c ^ dc)]
            pltpu.make_async_remote_copy(src_ref=land, dst_ref=land, send_sem=ssend.at[r], recv_sem=srecv.at[r],
                                         device_id=(x, y, c), device_id_type=MESH).wait_recv()
        for cp in cps:
            cp.wait_send()
        local.wait()

    return _pcall(
        body, name="join_halves", in_specs=[_ANY] * (n + 1), out_specs=[_ANY] * (n + 1),
        out_shape=[jax.ShapeDtypeStruct(f.shape, f.dtype) for f in fulls]
        + [jax.ShapeDtypeStruct((N_DEV,) + small.shape, small.dtype)],
        input_output_aliases={k: k for k in range(n)},
        scratch_shapes=[pltpu.SemaphoreType.DMA((n,)), pltpu.SemaphoreType.DMA((n,)), pltpu.SemaphoreType.DMA((7,)),
                        pltpu.SemaphoreType.DMA((7,)), pltpu.SemaphoreType.DMA],
    )(*fulls, small)


def _local_step(cfg, x2, target, norm_gain, w_my, fb, mu_g, w0, a0, k_k, k_a, r_k, ln_w, ln_b, fng, rest,
                exchange=None, h=None):
    T, D, FW, FH, RW, RH, LP, lora = cfg.T, cfg.D, cfg.FW, cfg.FH, cfg.RW, cfg.RH, cfg.LP, cfg.lora
    fb_p = jnp.pad(fb, ((0, 0), (0, LANES - FH)))
    mu = _rwkv_vec_to_my(cfg, mu_g)
    rk = r_k.reshape(1, RW)
    tm = min(1024, T)

    if h is None:
        h = _rms_fwd(cfg, x2, norm_gain)
    if len(rest) == 2:
        u, *got = _mm("in_proj", h, w_my, "nn", F32, tm, cfg.tn, 2048, comm=rest[0])
        rest = rest[1](got)
    else:
        u = _mm("in_proj", h, w_my, "nn", F32, tm, cfg.tn, 2048)
    w2, a2, wpf, wpr, wout = rest
    w2p = jnp.pad(w2, ((0, LP - lora), (0, 0)))
    a2p = jnp.pad(a2, ((0, LP - lora), (0, 0)))
    c_cols = _fox_prep(cfg, u, fb_p)
    c_rows = c_cols[:, :FH].T.reshape(FH, 1, T)
    o, lse = _attn_fwd(cfg, u, c_rows)
    oa = _gate_a_fwd(cfg, o, u)
    prep = _rwkv_prep_fwd(cfg, u, mu, w0, w2p, a0, a2p, k_k, k_a)
    r, lw, kp, v, an, b, zb = prep
    toks = [r, lw, kp, v, an, b]
    q_s, yloc, a_m, sloc = _scan_local_fwd(cfg, toks)
    y, ckpt = _scan_carry_fwd(cfg, q_s, yloc, a_m, sloc)
    ob = _rwkv_post_fwd(cfg, y, r, kp, v, zb, ln_w, ln_b, rk)
    pa = _mm("proj_fox", oa, wpf, "nn", F32, tm, 1024, 2048)
    pb = _mm("proj_rwkv", ob, wpr, "nn", F32, tm, 1024, 2048)
    m = _merge_fwd(cfg, pa, pb, u)
    mo = _mm("out_proj", m, wout, "nn", F32, tm, 1024, 2048)
    loss8, dres, dres16, d_fng = _final(cfg, x2, mo, fng.reshape(1, D), target)

    dm = _mm("out_proj_dx", dres16, wout, "nt", F32, tm, 1024, 2048)
    d_wout = _mm("out_proj_dw", m, dres16, "tn", BF16, 1024, 1024, 2048)
    dpa, dpb, du = _merge_bwd(cfg, pa, pb, u, dm)
    doa = _mm("proj_fox_dx", dpa, wpf, "nt", F32, tm, 1024, 2048)
    d_wpf = _mm("proj_fox_dw", oa, dpa, "tn", BF16, 1024, 1024, 2048)
    dob = _mm("proj_rwkv_dx", dpb, wpr, "nt", F32, tm, 1024, 2048)
    d_wpr = _mm("proj_rwkv_dw", ob, dpb, "tn", BF16, 1024, 1024, 2048)

    do, du = _gate_a_bwd(cfg, o, u, doa, du)
    du, dcol = _attn_bwd(cfg, u, c_rows, lse, do, du)
    dc = jnp.pad(-dcol.reshape(FH, T).T, ((0, 0), (0, LANES - FH)))
    df, d_fb = _fox_prep_bwd(cfg, u, fb_p, dc)

    dy, dr_p, dk_p, dv_p, dzb, d_lnw, d_lnb, d_rk = _rwkv_post_bwd(cfg, y, r, kp, v, zb, ln_w, ln_b, rk, dob)
    early = dict(w_proj_fox=d_wpf, w_proj_rwkv=d_wpr, w_out=d_wout)
    res = _scan_carry_bwd(cfg, q_s, a_m, ckpt, dy, exchange(early) if exchange else None)
    dq_s, da_m, dsl = res[:3]
    res = _scan_local_bwd(cfg, toks, dq_s, dy, da_m, dsl, [dr_p, dk_p, dv_p],
                          exchange(("swapped", list(res[3:]))) if exchange else None)
    cots, received = res[:6], list(res[6:])
    dus, d_mu, d_w0, d_w2p, d_a0, d_a2p, d_kk, d_ka = _rwkv_prep_bwd(cfg, u, mu, w0, w2p, a0, a2p, k_k, k_a, cots, dzb)
    du = _shift_bwd(cfg, dus, mu, df, du)
    if exchange:
        late = dict(w_in=exchange((h, du, d_w2p[:lora], d_a2p[:lora])))
    else:
        late = dict(w_in=_mm("in_proj_dw", h, du, "tn", BF16, 1024, cfg.tn, 2048), rwkv_w2=d_w2p[:lora],
                    rwkv_a2=d_a2p[:lora])
    tkx = 2 * cfg.tn if cfg.ncol % (2 * cfg.tn) == 0 else cfg.tn
    res = _mm("in_proj_dx", du, w_my, "nt", F32, tm, 1024, tkx, comm=exchange(late) if exchange else None)
    dh = res[0] if exchange else res
    big = dict(early, **late)
    res = _rms_bwd(cfg, x2, norm_gain, dh, dres, exchange(list(res[1:])) if exchange else None)
    gx, d_ng = res[:2]
    received += list(res[2:])

    small = dict(norm_gain=d_ng, fox_forget_bias=d_fb[:, :FH], rwkv_shift_mix=_rwkv_vec_from_my(cfg, d_mu),
                 rwkv_w0=d_w0, rwkv_a0=d_a0, rwkv_k_k=d_kk, rwkv_k_a=d_ka, rwkv_r_k=d_rk, rwkv_ln_w=d_lnw,
                 rwkv_ln_b=d_lnb, final_norm_gain=d_fng)
    return loss8[0, 0], gx, small, big, received


_SMALL = ["norm_gain", "fox_forget_bias", "rwkv_shift_mix", "rwkv_w0", "rwkv_a0", "rwkv_k_k", "rwkv_k_a", "rwkv_r_k",
          "rwkv_ln_w", "rwkv_ln_b", "final_norm_gain"]
_WEIGHTS = ["norm_gain", "w_in", "fox_forget_bias", "rwkv_shift_mix", "rwkv_w0", "rwkv_w2", "rwkv_a0", "rwkv_a2",
            "rwkv_k_k", "rwkv_k_a", "rwkv_r_k", "rwkv_ln_w", "rwkv_ln_b", "w_proj_fox", "w_proj_rwkv", "w_out",
            "final_norm_gain"]


def _pack_small(arrs):
    parts, n = [], 0
    for a in arrs:
        f = a.reshape(-1)
        fill = (-f.shape[0]) % LANES
        parts += [f] + ([jnp.zeros((fill,), f.dtype)] if fill else [])
        n += f.shape[0] + fill
    tail = ((-(n // LANES)) % 8) * LANES
    return jnp.concatenate(parts + ([jnp.zeros((tail,), parts[0].dtype)] if tail else [])).reshape(-1, LANES)


def _unpack_small(packed, shapes):
    flat = packed.reshape(-1)
    out, pos = [], 0
    for s in shapes:
        n = int(np.prod(s))
        out.append(flat[pos:pos + n].reshape(s))
        pos += n + ((-n) % LANES)
    return out


def _shard_major(a, axis):
    parts = jnp.split(a, N_CHIPS, axis=axis)
    return jnp.stack(parts, axis=0)


def kernel(x, norm_gain, w_in, fox_forget_bias, rwkv_shift_mix, rwkv_w0, rwkv_w2, rwkv_a0, rwkv_a2, rwkv_k_k, rwkv_k_a, rwkv_r_k, rwkv_ln_w, rwkv_ln_b, w_proj_fox, w_proj_rwkv, w_out, final_norm_gain, loss_target, m_norm_gain, m_w_in, m_fox_forget_bias, m_rwkv_shift_mix, m_rwkv_w0, m_rwkv_w2, m_rwkv_a0, m_rwkv_a2, m_rwkv_k_k, m_rwkv_k_a, m_rwkv_r_k, m_rwkv_ln_w, m_rwkv_ln_b, m_w_proj_fox, m_w_proj_rwkv, m_w_out, m_final_norm_gain, v_norm_gain, v_w_in, v_fox_forget_bias, v_rwkv_shift_mix, v_rwkv_w0, v_rwkv_w2, v_rwkv_a0, v_rwkv_a2, v_rwkv_k_k, v_rwkv_k_a, v_rwkv_r_k, v_rwkv_ln_w, v_rwkv_ln_b, v_w_proj_fox, v_w_proj_rwkv, v_w_out, v_final_norm_gain):
    args = dict(locals())
    T, D = x.shape[1], x.shape[2]
    lora = rwkv_w2.shape[1]
    cfg = _Cfg(T, D, lora)
    RW = cfg.RW
    c_idx = lax.axis_index("c").astype(jnp.int32).reshape(1)
    me_chip = (2 * lax.axis_index("x") + lax.axis_index("y")).astype(jnp.int32)
    place = jnp.concatenate([me_chip.reshape(1), c_idx])

    w_in_s = w_in[0].astype(BF16)
    lora_s = jnp.concatenate([rwkv_w2[0], rwkv_a2[0]], axis=0)
    h, g_in = _rms_fwd(cfg, x[0], norm_gain, _gather_parts([w_in_s]))
    w_my = _shards_to_my_layout(cfg, g_in)
    mine = [_cast_bf16("cast_w_proj_fox", w_proj_fox[0]), _cast_bf16("cast_w_proj_rwkv", w_proj_rwkv[0]),
            _cast_bf16("cast_w_out", w_out[0]), lora_s]

    def unpack(gathered):
        g_wpf, g_wpr, g_out, g_lora = gathered
        lo = g_lora.transpose(1, 0, 2).reshape(2 * lora, RW)
        return (lo[:lora], lo[lora:], g_wpf.transpose(1, 0, 2).reshape(RW, D),
                g_wpr.transpose(1, 0, 2).reshape(RW, D), g_out.reshape(D, D))

    early, late = ["w_proj_fox", "w_proj_rwkv", "w_out"], ["w_in", "lora"]
    names = early + late
    chip_sums, direct, shard_major = {}, {}, []
    n1_slot = (2 * (lax.axis_index("x") ^ (1 - lax.axis_index("c")))
               + (lax.axis_index("y") ^ lax.axis_index("c"))).astype(jnp.int32).reshape(1)

    def exchange(got):
        if isinstance(got, tuple) and len(got) == 4:
            h, du, d_w2, d_a2 = got
            c, half = lax.axis_index("c"), D // 2
            cols = lambda base: lax.dynamic_slice_in_dim(h, base * half, half, axis=1)
            lora_g = _shard_major(jnp.concatenate([d_w2, d_a2], axis=0).astype(BF16), 1)
            lora_rows = lambda base: lax.dynamic_slice_in_dim(lora_g, base * lora, lora, axis=1).reshape(-1, RW // 4)
            tiles = (BF16, min(1024, half), cfg.tn, 2048)
            sent = _mm("in_proj_dw_sibling", cols(1 - c), du, "tn", *tiles)
            kept, got_w, got_l = _mm("in_proj_dw", cols(c), du, "tn", *tiles,
                                     comm=_swap_with_sibling([sent, lora_rows(1 - c)]))
            return (_add_pair("add_halves_w_in", kept, got_w),
                    _add_pair("add_halves_lora", lora_rows(c), got_l).reshape(N_CHIPS, lora, RW // 4))
        if isinstance(got, dict):
            if "w_in" in got:
                sums = [_my_layout_to_shards(cfg, got["w_in"][0]), got["w_in"][1]]
                chip_sums.update(zip(late, sums))
                return _scatter_stage1(sums)
            shard_major.extend([_shard_major(got["w_proj_fox"], 1), _shard_major(got["w_proj_rwkv"], 1),
                                _shard_major(got["w_out"], 0)])
            return _swap_halves(shard_major)
        if got[0] == "swapped":
            sums = [_add_halves("add_halves_" + nm, g, r, c_idx) for nm, g, r in zip(early, shard_major, got[1])]
            chip_sums.update(zip(early, sums))
            return _scatter_to_owners(sums)
        direct.update(zip(late, got[0::2]))
        return _scatter_stage2([_add_passed("add_passed_" + nm, chip_sums[nm], g, n1_slot)
                                for nm, g in zip(late, got[1::2])])

    loss_dev, gx, small, _, recv2 = _local_step(
        cfg, x[0], loss_target[0], norm_gain, w_my, fox_forget_bias, rwkv_shift_mix, rwkv_w0, rwkv_a0, rwkv_k_k,
        rwkv_k_a, rwkv_r_k, rwkv_ln_w, rwkv_ln_b, final_norm_gain, (_gather_parts(mine), unpack), exchange, h)
    loss = lax.psum(loss_dev, ("x", "y", "c"))

    small_shapes = [args[nm].shape for nm in _SMALL]
    packed = _pack_small([small[nm] for nm in _SMALL])
    reduced = [_sum_chips("sum_chips_" + nm, r, chip_sums[nm], place) for nm, r in zip(early, recv2[:3])]
    reduced += [_sum_stages("sum_stages_" + nm, chip_sums[nm], direct[nm], via, place, transposed=nm == "w_in")
                for nm, via in zip(late, recv2[3:])]
    *joined, small_all = _join_halves(reduced, [int(nm == "w_in") for nm in names], packed)
    g_small = _sum_slots("sum_small", small_all)

    grads = dict(zip(_SMALL, _unpack_small(g_small, small_shapes)))
    grads.update({nm: g[None] for nm, g in zip(names, joined) if nm not in ("lora", "w_in")})
    g_lora_f = joined[names.index("lora")]
    grads["rwkv_w2"] = g_lora_f[None, :lora]
    grads["rwkv_a2"] = g_lora_f[None, lora:]

    delta, new_m, new_v = {}, {}, {}
    w_small = _pack_small([args[nm] for nm in _SMALL])
    m_small = _pack_small([args["m_" + nm] for nm in _SMALL])
    v_small = _pack_small([args["v_" + nm] for nm in _SMALL])
    d_s, m_s, v_s = _adamw("adamw_small", w_small, g_small, m_small, v_small)
    for tgt, pk in ((delta, d_s), (new_m, m_s), (new_v, v_s)):
        tgt.update(zip(_SMALL, _unpack_small(pk, small_shapes)))
    t_out = _adamw("adamw_w_in", w_in[0].T, joined[names.index("w_in")], m_w_in[0].T, v_w_in[0].T, copy_grad=True)
    delta["w_in"], new_m["w_in"], new_v["w_in"], grads["w_in"] = [t.T[None] for t in t_out]
    for nm in ("w_proj_fox", "w_proj_rwkv", "w_out", "rwkv_w2", "rwkv_a2"):
        shp = args[nm].shape
        two_d = (shp[1], shp[2])
        d_b, m_b, v_b = _adamw("adamw_" + nm, args[nm].reshape(two_d), grads[nm].reshape(two_d),
                               args["m_" + nm].reshape(two_d), args["v_" + nm].reshape(two_d))
        delta[nm], new_m[nm], new_v[nm] = d_b.reshape(shp), m_b.reshape(shp), v_b.reshape(shp)

    return (loss, gx[None], *[grads[n] for n in _WEIGHTS], *[delta[n] for n in _WEIGHTS],
            *[new_m[n] for n in _WEIGHTS], *[new_v[n] for n in _WEIGHTS])
```

```python
import functools

import numpy as np
import jax
import jax.numpy as jnp
from jax import lax
from jax.experimental import pallas as pl
from jax.experimental.pallas import tpu as pltpu

F32 = jnp.float32
BF16 = jnp.bfloat16
HI = lax.Precision.HIGHEST
MESH = pl.DeviceIdType.MESH

FOX_HEAD_DIM = 128
RWKV_HEAD_DIM = 64
RMS_EPS = 1e-6
GN_EPS = 64e-5
L2_EPS = 1e-12
ADAM_LR = 0.001
ADAM_B1 = 0.9
ADAM_B2 = 0.999
ADAM_EPS = 1e-08
ADAM_WD = 0.01
ADAM_STEP = 10

LANES = 128
VMEM_LIMIT = 56 * 1024 * 1024
SCAN_CHUNK = 64
SCAN_HEADS_PER_STEP = 16
SCAN_CHUNKS_PER_STEP = 2
SCAN_PASSES = ((3, 1), 1, 1)
N_CHIPS = 4
N_DEV = 8

_pcall = pl.pallas_call


def _cparams(sem=None):
    return pltpu.CompilerParams(dimension_semantics=sem, vmem_limit_bytes=VMEM_LIMIT)


def _softplus(x):
    return jnp.maximum(x, 0.0) + jnp.log(1.0 + jnp.exp(-jnp.abs(x)))


def _silu(z):
    return z * jax.nn.sigmoid(z)


def _rmsn(x, g):
    return x * lax.rsqrt(jnp.mean(x * x, axis=-1, keepdims=True) + RMS_EPS) * g


def _dot(a, b, dims="nn", precision=None):
    dn = {"nn": (((1,), (0,)), ((), ())), "nt": (((1,), (1,)), ((), ())), "tn": (((0,), (0,)), ((), ()))}[dims]
    return lax.dot_general(a, b, dn, precision=precision, preferred_element_type=F32)


def _split_bf16(x):
    hi = x.astype(BF16)
    return hi, (x - hi.astype(F32)).astype(BF16)


def _bdot_raw(a, b, ca, cb, passes):
    dn = (((ca,), (cb,)), ((0,), (0,)))
    mm = lambda p, q: lax.dot_general(p, q, dn, preferred_element_type=F32)
    passes = passes[0] if isinstance(passes, tuple) else passes
    if passes == 1:
        return mm(a.astype(BF16), b.astype(BF16))
    ah, al = _split_bf16(a)
    bh, bl = _split_bf16(b)
    return mm(ah, bh) + (mm(ah, bl) + mm(al, bh))


@functools.partial(jax.custom_vjp, nondiff_argnums=(2, 3, 4))
def _bdot_p(a, b, ca, cb, passes):
    return _bdot_raw(a, b, ca, cb, passes)


def _bdot_fwd(a, b, ca, cb, passes):
    return _bdot_raw(a, b, ca, cb, passes), (a, b)


def _bdot_bwd(ca, cb, passes, res, g):
    a, b = res
    passes = passes[1] if isinstance(passes, tuple) else passes
    if (ca, cb) == (2, 1):
        return _bdot_p(g, b, 2, 2, passes), _bdot_p(a, g, 1, 1, passes)
    if (ca, cb) == (2, 2):
        return _bdot_p(g, b, 2, 1, passes), _bdot_p(g, a, 1, 1, passes)
    assert (ca, cb) == (1, 1)
    return _bdot_p(b, g, 2, 2, passes), _bdot_p(a, g, 2, 1, passes)


_bdot_p.defvjp(_bdot_fwd, _bdot_bwd)


def _bdot(a, b, ca, cb, passes=3):
    return _bdot_p(a, b, ca, cb, passes)


def _dot3(a, b):
    return _bdot(a[None], b[None], 2, 1)[0]


@jax.custom_vjp
def _xdot(x, m, mt):
    hi, lo = _split_bf16(x)
    m16 = m.astype(BF16)
    return _dot(hi, m16) + _dot(lo, m16)


def _xdot_fwd(x, m, mt):
    return _xdot(x, m, mt), (m, mt)


def _xdot_bwd(res, g):
    m, mt = res
    return _xdot(g, mt, m), jnp.zeros_like(m), jnp.zeros_like(mt)


_xdot.defvjp(_xdot_fwd, _xdot_bwd)


class _Cfg:
    def __init__(self, T, D, lora):
        self.T, self.D, self.lora = T, D, lora
        self.FW = D // 2
        self.FH = self.FW // FOX_HEAD_DIM
        self.RW = D // 2
        self.RH = self.RW // RWKV_HEAD_DIM
        self.LP = -(-lora // LANES) * LANES
        self.o_fox = 0
        self.o_rwkv = 4 * self.FW
        self.o_gate = self.o_rwkv + 4 * self.RW
        self.o_f = self.o_gate + 2 * D
        self.o_wd = self.o_f + LANES
        self.o_ad = self.o_wd + self.LP
        end = self.o_ad + self.LP
        self.tn = 1280 if D >= 2048 else LANES
        self.ncol = -(-end // self.tn) * self.tn
        self.in_cols = 4 * self.FW + self.FH + 4 * self.RW + 2 * lora + 2 * D
        self.scp = -(-(self.in_cols // N_CHIPS) // LANES) * LANES
        self.rseg = 4 * self.RW + 2 * self.LP
        self.C = min(SCAN_CHUNK, T)
        self.tr = min(256, T)
        self.hb = min(SCAN_HEADS_PER_STEP, self.RH)
        self.cb = SCAN_CHUNKS_PER_STEP if (T // self.C) % SCAN_CHUNKS_PER_STEP == 0 else 1

    def segments(self):
        FW, FH, RW, lo, D = self.FW, self.FH, self.RW, self.lora, self.D
        g_f = 4 * FW
        g_r = g_f + FH
        g_wd = g_r + 4 * RW
        g_ad = g_wd + lo
        g_g = g_ad + lo
        dh = FOX_HEAD_DIM
        qkv = [(j * FW + h * dh, dh, (3 * h + j) * dh) for h in range(FH) for j in range(3)]
        return qkv + [(3 * FW, FW, 3 * FW), (g_f, FH, self.o_f), (g_r, 4 * RW, self.o_rwkv), (g_wd, lo, self.o_wd),
                      (g_ad, lo, self.o_ad), (g_g, 2 * D, self.o_gate)]


def _shards_to_my_layout(cfg, g):
    R, sc = g.shape[1], g.shape[2]
    segs = sorted(cfg.segments(), key=lambda s: s[2])
    parts, pos = [], 0
    for g0, w, m0 in segs:
        if m0 > pos:
            parts.append(jnp.zeros((R, m0 - pos), g.dtype))
        for s in range(N_CHIPS):
            lo, hi = max(g0, s * sc), min(g0 + w, (s + 1) * sc)
            if lo < hi:
                parts.append(g[s, :, lo - s * sc:hi - s * sc])
        pos = m0 + w
    if cfg.ncol > pos:
        parts.append(jnp.zeros((R, cfg.ncol - pos), g.dtype))
    return jnp.concatenate(parts, axis=1)


def _my_layout_to_shards(cfg, wm):
    sc, R = cfg.in_cols // N_CHIPS, wm.shape[0]
    segs = sorted(cfg.segments(), key=lambda s: s[0])
    shards = []
    for s in range(N_CHIPS):
        parts = []
        for g0, w, m0 in segs:
            lo, hi = max(g0, s * sc), min(g0 + w, (s + 1) * sc)
            if lo < hi:
                parts.append(wm[:, m0 + lo - g0:m0 + hi - g0])
        parts.append(jnp.zeros((R, cfg.scp - sc), wm.dtype))
        shards.append(jnp.concatenate(parts, axis=1))
    return jnp.stack(shards, axis=0)


def _rwkv_vec_to_my(cfg, v):
    RW4, lo, LP = 4 * cfg.RW, cfg.lora, cfg.LP
    z = jnp.zeros((1, LP - lo), v.dtype)
    return jnp.concatenate([v[:, :RW4], v[:, RW4:RW4 + lo], z, v[:, RW4 + lo:], z], axis=1)


def _rwkv_vec_from_my(cfg, v):
    RW4, lo, LP = 4 * cfg.RW, cfg.lora, cfg.LP
    return jnp.concatenate([v[:, :RW4], v[:, RW4:RW4 + lo], v[:, RW4 + LP:RW4 + LP + lo]], axis=1)


def _comm_at(comm, which, steps, cin, cout, scr):
    if not comm or len(comm) <= which:
        return
    lin, total = 0, 1
    for d, n in enumerate(steps):
        lin = lin * n + pl.program_id(d)
        total *= n
    pl.when(lin == {3: 0, 4: total - 1, 5: total // 2}[which])(lambda: comm[which](cin, cout, scr))


def _hosted(body, n_in, n_out, steps, comm):
    if not comm:
        return body, [], [], [], []
    ci, co, cs = len(comm[0]), len(comm[1]), len(comm[2])

    def wrapped(*refs):
        ins, cin = refs[:n_in], refs[n_in:n_in + ci]
        outs, cout = refs[n_in + ci:n_in + ci + n_out], refs[n_in + ci + n_out:n_in + ci + n_out + co]
        cscr, scr = refs[n_in + ci + n_out + co:n_in + ci + n_out + co + cs], refs[n_in + ci + n_out + co + cs:]
        _comm_at(comm, 3, steps, cin, cout, cscr)
        body(*ins, *outs, *scr)
        _comm_at(comm, 5, steps, cin, cout, cscr)
        _comm_at(comm, 4, steps, cin, cout, cscr)

    return wrapped, [_ANY] * ci, [_ANY] * co, list(comm[1]), list(comm[2])


def _mm(name, a, b, dims, out_dtype, tm, tn, tk, comm=None):
    (M, K) = a.shape if dims != "tn" else a.shape[::-1]
    N = b.shape[0] if dims == "nt" else b.shape[1]
    tm, tn, tk = min(tm, M), min(tn, N), min(tk, K)
    assert M % tm == 0 and N % tn == 0 and K % tk == 0, (name, M, N, K, tm, tn, tk)
    nk = K // tk
    steps = (M // tm, N // tn, nk)
    c_in, c_out, c_scr = comm[:3] if comm else ([], [], [])
    if dims == "nn":
        a_spec = pl.BlockSpec((tm, tk), lambda i, j, k: (i, k))
        b_spec = pl.BlockSpec((tk, tn), lambda i, j, k: (k, j))
    elif dims == "nt":
        a_spec = pl.BlockSpec((tm, tk), lambda i, j, k: (i, k))
        b_spec = pl.BlockSpec((tn, tk), lambda i, j, k: (j, k))
    else:
        a_spec = pl.BlockSpec((tk, tm), lambda i, j, k: (k, i))
        b_spec = pl.BlockSpec((tk, tn), lambda i, j, k: (k, j))

    n_acc = 1 if nk > 1 else 0

    def body(a_ref, b_ref, *rest):
        cin, o_ref = rest[:len(c_in)], rest[len(c_in)]
        cout = rest[len(c_in) + 1:len(c_in) + 1 + len(c_out)]
        scr = rest[len(c_in) + 1 + len(c_out):]
        _comm_at(comm, 3, steps, cin, cout, scr[n_acc:])
        if nk == 1:
            o_ref[...] = _dot(a_ref[...], b_ref[...], dims).astype(o_ref.dtype)
        else:
            acc_ref, k = scr[0], pl.program_id(2)

            @pl.when(k == 0)
            def _():
                acc_ref[...] = jnp.zeros_like(acc_ref)

            acc_ref[...] += _dot(a_ref[...], b_ref[...], dims)

            @pl.when(k == nk - 1)
            def _():
                o_ref[...] = acc_ref[...].astype(o_ref.dtype)

        _comm_at(comm, 5, steps, cin, cout, scr[n_acc:])
        _comm_at(comm, 4, steps, cin, cout, scr[n_acc:])

    res = _pcall(
        body, name=name, grid=steps,
        in_specs=[a_spec, b_spec] + [_ANY] * len(c_in),
        out_specs=[pl.BlockSpec((tm, tn), lambda i, j, k: (i, j))] + [_ANY] * len(c_out),
        out_shape=[jax.ShapeDtypeStruct((M, N), out_dtype)] + list(c_out),
        scratch_shapes=([pltpu.VMEM((tm, tn), F32)] if nk > 1 else []) + list(c_scr),
        compiler_params=_cparams(("arbitrary",) * 3 if comm else ("parallel", "parallel", "arbitrary")),
    )(a, b, *c_in)
    return res if comm else res[0]


def _tile(tr, w, cb=0):
    return pl.BlockSpec((tr, w), lambda i: (i, cb))


def _const(shape):
    nd = len(shape)
    return pl.BlockSpec(shape, lambda i: (0,) * nd)


def _acc_store(i, ref, val):
    @pl.when(i == 0)
    def _():
        ref[...] = val

    @pl.when(i > 0)
    def _():
        ref[...] += val


def _rms_fwd(cfg, x2, g, comm=None):
    T, D, tr = cfg.T, cfg.D, cfg.tr
    steps = (T // tr,)

    def body(x_ref, g_ref, h_ref):
        h_ref[...] = _rmsn(x_ref[...], g_ref[...]).astype(BF16)

    body, c_in, c_out, c_shapes, c_scr = _hosted(body, 2, 1, steps, comm)
    res = _pcall(body, name="rms_fwd", grid=steps, in_specs=[_tile(tr, D), _const((1, D))] + c_in,
                 out_specs=[_tile(tr, D)] + c_out, out_shape=[jax.ShapeDtypeStruct((T, D), BF16)] + c_shapes,
                 scratch_shapes=c_scr, compiler_params=_cparams(("arbitrary",) if comm else ("parallel",)),
                 )(x2, g, *(comm[0] if comm else []))
    return res if comm else res[0]


def _rms_bwd(cfg, x2, g, dh, dres, comm=None):
    T, D, tr = cfg.T, cfg.D, cfg.tr
    c_in, c_out, c_scr = comm[:3] if comm else ([], [], [])
    steps = (T // tr,)

    def body(x_ref, g_ref, dh_ref, dres_ref, *rest):
        cin, (gx_ref, dg_ref) = rest[:len(c_in)], rest[len(c_in):len(c_in) + 2]
        cout, scr = rest[len(c_in) + 2:len(c_in) + 2 + len(c_out)], rest[len(c_in) + 2 + len(c_out):]
        _comm_at(comm, 3, steps, cin, cout, scr)
        _, vjp = jax.vjp(_rmsn, x_ref[...], g_ref[...])
        dx, dg = vjp(dh_ref[...])
        gx_ref[...] = dx + dres_ref[...]
        _acc_store(pl.program_id(0), dg_ref, dg)
        _comm_at(comm, 4, steps, cin, cout, scr)

    return _pcall(body, name="rms_bwd", grid=steps,
                  in_specs=[_tile(tr, D), _const((1, D)), _tile(tr, D), _tile(tr, D)] + [_ANY] * len(c_in),
                  out_specs=[_tile(tr, D), _const((1, D))] + [_ANY] * len(c_out),
                  out_shape=[jax.ShapeDtypeStruct((T, D), F32), jax.ShapeDtypeStruct((1, D), F32)] + list(c_out),
                  scratch_shapes=list(c_scr), compiler_params=_cparams(("arbitrary",)))(x2, g, dh, dres, *c_in)


def _final(cfg, x2, mo, fg, target):
    T, D, tr = cfg.T, cfg.D, cfg.tr

    def loss_fn(hres, g, tgt):
        err = _rmsn(hres, g) - tgt
        return 0.5 * jnp.sum(jnp.mean(err * err, axis=-1, keepdims=True), axis=0, keepdims=True)

    def body(x_ref, mo_ref, g_ref, t_ref, loss_ref, dres_ref, dres16_ref, dg_ref):
        hres = x_ref[...] + mo_ref[...]
        loss, vjp = jax.vjp(functools.partial(loss_fn, tgt=t_ref[...]), hres, g_ref[...])
        dres, dg = vjp(jnp.ones((1, 1), F32))
        dres_ref[...] = dres
        dres16_ref[...] = dres.astype(BF16)
        i = pl.program_id(0)
        _acc_store(i, dg_ref, dg)
        _acc_store(i, loss_ref, jnp.broadcast_to(loss, (8, LANES)))

    return _pcall(body, name="final_loss", grid=(T // tr,),
                  in_specs=[_tile(tr, D), _tile(tr, D), _const((1, D)), _tile(tr, D)],
                  out_specs=[_const((8, LANES)), _tile(tr, D), _tile(tr, D), _const((1, D))],
                  out_shape=[jax.ShapeDtypeStruct((8, LANES), F32), jax.ShapeDtypeStruct((T, D), F32),
                             jax.ShapeDtypeStruct((T, D), BF16), jax.ShapeDtypeStruct((1, D), F32)],
                  compiler_params=_cparams(("arbitrary",)))(x2, mo, fg, target)


def _merge_fn(pa, pb, ga, gb):
    return jax.nn.sigmoid(ga) * pa + jax.nn.sigmoid(gb) * pb


def _merge_fwd(cfg, pa, pb, u):
    T, D, tr = cfg.T, cfg.D, cfg.tr
    cga, cgb = cfg.o_gate // D, cfg.o_gate // D + 1

    def body(pa_ref, pb_ref, ga_ref, gb_ref, m_ref):
        m_ref[...] = _merge_fn(pa_ref[...], pb_ref[...], ga_ref[...], gb_ref[...]).astype(BF16)

    return _pcall(body, name="merge_fwd", grid=(T // tr,),
                  in_specs=[_tile(tr, D), _tile(tr, D), _tile(tr, D, cga), _tile(tr, D, cgb)],
                  out_specs=_tile(tr, D), out_shape=jax.ShapeDtypeStruct((T, D), BF16),
                  compiler_params=_cparams(("parallel",)))(pa, pb, u, u)


def _merge_bwd(cfg, pa, pb, u, dm):
    T, D, tr = cfg.T, cfg.D, cfg.tr
    cga, cgb = cfg.o_gate // D, cfg.o_gate // D + 1

    def body(pa_ref, pb_ref, ga_ref, gb_ref, dm_ref, dpa_ref, dpb_ref, dg_ref):
        _, vjp = jax.vjp(_merge_fn, pa_ref[...], pb_ref[...], ga_ref[...], gb_ref[...])
        dpa, dpb, dga, dgb = vjp(dm_ref[...])
        dpa_ref[...] = dpa.astype(BF16)
        dpb_ref[...] = dpb.astype(BF16)
        dg_ref[:, :D] = dga.astype(BF16)
        dg_ref[:, D:] = dgb.astype(BF16)

    return _pcall(body, name="merge_bwd", grid=(T // tr,),
                  in_specs=[_tile(tr, D), _tile(tr, D), _tile(tr, D, cga), _tile(tr, D, cgb), _tile(tr, D)],
                  out_specs=[_tile(tr, D), _tile(tr, D), _tile(tr, 2 * D, cfg.o_gate // (2 * D))],
                  out_shape=[jax.ShapeDtypeStruct((T, D), BF16), jax.ShapeDtypeStruct((T, D), BF16),
                             jax.ShapeDtypeStruct((T, cfg.ncol), BF16)],
                  compiler_params=_cparams(("parallel",)))(pa, pb, u, u, dm)


def _gate_fn(o, z):
    return o * _silu(z)


def _gate_a_fwd(cfg, o, u):
    T, FW, tr = cfg.T, cfg.FW, cfg.tr

    def body(o_ref, z_ref, oa_ref):
        oa_ref[...] = _gate_fn(o_ref[...], z_ref[...]).astype(BF16)

    return _pcall(body, name="gate_a_fwd", grid=(T // tr,), in_specs=[_tile(tr, FW), _tile(tr, FW, 3)],
                  out_specs=_tile(tr, FW), out_shape=jax.ShapeDtypeStruct((T, FW), BF16),
                  compiler_params=_cparams(("parallel",)))(o, u)


def _gate_a_bwd(cfg, o, u, doa, du):
    T, FW, tr = cfg.T, cfg.FW, cfg.tr

    def body(o_ref, z_ref, doa_ref, du_in, do_ref, dz_ref):
        _, vjp = jax.vjp(_gate_fn, o_ref[...], z_ref[...])
        do, dz = vjp(doa_ref[...])
        do_ref[...] = do
        dz_ref[...] = dz.astype(BF16)

    return _pcall(body, name="gate_a_bwd", grid=(T // tr,),
                  in_specs=[_tile(tr, FW), _tile(tr, FW, 3), _tile(tr, FW), _ANY],
                  out_specs=[_tile(tr, FW), _tile(tr, FW, 3)],
                  out_shape=[jax.ShapeDtypeStruct((T, FW), F32), jax.ShapeDtypeStruct(du.shape, BF16)],
                  input_output_aliases={3: 1},
                  compiler_params=_cparams(("parallel",)))(o, u, doa, du)


def _fox_prep(cfg, u, fb):
    T, tr = cfg.T, cfg.tr
    cf = cfg.o_f // LANES

    def body(f_ref, fb_ref, c_ref, carry_ref):
        i = pl.program_id(0)

        @pl.when(i == 0)
        def _():
            carry_ref[...] = jnp.zeros_like(carry_ref)

        lf = -_softplus(-(f_ref[...] + fb_ref[...]))
        r = lax.broadcasted_iota(jnp.int32, (tr, tr), 0)
        c = lax.broadcasted_iota(jnp.int32, (tr, tr), 1)
        tri = (r >= c).astype(F32)
        c_ref[...] = _dot(tri, lf, precision=HI) + carry_ref[...]
        carry_ref[...] += jnp.sum(lf, axis=0, keepdims=True)

    return _pcall(body, name="fox_prep", grid=(T // tr,), in_specs=[_tile(tr, LANES, cf), _const((1, LANES))],
                  out_specs=_tile(tr, LANES), out_shape=jax.ShapeDtypeStruct((T, LANES), F32),
                  scratch_shapes=[pltpu.VMEM((1, LANES), F32)], compiler_params=_cparams(("arbitrary",)))(u, fb)


def _fox_prep_bwd(cfg, u, fb, dc):
    T, tr = cfg.T, cfg.tr
    cf = cfg.o_f // LANES
    nb = T // tr

    def body(f_ref, fb_ref, dc_ref, df_ref, dfb_ref, carry_ref):
        i = pl.program_id(0)

        @pl.when(i == 0)
        def _():
            carry_ref[...] = jnp.zeros_like(carry_ref)

        dc = dc_ref[...]
        r = lax.broadcasted_iota(jnp.int32, (tr, tr), 0)
        c = lax.broadcasted_iota(jnp.int32, (tr, tr), 1)
        triu = (r <= c).astype(F32)
        dlf = _dot(triu, dc, precision=HI) + carry_ref[...]
        carry_ref[...] += jnp.sum(dc, axis=0, keepdims=True)
        dz = dlf * jax.nn.sigmoid(-(f_ref[...] + fb_ref[...]))
        df_ref[...] = dz.astype(BF16)
        _acc_store(i, dfb_ref, jnp.sum(dz, axis=0, keepdims=True))

    rev = lambda i: (nb - 1 - i, 0)
    return _pcall(body, name="fox_prep_bwd", grid=(nb,),
                  in_specs=[pl.BlockSpec((tr, LANES), lambda i: (nb - 1 - i, cf)), _const((1, LANES)),
                            pl.BlockSpec((tr, LANES), rev)],
                  out_specs=[pl.BlockSpec((tr, LANES), rev), _const((1, LANES))],
                  out_shape=[jax.ShapeDtypeStruct((T, LANES), BF16), jax.ShapeDtypeStruct((1, LANES), F32)],
                  scratch_shapes=[pltpu.VMEM((1, LANES), F32)], compiler_params=_cparams(("arbitrary",)))(u, fb, dc)


def _attn_logits(q_ref, k_ref, c_ref, tq, te):
    q = q_ref[...].astype(BF16)
    scale = FOX_HEAD_DIM ** -0.5
    part = lambda k0, k1: _dot(q, k_ref[k0:k1, :].astype(BF16), "nt") * scale - c_ref[0, :, k0:k1]
    row = lax.broadcasted_iota(jnp.int32, (tq, tq), 0)
    col = lax.broadcasted_iota(jnp.int32, (tq, tq), 1)
    own = ((te - tq, te), jnp.where(col <= row, part(te - tq, te), -1e30))
    return [((0, te - tq), part(0, te - tq)), own] if te > tq else [own]


def _per_query_tile(i, nq, tq, fn):
    for ii in range(nq):
        pl.when(i == ii)(functools.partial(fn, (ii + 1) * tq))


def _attn_fwd(cfg, u, c_rows):
    T, FW, FH = cfg.T, cfg.FW, cfg.FH
    tq = min(256, T)
    dh = FOX_HEAD_DIM

    def body(q_ref, k_ref, v_ref, c_ref, o_ref, lse_ref):
        i = pl.program_id(1)

        def tile(te):
            parts = _attn_logits(q_ref, k_ref, c_ref, tq, te)
            m = functools.reduce(jnp.maximum, [jnp.max(s, axis=1, keepdims=True) for _, s in parts])
            l, acc = 0.0, 0.0
            for (k0, k1), s in parts:
                p = jnp.exp(s - m)
                l = l + jnp.sum(p, axis=1, keepdims=True)
                acc = acc + _dot(p.astype(BF16), v_ref[k0:k1, :].astype(BF16))
            o_ref[...] = acc / l
            lse_ref[0] = m + jnp.log(l)

        _per_query_tile(i, T // tq, tq, tile)

    return _pcall(
        body, name="fox_attn_fwd", grid=(FH, T // tq),
        in_specs=[pl.BlockSpec((tq, dh), lambda h, i: (i, 3 * h)), pl.BlockSpec((T, dh), lambda h, i: (0, 3 * h + 1)),
                  pl.BlockSpec((T, dh), lambda h, i: (0, 3 * h + 2)), pl.BlockSpec((1, 1, T), lambda h, i: (h, 0, 0))],
        out_specs=[pl.BlockSpec((tq, dh), lambda h, i: (i, h)), pl.BlockSpec((1, tq, 1), lambda h, i: (h, i, 0))],
        out_shape=[jax.ShapeDtypeStruct((T, FW), F32), jax.ShapeDtypeStruct((FH, T, 1), F32)],
        compiler_params=_cparams(("parallel", "arbitrary")),
    )(u, u, u, c_rows)


def _attn_bwd(cfg, u, c_rows, lse, do, du):
    T, FW, FH = cfg.T, cfg.FW, cfg.FH
    tq = min(256, T)
    nq = T // tq
    dh = FOX_HEAD_DIM
    scale = dh ** -0.5

    def body(q_ref, k_ref, v_ref, c_ref, lse_ref, do_ref, du_in, du_ref, dcol_ref, dk_acc, dv_acc):
        i = pl.program_id(1)

        @pl.when(i == 0)
        def _():
            dk_acc[...] = jnp.zeros_like(dk_acc)
            dv_acc[...] = jnp.zeros_like(dv_acc)
            dcol_ref[...] = jnp.zeros_like(dcol_ref)

        def tile(te):
            lse, q16, do16 = lse_ref[0], q_ref[...].astype(BF16), do_ref[...].astype(BF16)
            probs = [(ks, jnp.exp(s - lse)) for ks, s in _attn_logits(q_ref, k_ref, c_ref, tq, te)]
            dps = [_dot(do16, v_ref[k0:k1, :].astype(BF16), "nt") for (k0, k1), _ in probs]
            delta = sum(jnp.sum(p * dp, axis=1, keepdims=True) for (_, p), dp in zip(probs, dps))
            dq = 0.0
            for ((k0, k1), p), dp in zip(probs, dps):
                ds = p * (dp - delta)
                ds16 = ds.astype(BF16)
                dq = dq + _dot(ds16, k_ref[k0:k1, :].astype(BF16))
                dk_acc[k0:k1, :] += _dot(ds16, q16, "tn") * scale
                dv_acc[k0:k1, :] += _dot(p.astype(BF16), do16, "tn")
                dcol_ref[0, :, k0:k1] += jnp.sum(ds, axis=0, keepdims=True)
            du_ref[te - tq:te, 0:dh] = (dq * scale).astype(BF16)

        _per_query_tile(i, nq, tq, tile)

        @pl.when(i == nq - 1)
        def _():
            du_ref[:, dh:2 * dh] = dk_acc[...].astype(BF16)
            du_ref[:, 2 * dh:3 * dh] = dv_acc[...].astype(BF16)

    return _pcall(
        body, name="fox_attn_bwd", grid=(FH, nq),
        in_specs=[pl.BlockSpec((tq, dh), lambda h, i: (i, 3 * h)), pl.BlockSpec((T, dh), lambda h, i: (0, 3 * h + 1)),
                  pl.BlockSpec((T, dh), lambda h, i: (0, 3 * h + 2)), pl.BlockSpec((1, 1, T), lambda h, i: (h, 0, 0)),
                  pl.BlockSpec((1, tq, 1), lambda h, i: (h, i, 0)), pl.BlockSpec((tq, dh), lambda h, i: (i, h)), _ANY],
        out_specs=[pl.BlockSpec((T, 3 * dh), lambda h, i: (0, h)), pl.BlockSpec((1, 1, T), lambda h, i: (h, 0, 0))],
        out_shape=[jax.ShapeDtypeStruct(du.shape, BF16), jax.ShapeDtypeStruct((FH, 1, T), F32)],
        scratch_shapes=[pltpu.VMEM((T, dh), F32), pltpu.VMEM((T, dh), F32)],
        input_output_aliases={6: 0},
        compiler_params=_cparams(("parallel", "arbitrary")),
    )(u, u, u, c_rows, lse, do, du)


def _head_indicators(cfg):
    ind = np.zeros((cfg.RW, LANES), np.float32)
    ind[np.arange(cfg.RW), np.arange(cfg.RW) // RWKV_HEAD_DIM] = 1.0
    pad = np.zeros((1, LANES), np.float32)
    pad[0, cfg.RH:] = 1.0
    return jnp.asarray(ind), jnp.asarray(ind.T.copy()), jnp.asarray(pad)


def _prep_fn(us_r, us_k, us_v, us_wd, us_ad, w0, w2p, a0, a2p, k_k, k_a, ind, ind_t, pad):
    wpre = w0 + _dot3(jnp.tanh(us_wd), w2p)
    w = -_softplus(-wpre) - 0.5
    lw = -jnp.exp(w)
    a = jax.nn.sigmoid(a0 + _dot3(us_ad, a2p))
    kk = us_k * k_k
    ss = _xdot(kk * kk, ind, ind_t) + pad
    inv = 1.0 / jnp.maximum(jnp.sqrt(ss), L2_EPS)
    kkn = kk * _xdot(inv, ind_t, ind)
    kp = us_k * (1.0 + (a - 1.0) * k_a)
    return us_r, lw, kp, us_v, -kkn, kkn * a


def _shifted(u, prev_row, mu, first):
    n = u.shape[0]
    rolled = pltpu.roll(u, 1, 0)
    row = lax.broadcasted_iota(jnp.int32, u.shape, 0)
    p0 = jnp.where(first, jnp.zeros_like(prev_row), prev_row)
    prev = jnp.where(row == 0, jnp.broadcast_to(p0, u.shape), rolled)
    return u + (prev - u) * mu, prev


def _rwkv_specs(cfg, tr):
    RW, LP = cfg.RW, cfg.LP
    base = cfg.o_rwkv // RW
    cols = [(RW, base), (RW, base + 1), (RW, base + 2), (RW, base + 3), (LP, cfg.o_wd // LP), (LP, cfg.o_ad // LP)]
    cur = [pl.BlockSpec((tr, w), (lambda i, cb=cb: (i, cb))) for w, cb in cols]
    prv = [pl.BlockSpec((8, w), (lambda i, cb=cb: (jnp.maximum(i * (tr // 8) - 1, 0), cb))) for w, cb in cols]
    return cols, cur, prv


def _mu_pieces(cfg, mu_ref):
    RW, LP = cfg.RW, cfg.LP
    offs = [0, RW, 2 * RW, 3 * RW, 4 * RW, 4 * RW + LP, 4 * RW + 2 * LP]
    return [mu_ref[:, offs[j]:offs[j + 1]] for j in range(6)]


def _rwkv_prep_fwd(cfg, u, mu, w0, w2p, a0, a2p, k_k, k_a):
    T, RW, LP, tr = cfg.T, cfg.RW, cfg.LP, cfg.tr
    ind, ind_t, pad = _head_indicators(cfg)
    cols, cur, prv = _rwkv_specs(cfg, tr)

    def body(*refs):
        u_refs, p_refs = refs[0:6], refs[6:12]
        mu_ref, w0_ref, w2_ref, a0_ref, a2_ref, kk_ref, ka_ref, ind_ref, indt_ref, pad_ref = refs[12:22]
        outs = refs[22:]
        first = pl.program_id(0) == 0
        mus = _mu_pieces(cfg, mu_ref)
        us = [_shifted(u_refs[j][...], p_refs[j][7:8, :], mus[j], first)[0] for j in range(6)]
        res = _prep_fn(us[0], us[1], us[2], us[4], us[5], w0_ref[...], w2_ref[...], a0_ref[...], a2_ref[...],
                       kk_ref[...], ka_ref[...], ind_ref[...], indt_ref[...], pad_ref[...])
        for j in range(6):
            outs[j][...] = res[j]
        outs[6][...] = us[3]

    consts = [mu, w0, w2p, a0, a2p, k_k, k_a, ind, ind_t, pad]
    return _pcall(body, name="rwkv_prep_fwd", grid=(T // tr,),
                  in_specs=cur + prv + [_const(c.shape) for c in consts],
                  out_specs=[_tile(tr, RW)] * 7, out_shape=[jax.ShapeDtypeStruct((T, RW), F32)] * 7,
                  compiler_params=_cparams(("parallel",)))(*([u] * 12), *consts)


def _rwkv_prep_bwd(cfg, u, mu, w0, w2p, a0, a2p, k_k, k_a, cots, dzb):
    T, RW, LP = cfg.T, cfg.RW, cfg.LP
    tr = min(128, T)
    ind, ind_t, pad = _head_indicators(cfg)
    cols, cur, prv = _rwkv_specs(cfg, tr)
    rseg = cfg.rseg

    def body(*refs):
        u_refs, p_refs = refs[0:6], refs[6:12]
        mu_ref, w0_ref, w2_ref, a0_ref, a2_ref, kk_ref, ka_ref, ind_ref, indt_ref, pad_ref = refs[12:22]
        cot_refs, dzb_ref = refs[22:28], refs[28]
        dus_ref, dmu_ref, dw0_ref, dw2_ref, da0_ref, da2_ref, dkk_ref, dka_ref = refs[29:]
        i = pl.program_id(0)
        first = i == 0
        mus = _mu_pieces(cfg, mu_ref)
        sh = [_shifted(u_refs[j][...], p_refs[j][7:8, :], mus[j], first) for j in range(6)]
        us = [s[0] for s in sh]
        fn = functools.partial(_prep_fn, ind=ind_ref[...], ind_t=indt_ref[...], pad=pad_ref[...])
        _, vjp = jax.vjp(fn, us[0], us[1], us[2], us[4], us[5], w0_ref[...], w2_ref[...], a0_ref[...], a2_ref[...],
                         kk_ref[...], ka_ref[...])
        d = vjp(tuple(c[...] for c in cot_refs))
        dus = [d[0], d[1], d[2], dzb_ref[...], d[3], d[4]]
        offs = [0, RW, 2 * RW, 3 * RW, 4 * RW, 4 * RW + LP, 4 * RW + 2 * LP]
        for j in range(6):
            dus_ref[:, offs[j]:offs[j + 1]] = dus[j]
            dmu_j = jnp.sum(dus[j] * (sh[j][1] - u_refs[j][...]), axis=0, keepdims=True)

            @pl.when(first)
            def _(j=j, dmu_j=dmu_j):
                dmu_ref[:, offs[j]:offs[j + 1]] = dmu_j

            @pl.when(i > 0)
            def _(j=j, dmu_j=dmu_j):
                dmu_ref[:, offs[j]:offs[j + 1]] += dmu_j
        for ref, val in zip((dw0_ref, dw2_ref, da0_ref, da2_ref, dkk_ref, dka_ref), d[5:11]):
            _acc_store(i, ref, val)

    consts = [mu, w0, w2p, a0, a2p, k_k, k_a, ind, ind_t, pad]
    vec = jax.ShapeDtypeStruct((1, RW), F32)
    mat = jax.ShapeDtypeStruct((LP, RW), F32)
    return _pcall(body, name="rwkv_prep_bwd", grid=(T // tr,),
                  in_specs=cur + prv + [_const(c.shape) for c in consts] + [_tile(tr, RW)] * 7,
                  out_specs=[_tile(tr, rseg), _const((1, rseg)), _const((1, RW)), _const((LP, RW)), _const((1, RW)),
                             _const((LP, RW)), _const((1, RW)), _const((1, RW))],
                  out_shape=[jax.ShapeDtypeStruct((T, rseg), F32), jax.ShapeDtypeStruct((1, rseg), F32),
                             vec, mat, vec, mat, vec, vec],
                  compiler_params=_cparams(("arbitrary",)))(*([u] * 12), *consts, *cots, dzb)


def _shift_bwd(cfg, dus, mu, df, du):
    T, tr, RW, LP = cfg.T, cfg.tr, cfg.RW, cfg.LP
    nb = T // tr
    tail = cfg.ncol - cfg.o_f
    assert cfg.o_rwkv % (4 * RW) == 0 and (4 * RW) % (2 * LP) == 0 and cfg.o_f % tail == 0

    def shifted(d_ref, n_ref, mu_ref):
        d = d_ref[...]
        rolled = pltpu.roll(d, tr - 1, 0)
        row = lax.broadcasted_iota(jnp.int32, d.shape, 0)
        n0 = jnp.where(pl.program_id(0) == nb - 1, jnp.zeros_like(n_ref[0:1, :]), n_ref[0:1, :])
        nxt = jnp.where(row == tr - 1, jnp.broadcast_to(n0, d.shape), rolled)
        mu_v = mu_ref[...]
        return (d * (1.0 - mu_v) + nxt * mu_v).astype(BF16)

    def main_body(d_ref, n_ref, mu_ref, du_in, du_ref):
        du_ref[...] = shifted(d_ref, n_ref, mu_ref)

    def tail_body(d_ref, n_ref, mu_ref, df_ref, du_in, du_ref):
        du_ref[:, 0:LANES] = df_ref[...]
        du_ref[:, LANES:LANES + 2 * LP] = shifted(d_ref, n_ref, mu_ref)
        if tail > LANES + 2 * LP:
            du_ref[:, LANES + 2 * LP:] = jnp.zeros((tr, tail - LANES - 2 * LP), BF16)

    def specs(w, cb):
        return [_tile(tr, w, cb),
                pl.BlockSpec((8, w), lambda i: (jnp.minimum((i + 1) * (tr // 8), T // 8 - 1), cb)),
                pl.BlockSpec((1, w), lambda i: (0, cb))]

    out = jax.ShapeDtypeStruct(du.shape, BF16)
    du = _pcall(main_body, name="shift_bwd_main", grid=(nb,), in_specs=specs(4 * RW, 0) + [_ANY],
                out_specs=_tile(tr, 4 * RW, cfg.o_rwkv // (4 * RW)), out_shape=out, input_output_aliases={3: 0},
                compiler_params=_cparams(("parallel",)))(dus, dus, mu, du)
    return _pcall(tail_body, name="shift_bwd_tail", grid=(nb,),
                  in_specs=specs(2 * LP, 4 * RW // (2 * LP)) + [_tile(tr, LANES), _ANY],
                  out_specs=_tile(tr, tail, cfg.o_f // tail), out_shape=out, input_output_aliases={4: 0},
                  compiler_params=_cparams(("parallel",)))(dus, dus, mu, df, du)


def _chunk_local(r, lw, k, v, a, b):
    H, C, K = r.shape
    row = lax.broadcasted_iota(jnp.int32, (C, C), 0)
    col = lax.broadcasted_iota(jnp.int32, (C, C), 1)
    incl = jnp.broadcast_to((row >= col).astype(F32)[None], (H, C, C))
    strict = (row > col)[None]
    lower = (row >= col)[None]
    eye = (row == col)[None]
    zero = jnp.zeros((), F32)
    L = _bdot(incl, lw, 2, 1)
    LC = jnp.sum(lw, axis=1, keepdims=True)
    eL = jnp.exp(L)
    eLn = jnp.exp(-L)
    at = a * jnp.exp(L - lw)
    rt = r * eL
    bt = b * eLn
    kt = k * eLn
    eR = jnp.exp(LC - L)
    bh = b * eR
    kh = k * eR
    keys = functools.partial(_bdot, passes=SCAN_PASSES[0])
    inv = functools.partial(_bdot, passes=SCAN_PASSES[1])
    app = functools.partial(_bdot, passes=SCAN_PASSES[2])
    ar = jnp.concatenate([at, rt], axis=1)
    g_b = app(ar, bt, 2, 2)
    g_k = keys(ar, kt, 2, 2)
    n_ab = jnp.where(strict, g_b[:, :C], zero)
    n_ak = jnp.where(strict, g_k[:, :C], zero)
    m_rb = jnp.where(lower, g_b[:, C:], zero)
    m_rk = jnp.where(lower, g_k[:, C:], zero)
    P = jnp.where(eye, 1.0, zero) + n_ab
    squarings = max(1, int(np.ceil(np.log2(C)))) - 1
    if squarings:
        M = inv(n_ab, n_ab, 2, 1)
        for _ in range(squarings - 1):
            PM = inv(M, jnp.concatenate([P, M], axis=2), 2, 1)
            P, M = P + PM[:, :, :C], PM[:, :, C:]
        P = P + inv(M, P, 2, 1)
    W = app(P, at, 2, 1)
    Uloc = app(P, app(n_ak, v, 2, 1), 2, 1)
    Q = rt + app(m_rb, W, 2, 1)
    Yloc = app(m_rb, Uloc, 2, 1) + app(m_rk, v, 2, 1)
    A = jnp.where(eye, jnp.exp(LC), zero) + app(W, bh, 1, 1)
    Sloc = app(Uloc, bh, 1, 1) + app(v, kh, 1, 1)
    return Q, Yloc, A, Sloc


def _split_heads(ref, n):
    N = RWKV_HEAD_DIM
    return jnp.stack([ref[:, h * N:(h + 1) * N] for h in range(n)], axis=0)


def _merge_heads(x):
    return jnp.concatenate([x[h] for h in range(x.shape[0])], axis=1)


def _chains(x, cb):
    hb = x.shape[0]
    return x.reshape(hb, cb, -1, x.shape[-1]).reshape(hb * cb, -1, x.shape[-1])


def _unchains(x, cb, seq):
    hb = x.shape[0] // cb
    x = x.reshape(hb, cb, x.shape[1], x.shape[2])
    return x.reshape(hb, cb * x.shape[2], x.shape[3]) if seq else x


def _scan_local_specs(cfg):
    N, HB, CB = RWKV_HEAD_DIM, cfg.hb, cfg.cb
    grid = (cfg.RH // HB, cfg.T // (CB * cfg.C))
    seq = pl.BlockSpec((HB, CB * cfg.C, N), lambda h, j: (h, j, 0))
    mat = pl.BlockSpec((HB, CB, N, N), lambda h, j: (h, j, 0, 0))
    return grid, seq, mat


def _scan_local_bwd(cfg, toks, dq, dy, da, dsl, extra, comm=None):
    T, RW, N = cfg.T, cfg.RW, RWKV_HEAD_DIM
    grid, seq, mat = _scan_local_specs(cfg)
    c_in, c_out, c_scr = comm[:3] if comm else ([], [], [])

    def body(r_ref, lw_ref, k_ref, v_ref, a_ref, b_ref, dq_ref, dy_ref, da_ref, dsl_ref, xr_ref, xk_ref, xv_ref,
             *rest):
        cin, outs = rest[:len(c_in)], rest[len(c_in):len(c_in) + 6]
        cout, scr = rest[len(c_in) + 6:len(c_in) + 6 + len(c_out)], rest[len(c_in) + 6 + len(c_out):]
        _comm_at(comm, 3, grid, cin, cout, scr)
        ins = [_chains(_split_heads(ref, cfg.hb), cfg.cb) for ref in (r_ref, lw_ref, k_ref, v_ref, a_ref, b_ref)]
        _, vjp = jax.vjp(_chunk_local, *ins)
        d = vjp((_chains(dq_ref[...], cfg.cb), _chains(_split_heads(dy_ref, cfg.hb), cfg.cb),
                 _chains(da_ref[...], cfg.cb), _chains(dsl_ref[...], cfg.cb)))
        add = {0: xr_ref, 2: xk_ref, 3: xv_ref}
        for j in range(6):
            dj = _merge_heads(_unchains(d[j], cfg.cb, True))
            outs[j][...] = dj + add[j][...] if j in add else dj
        _comm_at(comm, 4, grid, cin, cout, scr)

    tok = pl.BlockSpec((cfg.cb * cfg.C, cfg.hb * N), lambda h, j: (j, h))
    return _pcall(body, name="rwkv_scan_local_bwd", grid=grid,
                  in_specs=[tok] * 6 + [seq, tok, mat, mat] + [tok] * 3 + [_ANY] * len(c_in),
                  out_specs=[tok] * 6 + [_ANY] * len(c_out),
                  out_shape=[jax.ShapeDtypeStruct((T, RW), F32)] * 6 + list(c_out), scratch_shapes=list(c_scr),
                  compiler_params=_cparams(("arbitrary", "arbitrary") if comm else ("parallel", "parallel")),
                  )(*toks, dq, dy, da, dsl, *extra, *c_in)


def _scan_carry_specs(cfg, rev):
    N, RH, C, nc = RWKV_HEAD_DIM, cfg.RH, cfg.C, cfg.T // cfg.C
    at = (lambda j: nc - 1 - j) if rev else (lambda j: j)
    seq = pl.BlockSpec((RH, C, N), lambda j: (0, at(j), 0))
    mat = pl.BlockSpec((RH, 1, N, N), lambda j: (0, at(j), 0, 0))
    return nc, seq, mat


def _scan_fwd(cfg, seqs):
    T, RH, N, C, CB = cfg.T, cfg.RH, RWKV_HEAD_DIM, cfg.C, cfg.cb
    nc = T // C

    def body(r_ref, lw_ref, k_ref, v_ref, a_ref, b_ref, y_ref, q_ref, a_out, ck_ref, s_ref):
        @pl.when(pl.program_id(0) == 0)
        def _():
            s_ref[...] = jnp.zeros_like(s_ref)

        ins = [_chains(_split_heads(ref, RH), CB) for ref in (r_ref, lw_ref, k_ref, v_ref, a_ref, b_ref)]
        Q, Yloc, A, Sloc = _chunk_local(*ins)
        q_ref[...] = _unchains(Q, CB, True)
        a_out[...] = _unchains(A, CB, False)
        Q, Yloc = Q.reshape(RH, CB, C, N), Yloc.reshape(RH, CB, C, N)
        A, Sloc = A.reshape(RH, CB, N, N), Sloc.reshape(RH, CB, N, N)
        S = s_ref[...]
        for c in range(CB):
            ck_ref[:, c] = S
            y_ref[c * C:(c + 1) * C, :] = _merge_heads(_bdot(Q[:, c], S, 2, 2, SCAN_PASSES[2]) + Yloc[:, c])
            S = _bdot(S, A[:, c], 2, 1) + Sloc[:, c]
        s_ref[...] = S

    tok = pl.BlockSpec((CB * C, cfg.RW), lambda j: (j, 0))
    seq = pl.BlockSpec((RH, CB * C, N), lambda j: (0, j, 0))
    mat = pl.BlockSpec((RH, CB, N, N), lambda j: (0, j, 0, 0))
    mt = jax.ShapeDtypeStruct((RH, nc, N, N), F32)
    return _pcall(body, name="rwkv_scan_fwd", grid=(nc // CB,), in_specs=[tok] * 6, out_specs=[tok, seq, mat, mat],
                  out_shape=[jax.ShapeDtypeStruct((T, cfg.RW), F32), jax.ShapeDtypeStruct((RH, T, N), F32), mt, mt],
                  scratch_shapes=[pltpu.VMEM((RH, N, N), F32)],
                  compiler_params=_cparams(("arbitrary",)))(*seqs)


def _scan_carry_bwd(cfg, q, a, ckpt, dy, comm=None):
    T, RH, N = cfg.T, cfg.RH, RWKV_HEAD_DIM
    nc, seq, mat = _scan_carry_specs(cfg, True)

    def body(q_ref, a_ref, ck_ref, dy_ref, dq_ref, da_ref, dsl_ref, ds_ref):
        @pl.when(pl.program_id(0) == 0)
        def _():
            ds_ref[...] = jnp.zeros_like(ds_ref)

        S, dS, dY = ck_ref[:, 0], ds_ref[...], _split_heads(dy_ref, RH)
        dq_ref[...] = _bdot(dY, S, 2, 1, SCAN_PASSES[2])
        da_ref[:, 0] = _bdot(S, dS, 1, 1, SCAN_PASSES[2])
        dsl_ref[:, 0] = dS
        ds_ref[...] = _bdot(dS, a_ref[:, 0], 2, 2) + _bdot(dY, q_ref[...], 1, 1, SCAN_PASSES[2])

    mt = jax.ShapeDtypeStruct((RH, nc, N, N), F32)
    tok = pl.BlockSpec((cfg.C, cfg.RW), lambda j: (nc - 1 - j, 0))
    body, c_in, c_out, c_shapes, c_scr = _hosted(body, 4, 3, (nc,), comm)
    return _pcall(body, name="rwkv_scan_carry_bwd", grid=(nc,), in_specs=[seq, mat, mat, tok] + c_in,
                  out_specs=[seq, mat, mat] + c_out,
                  out_shape=[jax.ShapeDtypeStruct((RH, T, N), F32), mt, mt] + c_shapes,
                  scratch_shapes=c_scr + [pltpu.VMEM((RH, N, N), F32)],
                  compiler_params=_cparams(("arbitrary",)))(q, a, ckpt, dy, *(comm[0] if comm else []))


def _post_fn(y, r, kp, v, zb, ln_w, ln_b, rk, ind, ind_t):
    n = float(RWKV_HEAD_DIM)
    mu = _xdot(_xdot(y, ind, ind_t) / n, ind_t, ind)
    yc = y - mu
    var = _xdot(yc * yc, ind, ind_t) / n
    rstd = _xdot(lax.rsqrt(var + GN_EPS), ind_t, ind)
    yn = yc * rstd * ln_w + ln_b
    bonus = _xdot(_xdot(r * kp * rk, ind, ind_t), ind_t, ind) * v
    return (yn + bonus) * _silu(zb)


def _rwkv_post_fwd(cfg, y, r, kp, v, zb, ln_w, ln_b, rk):
    T, RW, tr = cfg.T, cfg.RW, cfg.tr
    ind, ind_t, _ = _head_indicators(cfg)

    def body(y_ref, r_ref, k_ref, v_ref, z_ref, lw_ref, lb_ref, rk_ref, ind_ref, indt_ref, ob_ref):
        ob_ref[...] = _post_fn(y_ref[...], r_ref[...], k_ref[...], v_ref[...], z_ref[...], lw_ref[...], lb_ref[...],
                               rk_ref[...], ind_ref[...], indt_ref[...]).astype(BF16)

    consts = [ln_w, ln_b, rk, ind, ind_t]
    return _pcall(body, name="rwkv_post_fwd", grid=(T // tr,),
                  in_specs=[_tile(tr, RW)] * 5 + [_const(c.shape) for c in consts],
                  out_specs=_tile(tr, RW), out_shape=jax.ShapeDtypeStruct((T, RW), BF16),
                  compiler_params=_cparams(("parallel",)))(y, r, kp, v, zb, *consts)


def _rwkv_post_bwd(cfg, y, r, kp, v, zb, ln_w, ln_b, rk, dob):
    T, RW = cfg.T, cfg.RW
    tr = min(128, T)
    ind, ind_t, _ = _head_indicators(cfg)

    def body(y_ref, r_ref, k_ref, v_ref, z_ref, lw_ref, lb_ref, rk_ref, ind_ref, indt_ref, dob_ref,
             dy_ref, dr_ref, dk_ref, dv_ref, dz_ref, dlw_ref, dlb_ref, drk_ref):
        fn = functools.partial(_post_fn, ind=ind_ref[...], ind_t=indt_ref[...])
        _, vjp = jax.vjp(fn, y_ref[...], r_ref[...], k_ref[...], v_ref[...], z_ref[...], lw_ref[...], lb_ref[...],
                         rk_ref[...])
        d = vjp(dob_ref[...])
        for ref, val in zip((dy_ref, dr_ref, dk_ref, dv_ref, dz_ref), d[:5]):
            ref[...] = val
        i = pl.program_id(0)
        for ref, val in zip((dlw_ref, dlb_ref, drk_ref), d[5:8]):
            _acc_store(i, ref, val)

    consts = [ln_w, ln_b, rk, ind, ind_t]
    vec = jax.ShapeDtypeStruct((1, RW), F32)
    return _pcall(body, name="rwkv_post_bwd", grid=(T // tr,),
                  in_specs=[_tile(tr, RW)] * 5 + [_const(c.shape) for c in consts] + [_tile(tr, RW)],
                  out_specs=[_tile(tr, RW)] * 5 + [_const((1, RW))] * 3,
                  out_shape=[jax.ShapeDtypeStruct((T, RW), F32)] * 5 + [vec] * 3,
                  compiler_params=_cparams(("arbitrary",)))(y, r, kp, v, zb, *consts, dob)


def _adamw_math(w, g, m, v):
    m = ADAM_B1 * m + (1.0 - ADAM_B1) * g
    v = ADAM_B2 * v + (1.0 - ADAM_B2) * (g * g)
    m_hat = m / (1.0 - ADAM_B1 ** ADAM_STEP)
    v_hat = v / (1.0 - ADAM_B2 ** ADAM_STEP)
    delta = -ADAM_LR * (m_hat / (jnp.sqrt(v_hat) + ADAM_EPS) + ADAM_WD * w)
    return delta, m, v


def _adamw(name, w, g, m, v, copy_grad=False, comm=None):
    R, Cc = w.shape
    Rp = -(-R // 8) * 8
    tr = Rp
    for nb in range(1, Rp // 8 + 1):
        if (Rp // 8) % nb == 0 and (Rp // nb) * Cc * 4 <= 2 * 1024 * 1024:
            tr = Rp // nb
            break

    def body(w_ref, g_ref, m_ref, v_ref, d_ref, nm_ref, nv_ref, *g_out):
        g_v = g_ref[...]
        d, nm, nv = _adamw_math(w_ref[...], g_v, m_ref[...], v_ref[...])
        d_ref[...] = d
        nm_ref[...] = nm
        nv_ref[...] = nv
        if copy_grad:
            g_out[0][...] = g_v

    spec = _tile(tr, Cc)
    n_out = 4 if copy_grad else 3
    body, c_in, c_out, c_shapes, c_scr = _hosted(body, 4, n_out, (Rp // tr,), comm)
    return _pcall(body, name=name, grid=(Rp // tr,), in_specs=[spec] * 4 + c_in, out_specs=[spec] * n_out + c_out,
                  out_shape=[jax.ShapeDtypeStruct((R, Cc), F32)] * n_out + c_shapes, scratch_shapes=c_scr,
                  compiler_params=_cparams(("arbitrary",) if comm else ("parallel",)),
                  )(w, g, m, v, *(comm[0] if comm else []))


def _row_tile(R, Cc, itemsize, budget=2 * 1024 * 1024):
    for nb in range(1, R // 16 + 1):
        if R % nb == 0 and (R // nb) % 16 == 0 and (R // nb) * Cc * itemsize <= budget:
            return R // nb
    return R


def _add_halves(name, gs, r1, c_idx):
    S, R, Cc = gs.shape
    half = R // 2
    tr = _row_tile(half, Cc, 4)
    nb = half // tr

    def body(c_ref, g_ref, r_ref, o_ref):
        o_ref[...] = (g_ref[...].astype(F32) + r_ref[...].astype(F32)).astype(BF16)

    grid_spec = pltpu.PrefetchScalarGridSpec(
        num_scalar_prefetch=1, grid=(S, nb),
        in_specs=[pl.BlockSpec((1, tr, Cc), lambda s, i, c: (s, c[0] * nb + i, 0)),
                  pl.BlockSpec((1, tr, Cc), lambda s, i, c: (s, i, 0))],
        out_specs=pl.BlockSpec((1, tr, Cc), lambda s, i, c: (s, i, 0)))
    return _pcall(body, name=name, grid_spec=grid_spec, out_shape=jax.ShapeDtypeStruct((S, half, Cc), BF16),
                  compiler_params=_cparams(("parallel", "parallel")))(c_idx, gs, r1)


def _sum_slots(name, r2):
    S, R, Cc = r2.shape
    tr = _row_tile(R, Cc, 4 * S // 2 if r2.dtype == BF16 else 4 * S)

    def body(r_ref, o_ref):
        acc = r_ref[0].astype(F32)
        for s in range(1, S):
            acc = acc + r_ref[s].astype(F32)
        o_ref[...] = acc

    return _pcall(body, name=name, grid=(R // tr,), in_specs=[pl.BlockSpec((S, tr, Cc), lambda i: (0, i, 0))],
                  out_specs=_tile(tr, Cc), out_shape=jax.ShapeDtypeStruct((R, Cc), F32),
                  compiler_params=_cparams(("parallel",)))(r2)


def _sum_chips(name, recv, own, place):
    S, H, Cc = recv.shape
    tr = _row_tile(H, Cc, 4, 1024 * 1024)
    nb = H // tr

    def body(p_ref, r_ref, own_ref, o_ref):
        s = pl.program_id(1)
        me = p_ref[0]

        @pl.when(s == 0)
        def _():
            o_ref[...] = jnp.zeros_like(o_ref)

        @pl.when(s == me)
        def _():
            o_ref[...] += own_ref[0].astype(F32)

        @pl.when(s != me)
        def _():
            o_ref[...] += r_ref[0].astype(F32)

    grid_spec = pltpu.PrefetchScalarGridSpec(
        num_scalar_prefetch=1, grid=(nb, S),
        in_specs=[pl.BlockSpec((1, tr, Cc), lambda i, s, p: (jnp.where(s == p[0], (s + 1) % S, s), i, 0)),
                  pl.BlockSpec((1, tr, Cc), lambda i, s, p: (p[0], i, 0))],
        out_specs=pl.BlockSpec((tr, Cc), lambda i, s, p: (p[1] * nb + i, 0)))
    return _pcall(body, name=name, grid_spec=grid_spec, out_shape=jax.ShapeDtypeStruct((2 * H, Cc), F32),
                  compiler_params=_cparams(("parallel", "arbitrary")))(place, recv, own)


def _cast_bf16(name, w):
    R, Cc = w.shape
    tr = _row_tile(R, Cc, 4)

    def body(w_ref, o_ref):
        o_ref[...] = w_ref[...].astype(BF16)

    return _pcall(body, name=name, grid=(R // tr,), in_specs=[_tile(tr, Cc)], out_specs=_tile(tr, Cc),
                  out_shape=jax.ShapeDtypeStruct((R, Cc), BF16), compiler_params=_cparams(("parallel",)))(w)


_ANY = pl.BlockSpec(memory_space=pl.ANY)


def _place():
    x, y, c = lax.axis_index("x"), lax.axis_index("y"), lax.axis_index("c")
    others = [(1 - x, y), (x, 1 - y), (1 - x, 1 - y)]
    return x, y, c, others


def _gather_parts(shards):
    n = len(shards)
    halves = [s.shape[0] // 2 for s in shards]

    def parts(ins, outs, sems):
        x, y, c, _ = _place()
        me = 2 * x + y
        n1 = (x ^ (1 - c), y ^ c)
        n2 = (x ^ c, y ^ (1 - c))
        s1, s2, sd = 2 * n1[0] + n1[1], 2 * n2[0] + n2[1], 2 * (1 - x) + (1 - y)
        sib = (x, y, 1 - c)

        def rows(k, chip, hc):
            return outs[k].at[chip, pl.ds(hc * halves[k], halves[k]), :]

        def remote(k, j, src, dst, to):
            return pltpu.make_async_remote_copy(src_ref=src, dst_ref=dst, send_sem=sems[0].at[7 * k + j],
                                                recv_sem=sems[1].at[7 * k + j], device_id=to, device_id_type=MESH)

        def copy(k, j):
            if j == 6:
                return remote(k, j, ins[k], outs[k].at[me], sib)
            if j < 2:
                mine = ins[k].at[pl.ds(c * halves[k], halves[k]), :]
                return remote(k, j, mine, rows(k, me, c), (*(n1 if j == 0 else n2), c))
            land = rows(k, {2: s1, 3: s1, 4: s2, 5: sd}[j], c)
            return remote(k, j, land, land, (*n2, c) if j == 2 else sib)

        def arrived(k, j):
            hc = c if j < 3 else 1 - c
            land = outs[k].at[me] if j == 6 else rows(k, {0: s1, 1: s2, 2: sd, 3: s2, 4: s1, 5: sd}[j], hc)
            remote(k, j, land, land, (x, y, c)).wait_recv()

        return copy, arrived

    def start(ins, outs, sems):
        copy, _ = parts(ins, outs, sems)
        for k in range(n):
            copy(k, 0).start()
            copy(k, 1).start()
            copy(k, 6).start()

    def middle(ins, outs, sems):
        copy, arrived = parts(ins, outs, sems)
        for k in range(n):
            arrived(k, 0)
            copy(k, 2).start()
            copy(k, 3).start()
            arrived(k, 1)
            copy(k, 4).start()

    def finish(ins, outs, sems):
        copy, arrived = parts(ins, outs, sems)
        for k in range(n):
            arrived(k, 2)
            copy(k, 5).start()
        for k in range(n):
            for j in (3, 4, 5, 6):
                arrived(k, j)
        for k in range(n):
            for j in range(7):
                copy(k, j).wait_send()

    out_shapes = [jax.ShapeDtypeStruct((N_CHIPS,) + s.shape, s.dtype) for s in shards]
    scratch = [pltpu.SemaphoreType.DMA((7 * n,)), pltpu.SemaphoreType.DMA((7 * n,))]
    return list(shards), out_shapes, scratch, start, finish, middle


def _swap_halves(grads):
    n = len(grads)
    halves = [g.shape[1] // 2 for g in grads]

    def copies(ins, outs, sems):
        x, y, c, _ = _place()
        return [pltpu.make_async_remote_copy(
            src_ref=ins[k].at[:, pl.ds((1 - c) * halves[k], halves[k]), :], dst_ref=outs[k], send_sem=sems[0].at[k],
            recv_sem=sems[1].at[k], device_id=(x, y, 1 - c), device_id_type=MESH) for k in range(n)]

    def start(ins, outs, sems):
        for cp in copies(ins, outs, sems):
            cp.start()

    def finish(ins, outs, sems):
        for cp in copies(ins, outs, sems):
            cp.wait()

    out_shapes = [jax.ShapeDtypeStruct((g.shape[0], h) + g.shape[2:], g.dtype) for g, h in zip(grads, halves)]
    scratch = [pltpu.SemaphoreType.DMA((n,)), pltpu.SemaphoreType.DMA((n,))]
    return list(grads), out_shapes, scratch, start, finish


def _scatter_to_owners(chip_sums):
    n = len(chip_sums)

    def sends(ins, outs, sems):
        x, y, c, others = _place()
        me = 2 * x + y
        return [pltpu.make_async_remote_copy(
            src_ref=ins[k].at[2 * px + py], dst_ref=outs[k].at[me], send_sem=sems[0].at[3 * k + j],
            recv_sem=sems[1].at[3 * k + j], device_id=(px, py, c), device_id_type=MESH)
            for k in range(n) for j, (px, py) in enumerate(others)]

    def start(ins, outs, sems):
        for cp in sends(ins, outs, sems):
            cp.start()

    def finish(ins, outs, sems):
        x, y, c, others = _place()
        for k in range(n):
            for j, (px, py) in enumerate(others):
                land = outs[k].at[2 * px + py]
                pltpu.make_async_remote_copy(src_ref=land, dst_ref=land, send_sem=sems[0].at[3 * k + j],
                                             recv_sem=sems[1].at[3 * k + j], device_id=(x, y, c),
                                             device_id_type=MESH).wait_recv()
        for cp in sends(ins, outs, sems):
            cp.wait_send()

    out_shapes = [jax.ShapeDtypeStruct(g.shape, g.dtype) for g in chip_sums]
    scratch = [pltpu.SemaphoreType.DMA((3 * n,)), pltpu.SemaphoreType.DMA((3 * n,))]
    return list(chip_sums), out_shapes, scratch, start, finish


def _swap_with_sibling(arrays):
    n = len(arrays)

    def copies(ins, outs, sems):
        x, y, c, _ = _place()
        return [pltpu.make_async_remote_copy(src_ref=ins[k], dst_ref=outs[k], send_sem=sems[0].at[k],
                                             recv_sem=sems[1].at[k], device_id=(x, y, 1 - c), device_id_type=MESH)
                for k in range(n)]

    def start(ins, outs, sems):
        for cp in copies(ins, outs, sems):
            cp.start()

    def finish(ins, outs, sems):
        for cp in copies(ins, outs, sems):
            cp.wait()

    out_shapes = [jax.ShapeDtypeStruct(a.shape, a.dtype) for a in arrays]
    scratch = [pltpu.SemaphoreType.DMA((n,)), pltpu.SemaphoreType.DMA((n,))]
    return list(arrays), out_shapes, scratch, start, finish


def _add_pair(name, a, b):
    R, Cc = a.shape
    tr = _row_tile(R, Cc, 4)

    def body(a_ref, b_ref, o_ref):
        o_ref[...] = (a_ref[...].astype(F32) + b_ref[...].astype(F32)).astype(BF16)

    return _pcall(body, name=name, grid=(R // tr,), in_specs=[_tile(tr, Cc)] * 2, out_specs=_tile(tr, Cc),
                  out_shape=jax.ShapeDtypeStruct((R, Cc), BF16), compiler_params=_cparams(("parallel",)))(a, b)


def _second_neighbour():
    x, y, c, _ = _place()
    return (x, y, c), (x ^ c, y ^ (1 - c)), (x ^ (1 - c), y ^ c)


def _scatter_stage1(chip_sums):
    n = len(chip_sums)

    def copies(ins, outs, sems):
        (x, y, c), n2, n1 = _second_neighbour()
        diag = 2 * (1 - x) + (1 - y)
        return [pltpu.make_async_remote_copy(
            src_ref=ins[k].at[slot], dst_ref=outs[2 * k + j], send_sem=sems[0].at[2 * k + j],
            recv_sem=sems[1].at[2 * k + j], device_id=(*n2, c), device_id_type=MESH)
            for k in range(n) for j, slot in enumerate((2 * n2[0] + n2[1], diag))]

    def start(ins, outs, sems):
        for cp in copies(ins, outs, sems):
            cp.start()

    def finish(ins, outs, sems):
        for cp in copies(ins, outs, sems):
            cp.wait()

    out_shapes = [jax.ShapeDtypeStruct(g.shape[1:], g.dtype) for g in chip_sums for _ in range(2)]
    scratch = [pltpu.SemaphoreType.DMA((2 * n,)), pltpu.SemaphoreType.DMA((2 * n,))]
    return list(chip_sums), out_shapes, scratch, start, finish


def _scatter_stage2(passed):
    n = len(passed)

    def copies(ins, outs, sems):
        (x, y, c), n2, n1 = _second_neighbour()
        return [pltpu.make_async_remote_copy(src_ref=ins[k], dst_ref=outs[k], send_sem=sems[0].at[k],
                                             recv_sem=sems[1].at[k], device_id=(*n1, c), device_id_type=MESH)
                for k in range(n)]

    def start(ins, outs, sems):
        for cp in copies(ins, outs, sems):
            cp.start()

    def finish(ins, outs, sems):
        for cp in copies(ins, outs, sems):
            cp.wait()

    out_shapes = [jax.ShapeDtypeStruct(p.shape, p.dtype) for p in passed]
    scratch = [pltpu.SemaphoreType.DMA((n,)), pltpu.SemaphoreType.DMA((n,))]
    return list(passed), out_shapes, scratch, start, finish


def _add_passed(name, own, got, slot):
    _, H, Cc = own.shape
    tr = _row_tile(H, Cc, 4)

    def body(s_ref, o_ref, g_ref, out_ref):
        out_ref[...] = (o_ref[0].astype(F32) + g_ref[...].astype(F32)).astype(BF16)

    grid_spec = pltpu.PrefetchScalarGridSpec(
        num_scalar_prefetch=1, grid=(H // tr,),
        in_specs=[pl.BlockSpec((1, tr, Cc), lambda i, s: (s[0], i, 0)), pl.BlockSpec((tr, Cc), lambda i, s: (i, 0))],
        out_specs=pl.BlockSpec((tr, Cc), lambda i, s: (i, 0)))
    return _pcall(body, name=name, grid_spec=grid_spec, out_shape=jax.ShapeDtypeStruct((H, Cc), BF16),
                  compiler_params=_cparams(("parallel",)))(slot, own, got)


def _sum_stages(name, own, direct, via, place, transposed=False):
    _, H, Cc = own.shape
    tr = LANES if transposed else _row_tile(H, Cc, 4, 1024 * 1024)
    nb = H // tr

    def body(p_ref, own_ref, d_ref, v_ref, o_ref):
        acc = (own_ref[0].astype(F32) + d_ref[...].astype(F32)) + v_ref[...].astype(F32)
        o_ref[...] = acc.T if transposed else acc

    flat = pl.BlockSpec((tr, Cc), lambda i, p: (i, 0))
    out_spec = (pl.BlockSpec((Cc, tr), lambda i, p: (0, p[1] * nb + i)) if transposed
                else pl.BlockSpec((tr, Cc), lambda i, p: (p[1] * nb + i, 0)))
    grid_spec = pltpu.PrefetchScalarGridSpec(
        num_scalar_prefetch=1, grid=(nb,),
        in_specs=[pl.BlockSpec((1, tr, Cc), lambda i, p: (p[0], i, 0)), flat, flat], out_specs=out_spec)
    return _pcall(body, name=name, grid_spec=grid_spec,
                  out_shape=jax.ShapeDtypeStruct((Cc, 2 * H) if transposed else (2 * H, Cc), F32),
                  compiler_params=_cparams(("parallel",)))(place, own, direct, via)


def _join_halves(fulls, axes, small):
    n = len(fulls)
    hs = [f.shape[ax] // 2 for f, ax in zip(fulls, axes)]
    rel = [(dx, dy, dc) for dx in (0, 1) for dy in (0, 1) for dc in (0, 1)][1:]

    def half(ref, k, hc):
        part = pl.ds(hc * hs[k], hs[k])
        return ref.at[:, part] if axes[k] else ref.at[part, :]

    def body(*refs):
        ins, small_in = refs[:n], refs[n]
        outs, small_out = refs[n + 1:2 * n + 1], refs[2 * n + 1]
        send_sems, recv_sems, ssend, srecv, local_sem = refs[2 * n + 2:]
        x, y, c, _ = _place()
        dev = 4 * x + 2 * y + c
        local = pltpu.make_async_copy(small_in, small_out.at[dev], local_sem)
        local.start()
        cps = []
        for k in range(n):
            cp = pltpu.make_async_remote_copy(src_ref=half(ins[k], k, c), dst_ref=half(outs[k], k, c),
                                              send_sem=send_sems.at[k], recv_sem=recv_sems.at[k],
                                              device_id=(x, y, 1 - c), device_id_type=MESH)
            cp.start()
            cps.append(cp)
        for r, (dx, dy, dc) in enumerate(rel):
            cp = pltpu.make_async_remote_copy(src_ref=small_in, dst_ref=small_out.at[dev], send_sem=ssend.at[r],
                                              recv_sem=srecv.at[r], device_id=(x ^ dx, y ^ dy, c ^ dc),
                                              device_id_type=MESH)
            cp.start()
            cps.append(cp)
        for k in range(n):
            land = half(outs[k], k, 1 - c)
            pltpu.make_async_remote_copy(src_ref=land, dst_ref=land, send_sem=send_sems.at[k],
                                         recv_sem=recv_sems.at[k], device_id=(x, y, c), device_id_type=MESH).wait_recv()
        for r, (dx, dy, dc) in enumerate(rel):
            land = small_out.at[4 * (x ^ dx) + 2 * (y ^ dy) + (c ^ dc)]
            pltpu.make_async_remote_copy(src_ref=land, dst_ref=land, send_sem=ssend.at[r], recv_sem=srecv.at[r],
                                         device_id=(x, y, c), device_id_type=MESH).wait_recv()
        for cp in cps:
            cp.wait_send()
        local.wait()

    return _pcall(
        body, name="join_halves", in_specs=[_ANY] * (n + 1), out_specs=[_ANY] * (n + 1),
        out_shape=[jax.ShapeDtypeStruct(f.shape, f.dtype) for f in fulls]
        + [jax.ShapeDtypeStruct((N_DEV,) + small.shape, small.dtype)],
        input_output_aliases={k: k for k in range(n)},
        scratch_shapes=[pltpu.SemaphoreType.DMA((n,)), pltpu.SemaphoreType.DMA((n,)), pltpu.SemaphoreType.DMA((7,)),
                        pltpu.SemaphoreType.DMA((7,)), pltpu.SemaphoreType.DMA],
    )(*fulls, small)


def _local_step(cfg, x2, target, norm_gain, w_my, fb, mu_g, w0, a0, k_k, k_a, r_k, ln_w, ln_b, fng, rest,
                exchange=None, h=None):
    T, D, FW, FH, RW, RH, LP, lora = cfg.T, cfg.D, cfg.FW, cfg.FH, cfg.RW, cfg.RH, cfg.LP, cfg.lora
    fb_p = jnp.pad(fb, ((0, 0), (0, LANES - FH)))
    mu = _rwkv_vec_to_my(cfg, mu_g)
    rk = r_k.reshape(1, RW)
    tm = min(1024, T)

    if h is None:
        h = _rms_fwd(cfg, x2, norm_gain)
    if len(rest) == 2:
        u, *got = _mm("in_proj", h, w_my, "nn", F32, tm, cfg.tn, 2048, comm=rest[0])
        rest = rest[1](got)
    else:
        u = _mm("in_proj", h, w_my, "nn", F32, tm, cfg.tn, 2048)
    w2, a2, wpf, wpr, wout = rest
    w2p = jnp.pad(w2, ((0, LP - lora), (0, 0)))
    a2p = jnp.pad(a2, ((0, LP - lora), (0, 0)))
    c_cols = _fox_prep(cfg, u, fb_p)
    c_rows = c_cols[:, :FH].T.reshape(FH, 1, T)
    o, lse = _attn_fwd(cfg, u, c_rows)
    oa = _gate_a_fwd(cfg, o, u)
    prep = _rwkv_prep_fwd(cfg, u, mu, w0, w2p, a0, a2p, k_k, k_a)
    r, lw, kp, v, an, b, zb = prep
    toks = [r, lw, kp, v, an, b]
    y, q_s, a_m, ckpt = _scan_fwd(cfg, toks)
    ob = _rwkv_post_fwd(cfg, y, r, kp, v, zb, ln_w, ln_b, rk)
    pa = _mm("proj_fox", oa, wpf, "nn", F32, tm, 1024, 2048)
    pb = _mm("proj_rwkv", ob, wpr, "nn", F32, tm, 1024, 2048)
    m = _merge_fwd(cfg, pa, pb, u)
    mo = _mm("out_proj", m, wout, "nn", F32, tm, 1024, 2048)
    loss8, dres, dres16, d_fng = _final(cfg, x2, mo, fng.reshape(1, D), target)

    dm = _mm("out_proj_dx", dres16, wout, "nt", F32, tm, 1024, 2048)
    d_wout = _mm("out_proj_dw", m, dres16, "tn", BF16, 1024, 1024, 2048)
    dpa, dpb, du = _merge_bwd(cfg, pa, pb, u, dm)
    doa = _mm("proj_fox_dx", dpa, wpf, "nt", F32, tm, 1024, 2048)
    d_wpf = _mm("proj_fox_dw", oa, dpa, "tn", BF16, 1024, 1024, 2048)
    dob = _mm("proj_rwkv_dx", dpb, wpr, "nt", F32, tm, 1024, 2048)
    d_wpr = _mm("proj_rwkv_dw", ob, dpb, "tn", BF16, 1024, 1024, 2048)

    do, du = _gate_a_bwd(cfg, o, u, doa, du)
    du, dcol = _attn_bwd(cfg, u, c_rows, lse, do, du)
    dc = jnp.pad(-dcol.reshape(FH, T).T, ((0, 0), (0, LANES - FH)))
    df, d_fb = _fox_prep_bwd(cfg, u, fb_p, dc)

    dy, dr_p, dk_p, dv_p, dzb, d_lnw, d_lnb, d_rk = _rwkv_post_bwd(cfg, y, r, kp, v, zb, ln_w, ln_b, rk, dob)
    early = dict(w_proj_fox=d_wpf, w_proj_rwkv=d_wpr, w_out=d_wout)
    res = _scan_carry_bwd(cfg, q_s, a_m, ckpt, dy, exchange(early) if exchange else None)
    dq_s, da_m, dsl = res[:3]
    res = _scan_local_bwd(cfg, toks, dq_s, dy, da_m, dsl, [dr_p, dk_p, dv_p],
                          exchange(("swapped", list(res[3:]))) if exchange else None)
    cots, received = res[:6], list(res[6:])
    dus, d_mu, d_w0, d_w2p, d_a0, d_a2p, d_kk, d_ka = _rwkv_prep_bwd(cfg, u, mu, w0, w2p, a0, a2p, k_k, k_a, cots, dzb)
    du = _shift_bwd(cfg, dus, mu, df, du)
    if exchange:
        late = dict(w_in=exchange((h, du, d_w2p[:lora], d_a2p[:lora])))
    else:
        late = dict(w_in=_mm("in_proj_dw", h, du, "tn", BF16, 1024, cfg.tn, 2048), rwkv_w2=d_w2p[:lora],
                    rwkv_a2=d_a2p[:lora])
    tkx = 2 * cfg.tn if cfg.ncol % (2 * cfg.tn) == 0 else cfg.tn
    res = _mm("in_proj_dx", du, w_my, "nt", F32, tm, 1024, tkx, comm=exchange(late) if exchange else None)
    dh = res[0] if exchange else res
    big = dict(early, **late)
    res = _rms_bwd(cfg, x2, norm_gain, dh, dres, exchange(list(res[1:])) if exchange else None)
    gx, d_ng = res[:2]
    received += list(res[2:])

    small = dict(norm_gain=d_ng, fox_forget_bias=d_fb[:, :FH], rwkv_shift_mix=_rwkv_vec_from_my(cfg, d_mu),
                 rwkv_w0=d_w0, rwkv_a0=d_a0, rwkv_k_k=d_kk, rwkv_k_a=d_ka, rwkv_r_k=d_rk, rwkv_ln_w=d_lnw,
                 rwkv_ln_b=d_lnb, final_norm_gain=d_fng)
    return loss8[0, 0], gx, small, big, received


_SMALL = ["norm_gain", "fox_forget_bias", "rwkv_shift_mix", "rwkv_w0", "rwkv_a0", "rwkv_k_k", "rwkv_k_a", "rwkv_r_k",
          "rwkv_ln_w", "rwkv_ln_b", "final_norm_gain"]
_WEIGHTS = ["norm_gain", "w_in", "fox_forget_bias", "rwkv_shift_mix", "rwkv_w0", "rwkv_w2", "rwkv_a0", "rwkv_a2",
            "rwkv_k_k", "rwkv_k_a", "rwkv_r_k", "rwkv_ln_w", "rwkv_ln_b", "w_proj_fox", "w_proj_rwkv", "w_out",
            "final_norm_gain"]


def _pack_small(arrs):
    parts, n = [], 0
    for a in arrs:
        f = a.reshape(-1)
        fill = (-f.shape[0]) % LANES
        parts += [f] + ([jnp.zeros((fill,), f.dtype)] if fill else [])
        n += f.shape[0] + fill
    tail = ((-(n // LANES)) % 8) * LANES
    return jnp.concatenate(parts + ([jnp.zeros((tail,), parts[0].dtype)] if tail else [])).reshape(-1, LANES)


def _unpack_small(packed, shapes):
    flat = packed.reshape(-1)
    out, pos = [], 0
    for s in shapes:
        n = int(np.prod(s))
        out.append(flat[pos:pos + n].reshape(s))
        pos += n + ((-n) % LANES)
    return out


def _shard_major(a, axis):
    parts = jnp.split(a, N_CHIPS, axis=axis)
    return jnp.stack(parts, axis=0)


def kernel(x, norm_gain, w_in, fox_forget_bias, rwkv_shift_mix, rwkv_w0, rwkv_w2, rwkv_a0, rwkv_a2, rwkv_k_k, rwkv_k_a, rwkv_r_k, rwkv_ln_w, rwkv_ln_b, w_proj_fox, w_proj_rwkv, w_out, final_norm_gain, loss_target, m_norm_gain, m_w_in, m_fox_forget_bias, m_rwkv_shift_mix, m_rwkv_w0, m_rwkv_w2, m_rwkv_a0, m_rwkv_a2, m_rwkv_k_k, m_rwkv_k_a, m_rwkv_r_k, m_rwkv_ln_w, m_rwkv_ln_b, m_w_proj_fox, m_w_proj_rwkv, m_w_out, m_final_norm_gain, v_norm_gain, v_w_in, v_fox_forget_bias, v_rwkv_shift_mix, v_rwkv_w0, v_rwkv_w2, v_rwkv_a0, v_rwkv_a2, v_rwkv_k_k, v_rwkv_k_a, v_rwkv_r_k, v_rwkv_ln_w, v_rwkv_ln_b, v_w_proj_fox, v_w_proj_rwkv, v_w_out, v_final_norm_gain):
    args = dict(locals())
    T, D = x.shape[1], x.shape[2]
    lora = rwkv_w2.shape[1]
    cfg = _Cfg(T, D, lora)
    RW = cfg.RW
    c_idx = lax.axis_index("c").astype(jnp.int32).reshape(1)
    me_chip = (2 * lax.axis_index("x") + lax.axis_index("y")).astype(jnp.int32)
    place = jnp.concatenate([me_chip.reshape(1), c_idx])

    w_in_s = w_in[0].astype(BF16)
    lora_s = jnp.concatenate([rwkv_w2[0], rwkv_a2[0]], axis=0)
    h, g_in = _rms_fwd(cfg, x[0], norm_gain, _gather_parts([w_in_s]))
    w_my = _shards_to_my_layout(cfg, g_in)
    mine = [_cast_bf16("cast_w_proj_fox", w_proj_fox[0]), _cast_bf16("cast_w_proj_rwkv", w_proj_rwkv[0]),
            _cast_bf16("cast_w_out", w_out[0]), lora_s]

    def unpack(gathered):
        g_wpf, g_wpr, g_out, g_lora = gathered
        lo = g_lora.transpose(1, 0, 2).reshape(2 * lora, RW)
        return (lo[:lora], lo[lora:], g_wpf.transpose(1, 0, 2).reshape(RW, D),
                g_wpr.transpose(1, 0, 2).reshape(RW, D), g_out.reshape(D, D))

    early, late = ["w_proj_fox", "w_proj_rwkv", "w_out"], ["w_in", "lora"]
    names = early + late
    chip_sums, direct, shard_major = {}, {}, []
    n1_slot = (2 * (lax.axis_index("x") ^ (1 - lax.axis_index("c")))
               + (lax.axis_index("y") ^ lax.axis_index("c"))).astype(jnp.int32).reshape(1)

    def exchange(got):
        if isinstance(got, tuple) and len(got) == 4:
            h, du, d_w2, d_a2 = got
            c, half = lax.axis_index("c"), D // 2
            cols = lambda base: lax.dynamic_slice_in_dim(h, base * half, half, axis=1)
            lora_g = _shard_major(jnp.concatenate([d_w2, d_a2], axis=0).astype(BF16), 1)
            lora_rows = lambda base: lax.dynamic_slice_in_dim(lora_g, base * lora, lora, axis=1).reshape(-1, RW // 4)
            tiles = (BF16, min(1024, half), cfg.tn, 2048)
            sent = _mm("in_proj_dw_sibling", cols(1 - c), du, "tn", *tiles)
            kept, got_w, got_l = _mm("in_proj_dw", cols(c), du, "tn", *tiles,
                                     comm=_swap_with_sibling([sent, lora_rows(1 - c)]))
            return (_add_pair("add_halves_w_in", kept, got_w),
                    _add_pair("add_halves_lora", lora_rows(c), got_l).reshape(N_CHIPS, lora, RW // 4))
        if isinstance(got, dict):
            if "w_in" in got:
                sums = [_my_layout_to_shards(cfg, got["w_in"][0]), got["w_in"][1]]
                chip_sums.update(zip(late, sums))
                return _scatter_stage1(sums)
            shard_major.extend([_shard_major(got["w_proj_fox"], 1), _shard_major(got["w_proj_rwkv"], 1),
                                _shard_major(got["w_out"], 0)])
            return _swap_halves(shard_major)
        if got[0] == "swapped":
            sums = [_add_halves("add_halves_" + nm, g, r, c_idx) for nm, g, r in zip(early, shard_major, got[1])]
            chip_sums.update(zip(early, sums))
            return _scatter_to_owners(sums)
        direct.update(zip(late, got[0::2]))
        return _scatter_stage2([_add_passed("add_passed_" + nm, chip_sums[nm], g, n1_slot)
                                for nm, g in zip(late, got[1::2])])

    loss_dev, gx, small, _, recv2 = _local_step(
        cfg, x[0], loss_target[0], norm_gain, w_my, fox_forget_bias, rwkv_shift_mix, rwkv_w0, rwkv_a0, rwkv_k_k,
        rwkv_k_a, rwkv_r_k, rwkv_ln_w, rwkv_ln_b, final_norm_gain, (_gather_parts(mine), unpack), exchange, h)
    loss = lax.psum(loss_dev, ("x", "y", "c"))

    small_shapes = [args[nm].shape for nm in _SMALL]
    packed = _pack_small([small[nm] for nm in _SMALL])
    reduced = [_sum_chips("sum_chips_" + nm, r, chip_sums[nm], place) for nm, r in zip(early, recv2[:3])]
    reduced += [_sum_stages("sum_stages_" + nm, chip_sums[nm], direct[nm], via, place, transposed=nm == "w_in")
                for nm, via in zip(late, recv2[3:])]
    *joined, small_all = _join_halves(reduced, [int(nm == "w_in") for nm in names], packed)
    g_small = _sum_slots("sum_small", small_all)

    grads = dict(zip(_SMALL, _unpack_small(g_small, small_shapes)))
    grads.update({nm: g[None] for nm, g in zip(names, joined) if nm not in ("lora", "w_in")})
    g_lora_f = joined[names.index("lora")]
    grads["rwkv_w2"] = g_lora_f[None, :lora]
    grads["rwkv_a2"] = g_lora_f[None, lora:]

    delta, new_m, new_v = {}, {}, {}
    w_small = _pack_small([args[nm] for nm in _SMALL])
    m_small = _pack_small([args["m_" + nm] for nm in _SMALL])
    v_small = _pack_small([args["v_" + nm] for nm in _SMALL])
    d_s, m_s, v_s = _adamw("adamw_small", w_small, g_small, m_small, v_small)
    for tgt, pk in ((delta, d_s), (new_m, m_s), (new_v, v_s)):
        tgt.update(zip(_SMALL, _unpack_small(pk, small_shapes)))
    t_out = _adamw("adamw_w_in", w_in[0].T, joined[names.index("w_in")], m_w_in[0].T, v_w_in[0].T, copy_grad=True)
    delta["w_in"], new_m["w_in"], new_v["w_in"], grads["w_in"] = [t.T[None] for t in t_out]
    for nm in ("w_proj_fox", "w_proj_rwkv", "w_out", "rwkv_w2", "rwkv_a2"):
        shp = args[nm].shape
        two_d = (shp[1], shp[2])
        d_b, m_b, v_b = _adamw("adamw_" + nm, args[nm].reshape(two_d), grads[nm].reshape(two_d),
                               args["m_" + nm].reshape(two_d), args["v_" + nm].reshape(two_d))
        delta[nm], new_m[nm], new_v[nm] = d_b.reshape(shp), m_b.reshape(shp), v_b.reshape(shp)

    return (loss, gx[None], *[grads[n] for n in _WEIGHTS], *[delta[n] for n in _WEIGHTS],
            *[new_m[n] for n in _WEIGHTS], *[new_v[n] for n in _WEIGHTS])
```

```python
import functools

import numpy as np
import jax
import jax.numpy as jnp
from jax import lax
from jax.experimental import pallas as pl
from jax.experimental.pallas import tpu as pltpu

F32 = jnp.float32
BF16 = jnp.bfloat16
HI = lax.Precision.HIGHEST
MESH = pl.DeviceIdType.MESH

FOX_HEAD_DIM = 128
RWKV_HEAD_DIM = 64
RMS_EPS = 1e-6
GN_EPS = 64e-5
L2_EPS = 1e-12
ADAM_LR = 0.001
ADAM_B1 = 0.9
ADAM_B2 = 0.999
ADAM_EPS = 1e-08
ADAM_WD = 0.01
ADAM_STEP = 10

LANES = 128
VMEM_LIMIT = 56 * 1024 * 1024
SCAN_CHUNK = 64
SCAN_HEADS_PER_STEP = 16
SCAN_CHUNKS_PER_STEP = 2
SCAN_PASSES = ((3, 1), 1, 1)
N_CHIPS = 4
N_DEV = 8

_pcall = pl.pallas_call


def _cparams(sem=None):
    return pltpu.CompilerParams(dimension_semantics=sem, vmem_limit_bytes=VMEM_LIMIT)


def _softplus(x):
    return jnp.maximum(x, 0.0) + jnp.log(1.0 + jnp.exp(-jnp.abs(x)))


def _silu(z):
    return z * jax.nn.sigmoid(z)


def _rmsn(x, g):
    return x * lax.rsqrt(jnp.mean(x * x, axis=-1, keepdims=True) + RMS_EPS) * g


def _dot(a, b, dims="nn", precision=None):
    dn = {"nn": (((1,), (0,)), ((), ())), "nt": (((1,), (1,)), ((), ())), "tn": (((0,), (0,)), ((), ()))}[dims]
    return lax.dot_general(a, b, dn, precision=precision, preferred_element_type=F32)


def _split_bf16(x):
    hi = x.astype(BF16)
    return hi, (x - hi.astype(F32)).astype(BF16)


def _bdot_raw(a, b, ca, cb, passes):
    dn = (((ca,), (cb,)), ((0,), (0,)))
    mm = lambda p, q: lax.dot_general(p, q, dn, preferred_element_type=F32)
    passes = passes[0] if isinstance(passes, tuple) else passes
    if passes == 1:
        return mm(a.astype(BF16), b.astype(BF16))
    ah, al = _split_bf16(a)
    bh, bl = _split_bf16(b)
    return mm(ah, bh) + (mm(ah, bl) + mm(al, bh))


@functools.partial(jax.custom_vjp, nondiff_argnums=(2, 3, 4))
def _bdot_p(a, b, ca, cb, passes):
    return _bdot_raw(a, b, ca, cb, passes)


def _bdot_fwd(a, b, ca, cb, passes):
    return _bdot_raw(a, b, ca, cb, passes), (a, b)


def _bdot_bwd(ca, cb, passes, res, g):
    a, b = res
    passes = passes[1] if isinstance(passes, tuple) else passes
    if (ca, cb) == (2, 1):
        return _bdot_p(g, b, 2, 2, passes), _bdot_p(a, g, 1, 1, passes)
    if (ca, cb) == (2, 2):
        return _bdot_p(g, b, 2, 1, passes), _bdot_p(g, a, 1, 1, passes)
    assert (ca, cb) == (1, 1)
    return _bdot_p(b, g, 2, 2, passes), _bdot_p(a, g, 2, 1, passes)


_bdot_p.defvjp(_bdot_fwd, _bdot_bwd)


def _bdot(a, b, ca, cb, passes=3):
    return _bdot_p(a, b, ca, cb, passes)


def _dot3(a, b):
    return _bdot(a[None], b[None], 2, 1)[0]


@jax.custom_vjp
def _xdot(x, m, mt):
    hi, lo = _split_bf16(x)
    m16 = m.astype(BF16)
    return _dot(hi, m16) + _dot(lo, m16)


def _xdot_fwd(x, m, mt):
    return _xdot(x, m, mt), (m, mt)


def _xdot_bwd(res, g):
    m, mt = res
    return _xdot(g, mt, m), jnp.zeros_like(m), jnp.zeros_like(mt)


_xdot.defvjp(_xdot_fwd, _xdot_bwd)


class _Cfg:
    def __init__(self, T, D, lora):
        self.T, self.D, self.lora = T, D, lora
        self.FW = D // 2
        self.FH = self.FW // FOX_HEAD_DIM
        self.RW = D // 2
        self.RH = self.RW // RWKV_HEAD_DIM
        self.LP = -(-lora // LANES) * LANES
        self.o_fox = 0
        self.o_rwkv = 4 * self.FW
        self.o_gate = self.o_rwkv + 4 * self.RW
        self.o_f = self.o_gate + 2 * D
        self.o_wd = self.o_f + LANES
        self.o_ad = self.o_wd + self.LP
        end = self.o_ad + self.LP
        self.tn = 1280 if D >= 2048 else LANES
        self.ncol = -(-end // self.tn) * self.tn
        self.in_cols = 4 * self.FW + self.FH + 4 * self.RW + 2 * lora + 2 * D
        self.scp = -(-(self.in_cols // N_CHIPS) // LANES) * LANES
        self.rseg = 4 * self.RW + 2 * self.LP
        self.C = min(SCAN_CHUNK, T)
        self.tr = min(256, T)
        self.hb = min(SCAN_HEADS_PER_STEP, self.RH)
        self.cb = SCAN_CHUNKS_PER_STEP if (T // self.C) % SCAN_CHUNKS_PER_STEP == 0 else 1

    def segments(self):
        FW, FH, RW, lo, D = self.FW, self.FH, self.RW, self.lora, self.D
        g_f = 4 * FW
        g_r = g_f + FH
        g_wd = g_r + 4 * RW
        g_ad = g_wd + lo
        g_g = g_ad + lo
        dh = FOX_HEAD_DIM
        qkv = [(j * FW + h * dh, dh, (3 * h + j) * dh) for h in range(FH) for j in range(3)]
        return qkv + [(3 * FW, FW, 3 * FW), (g_f, FH, self.o_f), (g_r, 4 * RW, self.o_rwkv), (g_wd, lo, self.o_wd),
                      (g_ad, lo, self.o_ad), (g_g, 2 * D, self.o_gate)]


def _shards_to_my_layout(cfg, g):
    R, sc = g.shape[1], g.shape[2]
    segs = sorted(cfg.segments(), key=lambda s: s[2])
    parts, pos = [], 0
    for g0, w, m0 in segs:
        if m0 > pos:
            parts.append(jnp.zeros((R, m0 - pos), g.dtype))
        for s in range(N_CHIPS):
            lo, hi = max(g0, s * sc), min(g0 + w, (s + 1) * sc)
            if lo < hi:
                parts.append(g[s, :, lo - s * sc:hi - s * sc])
        pos = m0 + w
    if cfg.ncol > pos:
        parts.append(jnp.zeros((R, cfg.ncol - pos), g.dtype))
    return jnp.concatenate(parts, axis=1)


def _my_layout_to_shards(cfg, wm):
    sc, R = cfg.in_cols // N_CHIPS, wm.shape[0]
    segs = sorted(cfg.segments(), key=lambda s: s[0])
    shards = []
    for s in range(N_CHIPS):
        parts = []
        for g0, w, m0 in segs:
            lo, hi = max(g0, s * sc), min(g0 + w, (s + 1) * sc)
            if lo < hi:
                parts.append(wm[:, m0 + lo - g0:m0 + hi - g0])
        parts.append(jnp.zeros((R, cfg.scp - sc), wm.dtype))
        shards.append(jnp.concatenate(parts, axis=1))
    return jnp.stack(shards, axis=0)


def _rwkv_vec_to_my(cfg, v):
    RW4, lo, LP = 4 * cfg.RW, cfg.lora, cfg.LP
    z = jnp.zeros((1, LP - lo), v.dtype)
    return jnp.concatenate([v[:, :RW4], v[:, RW4:RW4 + lo], z, v[:, RW4 + lo:], z], axis=1)


def _rwkv_vec_from_my(cfg, v):
    RW4, lo, LP = 4 * cfg.RW, cfg.lora, cfg.LP
    return jnp.concatenate([v[:, :RW4], v[:, RW4:RW4 + lo], v[:, RW4 + LP:RW4 + LP + lo]], axis=1)


def _comm_at(comm, which, steps, cin, cout, scr):
    if not comm or len(comm) <= which:
        return
    lin, total = 0, 1
    for d, n in enumerate(steps):
        lin = lin * n + pl.program_id(d)
        total *= n
    pl.when(lin == {3: 0, 4: total - 1, 5: total // 2}[which])(lambda: comm[which](cin, cout, scr))


def _hosted(body, n_in, n_out, steps, comm):
    if not comm:
        return body, [], [], [], []
    ci, co, cs = len(comm[0]), len(comm[1]), len(comm[2])

    def wrapped(*refs):
        ins, cin = refs[:n_in], refs[n_in:n_in + ci]
        outs, cout = refs[n_in + ci:n_in + ci + n_out], refs[n_in + ci + n_out:n_in + ci + n_out + co]
        cscr, scr = refs[n_in + ci + n_out + co:n_in + ci + n_out + co + cs], refs[n_in + ci + n_out + co + cs:]
        _comm_at(comm, 3, steps, cin, cout, cscr)
        body(*ins, *outs, *scr)
        _comm_at(comm, 5, steps, cin, cout, cscr)
        _comm_at(comm, 4, steps, cin, cout, cscr)

    return wrapped, [_ANY] * ci, [_ANY] * co, list(comm[1]), list(comm[2])


def _mm(name, a, b, dims, out_dtype, tm, tn, tk, comm=None):
    (M, K) = a.shape if dims != "tn" else a.shape[::-1]
    N = b.shape[0] if dims == "nt" else b.shape[1]
    tm, tn, tk = min(tm, M), min(tn, N), min(tk, K)
    assert M % tm == 0 and N % tn == 0 and K % tk == 0, (name, M, N, K, tm, tn, tk)
    nk = K // tk
    steps = (M // tm, N // tn, nk)
    c_in, c_out, c_scr = comm[:3] if comm else ([], [], [])
    if dims == "nn":
        a_spec = pl.BlockSpec((tm, tk), lambda i, j, k: (i, k))
        b_spec = pl.BlockSpec((tk, tn), lambda i, j, k: (k, j))
    elif dims == "nt":
        a_spec = pl.BlockSpec((tm, tk), lambda i, j, k: (i, k))
        b_spec = pl.BlockSpec((tn, tk), lambda i, j, k: (j, k))
    else:
        a_spec = pl.BlockSpec((tk, tm), lambda i, j, k: (k, i))
        b_spec = pl.BlockSpec((tk, tn), lambda i, j, k: (k, j))

    n_acc = 1 if nk > 1 else 0

    def body(a_ref, b_ref, *rest):
        cin, o_ref = rest[:len(c_in)], rest[len(c_in)]
        cout = rest[len(c_in) + 1:len(c_in) + 1 + len(c_out)]
        scr = rest[len(c_in) + 1 + len(c_out):]
        _comm_at(comm, 3, steps, cin, cout, scr[n_acc:])
        if nk == 1:
            o_ref[...] = _dot(a_ref[...], b_ref[...], dims).astype(o_ref.dtype)
        else:
            acc_ref, k = scr[0], pl.program_id(2)

            @pl.when(k == 0)
            def _():
                acc_ref[...] = jnp.zeros_like(acc_ref)

            acc_ref[...] += _dot(a_ref[...], b_ref[...], dims)

            @pl.when(k == nk - 1)
            def _():
                o_ref[...] = acc_ref[...].astype(o_ref.dtype)

        _comm_at(comm, 5, steps, cin, cout, scr[n_acc:])
        _comm_at(comm, 4, steps, cin, cout, scr[n_acc:])

    res = _pcall(
        body, name=name, grid=steps,
        in_specs=[a_spec, b_spec] + [_ANY] * len(c_in),
        out_specs=[pl.BlockSpec((tm, tn), lambda i, j, k: (i, j))] + [_ANY] * len(c_out),
        out_shape=[jax.ShapeDtypeStruct((M, N), out_dtype)] + list(c_out),
        scratch_shapes=([pltpu.VMEM((tm, tn), F32)] if nk > 1 else []) + list(c_scr),
        compiler_params=_cparams(("arbitrary",) * 3 if comm else ("parallel", "parallel", "arbitrary")),
    )(a, b, *c_in)
    return res if comm else res[0]


def _tile(tr, w, cb=0):
    return pl.BlockSpec((tr, w), lambda i: (i, cb))


def _const(shape):
    nd = len(shape)
    return pl.BlockSpec(shape, lambda i: (0,) * nd)


def _acc_store(i, ref, val):
    @pl.when(i == 0)
    def _():
        ref[...] = val

    @pl.when(i > 0)
    def _():
        ref[...] += val


def _rms_fwd(cfg, x2, g, comm=None):
    T, D, tr = cfg.T, cfg.D, cfg.tr
    steps = (T // tr,)

    def body(x_ref, g_ref, h_ref):
        h_ref[...] = _rmsn(x_ref[...], g_ref[...]).astype(BF16)

    body, c_in, c_out, c_shapes, c_scr = _hosted(body, 2, 1, steps, comm)
    res = _pcall(body, name="rms_fwd", grid=steps, in_specs=[_tile(tr, D), _const((1, D))] + c_in,
                 out_specs=[_tile(tr, D)] + c_out, out_shape=[jax.ShapeDtypeStruct((T, D), BF16)] + c_shapes,
                 scratch_shapes=c_scr, compiler_params=_cparams(("arbitrary",) if comm else ("parallel",)),
                 )(x2, g, *(comm[0] if comm else []))
    return res if comm else res[0]


def _rms_bwd(cfg, x2, g, dh, dres, comm=None):
    T, D, tr = cfg.T, cfg.D, cfg.tr
    c_in, c_out, c_scr = comm[:3] if comm else ([], [], [])
    steps = (T // tr,)

    def body(x_ref, g_ref, dh_ref, dres_ref, *rest):
        cin, (gx_ref, dg_ref) = rest[:len(c_in)], rest[len(c_in):len(c_in) + 2]
        cout, scr = rest[len(c_in) + 2:len(c_in) + 2 + len(c_out)], rest[len(c_in) + 2 + len(c_out):]
        _comm_at(comm, 3, steps, cin, cout, scr)
        _, vjp = jax.vjp(_rmsn, x_ref[...], g_ref[...])
        dx, dg = vjp(dh_ref[...])
        gx_ref[...] = dx + dres_ref[...]
        _acc_store(pl.program_id(0), dg_ref, dg)
        _comm_at(comm, 4, steps, cin, cout, scr)

    return _pcall(body, name="rms_bwd", grid=steps,
                  in_specs=[_tile(tr, D), _const((1, D)), _tile(tr, D), _tile(tr, D)] + [_ANY] * len(c_in),
                  out_specs=[_tile(tr, D), _const((1, D))] + [_ANY] * len(c_out),
                  out_shape=[jax.ShapeDtypeStruct((T, D), F32), jax.ShapeDtypeStruct((1, D), F32)] + list(c_out),
                  scratch_shapes=list(c_scr), compiler_params=_cparams(("arbitrary",)))(x2, g, dh, dres, *c_in)


def _final(cfg, x2, mo, fg, target):
    T, D, tr = cfg.T, cfg.D, cfg.tr

    def loss_fn(hres, g, tgt):
        err = _rmsn(hres, g) - tgt
        return 0.5 * jnp.sum(jnp.mean(err * err, axis=-1, keepdims=True), axis=0, keepdims=True)

    def body(x_ref, mo_ref, g_ref, t_ref, loss_ref, dres_ref, dres16_ref, dg_ref):
        hres = x_ref[...] + mo_ref[...]
        loss, vjp = jax.vjp(functools.partial(loss_fn, tgt=t_ref[...]), hres, g_ref[...])
        dres, dg = vjp(jnp.ones((1, 1), F32))
        dres_ref[...] = dres
        dres16_ref[...] = dres.astype(BF16)
        i = pl.program_id(0)
        _acc_store(i, dg_ref, dg)
        _acc_store(i, loss_ref, jnp.broadcast_to(loss, (8, LANES)))

    return _pcall(body, name="final_loss", grid=(T // tr,),
                  in_specs=[_tile(tr, D), _tile(tr, D), _const((1, D)), _tile(tr, D)],
                  out_specs=[_const((8, LANES)), _tile(tr, D), _tile(tr, D), _const((1, D))],
                  out_shape=[jax.ShapeDtypeStruct((8, LANES), F32), jax.ShapeDtypeStruct((T, D), F32),
                             jax.ShapeDtypeStruct((T, D), BF16), jax.ShapeDtypeStruct((1, D), F32)],
                  compiler_params=_cparams(("arbitrary",)))(x2, mo, fg, target)


def _merge_fn(pa, pb, ga, gb):
    return jax.nn.sigmoid(ga) * pa + jax.nn.sigmoid(gb) * pb


def _merge_fwd(cfg, pa, pb, u):
    T, D, tr = cfg.T, cfg.D, cfg.tr
    cga, cgb = cfg.o_gate // D, cfg.o_gate // D + 1

    def body(pa_ref, pb_ref, ga_ref, gb_ref, m_ref):
        m_ref[...] = _merge_fn(pa_ref[...], pb_ref[...], ga_ref[...], gb_ref[...]).astype(BF16)

    return _pcall(body, name="merge_fwd", grid=(T // tr,),
                  in_specs=[_tile(tr, D), _tile(tr, D), _tile(tr, D, cga), _tile(tr, D, cgb)],
                  out_specs=_tile(tr, D), out_shape=jax.ShapeDtypeStruct((T, D), BF16),
                  compiler_params=_cparams(("parallel",)))(pa, pb, u, u)


def _merge_bwd(cfg, pa, pb, u, dm):
    T, D, tr = cfg.T, cfg.D, cfg.tr
    cga, cgb = cfg.o_gate // D, cfg.o_gate // D + 1

    def body(pa_ref, pb_ref, ga_ref, gb_ref, dm_ref, dpa_ref, dpb_ref, dg_ref):
        _, vjp = jax.vjp(_merge_fn, pa_ref[...], pb_ref[...], ga_ref[...], gb_ref[...])
        dpa, dpb, dga, dgb = vjp(dm_ref[...])
        dpa_ref[...] = dpa.astype(BF16)
        dpb_ref[...] = dpb.astype(BF16)
        dg_ref[:, :D] = dga.astype(BF16)
        dg_ref[:, D:] = dgb.astype(BF16)

    return _pcall(body, name="merge_bwd", grid=(T // tr,),
                  in_specs=[_tile(tr, D), _tile(tr, D), _tile(tr, D, cga), _tile(tr, D, cgb), _tile(tr, D)],
                  out_specs=[_tile(tr, D), _tile(tr, D), _tile(tr, 2 * D, cfg.o_gate // (2 * D))],
                  out_shape=[jax.ShapeDtypeStruct((T, D), BF16), jax.ShapeDtypeStruct((T, D), BF16),
                             jax.ShapeDtypeStruct((T, cfg.ncol), BF16)],
                  compiler_params=_cparams(("parallel",)))(pa, pb, u, u, dm)


def _gate_fn(o, z):
    return o * _silu(z)


def _gate_a_fwd(cfg, o, u):
    T, FW, tr = cfg.T, cfg.FW, cfg.tr

    def body(o_ref, z_ref, oa_ref):
        oa_ref[...] = _gate_fn(o_ref[...], z_ref[...]).astype(BF16)

    return _pcall(body, name="gate_a_fwd", grid=(T // tr,), in_specs=[_tile(tr, FW), _tile(tr, FW, 3)],
                  out_specs=_tile(tr, FW), out_shape=jax.ShapeDtypeStruct((T, FW), BF16),
                  compiler_params=_cparams(("parallel",)))(o, u)


def _gate_a_bwd(cfg, o, u, doa, du):
    T, FW, tr = cfg.T, cfg.FW, cfg.tr

    def body(o_ref, z_ref, doa_ref, du_in, do_ref, dz_ref):
        _, vjp = jax.vjp(_gate_fn, o_ref[...], z_ref[...])
        do, dz = vjp(doa_ref[...])
        do_ref[...] = do
        dz_ref[...] = dz.astype(BF16)

    return _pcall(body, name="gate_a_bwd", grid=(T // tr,),
                  in_specs=[_tile(tr, FW), _tile(tr, FW, 3), _tile(tr, FW), _ANY],
                  out_specs=[_tile(tr, FW), _tile(tr, FW, 3)],
                  out_shape=[jax.ShapeDtypeStruct((T, FW), F32), jax.ShapeDtypeStruct(du.shape, BF16)],
                  input_output_aliases={3: 1},
                  compiler_params=_cparams(("parallel",)))(o, u, doa, du)


def _fox_prep(cfg, u, fb):
    T, tr = cfg.T, cfg.tr
    cf = cfg.o_f // LANES

    def body(f_ref, fb_ref, c_ref, carry_ref):
        i = pl.program_id(0)

        @pl.when(i == 0)
        def _():
            carry_ref[...] = jnp.zeros_like(carry_ref)

        lf = -_softplus(-(f_ref[...] + fb_ref[...]))
        r = lax.broadcasted_iota(jnp.int32, (tr, tr), 0)
        c = lax.broadcasted_iota(jnp.int32, (tr, tr), 1)
        tri = (r >= c).astype(F32)
        c_ref[...] = _dot(tri, lf, precision=HI) + carry_ref[...]
        carry_ref[...] += jnp.sum(lf, axis=0, keepdims=True)

    return _pcall(body, name="fox_prep", grid=(T // tr,), in_specs=[_tile(tr, LANES, cf), _const((1, LANES))],
                  out_specs=_tile(tr, LANES), out_shape=jax.ShapeDtypeStruct((T, LANES), F32),
                  scratch_shapes=[pltpu.VMEM((1, LANES), F32)], compiler_params=_cparams(("arbitrary",)))(u, fb)


def _fox_prep_bwd(cfg, u, fb, dc):
    T, tr = cfg.T, cfg.tr
    cf = cfg.o_f // LANES
    nb = T // tr

    def body(f_ref, fb_ref, dc_ref, df_ref, dfb_ref, carry_ref):
        i = pl.program_id(0)

        @pl.when(i == 0)
        def _():
            carry_ref[...] = jnp.zeros_like(carry_ref)

        dc = dc_ref[...]
        r = lax.broadcasted_iota(jnp.int32, (tr, tr), 0)
        c = lax.broadcasted_iota(jnp.int32, (tr, tr), 1)
        triu = (r <= c).astype(F32)
        dlf = _dot(triu, dc, precision=HI) + carry_ref[...]
        carry_ref[...] += jnp.sum(dc, axis=0, keepdims=True)
        dz = dlf * jax.nn.sigmoid(-(f_ref[...] + fb_ref[...]))
        df_ref[...] = dz.astype(BF16)
        _acc_store(i, dfb_ref, jnp.sum(dz, axis=0, keepdims=True))

    rev = lambda i: (nb - 1 - i, 0)
    return _pcall(body, name="fox_prep_bwd", grid=(nb,),
                  in_specs=[pl.BlockSpec((tr, LANES), lambda i: (nb - 1 - i, cf)), _const((1, LANES)),
                            pl.BlockSpec((tr, LANES), rev)],
                  out_specs=[pl.BlockSpec((tr, LANES), rev), _const((1, LANES))],
                  out_shape=[jax.ShapeDtypeStruct((T, LANES), BF16), jax.ShapeDtypeStruct((1, LANES), F32)],
                  scratch_shapes=[pltpu.VMEM((1, LANES), F32)], compiler_params=_cparams(("arbitrary",)))(u, fb, dc)


def _attn_logits(q_ref, k_ref, c_ref, tq, te):
    q = q_ref[...].astype(BF16)
    scale = FOX_HEAD_DIM ** -0.5
    part = lambda k0, k1: _dot(q, k_ref[k0:k1, :].astype(BF16), "nt") * scale - c_ref[0, :, k0:k1]
    row = lax.broadcasted_iota(jnp.int32, (tq, tq), 0)
    col = lax.broadcasted_iota(jnp.int32, (tq, tq), 1)
    own = ((te - tq, te), jnp.where(col <= row, part(te - tq, te), -1e30))
    return [((0, te - tq), part(0, te - tq)), own] if te > tq else [own]


def _per_query_tile(i, nq, tq, fn):
    for ii in range(nq):
        pl.when(i == ii)(functools.partial(fn, (ii + 1) * tq))


def _attn_fwd(cfg, u, c_rows):
    T, FW, FH = cfg.T, cfg.FW, cfg.FH
    tq = min(256, T)
    dh = FOX_HEAD_DIM

    def body(q_ref, k_ref, v_ref, c_ref, o_ref, lse_ref):
        i = pl.program_id(1)

        def tile(te):
            parts = _attn_logits(q_ref, k_ref, c_ref, tq, te)
            m = functools.reduce(jnp.maximum, [jnp.max(s, axis=1, keepdims=True) for _, s in parts])
            l, acc = 0.0, 0.0
            for (k0, k1), s in parts:
                p = jnp.exp(s - m)
                l = l + jnp.sum(p, axis=1, keepdims=True)
                acc = acc + _dot(p.astype(BF16), v_ref[k0:k1, :].astype(BF16))
            o_ref[...] = acc / l
            lse_ref[0] = m + jnp.log(l)

        _per_query_tile(i, T // tq, tq, tile)

    return _pcall(
        body, name="fox_attn_fwd", grid=(FH, T // tq),
        in_specs=[pl.BlockSpec((tq, dh), lambda h, i: (i, 3 * h)), pl.BlockSpec((T, dh), lambda h, i: (0, 3 * h + 1)),
                  pl.BlockSpec((T, dh), lambda h, i: (0, 3 * h + 2)), pl.BlockSpec((1, 1, T), lambda h, i: (h, 0, 0))],
        out_specs=[pl.BlockSpec((tq, dh), lambda h, i: (i, h)), pl.BlockSpec((1, tq, 1), lambda h, i: (h, i, 0))],
        out_shape=[jax.ShapeDtypeStruct((T, FW), F32), jax.ShapeDtypeStruct((FH, T, 1), F32)],
        compiler_params=_cparams(("parallel", "arbitrary")),
    )(u, u, u, c_rows)


def _attn_bwd(cfg, u, c_rows, lse, do, du, comm=None):
    T, FW, FH = cfg.T, cfg.FW, cfg.FH
    tq = min(256, T)
    nq = T // tq
    dh = FOX_HEAD_DIM
    scale = dh ** -0.5

    def body(q_ref, k_ref, v_ref, c_ref, lse_ref, do_ref, du_in, du_ref, dcol_ref, dk_acc, dv_acc):
        i = pl.program_id(1)

        @pl.when(i == 0)
        def _():
            dk_acc[...] = jnp.zeros_like(dk_acc)
            dv_acc[...] = jnp.zeros_like(dv_acc)
            dcol_ref[...] = jnp.zeros_like(dcol_ref)

        def tile(te):
            lse, q16, do16 = lse_ref[0], q_ref[...].astype(BF16), do_ref[...].astype(BF16)
            probs = [(ks, jnp.exp(s - lse)) for ks, s in _attn_logits(q_ref, k_ref, c_ref, tq, te)]
            dps = [_dot(do16, v_ref[k0:k1, :].astype(BF16), "nt") for (k0, k1), _ in probs]
            delta = sum(jnp.sum(p * dp, axis=1, keepdims=True) for (_, p), dp in zip(probs, dps))
            dq = 0.0
            for ((k0, k1), p), dp in zip(probs, dps):
                ds = p * (dp - delta)
                ds16 = ds.astype(BF16)
                dq = dq + _dot(ds16, k_ref[k0:k1, :].astype(BF16))
                dk_acc[k0:k1, :] += _dot(ds16, q16, "tn") * scale
                dv_acc[k0:k1, :] += _dot(p.astype(BF16), do16, "tn")
                dcol_ref[0, :, k0:k1] += jnp.sum(ds, axis=0, keepdims=True)
            du_ref[te - tq:te, 0:dh] = (dq * scale).astype(BF16)

        _per_query_tile(i, nq, tq, tile)

        @pl.when(i == nq - 1)
        def _():
            du_ref[:, dh:2 * dh] = dk_acc[...].astype(BF16)
            du_ref[:, 2 * dh:3 * dh] = dv_acc[...].astype(BF16)

    body, c_in, c_out, c_shapes, c_scr = _hosted(body, 7, 2, (FH, nq), comm)
    return _pcall(
        body, name="fox_attn_bwd", grid=(FH, nq),
        in_specs=[pl.BlockSpec((tq, dh), lambda h, i: (i, 3 * h)), pl.BlockSpec((T, dh), lambda h, i: (0, 3 * h + 1)),
                  pl.BlockSpec((T, dh), lambda h, i: (0, 3 * h + 2)), pl.BlockSpec((1, 1, T), lambda h, i: (h, 0, 0)),
                  pl.BlockSpec((1, tq, 1), lambda h, i: (h, i, 0)), pl.BlockSpec((tq, dh), lambda h, i: (i, h)), _ANY]
        + c_in,
        out_specs=[pl.BlockSpec((T, 3 * dh), lambda h, i: (0, h)), pl.BlockSpec((1, 1, T), lambda h, i: (h, 0, 0))]
        + c_out,
        out_shape=[jax.ShapeDtypeStruct(du.shape, BF16), jax.ShapeDtypeStruct((FH, 1, T), F32)] + c_shapes,
        scratch_shapes=c_scr + [pltpu.VMEM((T, dh), F32), pltpu.VMEM((T, dh), F32)],
        input_output_aliases={6: 0},
        compiler_params=_cparams(("arbitrary", "arbitrary") if comm else ("parallel", "arbitrary")),
    )(u, u, u, c_rows, lse, do, du, *(comm[0] if comm else []))


def _head_indicators(cfg):
    ind = np.zeros((cfg.RW, LANES), np.float32)
    ind[np.arange(cfg.RW), np.arange(cfg.RW) // RWKV_HEAD_DIM] = 1.0
    pad = np.zeros((1, LANES), np.float32)
    pad[0, cfg.RH:] = 1.0
    return jnp.asarray(ind), jnp.asarray(ind.T.copy()), jnp.asarray(pad)


def _prep_fn(us_r, us_k, us_v, us_wd, us_ad, w0, w2p, a0, a2p, k_k, k_a, ind, ind_t, pad):
    wpre = w0 + _dot3(jnp.tanh(us_wd), w2p)
    w = -_softplus(-wpre) - 0.5
    lw = -jnp.exp(w)
    a = jax.nn.sigmoid(a0 + _dot3(us_ad, a2p))
    kk = us_k * k_k
    ss = _xdot(kk * kk, ind, ind_t) + pad
    inv = 1.0 / jnp.maximum(jnp.sqrt(ss), L2_EPS)
    kkn = kk * _xdot(inv, ind_t, ind)
    kp = us_k * (1.0 + (a - 1.0) * k_a)
    return us_r, lw, kp, us_v, -kkn, kkn * a


def _shifted(u, prev_row, mu, first):
    n = u.shape[0]
    rolled = pltpu.roll(u, 1, 0)
    row = lax.broadcasted_iota(jnp.int32, u.shape, 0)
    p0 = jnp.where(first, jnp.zeros_like(prev_row), prev_row)
    prev = jnp.where(row == 0, jnp.broadcast_to(p0, u.shape), rolled)
    return u + (prev - u) * mu, prev


def _rwkv_specs(cfg, tr):
    RW, LP = cfg.RW, cfg.LP
    base = cfg.o_rwkv // RW
    cols = [(RW, base), (RW, base + 1), (RW, base + 2), (RW, base + 3), (LP, cfg.o_wd // LP), (LP, cfg.o_ad // LP)]
    cur = [pl.BlockSpec((tr, w), (lambda i, cb=cb: (i, cb))) for w, cb in cols]
    prv = [pl.BlockSpec((8, w), (lambda i, cb=cb: (jnp.maximum(i * (tr // 8) - 1, 0), cb))) for w, cb in cols]
    return cols, cur, prv


def _mu_pieces(cfg, mu_ref):
    RW, LP = cfg.RW, cfg.LP
    offs = [0, RW, 2 * RW, 3 * RW, 4 * RW, 4 * RW + LP, 4 * RW + 2 * LP]
    return [mu_ref[:, offs[j]:offs[j + 1]] for j in range(6)]


def _rwkv_prep_fwd(cfg, u, mu, w0, w2p, a0, a2p, k_k, k_a):
    T, RW, LP, tr = cfg.T, cfg.RW, cfg.LP, cfg.tr
    ind, ind_t, pad = _head_indicators(cfg)
    cols, cur, prv = _rwkv_specs(cfg, tr)

    def body(*refs):
        u_refs, p_refs = refs[0:6], refs[6:12]
        mu_ref, w0_ref, w2_ref, a0_ref, a2_ref, kk_ref, ka_ref, ind_ref, indt_ref, pad_ref = refs[12:22]
        outs = refs[22:]
        first = pl.program_id(0) == 0
        mus = _mu_pieces(cfg, mu_ref)
        us = [_shifted(u_refs[j][...], p_refs[j][7:8, :], mus[j], first)[0] for j in range(6)]
        res = _prep_fn(us[0], us[1], us[2], us[4], us[5], w0_ref[...], w2_ref[...], a0_ref[...], a2_ref[...],
                       kk_ref[...], ka_ref[...], ind_ref[...], indt_ref[...], pad_ref[...])
        for j in range(6):
            outs[j][...] = res[j]
        outs[6][...] = us[3]

    consts = [mu, w0, w2p, a0, a2p, k_k, k_a, ind, ind_t, pad]
    return _pcall(body, name="rwkv_prep_fwd", grid=(T // tr,),
                  in_specs=cur + prv + [_const(c.shape) for c in consts],
                  out_specs=[_tile(tr, RW)] * 7, out_shape=[jax.ShapeDtypeStruct((T, RW), F32)] * 7,
                  compiler_params=_cparams(("parallel",)))(*([u] * 12), *consts)


def _rwkv_prep_bwd(cfg, u, mu, w0, w2p, a0, a2p, k_k, k_a, cots, dzb):
    T, RW, LP = cfg.T, cfg.RW, cfg.LP
    tr = min(128, T)
    ind, ind_t, pad = _head_indicators(cfg)
    cols, cur, prv = _rwkv_specs(cfg, tr)
    rseg = cfg.rseg

    def body(*refs):
        u_refs, p_refs = refs[0:6], refs[6:12]
        mu_ref, w0_ref, w2_ref, a0_ref, a2_ref, kk_ref, ka_ref, ind_ref, indt_ref, pad_ref = refs[12:22]
        cot_refs, dzb_ref = refs[22:28], refs[28]
        dus_ref, dmu_ref, dw0_ref, dw2_ref, da0_ref, da2_ref, dkk_ref, dka_ref = refs[29:]
        i = pl.program_id(0)
        first = i == 0
        mus = _mu_pieces(cfg, mu_ref)
        sh = [_shifted(u_refs[j][...], p_refs[j][7:8, :], mus[j], first) for j in range(6)]
        us = [s[0] for s in sh]
        fn = functools.partial(_prep_fn, ind=ind_ref[...], ind_t=indt_ref[...], pad=pad_ref[...])
        _, vjp = jax.vjp(fn, us[0], us[1], us[2], us[4], us[5], w0_ref[...], w2_ref[...], a0_ref[...], a2_ref[...],
                         kk_ref[...], ka_ref[...])
        d = vjp(tuple(c[...] for c in cot_refs))
        dus = [d[0], d[1], d[2], dzb_ref[...], d[3], d[4]]
        offs = [0, RW, 2 * RW, 3 * RW, 4 * RW, 4 * RW + LP, 4 * RW + 2 * LP]
        for j in range(6):
            dus_ref[:, offs[j]:offs[j + 1]] = dus[j]
            dmu_j = jnp.sum(dus[j] * (sh[j][1] - u_refs[j][...]), axis=0, keepdims=True)

            @pl.when(first)
            def _(j=j, dmu_j=dmu_j):
                dmu_ref[:, offs[j]:offs[j + 1]] = dmu_j

            @pl.when(i > 0)
            def _(j=j, dmu_j=dmu_j):
                dmu_ref[:, offs[j]:offs[j + 1]] += dmu_j
        for ref, val in zip((dw0_ref, dw2_ref, da0_ref, da2_ref, dkk_ref, dka_ref), d[5:11]):
            _acc_store(i, ref, val)

    consts = [mu, w0, w2p, a0, a2p, k_k, k_a, ind, ind_t, pad]
    vec = jax.ShapeDtypeStruct((1, RW), F32)
    mat = jax.ShapeDtypeStruct((LP, RW), F32)
    return _pcall(body, name="rwkv_prep_bwd", grid=(T // tr,),
                  in_specs=cur + prv + [_const(c.shape) for c in consts] + [_tile(tr, RW)] * 7,
                  out_specs=[_tile(tr, rseg), _const((1, rseg)), _const((1, RW)), _const((LP, RW)), _const((1, RW)),
                             _const((LP, RW)), _const((1, RW)), _const((1, RW))],
                  out_shape=[jax.ShapeDtypeStruct((T, rseg), F32), jax.ShapeDtypeStruct((1, rseg), F32),
                             vec, mat, vec, mat, vec, vec],
                  compiler_params=_cparams(("arbitrary",)))(*([u] * 12), *consts, *cots, dzb)


def _shift_bwd(cfg, dus, mu, df, du):
    T, tr, RW, LP = cfg.T, cfg.tr, cfg.RW, cfg.LP
    nb = T // tr
    tail = cfg.ncol - cfg.o_f
    assert cfg.o_rwkv % (4 * RW) == 0 and (4 * RW) % (2 * LP) == 0 and cfg.o_f % tail == 0

    def shifted(d_ref, n_ref, mu_ref):
        d = d_ref[...]
        rolled = pltpu.roll(d, tr - 1, 0)
        row = lax.broadcasted_iota(jnp.int32, d.shape, 0)
        n0 = jnp.where(pl.program_id(0) == nb - 1, jnp.zeros_like(n_ref[0:1, :]), n_ref[0:1, :])
        nxt = jnp.where(row == tr - 1, jnp.broadcast_to(n0, d.shape), rolled)
        mu_v = mu_ref[...]
        return (d * (1.0 - mu_v) + nxt * mu_v).astype(BF16)

    def main_body(d_ref, n_ref, mu_ref, du_in, du_ref):
        du_ref[...] = shifted(d_ref, n_ref, mu_ref)

    def tail_body(d_ref, n_ref, mu_ref, df_ref, du_in, du_ref):
        du_ref[:, 0:LANES] = df_ref[...]
        du_ref[:, LANES:LANES + 2 * LP] = shifted(d_ref, n_ref, mu_ref)
        if tail > LANES + 2 * LP:
            du_ref[:, LANES + 2 * LP:] = jnp.zeros((tr, tail - LANES - 2 * LP), BF16)

    def specs(w, cb):
        return [_tile(tr, w, cb),
                pl.BlockSpec((8, w), lambda i: (jnp.minimum((i + 1) * (tr // 8), T // 8 - 1), cb)),
                pl.BlockSpec((1, w), lambda i: (0, cb))]

    out = jax.ShapeDtypeStruct(du.shape, BF16)
    du = _pcall(main_body, name="shift_bwd_main", grid=(nb,), in_specs=specs(4 * RW, 0) + [_ANY],
                out_specs=_tile(tr, 4 * RW, cfg.o_rwkv // (4 * RW)), out_shape=out, input_output_aliases={3: 0},
                compiler_params=_cparams(("parallel",)))(dus, dus, mu, du)
    return _pcall(tail_body, name="shift_bwd_tail", grid=(nb,),
                  in_specs=specs(2 * LP, 4 * RW // (2 * LP)) + [_tile(tr, LANES), _ANY],
                  out_specs=_tile(tr, tail, cfg.o_f // tail), out_shape=out, input_output_aliases={4: 0},
                  compiler_params=_cparams(("parallel",)))(dus, dus, mu, df, du)


def _chunk_local(r, lw, k, v, a, b):
    H, C, K = r.shape
    row = lax.broadcasted_iota(jnp.int32, (C, C), 0)
    col = lax.broadcasted_iota(jnp.int32, (C, C), 1)
    incl = jnp.broadcast_to((row >= col).astype(F32)[None], (H, C, C))
    strict = (row > col)[None]
    lower = (row >= col)[None]
    eye = (row == col)[None]
    zero = jnp.zeros((), F32)
    L = _bdot(incl, lw, 2, 1)
    LC = jnp.sum(lw, axis=1, keepdims=True)
    eL = jnp.exp(L)
    eLn = jnp.exp(-L)
    at = a * jnp.exp(L - lw)
    rt = r * eL
    bt = b * eLn
    kt = k * eLn
    eR = jnp.exp(LC - L)
    bh = b * eR
    kh = k * eR
    keys = functools.partial(_bdot, passes=SCAN_PASSES[0])
    inv = functools.partial(_bdot, passes=SCAN_PASSES[1])
    app = functools.partial(_bdot, passes=SCAN_PASSES[2])
    ar = jnp.concatenate([at, rt], axis=1)
    g_b = app(ar, bt, 2, 2)
    g_k = keys(ar, kt, 2, 2)
    n_ab = jnp.where(strict, g_b[:, :C], zero)
    n_ak = jnp.where(strict, g_k[:, :C], zero)
    m_rb = jnp.where(lower, g_b[:, C:], zero)
    m_rk = jnp.where(lower, g_k[:, C:], zero)
    P = jnp.where(eye, 1.0, zero) + n_ab
    squarings = max(1, int(np.ceil(np.log2(C)))) - 1
    if squarings:
        M = inv(n_ab, n_ab, 2, 1)
        for _ in range(squarings - 1):
            PM = inv(M, jnp.concatenate([P, M], axis=2), 2, 1)
            P, M = P + PM[:, :, :C], PM[:, :, C:]
        P = P + inv(M, P, 2, 1)
    W = app(P, at, 2, 1)
    Uloc = app(P, app(n_ak, v, 2, 1), 2, 1)
    Q = rt + app(m_rb, W, 2, 1)
    Yloc = app(m_rb, Uloc, 2, 1) + app(m_rk, v, 2, 1)
    A = jnp.where(eye, jnp.exp(LC), zero) + app(W, bh, 1, 1)
    Sloc = app(Uloc, bh, 1, 1) + app(v, kh, 1, 1)
    return Q, Yloc, A, Sloc


def _split_heads(ref, n):
    N = RWKV_HEAD_DIM
    return jnp.stack([ref[:, h * N:(h + 1) * N] for h in range(n)], axis=0)


def _merge_heads(x):
    return jnp.concatenate([x[h] for h in range(x.shape[0])], axis=1)


def _chains(x, cb):
    hb = x.shape[0]
    return x.reshape(hb, cb, -1, x.shape[-1]).reshape(hb * cb, -1, x.shape[-1])


def _scan_fwd(cfg, seqs):
    T, RH, N, C, CB = cfg.T, cfg.RH, RWKV_HEAD_DIM, cfg.C, cfg.cb
    nc = T // C

    def body(r_ref, lw_ref, k_ref, v_ref, a_ref, b_ref, y_ref, ck_ref, s_ref):
        @pl.when(pl.program_id(0) == 0)
        def _():
            s_ref[...] = jnp.zeros_like(s_ref)

        ins = [_chains(_split_heads(ref, RH), CB) for ref in (r_ref, lw_ref, k_ref, v_ref, a_ref, b_ref)]
        Q, Yloc, A, Sloc = _chunk_local(*ins)
        Q, Yloc = Q.reshape(RH, CB, C, N), Yloc.reshape(RH, CB, C, N)
        A, Sloc = A.reshape(RH, CB, N, N), Sloc.reshape(RH, CB, N, N)
        S = s_ref[...]
        for c in range(CB):
            ck_ref[:, c] = S
            y_ref[c * C:(c + 1) * C, :] = _merge_heads(_bdot(Q[:, c], S, 2, 2, SCAN_PASSES[2]) + Yloc[:, c])
            S = _bdot(S, A[:, c], 2, 1) + Sloc[:, c]
        s_ref[...] = S

    tok = pl.BlockSpec((CB * C, cfg.RW), lambda j: (j, 0))
    mat = pl.BlockSpec((RH, CB, N, N), lambda j: (0, j, 0, 0))
    return _pcall(body, name="rwkv_scan_fwd", grid=(nc // CB,), in_specs=[tok] * 6, out_specs=[tok, mat],
                  out_shape=[jax.ShapeDtypeStruct((T, cfg.RW), F32), jax.ShapeDtypeStruct((RH, nc, N, N), F32)],
                  scratch_shapes=[pltpu.VMEM((RH, N, N), F32)],
                  compiler_params=_cparams(("arbitrary",)))(*seqs)


def _scan_bwd(cfg, toks, ckpt, dy, extra, comm=None):
    T, RW, RH, N, C = cfg.T, cfg.RW, cfg.RH, RWKV_HEAD_DIM, cfg.C
    nc = T // C

    def body(r_ref, lw_ref, k_ref, v_ref, a_ref, b_ref, ck_ref, dy_ref, xr_ref, xk_ref, xv_ref, *rest):
        outs, ds_ref = rest[:6], rest[6]

        @pl.when(pl.program_id(0) == 0)
        def _():
            ds_ref[...] = jnp.zeros_like(ds_ref)

        ins = [_split_heads(ref, RH) for ref in (r_ref, lw_ref, k_ref, v_ref, a_ref, b_ref)]
        (Q, _, A, _), vjp = jax.vjp(_chunk_local, *ins)
        S, dS, dY = ck_ref[:, 0], ds_ref[...], _split_heads(dy_ref, RH)
        d = vjp((_bdot(dY, S, 2, 1, SCAN_PASSES[2]), dY, _bdot(S, dS, 1, 1, SCAN_PASSES[2]), dS))
        ds_ref[...] = _bdot(dS, A, 2, 2) + _bdot(dY, Q, 1, 1, SCAN_PASSES[2])
        add = {0: xr_ref, 2: xk_ref, 3: xv_ref}
        for j in range(6):
            dj = _merge_heads(d[j])
            outs[j][...] = dj + add[j][...] if j in add else dj

    tok = pl.BlockSpec((C, RW), lambda j: (nc - 1 - j, 0))
    mat = pl.BlockSpec((RH, 1, N, N), lambda j: (0, nc - 1 - j, 0, 0))
    body, c_in, c_out, c_shapes, c_scr = _hosted(body, 11, 6, (nc,), comm)
    return _pcall(body, name="rwkv_scan_bwd", grid=(nc,), in_specs=[tok] * 6 + [mat] + [tok] * 4 + c_in,
                  out_specs=[tok] * 6 + c_out, out_shape=[jax.ShapeDtypeStruct((T, RW), F32)] * 6 + c_shapes,
                  scratch_shapes=c_scr + [pltpu.VMEM((RH, N, N), F32)],
                  compiler_params=_cparams(("arbitrary",)))(*toks, ckpt, dy, *extra, *(comm[0] if comm else []))


def _post_fn(y, r, kp, v, zb, ln_w, ln_b, rk, ind, ind_t):
    n = float(RWKV_HEAD_DIM)
    mu = _xdot(_xdot(y, ind, ind_t) / n, ind_t, ind)
    yc = y - mu
    var = _xdot(yc * yc, ind, ind_t) / n
    rstd = _xdot(lax.rsqrt(var + GN_EPS), ind_t, ind)
    yn = yc * rstd * ln_w + ln_b
    bonus = _xdot(_xdot(r * kp * rk, ind, ind_t), ind_t, ind) * v
    return (yn + bonus) * _silu(zb)


def _rwkv_post_fwd(cfg, y, r, kp, v, zb, ln_w, ln_b, rk):
    T, RW, tr = cfg.T, cfg.RW, cfg.tr
    ind, ind_t, _ = _head_indicators(cfg)

    def body(y_ref, r_ref, k_ref, v_ref, z_ref, lw_ref, lb_ref, rk_ref, ind_ref, indt_ref, ob_ref):
        ob_ref[...] = _post_fn(y_ref[...], r_ref[...], k_ref[...], v_ref[...], z_ref[...], lw_ref[...], lb_ref[...],
                               rk_ref[...], ind_ref[...], indt_ref[...]).astype(BF16)

    consts = [ln_w, ln_b, rk, ind, ind_t]
    return _pcall(body, name="rwkv_post_fwd", grid=(T // tr,),
                  in_specs=[_tile(tr, RW)] * 5 + [_const(c.shape) for c in consts],
                  out_specs=_tile(tr, RW), out_shape=jax.ShapeDtypeStruct((T, RW), BF16),
                  compiler_params=_cparams(("parallel",)))(y, r, kp, v, zb, *consts)


def _rwkv_post_bwd(cfg, y, r, kp, v, zb, ln_w, ln_b, rk, dob):
    T, RW = cfg.T, cfg.RW
    tr = min(128, T)
    ind, ind_t, _ = _head_indicators(cfg)

    def body(y_ref, r_ref, k_ref, v_ref, z_ref, lw_ref, lb_ref, rk_ref, ind_ref, indt_ref, dob_ref,
             dy_ref, dr_ref, dk_ref, dv_ref, dz_ref, dlw_ref, dlb_ref, drk_ref):
        fn = functools.partial(_post_fn, ind=ind_ref[...], ind_t=indt_ref[...])
        _, vjp = jax.vjp(fn, y_ref[...], r_ref[...], k_ref[...], v_ref[...], z_ref[...], lw_ref[...], lb_ref[...],
                         rk_ref[...])
        d = vjp(dob_ref[...])
        for ref, val in zip((dy_ref, dr_ref, dk_ref, dv_ref, dz_ref), d[:5]):
            ref[...] = val
        i = pl.program_id(0)
        for ref, val in zip((dlw_ref, dlb_ref, drk_ref), d[5:8]):
            _acc_store(i, ref, val)

    consts = [ln_w, ln_b, rk, ind, ind_t]
    vec = jax.ShapeDtypeStruct((1, RW), F32)
    return _pcall(body, name="rwkv_post_bwd", grid=(T // tr,),
                  in_specs=[_tile(tr, RW)] * 5 + [_const(c.shape) for c in consts] + [_tile(tr, RW)],
                  out_specs=[_tile(tr, RW)] * 5 + [_const((1, RW))] * 3,
                  out_shape=[jax.ShapeDtypeStruct((T, RW), F32)] * 5 + [vec] * 3,
                  compiler_params=_cparams(("arbitrary",)))(y, r, kp, v, zb, *consts, dob)


def _adamw_math(w, g, m, v):
    m = ADAM_B1 * m + (1.0 - ADAM_B1) * g
    v = ADAM_B2 * v + (1.0 - ADAM_B2) * (g * g)
    m_hat = m / (1.0 - ADAM_B1 ** ADAM_STEP)
    v_hat = v / (1.0 - ADAM_B2 ** ADAM_STEP)
    delta = -ADAM_LR * (m_hat / (jnp.sqrt(v_hat) + ADAM_EPS) + ADAM_WD * w)
    return delta, m, v


def _adamw(name, w, g, m, v, copy_grad=False, comm=None):
    R, Cc = w.shape
    Rp = -(-R // 8) * 8
    tr = Rp
    for nb in range(1, Rp // 8 + 1):
        if (Rp // 8) % nb == 0 and (Rp // nb) * Cc * 4 <= 2 * 1024 * 1024:
            tr = Rp // nb
            break

    def body(w_ref, g_ref, m_ref, v_ref, d_ref, nm_ref, nv_ref, *g_out):
        g_v = g_ref[...]
        d, nm, nv = _adamw_math(w_ref[...], g_v, m_ref[...], v_ref[...])
        d_ref[...] = d
        nm_ref[...] = nm
        nv_ref[...] = nv
        if copy_grad:
            g_out[0][...] = g_v

    spec = _tile(tr, Cc)
    n_out = 4 if copy_grad else 3
    body, c_in, c_out, c_shapes, c_scr = _hosted(body, 4, n_out, (Rp // tr,), comm)
    return _pcall(body, name=name, grid=(Rp // tr,), in_specs=[spec] * 4 + c_in, out_specs=[spec] * n_out + c_out,
                  out_shape=[jax.ShapeDtypeStruct((R, Cc), F32)] * n_out + c_shapes, scratch_shapes=c_scr,
                  compiler_params=_cparams(("arbitrary",) if comm else ("parallel",)),
                  )(w, g, m, v, *(comm[0] if comm else []))


def _row_tile(R, Cc, itemsize, budget=2 * 1024 * 1024):
    for nb in range(1, R // 16 + 1):
        if R % nb == 0 and (R // nb) % 16 == 0 and (R // nb) * Cc * itemsize <= budget:
            return R // nb
    return R


def _add_halves(name, gs, r1, c_idx):
    S, R, Cc = gs.shape
    half = R // 2
    tr = _row_tile(half, Cc, 4)
    nb = half // tr

    def body(c_ref, g_ref, r_ref, o_ref):
        o_ref[...] = (g_ref[...].astype(F32) + r_ref[...].astype(F32)).astype(BF16)

    grid_spec = pltpu.PrefetchScalarGridSpec(
        num_scalar_prefetch=1, grid=(S, nb),
        in_specs=[pl.BlockSpec((1, tr, Cc), lambda s, i, c: (s, c[0] * nb + i, 0)),
                  pl.BlockSpec((1, tr, Cc), lambda s, i, c: (s, i, 0))],
        out_specs=pl.BlockSpec((1, tr, Cc), lambda s, i, c: (s, i, 0)))
    return _pcall(body, name=name, grid_spec=grid_spec, out_shape=jax.ShapeDtypeStruct((S, half, Cc), BF16),
                  compiler_params=_cparams(("parallel", "parallel")))(c_idx, gs, r1)


def _sum_slots(name, r2):
    S, R, Cc = r2.shape
    tr = _row_tile(R, Cc, 4 * S // 2 if r2.dtype == BF16 else 4 * S)

    def body(r_ref, o_ref):
        acc = r_ref[0].astype(F32)
        for s in range(1, S):
            acc = acc + r_ref[s].astype(F32)
        o_ref[...] = acc

    return _pcall(body, name=name, grid=(R // tr,), in_specs=[pl.BlockSpec((S, tr, Cc), lambda i: (0, i, 0))],
                  out_specs=_tile(tr, Cc), out_shape=jax.ShapeDtypeStruct((R, Cc), F32),
                  compiler_params=_cparams(("parallel",)))(r2)


def _sum_chips(name, recv, own, place):
    S, H, Cc = recv.shape
    tr = _row_tile(H, Cc, 4, 1024 * 1024)
    nb = H // tr

    def body(p_ref, r_ref, own_ref, o_ref):
        s = pl.program_id(1)
        me = p_ref[0]

        @pl.when(s == 0)
        def _():
            o_ref[...] = jnp.zeros_like(o_ref)

        @pl.when(s == me)
        def _():
            o_ref[...] += own_ref[0].astype(F32)

        @pl.when(s != me)
        def _():
            o_ref[...] += r_ref[0].astype(F32)

    grid_spec = pltpu.PrefetchScalarGridSpec(
        num_scalar_prefetch=1, grid=(nb, S),
        in_specs=[pl.BlockSpec((1, tr, Cc), lambda i, s, p: (jnp.where(s == p[0], (s + 1) % S, s), i, 0)),
                  pl.BlockSpec((1, tr, Cc), lambda i, s, p: (p[0], i, 0))],
        out_specs=pl.BlockSpec((tr, Cc), lambda i, s, p: (p[1] * nb + i, 0)))
    return _pcall(body, name=name, grid_spec=grid_spec, out_shape=jax.ShapeDtypeStruct((2 * H, Cc), F32),
                  compiler_params=_cparams(("parallel", "arbitrary")))(place, recv, own)


def _cast_bf16(name, w):
    R, Cc = w.shape
    tr = _row_tile(R, Cc, 4)

    def body(w_ref, o_ref):
        o_ref[...] = w_ref[...].astype(BF16)

    return _pcall(body, name=name, grid=(R // tr,), in_specs=[_tile(tr, Cc)], out_specs=_tile(tr, Cc),
                  out_shape=jax.ShapeDtypeStruct((R, Cc), BF16), compiler_params=_cparams(("parallel",)))(w)


_ANY = pl.BlockSpec(memory_space=pl.ANY)


def _place():
    x, y, c = lax.axis_index("x"), lax.axis_index("y"), lax.axis_index("c")
    others = [(1 - x, y), (x, 1 - y), (1 - x, 1 - y)]
    return x, y, c, others


def _gather_parts(shards):
    n = len(shards)
    halves = [s.shape[0] // 2 for s in shards]

    def parts(ins, outs, sems):
        x, y, c, _ = _place()
        me = 2 * x + y
        n1 = (x ^ (1 - c), y ^ c)
        n2 = (x ^ c, y ^ (1 - c))
        s1, s2, sd = 2 * n1[0] + n1[1], 2 * n2[0] + n2[1], 2 * (1 - x) + (1 - y)
        sib = (x, y, 1 - c)

        def rows(k, chip, hc):
            return outs[k].at[chip, pl.ds(hc * halves[k], halves[k]), :]

        def remote(k, j, src, dst, to):
            return pltpu.make_async_remote_copy(src_ref=src, dst_ref=dst, send_sem=sems[0].at[7 * k + j],
                                                recv_sem=sems[1].at[7 * k + j], device_id=to, device_id_type=MESH)

        def copy(k, j):
            if j == 6:
                return remote(k, j, ins[k], outs[k].at[me], sib)
            if j < 2:
                mine = ins[k].at[pl.ds(c * halves[k], halves[k]), :]
                return remote(k, j, mine, rows(k, me, c), (*(n1 if j == 0 else n2), c))
            land = rows(k, {2: s1, 3: s1, 4: s2, 5: sd}[j], c)
            return remote(k, j, land, land, (*n2, c) if j == 2 else sib)

        def arrived(k, j):
            hc = c if j < 3 else 1 - c
            land = outs[k].at[me] if j == 6 else rows(k, {0: s1, 1: s2, 2: sd, 3: s2, 4: s1, 5: sd}[j], hc)
            remote(k, j, land, land, (x, y, c)).wait_recv()

        return copy, arrived

    def start(ins, outs, sems):
        copy, _ = parts(ins, outs, sems)
        for k in range(n):
            copy(k, 0).start()
            copy(k, 1).start()
            copy(k, 6).start()

    def middle(ins, outs, sems):
        copy, arrived = parts(ins, outs, sems)
        for k in range(n):
            arrived(k, 0)
            copy(k, 2).start()
            copy(k, 3).start()
            arrived(k, 1)
            copy(k, 4).start()

    def finish(ins, outs, sems):
        copy, arrived = parts(ins, outs, sems)
        for k in range(n):
            arrived(k, 2)
            copy(k, 5).start()
        for k in range(n):
            for j in (3, 4, 5, 6):
                arrived(k, j)
        for k in range(n):
            for j in range(7):
                copy(k, j).wait_send()

    out_shapes = [jax.ShapeDtypeStruct((N_CHIPS,) + s.shape, s.dtype) for s in shards]
    scratch = [pltpu.SemaphoreType.DMA((7 * n,)), pltpu.SemaphoreType.DMA((7 * n,))]
    return list(shards), out_shapes, scratch, start, finish, middle


def _swap_halves(grads):
    n = len(grads)
    halves = [g.shape[1] // 2 for g in grads]

    def copies(ins, outs, sems):
        x, y, c, _ = _place()
        return [pltpu.make_async_remote_copy(
            src_ref=ins[k].at[:, pl.ds((1 - c) * halves[k], halves[k]), :], dst_ref=outs[k], send_sem=sems[0].at[k],
            recv_sem=sems[1].at[k], device_id=(x, y, 1 - c), device_id_type=MESH) for k in range(n)]

    def start(ins, outs, sems):
        for cp in copies(ins, outs, sems):
            cp.start()

    def finish(ins, outs, sems):
        for cp in copies(ins, outs, sems):
            cp.wait()

    out_shapes = [jax.ShapeDtypeStruct((g.shape[0], h) + g.shape[2:], g.dtype) for g, h in zip(grads, halves)]
    scratch = [pltpu.SemaphoreType.DMA((n,)), pltpu.SemaphoreType.DMA((n,))]
    return list(grads), out_shapes, scratch, start, finish


def _scatter_to_owners(chip_sums):
    n = len(chip_sums)

    def sends(ins, outs, sems):
        x, y, c, others = _place()
        me = 2 * x + y
        return [pltpu.make_async_remote_copy(
            src_ref=ins[k].at[2 * px + py], dst_ref=outs[k].at[me], send_sem=sems[0].at[3 * k + j],
            recv_sem=sems[1].at[3 * k + j], device_id=(px, py, c), device_id_type=MESH)
            for k in range(n) for j, (px, py) in enumerate(others)]

    def start(ins, outs, sems):
        for cp in sends(ins, outs, sems):
            cp.start()

    def finish(ins, outs, sems):
        x, y, c, others = _place()
        for k in range(n):
            for j, (px, py) in enumerate(others):
                land = outs[k].at[2 * px + py]
                pltpu.make_async_remote_copy(src_ref=land, dst_ref=land, send_sem=sems[0].at[3 * k + j],
                                             recv_sem=sems[1].at[3 * k + j], device_id=(x, y, c),
                                             device_id_type=MESH).wait_recv()
        for cp in sends(ins, outs, sems):
            cp.wait_send()

    out_shapes = [jax.ShapeDtypeStruct(g.shape, g.dtype) for g in chip_sums]
    scratch = [pltpu.SemaphoreType.DMA((3 * n,)), pltpu.SemaphoreType.DMA((3 * n,))]
    return list(chip_sums), out_shapes, scratch, start, finish


def _swap_with_sibling(arrays):
    n = len(arrays)

    def copies(ins, outs, sems):
        x, y, c, _ = _place()
        return [pltpu.make_async_remote_copy(src_ref=ins[k], dst_ref=outs[k], send_sem=sems[0].at[k],
                                             recv_sem=sems[1].at[k], device_id=(x, y, 1 - c), device_id_type=MESH)
                for k in range(n)]

    def start(ins, outs, sems):
        for cp in copies(ins, outs, sems):
            cp.start()

    def finish(ins, outs, sems):
        for cp in copies(ins, outs, sems):
            cp.wait()

    out_shapes = [jax.ShapeDtypeStruct(a.shape, a.dtype) for a in arrays]
    scratch = [pltpu.SemaphoreType.DMA((n,)), pltpu.SemaphoreType.DMA((n,))]
    return list(arrays), out_shapes, scratch, start, finish


def _add_pair(name, a, b):
    R, Cc = a.shape
    tr = _row_tile(R, Cc, 4)

    def body(a_ref, b_ref, o_ref):
        o_ref[...] = (a_ref[...].astype(F32) + b_ref[...].astype(F32)).astype(BF16)

    return _pcall(body, name=name, grid=(R // tr,), in_specs=[_tile(tr, Cc)] * 2, out_specs=_tile(tr, Cc),
                  out_shape=jax.ShapeDtypeStruct((R, Cc), BF16), compiler_params=_cparams(("parallel",)))(a, b)


def _second_neighbour():
    x, y, c, _ = _place()
    return (x, y, c), (x ^ c, y ^ (1 - c)), (x ^ (1 - c), y ^ c)


def _scatter_stage1(chip_sums):
    n = len(chip_sums)

    def copies(ins, outs, sems):
        (x, y, c), n2, n1 = _second_neighbour()
        diag = 2 * (1 - x) + (1 - y)
        return [pltpu.make_async_remote_copy(
            src_ref=ins[k].at[slot], dst_ref=outs[2 * k + j], send_sem=sems[0].at[2 * k + j],
            recv_sem=sems[1].at[2 * k + j], device_id=(*n2, c), device_id_type=MESH)
            for k in range(n) for j, slot in enumerate((2 * n2[0] + n2[1], diag))]

    def start(ins, outs, sems):
        for cp in copies(ins, outs, sems):
            cp.start()

    def finish(ins, outs, sems):
        for cp in copies(ins, outs, sems):
            cp.wait()

    out_shapes = [jax.ShapeDtypeStruct(g.shape[1:], g.dtype) for g in chip_sums for _ in range(2)]
    scratch = [pltpu.SemaphoreType.DMA((2 * n,)), pltpu.SemaphoreType.DMA((2 * n,))]
    return list(chip_sums), out_shapes, scratch, start, finish


def _scatter_stage2(passed):
    n = len(passed)

    def copies(ins, outs, sems):
        (x, y, c), n2, n1 = _second_neighbour()
        return [pltpu.make_async_remote_copy(src_ref=ins[k], dst_ref=outs[k], send_sem=sems[0].at[k],
                                             recv_sem=sems[1].at[k], device_id=(*n1, c), device_id_type=MESH)
                for k in range(n)]

    def start(ins, outs, sems):
        for cp in copies(ins, outs, sems):
            cp.start()

    def finish(ins, outs, sems):
        for cp in copies(ins, outs, sems):
            cp.wait()

    out_shapes = [jax.ShapeDtypeStruct(p.shape, p.dtype) for p in passed]
    scratch = [pltpu.SemaphoreType.DMA((n,)), pltpu.SemaphoreType.DMA((n,))]
    return list(passed), out_shapes, scratch, start, finish


def _add_passed(name, own, got, slot):
    _, H, Cc = own.shape
    tr = _row_tile(H, Cc, 4)

    def body(s_ref, o_ref, g_ref, out_ref):
        out_ref[...] = (o_ref[0].astype(F32) + g_ref[...].astype(F32)).astype(BF16)

    grid_spec = pltpu.PrefetchScalarGridSpec(
        num_scalar_prefetch=1, grid=(H // tr,),
        in_specs=[pl.BlockSpec((1, tr, Cc), lambda i, s: (s[0], i, 0)), pl.BlockSpec((tr, Cc), lambda i, s: (i, 0))],
        out_specs=pl.BlockSpec((tr, Cc), lambda i, s: (i, 0)))
    return _pcall(body, name=name, grid_spec=grid_spec, out_shape=jax.ShapeDtypeStruct((H, Cc), BF16),
                  compiler_params=_cparams(("parallel",)))(slot, own, got)


def _sum_stages(name, own, direct, via, place, transposed=False):
    _, H, Cc = own.shape
    tr = LANES if transposed else _row_tile(H, Cc, 4, 1024 * 1024)
    nb = H // tr

    def body(p_ref, own_ref, d_ref, v_ref, o_ref):
        acc = (own_ref[0].astype(F32) + d_ref[...].astype(F32)) + v_ref[...].astype(F32)
        o_ref[...] = acc.T if transposed else acc

    flat = pl.BlockSpec((tr, Cc), lambda i, p: (i, 0))
    out_spec = (pl.BlockSpec((Cc, tr), lambda i, p: (0, p[1] * nb + i)) if transposed
                else pl.BlockSpec((tr, Cc), lambda i, p: (p[1] * nb + i, 0)))
    grid_spec = pltpu.PrefetchScalarGridSpec(
        num_scalar_prefetch=1, grid=(nb,),
        in_specs=[pl.BlockSpec((1, tr, Cc), lambda i, p: (p[0], i, 0)), flat, flat], out_specs=out_spec)
    return _pcall(body, name=name, grid_spec=grid_spec,
                  out_shape=jax.ShapeDtypeStruct((Cc, 2 * H) if transposed else (2 * H, Cc), F32),
                  compiler_params=_cparams(("parallel",)))(place, own, direct, via)


def _join_halves(fulls, axes, small):
    n = len(fulls)
    hs = [f.shape[ax] // 2 for f, ax in zip(fulls, axes)]
    rel = [(dx, dy, dc) for dx in (0, 1) for dy in (0, 1) for dc in (0, 1)][1:]

    def half(ref, k, hc):
        part = pl.ds(hc * hs[k], hs[k])
        return ref.at[:, part] if axes[k] else ref.at[part, :]

    def body(*refs):
        ins, small_in = refs[:n], refs[n]
        outs, small_out = refs[n + 1:2 * n + 1], refs[2 * n + 1]
        send_sems, recv_sems, ssend, srecv, local_sem = refs[2 * n + 2:]
        x, y, c, _ = _place()
        dev = 4 * x + 2 * y + c
        local = pltpu.make_async_copy(small_in, small_out.at[dev], local_sem)
        local.start()
        cps = []
        for k in range(n):
            cp = pltpu.make_async_remote_copy(src_ref=half(ins[k], k, c), dst_ref=half(outs[k], k, c),
                                              send_sem=send_sems.at[k], recv_sem=recv_sems.at[k],
                                              device_id=(x, y, 1 - c), device_id_type=MESH)
            cp.start()
            cps.append(cp)
        for r, (dx, dy, dc) in enumerate(rel):
            cp = pltpu.make_async_remote_copy(src_ref=small_in, dst_ref=small_out.at[dev], send_sem=ssend.at[r],
                                              recv_sem=srecv.at[r], device_id=(x ^ dx, y ^ dy, c ^ dc),
                                              device_id_type=MESH)
            cp.start()
            cps.append(cp)
        for k in range(n):
            land = half(outs[k], k, 1 - c)
            pltpu.make_async_remote_copy(src_ref=land, dst_ref=land, send_sem=send_sems.at[k],
                                         recv_sem=recv_sems.at[k], device_id=(x, y, c), device_id_type=MESH).wait_recv()
        for r, (dx, dy, dc) in enumerate(rel):
            land = small_out.at[4 * (x ^ dx) + 2 * (y ^ dy) + (c ^ dc)]
            pltpu.make_async_remote_copy(src_ref=land, dst_ref=land, send_sem=ssend.at[r], recv_sem=srecv.at[r],
                                         device_id=(x, y, c), device_id_type=MESH).wait_recv()
        for cp in cps:
            cp.wait_send()
        local.wait()

    return _pcall(
        body, name="join_halves", in_specs=[_ANY] * (n + 1), out_specs=[_ANY] * (n + 1),
        out_shape=[jax.ShapeDtypeStruct(f.shape, f.dtype) for f in fulls]
        + [jax.ShapeDtypeStruct((N_DEV,) + small.shape, small.dtype)],
        input_output_aliases={k: k for k in range(n)},
        scratch_shapes=[pltpu.SemaphoreType.DMA((n,)), pltpu.SemaphoreType.DMA((n,)), pltpu.SemaphoreType.DMA((7,)),
                        pltpu.SemaphoreType.DMA((7,)), pltpu.SemaphoreType.DMA],
    )(*fulls, small)


def _local_step(cfg, x2, target, norm_gain, w_my, fb, mu_g, w0, a0, k_k, k_a, r_k, ln_w, ln_b, fng, rest,
                exchange=None, h=None):
    T, D, FW, FH, RW, RH, LP, lora = cfg.T, cfg.D, cfg.FW, cfg.FH, cfg.RW, cfg.RH, cfg.LP, cfg.lora
    fb_p = jnp.pad(fb, ((0, 0), (0, LANES - FH)))
    mu = _rwkv_vec_to_my(cfg, mu_g)
    rk = r_k.reshape(1, RW)
    tm = min(1024, T)

    if h is None:
        h = _rms_fwd(cfg, x2, norm_gain)
    if len(rest) == 2:
        u, *got = _mm("in_proj", h, w_my, "nn", F32, tm, cfg.tn, 2048, comm=rest[0])
        rest = rest[1](got)
    else:
        u = _mm("in_proj", h, w_my, "nn", F32, tm, cfg.tn, 2048)
    w2, a2, wpf, wpr, wout = rest
    w2p = jnp.pad(w2, ((0, LP - lora), (0, 0)))
    a2p = jnp.pad(a2, ((0, LP - lora), (0, 0)))
    c_cols = _fox_prep(cfg, u, fb_p)
    c_rows = c_cols[:, :FH].T.reshape(FH, 1, T)
    o, lse = _attn_fwd(cfg, u, c_rows)
    oa = _gate_a_fwd(cfg, o, u)
    prep = _rwkv_prep_fwd(cfg, u, mu, w0, w2p, a0, a2p, k_k, k_a)
    r, lw, kp, v, an, b, zb = prep
    toks = [r, lw, kp, v, an, b]
    y, ckpt = _scan_fwd(cfg, toks)
    ob = _rwkv_post_fwd(cfg, y, r, kp, v, zb, ln_w, ln_b, rk)
    pa = _mm("proj_fox", oa, wpf, "nn", F32, tm, 1024, 2048)
    pb = _mm("proj_rwkv", ob, wpr, "nn", F32, tm, 1024, 2048)
    m = _merge_fwd(cfg, pa, pb, u)
    mo = _mm("out_proj", m, wout, "nn", F32, tm, 1024, 2048)
    loss8, dres, dres16, d_fng = _final(cfg, x2, mo, fng.reshape(1, D), target)

    dm = _mm("out_proj_dx", dres16, wout, "nt", F32, tm, 1024, 2048)
    d_wout = _mm("out_proj_dw", m, dres16, "tn", BF16, 1024, 1024, 2048)
    dpa, dpb, du = _merge_bwd(cfg, pa, pb, u, dm)
    doa = _mm("proj_fox_dx", dpa, wpf, "nt", F32, tm, 1024, 2048)
    d_wpf = _mm("proj_fox_dw", oa, dpa, "tn", BF16, 1024, 1024, 2048)
    dob = _mm("proj_rwkv_dx", dpb, wpr, "nt", F32, tm, 1024, 2048)
    d_wpr = _mm("proj_rwkv_dw", ob, dpb, "tn", BF16, 1024, 1024, 2048)

    do, du = _gate_a_bwd(cfg, o, u, doa, du)
    early = dict(w_proj_fox=d_wpf, w_proj_rwkv=d_wpr, w_out=d_wout)
    res = _attn_bwd(cfg, u, c_rows, lse, do, du, exchange(early) if exchange else None)
    du, dcol, swapped = res[0], res[1], list(res[2:])
    dc =jnp.pad(-dcol.reshape(FH, T).T, ((0, 0), (0, LANES - FH)))
    df, d_fb = _fox_prep_bwd(cfg, u, fb_p, dc)

    dy, dr_p, dk_p, dv_p, dzb, d_lnw, d_lnb, d_rk = _rwkv_post_bwd(cfg, y, r, kp, v, zb, ln_w, ln_b, rk, dob)
    res = _scan_bwd(cfg, toks, ckpt, dy, [dr_p, dk_p, dv_p], exchange(("swapped", swapped)) if exchange else None)
    cots, received = res[:6], list(res[6:])
    dus, d_mu, d_w0, d_w2p, d_a0, d_a2p, d_kk, d_ka = _rwkv_prep_bwd(cfg, u, mu, w0, w2p, a0, a2p, k_k, k_a, cots, dzb)
    du = _shift_bwd(cfg, dus, mu, df, du)
    if exchange:
        late = dict(w_in=exchange((h, du, d_w2p[:lora], d_a2p[:lora])))
    else:
        late = dict(w_in=_mm("in_proj_dw", h, du, "tn", BF16, 1024, cfg.tn, 2048), rwkv_w2=d_w2p[:lora],
                    rwkv_a2=d_a2p[:lora])
    tkx = 2 * cfg.tn if cfg.ncol % (2 * cfg.tn) == 0 else cfg.tn
    res = _mm("in_proj_dx", du, w_my, "nt", F32, tm, 1024, tkx, comm=exchange(late) if exchange else None)
    dh = res[0] if exchange else res
    big = dict(early, **late)
    res = _rms_bwd(cfg, x2, norm_gain, dh, dres, exchange(list(res[1:])) if exchange else None)
    gx, d_ng = res[:2]
    received += list(res[2:])

    small = dict(norm_gain=d_ng, fox_forget_bias=d_fb[:, :FH], rwkv_shift_mix=_rwkv_vec_from_my(cfg, d_mu),
                 rwkv_w0=d_w0, rwkv_a0=d_a0, rwkv_k_k=d_kk, rwkv_k_a=d_ka, rwkv_r_k=d_rk, rwkv_ln_w=d_lnw,
                 rwkv_ln_b=d_lnb, final_norm_gain=d_fng)
    return loss8[0, 0], gx, small, big, received


_SMALL = ["norm_gain", "fox_forget_bias", "rwkv_shift_mix", "rwkv_w0", "rwkv_a0", "rwkv_k_k", "rwkv_k_a", "rwkv_r_k",
          "rwkv_ln_w", "rwkv_ln_b", "final_norm_gain"]
_WEIGHTS = ["norm_gain", "w_in", "fox_forget_bias", "rwkv_shift_mix", "rwkv_w0", "rwkv_w2", "rwkv_a0", "rwkv_a2",
            "rwkv_k_k", "rwkv_k_a", "rwkv_r_k", "rwkv_ln_w", "rwkv_ln_b", "w_proj_fox", "w_proj_rwkv", "w_out",
            "final_norm_gain"]


def _pack_small(arrs):
    parts, n = [], 0
    for a in arrs:
        f = a.reshape(-1)
        fill = (-f.shape[0]) % LANES
        parts += [f] + ([jnp.zeros((fill,), f.dtype)] if fill else [])
        n += f.shape[0] + fill
    tail = ((-(n // LANES)) % 8) * LANES
    return jnp.concatenate(parts + ([jnp.zeros((tail,), parts[0].dtype)] if tail else [])).reshape(-1, LANES)


def _unpack_small(packed, shapes):
    flat = packed.reshape(-1)
    out, pos = [], 0
    for s in shapes:
        n = int(np.prod(s))
        out.append(flat[pos:pos + n].reshape(s))
        pos += n + ((-n) % LANES)
    return out


def _shard_major(a, axis):
    parts = jnp.split(a, N_CHIPS, axis=axis)
    return jnp.stack(parts, axis=0)


def kernel(x, norm_gain, w_in, fox_forget_bias, rwkv_shift_mix, rwkv_w0, rwkv_w2, rwkv_a0, rwkv_a2, rwkv_k_k, rwkv_k_a, rwkv_r_k, rwkv_ln_w, rwkv_ln_b, w_proj_fox, w_proj_rwkv, w_out, final_norm_gain, loss_target, m_norm_gain, m_w_in, m_fox_forget_bias, m_rwkv_shift_mix, m_rwkv_w0, m_rwkv_w2, m_rwkv_a0, m_rwkv_a2, m_rwkv_k_k, m_rwkv_k_a, m_rwkv_r_k, m_rwkv_ln_w, m_rwkv_ln_b, m_w_proj_fox, m_w_proj_rwkv, m_w_out, m_final_norm_gain, v_norm_gain, v_w_in, v_fox_forget_bias, v_rwkv_shift_mix, v_rwkv_w0, v_rwkv_w2, v_rwkv_a0, v_rwkv_a2, v_rwkv_k_k, v_rwkv_k_a, v_rwkv_r_k, v_rwkv_ln_w, v_rwkv_ln_b, v_w_proj_fox, v_w_proj_rwkv, v_w_out, v_final_norm_gain):
    args = dict(locals())
    T, D = x.shape[1], x.shape[2]
    lora = rwkv_w2.shape[1]
    cfg = _Cfg(T, D, lora)
    RW = cfg.RW
    c_idx = lax.axis_index("c").astype(jnp.int32).reshape(1)
    me_chip = (2 * lax.axis_index("x") + lax.axis_index("y")).astype(jnp.int32)
    place = jnp.concatenate([me_chip.reshape(1), c_idx])

    w_in_s = w_in[0].astype(BF16)
    lora_s = jnp.concatenate([rwkv_w2[0], rwkv_a2[0]], axis=0)
    h, g_in = _rms_fwd(cfg, x[0], norm_gain, _gather_parts([w_in_s]))
    w_my = _shards_to_my_layout(cfg, g_in)
    mine = [_cast_bf16("cast_w_proj_fox", w_proj_fox[0]), _cast_bf16("cast_w_proj_rwkv", w_proj_rwkv[0]),
            _cast_bf16("cast_w_out", w_out[0]), lora_s]

    def unpack(gathered):
        g_wpf, g_wpr, g_out, g_lora = gathered
        lo = g_lora.transpose(1, 0, 2).reshape(2 * lora, RW)
        return (lo[:lora], lo[lora:], g_wpf.transpose(1, 0, 2).reshape(RW, D),
                g_wpr.transpose(1, 0, 2).reshape(RW, D), g_out.reshape(D, D))

    early, late = ["w_proj_fox", "w_proj_rwkv", "w_out"], ["w_in", "lora"]
    names = early + late
    chip_sums, direct, shard_major = {}, {}, []
    n1_slot = (2 * (lax.axis_index("x") ^ (1 - lax.axis_index("c")))
               + (lax.axis_index("y") ^ lax.axis_index("c"))).astype(jnp.int32).reshape(1)

    def exchange(got):
        if isinstance(got, tuple) and len(got) == 4:
            h, du, d_w2, d_a2 = got
            c, half = lax.axis_index("c"), D // 2
            cols = lambda base: lax.dynamic_slice_in_dim(h, base * half, half, axis=1)
            lora_g = _shard_major(jnp.concatenate([d_w2, d_a2], axis=0).astype(BF16), 1)
            lora_rows = lambda base: lax.dynamic_slice_in_dim(lora_g, base * lora, lora, axis=1).reshape(-1, RW // 4)
            tiles = (BF16, min(1024, half), cfg.tn, 2048)
            sent = _mm("in_proj_dw_sibling", cols(1 - c), du, "tn", *tiles)
            kept, got_w, got_l = _mm("in_proj_dw", cols(c), du, "tn", *tiles,
                                     comm=_swap_with_sibling([sent, lora_rows(1 - c)]))
            return (_add_pair("add_halves_w_in", kept, got_w),
                    _add_pair("add_halves_lora", lora_rows(c), got_l).reshape(N_CHIPS, lora, RW // 4))
        if isinstance(got, dict):
            if "w_in" in got:
                sums = [_my_layout_to_shards(cfg, got["w_in"][0]), got["w_in"][1]]
                chip_sums.update(zip(late, sums))
                return _scatter_stage1(sums)
            shard_major.extend([_shard_major(got["w_proj_fox"], 1), _shard_major(got["w_proj_rwkv"], 1),
                                _shard_major(got["w_out"], 0)])
            return _swap_halves(shard_major)
        if got[0] == "swapped":
            sums = [_add_halves("add_halves_" + nm, g, r, c_idx) for nm, g, r in zip(early, shard_major, got[1])]
            chip_sums.update(zip(early, sums))
            return _scatter_to_owners(sums)
        direct.update(zip(late, got[0::2]))
        return _scatter_stage2([_add_passed("add_passed_" + nm, chip_sums[nm], g, n1_slot)
                                for nm, g in zip(late, got[1::2])])

    loss_dev, gx, small, _, recv2 = _local_step(
        cfg, x[0], loss_target[0], norm_gain, w_my, fox_forget_bias, rwkv_shift_mix, rwkv_w0, rwkv_a0, rwkv_k_k,
        rwkv_k_a, rwkv_r_k, rwkv_ln_w, rwkv_ln_b, final_norm_gain, (_gather_parts(mine), unpack), exchange, h)
    loss = lax.psum(loss_dev, ("x", "y", "c"))

    small_shapes = [args[nm].shape for nm in _SMALL]
    packed = _pack_small([small[nm] for nm in _SMALL])
    reduced = [_sum_chips("sum_chips_" + nm, r, chip_sums[nm], place) for nm, r in zip(early, recv2[:3])]
    reduced += [_sum_stages("sum_stages_" + nm, chip_sums[nm], direct[nm], via, place, transposed=nm == "w_in")
                for nm, via in zip(late, recv2[3:])]
    *joined, small_all = _join_halves(reduced, [int(nm == "w_in") for nm in names], packed)
    g_small = _sum_slots("sum_small", small_all)

    grads = dict(zip(_SMALL, _unpack_small(g_small, small_shapes)))
    grads.update({nm: g[None] for nm, g in zip(names, joined) if nm not in ("lora", "w_in")})
    g_lora_f = joined[names.index("lora")]
    grads["rwkv_w2"] = g_lora_f[None, :lora]
    grads["rwkv_a2"] = g_lora_f[None, lora:]

    delta, new_m, new_v = {}, {}, {}
    w_small = _pack_small([args[nm] for nm in _SMALL])
    m_small = _pack_small([args["m_" + nm] for nm in _SMALL])
    v_small = _pack_small([args["v_" + nm] for nm in _SMALL])
    d_s, m_s, v_s = _adamw("adamw_small", w_small, g_small, m_small, v_small)
    for tgt, pk in ((delta, d_s), (new_m, m_s), (new_v, v_s)):
        tgt.update(zip(_SMALL, _unpack_small(pk, small_shapes)))
    t_out = _adamw("adamw_w_in", w_in[0].T, joined[names.index("w_in")], m_w_in[0].T, v_w_in[0].T, copy_grad=True)
    delta["w_in"], new_m["w_in"], new_v["w_in"], grads["w_in"] = [t.T[None] for t in t_out]
    for nm in ("w_proj_fox", "w_proj_rwkv", "w_out", "rwkv_w2", "rwkv_a2"):
        shp = args[nm].shape
        two_d = (shp[1], shp[2])
        d_b, m_b, v_b = _adamw("adamw_" + nm, args[nm].reshape(two_d), grads[nm].reshape(two_d),
                               args["m_" + nm].reshape(two_d), args["v_" + nm].reshape(two_d))
        delta[nm], new_m[nm], new_v[nm] = d_b.reshape(shp), m_b.reshape(shp), v_b.reshape(shp)

    return (loss, gx[None], *[grads[n] for n in _WEIGHTS], *[delta[n] for n in _WEIGHTS],
            *[new_m[n] for n in _WEIGHTS], *[new_v[n] for n in _WEIGHTS])
```
